```python
import math
import jax, jax.numpy as jnp
from jax import lax
import numpy as np

D_MODEL = 2048
BATCH = 8
SEQ = 4096
DEPTH = 1

HEAD_DIM = 128
D_MIX = D_MODEL
N_HEADS_A = 8
N_KV_A = 2
N_HEADS_B = 8
N_KV_B = 2
D_FF = 4 * D_MODEL
D_PLE = 256
GRID_W = 64
BLOCK_Q = 128
WINDOW = 128
N_BUCKETS = 32
MAX_DISTANCE = 128
ROPE_THETA = 10000.0
EPS = 1e-6
NEG_INF = -1e30

Q_A = N_HEADS_A * HEAD_DIM
KV_A = N_KV_A * HEAD_DIM
Q_B = N_HEADS_B * HEAD_DIM
KV_B = N_KV_B * HEAD_DIM
D_IN_PROJ = Q_A + 2 * KV_A + Q_B + 2 * KV_B

kernel_name = "hybrid_axial_window_sink_encoder_layer"


def rmsnorm(x, g):
    xf = x.astype(jnp.float32)
    y = xf * lax.rsqrt(jnp.mean(xf * xf, axis=-1, keepdims=True) + EPS)
    return (y * g.astype(jnp.float32)).astype(x.dtype)


def axial_rope_tables(seq):
    rows = seq // GRID_W
    row = jnp.repeat(jnp.arange(rows, dtype=jnp.int32), GRID_W)
    col = jnp.tile(jnp.arange(GRID_W, dtype=jnp.int32), rows)
    half = HEAD_DIM // 2
    inv_freq = ROPE_THETA ** (-jnp.arange(0, half, 2, dtype=jnp.float32) / half)
    ang_r = row.astype(jnp.float32)[:, None] * inv_freq
    ang_c = col.astype(jnp.float32)[:, None] * inv_freq
    return jnp.cos(ang_r), jnp.sin(ang_r), jnp.cos(ang_c), jnp.sin(ang_c)


def _rotate(x, cos, sin):
    x1, x2 = x[..., : x.shape[-1] // 2], x[..., x.shape[-1] // 2:]
    return jnp.concatenate([x1 * cos - x2 * sin, x2 * cos + x1 * sin], axis=-1)


def apply_axial_rope(x, tabs):
    cr, sr, cc, sc = tabs
    xf = x.astype(jnp.float32)
    half = HEAD_DIM // 2
    out = jnp.concatenate([_rotate(xf[..., :half], cr, sr), _rotate(xf[..., half:], cc, sc)], axis=-1)
    return out.astype(x.dtype)


def global_axial_attention(q, k, v, g_q, g_k, tabs):
    B, S, H, D = q.shape
    KV = k.shape[2]
    G = H // KV
    nb = S // BLOCK_Q
    q = apply_axial_rope(rmsnorm(q, g_q).transpose(0, 2, 1, 3), tabs)
    k = apply_axial_rope(rmsnorm(k, g_k).transpose(0, 2, 1, 3), tabs)
    v = v.transpose(0, 2, 1, 3)
    qb = q.reshape(B, KV, G, nb, BLOCK_Q, D).transpose(3, 0, 1, 2, 4, 5)
    scale = D ** -0.5

    def attend(q_blk):
        s = jnp.einsum('bkgqd,bksd->bkgqs', q_blk, k, preferred_element_type=jnp.float32) * scale
        pr = jax.nn.softmax(s, axis=-1)
        return jnp.einsum('bkgqs,bksd->bkgqd', pr.astype(v.dtype), v)

    o = lax.map(attend, qb)
    return o.transpose(1, 0, 4, 2, 3, 5).reshape(B, S, H * D)


def t5_bucket(rel):
    nb = N_BUCKETS // 2
    ret = jnp.where(rel > 0, nb, 0)
    n = jnp.abs(rel)
    max_exact = nb // 2
    nf = jnp.maximum(n, 1).astype(jnp.float32)
    large = max_exact + (jnp.log(nf / max_exact) / math.log(MAX_DISTANCE / max_exact)
                         * (nb - max_exact)).astype(jnp.int32)
    large = jnp.minimum(large, nb - 1)
    return ret + jnp.where(n < max_exact, n, large)


def window_sink_attention(q, k, v, rel_bias_table, sink):
    B, S, H, D = q.shape
    KV = k.shape[2]
    G = H // KV
    nb = S // BLOCK_Q
    Q = BLOCK_Q
    q = q.transpose(0, 2, 1, 3).reshape(B, KV, G, nb, Q, D)
    pad = ((0, 0), (0, 0), (Q, Q), (0, 0))
    kp = jnp.pad(k.transpose(0, 2, 1, 3), pad).reshape(B, KV, nb + 2, Q, D)
    vp = jnp.pad(v.transpose(0, 2, 1, 3), pad).reshape(B, KV, nb + 2, Q, D)
    kband = jnp.concatenate([kp[:, :, :-2], kp[:, :, 1:-1], kp[:, :, 2:]], axis=3)
    vband = jnp.concatenate([vp[:, :, :-2], vp[:, :, 1:-1], vp[:, :, 2:]], axis=3)
    s = jnp.einsum('bkgnqd,bknjd->bkgnqj', q, kband,
                   preferred_element_type=jnp.float32) * (D ** -0.5)
    r = jnp.arange(Q, dtype=jnp.int32)
    j = jnp.arange(3 * Q, dtype=jnp.int32)
    rel = (j[None, :] - Q) - r[:, None]
    bias = rel_bias_table[t5_bucket(rel)].astype(jnp.float32)
    bias = bias.transpose(2, 0, 1).reshape(KV, G, 1, Q, 3 * Q)
    kabs = jnp.arange(nb, dtype=jnp.int32)[:, None] * Q + j[None, :] - Q
    in_range = (kabs >= 0) & (kabs < S)
    mask = (jnp.abs(rel) <= WINDOW)[None, :, :] & in_range[:, None, :]
    s = jnp.where(mask, s + bias, NEG_INF)
    sink_col = jnp.broadcast_to(sink.astype(jnp.float32).reshape(1, KV, G, 1, 1, 1),
                                s.shape[:-1] + (1,))
    pr = jax.nn.softmax(jnp.concatenate([s, sink_col], axis=-1), axis=-1)[..., :-1]
    o = jnp.einsum('bkgnqj,bknjd->bkgnqd', pr.astype(vband.dtype), vband)
    return o.transpose(0, 3, 4, 1, 2, 5).reshape(B, S, H * D)


def _fwd_setup_inputs(seed: int = 0) -> dict:
    key = jax.random.key(seed)
    ks = jax.random.split(key, 20)
    f32 = jnp.float32

    def nrm(k, shape, scale):
        return jax.random.normal(k, shape, f32) * scale

    def gain(k, shape):
        return 1.0 + 0.02 * jax.random.normal(k, shape, f32)

    return {
        "x": nrm(ks[0], (BATCH, SEQ, D_MODEL), 1.0),
        "p": nrm(ks[1], (DEPTH, BATCH, SEQ, D_PLE), 1.0),
        "attn_norm_g": gain(ks[2], (DEPTH, D_MODEL)),
        "w_in": nrm(ks[3], (DEPTH, D_MODEL, D_IN_PROJ), D_MODEL ** -0.5),
        "q_norm_g": gain(ks[4], (DEPTH, HEAD_DIM)),
        "k_norm_g": gain(ks[5], (DEPTH, HEAD_DIM)),
        "sink_logits": nrm(ks[6], (DEPTH, N_HEADS_B), 1.0),
        "w_out": nrm(ks[7], (DEPTH, D_MIX, D_MODEL), D_MIX ** -0.5),
        "mlp_norm_g": gain(ks[8], (DEPTH, D_MODEL)),
        "w_up": nrm(ks[9], (DEPTH, D_MODEL, D_FF), D_MODEL ** -0.5),
        "w_down": nrm(ks[10], (DEPTH, D_FF, D_MODEL), D_FF ** -0.5),
        "ple_w": nrm(ks[11], (DEPTH, D_PLE, D_MODEL), D_PLE ** -0.5),
        "ple_norm_g": gain(ks[12], (DEPTH, D_MODEL)),
        "gate_norm_g": gain(ks[13], (DEPTH, D_MODEL)),
        "w_gate": nrm(ks[14], (DEPTH, D_MODEL, D_MODEL), D_MODEL ** -0.5),
        "rel_bias_table": nrm(ks[15], (N_BUCKETS, N_HEADS_B), 0.5),
        "final_norm_g": gain(ks[16], (D_MODEL,)),
    }


def _fwd_reference(x, p, attn_norm_g, w_in, q_norm_g, k_norm_g, sink_logits, w_out,
              mlp_norm_g, w_up, w_down, ple_w, ple_norm_g, gate_norm_g, w_gate,
              rel_bias_table, final_norm_g):
    B, S, _ = x.shape
    tabs = axial_rope_tables(S)
    splits = [Q_A, Q_A + KV_A, Q_A + 2 * KV_A, Q_A + 2 * KV_A + Q_B, Q_A + 2 * KV_A + Q_B + KV_B]
    h = x
    for i in range(DEPTH):
        u = rmsnorm(h, attn_norm_g[i])
        proj = u @ w_in[i]
        qa, ka, va, qb, kb, vb = jnp.split(proj, splits, axis=-1)
        oa = global_axial_attention(
            qa.reshape(B, S, N_HEADS_A, HEAD_DIM), ka.reshape(B, S, N_KV_A, HEAD_DIM),
            va.reshape(B, S, N_KV_A, HEAD_DIM), q_norm_g[i], k_norm_g[i], tabs)
        ob = window_sink_attention(
            qb.reshape(B, S, N_HEADS_B, HEAD_DIM), kb.reshape(B, S, N_KV_B, HEAD_DIM),
            vb.reshape(B, S, N_KV_B, HEAD_DIM), rel_bias_table, sink_logits[i])
        h = h + jnp.concatenate([oa, ob], axis=-1) @ w_out[i]
        m = rmsnorm(h, mlp_norm_g[i])
        h = h + jnp.square(jax.nn.relu(m @ w_up[i])) @ w_down[i]
        e = rmsnorm(p[i] @ ple_w[i], ple_norm_g[i])
        gate = jax.nn.sigmoid(rmsnorm(h, gate_norm_g[i]) @ w_gate[i])
        h = h + gate * e
    return rmsnorm(h, final_norm_g)


import jax as _jax
import jax.numpy as _jnp

TWIN_FORMAT = 'train_step'
FWD_PARAMS = ['x', 'p', 'attn_norm_g', 'w_in', 'q_norm_g', 'k_norm_g', 'sink_logits', 'w_out', 'mlp_norm_g', 'w_up', 'w_down', 'ple_w', 'ple_norm_g', 'gate_norm_g', 'w_gate', 'rel_bias_table', 'final_norm_g']
TWIN_WEIGHTS = ['attn_norm_g', 'w_in', 'q_norm_g', 'k_norm_g', 'sink_logits', 'w_out', 'mlp_norm_g', 'w_up', 'w_down', 'ple_w', 'ple_norm_g', 'gate_norm_g', 'w_gate', 'rel_bias_table', 'final_norm_g']
TWIN_DIFF_INPUT = 'x'
TWIN_INPUTS = ['x', 'p', 'attn_norm_g', 'w_in', 'q_norm_g', 'k_norm_g', 'sink_logits', 'w_out', 'mlp_norm_g', 'w_up', 'w_down', 'ple_w', 'ple_norm_g', 'gate_norm_g', 'w_gate', 'rel_bias_table', 'final_norm_g', 'loss_target', 'm_attn_norm_g', 'm_w_in', 'm_q_norm_g', 'm_k_norm_g', 'm_sink_logits', 'm_w_out', 'm_mlp_norm_g', 'm_w_up', 'm_w_down', 'm_ple_w', 'm_ple_norm_g', 'm_gate_norm_g', 'm_w_gate', 'm_rel_bias_table', 'm_final_norm_g', 'v_attn_norm_g', 'v_w_in', 'v_q_norm_g', 'v_k_norm_g', 'v_sink_logits', 'v_w_out', 'v_mlp_norm_g', 'v_w_up', 'v_w_down', 'v_ple_w', 'v_ple_norm_g', 'v_gate_norm_g', 'v_w_gate', 'v_rel_bias_table', 'v_final_norm_g']
TWIN_OUTPUTS = ['loss', 'grad_x', 'grad_attn_norm_g', 'grad_w_in', 'grad_q_norm_g', 'grad_k_norm_g', 'grad_sink_logits', 'grad_w_out', 'grad_mlp_norm_g', 'grad_w_up', 'grad_w_down', 'grad_ple_w', 'grad_ple_norm_g', 'grad_gate_norm_g', 'grad_w_gate', 'grad_rel_bias_table', 'grad_final_norm_g', 'delta_attn_norm_g', 'delta_w_in', 'delta_q_norm_g', 'delta_k_norm_g', 'delta_sink_logits', 'delta_w_out', 'delta_mlp_norm_g', 'delta_w_up', 'delta_w_down', 'delta_ple_w', 'delta_ple_norm_g', 'delta_gate_norm_g', 'delta_w_gate', 'delta_rel_bias_table', 'delta_final_norm_g', 'new_m_attn_norm_g', 'new_m_w_in', 'new_m_q_norm_g', 'new_m_k_norm_g', 'new_m_sink_logits', 'new_m_w_out', 'new_m_mlp_norm_g', 'new_m_w_up', 'new_m_w_down', 'new_m_ple_w', 'new_m_ple_norm_g', 'new_m_gate_norm_g', 'new_m_w_gate', 'new_m_rel_bias_table', 'new_m_final_norm_g', 'new_v_attn_norm_g', 'new_v_w_in', 'new_v_q_norm_g', 'new_v_k_norm_g', 'new_v_sink_logits', 'new_v_w_out', 'new_v_mlp_norm_g', 'new_v_w_up', 'new_v_w_down', 'new_v_ple_w', 'new_v_ple_norm_g', 'new_v_gate_norm_g', 'new_v_w_gate', 'new_v_rel_bias_table', 'new_v_final_norm_g']
TWIN_LEAF_KINDS = {'loss': 'loss', 'grad_x': 'grad_x', 'grad_attn_norm_g': 'grad_w', 'grad_w_in': 'grad_w', 'grad_q_norm_g': 'grad_w', 'grad_k_norm_g': 'grad_w', 'grad_sink_logits': 'grad_w', 'grad_w_out': 'grad_w', 'grad_mlp_norm_g': 'grad_w', 'grad_w_up': 'grad_w', 'grad_w_down': 'grad_w', 'grad_ple_w': 'grad_w', 'grad_ple_norm_g': 'grad_w', 'grad_gate_norm_g': 'grad_w', 'grad_w_gate': 'grad_w', 'grad_rel_bias_table': 'grad_w', 'grad_final_norm_g': 'grad_w', 'delta_attn_norm_g': 'delta_w', 'delta_w_in': 'delta_w', 'delta_q_norm_g': 'delta_w', 'delta_k_norm_g': 'delta_w', 'delta_sink_logits': 'delta_w', 'delta_w_out': 'delta_w', 'delta_mlp_norm_g': 'delta_w', 'delta_w_up': 'delta_w', 'delta_w_down': 'delta_w', 'delta_ple_w': 'delta_w', 'delta_ple_norm_g': 'delta_w', 'delta_gate_norm_g': 'delta_w', 'delta_w_gate': 'delta_w', 'delta_rel_bias_table': 'delta_w', 'delta_final_norm_g': 'delta_w', 'new_m_attn_norm_g': 'new_m', 'new_m_w_in': 'new_m', 'new_m_q_norm_g': 'new_m', 'new_m_k_norm_g': 'new_m', 'new_m_sink_logits': 'new_m', 'new_m_w_out': 'new_m', 'new_m_mlp_norm_g': 'new_m', 'new_m_w_up': 'new_m', 'new_m_w_down': 'new_m', 'new_m_ple_w': 'new_m', 'new_m_ple_norm_g': 'new_m', 'new_m_gate_norm_g': 'new_m', 'new_m_w_gate': 'new_m', 'new_m_rel_bias_table': 'new_m', 'new_m_final_norm_g': 'new_m', 'new_v_attn_norm_g': 'new_v', 'new_v_w_in': 'new_v', 'new_v_q_norm_g': 'new_v', 'new_v_k_norm_g': 'new_v', 'new_v_sink_logits': 'new_v', 'new_v_w_out': 'new_v', 'new_v_mlp_norm_g': 'new_v', 'new_v_w_up': 'new_v', 'new_v_w_down': 'new_v', 'new_v_ple_w': 'new_v', 'new_v_ple_norm_g': 'new_v', 'new_v_gate_norm_g': 'new_v', 'new_v_w_gate': 'new_v', 'new_v_rel_bias_table': 'new_v', 'new_v_final_norm_g': 'new_v'}


def _forward(args):
    return _fwd_reference(*[args[k] for k in FWD_PARAMS])


def _output_shape():
    def fwd():
        inp = _fwd_setup_inputs(0)
        return _fwd_reference(*[inp[k] for k in FWD_PARAMS])
    out = _jax.eval_shape(fwd)
    return out.shape, out.dtype

N_MICROBATCH = 1
ADAM_LR = 0.001
ADAM_B1 = 0.9
ADAM_B2 = 0.999
ADAM_EPS = 1e-08
ADAM_WD = 0.01
ADAM_STEP = 10
PER_EXAMPLE_BATCH_AXIS = {'x': 0, 'p': 1, 'loss_target': 0}
SHARED_INPUTS = []
_WEIGHT_DTYPES = {'attn_norm_g': _jnp.float32, 'w_in': _jnp.float32, 'q_norm_g': _jnp.float32, 'k_norm_g': _jnp.float32, 'sink_logits': _jnp.float32, 'w_out': _jnp.float32, 'mlp_norm_g': _jnp.float32, 'w_up': _jnp.float32, 'w_down': _jnp.float32, 'ple_w': _jnp.float32, 'ple_norm_g': _jnp.float32, 'gate_norm_g': _jnp.float32, 'w_gate': _jnp.float32, 'rel_bias_table': _jnp.float32, 'final_norm_g': _jnp.float32}
MOMENT_SCALE = {'attn_norm_g': 1.851460e-02, 'w_in': 1.508342e-02, 'q_norm_g': 2.573595e-02, 'k_norm_g': 2.736006e-02, 'sink_logits': 1.191725e-03, 'w_out': 1.084983e-02, 'mlp_norm_g': 7.797260e-02, 'w_up': 3.783665e-02, 'w_down': 7.647717e-02, 'ple_w': 2.899881e-02, 'ple_norm_g': 5.933407e-02, 'gate_norm_g': 1.237344e-02, 'w_gate': 1.227762e-02, 'rel_bias_table': 2.034997e-02, 'final_norm_g': 1.613345e+01}


def _to_microbatches(a, axis):
    t = _jnp.moveaxis(a, axis, 0)
    t = t.reshape((N_MICROBATCH, t.shape[0] // N_MICROBATCH) + t.shape[1:])
    return _jnp.moveaxis(t, 1, axis + 1)


def setup_inputs(seed: int = 0) -> dict:
    inp = _fwd_setup_inputs(seed)
    key = _jax.random.fold_in(_jax.random.key(seed), 7919)
    shape, _ = _output_shape()
    out = dict(inp)
    out["loss_target"] = _jax.random.normal(_jax.random.fold_in(key, 0), shape, _jnp.float32)
    for i, name in enumerate(TWIN_WEIGHTS):
        w = inp[name].astype(_jnp.float32)
        if MOMENT_SCALE is None:
            s = _jnp.sqrt(_jnp.mean(_jnp.square(w)) + 1e-30)
        else:
            s = MOMENT_SCALE[name]
        km, kv = _jax.random.split(_jax.random.fold_in(key, i + 1))
        out[name] = w
        out["m_" + name] = s * _jax.random.normal(km, w.shape, _jnp.float32)
        out["v_" + name] = (s * s) * _jax.random.uniform(kv, w.shape, _jnp.float32, 0.5, 1.5)
    if N_MICROBATCH > 1:
        for name, axis in PER_EXAMPLE_BATCH_AXIS.items():
            out[name] = _to_microbatches(out[name], axis)
    return {'x': out['x'], 'p': out['p'], 'attn_norm_g': out['attn_norm_g'], 'w_in': out['w_in'], 'q_norm_g': out['q_norm_g'], 'k_norm_g': out['k_norm_g'], 'sink_logits': out['sink_logits'], 'w_out': out['w_out'], 'mlp_norm_g': out['mlp_norm_g'], 'w_up': out['w_up'], 'w_down': out['w_down'], 'ple_w': out['ple_w'], 'ple_norm_g': out['ple_norm_g'], 'gate_norm_g': out['gate_norm_g'], 'w_gate': out['w_gate'], 'rel_bias_table': out['rel_bias_table'], 'final_norm_g': out['final_norm_g'], 'loss_target': out['loss_target'], 'm_attn_norm_g': out['m_attn_norm_g'], 'm_w_in': out['m_w_in'], 'm_q_norm_g': out['m_q_norm_g'], 'm_k_norm_g': out['m_k_norm_g'], 'm_sink_logits': out['m_sink_logits'], 'm_w_out': out['m_w_out'], 'm_mlp_norm_g': out['m_mlp_norm_g'], 'm_w_up': out['m_w_up'], 'm_w_down': out['m_w_down'], 'm_ple_w': out['m_ple_w'], 'm_ple_norm_g': out['m_ple_norm_g'], 'm_gate_norm_g': out['m_gate_norm_g'], 'm_w_gate': out['m_w_gate'], 'm_rel_bias_table': out['m_rel_bias_table'], 'm_final_norm_g': out['m_final_norm_g'], 'v_attn_norm_g': out['v_attn_norm_g'], 'v_w_in': out['v_w_in'], 'v_q_norm_g': out['v_q_norm_g'], 'v_k_norm_g': out['v_k_norm_g'], 'v_sink_logits': out['v_sink_logits'], 'v_w_out': out['v_w_out'], 'v_mlp_norm_g': out['v_mlp_norm_g'], 'v_w_up': out['v_w_up'], 'v_w_down': out['v_w_down'], 'v_ple_w': out['v_ple_w'], 'v_ple_norm_g': out['v_ple_norm_g'], 'v_gate_norm_g': out['v_gate_norm_g'], 'v_w_gate': out['v_w_gate'], 'v_rel_bias_table': out['v_rel_bias_table'], 'v_final_norm_g': out['v_final_norm_g']}


def _loss(weights, diff, rest, loss_target):
    with _jax.named_scope("forward"):
        args = {**rest, TWIN_DIFF_INPUT: diff, **{k: w.astype(_WEIGHT_DTYPES[k]) for k, w in weights.items()}}
        y = _forward(args)
    with _jax.named_scope("loss_head"):
        err = _jnp.square(y.astype(_jnp.float32) - loss_target)
        return 0.5 * _jnp.sum(_jnp.mean(err, axis=-1)) if err.ndim else 0.5 * err


def _adamw(w, g, m, v):
    m = ADAM_B1 * m + (1.0 - ADAM_B1) * g
    v = ADAM_B2 * v + (1.0 - ADAM_B2) * _jnp.square(g)
    m_hat = m / (1.0 - ADAM_B1 ** ADAM_STEP)
    v_hat = v / (1.0 - ADAM_B2 ** ADAM_STEP)
    delta = -ADAM_LR * (m_hat / (_jnp.sqrt(v_hat) + ADAM_EPS) + ADAM_WD * w)
    return delta, m, v


def reference(x, p, attn_norm_g, w_in, q_norm_g, k_norm_g, sink_logits, w_out, mlp_norm_g, w_up, w_down, ple_w, ple_norm_g, gate_norm_g, w_gate, rel_bias_table, final_norm_g, loss_target, m_attn_norm_g, m_w_in, m_q_norm_g, m_k_norm_g, m_sink_logits, m_w_out, m_mlp_norm_g, m_w_up, m_w_down, m_ple_w, m_ple_norm_g, m_gate_norm_g, m_w_gate, m_rel_bias_table, m_final_norm_g, v_attn_norm_g, v_w_in, v_q_norm_g, v_k_norm_g, v_sink_logits, v_w_out, v_mlp_norm_g, v_w_up, v_w_down, v_ple_w, v_ple_norm_g, v_gate_norm_g, v_w_gate, v_rel_bias_table, v_final_norm_g):
    given = dict(x=x, p=p, attn_norm_g=attn_norm_g, w_in=w_in, q_norm_g=q_norm_g, k_norm_g=k_norm_g, sink_logits=sink_logits, w_out=w_out, mlp_norm_g=mlp_norm_g, w_up=w_up, w_down=w_down, ple_w=ple_w, ple_norm_g=ple_norm_g, gate_norm_g=gate_norm_g, w_gate=w_gate, rel_bias_table=rel_bias_table, final_norm_g=final_norm_g, loss_target=loss_target, m_attn_norm_g=m_attn_norm_g, m_w_in=m_w_in, m_q_norm_g=m_q_norm_g, m_k_norm_g=m_k_norm_g, m_sink_logits=m_sink_logits, m_w_out=m_w_out, m_mlp_norm_g=m_mlp_norm_g, m_w_up=m_w_up, m_w_down=m_w_down, m_ple_w=m_ple_w, m_ple_norm_g=m_ple_norm_g, m_gate_norm_g=m_gate_norm_g, m_w_gate=m_w_gate, m_rel_bias_table=m_rel_bias_table, m_final_norm_g=m_final_norm_g, v_attn_norm_g=v_attn_norm_g, v_w_in=v_w_in, v_q_norm_g=v_q_norm_g, v_k_norm_g=v_k_norm_g, v_sink_logits=v_sink_logits, v_w_out=v_w_out, v_mlp_norm_g=v_mlp_norm_g, v_w_up=v_w_up, v_w_down=v_w_down, v_ple_w=v_ple_w, v_ple_norm_g=v_ple_norm_g, v_gate_norm_g=v_gate_norm_g, v_w_gate=v_w_gate, v_rel_bias_table=v_rel_bias_table, v_final_norm_g=v_final_norm_g)
    weights = {n: given[n] for n in TWIN_WEIGHTS}
    shared = {n: given[n] for n in SHARED_INPUTS}
    per_example = {n: given[n] for n in ['x', 'p']}
    grad_fn = _jax.value_and_grad(_loss, argnums=(0, 1))

    def one_microbatch(ex, loss_target):
        ex = dict(ex)
        diff = ex.pop(TWIN_DIFF_INPUT)
        return grad_fn(weights, diff, {**shared, **ex}, loss_target)

    if N_MICROBATCH == 1:
        loss, (grad_w, grad_x) = one_microbatch(per_example, given["loss_target"])
    else:
        def body(carry, xs):
            loss_sum, grad_sum = carry
            l_k, (gw_k, gx_k) = one_microbatch(xs[0], xs[1])
            with _jax.named_scope("update"):
                return (loss_sum + l_k, _jax.tree.map(_jnp.add, grad_sum, gw_k)), gx_k

        init = (_jnp.zeros((), _jnp.float32), _jax.tree.map(_jnp.zeros_like, weights))
        (loss, grad_w), grad_x = _jax.lax.scan(body, init, (per_example, given["loss_target"]))
    with _jax.named_scope("update"):
        delta_w, new_m, new_v = {}, {}, {}
        for n in TWIN_WEIGHTS:
            delta_w[n], new_m[n], new_v[n] = _adamw(weights[n], grad_w[n], given["m_" + n], given["v_" + n])
    return (loss, grad_x, *[grad_w[n] for n in TWIN_WEIGHTS], *[delta_w[n] for n in TWIN_WEIGHTS],
            *[new_m[n] for n in TWIN_WEIGHTS], *[new_v[n] for n in TWIN_WEIGHTS])
```

```python
import functools
import math

import jax
import jax.numpy as jnp
import numpy as np
from jax import lax
from jax.experimental import pallas as pl
from jax.experimental.pallas import tpu as pltpu

F32 = jnp.float32
BF16 = jnp.bfloat16

HEAD_DIM = 128
N_HEADS_A = 8
N_KV_A = 2
N_HEADS_B = 8
N_KV_B = 2
GROUP = 4
GRID_W = 64
BLOCK_Q = 128
WINDOW = 128
N_BUCKETS = 32
MAX_DISTANCE = 128
ROPE_THETA = 10000.0
EPS = 1e-6
NEG_INF = -1e30
ATT_SCALE = HEAD_DIM ** -0.5

ADAM_LR = 0.001
ADAM_B1 = 0.9
ADAM_B2 = 0.999
ADAM_EPS = 1e-08
ADAM_WD = 0.01
ADAM_STEP = 10

N_CHIPS = 4
N_DEV = 8
COL_QA, COL_KA, COL_VA, COL_QB, COL_KB, COL_VB = 0, 8, 10, 12, 20, 22
N_COLS = 24

VMEM_LIMIT = 52 * 1024 * 1024


def _params(sem=None):
    return pltpu.CompilerParams(dimension_semantics=sem, vmem_limit_bytes=VMEM_LIMIT)


def _matmul(a, b, *, mode, out_dtypes, name, epilogue=None, extras=(), bm=1024, bn=1024, bk=2048,
            out_stack=0):
    stacked = b.ndim == 3
    if mode == "nn":
        m, k = a.shape
        if stacked:
            nj, kb, ns = b.shape
            n, ks = nj * ns, k
        else:
            kb, n = b.shape
            ns, ks = n, k
        dn = (((1,), (0,)), ((), ()))
    elif mode == "nt":
        m, k = a.shape
        if stacked:
            nj, n, ks = b.shape
            kb = nj * ks
        else:
            n, kb = b.shape
            ks = kb
        ns = n
        dn = (((1,), (1,)), ((), ()))
    else:
        k, m = a.shape
        kb, n = b.shape
        ns, ks = n, k
        dn = (((0,), (0,)), ((), ()))
    assert k == kb and not (stacked and mode == "tn")
    ns_out = n // out_stack if out_stack else n
    bm, bn, bk = min(bm, m), min(bn, ns, ns_out), min(bk, ks)
    assert m % bm == 0 and ns % bn == 0 and ns_out % bn == 0 and ks % bk == 0
    gm, gn, gk = m // bm, n // bn, k // bk

    if mode == "tn":
        a_spec = pl.BlockSpec((bk, bm), lambda i, j, q: (q, i))
    else:
        a_spec = pl.BlockSpec((bm, bk), lambda i, j, q: (i, q))
    if mode == "nt":
        if stacked:
            per = ks // bk
            b_spec = pl.BlockSpec((None, bn, bk), lambda i, j, q: (q // per, j, q % per))
        else:
            b_spec = pl.BlockSpec((bn, bk), lambda i, j, q: (j, q))
    else:
        if stacked:
            per = ns // bn
            b_spec = pl.BlockSpec((None, bk, bn), lambda i, j, q: (j // per, q, j % per))
        else:
            b_spec = pl.BlockSpec((bk, bn), lambda i, j, q: (q, j))
    ex_spec = pl.BlockSpec((bm, bn), lambda i, j, q: (i, j))
    if out_stack:
        per_o = ns_out // bn
        o_spec = pl.BlockSpec((None, bm, bn), lambda i, j, q: (j // per_o, i, j % per_o))
        o_shape = (out_stack, m, ns_out)
    else:
        o_spec = ex_spec
        o_shape = (m, n)
    n_ex, n_out = len(extras), len(out_dtypes)

    def body(a_ref, b_ref, *rest):
        ex, outs = rest[:n_ex], rest[n_ex:n_ex + n_out]
        part = lax.dot_general(a_ref[...], b_ref[...], dn, preferred_element_type=F32)

        def finish(acc):
            res = epilogue(acc, *[e[...] for e in ex]) if epilogue else (acc,)
            for o, r in zip(outs, res):
                o[...] = r.astype(o.dtype)

        if gk == 1:
            finish(part)
        else:
            acc_ref = rest[-1]
            q = pl.program_id(2)

            @pl.when(q == 0)
            def _():
                acc_ref[...] = part

            @pl.when(q > 0)
            def _():
                acc_ref[...] += part

            @pl.when(q == gk - 1)
            def _():
                finish(acc_ref[...])

    res = pl.pallas_call(
        body,
        name=name,
        grid=(gm, gn, gk),
        in_specs=[a_spec, b_spec] + [ex_spec] * n_ex,
        out_specs=[o_spec] * n_out,
        out_shape=[jax.ShapeDtypeStruct(o_shape, dt) for dt in out_dtypes],
        scratch_shapes=[pltpu.VMEM((bm, bn), F32)] if gk > 1 else [],
        compiler_params=_params(("parallel", "parallel", "arbitrary")),
    )(a, b, *extras)
    return res[0] if n_out == 1 else res


def _rms_fwd(x, g, *, name, tm=256):
    s, d = x.shape
    tm = min(tm, s)

    def body(x_ref, g_ref, o_ref):
        xf = x_ref[...]
        r = lax.rsqrt(jnp.mean(xf * xf, axis=-1, keepdims=True) + EPS)
        o_ref[...] = (xf * r * g_ref[...]).astype(o_ref.dtype)

    return pl.pallas_call(
        body,
        name=name,
        grid=(s // tm,),
        in_specs=[pl.BlockSpec((tm, d), lambda i: (i, 0)), pl.BlockSpec((1, d), lambda i: (0, 0))],
        out_specs=pl.BlockSpec((tm, d), lambda i: (i, 0)),
        out_shape=jax.ShapeDtypeStruct((s, d), BF16),
        compiler_params=_params(("parallel",)),
    )(x, g)


def _rms_bwd(x, dy, g, add, *, name, want_bf16, tm=256):
    s, d = x.shape
    tm = min(tm, s)

    def body(x_ref, dy_ref, g_ref, add_ref, dx_ref, *rest):
        dg_ref = rest[-1]
        i = pl.program_id(0)
        xf = x_ref[...]
        dyf = dy_ref[...].astype(F32)
        r = lax.rsqrt(jnp.mean(xf * xf, axis=-1, keepdims=True) + EPS)
        xh = xf * r
        dyg = dyf * g_ref[...]
        dx = r * (dyg - xh * jnp.mean(dyg * xh, axis=-1, keepdims=True))
        tot = add_ref[...] + dx
        dx_ref[...] = tot
        if want_bf16:
            rest[0][...] = tot.astype(BF16)
        part = jnp.sum(dyf * xh, axis=0, keepdims=True)

        @pl.when(i == 0)
        def _():
            dg_ref[...] = part

        @pl.when(i > 0)
        def _():
            dg_ref[...] += part

    row = pl.BlockSpec((tm, d), lambda i: (i, 0))
    vec = pl.BlockSpec((1, d), lambda i: (0, 0))
    out_specs = [row] + ([row] if want_bf16 else []) + [vec]
    out_shape = [jax.ShapeDtypeStruct((s, d), F32)]
    if want_bf16:
        out_shape.append(jax.ShapeDtypeStruct((s, d), BF16))
    out_shape.append(jax.ShapeDtypeStruct((1, d), F32))
    return pl.pallas_call(
        body,
        name=name,
        grid=(s // tm,),
        in_specs=[row, row, vec, row],
        out_specs=out_specs,
        out_shape=out_shape,
        compiler_params=_params(("arbitrary",)),
    )(x, dy, g, add)


def _tail(h2, gate, pp, target, g_ple, g_final, *, tm=128):
    s, d = h2.shape
    tm = min(tm, s)

    def body(h2_ref, gate_ref, pp_ref, t_ref, gp_ref, gf_ref, dh3_ref, dz_ref, dpp_ref, dgf_ref, dgp_ref, loss_ref):
        i = pl.program_id(0)
        ppf = pp_ref[...]
        gate_v = gate_ref[...]
        r_p = lax.rsqrt(jnp.mean(ppf * ppf, axis=-1, keepdims=True) + EPS)
        eh = ppf * r_p
        e = eh * gp_ref[...]
        h3 = h2_ref[...] + gate_v * e
        r_f = lax.rsqrt(jnp.mean(h3 * h3, axis=-1, keepdims=True) + EPS)
        yh = h3 * r_f
        diff = yh * gf_ref[...] - t_ref[...]
        loss_part = 0.5 * jnp.sum(jnp.mean(diff * diff, axis=-1, keepdims=True), axis=0, keepdims=True)
        dy = diff / d
        dgf = jnp.sum(dy * yh, axis=0, keepdims=True)
        dyg = dy * gf_ref[...]
        dh3 = r_f * (dyg - yh * jnp.mean(dyg * yh, axis=-1, keepdims=True))
        dh3_ref[...] = dh3
        de = dh3 * gate_v
        dz_ref[...] = (dh3 * e * gate_v * (1.0 - gate_v)).astype(BF16)
        dgp = jnp.sum(de * eh, axis=0, keepdims=True)
        deg = de * gp_ref[...]
        dpp_ref[...] = (r_p * (deg - eh * jnp.mean(deg * eh, axis=-1, keepdims=True))).astype(BF16)
        loss_row = jnp.broadcast_to(loss_part, (1, 128))

        @pl.when(i == 0)
        def _():
            dgf_ref[...] = dgf
            dgp_ref[...] = dgp
            loss_ref[...] = loss_row

        @pl.when(i > 0)
        def _():
            dgf_ref[...] += dgf
            dgp_ref[...] += dgp
            loss_ref[...] += loss_row

    row = pl.BlockSpec((tm, d), lambda i: (i, 0))
    vec = pl.BlockSpec((1, d), lambda i: (0, 0))
    return pl.pallas_call(
        body,
        name="tail_fwd_bwd",
        grid=(s // tm,),
        in_specs=[row, row, row, row, vec, vec],
        out_specs=[row, row, row, vec, vec, pl.BlockSpec((1, 128), lambda i: (0, 0))],
        out_shape=[
            jax.ShapeDtypeStruct((s, d), F32),
            jax.ShapeDtypeStruct((s, d), BF16),
            jax.ShapeDtypeStruct((s, d), BF16),
            jax.ShapeDtypeStruct((1, d), F32),
            jax.ShapeDtypeStruct((1, d), F32),
            jax.ShapeDtypeStruct((1, 128), F32),
        ],
        compiler_params=_params(("arbitrary",)),
    )(h2, gate, pp, target, g_ple, g_final)


def _rope_tables(s):
    t = jnp.arange(s, dtype=jnp.int32)
    row = (t // GRID_W).astype(F32)
    col = (t % GRID_W).astype(F32)
    half = HEAD_DIM // 2
    inv_freq = ROPE_THETA ** (-jnp.arange(0, half, 2, dtype=F32) / half)
    ang_r = row[:, None] * inv_freq
    ang_c = col[:, None] * inv_freq
    cr, sr, cc, sc = jnp.cos(ang_r), jnp.sin(ang_r), jnp.cos(ang_c), jnp.sin(ang_c)
    cos_t = jnp.concatenate([cr, cr, cc, cc], axis=-1)
    sin_t = jnp.concatenate([-sr, sr, -sc, sc], axis=-1)
    return cos_t, sin_t


def _swap_quarters(x):
    lane = lax.broadcasted_iota(jnp.int32, x.shape, x.ndim - 1)
    up = pltpu.roll(x, HEAD_DIM - 32, x.ndim - 1)
    down = pltpu.roll(x, 32, x.ndim - 1)
    return jnp.where((lane % 64) < 32, up, down)


def _qk_prep(proj, g_q, g_k, cos_t, sin_t, *, tm=512):
    s, n = proj.shape
    tm = min(tm, s)

    def body(x_ref, gq_ref, gk_ref, c_ref, s_ref, o_ref):
        col = pl.program_id(1)
        x = x_ref[...]

        @pl.when(col < COL_VA)
        def _():
            g = jnp.where(col < COL_KA, gq_ref[...], gk_ref[...])
            xn = x * lax.rsqrt(jnp.mean(x * x, axis=-1, keepdims=True) + EPS) * g
            o_ref[...] = (xn * c_ref[...] + _swap_quarters(xn) * s_ref[...]).astype(BF16)

        @pl.when(col >= COL_VA)
        def _():
            o_ref[...] = x.astype(BF16)

    blk = pl.BlockSpec((tm, HEAD_DIM), lambda i, j: (i, j))
    tab = pl.BlockSpec((tm, HEAD_DIM), lambda i, j: (i, 0))
    vec = pl.BlockSpec((1, HEAD_DIM), lambda i, j: (0, 0))
    return pl.pallas_call(
        body,
        name="qk_prep",
        grid=(s // tm, n // HEAD_DIM),
        in_specs=[blk, vec, vec, tab, tab],
        out_specs=blk,
        out_shape=jax.ShapeDtypeStruct((s, n), BF16),
        compiler_params=_params(("parallel", "parallel")),
    )(proj, g_q, g_k, cos_t, sin_t)


def _qk_bwd(dpre, proj, g_q, g_k, cos_t, sin_t, *, tm=512):
    s, n = proj.shape
    tm = min(tm, s)
    n_i = s // tm

    def body(d_ref, x_ref, gq_ref, gk_ref, c_ref, s_ref, o_ref, dgq_ref, dgk_ref):
        col = pl.program_id(0)
        i = pl.program_id(1)
        d = d_ref[...]

        @pl.when(jnp.logical_and(col == 0, i == 0))
        def _():
            dgq_ref[...] = jnp.zeros_like(dgq_ref)
            dgk_ref[...] = jnp.zeros_like(dgk_ref)

        @pl.when(col < COL_VA)
        def _():
            x = x_ref[...]
            g = jnp.where(col < COL_KA, gq_ref[...], gk_ref[...])
            dn = d * c_ref[...] + _swap_quarters(d * s_ref[...])
            r = lax.rsqrt(jnp.mean(x * x, axis=-1, keepdims=True) + EPS)
            xh = x * r
            dng = dn * g
            o_ref[...] = (r * (dng - xh * jnp.mean(dng * xh, axis=-1, keepdims=True))).astype(BF16)
            part = jnp.sum(dn * xh, axis=0, keepdims=True)

            @pl.when(col < COL_KA)
            def _():
                dgq_ref[...] += part

            @pl.when(col >= COL_KA)
            def _():
                dgk_ref[...] += part

        @pl.when(col >= COL_VA)
        def _():
            o_ref[...] = d.astype(BF16)

    blk = pl.BlockSpec((tm, HEAD_DIM), lambda j, i: (i, j))
    tab = pl.BlockSpec((tm, HEAD_DIM), lambda j, i: (i, 0))
    vec = pl.BlockSpec((1, HEAD_DIM), lambda j, i: (0, 0))
    return pl.pallas_call(
        body,
        name="qk_bwd",
        grid=(n // HEAD_DIM, n_i),
        in_specs=[blk, blk, vec, vec, tab, tab],
        out_specs=[blk, vec, vec],
        out_shape=[
            jax.ShapeDtypeStruct((s, n), BF16),
            jax.ShapeDtypeStruct((1, HEAD_DIM), F32),
            jax.ShapeDtypeStruct((1, HEAD_DIM), F32),
        ],
        compiler_params=_params(("arbitrary", "arbitrary")),
    )(dpre, proj, g_q, g_k, cos_t, sin_t)


_NT = (((1,), (1,)), ((), ()))
_TN = (((0,), (0,)), ((), ()))


def _attn_a_fwd(pb, *, tq=256):
    s = pb.shape[0]
    tq = min(tq, s)

    def body(q_ref, k_ref, v_ref, o_ref, lse_ref):
        sc = lax.dot_general(q_ref[...], k_ref[...], _NT, preferred_element_type=F32) * ATT_SCALE
        m = jnp.max(sc, axis=-1, keepdims=True)
        p = jnp.exp(sc - m)
        l = jnp.sum(p, axis=-1, keepdims=True)
        o = jnp.dot(p.astype(BF16), v_ref[...], preferred_element_type=F32)
        o_ref[...] = (o / l).astype(BF16)
        lse_ref[...] = jnp.broadcast_to(m + jnp.log(l), lse_ref.shape)

    return pl.pallas_call(
        body,
        name="attn_a_fwd",
        grid=(N_HEADS_A, s // tq),
        in_specs=[
            pl.BlockSpec((tq, HEAD_DIM), lambda h, i: (i, COL_QA + h)),
            pl.BlockSpec((s, HEAD_DIM), lambda h, i: (0, COL_KA + h // GROUP)),
            pl.BlockSpec((s, HEAD_DIM), lambda h, i: (0, COL_VA + h // GROUP)),
        ],
        out_specs=[
            pl.BlockSpec((tq, HEAD_DIM), lambda h, i: (i, h)),
            pl.BlockSpec((None, tq, HEAD_DIM), lambda h, i: (h, i, 0)),
        ],
        out_shape=[
            jax.ShapeDtypeStruct((s, N_HEADS_A * HEAD_DIM), BF16),
            jax.ShapeDtypeStruct((N_HEADS_A, s, HEAD_DIM), F32),
        ],
        compiler_params=_params(("parallel", "parallel")),
    )(pb, pb, pb)


def _attn_a_bwd(pb, att, datt, lse, *, tq=256):
    s = pb.shape[0]
    tq = min(tq, s)

    def body(q_ref, k_ref, v_ref, o_ref, do_ref, lse_ref, dq_ref, dk_ref, dv_ref):
        first = jnp.logical_and(pl.program_id(1) == 0, pl.program_id(2) == 0)
        q = q_ref[...]
        k = k_ref[...]
        do = do_ref[...]
        sc = lax.dot_general(q, k, _NT, preferred_element_type=F32) * ATT_SCALE
        p = jnp.exp(sc - lse_ref[...][:, :1])
        dp = lax.dot_general(do, v_ref[...], _NT, preferred_element_type=F32)
        delta = jnp.sum(do.astype(F32) * o_ref[...].astype(F32), axis=-1, keepdims=True)
        ds = (p * (dp - delta) * ATT_SCALE).astype(BF16)
        dq_ref[...] = jnp.dot(ds, k, preferred_element_type=F32)
        dk = lax.dot_general(ds, q, _TN, preferred_element_type=F32)
        dv = lax.dot_general(p.astype(BF16), do, _TN, preferred_element_type=F32)

        @pl.when(first)
        def _():
            dk_ref[...] = dk
            dv_ref[...] = dv

        @pl.when(jnp.logical_not(first))
        def _():
            dk_ref[...] += dk
            dv_ref[...] += dv

    qmap = lambda kv, g, i: (i, kv * GROUP + g)
    return pl.pallas_call(
        body,
        name="attn_a_bwd",
        grid=(N_KV_A, GROUP, s // tq),
        in_specs=[
            pl.BlockSpec((tq, HEAD_DIM), lambda kv, g, i: (i, COL_QA + kv * GROUP + g)),
            pl.BlockSpec((s, HEAD_DIM), lambda kv, g, i: (0, COL_KA + kv)),
            pl.BlockSpec((s, HEAD_DIM), lambda kv, g, i: (0, COL_VA + kv)),
            pl.BlockSpec((tq, HEAD_DIM), qmap),
            pl.BlockSpec((tq, HEAD_DIM), qmap),
            pl.BlockSpec((None, tq, HEAD_DIM), lambda kv, g, i: (kv * GROUP + g, i, 0)),
        ],
        out_specs=[
            pl.BlockSpec((tq, HEAD_DIM), qmap),
            pl.BlockSpec((s, HEAD_DIM), lambda kv, g, i: (0, kv)),
            pl.BlockSpec((s, HEAD_DIM), lambda kv, g, i: (0, kv)),
        ],
        out_shape=[
            jax.ShapeDtypeStruct((s, N_HEADS_A * HEAD_DIM), F32),
            jax.ShapeDtypeStruct((s, N_KV_A * HEAD_DIM), F32),
            jax.ShapeDtypeStruct((s, N_KV_A * HEAD_DIM), F32),
        ],
        compiler_params=_params(("arbitrary", "arbitrary", "arbitrary")),
    )(pb, pb, pb, att, datt, lse)


def _t5_bucket(rel):
    nb = N_BUCKETS // 2
    ret = jnp.where(rel > 0, nb, 0)
    n = jnp.abs(rel)
    max_exact = nb // 2
    nf = jnp.maximum(n, 1).astype(F32)
    large = max_exact + (jnp.log(nf / max_exact) / math.log(MAX_DISTANCE / max_exact)
                         * (nb - max_exact)).astype(jnp.int32)
    large = jnp.minimum(large, nb - 1)
    return ret + jnp.where(n < max_exact, n, large)


def _band_buckets():
    r = jnp.arange(BLOCK_Q, dtype=jnp.int32)
    j = jnp.arange(3 * BLOCK_Q, dtype=jnp.int32)
    return _t5_bucket((j[None, :] - BLOCK_Q) - r[:, None])


def _band_bias(bucket, table_ref, h):
    acc = jnp.zeros(bucket.shape, F32)
    for b in range(N_BUCKETS):
        acc = jnp.where(bucket == b, table_ref[b, h], acc)
    return acc


def _band_mask(n, s):
    r = lax.broadcasted_iota(jnp.int32, (BLOCK_Q, 3 * BLOCK_Q), 0)
    j = lax.broadcasted_iota(jnp.int32, (BLOCK_Q, 3 * BLOCK_Q), 1)
    rel = j - BLOCK_Q - r
    kabs = n * BLOCK_Q + j - BLOCK_Q
    return (jnp.abs(rel) <= WINDOW) & (kabs >= 0) & (kabs < s)


def _attn_b_fwd(pb, kpad, vpad, bucket, table, sink):
    s = pb.shape[0]
    nblk = s // BLOCK_Q

    def body(table_ref, sink_ref, q_ref, k_ref, v_ref, bucket_ref, o_ref, lse_ref, bias_ref):
        h = pl.program_id(0)
        n = pl.program_id(1)

        @pl.when(n == 0)
        def _():
            bias_ref[...] = _band_bias(bucket_ref[...], table_ref, h)

        start = pl.multiple_of(n * BLOCK_Q, BLOCK_Q)
        kb = k_ref[pl.ds(start, 3 * BLOCK_Q), :]
        vb = v_ref[pl.ds(start, 3 * BLOCK_Q), :]
        sc = lax.dot_general(q_ref[...], kb, _NT, preferred_element_type=F32) * ATT_SCALE + bias_ref[...]
        sc = jnp.where(_band_mask(n, s), sc, NEG_INF)
        snk = sink_ref[0, h]
        m = jnp.maximum(jnp.max(sc, axis=-1, keepdims=True), snk)
        p = jnp.exp(sc - m)
        l = jnp.sum(p, axis=-1, keepdims=True) + jnp.exp(snk - m)
        o = jnp.dot(p.astype(BF16), vb, preferred_element_type=F32)
        o_ref[...] = (o / l).astype(BF16)
        lse_ref[...] = jnp.broadcast_to(m + jnp.log(l), lse_ref.shape)

    smem = pl.BlockSpec(memory_space=pltpu.SMEM)
    return pl.pallas_call(
        body,
        name="attn_b_fwd",
        grid=(N_HEADS_B, nblk),
        in_specs=[
            smem,
            smem,
            pl.BlockSpec((BLOCK_Q, HEAD_DIM), lambda h, n: (n, COL_QB + h)),
            pl.BlockSpec((s + 2 * BLOCK_Q, HEAD_DIM), lambda h, n: (0, h // GROUP)),
            pl.BlockSpec((s + 2 * BLOCK_Q, HEAD_DIM), lambda h, n: (0, h // GROUP)),
            pl.BlockSpec((BLOCK_Q, 3 * BLOCK_Q), lambda h, n: (0, 0)),
        ],
        out_specs=[
            pl.BlockSpec((BLOCK_Q, HEAD_DIM), lambda h, n: (n, h)),
            pl.BlockSpec((None, BLOCK_Q, HEAD_DIM), lambda h, n: (h, n, 0)),
        ],
        out_shape=[
            jax.ShapeDtypeStruct((s, N_HEADS_B * HEAD_DIM), BF16),
            jax.ShapeDtypeStruct((N_HEADS_B, s, HEAD_DIM), F32),
        ],
        scratch_shapes=[pltpu.VMEM((BLOCK_Q, 3 * BLOCK_Q), F32)],
        compiler_params=_params(("arbitrary", "arbitrary")),
    )(table, sink, pb, kpad, vpad, bucket)


def _attn_b_bwd(pb, kpad, vpad, att, datt, lse, bucket, table, sink):
    s = pb.shape[0]
    nblk = s // BLOCK_Q
    sp = s + 2 * BLOCK_Q

    def body(table_ref, sink_ref, q_ref, k_ref, v_ref, o_ref, do_ref, lse_ref, bucket_ref,
             dq_ref, dk_ref, dv_ref, dtab_ref, dsink_ref, bias_ref, dbias_ref):
        kv = pl.program_id(0)
        g = pl.program_id(1)
        n = pl.program_id(2)
        h = kv * GROUP + g

        @pl.when(jnp.logical_and(g == 0, n == 0))
        def _():
            dk_ref[...] = jnp.zeros_like(dk_ref)
            dv_ref[...] = jnp.zeros_like(dv_ref)

        @pl.when(n == 0)
        def _():
            bias_ref[...] = _band_bias(bucket_ref[...], table_ref, h)
            dbias_ref[...] = jnp.zeros_like(dbias_ref)
            dsink_ref[...] = jnp.zeros_like(dsink_ref)

        start = pl.multiple_of(n * BLOCK_Q, BLOCK_Q)
        band = pl.ds(start, 3 * BLOCK_Q)
        q = q_ref[...]
        do = do_ref[...]
        kb = k_ref[band, :]
        vb = v_ref[band, :]
        lse_col = lse_ref[...][:, :1]
        sc = lax.dot_general(q, kb, _NT, preferred_element_type=F32) * ATT_SCALE + bias_ref[...]
        sc = jnp.where(_band_mask(n, s), sc, NEG_INF)
        p = jnp.exp(sc - lse_col)
        dp = lax.dot_general(do, vb, _NT, preferred_element_type=F32)
        delta = jnp.sum(do.astype(F32) * o_ref[...].astype(F32), axis=-1, keepdims=True)
        ds = p * (dp - delta)
        dsb = (ds * ATT_SCALE).astype(BF16)
        dq_ref[...] = jnp.dot(dsb, kb, preferred_element_type=F32)
        dk_ref[band, :] += lax.dot_general(dsb, q, _TN, preferred_element_type=F32)
        dv_ref[band, :] += lax.dot_general(p.astype(BF16), do, _TN, preferred_element_type=F32)
        dbias_ref[...] += ds
        p_sink = jnp.exp(sink_ref[0, h] - lse_col)
        dsink_ref[...] += jnp.broadcast_to(jnp.sum(-p_sink * delta, axis=0, keepdims=True), dsink_ref.shape)

        @pl.when(n == nblk - 1)
        def _():
            bucket_v = bucket_ref[...]
            acc = dbias_ref[...]
            row = lax.broadcasted_iota(jnp.int32, dtab_ref.shape, 0)
            tot = jnp.zeros(dtab_ref.shape, F32)
            for b in range(N_BUCKETS):
                tot = jnp.where(row == b, jnp.sum(jnp.where(bucket_v == b, acc, 0.0), keepdims=True), tot)
            dtab_ref[...] = tot

    smem = pl.BlockSpec(memory_space=pltpu.SMEM)
    qmap = lambda kv, g, n: (n, kv * GROUP + g)
    qmap_b = lambda kv, g, n: (n, N_HEADS_A + kv * GROUP + g)
    hmap = lambda kv, g, n: (kv * GROUP + g, 0, 0)
    return pl.pallas_call(
        body,
        name="attn_b_bwd",
        grid=(N_KV_B, GROUP, nblk),
        in_specs=[
            smem,
            smem,
            pl.BlockSpec((BLOCK_Q, HEAD_DIM), lambda kv, g, n: (n, COL_QB + kv * GROUP + g)),
            pl.BlockSpec((sp, HEAD_DIM), lambda kv, g, n: (0, kv)),
            pl.BlockSpec((sp, HEAD_DIM), lambda kv, g, n: (0, kv)),
            pl.BlockSpec((BLOCK_Q, HEAD_DIM), qmap_b),
            pl.BlockSpec((BLOCK_Q, HEAD_DIM), qmap_b),
            pl.BlockSpec((None, BLOCK_Q, HEAD_DIM), lambda kv, g, n: (kv * GROUP + g, n, 0)),
            pl.BlockSpec((BLOCK_Q, 3 * BLOCK_Q), lambda kv, g, n: (0, 0)),
        ],
        out_specs=[
            pl.BlockSpec((BLOCK_Q, HEAD_DIM), qmap),
            pl.BlockSpec((sp, HEAD_DIM), lambda kv, g, n: (0, kv)),
            pl.BlockSpec((sp, HEAD_DIM), lambda kv, g, n: (0, kv)),
            pl.BlockSpec((None, N_BUCKETS, HEAD_DIM), hmap),
            pl.BlockSpec((None, 1, HEAD_DIM), hmap),
        ],
        out_shape=[
            jax.ShapeDtypeStruct((s, N_HEADS_B * HEAD_DIM), F32),
            jax.ShapeDtypeStruct((sp, N_KV_B * HEAD_DIM), F32),
            jax.ShapeDtypeStruct((sp, N_KV_B * HEAD_DIM), F32),
            jax.ShapeDtypeStruct((N_HEADS_B, N_BUCKETS, HEAD_DIM), F32),
            jax.ShapeDtypeStruct((N_HEADS_B, 1, HEAD_DIM), F32),
        ],
        scratch_shapes=[pltpu.VMEM((BLOCK_Q, 3 * BLOCK_Q), F32), pltpu.VMEM((BLOCK_Q, 3 * BLOCK_Q), F32)],
        compiler_params=_params(("arbitrary", "arbitrary", "arbitrary")),
    )(table, sink, pb, kpad, vpad, att, datt, lse, bucket)


_MESH = pl.DeviceIdType.MESH
_ANY = pl.BlockSpec(memory_space=pl.ANY)


def _other_chips(x, y):
    return [(x, 1 - y), (1 - x, y), (1 - x, 1 - y)]


def _gather_weights(shards):
    n_t = len(shards)

    def body(*refs):
        src, dst = refs[:n_t], refs[n_t:2 * n_t]
        send_sems, recv_sems, local_sems = refs[2 * n_t:]
        x, y, c = lax.axis_index("x"), lax.axis_index("y"), lax.axis_index("c")
        me = 2 * x + y
        chips = _other_chips(x, y)
        local = [pltpu.make_async_copy(src[t], dst[t].at[me], local_sems.at[t]) for t in range(n_t)]
        for cp in local:
            cp.start()
        sends = []
        for t in range(n_t):
            for j, (px, py) in enumerate(chips):
                cp = pltpu.make_async_remote_copy(
                    src_ref=src[t], dst_ref=dst[t].at[me],
                    send_sem=send_sems.at[3 * t + j], recv_sem=recv_sems.at[3 * t + j],
                    device_id=(px, py, c), device_id_type=_MESH)
                cp.start()
                sends.append(cp)
        for t in range(n_t):
            for j, (px, py) in enumerate(chips):
                pltpu.make_async_remote_copy(
                    src_ref=src[t], dst_ref=dst[t].at[2 * px + py],
                    send_sem=send_sems.at[3 * t + j], recv_sem=recv_sems.at[3 * t + j],
                    device_id=(px, py, c), device_id_type=_MESH).wait_recv()
        for cp in sends:
            cp.wait_send()
        for cp in local:
            cp.wait()

    return pl.pallas_call(
        body,
        name="gather_weights",
        in_specs=[_ANY] * n_t,
        out_specs=[_ANY] * n_t,
        out_shape=[jax.ShapeDtypeStruct((N_CHIPS,) + w.shape, w.dtype) for w in shards],
        scratch_shapes=[
            pltpu.SemaphoreType.DMA((3 * n_t,)),
            pltpu.SemaphoreType.DMA((3 * n_t,)),
            pltpu.SemaphoreType.DMA((n_t,)),
        ],
    )(*shards)


def _exchange_grads(partials):
    n_t = len(partials)

    def body(*refs):
        src, dst = refs[:n_t], refs[n_t:2 * n_t]
        send_sems, recv_sems, local_sems = refs[2 * n_t:]
        x, y, c = lax.axis_index("x"), lax.axis_index("y"), lax.axis_index("c")
        me = 2 * x + y
        sibling = (x, y, 1 - c)
        chips = _other_chips(x, y)

        def copy(t, k, src_ref, slot, to):
            return pltpu.make_async_remote_copy(
                src_ref=src_ref, dst_ref=dst[t].at[slot],
                send_sem=send_sems.at[7 * t + k], recv_sem=recv_sems.at[7 * t + k],
                device_id=to, device_id_type=_MESH)

        local = [pltpu.make_async_copy(src[t].at[me], dst[t].at[2 * me + c], local_sems.at[t]) for t in range(n_t)]
        for cp in local:
            cp.start()
        first = []
        for t in range(n_t):
            for j, (px, py) in enumerate(chips):
                first.append(copy(t, 1 + j, src[t].at[2 * px + py], 2 * me + c, (px, py, c)))
            first.append(copy(t, 0, src[t].at[me], 2 * me + c, sibling))
        for cp in first:
            cp.start()
        passed = []
        for t in range(n_t):
            for j, (px, py) in enumerate(chips):
                slot = 2 * (2 * px + py) + c
                copy(t, 1 + j, dst[t].at[slot], slot, (px, py, c)).wait_recv()
                fwd = copy(t, 4 + j, dst[t].at[slot], slot, sibling)
                fwd.start()
                passed.append(fwd)
        for t in range(n_t):
            copy(t, 0, src[t].at[me], 2 * me + 1 - c, sibling).wait_recv()
            for j, (px, py) in enumerate(chips):
                slot = 2 * (2 * px + py) + 1 - c
                copy(t, 4 + j, dst[t].at[slot], slot, sibling).wait_recv()
        for cp in first + passed:
            cp.wait_send()
        for cp in local:
            cp.wait()

    return pl.pallas_call(
        body,
        name="exchange_grads",
        in_specs=[_ANY] * n_t,
        out_specs=[_ANY] * n_t,
        out_shape=[jax.ShapeDtypeStruct((N_DEV,) + g.shape[1:], g.dtype) for g in partials],
        scratch_shapes=[
            pltpu.SemaphoreType.DMA((7 * n_t,)),
            pltpu.SemaphoreType.DMA((7 * n_t,)),
            pltpu.SemaphoreType.DMA((n_t,)),
        ],
    )(*partials)


def _allreduce_small(pack):
    rows, d = pack.shape

    def body(p_ref, sum_ref, all_ref, send_sems, recv_sems):
        x, y, c = lax.axis_index("x"), lax.axis_index("y"), lax.axis_index("c")
        me = 4 * x + 2 * y + c
        all_ref[me] = p_ref[...]
        peers = []
        for dx in range(2):
            for dy in range(2):
                for dc in range(2):
                    if dx or dy or dc:
                        px = 1 - x if dx else x
                        py = 1 - y if dy else y
                        pc = 1 - c if dc else c
                        peers.append((4 * dx + 2 * dy + dc - 1, (px, py, pc)))
        sends = []
        for k, to in peers:
            cp = pltpu.make_async_remote_copy(
                src_ref=p_ref, dst_ref=all_ref.at[me], send_sem=send_sems.at[k], recv_sem=recv_sems.at[k],
                device_id=to, device_id_type=_MESH)
            cp.start()
            sends.append(cp)
        for k, (px, py, pc) in peers:
            pltpu.make_async_remote_copy(
                src_ref=p_ref, dst_ref=all_ref.at[4 * px + 2 * py + pc], send_sem=send_sems.at[k],
                recv_sem=recv_sems.at[k], device_id=(px, py, pc), device_id_type=_MESH).wait_recv()
        for cp in sends:
            cp.wait_send()
        tot = all_ref[0]
        for i in range(1, N_DEV):
            tot = tot + all_ref[i]
        sum_ref[...] = tot

    vm = pl.BlockSpec(memory_space=pltpu.VMEM)
    return pl.pallas_call(
        body,
        name="allreduce_small",
        in_specs=[vm],
        out_specs=vm,
        out_shape=jax.ShapeDtypeStruct((rows, d), F32),
        scratch_shapes=[
            pltpu.VMEM((N_DEV, rows, d), F32),
            pltpu.SemaphoreType.DMA((N_DEV - 1,)),
            pltpu.SemaphoreType.DMA((N_DEV - 1,)),
        ],
    )(pack)


def _adamw_math(w, g, m, v):
    m = ADAM_B1 * m + (1.0 - ADAM_B1) * g
    v = ADAM_B2 * v + (1.0 - ADAM_B2) * (g * g)
    m_hat = m / (1.0 - ADAM_B1 ** ADAM_STEP)
    v_hat = v / (1.0 - ADAM_B2 ** ADAM_STEP)
    delta = -ADAM_LR * (m_hat / (jnp.sqrt(v_hat) + ADAM_EPS) + ADAM_WD * w)
    return delta, m, v


def _sum_adamw(parts, w, m, v, *, name, tr=256):
    r, c = w.shape
    tr = min(tr, r)
    tc = min(c, 1024)

    def body(p_ref, w_ref, m_ref, v_ref, g_ref, d_ref, m2_ref, v2_ref):
        g = p_ref[0].astype(F32)
        for i in range(1, N_DEV):
            g = g + p_ref[i].astype(F32)
        delta, m2, v2 = _adamw_math(w_ref[...], g, m_ref[...], v_ref[...])
        g_ref[...] = g
        d_ref[...] = delta
        m2_ref[...] = m2
        v2_ref[...] = v2

    blk = pl.BlockSpec((tr, tc), lambda i, j: (i, j))
    return pl.pallas_call(
        body,
        name=name,
        grid=(r // tr, c // tc),
        in_specs=[pl.BlockSpec((N_DEV, tr, tc), lambda i, j: (0, i, j)), blk, blk, blk],
        out_specs=[blk] * 4,
        out_shape=[jax.ShapeDtypeStruct((r, c), F32)] * 4,
        compiler_params=_params(("parallel", "parallel")),
    )(parts, w, m, v)


def _adamw_small(g, w, m, v):
    def body(g_ref, w_ref, m_ref, v_ref, d_ref, m2_ref, v2_ref):
        delta, m2, v2 = _adamw_math(w_ref[...], g_ref[...], m_ref[...], v_ref[...])
        d_ref[...] = delta
        m2_ref[...] = m2
        v2_ref[...] = v2

    vm = pl.BlockSpec(memory_space=pltpu.VMEM)
    return pl.pallas_call(
        body,
        name="adamw_small",
        in_specs=[vm] * 4,
        out_specs=[vm] * 3,
        out_shape=[jax.ShapeDtypeStruct(g.shape, F32)] * 3,
    )(g, w, m, v)


def _relu2_epilogue(acc):
    ra = jnp.maximum(acc, 0.0)
    return ra * ra, ra


def _local_step(x, p, target, wts, small):
    s, d = x.shape
    cos_t, sin_t = _rope_tables(s)
    bucket = _band_buckets()
    p_bf = p.astype(BF16)

    u = _rms_fwd(x, small["attn_norm_g"], name="norm_attn")
    proj = _matmul(u, wts["w_in"], mode="nn", out_dtypes=[F32], name="mm_in", bn=768)
    pb = _qk_prep(proj, small["q_norm_g"], small["k_norm_g"], cos_t, sin_t)
    oa, lse_a = _attn_a_fwd(pb)
    pad = ((BLOCK_Q, BLOCK_Q), (0, 0))
    kpad = jnp.pad(pb[:, COL_KB * HEAD_DIM:COL_VB * HEAD_DIM], pad)
    vpad = jnp.pad(pb[:, COL_VB * HEAD_DIM:], pad)
    ob, lse_b = _attn_b_fwd(pb, kpad, vpad, bucket, small["rel_bias_table"], small["sink_logits"])
    att = jnp.concatenate([oa, ob], axis=-1)
    h1 = _matmul(att, wts["w_out"], mode="nn", out_dtypes=[F32], name="mm_out",
                 epilogue=lambda acc, res: (acc + res,), extras=(x,))
    mn = _rms_fwd(h1, small["mlp_norm_g"], name="norm_mlp")
    r, ra = _matmul(mn, wts["w_up"], mode="nn", out_dtypes=[BF16, BF16], name="mm_up", epilogue=_relu2_epilogue)
    h2 = _matmul(r, wts["w_down"], mode="nn", out_dtypes=[F32], name="mm_down",
                 epilogue=lambda acc, res: (acc + res,), extras=(h1,))
    ng = _rms_fwd(h2, small["gate_norm_g"], name="norm_gate")
    gate = _matmul(ng, wts["w_gate"], mode="nn", out_dtypes=[F32], name="mm_gate",
                   epilogue=lambda acc: (1.0 / (1.0 + jnp.exp(-acc)),))
    pp = _matmul(p_bf, wts["ple_w"], mode="nn", out_dtypes=[F32], name="mm_ple", bn=512)
    dh3, dz, dpp, dg_final, dg_ple, loss = _tail(h2, gate, pp, target, small["ple_norm_g"], small["final_norm_g"])

    dng = _matmul(dz, wts["w_gate"], mode="nt", out_dtypes=[F32], name="mm_gate_dx")
    gw_gate = _matmul(ng, dz, mode="tn", out_dtypes=[BF16], name="mm_gate_dw")
    gw_ple = _matmul(p_bf, dpp, mode="tn", out_dtypes=[BF16], name="mm_ple_dw", bn=512, out_stack=N_CHIPS)
    dh2, dh2_bf, dg_gate = _rms_bwd(h2, dng, small["gate_norm_g"], dh3, name="norm_gate_bwd", want_bf16=True)
    da = _matmul(dh2_bf, wts["w_down"], mode="nt", out_dtypes=[BF16], name="mm_down_dx",
                 epilogue=lambda acc, ra_v: (acc * (2.0 * ra_v.astype(F32)),), extras=(ra,))
    gw_down = _matmul(r, dh2_bf, mode="tn", out_dtypes=[BF16], name="mm_down_dw")
    gw_up = _matmul(mn, da, mode="tn", out_dtypes=[BF16], name="mm_up_dw", out_stack=N_CHIPS)
    dmn = _matmul(da, wts["w_up"], mode="nt", out_dtypes=[F32], name="mm_up_dx")
    dh1, dh1_bf, dg_mlp = _rms_bwd(h1, dmn, small["mlp_norm_g"], dh2, name="norm_mlp_bwd", want_bf16=True)
    datt = _matmul(dh1_bf, wts["w_out"], mode="nt", out_dtypes=[BF16], name="mm_out_dx")
    gw_out = _matmul(att, dh1_bf, mode="tn", out_dtypes=[BF16], name="mm_out_dw")
    dqb, dkpad, dvpad, dtab, dsink = _attn_b_bwd(pb, kpad, vpad, att, datt, lse_b, bucket,
                                                  small["rel_bias_table"], small["sink_logits"])
    dqa, dka, dva = _attn_a_bwd(pb, att, datt, lse_a)
    dpre = jnp.concatenate([dqa, dka, dva, dqb, dkpad[BLOCK_Q:-BLOCK_Q], dvpad[BLOCK_Q:-BLOCK_Q]], axis=-1)
    dproj, dg_q, dg_k = _qk_bwd(dpre, proj, small["q_norm_g"], small["k_norm_g"], cos_t, sin_t)
    gw_in = _matmul(u, dproj, mode="tn", out_dtypes=[BF16], name="mm_in_dw", bn=768, out_stack=N_CHIPS)
    du = _matmul(dproj, wts["w_in"], mode="nt", out_dtypes=[F32], name="mm_in_dx", bk=768)
    grad_x, dg_attn = _rms_bwd(x, du, small["attn_norm_g"], dh1, name="norm_attn_bwd", want_bf16=False)

    big = {"w_in": gw_in, "w_out": gw_out, "w_up": gw_up, "w_down": gw_down, "ple_w": gw_ple, "w_gate": gw_gate}
    small_g = {
        "attn_norm_g": dg_attn, "mlp_norm_g": dg_mlp, "ple_norm_g": dg_ple, "gate_norm_g": dg_gate,
        "final_norm_g": dg_final, "q_norm_g": dg_q, "k_norm_g": dg_k,
        "sink_logits": dsink[:, 0, 0][None, :], "rel_bias_table": dtab[:, :, 0].T,
    }
    return loss, grad_x, big, small_g


_SMALL_ROWS = ["attn_norm_g", "mlp_norm_g", "ple_norm_g", "gate_norm_g", "final_norm_g"]
_PACK_ROWS = 8


def _pack_small(vals, d):
    rows = [vals[n].reshape(1, d) for n in _SMALL_ROWS]
    misc = jnp.concatenate([
        vals["q_norm_g"].reshape(1, HEAD_DIM), vals["k_norm_g"].reshape(1, HEAD_DIM),
        jnp.pad(vals["sink_logits"].reshape(1, N_HEADS_B), ((0, 0), (0, HEAD_DIM - N_HEADS_B))),
        vals["rel_bias_table"].reshape(1, N_BUCKETS * N_HEADS_B)], axis=1)
    rows.append(jnp.pad(misc, ((0, 0), (0, d - misc.shape[1]))))
    rows.append(jnp.zeros((_PACK_ROWS - len(rows), d), F32))
    return jnp.concatenate(rows, axis=0).astype(F32)


def _unpack_small(pack, shapes):
    out = {n: pack[i].reshape(shapes[n]) for i, n in enumerate(_SMALL_ROWS)}
    misc = pack[len(_SMALL_ROWS)]
    out["q_norm_g"] = misc[:HEAD_DIM].reshape(shapes["q_norm_g"])
    out["k_norm_g"] = misc[HEAD_DIM:2 * HEAD_DIM].reshape(shapes["k_norm_g"])
    out["sink_logits"] = misc[2 * HEAD_DIM:2 * HEAD_DIM + N_HEADS_B].reshape(shapes["sink_logits"])
    out["rel_bias_table"] = misc[3 * HEAD_DIM:3 * HEAD_DIM + N_BUCKETS * N_HEADS_B].reshape(shapes["rel_bias_table"])
    return out


_WEIGHTS = ["attn_norm_g", "w_in", "q_norm_g", "k_norm_g", "sink_logits", "w_out", "mlp_norm_g", "w_up", "w_down",
            "ple_w", "ple_norm_g", "gate_norm_g", "w_gate", "rel_bias_table", "final_norm_g"]
_BIG = ["w_in", "w_out", "w_up", "w_down", "ple_w", "w_gate"]
_ROW_SHARDED = ["w_out", "w_down", "w_gate"]


def kernel(x, p, attn_norm_g, w_in, q_norm_g, k_norm_g, sink_logits, w_out, mlp_norm_g, w_up, w_down, ple_w, ple_norm_g, gate_norm_g, w_gate, rel_bias_table, final_norm_g, loss_target, m_attn_norm_g, m_w_in, m_q_norm_g, m_k_norm_g, m_sink_logits, m_w_out, m_mlp_norm_g, m_w_up, m_w_down, m_ple_w, m_ple_norm_g, m_gate_norm_g, m_w_gate, m_rel_bias_table, m_final_norm_g, v_attn_norm_g, v_w_in, v_q_norm_g, v_k_norm_g, v_sink_logits, v_w_out, v_mlp_norm_g, v_w_up, v_w_down, v_ple_w, v_ple_norm_g, v_gate_norm_g, v_w_gate, v_rel_bias_table, v_final_norm_g):
    given = dict(locals())
    w = {n: given[n] for n in _WEIGHTS}
    m = {n: given["m_" + n] for n in _WEIGHTS}
    v = {n: given["v_" + n] for n in _WEIGHTS}
    d = x.shape[-1]

    shards = [w[n][0].astype(BF16) for n in _BIG]
    gathered = dict(zip(_BIG, _gather_weights(shards)))
    for n in _ROW_SHARDED:
        g = gathered[n]
        gathered[n] = g.reshape(g.shape[0] * g.shape[1], g.shape[2])
    small = {
        "attn_norm_g": w["attn_norm_g"], "mlp_norm_g": w["mlp_norm_g"], "ple_norm_g": w["ple_norm_g"],
        "gate_norm_g": w["gate_norm_g"], "final_norm_g": w["final_norm_g"].reshape(1, d),
        "q_norm_g": w["q_norm_g"], "k_norm_g": w["k_norm_g"], "sink_logits": w["sink_logits"],
        "rel_bias_table": w["rel_bias_table"],
    }

    loss_part, grad_x, big_g, small_g = _local_step(x[0], p[0, 0], loss_target[0], gathered, small)
    loss = lax.psum(loss_part[0, 0], ("x", "y", "c"))

    partials = []
    for n in _BIG:
        g = big_g[n]
        if n in _ROW_SHARDED:
            g = g.reshape(N_CHIPS, g.shape[0] // N_CHIPS, g.shape[1])
        partials.append(g)
    exchanged = dict(zip(_BIG, _exchange_grads(partials)))
    grads, deltas, new_m, new_v = {}, {}, {}, {}
    for n in _BIG:
        shape = w[n].shape
        res = _sum_adamw(exchanged[n], w[n][0], m[n][0], v[n][0], name="adamw_" + n)
        grads[n], deltas[n], new_m[n], new_v[n] = [t.reshape(shape) for t in res]

    shapes = {n: w[n].shape for n in _WEIGHTS if n not in _BIG}
    g_small = _allreduce_small(_pack_small(small_g, d))
    d_small, m_small, v_small = _adamw_small(g_small, _pack_small(w, d), _pack_small(m, d), _pack_small(v, d))
    grads.update(_unpack_small(g_small, shapes))
    deltas.update(_unpack_small(d_small, shapes))
    new_m.update(_unpack_small(m_small, shapes))
    new_v.update(_unpack_small(v_small, shapes))

    return (loss, grad_x[None], *[grads[n] for n in _WEIGHTS], *[deltas[n] for n in _WEIGHTS],
            *[new_m[n] for n in _WEIGHTS], *[new_v[n] for n in _WEIGHTS])
```

```python
import functools
import math

import jax
import jax.numpy as jnp
import numpy as np
from jax import lax
from jax.experimental import pallas as pl
from jax.experimental.pallas import tpu as pltpu

F32 = jnp.float32
BF16 = jnp.bfloat16

HEAD_DIM = 128
N_HEADS_A = 8
N_KV_A = 2
N_HEADS_B = 8
N_KV_B = 2
GROUP = 4
GRID_W = 64
BLOCK_Q = 128
WINDOW = 128
N_BUCKETS = 32
MAX_DISTANCE = 128
ROPE_THETA = 10000.0
EPS = 1e-6
NEG_INF = -1e30
ATT_SCALE = HEAD_DIM ** -0.5

ADAM_LR = 0.001
ADAM_B1 = 0.9
ADAM_B2 = 0.999
ADAM_EPS = 1e-08
ADAM_WD = 0.01
ADAM_STEP = 10

N_CHIPS = 4
N_DEV = 8
COL_QA, COL_KA, COL_VA, COL_QB, COL_KB, COL_VB = 0, 8, 10, 12, 20, 22
N_COLS = 24

VMEM_LIMIT = 52 * 1024 * 1024


def _params(sem=None):
    return pltpu.CompilerParams(dimension_semantics=sem, vmem_limit_bytes=VMEM_LIMIT)


_ANY = pl.BlockSpec(memory_space=pl.ANY)


class _Comm:
    def __init__(self, inputs, out_shapes, sems, start, finish):
        self.inputs, self.out_shapes, self.sems = list(inputs), list(out_shapes), list(sems)
        self.start, self.finish = start, finish


def _call(body, *, name, grid, in_specs, out_specs, out_shape, args, scratch_shapes=(), sem=None, comm=None):
    in_specs, out_specs, out_shape = list(in_specs), list(out_specs), list(out_shape)
    scratch_shapes = list(scratch_shapes)
    if comm is None:
        res = pl.pallas_call(
            body, name=name, grid=grid, in_specs=in_specs, out_specs=out_specs, out_shape=out_shape,
            scratch_shapes=scratch_shapes, compiler_params=_params(sem))(*args)
        return list(res), []
    n_in, n_out, n_sc = len(in_specs), len(out_specs), len(scratch_shapes)
    c_in, c_out = len(comm.inputs), len(comm.out_shapes)

    def hosted(*refs):
        pos = [0]

        def take(n):
            pos[0] += n
            return refs[pos[0] - n:pos[0]]

        ins, c_ins, outs, c_outs, scr = take(n_in), take(c_in), take(n_out), take(c_out), take(n_sc)
        c_sems = refs[pos[0]:]
        ids = [pl.program_id(a) for a in range(len(grid))]
        first = functools.reduce(jnp.logical_and, [i == 0 for i in ids])
        last = functools.reduce(jnp.logical_and, [i == g - 1 for i, g in zip(ids, grid)])

        @pl.when(first)
        def _():
            comm.start(c_ins, c_outs, c_sems)

        body(*ins, *outs, *scr)

        @pl.when(last)
        def _():
            comm.finish(c_ins, c_outs, c_sems)

    res = pl.pallas_call(
        hosted, name=name, grid=grid, in_specs=in_specs + [_ANY] * c_in, out_specs=out_specs + [_ANY] * c_out,
        out_shape=out_shape + comm.out_shapes, scratch_shapes=scratch_shapes + comm.sems,
        compiler_params=_params(("arbitrary",) * len(grid)))(*args, *comm.inputs)
    return list(res[:n_out]), list(res[n_out:])


def _matmul(a, b, *, mode, out_dtypes, name, epilogue=None, extras=(), bm=1024, bn=1024, bk=2048,
            out_stack=0, comm=None):
    stacked = b.ndim == 3
    if mode == "nn":
        m, k = a.shape
        if stacked:
            nj, kb, ns = b.shape
            n, ks = nj * ns, k
        else:
            kb, n = b.shape
            ns, ks = n, k
        dn = (((1,), (0,)), ((), ()))
    elif mode == "nt":
        m, k = a.shape
        if stacked:
            nj, n, ks = b.shape
            kb = nj * ks
        else:
            n, kb = b.shape
            ks = kb
        ns = n
        dn = (((1,), (1,)), ((), ()))
    else:
        k, m = a.shape
        kb, n = b.shape
        ns, ks = n, k
        dn = (((0,), (0,)), ((), ()))
    assert k == kb and not (stacked and mode == "tn")
    ns_out = n // out_stack if out_stack else n
    bm, bn, bk = min(bm, m), min(bn, ns, ns_out), min(bk, ks)
    assert m % bm == 0 and ns % bn == 0 and ns_out % bn == 0 and ks % bk == 0
    gm, gn, gk = m // bm, n // bn, k // bk

    if mode == "tn":
        a_spec = pl.BlockSpec((bk, bm), lambda i, j, q: (q, i))
    else:
        a_spec = pl.BlockSpec((bm, bk), lambda i, j, q: (i, q))
    if mode == "nt":
        if stacked:
            per = ks // bk
            b_spec = pl.BlockSpec((None, bn, bk), lambda i, j, q: (q // per, j, q % per))
        else:
            b_spec = pl.BlockSpec((bn, bk), lambda i, j, q: (j, q))
    else:
        if stacked:
            per = ns // bn
            b_spec = pl.BlockSpec((None, bk, bn), lambda i, j, q: (j // per, q, j % per))
        else:
            b_spec = pl.BlockSpec((bk, bn), lambda i, j, q: (q, j))
    ex_spec = pl.BlockSpec((bm, bn), lambda i, j, q: (i, j))
    if out_stack:
        per_o = ns_out // bn
        o_spec = pl.BlockSpec((None, bm, bn), lambda i, j, q: (j // per_o, i, j % per_o))
        o_shape = (out_stack, m, ns_out)
    else:
        o_spec = ex_spec
        o_shape = (m, n)
    n_ex, n_out = len(extras), len(out_dtypes)

    def body(a_ref, b_ref, *rest):
        ex, outs = rest[:n_ex], rest[n_ex:n_ex + n_out]
        part = lax.dot_general(a_ref[...], b_ref[...], dn, preferred_element_type=F32)

        def finish(acc):
            res = epilogue(acc, *[e[...] for e in ex]) if epilogue else (acc,)
            for o, r in zip(outs, res):
                o[...] = r.astype(o.dtype)

        if gk == 1:
            finish(part)
        else:
            acc_ref = rest[-1]
            q = pl.program_id(2)

            @pl.when(q == 0)
            def _():
                acc_ref[...] = part

            @pl.when(q > 0)
            def _():
                acc_ref[...] += part

            @pl.when(q == gk - 1)
            def _():
                finish(acc_ref[...])

    res, c_res = _call(
        body, name=name, grid=(gm, gn, gk),
        in_specs=[a_spec, b_spec] + [ex_spec] * n_ex,
        out_specs=[o_spec] * n_out,
        out_shape=[jax.ShapeDtypeStruct(o_shape, dt) for dt in out_dtypes],
        scratch_shapes=[pltpu.VMEM((bm, bn), F32)] if gk > 1 else [],
        sem=("parallel", "parallel", "arbitrary"), args=(a, b, *extras), comm=comm)
    res = res[0] if n_out == 1 else res
    return res if comm is None else (res, c_res)


def _rms_fwd(x, g, *, name, tm=256, comm=None):
    s, d = x.shape
    tm = min(tm, s)

    def body(x_ref, g_ref, o_ref):
        xf = x_ref[...]
        r = lax.rsqrt(jnp.mean(xf * xf, axis=-1, keepdims=True) + EPS)
        o_ref[...] = (xf * r * g_ref[...]).astype(o_ref.dtype)

    res, c_res = _call(
        body, name=name, grid=(s // tm,),
        in_specs=[pl.BlockSpec((tm, d), lambda i: (i, 0)), pl.BlockSpec((1, d), lambda i: (0, 0))],
        out_specs=[pl.BlockSpec((tm, d), lambda i: (i, 0))],
        out_shape=[jax.ShapeDtypeStruct((s, d), BF16)],
        sem=("parallel",), args=(x, g), comm=comm)
    return res[0] if comm is None else (res[0], c_res)


def _rms_bwd(x, dy, g, add, *, name, want_bf16, tm=256):
    s, d = x.shape
    tm = min(tm, s)

    def body(x_ref, dy_ref, g_ref, add_ref, dx_ref, *rest):
        dg_ref = rest[-1]
        i = pl.program_id(0)
        xf = x_ref[...]
        dyf = dy_ref[...].astype(F32)
        r = lax.rsqrt(jnp.mean(xf * xf, axis=-1, keepdims=True) + EPS)
        xh = xf * r
        dyg = dyf * g_ref[...]
        dx = r * (dyg - xh * jnp.mean(dyg * xh, axis=-1, keepdims=True))
        tot = add_ref[...] + dx
        dx_ref[...] = tot
        if want_bf16:
            rest[0][...] = tot.astype(BF16)
        part = jnp.sum(dyf * xh, axis=0, keepdims=True)

        @pl.when(i == 0)
        def _():
            dg_ref[...] = part

        @pl.when(i > 0)
        def _():
            dg_ref[...] += part

    row = pl.BlockSpec((tm, d), lambda i: (i, 0))
    vec = pl.BlockSpec((1, d), lambda i: (0, 0))
    out_specs = [row] + ([row] if want_bf16 else []) + [vec]
    out_shape = [jax.ShapeDtypeStruct((s, d), F32)]
    if want_bf16:
        out_shape.append(jax.ShapeDtypeStruct((s, d), BF16))
    out_shape.append(jax.ShapeDtypeStruct((1, d), F32))
    return pl.pallas_call(
        body,
        name=name,
        grid=(s // tm,),
        in_specs=[row, row, vec, row],
        out_specs=out_specs,
        out_shape=out_shape,
        compiler_params=_params(("arbitrary",)),
    )(x, dy, g, add)


def _tail(h2, gate, pp, target, g_ple, g_final, *, tm=128):
    s, d = h2.shape
    tm = min(tm, s)

    def body(h2_ref, gate_ref, pp_ref, t_ref, gp_ref, gf_ref, dh3_ref, dz_ref, dpp_ref, dgf_ref, dgp_ref, loss_ref):
        i = pl.program_id(0)
        ppf = pp_ref[...]
        gate_v = gate_ref[...]
        r_p = lax.rsqrt(jnp.mean(ppf * ppf, axis=-1, keepdims=True) + EPS)
        eh = ppf * r_p
        e = eh * gp_ref[...]
        h3 = h2_ref[...] + gate_v * e
        r_f = lax.rsqrt(jnp.mean(h3 * h3, axis=-1, keepdims=True) + EPS)
        yh = h3 * r_f
        diff = yh * gf_ref[...] - t_ref[...]
        loss_part = 0.5 * jnp.sum(jnp.mean(diff * diff, axis=-1, keepdims=True), axis=0, keepdims=True)
        dy = diff / d
        dgf = jnp.sum(dy * yh, axis=0, keepdims=True)
        dyg = dy * gf_ref[...]
        dh3 = r_f * (dyg - yh * jnp.mean(dyg * yh, axis=-1, keepdims=True))
        dh3_ref[...] = dh3
        de = dh3 * gate_v
        dz_ref[...] = (dh3 * e * gate_v * (1.0 - gate_v)).astype(BF16)
        dgp = jnp.sum(de * eh, axis=0, keepdims=True)
        deg = de * gp_ref[...]
        dpp_ref[...] = (r_p * (deg - eh * jnp.mean(deg * eh, axis=-1, keepdims=True))).astype(BF16)
        loss_row = jnp.broadcast_to(loss_part, (1, 128))

        @pl.when(i == 0)
        def _():
            dgf_ref[...] = dgf
            dgp_ref[...] = dgp
            loss_ref[...] = loss_row

        @pl.when(i > 0)
        def _():
            dgf_ref[...] += dgf
            dgp_ref[...] += dgp
            loss_ref[...] += loss_row

    row = pl.BlockSpec((tm, d), lambda i: (i, 0))
    vec = pl.BlockSpec((1, d), lambda i: (0, 0))
    return pl.pallas_call(
        body,
        name="tail_fwd_bwd",
        grid=(s // tm,),
        in_specs=[row, row, row, row, vec, vec],
        out_specs=[row, row, row, vec, vec, pl.BlockSpec((1, 128), lambda i: (0, 0))],
        out_shape=[
            jax.ShapeDtypeStruct((s, d), F32),
            jax.ShapeDtypeStruct((s, d), BF16),
            jax.ShapeDtypeStruct((s, d), BF16),
            jax.ShapeDtypeStruct((1, d), F32),
            jax.ShapeDtypeStruct((1, d), F32),
            jax.ShapeDtypeStruct((1, 128), F32),
        ],
        compiler_params=_params(("arbitrary",)),
    )(h2, gate, pp, target, g_ple, g_final)


def _rope_tables(s):
    t = jnp.arange(s, dtype=jnp.int32)
    row = (t // GRID_W).astype(F32)
    col = (t % GRID_W).astype(F32)
    half = HEAD_DIM // 2
    inv_freq = ROPE_THETA ** (-jnp.arange(0, half, 2, dtype=F32) / half)
    ang_r = row[:, None] * inv_freq
    ang_c = col[:, None] * inv_freq
    cr, sr, cc, sc = jnp.cos(ang_r), jnp.sin(ang_r), jnp.cos(ang_c), jnp.sin(ang_c)
    cos_t = jnp.concatenate([cr, cr, cc, cc], axis=-1)
    sin_t = jnp.concatenate([-sr, sr, -sc, sc], axis=-1)
    return cos_t, sin_t


def _swap_quarters(x):
    lane = lax.broadcasted_iota(jnp.int32, x.shape, x.ndim - 1)
    up = pltpu.roll(x, HEAD_DIM - 32, x.ndim - 1)
    down = pltpu.roll(x, 32, x.ndim - 1)
    return jnp.where((lane % 64) < 32, up, down)


def _qk_prep(proj, g_q, g_k, cos_t, sin_t, *, tm=512):
    s, n = proj.shape
    tm = min(tm, s)

    def body(x_ref, gq_ref, gk_ref, c_ref, s_ref, o_ref):
        col = pl.program_id(1)
        x = x_ref[...]

        @pl.when(col < COL_VA)
        def _():
            g = jnp.where(col < COL_KA, gq_ref[...], gk_ref[...])
            xn = x * lax.rsqrt(jnp.mean(x * x, axis=-1, keepdims=True) + EPS) * g
            o_ref[...] = (xn * c_ref[...] + _swap_quarters(xn) * s_ref[...]).astype(BF16)

        @pl.when(col >= COL_VA)
        def _():
            o_ref[...] = x.astype(BF16)

    blk = pl.BlockSpec((tm, HEAD_DIM), lambda i, j: (i, j))
    tab = pl.BlockSpec((tm, HEAD_DIM), lambda i, j: (i, 0))
    vec = pl.BlockSpec((1, HEAD_DIM), lambda i, j: (0, 0))
    return pl.pallas_call(
        body,
        name="qk_prep",
        grid=(s // tm, n // HEAD_DIM),
        in_specs=[blk, vec, vec, tab, tab],
        out_specs=blk,
        out_shape=jax.ShapeDtypeStruct((s, n), BF16),
        compiler_params=_params(("parallel", "parallel")),
    )(proj, g_q, g_k, cos_t, sin_t)


def _qk_bwd(dpre, proj, g_q, g_k, cos_t, sin_t, *, tm=512, comm=None):
    s, n = proj.shape
    tm = min(tm, s)
    n_i = s // tm

    def body(d_ref, x_ref, gq_ref, gk_ref, c_ref, s_ref, o_ref, dgq_ref, dgk_ref):
        col = pl.program_id(0)
        i = pl.program_id(1)
        d = d_ref[...]

        @pl.when(jnp.logical_and(col == 0, i == 0))
        def _():
            dgq_ref[...] = jnp.zeros_like(dgq_ref)
            dgk_ref[...] = jnp.zeros_like(dgk_ref)

        @pl.when(col < COL_VA)
        def _():
            x = x_ref[...]
            g = jnp.where(col < COL_KA, gq_ref[...], gk_ref[...])
            dn = d * c_ref[...] + _swap_quarters(d * s_ref[...])
            r = lax.rsqrt(jnp.mean(x * x, axis=-1, keepdims=True) + EPS)
            xh = x * r
            dng = dn * g
            o_ref[...] = (r * (dng - xh * jnp.mean(dng * xh, axis=-1, keepdims=True))).astype(BF16)
            part = jnp.sum(dn * xh, axis=0, keepdims=True)

            @pl.when(col < COL_KA)
            def _():
                dgq_ref[...] += part

            @pl.when(col >= COL_KA)
            def _():
                dgk_ref[...] += part

        @pl.when(col >= COL_VA)
        def _():
            o_ref[...] = d.astype(BF16)

    blk = pl.BlockSpec((tm, HEAD_DIM), lambda j, i: (i, j))
    tab = pl.BlockSpec((tm, HEAD_DIM), lambda j, i: (i, 0))
    vec = pl.BlockSpec((1, HEAD_DIM), lambda j, i: (0, 0))
    res, c_res = _call(
        body, name="qk_bwd", grid=(n // HEAD_DIM, n_i),
        in_specs=[blk, blk, vec, vec, tab, tab],
        out_specs=[blk, vec, vec],
        out_shape=[
            jax.ShapeDtypeStruct((s, n), BF16),
            jax.ShapeDtypeStruct((1, HEAD_DIM), F32),
            jax.ShapeDtypeStruct((1, HEAD_DIM), F32),
        ],
        sem=("arbitrary", "arbitrary"), args=(dpre, proj, g_q, g_k, cos_t, sin_t), comm=comm)
    return res if comm is None else (res, c_res)


_NT = (((1,), (1,)), ((), ()))
_TN = (((0,), (0,)), ((), ()))


def _attn_a_fwd(pb, *, tq=256, comm=None):
    s = pb.shape[0]
    tq = min(tq, s)

    def body(q_ref, k_ref, v_ref, o_ref, lse_ref):
        sc = lax.dot_general(q_ref[...], k_ref[...], _NT, preferred_element_type=F32) * ATT_SCALE
        m = jnp.max(sc, axis=-1, keepdims=True)
        p = jnp.exp(sc - m)
        l = jnp.sum(p, axis=-1, keepdims=True)
        o = jnp.dot(p.astype(BF16), v_ref[...], preferred_element_type=F32)
        o_ref[...] = (o / l).astype(BF16)
        lse_ref[...] = jnp.broadcast_to(m + jnp.log(l), lse_ref.shape)

    res, c_res = _call(
        body, name="attn_a_fwd", grid=(N_HEADS_A, s // tq),
        in_specs=[
            pl.BlockSpec((tq, HEAD_DIM), lambda h, i: (i, COL_QA + h)),
            pl.BlockSpec((s, HEAD_DIM), lambda h, i: (0, COL_KA + h // GROUP)),
            pl.BlockSpec((s, HEAD_DIM), lambda h, i: (0, COL_VA + h // GROUP)),
        ],
        out_specs=[
            pl.BlockSpec((tq, HEAD_DIM), lambda h, i: (i, h)),
            pl.BlockSpec((None, tq, HEAD_DIM), lambda h, i: (h, i, 0)),
        ],
        out_shape=[
            jax.ShapeDtypeStruct((s, N_HEADS_A * HEAD_DIM), BF16),
            jax.ShapeDtypeStruct((N_HEADS_A, s, HEAD_DIM), F32),
        ],
        sem=("parallel", "parallel"), args=(pb, pb, pb), comm=comm)
    return res if comm is None else (res, c_res)


def _attn_a_bwd(pb, att, datt, lse, *, tq=256, comm=None):
    s = pb.shape[0]
    tq = min(tq, s)

    def body(q_ref, k_ref, v_ref, o_ref, do_ref, lse_ref, dq_ref, dk_ref, dv_ref):
        first = jnp.logical_and(pl.program_id(1) == 0, pl.program_id(2) == 0)
        q = q_ref[...]
        k = k_ref[...]
        do = do_ref[...]
        sc = lax.dot_general(q, k, _NT, preferred_element_type=F32) * ATT_SCALE
        p = jnp.exp(sc - lse_ref[...][:, :1])
        dp = lax.dot_general(do, v_ref[...], _NT, preferred_element_type=F32)
        delta = jnp.sum(do.astype(F32) * o_ref[...].astype(F32), axis=-1, keepdims=True)
        ds = (p * (dp - delta) * ATT_SCALE).astype(BF16)
        dq_ref[...] = jnp.dot(ds, k, preferred_element_type=F32)
        dk = lax.dot_general(ds, q, _TN, preferred_element_type=F32)
        dv = lax.dot_general(p.astype(BF16), do, _TN, preferred_element_type=F32)

        @pl.when(first)
        def _():
            dk_ref[...] = dk
            dv_ref[...] = dv

        @pl.when(jnp.logical_not(first))
        def _():
            dk_ref[...] += dk
            dv_ref[...] += dv

    qmap = lambda kv, g, i: (i, kv * GROUP + g)
    res, c_res = _call(
        body, name="attn_a_bwd", grid=(N_KV_A, GROUP, s // tq),
        in_specs=[
            pl.BlockSpec((tq, HEAD_DIM), lambda kv, g, i: (i, COL_QA + kv * GROUP + g)),
            pl.BlockSpec((s, HEAD_DIM), lambda kv, g, i: (0, COL_KA + kv)),
            pl.BlockSpec((s, HEAD_DIM), lambda kv, g, i: (0, COL_VA + kv)),
            pl.BlockSpec((tq, HEAD_DIM), qmap),
            pl.BlockSpec((tq, HEAD_DIM), qmap),
            pl.BlockSpec((None, tq, HEAD_DIM), lambda kv, g, i: (kv * GROUP + g, i, 0)),
        ],
        out_specs=[
            pl.BlockSpec((tq, HEAD_DIM), qmap),
            pl.BlockSpec((s, HEAD_DIM), lambda kv, g, i: (0, kv)),
            pl.BlockSpec((s, HEAD_DIM), lambda kv, g, i: (0, kv)),
        ],
        out_shape=[
            jax.ShapeDtypeStruct((s, N_HEADS_A * HEAD_DIM), F32),
            jax.ShapeDtypeStruct((s, N_KV_A * HEAD_DIM), F32),
            jax.ShapeDtypeStruct((s, N_KV_A * HEAD_DIM), F32),
        ],
        sem=("arbitrary", "arbitrary", "arbitrary"), args=(pb, pb, pb, att, datt, lse), comm=comm)
    return res if comm is None else (res, c_res)


def _t5_bucket(rel):
    nb = N_BUCKETS // 2
    ret = jnp.where(rel > 0, nb, 0)
    n = jnp.abs(rel)
    max_exact = nb // 2
    nf = jnp.maximum(n, 1).astype(F32)
    large = max_exact + (jnp.log(nf / max_exact) / math.log(MAX_DISTANCE / max_exact)
                         * (nb - max_exact)).astype(jnp.int32)
    large = jnp.minimum(large, nb - 1)
    return ret + jnp.where(n < max_exact, n, large)


def _band_buckets():
    r = jnp.arange(BLOCK_Q, dtype=jnp.int32)
    j = jnp.arange(3 * BLOCK_Q, dtype=jnp.int32)
    return _t5_bucket((j[None, :] - BLOCK_Q) - r[:, None])


def _band_bias(bucket, table_ref, h):
    acc = jnp.zeros(bucket.shape, F32)
    for b in range(N_BUCKETS):
        acc = jnp.where(bucket == b, table_ref[b, h], acc)
    return acc


def _band_mask(n, s):
    r = lax.broadcasted_iota(jnp.int32, (BLOCK_Q, 3 * BLOCK_Q), 0)
    j = lax.broadcasted_iota(jnp.int32, (BLOCK_Q, 3 * BLOCK_Q), 1)
    rel = j - BLOCK_Q - r
    kabs = n * BLOCK_Q + j - BLOCK_Q
    return (jnp.abs(rel) <= WINDOW) & (kabs >= 0) & (kabs < s)


def _attn_b_fwd(pb, kpad, vpad, bucket, table, sink, *, comm=None):
    s = pb.shape[0]
    nblk = s // BLOCK_Q

    def body(table_ref, sink_ref, q_ref, k_ref, v_ref, bucket_ref, o_ref, lse_ref, bias_ref):
        h = pl.program_id(0)
        n = pl.program_id(1)

        @pl.when(n == 0)
        def _():
            bias_ref[...] = _band_bias(bucket_ref[...], table_ref, h)

        start = pl.multiple_of(n * BLOCK_Q, BLOCK_Q)
        kb = k_ref[pl.ds(start, 3 * BLOCK_Q), :]
        vb = v_ref[pl.ds(start, 3 * BLOCK_Q), :]
        sc = lax.dot_general(q_ref[...], kb, _NT, preferred_element_type=F32) * ATT_SCALE + bias_ref[...]
        sc = jnp.where(_band_mask(n, s), sc, NEG_INF)
        snk = sink_ref[0, h]
        m = jnp.maximum(jnp.max(sc, axis=-1, keepdims=True), snk)
        p = jnp.exp(sc - m)
        l = jnp.sum(p, axis=-1, keepdims=True) + jnp.exp(snk - m)
        o = jnp.dot(p.astype(BF16), vb, preferred_element_type=F32)
        o_ref[...] = (o / l).astype(BF16)
        lse_ref[...] = jnp.broadcast_to(m + jnp.log(l), lse_ref.shape)

    smem = pl.BlockSpec(memory_space=pltpu.SMEM)
    res, c_res = _call(
        body, name="attn_b_fwd", grid=(N_HEADS_B, nblk),
        in_specs=[
            smem,
            smem,
            pl.BlockSpec((BLOCK_Q, HEAD_DIM), lambda h, n: (n, COL_QB + h)),
            pl.BlockSpec((s + 2 * BLOCK_Q, HEAD_DIM), lambda h, n: (0, h // GROUP)),
            pl.BlockSpec((s + 2 * BLOCK_Q, HEAD_DIM), lambda h, n: (0, h // GROUP)),
            pl.BlockSpec((BLOCK_Q, 3 * BLOCK_Q), lambda h, n: (0, 0)),
        ],
        out_specs=[
            pl.BlockSpec((BLOCK_Q, HEAD_DIM), lambda h, n: (n, h)),
            pl.BlockSpec((None, BLOCK_Q, HEAD_DIM), lambda h, n: (h, n, 0)),
        ],
        out_shape=[
            jax.ShapeDtypeStruct((s, N_HEADS_B * HEAD_DIM), BF16),
            jax.ShapeDtypeStruct((N_HEADS_B, s, HEAD_DIM), F32),
        ],
        scratch_shapes=[pltpu.VMEM((BLOCK_Q, 3 * BLOCK_Q), F32)],
        sem=("arbitrary", "arbitrary"), args=(table, sink, pb, kpad, vpad, bucket), comm=comm)
    return res if comm is None else (res, c_res)


def _attn_b_bwd(pb, kpad, vpad, att, datt, lse, bucket, table, sink, *, comm=None):
    s = pb.shape[0]
    nblk = s // BLOCK_Q
    sp = s + 2 * BLOCK_Q

    def body(table_ref, sink_ref, q_ref, k_ref, v_ref, o_ref, do_ref, lse_ref, bucket_ref,
             dq_ref, dk_ref, dv_ref, dtab_ref, dsink_ref, bias_ref, dbias_ref):
        kv = pl.program_id(0)
        g = pl.program_id(1)
        n = pl.program_id(2)
        h = kv * GROUP + g

        @pl.when(jnp.logical_and(g == 0, n == 0))
        def _():
            dk_ref[...] = jnp.zeros_like(dk_ref)
            dv_ref[...] = jnp.zeros_like(dv_ref)

        @pl.when(n == 0)
        def _():
            bias_ref[...] = _band_bias(bucket_ref[...], table_ref, h)
            dbias_ref[...] = jnp.zeros_like(dbias_ref)
            dsink_ref[...] = jnp.zeros_like(dsink_ref)

        start = pl.multiple_of(n * BLOCK_Q, BLOCK_Q)
        band = pl.ds(start, 3 * BLOCK_Q)
        q = q_ref[...]
        do = do_ref[...]
        kb = k_ref[band, :]
        vb = v_ref[band, :]
        lse_col = lse_ref[...][:, :1]
        sc = lax.dot_general(q, kb, _NT, preferred_element_type=F32) * ATT_SCALE + bias_ref[...]
        sc = jnp.where(_band_mask(n, s), sc, NEG_INF)
        p = jnp.exp(sc - lse_col)
        dp = lax.dot_general(do, vb, _NT, preferred_element_type=F32)
        delta = jnp.sum(do.astype(F32) * o_ref[...].astype(F32), axis=-1, keepdims=True)
        ds = p * (dp - delta)
        dsb = (ds * ATT_SCALE).astype(BF16)
        dq_ref[...] = jnp.dot(dsb, kb, preferred_element_type=F32)
        dk_ref[band, :] += lax.dot_general(dsb, q, _TN, preferred_element_type=F32)
        dv_ref[band, :] += lax.dot_general(p.astype(BF16), do, _TN, preferred_element_type=F32)
        dbias_ref[...] += ds
        p_sink = jnp.exp(sink_ref[0, h] - lse_col)
        dsink_ref[...] += jnp.broadcast_to(jnp.sum(-p_sink * delta, axis=0, keepdims=True), dsink_ref.shape)

        @pl.when(n == nblk - 1)
        def _():
            bucket_v = bucket_ref[...]
            acc = dbias_ref[...]
            row = lax.broadcasted_iota(jnp.int32, dtab_ref.shape, 0)
            tot = jnp.zeros(dtab_ref.shape, F32)
            for b in range(N_BUCKETS):
                tot = jnp.where(row == b, jnp.sum(jnp.where(bucket_v == b, acc, 0.0), keepdims=True), tot)
            dtab_ref[...] = tot

    smem = pl.BlockSpec(memory_space=pltpu.SMEM)
    qmap = lambda kv, g, n: (n, kv * GROUP + g)
    qmap_b = lambda kv, g, n: (n, N_HEADS_A + kv * GROUP + g)
    hmap = lambda kv, g, n: (kv * GROUP + g, 0, 0)
    res, c_res = _call(
        body, name="attn_b_bwd", grid=(N_KV_B, GROUP, nblk),
        in_specs=[
            smem,
            smem,
            pl.BlockSpec((BLOCK_Q, HEAD_DIM), lambda kv, g, n: (n, COL_QB + kv * GROUP + g)),
            pl.BlockSpec((sp, HEAD_DIM), lambda kv, g, n: (0, kv)),
            pl.BlockSpec((sp, HEAD_DIM), lambda kv, g, n: (0, kv)),
            pl.BlockSpec((BLOCK_Q, HEAD_DIM), qmap_b),
            pl.BlockSpec((BLOCK_Q, HEAD_DIM), qmap_b),
            pl.BlockSpec((None, BLOCK_Q, HEAD_DIM), lambda kv, g, n: (kv * GROUP + g, n, 0)),
            pl.BlockSpec((BLOCK_Q, 3 * BLOCK_Q), lambda kv, g, n: (0, 0)),
        ],
        out_specs=[
            pl.BlockSpec((BLOCK_Q, HEAD_DIM), qmap),
            pl.BlockSpec((sp, HEAD_DIM), lambda kv, g, n: (0, kv)),
            pl.BlockSpec((sp, HEAD_DIM), lambda kv, g, n: (0, kv)),
            pl.BlockSpec((None, N_BUCKETS, HEAD_DIM), hmap),
            pl.BlockSpec((None, 1, HEAD_DIM), hmap),
        ],
        out_shape=[
            jax.ShapeDtypeStruct((s, N_HEADS_B * HEAD_DIM), F32),
            jax.ShapeDtypeStruct((sp, N_KV_B * HEAD_DIM), F32),
            jax.ShapeDtypeStruct((sp, N_KV_B * HEAD_DIM), F32),
            jax.ShapeDtypeStruct((N_HEADS_B, N_BUCKETS, HEAD_DIM), F32),
            jax.ShapeDtypeStruct((N_HEADS_B, 1, HEAD_DIM), F32),
        ],
        scratch_shapes=[pltpu.VMEM((BLOCK_Q, 3 * BLOCK_Q), F32), pltpu.VMEM((BLOCK_Q, 3 * BLOCK_Q), F32)],
        sem=("arbitrary", "arbitrary", "arbitrary"),
        args=(table, sink, pb, kpad, vpad, att, datt, lse, bucket), comm=comm)
    return res if comm is None else (res, c_res)


_MESH = pl.DeviceIdType.MESH


def _other_chips(x, y):
    return [(x, 1 - y), (1 - x, y), (1 - x, 1 - y)]


def _gather_comm(shards):
    n_t = len(shards)

    def copies(src, dst, sems, later):
        send_sems, recv_sems, local_sems = sems
        x, y, c = lax.axis_index("x"), lax.axis_index("y"), lax.axis_index("c")
        me = 2 * x + y
        sibling = (x, y, 1 - c)
        local, ici, landed, passed, from_sibling = [], [], [], [], []
        for t in range(n_t):
            half = shards[t].shape[0] // 2
            mine = pl.ds(pl.multiple_of(c * half, half), half)
            other = pl.ds(pl.multiple_of((1 - c) * half, half), half)

            def copy(k, src_ref, dst_ref, to, t=t):
                return pltpu.make_async_remote_copy(
                    src_ref=src_ref, dst_ref=dst_ref, send_sem=send_sems.at[6 * t + k],
                    recv_sem=recv_sems.at[6 * t + k], device_id=to, device_id_type=_MESH)

            local.append(pltpu.make_async_copy(src[t], dst[t].at[me], local_sems.at[t]))
            for j, (px, py) in enumerate(_other_chips(x, y)):
                k = 2 * px + py
                ici.append(copy(j, src[t].at[mine], dst[t].at[me, mine], (px, py, c)))
                if later:
                    landed.append(copy(j, src[t].at[mine], dst[t].at[k, mine], (px, py, c)))
                    passed.append(copy(3 + j, dst[t].at[k, mine], dst[t].at[k, mine], sibling))
                    from_sibling.append(copy(3 + j, dst[t].at[k, other], dst[t].at[k, other], sibling))
        return local, ici, landed, passed, from_sibling

    def start(src, dst, sems):
        local, ici, _, _, _ = copies(src, dst, sems, False)
        for cp in local + ici:
            cp.start()

    def finish(src, dst, sems):
        local, ici, landed, passed, from_sibling = copies(src, dst, sems, True)
        for got, fwd in zip(landed, passed):
            got.wait_recv()
            fwd.start()
        for cp in from_sibling:
            cp.wait_recv()
        for cp in ici + passed:
            cp.wait_send()
        for cp in local:
            cp.wait()

    return _Comm(
        shards, [jax.ShapeDtypeStruct((N_CHIPS,) + w.shape, w.dtype) for w in shards],
        [pltpu.SemaphoreType.DMA((6 * n_t,)), pltpu.SemaphoreType.DMA((6 * n_t,)), pltpu.SemaphoreType.DMA((n_t,))],
        start, finish)


def _exchange_comm(partials):
    n_t = len(partials)

    def copies(src, dst, sems, later):
        send_sems, recv_sems, local_sems = sems
        x, y, c = lax.axis_index("x"), lax.axis_index("y"), lax.axis_index("c")
        me = 2 * x + y
        sibling = (x, y, 1 - c)
        local, first, landed, passed, from_sibling = [], [], [], [], []
        for t in range(n_t):

            def copy(k, src_ref, slot, to, t=t):
                return pltpu.make_async_remote_copy(
                    src_ref=src_ref, dst_ref=dst[t].at[slot], send_sem=send_sems.at[7 * t + k],
                    recv_sem=recv_sems.at[7 * t + k], device_id=to, device_id_type=_MESH)

            local.append(pltpu.make_async_copy(src[t].at[me], dst[t].at[2 * me + c], local_sems.at[t]))
            for j, (px, py) in enumerate(_other_chips(x, y)):
                k = 2 * px + py
                first.append(copy(1 + j, src[t].at[k], 2 * me + c, (px, py, c)))
                if later:
                    landed.append(copy(1 + j, dst[t].at[2 * k + c], 2 * k + c, (px, py, c)))
                    passed.append(copy(4 + j, dst[t].at[2 * k + c], 2 * k + c, sibling))
                    from_sibling.append(copy(4 + j, dst[t].at[2 * k + 1 - c], 2 * k + 1 - c, sibling))
            first.append(copy(0, src[t].at[me], 2 * me + c, sibling))
            if later:
                from_sibling.append(copy(0, src[t].at[me], 2 * me + 1 - c, sibling))
        return local, first, landed, passed, from_sibling

    def start(src, dst, sems):
        local, first, _, _, _ = copies(src, dst, sems, False)
        for cp in local + first:
            cp.start()

    def finish(src, dst, sems):
        local, first, landed, passed, from_sibling = copies(src, dst, sems, True)
        for got, fwd in zip(landed, passed):
            got.wait_recv()
            fwd.start()
        for cp in from_sibling:
            cp.wait_recv()
        for cp in first + passed:
            cp.wait_send()
        for cp in local:
            cp.wait()

    return _Comm(
        partials, [jax.ShapeDtypeStruct((N_DEV,) + g.shape[1:], g.dtype) for g in partials],
        [pltpu.SemaphoreType.DMA((7 * n_t,)), pltpu.SemaphoreType.DMA((7 * n_t,)), pltpu.SemaphoreType.DMA((n_t,))],
        start, finish)


def _allreduce_small(pack):
    rows, d = pack.shape

    def body(p_ref, sum_ref, all_ref, send_sems, recv_sems):
        x, y, c = lax.axis_index("x"), lax.axis_index("y"), lax.axis_index("c")
        me = 4 * x + 2 * y + c
        all_ref[me] = p_ref[...]
        peers = []
        for dx in range(2):
            for dy in range(2):
                for dc in range(2):
                    if dx or dy or dc:
                        px = 1 - x if dx else x
                        py = 1 - y if dy else y
                        pc = 1 - c if dc else c
                        peers.append((4 * dx + 2 * dy + dc - 1, (px, py, pc)))
        sends = []
        for k, to in peers:
            cp = pltpu.make_async_remote_copy(
                src_ref=p_ref, dst_ref=all_ref.at[me], send_sem=send_sems.at[k], recv_sem=recv_sems.at[k],
                device_id=to, device_id_type=_MESH)
            cp.start()
            sends.append(cp)
        for k, (px, py, pc) in peers:
            pltpu.make_async_remote_copy(
                src_ref=p_ref, dst_ref=all_ref.at[4 * px + 2 * py + pc], send_sem=send_sems.at[k],
                recv_sem=recv_sems.at[k], device_id=(px, py, pc), device_id_type=_MESH).wait_recv()
        for cp in sends:
            cp.wait_send()
        tot = all_ref[0]
        for i in range(1, N_DEV):
            tot = tot + all_ref[i]
        sum_ref[...] = tot

    vm = pl.BlockSpec(memory_space=pltpu.VMEM)
    return pl.pallas_call(
        body,
        name="allreduce_small",
        in_specs=[vm],
        out_specs=vm,
        out_shape=jax.ShapeDtypeStruct((rows, d), F32),
        scratch_shapes=[
            pltpu.VMEM((N_DEV, rows, d), F32),
            pltpu.SemaphoreType.DMA((N_DEV - 1,)),
            pltpu.SemaphoreType.DMA((N_DEV - 1,)),
        ],
    )(pack)


def _adamw_math(w, g, m, v):
    m = ADAM_B1 * m + (1.0 - ADAM_B1) * g
    v = ADAM_B2 * v + (1.0 - ADAM_B2) * (g * g)
    m_hat = m / (1.0 - ADAM_B1 ** ADAM_STEP)
    v_hat = v / (1.0 - ADAM_B2 ** ADAM_STEP)
    delta = -ADAM_LR * (m_hat / (jnp.sqrt(v_hat) + ADAM_EPS) + ADAM_WD * w)
    return delta, m, v


def _sum_adamw(parts, w, m, v, *, name, tr=256):
    r, c = w.shape
    tr = min(tr, r)
    tc = min(c, 1024)

    def body(p_ref, w_ref, m_ref, v_ref, g_ref, d_ref, m2_ref, v2_ref):
        g = p_ref[0].astype(F32)
        for i in range(1, N_DEV):
            g = g + p_ref[i].astype(F32)
        delta, m2, v2 = _adamw_math(w_ref[...], g, m_ref[...], v_ref[...])
        g_ref[...] = g
        d_ref[...] = delta
        m2_ref[...] = m2
        v2_ref[...] = v2

    blk = pl.BlockSpec((tr, tc), lambda i, j: (i, j))
    return pl.pallas_call(
        body,
        name=name,
        grid=(r // tr, c // tc),
        in_specs=[pl.BlockSpec((N_DEV, tr, tc), lambda i, j: (0, i, j)), blk, blk, blk],
        out_specs=[blk] * 4,
        out_shape=[jax.ShapeDtypeStruct((r, c), F32)] * 4,
        compiler_params=_params(("parallel", "parallel")),
    )(parts, w, m, v)


def _adamw_small(g, w, m, v):
    def body(g_ref, w_ref, m_ref, v_ref, d_ref, m2_ref, v2_ref):
        delta, m2, v2 = _adamw_math(w_ref[...], g_ref[...], m_ref[...], v_ref[...])
        d_ref[...] = delta
        m2_ref[...] = m2
        v2_ref[...] = v2

    vm = pl.BlockSpec(memory_space=pltpu.VMEM)
    return pl.pallas_call(
        body,
        name="adamw_small",
        in_specs=[vm] * 4,
        out_specs=[vm] * 3,
        out_shape=[jax.ShapeDtypeStruct(g.shape, F32)] * 3,
    )(g, w, m, v)


def _relu2_epilogue(acc):
    ra = jnp.maximum(acc, 0.0)
    return ra * ra, ra


def _rows(stacked):
    return stacked.reshape(stacked.shape[0] * stacked.shape[1], stacked.shape[2])


def _by_chip(mat):
    return mat.reshape(N_CHIPS, mat.shape[0] // N_CHIPS, mat.shape[1])


def _local_step(x, p, target, shards, small):
    s, d = x.shape
    cos_t, sin_t = _rope_tables(s)
    bucket = _band_buckets()
    p_bf = p.astype(BF16)
    wts = {}

    u, (wts["w_in"],) = _rms_fwd(x, small["attn_norm_g"], name="norm_attn", comm=_gather_comm([shards["w_in"]]))
    proj, (w_out_s, wts["ple_w"], w_gate_s) = _matmul(
        u, wts["w_in"], mode="nn", out_dtypes=[F32], name="mm_in", bn=768,
        comm=_gather_comm([shards["w_out"], shards["ple_w"], shards["w_gate"]]))
    wts["w_out"], wts["w_gate"] = _rows(w_out_s), _rows(w_gate_s)
    pb = _qk_prep(proj, small["q_norm_g"], small["k_norm_g"], cos_t, sin_t)
    (oa, lse_a), (wts["w_up"],) = _attn_a_fwd(pb, comm=_gather_comm([shards["w_up"]]))
    pad = ((BLOCK_Q, BLOCK_Q), (0, 0))
    kpad = jnp.pad(pb[:, COL_KB * HEAD_DIM:COL_VB * HEAD_DIM], pad)
    vpad = jnp.pad(pb[:, COL_VB * HEAD_DIM:], pad)
    (ob, lse_b), (w_down_s,) = _attn_b_fwd(pb, kpad, vpad, bucket, small["rel_bias_table"], small["sink_logits"],
                                            comm=_gather_comm([shards["w_down"]]))
    wts["w_down"] = _rows(w_down_s)
    att = jnp.concatenate([oa, ob], axis=-1)
    h1 = _matmul(att, wts["w_out"], mode="nn", out_dtypes=[F32], name="mm_out",
                 epilogue=lambda acc, res: (acc + res,), extras=(x,))
    mn = _rms_fwd(h1, small["mlp_norm_g"], name="norm_mlp")
    r, ra = _matmul(mn, wts["w_up"], mode="nn", out_dtypes=[BF16, BF16], name="mm_up", epilogue=_relu2_epilogue)
    h2 = _matmul(r, wts["w_down"], mode="nn", out_dtypes=[F32], name="mm_down",
                 epilogue=lambda acc, res: (acc + res,), extras=(h1,))
    ng = _rms_fwd(h2, small["gate_norm_g"], name="norm_gate")
    gate = _matmul(ng, wts["w_gate"], mode="nn", out_dtypes=[F32], name="mm_gate",
                   epilogue=lambda acc: (1.0 / (1.0 + jnp.exp(-acc)),))
    pp = _matmul(p_bf, wts["ple_w"], mode="nn", out_dtypes=[F32], name="mm_ple", bn=512)
    dh3, dz, dpp, dg_final, dg_ple, loss = _tail(h2, gate, pp, target, small["ple_norm_g"], small["final_norm_g"])

    dng = _matmul(dz, wts["w_gate"], mode="nt", out_dtypes=[F32], name="mm_gate_dx")
    gw_gate = _matmul(ng, dz, mode="tn", out_dtypes=[BF16], name="mm_gate_dw")
    gw_ple = _matmul(p_bf, dpp, mode="tn", out_dtypes=[BF16], name="mm_ple_dw", bn=512, out_stack=N_CHIPS)
    dh2, dh2_bf, dg_gate = _rms_bwd(h2, dng, small["gate_norm_g"], dh3, name="norm_gate_bwd", want_bf16=True)
    big = {}
    da, (big["w_gate"], big["ple_w"]) = _matmul(
        dh2_bf, wts["w_down"], mode="nt", out_dtypes=[BF16], name="mm_down_dx",
        epilogue=lambda acc, ra_v: (acc * (2.0 * ra_v.astype(F32)),), extras=(ra,),
        comm=_exchange_comm([_by_chip(gw_gate), gw_ple]))
    gw_down = _matmul(r, dh2_bf, mode="tn", out_dtypes=[BF16], name="mm_down_dw")
    gw_up = _matmul(mn, da, mode="tn", out_dtypes=[BF16], name="mm_up_dw", out_stack=N_CHIPS)
    dmn = _matmul(da, wts["w_up"], mode="nt", out_dtypes=[F32], name="mm_up_dx")
    dh1, dh1_bf, dg_mlp = _rms_bwd(h1, dmn, small["mlp_norm_g"], dh2, name="norm_mlp_bwd", want_bf16=True)
    datt = _matmul(dh1_bf, wts["w_out"], mode="nt", out_dtypes=[BF16], name="mm_out_dx")
    gw_out = _matmul(att, dh1_bf, mode="tn", out_dtypes=[BF16], name="mm_out_dw")
    (dqb, dkpad, dvpad, dtab, dsink), (big["w_up"],) = _attn_b_bwd(
        pb, kpad, vpad, att, datt, lse_b, bucket, small["rel_bias_table"], small["sink_logits"],
        comm=_exchange_comm([gw_up]))
    (dqa, dka, dva), (big["w_down"],) = _attn_a_bwd(pb, att, datt, lse_a, comm=_exchange_comm([_by_chip(gw_down)]))
    dpre = jnp.concatenate([dqa, dka, dva, dqb, dkpad[BLOCK_Q:-BLOCK_Q], dvpad[BLOCK_Q:-BLOCK_Q]], axis=-1)
    (dproj, dg_q, dg_k), (big["w_out"],) = _qk_bwd(dpre, proj, small["q_norm_g"], small["k_norm_g"], cos_t, sin_t,
                                                   comm=_exchange_comm([_by_chip(gw_out)]))
    gw_in = _matmul(u, dproj, mode="tn", out_dtypes=[BF16], name="mm_in_dw", bn=768, out_stack=N_CHIPS)
    du, (big["w_in"],) = _matmul(dproj, wts["w_in"], mode="nt", out_dtypes=[F32], name="mm_in_dx", bk=768,
                                 comm=_exchange_comm([gw_in]))
    grad_x, dg_attn = _rms_bwd(x, du, small["attn_norm_g"], dh1, name="norm_attn_bwd", want_bf16=False)

    small_g = {
        "attn_norm_g": dg_attn, "mlp_norm_g": dg_mlp, "ple_norm_g": dg_ple, "gate_norm_g": dg_gate,
        "final_norm_g": dg_final, "q_norm_g": dg_q, "k_norm_g": dg_k,
        "sink_logits": dsink[:, 0, 0][None, :], "rel_bias_table": dtab[:, :, 0].T,
    }
    return loss, grad_x, big, small_g


_SMALL_ROWS = ["attn_norm_g", "mlp_norm_g", "ple_norm_g", "gate_norm_g", "final_norm_g"]
_PACK_ROWS = 8


def _pack_small(vals, d):
    rows = [vals[n].reshape(1, d) for n in _SMALL_ROWS]
    misc = jnp.concatenate([
        vals["q_norm_g"].reshape(1, HEAD_DIM), vals["k_norm_g"].reshape(1, HEAD_DIM),
        jnp.pad(vals["sink_logits"].reshape(1, N_HEADS_B), ((0, 0), (0, HEAD_DIM - N_HEADS_B))),
        vals["rel_bias_table"].reshape(1, N_BUCKETS * N_HEADS_B)], axis=1)
    rows.append(jnp.pad(misc, ((0, 0), (0, d - misc.shape[1]))))
    rows.append(jnp.zeros((_PACK_ROWS - len(rows), d), F32))
    return jnp.concatenate(rows, axis=0).astype(F32)


def _unpack_small(pack, shapes):
    out = {n: pack[i].reshape(shapes[n]) for i, n in enumerate(_SMALL_ROWS)}
    misc = pack[len(_SMALL_ROWS)]
    out["q_norm_g"] = misc[:HEAD_DIM].reshape(shapes["q_norm_g"])
    out["k_norm_g"] = misc[HEAD_DIM:2 * HEAD_DIM].reshape(shapes["k_norm_g"])
    out["sink_logits"] = misc[2 * HEAD_DIM:2 * HEAD_DIM + N_HEADS_B].reshape(shapes["sink_logits"])
    out["rel_bias_table"] = misc[3 * HEAD_DIM:3 * HEAD_DIM + N_BUCKETS * N_HEADS_B].reshape(shapes["rel_bias_table"])
    return out


_WEIGHTS = ["attn_norm_g", "w_in", "q_norm_g", "k_norm_g", "sink_logits", "w_out", "mlp_norm_g", "w_up", "w_down",
            "ple_w", "ple_norm_g", "gate_norm_g", "w_gate", "rel_bias_table", "final_norm_g"]
_BIG = ["w_in", "w_out", "w_up", "w_down", "ple_w", "w_gate"]


def kernel(x, p, attn_norm_g, w_in, q_norm_g, k_norm_g, sink_logits, w_out, mlp_norm_g, w_up, w_down, ple_w, ple_norm_g, gate_norm_g, w_gate, rel_bias_table, final_norm_g, loss_target, m_attn_norm_g, m_w_in, m_q_norm_g, m_k_norm_g, m_sink_logits, m_w_out, m_mlp_norm_g, m_w_up, m_w_down, m_ple_w, m_ple_norm_g, m_gate_norm_g, m_w_gate, m_rel_bias_table, m_final_norm_g, v_attn_norm_g, v_w_in, v_q_norm_g, v_k_norm_g, v_sink_logits, v_w_out, v_mlp_norm_g, v_w_up, v_w_down, v_ple_w, v_ple_norm_g, v_gate_norm_g, v_w_gate, v_rel_bias_table, v_final_norm_g):
    given = dict(locals())
    w = {n: given[n] for n in _WEIGHTS}
    m = {n: given["m_" + n] for n in _WEIGHTS}
    v = {n: given["v_" + n] for n in _WEIGHTS}
    d = x.shape[-1]

    shards = {n: w[n][0].astype(BF16) for n in _BIG}
    small = {
        "attn_norm_g": w["attn_norm_g"], "mlp_norm_g": w["mlp_norm_g"], "ple_norm_g": w["ple_norm_g"],
        "gate_norm_g": w["gate_norm_g"], "final_norm_g": w["final_norm_g"].reshape(1, d),
        "q_norm_g": w["q_norm_g"], "k_norm_g": w["k_norm_g"], "sink_logits": w["sink_logits"],
        "rel_bias_table": w["rel_bias_table"],
    }

    loss_part, grad_x, exchanged, small_g = _local_step(x[0], p[0, 0], loss_target[0], shards, small)
    loss = lax.psum(loss_part[0, 0], ("x", "y", "c"))

    grads, deltas, new_m, new_v = {}, {}, {}, {}
    for n in _BIG:
        shape = w[n].shape
        res = _sum_adamw(exchanged[n], w[n][0], m[n][0], v[n][0], name="adamw_" + n)
        grads[n], deltas[n], new_m[n], new_v[n] = [t.reshape(shape) for t in res]

    shapes = {n: w[n].shape for n in _WEIGHTS if n not in _BIG}
    g_small = _allreduce_small(_pack_small(small_g, d))
    d_small, m_small, v_small = _adamw_small(g_small, _pack_small(w, d), _pack_small(m, d), _pack_small(v, d))
    grads.update(_unpack_small(g_small, shapes))
    deltas.update(_unpack_small(d_small, shapes))
    new_m.update(_unpack_small(m_small, shapes))
    new_v.update(_unpack_small(v_small, shapes))

    return (loss, grad_x[None], *[grads[n] for n in _WEIGHTS], *[deltas[n] for n in _WEIGHTS],
            *[new_m[n] for n in _WEIGHTS], *[new_v[n] for n in _WEIGHTS])
```

```python
import functools
import math

import jax
import jax.numpy as jnp
import numpy as np
from jax import lax
from jax.experimental import pallas as pl
from jax.experimental.pallas import tpu as pltpu

F32 = jnp.float32
BF16 = jnp.bfloat16

HEAD_DIM = 128
N_HEADS_A = 8
N_KV_A = 2
N_HEADS_B = 8
N_KV_B = 2
GROUP = 4
GRID_W = 64
BLOCK_Q = 128
WINDOW = 128
N_BUCKETS = 32
MAX_DISTANCE = 128
ROPE_THETA = 10000.0
EPS = 1e-6
NEG_INF = -1e30
ATT_SCALE = HEAD_DIM ** -0.5
LOG2E = math.log2(math.e)
LN2 = math.log(2.0)
Q_SCALE = ATT_SCALE * LOG2E
PAD_LO, PAD_HI = 256, 128

ADAM_LR = 0.001
ADAM_B1 = 0.9
ADAM_B2 = 0.999
ADAM_EPS = 1e-08
ADAM_WD = 0.01
ADAM_STEP = 10

N_CHIPS = 4
N_DEV = 8
COL_QA, COL_KA, COL_VA, COL_QB, COL_KB, COL_VB = 0, 8, 10, 12, 20, 22
N_COLS = 24

VMEM_LIMIT = 52 * 1024 * 1024


def _params(sem=None):
    return pltpu.CompilerParams(dimension_semantics=sem, vmem_limit_bytes=VMEM_LIMIT)


_ANY = pl.BlockSpec(memory_space=pl.ANY)


class _Comm:
    def __init__(self, inputs, out_shapes, sems, start, finish):
        self.inputs, self.out_shapes, self.sems = list(inputs), list(out_shapes), list(sems)
        self.start, self.finish = start, finish


def _call(body, *, name, grid, in_specs, out_specs, out_shape, args, scratch_shapes=(), sem=None, comm=None):
    in_specs, out_specs, out_shape = list(in_specs), list(out_specs), list(out_shape)
    scratch_shapes = list(scratch_shapes)
    if comm is None:
        res = pl.pallas_call(
            body, name=name, grid=grid, in_specs=in_specs, out_specs=out_specs, out_shape=out_shape,
            scratch_shapes=scratch_shapes, compiler_params=_params(sem))(*args)
        return list(res), []
    n_in, n_out, n_sc = len(in_specs), len(out_specs), len(scratch_shapes)
    c_in, c_out = len(comm.inputs), len(comm.out_shapes)

    def hosted(*refs):
        pos = [0]

        def take(n):
            pos[0] += n
            return refs[pos[0] - n:pos[0]]

        ins, c_ins, outs, c_outs, scr = take(n_in), take(c_in), take(n_out), take(c_out), take(n_sc)
        c_sems = refs[pos[0]:]
        ids = [pl.program_id(a) for a in range(len(grid))]
        first = functools.reduce(jnp.logical_and, [i == 0 for i in ids])
        last = functools.reduce(jnp.logical_and, [i == g - 1 for i, g in zip(ids, grid)])

        @pl.when(first)
        def _():
            comm.start(c_ins, c_outs, c_sems)

        body(*ins, *outs, *scr)

        @pl.when(last)
        def _():
            comm.finish(c_ins, c_outs, c_sems)

    res = pl.pallas_call(
        hosted, name=name, grid=grid, in_specs=in_specs + [_ANY] * c_in, out_specs=out_specs + [_ANY] * c_out,
        out_shape=out_shape + comm.out_shapes, scratch_shapes=scratch_shapes + comm.sems,
        compiler_params=_params(("arbitrary",) * len(grid)))(*args, *comm.inputs)
    return list(res[:n_out]), list(res[n_out:])


def _matmul(a, b, *, mode, out_dtypes, name, epilogue=None, extras=(), bm=1024, bn=1024, bk=2048,
            out_stack=0, comm=None):
    stacked = b.ndim == 3
    if mode == "nn":
        m, k = a.shape
        if stacked:
            nj, kb, ns = b.shape
            n, ks = nj * ns, k
        else:
            kb, n = b.shape
            ns, ks = n, k
        dn = (((1,), (0,)), ((), ()))
    elif mode == "nt":
        m, k = a.shape
        if stacked:
            nj, n, ks = b.shape
            kb = nj * ks
        else:
            n, kb = b.shape
            ks = kb
        ns = n
        dn = (((1,), (1,)), ((), ()))
    else:
        k, m = a.shape
        kb, n = b.shape
        ns, ks = n, k
        dn = (((0,), (0,)), ((), ()))
    assert k == kb and not (stacked and mode == "tn")
    ns_out = n // out_stack if out_stack else n
    bm, bn, bk = min(bm, m), min(bn, ns, ns_out), min(bk, ks)
    assert m % bm == 0 and ns % bn == 0 and ns_out % bn == 0 and ks % bk == 0
    gm, gn, gk = m // bm, n // bn, k // bk

    if mode == "tn":
        a_spec = pl.BlockSpec((bk, bm), lambda i, j, q: (q, i))
    else:
        a_spec = pl.BlockSpec((bm, bk), lambda i, j, q: (i, q))
    if mode == "nt":
        if stacked:
            per = ks // bk
            b_spec = pl.BlockSpec((None, bn, bk), lambda i, j, q: (q // per, j, q % per))
        else:
            b_spec = pl.BlockSpec((bn, bk), lambda i, j, q: (j, q))
    else:
        if stacked:
            per = ns // bn
            b_spec = pl.BlockSpec((None, bk, bn), lambda i, j, q: (j // per, q, j % per))
        else:
            b_spec = pl.BlockSpec((bk, bn), lambda i, j, q: (q, j))
    ex_spec = pl.BlockSpec((bm, bn), lambda i, j, q: (i, j))
    if out_stack:
        per_o = ns_out // bn
        o_spec = pl.BlockSpec((None, bm, bn), lambda i, j, q: (j // per_o, i, j % per_o))
        o_shape = (out_stack, m, ns_out)
    else:
        o_spec = ex_spec
        o_shape = (m, n)
    n_ex, n_out = len(extras), len(out_dtypes)

    def body(a_ref, b_ref, *rest):
        ex, outs = rest[:n_ex], rest[n_ex:n_ex + n_out]
        part = lax.dot_general(a_ref[...], b_ref[...], dn, preferred_element_type=F32)

        def finish(acc):
            res = epilogue(acc, *[e[...] for e in ex]) if epilogue else (acc,)
            for o, r in zip(outs, res):
                o[...] = r.astype(o.dtype)

        if gk == 1:
            finish(part)
        else:
            acc_ref = rest[-1]
            q = pl.program_id(2)

            @pl.when(q == 0)
            def _():
                acc_ref[...] = part

            @pl.when(q > 0)
            def _():
                acc_ref[...] += part

            @pl.when(q == gk - 1)
            def _():
                finish(acc_ref[...])

    res, c_res = _call(
        body, name=name, grid=(gm, gn, gk),
        in_specs=[a_spec, b_spec] + [ex_spec] * n_ex,
        out_specs=[o_spec] * n_out,
        out_shape=[jax.ShapeDtypeStruct(o_shape, dt) for dt in out_dtypes],
        scratch_shapes=[pltpu.VMEM((bm, bn), F32)] if gk > 1 else [],
        sem=("parallel", "parallel", "arbitrary"), args=(a, b, *extras), comm=comm)
    res = res[0] if n_out == 1 else res
    return res if comm is None else (res, c_res)


def _rms_fwd(x, g, *, name, tm=256, comm=None):
    s, d = x.shape
    tm = min(tm, s)

    def body(x_ref, g_ref, o_ref):
        xf = x_ref[...]
        r = lax.rsqrt(jnp.mean(xf * xf, axis=-1, keepdims=True) + EPS)
        o_ref[...] = (xf * r * g_ref[...]).astype(o_ref.dtype)

    res, c_res = _call(
        body, name=name, grid=(s // tm,),
        in_specs=[pl.BlockSpec((tm, d), lambda i: (i, 0)), pl.BlockSpec((1, d), lambda i: (0, 0))],
        out_specs=[pl.BlockSpec((tm, d), lambda i: (i, 0))],
        out_shape=[jax.ShapeDtypeStruct((s, d), BF16)],
        sem=("parallel",), args=(x, g), comm=comm)
    return res[0] if comm is None else (res[0], c_res)


def _rms_bwd(x, dy, g, add, *, name, want_bf16, tm=256):
    s, d = x.shape
    tm = min(tm, s)

    def body(x_ref, dy_ref, g_ref, add_ref, dx_ref, *rest):
        dg_ref = rest[-1]
        i = pl.program_id(0)
        xf = x_ref[...]
        dyf = dy_ref[...].astype(F32)
        r = lax.rsqrt(jnp.mean(xf * xf, axis=-1, keepdims=True) + EPS)
        xh = xf * r
        dyg = dyf * g_ref[...]
        dx = r * (dyg - xh * jnp.mean(dyg * xh, axis=-1, keepdims=True))
        tot = add_ref[...] + dx
        dx_ref[...] = tot
        if want_bf16:
            rest[0][...] = tot.astype(BF16)
        part = jnp.sum(dyf * xh, axis=0, keepdims=True)

        @pl.when(i == 0)
        def _():
            dg_ref[...] = part

        @pl.when(i > 0)
        def _():
            dg_ref[...] += part

    row = pl.BlockSpec((tm, d), lambda i: (i, 0))
    vec = pl.BlockSpec((1, d), lambda i: (0, 0))
    out_specs = [row] + ([row] if want_bf16 else []) + [vec]
    out_shape = [jax.ShapeDtypeStruct((s, d), F32)]
    if want_bf16:
        out_shape.append(jax.ShapeDtypeStruct((s, d), BF16))
    out_shape.append(jax.ShapeDtypeStruct((1, d), F32))
    return pl.pallas_call(
        body,
        name=name,
        grid=(s // tm,),
        in_specs=[row, row, vec, row],
        out_specs=out_specs,
        out_shape=out_shape,
        compiler_params=_params(("arbitrary",)),
    )(x, dy, g, add)


def _tail(h2, gate, pp, target, g_ple, g_final, *, tm=128):
    s, d = h2.shape
    tm = min(tm, s)

    def body(h2_ref, gate_ref, pp_ref, t_ref, gp_ref, gf_ref, dh3_ref, dz_ref, dpp_ref, dgf_ref, dgp_ref, loss_ref):
        i = pl.program_id(0)
        ppf = pp_ref[...]
        gate_v = gate_ref[...]
        r_p = lax.rsqrt(jnp.mean(ppf * ppf, axis=-1, keepdims=True) + EPS)
        eh = ppf * r_p
        e = eh * gp_ref[...]
        h3 = h2_ref[...] + gate_v * e
        r_f = lax.rsqrt(jnp.mean(h3 * h3, axis=-1, keepdims=True) + EPS)
        yh = h3 * r_f
        diff = yh * gf_ref[...] - t_ref[...]
        loss_part = 0.5 * jnp.sum(jnp.mean(diff * diff, axis=-1, keepdims=True), axis=0, keepdims=True)
        dy = diff / d
        dgf = jnp.sum(dy * yh, axis=0, keepdims=True)
        dyg = dy * gf_ref[...]
        dh3 = r_f * (dyg - yh * jnp.mean(dyg * yh, axis=-1, keepdims=True))
        dh3_ref[...] = dh3
        de = dh3 * gate_v
        dz_ref[...] = (dh3 * e * gate_v * (1.0 - gate_v)).astype(BF16)
        dgp = jnp.sum(de * eh, axis=0, keepdims=True)
        deg = de * gp_ref[...]
        dpp_ref[...] = (r_p * (deg - eh * jnp.mean(deg * eh, axis=-1, keepdims=True))).astype(BF16)
        loss_row = jnp.broadcast_to(loss_part, (1, 128))

        @pl.when(i == 0)
        def _():
            dgf_ref[...] = dgf
            dgp_ref[...] = dgp
            loss_ref[...] = loss_row

        @pl.when(i > 0)
        def _():
            dgf_ref[...] += dgf
            dgp_ref[...] += dgp
            loss_ref[...] += loss_row

    row = pl.BlockSpec((tm, d), lambda i: (i, 0))
    vec = pl.BlockSpec((1, d), lambda i: (0, 0))
    return pl.pallas_call(
        body,
        name="tail_fwd_bwd",
        grid=(s // tm,),
        in_specs=[row, row, row, row, vec, vec],
        out_specs=[row, row, row, vec, vec, pl.BlockSpec((1, 128), lambda i: (0, 0))],
        out_shape=[
            jax.ShapeDtypeStruct((s, d), F32),
            jax.ShapeDtypeStruct((s, d), BF16),
            jax.ShapeDtypeStruct((s, d), BF16),
            jax.ShapeDtypeStruct((1, d), F32),
            jax.ShapeDtypeStruct((1, d), F32),
            jax.ShapeDtypeStruct((1, 128), F32),
        ],
        compiler_params=_params(("arbitrary",)),
    )(h2, gate, pp, target, g_ple, g_final)


def _rope_tables(s):
    t = jnp.arange(s, dtype=jnp.int32)
    row = (t // GRID_W).astype(F32)
    col = (t % GRID_W).astype(F32)
    half = HEAD_DIM // 2
    inv_freq = ROPE_THETA ** (-jnp.arange(0, half, 2, dtype=F32) / half)
    ang_r = row[:, None] * inv_freq
    ang_c = col[:, None] * inv_freq
    cr, sr, cc, sc = jnp.cos(ang_r), jnp.sin(ang_r), jnp.cos(ang_c), jnp.sin(ang_c)
    cos_t = jnp.concatenate([cr, cr, cc, cc], axis=-1)
    sin_t = jnp.concatenate([-sr, sr, -sc, sc], axis=-1)
    return cos_t, sin_t


def _swap_quarters(x):
    lane = lax.broadcasted_iota(jnp.int32, x.shape, x.ndim - 1)
    up = pltpu.roll(x, HEAD_DIM - 32, x.ndim - 1)
    down = pltpu.roll(x, 32, x.ndim - 1)
    return jnp.where((lane % 64) < 32, up, down)


def _cols(first, count=1):
    return slice(first * HEAD_DIM, (first + count) * HEAD_DIM)


def _qk_prep(proj, g_q, g_k, cos_t, sin_t, *, tm=256):
    s, n = proj.shape
    tm = min(tm, s)

    def body(x_ref, gq_ref, gk_ref, c_ref, s_ref, o_ref):
        cos_v, sin_v = c_ref[...], s_ref[...]
        for h in range(COL_VA):
            x = x_ref[:, _cols(h)]
            g = gq_ref[...] if h < COL_KA else gk_ref[...]
            xn = x * lax.rsqrt(jnp.mean(x * x, axis=-1, keepdims=True) + EPS) * g
            xr = xn * cos_v + _swap_quarters(xn) * sin_v
            if h < COL_KA:
                xr = xr * Q_SCALE
            o_ref[:, _cols(h)] = xr.astype(BF16)
        o_ref[:, _cols(COL_VA, 2)] = x_ref[:, _cols(COL_VA, 2)].astype(BF16)
        o_ref[:, _cols(COL_QB, N_HEADS_B)] = (x_ref[:, _cols(COL_QB, N_HEADS_B)] * Q_SCALE).astype(BF16)
        o_ref[:, _cols(COL_KB, 4)] = x_ref[:, _cols(COL_KB, 4)].astype(BF16)

    row = pl.BlockSpec((tm, n), lambda i: (i, 0))
    tab = pl.BlockSpec((tm, HEAD_DIM), lambda i: (i, 0))
    vec = pl.BlockSpec((1, HEAD_DIM), lambda i: (0, 0))
    return pl.pallas_call(
        body,
        name="qk_prep",
        grid=(s // tm,),
        in_specs=[row, vec, vec, tab, tab],
        out_specs=row,
        out_shape=jax.ShapeDtypeStruct((s, n), BF16),
        compiler_params=_params(("parallel",)),
    )(proj, g_q, g_k, cos_t, sin_t)


def _qk_bwd(dqa, dka, dva, dqb, dkpad, dvpad, proj, g_q, g_k, cos_t, sin_t, *, comm=None):
    s, n = proj.shape
    tm = min(PAD_LO, s)
    assert PAD_LO % tm == 0
    lo = PAD_LO // tm

    def body(dqa_ref, dka_ref, dva_ref, dqb_ref, dkb_ref, dvb_ref, x_ref, gq_ref, gk_ref, c_ref, s_ref,
             o_ref, dgq_ref, dgk_ref):
        i = pl.program_id(0)
        cos_v, sin_v = c_ref[...], s_ref[...]

        def head(d, x, g):
            dn = d * cos_v + _swap_quarters(d * sin_v)
            r = lax.rsqrt(jnp.mean(x * x, axis=-1, keepdims=True) + EPS)
            xh = x * r
            dng = dn * g
            dx = r * (dng - xh * jnp.mean(dng * xh, axis=-1, keepdims=True))
            return dx.astype(BF16), jnp.sum(dn * xh, axis=0, keepdims=True)

        acc_q = jnp.zeros((1, HEAD_DIM), F32)
        acc_k = jnp.zeros((1, HEAD_DIM), F32)
        for h in range(N_HEADS_A):
            o_ref[:, _cols(h)], part = head(dqa_ref[:, _cols(h)] * ATT_SCALE, x_ref[:, _cols(h)], gq_ref[...])
            acc_q = acc_q + part
        for h in range(N_KV_A):
            o_ref[:, _cols(COL_KA + h)], part = head(dka_ref[:, _cols(h)] * LN2, x_ref[:, _cols(COL_KA + h)],
                                                     gk_ref[...])
            acc_k = acc_k + part
        o_ref[:, _cols(COL_VA, 2)] = dva_ref[...].astype(BF16)
        o_ref[:, _cols(COL_QB, N_HEADS_B)] = (dqb_ref[...] * ATT_SCALE).astype(BF16)
        o_ref[:, _cols(COL_KB, 2)] = (dkb_ref[...] * LN2).astype(BF16)
        o_ref[:, _cols(COL_VB, 2)] = dvb_ref[...].astype(BF16)

        @pl.when(i == 0)
        def _():
            dgq_ref[...] = acc_q
            dgk_ref[...] = acc_k

        @pl.when(i > 0)
        def _():
            dgq_ref[...] += acc_q
            dgk_ref[...] += acc_k

    def rows(width, shift=0):
        return pl.BlockSpec((tm, width), lambda i: (i + shift, 0))

    kv_w = N_KV_A * HEAD_DIM
    q_w = N_HEADS_A * HEAD_DIM
    vec = pl.BlockSpec((1, HEAD_DIM), lambda i: (0, 0))
    res, c_res = _call(
        body, name="qk_bwd", grid=(s // tm,),
        in_specs=[rows(q_w), rows(kv_w), rows(kv_w), rows(q_w), rows(kv_w, lo), rows(kv_w, lo), rows(n),
                  vec, vec, rows(HEAD_DIM), rows(HEAD_DIM)],
        out_specs=[rows(n), vec, vec],
        out_shape=[
            jax.ShapeDtypeStruct((s, n), BF16),
            jax.ShapeDtypeStruct((1, HEAD_DIM), F32),
            jax.ShapeDtypeStruct((1, HEAD_DIM), F32),
        ],
        sem=("arbitrary",), args=(dqa, dka, dva, dqb, dkpad, dvpad, proj, g_q, g_k, cos_t, sin_t), comm=comm)
    return res if comm is None else (res, c_res)


_NT = (((1,), (1,)), ((), ()))
_TN = (((0,), (0,)), ((), ()))


def _attn_a_fwd(pb, *, tq=512, sub=256, comm=None):
    s = pb.shape[0]
    tq = min(tq, s)

    sub = min(sub, tq)

    def body(q_ref, k_ref, v_ref, o_ref, lse_ref):
        k = k_ref[...]
        v = v_ref[...]
        for r in range(tq // sub):
            rows = pl.ds(r * sub, sub)
            sc = lax.dot_general(q_ref[rows, :], k, _NT, preferred_element_type=F32)
            m = jnp.max(sc, axis=-1, keepdims=True)
            p = jnp.exp2(sc - m)
            l = jnp.sum(p, axis=-1, keepdims=True)
            o = jnp.dot(p.astype(BF16), v, preferred_element_type=F32)
            o_ref[rows, :] = (o / l).astype(BF16)
            lse_ref[rows, :] = jnp.broadcast_to(m + jnp.log2(l), (sub, HEAD_DIM))

    res, c_res = _call(
        body, name="attn_a_fwd", grid=(N_HEADS_A, s // tq),
        in_specs=[
            pl.BlockSpec((tq, HEAD_DIM), lambda h, i: (i, COL_QA + h)),
            pl.BlockSpec((s, HEAD_DIM), lambda h, i: (0, COL_KA + h // GROUP)),
            pl.BlockSpec((s, HEAD_DIM), lambda h, i: (0, COL_VA + h // GROUP)),
        ],
        out_specs=[
            pl.BlockSpec((tq, HEAD_DIM), lambda h, i: (i, h)),
            pl.BlockSpec((None, tq, HEAD_DIM), lambda h, i: (h, i, 0)),
        ],
        out_shape=[
            jax.ShapeDtypeStruct((s, N_HEADS_A * HEAD_DIM), BF16),
            jax.ShapeDtypeStruct((N_HEADS_A, s, HEAD_DIM), F32),
        ],
        sem=("parallel", "parallel"), args=(pb, pb, pb), comm=comm)
    return res if comm is None else (res, c_res)


def _attn_a_bwd(pb, att, datt, lse, *, tq=256, sub=128, comm=None):
    s = pb.shape[0]
    tq = min(tq, s)
    sub = min(sub, tq)

    def body(q_ref, k_ref, v_ref, o_ref, do_ref, lse_ref, dq_ref, dk_ref, dv_ref):
        first = jnp.logical_and(pl.program_id(1) == 0, pl.program_id(2) == 0)
        k = k_ref[...]
        v = v_ref[...]
        dk = dv = None
        for r in range(tq // sub):
            rows = pl.ds(r * sub, sub)
            q = q_ref[rows, :]
            do = do_ref[rows, :]
            sc = lax.dot_general(q, k, _NT, preferred_element_type=F32)
            p = jnp.exp2(sc - lse_ref[rows, :][:, :1])
            dp = lax.dot_general(do, v, _NT, preferred_element_type=F32)
            delta = jnp.sum(do.astype(F32) * o_ref[rows, :].astype(F32), axis=-1, keepdims=True)
            ds = (p * (dp - delta)).astype(BF16)
            dq_ref[rows, :] = jnp.dot(ds, k, preferred_element_type=F32)
            dk_r = lax.dot_general(ds, q, _TN, preferred_element_type=F32)
            dv_r = lax.dot_general(p.astype(BF16), do, _TN, preferred_element_type=F32)
            dk = dk_r if dk is None else dk + dk_r
            dv = dv_r if dv is None else dv + dv_r

        @pl.when(first)
        def _():
            dk_ref[...] = dk
            dv_ref[...] = dv

        @pl.when(jnp.logical_not(first))
        def _():
            dk_ref[...] += dk
            dv_ref[...] += dv

    qmap = lambda kv, g, i: (i, kv * GROUP + g)
    res, c_res = _call(
        body, name="attn_a_bwd", grid=(N_KV_A, GROUP, s // tq),
        in_specs=[
            pl.BlockSpec((tq, HEAD_DIM), lambda kv, g, i: (i, COL_QA + kv * GROUP + g)),
            pl.BlockSpec((s, HEAD_DIM), lambda kv, g, i: (0, COL_KA + kv)),
            pl.BlockSpec((s, HEAD_DIM), lambda kv, g, i: (0, COL_VA + kv)),
            pl.BlockSpec((tq, HEAD_DIM), qmap),
            pl.BlockSpec((tq, HEAD_DIM), qmap),
            pl.BlockSpec((None, tq, HEAD_DIM), lambda kv, g, i: (kv * GROUP + g, i, 0)),
        ],
        out_specs=[
            pl.BlockSpec((tq, HEAD_DIM), qmap),
            pl.BlockSpec((s, HEAD_DIM), lambda kv, g, i: (0, kv)),
            pl.BlockSpec((s, HEAD_DIM), lambda kv, g, i: (0, kv)),
        ],
        out_shape=[
            jax.ShapeDtypeStruct((s, N_HEADS_A * HEAD_DIM), F32),
            jax.ShapeDtypeStruct((s, N_KV_A * HEAD_DIM), F32),
            jax.ShapeDtypeStruct((s, N_KV_A * HEAD_DIM), F32),
        ],
        sem=("arbitrary", "arbitrary", "arbitrary"), args=(pb, pb, pb, att, datt, lse), comm=comm)
    return res if comm is None else (res, c_res)


def _t5_bucket(rel):
    nb = N_BUCKETS // 2
    ret = jnp.where(rel > 0, nb, 0)
    n = jnp.abs(rel)
    max_exact = nb // 2
    nf = jnp.maximum(n, 1).astype(F32)
    large = max_exact + (jnp.log(nf / max_exact) / math.log(MAX_DISTANCE / max_exact)
                         * (nb - max_exact)).astype(jnp.int32)
    large = jnp.minimum(large, nb - 1)
    return ret + jnp.where(n < max_exact, n, large)


def _band_buckets():
    r = jnp.arange(BLOCK_Q, dtype=jnp.int32)
    j = jnp.arange(3 * BLOCK_Q, dtype=jnp.int32)
    return _t5_bucket((j[None, :] - BLOCK_Q) - r[:, None])


def _band_bias(bucket, table_ref, h):
    acc = jnp.zeros(bucket.shape, F32)
    for b in range(N_BUCKETS):
        acc = jnp.where(bucket == b, table_ref[b, h], acc)
    return acc


GQ = GROUP * BLOCK_Q


def _stack_heads(x):
    return jnp.concatenate([x[:, _cols(g)] for g in range(GROUP)], axis=0)


def _unstack_heads(x):
    return jnp.concatenate([x[g * BLOCK_Q:(g + 1) * BLOCK_Q] for g in range(GROUP)], axis=1)


def _group_bias(bucket, table_ref, kv):
    return jnp.concatenate([_band_bias(bucket, table_ref, kv * GROUP + g) * LOG2E for g in range(GROUP)], axis=0)


def _group_sink(sink_ref, kv):
    head = lax.broadcasted_iota(jnp.int32, (GQ, 1), 0) // BLOCK_Q
    snk = jnp.zeros((GQ, 1), F32)
    for g in range(GROUP):
        snk = jnp.where(head == g, sink_ref[0, kv * GROUP + g] * LOG2E, snk)
    return snk


def _band_mask(n, s):
    r = lax.broadcasted_iota(jnp.int32, (GQ, 3 * BLOCK_Q), 0) % BLOCK_Q
    j = lax.broadcasted_iota(jnp.int32, (GQ, 3 * BLOCK_Q), 1)
    rel = j - BLOCK_Q - r
    kabs = n * BLOCK_Q + j - BLOCK_Q
    return (jnp.abs(rel) <= WINDOW) & (kabs >= 0) & (kabs < s)


def _band_start(n):
    return pl.multiple_of(n * BLOCK_Q + (PAD_LO - BLOCK_Q), BLOCK_Q)


def _attn_b_fwd(pb, kpad, vpad, bucket, table, sink, *, comm=None):
    s = pb.shape[0]
    nblk = s // BLOCK_Q
    sp = kpad.shape[0]

    def body(table_ref, sink_ref, q_ref, k_ref, v_ref, bucket_ref, o_ref, lse_ref, bias_ref):
        kv = pl.program_id(0)
        n = pl.program_id(1)

        @pl.when(n == 0)
        def _():
            bias_ref[...] = _group_bias(bucket_ref[...], table_ref, kv)

        band = pl.ds(_band_start(n), 3 * BLOCK_Q)
        kb = k_ref[band, :]
        vb = v_ref[band, :]
        sc = lax.dot_general(_stack_heads(q_ref[...]), kb, _NT, preferred_element_type=F32) + bias_ref[...]
        sc = jnp.where(_band_mask(n, s), sc, NEG_INF)
        snk = _group_sink(sink_ref, kv)
        m = jnp.maximum(jnp.max(sc, axis=-1, keepdims=True), snk)
        p = jnp.exp2(sc - m)
        l = jnp.sum(p, axis=-1, keepdims=True) + jnp.exp2(snk - m)
        o = jnp.dot(p.astype(BF16), vb, preferred_element_type=F32)
        o_ref[...] = _unstack_heads((o / l).astype(BF16))
        lse = m + jnp.log2(l)
        for g in range(GROUP):
            lse_ref[g] = jnp.broadcast_to(lse[g * BLOCK_Q:(g + 1) * BLOCK_Q], (BLOCK_Q, HEAD_DIM))

    smem = pl.BlockSpec(memory_space=pltpu.SMEM)
    res, c_res = _call(
        body, name="attn_b_fwd", grid=(N_KV_B, nblk),
        in_specs=[
            smem,
            smem,
            pl.BlockSpec((BLOCK_Q, GROUP * HEAD_DIM), lambda kv, n: (n, COL_QB // GROUP + kv)),
            pl.BlockSpec((sp, HEAD_DIM), lambda kv, n: (0, kv)),
            pl.BlockSpec((sp, HEAD_DIM), lambda kv, n: (0, kv)),
            pl.BlockSpec((BLOCK_Q, 3 * BLOCK_Q), lambda kv, n: (0, 0)),
        ],
        out_specs=[
            pl.BlockSpec((BLOCK_Q, GROUP * HEAD_DIM), lambda kv, n: (n, kv)),
            pl.BlockSpec((GROUP, BLOCK_Q, HEAD_DIM), lambda kv, n: (kv, n, 0)),
        ],
        out_shape=[
            jax.ShapeDtypeStruct((s, N_HEADS_B * HEAD_DIM), BF16),
            jax.ShapeDtypeStruct((N_HEADS_B, s, HEAD_DIM), F32),
        ],
        scratch_shapes=[pltpu.VMEM((GQ, 3 * BLOCK_Q), F32)],
        sem=("arbitrary", "arbitrary"), args=(table, sink, pb, kpad, vpad, bucket), comm=comm)
    return res if comm is None else (res, c_res)


def _attn_b_bwd(pb, kpad, vpad, att, datt, lse, bucket, table, sink, *, comm=None):
    s = pb.shape[0]
    nblk = s // BLOCK_Q
    sp = kpad.shape[0]

    def body(table_ref, sink_ref, q_ref, k_ref, v_ref, o_ref, do_ref, lse_ref, bucket_ref,
             dq_ref, dk_ref, dv_ref, dtab_ref, dsink_ref, bias_ref, dbias_ref):
        kv = pl.program_id(0)
        n = pl.program_id(1)

        @pl.when(n == 0)
        def _():
            dk_ref[...] = jnp.zeros_like(dk_ref)
            dv_ref[...] = jnp.zeros_like(dv_ref)
            bias_ref[...] = _group_bias(bucket_ref[...], table_ref, kv)
            dbias_ref[...] = jnp.zeros_like(dbias_ref)
            dsink_ref[...] = jnp.zeros_like(dsink_ref)

        band = pl.ds(_band_start(n), 3 * BLOCK_Q)
        q = _stack_heads(q_ref[...])
        do = _stack_heads(do_ref[...])
        o = _stack_heads(o_ref[...])
        kb = k_ref[band, :]
        vb = v_ref[band, :]
        lse = jnp.concatenate([lse_ref[g][:, :1] for g in range(GROUP)], axis=0)
        sc = lax.dot_general(q, kb, _NT, preferred_element_type=F32) + bias_ref[...]
        sc = jnp.where(_band_mask(n, s), sc, NEG_INF)
        p = jnp.exp2(sc - lse)
        dp = lax.dot_general(do, vb, _NT, preferred_element_type=F32)
        delta = jnp.sum(do.astype(F32) * o.astype(F32), axis=-1, keepdims=True)
        ds = p * (dp - delta)
        dsb = ds.astype(BF16)
        dq_ref[...] = _unstack_heads(jnp.dot(dsb, kb, preferred_element_type=F32))
        dk_ref[band, :] += lax.dot_general(dsb, q, _TN, preferred_element_type=F32)
        dv_ref[band, :] += lax.dot_general(p.astype(BF16), do, _TN, preferred_element_type=F32)
        dbias_ref[...] += ds
        sink_part = -jnp.exp2(_group_sink(sink_ref, kv) - lse) * delta
        for g in range(GROUP):
            rows = slice(g * BLOCK_Q, (g + 1) * BLOCK_Q)
            dsink_ref[g] += jnp.broadcast_to(jnp.sum(sink_part[rows], axis=0, keepdims=True), (1, HEAD_DIM))

        @pl.when(n == nblk - 1)
        def _():
            bucket_v = bucket_ref[...]
            row = lax.broadcasted_iota(jnp.int32, (N_BUCKETS, HEAD_DIM), 0)
            for g in range(GROUP):
                acc = dbias_ref[g * BLOCK_Q:(g + 1) * BLOCK_Q, :]
                tot = jnp.zeros((N_BUCKETS, HEAD_DIM), F32)
                for b in range(N_BUCKETS):
                    tot = jnp.where(row == b, jnp.sum(jnp.where(bucket_v == b, acc, 0.0), keepdims=True), tot)
                dtab_ref[g] = tot

    smem = pl.BlockSpec(memory_space=pltpu.SMEM)
    wide = GROUP * HEAD_DIM
    res, c_res = _call(
        body, name="attn_b_bwd", grid=(N_KV_B, nblk),
        in_specs=[
            smem,
            smem,
            pl.BlockSpec((BLOCK_Q, wide), lambda kv, n: (n, COL_QB // GROUP + kv)),
            pl.BlockSpec((sp, HEAD_DIM), lambda kv, n: (0, kv)),
            pl.BlockSpec((sp, HEAD_DIM), lambda kv, n: (0, kv)),
            pl.BlockSpec((BLOCK_Q, wide), lambda kv, n: (n, N_HEADS_A // GROUP + kv)),
            pl.BlockSpec((BLOCK_Q, wide), lambda kv, n: (n, N_HEADS_A // GROUP + kv)),
            pl.BlockSpec((GROUP, BLOCK_Q, HEAD_DIM), lambda kv, n: (kv, n, 0)),
            pl.BlockSpec((BLOCK_Q, 3 * BLOCK_Q), lambda kv, n: (0, 0)),
        ],
        out_specs=[
            pl.BlockSpec((BLOCK_Q, wide), lambda kv, n: (n, kv)),
            pl.BlockSpec((sp, HEAD_DIM), lambda kv, n: (0, kv)),
            pl.BlockSpec((sp, HEAD_DIM), lambda kv, n: (0, kv)),
            pl.BlockSpec((GROUP, N_BUCKETS, HEAD_DIM), lambda kv, n: (kv, 0, 0)),
            pl.BlockSpec((GROUP, 1, HEAD_DIM), lambda kv, n: (kv, 0, 0)),
        ],
        out_shape=[
            jax.ShapeDtypeStruct((s, N_HEADS_B * HEAD_DIM), F32),
            jax.ShapeDtypeStruct((sp, N_KV_B * HEAD_DIM), F32),
            jax.ShapeDtypeStruct((sp, N_KV_B * HEAD_DIM), F32),
            jax.ShapeDtypeStruct((N_HEADS_B, N_BUCKETS, HEAD_DIM), F32),
            jax.ShapeDtypeStruct((N_HEADS_B, 1, HEAD_DIM), F32),
        ],
        scratch_shapes=[pltpu.VMEM((GQ, 3 * BLOCK_Q), F32), pltpu.VMEM((GQ, 3 * BLOCK_Q), F32)],
        sem=("arbitrary", "arbitrary"),
        args=(table, sink, pb, kpad, vpad, att, datt, lse, bucket), comm=comm)
    return res if comm is None else (res, c_res)


_MESH = pl.DeviceIdType.MESH


def _other_chips(x, y):
    return [(x, 1 - y), (1 - x, y), (1 - x, 1 - y)]


def _gather_comm(shards):
    n_t = len(shards)

    def copies(src, dst, sems, later):
        send_sems, recv_sems, local_sems = sems
        x, y, c = lax.axis_index("x"), lax.axis_index("y"), lax.axis_index("c")
        me = 2 * x + y
        sibling = (x, y, 1 - c)
        local, ici, landed, passed, from_sibling = [], [], [], [], []
        for t in range(n_t):
            half = shards[t].shape[0] // 2
            mine = pl.ds(pl.multiple_of(c * half, half), half)
            other = pl.ds(pl.multiple_of((1 - c) * half, half), half)

            def copy(k, src_ref, dst_ref, to, t=t):
                return pltpu.make_async_remote_copy(
                    src_ref=src_ref, dst_ref=dst_ref, send_sem=send_sems.at[6 * t + k],
                    recv_sem=recv_sems.at[6 * t + k], device_id=to, device_id_type=_MESH)

            local.append(pltpu.make_async_copy(src[t], dst[t].at[me], local_sems.at[t]))
            for j, (px, py) in enumerate(_other_chips(x, y)):
                k = 2 * px + py
                ici.append(copy(j, src[t].at[mine], dst[t].at[me, mine], (px, py, c)))
                if later:
                    landed.append(copy(j, src[t].at[mine], dst[t].at[k, mine], (px, py, c)))
                    passed.append(copy(3 + j, dst[t].at[k, mine], dst[t].at[k, mine], sibling))
                    from_sibling.append(copy(3 + j, dst[t].at[k, other], dst[t].at[k, other], sibling))
        return local, ici, landed, passed, from_sibling

    def start(src, dst, sems):
        local, ici, _, _, _ = copies(src, dst, sems, False)
        for cp in local + ici:
            cp.start()

    def finish(src, dst, sems):
        local, ici, landed, passed, from_sibling = copies(src, dst, sems, True)
        for got, fwd in zip(landed, passed):
            got.wait_recv()
            fwd.start()
        for cp in from_sibling:
            cp.wait_recv()
        for cp in ici + passed:
            cp.wait_send()
        for cp in local:
            cp.wait()

    return _Comm(
        shards, [jax.ShapeDtypeStruct((N_CHIPS,) + w.shape, w.dtype) for w in shards],
        [pltpu.SemaphoreType.DMA((6 * n_t,)), pltpu.SemaphoreType.DMA((6 * n_t,)), pltpu.SemaphoreType.DMA((n_t,))],
        start, finish)


def _exchange_comm(partials):
    n_t = len(partials)

    def copies(src, dst, sems, later):
        send_sems, recv_sems, local_sems = sems
        x, y, c = lax.axis_index("x"), lax.axis_index("y"), lax.axis_index("c")
        me = 2 * x + y
        sibling = (x, y, 1 - c)
        local, first, landed, passed, from_sibling = [], [], [], [], []
        for t in range(n_t):

            def copy(k, src_ref, slot, to, t=t):
                return pltpu.make_async_remote_copy(
                    src_ref=src_ref, dst_ref=dst[t].at[slot], send_sem=send_sems.at[7 * t + k],
                    recv_sem=recv_sems.at[7 * t + k], device_id=to, device_id_type=_MESH)

            local.append(pltpu.make_async_copy(src[t].at[me], dst[t].at[2 * me + c], local_sems.at[t]))
            for j, (px, py) in enumerate(_other_chips(x, y)):
                k = 2 * px + py
                first.append(copy(1 + j, src[t].at[k], 2 * me + c, (px, py, c)))
                if later:
                    landed.append(copy(1 + j, dst[t].at[2 * k + c], 2 * k + c, (px, py, c)))
                    passed.append(copy(4 + j, dst[t].at[2 * k + c], 2 * k + c, sibling))
                    from_sibling.append(copy(4 + j, dst[t].at[2 * k + 1 - c], 2 * k + 1 - c, sibling))
            first.append(copy(0, src[t].at[me], 2 * me + c, sibling))
            if later:
                from_sibling.append(copy(0, src[t].at[me], 2 * me + 1 - c, sibling))
        return local, first, landed, passed, from_sibling

    def start(src, dst, sems):
        local, first, _, _, _ = copies(src, dst, sems, False)
        for cp in local + first:
            cp.start()

    def finish(src, dst, sems):
        local, first, landed, passed, from_sibling = copies(src, dst, sems, True)
        for got, fwd in zip(landed, passed):
            got.wait_recv()
            fwd.start()
        for cp in from_sibling:
            cp.wait_recv()
        for cp in first + passed:
            cp.wait_send()
        for cp in local:
            cp.wait()

    return _Comm(
        partials, [jax.ShapeDtypeStruct((N_DEV,) + g.shape[1:], g.dtype) for g in partials],
        [pltpu.SemaphoreType.DMA((7 * n_t,)), pltpu.SemaphoreType.DMA((7 * n_t,)), pltpu.SemaphoreType.DMA((n_t,))],
        start, finish)


def _allreduce_small(pack):
    rows, d = pack.shape

    def body(p_ref, sum_ref, all_ref, send_sems, recv_sems):
        x, y, c = lax.axis_index("x"), lax.axis_index("y"), lax.axis_index("c")
        me = 4 * x + 2 * y + c
        all_ref[me] = p_ref[...]
        peers = []
        for dx in range(2):
            for dy in range(2):
                for dc in range(2):
                    if dx or dy or dc:
                        px = 1 - x if dx else x
                        py = 1 - y if dy else y
                        pc = 1 - c if dc else c
                        peers.append((4 * dx + 2 * dy + dc - 1, (px, py, pc)))
        sends = []
        for k, to in peers:
            cp = pltpu.make_async_remote_copy(
                src_ref=p_ref, dst_ref=all_ref.at[me], send_sem=send_sems.at[k], recv_sem=recv_sems.at[k],
                device_id=to, device_id_type=_MESH)
            cp.start()
            sends.append(cp)
        for k, (px, py, pc) in peers:
            pltpu.make_async_remote_copy(
                src_ref=p_ref, dst_ref=all_ref.at[4 * px + 2 * py + pc], send_sem=send_sems.at[k],
                recv_sem=recv_sems.at[k], device_id=(px, py, pc), device_id_type=_MESH).wait_recv()
        for cp in sends:
            cp.wait_send()
        tot = all_ref[0]
        for i in range(1, N_DEV):
            tot = tot + all_ref[i]
        sum_ref[...] = tot

    vm = pl.BlockSpec(memory_space=pltpu.VMEM)
    return pl.pallas_call(
        body,
        name="allreduce_small",
        in_specs=[vm],
        out_specs=vm,
        out_shape=jax.ShapeDtypeStruct((rows, d), F32),
        scratch_shapes=[
            pltpu.VMEM((N_DEV, rows, d), F32),
            pltpu.SemaphoreType.DMA((N_DEV - 1,)),
            pltpu.SemaphoreType.DMA((N_DEV - 1,)),
        ],
    )(pack)


def _adamw_math(w, g, m, v):
    m = ADAM_B1 * m + (1.0 - ADAM_B1) * g
    v = ADAM_B2 * v + (1.0 - ADAM_B2) * (g * g)
    m_hat = m / (1.0 - ADAM_B1 ** ADAM_STEP)
    v_hat = v / (1.0 - ADAM_B2 ** ADAM_STEP)
    delta = -ADAM_LR * (m_hat / (jnp.sqrt(v_hat) + ADAM_EPS) + ADAM_WD * w)
    return delta, m, v


def _sum_adamw(parts, w, m, v, *, name, tr=256):
    r, c = w.shape
    tr = min(tr, r)
    tc = min(c, 1024)

    def body(p_ref, w_ref, m_ref, v_ref, g_ref, d_ref, m2_ref, v2_ref):
        g = p_ref[0].astype(F32)
        for i in range(1, N_DEV):
            g = g + p_ref[i].astype(F32)
        delta, m2, v2 = _adamw_math(w_ref[...], g, m_ref[...], v_ref[...])
        g_ref[...] = g
        d_ref[...] = delta
        m2_ref[...] = m2
        v2_ref[...] = v2

    blk = pl.BlockSpec((tr, tc), lambda i, j: (i, j))
    return pl.pallas_call(
        body,
        name=name,
        grid=(r // tr, c // tc),
        in_specs=[pl.BlockSpec((N_DEV, tr, tc), lambda i, j: (0, i, j)), blk, blk, blk],
        out_specs=[blk] * 4,
        out_shape=[jax.ShapeDtypeStruct((r, c), F32)] * 4,
        compiler_params=_params(("parallel", "parallel")),
    )(parts, w, m, v)


def _adamw_small(g, w, m, v):
    def body(g_ref, w_ref, m_ref, v_ref, d_ref, m2_ref, v2_ref):
        delta, m2, v2 = _adamw_math(w_ref[...], g_ref[...], m_ref[...], v_ref[...])
        d_ref[...] = delta
        m2_ref[...] = m2
        v2_ref[...] = v2

    vm = pl.BlockSpec(memory_space=pltpu.VMEM)
    return pl.pallas_call(
        body,
        name="adamw_small",
        in_specs=[vm] * 4,
        out_specs=[vm] * 3,
        out_shape=[jax.ShapeDtypeStruct(g.shape, F32)] * 3,
    )(g, w, m, v)


def _relu2_epilogue(acc):
    ra = jnp.maximum(acc, 0.0)
    return ra * ra, ra


def _rows(stacked):
    return stacked.reshape(stacked.shape[0] * stacked.shape[1], stacked.shape[2])


def _by_chip(mat):
    return mat.reshape(N_CHIPS, mat.shape[0] // N_CHIPS, mat.shape[1])


def _local_step(x, p, target, shards, small):
    s, d = x.shape
    cos_t, sin_t = _rope_tables(s)
    bucket = _band_buckets()
    p_bf = p.astype(BF16)
    wts = {}

    u, (wts["w_in"],) = _rms_fwd(x, small["attn_norm_g"], name="norm_attn", comm=_gather_comm([shards["w_in"]]))
    proj, (w_out_s, wts["ple_w"], w_gate_s) = _matmul(
        u, wts["w_in"], mode="nn", out_dtypes=[F32], name="mm_in", bn=768,
        comm=_gather_comm([shards["w_out"], shards["ple_w"], shards["w_gate"]]))
    wts["w_out"], wts["w_gate"] = _rows(w_out_s), _rows(w_gate_s)
    pb = _qk_prep(proj, small["q_norm_g"], small["k_norm_g"], cos_t, sin_t)
    (oa, lse_a), (wts["w_up"],) = _attn_a_fwd(pb, comm=_gather_comm([shards["w_up"]]))
    pad = ((PAD_LO, PAD_HI), (0, 0))
    kpad = jnp.pad(pb[:, COL_KB * HEAD_DIM:COL_VB * HEAD_DIM], pad)
    vpad = jnp.pad(pb[:, COL_VB * HEAD_DIM:], pad)
    (ob, lse_b), (w_down_s,) = _attn_b_fwd(pb, kpad, vpad, bucket, small["rel_bias_table"], small["sink_logits"],
                                            comm=_gather_comm([shards["w_down"]]))
    wts["w_down"] = _rows(w_down_s)
    att = jnp.concatenate([oa, ob], axis=-1)
    h1 = _matmul(att, wts["w_out"], mode="nn", out_dtypes=[F32], name="mm_out",
                 epilogue=lambda acc, res: (acc + res,), extras=(x,))
    mn = _rms_fwd(h1, small["mlp_norm_g"], name="norm_mlp")
    r, ra = _matmul(mn, wts["w_up"], mode="nn", out_dtypes=[BF16, BF16], name="mm_up", epilogue=_relu2_epilogue)
    h2 = _matmul(r, wts["w_down"], mode="nn", out_dtypes=[F32], name="mm_down",
                 epilogue=lambda acc, res: (acc + res,), extras=(h1,))
    ng = _rms_fwd(h2, small["gate_norm_g"], name="norm_gate")
    gate = _matmul(ng, wts["w_gate"], mode="nn", out_dtypes=[F32], name="mm_gate",
                   epilogue=lambda acc: (1.0 / (1.0 + jnp.exp(-acc)),))
    pp = _matmul(p_bf, wts["ple_w"], mode="nn", out_dtypes=[F32], name="mm_ple", bn=512)
    dh3, dz, dpp, dg_final, dg_ple, loss = _tail(h2, gate, pp, target, small["ple_norm_g"], small["final_norm_g"])

    dng = _matmul(dz, wts["w_gate"], mode="nt", out_dtypes=[F32], name="mm_gate_dx")
    gw_gate = _matmul(ng, dz, mode="tn", out_dtypes=[BF16], name="mm_gate_dw")
    gw_ple = _matmul(p_bf, dpp, mode="tn", out_dtypes=[BF16], name="mm_ple_dw", bn=512, out_stack=N_CHIPS)
    dh2, dh2_bf, dg_gate = _rms_bwd(h2, dng, small["gate_norm_g"], dh3, name="norm_gate_bwd", want_bf16=True)
    big = {}
    da, (big["w_gate"], big["ple_w"]) = _matmul(
        dh2_bf, wts["w_down"], mode="nt", out_dtypes=[BF16], name="mm_down_dx",
        epilogue=lambda acc, ra_v: (acc * (2.0 * ra_v.astype(F32)),), extras=(ra,),
        comm=_exchange_comm([_by_chip(gw_gate), gw_ple]))
    gw_down = _matmul(r, dh2_bf, mode="tn", out_dtypes=[BF16], name="mm_down_dw")
    gw_up = _matmul(mn, da, mode="tn", out_dtypes=[BF16], name="mm_up_dw", out_stack=N_CHIPS)
    dmn = _matmul(da, wts["w_up"], mode="nt", out_dtypes=[F32], name="mm_up_dx")
    dh1, dh1_bf, dg_mlp = _rms_bwd(h1, dmn, small["mlp_norm_g"], dh2, name="norm_mlp_bwd", want_bf16=True)
    datt = _matmul(dh1_bf, wts["w_out"], mode="nt", out_dtypes=[BF16], name="mm_out_dx")
    gw_out = _matmul(att, dh1_bf, mode="tn", out_dtypes=[BF16], name="mm_out_dw")
    (dqb, dkpad, dvpad, dtab, dsink), (big["w_up"],) = _attn_b_bwd(
        pb, kpad, vpad, att, datt, lse_b, bucket, small["rel_bias_table"], small["sink_logits"],
        comm=_exchange_comm([gw_up]))
    (dqa, dka, dva), (big["w_down"],) = _attn_a_bwd(pb, att, datt, lse_a, comm=_exchange_comm([_by_chip(gw_down)]))
    (dproj, dg_q, dg_k), (big["w_out"],) = _qk_bwd(dqa, dka, dva, dqb, dkpad, dvpad, proj,
                                                   small["q_norm_g"], small["k_norm_g"], cos_t, sin_t,
                                                   comm=_exchange_comm([_by_chip(gw_out)]))
    gw_in = _matmul(u, dproj, mode="tn", out_dtypes=[BF16], name="mm_in_dw", bn=768, out_stack=N_CHIPS)
    du, (big["w_in"],) = _matmul(dproj, wts["w_in"], mode="nt", out_dtypes=[F32], name="mm_in_dx", bk=768,
                                 comm=_exchange_comm([gw_in]))
    grad_x, dg_attn = _rms_bwd(x, du, small["attn_norm_g"], dh1, name="norm_attn_bwd", want_bf16=False)

    small_g = {
        "attn_norm_g": dg_attn, "mlp_norm_g": dg_mlp, "ple_norm_g": dg_ple, "gate_norm_g": dg_gate,
        "final_norm_g": dg_final, "q_norm_g": dg_q, "k_norm_g": dg_k,
        "sink_logits": dsink[:, 0, 0][None, :], "rel_bias_table": dtab[:, :, 0].T,
    }
    return loss, grad_x, big, small_g


_SMALL_ROWS = ["attn_norm_g", "mlp_norm_g", "ple_norm_g", "gate_norm_g", "final_norm_g"]
_PACK_ROWS = 8


def _pack_small(vals, d):
    rows = [vals[n].reshape(1, d) for n in _SMALL_ROWS]
    misc = jnp.concatenate([
        vals["q_norm_g"].reshape(1, HEAD_DIM), vals["k_norm_g"].reshape(1, HEAD_DIM),
        jnp.pad(vals["sink_logits"].reshape(1, N_HEADS_B), ((0, 0), (0, HEAD_DIM - N_HEADS_B))),
        vals["rel_bias_table"].reshape(1, N_BUCKETS * N_HEADS_B)], axis=1)
    rows.append(jnp.pad(misc, ((0, 0), (0, d - misc.shape[1]))))
    rows.append(jnp.zeros((_PACK_ROWS - len(rows), d), F32))
    return jnp.concatenate(rows, axis=0).astype(F32)


def _unpack_small(pack, shapes):
    out = {n: pack[i].reshape(shapes[n]) for i, n in enumerate(_SMALL_ROWS)}
    misc = pack[len(_SMALL_ROWS)]
    out["q_norm_g"] = misc[:HEAD_DIM].reshape(shapes["q_norm_g"])
    out["k_norm_g"] = misc[HEAD_DIM:2 * HEAD_DIM].reshape(shapes["k_norm_g"])
    out["sink_logits"] = misc[2 * HEAD_DIM:2 * HEAD_DIM + N_HEADS_B].reshape(shapes["sink_logits"])
    out["rel_bias_table"] = misc[3 * HEAD_DIM:3 * HEAD_DIM + N_BUCKETS * N_HEADS_B].reshape(shapes["rel_bias_table"])
    return out


_WEIGHTS = ["attn_norm_g", "w_in", "q_norm_g", "k_norm_g", "sink_logits", "w_out", "mlp_norm_g", "w_up", "w_down",
            "ple_w", "ple_norm_g", "gate_norm_g", "w_gate", "rel_bias_table", "final_norm_g"]
_BIG = ["w_in", "w_out", "w_up", "w_down", "ple_w", "w_gate"]


def kernel(x, p, attn_norm_g, w_in, q_norm_g, k_norm_g, sink_logits, w_out, mlp_norm_g, w_up, w_down, ple_w, ple_norm_g, gate_norm_g, w_gate, rel_bias_table, final_norm_g, loss_target, m_attn_norm_g, m_w_in, m_q_norm_g, m_k_norm_g, m_sink_logits, m_w_out, m_mlp_norm_g, m_w_up, m_w_down, m_ple_w, m_ple_norm_g, m_gate_norm_g, m_w_gate, m_rel_bias_table, m_final_norm_g, v_attn_norm_g, v_w_in, v_q_norm_g, v_k_norm_g, v_sink_logits, v_w_out, v_mlp_norm_g, v_w_up, v_w_down, v_ple_w, v_ple_norm_g, v_gate_norm_g, v_w_gate, v_rel_bias_table, v_final_norm_g):
    given = dict(locals())
    w = {n: given[n] for n in _WEIGHTS}
    m = {n: given["m_" + n] for n in _WEIGHTS}
    v = {n: given["v_" + n] for n in _WEIGHTS}
    d = x.shape[-1]

    shards = {n: w[n][0].astype(BF16) for n in _BIG}
    small = {
        "attn_norm_g": w["attn_norm_g"], "mlp_norm_g": w["mlp_norm_g"], "ple_norm_g": w["ple_norm_g"],
        "gate_norm_g": w["gate_norm_g"], "final_norm_g": w["final_norm_g"].reshape(1, d),
        "q_norm_g": w["q_norm_g"], "k_norm_g": w["k_norm_g"], "sink_logits": w["sink_logits"],
        "rel_bias_table": w["rel_bias_table"],
    }

    loss_part, grad_x, exchanged, small_g = _local_step(x[0], p[0, 0], loss_target[0], shards, small)
    loss = lax.psum(loss_part[0, 0], ("x", "y", "c"))

    grads, deltas, new_m, new_v = {}, {}, {}, {}
    for n in _BIG:
        shape = w[n].shape
        res = _sum_adamw(exchanged[n], w[n][0], m[n][0], v[n][0], name="adamw_" + n)
        grads[n], deltas[n], new_m[n], new_v[n] = [t.reshape(shape) for t in res]

    shapes = {n: w[n].shape for n in _WEIGHTS if n not in _BIG}
    g_small = _allreduce_small(_pack_small(small_g, d))
    d_small, m_small, v_small = _adamw_small(g_small, _pack_small(w, d), _pack_small(m, d), _pack_small(v, d))
    grads.update(_unpack_small(g_small, shapes))
    deltas.update(_unpack_small(d_small, shapes))
    new_m.update(_unpack_small(m_small, shapes))
    new_v.update(_unpack_small(v_small, shapes))

    return (loss, grad_x[None], *[grads[n] for n in _WEIGHTS], *[deltas[n] for n in _WEIGHTS],
            *[new_m[n] for n in _WEIGHTS], *[new_v[n] for n in _WEIGHTS])
```

```python
import functools
import math

import jax
import jax.numpy as jnp
import numpy as np
from jax import lax
from jax.experimental import pallas as pl
from jax.experimental.pallas import tpu as pltpu

F32 = jnp.float32
BF16 = jnp.bfloat16

HEAD_DIM = 128
N_HEADS_A = 8
N_KV_A = 2
N_HEADS_B = 8
N_KV_B = 2
GROUP = 4
GRID_W = 64
BLOCK_Q = 128
WINDOW = 128
N_BUCKETS = 32
MAX_DISTANCE = 128
ROPE_THETA = 10000.0
EPS = 1e-6
NEG_INF = -1e30
ATT_SCALE = HEAD_DIM ** -0.5
LOG2E = math.log2(math.e)
LN2 = math.log(2.0)
Q_SCALE = ATT_SCALE * LOG2E
PAD_LO, PAD_HI = 256, 128

ADAM_LR = 0.001
ADAM_B1 = 0.9
ADAM_B2 = 0.999
ADAM_EPS = 1e-08
ADAM_WD = 0.01
ADAM_STEP = 10

N_CHIPS = 4
N_DEV = 8
COL_QA, COL_KA, COL_VA, COL_QB, COL_KB, COL_VB = 0, 8, 10, 12, 20, 22
N_COLS = 24

VMEM_LIMIT = 52 * 1024 * 1024


def _params(sem=None):
    return pltpu.CompilerParams(dimension_semantics=sem, vmem_limit_bytes=VMEM_LIMIT)


_ANY = pl.BlockSpec(memory_space=pl.ANY)


class _Comm:
    def __init__(self, inputs, out_shapes, sems, start, finish, aliases=None):
        self.inputs, self.out_shapes, self.sems = list(inputs), list(out_shapes), list(sems)
        self.start, self.finish, self.aliases = start, finish, dict(aliases or {})


def _call(body, *, name, grid, in_specs, out_specs, out_shape, args, scratch_shapes=(), sem=None, comm=None):
    in_specs, out_specs, out_shape = list(in_specs), list(out_specs), list(out_shape)
    scratch_shapes = list(scratch_shapes)
    if comm is None:
        res = pl.pallas_call(
            body, name=name, grid=grid, in_specs=in_specs, out_specs=out_specs, out_shape=out_shape,
            scratch_shapes=scratch_shapes, compiler_params=_params(sem))(*args)
        return list(res), []
    n_in, n_out, n_sc = len(in_specs), len(out_specs), len(scratch_shapes)
    c_in, c_out = len(comm.inputs), len(comm.out_shapes)

    def hosted(*refs):
        pos = [0]

        def take(n):
            pos[0] += n
            return refs[pos[0] - n:pos[0]]

        ins, c_ins, outs, c_outs, scr = take(n_in), take(c_in), take(n_out), take(c_out), take(n_sc)
        c_sems = refs[pos[0]:]
        ids = [pl.program_id(a) for a in range(len(grid))]
        first = functools.reduce(jnp.logical_and, [i == 0 for i in ids])
        last = functools.reduce(jnp.logical_and, [i == g - 1 for i, g in zip(ids, grid)])

        @pl.when(first)
        def _():
            comm.start(c_ins, c_outs, c_sems)

        body(*ins, *outs, *scr)

        @pl.when(last)
        def _():
            comm.finish(c_ins, c_outs, c_sems)

    res = pl.pallas_call(
        hosted, name=name, grid=grid, in_specs=in_specs + [_ANY] * c_in, out_specs=out_specs + [_ANY] * c_out,
        out_shape=out_shape + comm.out_shapes, scratch_shapes=scratch_shapes + comm.sems,
        input_output_aliases={n_in + i: n_out + o for i, o in comm.aliases.items()},
        compiler_params=_params(("arbitrary",) * len(grid)))(*args, *comm.inputs)
    return list(res[:n_out]), list(res[n_out:])


def _matmul(a, b, *, mode, out_dtypes, name, epilogue=None, extras=(), bm=1024, bn=1024, bk=2048,
            out_stack=0, comm=None):
    stacked = b.ndim == 3
    if mode == "nn":
        m, k = a.shape
        if stacked:
            nj, kb, ns = b.shape
            n, ks = nj * ns, k
        else:
            kb, n = b.shape
            ns, ks = n, k
        dn = (((1,), (0,)), ((), ()))
    elif mode == "nt":
        m, k = a.shape
        if stacked:
            nj, n, ks = b.shape
            kb = nj * ks
        else:
            n, kb = b.shape
            ks = kb
        ns = n
        dn = (((1,), (1,)), ((), ()))
    else:
        k, m = a.shape
        kb, n = b.shape
        ns, ks = n, k
        dn = (((0,), (0,)), ((), ()))
    assert k == kb and not (stacked and mode == "tn")
    ns_out = n // out_stack if out_stack else n
    bm, bn, bk = min(bm, m), min(bn, ns, ns_out), min(bk, ks)
    assert m % bm == 0 and ns % bn == 0 and ns_out % bn == 0 and ks % bk == 0
    gm, gn, gk = m // bm, n // bn, k // bk

    if mode == "tn":
        a_spec = pl.BlockSpec((bk, bm), lambda i, j, q: (q, i))
    else:
        a_spec = pl.BlockSpec((bm, bk), lambda i, j, q: (i, q))
    if mode == "nt":
        if stacked:
            per = ks // bk
            b_spec = pl.BlockSpec((None, bn, bk), lambda i, j, q: (q // per, j, q % per))
        else:
            b_spec = pl.BlockSpec((bn, bk), lambda i, j, q: (j, q))
    else:
        if stacked:
            per = ns // bn
            b_spec = pl.BlockSpec((None, bk, bn), lambda i, j, q: (j // per, q, j % per))
        else:
            b_spec = pl.BlockSpec((bk, bn), lambda i, j, q: (q, j))
    ex_spec = pl.BlockSpec((bm, bn), lambda i, j, q: (i, j))
    if out_stack:
        per_o = ns_out // bn
        o_spec = pl.BlockSpec((None, bm, bn), lambda i, j, q: (j // per_o, i, j % per_o))
        o_shape = (out_stack, m, ns_out)
    else:
        o_spec = ex_spec
        o_shape = (m, n)
    n_ex, n_out = len(extras), len(out_dtypes)

    def body(a_ref, b_ref, *rest):
        ex, outs = rest[:n_ex], rest[n_ex:n_ex + n_out]
        part = lax.dot_general(a_ref[...], b_ref[...], dn, preferred_element_type=F32)

        def finish(acc):
            res = epilogue(acc, *[e[...] for e in ex]) if epilogue else (acc,)
            for o, r in zip(outs, res):
                o[...] = r.astype(o.dtype)

        if gk == 1:
            finish(part)
        else:
            acc_ref = rest[-1]
            q = pl.program_id(2)

            @pl.when(q == 0)
            def _():
                acc_ref[...] = part

            @pl.when(q > 0)
            def _():
                acc_ref[...] += part

            @pl.when(q == gk - 1)
            def _():
                finish(acc_ref[...])

    res, c_res = _call(
        body, name=name, grid=(gm, gn, gk),
        in_specs=[a_spec, b_spec] + [ex_spec] * n_ex,
        out_specs=[o_spec] * n_out,
        out_shape=[jax.ShapeDtypeStruct(o_shape, dt) for dt in out_dtypes],
        scratch_shapes=[pltpu.VMEM((bm, bn), F32)] if gk > 1 else [],
        sem=("parallel", "parallel", "arbitrary"), args=(a, b, *extras), comm=comm)
    res = res[0] if n_out == 1 else res
    return res if comm is None else (res, c_res)


def _rms_fwd(x, g, *, name, tm=256, comm=None):
    s, d = x.shape
    tm = min(tm, s)

    def body(x_ref, g_ref, o_ref):
        xf = x_ref[...]
        r = lax.rsqrt(jnp.mean(xf * xf, axis=-1, keepdims=True) + EPS)
        o_ref[...] = (xf * r * g_ref[...]).astype(o_ref.dtype)

    res, c_res = _call(
        body, name=name, grid=(s // tm,),
        in_specs=[pl.BlockSpec((tm, d), lambda i: (i, 0)), pl.BlockSpec((1, d), lambda i: (0, 0))],
        out_specs=[pl.BlockSpec((tm, d), lambda i: (i, 0))],
        out_shape=[jax.ShapeDtypeStruct((s, d), BF16)],
        sem=("parallel",), args=(x, g), comm=comm)
    return res[0] if comm is None else (res[0], c_res)


def _rms_bwd(x, dy, g, add, *, name, want_bf16, tm=256):
    s, d = x.shape
    tm = min(tm, s)

    def body(x_ref, dy_ref, g_ref, add_ref, dx_ref, *rest):
        dg_ref = rest[-1]
        i = pl.program_id(0)
        xf = x_ref[...]
        dyf = dy_ref[...].astype(F32)
        r = lax.rsqrt(jnp.mean(xf * xf, axis=-1, keepdims=True) + EPS)
        xh = xf * r
        dyg = dyf * g_ref[...]
        dx = r * (dyg - xh * jnp.mean(dyg * xh, axis=-1, keepdims=True))
        tot = add_ref[...] + dx
        dx_ref[...] = tot
        if want_bf16:
            rest[0][...] = tot.astype(BF16)
        part = jnp.sum(dyf * xh, axis=0, keepdims=True)

        @pl.when(i == 0)
        def _():
            dg_ref[...] = part

        @pl.when(i > 0)
        def _():
            dg_ref[...] += part

    row = pl.BlockSpec((tm, d), lambda i: (i, 0))
    vec = pl.BlockSpec((1, d), lambda i: (0, 0))
    out_specs = [row] + ([row] if want_bf16 else []) + [vec]
    out_shape = [jax.ShapeDtypeStruct((s, d), F32)]
    if want_bf16:
        out_shape.append(jax.ShapeDtypeStruct((s, d), BF16))
    out_shape.append(jax.ShapeDtypeStruct((1, d), F32))
    return pl.pallas_call(
        body,
        name=name,
        grid=(s // tm,),
        in_specs=[row, row, vec, row],
        out_specs=out_specs,
        out_shape=out_shape,
        compiler_params=_params(("arbitrary",)),
    )(x, dy, g, add)


def _tail(h2, gate, pp, target, g_ple, g_final, *, tm=128):
    s, d = h2.shape
    tm = min(tm, s)

    def body(h2_ref, gate_ref, pp_ref, t_ref, gp_ref, gf_ref, dh3_ref, dz_ref, dpp_ref, dgf_ref, dgp_ref, loss_ref):
        i = pl.program_id(0)
        ppf = pp_ref[...]
        gate_v = gate_ref[...]
        r_p = lax.rsqrt(jnp.mean(ppf * ppf, axis=-1, keepdims=True) + EPS)
        eh = ppf * r_p
        e = eh * gp_ref[...]
        h3 = h2_ref[...] + gate_v * e
        r_f = lax.rsqrt(jnp.mean(h3 * h3, axis=-1, keepdims=True) + EPS)
        yh = h3 * r_f
        diff = yh * gf_ref[...] - t_ref[...]
        loss_part = 0.5 * jnp.sum(jnp.mean(diff * diff, axis=-1, keepdims=True), axis=0, keepdims=True)
        dy = diff / d
        dgf = jnp.sum(dy * yh, axis=0, keepdims=True)
        dyg = dy * gf_ref[...]
        dh3 = r_f * (dyg - yh * jnp.mean(dyg * yh, axis=-1, keepdims=True))
        dh3_ref[...] = dh3
        de = dh3 * gate_v
        dz_ref[...] = (dh3 * e * gate_v * (1.0 - gate_v)).astype(BF16)
        dgp = jnp.sum(de * eh, axis=0, keepdims=True)
        deg = de * gp_ref[...]
        dpp_ref[...] = (r_p * (deg - eh * jnp.mean(deg * eh, axis=-1, keepdims=True))).astype(BF16)
        loss_row = jnp.broadcast_to(loss_part, (1, 128))

        @pl.when(i == 0)
        def _():
            dgf_ref[...] = dgf
            dgp_ref[...] = dgp
            loss_ref[...] = loss_row

        @pl.when(i > 0)
        def _():
            dgf_ref[...] += dgf
            dgp_ref[...] += dgp
            loss_ref[...] += loss_row

    row = pl.BlockSpec((tm, d), lambda i: (i, 0))
    vec = pl.BlockSpec((1, d), lambda i: (0, 0))
    return pl.pallas_call(
        body,
        name="tail_fwd_bwd",
        grid=(s // tm,),
        in_specs=[row, row, row, row, vec, vec],
        out_specs=[row, row, row, vec, vec, pl.BlockSpec((1, 128), lambda i: (0, 0))],
        out_shape=[
            jax.ShapeDtypeStruct((s, d), F32),
            jax.ShapeDtypeStruct((s, d), BF16),
            jax.ShapeDtypeStruct((s, d), BF16),
            jax.ShapeDtypeStruct((1, d), F32),
            jax.ShapeDtypeStruct((1, d), F32),
            jax.ShapeDtypeStruct((1, 128), F32),
        ],
        compiler_params=_params(("arbitrary",)),
    )(h2, gate, pp, target, g_ple, g_final)


def _rope_tables(s):
    t = jnp.arange(s, dtype=jnp.int32)
    row = (t // GRID_W).astype(F32)
    col = (t % GRID_W).astype(F32)
    half = HEAD_DIM // 2
    inv_freq = ROPE_THETA ** (-jnp.arange(0, half, 2, dtype=F32) / half)
    ang_r = row[:, None] * inv_freq
    ang_c = col[:, None] * inv_freq
    cr, sr, cc, sc = jnp.cos(ang_r), jnp.sin(ang_r), jnp.cos(ang_c), jnp.sin(ang_c)
    cos_t = jnp.concatenate([cr, cr, cc, cc], axis=-1)
    sin_t = jnp.concatenate([-sr, sr, -sc, sc], axis=-1)
    return cos_t, sin_t


def _swap_quarters(x):
    lane = lax.broadcasted_iota(jnp.int32, x.shape, x.ndim - 1)
    up = pltpu.roll(x, HEAD_DIM - 32, x.ndim - 1)
    down = pltpu.roll(x, 32, x.ndim - 1)
    return jnp.where((lane % 64) < 32, up, down)


def _cols(first, count=1):
    return slice(first * HEAD_DIM, (first + count) * HEAD_DIM)


def _qk_prep(proj, g_q, g_k, cos_t, sin_t, *, tm=256, comm=None):
    s, n = proj.shape
    tm = min(tm, s)

    def body(x_ref, gq_ref, gk_ref, c_ref, s_ref, o_ref):
        cos_v, sin_v = c_ref[...], s_ref[...]
        for h in range(COL_VA):
            x = x_ref[:, _cols(h)]
            g = gq_ref[...] if h < COL_KA else gk_ref[...]
            xn = x * lax.rsqrt(jnp.mean(x * x, axis=-1, keepdims=True) + EPS) * g
            xr = xn * cos_v + _swap_quarters(xn) * sin_v
            if h < COL_KA:
                xr = xr * Q_SCALE
            o_ref[:, _cols(h)] = xr.astype(BF16)
        o_ref[:, _cols(COL_VA, 2)] = x_ref[:, _cols(COL_VA, 2)].astype(BF16)
        o_ref[:, _cols(COL_QB, N_HEADS_B)] = (x_ref[:, _cols(COL_QB, N_HEADS_B)] * Q_SCALE).astype(BF16)
        o_ref[:, _cols(COL_KB, 4)] = x_ref[:, _cols(COL_KB, 4)].astype(BF16)

    row = pl.BlockSpec((tm, n), lambda i: (i, 0))
    tab = pl.BlockSpec((tm, HEAD_DIM), lambda i: (i, 0))
    vec = pl.BlockSpec((1, HEAD_DIM), lambda i: (0, 0))
    res, c_res = _call(
        body, name="qk_prep", grid=(s // tm,),
        in_specs=[row, vec, vec, tab, tab],
        out_specs=[row],
        out_shape=[jax.ShapeDtypeStruct((s, n), BF16)],
        sem=("parallel",), args=(proj, g_q, g_k, cos_t, sin_t), comm=comm)
    return res[0] if comm is None else (res[0], c_res)


def _qk_bwd(dqa, dka, dva, dqb, dkpad, dvpad, proj, g_q, g_k, cos_t, sin_t, *, comm=None):
    s, n = proj.shape
    tm = min(PAD_LO, s)
    assert PAD_LO % tm == 0
    lo = PAD_LO // tm

    def body(dqa_ref, dka_ref, dva_ref, dqb_ref, dkb_ref, dvb_ref, x_ref, gq_ref, gk_ref, c_ref, s_ref,
             o_ref, dgq_ref, dgk_ref):
        i = pl.program_id(0)
        cos_v, sin_v = c_ref[...], s_ref[...]

        def head(d, x, g):
            dn = d * cos_v + _swap_quarters(d * sin_v)
            r = lax.rsqrt(jnp.mean(x * x, axis=-1, keepdims=True) + EPS)
            xh = x * r
            dng = dn * g
            dx = r * (dng - xh * jnp.mean(dng * xh, axis=-1, keepdims=True))
            return dx.astype(BF16), jnp.sum(dn * xh, axis=0, keepdims=True)

        acc_q = jnp.zeros((1, HEAD_DIM), F32)
        acc_k = jnp.zeros((1, HEAD_DIM), F32)
        for h in range(N_HEADS_A):
            o_ref[:, _cols(h)], part = head(dqa_ref[:, _cols(h)] * ATT_SCALE, x_ref[:, _cols(h)], gq_ref[...])
            acc_q = acc_q + part
        for h in range(N_KV_A):
            o_ref[:, _cols(COL_KA + h)], part = head(dka_ref[:, _cols(h)] * LN2, x_ref[:, _cols(COL_KA + h)],
                                                     gk_ref[...])
            acc_k = acc_k + part
        o_ref[:, _cols(COL_VA, 2)] = dva_ref[...].astype(BF16)
        o_ref[:, _cols(COL_QB, N_HEADS_B)] = (dqb_ref[...] * ATT_SCALE).astype(BF16)
        o_ref[:, _cols(COL_KB, 2)] = (dkb_ref[...] * LN2).astype(BF16)
        o_ref[:, _cols(COL_VB, 2)] = dvb_ref[...].astype(BF16)

        @pl.when(i == 0)
        def _():
            dgq_ref[...] = acc_q
            dgk_ref[...] = acc_k

        @pl.when(i > 0)
        def _():
            dgq_ref[...] += acc_q
            dgk_ref[...] += acc_k

    def rows(width, shift=0):
        return pl.BlockSpec((tm, width), lambda i: (i + shift, 0))

    kv_w = N_KV_A * HEAD_DIM
    q_w = N_HEADS_A * HEAD_DIM
    vec = pl.BlockSpec((1, HEAD_DIM), lambda i: (0, 0))
    res, c_res = _call(
        body, name="qk_bwd", grid=(s // tm,),
        in_specs=[rows(q_w), rows(kv_w), rows(kv_w), rows(q_w), rows(kv_w, lo), rows(kv_w, lo), rows(n),
                  vec, vec, rows(HEAD_DIM), rows(HEAD_DIM)],
        out_specs=[rows(n), vec, vec],
        out_shape=[
            jax.ShapeDtypeStruct((s, n), BF16),
            jax.ShapeDtypeStruct((1, HEAD_DIM), F32),
            jax.ShapeDtypeStruct((1, HEAD_DIM), F32),
        ],
        sem=("arbitrary",), args=(dqa, dka, dva, dqb, dkpad, dvpad, proj, g_q, g_k, cos_t, sin_t), comm=comm)
    return res if comm is None else (res, c_res)


_NT = (((1,), (1,)), ((), ()))
_TN = (((0,), (0,)), ((), ()))


def _attn_a_fwd(pb, *, tq=512, sub=256, comm=None):
    s = pb.shape[0]
    tq = min(tq, s)

    sub = min(sub, tq)

    def body(q_ref, k_ref, v_ref, o_ref, lse_ref):
        k = k_ref[...]
        v = v_ref[...]
        for r in range(tq // sub):
            rows = pl.ds(r * sub, sub)
            sc = lax.dot_general(q_ref[rows, :], k, _NT, preferred_element_type=F32)
            m = jnp.max(sc, axis=-1, keepdims=True)
            p = jnp.exp2(sc - m)
            l = jnp.sum(p, axis=-1, keepdims=True)
            o = jnp.dot(p.astype(BF16), v, preferred_element_type=F32)
            o_ref[rows, :] = (o / l).astype(BF16)
            lse_ref[rows, :] = jnp.broadcast_to(m + jnp.log2(l), (sub, HEAD_DIM))

    res, c_res = _call(
        body, name="attn_a_fwd", grid=(N_HEADS_A, s // tq),
        in_specs=[
            pl.BlockSpec((tq, HEAD_DIM), lambda h, i: (i, COL_QA + h)),
            pl.BlockSpec((s, HEAD_DIM), lambda h, i: (0, COL_KA + h // GROUP)),
            pl.BlockSpec((s, HEAD_DIM), lambda h, i: (0, COL_VA + h // GROUP)),
        ],
        out_specs=[
            pl.BlockSpec((tq, HEAD_DIM), lambda h, i: (i, h)),
            pl.BlockSpec((None, tq, HEAD_DIM), lambda h, i: (h, i, 0)),
        ],
        out_shape=[
            jax.ShapeDtypeStruct((s, N_HEADS_A * HEAD_DIM), BF16),
            jax.ShapeDtypeStruct((N_HEADS_A, s, HEAD_DIM), F32),
        ],
        sem=("parallel", "parallel"), args=(pb, pb, pb), comm=comm)
    return res if comm is None else (res, c_res)


def _attn_a_bwd(pb, att, datt, lse, *, tq=256, sub=128, comm=None):
    s = pb.shape[0]
    tq = min(tq, s)
    sub = min(sub, tq)

    def body(q_ref, k_ref, v_ref, o_ref, do_ref, lse_ref, dq_ref, dk_ref, dv_ref):
        first = jnp.logical_and(pl.program_id(1) == 0, pl.program_id(2) == 0)
        k = k_ref[...]
        v = v_ref[...]
        dk = dv = None
        for r in range(tq // sub):
            rows = pl.ds(r * sub, sub)
            q = q_ref[rows, :]
            do = do_ref[rows, :]
            sc = lax.dot_general(q, k, _NT, preferred_element_type=F32)
            p = jnp.exp2(sc - lse_ref[rows, :][:, :1])
            dp = lax.dot_general(do, v, _NT, preferred_element_type=F32)
            delta = jnp.sum(do.astype(F32) * o_ref[rows, :].astype(F32), axis=-1, keepdims=True)
            ds = (p * (dp - delta)).astype(BF16)
            dq_ref[rows, :] = jnp.dot(ds, k, preferred_element_type=F32)
            dk_r = lax.dot_general(ds, q, _TN, preferred_element_type=F32)
            dv_r = lax.dot_general(p.astype(BF16), do, _TN, preferred_element_type=F32)
            dk = dk_r if dk is None else dk + dk_r
            dv = dv_r if dv is None else dv + dv_r

        @pl.when(first)
        def _():
            dk_ref[...] = dk
            dv_ref[...] = dv

        @pl.when(jnp.logical_not(first))
        def _():
            dk_ref[...] += dk
            dv_ref[...] += dv

    qmap = lambda kv, g, i: (i, kv * GROUP + g)
    res, c_res = _call(
        body, name="attn_a_bwd", grid=(N_KV_A, GROUP, s // tq),
        in_specs=[
            pl.BlockSpec((tq, HEAD_DIM), lambda kv, g, i: (i, COL_QA + kv * GROUP + g)),
            pl.BlockSpec((s, HEAD_DIM), lambda kv, g, i: (0, COL_KA + kv)),
            pl.BlockSpec((s, HEAD_DIM), lambda kv, g, i: (0, COL_VA + kv)),
            pl.BlockSpec((tq, HEAD_DIM), qmap),
            pl.BlockSpec((tq, HEAD_DIM), qmap),
            pl.BlockSpec((None, tq, HEAD_DIM), lambda kv, g, i: (kv * GROUP + g, i, 0)),
        ],
        out_specs=[
            pl.BlockSpec((tq, HEAD_DIM), qmap),
            pl.BlockSpec((s, HEAD_DIM), lambda kv, g, i: (0, kv)),
            pl.BlockSpec((s, HEAD_DIM), lambda kv, g, i: (0, kv)),
        ],
        out_shape=[
            jax.ShapeDtypeStruct((s, N_HEADS_A * HEAD_DIM), F32),
            jax.ShapeDtypeStruct((s, N_KV_A * HEAD_DIM), F32),
            jax.ShapeDtypeStruct((s, N_KV_A * HEAD_DIM), F32),
        ],
        sem=("arbitrary", "arbitrary", "arbitrary"), args=(pb, pb, pb, att, datt, lse), comm=comm)
    return res if comm is None else (res, c_res)


def _t5_bucket(rel):
    nb = N_BUCKETS // 2
    ret = jnp.where(rel > 0, nb, 0)
    n = jnp.abs(rel)
    max_exact = nb // 2
    nf = jnp.maximum(n, 1).astype(F32)
    large = max_exact + (jnp.log(nf / max_exact) / math.log(MAX_DISTANCE / max_exact)
                         * (nb - max_exact)).astype(jnp.int32)
    large = jnp.minimum(large, nb - 1)
    return ret + jnp.where(n < max_exact, n, large)


def _band_buckets():
    r = jnp.arange(BLOCK_Q, dtype=jnp.int32)
    j = jnp.arange(3 * BLOCK_Q, dtype=jnp.int32)
    return _t5_bucket((j[None, :] - BLOCK_Q) - r[:, None])


def _band_bias(bucket, table_ref, h):
    acc = jnp.zeros(bucket.shape, F32)
    for b in range(N_BUCKETS):
        acc = jnp.where(bucket == b, table_ref[b, h], acc)
    return acc


GQ = GROUP * BLOCK_Q


def _stack_heads(x):
    return jnp.concatenate([x[:, _cols(g)] for g in range(GROUP)], axis=0)


def _unstack_heads(x):
    return jnp.concatenate([x[g * BLOCK_Q:(g + 1) * BLOCK_Q] for g in range(GROUP)], axis=1)


def _group_bias(bucket, table_ref, kv):
    return jnp.concatenate([_band_bias(bucket, table_ref, kv * GROUP + g) * LOG2E for g in range(GROUP)], axis=0)


def _group_sink(sink_ref, kv):
    head = lax.broadcasted_iota(jnp.int32, (GQ, 1), 0) // BLOCK_Q
    snk = jnp.zeros((GQ, 1), F32)
    for g in range(GROUP):
        snk = jnp.where(head == g, sink_ref[0, kv * GROUP + g] * LOG2E, snk)
    return snk


def _band_mask(n, s):
    r = lax.broadcasted_iota(jnp.int32, (GQ, 3 * BLOCK_Q), 0) % BLOCK_Q
    j = lax.broadcasted_iota(jnp.int32, (GQ, 3 * BLOCK_Q), 1)
    rel = j - BLOCK_Q - r
    kabs = n * BLOCK_Q + j - BLOCK_Q
    return (jnp.abs(rel) <= WINDOW) & (kabs >= 0) & (kabs < s)


def _band_start(n):
    return pl.multiple_of(n * BLOCK_Q + (PAD_LO - BLOCK_Q), BLOCK_Q)


def _attn_b_fwd(pb, kpad, vpad, bucket, table, sink, *, comm=None):
    s = pb.shape[0]
    nblk = s // BLOCK_Q
    sp = kpad.shape[0]

    def body(table_ref, sink_ref, q_ref, k_ref, v_ref, bucket_ref, o_ref, lse_ref, bias_ref):
        kv = pl.program_id(0)
        n = pl.program_id(1)

        @pl.when(n == 0)
        def _():
            bias_ref[...] = _group_bias(bucket_ref[...], table_ref, kv)

        band = pl.ds(_band_start(n), 3 * BLOCK_Q)
        kb = k_ref[band, :]
        vb = v_ref[band, :]
        sc = lax.dot_general(_stack_heads(q_ref[...]), kb, _NT, preferred_element_type=F32) + bias_ref[...]
        sc = jnp.where(_band_mask(n, s), sc, NEG_INF)
        snk = _group_sink(sink_ref, kv)
        m = jnp.maximum(jnp.max(sc, axis=-1, keepdims=True), snk)
        p = jnp.exp2(sc - m)
        l = jnp.sum(p, axis=-1, keepdims=True) + jnp.exp2(snk - m)
        o = jnp.dot(p.astype(BF16), vb, preferred_element_type=F32)
        o_ref[...] = _unstack_heads((o / l).astype(BF16))
        lse = m + jnp.log2(l)
        for g in range(GROUP):
            lse_ref[g] = jnp.broadcast_to(lse[g * BLOCK_Q:(g + 1) * BLOCK_Q], (BLOCK_Q, HEAD_DIM))

    smem = pl.BlockSpec(memory_space=pltpu.SMEM)
    res, c_res = _call(
        body, name="attn_b_fwd", grid=(N_KV_B, nblk),
        in_specs=[
            smem,
            smem,
            pl.BlockSpec((BLOCK_Q, GROUP * HEAD_DIM), lambda kv, n: (n, COL_QB // GROUP + kv)),
            pl.BlockSpec((sp, HEAD_DIM), lambda kv, n: (0, kv)),
            pl.BlockSpec((sp, HEAD_DIM), lambda kv, n: (0, kv)),
            pl.BlockSpec((BLOCK_Q, 3 * BLOCK_Q), lambda kv, n: (0, 0)),
        ],
        out_specs=[
            pl.BlockSpec((BLOCK_Q, GROUP * HEAD_DIM), lambda kv, n: (n, kv)),
            pl.BlockSpec((GROUP, BLOCK_Q, HEAD_DIM), lambda kv, n: (kv, n, 0)),
        ],
        out_shape=[
            jax.ShapeDtypeStruct((s, N_HEADS_B * HEAD_DIM), BF16),
            jax.ShapeDtypeStruct((N_HEADS_B, s, HEAD_DIM), F32),
        ],
        scratch_shapes=[pltpu.VMEM((GQ, 3 * BLOCK_Q), F32)],
        sem=("arbitrary", "arbitrary"), args=(table, sink, pb, kpad, vpad, bucket), comm=comm)
    return res if comm is None else (res, c_res)


def _attn_b_bwd(pb, kpad, vpad, att, datt, lse, bucket, table, sink, *, comm=None):
    s = pb.shape[0]
    nblk = s // BLOCK_Q
    sp = kpad.shape[0]

    def body(table_ref, sink_ref, q_ref, k_ref, v_ref, o_ref, do_ref, lse_ref, bucket_ref,
             dq_ref, dk_ref, dv_ref, dtab_ref, dsink_ref, bias_ref, dbias_ref):
        kv = pl.program_id(0)
        n = pl.program_id(1)

        @pl.when(n == 0)
        def _():
            dk_ref[...] = jnp.zeros_like(dk_ref)
            dv_ref[...] = jnp.zeros_like(dv_ref)
            bias_ref[...] = _group_bias(bucket_ref[...], table_ref, kv)
            dbias_ref[...] = jnp.zeros_like(dbias_ref)
            dsink_ref[...] = jnp.zeros_like(dsink_ref)

        band = pl.ds(_band_start(n), 3 * BLOCK_Q)
        q = _stack_heads(q_ref[...])
        do = _stack_heads(do_ref[...])
        o = _stack_heads(o_ref[...])
        kb = k_ref[band, :]
        vb = v_ref[band, :]
        lse = jnp.concatenate([lse_ref[g][:, :1] for g in range(GROUP)], axis=0)
        sc = lax.dot_general(q, kb, _NT, preferred_element_type=F32) + bias_ref[...]
        sc = jnp.where(_band_mask(n, s), sc, NEG_INF)
        p = jnp.exp2(sc - lse)
        dp = lax.dot_general(do, vb, _NT, preferred_element_type=F32)
        delta = jnp.sum(do.astype(F32) * o.astype(F32), axis=-1, keepdims=True)
        ds = p * (dp - delta)
        dsb = ds.astype(BF16)
        dq_ref[...] = _unstack_heads(jnp.dot(dsb, kb, preferred_element_type=F32))
        dk_ref[band, :] += lax.dot_general(dsb, q, _TN, preferred_element_type=F32)
        dv_ref[band, :] += lax.dot_general(p.astype(BF16), do, _TN, preferred_element_type=F32)
        dbias_ref[...] += ds
        sink_part = -jnp.exp2(_group_sink(sink_ref, kv) - lse) * delta
        for g in range(GROUP):
            rows = slice(g * BLOCK_Q, (g + 1) * BLOCK_Q)
            dsink_ref[g] += jnp.broadcast_to(jnp.sum(sink_part[rows], axis=0, keepdims=True), (1, HEAD_DIM))

        @pl.when(n == nblk - 1)
        def _():
            bucket_v = bucket_ref[...]
            row = lax.broadcasted_iota(jnp.int32, (N_BUCKETS, HEAD_DIM), 0)
            for g in range(GROUP):
                acc = dbias_ref[g * BLOCK_Q:(g + 1) * BLOCK_Q, :]
                tot = jnp.zeros((N_BUCKETS, HEAD_DIM), F32)
                for b in range(N_BUCKETS):
                    tot = jnp.where(row == b, jnp.sum(jnp.where(bucket_v == b, acc, 0.0), keepdims=True), tot)
                dtab_ref[g] = tot

    smem = pl.BlockSpec(memory_space=pltpu.SMEM)
    wide = GROUP * HEAD_DIM
    res, c_res = _call(
        body, name="attn_b_bwd", grid=(N_KV_B, nblk),
        in_specs=[
            smem,
            smem,
            pl.BlockSpec((BLOCK_Q, wide), lambda kv, n: (n, COL_QB // GROUP + kv)),
            pl.BlockSpec((sp, HEAD_DIM), lambda kv, n: (0, kv)),
            pl.BlockSpec((sp, HEAD_DIM), lambda kv, n: (0, kv)),
            pl.BlockSpec((BLOCK_Q, wide), lambda kv, n: (n, N_HEADS_A // GROUP + kv)),
            pl.BlockSpec((BLOCK_Q, wide), lambda kv, n: (n, N_HEADS_A // GROUP + kv)),
            pl.BlockSpec((GROUP, BLOCK_Q, HEAD_DIM), lambda kv, n: (kv, n, 0)),
            pl.BlockSpec((BLOCK_Q, 3 * BLOCK_Q), lambda kv, n: (0, 0)),
        ],
        out_specs=[
            pl.BlockSpec((BLOCK_Q, wide), lambda kv, n: (n, kv)),
            pl.BlockSpec((sp, HEAD_DIM), lambda kv, n: (0, kv)),
            pl.BlockSpec((sp, HEAD_DIM), lambda kv, n: (0, kv)),
            pl.BlockSpec((GROUP, N_BUCKETS, HEAD_DIM), lambda kv, n: (kv, 0, 0)),
            pl.BlockSpec((GROUP, 1, HEAD_DIM), lambda kv, n: (kv, 0, 0)),
        ],
        out_shape=[
            jax.ShapeDtypeStruct((s, N_HEADS_B * HEAD_DIM), F32),
            jax.ShapeDtypeStruct((sp, N_KV_B * HEAD_DIM), F32),
            jax.ShapeDtypeStruct((sp, N_KV_B * HEAD_DIM), F32),
            jax.ShapeDtypeStruct((N_HEADS_B, N_BUCKETS, HEAD_DIM), F32),
            jax.ShapeDtypeStruct((N_HEADS_B, 1, HEAD_DIM), F32),
        ],
        scratch_shapes=[pltpu.VMEM((GQ, 3 * BLOCK_Q), F32), pltpu.VMEM((GQ, 3 * BLOCK_Q), F32)],
        sem=("arbitrary", "arbitrary"),
        args=(table, sink, pb, kpad, vpad, att, datt, lse, bucket), comm=comm)
    return res if comm is None else (res, c_res)


_MESH = pl.DeviceIdType.MESH


def _other_chips(x, y):
    return [(x, 1 - y), (1 - x, y), (1 - x, 1 - y)]


_HBM = pl.BlockSpec(memory_space=pltpu.HBM)
_SEM = pl.BlockSpec(memory_space=pltpu.SEMAPHORE)
_SPLIT = pltpu.CompilerParams(has_side_effects=pltpu.SideEffectType.DATAFLOW_SIDE_EFFECTING)


def _in_hbm(a):
    return pltpu.with_memory_space_constraint(a, pltpu.HBM)


def _my_half(rows):
    c = lax.axis_index("c")
    half = rows // 2
    return pl.ds(pl.multiple_of(c * half, half), half), pl.ds(pl.multiple_of((1 - c) * half, half), half)


def _gather_start(shards, groups):
    n_t, n_g = len(shards), len(groups)

    def body(*refs):
        src, land = refs[:n_t], refs[n_t:2 * n_t]
        sems = refs[2 * n_t:2 * n_t + 2 * n_g]
        token = refs[-1]
        x, y, c = lax.axis_index("x"), lax.axis_index("y"), lax.axis_index("c")
        me = 2 * x + y
        for gi, group in enumerate(groups):
            i = 0
            for t in group:
                mine, _ = _my_half(shards[t].shape[0])
                for px, py in _other_chips(x, y):
                    pltpu.make_async_remote_copy(
                        src_ref=src[t].at[mine], dst_ref=land[t].at[me, mine], send_sem=sems[2 * gi].at[i],
                        recv_sem=sems[2 * gi + 1].at[i], device_id=(px, py, c), device_id_type=_MESH).start()
                    i += 1
        token[...] = jnp.zeros_like(token)

    sem_shapes = []
    for group in groups:
        sem_shapes += [pltpu.SemaphoreType.DMA((3 * len(group),)), pltpu.SemaphoreType.DMA((3 * len(group),))]
    land_shapes = [(N_CHIPS,) + w.shape for w in shards]
    res = pl.pallas_call(
        body, name="gather_start",
        in_specs=[_HBM] * (2 * n_t),
        out_specs=[_SEM] * (2 * n_g) + [_HBM] * (2 * n_t) + [pl.BlockSpec(memory_space=pltpu.VMEM)],
        out_shape=sem_shapes + [pltpu.HBM(w.shape, w.dtype) for w in shards]
        + [pltpu.HBM(sh, w.dtype) for sh, w in zip(land_shapes, shards)] + [jax.ShapeDtypeStruct((8, 128), F32)],
        input_output_aliases={i: 2 * n_g + i for i in range(2 * n_t)},
        compiler_params=_SPLIT,
    )(*[_in_hbm(w) for w in shards], *[_in_hbm(lax.empty(sh, w.dtype)) for sh, w in zip(land_shapes, shards)])
    sems = [(res[2 * gi], res[2 * gi + 1]) for gi in range(n_g)]
    return sems, res[2 * n_g:2 * n_g + n_t], res[2 * n_g + n_t:2 * n_g + 2 * n_t], res[-1]


def _gather_wait(name, srcs, lands, sems, after):
    n = len(srcs)

    def body(*refs):
        src, land, send_sems, recv_sems = refs[:n], refs[n:2 * n], refs[2 * n], refs[2 * n + 1]
        x, y, c = lax.axis_index("x"), lax.axis_index("y"), lax.axis_index("c")
        i = 0
        for t in range(n):
            mine, _ = _my_half(srcs[t].shape[0])
            for px, py in _other_chips(x, y):
                cp = pltpu.make_async_remote_copy(
                    src_ref=src[t].at[mine], dst_ref=land[t].at[2 * px + py, mine], send_sem=send_sems.at[i],
                    recv_sem=recv_sems.at[i], device_id=(px, py, c), device_id_type=_MESH)
                cp.wait_send()
                cp.wait_recv()
                i += 1

    res = pl.pallas_call(
        body, name=name,
        in_specs=[_HBM] * (2 * n) + [_SEM, _SEM, _ANY],
        out_specs=[_HBM] * (2 * n),
        out_shape=[pltpu.HBM(a.shape, a.dtype) for a in list(srcs) + list(lands)],
        input_output_aliases={i: i for i in range(2 * n)},
        compiler_params=_SPLIT,
    )(*srcs, *lands, sems[0], sems[1], after)
    return res[:n], res[n:]


def _swap_comm(shards, lands):
    n_t = len(shards)

    def copies(ins, land, sems, later):
        send_sems, recv_sems, local_sems = sems
        x, y, c = lax.axis_index("x"), lax.axis_index("y"), lax.axis_index("c")
        sibling = (x, y, 1 - c)
        local, sends, recvs = [], [], []
        for t in range(n_t):
            mine, other = _my_half(shards[t].shape[0])
            local.append(pltpu.make_async_copy(ins[t], land[t].at[2 * x + y], local_sems.at[t]))
            for j, (px, py) in enumerate(_other_chips(x, y)):
                k = 2 * px + py
                for part, out in ((mine, sends), (other, recvs)) if later else ((mine, sends),):
                    out.append(pltpu.make_async_remote_copy(
                        src_ref=land[t].at[k, part], dst_ref=land[t].at[k, part], send_sem=send_sems.at[3 * t + j],
                        recv_sem=recv_sems.at[3 * t + j], device_id=sibling, device_id_type=_MESH))
        return local, sends, recvs

    def start(ins, land, sems):
        local, sends, _ = copies(ins, land, sems, False)
        for cp in local + sends:
            cp.start()

    def finish(ins, land, sems):
        local, sends, recvs = copies(ins, land, sems, True)
        for cp in recvs:
            cp.wait_recv()
        for cp in sends:
            cp.wait_send()
        for cp in local:
            cp.wait()

    return _Comm(
        list(shards) + list(lands), [jax.ShapeDtypeStruct(a.shape, a.dtype) for a in lands],
        [pltpu.SemaphoreType.DMA((3 * n_t,)), pltpu.SemaphoreType.DMA((3 * n_t,)), pltpu.SemaphoreType.DMA((n_t,))],
        start, finish, aliases={n_t + t: t for t in range(n_t)})


def _exchange_comm(partials):
    n_t = len(partials)

    def copies(src, dst, sems, later):
        send_sems, recv_sems, local_sems = sems
        x, y, c = lax.axis_index("x"), lax.axis_index("y"), lax.axis_index("c")
        me = 2 * x + y
        sibling = (x, y, 1 - c)
        local, first, landed, passed, from_sibling = [], [], [], [], []
        for t in range(n_t):

            def copy(k, src_ref, slot, to, t=t):
                return pltpu.make_async_remote_copy(
                    src_ref=src_ref, dst_ref=dst[t].at[slot], send_sem=send_sems.at[7 * t + k],
                    recv_sem=recv_sems.at[7 * t + k], device_id=to, device_id_type=_MESH)

            local.append(pltpu.make_async_copy(src[t].at[me], dst[t].at[2 * me + c], local_sems.at[t]))
            for j, (px, py) in enumerate(_other_chips(x, y)):
                k = 2 * px + py
                first.append(copy(1 + j, src[t].at[k], 2 * me + c, (px, py, c)))
                if later:
                    landed.append(copy(1 + j, dst[t].at[2 * k + c], 2 * k + c, (px, py, c)))
                    passed.append(copy(4 + j, dst[t].at[2 * k + c], 2 * k + c, sibling))
                    from_sibling.append(copy(4 + j, dst[t].at[2 * k + 1 - c], 2 * k + 1 - c, sibling))
            first.append(copy(0, src[t].at[me], 2 * me + c, sibling))
            if later:
                from_sibling.append(copy(0, src[t].at[me], 2 * me + 1 - c, sibling))
        return local, first, landed, passed, from_sibling

    def start(src, dst, sems):
        local, first, _, _, _ = copies(src, dst, sems, False)
        for cp in local + first:
            cp.start()

    def finish(src, dst, sems):
        local, first, landed, passed, from_sibling = copies(src, dst, sems, True)
        for got, fwd in zip(landed, passed):
            got.wait_recv()
            fwd.start()
        for cp in from_sibling:
            cp.wait_recv()
        for cp in first + passed:
            cp.wait_send()
        for cp in local:
            cp.wait()

    return _Comm(
        partials, [jax.ShapeDtypeStruct((N_DEV,) + g.shape[1:], g.dtype) for g in partials],
        [pltpu.SemaphoreType.DMA((7 * n_t,)), pltpu.SemaphoreType.DMA((7 * n_t,)), pltpu.SemaphoreType.DMA((n_t,))],
        start, finish)


def _allreduce_small(pack):
    rows, d = pack.shape

    def body(p_ref, sum_ref, all_ref, send_sems, recv_sems):
        x, y, c = lax.axis_index("x"), lax.axis_index("y"), lax.axis_index("c")
        me = 4 * x + 2 * y + c
        all_ref[me] = p_ref[...]
        peers = []
        for dx in range(2):
            for dy in range(2):
                for dc in range(2):
                    if dx or dy or dc:
                        px = 1 - x if dx else x
                        py = 1 - y if dy else y
                        pc = 1 - c if dc else c
                        peers.append((4 * dx + 2 * dy + dc - 1, (px, py, pc)))
        sends = []
        for k, to in peers:
            cp = pltpu.make_async_remote_copy(
                src_ref=p_ref, dst_ref=all_ref.at[me], send_sem=send_sems.at[k], recv_sem=recv_sems.at[k],
                device_id=to, device_id_type=_MESH)
            cp.start()
            sends.append(cp)
        for k, (px, py, pc) in peers:
            pltpu.make_async_remote_copy(
                src_ref=p_ref, dst_ref=all_ref.at[4 * px + 2 * py + pc], send_sem=send_sems.at[k],
                recv_sem=recv_sems.at[k], device_id=(px, py, pc), device_id_type=_MESH).wait_recv()
        for cp in sends:
            cp.wait_send()
        tot = all_ref[0]
        for i in range(1, N_DEV):
            tot = tot + all_ref[i]
        sum_ref[...] = tot

    vm = pl.BlockSpec(memory_space=pltpu.VMEM)
    return pl.pallas_call(
        body,
        name="allreduce_small",
        in_specs=[vm],
        out_specs=vm,
        out_shape=jax.ShapeDtypeStruct((rows, d), F32),
        scratch_shapes=[
            pltpu.VMEM((N_DEV, rows, d), F32),
            pltpu.SemaphoreType.DMA((N_DEV - 1,)),
            pltpu.SemaphoreType.DMA((N_DEV - 1,)),
        ],
    )(pack)


def _adamw_math(w, g, m, v):
    m = ADAM_B1 * m + (1.0 - ADAM_B1) * g
    v = ADAM_B2 * v + (1.0 - ADAM_B2) * (g * g)
    m_hat = m / (1.0 - ADAM_B1 ** ADAM_STEP)
    v_hat = v / (1.0 - ADAM_B2 ** ADAM_STEP)
    delta = -ADAM_LR * (m_hat / (jnp.sqrt(v_hat) + ADAM_EPS) + ADAM_WD * w)
    return delta, m, v


def _sum_adamw(parts, w, m, v, *, name, tr=256):
    r, c = w.shape
    tr = min(tr, r)
    tc = min(c, 1024)

    def body(p_ref, w_ref, m_ref, v_ref, g_ref, d_ref, m2_ref, v2_ref):
        g = p_ref[0].astype(F32)
        for i in range(1, N_DEV):
            g = g + p_ref[i].astype(F32)
        delta, m2, v2 = _adamw_math(w_ref[...], g, m_ref[...], v_ref[...])
        g_ref[...] = g
        d_ref[...] = delta
        m2_ref[...] = m2
        v2_ref[...] = v2

    blk = pl.BlockSpec((tr, tc), lambda i, j: (i, j))
    return pl.pallas_call(
        body,
        name=name,
        grid=(r // tr, c // tc),
        in_specs=[pl.BlockSpec((N_DEV, tr, tc), lambda i, j: (0, i, j)), blk, blk, blk],
        out_specs=[blk] * 4,
        out_shape=[jax.ShapeDtypeStruct((r, c), F32)] * 4,
        compiler_params=_params(("parallel", "parallel")),
    )(parts, w, m, v)


def _adamw_small(g, w, m, v):
    def body(g_ref, w_ref, m_ref, v_ref, d_ref, m2_ref, v2_ref):
        delta, m2, v2 = _adamw_math(w_ref[...], g_ref[...], m_ref[...], v_ref[...])
        d_ref[...] = delta
        m2_ref[...] = m2
        v2_ref[...] = v2

    vm = pl.BlockSpec(memory_space=pltpu.VMEM)
    return pl.pallas_call(
        body,
        name="adamw_small",
        in_specs=[vm] * 4,
        out_specs=[vm] * 3,
        out_shape=[jax.ShapeDtypeStruct(g.shape, F32)] * 3,
    )(g, w, m, v)


def _relu2_epilogue(acc):
    ra = jnp.maximum(acc, 0.0)
    return ra * ra, ra


def _rows(stacked):
    return stacked.reshape(stacked.shape[0] * stacked.shape[1], stacked.shape[2])


def _by_chip(mat):
    return mat.reshape(N_CHIPS, mat.shape[0] // N_CHIPS, mat.shape[1])


def _local_step(x, p, target, shards, small):
    s, d = x.shape
    cos_t, sin_t = _rope_tables(s)
    bucket = _band_buckets()
    p_bf = p.astype(BF16)
    wts = {}

    order = ["w_in", "w_out", "w_up", "w_down", "w_gate", "ple_w"]
    groups = [[0], [1], [2], [3], [4, 5]]
    sems, srcs, lands, token = _gather_start([shards[n] for n in order], groups)

    def landed(gi, after):
        got_src, got_land = _gather_wait(f"gather_wait_{gi}", [srcs[t] for t in groups[gi]],
                                         [lands[t] for t in groups[gi]], sems[gi], after)
        return _swap_comm(got_src, got_land)

    g_attn = small["attn_norm_g"] + token[:1, :1]
    u, (wts["w_in"],) = _rms_fwd(x, g_attn, name="norm_attn", comm=landed(0, token))
    proj = _matmul(u, wts["w_in"], mode="nn", out_dtypes=[F32], name="mm_in", bn=768)
    pb, (w_out_s,) = _qk_prep(proj, small["q_norm_g"], small["k_norm_g"], cos_t, sin_t, comm=landed(1, proj))
    wts["w_out"] = _rows(w_out_s)
    oa, lse_a = _attn_a_fwd(pb)
    pad = ((PAD_LO, PAD_HI), (0, 0))
    kpad = jnp.pad(pb[:, COL_KB * HEAD_DIM:COL_VB * HEAD_DIM], pad)
    vpad = jnp.pad(pb[:, COL_VB * HEAD_DIM:], pad)
    (ob, lse_b), (wts["w_up"],) = _attn_b_fwd(pb, kpad, vpad, bucket, small["rel_bias_table"],
                                              small["sink_logits"], comm=landed(2, oa))
    att = jnp.concatenate([oa, ob], axis=-1)
    h1 = _matmul(att, wts["w_out"], mode="nn", out_dtypes=[F32], name="mm_out",
                 epilogue=lambda acc, res: (acc + res,), extras=(x,))
    mn = _rms_fwd(h1, small["mlp_norm_g"], name="norm_mlp")
    (r, ra), (w_down_s,) = _matmul(mn, wts["w_up"], mode="nn", out_dtypes=[BF16, BF16], name="mm_up",
                                   epilogue=_relu2_epilogue, comm=landed(3, h1))
    wts["w_down"] = _rows(w_down_s)
    h2, (w_gate_s, wts["ple_w"]) = _matmul(r, wts["w_down"], mode="nn", out_dtypes=[F32], name="mm_down",
                                           epilogue=lambda acc, res: (acc + res,), extras=(h1,),
                                           comm=landed(4, mn))
    wts["w_gate"] = _rows(w_gate_s)
    ng = _rms_fwd(h2, small["gate_norm_g"], name="norm_gate")
    gate = _matmul(ng, wts["w_gate"], mode="nn", out_dtypes=[F32], name="mm_gate",
                   epilogue=lambda acc: (1.0 / (1.0 + jnp.exp(-acc)),))
    pp = _matmul(p_bf, wts["ple_w"], mode="nn", out_dtypes=[F32], name="mm_ple", bn=512)
    dh3, dz, dpp, dg_final, dg_ple, loss = _tail(h2, gate, pp, target, small["ple_norm_g"], small["final_norm_g"])

    dng = _matmul(dz, wts["w_gate"], mode="nt", out_dtypes=[F32], name="mm_gate_dx")
    gw_gate = _matmul(ng, dz, mode="tn", out_dtypes=[BF16], name="mm_gate_dw")
    gw_ple = _matmul(p_bf, dpp, mode="tn", out_dtypes=[BF16], name="mm_ple_dw", bn=512, out_stack=N_CHIPS)
    dh2, dh2_bf, dg_gate = _rms_bwd(h2, dng, small["gate_norm_g"], dh3, name="norm_gate_bwd", want_bf16=True)
    big = {}
    da, (big["w_gate"], big["ple_w"]) = _matmul(
        dh2_bf, wts["w_down"], mode="nt", out_dtypes=[BF16], name="mm_down_dx",
        epilogue=lambda acc, ra_v: (acc * (2.0 * ra_v.astype(F32)),), extras=(ra,),
        comm=_exchange_comm([_by_chip(gw_gate), gw_ple]))
    gw_down = _matmul(r, dh2_bf, mode="tn", out_dtypes=[BF16], name="mm_down_dw")
    gw_up = _matmul(mn, da, mode="tn", out_dtypes=[BF16], name="mm_up_dw", out_stack=N_CHIPS)
    dmn = _matmul(da, wts["w_up"], mode="nt", out_dtypes=[F32], name="mm_up_dx")
    dh1, dh1_bf, dg_mlp = _rms_bwd(h1, dmn, small["mlp_norm_g"], dh2, name="norm_mlp_bwd", want_bf16=True)
    datt = _matmul(dh1_bf, wts["w_out"], mode="nt", out_dtypes=[BF16], name="mm_out_dx")
    gw_out = _matmul(att, dh1_bf, mode="tn", out_dtypes=[BF16], name="mm_out_dw")
    (dqb, dkpad, dvpad, dtab, dsink), (big["w_up"],) = _attn_b_bwd(
        pb, kpad, vpad, att, datt, lse_b, bucket, small["rel_bias_table"], small["sink_logits"],
        comm=_exchange_comm([gw_up]))
    (dqa, dka, dva), (big["w_down"],) = _attn_a_bwd(pb, att, datt, lse_a, comm=_exchange_comm([_by_chip(gw_down)]))
    (dproj, dg_q, dg_k), (big["w_out"],) = _qk_bwd(dqa, dka, dva, dqb, dkpad, dvpad, proj,
                                                   small["q_norm_g"], small["k_norm_g"], cos_t, sin_t,
                                                   comm=_exchange_comm([_by_chip(gw_out)]))
    gw_in = _matmul(u, dproj, mode="tn", out_dtypes=[BF16], name="mm_in_dw", bn=768, out_stack=N_CHIPS)
    du, (big["w_in"],) = _matmul(dproj, wts["w_in"], mode="nt", out_dtypes=[F32], name="mm_in_dx", bk=768,
                                 comm=_exchange_comm([gw_in]))
    grad_x, dg_attn = _rms_bwd(x, du, small["attn_norm_g"], dh1, name="norm_attn_bwd", want_bf16=False)

    small_g = {
        "attn_norm_g": dg_attn, "mlp_norm_g": dg_mlp, "ple_norm_g": dg_ple, "gate_norm_g": dg_gate,
        "final_norm_g": dg_final, "q_norm_g": dg_q, "k_norm_g": dg_k,
        "sink_logits": dsink[:, 0, 0][None, :], "rel_bias_table": dtab[:, :, 0].T,
    }
    return loss, grad_x, big, small_g


_SMALL_ROWS = ["attn_norm_g", "mlp_norm_g", "ple_norm_g", "gate_norm_g", "final_norm_g"]
_PACK_ROWS = 8


def _pack_small(vals, d):
    rows = [vals[n].reshape(1, d) for n in _SMALL_ROWS]
    misc = jnp.concatenate([
        vals["q_norm_g"].reshape(1, HEAD_DIM), vals["k_norm_g"].reshape(1, HEAD_DIM),
        jnp.pad(vals["sink_logits"].reshape(1, N_HEADS_B), ((0, 0), (0, HEAD_DIM - N_HEADS_B))),
        vals["rel_bias_table"].reshape(1, N_BUCKETS * N_HEADS_B)], axis=1)
    rows.append(jnp.pad(misc, ((0, 0), (0, d - misc.shape[1]))))
    rows.append(jnp.zeros((_PACK_ROWS - len(rows), d), F32))
    return jnp.concatenate(rows, axis=0).astype(F32)


def _unpack_small(pack, shapes):
    out = {n: pack[i].reshape(shapes[n]) for i, n in enumerate(_SMALL_ROWS)}
    misc = pack[len(_SMALL_ROWS)]
    out["q_norm_g"] = misc[:HEAD_DIM].reshape(shapes["q_norm_g"])
    out["k_norm_g"] = misc[HEAD_DIM:2 * HEAD_DIM].reshape(shapes["k_norm_g"])
    out["sink_logits"] = misc[2 * HEAD_DIM:2 * HEAD_DIM + N_HEADS_B].reshape(shapes["sink_logits"])
    out["rel_bias_table"] = misc[3 * HEAD_DIM:3 * HEAD_DIM + N_BUCKETS * N_HEADS_B].reshape(shapes["rel_bias_table"])
    return out


_WEIGHTS = ["attn_norm_g", "w_in", "q_norm_g", "k_norm_g", "sink_logits", "w_out", "mlp_norm_g", "w_up", "w_down",
            "ple_w", "ple_norm_g", "gate_norm_g", "w_gate", "rel_bias_table", "final_norm_g"]
_BIG = ["w_in", "w_out", "w_up", "w_down", "ple_w", "w_gate"]


def kernel(x, p, attn_norm_g, w_in, q_norm_g, k_norm_g, sink_logits, w_out, mlp_norm_g, w_up, w_down, ple_w, ple_norm_g, gate_norm_g, w_gate, rel_bias_table, final_norm_g, loss_target, m_attn_norm_g, m_w_in, m_q_norm_g, m_k_norm_g, m_sink_logits, m_w_out, m_mlp_norm_g, m_w_up, m_w_down, m_ple_w, m_ple_norm_g, m_gate_norm_g, m_w_gate, m_rel_bias_table, m_final_norm_g, v_attn_norm_g, v_w_in, v_q_norm_g, v_k_norm_g, v_sink_logits, v_w_out, v_mlp_norm_g, v_w_up, v_w_down, v_ple_w, v_ple_norm_g, v_gate_norm_g, v_w_gate, v_rel_bias_table, v_final_norm_g):
    given = dict(locals())
    w = {n: given[n] for n in _WEIGHTS}
    m = {n: given["m_" + n] for n in _WEIGHTS}
    v = {n: given["v_" + n] for n in _WEIGHTS}
    d = x.shape[-1]

    shards = {n: w[n][0].astype(BF16) for n in _BIG}
    small = {
        "attn_norm_g": w["attn_norm_g"], "mlp_norm_g": w["mlp_norm_g"], "ple_norm_g": w["ple_norm_g"],
        "gate_norm_g": w["gate_norm_g"], "final_norm_g": w["final_norm_g"].reshape(1, d),
        "q_norm_g": w["q_norm_g"], "k_norm_g": w["k_norm_g"], "sink_logits": w["sink_logits"],
        "rel_bias_table": w["rel_bias_table"],
    }

    loss_part, grad_x, exchanged, small_g = _local_step(x[0], p[0, 0], loss_target[0], shards, small)
    loss = lax.psum(loss_part[0, 0], ("x", "y", "c"))

    grads, deltas, new_m, new_v = {}, {}, {}, {}
    for n in _BIG:
        shape = w[n].shape
        res = _sum_adamw(exchanged[n], w[n][0], m[n][0], v[n][0], name="adamw_" + n)
        grads[n], deltas[n], new_m[n], new_v[n] = [t.reshape(shape) for t in res]

    shapes = {n: w[n].shape for n in _WEIGHTS if n not in _BIG}
    g_small = _allreduce_small(_pack_small(small_g, d))
    d_small, m_small, v_small = _adamw_small(g_small, _pack_small(w, d), _pack_small(m, d), _pack_small(v, d))
    grads.update(_unpack_small(g_small, shapes))
    deltas.update(_unpack_small(d_small, shapes))
    new_m.update(_unpack_small(m_small, shapes))
    new_v.update(_unpack_small(v_small, shapes))

    return (loss, grad_x[None], *[grads[n] for n in _WEIGHTS], *[deltas[n] for n in _WEIGHTS],
            *[new_m[n] for n in _WEIGHTS], *[new_v[n] for n in _WEIGHTS])
```

```python
import functools
import math

import jax
import jax.numpy as jnp
import numpy as np
from jax import lax
from jax.experimental import pallas as pl
from jax.experimental.pallas import tpu as pltpu

F32 = jnp.float32
BF16 = jnp.bfloat16

HEAD_DIM = 128
N_HEADS_A = 8
N_KV_A = 2
N_HEADS_B = 8
N_KV_B = 2
GROUP = 4
GRID_W = 64
BLOCK_Q = 128
WINDOW = 128
N_BUCKETS = 32
MAX_DISTANCE = 128
ROPE_THETA = 10000.0
EPS = 1e-6
NEG_INF = -1e30
ATT_SCALE = HEAD_DIM ** -0.5
LOG2E = math.log2(math.e)
LN2 = math.log(2.0)
Q_SCALE = ATT_SCALE * LOG2E
PAD_LO, PAD_HI = 256, 128

ADAM_LR = 0.001
ADAM_B1 = 0.9
ADAM_B2 = 0.999
ADAM_EPS = 1e-08
ADAM_WD = 0.01
ADAM_STEP = 10

N_CHIPS = 4
N_DEV = 8
COL_QA, COL_KA, COL_VA, COL_QB, COL_KB, COL_VB = 0, 8, 10, 12, 20, 22
N_COLS = 24

VMEM_LIMIT = 52 * 1024 * 1024


def _params(sem=None, collective_id=None):
    return pltpu.CompilerParams(dimension_semantics=sem, vmem_limit_bytes=VMEM_LIMIT, collective_id=collective_id)


_ANY = pl.BlockSpec(memory_space=pl.ANY)
_MESH = pl.DeviceIdType.MESH
SIBLING_BARRIER_ID = 1


def _sibling():
    return (lax.axis_index("x"), lax.axis_index("y"), 1 - lax.axis_index("c"))


class _Comm:
    def __init__(self, inputs, out_shapes, sems, start, finish, aliases=None):
        self.inputs, self.out_shapes, self.sems = list(inputs), list(out_shapes), list(sems)
        self.start, self.finish, self.aliases = start, finish, dict(aliases or {})


def _call(body, *, name, grid, in_specs, out_specs, out_shape, args, scratch_shapes=(), sem=None, comm=None):
    in_specs, out_specs, out_shape = list(in_specs), list(out_specs), list(out_shape)
    scratch_shapes = list(scratch_shapes)
    if comm is None:
        res = pl.pallas_call(
            body, name=name, grid=grid, in_specs=in_specs, out_specs=out_specs, out_shape=out_shape,
            scratch_shapes=scratch_shapes, compiler_params=_params(sem))(*args)
        return list(res), []
    n_in, n_out, n_sc = len(in_specs), len(out_specs), len(scratch_shapes)
    c_in, c_out = len(comm.inputs), len(comm.out_shapes)

    def hosted(*refs):
        pos = [0]

        def take(n):
            pos[0] += n
            return refs[pos[0] - n:pos[0]]

        ins, c_ins, outs, c_outs, scr = take(n_in), take(c_in), take(n_out), take(c_out), take(n_sc)
        c_sems = refs[pos[0]:]
        ids = [pl.program_id(a) for a in range(len(grid))]
        first = functools.reduce(jnp.logical_and, [i == 0 for i in ids])
        last = functools.reduce(jnp.logical_and, [i == g - 1 for i, g in zip(ids, grid)])

        @pl.when(first)
        def _():
            barrier = pltpu.get_barrier_semaphore()
            pl.semaphore_signal(barrier, inc=1, device_id=_sibling(), device_id_type=_MESH)
            pl.semaphore_wait(barrier, 1)
            comm.start(c_ins, c_outs, c_sems)

        body(*ins, *outs, *scr)

        @pl.when(last)
        def _():
            comm.finish(c_ins, c_outs, c_sems)

    res = pl.pallas_call(
        hosted, name=name, grid=grid, in_specs=in_specs + [_ANY] * c_in, out_specs=out_specs + [_ANY] * c_out,
        out_shape=out_shape + comm.out_shapes, scratch_shapes=scratch_shapes + comm.sems,
        input_output_aliases={n_in + i: n_out + o for i, o in comm.aliases.items()},
        compiler_params=_params(("arbitrary",) * len(grid), SIBLING_BARRIER_ID))(*args, *comm.inputs)
    return list(res[:n_out]), list(res[n_out:])


def _matmul(a, b, *, mode, out_dtypes, name, epilogue=None, extras=(), bm=1024, bn=1024, bk=2048,
            out_stack=0, comm=None):
    stacked = b.ndim == 3
    if mode == "nn":
        m, k = a.shape
        if stacked:
            nj, kb, ns = b.shape
            n, ks = nj * ns, k
        else:
            kb, n = b.shape
            ns, ks = n, k
        dn = (((1,), (0,)), ((), ()))
    elif mode == "nt":
        m, k = a.shape
        if stacked:
            nj, n, ks = b.shape
            kb = nj * ks
        else:
            n, kb = b.shape
            ks = kb
        ns = n
        dn = (((1,), (1,)), ((), ()))
    else:
        k, m = a.shape
        kb, n = b.shape
        ns, ks = n, k
        dn = (((0,), (0,)), ((), ()))
    assert k == kb and not (stacked and mode == "tn")
    ns_out = n // out_stack if out_stack else n
    bm, bn, bk = min(bm, m), min(bn, ns, ns_out), min(bk, ks)
    assert m % bm == 0 and ns % bn == 0 and ns_out % bn == 0 and ks % bk == 0
    gm, gn, gk = m // bm, n // bn, k // bk

    if mode == "tn":
        a_spec = pl.BlockSpec((bk, bm), lambda i, j, q: (q, i))
    else:
        a_spec = pl.BlockSpec((bm, bk), lambda i, j, q: (i, q))
    if mode == "nt":
        if stacked:
            per = ks // bk
            b_spec = pl.BlockSpec((None, bn, bk), lambda i, j, q: (q // per, j, q % per))
        else:
            b_spec = pl.BlockSpec((bn, bk), lambda i, j, q: (j, q))
    else:
        if stacked:
            per = ns // bn
            b_spec = pl.BlockSpec((None, bk, bn), lambda i, j, q: (j // per, q, j % per))
        else:
            b_spec = pl.BlockSpec((bk, bn), lambda i, j, q: (q, j))
    ex_spec = pl.BlockSpec((bm, bn), lambda i, j, q: (i, j))
    if out_stack:
        per_o = ns_out // bn
        o_spec = pl.BlockSpec((None, bm, bn), lambda i, j, q: (j // per_o, i, j % per_o))
        o_shape = (out_stack, m, ns_out)
    else:
        o_spec = ex_spec
        o_shape = (m, n)
    n_ex, n_out = len(extras), len(out_dtypes)

    def body(a_ref, b_ref, *rest):
        ex, outs = rest[:n_ex], rest[n_ex:n_ex + n_out]
        part = lax.dot_general(a_ref[...], b_ref[...], dn, preferred_element_type=F32)

        def finish(acc):
            res = epilogue(acc, *[e[...] for e in ex]) if epilogue else (acc,)
            for o, r in zip(outs, res):
                o[...] = r.astype(o.dtype)

        if gk == 1:
            finish(part)
        else:
            acc_ref = rest[-1]
            q = pl.program_id(2)

            @pl.when(q == 0)
            def _():
                acc_ref[...] = part

            @pl.when(q > 0)
            def _():
                acc_ref[...] += part

            @pl.when(q == gk - 1)
            def _():
                finish(acc_ref[...])

    res, c_res = _call(
        body, name=name, grid=(gm, gn, gk),
        in_specs=[a_spec, b_spec] + [ex_spec] * n_ex,
        out_specs=[o_spec] * n_out,
        out_shape=[jax.ShapeDtypeStruct(o_shape, dt) for dt in out_dtypes],
        scratch_shapes=[pltpu.VMEM((bm, bn), F32)] if gk > 1 else [],
        sem=("parallel", "parallel", "arbitrary"), args=(a, b, *extras), comm=comm)
    res = res[0] if n_out == 1 else res
    return res if comm is None else (res, c_res)


def _rms_fwd(x, g, *, name, tm=256, comm=None):
    s, d = x.shape
    tm = min(tm, s)

    def body(x_ref, g_ref, o_ref):
        xf = x_ref[...]
        r = lax.rsqrt(jnp.mean(xf * xf, axis=-1, keepdims=True) + EPS)
        o_ref[...] = (xf * r * g_ref[...]).astype(o_ref.dtype)

    res, c_res = _call(
        body, name=name, grid=(s // tm,),
        in_specs=[pl.BlockSpec((tm, d), lambda i: (i, 0)), pl.BlockSpec((1, d), lambda i: (0, 0))],
        out_specs=[pl.BlockSpec((tm, d), lambda i: (i, 0))],
        out_shape=[jax.ShapeDtypeStruct((s, d), BF16)],
        sem=("parallel",), args=(x, g), comm=comm)
    return res[0] if comm is None else (res[0], c_res)


def _rms_bwd(x, dy, g, add, *, name, want_bf16, tm=256):
    s, d = x.shape
    tm = min(tm, s)

    def body(x_ref, dy_ref, g_ref, add_ref, dx_ref, *rest):
        dg_ref = rest[-1]
        i = pl.program_id(0)
        xf = x_ref[...]
        dyf = dy_ref[...].astype(F32)
        r = lax.rsqrt(jnp.mean(xf * xf, axis=-1, keepdims=True) + EPS)
        xh = xf * r
        dyg = dyf * g_ref[...]
        dx = r * (dyg - xh * jnp.mean(dyg * xh, axis=-1, keepdims=True))
        tot = add_ref[...] + dx
        dx_ref[...] = tot
        if want_bf16:
            rest[0][...] = tot.astype(BF16)
        part = jnp.sum(dyf * xh, axis=0, keepdims=True)

        @pl.when(i == 0)
        def _():
            dg_ref[...] = part

        @pl.when(i > 0)
        def _():
            dg_ref[...] += part

    row = pl.BlockSpec((tm, d), lambda i: (i, 0))
    vec = pl.BlockSpec((1, d), lambda i: (0, 0))
    out_specs = [row] + ([row] if want_bf16 else []) + [vec]
    out_shape = [jax.ShapeDtypeStruct((s, d), F32)]
    if want_bf16:
        out_shape.append(jax.ShapeDtypeStruct((s, d), BF16))
    out_shape.append(jax.ShapeDtypeStruct((1, d), F32))
    return pl.pallas_call(
        body,
        name=name,
        grid=(s // tm,),
        in_specs=[row, row, vec, row],
        out_specs=out_specs,
        out_shape=out_shape,
        compiler_params=_params(("arbitrary",)),
    )(x, dy, g, add)


def _tail(h2, gate, pp, target, g_ple, g_final, *, tm=128):
    s, d = h2.shape
    tm = min(tm, s)

    def body(h2_ref, gate_ref, pp_ref, t_ref, gp_ref, gf_ref, dh3_ref, dz_ref, dpp_ref, dgf_ref, dgp_ref, loss_ref):
        i = pl.program_id(0)
        ppf = pp_ref[...]
        gate_v = gate_ref[...]
        r_p = lax.rsqrt(jnp.mean(ppf * ppf, axis=-1, keepdims=True) + EPS)
        eh = ppf * r_p
        e = eh * gp_ref[...]
        h3 = h2_ref[...] + gate_v * e
        r_f = lax.rsqrt(jnp.mean(h3 * h3, axis=-1, keepdims=True) + EPS)
        yh = h3 * r_f
        diff = yh * gf_ref[...] - t_ref[...]
        loss_part = 0.5 * jnp.sum(jnp.mean(diff * diff, axis=-1, keepdims=True), axis=0, keepdims=True)
        dy = diff / d
        dgf = jnp.sum(dy * yh, axis=0, keepdims=True)
        dyg = dy * gf_ref[...]
        dh3 = r_f * (dyg - yh * jnp.mean(dyg * yh, axis=-1, keepdims=True))
        dh3_ref[...] = dh3
        de = dh3 * gate_v
        dz_ref[...] = (dh3 * e * gate_v * (1.0 - gate_v)).astype(BF16)
        dgp = jnp.sum(de * eh, axis=0, keepdims=True)
        deg = de * gp_ref[...]
        dpp_ref[...] = (r_p * (deg - eh * jnp.mean(deg * eh, axis=-1, keepdims=True))).astype(BF16)
        loss_row = jnp.broadcast_to(loss_part, (1, 128))

        @pl.when(i == 0)
        def _():
            dgf_ref[...] = dgf
            dgp_ref[...] = dgp
            loss_ref[...] = loss_row

        @pl.when(i > 0)
        def _():
            dgf_ref[...] += dgf
            dgp_ref[...] += dgp
            loss_ref[...] += loss_row

    row = pl.BlockSpec((tm, d), lambda i: (i, 0))
    vec = pl.BlockSpec((1, d), lambda i: (0, 0))
    return pl.pallas_call(
        body,
        name="tail_fwd_bwd",
        grid=(s // tm,),
        in_specs=[row, row, row, row, vec, vec],
        out_specs=[row, row, row, vec, vec, pl.BlockSpec((1, 128), lambda i: (0, 0))],
        out_shape=[
            jax.ShapeDtypeStruct((s, d), F32),
            jax.ShapeDtypeStruct((s, d), BF16),
            jax.ShapeDtypeStruct((s, d), BF16),
            jax.ShapeDtypeStruct((1, d), F32),
            jax.ShapeDtypeStruct((1, d), F32),
            jax.ShapeDtypeStruct((1, 128), F32),
        ],
        compiler_params=_params(("arbitrary",)),
    )(h2, gate, pp, target, g_ple, g_final)


def _rope_tables(s):
    t = jnp.arange(s, dtype=jnp.int32)
    row = (t // GRID_W).astype(F32)
    col = (t % GRID_W).astype(F32)
    half = HEAD_DIM // 2
    inv_freq = ROPE_THETA ** (-jnp.arange(0, half, 2, dtype=F32) / half)
    ang_r = row[:, None] * inv_freq
    ang_c = col[:, None] * inv_freq
    cr, sr, cc, sc = jnp.cos(ang_r), jnp.sin(ang_r), jnp.cos(ang_c), jnp.sin(ang_c)
    cos_t = jnp.concatenate([cr, cr, cc, cc], axis=-1)
    sin_t = jnp.concatenate([-sr, sr, -sc, sc], axis=-1)
    return cos_t, sin_t


def _swap_quarters(x):
    lane = lax.broadcasted_iota(jnp.int32, x.shape, x.ndim - 1)
    up = pltpu.roll(x, HEAD_DIM - 32, x.ndim - 1)
    down = pltpu.roll(x, 32, x.ndim - 1)
    return jnp.where((lane % 64) < 32, up, down)


def _cols(first, count=1):
    return slice(first * HEAD_DIM, (first + count) * HEAD_DIM)


def _qk_prep(proj, g_q, g_k, cos_t, sin_t, *, tm=256, comm=None):
    s, n = proj.shape
    tm = min(tm, s)

    def body(x_ref, gq_ref, gk_ref, c_ref, s_ref, o_ref):
        cos_v, sin_v = c_ref[...], s_ref[...]
        for h in range(COL_VA):
            x = x_ref[:, _cols(h)]
            g = gq_ref[...] if h < COL_KA else gk_ref[...]
            xn = x * lax.rsqrt(jnp.mean(x * x, axis=-1, keepdims=True) + EPS) * g
            xr = xn * cos_v + _swap_quarters(xn) * sin_v
            if h < COL_KA:
                xr = xr * Q_SCALE
            o_ref[:, _cols(h)] = xr.astype(BF16)
        o_ref[:, _cols(COL_VA, 2)] = x_ref[:, _cols(COL_VA, 2)].astype(BF16)
        o_ref[:, _cols(COL_QB, N_HEADS_B)] = (x_ref[:, _cols(COL_QB, N_HEADS_B)] * Q_SCALE).astype(BF16)
        o_ref[:, _cols(COL_KB, 4)] = x_ref[:, _cols(COL_KB, 4)].astype(BF16)

    row = pl.BlockSpec((tm, n), lambda i: (i, 0))
    tab = pl.BlockSpec((tm, HEAD_DIM), lambda i: (i, 0))
    vec = pl.BlockSpec((1, HEAD_DIM), lambda i: (0, 0))
    res, c_res = _call(
        body, name="qk_prep", grid=(s // tm,),
        in_specs=[row, vec, vec, tab, tab],
        out_specs=[row],
        out_shape=[jax.ShapeDtypeStruct((s, n), BF16)],
        sem=("parallel",), args=(proj, g_q, g_k, cos_t, sin_t), comm=comm)
    return res[0] if comm is None else (res[0], c_res)


def _qk_bwd(dqa, dka, dva, dqb, dkpad, dvpad, proj, g_q, g_k, cos_t, sin_t, *, comm=None):
    s, n = proj.shape
    tm = min(PAD_LO, s)
    assert PAD_LO % tm == 0
    lo = PAD_LO // tm

    def body(dqa_ref, dka_ref, dva_ref, dqb_ref, dkb_ref, dvb_ref, x_ref, gq_ref, gk_ref, c_ref, s_ref,
             o_ref, dgq_ref, dgk_ref):
        i = pl.program_id(0)
        cos_v, sin_v = c_ref[...], s_ref[...]

        def head(d, x, g):
            dn = d * cos_v + _swap_quarters(d * sin_v)
            r = lax.rsqrt(jnp.mean(x * x, axis=-1, keepdims=True) + EPS)
            xh = x * r
            dng = dn * g
            dx = r * (dng - xh * jnp.mean(dng * xh, axis=-1, keepdims=True))
            return dx.astype(BF16), jnp.sum(dn * xh, axis=0, keepdims=True)

        acc_q = jnp.zeros((1, HEAD_DIM), F32)
        acc_k = jnp.zeros((1, HEAD_DIM), F32)
        for h in range(N_HEADS_A):
            o_ref[:, _cols(h)], part = head(dqa_ref[:, _cols(h)] * ATT_SCALE, x_ref[:, _cols(h)], gq_ref[...])
            acc_q = acc_q + part
        for h in range(N_KV_A):
            o_ref[:, _cols(COL_KA + h)], part = head(dka_ref[:, _cols(h)] * LN2, x_ref[:, _cols(COL_KA + h)],
                                                     gk_ref[...])
            acc_k = acc_k + part
        o_ref[:, _cols(COL_VA, 2)] = dva_ref[...].astype(BF16)
        o_ref[:, _cols(COL_QB, N_HEADS_B)] = (dqb_ref[...] * ATT_SCALE).astype(BF16)
        o_ref[:, _cols(COL_KB, 2)] = (dkb_ref[...] * LN2).astype(BF16)
        o_ref[:, _cols(COL_VB, 2)] = dvb_ref[...].astype(BF16)

        @pl.when(i == 0)
        def _():
            dgq_ref[...] = acc_q
            dgk_ref[...] = acc_k

        @pl.when(i > 0)
        def _():
            dgq_ref[...] += acc_q
            dgk_ref[...] += acc_k

    def rows(width, shift=0):
        return pl.BlockSpec((tm, width), lambda i: (i + shift, 0))

    kv_w = N_KV_A * HEAD_DIM
    q_w = N_HEADS_A * HEAD_DIM
    vec = pl.BlockSpec((1, HEAD_DIM), lambda i: (0, 0))
    res, c_res = _call(
        body, name="qk_bwd", grid=(s // tm,),
        in_specs=[rows(q_w), rows(kv_w), rows(kv_w), rows(q_w), rows(kv_w, lo), rows(kv_w, lo), rows(n),
                  vec, vec, rows(HEAD_DIM), rows(HEAD_DIM)],
        out_specs=[rows(n), vec, vec],
        out_shape=[
            jax.ShapeDtypeStruct((s, n), BF16),
            jax.ShapeDtypeStruct((1, HEAD_DIM), F32),
            jax.ShapeDtypeStruct((1, HEAD_DIM), F32),
        ],
        sem=("arbitrary",), args=(dqa, dka, dva, dqb, dkpad, dvpad, proj, g_q, g_k, cos_t, sin_t), comm=comm)
    return res if comm is None else (res, c_res)


_NT = (((1,), (1,)), ((), ()))
_TN = (((0,), (0,)), ((), ()))


def _attn_a_fwd(pb, *, tq=512, sub=256, comm=None):
    s = pb.shape[0]
    tq = min(tq, s)

    sub = min(sub, tq)

    def body(q_ref, k_ref, v_ref, o_ref, lse_ref):
        k = k_ref[...]
        v = v_ref[...]
        for r in range(tq // sub):
            rows = pl.ds(r * sub, sub)
            sc = lax.dot_general(q_ref[rows, :], k, _NT, preferred_element_type=F32)
            m = jnp.max(sc, axis=-1, keepdims=True)
            p = jnp.exp2(sc - m)
            l = jnp.sum(p, axis=-1, keepdims=True)
            o = jnp.dot(p.astype(BF16), v, preferred_element_type=F32)
            o_ref[rows, :] = (o / l).astype(BF16)
            lse_ref[rows, :] = jnp.broadcast_to(m + jnp.log2(l), (sub, HEAD_DIM))

    res, c_res = _call(
        body, name="attn_a_fwd", grid=(N_HEADS_A, s // tq),
        in_specs=[
            pl.BlockSpec((tq, HEAD_DIM), lambda h, i: (i, COL_QA + h)),
            pl.BlockSpec((s, HEAD_DIM), lambda h, i: (0, COL_KA + h // GROUP)),
            pl.BlockSpec((s, HEAD_DIM), lambda h, i: (0, COL_VA + h // GROUP)),
        ],
        out_specs=[
            pl.BlockSpec((tq, HEAD_DIM), lambda h, i: (i, h)),
            pl.BlockSpec((None, tq, HEAD_DIM), lambda h, i: (h, i, 0)),
        ],
        out_shape=[
            jax.ShapeDtypeStruct((s, N_HEADS_A * HEAD_DIM), BF16),
            jax.ShapeDtypeStruct((N_HEADS_A, s, HEAD_DIM), F32),
        ],
        sem=("parallel", "parallel"), args=(pb, pb, pb), comm=comm)
    return res if comm is None else (res, c_res)


def _attn_a_bwd(pb, att, datt, lse, *, tq=256, sub=128, comm=None):
    s = pb.shape[0]
    tq = min(tq, s)
    sub = min(sub, tq)

    def body(q_ref, k_ref, v_ref, o_ref, do_ref, lse_ref, dq_ref, dk_ref, dv_ref):
        first = jnp.logical_and(pl.program_id(1) == 0, pl.program_id(2) == 0)
        k = k_ref[...]
        v = v_ref[...]
        dk = dv = None
        for r in range(tq // sub):
            rows = pl.ds(r * sub, sub)
            q = q_ref[rows, :]
            do = do_ref[rows, :]
            sc = lax.dot_general(q, k, _NT, preferred_element_type=F32)
            p = jnp.exp2(sc - lse_ref[rows, :][:, :1])
            dp = lax.dot_general(do, v, _NT, preferred_element_type=F32)
            delta = jnp.sum(do.astype(F32) * o_ref[rows, :].astype(F32), axis=-1, keepdims=True)
            ds = (p * (dp - delta)).astype(BF16)
            dq_ref[rows, :] = jnp.dot(ds, k, preferred_element_type=F32)
            dk_r = lax.dot_general(ds, q, _TN, preferred_element_type=F32)
            dv_r = lax.dot_general(p.astype(BF16), do, _TN, preferred_element_type=F32)
            dk = dk_r if dk is None else dk + dk_r
            dv = dv_r if dv is None else dv + dv_r

        @pl.when(first)
        def _():
            dk_ref[...] = dk
            dv_ref[...] = dv

        @pl.when(jnp.logical_not(first))
        def _():
            dk_ref[...] += dk
            dv_ref[...] += dv

    qmap = lambda kv, g, i: (i, kv * GROUP + g)
    res, c_res = _call(
        body, name="attn_a_bwd", grid=(N_KV_A, GROUP, s // tq),
        in_specs=[
            pl.BlockSpec((tq, HEAD_DIM), lambda kv, g, i: (i, COL_QA + kv * GROUP + g)),
            pl.BlockSpec((s, HEAD_DIM), lambda kv, g, i: (0, COL_KA + kv)),
            pl.BlockSpec((s, HEAD_DIM), lambda kv, g, i: (0, COL_VA + kv)),
            pl.BlockSpec((tq, HEAD_DIM), qmap),
            pl.BlockSpec((tq, HEAD_DIM), qmap),
            pl.BlockSpec((None, tq, HEAD_DIM), lambda kv, g, i: (kv * GROUP + g, i, 0)),
        ],
        out_specs=[
            pl.BlockSpec((tq, HEAD_DIM), qmap),
            pl.BlockSpec((s, HEAD_DIM), lambda kv, g, i: (0, kv)),
            pl.BlockSpec((s, HEAD_DIM), lambda kv, g, i: (0, kv)),
        ],
        out_shape=[
            jax.ShapeDtypeStruct((s, N_HEADS_A * HEAD_DIM), F32),
            jax.ShapeDtypeStruct((s, N_KV_A * HEAD_DIM), F32),
            jax.ShapeDtypeStruct((s, N_KV_A * HEAD_DIM), F32),
        ],
        sem=("arbitrary", "arbitrary", "arbitrary"), args=(pb, pb, pb, att, datt, lse), comm=comm)
    return res if comm is None else (res, c_res)


def _t5_bucket(rel):
    nb = N_BUCKETS // 2
    ret = jnp.where(rel > 0, nb, 0)
    n = jnp.abs(rel)
    max_exact = nb // 2
    nf = jnp.maximum(n, 1).astype(F32)
    large = max_exact + (jnp.log(nf / max_exact) / math.log(MAX_DISTANCE / max_exact)
                         * (nb - max_exact)).astype(jnp.int32)
    large = jnp.minimum(large, nb - 1)
    return ret + jnp.where(n < max_exact, n, large)


def _band_buckets():
    r = jnp.arange(BLOCK_Q, dtype=jnp.int32)
    j = jnp.arange(3 * BLOCK_Q, dtype=jnp.int32)
    return _t5_bucket((j[None, :] - BLOCK_Q) - r[:, None])


def _band_bias(bucket, table_ref, h):
    acc = jnp.zeros(bucket.shape, F32)
    for b in range(N_BUCKETS):
        acc = jnp.where(bucket == b, table_ref[b, h], acc)
    return acc


GQ = GROUP * BLOCK_Q


def _stack_heads(x):
    return jnp.concatenate([x[:, _cols(g)] for g in range(GROUP)], axis=0)


def _unstack_heads(x):
    return jnp.concatenate([x[g * BLOCK_Q:(g + 1) * BLOCK_Q] for g in range(GROUP)], axis=1)


def _group_bias(bucket, table_ref, kv):
    return jnp.concatenate([_band_bias(bucket, table_ref, kv * GROUP + g) * LOG2E for g in range(GROUP)], axis=0)


def _group_sink(sink_ref, kv):
    head = lax.broadcasted_iota(jnp.int32, (GQ, 1), 0) // BLOCK_Q
    snk = jnp.zeros((GQ, 1), F32)
    for g in range(GROUP):
        snk = jnp.where(head == g, sink_ref[0, kv * GROUP + g] * LOG2E, snk)
    return snk


def _band_mask(n, s):
    r = lax.broadcasted_iota(jnp.int32, (GQ, 3 * BLOCK_Q), 0) % BLOCK_Q
    j = lax.broadcasted_iota(jnp.int32, (GQ, 3 * BLOCK_Q), 1)
    rel = j - BLOCK_Q - r
    kabs = n * BLOCK_Q + j - BLOCK_Q
    return (jnp.abs(rel) <= WINDOW) & (kabs >= 0) & (kabs < s)


def _band_start(n):
    return pl.multiple_of(n * BLOCK_Q + (PAD_LO - BLOCK_Q), BLOCK_Q)


def _attn_b_fwd(pb, kpad, vpad, bucket, table, sink, *, comm=None):
    s = pb.shape[0]
    nblk = s // BLOCK_Q
    sp = kpad.shape[0]

    def body(table_ref, sink_ref, q_ref, k_ref, v_ref, bucket_ref, o_ref, lse_ref, bias_ref):
        kv = pl.program_id(0)
        n = pl.program_id(1)

        @pl.when(n == 0)
        def _():
            bias_ref[...] = _group_bias(bucket_ref[...], table_ref, kv)

        band = pl.ds(_band_start(n), 3 * BLOCK_Q)
        kb = k_ref[band, :]
        vb = v_ref[band, :]
        sc = lax.dot_general(_stack_heads(q_ref[...]), kb, _NT, preferred_element_type=F32) + bias_ref[...]
        sc = jnp.where(_band_mask(n, s), sc, NEG_INF)
        snk = _group_sink(sink_ref, kv)
        m = jnp.maximum(jnp.max(sc, axis=-1, keepdims=True), snk)
        p = jnp.exp2(sc - m)
        l = jnp.sum(p, axis=-1, keepdims=True) + jnp.exp2(snk - m)
        o = jnp.dot(p.astype(BF16), vb, preferred_element_type=F32)
        o_ref[...] = _unstack_heads((o / l).astype(BF16))
        lse = m + jnp.log2(l)
        for g in range(GROUP):
            lse_ref[g] = jnp.broadcast_to(lse[g * BLOCK_Q:(g + 1) * BLOCK_Q], (BLOCK_Q, HEAD_DIM))

    smem = pl.BlockSpec(memory_space=pltpu.SMEM)
    res, c_res = _call(
        body, name="attn_b_fwd", grid=(N_KV_B, nblk),
        in_specs=[
            smem,
            smem,
            pl.BlockSpec((BLOCK_Q, GROUP * HEAD_DIM), lambda kv, n: (n, COL_QB // GROUP + kv)),
            pl.BlockSpec((sp, HEAD_DIM), lambda kv, n: (0, kv)),
            pl.BlockSpec((sp, HEAD_DIM), lambda kv, n: (0, kv)),
            pl.BlockSpec((BLOCK_Q, 3 * BLOCK_Q), lambda kv, n: (0, 0)),
        ],
        out_specs=[
            pl.BlockSpec((BLOCK_Q, GROUP * HEAD_DIM), lambda kv, n: (n, kv)),
            pl.BlockSpec((GROUP, BLOCK_Q, HEAD_DIM), lambda kv, n: (kv, n, 0)),
        ],
        out_shape=[
            jax.ShapeDtypeStruct((s, N_HEADS_B * HEAD_DIM), BF16),
            jax.ShapeDtypeStruct((N_HEADS_B, s, HEAD_DIM), F32),
        ],
        scratch_shapes=[pltpu.VMEM((GQ, 3 * BLOCK_Q), F32)],
        sem=("arbitrary", "arbitrary"), args=(table, sink, pb, kpad, vpad, bucket), comm=comm)
    return res if comm is None else (res, c_res)


def _attn_b_bwd(pb, kpad, vpad, att, datt, lse, bucket, table, sink, *, comm=None):
    s = pb.shape[0]
    nblk = s // BLOCK_Q
    sp = kpad.shape[0]

    def body(table_ref, sink_ref, q_ref, k_ref, v_ref, o_ref, do_ref, lse_ref, bucket_ref,
             dq_ref, dk_ref, dv_ref, dtab_ref, dsink_ref, bias_ref, dbias_ref):
        kv = pl.program_id(0)
        n = pl.program_id(1)

        @pl.when(n == 0)
        def _():
            dk_ref[...] = jnp.zeros_like(dk_ref)
            dv_ref[...] = jnp.zeros_like(dv_ref)
            bias_ref[...] = _group_bias(bucket_ref[...], table_ref, kv)
            dbias_ref[...] = jnp.zeros_like(dbias_ref)
            dsink_ref[...] = jnp.zeros_like(dsink_ref)

        band = pl.ds(_band_start(n), 3 * BLOCK_Q)
        q = _stack_heads(q_ref[...])
        do = _stack_heads(do_ref[...])
        o = _stack_heads(o_ref[...])
        kb = k_ref[band, :]
        vb = v_ref[band, :]
        lse = jnp.concatenate([lse_ref[g][:, :1] for g in range(GROUP)], axis=0)
        sc = lax.dot_general(q, kb, _NT, preferred_element_type=F32) + bias_ref[...]
        sc = jnp.where(_band_mask(n, s), sc, NEG_INF)
        p = jnp.exp2(sc - lse)
        dp = lax.dot_general(do, vb, _NT, preferred_element_type=F32)
        delta = jnp.sum(do.astype(F32) * o.astype(F32), axis=-1, keepdims=True)
        ds = p * (dp - delta)
        dsb = ds.astype(BF16)
        dq_ref[...] = _unstack_heads(jnp.dot(dsb, kb, preferred_element_type=F32))
        dk_ref[band, :] += lax.dot_general(dsb, q, _TN, preferred_element_type=F32)
        dv_ref[band, :] += lax.dot_general(p.astype(BF16), do, _TN, preferred_element_type=F32)
        dbias_ref[...] += ds
        sink_part = -jnp.exp2(_group_sink(sink_ref, kv) - lse) * delta
        for g in range(GROUP):
            rows = slice(g * BLOCK_Q, (g + 1) * BLOCK_Q)
            dsink_ref[g] += jnp.broadcast_to(jnp.sum(sink_part[rows], axis=0, keepdims=True), (1, HEAD_DIM))

        @pl.when(n == nblk - 1)
        def _():
            bucket_v = bucket_ref[...]
            row = lax.broadcasted_iota(jnp.int32, (N_BUCKETS, HEAD_DIM), 0)
            for g in range(GROUP):
                acc = dbias_ref[g * BLOCK_Q:(g + 1) * BLOCK_Q, :]
                tot = jnp.zeros((N_BUCKETS, HEAD_DIM), F32)
                for b in range(N_BUCKETS):
                    tot = jnp.where(row == b, jnp.sum(jnp.where(bucket_v == b, acc, 0.0), keepdims=True), tot)
                dtab_ref[g] = tot

    smem = pl.BlockSpec(memory_space=pltpu.SMEM)
    wide = GROUP * HEAD_DIM
    res, c_res = _call(
        body, name="attn_b_bwd", grid=(N_KV_B, nblk),
        in_specs=[
            smem,
            smem,
            pl.BlockSpec((BLOCK_Q, wide), lambda kv, n: (n, COL_QB // GROUP + kv)),
            pl.BlockSpec((sp, HEAD_DIM), lambda kv, n: (0, kv)),
            pl.BlockSpec((sp, HEAD_DIM), lambda kv, n: (0, kv)),
            pl.BlockSpec((BLOCK_Q, wide), lambda kv, n: (n, N_HEADS_A // GROUP + kv)),
            pl.BlockSpec((BLOCK_Q, wide), lambda kv, n: (n, N_HEADS_A // GROUP + kv)),
            pl.BlockSpec((GROUP, BLOCK_Q, HEAD_DIM), lambda kv, n: (kv, n, 0)),
            pl.BlockSpec((BLOCK_Q, 3 * BLOCK_Q), lambda kv, n: (0, 0)),
        ],
        out_specs=[
            pl.BlockSpec((BLOCK_Q, wide), lambda kv, n: (n, kv)),
            pl.BlockSpec((sp, HEAD_DIM), lambda kv, n: (0, kv)),
            pl.BlockSpec((sp, HEAD_DIM), lambda kv, n: (0, kv)),
            pl.BlockSpec((GROUP, N_BUCKETS, HEAD_DIM), lambda kv, n: (kv, 0, 0)),
            pl.BlockSpec((GROUP, 1, HEAD_DIM), lambda kv, n: (kv, 0, 0)),
        ],
        out_shape=[
            jax.ShapeDtypeStruct((s, N_HEADS_B * HEAD_DIM), F32),
            jax.ShapeDtypeStruct((sp, N_KV_B * HEAD_DIM), F32),
            jax.ShapeDtypeStruct((sp, N_KV_B * HEAD_DIM), F32),
            jax.ShapeDtypeStruct((N_HEADS_B, N_BUCKETS, HEAD_DIM), F32),
            jax.ShapeDtypeStruct((N_HEADS_B, 1, HEAD_DIM), F32),
        ],
        scratch_shapes=[pltpu.VMEM((GQ, 3 * BLOCK_Q), F32), pltpu.VMEM((GQ, 3 * BLOCK_Q), F32)],
        sem=("arbitrary", "arbitrary"),
        args=(table, sink, pb, kpad, vpad, att, datt, lse, bucket), comm=comm)
    return res if comm is None else (res, c_res)


_MESH = pl.DeviceIdType.MESH


def _other_chips(x, y):
    return [(x, 1 - y), (1 - x, y), (1 - x, 1 - y)]


_HBM = pl.BlockSpec(memory_space=pltpu.HBM)
_SEM = pl.BlockSpec(memory_space=pltpu.SEMAPHORE)
_SPLIT = pltpu.CompilerParams(has_side_effects=pltpu.SideEffectType.DATAFLOW_SIDE_EFFECTING)


def _in_hbm(a):
    return pltpu.with_memory_space_constraint(a, pltpu.HBM)


def _my_half(rows):
    c = lax.axis_index("c")
    half = rows // 2
    return pl.ds(pl.multiple_of(c * half, half), half), pl.ds(pl.multiple_of((1 - c) * half, half), half)


def _gather_route(shapes):
    def route(src, land):
        x, y, c = lax.axis_index("x"), lax.axis_index("y"), lax.axis_index("c")
        out = []
        for t, shape in enumerate(shapes):
            mine, _ = _my_half(shape[0])
            for px, py in _other_chips(x, y):
                out.append((src[t].at[mine], land[t].at[2 * x + y, mine], land[t].at[2 * px + py, mine], (px, py, c)))
        return out

    return route


def _exchange_route(n_t):
    def route(src, land):
        x, y, c = lax.axis_index("x"), lax.axis_index("y"), lax.axis_index("c")
        out = []
        for t in range(n_t):
            for px, py in _other_chips(x, y):
                k = 2 * px + py
                out.append((src[t].at[k], land[t].at[2 * (2 * x + y) + c], land[t].at[2 * k + c], (px, py, c)))
        return out

    return route


def _split_start(name, srcs, land_shapes, route, after):
    n = len(srcs)

    def body(*refs):
        src, land, send_sems, recv_sems, token = refs[:n], refs[n:2 * n], refs[2 * n + 1], refs[2 * n + 2], refs[-1]
        for i, (src_ref, dst_ref, _, to) in enumerate(route(src, land)):
            pltpu.make_async_remote_copy(src_ref=src_ref, dst_ref=dst_ref, send_sem=send_sems.at[i],
                                         recv_sem=recv_sems.at[i], device_id=to, device_id_type=_MESH).start()
        token[...] = jnp.zeros_like(token)

    sem = pltpu.SemaphoreType.DMA((3 * n,))
    lands = [lax.empty(sh, a.dtype) for sh, a in zip(land_shapes, srcs)]
    res = pl.pallas_call(
        body, name=name,
        in_specs=[_HBM] * (2 * n) + [_ANY],
        out_specs=[_SEM, _SEM] + [_HBM] * (2 * n) + [pl.BlockSpec(memory_space=pltpu.VMEM)],
        out_shape=[sem, sem] + [pltpu.HBM(a.shape, a.dtype) for a in list(srcs) + lands]
        + [jax.ShapeDtypeStruct((8, 128), F32)],
        input_output_aliases={i: 2 + i for i in range(2 * n)},
        compiler_params=_SPLIT,
    )(*[_in_hbm(a) for a in srcs], *[_in_hbm(a) for a in lands], after)
    return (res[0], res[1]), res[2:2 + n], res[2 + n:2 + 2 * n], res[-1]


def _split_wait(name, srcs, lands, sems, route, after):
    n = len(srcs)

    def body(*refs):
        src, land, send_sems, recv_sems = refs[:n], refs[n:2 * n], refs[2 * n], refs[2 * n + 1]
        for i, (src_ref, _, dst_ref, to) in enumerate(route(src, land)):
            cp = pltpu.make_async_remote_copy(src_ref=src_ref, dst_ref=dst_ref, send_sem=send_sems.at[i],
                                              recv_sem=recv_sems.at[i], device_id=to, device_id_type=_MESH)
            cp.wait_send()
            cp.wait_recv()

    res = pl.pallas_call(
        body, name=name,
        in_specs=[_HBM] * (2 * n) + [_SEM, _SEM, _ANY],
        out_specs=[_HBM] * (2 * n),
        out_shape=[pltpu.HBM(a.shape, a.dtype) for a in list(srcs) + list(lands)],
        input_output_aliases={i: i for i in range(2 * n)},
        compiler_params=_SPLIT,
    )(*srcs, *lands, sems[0], sems[1], after)
    return res[:n], res[n:]


def _comm_only(name, comm):
    return _call(lambda: None, name=name, grid=(1,), in_specs=[], out_specs=[], out_shape=[], args=(), comm=comm)[1]


def _swap_comm(shards, lands):
    n_t = len(shards)

    def copies(ins, land, sems, later):
        send_sems, recv_sems, local_sems = sems
        x, y, c = lax.axis_index("x"), lax.axis_index("y"), lax.axis_index("c")
        sibling = (x, y, 1 - c)
        local, sends, recvs = [], [], []
        for t in range(n_t):
            mine, other = _my_half(shards[t].shape[0])
            local.append(pltpu.make_async_copy(ins[t], land[t].at[2 * x + y], local_sems.at[t]))
            for j, (px, py) in enumerate(_other_chips(x, y)):
                k = 2 * px + py
                for part, out in ((mine, sends), (other, recvs)) if later else ((mine, sends),):
                    out.append(pltpu.make_async_remote_copy(
                        src_ref=land[t].at[k, part], dst_ref=land[t].at[k, part], send_sem=send_sems.at[3 * t + j],
                        recv_sem=recv_sems.at[3 * t + j], device_id=sibling, device_id_type=_MESH))
        return local, sends, recvs

    def start(ins, land, sems):
        local, sends, _ = copies(ins, land, sems, False)
        for cp in local + sends:
            cp.start()

    def finish(ins, land, sems):
        local, sends, recvs = copies(ins, land, sems, True)
        for cp in recvs:
            cp.wait_recv()
        for cp in sends:
            cp.wait_send()
        for cp in local:
            cp.wait()

    return _Comm(
        list(shards) + list(lands), [jax.ShapeDtypeStruct(a.shape, a.dtype) for a in lands],
        [pltpu.SemaphoreType.DMA((3 * n_t,)), pltpu.SemaphoreType.DMA((3 * n_t,)), pltpu.SemaphoreType.DMA((n_t,))],
        start, finish, aliases={n_t + t: t for t in range(n_t)})


def _forward_comm(partials, lands):
    n_t = len(partials)

    def copies(ins, land, sems, later):
        send_sems, recv_sems, local_sems = sems
        x, y, c = lax.axis_index("x"), lax.axis_index("y"), lax.axis_index("c")
        me = 2 * x + y
        local, sends, recvs = [], [], []
        for t in range(n_t):

            def copy(k, src_ref, slot, t=t):
                return pltpu.make_async_remote_copy(
                    src_ref=src_ref, dst_ref=land[t].at[slot], send_sem=send_sems.at[4 * t + k],
                    recv_sem=recv_sems.at[4 * t + k], device_id=_sibling(), device_id_type=_MESH)

            local.append(pltpu.make_async_copy(ins[t].at[me], land[t].at[2 * me + c], local_sems.at[t]))
            chips = [me] + [2 * px + py for px, py in _other_chips(x, y)]
            for j, k in enumerate(chips):
                sends.append(copy(j, ins[t].at[me] if j == 0 else land[t].at[2 * k + c], 2 * k + c))
                if later:
                    recvs.append(copy(j, land[t].at[2 * k + 1 - c], 2 * k + 1 - c))
        return local, sends, recvs

    def start(ins, land, sems):
        local, sends, _ = copies(ins, land, sems, False)
        for cp in local + sends:
            cp.start()

    def finish(ins, land, sems):
        local, sends, recvs = copies(ins, land, sems, True)
        for cp in recvs:
            cp.wait_recv()
        for cp in sends:
            cp.wait_send()
        for cp in local:
            cp.wait()

    return _Comm(
        list(partials) + list(lands), [jax.ShapeDtypeStruct(a.shape, a.dtype) for a in lands],
        [pltpu.SemaphoreType.DMA((4 * n_t,)), pltpu.SemaphoreType.DMA((4 * n_t,)), pltpu.SemaphoreType.DMA((n_t,))],
        start, finish, aliases={n_t + t: t for t in range(n_t)})


def _allreduce_small(pack):
    rows, d = pack.shape

    def body(p_ref, sum_ref, all_ref, send_sems, recv_sems):
        x, y, c = lax.axis_index("x"), lax.axis_index("y"), lax.axis_index("c")
        me = 4 * x + 2 * y + c
        all_ref[me] = p_ref[...]
        peers = []
        for dx in range(2):
            for dy in range(2):
                for dc in range(2):
                    if dx or dy or dc:
                        px = 1 - x if dx else x
                        py = 1 - y if dy else y
                        pc = 1 - c if dc else c
                        peers.append((4 * dx + 2 * dy + dc - 1, (px, py, pc)))
        sends = []
        for k, to in peers:
            cp = pltpu.make_async_remote_copy(
                src_ref=p_ref, dst_ref=all_ref.at[me], send_sem=send_sems.at[k], recv_sem=recv_sems.at[k],
                device_id=to, device_id_type=_MESH)
            cp.start()
            sends.append(cp)
        for k, (px, py, pc) in peers:
            pltpu.make_async_remote_copy(
                src_ref=p_ref, dst_ref=all_ref.at[4 * px + 2 * py + pc], send_sem=send_sems.at[k],
                recv_sem=recv_sems.at[k], device_id=(px, py, pc), device_id_type=_MESH).wait_recv()
        for cp in sends:
            cp.wait_send()
        tot = all_ref[0]
        for i in range(1, N_DEV):
            tot = tot + all_ref[i]
        sum_ref[...] = tot

    vm = pl.BlockSpec(memory_space=pltpu.VMEM)
    return pl.pallas_call(
        body,
        name="allreduce_small",
        in_specs=[vm],
        out_specs=vm,
        out_shape=jax.ShapeDtypeStruct((rows, d), F32),
        scratch_shapes=[
            pltpu.VMEM((N_DEV, rows, d), F32),
            pltpu.SemaphoreType.DMA((N_DEV - 1,)),
            pltpu.SemaphoreType.DMA((N_DEV - 1,)),
        ],
    )(pack)


def _adamw_math(w, g, m, v):
    m = ADAM_B1 * m + (1.0 - ADAM_B1) * g
    v = ADAM_B2 * v + (1.0 - ADAM_B2) * (g * g)
    m_hat = m / (1.0 - ADAM_B1 ** ADAM_STEP)
    v_hat = v / (1.0 - ADAM_B2 ** ADAM_STEP)
    delta = -ADAM_LR * (m_hat / (jnp.sqrt(v_hat) + ADAM_EPS) + ADAM_WD * w)
    return delta, m, v


def _sum_adamw(parts, w, m, v, *, name, tr=256):
    r, c = w.shape
    tr = min(tr, r)
    tc = min(c, 1024)

    def body(p_ref, w_ref, m_ref, v_ref, g_ref, d_ref, m2_ref, v2_ref):
        g = p_ref[0].astype(F32)
        for i in range(1, N_DEV):
            g = g + p_ref[i].astype(F32)
        delta, m2, v2 = _adamw_math(w_ref[...], g, m_ref[...], v_ref[...])
        g_ref[...] = g
        d_ref[...] = delta
        m2_ref[...] = m2
        v2_ref[...] = v2

    blk = pl.BlockSpec((tr, tc), lambda i, j: (i, j))
    return pl.pallas_call(
        body,
        name=name,
        grid=(r // tr, c // tc),
        in_specs=[pl.BlockSpec((N_DEV, tr, tc), lambda i, j: (0, i, j)), blk, blk, blk],
        out_specs=[blk] * 4,
        out_shape=[jax.ShapeDtypeStruct((r, c), F32)] * 4,
        compiler_params=_params(("parallel", "parallel")),
    )(parts, w, m, v)


def _adamw_small(g, w, m, v):
    def body(g_ref, w_ref, m_ref, v_ref, d_ref, m2_ref, v2_ref):
        delta, m2, v2 = _adamw_math(w_ref[...], g_ref[...], m_ref[...], v_ref[...])
        d_ref[...] = delta
        m2_ref[...] = m2
        v2_ref[...] = v2

    vm = pl.BlockSpec(memory_space=pltpu.VMEM)
    return pl.pallas_call(
        body,
        name="adamw_small",
        in_specs=[vm] * 4,
        out_specs=[vm] * 3,
        out_shape=[jax.ShapeDtypeStruct(g.shape, F32)] * 3,
    )(g, w, m, v)


def _relu2_epilogue(acc):
    ra = jnp.maximum(acc, 0.0)
    return ra * ra, ra


def _rows(stacked):
    return stacked.reshape(stacked.shape[0] * stacked.shape[1], stacked.shape[2])


def _by_chip(mat):
    return mat.reshape(N_CHIPS, mat.shape[0] // N_CHIPS, mat.shape[1])


def _local_step(x, p, target, shards, small, update):
    s, d = x.shape
    cos_t, sin_t = _rope_tables(s)
    bucket = _band_buckets()
    p_bf = p.astype(BF16)
    wts = {}

    def gather(tag, names, after):
        srcs = [shards[n] for n in names]
        route = _gather_route([a.shape for a in srcs])
        sems, srcs, lands, token = _split_start(f"gather_start_{tag}", srcs, [(N_CHIPS,) + a.shape for a in srcs],
                                                route, after)
        return lambda done: _swap_comm(*_split_wait(f"gather_wait_{tag}", srcs, lands, sems, route, done)), token

    in_landed, token = gather("in", ["w_in"], shards["w_in"])
    g_attn = small["attn_norm_g"] + token[:1, :1]
    u, (wts["w_in"],) = _rms_fwd(x, g_attn, name="norm_attn", comm=in_landed(token))
    mid_landed, _ = gather("mid", ["w_out", "w_gate", "ple_w"], u)
    proj = _matmul(u, wts["w_in"], mode="nn", out_dtypes=[F32], name="mm_in", bn=768)
    pb, (w_out_s, w_gate_s, wts["ple_w"]) = _qk_prep(proj, small["q_norm_g"], small["k_norm_g"], cos_t, sin_t,
                                                     comm=mid_landed(proj))
    wts["w_out"], wts["w_gate"] = _rows(w_out_s), _rows(w_gate_s)
    up_landed, _ = gather("up", ["w_up"], pb)
    oa, lse_a = _attn_a_fwd(pb)
    pad = ((PAD_LO, PAD_HI), (0, 0))
    kpad = jnp.pad(pb[:, COL_KB * HEAD_DIM:COL_VB * HEAD_DIM], pad)
    vpad = jnp.pad(pb[:, COL_VB * HEAD_DIM:], pad)
    (ob, lse_b), (wts["w_up"],) = _attn_b_fwd(pb, kpad, vpad, bucket, small["rel_bias_table"],
                                              small["sink_logits"], comm=up_landed(oa))
    down_landed, _ = gather("down", ["w_down"], ob)
    att = jnp.concatenate([oa, ob], axis=-1)
    h1 = _matmul(att, wts["w_out"], mode="nn", out_dtypes=[F32], name="mm_out",
                 epilogue=lambda acc, res: (acc + res,), extras=(x,))
    mn = _rms_fwd(h1, small["mlp_norm_g"], name="norm_mlp")
    r, ra = _matmul(mn, wts["w_up"], mode="nn", out_dtypes=[BF16, BF16], name="mm_up", epilogue=_relu2_epilogue)
    (w_down_s,) = _comm_only("swap_w_down", down_landed(r))
    wts["w_down"] = _rows(w_down_s)
    h2 = _matmul(r, wts["w_down"], mode="nn", out_dtypes=[F32], name="mm_down",
                 epilogue=lambda acc, res: (acc + res,), extras=(h1,))
    ng = _rms_fwd(h2, small["gate_norm_g"], name="norm_gate")
    gate = _matmul(ng, wts["w_gate"], mode="nn", out_dtypes=[F32], name="mm_gate",
                   epilogue=lambda acc: (1.0 / (1.0 + jnp.exp(-acc)),))
    pp = _matmul(p_bf, wts["ple_w"], mode="nn", out_dtypes=[F32], name="mm_ple", bn=512)
    dh3, dz, dpp, dg_final, dg_ple, loss = _tail(h2, gate, pp, target, small["ple_norm_g"], small["final_norm_g"])

    dng = _matmul(dz, wts["w_gate"], mode="nt", out_dtypes=[F32], name="mm_gate_dx")
    gw_gate = _matmul(ng, dz, mode="tn", out_dtypes=[BF16], name="mm_gate_dw")
    gw_ple = _matmul(p_bf, dpp, mode="tn", out_dtypes=[BF16], name="mm_ple_dw", bn=512, out_stack=N_CHIPS)
    dh2, dh2_bf, dg_gate = _rms_bwd(h2, dng, small["gate_norm_g"], dh3, name="norm_gate_bwd", want_bf16=True)
    def exchange(tag, partials, after):
        route = _exchange_route(len(partials))
        sems, srcs, lands, _ = _split_start(f"exchange_start_{tag}", partials,
                                            [(N_DEV,) + g.shape[1:] for g in partials], route, after)
        return lambda done: _forward_comm(*_split_wait(f"exchange_wait_{tag}", srcs, lands, sems, route, done))

    big = {}
    gate_landed = exchange("gate", [_by_chip(gw_gate), gw_ple], dh2_bf)
    gw_down = _matmul(r, dh2_bf, mode="tn", out_dtypes=[BF16], name="mm_down_dw")
    da, (parts_gate, parts_ple) = _matmul(
        dh2_bf, wts["w_down"], mode="nt", out_dtypes=[BF16], name="mm_down_dx",
        epilogue=lambda acc, ra_v: (acc * (2.0 * ra_v.astype(F32)),), extras=(ra,), comm=gate_landed(gw_down))
    down_landed = exchange("down", [_by_chip(gw_down)], da)
    big["w_gate"], big["ple_w"] = update("w_gate", parts_gate), update("ple_w", parts_ple)
    gw_up = _matmul(mn, da, mode="tn", out_dtypes=[BF16], name="mm_up_dw", out_stack=N_CHIPS)
    dmn = _matmul(da, wts["w_up"], mode="nt", out_dtypes=[F32], name="mm_up_dx")
    dh1, dh1_bf, dg_mlp = _rms_bwd(h1, dmn, small["mlp_norm_g"], dh2, name="norm_mlp_bwd", want_bf16=True)
    datt, (parts_down,) = _matmul(dh1_bf, wts["w_out"], mode="nt", out_dtypes=[BF16], name="mm_out_dx",
                                  comm=down_landed(dh1_bf))
    gw_out = _matmul(att, dh1_bf, mode="tn", out_dtypes=[BF16], name="mm_out_dw")
    up_landed = exchange("up", [gw_up, _by_chip(gw_out)], datt)
    dqb, dkpad, dvpad, dtab, dsink = _attn_b_bwd(pb, kpad, vpad, att, datt, lse_b, bucket,
                                                 small["rel_bias_table"], small["sink_logits"])
    dqa, dka, dva = _attn_a_bwd(pb, att, datt, lse_a)
    (dproj, dg_q, dg_k), (parts_up, parts_out) = _qk_bwd(dqa, dka, dva, dqb, dkpad, dvpad, proj,
                                                         small["q_norm_g"], small["k_norm_g"], cos_t, sin_t,
                                                         comm=up_landed(dqa))
    gw_in = _matmul(u, dproj, mode="tn", out_dtypes=[BF16], name="mm_in_dw", bn=768, out_stack=N_CHIPS)
    in_landed = exchange("in", [gw_in], gw_in)
    du = _matmul(dproj, wts["w_in"], mode="nt", out_dtypes=[F32], name="mm_in_dx", bk=768)
    grad_x, dg_attn = _rms_bwd(x, du, small["attn_norm_g"], dh1, name="norm_attn_bwd", want_bf16=False)
    for n, parts in (("w_down", parts_down), ("w_up", parts_up), ("w_out", parts_out)):
        big[n] = update(n, parts)
    done = dg_attn + sum(big[n][0][0, :1, :] for n in ("w_down", "w_up", "w_out"))
    (parts_in,) = _comm_only("forward_w_in", in_landed(done))
    big["w_in"] = update("w_in", parts_in)

    small_g = {
        "attn_norm_g": dg_attn, "mlp_norm_g": dg_mlp, "ple_norm_g": dg_ple, "gate_norm_g": dg_gate,
        "final_norm_g": dg_final, "q_norm_g": dg_q, "k_norm_g": dg_k,
        "sink_logits": dsink[:, 0, 0][None, :], "rel_bias_table": dtab[:, :, 0].T,
    }
    return loss, grad_x, big, small_g


_SMALL_ROWS = ["attn_norm_g", "mlp_norm_g", "ple_norm_g", "gate_norm_g", "final_norm_g"]
_PACK_ROWS = 8


def _pack_small(vals, d):
    rows = [vals[n].reshape(1, d) for n in _SMALL_ROWS]
    misc = jnp.concatenate([
        vals["q_norm_g"].reshape(1, HEAD_DIM), vals["k_norm_g"].reshape(1, HEAD_DIM),
        jnp.pad(vals["sink_logits"].reshape(1, N_HEADS_B), ((0, 0), (0, HEAD_DIM - N_HEADS_B))),
        vals["rel_bias_table"].reshape(1, N_BUCKETS * N_HEADS_B)], axis=1)
    rows.append(jnp.pad(misc, ((0, 0), (0, d - misc.shape[1]))))
    rows.append(jnp.zeros((_PACK_ROWS - len(rows), d), F32))
    return jnp.concatenate(rows, axis=0).astype(F32)


def _unpack_small(pack, shapes):
    out = {n: pack[i].reshape(shapes[n]) for i, n in enumerate(_SMALL_ROWS)}
    misc = pack[len(_SMALL_ROWS)]
    out["q_norm_g"] = misc[:HEAD_DIM].reshape(shapes["q_norm_g"])
    out["k_norm_g"] = misc[HEAD_DIM:2 * HEAD_DIM].reshape(shapes["k_norm_g"])
    out["sink_logits"] = misc[2 * HEAD_DIM:2 * HEAD_DIM + N_HEADS_B].reshape(shapes["sink_logits"])
    out["rel_bias_table"] = misc[3 * HEAD_DIM:3 * HEAD_DIM + N_BUCKETS * N_HEADS_B].reshape(shapes["rel_bias_table"])
    return out


_WEIGHTS = ["attn_norm_g", "w_in", "q_norm_g", "k_norm_g", "sink_logits", "w_out", "mlp_norm_g", "w_up", "w_down",
            "ple_w", "ple_norm_g", "gate_norm_g", "w_gate", "rel_bias_table", "final_norm_g"]
_BIG = ["w_in", "w_out", "w_up", "w_down", "ple_w", "w_gate"]


def kernel(x, p, attn_norm_g, w_in, q_norm_g, k_norm_g, sink_logits, w_out, mlp_norm_g, w_up, w_down, ple_w, ple_norm_g, gate_norm_g, w_gate, rel_bias_table, final_norm_g, loss_target, m_attn_norm_g, m_w_in, m_q_norm_g, m_k_norm_g, m_sink_logits, m_w_out, m_mlp_norm_g, m_w_up, m_w_down, m_ple_w, m_ple_norm_g, m_gate_norm_g, m_w_gate, m_rel_bias_table, m_final_norm_g, v_attn_norm_g, v_w_in, v_q_norm_g, v_k_norm_g, v_sink_logits, v_w_out, v_mlp_norm_g, v_w_up, v_w_down, v_ple_w, v_ple_norm_g, v_gate_norm_g, v_w_gate, v_rel_bias_table, v_final_norm_g):
    given = dict(locals())
    w = {n: given[n] for n in _WEIGHTS}
    m = {n: given["m_" + n] for n in _WEIGHTS}
    v = {n: given["v_" + n] for n in _WEIGHTS}
    d = x.shape[-1]

    shards = {n: w[n][0].astype(BF16) for n in _BIG}
    small = {
        "attn_norm_g": w["attn_norm_g"], "mlp_norm_g": w["mlp_norm_g"], "ple_norm_g": w["ple_norm_g"],
        "gate_norm_g": w["gate_norm_g"], "final_norm_g": w["final_norm_g"].reshape(1, d),
        "q_norm_g": w["q_norm_g"], "k_norm_g": w["k_norm_g"], "sink_logits": w["sink_logits"],
        "rel_bias_table": w["rel_bias_table"],
    }

    def update(n, parts):
        res = _sum_adamw(parts, w[n][0], m[n][0], v[n][0], name="adamw_" + n)
        return [t.reshape(w[n].shape) for t in res]

    loss_part, grad_x, big, small_g = _local_step(x[0], p[0, 0], loss_target[0], shards, small, update)
    loss = lax.psum(loss_part[0, 0], ("x", "y", "c"))
    grads, deltas, new_m, new_v = [{n: big[n][i] for n in _BIG} for i in range(4)]

    shapes = {n: w[n].shape for n in _WEIGHTS if n not in _BIG}
    pack = _pack_small(small_g, d)
    pack = pack.at[_PACK_ROWS - 1, :1].add(0.0 * grads["w_in"][0, 0, :1])
    g_small = _allreduce_small(pack)
    d_small, m_small, v_small = _adamw_small(g_small, _pack_small(w, d), _pack_small(m, d), _pack_small(v, d))
    grads.update(_unpack_small(g_small, shapes))
    deltas.update(_unpack_small(d_small, shapes))
    new_m.update(_unpack_small(m_small, shapes))
    new_v.update(_unpack_small(v_small, shapes))

    return (loss, grad_x[None], *[grads[n] for n in _WEIGHTS], *[deltas[n] for n in _WEIGHTS],
            *[new_m[n] for n in _WEIGHTS], *[new_v[n] for n in _WEIGHTS])
```

```python
import functools
import math

import jax
import jax.numpy as jnp
import numpy as np
from jax import lax
from jax.experimental import pallas as pl
from jax.experimental.pallas import tpu as pltpu

F32 = jnp.float32
BF16 = jnp.bfloat16

HEAD_DIM = 128
N_HEADS_A = 8
N_KV_A = 2
N_HEADS_B = 8
N_KV_B = 2
GROUP = 4
GRID_W = 64
BLOCK_Q = 128
WINDOW = 128
N_BUCKETS = 32
MAX_DISTANCE = 128
ROPE_THETA = 10000.0
EPS = 1e-6
NEG_INF = -1e30
ATT_SCALE = HEAD_DIM ** -0.5
LOG2E = math.log2(math.e)
LN2 = math.log(2.0)
Q_SCALE = ATT_SCALE * LOG2E
PAD_LO, PAD_HI = 256, 128

ADAM_LR = 0.001
ADAM_B1 = 0.9
ADAM_B2 = 0.999
ADAM_EPS = 1e-08
ADAM_WD = 0.01
ADAM_STEP = 10

N_CHIPS = 4
N_DEV = 8
COL_QA, COL_KA, COL_VA, COL_QB, COL_KB, COL_VB = 0, 8, 10, 12, 20, 22
N_COLS = 24

VMEM_LIMIT = 52 * 1024 * 1024


def _params(sem=None, collective_id=None):
    return pltpu.CompilerParams(dimension_semantics=sem, vmem_limit_bytes=VMEM_LIMIT, collective_id=collective_id)


_ANY = pl.BlockSpec(memory_space=pl.ANY)
_MESH = pl.DeviceIdType.MESH
SIBLING_BARRIER_ID = 1


def _sibling():
    return (lax.axis_index("x"), lax.axis_index("y"), 1 - lax.axis_index("c"))


class _Comm:
    def __init__(self, inputs, out_shapes, sems, start, finish, aliases=None):
        self.inputs, self.out_shapes, self.sems = list(inputs), list(out_shapes), list(sems)
        self.start, self.finish, self.aliases = start, finish, dict(aliases or {})


def _call(body, *, name, grid, in_specs, out_specs, out_shape, args, scratch_shapes=(), sem=None, comm=None,
          after=None):
    in_specs, out_specs, out_shape = list(in_specs), list(out_specs), list(out_shape)
    scratch_shapes = list(scratch_shapes)
    n_in, n_out, n_sc = len(in_specs), len(out_specs), len(scratch_shapes)
    behind = [] if after is None else [after]
    if comm is None:
        res = pl.pallas_call(
            (lambda *refs: body(*refs[:n_in], *refs[n_in + len(behind):])) if behind else body,
            name=name, grid=grid, in_specs=in_specs + [_ANY] * len(behind), out_specs=out_specs,
            out_shape=out_shape, scratch_shapes=scratch_shapes, compiler_params=_params(sem))(*args, *behind)
        return list(res), []
    assert not behind
    c_in, c_out = len(comm.inputs), len(comm.out_shapes)

    def hosted(*refs):
        pos = [0]

        def take(n):
            pos[0] += n
            return refs[pos[0] - n:pos[0]]

        ins, c_ins, outs, c_outs, scr = take(n_in), take(c_in), take(n_out), take(c_out), take(n_sc)
        c_sems = refs[pos[0]:]
        ids = [pl.program_id(a) for a in range(len(grid))]
        first = functools.reduce(jnp.logical_and, [i == 0 for i in ids])
        last = functools.reduce(jnp.logical_and, [i == g - 1 for i, g in zip(ids, grid)])

        @pl.when(first)
        def _():
            barrier = pltpu.get_barrier_semaphore()
            pl.semaphore_signal(barrier, inc=1, device_id=_sibling(), device_id_type=_MESH)
            pl.semaphore_wait(barrier, 1)
            comm.start(c_ins, c_outs, c_sems)

        body(*ins, *outs, *scr)

        @pl.when(last)
        def _():
            comm.finish(c_ins, c_outs, c_sems)

    res = pl.pallas_call(
        hosted, name=name, grid=grid, in_specs=in_specs + [_ANY] * c_in, out_specs=out_specs + [_ANY] * c_out,
        out_shape=out_shape + comm.out_shapes, scratch_shapes=scratch_shapes + comm.sems,
        input_output_aliases={n_in + i: n_out + o for i, o in comm.aliases.items()},
        compiler_params=_params(("arbitrary",) * len(grid), SIBLING_BARRIER_ID))(*args, *comm.inputs)
    return list(res[:n_out]), list(res[n_out:])


def _matmul(a, b, *, mode, out_dtypes, name, epilogue=None, extras=(), bm=1024, bn=1024, bk=2048,
            out_stack=0, comm=None, after=None):
    stacked = b.ndim == 3
    if mode == "nn":
        m, k = a.shape
        if stacked:
            nj, kb, ns = b.shape
            n, ks = nj * ns, k
        else:
            kb, n = b.shape
            ns, ks = n, k
        dn = (((1,), (0,)), ((), ()))
    elif mode == "nt":
        m, k = a.shape
        if stacked:
            nj, n, ks = b.shape
            kb = nj * ks
        else:
            n, kb = b.shape
            ks = kb
        ns = n
        dn = (((1,), (1,)), ((), ()))
    else:
        k, m = a.shape
        kb, n = b.shape
        ns, ks = n, k
        dn = (((0,), (0,)), ((), ()))
    assert k == kb and not (stacked and mode == "tn")
    ns_out = n // out_stack if out_stack else n
    bm, bn, bk = min(bm, m), min(bn, ns, ns_out), min(bk, ks)
    assert m % bm == 0 and ns % bn == 0 and ns_out % bn == 0 and ks % bk == 0
    gm, gn, gk = m // bm, n // bn, k // bk

    if mode == "tn":
        a_spec = pl.BlockSpec((bk, bm), lambda i, j, q: (q, i))
    else:
        a_spec = pl.BlockSpec((bm, bk), lambda i, j, q: (i, q))
    if mode == "nt":
        if stacked:
            per = ks // bk
            b_spec = pl.BlockSpec((None, bn, bk), lambda i, j, q: (q // per, j, q % per))
        else:
            b_spec = pl.BlockSpec((bn, bk), lambda i, j, q: (j, q))
    else:
        if stacked:
            per = ns // bn
            b_spec = pl.BlockSpec((None, bk, bn), lambda i, j, q: (j // per, q, j % per))
        else:
            b_spec = pl.BlockSpec((bk, bn), lambda i, j, q: (q, j))
    ex_spec = pl.BlockSpec((bm, bn), lambda i, j, q: (i, j))
    if out_stack:
        per_o = ns_out // bn
        o_spec = pl.BlockSpec((None, bm, bn), lambda i, j, q: (j // per_o, i, j % per_o))
        o_shape = (out_stack, m, ns_out)
    else:
        o_spec = ex_spec
        o_shape = (m, n)
    n_ex, n_out = len(extras), len(out_dtypes)

    def body(a_ref, b_ref, *rest):
        ex, outs = rest[:n_ex], rest[n_ex:n_ex + n_out]
        part = lax.dot_general(a_ref[...], b_ref[...], dn, preferred_element_type=F32)

        def finish(acc):
            res = epilogue(acc, *[e[...] for e in ex]) if epilogue else (acc,)
            for o, r in zip(outs, res):
                o[...] = r.astype(o.dtype)

        if gk == 1:
            finish(part)
        else:
            acc_ref = rest[-1]
            q = pl.program_id(2)

            @pl.when(q == 0)
            def _():
                acc_ref[...] = part

            @pl.when(q > 0)
            def _():
                acc_ref[...] += part

            @pl.when(q == gk - 1)
            def _():
                finish(acc_ref[...])

    res, c_res = _call(
        body, name=name, grid=(gm, gn, gk),
        in_specs=[a_spec, b_spec] + [ex_spec] * n_ex,
        out_specs=[o_spec] * n_out,
        out_shape=[jax.ShapeDtypeStruct(o_shape, dt) for dt in out_dtypes],
        scratch_shapes=[pltpu.VMEM((bm, bn), F32)] if gk > 1 else [],
        sem=("parallel", "parallel", "arbitrary"), args=(a, b, *extras), comm=comm, after=after)
    res = res[0] if n_out == 1 else res
    return res if comm is None else (res, c_res)


def _rms_fwd(x, g, *, name, tm=256, comm=None):
    s, d = x.shape
    tm = min(tm, s)

    def body(x_ref, g_ref, o_ref):
        xf = x_ref[...]
        r = lax.rsqrt(jnp.mean(xf * xf, axis=-1, keepdims=True) + EPS)
        o_ref[...] = (xf * r * g_ref[...]).astype(o_ref.dtype)

    res, c_res = _call(
        body, name=name, grid=(s // tm,),
        in_specs=[pl.BlockSpec((tm, d), lambda i: (i, 0)), pl.BlockSpec((1, d), lambda i: (0, 0))],
        out_specs=[pl.BlockSpec((tm, d), lambda i: (i, 0))],
        out_shape=[jax.ShapeDtypeStruct((s, d), BF16)],
        sem=("parallel",), args=(x, g), comm=comm)
    return res[0] if comm is None else (res[0], c_res)


def _rms_bwd(x, dy, g, add, *, name, want_bf16, tm=256):
    s, d = x.shape
    tm = min(tm, s)

    def body(x_ref, dy_ref, g_ref, add_ref, dx_ref, *rest):
        dg_ref = rest[-1]
        i = pl.program_id(0)
        xf = x_ref[...]
        dyf = dy_ref[...].astype(F32)
        r = lax.rsqrt(jnp.mean(xf * xf, axis=-1, keepdims=True) + EPS)
        xh = xf * r
        dyg = dyf * g_ref[...]
        dx = r * (dyg - xh * jnp.mean(dyg * xh, axis=-1, keepdims=True))
        tot = add_ref[...] + dx
        dx_ref[...] = tot
        if want_bf16:
            rest[0][...] = tot.astype(BF16)
        part = jnp.sum(dyf * xh, axis=0, keepdims=True)

        @pl.when(i == 0)
        def _():
            dg_ref[...] = part

        @pl.when(i > 0)
        def _():
            dg_ref[...] += part

    row = pl.BlockSpec((tm, d), lambda i: (i, 0))
    vec = pl.BlockSpec((1, d), lambda i: (0, 0))
    out_specs = [row] + ([row] if want_bf16 else []) + [vec]
    out_shape = [jax.ShapeDtypeStruct((s, d), F32)]
    if want_bf16:
        out_shape.append(jax.ShapeDtypeStruct((s, d), BF16))
    out_shape.append(jax.ShapeDtypeStruct((1, d), F32))
    return pl.pallas_call(
        body,
        name=name,
        grid=(s // tm,),
        in_specs=[row, row, vec, row],
        out_specs=out_specs,
        out_shape=out_shape,
        compiler_params=_params(("arbitrary",)),
    )(x, dy, g, add)


def _tail(h2, gate, pp, target, g_ple, g_final, *, tm=128):
    s, d = h2.shape
    tm = min(tm, s)

    def body(h2_ref, gate_ref, pp_ref, t_ref, gp_ref, gf_ref, dh3_ref, dz_ref, dpp_ref, dgf_ref, dgp_ref, loss_ref):
        i = pl.program_id(0)
        ppf = pp_ref[...]
        gate_v = gate_ref[...]
        r_p = lax.rsqrt(jnp.mean(ppf * ppf, axis=-1, keepdims=True) + EPS)
        eh = ppf * r_p
        e = eh * gp_ref[...]
        h3 = h2_ref[...] + gate_v * e
        r_f = lax.rsqrt(jnp.mean(h3 * h3, axis=-1, keepdims=True) + EPS)
        yh = h3 * r_f
        diff = yh * gf_ref[...] - t_ref[...]
        loss_part = 0.5 * jnp.sum(jnp.mean(diff * diff, axis=-1, keepdims=True), axis=0, keepdims=True)
        dy = diff / d
        dgf = jnp.sum(dy * yh, axis=0, keepdims=True)
        dyg = dy * gf_ref[...]
        dh3 = r_f * (dyg - yh * jnp.mean(dyg * yh, axis=-1, keepdims=True))
        dh3_ref[...] = dh3
        de = dh3 * gate_v
        dz_ref[...] = (dh3 * e * gate_v * (1.0 - gate_v)).astype(BF16)
        dgp = jnp.sum(de * eh, axis=0, keepdims=True)
        deg = de * gp_ref[...]
        dpp_ref[...] = (r_p * (deg - eh * jnp.mean(deg * eh, axis=-1, keepdims=True))).astype(BF16)
        loss_row = jnp.broadcast_to(loss_part, (1, 128))

        @pl.when(i == 0)
        def _():
            dgf_ref[...] = dgf
            dgp_ref[...] = dgp
            loss_ref[...] = loss_row

        @pl.when(i > 0)
        def _():
            dgf_ref[...] += dgf
            dgp_ref[...] += dgp
            loss_ref[...] += loss_row

    row = pl.BlockSpec((tm, d), lambda i: (i, 0))
    vec = pl.BlockSpec((1, d), lambda i: (0, 0))
    return pl.pallas_call(
        body,
        name="tail_fwd_bwd",
        grid=(s // tm,),
        in_specs=[row, row, row, row, vec, vec],
        out_specs=[row, row, row, vec, vec, pl.BlockSpec((1, 128), lambda i: (0, 0))],
        out_shape=[
            jax.ShapeDtypeStruct((s, d), F32),
            jax.ShapeDtypeStruct((s, d), BF16),
            jax.ShapeDtypeStruct((s, d), BF16),
            jax.ShapeDtypeStruct((1, d), F32),
            jax.ShapeDtypeStruct((1, d), F32),
            jax.ShapeDtypeStruct((1, 128), F32),
        ],
        compiler_params=_params(("arbitrary",)),
    )(h2, gate, pp, target, g_ple, g_final)


def _rope_tables(s):
    t = jnp.arange(s, dtype=jnp.int32)
    row = (t // GRID_W).astype(F32)
    col = (t % GRID_W).astype(F32)
    half = HEAD_DIM // 2
    inv_freq = ROPE_THETA ** (-jnp.arange(0, half, 2, dtype=F32) / half)
    ang_r = row[:, None] * inv_freq
    ang_c = col[:, None] * inv_freq
    cr, sr, cc, sc = jnp.cos(ang_r), jnp.sin(ang_r), jnp.cos(ang_c), jnp.sin(ang_c)
    cos_t = jnp.concatenate([cr, cr, cc, cc], axis=-1)
    sin_t = jnp.concatenate([-sr, sr, -sc, sc], axis=-1)
    return cos_t, sin_t


def _swap_quarters(x):
    lane = lax.broadcasted_iota(jnp.int32, x.shape, x.ndim - 1)
    up = pltpu.roll(x, HEAD_DIM - 32, x.ndim - 1)
    down = pltpu.roll(x, 32, x.ndim - 1)
    return jnp.where((lane % 64) < 32, up, down)


def _cols(first, count=1):
    return slice(first * HEAD_DIM, (first + count) * HEAD_DIM)


def _qk_prep(proj, g_q, g_k, cos_t, sin_t, *, tm=256, comm=None):
    s, n = proj.shape
    tm = min(tm, s)

    def body(x_ref, gq_ref, gk_ref, c_ref, s_ref, o_ref):
        cos_v, sin_v = c_ref[...], s_ref[...]
        for h in range(COL_VA):
            x = x_ref[:, _cols(h)]
            g = gq_ref[...] if h < COL_KA else gk_ref[...]
            xn = x * lax.rsqrt(jnp.mean(x * x, axis=-1, keepdims=True) + EPS) * g
            xr = xn * cos_v + _swap_quarters(xn) * sin_v
            if h < COL_KA:
                xr = xr * Q_SCALE
            o_ref[:, _cols(h)] = xr.astype(BF16)
        o_ref[:, _cols(COL_VA, 2)] = x_ref[:, _cols(COL_VA, 2)].astype(BF16)
        o_ref[:, _cols(COL_QB, N_HEADS_B)] = (x_ref[:, _cols(COL_QB, N_HEADS_B)] * Q_SCALE).astype(BF16)
        o_ref[:, _cols(COL_KB, 4)] = x_ref[:, _cols(COL_KB, 4)].astype(BF16)

    row = pl.BlockSpec((tm, n), lambda i: (i, 0))
    tab = pl.BlockSpec((tm, HEAD_DIM), lambda i: (i, 0))
    vec = pl.BlockSpec((1, HEAD_DIM), lambda i: (0, 0))
    res, c_res = _call(
        body, name="qk_prep", grid=(s // tm,),
        in_specs=[row, vec, vec, tab, tab],
        out_specs=[row],
        out_shape=[jax.ShapeDtypeStruct((s, n), BF16)],
        sem=("parallel",), args=(proj, g_q, g_k, cos_t, sin_t), comm=comm)
    return res[0] if comm is None else (res[0], c_res)


def _qk_bwd(dqa, dka, dva, dqb, dkpad, dvpad, proj, g_q, g_k, cos_t, sin_t, *, comm=None):
    s, n = proj.shape
    tm = min(PAD_LO, s)
    assert PAD_LO % tm == 0
    lo = PAD_LO // tm

    def body(dqa_ref, dka_ref, dva_ref, dqb_ref, dkb_ref, dvb_ref, x_ref, gq_ref, gk_ref, c_ref, s_ref,
             o_ref, dgq_ref, dgk_ref):
        i = pl.program_id(0)
        cos_v, sin_v = c_ref[...], s_ref[...]

        def head(d, x, g):
            dn = d * cos_v + _swap_quarters(d * sin_v)
            r = lax.rsqrt(jnp.mean(x * x, axis=-1, keepdims=True) + EPS)
            xh = x * r
            dng = dn * g
            dx = r * (dng - xh * jnp.mean(dng * xh, axis=-1, keepdims=True))
            return dx.astype(BF16), jnp.sum(dn * xh, axis=0, keepdims=True)

        acc_q = jnp.zeros((1, HEAD_DIM), F32)
        acc_k = jnp.zeros((1, HEAD_DIM), F32)
        for h in range(N_HEADS_A):
            o_ref[:, _cols(h)], part = head(dqa_ref[:, _cols(h)] * ATT_SCALE, x_ref[:, _cols(h)], gq_ref[...])
            acc_q = acc_q + part
        for h in range(N_KV_A):
            o_ref[:, _cols(COL_KA + h)], part = head(dka_ref[:, _cols(h)] * LN2, x_ref[:, _cols(COL_KA + h)],
                                                     gk_ref[...])
            acc_k = acc_k + part
        o_ref[:, _cols(COL_VA, 2)] = dva_ref[...].astype(BF16)
        o_ref[:, _cols(COL_QB, N_HEADS_B)] = (dqb_ref[...] * ATT_SCALE).astype(BF16)
        o_ref[:, _cols(COL_KB, 2)] = (dkb_ref[...] * LN2).astype(BF16)
        o_ref[:, _cols(COL_VB, 2)] = dvb_ref[...].astype(BF16)

        @pl.when(i == 0)
        def _():
            dgq_ref[...] = acc_q
            dgk_ref[...] = acc_k

        @pl.when(i > 0)
        def _():
            dgq_ref[...] += acc_q
            dgk_ref[...] += acc_k

    def rows(width, shift=0):
        return pl.BlockSpec((tm, width), lambda i: (i + shift, 0))

    kv_w = N_KV_A * HEAD_DIM
    q_w = N_HEADS_A * HEAD_DIM
    vec = pl.BlockSpec((1, HEAD_DIM), lambda i: (0, 0))
    res, c_res = _call(
        body, name="qk_bwd", grid=(s // tm,),
        in_specs=[rows(q_w), rows(kv_w), rows(kv_w), rows(q_w), rows(kv_w, lo), rows(kv_w, lo), rows(n),
                  vec, vec, rows(HEAD_DIM), rows(HEAD_DIM)],
        out_specs=[rows(n), vec, vec],
        out_shape=[
            jax.ShapeDtypeStruct((s, n), BF16),
            jax.ShapeDtypeStruct((1, HEAD_DIM), F32),
            jax.ShapeDtypeStruct((1, HEAD_DIM), F32),
        ],
        sem=("arbitrary",), args=(dqa, dka, dva, dqb, dkpad, dvpad, proj, g_q, g_k, cos_t, sin_t), comm=comm)
    return res if comm is None else (res, c_res)


_NT = (((1,), (1,)), ((), ()))
_TN = (((0,), (0,)), ((), ()))


def _attn_a_fwd(pb, *, tq=512, sub=256, comm=None, after=None):
    s = pb.shape[0]
    tq = min(tq, s)

    sub = min(sub, tq)

    def body(q_ref, k_ref, v_ref, o_ref, lse_ref):
        k = k_ref[...]
        v = v_ref[...]
        for r in range(tq // sub):
            rows = pl.ds(r * sub, sub)
            sc = lax.dot_general(q_ref[rows, :], k, _NT, preferred_element_type=F32)
            m = jnp.max(sc, axis=-1, keepdims=True)
            p = jnp.exp2(sc - m)
            l = jnp.sum(p, axis=-1, keepdims=True)
            o = jnp.dot(p.astype(BF16), v, preferred_element_type=F32)
            o_ref[rows, :] = (o / l).astype(BF16)
            lse_ref[rows, :] = jnp.broadcast_to(m + jnp.log2(l), (sub, HEAD_DIM))

    res, c_res = _call(
        body, name="attn_a_fwd", grid=(N_HEADS_A, s // tq),
        in_specs=[
            pl.BlockSpec((tq, HEAD_DIM), lambda h, i: (i, COL_QA + h)),
            pl.BlockSpec((s, HEAD_DIM), lambda h, i: (0, COL_KA + h // GROUP)),
            pl.BlockSpec((s, HEAD_DIM), lambda h, i: (0, COL_VA + h // GROUP)),
        ],
        out_specs=[
            pl.BlockSpec((tq, HEAD_DIM), lambda h, i: (i, h)),
            pl.BlockSpec((None, tq, HEAD_DIM), lambda h, i: (h, i, 0)),
        ],
        out_shape=[
            jax.ShapeDtypeStruct((s, N_HEADS_A * HEAD_DIM), BF16),
            jax.ShapeDtypeStruct((N_HEADS_A, s, HEAD_DIM), F32),
        ],
        sem=("parallel", "parallel"), args=(pb, pb, pb), comm=comm, after=after)
    return res if comm is None else (res, c_res)


def _attn_a_bwd(pb, att, datt, lse, *, tq=256, sub=128, comm=None):
    s = pb.shape[0]
    tq = min(tq, s)
    sub = min(sub, tq)

    def body(q_ref, k_ref, v_ref, o_ref, do_ref, lse_ref, dq_ref, dk_ref, dv_ref):
        first = jnp.logical_and(pl.program_id(1) == 0, pl.program_id(2) == 0)
        k = k_ref[...]
        v = v_ref[...]
        dk = dv = None
        for r in range(tq // sub):
            rows = pl.ds(r * sub, sub)
            q = q_ref[rows, :]
            do = do_ref[rows, :]
            sc = lax.dot_general(q, k, _NT, preferred_element_type=F32)
            p = jnp.exp2(sc - lse_ref[rows, :][:, :1])
            dp = lax.dot_general(do, v, _NT, preferred_element_type=F32)
            delta = jnp.sum(do.astype(F32) * o_ref[rows, :].astype(F32), axis=-1, keepdims=True)
            ds = (p * (dp - delta)).astype(BF16)
            dq_ref[rows, :] = jnp.dot(ds, k, preferred_element_type=F32)
            dk_r = lax.dot_general(ds, q, _TN, preferred_element_type=F32)
            dv_r = lax.dot_general(p.astype(BF16), do, _TN, preferred_element_type=F32)
            dk = dk_r if dk is None else dk + dk_r
            dv = dv_r if dv is None else dv + dv_r

        @pl.when(first)
        def _():
            dk_ref[...] = dk
            dv_ref[...] = dv

        @pl.when(jnp.logical_not(first))
        def _():
            dk_ref[...] += dk
            dv_ref[...] += dv

    qmap = lambda kv, g, i: (i, kv * GROUP + g)
    res, c_res = _call(
        body, name="attn_a_bwd", grid=(N_KV_A, GROUP, s // tq),
        in_specs=[
            pl.BlockSpec((tq, HEAD_DIM), lambda kv, g, i: (i, COL_QA + kv * GROUP + g)),
            pl.BlockSpec((s, HEAD_DIM), lambda kv, g, i: (0, COL_KA + kv)),
            pl.BlockSpec((s, HEAD_DIM), lambda kv, g, i: (0, COL_VA + kv)),
            pl.BlockSpec((tq, HEAD_DIM), qmap),
            pl.BlockSpec((tq, HEAD_DIM), qmap),
            pl.BlockSpec((None, tq, HEAD_DIM), lambda kv, g, i: (kv * GROUP + g, i, 0)),
        ],
        out_specs=[
            pl.BlockSpec((tq, HEAD_DIM), qmap),
            pl.BlockSpec((s, HEAD_DIM), lambda kv, g, i: (0, kv)),
            pl.BlockSpec((s, HEAD_DIM), lambda kv, g, i: (0, kv)),
        ],
        out_shape=[
            jax.ShapeDtypeStruct((s, N_HEADS_A * HEAD_DIM), F32),
            jax.ShapeDtypeStruct((s, N_KV_A * HEAD_DIM), F32),
            jax.ShapeDtypeStruct((s, N_KV_A * HEAD_DIM), F32),
        ],
        sem=("arbitrary", "arbitrary", "arbitrary"), args=(pb, pb, pb, att, datt, lse), comm=comm)
    return res if comm is None else (res, c_res)


def _t5_bucket(rel):
    nb = N_BUCKETS // 2
    ret = jnp.where(rel > 0, nb, 0)
    n = jnp.abs(rel)
    max_exact = nb // 2
    nf = jnp.maximum(n, 1).astype(F32)
    large = max_exact + (jnp.log(nf / max_exact) / math.log(MAX_DISTANCE / max_exact)
                         * (nb - max_exact)).astype(jnp.int32)
    large = jnp.minimum(large, nb - 1)
    return ret + jnp.where(n < max_exact, n, large)


def _band_buckets():
    r = jnp.arange(BLOCK_Q, dtype=jnp.int32)
    j = jnp.arange(3 * BLOCK_Q, dtype=jnp.int32)
    return _t5_bucket((j[None, :] - BLOCK_Q) - r[:, None])


def _band_bias(bucket, table_ref, h):
    acc = jnp.zeros(bucket.shape, F32)
    for b in range(N_BUCKETS):
        acc = jnp.where(bucket == b, table_ref[b, h], acc)
    return acc


GQ = GROUP * BLOCK_Q


def _stack_heads(x):
    return jnp.concatenate([x[:, _cols(g)] for g in range(GROUP)], axis=0)


def _unstack_heads(x):
    return jnp.concatenate([x[g * BLOCK_Q:(g + 1) * BLOCK_Q] for g in range(GROUP)], axis=1)


def _group_bias(bucket, table_ref, kv):
    return jnp.concatenate([_band_bias(bucket, table_ref, kv * GROUP + g) * LOG2E for g in range(GROUP)], axis=0)


def _group_sink(sink_ref, kv):
    head = lax.broadcasted_iota(jnp.int32, (GQ, 1), 0) // BLOCK_Q
    snk = jnp.zeros((GQ, 1), F32)
    for g in range(GROUP):
        snk = jnp.where(head == g, sink_ref[0, kv * GROUP + g] * LOG2E, snk)
    return snk


def _band_mask(n, s):
    r = lax.broadcasted_iota(jnp.int32, (GQ, 3 * BLOCK_Q), 0) % BLOCK_Q
    j = lax.broadcasted_iota(jnp.int32, (GQ, 3 * BLOCK_Q), 1)
    rel = j - BLOCK_Q - r
    kabs = n * BLOCK_Q + j - BLOCK_Q
    return (jnp.abs(rel) <= WINDOW) & (kabs >= 0) & (kabs < s)


def _band_start(n):
    return pl.multiple_of(n * BLOCK_Q + (PAD_LO - BLOCK_Q), BLOCK_Q)


def _attn_b_fwd(pb, kpad, vpad, bucket, table, sink, *, comm=None):
    s = pb.shape[0]
    nblk = s // BLOCK_Q
    sp = kpad.shape[0]

    def body(table_ref, sink_ref, q_ref, k_ref, v_ref, bucket_ref, o_ref, lse_ref, bias_ref):
        kv = pl.program_id(0)
        n = pl.program_id(1)

        @pl.when(n == 0)
        def _():
            bias_ref[...] = _group_bias(bucket_ref[...], table_ref, kv)

        band = pl.ds(_band_start(n), 3 * BLOCK_Q)
        kb = k_ref[band, :]
        vb = v_ref[band, :]
        sc = lax.dot_general(_stack_heads(q_ref[...]), kb, _NT, preferred_element_type=F32) + bias_ref[...]
        sc = jnp.where(_band_mask(n, s), sc, NEG_INF)
        snk = _group_sink(sink_ref, kv)
        m = jnp.maximum(jnp.max(sc, axis=-1, keepdims=True), snk)
        p = jnp.exp2(sc - m)
        l = jnp.sum(p, axis=-1, keepdims=True) + jnp.exp2(snk - m)
        o = jnp.dot(p.astype(BF16), vb, preferred_element_type=F32)
        o_ref[...] = _unstack_heads((o / l).astype(BF16))
        lse = m + jnp.log2(l)
        for g in range(GROUP):
            lse_ref[g] = jnp.broadcast_to(lse[g * BLOCK_Q:(g + 1) * BLOCK_Q], (BLOCK_Q, HEAD_DIM))

    smem = pl.BlockSpec(memory_space=pltpu.SMEM)
    res, c_res = _call(
        body, name="attn_b_fwd", grid=(N_KV_B, nblk),
        in_specs=[
            smem,
            smem,
            pl.BlockSpec((BLOCK_Q, GROUP * HEAD_DIM), lambda kv, n: (n, COL_QB // GROUP + kv)),
            pl.BlockSpec((sp, HEAD_DIM), lambda kv, n: (0, kv)),
            pl.BlockSpec((sp, HEAD_DIM), lambda kv, n: (0, kv)),
            pl.BlockSpec((BLOCK_Q, 3 * BLOCK_Q), lambda kv, n: (0, 0)),
        ],
        out_specs=[
            pl.BlockSpec((BLOCK_Q, GROUP * HEAD_DIM), lambda kv, n: (n, kv)),
            pl.BlockSpec((GROUP, BLOCK_Q, HEAD_DIM), lambda kv, n: (kv, n, 0)),
        ],
        out_shape=[
            jax.ShapeDtypeStruct((s, N_HEADS_B * HEAD_DIM), BF16),
            jax.ShapeDtypeStruct((N_HEADS_B, s, HEAD_DIM), F32),
        ],
        scratch_shapes=[pltpu.VMEM((GQ, 3 * BLOCK_Q), F32)],
        sem=("arbitrary", "arbitrary"), args=(table, sink, pb, kpad, vpad, bucket), comm=comm)
    return res if comm is None else (res, c_res)


def _attn_b_bwd(pb, kpad, vpad, att, datt, lse, bucket, table, sink, *, comm=None, after=None):
    s = pb.shape[0]
    nblk = s // BLOCK_Q
    sp = kpad.shape[0]

    def body(table_ref, sink_ref, q_ref, k_ref, v_ref, o_ref, do_ref, lse_ref, bucket_ref,
             dq_ref, dk_ref, dv_ref, dtab_ref, dsink_ref, bias_ref, dbias_ref):
        kv = pl.program_id(0)
        n = pl.program_id(1)

        @pl.when(n == 0)
        def _():
            dk_ref[...] = jnp.zeros_like(dk_ref)
            dv_ref[...] = jnp.zeros_like(dv_ref)
            bias_ref[...] = _group_bias(bucket_ref[...], table_ref, kv)
            dbias_ref[...] = jnp.zeros_like(dbias_ref)
            dsink_ref[...] = jnp.zeros_like(dsink_ref)

        band = pl.ds(_band_start(n), 3 * BLOCK_Q)
        q = _stack_heads(q_ref[...])
        do = _stack_heads(do_ref[...])
        o = _stack_heads(o_ref[...])
        kb = k_ref[band, :]
        vb = v_ref[band, :]
        lse = jnp.concatenate([lse_ref[g][:, :1] for g in range(GROUP)], axis=0)
        sc = lax.dot_general(q, kb, _NT, preferred_element_type=F32) + bias_ref[...]
        sc = jnp.where(_band_mask(n, s), sc, NEG_INF)
        p = jnp.exp2(sc - lse)
        dp = lax.dot_general(do, vb, _NT, preferred_element_type=F32)
        delta = jnp.sum(do.astype(F32) * o.astype(F32), axis=-1, keepdims=True)
        ds = p * (dp - delta)
        dsb = ds.astype(BF16)
        dq_ref[...] = _unstack_heads(jnp.dot(dsb, kb, preferred_element_type=F32))
        dk_ref[band, :] += lax.dot_general(dsb, q, _TN, preferred_element_type=F32)
        dv_ref[band, :] += lax.dot_general(p.astype(BF16), do, _TN, preferred_element_type=F32)
        dbias_ref[...] += ds
        sink_part = -jnp.exp2(_group_sink(sink_ref, kv) - lse) * delta
        for g in range(GROUP):
            rows = slice(g * BLOCK_Q, (g + 1) * BLOCK_Q)
            dsink_ref[g] += jnp.broadcast_to(jnp.sum(sink_part[rows], axis=0, keepdims=True), (1, HEAD_DIM))

        @pl.when(n == nblk - 1)
        def _():
            bucket_v = bucket_ref[...]
            row = lax.broadcasted_iota(jnp.int32, (N_BUCKETS, HEAD_DIM), 0)
            for g in range(GROUP):
                acc = dbias_ref[g * BLOCK_Q:(g + 1) * BLOCK_Q, :]
                tot = jnp.zeros((N_BUCKETS, HEAD_DIM), F32)
                for b in range(N_BUCKETS):
                    tot = jnp.where(row == b, jnp.sum(jnp.where(bucket_v == b, acc, 0.0), keepdims=True), tot)
                dtab_ref[g] = tot

    smem = pl.BlockSpec(memory_space=pltpu.SMEM)
    wide = GROUP * HEAD_DIM
    res, c_res = _call(
        body, name="attn_b_bwd", grid=(N_KV_B, nblk),
        in_specs=[
            smem,
            smem,
            pl.BlockSpec((BLOCK_Q, wide), lambda kv, n: (n, COL_QB // GROUP + kv)),
            pl.BlockSpec((sp, HEAD_DIM), lambda kv, n: (0, kv)),
            pl.BlockSpec((sp, HEAD_DIM), lambda kv, n: (0, kv)),
            pl.BlockSpec((BLOCK_Q, wide), lambda kv, n: (n, N_HEADS_A // GROUP + kv)),
            pl.BlockSpec((BLOCK_Q, wide), lambda kv, n: (n, N_HEADS_A // GROUP + kv)),
            pl.BlockSpec((GROUP, BLOCK_Q, HEAD_DIM), lambda kv, n: (kv, n, 0)),
            pl.BlockSpec((BLOCK_Q, 3 * BLOCK_Q), lambda kv, n: (0, 0)),
        ],
        out_specs=[
            pl.BlockSpec((BLOCK_Q, wide), lambda kv, n: (n, kv)),
            pl.BlockSpec((sp, HEAD_DIM), lambda kv, n: (0, kv)),
            pl.BlockSpec((sp, HEAD_DIM), lambda kv, n: (0, kv)),
            pl.BlockSpec((GROUP, N_BUCKETS, HEAD_DIM), lambda kv, n: (kv, 0, 0)),
            pl.BlockSpec((GROUP, 1, HEAD_DIM), lambda kv, n: (kv, 0, 0)),
        ],
        out_shape=[
            jax.ShapeDtypeStruct((s, N_HEADS_B * HEAD_DIM), F32),
            jax.ShapeDtypeStruct((sp, N_KV_B * HEAD_DIM), F32),
            jax.ShapeDtypeStruct((sp, N_KV_B * HEAD_DIM), F32),
            jax.ShapeDtypeStruct((N_HEADS_B, N_BUCKETS, HEAD_DIM), F32),
            jax.ShapeDtypeStruct((N_HEADS_B, 1, HEAD_DIM), F32),
        ],
        scratch_shapes=[pltpu.VMEM((GQ, 3 * BLOCK_Q), F32), pltpu.VMEM((GQ, 3 * BLOCK_Q), F32)],
        sem=("arbitrary", "arbitrary"),
        args=(table, sink, pb, kpad, vpad, att, datt, lse, bucket), comm=comm, after=after)
    return res if comm is None else (res, c_res)


_MESH = pl.DeviceIdType.MESH


def _other_chips(x, y):
    return [(x, 1 - y), (1 - x, y), (1 - x, 1 - y)]


_HBM = pl.BlockSpec(memory_space=pltpu.HBM)
_SEM = pl.BlockSpec(memory_space=pltpu.SEMAPHORE)
_SPLIT = pltpu.CompilerParams(has_side_effects=pltpu.SideEffectType.DATAFLOW_SIDE_EFFECTING)


def _in_hbm(a):
    return pltpu.with_memory_space_constraint(a, pltpu.HBM)


def _my_half(rows):
    c = lax.axis_index("c")
    half = rows // 2
    return pl.ds(pl.multiple_of(c * half, half), half), pl.ds(pl.multiple_of((1 - c) * half, half), half)


def _gather_route(shapes):
    def route(src, land):
        x, y, c = lax.axis_index("x"), lax.axis_index("y"), lax.axis_index("c")
        out = []
        for t, shape in enumerate(shapes):
            mine, _ = _my_half(shape[0])
            for px, py in _other_chips(x, y):
                out.append((src[t].at[mine], land[t].at[2 * x + y, mine], land[t].at[2 * px + py, mine], (px, py, c)))
        return out

    return route


def _exchange_route(n_t):
    def route(src, land):
        x, y, c = lax.axis_index("x"), lax.axis_index("y"), lax.axis_index("c")
        out = []
        for t in range(n_t):
            for px, py in _other_chips(x, y):
                k = 2 * px + py
                out.append((src[t].at[k], land[t].at[2 * (2 * x + y) + c], land[t].at[2 * k + c], (px, py, c)))
        return out

    return route


def _own_slot(shape, dtype, slot, block):
    return lax.dynamic_update_slice(lax.empty(shape, dtype), block[None], (slot,) + (0,) * (len(shape) - 1))


def _split_start(name, srcs, lands, route, after):
    n = len(srcs)

    def body(*refs):
        src, land, send_sems, recv_sems, token = refs[:n], refs[n:2 * n], refs[2 * n + 1], refs[2 * n + 2], refs[-1]
        for i, (src_ref, dst_ref, _, to) in enumerate(route(src, land)):
            pltpu.make_async_remote_copy(src_ref=src_ref, dst_ref=dst_ref, send_sem=send_sems.at[i],
                                         recv_sem=recv_sems.at[i], device_id=to, device_id_type=_MESH).start()
        token[...] = jnp.zeros_like(token)

    sem = pltpu.SemaphoreType.DMA((3 * n,))
    lands = list(lands)
    res = pl.pallas_call(
        body, name=name,
        in_specs=[_HBM] * (2 * n) + [_ANY],
        out_specs=[_SEM, _SEM] + [_HBM] * (2 * n) + [pl.BlockSpec(memory_space=pltpu.VMEM)],
        out_shape=[sem, sem] + [pltpu.HBM(a.shape, a.dtype) for a in list(srcs) + lands]
        + [jax.ShapeDtypeStruct((8, 128), F32)],
        input_output_aliases={i: 2 + i for i in range(2 * n)},
        compiler_params=_SPLIT,
    )(*[_in_hbm(a) for a in srcs], *[_in_hbm(a) for a in lands], after)
    return (res[0], res[1]), res[2:2 + n], res[2 + n:2 + 2 * n], res[-1]


def _split_wait(name, srcs, lands, sems, route, after):
    n = len(srcs)

    def body(*refs):
        src, land, send_sems, recv_sems = refs[:n], refs[n:2 * n], refs[2 * n], refs[2 * n + 1]
        for i, (src_ref, _, dst_ref, to) in enumerate(route(src, land)):
            cp = pltpu.make_async_remote_copy(src_ref=src_ref, dst_ref=dst_ref, send_sem=send_sems.at[i],
                                              recv_sem=recv_sems.at[i], device_id=to, device_id_type=_MESH)
            cp.wait_send()
            cp.wait_recv()

    res = pl.pallas_call(
        body, name=name,
        in_specs=[_HBM] * (2 * n) + [_SEM, _SEM, _ANY],
        out_specs=[_HBM] * (2 * n),
        out_shape=[pltpu.HBM(a.shape, a.dtype) for a in list(srcs) + list(lands)],
        input_output_aliases={i: i for i in range(2 * n)},
        compiler_params=_SPLIT,
    )(*srcs, *lands, sems[0], sems[1], after)
    return res[:n], res[n:]


def _comm_only(name, comm):
    return _call(lambda: None, name=name, grid=(1,), in_specs=[], out_specs=[], out_shape=[], args=(), comm=comm)[1]


def _swap_comm(shards, lands):
    n_t = len(lands)

    def copies(land, sems, later):
        send_sems, recv_sems = sems
        x, y = lax.axis_index("x"), lax.axis_index("y")
        sends, recvs = [], []
        for t in range(n_t):
            mine, other = _my_half(shards[t].shape[0])
            for j, (px, py) in enumerate(_other_chips(x, y)):
                k = 2 * px + py
                for part, out in ((mine, sends), (other, recvs)) if later else ((mine, sends),):
                    out.append(pltpu.make_async_remote_copy(
                        src_ref=land[t].at[k, part], dst_ref=land[t].at[k, part], send_sem=send_sems.at[3 * t + j],
                        recv_sem=recv_sems.at[3 * t + j], device_id=_sibling(), device_id_type=_MESH))
        return sends, recvs

    def start(ins, land, sems):
        for cp in copies(land, sems, False)[0]:
            cp.start()

    def finish(ins, land, sems):
        sends, recvs = copies(land, sems, True)
        for cp in recvs:
            cp.wait_recv()
        for cp in sends:
            cp.wait_send()

    return _Comm(
        lands, [jax.ShapeDtypeStruct(a.shape, a.dtype) for a in lands],
        [pltpu.SemaphoreType.DMA((3 * n_t,)), pltpu.SemaphoreType.DMA((3 * n_t,))],
        start, finish, aliases={t: t for t in range(n_t)})


def _forward_comm(partials, lands):
    n_t = len(lands)

    def copies(land, sems, later):
        send_sems, recv_sems = sems
        x, y, c = lax.axis_index("x"), lax.axis_index("y"), lax.axis_index("c")
        sends, recvs = [], []
        for t in range(n_t):
            for j, k in enumerate([2 * x + y] + [2 * px + py for px, py in _other_chips(x, y)]):
                for slot, out in ((2 * k + c, sends), (2 * k + 1 - c, recvs)) if later else ((2 * k + c, sends),):
                    out.append(pltpu.make_async_remote_copy(
                        src_ref=land[t].at[slot], dst_ref=land[t].at[slot], send_sem=send_sems.at[4 * t + j],
                        recv_sem=recv_sems.at[4 * t + j], device_id=_sibling(), device_id_type=_MESH))
        return sends, recvs

    def start(ins, land, sems):
        for cp in copies(land, sems, False)[0]:
            cp.start()

    def finish(ins, land, sems):
        sends, recvs = copies(land, sems, True)
        for cp in recvs:
            cp.wait_recv()
        for cp in sends:
            cp.wait_send()

    return _Comm(
        lands, [jax.ShapeDtypeStruct(a.shape, a.dtype) for a in lands],
        [pltpu.SemaphoreType.DMA((4 * n_t,)), pltpu.SemaphoreType.DMA((4 * n_t,))],
        start, finish, aliases={t: t for t in range(n_t)})


def _allreduce_small(pack):
    rows, d = pack.shape

    def body(p_ref, sum_ref, all_ref, send_sems, recv_sems):
        x, y, c = lax.axis_index("x"), lax.axis_index("y"), lax.axis_index("c")
        me = 4 * x + 2 * y + c
        all_ref[me] = p_ref[...]
        peers = []
        for dx in range(2):
            for dy in range(2):
                for dc in range(2):
                    if dx or dy or dc:
                        px = 1 - x if dx else x
                        py = 1 - y if dy else y
                        pc = 1 - c if dc else c
                        peers.append((4 * dx + 2 * dy + dc - 1, (px, py, pc)))
        sends = []
        for k, to in peers:
            cp = pltpu.make_async_remote_copy(
                src_ref=p_ref, dst_ref=all_ref.at[me], send_sem=send_sems.at[k], recv_sem=recv_sems.at[k],
                device_id=to, device_id_type=_MESH)
            cp.start()
            sends.append(cp)
        for k, (px, py, pc) in peers:
            pltpu.make_async_remote_copy(
                src_ref=p_ref, dst_ref=all_ref.at[4 * px + 2 * py + pc], send_sem=send_sems.at[k],
                recv_sem=recv_sems.at[k], device_id=(px, py, pc), device_id_type=_MESH).wait_recv()
        for cp in sends:
            cp.wait_send()
        tot = all_ref[0]
        for i in range(1, N_DEV):
            tot = tot + all_ref[i]
        sum_ref[...] = tot

    vm = pl.BlockSpec(memory_space=pltpu.VMEM)
    return pl.pallas_call(
        body,
        name="allreduce_small",
        in_specs=[vm],
        out_specs=vm,
        out_shape=jax.ShapeDtypeStruct((rows, d), F32),
        scratch_shapes=[
            pltpu.VMEM((N_DEV, rows, d), F32),
            pltpu.SemaphoreType.DMA((N_DEV - 1,)),
            pltpu.SemaphoreType.DMA((N_DEV - 1,)),
        ],
    )(pack)


def _adamw_math(w, g, m, v):
    m = ADAM_B1 * m + (1.0 - ADAM_B1) * g
    v = ADAM_B2 * v + (1.0 - ADAM_B2) * (g * g)
    m_hat = m / (1.0 - ADAM_B1 ** ADAM_STEP)
    v_hat = v / (1.0 - ADAM_B2 ** ADAM_STEP)
    delta = -ADAM_LR * (m_hat / (jnp.sqrt(v_hat) + ADAM_EPS) + ADAM_WD * w)
    return delta, m, v


def _sum_adamw(parts, w, m, v, *, name, tr=256):
    r, c = w.shape
    tr = min(tr, r)
    tc = min(c, 1024)

    def body(p_ref, w_ref, m_ref, v_ref, g_ref, d_ref, m2_ref, v2_ref):
        g = p_ref[0].astype(F32)
        for i in range(1, N_DEV):
            g = g + p_ref[i].astype(F32)
        delta, m2, v2 = _adamw_math(w_ref[...], g, m_ref[...], v_ref[...])
        g_ref[...] = g
        d_ref[...] = delta
        m2_ref[...] = m2
        v2_ref[...] = v2

    blk = pl.BlockSpec((tr, tc), lambda i, j: (i, j))
    return pl.pallas_call(
        body,
        name=name,
        grid=(r // tr, c // tc),
        in_specs=[pl.BlockSpec((N_DEV, tr, tc), lambda i, j: (0, i, j)), blk, blk, blk],
        out_specs=[blk] * 4,
        out_shape=[jax.ShapeDtypeStruct((r, c), F32)] * 4,
        compiler_params=_params(("parallel", "parallel")),
    )(parts, w, m, v)


def _adamw_small(g, w, m, v):
    def body(g_ref, w_ref, m_ref, v_ref, d_ref, m2_ref, v2_ref):
        delta, m2, v2 = _adamw_math(w_ref[...], g_ref[...], m_ref[...], v_ref[...])
        d_ref[...] = delta
        m2_ref[...] = m2
        v2_ref[...] = v2

    vm = pl.BlockSpec(memory_space=pltpu.VMEM)
    return pl.pallas_call(
        body,
        name="adamw_small",
        in_specs=[vm] * 4,
        out_specs=[vm] * 3,
        out_shape=[jax.ShapeDtypeStruct(g.shape, F32)] * 3,
    )(g, w, m, v)


def _relu2_epilogue(acc):
    ra = jnp.maximum(acc, 0.0)
    return ra * ra, ra


def _rows(stacked):
    return stacked.reshape(stacked.shape[0] * stacked.shape[1], stacked.shape[2])


def _by_chip(mat):
    return mat.reshape(N_CHIPS, mat.shape[0] // N_CHIPS, mat.shape[1])


def _local_step(x, p, target, shards, small, update):
    s, d = x.shape
    cos_t, sin_t = _rope_tables(s)
    bucket = _band_buckets()
    p_bf = p.astype(BF16)
    wts = {}

    chip = 2 * lax.axis_index("x") + lax.axis_index("y")
    core = lax.axis_index("c")

    def gather(tag, names, after):
        srcs = [shards[n] for n in names]
        route = _gather_route([a.shape for a in srcs])
        lands = [_own_slot((N_CHIPS,) + a.shape, a.dtype, chip, a) for a in srcs]
        sems, srcs, lands, token = _split_start(f"gather_start_{tag}", srcs, lands, route, after)
        return lambda done: _swap_comm(*_split_wait(f"gather_wait_{tag}", srcs, lands, sems, route, done)), token

    in_landed, token = gather("in", ["w_in"], shards["w_in"])
    g_attn = small["attn_norm_g"] + token[:1, :1]
    u, (wts["w_in"],) = _rms_fwd(x, g_attn, name="norm_attn", comm=in_landed(token))
    mid_landed, token = gather("mid", ["w_out", "w_gate", "ple_w"], u)
    proj = _matmul(u, wts["w_in"], mode="nn", out_dtypes=[F32], name="mm_in", bn=768, after=token)
    pb, (w_out_s, w_gate_s, wts["ple_w"]) = _qk_prep(proj, small["q_norm_g"], small["k_norm_g"], cos_t, sin_t,
                                                     comm=mid_landed(proj))
    wts["w_out"], wts["w_gate"] = _rows(w_out_s), _rows(w_gate_s)
    up_landed, token = gather("up", ["w_up"], pb)
    oa, lse_a = _attn_a_fwd(pb, after=token)
    pad = ((PAD_LO, PAD_HI), (0, 0))
    kpad = jnp.pad(pb[:, COL_KB * HEAD_DIM:COL_VB * HEAD_DIM], pad)
    vpad = jnp.pad(pb[:, COL_VB * HEAD_DIM:], pad)
    (ob, lse_b), (wts["w_up"],) = _attn_b_fwd(pb, kpad, vpad, bucket, small["rel_bias_table"],
                                              small["sink_logits"], comm=up_landed(oa))
    down_landed, token = gather("down", ["w_down"], ob)
    att = jnp.concatenate([oa, ob], axis=-1)
    h1 = _matmul(att, wts["w_out"], mode="nn", out_dtypes=[F32], name="mm_out",
                 epilogue=lambda acc, res: (acc + res,), extras=(x,), after=token)
    mn = _rms_fwd(h1, small["mlp_norm_g"], name="norm_mlp")
    r, ra = _matmul(mn, wts["w_up"], mode="nn", out_dtypes=[BF16, BF16], name="mm_up", epilogue=_relu2_epilogue)
    (w_down_s,) = _comm_only("swap_w_down", down_landed(r))
    wts["w_down"] = _rows(w_down_s)
    h2 = _matmul(r, wts["w_down"], mode="nn", out_dtypes=[F32], name="mm_down",
                 epilogue=lambda acc, res: (acc + res,), extras=(h1,))
    ng = _rms_fwd(h2, small["gate_norm_g"], name="norm_gate")
    gate = _matmul(ng, wts["w_gate"], mode="nn", out_dtypes=[F32], name="mm_gate",
                   epilogue=lambda acc: (1.0 / (1.0 + jnp.exp(-acc)),))
    pp = _matmul(p_bf, wts["ple_w"], mode="nn", out_dtypes=[F32], name="mm_ple", bn=512)
    dh3, dz, dpp, dg_final, dg_ple, loss = _tail(h2, gate, pp, target, small["ple_norm_g"], small["final_norm_g"])

    dng = _matmul(dz, wts["w_gate"], mode="nt", out_dtypes=[F32], name="mm_gate_dx")
    gw_gate = _matmul(ng, dz, mode="tn", out_dtypes=[BF16], name="mm_gate_dw")
    gw_ple = _matmul(p_bf, dpp, mode="tn", out_dtypes=[BF16], name="mm_ple_dw", bn=512, out_stack=N_CHIPS)
    dh2, dh2_bf, dg_gate = _rms_bwd(h2, dng, small["gate_norm_g"], dh3, name="norm_gate_bwd", want_bf16=True)

    def exchange(tag, partials, after):
        route = _exchange_route(len(partials))
        lands = [_own_slot((N_DEV,) + g.shape[1:], g.dtype, 2 * chip + core,
                           lax.dynamic_index_in_dim(g, chip, 0, keepdims=False)) for g in partials]
        sems, srcs, lands, token = _split_start(f"exchange_start_{tag}", partials, lands, route, after)
        return lambda done: _forward_comm(*_split_wait(f"exchange_wait_{tag}", srcs, lands, sems, route, done)), token

    big = {}
    gate_landed, token = exchange("gate", [_by_chip(gw_gate), gw_ple], dh2_bf)
    gw_down = _matmul(r, dh2_bf, mode="tn", out_dtypes=[BF16], name="mm_down_dw", after=token)
    da, (parts_gate, parts_ple) = _matmul(
        dh2_bf, wts["w_down"], mode="nt", out_dtypes=[BF16], name="mm_down_dx",
        epilogue=lambda acc, ra_v: (acc * (2.0 * ra_v.astype(F32)),), extras=(ra,), comm=gate_landed(gw_down))
    down_landed, token = exchange("down", [_by_chip(gw_down)], da)
    big["w_gate"], big["ple_w"] = update("w_gate", parts_gate), update("ple_w", parts_ple)
    gw_up = _matmul(mn, da, mode="tn", out_dtypes=[BF16], name="mm_up_dw", out_stack=N_CHIPS, after=token)
    dmn = _matmul(da, wts["w_up"], mode="nt", out_dtypes=[F32], name="mm_up_dx")
    dh1, dh1_bf, dg_mlp = _rms_bwd(h1, dmn, small["mlp_norm_g"], dh2, name="norm_mlp_bwd", want_bf16=True)
    datt, (parts_down,) = _matmul(dh1_bf, wts["w_out"], mode="nt", out_dtypes=[BF16], name="mm_out_dx",
                                  comm=down_landed(dh1_bf))
    gw_out = _matmul(att, dh1_bf, mode="tn", out_dtypes=[BF16], name="mm_out_dw")
    up_landed, token = exchange("up", [gw_up, _by_chip(gw_out)], datt)
    dqb, dkpad, dvpad, dtab, dsink = _attn_b_bwd(pb, kpad, vpad, att, datt, lse_b, bucket,
                                                 small["rel_bias_table"], small["sink_logits"], after=token)
    dqa, dka, dva = _attn_a_bwd(pb, att, datt, lse_a)
    (dproj, dg_q, dg_k), (parts_up, parts_out) = _qk_bwd(dqa, dka, dva, dqb, dkpad, dvpad, proj,
                                                         small["q_norm_g"], small["k_norm_g"], cos_t, sin_t,
                                                         comm=up_landed(dqa))
    gw_in = _matmul(u, dproj, mode="tn", out_dtypes=[BF16], name="mm_in_dw", bn=768, out_stack=N_CHIPS)
    in_landed, token = exchange("in", [gw_in], gw_in)
    du = _matmul(dproj, wts["w_in"], mode="nt", out_dtypes=[F32], name="mm_in_dx", bk=768, after=token)
    grad_x, dg_attn = _rms_bwd(x, du, small["attn_norm_g"], dh1, name="norm_attn_bwd", want_bf16=False)
    for n, parts in (("w_down", parts_down), ("w_up", parts_up), ("w_out", parts_out)):
        big[n] = update(n, parts)
    done = dg_attn + sum(big[n][0][0, :1, :] for n in ("w_down", "w_up", "w_out"))
    (parts_in,) = _comm_only("forward_w_in", in_landed(done))
    big["w_in"] = update("w_in", parts_in)

    small_g = {
        "attn_norm_g": dg_attn, "mlp_norm_g": dg_mlp, "ple_norm_g": dg_ple, "gate_norm_g": dg_gate,
        "final_norm_g": dg_final, "q_norm_g": dg_q, "k_norm_g": dg_k,
        "sink_logits": dsink[:, 0, 0][None, :], "rel_bias_table": dtab[:, :, 0].T,
    }
    return loss, grad_x, big, small_g


_SMALL_ROWS = ["attn_norm_g", "mlp_norm_g", "ple_norm_g", "gate_norm_g", "final_norm_g"]
_PACK_ROWS = 8


def _pack_small(vals, d):
    rows = [vals[n].reshape(1, d) for n in _SMALL_ROWS]
    misc = jnp.concatenate([
        vals["q_norm_g"].reshape(1, HEAD_DIM), vals["k_norm_g"].reshape(1, HEAD_DIM),
        jnp.pad(vals["sink_logits"].reshape(1, N_HEADS_B), ((0, 0), (0, HEAD_DIM - N_HEADS_B))),
        vals["rel_bias_table"].reshape(1, N_BUCKETS * N_HEADS_B)], axis=1)
    rows.append(jnp.pad(misc, ((0, 0), (0, d - misc.shape[1]))))
    rows.append(jnp.zeros((_PACK_ROWS - len(rows), d), F32))
    return jnp.concatenate(rows, axis=0).astype(F32)


def _unpack_small(pack, shapes):
    out = {n: pack[i].reshape(shapes[n]) for i, n in enumerate(_SMALL_ROWS)}
    misc = pack[len(_SMALL_ROWS)]
    out["q_norm_g"] = misc[:HEAD_DIM].reshape(shapes["q_norm_g"])
    out["k_norm_g"] = misc[HEAD_DIM:2 * HEAD_DIM].reshape(shapes["k_norm_g"])
    out["sink_logits"] = misc[2 * HEAD_DIM:2 * HEAD_DIM + N_HEADS_B].reshape(shapes["sink_logits"])
    out["rel_bias_table"] = misc[3 * HEAD_DIM:3 * HEAD_DIM + N_BUCKETS * N_HEADS_B].reshape(shapes["rel_bias_table"])
    return out


_WEIGHTS = ["attn_norm_g", "w_in", "q_norm_g", "k_norm_g", "sink_logits", "w_out", "mlp_norm_g", "w_up", "w_down",
            "ple_w", "ple_norm_g", "gate_norm_g", "w_gate", "rel_bias_table", "final_norm_g"]
_BIG = ["w_in", "w_out", "w_up", "w_down", "ple_w", "w_gate"]


def kernel(x, p, attn_norm_g, w_in, q_norm_g, k_norm_g, sink_logits, w_out, mlp_norm_g, w_up, w_down, ple_w, ple_norm_g, gate_norm_g, w_gate, rel_bias_table, final_norm_g, loss_target, m_attn_norm_g, m_w_in, m_q_norm_g, m_k_norm_g, m_sink_logits, m_w_out, m_mlp_norm_g, m_w_up, m_w_down, m_ple_w, m_ple_norm_g, m_gate_norm_g, m_w_gate, m_rel_bias_table, m_final_norm_g, v_attn_norm_g, v_w_in, v_q_norm_g, v_k_norm_g, v_sink_logits, v_w_out, v_mlp_norm_g, v_w_up, v_w_down, v_ple_w, v_ple_norm_g, v_gate_norm_g, v_w_gate, v_rel_bias_table, v_final_norm_g):
    given = dict(locals())
    w = {n: given[n] for n in _WEIGHTS}
    m = {n: given["m_" + n] for n in _WEIGHTS}
    v = {n: given["v_" + n] for n in _WEIGHTS}
    d = x.shape[-1]

    shards = {n: w[n][0].astype(BF16) for n in _BIG}
    small = {
        "attn_norm_g": w["attn_norm_g"], "mlp_norm_g": w["mlp_norm_g"], "ple_norm_g": w["ple_norm_g"],
        "gate_norm_g": w["gate_norm_g"], "final_norm_g": w["final_norm_g"].reshape(1, d),
        "q_norm_g": w["q_norm_g"], "k_norm_g": w["k_norm_g"], "sink_logits": w["sink_logits"],
        "rel_bias_table": w["rel_bias_table"],
    }

    def update(n, parts):
        res = _sum_adamw(parts, w[n][0], m[n][0], v[n][0], name="adamw_" + n)
        return [t.reshape(w[n].shape) for t in res]

    loss_part, grad_x, big, small_g = _local_step(x[0], p[0, 0], loss_target[0], shards, small, update)
    loss = lax.psum(loss_part[0, 0], ("x", "y", "c"))
    grads, deltas, new_m, new_v = [{n: big[n][i] for n in _BIG} for i in range(4)]

    shapes = {n: w[n].shape for n in _WEIGHTS if n not in _BIG}
    pack = _pack_small(small_g, d)
    pack = pack.at[_PACK_ROWS - 1, :1].add(0.0 * grads["w_in"][0, 0, :1])
    g_small = _allreduce_small(pack)
    d_small, m_small, v_small = _adamw_small(g_small, _pack_small(w, d), _pack_small(m, d), _pack_small(v, d))
    grads.update(_unpack_small(g_small, shapes))
    deltas.update(_unpack_small(d_small, shapes))
    new_m.update(_unpack_small(m_small, shapes))
    new_v.update(_unpack_small(v_small, shapes))

    return (loss, grad_x[None], *[grads[n] for n in _WEIGHTS], *[deltas[n] for n in _WEIGHTS],
            *[new_m[n] for n in _WEIGHTS], *[new_v[n] for n in _WEIGHTS])
```

```python
import functools
import math

import jax
import jax.numpy as jnp
import numpy as np
from jax import lax
from jax.experimental import pallas as pl
from jax.experimental.pallas import tpu as pltpu

F32 = jnp.float32
BF16 = jnp.bfloat16

HEAD_DIM = 128
N_HEADS_A = 8
N_KV_A = 2
N_HEADS_B = 8
N_KV_B = 2
GROUP = 4
GRID_W = 64
BLOCK_Q = 128
WINDOW = 128
N_BUCKETS = 32
MAX_DISTANCE = 128
ROPE_THETA = 10000.0
EPS = 1e-6
NEG_INF = -1e30
ATT_SCALE = HEAD_DIM ** -0.5
LOG2E = math.log2(math.e)
LN2 = math.log(2.0)
Q_SCALE = ATT_SCALE * LOG2E
PAD_LO, PAD_HI = 256, 128

ADAM_LR = 0.001
ADAM_B1 = 0.9
ADAM_B2 = 0.999
ADAM_EPS = 1e-08
ADAM_WD = 0.01
ADAM_STEP = 10

N_CHIPS = 4
N_DEV = 8
COL_QA, COL_KA, COL_VA, COL_QB, COL_KB, COL_VB = 0, 8, 10, 12, 20, 22
N_COLS = 24

VMEM_LIMIT = 52 * 1024 * 1024


def _params(sem=None, collective_id=None):
    return pltpu.CompilerParams(dimension_semantics=sem, vmem_limit_bytes=VMEM_LIMIT, collective_id=collective_id)


_ANY = pl.BlockSpec(memory_space=pl.ANY)
_MESH = pl.DeviceIdType.MESH
SIBLING_BARRIER_ID = 1


def _sibling():
    return (lax.axis_index("x"), lax.axis_index("y"), 1 - lax.axis_index("c"))


class _Comm:
    def __init__(self, inputs, out_shapes, sems, start, finish, aliases=None):
        self.inputs, self.out_shapes, self.sems = list(inputs), list(out_shapes), list(sems)
        self.start, self.finish, self.aliases = start, finish, dict(aliases or {})


def _call(body, *, name, grid, in_specs, out_specs, out_shape, args, scratch_shapes=(), sem=None, comm=None,
          after=None, aliases=None):
    in_specs, out_specs, out_shape = list(in_specs), list(out_specs), list(out_shape)
    scratch_shapes = list(scratch_shapes)
    n_in, n_out, n_sc = len(in_specs), len(out_specs), len(scratch_shapes)
    behind = [] if after is None else [after]
    aliases = dict(aliases or {})
    if comm is None:
        res = pl.pallas_call(
            (lambda *refs: body(*refs[:n_in], *refs[n_in + len(behind):])) if behind else body,
            name=name, grid=grid, in_specs=in_specs + [_ANY] * len(behind), out_specs=out_specs,
            out_shape=out_shape, scratch_shapes=scratch_shapes, input_output_aliases=aliases,
            compiler_params=_params(sem))(*args, *behind)
        return list(res), []
    assert not behind
    c_in, c_out = len(comm.inputs), len(comm.out_shapes)

    def hosted(*refs):
        pos = [0]

        def take(n):
            pos[0] += n
            return refs[pos[0] - n:pos[0]]

        ins, c_ins, outs, c_outs, scr = take(n_in), take(c_in), take(n_out), take(c_out), take(n_sc)
        c_sems = refs[pos[0]:]
        ids = [pl.program_id(a) for a in range(len(grid))]
        first = functools.reduce(jnp.logical_and, [i == 0 for i in ids])
        last = functools.reduce(jnp.logical_and, [i == g - 1 for i, g in zip(ids, grid)])

        @pl.when(first)
        def _():
            barrier = pltpu.get_barrier_semaphore()
            pl.semaphore_signal(barrier, inc=1, device_id=_sibling(), device_id_type=_MESH)
            pl.semaphore_wait(barrier, 1)
            comm.start(c_ins, c_outs, c_sems)

        body(*ins, *outs, *scr)

        @pl.when(last)
        def _():
            comm.finish(c_ins, c_outs, c_sems)

    res = pl.pallas_call(
        hosted, name=name, grid=grid, in_specs=in_specs + [_ANY] * c_in, out_specs=out_specs + [_ANY] * c_out,
        out_shape=out_shape + comm.out_shapes, scratch_shapes=scratch_shapes + comm.sems,
        input_output_aliases={**aliases, **{n_in + i: n_out + o for i, o in comm.aliases.items()}},
        compiler_params=_params(("arbitrary",) * len(grid), SIBLING_BARRIER_ID))(*args, *comm.inputs)
    return list(res[:n_out]), list(res[n_out:])


def _matmul(a, b, *, mode, out_dtypes, name, epilogue=None, extras=(), bm=1024, bn=1024, bk=2048,
            out_stack=0, comm=None, after=None):
    stacked = b.ndim == 3
    if mode == "nn":
        m, k = a.shape
        if stacked:
            nj, kb, ns = b.shape
            n, ks = nj * ns, k
        else:
            kb, n = b.shape
            ns, ks = n, k
        dn = (((1,), (0,)), ((), ()))
    elif mode == "nt":
        m, k = a.shape
        if stacked:
            nj, n, ks = b.shape
            kb = nj * ks
        else:
            n, kb = b.shape
            ks = kb
        ns = n
        dn = (((1,), (1,)), ((), ()))
    else:
        k, m = a.shape
        kb, n = b.shape
        ns, ks = n, k
        dn = (((0,), (0,)), ((), ()))
    assert k == kb and not (stacked and mode == "tn")
    ns_out = n // out_stack if out_stack else n
    bm, bn, bk = min(bm, m), min(bn, ns, ns_out), min(bk, ks)
    assert m % bm == 0 and ns % bn == 0 and ns_out % bn == 0 and ks % bk == 0
    gm, gn, gk = m // bm, n // bn, k // bk

    if mode == "tn":
        a_spec = pl.BlockSpec((bk, bm), lambda i, j, q: (q, i))
    else:
        a_spec = pl.BlockSpec((bm, bk), lambda i, j, q: (i, q))
    if mode == "nt":
        if stacked:
            per = ks // bk
            b_spec = pl.BlockSpec((None, bn, bk), lambda i, j, q: (q // per, j, q % per))
        else:
            b_spec = pl.BlockSpec((bn, bk), lambda i, j, q: (j, q))
    else:
        if stacked:
            per = ns // bn
            b_spec = pl.BlockSpec((None, bk, bn), lambda i, j, q: (j // per, q, j % per))
        else:
            b_spec = pl.BlockSpec((bk, bn), lambda i, j, q: (q, j))
    ex_spec = pl.BlockSpec((bm, bn), lambda i, j, q: (i, j))
    if out_stack:
        per_o = ns_out // bn
        o_spec = pl.BlockSpec((None, bm, bn), lambda i, j, q: (j // per_o, i, j % per_o))
        o_shape = (out_stack, m, ns_out)
    else:
        o_spec = ex_spec
        o_shape = (m, n)
    n_ex, n_out = len(extras), len(out_dtypes)

    def body(a_ref, b_ref, *rest):
        ex, outs = rest[:n_ex], rest[n_ex:n_ex + n_out]
        part = lax.dot_general(a_ref[...], b_ref[...], dn, preferred_element_type=F32)

        def finish(acc):
            res = epilogue(acc, *[e[...] for e in ex]) if epilogue else (acc,)
            for o, r in zip(outs, res):
                o[...] = r.astype(o.dtype)

        if gk == 1:
            finish(part)
        else:
            acc_ref = rest[-1]
            q = pl.program_id(2)

            @pl.when(q == 0)
            def _():
                acc_ref[...] = part

            @pl.when(q > 0)
            def _():
                acc_ref[...] += part

            @pl.when(q == gk - 1)
            def _():
                finish(acc_ref[...])

    res, c_res = _call(
        body, name=name, grid=(gm, gn, gk),
        in_specs=[a_spec, b_spec] + [ex_spec] * n_ex,
        out_specs=[o_spec] * n_out,
        out_shape=[jax.ShapeDtypeStruct(o_shape, dt) for dt in out_dtypes],
        scratch_shapes=[pltpu.VMEM((bm, bn), F32)] if gk > 1 else [],
        sem=("parallel", "parallel", "arbitrary"), args=(a, b, *extras), comm=comm, after=after)
    res = res[0] if n_out == 1 else res
    return res if comm is None else (res, c_res)


def _rms_fwd(x, g, *, name, tm=256, comm=None):
    s, d = x.shape
    tm = min(tm, s)

    def body(x_ref, g_ref, o_ref):
        xf = x_ref[...]
        r = lax.rsqrt(jnp.mean(xf * xf, axis=-1, keepdims=True) + EPS)
        o_ref[...] = (xf * r * g_ref[...]).astype(o_ref.dtype)

    res, c_res = _call(
        body, name=name, grid=(s // tm,),
        in_specs=[pl.BlockSpec((tm, d), lambda i: (i, 0)), pl.BlockSpec((1, d), lambda i: (0, 0))],
        out_specs=[pl.BlockSpec((tm, d), lambda i: (i, 0))],
        out_shape=[jax.ShapeDtypeStruct((s, d), BF16)],
        sem=("parallel",), args=(x, g), comm=comm)
    return res[0] if comm is None else (res[0], c_res)


def _rms_bwd(x, dy, g, add, *, name, want_bf16, tm=256):
    s, d = x.shape
    tm = min(tm, s)

    def body(x_ref, dy_ref, g_ref, add_ref, dx_ref, *rest):
        dg_ref = rest[-1]
        i = pl.program_id(0)
        xf = x_ref[...]
        dyf = dy_ref[...].astype(F32)
        r = lax.rsqrt(jnp.mean(xf * xf, axis=-1, keepdims=True) + EPS)
        xh = xf * r
        dyg = dyf * g_ref[...]
        dx = r * (dyg - xh * jnp.mean(dyg * xh, axis=-1, keepdims=True))
        tot = add_ref[...] + dx
        dx_ref[...] = tot
        if want_bf16:
            rest[0][...] = tot.astype(BF16)
        part = jnp.sum(dyf * xh, axis=0, keepdims=True)

        @pl.when(i == 0)
        def _():
            dg_ref[...] = part

        @pl.when(i > 0)
        def _():
            dg_ref[...] += part

    row = pl.BlockSpec((tm, d), lambda i: (i, 0))
    vec = pl.BlockSpec((1, d), lambda i: (0, 0))
    out_specs = [row] + ([row] if want_bf16 else []) + [vec]
    out_shape = [jax.ShapeDtypeStruct((s, d), F32)]
    if want_bf16:
        out_shape.append(jax.ShapeDtypeStruct((s, d), BF16))
    out_shape.append(jax.ShapeDtypeStruct((1, d), F32))
    return pl.pallas_call(
        body,
        name=name,
        grid=(s // tm,),
        in_specs=[row, row, vec, row],
        out_specs=out_specs,
        out_shape=out_shape,
        compiler_params=_params(("arbitrary",)),
    )(x, dy, g, add)


def _tail(h2, gate, pp, target, g_ple, g_final, *, tm=128):
    s, d = h2.shape
    tm = min(tm, s)

    def body(h2_ref, gate_ref, pp_ref, t_ref, gp_ref, gf_ref, dh3_ref, dz_ref, dpp_ref, dgf_ref, dgp_ref, loss_ref):
        i = pl.program_id(0)
        ppf = pp_ref[...]
        gate_v = gate_ref[...]
        r_p = lax.rsqrt(jnp.mean(ppf * ppf, axis=-1, keepdims=True) + EPS)
        eh = ppf * r_p
        e = eh * gp_ref[...]
        h3 = h2_ref[...] + gate_v * e
        r_f = lax.rsqrt(jnp.mean(h3 * h3, axis=-1, keepdims=True) + EPS)
        yh = h3 * r_f
        diff = yh * gf_ref[...] - t_ref[...]
        loss_part = 0.5 * jnp.sum(jnp.mean(diff * diff, axis=-1, keepdims=True), axis=0, keepdims=True)
        dy = diff / d
        dgf = jnp.sum(dy * yh, axis=0, keepdims=True)
        dyg = dy * gf_ref[...]
        dh3 = r_f * (dyg - yh * jnp.mean(dyg * yh, axis=-1, keepdims=True))
        dh3_ref[...] = dh3
        de = dh3 * gate_v
        dz_ref[...] = (dh3 * e * gate_v * (1.0 - gate_v)).astype(BF16)
        dgp = jnp.sum(de * eh, axis=0, keepdims=True)
        deg = de * gp_ref[...]
        dpp_ref[...] = (r_p * (deg - eh * jnp.mean(deg * eh, axis=-1, keepdims=True))).astype(BF16)
        loss_row = jnp.broadcast_to(loss_part, (1, 128))

        @pl.when(i == 0)
        def _():
            dgf_ref[...] = dgf
            dgp_ref[...] = dgp
            loss_ref[...] = loss_row

        @pl.when(i > 0)
        def _():
            dgf_ref[...] += dgf
            dgp_ref[...] += dgp
            loss_ref[...] += loss_row

    row = pl.BlockSpec((tm, d), lambda i: (i, 0))
    vec = pl.BlockSpec((1, d), lambda i: (0, 0))
    return pl.pallas_call(
        body,
        name="tail_fwd_bwd",
        grid=(s // tm,),
        in_specs=[row, row, row, row, vec, vec],
        out_specs=[row, row, row, vec, vec, pl.BlockSpec((1, 128), lambda i: (0, 0))],
        out_shape=[
            jax.ShapeDtypeStruct((s, d), F32),
            jax.ShapeDtypeStruct((s, d), BF16),
            jax.ShapeDtypeStruct((s, d), BF16),
            jax.ShapeDtypeStruct((1, d), F32),
            jax.ShapeDtypeStruct((1, d), F32),
            jax.ShapeDtypeStruct((1, 128), F32),
        ],
        compiler_params=_params(("arbitrary",)),
    )(h2, gate, pp, target, g_ple, g_final)


def _rope_tables(s):
    t = jnp.arange(s, dtype=jnp.int32)
    row = (t // GRID_W).astype(F32)
    col = (t % GRID_W).astype(F32)
    half = HEAD_DIM // 2
    inv_freq = ROPE_THETA ** (-jnp.arange(0, half, 2, dtype=F32) / half)
    ang_r = row[:, None] * inv_freq
    ang_c = col[:, None] * inv_freq
    cr, sr, cc, sc = jnp.cos(ang_r), jnp.sin(ang_r), jnp.cos(ang_c), jnp.sin(ang_c)
    cos_t = jnp.concatenate([cr, cr, cc, cc], axis=-1)
    sin_t = jnp.concatenate([-sr, sr, -sc, sc], axis=-1)
    return cos_t, sin_t


def _swap_quarters(x):
    lane = lax.broadcasted_iota(jnp.int32, x.shape, x.ndim - 1)
    up = pltpu.roll(x, HEAD_DIM - 32, x.ndim - 1)
    down = pltpu.roll(x, 32, x.ndim - 1)
    return jnp.where((lane % 64) < 32, up, down)


def _cols(first, count=1):
    return slice(first * HEAD_DIM, (first + count) * HEAD_DIM)


def _qk_prep(proj, g_q, g_k, cos_t, sin_t, *, tm=256, comm=None):
    s, n = proj.shape
    tm = min(tm, s)

    def body(x_ref, gq_ref, gk_ref, c_ref, s_ref, o_ref):
        cos_v, sin_v = c_ref[...], s_ref[...]
        for h in range(COL_VA):
            x = x_ref[:, _cols(h)]
            g = gq_ref[...] if h < COL_KA else gk_ref[...]
            xn = x * lax.rsqrt(jnp.mean(x * x, axis=-1, keepdims=True) + EPS) * g
            xr = xn * cos_v + _swap_quarters(xn) * sin_v
            if h < COL_KA:
                xr = xr * Q_SCALE
            o_ref[:, _cols(h)] = xr.astype(BF16)
        o_ref[:, _cols(COL_VA, 2)] = x_ref[:, _cols(COL_VA, 2)].astype(BF16)
        o_ref[:, _cols(COL_QB, N_HEADS_B)] = (x_ref[:, _cols(COL_QB, N_HEADS_B)] * Q_SCALE).astype(BF16)
        o_ref[:, _cols(COL_KB, 4)] = x_ref[:, _cols(COL_KB, 4)].astype(BF16)

    row = pl.BlockSpec((tm, n), lambda i: (i, 0))
    tab = pl.BlockSpec((tm, HEAD_DIM), lambda i: (i, 0))
    vec = pl.BlockSpec((1, HEAD_DIM), lambda i: (0, 0))
    res, c_res = _call(
        body, name="qk_prep", grid=(s // tm,),
        in_specs=[row, vec, vec, tab, tab],
        out_specs=[row],
        out_shape=[jax.ShapeDtypeStruct((s, n), BF16)],
        sem=("parallel",), args=(proj, g_q, g_k, cos_t, sin_t), comm=comm)
    return res[0] if comm is None else (res[0], c_res)


def _qk_bwd(dqa, dka, dva, dqb, dkpad, dvpad, proj, g_q, g_k, cos_t, sin_t, *, comm=None):
    s, n = proj.shape
    tm = min(PAD_LO, s)
    assert PAD_LO % tm == 0
    lo = PAD_LO // tm

    def body(dqa_ref, dka_ref, dva_ref, dqb_ref, dkb_ref, dvb_ref, x_ref, gq_ref, gk_ref, c_ref, s_ref,
             o_ref, dgq_ref, dgk_ref):
        i = pl.program_id(0)
        cos_v, sin_v = c_ref[...], s_ref[...]

        def head(d, x, g):
            dn = d * cos_v + _swap_quarters(d * sin_v)
            r = lax.rsqrt(jnp.mean(x * x, axis=-1, keepdims=True) + EPS)
            xh = x * r
            dng = dn * g
            dx = r * (dng - xh * jnp.mean(dng * xh, axis=-1, keepdims=True))
            return dx.astype(BF16), jnp.sum(dn * xh, axis=0, keepdims=True)

        acc_q = jnp.zeros((1, HEAD_DIM), F32)
        acc_k = jnp.zeros((1, HEAD_DIM), F32)
        for h in range(N_HEADS_A):
            o_ref[:, _cols(h)], part = head(dqa_ref[:, _cols(h)] * ATT_SCALE, x_ref[:, _cols(h)], gq_ref[...])
            acc_q = acc_q + part
        for h in range(N_KV_A):
            o_ref[:, _cols(COL_KA + h)], part = head(dka_ref[:, _cols(h)] * LN2, x_ref[:, _cols(COL_KA + h)],
                                                     gk_ref[...])
            acc_k = acc_k + part
        o_ref[:, _cols(COL_VA, 2)] = dva_ref[...].astype(BF16)
        o_ref[:, _cols(COL_QB, N_HEADS_B)] = (dqb_ref[...] * ATT_SCALE).astype(BF16)
        o_ref[:, _cols(COL_KB, 2)] = (dkb_ref[...] * LN2).astype(BF16)
        o_ref[:, _cols(COL_VB, 2)] = dvb_ref[...].astype(BF16)

        @pl.when(i == 0)
        def _():
            dgq_ref[...] = acc_q
            dgk_ref[...] = acc_k

        @pl.when(i > 0)
        def _():
            dgq_ref[...] += acc_q
            dgk_ref[...] += acc_k

    def rows(width, shift=0):
        return pl.BlockSpec((tm, width), lambda i: (i + shift, 0))

    kv_w = N_KV_A * HEAD_DIM
    q_w = N_HEADS_A * HEAD_DIM
    vec = pl.BlockSpec((1, HEAD_DIM), lambda i: (0, 0))
    res, c_res = _call(
        body, name="qk_bwd", grid=(s // tm,),
        in_specs=[rows(q_w), rows(kv_w), rows(kv_w), rows(q_w), rows(kv_w, lo), rows(kv_w, lo), rows(n),
                  vec, vec, rows(HEAD_DIM), rows(HEAD_DIM)],
        out_specs=[rows(n), vec, vec],
        out_shape=[
            jax.ShapeDtypeStruct((s, n), BF16),
            jax.ShapeDtypeStruct((1, HEAD_DIM), F32),
            jax.ShapeDtypeStruct((1, HEAD_DIM), F32),
        ],
        sem=("arbitrary",), args=(dqa, dka, dva, dqb, dkpad, dvpad, proj, g_q, g_k, cos_t, sin_t), comm=comm)
    return res if comm is None else (res, c_res)


_NT = (((1,), (1,)), ((), ()))
_TN = (((0,), (0,)), ((), ()))


def _attn_a_fwd(pb, *, tq=512, sub=256, comm=None, after=None):
    s = pb.shape[0]
    tq = min(tq, s)

    sub = min(sub, tq)

    def body(q_ref, k_ref, v_ref, o_ref, lse_ref):
        k = k_ref[...]
        v = v_ref[...]
        for r in range(tq // sub):
            rows = pl.ds(r * sub, sub)
            sc = lax.dot_general(q_ref[rows, :], k, _NT, preferred_element_type=F32)
            m = jnp.max(sc, axis=-1, keepdims=True)
            p = jnp.exp2(sc - m)
            l = jnp.sum(p, axis=-1, keepdims=True)
            o = jnp.dot(p.astype(BF16), v, preferred_element_type=F32)
            o_ref[rows, :] = (o / l).astype(BF16)
            lse_ref[rows, :] = jnp.broadcast_to(m + jnp.log2(l), (sub, HEAD_DIM))

    res, c_res = _call(
        body, name="attn_a_fwd", grid=(N_HEADS_A, s // tq),
        in_specs=[
            pl.BlockSpec((tq, HEAD_DIM), lambda h, i: (i, COL_QA + h)),
            pl.BlockSpec((s, HEAD_DIM), lambda h, i: (0, COL_KA + h // GROUP)),
            pl.BlockSpec((s, HEAD_DIM), lambda h, i: (0, COL_VA + h // GROUP)),
        ],
        out_specs=[
            pl.BlockSpec((tq, HEAD_DIM), lambda h, i: (i, h)),
            pl.BlockSpec((None, tq, HEAD_DIM), lambda h, i: (h, i, 0)),
        ],
        out_shape=[
            jax.ShapeDtypeStruct((s, (N_HEADS_A + N_HEADS_B) * HEAD_DIM), BF16),
            jax.ShapeDtypeStruct((N_HEADS_A, s, HEAD_DIM), F32),
        ],
        sem=("parallel", "parallel"), args=(pb, pb, pb), comm=comm, after=after)
    return res if comm is None else (res, c_res)


def _attn_a_bwd(pb, att, datt, lse, *, tq=256, sub=128, comm=None):
    s = pb.shape[0]
    tq = min(tq, s)
    sub = min(sub, tq)

    def body(q_ref, k_ref, v_ref, o_ref, do_ref, lse_ref, dq_ref, dk_ref, dv_ref):
        first = jnp.logical_and(pl.program_id(1) == 0, pl.program_id(2) == 0)
        k = k_ref[...]
        v = v_ref[...]
        dk = dv = None
        for r in range(tq // sub):
            rows = pl.ds(r * sub, sub)
            q = q_ref[rows, :]
            do = do_ref[rows, :]
            sc = lax.dot_general(q, k, _NT, preferred_element_type=F32)
            p = jnp.exp2(sc - lse_ref[rows, :][:, :1])
            dp = lax.dot_general(do, v, _NT, preferred_element_type=F32)
            delta = jnp.sum(do.astype(F32) * o_ref[rows, :].astype(F32), axis=-1, keepdims=True)
            ds = (p * (dp - delta)).astype(BF16)
            dq_ref[rows, :] = jnp.dot(ds, k, preferred_element_type=F32)
            dk_r = lax.dot_general(ds, q, _TN, preferred_element_type=F32)
            dv_r = lax.dot_general(p.astype(BF16), do, _TN, preferred_element_type=F32)
            dk = dk_r if dk is None else dk + dk_r
            dv = dv_r if dv is None else dv + dv_r

        @pl.when(first)
        def _():
            dk_ref[...] = dk
            dv_ref[...] = dv

        @pl.when(jnp.logical_not(first))
        def _():
            dk_ref[...] += dk
            dv_ref[...] += dv

    qmap = lambda kv, g, i: (i, kv * GROUP + g)
    res, c_res = _call(
        body, name="attn_a_bwd", grid=(N_KV_A, GROUP, s // tq),
        in_specs=[
            pl.BlockSpec((tq, HEAD_DIM), lambda kv, g, i: (i, COL_QA + kv * GROUP + g)),
            pl.BlockSpec((s, HEAD_DIM), lambda kv, g, i: (0, COL_KA + kv)),
            pl.BlockSpec((s, HEAD_DIM), lambda kv, g, i: (0, COL_VA + kv)),
            pl.BlockSpec((tq, HEAD_DIM), qmap),
            pl.BlockSpec((tq, HEAD_DIM), qmap),
            pl.BlockSpec((None, tq, HEAD_DIM), lambda kv, g, i: (kv * GROUP + g, i, 0)),
        ],
        out_specs=[
            pl.BlockSpec((tq, HEAD_DIM), qmap),
            pl.BlockSpec((s, HEAD_DIM), lambda kv, g, i: (0, kv)),
            pl.BlockSpec((s, HEAD_DIM), lambda kv, g, i: (0, kv)),
        ],
        out_shape=[
            jax.ShapeDtypeStruct((s, N_HEADS_A * HEAD_DIM), F32),
            jax.ShapeDtypeStruct((s, N_KV_A * HEAD_DIM), F32),
            jax.ShapeDtypeStruct((s, N_KV_A * HEAD_DIM), F32),
        ],
        sem=("arbitrary", "arbitrary", "arbitrary"), args=(pb, pb, pb, att, datt, lse), comm=comm)
    return res if comm is None else (res, c_res)


def _t5_bucket(rel):
    nb = N_BUCKETS // 2
    ret = jnp.where(rel > 0, nb, 0)
    n = jnp.abs(rel)
    max_exact = nb // 2
    nf = jnp.maximum(n, 1).astype(F32)
    large = max_exact + (jnp.log(nf / max_exact) / math.log(MAX_DISTANCE / max_exact)
                         * (nb - max_exact)).astype(jnp.int32)
    large = jnp.minimum(large, nb - 1)
    return ret + jnp.where(n < max_exact, n, large)


def _band_buckets():
    r = jnp.arange(BLOCK_Q, dtype=jnp.int32)
    j = jnp.arange(3 * BLOCK_Q, dtype=jnp.int32)
    return _t5_bucket((j[None, :] - BLOCK_Q) - r[:, None])


def _band_bias(bucket, table_ref, h):
    acc = jnp.zeros(bucket.shape, F32)
    for b in range(N_BUCKETS):
        acc = jnp.where(bucket == b, table_ref[b, h], acc)
    return acc


GQ = GROUP * BLOCK_Q


def _stack_heads(x):
    return jnp.concatenate([x[:, _cols(g)] for g in range(GROUP)], axis=0)


def _unstack_heads(x):
    return jnp.concatenate([x[g * BLOCK_Q:(g + 1) * BLOCK_Q] for g in range(GROUP)], axis=1)


def _group_bias(bucket, table_ref, kv):
    return jnp.concatenate([_band_bias(bucket, table_ref, kv * GROUP + g) * LOG2E for g in range(GROUP)], axis=0)


def _group_sink(sink_ref, kv):
    head = lax.broadcasted_iota(jnp.int32, (GQ, 1), 0) // BLOCK_Q
    snk = jnp.zeros((GQ, 1), F32)
    for g in range(GROUP):
        snk = jnp.where(head == g, sink_ref[0, kv * GROUP + g] * LOG2E, snk)
    return snk


def _band_mask(n, s):
    r = lax.broadcasted_iota(jnp.int32, (GQ, 3 * BLOCK_Q), 0) % BLOCK_Q
    j = lax.broadcasted_iota(jnp.int32, (GQ, 3 * BLOCK_Q), 1)
    rel = j - BLOCK_Q - r
    kabs = n * BLOCK_Q + j - BLOCK_Q
    return (jnp.abs(rel) <= WINDOW) & (kabs >= 0) & (kabs < s)


def _band_start(n):
    return pl.multiple_of(n * BLOCK_Q + (PAD_LO - BLOCK_Q), BLOCK_Q)


def _attn_b_fwd(pb, kpad, vpad, bucket, table, sink, att, *, comm=None):
    s = pb.shape[0]
    nblk = s // BLOCK_Q
    sp = kpad.shape[0]

    def body(table_ref, sink_ref, q_ref, k_ref, v_ref, bucket_ref, _, o_ref, lse_ref, bias_ref):
        kv = pl.program_id(0)
        n = pl.program_id(1)

        @pl.when(n == 0)
        def _():
            bias_ref[...] = _group_bias(bucket_ref[...], table_ref, kv)

        band = pl.ds(_band_start(n), 3 * BLOCK_Q)
        kb = k_ref[band, :]
        vb = v_ref[band, :]
        sc = lax.dot_general(_stack_heads(q_ref[...]), kb, _NT, preferred_element_type=F32) + bias_ref[...]
        sc = jnp.where(_band_mask(n, s), sc, NEG_INF)
        snk = _group_sink(sink_ref, kv)
        m = jnp.maximum(jnp.max(sc, axis=-1, keepdims=True), snk)
        p = jnp.exp2(sc - m)
        l = jnp.sum(p, axis=-1, keepdims=True) + jnp.exp2(snk - m)
        o = jnp.dot(p.astype(BF16), vb, preferred_element_type=F32)
        o_ref[...] = _unstack_heads((o / l).astype(BF16))
        lse = m + jnp.log2(l)
        for g in range(GROUP):
            lse_ref[g] = jnp.broadcast_to(lse[g * BLOCK_Q:(g + 1) * BLOCK_Q], (BLOCK_Q, HEAD_DIM))

    smem = pl.BlockSpec(memory_space=pltpu.SMEM)
    res, c_res = _call(
        body, name="attn_b_fwd", grid=(N_KV_B, nblk),
        in_specs=[
            smem,
            smem,
            pl.BlockSpec((BLOCK_Q, GROUP * HEAD_DIM), lambda kv, n: (n, COL_QB // GROUP + kv)),
            pl.BlockSpec((sp, HEAD_DIM), lambda kv, n: (0, kv)),
            pl.BlockSpec((sp, HEAD_DIM), lambda kv, n: (0, kv)),
            pl.BlockSpec((BLOCK_Q, 3 * BLOCK_Q), lambda kv, n: (0, 0)),
            _ANY,
        ],
        out_specs=[
            pl.BlockSpec((BLOCK_Q, GROUP * HEAD_DIM), lambda kv, n: (n, N_HEADS_A // GROUP + kv)),
            pl.BlockSpec((GROUP, BLOCK_Q, HEAD_DIM), lambda kv, n: (kv, n, 0)),
        ],
        out_shape=[
            jax.ShapeDtypeStruct(att.shape, BF16),
            jax.ShapeDtypeStruct((N_HEADS_B, s, HEAD_DIM), F32),
        ],
        scratch_shapes=[pltpu.VMEM((GQ, 3 * BLOCK_Q), F32)],
        sem=("arbitrary", "arbitrary"), args=(table, sink, pb, kpad, vpad, bucket, att), comm=comm,
        aliases={6: 0})
    return res if comm is None else (res, c_res)


def _attn_b_bwd(pb, kpad, vpad, att, datt, lse, bucket, table, sink, *, comm=None, after=None):
    s = pb.shape[0]
    nblk = s // BLOCK_Q
    sp = kpad.shape[0]

    def body(table_ref, sink_ref, q_ref, k_ref, v_ref, o_ref, do_ref, lse_ref, bucket_ref,
             dq_ref, dk_ref, dv_ref, dtab_ref, dsink_ref, bias_ref, dbias_ref):
        kv = pl.program_id(0)
        n = pl.program_id(1)

        @pl.when(n == 0)
        def _():
            dk_ref[...] = jnp.zeros_like(dk_ref)
            dv_ref[...] = jnp.zeros_like(dv_ref)
            bias_ref[...] = _group_bias(bucket_ref[...], table_ref, kv)
            dbias_ref[...] = jnp.zeros_like(dbias_ref)
            dsink_ref[...] = jnp.zeros_like(dsink_ref)

        band = pl.ds(_band_start(n), 3 * BLOCK_Q)
        q = _stack_heads(q_ref[...])
        do = _stack_heads(do_ref[...])
        o = _stack_heads(o_ref[...])
        kb = k_ref[band, :]
        vb = v_ref[band, :]
        lse = jnp.concatenate([lse_ref[g][:, :1] for g in range(GROUP)], axis=0)
        sc = lax.dot_general(q, kb, _NT, preferred_element_type=F32) + bias_ref[...]
        sc = jnp.where(_band_mask(n, s), sc, NEG_INF)
        p = jnp.exp2(sc - lse)
        dp = lax.dot_general(do, vb, _NT, preferred_element_type=F32)
        delta = jnp.sum(do.astype(F32) * o.astype(F32), axis=-1, keepdims=True)
        ds = p * (dp - delta)
        dsb = ds.astype(BF16)
        dq_ref[...] = _unstack_heads(jnp.dot(dsb, kb, preferred_element_type=F32))
        dk_ref[band, :] += lax.dot_general(dsb, q, _TN, preferred_element_type=F32)
        dv_ref[band, :] += lax.dot_general(p.astype(BF16), do, _TN, preferred_element_type=F32)
        dbias_ref[...] += ds
        sink_part = -jnp.exp2(_group_sink(sink_ref, kv) - lse) * delta
        for g in range(GROUP):
            rows = slice(g * BLOCK_Q, (g + 1) * BLOCK_Q)
            dsink_ref[g] += jnp.broadcast_to(jnp.sum(sink_part[rows], axis=0, keepdims=True), (1, HEAD_DIM))

        @pl.when(n == nblk - 1)
        def _():
            bucket_v = bucket_ref[...]
            row = lax.broadcasted_iota(jnp.int32, (N_BUCKETS, HEAD_DIM), 0)
            for g in range(GROUP):
                acc = dbias_ref[g * BLOCK_Q:(g + 1) * BLOCK_Q, :]
                tot = jnp.zeros((N_BUCKETS, HEAD_DIM), F32)
                for b in range(N_BUCKETS):
                    tot = jnp.where(row == b, jnp.sum(jnp.where(bucket_v == b, acc, 0.0), keepdims=True), tot)
                dtab_ref[g] = tot

    smem = pl.BlockSpec(memory_space=pltpu.SMEM)
    wide = GROUP * HEAD_DIM
    res, c_res = _call(
        body, name="attn_b_bwd", grid=(N_KV_B, nblk),
        in_specs=[
            smem,
            smem,
            pl.BlockSpec((BLOCK_Q, wide), lambda kv, n: (n, COL_QB // GROUP + kv)),
            pl.BlockSpec((sp, HEAD_DIM), lambda kv, n: (0, kv)),
            pl.BlockSpec((sp, HEAD_DIM), lambda kv, n: (0, kv)),
            pl.BlockSpec((BLOCK_Q, wide), lambda kv, n: (n, N_HEADS_A // GROUP + kv)),
            pl.BlockSpec((BLOCK_Q, wide), lambda kv, n: (n, N_HEADS_A // GROUP + kv)),
            pl.BlockSpec((GROUP, BLOCK_Q, HEAD_DIM), lambda kv, n: (kv, n, 0)),
            pl.BlockSpec((BLOCK_Q, 3 * BLOCK_Q), lambda kv, n: (0, 0)),
        ],
        out_specs=[
            pl.BlockSpec((BLOCK_Q, wide), lambda kv, n: (n, kv)),
            pl.BlockSpec((sp, HEAD_DIM), lambda kv, n: (0, kv)),
            pl.BlockSpec((sp, HEAD_DIM), lambda kv, n: (0, kv)),
            pl.BlockSpec((GROUP, N_BUCKETS, HEAD_DIM), lambda kv, n: (kv, 0, 0)),
            pl.BlockSpec((GROUP, 1, HEAD_DIM), lambda kv, n: (kv, 0, 0)),
        ],
        out_shape=[
            jax.ShapeDtypeStruct((s, N_HEADS_B * HEAD_DIM), F32),
            jax.ShapeDtypeStruct((sp, N_KV_B * HEAD_DIM), F32),
            jax.ShapeDtypeStruct((sp, N_KV_B * HEAD_DIM), F32),
            jax.ShapeDtypeStruct((N_HEADS_B, N_BUCKETS, HEAD_DIM), F32),
            jax.ShapeDtypeStruct((N_HEADS_B, 1, HEAD_DIM), F32),
        ],
        scratch_shapes=[pltpu.VMEM((GQ, 3 * BLOCK_Q), F32), pltpu.VMEM((GQ, 3 * BLOCK_Q), F32)],
        sem=("arbitrary", "arbitrary"),
        args=(table, sink, pb, kpad, vpad, att, datt, lse, bucket), comm=comm, after=after)
    return res if comm is None else (res, c_res)


_MESH = pl.DeviceIdType.MESH


def _other_chips(x, y):
    return [(x, 1 - y), (1 - x, y), (1 - x, 1 - y)]


_HBM = pl.BlockSpec(memory_space=pltpu.HBM)
_SEM = pl.BlockSpec(memory_space=pltpu.SEMAPHORE)
_SPLIT = pltpu.CompilerParams(has_side_effects=pltpu.SideEffectType.DATAFLOW_SIDE_EFFECTING)


def _in_hbm(a):
    return pltpu.with_memory_space_constraint(a, pltpu.HBM)


def _my_half(rows):
    c = lax.axis_index("c")
    half = rows // 2
    return pl.ds(pl.multiple_of(c * half, half), half), pl.ds(pl.multiple_of((1 - c) * half, half), half)


def _gather_route(shapes):
    def route(src, land):
        x, y, c = lax.axis_index("x"), lax.axis_index("y"), lax.axis_index("c")
        out = []
        for t, shape in enumerate(shapes):
            mine, _ = _my_half(shape[0])
            for px, py in _other_chips(x, y):
                out.append((src[t].at[mine], land[t].at[2 * x + y, mine], land[t].at[2 * px + py, mine], (px, py, c)))
        return out

    return route


def _exchange_route(n_t):
    def route(src, land):
        x, y, c = lax.axis_index("x"), lax.axis_index("y"), lax.axis_index("c")
        out = []
        for t in range(n_t):
            for px, py in _other_chips(x, y):
                k = 2 * px + py
                out.append((src[t].at[k], land[t].at[2 * (2 * x + y) + c], land[t].at[2 * k + c], (px, py, c)))
        return out

    return route


def _own_slot(shape, dtype, slot, block):
    return lax.dynamic_update_slice(lax.empty(shape, dtype), block[None], (slot,) + (0,) * (len(shape) - 1))


def _split_start(name, srcs, lands, route, after):
    n = len(srcs)

    def body(*refs):
        src, land, send_sems, recv_sems, token = refs[:n], refs[n:2 * n], refs[2 * n + 1], refs[2 * n + 2], refs[-1]
        for i, (src_ref, dst_ref, _, to) in enumerate(route(src, land)):
            pltpu.make_async_remote_copy(src_ref=src_ref, dst_ref=dst_ref, send_sem=send_sems.at[i],
                                         recv_sem=recv_sems.at[i], device_id=to, device_id_type=_MESH).start()
        token[...] = jnp.zeros_like(token)

    sem = pltpu.SemaphoreType.DMA((3 * n,))
    lands = list(lands)
    res = pl.pallas_call(
        body, name=name,
        in_specs=[_HBM] * (2 * n) + [_ANY],
        out_specs=[_SEM, _SEM] + [_HBM] * (2 * n) + [pl.BlockSpec(memory_space=pltpu.VMEM)],
        out_shape=[sem, sem] + [pltpu.HBM(a.shape, a.dtype) for a in list(srcs) + lands]
        + [jax.ShapeDtypeStruct((8, 128), F32)],
        input_output_aliases={i: 2 + i for i in range(2 * n)},
        compiler_params=_SPLIT,
    )(*[_in_hbm(a) for a in srcs], *[_in_hbm(a) for a in lands], after)
    return (res[0], res[1]), res[2:2 + n], res[2 + n:2 + 2 * n], res[-1]


def _split_wait(name, srcs, lands, sems, route, after):
    n = len(srcs)

    def body(*refs):
        src, land, send_sems, recv_sems = refs[:n], refs[n:2 * n], refs[2 * n], refs[2 * n + 1]
        for i, (src_ref, _, dst_ref, to) in enumerate(route(src, land)):
            cp = pltpu.make_async_remote_copy(src_ref=src_ref, dst_ref=dst_ref, send_sem=send_sems.at[i],
                                              recv_sem=recv_sems.at[i], device_id=to, device_id_type=_MESH)
            cp.wait_send()
            cp.wait_recv()

    res = pl.pallas_call(
        body, name=name,
        in_specs=[_HBM] * (2 * n) + [_SEM, _SEM, _ANY],
        out_specs=[_HBM] * (2 * n),
        out_shape=[pltpu.HBM(a.shape, a.dtype) for a in list(srcs) + list(lands)],
        input_output_aliases={i: i for i in range(2 * n)},
        compiler_params=_SPLIT,
    )(*srcs, *lands, sems[0], sems[1], after)
    return res[:n], res[n:]


def _comm_only(name, comm):
    return _call(lambda: None, name=name, grid=(1,), in_specs=[], out_specs=[], out_shape=[], args=(), comm=comm)[1]


def _swap_comm(shards, lands):
    n_t = len(lands)

    def copies(land, sems, later):
        send_sems, recv_sems = sems
        x, y = lax.axis_index("x"), lax.axis_index("y")
        sends, recvs = [], []
        for t in range(n_t):
            mine, other = _my_half(shards[t].shape[0])
            for j, (px, py) in enumerate(_other_chips(x, y)):
                k = 2 * px + py
                for part, out in ((mine, sends), (other, recvs)) if later else ((mine, sends),):
                    out.append(pltpu.make_async_remote_copy(
                        src_ref=land[t].at[k, part], dst_ref=land[t].at[k, part], send_sem=send_sems.at[3 * t + j],
                        recv_sem=recv_sems.at[3 * t + j], device_id=_sibling(), device_id_type=_MESH))
        return sends, recvs

    def start(ins, land, sems):
        for cp in copies(land, sems, False)[0]:
            cp.start()

    def finish(ins, land, sems):
        sends, recvs = copies(land, sems, True)
        for cp in recvs:
            cp.wait_recv()
        for cp in sends:
            cp.wait_send()

    return _Comm(
        lands, [jax.ShapeDtypeStruct(a.shape, a.dtype) for a in lands],
        [pltpu.SemaphoreType.DMA((3 * n_t,)), pltpu.SemaphoreType.DMA((3 * n_t,))],
        start, finish, aliases={t: t for t in range(n_t)})


def _forward_comm(partials, lands):
    n_t = len(lands)

    def copies(land, sems, later):
        send_sems, recv_sems = sems
        x, y, c = lax.axis_index("x"), lax.axis_index("y"), lax.axis_index("c")
        sends, recvs = [], []
        for t in range(n_t):
            for j, k in enumerate([2 * x + y] + [2 * px + py for px, py in _other_chips(x, y)]):
                for slot, out in ((2 * k + c, sends), (2 * k + 1 - c, recvs)) if later else ((2 * k + c, sends),):
                    out.append(pltpu.make_async_remote_copy(
                        src_ref=land[t].at[slot], dst_ref=land[t].at[slot], send_sem=send_sems.at[4 * t + j],
                        recv_sem=recv_sems.at[4 * t + j], device_id=_sibling(), device_id_type=_MESH))
        return sends, recvs

    def start(ins, land, sems):
        for cp in copies(land, sems, False)[0]:
            cp.start()

    def finish(ins, land, sems):
        sends, recvs = copies(land, sems, True)
        for cp in recvs:
            cp.wait_recv()
        for cp in sends:
            cp.wait_send()

    return _Comm(
        lands, [jax.ShapeDtypeStruct(a.shape, a.dtype) for a in lands],
        [pltpu.SemaphoreType.DMA((4 * n_t,)), pltpu.SemaphoreType.DMA((4 * n_t,))],
        start, finish, aliases={t: t for t in range(n_t)})


def _allreduce_small(pack):
    rows, d = pack.shape

    def body(p_ref, sum_ref, all_ref, send_sems, recv_sems):
        x, y, c = lax.axis_index("x"), lax.axis_index("y"), lax.axis_index("c")
        me = 4 * x + 2 * y + c
        all_ref[me] = p_ref[...]
        peers = []
        for dx in range(2):
            for dy in range(2):
                for dc in range(2):
                    if dx or dy or dc:
                        px = 1 - x if dx else x
                        py = 1 - y if dy else y
                        pc = 1 - c if dc else c
                        peers.append((4 * dx + 2 * dy + dc - 1, (px, py, pc)))
        sends = []
        for k, to in peers:
            cp = pltpu.make_async_remote_copy(
                src_ref=p_ref, dst_ref=all_ref.at[me], send_sem=send_sems.at[k], recv_sem=recv_sems.at[k],
                device_id=to, device_id_type=_MESH)
            cp.start()
            sends.append(cp)
        for k, (px, py, pc) in peers:
            pltpu.make_async_remote_copy(
                src_ref=p_ref, dst_ref=all_ref.at[4 * px + 2 * py + pc], send_sem=send_sems.at[k],
                recv_sem=recv_sems.at[k], device_id=(px, py, pc), device_id_type=_MESH).wait_recv()
        for cp in sends:
            cp.wait_send()
        tot = all_ref[0]
        for i in range(1, N_DEV):
            tot = tot + all_ref[i]
        sum_ref[...] = tot

    vm = pl.BlockSpec(memory_space=pltpu.VMEM)
    return pl.pallas_call(
        body,
        name="allreduce_small",
        in_specs=[vm],
        out_specs=vm,
        out_shape=jax.ShapeDtypeStruct((rows, d), F32),
        scratch_shapes=[
            pltpu.VMEM((N_DEV, rows, d), F32),
            pltpu.SemaphoreType.DMA((N_DEV - 1,)),
            pltpu.SemaphoreType.DMA((N_DEV - 1,)),
        ],
    )(pack)


def _adamw_math(w, g, m, v):
    m = ADAM_B1 * m + (1.0 - ADAM_B1) * g
    v = ADAM_B2 * v + (1.0 - ADAM_B2) * (g * g)
    m_hat = m / (1.0 - ADAM_B1 ** ADAM_STEP)
    v_hat = v / (1.0 - ADAM_B2 ** ADAM_STEP)
    delta = -ADAM_LR * (m_hat / (jnp.sqrt(v_hat) + ADAM_EPS) + ADAM_WD * w)
    return delta, m, v


def _sum_adamw(parts, w, m, v, *, name, tr=256):
    r, c = w.shape
    tr = min(tr, r)
    tc = min(c, 1024)

    def body(p_ref, w_ref, m_ref, v_ref, g_ref, d_ref, m2_ref, v2_ref):
        g = p_ref[0].astype(F32)
        for i in range(1, N_DEV):
            g = g + p_ref[i].astype(F32)
        delta, m2, v2 = _adamw_math(w_ref[...], g, m_ref[...], v_ref[...])
        g_ref[...] = g
        d_ref[...] = delta
        m2_ref[...] = m2
        v2_ref[...] = v2

    blk = pl.BlockSpec((tr, tc), lambda i, j: (i, j))
    return pl.pallas_call(
        body,
        name=name,
        grid=(r // tr, c // tc),
        in_specs=[pl.BlockSpec((N_DEV, tr, tc), lambda i, j: (0, i, j)), blk, blk, blk],
        out_specs=[blk] * 4,
        out_shape=[jax.ShapeDtypeStruct((r, c), F32)] * 4,
        compiler_params=_params(("parallel", "parallel")),
    )(parts, w, m, v)


def _adamw_small(g, w, m, v):
    def body(g_ref, w_ref, m_ref, v_ref, d_ref, m2_ref, v2_ref):
        delta, m2, v2 = _adamw_math(w_ref[...], g_ref[...], m_ref[...], v_ref[...])
        d_ref[...] = delta
        m2_ref[...] = m2
        v2_ref[...] = v2

    vm = pl.BlockSpec(memory_space=pltpu.VMEM)
    return pl.pallas_call(
        body,
        name="adamw_small",
        in_specs=[vm] * 4,
        out_specs=[vm] * 3,
        out_shape=[jax.ShapeDtypeStruct(g.shape, F32)] * 3,
    )(g, w, m, v)


def _relu2_epilogue(acc):
    ra = jnp.maximum(acc, 0.0)
    return ra * ra, ra


def _rows(stacked):
    return stacked.reshape(stacked.shape[0] * stacked.shape[1], stacked.shape[2])


def _by_chip(mat):
    return mat.reshape(N_CHIPS, mat.shape[0] // N_CHIPS, mat.shape[1])


def _local_step(x, p, target, shards, small, update):
    s, d = x.shape
    cos_t, sin_t = _rope_tables(s)
    bucket = _band_buckets()
    p_bf = p.astype(BF16)
    wts = {}

    chip = 2 * lax.axis_index("x") + lax.axis_index("y")
    core = lax.axis_index("c")

    def gather(tag, names, after):
        srcs = [shards[n] for n in names]
        route = _gather_route([a.shape for a in srcs])
        lands = [_own_slot((N_CHIPS,) + a.shape, a.dtype, chip, a) for a in srcs]
        sems, srcs, lands, token = _split_start(f"gather_start_{tag}", srcs, lands, route, after)
        return lambda done: _swap_comm(*_split_wait(f"gather_wait_{tag}", srcs, lands, sems, route, done)), token

    in_landed, token = gather("in", ["w_in"], shards["w_in"])
    g_attn = small["attn_norm_g"] + token[:1, :1]
    u, (wts["w_in"],) = _rms_fwd(x, g_attn, name="norm_attn", comm=in_landed(token))
    mid_landed, token = gather("mid", ["w_out"], u)
    proj = _matmul(u, wts["w_in"], mode="nn", out_dtypes=[F32], name="mm_in", bn=768, after=token)
    pb, (w_out_s,) = _qk_prep(proj, small["q_norm_g"], small["k_norm_g"], cos_t, sin_t, comm=mid_landed(proj))
    wts["w_out"] = _rows(w_out_s)
    up_landed, token = gather("up", ["w_up"], pb)
    att_a, lse_a = _attn_a_fwd(pb, after=token)
    pad = ((PAD_LO, PAD_HI), (0, 0))
    kpad = jnp.pad(pb[:, COL_KB * HEAD_DIM:COL_VB * HEAD_DIM], pad)
    vpad = jnp.pad(pb[:, COL_VB * HEAD_DIM:], pad)
    (att, lse_b), (wts["w_up"],) = _attn_b_fwd(pb, kpad, vpad, bucket, small["rel_bias_table"],
                                               small["sink_logits"], att_a, comm=up_landed(att_a))
    down_landed, token = gather("down", ["w_down"], att)
    h1 = _matmul(att, wts["w_out"], mode="nn", out_dtypes=[F32], name="mm_out",
                 epilogue=lambda acc, res: (acc + res,), extras=(x,), after=token)
    mn = _rms_fwd(h1, small["mlp_norm_g"], name="norm_mlp")
    r, ra = _matmul(mn, wts["w_up"], mode="nn", out_dtypes=[BF16, BF16], name="mm_up", epilogue=_relu2_epilogue,
                    bm=2048)
    (w_down_s,) = _comm_only("swap_w_down", down_landed(r))
    wts["w_down"] = _rows(w_down_s)
    late_landed, token = gather("late", ["w_gate", "ple_w"], w_down_s)
    h2 = _matmul(r, wts["w_down"], mode="nn", out_dtypes=[F32], name="mm_down",
                 epilogue=lambda acc, res: (acc + res,), extras=(h1,), after=token)
    ng, (w_gate_s, wts["ple_w"]) = _rms_fwd(h2, small["gate_norm_g"], name="norm_gate", comm=late_landed(h2))
    wts["w_gate"] = _rows(w_gate_s)
    gate = _matmul(ng, wts["w_gate"], mode="nn", out_dtypes=[F32], name="mm_gate",
                   epilogue=lambda acc: (1.0 / (1.0 + jnp.exp(-acc)),))
    pp = _matmul(p_bf, wts["ple_w"], mode="nn", out_dtypes=[F32], name="mm_ple", bn=512)
    dh3, dz, dpp, dg_final, dg_ple, loss = _tail(h2, gate, pp, target, small["ple_norm_g"], small["final_norm_g"])

    dng = _matmul(dz, wts["w_gate"], mode="nt", out_dtypes=[F32], name="mm_gate_dx")
    gw_gate = _matmul(ng, dz, mode="tn", out_dtypes=[BF16], name="mm_gate_dw")
    gw_ple = _matmul(p_bf, dpp, mode="tn", out_dtypes=[BF16], name="mm_ple_dw", bn=512, out_stack=N_CHIPS)
    dh2, dh2_bf, dg_gate = _rms_bwd(h2, dng, small["gate_norm_g"], dh3, name="norm_gate_bwd", want_bf16=True)

    def exchange(tag, partials, after):
        route = _exchange_route(len(partials))
        lands = [_own_slot((N_DEV,) + g.shape[1:], g.dtype, 2 * chip + core,
                           lax.dynamic_index_in_dim(g, chip, 0, keepdims=False)) for g in partials]
        sems, srcs, lands, token = _split_start(f"exchange_start_{tag}", partials, lands, route, after)
        return lambda done: _forward_comm(*_split_wait(f"exchange_wait_{tag}", srcs, lands, sems, route, done)), token

    big = {}
    gate_landed, token = exchange("gate", [_by_chip(gw_gate), gw_ple], dh2_bf)
    gw_down = _matmul(r, dh2_bf, mode="tn", out_dtypes=[BF16], name="mm_down_dw", after=token)
    da, (parts_gate, parts_ple) = _matmul(
        dh2_bf, wts["w_down"], mode="nt", out_dtypes=[BF16], name="mm_down_dx", bm=2048,
        epilogue=lambda acc, ra_v: (acc * (2.0 * ra_v.astype(F32)),), extras=(ra,), comm=gate_landed(gw_down))
    down_landed, token = exchange("down", [_by_chip(gw_down)], da)
    big["w_gate"], big["ple_w"] = update("w_gate", parts_gate), update("ple_w", parts_ple)
    gw_up = _matmul(mn, da, mode="tn", out_dtypes=[BF16], name="mm_up_dw", out_stack=N_CHIPS, after=token)
    dmn = _matmul(da, wts["w_up"], mode="nt", out_dtypes=[F32], name="mm_up_dx")
    dh1, dh1_bf, dg_mlp = _rms_bwd(h1, dmn, small["mlp_norm_g"], dh2, name="norm_mlp_bwd", want_bf16=True)
    datt, (parts_down,) = _matmul(dh1_bf, wts["w_out"], mode="nt", out_dtypes=[BF16], name="mm_out_dx",
                                  comm=down_landed(dh1_bf))
    gw_out = _matmul(att, dh1_bf, mode="tn", out_dtypes=[BF16], name="mm_out_dw")
    up_landed, token = exchange("up", [gw_up, _by_chip(gw_out)], datt)
    dqb, dkpad, dvpad, dtab, dsink = _attn_b_bwd(pb, kpad, vpad, att, datt, lse_b, bucket,
                                                 small["rel_bias_table"], small["sink_logits"], after=token)
    dqa, dka, dva = _attn_a_bwd(pb, att, datt, lse_a)
    (dproj, dg_q, dg_k), (parts_up, parts_out) = _qk_bwd(dqa, dka, dva, dqb, dkpad, dvpad, proj,
                                                         small["q_norm_g"], small["k_norm_g"], cos_t, sin_t,
                                                         comm=up_landed(dqa))
    gw_in = _matmul(u, dproj, mode="tn", out_dtypes=[BF16], name="mm_in_dw", bn=768, out_stack=N_CHIPS)
    in_landed, token = exchange("in", [gw_in], gw_in)
    du = _matmul(dproj, wts["w_in"], mode="nt", out_dtypes=[F32], name="mm_in_dx", bk=768, after=token)
    grad_x, dg_attn = _rms_bwd(x, du, small["attn_norm_g"], dh1, name="norm_attn_bwd", want_bf16=False)
    for n, parts in (("w_down", parts_down), ("w_up", parts_up), ("w_out", parts_out)):
        big[n] = update(n, parts)
    done = dg_attn + sum(big[n][0][0, :1, :] for n in ("w_down", "w_up", "w_out"))
    (parts_in,) = _comm_only("forward_w_in", in_landed(done))
    big["w_in"] = update("w_in", parts_in)

    small_g = {
        "attn_norm_g": dg_attn, "mlp_norm_g": dg_mlp, "ple_norm_g": dg_ple, "gate_norm_g": dg_gate,
        "final_norm_g": dg_final, "q_norm_g": dg_q, "k_norm_g": dg_k,
        "sink_logits": dsink[:, 0, 0][None, :], "rel_bias_table": dtab[:, :, 0].T,
    }
    return loss, grad_x, big, small_g


_SMALL_ROWS = ["attn_norm_g", "mlp_norm_g", "ple_norm_g", "gate_norm_g", "final_norm_g"]
_PACK_ROWS = 8


def _pack_small(vals, d):
    rows = [vals[n].reshape(1, d) for n in _SMALL_ROWS]
    misc = jnp.concatenate([
        vals["q_norm_g"].reshape(1, HEAD_DIM), vals["k_norm_g"].reshape(1, HEAD_DIM),
        jnp.pad(vals["sink_logits"].reshape(1, N_HEADS_B), ((0, 0), (0, HEAD_DIM - N_HEADS_B))),
        vals["rel_bias_table"].reshape(1, N_BUCKETS * N_HEADS_B)], axis=1)
    rows.append(jnp.pad(misc, ((0, 0), (0, d - misc.shape[1]))))
    rows.append(jnp.zeros((_PACK_ROWS - len(rows), d), F32))
    return jnp.concatenate(rows, axis=0).astype(F32)


def _unpack_small(pack, shapes):
    out = {n: pack[i].reshape(shapes[n]) for i, n in enumerate(_SMALL_ROWS)}
    misc = pack[len(_SMALL_ROWS)]
    out["q_norm_g"] = misc[:HEAD_DIM].reshape(shapes["q_norm_g"])
    out["k_norm_g"] = misc[HEAD_DIM:2 * HEAD_DIM].reshape(shapes["k_norm_g"])
    out["sink_logits"] = misc[2 * HEAD_DIM:2 * HEAD_DIM + N_HEADS_B].reshape(shapes["sink_logits"])
    out["rel_bias_table"] = misc[3 * HEAD_DIM:3 * HEAD_DIM + N_BUCKETS * N_HEADS_B].reshape(shapes["rel_bias_table"])
    return out


_WEIGHTS = ["attn_norm_g", "w_in", "q_norm_g", "k_norm_g", "sink_logits", "w_out", "mlp_norm_g", "w_up", "w_down",
            "ple_w", "ple_norm_g", "gate_norm_g", "w_gate", "rel_bias_table", "final_norm_g"]
_BIG = ["w_in", "w_out", "w_up", "w_down", "ple_w", "w_gate"]


def kernel(x, p, attn_norm_g, w_in, q_norm_g, k_norm_g, sink_logits, w_out, mlp_norm_g, w_up, w_down, ple_w, ple_norm_g, gate_norm_g, w_gate, rel_bias_table, final_norm_g, loss_target, m_attn_norm_g, m_w_in, m_q_norm_g, m_k_norm_g, m_sink_logits, m_w_out, m_mlp_norm_g, m_w_up, m_w_down, m_ple_w, m_ple_norm_g, m_gate_norm_g, m_w_gate, m_rel_bias_table, m_final_norm_g, v_attn_norm_g, v_w_in, v_q_norm_g, v_k_norm_g, v_sink_logits, v_w_out, v_mlp_norm_g, v_w_up, v_w_down, v_ple_w, v_ple_norm_g, v_gate_norm_g, v_w_gate, v_rel_bias_table, v_final_norm_g):
    given = dict(locals())
    w = {n: given[n] for n in _WEIGHTS}
    m = {n: given["m_" + n] for n in _WEIGHTS}
    v = {n: given["v_" + n] for n in _WEIGHTS}
    d = x.shape[-1]

    shards = {n: w[n][0].astype(BF16) for n in _BIG}
    small = {
        "attn_norm_g": w["attn_norm_g"], "mlp_norm_g": w["mlp_norm_g"], "ple_norm_g": w["ple_norm_g"],
        "gate_norm_g": w["gate_norm_g"], "final_norm_g": w["final_norm_g"].reshape(1, d),
        "q_norm_g": w["q_norm_g"], "k_norm_g": w["k_norm_g"], "sink_logits": w["sink_logits"],
        "rel_bias_table": w["rel_bias_table"],
    }

    def update(n, parts):
        res = _sum_adamw(parts, w[n][0], m[n][0], v[n][0], name="adamw_" + n)
        return [t.reshape(w[n].shape) for t in res]

    loss_part, grad_x, big, small_g = _local_step(x[0], p[0, 0], loss_target[0], shards, small, update)
    loss = lax.psum(loss_part[0, 0], ("x", "y", "c"))
    grads, deltas, new_m, new_v = [{n: big[n][i] for n in _BIG} for i in range(4)]

    shapes = {n: w[n].shape for n in _WEIGHTS if n not in _BIG}
    pack = _pack_small(small_g, d)
    pack = pack.at[_PACK_ROWS - 1, :1].add(0.0 * grads["w_in"][0, 0, :1])
    g_small = _allreduce_small(pack)
    d_small, m_small, v_small = _adamw_small(g_small, _pack_small(w, d), _pack_small(m, d), _pack_small(v, d))
    grads.update(_unpack_small(g_small, shapes))
    deltas.update(_unpack_small(d_small, shapes))
    new_m.update(_unpack_small(m_small, shapes))
    new_v.update(_unpack_small(v_small, shapes))

    return (loss, grad_x[None], *[grads[n] for n in _WEIGHTS], *[deltas[n] for n in _WEIGHTS],
            *[new_m[n] for n in _WEIGHTS], *[new_v[n] for n in _WEIGHTS])
```

```python
import functools
import math

import jax
import jax.numpy as jnp
import numpy as np
from jax import lax
from jax.experimental import pallas as pl
from jax.experimental.pallas import tpu as pltpu

F32 = jnp.float32
BF16 = jnp.bfloat16

HEAD_DIM = 128
N_HEADS_A = 8
N_KV_A = 2
N_HEADS_B = 8
N_KV_B = 2
GROUP = 4
GRID_W = 64
BLOCK_Q = 128
WINDOW = 128
N_BUCKETS = 32
MAX_DISTANCE = 128
ROPE_THETA = 10000.0
EPS = 1e-6
NEG_INF = -1e30
ATT_SCALE = HEAD_DIM ** -0.5
LOG2E = math.log2(math.e)
LN2 = math.log(2.0)
Q_SCALE = ATT_SCALE * LOG2E
PAD_LO, PAD_HI = 256, 128

ADAM_LR = 0.001
ADAM_B1 = 0.9
ADAM_B2 = 0.999
ADAM_EPS = 1e-08
ADAM_WD = 0.01
ADAM_STEP = 10

N_CHIPS = 4
N_DEV = 8
COL_QA, COL_KA, COL_VA, COL_QB, COL_KB, COL_VB = 0, 8, 10, 12, 20, 22
N_COLS = 24

VMEM_LIMIT = 52 * 1024 * 1024


def _params(sem=None, collective_id=None):
    return pltpu.CompilerParams(dimension_semantics=sem, vmem_limit_bytes=VMEM_LIMIT, collective_id=collective_id)


_ANY = pl.BlockSpec(memory_space=pl.ANY)
_MESH = pl.DeviceIdType.MESH
SIBLING_BARRIER_ID = 1


def _sibling():
    return (lax.axis_index("x"), lax.axis_index("y"), 1 - lax.axis_index("c"))


class _Comm:
    def __init__(self, inputs, out_shapes, sems, start, finish, aliases=None):
        self.inputs, self.out_shapes, self.sems = list(inputs), list(out_shapes), list(sems)
        self.start, self.finish, self.aliases = start, finish, dict(aliases or {})


def _call(body, *, name, grid, in_specs, out_specs, out_shape, args, scratch_shapes=(), sem=None, comm=None,
          after=None, aliases=None):
    in_specs, out_specs, out_shape = list(in_specs), list(out_specs), list(out_shape)
    scratch_shapes = list(scratch_shapes)
    n_in, n_out, n_sc = len(in_specs), len(out_specs), len(scratch_shapes)
    behind = [] if after is None else [after]
    aliases = dict(aliases or {})
    if comm is None:
        res = pl.pallas_call(
            (lambda *refs: body(*refs[:n_in], *refs[n_in + len(behind):])) if behind else body,
            name=name, grid=grid, in_specs=in_specs + [_ANY] * len(behind), out_specs=out_specs,
            out_shape=out_shape, scratch_shapes=scratch_shapes, input_output_aliases=aliases,
            compiler_params=_params(sem))(*args, *behind)
        return list(res), []
    assert not behind
    c_in, c_out = len(comm.inputs), len(comm.out_shapes)

    def hosted(*refs):
        pos = [0]

        def take(n):
            pos[0] += n
            return refs[pos[0] - n:pos[0]]

        ins, c_ins, outs, c_outs, scr = take(n_in), take(c_in), take(n_out), take(c_out), take(n_sc)
        c_sems = refs[pos[0]:]
        ids = [pl.program_id(a) for a in range(len(grid))]
        first = functools.reduce(jnp.logical_and, [i == 0 for i in ids])
        last = functools.reduce(jnp.logical_and, [i == g - 1 for i, g in zip(ids, grid)])

        @pl.when(first)
        def _():
            barrier = pltpu.get_barrier_semaphore()
            pl.semaphore_signal(barrier, inc=1, device_id=_sibling(), device_id_type=_MESH)
            pl.semaphore_wait(barrier, 1)
            comm.start(c_ins, c_outs, c_sems)

        body(*ins, *outs, *scr)

        @pl.when(last)
        def _():
            comm.finish(c_ins, c_outs, c_sems)

    res = pl.pallas_call(
        hosted, name=name, grid=grid, in_specs=in_specs + [_ANY] * c_in, out_specs=out_specs + [_ANY] * c_out,
        out_shape=out_shape + comm.out_shapes, scratch_shapes=scratch_shapes + comm.sems,
        input_output_aliases={**aliases, **{n_in + i: n_out + o for i, o in comm.aliases.items()}},
        compiler_params=_params(("arbitrary",) * len(grid), SIBLING_BARRIER_ID))(*args, *comm.inputs)
    return list(res[:n_out]), list(res[n_out:])


def _matmul(a, b, *, mode, out_dtypes, name, epilogue=None, extras=(), bm=1024, bn=1024, bk=2048,
            out_stack=0, comm=None, after=None):
    stacked = b.ndim == 3
    if mode == "nn":
        m, k = a.shape
        if stacked:
            nj, kb, ns = b.shape
            n, ks = nj * ns, k
        else:
            kb, n = b.shape
            ns, ks = n, k
        dn = (((1,), (0,)), ((), ()))
    elif mode == "nt":
        m, k = a.shape
        if stacked:
            nj, n, ks = b.shape
            kb = nj * ks
        else:
            n, kb = b.shape
            ks = kb
        ns = n
        dn = (((1,), (1,)), ((), ()))
    else:
        k, m = a.shape
        kb, n = b.shape
        ns, ks = n, k
        dn = (((0,), (0,)), ((), ()))
    assert k == kb and not (stacked and mode == "tn")
    ns_out = n // out_stack if out_stack else n
    bm, bn, bk = min(bm, m), min(bn, ns, ns_out), min(bk, ks)
    assert m % bm == 0 and ns % bn == 0 and ns_out % bn == 0 and ks % bk == 0
    gm, gn, gk = m // bm, n // bn, k // bk

    if mode == "tn":
        a_spec = pl.BlockSpec((bk, bm), lambda i, j, q: (q, i))
    else:
        a_spec = pl.BlockSpec((bm, bk), lambda i, j, q: (i, q))
    if mode == "nt":
        if stacked:
            per = ks // bk
            b_spec = pl.BlockSpec((None, bn, bk), lambda i, j, q: (q // per, j, q % per))
        else:
            b_spec = pl.BlockSpec((bn, bk), lambda i, j, q: (j, q))
    else:
        if stacked:
            per = ns // bn
            b_spec = pl.BlockSpec((None, bk, bn), lambda i, j, q: (j // per, q, j % per))
        else:
            b_spec = pl.BlockSpec((bk, bn), lambda i, j, q: (q, j))
    ex_spec = pl.BlockSpec((bm, bn), lambda i, j, q: (i, j))
    if out_stack:
        per_o = ns_out // bn
        o_spec = pl.BlockSpec((None, bm, bn), lambda i, j, q: (j // per_o, i, j % per_o))
        o_shape = (out_stack, m, ns_out)
    else:
        o_spec = ex_spec
        o_shape = (m, n)
    n_ex, n_out = len(extras), len(out_dtypes)

    def body(a_ref, b_ref, *rest):
        ex, outs = rest[:n_ex], rest[n_ex:n_ex + n_out]
        part = lax.dot_general(a_ref[...], b_ref[...], dn, preferred_element_type=F32)

        def finish(acc):
            res = epilogue(acc, *[e[...] for e in ex]) if epilogue else (acc,)
            for o, r in zip(outs, res):
                o[...] = r.astype(o.dtype)

        if gk == 1:
            finish(part)
        else:
            acc_ref = rest[-1]
            q = pl.program_id(2)

            @pl.when(q == 0)
            def _():
                acc_ref[...] = part

            @pl.when(q > 0)
            def _():
                acc_ref[...] += part

            @pl.when(q == gk - 1)
            def _():
                finish(acc_ref[...])

    res, c_res = _call(
        body, name=name, grid=(gm, gn, gk),
        in_specs=[a_spec, b_spec] + [ex_spec] * n_ex,
        out_specs=[o_spec] * n_out,
        out_shape=[jax.ShapeDtypeStruct(o_shape, dt) for dt in out_dtypes],
        scratch_shapes=[pltpu.VMEM((bm, bn), F32)] if gk > 1 else [],
        sem=("parallel", "parallel", "arbitrary"), args=(a, b, *extras), comm=comm, after=after)
    res = res[0] if n_out == 1 else res
    return res if comm is None else (res, c_res)


def _rms_fwd(x, g, *, name, tm=256, comm=None):
    s, d = x.shape
    tm = min(tm, s)

    def body(x_ref, g_ref, o_ref):
        xf = x_ref[...]
        r = lax.rsqrt(jnp.mean(xf * xf, axis=-1, keepdims=True) + EPS)
        o_ref[...] = (xf * r * g_ref[...]).astype(o_ref.dtype)

    res, c_res = _call(
        body, name=name, grid=(s // tm,),
        in_specs=[pl.BlockSpec((tm, d), lambda i: (i, 0)), pl.BlockSpec((1, d), lambda i: (0, 0))],
        out_specs=[pl.BlockSpec((tm, d), lambda i: (i, 0))],
        out_shape=[jax.ShapeDtypeStruct((s, d), BF16)],
        sem=("parallel",), args=(x, g), comm=comm)
    return res[0] if comm is None else (res[0], c_res)


def _rms_bwd(x, dy, g, add, *, name, want_bf16, tm=256):
    s, d = x.shape
    tm = min(tm, s)

    def body(x_ref, dy_ref, g_ref, add_ref, dx_ref, *rest):
        dg_ref = rest[-1]
        i = pl.program_id(0)
        xf = x_ref[...]
        dyf = dy_ref[...].astype(F32)
        r = lax.rsqrt(jnp.mean(xf * xf, axis=-1, keepdims=True) + EPS)
        xh = xf * r
        dyg = dyf * g_ref[...]
        dx = r * (dyg - xh * jnp.mean(dyg * xh, axis=-1, keepdims=True))
        tot = add_ref[...] + dx
        dx_ref[...] = tot
        if want_bf16:
            rest[0][...] = tot.astype(BF16)
        part = jnp.sum(dyf * xh, axis=0, keepdims=True)

        @pl.when(i == 0)
        def _():
            dg_ref[...] = part

        @pl.when(i > 0)
        def _():
            dg_ref[...] += part

    row = pl.BlockSpec((tm, d), lambda i: (i, 0))
    vec = pl.BlockSpec((1, d), lambda i: (0, 0))
    out_specs = [row] + ([row] if want_bf16 else []) + [vec]
    out_shape = [jax.ShapeDtypeStruct((s, d), F32)]
    if want_bf16:
        out_shape.append(jax.ShapeDtypeStruct((s, d), BF16))
    out_shape.append(jax.ShapeDtypeStruct((1, d), F32))
    return pl.pallas_call(
        body,
        name=name,
        grid=(s // tm,),
        in_specs=[row, row, vec, row],
        out_specs=out_specs,
        out_shape=out_shape,
        compiler_params=_params(("arbitrary",)),
    )(x, dy, g, add)


def _tail(h2, gate, pp, target, g_ple, g_final, *, tm=128):
    s, d = h2.shape
    tm = min(tm, s)

    def body(h2_ref, gate_ref, pp_ref, t_ref, gp_ref, gf_ref, dh3_ref, dz_ref, dpp_ref, dgf_ref, dgp_ref, loss_ref):
        i = pl.program_id(0)
        ppf = pp_ref[...]
        gate_v = gate_ref[...]
        r_p = lax.rsqrt(jnp.mean(ppf * ppf, axis=-1, keepdims=True) + EPS)
        eh = ppf * r_p
        e = eh * gp_ref[...]
        h3 = h2_ref[...] + gate_v * e
        r_f = lax.rsqrt(jnp.mean(h3 * h3, axis=-1, keepdims=True) + EPS)
        yh = h3 * r_f
        diff = yh * gf_ref[...] - t_ref[...]
        loss_part = 0.5 * jnp.sum(jnp.mean(diff * diff, axis=-1, keepdims=True), axis=0, keepdims=True)
        dy = diff / d
        dgf = jnp.sum(dy * yh, axis=0, keepdims=True)
        dyg = dy * gf_ref[...]
        dh3 = r_f * (dyg - yh * jnp.mean(dyg * yh, axis=-1, keepdims=True))
        dh3_ref[...] = dh3
        de = dh3 * gate_v
        dz_ref[...] = (dh3 * e * gate_v * (1.0 - gate_v)).astype(BF16)
        dgp = jnp.sum(de * eh, axis=0, keepdims=True)
        deg = de * gp_ref[...]
        dpp_ref[...] = (r_p * (deg - eh * jnp.mean(deg * eh, axis=-1, keepdims=True))).astype(BF16)
        loss_row = jnp.broadcast_to(loss_part, (1, 128))

        @pl.when(i == 0)
        def _():
            dgf_ref[...] = dgf
            dgp_ref[...] = dgp
            loss_ref[...] = loss_row

        @pl.when(i > 0)
        def _():
            dgf_ref[...] += dgf
            dgp_ref[...] += dgp
            loss_ref[...] += loss_row

    row = pl.BlockSpec((tm, d), lambda i: (i, 0))
    vec = pl.BlockSpec((1, d), lambda i: (0, 0))
    return pl.pallas_call(
        body,
        name="tail_fwd_bwd",
        grid=(s // tm,),
        in_specs=[row, row, row, row, vec, vec],
        out_specs=[row, row, row, vec, vec, pl.BlockSpec((1, 128), lambda i: (0, 0))],
        out_shape=[
            jax.ShapeDtypeStruct((s, d), F32),
            jax.ShapeDtypeStruct((s, d), BF16),
            jax.ShapeDtypeStruct((s, d), BF16),
            jax.ShapeDtypeStruct((1, d), F32),
            jax.ShapeDtypeStruct((1, d), F32),
            jax.ShapeDtypeStruct((1, 128), F32),
        ],
        compiler_params=_params(("arbitrary",)),
    )(h2, gate, pp, target, g_ple, g_final)


def _rope_tables(s):
    t = jnp.arange(s, dtype=jnp.int32)
    row = (t // GRID_W).astype(F32)
    col = (t % GRID_W).astype(F32)
    half = HEAD_DIM // 2
    inv_freq = ROPE_THETA ** (-jnp.arange(0, half, 2, dtype=F32) / half)
    ang_r = row[:, None] * inv_freq
    ang_c = col[:, None] * inv_freq
    cr, sr, cc, sc = jnp.cos(ang_r), jnp.sin(ang_r), jnp.cos(ang_c), jnp.sin(ang_c)
    cos_t = jnp.concatenate([cr, cr, cc, cc], axis=-1)
    sin_t = jnp.concatenate([-sr, sr, -sc, sc], axis=-1)
    return cos_t, sin_t


def _swap_quarters(x):
    lane = lax.broadcasted_iota(jnp.int32, x.shape, x.ndim - 1)
    up = pltpu.roll(x, HEAD_DIM - 32, x.ndim - 1)
    down = pltpu.roll(x, 32, x.ndim - 1)
    return jnp.where((lane % 64) < 32, up, down)


def _cols(first, count=1):
    return slice(first * HEAD_DIM, (first + count) * HEAD_DIM)


def _qk_prep(proj, g_q, g_k, cos_t, sin_t, *, tm=256, comm=None):
    s, n = proj.shape
    tm = min(tm, s)

    def body(x_ref, gq_ref, gk_ref, c_ref, s_ref, o_ref):
        cos_v, sin_v = c_ref[...], s_ref[...]
        for h in range(COL_VA):
            x = x_ref[:, _cols(h)]
            g = gq_ref[...] if h < COL_KA else gk_ref[...]
            xn = x * lax.rsqrt(jnp.mean(x * x, axis=-1, keepdims=True) + EPS) * g
            xr = xn * cos_v + _swap_quarters(xn) * sin_v
            if h < COL_KA:
                xr = xr * Q_SCALE
            o_ref[:, _cols(h)] = xr.astype(BF16)
        o_ref[:, _cols(COL_VA, 2)] = x_ref[:, _cols(COL_VA, 2)].astype(BF16)
        o_ref[:, _cols(COL_QB, N_HEADS_B)] = (x_ref[:, _cols(COL_QB, N_HEADS_B)] * Q_SCALE).astype(BF16)
        o_ref[:, _cols(COL_KB, 4)] = x_ref[:, _cols(COL_KB, 4)].astype(BF16)

    row = pl.BlockSpec((tm, n), lambda i: (i, 0))
    tab = pl.BlockSpec((tm, HEAD_DIM), lambda i: (i, 0))
    vec = pl.BlockSpec((1, HEAD_DIM), lambda i: (0, 0))
    res, c_res = _call(
        body, name="qk_prep", grid=(s // tm,),
        in_specs=[row, vec, vec, tab, tab],
        out_specs=[row],
        out_shape=[jax.ShapeDtypeStruct((s, n), BF16)],
        sem=("parallel",), args=(proj, g_q, g_k, cos_t, sin_t), comm=comm)
    return res[0] if comm is None else (res[0], c_res)


def _qk_bwd(dqa, dka, dva, dqb, dkpad, dvpad, proj, g_q, g_k, cos_t, sin_t, *, comm=None):
    s, n = proj.shape
    tm = min(PAD_LO, s)
    assert PAD_LO % tm == 0
    lo = PAD_LO // tm

    def body(dqa_ref, dka_ref, dva_ref, dqb_ref, dkb_ref, dvb_ref, x_ref, gq_ref, gk_ref, c_ref, s_ref,
             o_ref, dgq_ref, dgk_ref):
        i = pl.program_id(0)
        cos_v, sin_v = c_ref[...], s_ref[...]

        def head(d, x, g):
            dn = d * cos_v + _swap_quarters(d * sin_v)
            r = lax.rsqrt(jnp.mean(x * x, axis=-1, keepdims=True) + EPS)
            xh = x * r
            dng = dn * g
            dx = r * (dng - xh * jnp.mean(dng * xh, axis=-1, keepdims=True))
            return dx.astype(BF16), jnp.sum(dn * xh, axis=0, keepdims=True)

        acc_q = jnp.zeros((1, HEAD_DIM), F32)
        acc_k = jnp.zeros((1, HEAD_DIM), F32)
        for h in range(N_HEADS_A):
            o_ref[:, _cols(h)], part = head(dqa_ref[:, _cols(h)] * ATT_SCALE, x_ref[:, _cols(h)], gq_ref[...])
            acc_q = acc_q + part
        for h in range(N_KV_A):
            o_ref[:, _cols(COL_KA + h)], part = head(dka_ref[:, _cols(h)] * LN2, x_ref[:, _cols(COL_KA + h)],
                                                     gk_ref[...])
            acc_k = acc_k + part
        o_ref[:, _cols(COL_VA, 2)] = dva_ref[...].astype(BF16)
        o_ref[:, _cols(COL_QB, N_HEADS_B)] = (dqb_ref[...] * ATT_SCALE).astype(BF16)
        o_ref[:, _cols(COL_KB, 2)] = (dkb_ref[...] * LN2).astype(BF16)
        o_ref[:, _cols(COL_VB, 2)] = dvb_ref[...].astype(BF16)

        @pl.when(i == 0)
        def _():
            dgq_ref[...] = acc_q
            dgk_ref[...] = acc_k

        @pl.when(i > 0)
        def _():
            dgq_ref[...] += acc_q
            dgk_ref[...] += acc_k

    def rows(width, shift=0):
        return pl.BlockSpec((tm, width), lambda i: (i + shift, 0))

    kv_w = N_KV_A * HEAD_DIM
    q_w = N_HEADS_A * HEAD_DIM
    vec = pl.BlockSpec((1, HEAD_DIM), lambda i: (0, 0))
    res, c_res = _call(
        body, name="qk_bwd", grid=(s // tm,),
        in_specs=[rows(q_w), rows(kv_w), rows(kv_w), rows(q_w), rows(kv_w, lo), rows(kv_w, lo), rows(n),
                  vec, vec, rows(HEAD_DIM), rows(HEAD_DIM)],
        out_specs=[rows(n), vec, vec],
        out_shape=[
            jax.ShapeDtypeStruct((s, n), BF16),
            jax.ShapeDtypeStruct((1, HEAD_DIM), F32),
            jax.ShapeDtypeStruct((1, HEAD_DIM), F32),
        ],
        sem=("arbitrary",), args=(dqa, dka, dva, dqb, dkpad, dvpad, proj, g_q, g_k, cos_t, sin_t), comm=comm)
    return res if comm is None else (res, c_res)


_NT = (((1,), (1,)), ((), ()))
_TN = (((0,), (0,)), ((), ()))


def _attn_a_fwd(pb, *, tq=2048, sub=256, comm=None, after=None):
    s = pb.shape[0]
    tq = min(tq, s)

    sub = min(sub, tq)

    def body(q_ref, k_ref, v_ref, o_ref, lse_ref):
        k = k_ref[...]
        v = v_ref[...]
        for r in range(tq // sub):
            rows = pl.ds(r * sub, sub)
            sc = lax.dot_general(q_ref[rows, :], k, _NT, preferred_element_type=F32)
            m = jnp.max(sc, axis=-1, keepdims=True)
            p = jnp.exp2(sc - m)
            l = jnp.sum(p, axis=-1, keepdims=True)
            o = jnp.dot(p.astype(BF16), v, preferred_element_type=F32)
            o_ref[rows, :] = (o / l).astype(BF16)
            lse_ref[rows, :] = jnp.broadcast_to(m + jnp.log2(l), (sub, HEAD_DIM))

    res, c_res = _call(
        body, name="attn_a_fwd", grid=(N_HEADS_A, s // tq),
        in_specs=[
            pl.BlockSpec((tq, HEAD_DIM), lambda h, i: (i, COL_QA + h)),
            pl.BlockSpec((s, HEAD_DIM), lambda h, i: (0, COL_KA + h // GROUP)),
            pl.BlockSpec((s, HEAD_DIM), lambda h, i: (0, COL_VA + h // GROUP)),
        ],
        out_specs=[
            pl.BlockSpec((tq, HEAD_DIM), lambda h, i: (i, h)),
            pl.BlockSpec((None, tq, HEAD_DIM), lambda h, i: (h, i, 0)),
        ],
        out_shape=[
            jax.ShapeDtypeStruct((s, (N_HEADS_A + N_HEADS_B) * HEAD_DIM), BF16),
            jax.ShapeDtypeStruct((N_HEADS_A, s, HEAD_DIM), F32),
        ],
        sem=("parallel", "parallel"), args=(pb, pb, pb), comm=comm, after=after)
    return res if comm is None else (res, c_res)


def _attn_a_bwd(pb, att, datt, lse, *, tq=2048, sub=256, comm=None):
    s = pb.shape[0]
    tq = min(tq, s)
    sub = min(sub, tq)

    def body(q_ref, k_ref, v_ref, o_ref, do_ref, lse_ref, dq_ref, dk_ref, dv_ref):
        first = jnp.logical_and(pl.program_id(1) == 0, pl.program_id(2) == 0)
        k = k_ref[...]
        v = v_ref[...]
        dk = dv = None
        for r in range(tq // sub):
            rows = pl.ds(r * sub, sub)
            q = q_ref[rows, :]
            do = do_ref[rows, :]
            sc = lax.dot_general(q, k, _NT, preferred_element_type=F32)
            p = jnp.exp2(sc - lse_ref[rows, :][:, :1])
            dp = lax.dot_general(do, v, _NT, preferred_element_type=F32)
            delta = jnp.sum(do.astype(F32) * o_ref[rows, :].astype(F32), axis=-1, keepdims=True)
            ds = (p * (dp - delta)).astype(BF16)
            dq_ref[rows, :] = jnp.dot(ds, k, preferred_element_type=F32)
            dk_r = lax.dot_general(ds, q, _TN, preferred_element_type=F32)
            dv_r = lax.dot_general(p.astype(BF16), do, _TN, preferred_element_type=F32)
            dk = dk_r if dk is None else dk + dk_r
            dv = dv_r if dv is None else dv + dv_r

        @pl.when(first)
        def _():
            dk_ref[...] = dk
            dv_ref[...] = dv

        @pl.when(jnp.logical_not(first))
        def _():
            dk_ref[...] += dk
            dv_ref[...] += dv

    qmap = lambda kv, g, i: (i, kv * GROUP + g)
    res, c_res = _call(
        body, name="attn_a_bwd", grid=(N_KV_A, GROUP, s // tq),
        in_specs=[
            pl.BlockSpec((tq, HEAD_DIM), lambda kv, g, i: (i, COL_QA + kv * GROUP + g)),
            pl.BlockSpec((s, HEAD_DIM), lambda kv, g, i: (0, COL_KA + kv)),
            pl.BlockSpec((s, HEAD_DIM), lambda kv, g, i: (0, COL_VA + kv)),
            pl.BlockSpec((tq, HEAD_DIM), qmap),
            pl.BlockSpec((tq, HEAD_DIM), qmap),
            pl.BlockSpec((None, tq, HEAD_DIM), lambda kv, g, i: (kv * GROUP + g, i, 0)),
        ],
        out_specs=[
            pl.BlockSpec((tq, HEAD_DIM), qmap),
            pl.BlockSpec((s, HEAD_DIM), lambda kv, g, i: (0, kv)),
            pl.BlockSpec((s, HEAD_DIM), lambda kv, g, i: (0, kv)),
        ],
        out_shape=[
            jax.ShapeDtypeStruct((s, N_HEADS_A * HEAD_DIM), F32),
            jax.ShapeDtypeStruct((s, N_KV_A * HEAD_DIM), F32),
            jax.ShapeDtypeStruct((s, N_KV_A * HEAD_DIM), F32),
        ],
        sem=("arbitrary", "arbitrary", "arbitrary"), args=(pb, pb, pb, att, datt, lse), comm=comm)
    return res if comm is None else (res, c_res)


def _t5_bucket(rel):
    nb = N_BUCKETS // 2
    ret = jnp.where(rel > 0, nb, 0)
    n = jnp.abs(rel)
    max_exact = nb // 2
    nf = jnp.maximum(n, 1).astype(F32)
    large = max_exact + (jnp.log(nf / max_exact) / math.log(MAX_DISTANCE / max_exact)
                         * (nb - max_exact)).astype(jnp.int32)
    large = jnp.minimum(large, nb - 1)
    return ret + jnp.where(n < max_exact, n, large)


def _band_buckets():
    r = jnp.arange(BLOCK_Q, dtype=jnp.int32)
    j = jnp.arange(3 * BLOCK_Q, dtype=jnp.int32)
    return _t5_bucket((j[None, :] - BLOCK_Q) - r[:, None])


def _band_bias(bucket, table_ref, h):
    acc = jnp.zeros(bucket.shape, F32)
    for b in range(N_BUCKETS):
        acc = jnp.where(bucket == b, table_ref[b, h], acc)
    return acc


GQ = GROUP * BLOCK_Q


def _stack_heads(x):
    return jnp.concatenate([x[:, _cols(g)] for g in range(GROUP)], axis=0)


def _unstack_heads(x):
    return jnp.concatenate([x[g * BLOCK_Q:(g + 1) * BLOCK_Q] for g in range(GROUP)], axis=1)


def _group_bias(bucket, table_ref, kv):
    return jnp.concatenate([_band_bias(bucket, table_ref, kv * GROUP + g) * LOG2E for g in range(GROUP)], axis=0)


def _group_sink(sink_ref, kv):
    head = lax.broadcasted_iota(jnp.int32, (GQ, 1), 0) // BLOCK_Q
    snk = jnp.zeros((GQ, 1), F32)
    for g in range(GROUP):
        snk = jnp.where(head == g, sink_ref[0, kv * GROUP + g] * LOG2E, snk)
    return snk


def _band_mask(n, s):
    r = lax.broadcasted_iota(jnp.int32, (GQ, 3 * BLOCK_Q), 0) % BLOCK_Q
    j = lax.broadcasted_iota(jnp.int32, (GQ, 3 * BLOCK_Q), 1)
    rel = j - BLOCK_Q - r
    kabs = n * BLOCK_Q + j - BLOCK_Q
    return (jnp.abs(rel) <= WINDOW) & (kabs >= 0) & (kabs < s)


def _band_start(n):
    return pl.multiple_of(n * BLOCK_Q + (PAD_LO - BLOCK_Q), BLOCK_Q)


def _attn_b_fwd(pb, kpad, vpad, bucket, table, sink, att, *, comm=None):
    s = pb.shape[0]
    nblk = s // BLOCK_Q
    sp = kpad.shape[0]

    def body(table_ref, sink_ref, q0_ref, q1_ref, k_ref, v_ref, bucket_ref, _, o_ref, lse_ref, bias_ref):
        n = pl.program_id(0)

        @pl.when(n == 0)
        def _():
            for kv in range(N_KV_B):
                bias_ref[kv * GQ:(kv + 1) * GQ, :] = _group_bias(bucket_ref[...], table_ref, kv)

        band = pl.ds(_band_start(n), 3 * BLOCK_Q)
        mask = _band_mask(n, s)
        for kv, q_ref in enumerate((q0_ref, q1_ref)):
            kb = k_ref[band, _cols(kv)]
            vb = v_ref[band, _cols(kv)]
            sc = lax.dot_general(_stack_heads(q_ref[...]), kb, _NT, preferred_element_type=F32)
            sc = jnp.where(mask, sc + bias_ref[kv * GQ:(kv + 1) * GQ, :], NEG_INF)
            snk = _group_sink(sink_ref, kv)
            m = jnp.maximum(jnp.max(sc, axis=-1, keepdims=True), snk)
            p = jnp.exp2(sc - m)
            l = jnp.sum(p, axis=-1, keepdims=True) + jnp.exp2(snk - m)
            o = jnp.dot(p.astype(BF16), vb, preferred_element_type=F32)
            o_ref[:, _cols(kv * GROUP, GROUP)] = _unstack_heads((o / l).astype(BF16))
            lse = m + jnp.log2(l)
            for g in range(GROUP):
                lse_ref[kv * GROUP + g] = jnp.broadcast_to(lse[g * BLOCK_Q:(g + 1) * BLOCK_Q], (BLOCK_Q, HEAD_DIM))

    smem = pl.BlockSpec(memory_space=pltpu.SMEM)
    wide = GROUP * HEAD_DIM
    whole = pl.BlockSpec((sp, N_KV_B * HEAD_DIM), lambda n: (0, 0))
    res, c_res = _call(
        body, name="attn_b_fwd", grid=(nblk,),
        in_specs=[
            smem,
            smem,
            pl.BlockSpec((BLOCK_Q, wide), lambda n: (n, COL_QB // GROUP)),
            pl.BlockSpec((BLOCK_Q, wide), lambda n: (n, COL_QB // GROUP + 1)),
            whole,
            whole,
            pl.BlockSpec((BLOCK_Q, 3 * BLOCK_Q), lambda n: (0, 0)),
            _ANY,
        ],
        out_specs=[
            pl.BlockSpec((BLOCK_Q, N_HEADS_B * HEAD_DIM), lambda n: (n, 1)),
            pl.BlockSpec((N_HEADS_B, BLOCK_Q, HEAD_DIM), lambda n: (0, n, 0)),
        ],
        out_shape=[
            jax.ShapeDtypeStruct(att.shape, BF16),
            jax.ShapeDtypeStruct((N_HEADS_B, s, HEAD_DIM), F32),
        ],
        scratch_shapes=[pltpu.VMEM((N_KV_B * GQ, 3 * BLOCK_Q), F32)],
        sem=("arbitrary",), args=(table, sink, pb, pb, kpad, vpad, bucket, att), comm=comm,
        aliases={7: 0})
    return res if comm is None else (res, c_res)


def _attn_b_bwd(pb, kpad, vpad, att, datt, lse, bucket, table, sink, *, comm=None, after=None):
    s = pb.shape[0]
    nblk = s // BLOCK_Q
    sp = kpad.shape[0]

    def body(table_ref, sink_ref, q0_ref, q1_ref, k_ref, v_ref, o_ref, do_ref, lse_ref, bucket_ref,
             dq_ref, dk_ref, dv_ref, dtab_ref, dsink_ref, bias_ref, dbias_ref):
        n = pl.program_id(0)

        @pl.when(n == 0)
        def _():
            dk_ref[...] = jnp.zeros_like(dk_ref)
            dv_ref[...] = jnp.zeros_like(dv_ref)
            dbias_ref[...] = jnp.zeros_like(dbias_ref)
            dsink_ref[...] = jnp.zeros_like(dsink_ref)
            for kv in range(N_KV_B):
                bias_ref[kv * GQ:(kv + 1) * GQ, :] = _group_bias(bucket_ref[...], table_ref, kv)

        band = pl.ds(_band_start(n), 3 * BLOCK_Q)
        mask = _band_mask(n, s)
        for kv, q_ref in enumerate((q0_ref, q1_ref)):
            wide_cols = _cols(kv * GROUP, GROUP)
            q = _stack_heads(q_ref[...])
            do = _stack_heads(do_ref[:, wide_cols])
            o = _stack_heads(o_ref[:, wide_cols])
            kb = k_ref[band, _cols(kv)]
            vb = v_ref[band, _cols(kv)]
            lse = jnp.concatenate([lse_ref[kv * GROUP + g][:, :1] for g in range(GROUP)], axis=0)
            sc = lax.dot_general(q, kb, _NT, preferred_element_type=F32)
            sc = jnp.where(mask, sc + bias_ref[kv * GQ:(kv + 1) * GQ, :], NEG_INF)
            p = jnp.exp2(sc - lse)
            dp = lax.dot_general(do, vb, _NT, preferred_element_type=F32)
            delta = jnp.sum(do.astype(F32) * o.astype(F32), axis=-1, keepdims=True)
            ds = p * (dp - delta)
            dsb = ds.astype(BF16)
            dq_ref[:, wide_cols] = _unstack_heads(jnp.dot(dsb, kb, preferred_element_type=F32))
            dk_ref[band, _cols(kv)] += lax.dot_general(dsb, q, _TN, preferred_element_type=F32)
            dv_ref[band, _cols(kv)] += lax.dot_general(p.astype(BF16), do, _TN, preferred_element_type=F32)
            dbias_ref[kv * GQ:(kv + 1) * GQ, :] += ds
            sink_part = -jnp.exp2(_group_sink(sink_ref, kv) - lse) * delta
            for g in range(GROUP):
                rows = slice(g * BLOCK_Q, (g + 1) * BLOCK_Q)
                dsink_ref[kv * GROUP + g] += jnp.broadcast_to(
                    jnp.sum(sink_part[rows], axis=0, keepdims=True), (1, HEAD_DIM))

        @pl.when(n == nblk - 1)
        def _():
            bucket_v = bucket_ref[...]
            row = lax.broadcasted_iota(jnp.int32, (N_BUCKETS, HEAD_DIM), 0)
            for h in range(N_HEADS_B):
                acc = dbias_ref[h * BLOCK_Q:(h + 1) * BLOCK_Q, :]
                tot = jnp.zeros((N_BUCKETS, HEAD_DIM), F32)
                for b in range(N_BUCKETS):
                    tot = jnp.where(row == b, jnp.sum(jnp.where(bucket_v == b, acc, 0.0), keepdims=True), tot)
                dtab_ref[h] = tot

    smem = pl.BlockSpec(memory_space=pltpu.SMEM)
    wide = GROUP * HEAD_DIM
    whole = pl.BlockSpec((sp, N_KV_B * HEAD_DIM), lambda n: (0, 0))
    group_b = pl.BlockSpec((BLOCK_Q, N_HEADS_B * HEAD_DIM), lambda n: (n, 1))
    res, c_res = _call(
        body, name="attn_b_bwd", grid=(nblk,),
        in_specs=[
            smem,
            smem,
            pl.BlockSpec((BLOCK_Q, wide), lambda n: (n, COL_QB // GROUP)),
            pl.BlockSpec((BLOCK_Q, wide), lambda n: (n, COL_QB // GROUP + 1)),
            whole,
            whole,
            group_b,
            group_b,
            pl.BlockSpec((N_HEADS_B, BLOCK_Q, HEAD_DIM), lambda n: (0, n, 0)),
            pl.BlockSpec((BLOCK_Q, 3 * BLOCK_Q), lambda n: (0, 0)),
        ],
        out_specs=[
            pl.BlockSpec((BLOCK_Q, N_HEADS_B * HEAD_DIM), lambda n: (n, 0)),
            whole,
            whole,
            pl.BlockSpec((N_HEADS_B, N_BUCKETS, HEAD_DIM), lambda n: (0, 0, 0)),
            pl.BlockSpec((N_HEADS_B, 1, HEAD_DIM), lambda n: (0, 0, 0)),
        ],
        out_shape=[
            jax.ShapeDtypeStruct((s, N_HEADS_B * HEAD_DIM), F32),
            jax.ShapeDtypeStruct((sp, N_KV_B * HEAD_DIM), F32),
            jax.ShapeDtypeStruct((sp, N_KV_B * HEAD_DIM), F32),
            jax.ShapeDtypeStruct((N_HEADS_B, N_BUCKETS, HEAD_DIM), F32),
            jax.ShapeDtypeStruct((N_HEADS_B, 1, HEAD_DIM), F32),
        ],
        scratch_shapes=[pltpu.VMEM((N_KV_B * GQ, 3 * BLOCK_Q), F32), pltpu.VMEM((N_KV_B * GQ, 3 * BLOCK_Q), F32)],
        sem=("arbitrary",),
        args=(table, sink, pb, pb, kpad, vpad, att, datt, lse, bucket), comm=comm, after=after)
    return res if comm is None else (res, c_res)


_MESH = pl.DeviceIdType.MESH


def _other_chips(x, y):
    return [(x, 1 - y), (1 - x, y), (1 - x, 1 - y)]


_HBM = pl.BlockSpec(memory_space=pltpu.HBM)
_SEM = pl.BlockSpec(memory_space=pltpu.SEMAPHORE)
_SPLIT = pltpu.CompilerParams(has_side_effects=pltpu.SideEffectType.DATAFLOW_SIDE_EFFECTING)


def _in_hbm(a):
    return pltpu.with_memory_space_constraint(a, pltpu.HBM)


def _my_half(rows):
    c = lax.axis_index("c")
    half = rows // 2
    return pl.ds(pl.multiple_of(c * half, half), half), pl.ds(pl.multiple_of((1 - c) * half, half), half)


def _gather_route(shapes):
    def route(src, land):
        x, y, c = lax.axis_index("x"), lax.axis_index("y"), lax.axis_index("c")
        out = []
        for t, shape in enumerate(shapes):
            mine, _ = _my_half(shape[0])
            for px, py in _other_chips(x, y):
                out.append((src[t].at[mine], land[t].at[2 * x + y, mine], land[t].at[2 * px + py, mine], (px, py, c)))
        return out

    return route


def _exchange_route(n_t):
    def route(src, land):
        x, y, c = lax.axis_index("x"), lax.axis_index("y"), lax.axis_index("c")
        out = []
        for t in range(n_t):
            for px, py in _other_chips(x, y):
                k = 2 * px + py
                out.append((src[t].at[k], land[t].at[2 * (2 * x + y) + c], land[t].at[2 * k + c], (px, py, c)))
        return out

    return route


def _own_slot(shape, dtype, slot, block):
    return lax.dynamic_update_slice(lax.empty(shape, dtype), block[None], (slot,) + (0,) * (len(shape) - 1))


def _split_start(name, srcs, lands, route, after):
    n = len(srcs)

    def body(*refs):
        src, land, send_sems, recv_sems, token = refs[:n], refs[n:2 * n], refs[2 * n + 1], refs[2 * n + 2], refs[-1]
        for i, (src_ref, dst_ref, _, to) in enumerate(route(src, land)):
            pltpu.make_async_remote_copy(src_ref=src_ref, dst_ref=dst_ref, send_sem=send_sems.at[i],
                                         recv_sem=recv_sems.at[i], device_id=to, device_id_type=_MESH).start()
        token[...] = jnp.zeros_like(token)

    sem = pltpu.SemaphoreType.DMA((3 * n,))
    lands = list(lands)
    res = pl.pallas_call(
        body, name=name,
        in_specs=[_HBM] * (2 * n) + [_ANY],
        out_specs=[_SEM, _SEM] + [_HBM] * (2 * n) + [pl.BlockSpec(memory_space=pltpu.VMEM)],
        out_shape=[sem, sem] + [pltpu.HBM(a.shape, a.dtype) for a in list(srcs) + lands]
        + [jax.ShapeDtypeStruct((8, 128), F32)],
        input_output_aliases={i: 2 + i for i in range(2 * n)},
        compiler_params=_SPLIT,
    )(*[_in_hbm(a) for a in srcs], *[_in_hbm(a) for a in lands], after)
    return (res[0], res[1]), res[2:2 + n], res[2 + n:2 + 2 * n], res[-1]


def _split_wait(name, srcs, lands, sems, route, after):
    n = len(srcs)

    def body(*refs):
        src, land, send_sems, recv_sems = refs[:n], refs[n:2 * n], refs[2 * n], refs[2 * n + 1]
        for i, (src_ref, _, dst_ref, to) in enumerate(route(src, land)):
            cp = pltpu.make_async_remote_copy(src_ref=src_ref, dst_ref=dst_ref, send_sem=send_sems.at[i],
                                              recv_sem=recv_sems.at[i], device_id=to, device_id_type=_MESH)
            cp.wait_send()
            cp.wait_recv()

    res = pl.pallas_call(
        body, name=name,
        in_specs=[_HBM] * (2 * n) + [_SEM, _SEM, _ANY],
        out_specs=[_HBM] * (2 * n),
        out_shape=[pltpu.HBM(a.shape, a.dtype) for a in list(srcs) + list(lands)],
        input_output_aliases={i: i for i in range(2 * n)},
        compiler_params=_SPLIT,
    )(*srcs, *lands, sems[0], sems[1], after)
    return res[:n], res[n:]


def _comm_only(name, comm):
    return _call(lambda: None, name=name, grid=(1,), in_specs=[], out_specs=[], out_shape=[], args=(), comm=comm)[1]


def _swap_comm(shards, lands):
    n_t = len(lands)

    def copies(land, sems, later):
        send_sems, recv_sems = sems
        x, y = lax.axis_index("x"), lax.axis_index("y")
        sends, recvs = [], []
        for t in range(n_t):
            mine, other = _my_half(shards[t].shape[0])
            for j, (px, py) in enumerate(_other_chips(x, y)):
                k = 2 * px + py
                for part, out in ((mine, sends), (other, recvs)) if later else ((mine, sends),):
                    out.append(pltpu.make_async_remote_copy(
                        src_ref=land[t].at[k, part], dst_ref=land[t].at[k, part], send_sem=send_sems.at[3 * t + j],
                        recv_sem=recv_sems.at[3 * t + j], device_id=_sibling(), device_id_type=_MESH))
        return sends, recvs

    def start(ins, land, sems):
        for cp in copies(land, sems, False)[0]:
            cp.start()

    def finish(ins, land, sems):
        sends, recvs = copies(land, sems, True)
        for cp in recvs:
            cp.wait_recv()
        for cp in sends:
            cp.wait_send()

    return _Comm(
        lands, [jax.ShapeDtypeStruct(a.shape, a.dtype) for a in lands],
        [pltpu.SemaphoreType.DMA((3 * n_t,)), pltpu.SemaphoreType.DMA((3 * n_t,))],
        start, finish, aliases={t: t for t in range(n_t)})


def _forward_comm(partials, lands):
    n_t = len(lands)

    def copies(land, sems, later):
        send_sems, recv_sems = sems
        x, y, c = lax.axis_index("x"), lax.axis_index("y"), lax.axis_index("c")
        sends, recvs = [], []
        for t in range(n_t):
            for j, k in enumerate([2 * x + y] + [2 * px + py for px, py in _other_chips(x, y)]):
                for slot, out in ((2 * k + c, sends), (2 * k + 1 - c, recvs)) if later else ((2 * k + c, sends),):
                    out.append(pltpu.make_async_remote_copy(
                        src_ref=land[t].at[slot], dst_ref=land[t].at[slot], send_sem=send_sems.at[4 * t + j],
                        recv_sem=recv_sems.at[4 * t + j], device_id=_sibling(), device_id_type=_MESH))
        return sends, recvs

    def start(ins, land, sems):
        for cp in copies(land, sems, False)[0]:
            cp.start()

    def finish(ins, land, sems):
        sends, recvs = copies(land, sems, True)
        for cp in recvs:
            cp.wait_recv()
        for cp in sends:
            cp.wait_send()

    return _Comm(
        lands, [jax.ShapeDtypeStruct(a.shape, a.dtype) for a in lands],
        [pltpu.SemaphoreType.DMA((4 * n_t,)), pltpu.SemaphoreType.DMA((4 * n_t,))],
        start, finish, aliases={t: t for t in range(n_t)})


def _allreduce_small(pack):
    rows, d = pack.shape

    def body(p_ref, sum_ref, all_ref, send_sems, recv_sems):
        x, y, c = lax.axis_index("x"), lax.axis_index("y"), lax.axis_index("c")
        me = 4 * x + 2 * y + c
        all_ref[me] = p_ref[...]
        peers = []
        for dx in range(2):
            for dy in range(2):
                for dc in range(2):
                    if dx or dy or dc:
                        px = 1 - x if dx else x
                        py = 1 - y if dy else y
                        pc = 1 - c if dc else c
                        peers.append((4 * dx + 2 * dy + dc - 1, (px, py, pc)))
        sends = []
        for k, to in peers:
            cp = pltpu.make_async_remote_copy(
                src_ref=p_ref, dst_ref=all_ref.at[me], send_sem=send_sems.at[k], recv_sem=recv_sems.at[k],
                device_id=to, device_id_type=_MESH)
            cp.start()
            sends.append(cp)
        for k, (px, py, pc) in peers:
            pltpu.make_async_remote_copy(
                src_ref=p_ref, dst_ref=all_ref.at[4 * px + 2 * py + pc], send_sem=send_sems.at[k],
                recv_sem=recv_sems.at[k], device_id=(px, py, pc), device_id_type=_MESH).wait_recv()
        for cp in sends:
            cp.wait_send()
        tot = all_ref[0]
        for i in range(1, N_DEV):
            tot = tot + all_ref[i]
        sum_ref[...] = tot

    vm = pl.BlockSpec(memory_space=pltpu.VMEM)
    return pl.pallas_call(
        body,
        name="allreduce_small",
        in_specs=[vm],
        out_specs=vm,
        out_shape=jax.ShapeDtypeStruct((rows, d), F32),
        scratch_shapes=[
            pltpu.VMEM((N_DEV, rows, d), F32),
            pltpu.SemaphoreType.DMA((N_DEV - 1,)),
            pltpu.SemaphoreType.DMA((N_DEV - 1,)),
        ],
    )(pack)


def _adamw_math(w, g, m, v):
    m = ADAM_B1 * m + (1.0 - ADAM_B1) * g
    v = ADAM_B2 * v + (1.0 - ADAM_B2) * (g * g)
    m_hat = m / (1.0 - ADAM_B1 ** ADAM_STEP)
    v_hat = v / (1.0 - ADAM_B2 ** ADAM_STEP)
    delta = -ADAM_LR * (m_hat / (jnp.sqrt(v_hat) + ADAM_EPS) + ADAM_WD * w)
    return delta, m, v


def _sum_adamw(parts, w, m, v, *, name, tr=256):
    r, c = w.shape
    tr = min(tr, r)
    tc = min(c, 1024)

    def body(p_ref, w_ref, m_ref, v_ref, g_ref, d_ref, m2_ref, v2_ref):
        g = p_ref[0].astype(F32)
        for i in range(1, N_DEV):
            g = g + p_ref[i].astype(F32)
        delta, m2, v2 = _adamw_math(w_ref[...], g, m_ref[...], v_ref[...])
        g_ref[...] = g
        d_ref[...] = delta
        m2_ref[...] = m2
        v2_ref[...] = v2

    blk = pl.BlockSpec((tr, tc), lambda i, j: (i, j))
    return pl.pallas_call(
        body,
        name=name,
        grid=(r // tr, c // tc),
        in_specs=[pl.BlockSpec((N_DEV, tr, tc), lambda i, j: (0, i, j)), blk, blk, blk],
        out_specs=[blk] * 4,
        out_shape=[jax.ShapeDtypeStruct((r, c), F32)] * 4,
        compiler_params=_params(("parallel", "parallel")),
    )(parts, w, m, v)


def _adamw_small(g, w, m, v):
    def body(g_ref, w_ref, m_ref, v_ref, d_ref, m2_ref, v2_ref):
        delta, m2, v2 = _adamw_math(w_ref[...], g_ref[...], m_ref[...], v_ref[...])
        d_ref[...] = delta
        m2_ref[...] = m2
        v2_ref[...] = v2

    vm = pl.BlockSpec(memory_space=pltpu.VMEM)
    return pl.pallas_call(
        body,
        name="adamw_small",
        in_specs=[vm] * 4,
        out_specs=[vm] * 3,
        out_shape=[jax.ShapeDtypeStruct(g.shape, F32)] * 3,
    )(g, w, m, v)


def _relu2_epilogue(acc):
    ra = jnp.maximum(acc, 0.0)
    return ra * ra, ra


def _rows(stacked):
    return stacked.reshape(stacked.shape[0] * stacked.shape[1], stacked.shape[2])


def _by_chip(mat):
    return mat.reshape(N_CHIPS, mat.shape[0] // N_CHIPS, mat.shape[1])


def _local_step(x, p, target, shards, small, update):
    s, d = x.shape
    cos_t, sin_t = _rope_tables(s)
    bucket = _band_buckets()
    p_bf = p.astype(BF16)
    wts = {}

    chip = 2 * lax.axis_index("x") + lax.axis_index("y")
    core = lax.axis_index("c")

    def gather(tag, names, after):
        srcs = [shards[n] for n in names]
        route = _gather_route([a.shape for a in srcs])
        lands = [_own_slot((N_CHIPS,) + a.shape, a.dtype, chip, a) for a in srcs]
        sems, srcs, lands, token = _split_start(f"gather_start_{tag}", srcs, lands, route, after)
        return lambda done: _swap_comm(*_split_wait(f"gather_wait_{tag}", srcs, lands, sems, route, done)), token

    in_landed, token = gather("in", ["w_in"], shards["w_in"])
    g_attn = small["attn_norm_g"] + token[:1, :1]
    u = _rms_fwd(x, g_attn, name="norm_attn")
    (wts["w_in"],) = _comm_only("swap_w_in", in_landed(u))
    mid_landed, token = gather("mid", ["w_out"], wts["w_in"])
    proj = _matmul(u, wts["w_in"], mode="nn", out_dtypes=[F32], name="mm_in", bn=768, after=token)
    pb, (w_out_s,) = _qk_prep(proj, small["q_norm_g"], small["k_norm_g"], cos_t, sin_t, comm=mid_landed(proj))
    wts["w_out"] = _rows(w_out_s)
    up_landed, token = gather("up", ["w_up"], pb)
    att_a, lse_a = _attn_a_fwd(pb, after=token)
    pad = ((PAD_LO, PAD_HI), (0, 0))
    kpad = jnp.pad(pb[:, COL_KB * HEAD_DIM:COL_VB * HEAD_DIM], pad)
    vpad = jnp.pad(pb[:, COL_VB * HEAD_DIM:], pad)
    (att, lse_b), (wts["w_up"],) = _attn_b_fwd(pb, kpad, vpad, bucket, small["rel_bias_table"],
                                               small["sink_logits"], att_a, comm=up_landed(att_a))
    down_landed, token = gather("down", ["w_down"], att)
    h1 = _matmul(att, wts["w_out"], mode="nn", out_dtypes=[F32], name="mm_out",
                 epilogue=lambda acc, res: (acc + res,), extras=(x,), after=token)
    mn = _rms_fwd(h1, small["mlp_norm_g"], name="norm_mlp")
    r, ra = _matmul(mn, wts["w_up"], mode="nn", out_dtypes=[BF16, BF16], name="mm_up", epilogue=_relu2_epilogue,
                    bm=2048)
    (w_down_s,) = _comm_only("swap_w_down", down_landed(r))
    wts["w_down"] = _rows(w_down_s)
    late_landed, token = gather("late", ["w_gate", "ple_w"], w_down_s)
    h2 = _matmul(r, wts["w_down"], mode="nn", out_dtypes=[F32], name="mm_down",
                 epilogue=lambda acc, res: (acc + res,), extras=(h1,), after=token)
    ng, (w_gate_s, wts["ple_w"]) = _rms_fwd(h2, small["gate_norm_g"], name="norm_gate", comm=late_landed(h2))
    wts["w_gate"] = _rows(w_gate_s)
    gate = _matmul(ng, wts["w_gate"], mode="nn", out_dtypes=[F32], name="mm_gate",
                   epilogue=lambda acc: (1.0 / (1.0 + jnp.exp(-acc)),))
    pp = _matmul(p_bf, wts["ple_w"], mode="nn", out_dtypes=[F32], name="mm_ple", bn=512)
    dh3, dz, dpp, dg_final, dg_ple, loss = _tail(h2, gate, pp, target, small["ple_norm_g"], small["final_norm_g"])

    dng = _matmul(dz, wts["w_gate"], mode="nt", out_dtypes=[F32], name="mm_gate_dx")
    gw_gate = _matmul(ng, dz, mode="tn", out_dtypes=[BF16], name="mm_gate_dw")
    gw_ple = _matmul(p_bf, dpp, mode="tn", out_dtypes=[BF16], name="mm_ple_dw", bn=512, out_stack=N_CHIPS)
    dh2, dh2_bf, dg_gate = _rms_bwd(h2, dng, small["gate_norm_g"], dh3, name="norm_gate_bwd", want_bf16=True)

    def exchange(tag, partials, after):
        route = _exchange_route(len(partials))
        lands = [_own_slot((N_DEV,) + g.shape[1:], g.dtype, 2 * chip + core,
                           lax.dynamic_index_in_dim(g, chip, 0, keepdims=False)) for g in partials]
        sems, srcs, lands, token = _split_start(f"exchange_start_{tag}", partials, lands, route, after)
        return lambda done: _forward_comm(*_split_wait(f"exchange_wait_{tag}", srcs, lands, sems, route, done)), token

    big = {}
    gate_landed, token = exchange("gate", [_by_chip(gw_gate), gw_ple], dh2_bf)
    gw_down = _matmul(r, dh2_bf, mode="tn", out_dtypes=[BF16], name="mm_down_dw", after=token)
    da, (parts_gate, parts_ple) = _matmul(
        dh2_bf, wts["w_down"], mode="nt", out_dtypes=[BF16], name="mm_down_dx", bm=2048,
        epilogue=lambda acc, ra_v: (acc * (2.0 * ra_v.astype(F32)),), extras=(ra,), comm=gate_landed(gw_down))
    down_landed, token = exchange("down", [_by_chip(gw_down)], da)
    big["w_gate"], big["ple_w"] = update("w_gate", parts_gate), update("ple_w", parts_ple)
    gw_up = _matmul(mn, da, mode="tn", out_dtypes=[BF16], name="mm_up_dw", out_stack=N_CHIPS, after=token)
    dmn = _matmul(da, wts["w_up"], mode="nt", out_dtypes=[F32], name="mm_up_dx")
    dh1, dh1_bf, dg_mlp = _rms_bwd(h1, dmn, small["mlp_norm_g"], dh2, name="norm_mlp_bwd", want_bf16=True)
    datt, (parts_down,) = _matmul(dh1_bf, wts["w_out"], mode="nt", out_dtypes=[BF16], name="mm_out_dx",
                                  comm=down_landed(dh1_bf))
    gw_out = _matmul(att, dh1_bf, mode="tn", out_dtypes=[BF16], name="mm_out_dw")
    up_landed, token = exchange("up", [gw_up, _by_chip(gw_out)], datt)
    dqb, dkpad, dvpad, dtab, dsink = _attn_b_bwd(pb, kpad, vpad, att, datt, lse_b, bucket,
                                                 small["rel_bias_table"], small["sink_logits"], after=token)
    dqa, dka, dva = _attn_a_bwd(pb, att, datt, lse_a)
    (dproj, dg_q, dg_k), (parts_up, parts_out) = _qk_bwd(dqa, dka, dva, dqb, dkpad, dvpad, proj,
                                                         small["q_norm_g"], small["k_norm_g"], cos_t, sin_t,
                                                         comm=up_landed(dqa))
    gw_in = _matmul(u, dproj, mode="tn", out_dtypes=[BF16], name="mm_in_dw", bn=768, out_stack=N_CHIPS)
    in_landed, token = exchange("in", [gw_in], gw_in)
    du = _matmul(dproj, wts["w_in"], mode="nt", out_dtypes=[F32], name="mm_in_dx", bk=768, after=token)
    grad_x, dg_attn = _rms_bwd(x, du, small["attn_norm_g"], dh1, name="norm_attn_bwd", want_bf16=False)
    for n, parts in (("w_down", parts_down), ("w_up", parts_up), ("w_out", parts_out)):
        big[n] = update(n, parts)
    done = dg_attn + sum(big[n][0][0, :1, :] for n in ("w_down", "w_up", "w_out"))
    (parts_in,) = _comm_only("forward_w_in", in_landed(done))
    big["w_in"] = update("w_in", parts_in)

    small_g = {
        "attn_norm_g": dg_attn, "mlp_norm_g": dg_mlp, "ple_norm_g": dg_ple, "gate_norm_g": dg_gate,
        "final_norm_g": dg_final, "q_norm_g": dg_q, "k_norm_g": dg_k,
        "sink_logits": dsink[:, 0, 0][None, :], "rel_bias_table": dtab[:, :, 0].T,
    }
    return loss, grad_x, big, small_g


_SMALL_ROWS = ["attn_norm_g", "mlp_norm_g", "ple_norm_g", "gate_norm_g", "final_norm_g"]
_PACK_ROWS = 8


def _pack_small(vals, d):
    rows = [vals[n].reshape(1, d) for n in _SMALL_ROWS]
    misc = jnp.concatenate([
        vals["q_norm_g"].reshape(1, HEAD_DIM), vals["k_norm_g"].reshape(1, HEAD_DIM),
        jnp.pad(vals["sink_logits"].reshape(1, N_HEADS_B), ((0, 0), (0, HEAD_DIM - N_HEADS_B))),
        vals["rel_bias_table"].reshape(1, N_BUCKETS * N_HEADS_B)], axis=1)
    rows.append(jnp.pad(misc, ((0, 0), (0, d - misc.shape[1]))))
    rows.append(jnp.zeros((_PACK_ROWS - len(rows), d), F32))
    return jnp.concatenate(rows, axis=0).astype(F32)


def _unpack_small(pack, shapes):
    out = {n: pack[i].reshape(shapes[n]) for i, n in enumerate(_SMALL_ROWS)}
    misc = pack[len(_SMALL_ROWS)]
    out["q_norm_g"] = misc[:HEAD_DIM].reshape(shapes["q_norm_g"])
    out["k_norm_g"] = misc[HEAD_DIM:2 * HEAD_DIM].reshape(shapes["k_norm_g"])
    out["sink_logits"] = misc[2 * HEAD_DIM:2 * HEAD_DIM + N_HEADS_B].reshape(shapes["sink_logits"])
    out["rel_bias_table"] = misc[3 * HEAD_DIM:3 * HEAD_DIM + N_BUCKETS * N_HEADS_B].reshape(shapes["rel_bias_table"])
    return out


_WEIGHTS = ["attn_norm_g", "w_in", "q_norm_g", "k_norm_g", "sink_logits", "w_out", "mlp_norm_g", "w_up", "w_down",
            "ple_w", "ple_norm_g", "gate_norm_g", "w_gate", "rel_bias_table", "final_norm_g"]
_BIG = ["w_in", "w_out", "w_up", "w_down", "ple_w", "w_gate"]


def kernel(x, p, attn_norm_g, w_in, q_norm_g, k_norm_g, sink_logits, w_out, mlp_norm_g, w_up, w_down, ple_w, ple_norm_g, gate_norm_g, w_gate, rel_bias_table, final_norm_g, loss_target, m_attn_norm_g, m_w_in, m_q_norm_g, m_k_norm_g, m_sink_logits, m_w_out, m_mlp_norm_g, m_w_up, m_w_down, m_ple_w, m_ple_norm_g, m_gate_norm_g, m_w_gate, m_rel_bias_table, m_final_norm_g, v_attn_norm_g, v_w_in, v_q_norm_g, v_k_norm_g, v_sink_logits, v_w_out, v_mlp_norm_g, v_w_up, v_w_down, v_ple_w, v_ple_norm_g, v_gate_norm_g, v_w_gate, v_rel_bias_table, v_final_norm_g):
    given = dict(locals())
    w = {n: given[n] for n in _WEIGHTS}
    m = {n: given["m_" + n] for n in _WEIGHTS}
    v = {n: given["v_" + n] for n in _WEIGHTS}
    d = x.shape[-1]

    shards = {n: w[n][0].astype(BF16) for n in _BIG}
    small = {
        "attn_norm_g": w["attn_norm_g"], "mlp_norm_g": w["mlp_norm_g"], "ple_norm_g": w["ple_norm_g"],
        "gate_norm_g": w["gate_norm_g"], "final_norm_g": w["final_norm_g"].reshape(1, d),
        "q_norm_g": w["q_norm_g"], "k_norm_g": w["k_norm_g"], "sink_logits": w["sink_logits"],
        "rel_bias_table": w["rel_bias_table"],
    }

    def update(n, parts):
        res = _sum_adamw(parts, w[n][0], m[n][0], v[n][0], name="adamw_" + n)
        return [t.reshape(w[n].shape) for t in res]

    loss_part, grad_x, big, small_g = _local_step(x[0], p[0, 0], loss_target[0], shards, small, update)
    loss = lax.psum(loss_part[0, 0], ("x", "y", "c"))
    grads, deltas, new_m, new_v = [{n: big[n][i] for n in _BIG} for i in range(4)]

    shapes = {n: w[n].shape for n in _WEIGHTS if n not in _BIG}
    pack = _pack_small(small_g, d)
    pack = pack.at[_PACK_ROWS - 1, :1].add(0.0 * grads["w_in"][0, 0, :1])
    g_small = _allreduce_small(pack)
    d_small, m_small, v_small = _adamw_small(g_small, _pack_small(w, d), _pack_small(m, d), _pack_small(v, d))
    grads.update(_unpack_small(g_small, shapes))
    deltas.update(_unpack_small(d_small, shapes))
    new_m.update(_unpack_small(m_small, shapes))
    new_v.update(_unpack_small(v_small, shapes))

    return (loss, grad_x[None], *[grads[n] for n in _WEIGHTS], *[deltas[n] for n in _WEIGHTS],
            *[new_m[n] for n in _WEIGHTS], *[new_v[n] for n in _WEIGHTS])
```

```python
import functools
import math

import jax
import jax.numpy as jnp
import numpy as np
from jax import lax
from jax.experimental import pallas as pl
from jax.experimental.pallas import tpu as pltpu

F32 = jnp.float32
BF16 = jnp.bfloat16

HEAD_DIM = 128
N_HEADS_A = 8
N_KV_A = 2
N_HEADS_B = 8
N_KV_B = 2
GROUP = 4
GRID_W = 64
BLOCK_Q = 128
WINDOW = 128
N_BUCKETS = 32
MAX_DISTANCE = 128
ROPE_THETA = 10000.0
EPS = 1e-6
NEG_INF = -1e30
ATT_SCALE = HEAD_DIM ** -0.5
LOG2E = math.log2(math.e)
LN2 = math.log(2.0)
Q_SCALE = ATT_SCALE * LOG2E
PAD_LO, PAD_HI = 256, 128

ADAM_LR = 0.001
ADAM_B1 = 0.9
ADAM_B2 = 0.999
ADAM_EPS = 1e-08
ADAM_WD = 0.01
ADAM_STEP = 10

N_CHIPS = 4
N_DEV = 8
COL_QA, COL_KA, COL_VA, COL_QB, COL_KB, COL_VB = 0, 8, 10, 12, 20, 22
N_COLS = 24

VMEM_LIMIT = 52 * 1024 * 1024


def _params(sem=None, collective_id=None):
    return pltpu.CompilerParams(dimension_semantics=sem, vmem_limit_bytes=VMEM_LIMIT, collective_id=collective_id)


_ANY = pl.BlockSpec(memory_space=pl.ANY)
_MESH = pl.DeviceIdType.MESH
SIBLING_BARRIER_ID = 1


def _sibling():
    return (lax.axis_index("x"), lax.axis_index("y"), 1 - lax.axis_index("c"))


class _Comm:
    def __init__(self, inputs, out_shapes, sems, start, finish, aliases=None):
        self.inputs, self.out_shapes, self.sems = list(inputs), list(out_shapes), list(sems)
        self.start, self.finish, self.aliases = start, finish, dict(aliases or {})


def _call(body, *, name, grid, in_specs, out_specs, out_shape, args, scratch_shapes=(), sem=None, comm=None,
          after=None, aliases=None):
    in_specs, out_specs, out_shape = list(in_specs), list(out_specs), list(out_shape)
    scratch_shapes = list(scratch_shapes)
    n_in, n_out, n_sc = len(in_specs), len(out_specs), len(scratch_shapes)
    behind = [] if after is None else [after]
    aliases = dict(aliases or {})
    if comm is None:
        res = pl.pallas_call(
            (lambda *refs: body(*refs[:n_in], *refs[n_in + len(behind):])) if behind else body,
            name=name, grid=grid, in_specs=in_specs + [_ANY] * len(behind), out_specs=out_specs,
            out_shape=out_shape, scratch_shapes=scratch_shapes, input_output_aliases=aliases,
            compiler_params=_params(sem))(*args, *behind)
        return list(res), []
    assert not behind
    c_in, c_out = len(comm.inputs), len(comm.out_shapes)

    def hosted(*refs):
        pos = [0]

        def take(n):
            pos[0] += n
            return refs[pos[0] - n:pos[0]]

        ins, c_ins, outs, c_outs, scr = take(n_in), take(c_in), take(n_out), take(c_out), take(n_sc)
        c_sems = refs[pos[0]:]
        ids = [pl.program_id(a) for a in range(len(grid))]
        first = functools.reduce(jnp.logical_and, [i == 0 for i in ids])
        last = functools.reduce(jnp.logical_and, [i == g - 1 for i, g in zip(ids, grid)])

        @pl.when(first)
        def _():
            barrier = pltpu.get_barrier_semaphore()
            pl.semaphore_signal(barrier, inc=1, device_id=_sibling(), device_id_type=_MESH)
            pl.semaphore_wait(barrier, 1)
            comm.start(c_ins, c_outs, c_sems)

        body(*ins, *outs, *scr)

        @pl.when(last)
        def _():
            comm.finish(c_ins, c_outs, c_sems)

    res = pl.pallas_call(
        hosted, name=name, grid=grid, in_specs=in_specs + [_ANY] * c_in, out_specs=out_specs + [_ANY] * c_out,
        out_shape=out_shape + comm.out_shapes, scratch_shapes=scratch_shapes + comm.sems,
        input_output_aliases={**aliases, **{n_in + i: n_out + o for i, o in comm.aliases.items()}},
        compiler_params=_params(("arbitrary",) * len(grid), SIBLING_BARRIER_ID))(*args, *comm.inputs)
    return list(res[:n_out]), list(res[n_out:])


def _matmul(a, b, *, mode, out_dtypes, name, epilogue=None, extras=(), bm=1024, bn=1024, bk=2048,
            out_stack=0, comm=None, after=None):
    stacked = b.ndim == 3
    if mode == "nn":
        m, k = a.shape
        if stacked:
            nj, kb, ns = b.shape
            n, ks = nj * ns, k
        else:
            kb, n = b.shape
            ns, ks = n, k
        dn = (((1,), (0,)), ((), ()))
    elif mode == "nt":
        m, k = a.shape
        if stacked:
            nj, n, ks = b.shape
            kb = nj * ks
        else:
            n, kb = b.shape
            ks = kb
        ns = n
        dn = (((1,), (1,)), ((), ()))
    else:
        k, m = a.shape
        kb, n = b.shape
        ns, ks = n, k
        dn = (((0,), (0,)), ((), ()))
    assert k == kb and not (stacked and mode == "tn")
    ns_out = n // out_stack if out_stack else n
    bm, bn, bk = min(bm, m), min(bn, ns, ns_out), min(bk, ks)
    assert m % bm == 0 and ns % bn == 0 and ns_out % bn == 0 and ks % bk == 0
    gm, gn, gk = m // bm, n // bn, k // bk

    if mode == "tn":
        a_spec = pl.BlockSpec((bk, bm), lambda i, j, q: (q, i))
    else:
        a_spec = pl.BlockSpec((bm, bk), lambda i, j, q: (i, q))
    if mode == "nt":
        if stacked:
            per = ks // bk
            b_spec = pl.BlockSpec((None, bn, bk), lambda i, j, q: (q // per, j, q % per))
        else:
            b_spec = pl.BlockSpec((bn, bk), lambda i, j, q: (j, q))
    else:
        if stacked:
            per = ns // bn
            b_spec = pl.BlockSpec((None, bk, bn), lambda i, j, q: (j // per, q, j % per))
        else:
            b_spec = pl.BlockSpec((bk, bn), lambda i, j, q: (q, j))
    ex_spec = pl.BlockSpec((bm, bn), lambda i, j, q: (i, j))
    if out_stack:
        per_o = ns_out // bn
        o_spec = pl.BlockSpec((None, bm, bn), lambda i, j, q: (j // per_o, i, j % per_o))
        o_shape = (out_stack, m, ns_out)
    else:
        o_spec = ex_spec
        o_shape = (m, n)
    n_ex, n_out = len(extras), len(out_dtypes)

    def body(a_ref, b_ref, *rest):
        ex, outs = rest[:n_ex], rest[n_ex:n_ex + n_out]
        part = lax.dot_general(a_ref[...], b_ref[...], dn, preferred_element_type=F32)

        def finish(acc):
            res = epilogue(acc, *[e[...] for e in ex]) if epilogue else (acc,)
            for o, r in zip(outs, res):
                o[...] = r.astype(o.dtype)

        if gk == 1:
            finish(part)
        else:
            acc_ref = rest[-1]
            q = pl.program_id(2)

            @pl.when(q == 0)
            def _():
                acc_ref[...] = part

            @pl.when(q > 0)
            def _():
                acc_ref[...] += part

            @pl.when(q == gk - 1)
            def _():
                finish(acc_ref[...])

    res, c_res = _call(
        body, name=name, grid=(gm, gn, gk),
        in_specs=[a_spec, b_spec] + [ex_spec] * n_ex,
        out_specs=[o_spec] * n_out,
        out_shape=[jax.ShapeDtypeStruct(o_shape, dt) for dt in out_dtypes],
        scratch_shapes=[pltpu.VMEM((bm, bn), F32)] if gk > 1 else [],
        sem=("parallel", "parallel", "arbitrary"), args=(a, b, *extras), comm=comm, after=after)
    res = res[0] if n_out == 1 else res
    return res if comm is None else (res, c_res)


def _rms_fwd(x, g, *, name, tm=256, comm=None):
    s, d = x.shape
    tm = min(tm, s)

    def body(x_ref, g_ref, o_ref):
        xf = x_ref[...]
        r = lax.rsqrt(jnp.mean(xf * xf, axis=-1, keepdims=True) + EPS)
        o_ref[...] = (xf * r * g_ref[...]).astype(o_ref.dtype)

    res, c_res = _call(
        body, name=name, grid=(s // tm,),
        in_specs=[pl.BlockSpec((tm, d), lambda i: (i, 0)), pl.BlockSpec((1, d), lambda i: (0, 0))],
        out_specs=[pl.BlockSpec((tm, d), lambda i: (i, 0))],
        out_shape=[jax.ShapeDtypeStruct((s, d), BF16)],
        sem=("parallel",), args=(x, g), comm=comm)
    return res[0] if comm is None else (res[0], c_res)


def _rms_bwd(x, dy, g, add, *, name, want_bf16, tm=256):
    s, d = x.shape
    tm = min(tm, s)

    def body(x_ref, dy_ref, g_ref, add_ref, dx_ref, *rest):
        dg_ref = rest[-1]
        i = pl.program_id(0)
        xf = x_ref[...]
        dyf = dy_ref[...].astype(F32)
        r = lax.rsqrt(jnp.mean(xf * xf, axis=-1, keepdims=True) + EPS)
        xh = xf * r
        dyg = dyf * g_ref[...]
        dx = r * (dyg - xh * jnp.mean(dyg * xh, axis=-1, keepdims=True))
        tot = add_ref[...] + dx
        dx_ref[...] = tot
        if want_bf16:
            rest[0][...] = tot.astype(BF16)
        part = jnp.sum(dyf * xh, axis=0, keepdims=True)

        @pl.when(i == 0)
        def _():
            dg_ref[...] = part

        @pl.when(i > 0)
        def _():
            dg_ref[...] += part

    row = pl.BlockSpec((tm, d), lambda i: (i, 0))
    vec = pl.BlockSpec((1, d), lambda i: (0, 0))
    out_specs = [row] + ([row] if want_bf16 else []) + [vec]
    out_shape = [jax.ShapeDtypeStruct((s, d), F32)]
    if want_bf16:
        out_shape.append(jax.ShapeDtypeStruct((s, d), BF16))
    out_shape.append(jax.ShapeDtypeStruct((1, d), F32))
    return pl.pallas_call(
        body,
        name=name,
        grid=(s // tm,),
        in_specs=[row, row, vec, row],
        out_specs=out_specs,
        out_shape=out_shape,
        compiler_params=_params(("arbitrary",)),
    )(x, dy, g, add)


def _tail(h2, gate, pp, target, g_ple, g_final, *, tm=128):
    s, d = h2.shape
    tm = min(tm, s)

    def body(h2_ref, gate_ref, pp_ref, t_ref, gp_ref, gf_ref, dh3_ref, dz_ref, dpp_ref, dgf_ref, dgp_ref, loss_ref):
        i = pl.program_id(0)
        ppf = pp_ref[...]
        gate_v = gate_ref[...]
        r_p = lax.rsqrt(jnp.mean(ppf * ppf, axis=-1, keepdims=True) + EPS)
        eh = ppf * r_p
        e = eh * gp_ref[...]
        h3 = h2_ref[...] + gate_v * e
        r_f = lax.rsqrt(jnp.mean(h3 * h3, axis=-1, keepdims=True) + EPS)
        yh = h3 * r_f
        diff = yh * gf_ref[...] - t_ref[...]
        loss_part = 0.5 * jnp.sum(jnp.mean(diff * diff, axis=-1, keepdims=True), axis=0, keepdims=True)
        dy = diff / d
        dgf = jnp.sum(dy * yh, axis=0, keepdims=True)
        dyg = dy * gf_ref[...]
        dh3 = r_f * (dyg - yh * jnp.mean(dyg * yh, axis=-1, keepdims=True))
        dh3_ref[...] = dh3
        de = dh3 * gate_v
        dz_ref[...] = (dh3 * e * gate_v * (1.0 - gate_v)).astype(BF16)
        dgp = jnp.sum(de * eh, axis=0, keepdims=True)
        deg = de * gp_ref[...]
        dpp_ref[...] = (r_p * (deg - eh * jnp.mean(deg * eh, axis=-1, keepdims=True))).astype(BF16)
        loss_row = jnp.broadcast_to(loss_part, (1, 128))

        @pl.when(i == 0)
        def _():
            dgf_ref[...] = dgf
            dgp_ref[...] = dgp
            loss_ref[...] = loss_row

        @pl.when(i > 0)
        def _():
            dgf_ref[...] += dgf
            dgp_ref[...] += dgp
            loss_ref[...] += loss_row

    row = pl.BlockSpec((tm, d), lambda i: (i, 0))
    vec = pl.BlockSpec((1, d), lambda i: (0, 0))
    return pl.pallas_call(
        body,
        name="tail_fwd_bwd",
        grid=(s // tm,),
        in_specs=[row, row, row, row, vec, vec],
        out_specs=[row, row, row, vec, vec, pl.BlockSpec((1, 128), lambda i: (0, 0))],
        out_shape=[
            jax.ShapeDtypeStruct((s, d), F32),
            jax.ShapeDtypeStruct((s, d), BF16),
            jax.ShapeDtypeStruct((s, d), BF16),
            jax.ShapeDtypeStruct((1, d), F32),
            jax.ShapeDtypeStruct((1, d), F32),
            jax.ShapeDtypeStruct((1, 128), F32),
        ],
        compiler_params=_params(("arbitrary",)),
    )(h2, gate, pp, target, g_ple, g_final)


def _rope_tables(s):
    rows = s // GRID_W
    half = HEAD_DIM // 2
    inv_freq = ROPE_THETA ** (-jnp.arange(0, half, 2, dtype=F32) / half)
    ang_r = jnp.arange(rows, dtype=jnp.int32).astype(F32)[:, None] * inv_freq
    ang_c = jnp.arange(GRID_W, dtype=jnp.int32).astype(F32)[:, None] * inv_freq
    cr, sr = (jnp.repeat(t, GRID_W, axis=0) for t in (jnp.cos(ang_r), jnp.sin(ang_r)))
    cc, sc = (jnp.tile(t, (rows, 1)) for t in (jnp.cos(ang_c), jnp.sin(ang_c)))
    cos_t = jnp.concatenate([cr, cr, cc, cc], axis=-1)
    sin_t = jnp.concatenate([-sr, sr, -sc, sc], axis=-1)
    return cos_t, sin_t


def _swap_quarters(x):
    lane = lax.broadcasted_iota(jnp.int32, x.shape, x.ndim - 1)
    up = pltpu.roll(x, HEAD_DIM - 32, x.ndim - 1)
    down = pltpu.roll(x, 32, x.ndim - 1)
    return jnp.where((lane % 64) < 32, up, down)


def _cols(first, count=1):
    return slice(first * HEAD_DIM, (first + count) * HEAD_DIM)


def _qk_prep(proj, g_q, g_k, cos_t, sin_t, *, tm=256, comm=None):
    s, n = proj.shape
    tm = min(tm, s)

    def body(x_ref, gq_ref, gk_ref, c_ref, s_ref, o_ref):
        cos_v, sin_v = c_ref[...], s_ref[...]
        for h in range(COL_VA):
            x = x_ref[:, _cols(h)]
            g = gq_ref[...] if h < COL_KA else gk_ref[...]
            xn = x * lax.rsqrt(jnp.mean(x * x, axis=-1, keepdims=True) + EPS) * g
            xr = xn * cos_v + _swap_quarters(xn) * sin_v
            if h < COL_KA:
                xr = xr * Q_SCALE
            o_ref[:, _cols(h)] = xr.astype(BF16)
        o_ref[:, _cols(COL_VA, 2)] = x_ref[:, _cols(COL_VA, 2)].astype(BF16)
        o_ref[:, _cols(COL_QB, N_HEADS_B)] = (x_ref[:, _cols(COL_QB, N_HEADS_B)] * Q_SCALE).astype(BF16)
        o_ref[:, _cols(COL_KB, 4)] = x_ref[:, _cols(COL_KB, 4)].astype(BF16)

    row = pl.BlockSpec((tm, n), lambda i: (i, 0))
    tab = pl.BlockSpec((tm, HEAD_DIM), lambda i: (i, 0))
    vec = pl.BlockSpec((1, HEAD_DIM), lambda i: (0, 0))
    res, c_res = _call(
        body, name="qk_prep", grid=(s // tm,),
        in_specs=[row, vec, vec, tab, tab],
        out_specs=[row],
        out_shape=[jax.ShapeDtypeStruct((s, n), BF16)],
        sem=("parallel",), args=(proj, g_q, g_k, cos_t, sin_t), comm=comm)
    return res[0] if comm is None else (res[0], c_res)


def _qk_bwd(dqa, dka, dva, dqb, dkpad, dvpad, proj, g_q, g_k, cos_t, sin_t, *, comm=None):
    s, n = proj.shape
    tm = min(PAD_LO, s)
    assert PAD_LO % tm == 0
    lo = PAD_LO // tm

    def body(dqa_ref, dka_ref, dva_ref, dqb_ref, dkb_ref, dvb_ref, x_ref, gq_ref, gk_ref, c_ref, s_ref,
             o_ref, dgq_ref, dgk_ref):
        i = pl.program_id(0)
        cos_v, sin_v = c_ref[...], s_ref[...]

        def head(d, x, g):
            dn = d * cos_v + _swap_quarters(d * sin_v)
            r = lax.rsqrt(jnp.mean(x * x, axis=-1, keepdims=True) + EPS)
            xh = x * r
            dng = dn * g
            dx = r * (dng - xh * jnp.mean(dng * xh, axis=-1, keepdims=True))
            return dx.astype(BF16), jnp.sum(dn * xh, axis=0, keepdims=True)

        acc_q = jnp.zeros((1, HEAD_DIM), F32)
        acc_k = jnp.zeros((1, HEAD_DIM), F32)
        for h in range(N_HEADS_A):
            o_ref[:, _cols(h)], part = head(dqa_ref[:, _cols(h)] * ATT_SCALE, x_ref[:, _cols(h)], gq_ref[...])
            acc_q = acc_q + part
        for h in range(N_KV_A):
            o_ref[:, _cols(COL_KA + h)], part = head(dka_ref[:, _cols(h)] * LN2, x_ref[:, _cols(COL_KA + h)],
                                                     gk_ref[...])
            acc_k = acc_k + part
        o_ref[:, _cols(COL_VA, 2)] = dva_ref[...].astype(BF16)
        o_ref[:, _cols(COL_QB, N_HEADS_B)] = (dqb_ref[...] * ATT_SCALE).astype(BF16)
        o_ref[:, _cols(COL_KB, 2)] = (dkb_ref[...] * LN2).astype(BF16)
        o_ref[:, _cols(COL_VB, 2)] = dvb_ref[...].astype(BF16)

        @pl.when(i == 0)
        def _():
            dgq_ref[...] = acc_q
            dgk_ref[...] = acc_k

        @pl.when(i > 0)
        def _():
            dgq_ref[...] += acc_q
            dgk_ref[...] += acc_k

    def rows(width, shift=0):
        return pl.BlockSpec((tm, width), lambda i: (i + shift, 0))

    kv_w = N_KV_A * HEAD_DIM
    q_w = N_HEADS_A * HEAD_DIM
    vec = pl.BlockSpec((1, HEAD_DIM), lambda i: (0, 0))
    res, c_res = _call(
        body, name="qk_bwd", grid=(s // tm,),
        in_specs=[rows(q_w), rows(kv_w), rows(kv_w), rows(q_w), rows(kv_w, lo), rows(kv_w, lo), rows(n),
                  vec, vec, rows(HEAD_DIM), rows(HEAD_DIM)],
        out_specs=[rows(n), vec, vec],
        out_shape=[
            jax.ShapeDtypeStruct((s, n), BF16),
            jax.ShapeDtypeStruct((1, HEAD_DIM), F32),
            jax.ShapeDtypeStruct((1, HEAD_DIM), F32),
        ],
        sem=("arbitrary",), args=(dqa, dka, dva, dqb, dkpad, dvpad, proj, g_q, g_k, cos_t, sin_t), comm=comm)
    return res if comm is None else (res, c_res)


_NT = (((1,), (1,)), ((), ()))
_TN = (((0,), (0,)), ((), ()))


def _attn_a_fwd(pb, *, tq=2048, sub=256, comm=None, after=None):
    s = pb.shape[0]
    tq = min(tq, s)

    sub = min(sub, tq)

    def body(q_ref, k_ref, v_ref, o_ref, lse_ref):
        k = k_ref[...]
        v = v_ref[...]
        for r in range(tq // sub):
            rows = pl.ds(r * sub, sub)
            sc = lax.dot_general(q_ref[rows, :], k, _NT, preferred_element_type=F32)
            m = jnp.max(sc, axis=-1, keepdims=True)
            p = jnp.exp2(sc - m)
            l = jnp.sum(p, axis=-1, keepdims=True)
            o = jnp.dot(p.astype(BF16), v, preferred_element_type=F32)
            o_ref[rows, :] = (o / l).astype(BF16)
            lse_ref[rows, :] = jnp.broadcast_to(m + jnp.log2(l), (sub, HEAD_DIM))

    res, c_res = _call(
        body, name="attn_a_fwd", grid=(N_HEADS_A, s // tq),
        in_specs=[
            pl.BlockSpec((tq, HEAD_DIM), lambda h, i: (i, COL_QA + h)),
            pl.BlockSpec((s, HEAD_DIM), lambda h, i: (0, COL_KA + h // GROUP)),
            pl.BlockSpec((s, HEAD_DIM), lambda h, i: (0, COL_VA + h // GROUP)),
        ],
        out_specs=[
            pl.BlockSpec((tq, HEAD_DIM), lambda h, i: (i, h)),
            pl.BlockSpec((None, tq, HEAD_DIM), lambda h, i: (h, i, 0)),
        ],
        out_shape=[
            jax.ShapeDtypeStruct((s, (N_HEADS_A + N_HEADS_B) * HEAD_DIM), BF16),
            jax.ShapeDtypeStruct((N_HEADS_A, s, HEAD_DIM), F32),
        ],
        sem=("parallel", "parallel"), args=(pb, pb, pb), comm=comm, after=after)
    return res if comm is None else (res, c_res)


def _attn_a_bwd(pb, att, datt, lse, *, tq=2048, sub=256, comm=None):
    s = pb.shape[0]
    tq = min(tq, s)
    sub = min(sub, tq)

    def body(q_ref, k_ref, v_ref, o_ref, do_ref, lse_ref, dq_ref, dk_ref, dv_ref):
        first = jnp.logical_and(pl.program_id(1) == 0, pl.program_id(2) == 0)
        k = k_ref[...]
        v = v_ref[...]
        dk = dv = None
        for r in range(tq // sub):
            rows = pl.ds(r * sub, sub)
            q = q_ref[rows, :]
            do = do_ref[rows, :]
            sc = lax.dot_general(q, k, _NT, preferred_element_type=F32)
            p = jnp.exp2(sc - lse_ref[rows, :][:, :1])
            dp = lax.dot_general(do, v, _NT, preferred_element_type=F32)
            delta = jnp.sum(do.astype(F32) * o_ref[rows, :].astype(F32), axis=-1, keepdims=True)
            ds = (p * (dp - delta)).astype(BF16)
            dq_ref[rows, :] = jnp.dot(ds, k, preferred_element_type=F32)
            dk_r = lax.dot_general(ds, q, _TN, preferred_element_type=F32)
            dv_r = lax.dot_general(p.astype(BF16), do, _TN, preferred_element_type=F32)
            dk = dk_r if dk is None else dk + dk_r
            dv = dv_r if dv is None else dv + dv_r

        @pl.when(first)
        def _():
            dk_ref[...] = dk
            dv_ref[...] = dv

        @pl.when(jnp.logical_not(first))
        def _():
            dk_ref[...] += dk
            dv_ref[...] += dv

    qmap = lambda kv, g, i: (i, kv * GROUP + g)
    res, c_res = _call(
        body, name="attn_a_bwd", grid=(N_KV_A, GROUP, s // tq),
        in_specs=[
            pl.BlockSpec((tq, HEAD_DIM), lambda kv, g, i: (i, COL_QA + kv * GROUP + g)),
            pl.BlockSpec((s, HEAD_DIM), lambda kv, g, i: (0, COL_KA + kv)),
            pl.BlockSpec((s, HEAD_DIM), lambda kv, g, i: (0, COL_VA + kv)),
            pl.BlockSpec((tq, HEAD_DIM), qmap),
            pl.BlockSpec((tq, HEAD_DIM), qmap),
            pl.BlockSpec((None, tq, HEAD_DIM), lambda kv, g, i: (kv * GROUP + g, i, 0)),
        ],
        out_specs=[
            pl.BlockSpec((tq, HEAD_DIM), qmap),
            pl.BlockSpec((s, HEAD_DIM), lambda kv, g, i: (0, kv)),
            pl.BlockSpec((s, HEAD_DIM), lambda kv, g, i: (0, kv)),
        ],
        out_shape=[
            jax.ShapeDtypeStruct((s, N_HEADS_A * HEAD_DIM), F32),
            jax.ShapeDtypeStruct((s, N_KV_A * HEAD_DIM), F32),
            jax.ShapeDtypeStruct((s, N_KV_A * HEAD_DIM), F32),
        ],
        sem=("arbitrary", "arbitrary", "arbitrary"), args=(pb, pb, pb, att, datt, lse), comm=comm)
    return res if comm is None else (res, c_res)


def _t5_bucket(rel):
    nb = N_BUCKETS // 2
    ret = jnp.where(rel > 0, nb, 0)
    n = jnp.abs(rel)
    max_exact = nb // 2
    nf = jnp.maximum(n, 1).astype(F32)
    large = max_exact + (jnp.log(nf / max_exact) / math.log(MAX_DISTANCE / max_exact)
                         * (nb - max_exact)).astype(jnp.int32)
    large = jnp.minimum(large, nb - 1)
    return ret + jnp.where(n < max_exact, n, large)


def _band_buckets():
    r = jnp.arange(BLOCK_Q, dtype=jnp.int32)
    j = jnp.arange(3 * BLOCK_Q, dtype=jnp.int32)
    return _t5_bucket((j[None, :] - BLOCK_Q) - r[:, None])


def _band_bias(bucket, table_ref, h):
    acc = jnp.zeros(bucket.shape, F32)
    for b in range(N_BUCKETS):
        acc = jnp.where(bucket == b, table_ref[b, h], acc)
    return acc


GQ = GROUP * BLOCK_Q


def _stack_heads(x):
    return jnp.concatenate([x[:, _cols(g)] for g in range(GROUP)], axis=0)


def _unstack_heads(x):
    return jnp.concatenate([x[g * BLOCK_Q:(g + 1) * BLOCK_Q] for g in range(GROUP)], axis=1)


def _group_bias(bucket, table_ref, kv):
    return jnp.concatenate([_band_bias(bucket, table_ref, kv * GROUP + g) * LOG2E for g in range(GROUP)], axis=0)


def _group_sink(sink_ref, kv):
    head = lax.broadcasted_iota(jnp.int32, (GQ, 1), 0) // BLOCK_Q
    snk = jnp.zeros((GQ, 1), F32)
    for g in range(GROUP):
        snk = jnp.where(head == g, sink_ref[0, kv * GROUP + g] * LOG2E, snk)
    return snk


def _band_mask(n, s):
    r = lax.broadcasted_iota(jnp.int32, (GQ, 3 * BLOCK_Q), 0) % BLOCK_Q
    j = lax.broadcasted_iota(jnp.int32, (GQ, 3 * BLOCK_Q), 1)
    rel = j - BLOCK_Q - r
    kabs = n * BLOCK_Q + j - BLOCK_Q
    return (jnp.abs(rel) <= WINDOW) & (kabs >= 0) & (kabs < s)


def _band_start(n):
    return pl.multiple_of(n * BLOCK_Q + (PAD_LO - BLOCK_Q), BLOCK_Q)


def _attn_b_fwd(pb, kpad, vpad, bucket, table, sink, att, *, comm=None):
    s = pb.shape[0]
    nblk = s // BLOCK_Q
    sp = kpad.shape[0]

    def body(table_ref, sink_ref, q0_ref, q1_ref, k_ref, v_ref, bucket_ref, _, o_ref, lse_ref, bias_ref):
        n = pl.program_id(0)

        @pl.when(n == 0)
        def _():
            for kv in range(N_KV_B):
                bias_ref[kv * GQ:(kv + 1) * GQ, :] = _group_bias(bucket_ref[...], table_ref, kv)

        band = pl.ds(_band_start(n), 3 * BLOCK_Q)
        mask = _band_mask(n, s)
        for kv, q_ref in enumerate((q0_ref, q1_ref)):
            kb = k_ref[band, _cols(kv)]
            vb = v_ref[band, _cols(kv)]
            sc = lax.dot_general(_stack_heads(q_ref[...]), kb, _NT, preferred_element_type=F32)
            sc = jnp.where(mask, sc + bias_ref[kv * GQ:(kv + 1) * GQ, :], NEG_INF)
            snk = _group_sink(sink_ref, kv)
            m = jnp.maximum(jnp.max(sc, axis=-1, keepdims=True), snk)
            p = jnp.exp2(sc - m)
            l = jnp.sum(p, axis=-1, keepdims=True) + jnp.exp2(snk - m)
            o = jnp.dot(p.astype(BF16), vb, preferred_element_type=F32)
            o_ref[:, _cols(kv * GROUP, GROUP)] = _unstack_heads((o / l).astype(BF16))
            lse = m + jnp.log2(l)
            for g in range(GROUP):
                lse_ref[kv * GROUP + g] = jnp.broadcast_to(lse[g * BLOCK_Q:(g + 1) * BLOCK_Q], (BLOCK_Q, HEAD_DIM))

    smem = pl.BlockSpec(memory_space=pltpu.SMEM)
    wide = GROUP * HEAD_DIM
    whole = pl.BlockSpec((sp, N_KV_B * HEAD_DIM), lambda n: (0, 0))
    res, c_res = _call(
        body, name="attn_b_fwd", grid=(nblk,),
        in_specs=[
            smem,
            smem,
            pl.BlockSpec((BLOCK_Q, wide), lambda n: (n, COL_QB // GROUP)),
            pl.BlockSpec((BLOCK_Q, wide), lambda n: (n, COL_QB // GROUP + 1)),
            whole,
            whole,
            pl.BlockSpec((BLOCK_Q, 3 * BLOCK_Q), lambda n: (0, 0)),
            _ANY,
        ],
        out_specs=[
            pl.BlockSpec((BLOCK_Q, N_HEADS_B * HEAD_DIM), lambda n: (n, 1)),
            pl.BlockSpec((N_HEADS_B, BLOCK_Q, HEAD_DIM), lambda n: (0, n, 0)),
        ],
        out_shape=[
            jax.ShapeDtypeStruct(att.shape, BF16),
            jax.ShapeDtypeStruct((N_HEADS_B, s, HEAD_DIM), F32),
        ],
        scratch_shapes=[pltpu.VMEM((N_KV_B * GQ, 3 * BLOCK_Q), F32)],
        sem=("arbitrary",), args=(table, sink, pb, pb, kpad, vpad, bucket, att), comm=comm,
        aliases={7: 0})
    return res if comm is None else (res, c_res)


def _attn_b_bwd(pb, kpad, vpad, att, datt, lse, bucket, table, sink, *, comm=None, after=None):
    s = pb.shape[0]
    nblk = s // BLOCK_Q
    sp = kpad.shape[0]

    def body(table_ref, sink_ref, q0_ref, q1_ref, k_ref, v_ref, o_ref, do_ref, lse_ref, bucket_ref,
             dq_ref, dk_ref, dv_ref, dtab_ref, dsink_ref, bias_ref, dbias_ref):
        n = pl.program_id(0)

        @pl.when(n == 0)
        def _():
            dk_ref[...] = jnp.zeros_like(dk_ref)
            dv_ref[...] = jnp.zeros_like(dv_ref)
            dbias_ref[...] = jnp.zeros_like(dbias_ref)
            dsink_ref[...] = jnp.zeros_like(dsink_ref)
            for kv in range(N_KV_B):
                bias_ref[kv * GQ:(kv + 1) * GQ, :] = _group_bias(bucket_ref[...], table_ref, kv)

        band = pl.ds(_band_start(n), 3 * BLOCK_Q)
        mask = _band_mask(n, s)
        for kv, q_ref in enumerate((q0_ref, q1_ref)):
            wide_cols = _cols(kv * GROUP, GROUP)
            q = _stack_heads(q_ref[...])
            do = _stack_heads(do_ref[:, wide_cols])
            o = _stack_heads(o_ref[:, wide_cols])
            kb = k_ref[band, _cols(kv)]
            vb = v_ref[band, _cols(kv)]
            lse = jnp.concatenate([lse_ref[kv * GROUP + g][:, :1] for g in range(GROUP)], axis=0)
            sc = lax.dot_general(q, kb, _NT, preferred_element_type=F32)
            sc = jnp.where(mask, sc + bias_ref[kv * GQ:(kv + 1) * GQ, :], NEG_INF)
            p = jnp.exp2(sc - lse)
            dp = lax.dot_general(do, vb, _NT, preferred_element_type=F32)
            delta = jnp.sum(do.astype(F32) * o.astype(F32), axis=-1, keepdims=True)
            ds = p * (dp - delta)
            dsb = ds.astype(BF16)
            dq_ref[:, wide_cols] = _unstack_heads(jnp.dot(dsb, kb, preferred_element_type=F32))
            dk_ref[band, _cols(kv)] += lax.dot_general(dsb, q, _TN, preferred_element_type=F32)
            dv_ref[band, _cols(kv)] += lax.dot_general(p.astype(BF16), do, _TN, preferred_element_type=F32)
            dbias_ref[kv * GQ:(kv + 1) * GQ, :] += ds
            sink_part = -jnp.exp2(_group_sink(sink_ref, kv) - lse) * delta
            for g in range(GROUP):
                rows = slice(g * BLOCK_Q, (g + 1) * BLOCK_Q)
                dsink_ref[kv * GROUP + g] += jnp.broadcast_to(
                    jnp.sum(sink_part[rows], axis=0, keepdims=True), (1, HEAD_DIM))

        @pl.when(n == nblk - 1)
        def _():
            bucket_v = bucket_ref[...]
            row = lax.broadcasted_iota(jnp.int32, (N_BUCKETS, HEAD_DIM), 0)
            for h in range(N_HEADS_B):
                acc = dbias_ref[h * BLOCK_Q:(h + 1) * BLOCK_Q, :]
                tot = jnp.zeros((N_BUCKETS, HEAD_DIM), F32)
                for b in range(N_BUCKETS):
                    tot = jnp.where(row == b, jnp.sum(jnp.where(bucket_v == b, acc, 0.0), keepdims=True), tot)
                dtab_ref[h] = tot

    smem = pl.BlockSpec(memory_space=pltpu.SMEM)
    wide = GROUP * HEAD_DIM
    whole = pl.BlockSpec((sp, N_KV_B * HEAD_DIM), lambda n: (0, 0))
    group_b = pl.BlockSpec((BLOCK_Q, N_HEADS_B * HEAD_DIM), lambda n: (n, 1))
    res, c_res = _call(
        body, name="attn_b_bwd", grid=(nblk,),
        in_specs=[
            smem,
            smem,
            pl.BlockSpec((BLOCK_Q, wide), lambda n: (n, COL_QB // GROUP)),
            pl.BlockSpec((BLOCK_Q, wide), lambda n: (n, COL_QB // GROUP + 1)),
            whole,
            whole,
            group_b,
            group_b,
            pl.BlockSpec((N_HEADS_B, BLOCK_Q, HEAD_DIM), lambda n: (0, n, 0)),
            pl.BlockSpec((BLOCK_Q, 3 * BLOCK_Q), lambda n: (0, 0)),
        ],
        out_specs=[
            pl.BlockSpec((BLOCK_Q, N_HEADS_B * HEAD_DIM), lambda n: (n, 0)),
            whole,
            whole,
            pl.BlockSpec((N_HEADS_B, N_BUCKETS, HEAD_DIM), lambda n: (0, 0, 0)),
            pl.BlockSpec((N_HEADS_B, 1, HEAD_DIM), lambda n: (0, 0, 0)),
        ],
        out_shape=[
            jax.ShapeDtypeStruct((s, N_HEADS_B * HEAD_DIM), F32),
            jax.ShapeDtypeStruct((sp, N_KV_B * HEAD_DIM), F32),
            jax.ShapeDtypeStruct((sp, N_KV_B * HEAD_DIM), F32),
            jax.ShapeDtypeStruct((N_HEADS_B, N_BUCKETS, HEAD_DIM), F32),
            jax.ShapeDtypeStruct((N_HEADS_B, 1, HEAD_DIM), F32),
        ],
        scratch_shapes=[pltpu.VMEM((N_KV_B * GQ, 3 * BLOCK_Q), F32), pltpu.VMEM((N_KV_B * GQ, 3 * BLOCK_Q), F32)],
        sem=("arbitrary",),
        args=(table, sink, pb, pb, kpad, vpad, att, datt, lse, bucket), comm=comm, after=after)
    return res if comm is None else (res, c_res)


_MESH = pl.DeviceIdType.MESH


def _other_chips(x, y):
    return [(x, 1 - y), (1 - x, y), (1 - x, 1 - y)]


_HBM = pl.BlockSpec(memory_space=pltpu.HBM)
_SEM = pl.BlockSpec(memory_space=pltpu.SEMAPHORE)
_SPLIT = pltpu.CompilerParams(has_side_effects=pltpu.SideEffectType.DATAFLOW_SIDE_EFFECTING)


def _in_hbm(a):
    return pltpu.with_memory_space_constraint(a, pltpu.HBM)


def _my_half(rows):
    c = lax.axis_index("c")
    half = rows // 2
    return pl.ds(pl.multiple_of(c * half, half), half), pl.ds(pl.multiple_of((1 - c) * half, half), half)


def _gather_route(shapes):
    def route(src, land):
        x, y, c = lax.axis_index("x"), lax.axis_index("y"), lax.axis_index("c")
        out = []
        for t, shape in enumerate(shapes):
            mine, _ = _my_half(shape[0])
            for px, py in _other_chips(x, y):
                out.append((src[t].at[mine], land[t].at[2 * x + y, mine], land[t].at[2 * px + py, mine], (px, py, c)))
        return out

    return route


def _exchange_route(n_t):
    def route(src, land):
        x, y, c = lax.axis_index("x"), lax.axis_index("y"), lax.axis_index("c")
        out = []
        for t in range(n_t):
            for px, py in _other_chips(x, y):
                k = 2 * px + py
                out.append((src[t].at[k], land[t].at[2 * (2 * x + y) + c], land[t].at[2 * k + c], (px, py, c)))
        return out

    return route


def _own_slot(shape, dtype, slot, block):
    return lax.dynamic_update_slice(lax.empty(shape, dtype), block[None], (slot,) + (0,) * (len(shape) - 1))


def _split_start(name, srcs, lands, route, after):
    n = len(srcs)

    def body(*refs):
        src, land, send_sems, recv_sems, token = refs[:n], refs[n:2 * n], refs[2 * n + 1], refs[2 * n + 2], refs[-1]
        for i, (src_ref, dst_ref, _, to) in enumerate(route(src, land)):
            pltpu.make_async_remote_copy(src_ref=src_ref, dst_ref=dst_ref, send_sem=send_sems.at[i],
                                         recv_sem=recv_sems.at[i], device_id=to, device_id_type=_MESH).start()
        token[...] = jnp.zeros_like(token)

    sem = pltpu.SemaphoreType.DMA((3 * n,))
    lands = list(lands)
    res = pl.pallas_call(
        body, name=name,
        in_specs=[_HBM] * (2 * n) + [_ANY],
        out_specs=[_SEM, _SEM] + [_HBM] * (2 * n) + [pl.BlockSpec(memory_space=pltpu.VMEM)],
        out_shape=[sem, sem] + [pltpu.HBM(a.shape, a.dtype) for a in list(srcs) + lands]
        + [jax.ShapeDtypeStruct((8, 128), F32)],
        input_output_aliases={i: 2 + i for i in range(2 * n)},
        compiler_params=_SPLIT,
    )(*[_in_hbm(a) for a in srcs], *[_in_hbm(a) for a in lands], after)
    return (res[0], res[1]), res[2:2 + n], res[2 + n:2 + 2 * n], res[-1]


def _split_wait(name, srcs, lands, sems, route, after):
    n = len(srcs)

    def body(*refs):
        src, land, send_sems, recv_sems = refs[:n], refs[n:2 * n], refs[2 * n], refs[2 * n + 1]
        for i, (src_ref, _, dst_ref, to) in enumerate(route(src, land)):
            cp = pltpu.make_async_remote_copy(src_ref=src_ref, dst_ref=dst_ref, send_sem=send_sems.at[i],
                                              recv_sem=recv_sems.at[i], device_id=to, device_id_type=_MESH)
            cp.wait_send()
            cp.wait_recv()

    res = pl.pallas_call(
        body, name=name,
        in_specs=[_HBM] * (2 * n) + [_SEM, _SEM, _ANY],
        out_specs=[_HBM] * (2 * n),
        out_shape=[pltpu.HBM(a.shape, a.dtype) for a in list(srcs) + list(lands)],
        input_output_aliases={i: i for i in range(2 * n)},
        compiler_params=_SPLIT,
    )(*srcs, *lands, sems[0], sems[1], after)
    return res[:n], res[n:]


def _comm_only(name, comm):
    return _call(lambda: None, name=name, grid=(1,), in_specs=[], out_specs=[], out_shape=[], args=(), comm=comm)[1]


def _swap_comm(shards, lands):
    n_t = len(lands)

    def copies(land, sems, later):
        send_sems, recv_sems = sems
        x, y = lax.axis_index("x"), lax.axis_index("y")
        sends, recvs = [], []
        for t in range(n_t):
            mine, other = _my_half(shards[t].shape[0])
            for j, (px, py) in enumerate(_other_chips(x, y)):
                k = 2 * px + py
                for part, out in ((mine, sends), (other, recvs)) if later else ((mine, sends),):
                    out.append(pltpu.make_async_remote_copy(
                        src_ref=land[t].at[k, part], dst_ref=land[t].at[k, part], send_sem=send_sems.at[3 * t + j],
                        recv_sem=recv_sems.at[3 * t + j], device_id=_sibling(), device_id_type=_MESH))
        return sends, recvs

    def start(ins, land, sems):
        for cp in copies(land, sems, False)[0]:
            cp.start()

    def finish(ins, land, sems):
        sends, recvs = copies(land, sems, True)
        for cp in recvs:
            cp.wait_recv()
        for cp in sends:
            cp.wait_send()

    return _Comm(
        lands, [jax.ShapeDtypeStruct(a.shape, a.dtype) for a in lands],
        [pltpu.SemaphoreType.DMA((3 * n_t,)), pltpu.SemaphoreType.DMA((3 * n_t,))],
        start, finish, aliases={t: t for t in range(n_t)})


def _forward_comm(partials, lands):
    n_t = len(lands)

    def copies(land, sems, later):
        send_sems, recv_sems = sems
        x, y, c = lax.axis_index("x"), lax.axis_index("y"), lax.axis_index("c")
        sends, recvs = [], []
        for t in range(n_t):
            for j, k in enumerate([2 * x + y] + [2 * px + py for px, py in _other_chips(x, y)]):
                for slot, out in ((2 * k + c, sends), (2 * k + 1 - c, recvs)) if later else ((2 * k + c, sends),):
                    out.append(pltpu.make_async_remote_copy(
                        src_ref=land[t].at[slot], dst_ref=land[t].at[slot], send_sem=send_sems.at[4 * t + j],
                        recv_sem=recv_sems.at[4 * t + j], device_id=_sibling(), device_id_type=_MESH))
        return sends, recvs

    def start(ins, land, sems):
        for cp in copies(land, sems, False)[0]:
            cp.start()

    def finish(ins, land, sems):
        sends, recvs = copies(land, sems, True)
        for cp in recvs:
            cp.wait_recv()
        for cp in sends:
            cp.wait_send()

    return _Comm(
        lands, [jax.ShapeDtypeStruct(a.shape, a.dtype) for a in lands],
        [pltpu.SemaphoreType.DMA((4 * n_t,)), pltpu.SemaphoreType.DMA((4 * n_t,))],
        start, finish, aliases={t: t for t in range(n_t)})


def _allreduce_small(pack):
    rows, d = pack.shape

    def body(p_ref, sum_ref, all_ref, send_sems, recv_sems):
        x, y, c = lax.axis_index("x"), lax.axis_index("y"), lax.axis_index("c")
        me = 4 * x + 2 * y + c
        all_ref[me] = p_ref[...]
        peers = []
        for dx in range(2):
            for dy in range(2):
                for dc in range(2):
                    if dx or dy or dc:
                        px = 1 - x if dx else x
                        py = 1 - y if dy else y
                        pc = 1 - c if dc else c
                        peers.append((4 * dx + 2 * dy + dc - 1, (px, py, pc)))
        sends = []
        for k, to in peers:
            cp = pltpu.make_async_remote_copy(
                src_ref=p_ref, dst_ref=all_ref.at[me], send_sem=send_sems.at[k], recv_sem=recv_sems.at[k],
                device_id=to, device_id_type=_MESH)
            cp.start()
            sends.append(cp)
        for k, (px, py, pc) in peers:
            pltpu.make_async_remote_copy(
                src_ref=p_ref, dst_ref=all_ref.at[4 * px + 2 * py + pc], send_sem=send_sems.at[k],
                recv_sem=recv_sems.at[k], device_id=(px, py, pc), device_id_type=_MESH).wait_recv()
        for cp in sends:
            cp.wait_send()
        tot = all_ref[0]
        for i in range(1, N_DEV):
            tot = tot + all_ref[i]
        sum_ref[...] = tot

    vm = pl.BlockSpec(memory_space=pltpu.VMEM)
    return pl.pallas_call(
        body,
        name="allreduce_small",
        in_specs=[vm],
        out_specs=vm,
        out_shape=jax.ShapeDtypeStruct((rows, d), F32),
        scratch_shapes=[
            pltpu.VMEM((N_DEV, rows, d), F32),
            pltpu.SemaphoreType.DMA((N_DEV - 1,)),
            pltpu.SemaphoreType.DMA((N_DEV - 1,)),
        ],
    )(pack)


def _adamw_math(w, g, m, v):
    m = ADAM_B1 * m + (1.0 - ADAM_B1) * g
    v = ADAM_B2 * v + (1.0 - ADAM_B2) * (g * g)
    m_hat = m / (1.0 - ADAM_B1 ** ADAM_STEP)
    v_hat = v / (1.0 - ADAM_B2 ** ADAM_STEP)
    delta = -ADAM_LR * (m_hat / (jnp.sqrt(v_hat) + ADAM_EPS) + ADAM_WD * w)
    return delta, m, v


def _sum_adamw(parts, w, m, v, *, name, tr=256):
    r, c = w.shape
    tr = min(tr, r)
    tc = min(c, 1024)

    def body(p_ref, w_ref, m_ref, v_ref, g_ref, d_ref, m2_ref, v2_ref):
        g = p_ref[0].astype(F32)
        for i in range(1, N_DEV):
            g = g + p_ref[i].astype(F32)
        delta, m2, v2 = _adamw_math(w_ref[...], g, m_ref[...], v_ref[...])
        g_ref[...] = g
        d_ref[...] = delta
        m2_ref[...] = m2
        v2_ref[...] = v2

    blk = pl.BlockSpec((tr, tc), lambda i, j: (i, j))
    return pl.pallas_call(
        body,
        name=name,
        grid=(r // tr, c // tc),
        in_specs=[pl.BlockSpec((N_DEV, tr, tc), lambda i, j: (0, i, j)), blk, blk, blk],
        out_specs=[blk] * 4,
        out_shape=[jax.ShapeDtypeStruct((r, c), F32)] * 4,
        compiler_params=_params(("parallel", "parallel")),
    )(parts, w, m, v)


def _adamw_small(g, w, m, v):
    def body(g_ref, w_ref, m_ref, v_ref, d_ref, m2_ref, v2_ref):
        delta, m2, v2 = _adamw_math(w_ref[...], g_ref[...], m_ref[...], v_ref[...])
        d_ref[...] = delta
        m2_ref[...] = m2
        v2_ref[...] = v2

    vm = pl.BlockSpec(memory_space=pltpu.VMEM)
    return pl.pallas_call(
        body,
        name="adamw_small",
        in_specs=[vm] * 4,
        out_specs=[vm] * 3,
        out_shape=[jax.ShapeDtypeStruct(g.shape, F32)] * 3,
    )(g, w, m, v)


def _relu2_epilogue(acc):
    ra = jnp.maximum(acc, 0.0)
    return ra * ra, ra


def _rows(stacked):
    return stacked.reshape(stacked.shape[0] * stacked.shape[1], stacked.shape[2])


def _by_chip(mat):
    return mat.reshape(N_CHIPS, mat.shape[0] // N_CHIPS, mat.shape[1])


def _local_step(x, p, target, shards, small, update):
    s, d = x.shape
    cos_t, sin_t = _rope_tables(s)
    bucket = _band_buckets()
    p_bf = p.astype(BF16)
    wts = {}

    chip = 2 * lax.axis_index("x") + lax.axis_index("y")
    core = lax.axis_index("c")

    def gather(tag, names, after):
        srcs = [shards[n] for n in names]
        route = _gather_route([a.shape for a in srcs])
        lands = [_own_slot((N_CHIPS,) + a.shape, a.dtype, chip, a) for a in srcs]
        sems, srcs, lands, token = _split_start(f"gather_start_{tag}", srcs, lands, route, after)
        return lambda done: _swap_comm(*_split_wait(f"gather_wait_{tag}", srcs, lands, sems, route, done)), token

    in_landed, token = gather("in", ["w_in"], small["attn_norm_g"])
    g_attn = small["attn_norm_g"] + token[:1, :1]
    u = _rms_fwd(x, g_attn, name="norm_attn")
    (wts["w_in"],) = _comm_only("swap_w_in", in_landed(u))
    mid_landed, token = gather("mid", ["w_out"], wts["w_in"])
    proj = _matmul(u, wts["w_in"], mode="nn", out_dtypes=[F32], name="mm_in", bn=768, after=token)
    pb, (w_out_s,) = _qk_prep(proj, small["q_norm_g"], small["k_norm_g"], cos_t, sin_t, comm=mid_landed(proj))
    wts["w_out"] = _rows(w_out_s)
    up_landed, token = gather("up", ["w_up"], pb)
    att_a, lse_a = _attn_a_fwd(pb, after=token)
    pad = ((PAD_LO, PAD_HI), (0, 0))
    kpad = jnp.pad(pb[:, COL_KB * HEAD_DIM:COL_VB * HEAD_DIM], pad)
    vpad = jnp.pad(pb[:, COL_VB * HEAD_DIM:], pad)
    (att, lse_b), (wts["w_up"],) = _attn_b_fwd(pb, kpad, vpad, bucket, small["rel_bias_table"],
                                               small["sink_logits"], att_a, comm=up_landed(att_a))
    down_landed, token = gather("down", ["w_down"], att)
    h1 = _matmul(att, wts["w_out"], mode="nn", out_dtypes=[F32], name="mm_out",
                 epilogue=lambda acc, res: (acc + res,), extras=(x,), after=token)
    mn = _rms_fwd(h1, small["mlp_norm_g"], name="norm_mlp")
    r, ra = _matmul(mn, wts["w_up"], mode="nn", out_dtypes=[BF16, BF16], name="mm_up", epilogue=_relu2_epilogue,
                    bm=2048)
    (w_down_s,) = _comm_only("swap_w_down", down_landed(r))
    wts["w_down"] = _rows(w_down_s)
    late_landed, token = gather("late", ["w_gate", "ple_w"], w_down_s)
    h2 = _matmul(r, wts["w_down"], mode="nn", out_dtypes=[F32], name="mm_down",
                 epilogue=lambda acc, res: (acc + res,), extras=(h1,), after=token)
    ng, (w_gate_s, wts["ple_w"]) = _rms_fwd(h2, small["gate_norm_g"], name="norm_gate", comm=late_landed(h2))
    wts["w_gate"] = _rows(w_gate_s)
    gate = _matmul(ng, wts["w_gate"], mode="nn", out_dtypes=[F32], name="mm_gate",
                   epilogue=lambda acc: (1.0 / (1.0 + jnp.exp(-acc)),))
    pp = _matmul(p_bf, wts["ple_w"], mode="nn", out_dtypes=[F32], name="mm_ple", bn=512)
    dh3, dz, dpp, dg_final, dg_ple, loss = _tail(h2, gate, pp, target, small["ple_norm_g"], small["final_norm_g"])

    dng = _matmul(dz, wts["w_gate"], mode="nt", out_dtypes=[F32], name="mm_gate_dx")
    gw_gate = _matmul(ng, dz, mode="tn", out_dtypes=[BF16], name="mm_gate_dw")
    gw_ple = _matmul(p_bf, dpp, mode="tn", out_dtypes=[BF16], name="mm_ple_dw", bn=512, out_stack=N_CHIPS)
    dh2, dh2_bf, dg_gate = _rms_bwd(h2, dng, small["gate_norm_g"], dh3, name="norm_gate_bwd", want_bf16=True)

    def exchange(tag, partials, after):
        route = _exchange_route(len(partials))
        lands = [_own_slot((N_DEV,) + g.shape[1:], g.dtype, 2 * chip + core,
                           lax.dynamic_index_in_dim(g, chip, 0, keepdims=False)) for g in partials]
        sems, srcs, lands, token = _split_start(f"exchange_start_{tag}", partials, lands, route, after)
        return lambda done: _forward_comm(*_split_wait(f"exchange_wait_{tag}", srcs, lands, sems, route, done)), token

    big = {}
    gate_landed, token = exchange("gate", [_by_chip(gw_gate), gw_ple], dh2_bf)
    gw_down = _matmul(r, dh2_bf, mode="tn", out_dtypes=[BF16], name="mm_down_dw", after=token)
    da, (parts_gate, parts_ple) = _matmul(
        dh2_bf, wts["w_down"], mode="nt", out_dtypes=[BF16], name="mm_down_dx", bm=2048,
        epilogue=lambda acc, ra_v: (acc * (2.0 * ra_v.astype(F32)),), extras=(ra,), comm=gate_landed(gw_down))
    down_landed, token = exchange("down", [_by_chip(gw_down)], da)
    big["w_gate"], big["ple_w"] = update("w_gate", parts_gate), update("ple_w", parts_ple)
    gw_up = _matmul(mn, da, mode="tn", out_dtypes=[BF16], name="mm_up_dw", out_stack=N_CHIPS, after=token)
    dmn = _matmul(da, wts["w_up"], mode="nt", out_dtypes=[F32], name="mm_up_dx")
    dh1, dh1_bf, dg_mlp = _rms_bwd(h1, dmn, small["mlp_norm_g"], dh2, name="norm_mlp_bwd", want_bf16=True)
    datt, (parts_down,) = _matmul(dh1_bf, wts["w_out"], mode="nt", out_dtypes=[BF16], name="mm_out_dx",
                                  comm=down_landed(dh1_bf))
    gw_out = _matmul(att, dh1_bf, mode="tn", out_dtypes=[BF16], name="mm_out_dw")
    up_landed, token = exchange("up", [gw_up], datt)
    dqb, dkpad, dvpad, dtab, dsink = _attn_b_bwd(pb, kpad, vpad, att, datt, lse_b, bucket,
                                                 small["rel_bias_table"], small["sink_logits"], after=token)
    dqa, dka, dva = _attn_a_bwd(pb, att, datt, lse_a)
    (dproj, dg_q, dg_k), (parts_up,) = _qk_bwd(dqa, dka, dva, dqb, dkpad, dvpad, proj,
                                               small["q_norm_g"], small["k_norm_g"], cos_t, sin_t,
                                               comm=up_landed(dqa))
    gw_in = _matmul(u, dproj, mode="tn", out_dtypes=[BF16], name="mm_in_dw", bn=768, out_stack=N_CHIPS)
    in_landed, token = exchange("in", [gw_in, _by_chip(gw_out)], dproj)
    du = _matmul(dproj, wts["w_in"], mode="nt", out_dtypes=[F32], name="mm_in_dx", bk=768, after=token)
    grad_x, dg_attn = _rms_bwd(x, du, small["attn_norm_g"], dh1, name="norm_attn_bwd", want_bf16=False)
    for n, parts in (("w_down", parts_down), ("w_up", parts_up)):
        big[n] = update(n, parts)
    done = dg_attn + sum(big[n][0][0, :1, :] for n in ("w_down", "w_up"))
    parts_in, parts_out = _comm_only("forward_w_in", in_landed(done))
    big["w_in"], big["w_out"] = update("w_in", parts_in), update("w_out", parts_out)

    small_g = {
        "attn_norm_g": dg_attn, "mlp_norm_g": dg_mlp, "ple_norm_g": dg_ple, "gate_norm_g": dg_gate,
        "final_norm_g": dg_final, "q_norm_g": dg_q, "k_norm_g": dg_k,
        "sink_logits": dsink[:, 0, 0][None, :], "rel_bias_table": dtab[:, :, 0].T,
    }
    return loss, grad_x, big, small_g


_SMALL_ROWS = ["attn_norm_g", "mlp_norm_g", "ple_norm_g", "gate_norm_g", "final_norm_g"]
_PACK_ROWS = 8


def _pack_small(vals, d):
    rows = [vals[n].reshape(1, d) for n in _SMALL_ROWS]
    misc = jnp.concatenate([
        vals["q_norm_g"].reshape(1, HEAD_DIM), vals["k_norm_g"].reshape(1, HEAD_DIM),
        jnp.pad(vals["sink_logits"].reshape(1, N_HEADS_B), ((0, 0), (0, HEAD_DIM - N_HEADS_B))),
        vals["rel_bias_table"].reshape(1, N_BUCKETS * N_HEADS_B)], axis=1)
    rows.append(jnp.pad(misc, ((0, 0), (0, d - misc.shape[1]))))
    rows.append(jnp.zeros((_PACK_ROWS - len(rows), d), F32))
    return jnp.concatenate(rows, axis=0).astype(F32)


def _unpack_small(pack, shapes):
    out = {n: pack[i].reshape(shapes[n]) for i, n in enumerate(_SMALL_ROWS)}
    misc = pack[len(_SMALL_ROWS)]
    out["q_norm_g"] = misc[:HEAD_DIM].reshape(shapes["q_norm_g"])
    out["k_norm_g"] = misc[HEAD_DIM:2 * HEAD_DIM].reshape(shapes["k_norm_g"])
    out["sink_logits"] = misc[2 * HEAD_DIM:2 * HEAD_DIM + N_HEADS_B].reshape(shapes["sink_logits"])
    out["rel_bias_table"] = misc[3 * HEAD_DIM:3 * HEAD_DIM + N_BUCKETS * N_HEADS_B].reshape(shapes["rel_bias_table"])
    return out


_WEIGHTS = ["attn_norm_g", "w_in", "q_norm_g", "k_norm_g", "sink_logits", "w_out", "mlp_norm_g", "w_up", "w_down",
            "ple_w", "ple_norm_g", "gate_norm_g", "w_gate", "rel_bias_table", "final_norm_g"]
_BIG = ["w_in", "w_out", "w_up", "w_down", "ple_w", "w_gate"]


def kernel(x, p, attn_norm_g, w_in, q_norm_g, k_norm_g, sink_logits, w_out, mlp_norm_g, w_up, w_down, ple_w, ple_norm_g, gate_norm_g, w_gate, rel_bias_table, final_norm_g, loss_target, m_attn_norm_g, m_w_in, m_q_norm_g, m_k_norm_g, m_sink_logits, m_w_out, m_mlp_norm_g, m_w_up, m_w_down, m_ple_w, m_ple_norm_g, m_gate_norm_g, m_w_gate, m_rel_bias_table, m_final_norm_g, v_attn_norm_g, v_w_in, v_q_norm_g, v_k_norm_g, v_sink_logits, v_w_out, v_mlp_norm_g, v_w_up, v_w_down, v_ple_w, v_ple_norm_g, v_gate_norm_g, v_w_gate, v_rel_bias_table, v_final_norm_g):
    given = dict(locals())
    w = {n: given[n] for n in _WEIGHTS}
    m = {n: given["m_" + n] for n in _WEIGHTS}
    v = {n: given["v_" + n] for n in _WEIGHTS}
    d = x.shape[-1]

    shards = {n: w[n][0].astype(BF16) for n in _BIG}
    small = {
        "attn_norm_g": w["attn_norm_g"], "mlp_norm_g": w["mlp_norm_g"], "ple_norm_g": w["ple_norm_g"],
        "gate_norm_g": w["gate_norm_g"], "final_norm_g": w["final_norm_g"].reshape(1, d),
        "q_norm_g": w["q_norm_g"], "k_norm_g": w["k_norm_g"], "sink_logits": w["sink_logits"],
        "rel_bias_table": w["rel_bias_table"],
    }

    def update(n, parts):
        res = _sum_adamw(parts, w[n][0], m[n][0], v[n][0], name="adamw_" + n)
        return [t.reshape(w[n].shape) for t in res]

    loss_part, grad_x, big, small_g = _local_step(x[0], p[0, 0], loss_target[0], shards, small, update)
    loss = lax.psum(loss_part[0, 0], ("x", "y", "c"))
    grads, deltas, new_m, new_v = [{n: big[n][i] for n in _BIG} for i in range(4)]

    shapes = {n: w[n].shape for n in _WEIGHTS if n not in _BIG}
    pack = _pack_small(small_g, d)
    pack = pack.at[_PACK_ROWS - 1, :1].add(0.0 * grads["w_in"][0, 0, :1])
    g_small = _allreduce_small(pack)
    d_small, m_small, v_small = _adamw_small(g_small, _pack_small(w, d), _pack_small(m, d), _pack_small(v, d))
    grads.update(_unpack_small(g_small, shapes))
    deltas.update(_unpack_small(d_small, shapes))
    new_m.update(_unpack_small(m_small, shapes))
    new_v.update(_unpack_small(v_small, shapes))

    return (loss, grad_x[None], *[grads[n] for n in _WEIGHTS], *[deltas[n] for n in _WEIGHTS],
            *[new_m[n] for n in _WEIGHTS], *[new_v[n] for n in _WEIGHTS])
```

```python
import functools
import math

import jax
import jax.numpy as jnp
import numpy as np
from jax import lax
from jax.experimental import pallas as pl
from jax.experimental.pallas import tpu as pltpu

F32 = jnp.float32
BF16 = jnp.bfloat16

HEAD_DIM = 128
N_HEADS_A = 8
N_KV_A = 2
N_HEADS_B = 8
N_KV_B = 2
GROUP = 4
GRID_W = 64
BLOCK_Q = 128
WINDOW = 128
N_BUCKETS = 32
MAX_DISTANCE = 128
ROPE_THETA = 10000.0
EPS = 1e-6
NEG_INF = -1e30
ATT_SCALE = HEAD_DIM ** -0.5
LOG2E = math.log2(math.e)
LN2 = math.log(2.0)
Q_SCALE = ATT_SCALE * LOG2E
PAD_LO, PAD_HI = 256, 128

ADAM_LR = 0.001
ADAM_B1 = 0.9
ADAM_B2 = 0.999
ADAM_EPS = 1e-08
ADAM_WD = 0.01
ADAM_STEP = 10

N_CHIPS = 4
N_DEV = 8
COL_QA, COL_KA, COL_VA, COL_QB, COL_KB, COL_VB = 0, 8, 10, 12, 20, 22
N_COLS = 24

VMEM_LIMIT = 52 * 1024 * 1024


def _params(sem=None, collective_id=None):
    return pltpu.CompilerParams(dimension_semantics=sem, vmem_limit_bytes=VMEM_LIMIT, collective_id=collective_id)


_ANY = pl.BlockSpec(memory_space=pl.ANY)
_MESH = pl.DeviceIdType.MESH
SIBLING_BARRIER_ID = 1


def _sibling():
    return (lax.axis_index("x"), lax.axis_index("y"), 1 - lax.axis_index("c"))


class _Comm:
    def __init__(self, inputs, out_shapes, sems, start, finish, aliases=None):
        self.inputs, self.out_shapes, self.sems = list(inputs), list(out_shapes), list(sems)
        self.start, self.finish, self.aliases = start, finish, dict(aliases or {})


def _call(body, *, name, grid, in_specs, out_specs, out_shape, args, scratch_shapes=(), sem=None, comm=None,
          after=None, aliases=None):
    in_specs, out_specs, out_shape = list(in_specs), list(out_specs), list(out_shape)
    scratch_shapes = list(scratch_shapes)
    n_in, n_out, n_sc = len(in_specs), len(out_specs), len(scratch_shapes)
    behind = [] if after is None else [after]
    aliases = dict(aliases or {})
    if comm is None:
        res = pl.pallas_call(
            (lambda *refs: body(*refs[:n_in], *refs[n_in + len(behind):])) if behind else body,
            name=name, grid=grid, in_specs=in_specs + [_ANY] * len(behind), out_specs=out_specs,
            out_shape=out_shape, scratch_shapes=scratch_shapes, input_output_aliases=aliases,
            compiler_params=_params(sem))(*args, *behind)
        return list(res), []
    c_in, c_out = len(comm.inputs), len(comm.out_shapes)

    def hosted(*refs):
        pos = [0]

        def take(n):
            pos[0] += n
            return refs[pos[0] - n:pos[0]]

        ins, c_ins, _, outs, c_outs, scr = (take(n_in), take(c_in), take(len(behind)), take(n_out), take(c_out),
                                            take(n_sc))
        c_sems = refs[pos[0]:]
        ids = [pl.program_id(a) for a in range(len(grid))]
        first = functools.reduce(jnp.logical_and, [i == 0 for i in ids])
        last = functools.reduce(jnp.logical_and, [i == g - 1 for i, g in zip(ids, grid)])

        @pl.when(first)
        def _():
            barrier = pltpu.get_barrier_semaphore()
            pl.semaphore_signal(barrier, inc=1, device_id=_sibling(), device_id_type=_MESH)
            pl.semaphore_wait(barrier, 1)
            comm.start(c_ins, c_outs, c_sems)

        body(*ins, *outs, *scr)

        @pl.when(last)
        def _():
            comm.finish(c_ins, c_outs, c_sems)

    res = pl.pallas_call(
        hosted, name=name, grid=grid, in_specs=in_specs + [_ANY] * (c_in + len(behind)),
        out_specs=out_specs + [_ANY] * c_out,
        out_shape=out_shape + comm.out_shapes, scratch_shapes=scratch_shapes + comm.sems,
        input_output_aliases={**aliases, **{n_in + i: n_out + o for i, o in comm.aliases.items()}},
        compiler_params=_params(("arbitrary",) * len(grid), SIBLING_BARRIER_ID))(*args, *comm.inputs, *behind)
    return list(res[:n_out]), list(res[n_out:])


def _matmul(a, b, *, mode, out_dtypes, name, epilogue=None, extras=(), bm=1024, bn=1024, bk=2048,
            out_stack=0, comm=None, after=None):
    stacked = b.ndim == 3
    if mode == "nn":
        m, k = a.shape
        if stacked:
            nj, kb, ns = b.shape
            n, ks = nj * ns, k
        else:
            kb, n = b.shape
            ns, ks = n, k
        dn = (((1,), (0,)), ((), ()))
    elif mode == "nt":
        m, k = a.shape
        if stacked:
            nj, n, ks = b.shape
            kb = nj * ks
        else:
            n, kb = b.shape
            ks = kb
        ns = n
        dn = (((1,), (1,)), ((), ()))
    else:
        k, m = a.shape
        kb, n = b.shape
        ns, ks = n, k
        dn = (((0,), (0,)), ((), ()))
    assert k == kb and not (stacked and mode == "tn")
    ns_out = n // out_stack if out_stack else n
    bm, bn, bk = min(bm, m), min(bn, ns, ns_out), min(bk, ks)
    assert m % bm == 0 and ns % bn == 0 and ns_out % bn == 0 and ks % bk == 0
    gm, gn, gk = m // bm, n // bn, k // bk

    if mode == "tn":
        a_spec = pl.BlockSpec((bk, bm), lambda i, j, q: (q, i))
    else:
        a_spec = pl.BlockSpec((bm, bk), lambda i, j, q: (i, q))
    if mode == "nt":
        if stacked:
            per = ks // bk
            b_spec = pl.BlockSpec((None, bn, bk), lambda i, j, q: (q // per, j, q % per))
        else:
            b_spec = pl.BlockSpec((bn, bk), lambda i, j, q: (j, q))
    else:
        if stacked:
            per = ns // bn
            b_spec = pl.BlockSpec((None, bk, bn), lambda i, j, q: (j // per, q, j % per))
        else:
            b_spec = pl.BlockSpec((bk, bn), lambda i, j, q: (q, j))
    ex_spec = pl.BlockSpec((bm, bn), lambda i, j, q: (i, j))
    if out_stack:
        per_o = ns_out // bn
        o_spec = pl.BlockSpec((None, bm, bn), lambda i, j, q: (j // per_o, i, j % per_o))
        o_shape = (out_stack, m, ns_out)
    else:
        o_spec = ex_spec
        o_shape = (m, n)
    n_ex, n_out = len(extras), len(out_dtypes)

    def body(a_ref, b_ref, *rest):
        ex, outs = rest[:n_ex], rest[n_ex:n_ex + n_out]
        part = lax.dot_general(a_ref[...], b_ref[...], dn, preferred_element_type=F32)

        def finish(acc):
            res = epilogue(acc, *[e[...] for e in ex]) if epilogue else (acc,)
            for o, r in zip(outs, res):
                o[...] = r.astype(o.dtype)

        if gk == 1:
            finish(part)
        else:
            acc_ref = rest[-1]
            q = pl.program_id(2)

            @pl.when(q == 0)
            def _():
                acc_ref[...] = part

            @pl.when(q > 0)
            def _():
                acc_ref[...] += part

            @pl.when(q == gk - 1)
            def _():
                finish(acc_ref[...])

    res, c_res = _call(
        body, name=name, grid=(gm, gn, gk),
        in_specs=[a_spec, b_spec] + [ex_spec] * n_ex,
        out_specs=[o_spec] * n_out,
        out_shape=[jax.ShapeDtypeStruct(o_shape, dt) for dt in out_dtypes],
        scratch_shapes=[pltpu.VMEM((bm, bn), F32)] if gk > 1 else [],
        sem=("parallel", "parallel", "arbitrary"), args=(a, b, *extras), comm=comm, after=after)
    res = res[0] if n_out == 1 else res
    return res if comm is None else (res, c_res)


def _rms_fwd(x, g, *, name, tm=256, comm=None):
    s, d = x.shape
    tm = min(tm, s)

    def body(x_ref, g_ref, o_ref):
        xf = x_ref[...]
        r = lax.rsqrt(jnp.mean(xf * xf, axis=-1, keepdims=True) + EPS)
        o_ref[...] = (xf * r * g_ref[...]).astype(o_ref.dtype)

    res, c_res = _call(
        body, name=name, grid=(s // tm,),
        in_specs=[pl.BlockSpec((tm, d), lambda i: (i, 0)), pl.BlockSpec((1, d), lambda i: (0, 0))],
        out_specs=[pl.BlockSpec((tm, d), lambda i: (i, 0))],
        out_shape=[jax.ShapeDtypeStruct((s, d), BF16)],
        sem=("parallel",), args=(x, g), comm=comm)
    return res[0] if comm is None else (res[0], c_res)


def _rms_bwd(x, dy, g, add, *, name, want_bf16, tm=256):
    s, d = x.shape
    tm = min(tm, s)

    def body(x_ref, dy_ref, g_ref, add_ref, dx_ref, *rest):
        dg_ref = rest[-1]
        i = pl.program_id(0)
        xf = x_ref[...]
        dyf = dy_ref[...].astype(F32)
        r = lax.rsqrt(jnp.mean(xf * xf, axis=-1, keepdims=True) + EPS)
        xh = xf * r
        dyg = dyf * g_ref[...]
        dx = r * (dyg - xh * jnp.mean(dyg * xh, axis=-1, keepdims=True))
        tot = add_ref[...] + dx
        dx_ref[...] = tot
        if want_bf16:
            rest[0][...] = tot.astype(BF16)
        part = jnp.sum(dyf * xh, axis=0, keepdims=True)

        @pl.when(i == 0)
        def _():
            dg_ref[...] = part

        @pl.when(i > 0)
        def _():
            dg_ref[...] += part

    row = pl.BlockSpec((tm, d), lambda i: (i, 0))
    vec = pl.BlockSpec((1, d), lambda i: (0, 0))
    out_specs = [row] + ([row] if want_bf16 else []) + [vec]
    out_shape = [jax.ShapeDtypeStruct((s, d), F32)]
    if want_bf16:
        out_shape.append(jax.ShapeDtypeStruct((s, d), BF16))
    out_shape.append(jax.ShapeDtypeStruct((1, d), F32))
    return pl.pallas_call(
        body,
        name=name,
        grid=(s // tm,),
        in_specs=[row, row, vec, row],
        out_specs=out_specs,
        out_shape=out_shape,
        compiler_params=_params(("arbitrary",)),
    )(x, dy, g, add)


def _tail(h2, gate, pp, target, g_ple, g_final, *, tm=128):
    s, d = h2.shape
    tm = min(tm, s)

    def body(h2_ref, gate_ref, pp_ref, t_ref, gp_ref, gf_ref, dh3_ref, dz_ref, dpp_ref, dgf_ref, dgp_ref, loss_ref):
        i = pl.program_id(0)
        ppf = pp_ref[...]
        gate_v = gate_ref[...]
        r_p = lax.rsqrt(jnp.mean(ppf * ppf, axis=-1, keepdims=True) + EPS)
        eh = ppf * r_p
        e = eh * gp_ref[...]
        h3 = h2_ref[...] + gate_v * e
        r_f = lax.rsqrt(jnp.mean(h3 * h3, axis=-1, keepdims=True) + EPS)
        yh = h3 * r_f
        diff = yh * gf_ref[...] - t_ref[...]
        loss_part = 0.5 * jnp.sum(jnp.mean(diff * diff, axis=-1, keepdims=True), axis=0, keepdims=True)
        dy = diff / d
        dgf = jnp.sum(dy * yh, axis=0, keepdims=True)
        dyg = dy * gf_ref[...]
        dh3 = r_f * (dyg - yh * jnp.mean(dyg * yh, axis=-1, keepdims=True))
        dh3_ref[...] = dh3
        de = dh3 * gate_v
        dz_ref[...] = (dh3 * e * gate_v * (1.0 - gate_v)).astype(BF16)
        dgp = jnp.sum(de * eh, axis=0, keepdims=True)
        deg = de * gp_ref[...]
        dpp_ref[...] = (r_p * (deg - eh * jnp.mean(deg * eh, axis=-1, keepdims=True))).astype(BF16)
        loss_row = jnp.broadcast_to(loss_part, (1, 128))

        @pl.when(i == 0)
        def _():
            dgf_ref[...] = dgf
            dgp_ref[...] = dgp
            loss_ref[...] = loss_row

        @pl.when(i > 0)
        def _():
            dgf_ref[...] += dgf
            dgp_ref[...] += dgp
            loss_ref[...] += loss_row

    row = pl.BlockSpec((tm, d), lambda i: (i, 0))
    vec = pl.BlockSpec((1, d), lambda i: (0, 0))
    return pl.pallas_call(
        body,
        name="tail_fwd_bwd",
        grid=(s // tm,),
        in_specs=[row, row, row, row, vec, vec],
        out_specs=[row, row, row, vec, vec, pl.BlockSpec((1, 128), lambda i: (0, 0))],
        out_shape=[
            jax.ShapeDtypeStruct((s, d), F32),
            jax.ShapeDtypeStruct((s, d), BF16),
            jax.ShapeDtypeStruct((s, d), BF16),
            jax.ShapeDtypeStruct((1, d), F32),
            jax.ShapeDtypeStruct((1, d), F32),
            jax.ShapeDtypeStruct((1, 128), F32),
        ],
        compiler_params=_params(("arbitrary",)),
    )(h2, gate, pp, target, g_ple, g_final)


def _rope_tables(s):
    rows = s // GRID_W
    half = HEAD_DIM // 2
    inv_freq = ROPE_THETA ** (-jnp.arange(0, half, 2, dtype=F32) / half)
    ang_r = jnp.arange(rows, dtype=jnp.int32).astype(F32)[:, None] * inv_freq
    ang_c = jnp.arange(GRID_W, dtype=jnp.int32).astype(F32)[:, None] * inv_freq
    cr, sr = (jnp.repeat(t, GRID_W, axis=0) for t in (jnp.cos(ang_r), jnp.sin(ang_r)))
    cc, sc = (jnp.tile(t, (rows, 1)) for t in (jnp.cos(ang_c), jnp.sin(ang_c)))
    cos_t = jnp.concatenate([cr, cr, cc, cc], axis=-1)
    sin_t = jnp.concatenate([-sr, sr, -sc, sc], axis=-1)
    return cos_t, sin_t


def _swap_quarters(x):
    lane = lax.broadcasted_iota(jnp.int32, x.shape, x.ndim - 1)
    up = pltpu.roll(x, HEAD_DIM - 32, x.ndim - 1)
    down = pltpu.roll(x, 32, x.ndim - 1)
    return jnp.where((lane % 64) < 32, up, down)


def _cols(first, count=1):
    return slice(first * HEAD_DIM, (first + count) * HEAD_DIM)


def _qk_prep(proj, g_q, g_k, cos_t, sin_t, *, tm=256, comm=None):
    s, n = proj.shape
    tm = min(tm, s)

    def body(x_ref, gq_ref, gk_ref, c_ref, s_ref, o_ref):
        cos_v, sin_v = c_ref[...], s_ref[...]
        for h in range(COL_VA):
            x = x_ref[:, _cols(h)]
            g = gq_ref[...] if h < COL_KA else gk_ref[...]
            xn = x * lax.rsqrt(jnp.mean(x * x, axis=-1, keepdims=True) + EPS) * g
            xr = xn * cos_v + _swap_quarters(xn) * sin_v
            if h < COL_KA:
                xr = xr * Q_SCALE
            o_ref[:, _cols(h)] = xr.astype(BF16)
        o_ref[:, _cols(COL_VA, 2)] = x_ref[:, _cols(COL_VA, 2)].astype(BF16)
        o_ref[:, _cols(COL_QB, N_HEADS_B)] = (x_ref[:, _cols(COL_QB, N_HEADS_B)] * Q_SCALE).astype(BF16)
        o_ref[:, _cols(COL_KB, 4)] = x_ref[:, _cols(COL_KB, 4)].astype(BF16)

    row = pl.BlockSpec((tm, n), lambda i: (i, 0))
    tab = pl.BlockSpec((tm, HEAD_DIM), lambda i: (i, 0))
    vec = pl.BlockSpec((1, HEAD_DIM), lambda i: (0, 0))
    res, c_res = _call(
        body, name="qk_prep", grid=(s // tm,),
        in_specs=[row, vec, vec, tab, tab],
        out_specs=[row],
        out_shape=[jax.ShapeDtypeStruct((s, n), BF16)],
        sem=("parallel",), args=(proj, g_q, g_k, cos_t, sin_t), comm=comm)
    return res[0] if comm is None else (res[0], c_res)


def _qk_bwd(dqa, dka, dva, dqb, dkpad, dvpad, proj, g_q, g_k, cos_t, sin_t, *, comm=None):
    s, n = proj.shape
    tm = min(PAD_LO, s)
    assert PAD_LO % tm == 0
    lo = PAD_LO // tm

    def body(dqa_ref, dka_ref, dva_ref, dqb_ref, dkb_ref, dvb_ref, x_ref, gq_ref, gk_ref, c_ref, s_ref,
             o_ref, dgq_ref, dgk_ref):
        i = pl.program_id(0)
        cos_v, sin_v = c_ref[...], s_ref[...]

        def head(d, x, g):
            dn = d * cos_v + _swap_quarters(d * sin_v)
            r = lax.rsqrt(jnp.mean(x * x, axis=-1, keepdims=True) + EPS)
            xh = x * r
            dng = dn * g
            dx = r * (dng - xh * jnp.mean(dng * xh, axis=-1, keepdims=True))
            return dx.astype(BF16), jnp.sum(dn * xh, axis=0, keepdims=True)

        acc_q = jnp.zeros((1, HEAD_DIM), F32)
        acc_k = jnp.zeros((1, HEAD_DIM), F32)
        for h in range(N_HEADS_A):
            o_ref[:, _cols(h)], part = head(dqa_ref[:, _cols(h)] * ATT_SCALE, x_ref[:, _cols(h)], gq_ref[...])
            acc_q = acc_q + part
        for h in range(N_KV_A):
            o_ref[:, _cols(COL_KA + h)], part = head(dka_ref[:, _cols(h)] * LN2, x_ref[:, _cols(COL_KA + h)],
                                                     gk_ref[...])
            acc_k = acc_k + part
        o_ref[:, _cols(COL_VA, 2)] = dva_ref[...].astype(BF16)
        o_ref[:, _cols(COL_QB, N_HEADS_B)] = (dqb_ref[...] * ATT_SCALE).astype(BF16)
        o_ref[:, _cols(COL_KB, 2)] = (dkb_ref[...] * LN2).astype(BF16)
        o_ref[:, _cols(COL_VB, 2)] = dvb_ref[...].astype(BF16)

        @pl.when(i == 0)
        def _():
            dgq_ref[...] = acc_q
            dgk_ref[...] = acc_k

        @pl.when(i > 0)
        def _():
            dgq_ref[...] += acc_q
            dgk_ref[...] += acc_k

    def rows(width, shift=0):
        return pl.BlockSpec((tm, width), lambda i: (i + shift, 0))

    kv_w = N_KV_A * HEAD_DIM
    q_w = N_HEADS_A * HEAD_DIM
    vec = pl.BlockSpec((1, HEAD_DIM), lambda i: (0, 0))
    res, c_res = _call(
        body, name="qk_bwd", grid=(s // tm,),
        in_specs=[rows(q_w), rows(kv_w), rows(kv_w), rows(q_w), rows(kv_w, lo), rows(kv_w, lo), rows(n),
                  vec, vec, rows(HEAD_DIM), rows(HEAD_DIM)],
        out_specs=[rows(n), vec, vec],
        out_shape=[
            jax.ShapeDtypeStruct((s, n), BF16),
            jax.ShapeDtypeStruct((1, HEAD_DIM), F32),
            jax.ShapeDtypeStruct((1, HEAD_DIM), F32),
        ],
        sem=("arbitrary",), args=(dqa, dka, dva, dqb, dkpad, dvpad, proj, g_q, g_k, cos_t, sin_t), comm=comm)
    return res if comm is None else (res, c_res)


_NT = (((1,), (1,)), ((), ()))
_TN = (((0,), (0,)), ((), ()))


def _attn_a_fwd(pb, *, tq=2048, sub=256, comm=None, after=None):
    s = pb.shape[0]
    tq = min(tq, s)

    sub = min(sub, tq)

    def body(q_ref, k_ref, v_ref, o_ref, lse_ref):
        k = k_ref[...]
        v = v_ref[...]
        for r in range(tq // sub):
            rows = pl.ds(r * sub, sub)
            sc = lax.dot_general(q_ref[rows, :], k, _NT, preferred_element_type=F32)
            m = jnp.max(sc, axis=-1, keepdims=True)
            p = jnp.exp2(sc - m)
            l = jnp.sum(p, axis=-1, keepdims=True)
            o = jnp.dot(p.astype(BF16), v, preferred_element_type=F32)
            o_ref[rows, :] = (o / l).astype(BF16)
            lse_ref[rows, :] = jnp.broadcast_to(m + jnp.log2(l), (sub, HEAD_DIM))

    res, c_res = _call(
        body, name="attn_a_fwd", grid=(N_HEADS_A, s // tq),
        in_specs=[
            pl.BlockSpec((tq, HEAD_DIM), lambda h, i: (i, COL_QA + h)),
            pl.BlockSpec((s, HEAD_DIM), lambda h, i: (0, COL_KA + h // GROUP)),
            pl.BlockSpec((s, HEAD_DIM), lambda h, i: (0, COL_VA + h // GROUP)),
        ],
        out_specs=[
            pl.BlockSpec((tq, HEAD_DIM), lambda h, i: (i, h)),
            pl.BlockSpec((None, tq, HEAD_DIM), lambda h, i: (h, i, 0)),
        ],
        out_shape=[
            jax.ShapeDtypeStruct((s, (N_HEADS_A + N_HEADS_B) * HEAD_DIM), BF16),
            jax.ShapeDtypeStruct((N_HEADS_A, s, HEAD_DIM), F32),
        ],
        sem=("parallel", "parallel"), args=(pb, pb, pb), comm=comm, after=after)
    return res if comm is None else (res, c_res)


def _attn_a_bwd(pb, att, datt, lse, *, tq=2048, sub=256, comm=None):
    s = pb.shape[0]
    tq = min(tq, s)
    sub = min(sub, tq)

    def body(q_ref, k_ref, v_ref, o_ref, do_ref, lse_ref, dq_ref, dk_ref, dv_ref):
        first = jnp.logical_and(pl.program_id(1) == 0, pl.program_id(2) == 0)
        k = k_ref[...]
        v = v_ref[...]
        dk = dv = None
        for r in range(tq // sub):
            rows = pl.ds(r * sub, sub)
            q = q_ref[rows, :]
            do = do_ref[rows, :]
            sc = lax.dot_general(q, k, _NT, preferred_element_type=F32)
            p = jnp.exp2(sc - lse_ref[rows, :][:, :1])
            dp = lax.dot_general(do, v, _NT, preferred_element_type=F32)
            delta = jnp.sum(do.astype(F32) * o_ref[rows, :].astype(F32), axis=-1, keepdims=True)
            ds = (p * (dp - delta)).astype(BF16)
            dq_ref[rows, :] = jnp.dot(ds, k, preferred_element_type=F32)
            dk_r = lax.dot_general(ds, q, _TN, preferred_element_type=F32)
            dv_r = lax.dot_general(p.astype(BF16), do, _TN, preferred_element_type=F32)
            dk = dk_r if dk is None else dk + dk_r
            dv = dv_r if dv is None else dv + dv_r

        @pl.when(first)
        def _():
            dk_ref[...] = dk
            dv_ref[...] = dv

        @pl.when(jnp.logical_not(first))
        def _():
            dk_ref[...] += dk
            dv_ref[...] += dv

    qmap = lambda kv, g, i: (i, kv * GROUP + g)
    res, c_res = _call(
        body, name="attn_a_bwd", grid=(N_KV_A, GROUP, s // tq),
        in_specs=[
            pl.BlockSpec((tq, HEAD_DIM), lambda kv, g, i: (i, COL_QA + kv * GROUP + g)),
            pl.BlockSpec((s, HEAD_DIM), lambda kv, g, i: (0, COL_KA + kv)),
            pl.BlockSpec((s, HEAD_DIM), lambda kv, g, i: (0, COL_VA + kv)),
            pl.BlockSpec((tq, HEAD_DIM), qmap),
            pl.BlockSpec((tq, HEAD_DIM), qmap),
            pl.BlockSpec((None, tq, HEAD_DIM), lambda kv, g, i: (kv * GROUP + g, i, 0)),
        ],
        out_specs=[
            pl.BlockSpec((tq, HEAD_DIM), qmap),
            pl.BlockSpec((s, HEAD_DIM), lambda kv, g, i: (0, kv)),
            pl.BlockSpec((s, HEAD_DIM), lambda kv, g, i: (0, kv)),
        ],
        out_shape=[
            jax.ShapeDtypeStruct((s, N_HEADS_A * HEAD_DIM), F32),
            jax.ShapeDtypeStruct((s, N_KV_A * HEAD_DIM), F32),
            jax.ShapeDtypeStruct((s, N_KV_A * HEAD_DIM), F32),
        ],
        sem=("arbitrary", "arbitrary", "arbitrary"), args=(pb, pb, pb, att, datt, lse), comm=comm)
    return res if comm is None else (res, c_res)


def _t5_bucket(rel):
    nb = N_BUCKETS // 2
    ret = jnp.where(rel > 0, nb, 0)
    n = jnp.abs(rel)
    max_exact = nb // 2
    nf = jnp.maximum(n, 1).astype(F32)
    large = max_exact + (jnp.log(nf / max_exact) / math.log(MAX_DISTANCE / max_exact)
                         * (nb - max_exact)).astype(jnp.int32)
    large = jnp.minimum(large, nb - 1)
    return ret + jnp.where(n < max_exact, n, large)


def _band_buckets():
    r = jnp.arange(BLOCK_Q, dtype=jnp.int32)
    j = jnp.arange(3 * BLOCK_Q, dtype=jnp.int32)
    return _t5_bucket((j[None, :] - BLOCK_Q) - r[:, None])


def _band_bias(bucket, table_ref, h):
    acc = jnp.zeros(bucket.shape, F32)
    for b in range(N_BUCKETS):
        acc = jnp.where(bucket == b, table_ref[b, h], acc)
    return acc


GQ = GROUP * BLOCK_Q


def _stack_heads(x):
    return jnp.concatenate([x[:, _cols(g)] for g in range(GROUP)], axis=0)


def _unstack_heads(x):
    return jnp.concatenate([x[g * BLOCK_Q:(g + 1) * BLOCK_Q] for g in range(GROUP)], axis=1)


def _group_bias(bucket, table_ref, kv):
    return jnp.concatenate([_band_bias(bucket, table_ref, kv * GROUP + g) * LOG2E for g in range(GROUP)], axis=0)


def _group_sink(sink_ref, kv):
    head = lax.broadcasted_iota(jnp.int32, (GQ, 1), 0) // BLOCK_Q
    snk = jnp.zeros((GQ, 1), F32)
    for g in range(GROUP):
        snk = jnp.where(head == g, sink_ref[0, kv * GROUP + g] * LOG2E, snk)
    return snk


def _band_mask(n, s):
    r = lax.broadcasted_iota(jnp.int32, (GQ, 3 * BLOCK_Q), 0) % BLOCK_Q
    j = lax.broadcasted_iota(jnp.int32, (GQ, 3 * BLOCK_Q), 1)
    rel = j - BLOCK_Q - r
    kabs = n * BLOCK_Q + j - BLOCK_Q
    return (jnp.abs(rel) <= WINDOW) & (kabs >= 0) & (kabs < s)


def _band_start(n):
    return pl.multiple_of(n * BLOCK_Q + (PAD_LO - BLOCK_Q), BLOCK_Q)


def _attn_b_fwd(pb, kpad, vpad, bucket, table, sink, att, *, comm=None, after=None):
    s = pb.shape[0]
    nblk = s // BLOCK_Q
    sp = kpad.shape[0]

    def body(table_ref, sink_ref, q0_ref, q1_ref, k_ref, v_ref, bucket_ref, _, o_ref, lse_ref, bias_ref):
        n = pl.program_id(0)

        @pl.when(n == 0)
        def _():
            for kv in range(N_KV_B):
                bias_ref[kv * GQ:(kv + 1) * GQ, :] = _group_bias(bucket_ref[...], table_ref, kv)

        band = pl.ds(_band_start(n), 3 * BLOCK_Q)
        mask = _band_mask(n, s)
        for kv, q_ref in enumerate((q0_ref, q1_ref)):
            kb = k_ref[band, _cols(kv)]
            vb = v_ref[band, _cols(kv)]
            sc = lax.dot_general(_stack_heads(q_ref[...]), kb, _NT, preferred_element_type=F32)
            sc = jnp.where(mask, sc + bias_ref[kv * GQ:(kv + 1) * GQ, :], NEG_INF)
            snk = _group_sink(sink_ref, kv)
            m = jnp.maximum(jnp.max(sc, axis=-1, keepdims=True), snk)
            p = jnp.exp2(sc - m)
            l = jnp.sum(p, axis=-1, keepdims=True) + jnp.exp2(snk - m)
            o = jnp.dot(p.astype(BF16), vb, preferred_element_type=F32)
            o_ref[:, _cols(kv * GROUP, GROUP)] = _unstack_heads((o / l).astype(BF16))
            lse = m + jnp.log2(l)
            for g in range(GROUP):
                lse_ref[kv * GROUP + g] = jnp.broadcast_to(lse[g * BLOCK_Q:(g + 1) * BLOCK_Q], (BLOCK_Q, HEAD_DIM))

    smem = pl.BlockSpec(memory_space=pltpu.SMEM)
    wide = GROUP * HEAD_DIM
    whole = pl.BlockSpec((sp, N_KV_B * HEAD_DIM), lambda n: (0, 0))
    res, c_res = _call(
        body, name="attn_b_fwd", grid=(nblk,),
        in_specs=[
            smem,
            smem,
            pl.BlockSpec((BLOCK_Q, wide), lambda n: (n, COL_QB // GROUP)),
            pl.BlockSpec((BLOCK_Q, wide), lambda n: (n, COL_QB // GROUP + 1)),
            whole,
            whole,
            pl.BlockSpec((BLOCK_Q, 3 * BLOCK_Q), lambda n: (0, 0)),
            _ANY,
        ],
        out_specs=[
            pl.BlockSpec((BLOCK_Q, N_HEADS_B * HEAD_DIM), lambda n: (n, 1)),
            pl.BlockSpec((N_HEADS_B, BLOCK_Q, HEAD_DIM), lambda n: (0, n, 0)),
        ],
        out_shape=[
            jax.ShapeDtypeStruct(att.shape, BF16),
            jax.ShapeDtypeStruct((N_HEADS_B, s, HEAD_DIM), F32),
        ],
        scratch_shapes=[pltpu.VMEM((N_KV_B * GQ, 3 * BLOCK_Q), F32)],
        sem=("arbitrary",), args=(table, sink, pb, pb, kpad, vpad, bucket, att), comm=comm, after=after,
        aliases={7: 0})
    return res if comm is None else (res, c_res)


def _attn_b_bwd(pb, kpad, vpad, att, datt, lse, bucket, table, sink, *, comm=None, after=None):
    s = pb.shape[0]
    nblk = s // BLOCK_Q
    sp = kpad.shape[0]

    def body(table_ref, sink_ref, q0_ref, q1_ref, k_ref, v_ref, o_ref, do_ref, lse_ref, bucket_ref,
             dq_ref, dk_ref, dv_ref, dtab_ref, dsink_ref, bias_ref, dbias_ref):
        n = pl.program_id(0)

        @pl.when(n == 0)
        def _():
            dk_ref[...] = jnp.zeros_like(dk_ref)
            dv_ref[...] = jnp.zeros_like(dv_ref)
            dbias_ref[...] = jnp.zeros_like(dbias_ref)
            dsink_ref[...] = jnp.zeros_like(dsink_ref)
            for kv in range(N_KV_B):
                bias_ref[kv * GQ:(kv + 1) * GQ, :] = _group_bias(bucket_ref[...], table_ref, kv)

        band = pl.ds(_band_start(n), 3 * BLOCK_Q)
        mask = _band_mask(n, s)
        for kv, q_ref in enumerate((q0_ref, q1_ref)):
            wide_cols = _cols(kv * GROUP, GROUP)
            q = _stack_heads(q_ref[...])
            do = _stack_heads(do_ref[:, wide_cols])
            o = _stack_heads(o_ref[:, wide_cols])
            kb = k_ref[band, _cols(kv)]
            vb = v_ref[band, _cols(kv)]
            lse = jnp.concatenate([lse_ref[kv * GROUP + g][:, :1] for g in range(GROUP)], axis=0)
            sc = lax.dot_general(q, kb, _NT, preferred_element_type=F32)
            sc = jnp.where(mask, sc + bias_ref[kv * GQ:(kv + 1) * GQ, :], NEG_INF)
            p = jnp.exp2(sc - lse)
            dp = lax.dot_general(do, vb, _NT, preferred_element_type=F32)
            delta = jnp.sum(do.astype(F32) * o.astype(F32), axis=-1, keepdims=True)
            ds = p * (dp - delta)
            dsb = ds.astype(BF16)
            dq_ref[:, wide_cols] = _unstack_heads(jnp.dot(dsb, kb, preferred_element_type=F32))
            dk_ref[band, _cols(kv)] += lax.dot_general(dsb, q, _TN, preferred_element_type=F32)
            dv_ref[band, _cols(kv)] += lax.dot_general(p.astype(BF16), do, _TN, preferred_element_type=F32)
            dbias_ref[kv * GQ:(kv + 1) * GQ, :] += ds
            sink_part = -jnp.exp2(_group_sink(sink_ref, kv) - lse) * delta
            for g in range(GROUP):
                rows = slice(g * BLOCK_Q, (g + 1) * BLOCK_Q)
                dsink_ref[kv * GROUP + g] += jnp.broadcast_to(
                    jnp.sum(sink_part[rows], axis=0, keepdims=True), (1, HEAD_DIM))

        @pl.when(n == nblk - 1)
        def _():
            bucket_v = bucket_ref[...]
            row = lax.broadcasted_iota(jnp.int32, (N_BUCKETS, HEAD_DIM), 0)
            for h in range(N_HEADS_B):
                acc = dbias_ref[h * BLOCK_Q:(h + 1) * BLOCK_Q, :]
                tot = jnp.zeros((N_BUCKETS, HEAD_DIM), F32)
                for b in range(N_BUCKETS):
                    tot = jnp.where(row == b, jnp.sum(jnp.where(bucket_v == b, acc, 0.0), keepdims=True), tot)
                dtab_ref[h] = tot

    smem = pl.BlockSpec(memory_space=pltpu.SMEM)
    wide = GROUP * HEAD_DIM
    whole = pl.BlockSpec((sp, N_KV_B * HEAD_DIM), lambda n: (0, 0))
    group_b = pl.BlockSpec((BLOCK_Q, N_HEADS_B * HEAD_DIM), lambda n: (n, 1))
    res, c_res = _call(
        body, name="attn_b_bwd", grid=(nblk,),
        in_specs=[
            smem,
            smem,
            pl.BlockSpec((BLOCK_Q, wide), lambda n: (n, COL_QB // GROUP)),
            pl.BlockSpec((BLOCK_Q, wide), lambda n: (n, COL_QB // GROUP + 1)),
            whole,
            whole,
            group_b,
            group_b,
            pl.BlockSpec((N_HEADS_B, BLOCK_Q, HEAD_DIM), lambda n: (0, n, 0)),
            pl.BlockSpec((BLOCK_Q, 3 * BLOCK_Q), lambda n: (0, 0)),
        ],
        out_specs=[
            pl.BlockSpec((BLOCK_Q, N_HEADS_B * HEAD_DIM), lambda n: (n, 0)),
            whole,
            whole,
            pl.BlockSpec((N_HEADS_B, N_BUCKETS, HEAD_DIM), lambda n: (0, 0, 0)),
            pl.BlockSpec((N_HEADS_B, 1, HEAD_DIM), lambda n: (0, 0, 0)),
        ],
        out_shape=[
            jax.ShapeDtypeStruct((s, N_HEADS_B * HEAD_DIM), F32),
            jax.ShapeDtypeStruct((sp, N_KV_B * HEAD_DIM), F32),
            jax.ShapeDtypeStruct((sp, N_KV_B * HEAD_DIM), F32),
            jax.ShapeDtypeStruct((N_HEADS_B, N_BUCKETS, HEAD_DIM), F32),
            jax.ShapeDtypeStruct((N_HEADS_B, 1, HEAD_DIM), F32),
        ],
        scratch_shapes=[pltpu.VMEM((N_KV_B * GQ, 3 * BLOCK_Q), F32), pltpu.VMEM((N_KV_B * GQ, 3 * BLOCK_Q), F32)],
        sem=("arbitrary",),
        args=(table, sink, pb, pb, kpad, vpad, att, datt, lse, bucket), comm=comm, after=after)
    return res if comm is None else (res, c_res)


_MESH = pl.DeviceIdType.MESH


def _other_chips(x, y):
    return [(x, 1 - y), (1 - x, y), (1 - x, 1 - y)]


_HBM = pl.BlockSpec(memory_space=pltpu.HBM)
_SEM = pl.BlockSpec(memory_space=pltpu.SEMAPHORE)
_SPLIT = pltpu.CompilerParams(has_side_effects=pltpu.SideEffectType.DATAFLOW_SIDE_EFFECTING)


def _in_hbm(a):
    return pltpu.with_memory_space_constraint(a, pltpu.HBM)


def _my_half(rows):
    c = lax.axis_index("c")
    half = rows // 2
    return pl.ds(pl.multiple_of(c * half, half), half), pl.ds(pl.multiple_of((1 - c) * half, half), half)


def _gather_route(shapes):
    def route(src, land):
        x, y, c = lax.axis_index("x"), lax.axis_index("y"), lax.axis_index("c")
        out = []
        for t, shape in enumerate(shapes):
            mine, _ = _my_half(shape[0])
            for px, py in _other_chips(x, y):
                out.append((src[t].at[mine], land[t].at[2 * x + y, mine], land[t].at[2 * px + py, mine], (px, py, c)))
        return out

    return route


def _exchange_route(n_t):
    def route(src, land):
        x, y, c = lax.axis_index("x"), lax.axis_index("y"), lax.axis_index("c")
        out = []
        for t in range(n_t):
            for px, py in _other_chips(x, y):
                k = 2 * px + py
                out.append((src[t].at[k], land[t].at[2 * (2 * x + y) + c], land[t].at[2 * k + c], (px, py, c)))
        return out

    return route


def _own_slot(shape, dtype, slot, block):
    return lax.dynamic_update_slice(lax.empty(shape, dtype), block[None], (slot,) + (0,) * (len(shape) - 1))


def _split_start(name, srcs, lands, route, after):
    n = len(srcs)

    def body(*refs):
        src, land, send_sems, recv_sems, token = refs[:n], refs[n:2 * n], refs[2 * n + 1], refs[2 * n + 2], refs[-1]
        for i, (src_ref, dst_ref, _, to) in enumerate(route(src, land)):
            pltpu.make_async_remote_copy(src_ref=src_ref, dst_ref=dst_ref, send_sem=send_sems.at[i],
                                         recv_sem=recv_sems.at[i], device_id=to, device_id_type=_MESH).start()
        token[...] = jnp.zeros_like(token)

    sem = pltpu.SemaphoreType.DMA((3 * n,))
    lands = list(lands)
    res = pl.pallas_call(
        body, name=name,
        in_specs=[_HBM] * (2 * n) + [_ANY],
        out_specs=[_SEM, _SEM] + [_HBM] * (2 * n) + [pl.BlockSpec(memory_space=pltpu.VMEM)],
        out_shape=[sem, sem] + [pltpu.HBM(a.shape, a.dtype) for a in list(srcs) + lands]
        + [jax.ShapeDtypeStruct((8, 128), F32)],
        input_output_aliases={i: 2 + i for i in range(2 * n)},
        compiler_params=_SPLIT,
    )(*[_in_hbm(a) for a in srcs], *[_in_hbm(a) for a in lands], after)
    return (res[0], res[1]), res[2:2 + n], res[2 + n:2 + 2 * n], res[-1]


def _split_wait(name, srcs, lands, sems, route, after):
    n = len(srcs)

    def body(*refs):
        src, land, send_sems, recv_sems = refs[:n], refs[n:2 * n], refs[2 * n], refs[2 * n + 1]
        for i, (src_ref, _, dst_ref, to) in enumerate(route(src, land)):
            cp = pltpu.make_async_remote_copy(src_ref=src_ref, dst_ref=dst_ref, send_sem=send_sems.at[i],
                                              recv_sem=recv_sems.at[i], device_id=to, device_id_type=_MESH)
            cp.wait_send()
            cp.wait_recv()

    res = pl.pallas_call(
        body, name=name,
        in_specs=[_HBM] * (2 * n) + [_SEM, _SEM, _ANY],
        out_specs=[_HBM] * (2 * n),
        out_shape=[pltpu.HBM(a.shape, a.dtype) for a in list(srcs) + list(lands)],
        input_output_aliases={i: i for i in range(2 * n)},
        compiler_params=_SPLIT,
    )(*srcs, *lands, sems[0], sems[1], after)
    return res[:n], res[n:]


def _comm_only(name, comm):
    return _call(lambda: None, name=name, grid=(1,), in_specs=[], out_specs=[], out_shape=[], args=(), comm=comm)[1]


def _swap_comm(shards, lands):
    n_t = len(lands)

    def copies(land, sems, later):
        send_sems, recv_sems = sems
        x, y = lax.axis_index("x"), lax.axis_index("y")
        sends, recvs = [], []
        for t in range(n_t):
            mine, other = _my_half(shards[t].shape[0])
            for j, (px, py) in enumerate(_other_chips(x, y)):
                k = 2 * px + py
                for part, out in ((mine, sends), (other, recvs)) if later else ((mine, sends),):
                    out.append(pltpu.make_async_remote_copy(
                        src_ref=land[t].at[k, part], dst_ref=land[t].at[k, part], send_sem=send_sems.at[3 * t + j],
                        recv_sem=recv_sems.at[3 * t + j], device_id=_sibling(), device_id_type=_MESH))
        return sends, recvs

    def start(ins, land, sems):
        for cp in copies(land, sems, False)[0]:
            cp.start()

    def finish(ins, land, sems):
        sends, recvs = copies(land, sems, True)
        for cp in recvs:
            cp.wait_recv()
        for cp in sends:
            cp.wait_send()

    return _Comm(
        lands, [jax.ShapeDtypeStruct(a.shape, a.dtype) for a in lands],
        [pltpu.SemaphoreType.DMA((3 * n_t,)), pltpu.SemaphoreType.DMA((3 * n_t,))],
        start, finish, aliases={t: t for t in range(n_t)})


def _forward_comm(partials, lands):
    n_t = len(lands)

    def copies(land, sems, later):
        send_sems, recv_sems = sems
        x, y, c = lax.axis_index("x"), lax.axis_index("y"), lax.axis_index("c")
        sends, recvs = [], []
        for t in range(n_t):
            for j, k in enumerate([2 * x + y] + [2 * px + py for px, py in _other_chips(x, y)]):
                for slot, out in ((2 * k + c, sends), (2 * k + 1 - c, recvs)) if later else ((2 * k + c, sends),):
                    out.append(pltpu.make_async_remote_copy(
                        src_ref=land[t].at[slot], dst_ref=land[t].at[slot], send_sem=send_sems.at[4 * t + j],
                        recv_sem=recv_sems.at[4 * t + j], device_id=_sibling(), device_id_type=_MESH))
        return sends, recvs

    def start(ins, land, sems):
        for cp in copies(land, sems, False)[0]:
            cp.start()

    def finish(ins, land, sems):
        sends, recvs = copies(land, sems, True)
        for cp in recvs:
            cp.wait_recv()
        for cp in sends:
            cp.wait_send()

    return _Comm(
        lands, [jax.ShapeDtypeStruct(a.shape, a.dtype) for a in lands],
        [pltpu.SemaphoreType.DMA((4 * n_t,)), pltpu.SemaphoreType.DMA((4 * n_t,))],
        start, finish, aliases={t: t for t in range(n_t)})


def _allreduce_small(pack):
    rows, d = pack.shape

    def body(p_ref, sum_ref, all_ref, send_sems, recv_sems):
        x, y, c = lax.axis_index("x"), lax.axis_index("y"), lax.axis_index("c")
        me = 4 * x + 2 * y + c
        all_ref[me] = p_ref[...]
        peers = []
        for dx in range(2):
            for dy in range(2):
                for dc in range(2):
                    if dx or dy or dc:
                        px = 1 - x if dx else x
                        py = 1 - y if dy else y
                        pc = 1 - c if dc else c
                        peers.append((4 * dx + 2 * dy + dc - 1, (px, py, pc)))
        sends = []
        for k, to in peers:
            cp = pltpu.make_async_remote_copy(
                src_ref=p_ref, dst_ref=all_ref.at[me], send_sem=send_sems.at[k], recv_sem=recv_sems.at[k],
                device_id=to, device_id_type=_MESH)
            cp.start()
            sends.append(cp)
        for k, (px, py, pc) in peers:
            pltpu.make_async_remote_copy(
                src_ref=p_ref, dst_ref=all_ref.at[4 * px + 2 * py + pc], send_sem=send_sems.at[k],
                recv_sem=recv_sems.at[k], device_id=(px, py, pc), device_id_type=_MESH).wait_recv()
        for cp in sends:
            cp.wait_send()
        tot = all_ref[0]
        for i in range(1, N_DEV):
            tot = tot + all_ref[i]
        sum_ref[...] = tot

    vm = pl.BlockSpec(memory_space=pltpu.VMEM)
    return pl.pallas_call(
        body,
        name="allreduce_small",
        in_specs=[vm],
        out_specs=vm,
        out_shape=jax.ShapeDtypeStruct((rows, d), F32),
        scratch_shapes=[
            pltpu.VMEM((N_DEV, rows, d), F32),
            pltpu.SemaphoreType.DMA((N_DEV - 1,)),
            pltpu.SemaphoreType.DMA((N_DEV - 1,)),
        ],
    )(pack)


def _adamw_math(w, g, m, v):
    m = ADAM_B1 * m + (1.0 - ADAM_B1) * g
    v = ADAM_B2 * v + (1.0 - ADAM_B2) * (g * g)
    m_hat = m / (1.0 - ADAM_B1 ** ADAM_STEP)
    v_hat = v / (1.0 - ADAM_B2 ** ADAM_STEP)
    delta = -ADAM_LR * (m_hat / (jnp.sqrt(v_hat) + ADAM_EPS) + ADAM_WD * w)
    return delta, m, v


def _sum_adamw(parts, w, m, v, *, name, tr=256):
    r, c = w.shape
    tr = min(tr, r)
    tc = min(c, 1024)

    def body(p_ref, w_ref, m_ref, v_ref, g_ref, d_ref, m2_ref, v2_ref):
        g = p_ref[0].astype(F32)
        for i in range(1, N_DEV):
            g = g + p_ref[i].astype(F32)
        delta, m2, v2 = _adamw_math(w_ref[...], g, m_ref[...], v_ref[...])
        g_ref[...] = g
        d_ref[...] = delta
        m2_ref[...] = m2
        v2_ref[...] = v2

    blk = pl.BlockSpec((tr, tc), lambda i, j: (i, j))
    return pl.pallas_call(
        body,
        name=name,
        grid=(r // tr, c // tc),
        in_specs=[pl.BlockSpec((N_DEV, tr, tc), lambda i, j: (0, i, j)), blk, blk, blk],
        out_specs=[blk] * 4,
        out_shape=[jax.ShapeDtypeStruct((r, c), F32)] * 4,
        compiler_params=_params(("parallel", "parallel")),
    )(parts, w, m, v)


def _adamw_small(g, w, m, v):
    def body(g_ref, w_ref, m_ref, v_ref, d_ref, m2_ref, v2_ref):
        delta, m2, v2 = _adamw_math(w_ref[...], g_ref[...], m_ref[...], v_ref[...])
        d_ref[...] = delta
        m2_ref[...] = m2
        v2_ref[...] = v2

    vm = pl.BlockSpec(memory_space=pltpu.VMEM)
    return pl.pallas_call(
        body,
        name="adamw_small",
        in_specs=[vm] * 4,
        out_specs=[vm] * 3,
        out_shape=[jax.ShapeDtypeStruct(g.shape, F32)] * 3,
    )(g, w, m, v)


def _relu2_epilogue(acc):
    ra = jnp.maximum(acc, 0.0)
    return ra * ra, ra


def _rows(stacked):
    return stacked.reshape(stacked.shape[0] * stacked.shape[1], stacked.shape[2])


def _by_chip(mat):
    return mat.reshape(N_CHIPS, mat.shape[0] // N_CHIPS, mat.shape[1])


def _local_step(x, p, target, shards, small, update):
    s, d = x.shape
    cos_t, sin_t = _rope_tables(s)
    bucket = _band_buckets()
    p_bf = p.astype(BF16)
    wts = {}

    chip = 2 * lax.axis_index("x") + lax.axis_index("y")
    core = lax.axis_index("c")

    def gather(tag, names, after):
        srcs = [shards[n] for n in names]
        route = _gather_route([a.shape for a in srcs])
        sems, srcs, lands, token = _split_start(f"gather_start_{tag}", srcs, [zones[n] for n in names], route, after)

        def landed(done):
            got_srcs, got_lands = _split_wait(f"gather_wait_{tag}", srcs, lands, sems, route, done)
            comm = _swap_comm(got_srcs, got_lands)
            comm.waited = got_srcs[0]
            return comm

        return landed, token

    zones = {n: _own_slot((N_CHIPS,) + a.shape, a.dtype, chip, a) for n, a in shards.items()}
    in_landed, token = gather("in", ["w_in"], small["attn_norm_g"])
    g_attn = small["attn_norm_g"] + token[:1, :1]
    u = _rms_fwd(x, g_attn, name="norm_attn")
    prepared = u[:1, :1].astype(F32) + sum(lax.dynamic_slice(z, (chip, 0, 0), (1, 1, 1))[0].astype(F32)
                                           for n, z in zones.items() if n != "w_in")
    (wts["w_in"],) = _comm_only("swap_w_in", in_landed(prepared))
    mid_landed, token = gather("mid", ["w_out"], wts["w_in"])
    proj = _matmul(u, wts["w_in"], mode="nn", out_dtypes=[F32], name="mm_in", bn=768, after=token)
    pb, (w_out_s,) = _qk_prep(proj, small["q_norm_g"], small["k_norm_g"], cos_t, sin_t, comm=mid_landed(proj))
    wts["w_out"] = _rows(w_out_s)
    up_landed, token = gather("up", ["w_up"], pb)
    att_a, lse_a = _attn_a_fwd(pb, after=token)
    pad = ((PAD_LO, PAD_HI), (0, 0))
    kpad = jnp.pad(pb[:, COL_KB * HEAD_DIM:COL_VB * HEAD_DIM], pad)
    vpad = jnp.pad(pb[:, COL_VB * HEAD_DIM:], pad)
    up_swap = up_landed(att_a)
    down_landed, token = gather("down", ["w_down"], up_swap.waited)
    (att, lse_b), (wts["w_up"],) = _attn_b_fwd(pb, kpad, vpad, bucket, small["rel_bias_table"],
                                               small["sink_logits"], att_a, comm=up_swap, after=token)
    h1 = _matmul(att, wts["w_out"], mode="nn", out_dtypes=[F32], name="mm_out",
                 epilogue=lambda acc, res: (acc + res,), extras=(x,))
    mn = _rms_fwd(h1, small["mlp_norm_g"], name="norm_mlp")
    r, ra = _matmul(mn, wts["w_up"], mode="nn", out_dtypes=[BF16, BF16], name="mm_up", epilogue=_relu2_epilogue,
                    bm=2048)
    (w_down_s,) = _comm_only("swap_w_down", down_landed(r))
    wts["w_down"] = _rows(w_down_s)
    late_landed, token = gather("late", ["w_gate", "ple_w"], w_down_s)
    h2 = _matmul(r, wts["w_down"], mode="nn", out_dtypes=[F32], name="mm_down",
                 epilogue=lambda acc, res: (acc + res,), extras=(h1,), after=token)
    ng, (w_gate_s, wts["ple_w"]) = _rms_fwd(h2, small["gate_norm_g"], name="norm_gate", comm=late_landed(h2))
    wts["w_gate"] = _rows(w_gate_s)
    gate = _matmul(ng, wts["w_gate"], mode="nn", out_dtypes=[F32], name="mm_gate",
                   epilogue=lambda acc: (1.0 / (1.0 + jnp.exp(-acc)),))
    pp = _matmul(p_bf, wts["ple_w"], mode="nn", out_dtypes=[F32], name="mm_ple", bn=512)
    dh3, dz, dpp, dg_final, dg_ple, loss = _tail(h2, gate, pp, target, small["ple_norm_g"], small["final_norm_g"])

    dng = _matmul(dz, wts["w_gate"], mode="nt", out_dtypes=[F32], name="mm_gate_dx")
    gw_gate = _matmul(ng, dz, mode="tn", out_dtypes=[BF16], name="mm_gate_dw")
    gw_ple = _matmul(p_bf, dpp, mode="tn", out_dtypes=[BF16], name="mm_ple_dw", bn=512, out_stack=N_CHIPS)
    dh2, dh2_bf, dg_gate = _rms_bwd(h2, dng, small["gate_norm_g"], dh3, name="norm_gate_bwd", want_bf16=True)

    def exchange(tag, partials, after):
        route = _exchange_route(len(partials))
        lands = [_own_slot((N_DEV,) + g.shape[1:], g.dtype, 2 * chip + core,
                           lax.dynamic_index_in_dim(g, chip, 0, keepdims=False)) for g in partials]
        sems, srcs, lands, token = _split_start(f"exchange_start_{tag}", partials, lands, route, after)
        return lambda done: _forward_comm(*_split_wait(f"exchange_wait_{tag}", srcs, lands, sems, route, done)), token

    big = {}
    gate_landed, token = exchange("gate", [_by_chip(gw_gate), gw_ple], dh2_bf)
    gw_down = _matmul(r, dh2_bf, mode="tn", out_dtypes=[BF16], name="mm_down_dw", after=token)
    da, (parts_gate, parts_ple) = _matmul(
        dh2_bf, wts["w_down"], mode="nt", out_dtypes=[BF16], name="mm_down_dx", bm=2048,
        epilogue=lambda acc, ra_v: (acc * (2.0 * ra_v.astype(F32)),), extras=(ra,), comm=gate_landed(gw_down))
    down_landed, token = exchange("down", [_by_chip(gw_down)], da)
    big["w_gate"], big["ple_w"] = update("w_gate", parts_gate), update("ple_w", parts_ple)
    gw_up = _matmul(mn, da, mode="tn", out_dtypes=[BF16], name="mm_up_dw", out_stack=N_CHIPS, after=token)
    dmn = _matmul(da, wts["w_up"], mode="nt", out_dtypes=[F32], name="mm_up_dx")
    dh1, dh1_bf, dg_mlp = _rms_bwd(h1, dmn, small["mlp_norm_g"], dh2, name="norm_mlp_bwd", want_bf16=True)
    datt, (parts_down,) = _matmul(dh1_bf, wts["w_out"], mode="nt", out_dtypes=[BF16], name="mm_out_dx",
                                  comm=down_landed(dh1_bf))
    gw_out = _matmul(att, dh1_bf, mode="tn", out_dtypes=[BF16], name="mm_out_dw")
    up_landed, token = exchange("up", [gw_up], datt)
    dqb, dkpad, dvpad, dtab, dsink = _attn_b_bwd(pb, kpad, vpad, att, datt, lse_b, bucket,
                                                 small["rel_bias_table"], small["sink_logits"], after=token)
    dqa, dka, dva = _attn_a_bwd(pb, att, datt, lse_a)
    (dproj, dg_q, dg_k), (parts_up,) = _qk_bwd(dqa, dka, dva, dqb, dkpad, dvpad, proj,
                                               small["q_norm_g"], small["k_norm_g"], cos_t, sin_t,
                                               comm=up_landed(dqa))
    gw_in = _matmul(u, dproj, mode="tn", out_dtypes=[BF16], name="mm_in_dw", bn=768, out_stack=N_CHIPS)
    in_landed, token = exchange("in", [gw_in, _by_chip(gw_out)], dproj)
    du = _matmul(dproj, wts["w_in"], mode="nt", out_dtypes=[F32], name="mm_in_dx", bk=768, after=token)
    grad_x, dg_attn = _rms_bwd(x, du, small["attn_norm_g"], dh1, name="norm_attn_bwd", want_bf16=False)
    for n, parts in (("w_down", parts_down), ("w_up", parts_up)):
        big[n] = update(n, parts)
    done = dg_attn + sum(big[n][0][0, :1, :] for n in ("w_down", "w_up"))
    parts_in, parts_out = _comm_only("forward_w_in", in_landed(done))
    big["w_in"], big["w_out"] = update("w_in", parts_in), update("w_out", parts_out)

    small_g = {
        "attn_norm_g": dg_attn, "mlp_norm_g": dg_mlp, "ple_norm_g": dg_ple, "gate_norm_g": dg_gate,
        "final_norm_g": dg_final, "q_norm_g": dg_q, "k_norm_g": dg_k,
        "sink_logits": dsink[:, 0, 0][None, :], "rel_bias_table": dtab[:, :, 0].T,
    }
    return loss, grad_x, big, small_g


_SMALL_ROWS = ["attn_norm_g", "mlp_norm_g", "ple_norm_g", "gate_norm_g", "final_norm_g"]
_PACK_ROWS = 8


def _pack_small(vals, d):
    rows = [vals[n].reshape(1, d) for n in _SMALL_ROWS]
    misc = jnp.concatenate([
        vals["q_norm_g"].reshape(1, HEAD_DIM), vals["k_norm_g"].reshape(1, HEAD_DIM),
        jnp.pad(vals["sink_logits"].reshape(1, N_HEADS_B), ((0, 0), (0, HEAD_DIM - N_HEADS_B))),
        vals["rel_bias_table"].reshape(1, N_BUCKETS * N_HEADS_B)], axis=1)
    rows.append(jnp.pad(misc, ((0, 0), (0, d - misc.shape[1]))))
    rows.append(jnp.zeros((_PACK_ROWS - len(rows), d), F32))
    return jnp.concatenate(rows, axis=0).astype(F32)


def _unpack_small(pack, shapes):
    out = {n: pack[i].reshape(shapes[n]) for i, n in enumerate(_SMALL_ROWS)}
    misc = pack[len(_SMALL_ROWS)]
    out["q_norm_g"] = misc[:HEAD_DIM].reshape(shapes["q_norm_g"])
    out["k_norm_g"] = misc[HEAD_DIM:2 * HEAD_DIM].reshape(shapes["k_norm_g"])
    out["sink_logits"] = misc[2 * HEAD_DIM:2 * HEAD_DIM + N_HEADS_B].reshape(shapes["sink_logits"])
    out["rel_bias_table"] = misc[3 * HEAD_DIM:3 * HEAD_DIM + N_BUCKETS * N_HEADS_B].reshape(shapes["rel_bias_table"])
    return out


_WEIGHTS = ["attn_norm_g", "w_in", "q_norm_g", "k_norm_g", "sink_logits", "w_out", "mlp_norm_g", "w_up", "w_down",
            "ple_w", "ple_norm_g", "gate_norm_g", "w_gate", "rel_bias_table", "final_norm_g"]
_BIG = ["w_in", "w_out", "w_up", "w_down", "ple_w", "w_gate"]


def kernel(x, p, attn_norm_g, w_in, q_norm_g, k_norm_g, sink_logits, w_out, mlp_norm_g, w_up, w_down, ple_w, ple_norm_g, gate_norm_g, w_gate, rel_bias_table, final_norm_g, loss_target, m_attn_norm_g, m_w_in, m_q_norm_g, m_k_norm_g, m_sink_logits, m_w_out, m_mlp_norm_g, m_w_up, m_w_down, m_ple_w, m_ple_norm_g, m_gate_norm_g, m_w_gate, m_rel_bias_table, m_final_norm_g, v_attn_norm_g, v_w_in, v_q_norm_g, v_k_norm_g, v_sink_logits, v_w_out, v_mlp_norm_g, v_w_up, v_w_down, v_ple_w, v_ple_norm_g, v_gate_norm_g, v_w_gate, v_rel_bias_table, v_final_norm_g):
    given = dict(locals())
    w = {n: given[n] for n in _WEIGHTS}
    m = {n: given["m_" + n] for n in _WEIGHTS}
    v = {n: given["v_" + n] for n in _WEIGHTS}
    d = x.shape[-1]

    shards = {n: w[n][0].astype(BF16) for n in _BIG}
    small = {
        "attn_norm_g": w["attn_norm_g"], "mlp_norm_g": w["mlp_norm_g"], "ple_norm_g": w["ple_norm_g"],
        "gate_norm_g": w["gate_norm_g"], "final_norm_g": w["final_norm_g"].reshape(1, d),
        "q_norm_g": w["q_norm_g"], "k_norm_g": w["k_norm_g"], "sink_logits": w["sink_logits"],
        "rel_bias_table": w["rel_bias_table"],
    }

    def update(n, parts):
        res = _sum_adamw(parts, w[n][0], m[n][0], v[n][0], name="adamw_" + n)
        return [t.reshape(w[n].shape) for t in res]

    loss_part, grad_x, big, small_g = _local_step(x[0], p[0, 0], loss_target[0], shards, small, update)
    loss = lax.psum(loss_part[0, 0], ("x", "y", "c"))
    grads, deltas, new_m, new_v = [{n: big[n][i] for n in _BIG} for i in range(4)]

    shapes = {n: w[n].shape for n in _WEIGHTS if n not in _BIG}
    pack = _pack_small(small_g, d)
    pack = pack.at[_PACK_ROWS - 1, :1].add(0.0 * grads["w_in"][0, 0, :1])
    g_small = _allreduce_small(pack)
    d_small, m_small, v_small = _adamw_small(g_small, _pack_small(w, d), _pack_small(m, d), _pack_small(v, d))
    grads.update(_unpack_small(g_small, shapes))
    deltas.update(_unpack_small(d_small, shapes))
    new_m.update(_unpack_small(m_small, shapes))
    new_v.update(_unpack_small(v_small, shapes))

    return (loss, grad_x[None], *[grads[n] for n in _WEIGHTS], *[deltas[n] for n in _WEIGHTS],
            *[new_m[n] for n in _WEIGHTS], *[new_v[n] for n in _WEIGHTS])
```

```python
import functools
import math

import jax
import jax.numpy as jnp
import numpy as np
from jax import lax
from jax.experimental import pallas as pl
from jax.experimental.pallas import tpu as pltpu

F32 = jnp.float32
BF16 = jnp.bfloat16

HEAD_DIM = 128
N_HEADS_A = 8
N_KV_A = 2
N_HEADS_B = 8
N_KV_B = 2
GROUP = 4
GRID_W = 64
BLOCK_Q = 128
WINDOW = 128
N_BUCKETS = 32
MAX_DISTANCE = 128
ROPE_THETA = 10000.0
EPS = 1e-6
NEG_INF = -1e30
ATT_SCALE = HEAD_DIM ** -0.5
LOG2E = math.log2(math.e)
LN2 = math.log(2.0)
Q_SCALE = ATT_SCALE * LOG2E
PAD_LO, PAD_HI = 256, 128

ADAM_LR = 0.001
ADAM_B1 = 0.9
ADAM_B2 = 0.999
ADAM_EPS = 1e-08
ADAM_WD = 0.01
ADAM_STEP = 10

N_CHIPS = 4
N_DEV = 8
COL_QA, COL_KA, COL_VA, COL_QB, COL_KB, COL_VB = 0, 8, 10, 12, 20, 22
N_COLS = 24

VMEM_LIMIT = 52 * 1024 * 1024


def _params(sem=None, collective_id=None):
    return pltpu.CompilerParams(dimension_semantics=sem, vmem_limit_bytes=VMEM_LIMIT, collective_id=collective_id)


_ANY = pl.BlockSpec(memory_space=pl.ANY)
_MESH = pl.DeviceIdType.MESH
SIBLING_BARRIER_ID = 1


def _sibling():
    return (lax.axis_index("x"), lax.axis_index("y"), 1 - lax.axis_index("c"))


class _Comm:
    def __init__(self, inputs, out_shapes, sems, start, finish, aliases=None):
        self.inputs, self.out_shapes, self.sems = list(inputs), list(out_shapes), list(sems)
        self.start, self.finish, self.aliases = start, finish, dict(aliases or {})


def _call(body, *, name, grid, in_specs, out_specs, out_shape, args, scratch_shapes=(), sem=None, comm=None,
          after=None, aliases=None):
    in_specs, out_specs, out_shape = list(in_specs), list(out_specs), list(out_shape)
    scratch_shapes = list(scratch_shapes)
    n_in, n_out, n_sc = len(in_specs), len(out_specs), len(scratch_shapes)
    behind = [] if after is None else [after]
    aliases = dict(aliases or {})
    if comm is None:
        res = pl.pallas_call(
            (lambda *refs: body(*refs[:n_in], *refs[n_in + len(behind):])) if behind else body,
            name=name, grid=grid, in_specs=in_specs + [_ANY] * len(behind), out_specs=out_specs,
            out_shape=out_shape, scratch_shapes=scratch_shapes, input_output_aliases=aliases,
            compiler_params=_params(sem))(*args, *behind)
        return list(res), []
    c_in, c_out = len(comm.inputs), len(comm.out_shapes)

    def hosted(*refs):
        pos = [0]

        def take(n):
            pos[0] += n
            return refs[pos[0] - n:pos[0]]

        ins, c_ins, _, outs, c_outs, scr = (take(n_in), take(c_in), take(len(behind)), take(n_out), take(c_out),
                                            take(n_sc))
        c_sems = refs[pos[0]:]
        ids = [pl.program_id(a) for a in range(len(grid))]
        first = functools.reduce(jnp.logical_and, [i == 0 for i in ids])
        last = functools.reduce(jnp.logical_and, [i == g - 1 for i, g in zip(ids, grid)])

        @pl.when(first)
        def _():
            barrier = pltpu.get_barrier_semaphore()
            pl.semaphore_signal(barrier, inc=1, device_id=_sibling(), device_id_type=_MESH)
            pl.semaphore_wait(barrier, 1)
            comm.start(c_ins, c_outs, c_sems)

        body(*ins, *outs, *scr)

        @pl.when(last)
        def _():
            comm.finish(c_ins, c_outs, c_sems)

    res = pl.pallas_call(
        hosted, name=name, grid=grid, in_specs=in_specs + [_ANY] * (c_in + len(behind)),
        out_specs=out_specs + [_ANY] * c_out,
        out_shape=out_shape + comm.out_shapes, scratch_shapes=scratch_shapes + comm.sems,
        input_output_aliases={**aliases, **{n_in + i: n_out + o for i, o in comm.aliases.items()}},
        compiler_params=_params(("arbitrary",) * len(grid), SIBLING_BARRIER_ID))(*args, *comm.inputs, *behind)
    return list(res[:n_out]), list(res[n_out:])


def _matmul(a, b, *, mode, out_dtypes, name, epilogue=None, extras=(), bm=1024, bn=1024, bk=2048,
            out_stack=0, comm=None, after=None):
    stacked = b.ndim == 3
    if mode == "nn":
        m, k = a.shape
        if stacked:
            nj, kb, ns = b.shape
            n, ks = nj * ns, k
        else:
            kb, n = b.shape
            ns, ks = n, k
        dn = (((1,), (0,)), ((), ()))
    elif mode == "nt":
        m, k = a.shape
        if stacked:
            nj, n, ks = b.shape
            kb = nj * ks
        else:
            n, kb = b.shape
            ks = kb
        ns = n
        dn = (((1,), (1,)), ((), ()))
    else:
        k, m = a.shape
        kb, n = b.shape
        ns, ks = n, k
        dn = (((0,), (0,)), ((), ()))
    assert k == kb and not (stacked and mode == "tn")
    ns_out = n // out_stack if out_stack else n
    bm, bn, bk = min(bm, m), min(bn, ns, ns_out), min(bk, ks)
    assert m % bm == 0 and ns % bn == 0 and ns_out % bn == 0 and ks % bk == 0
    gm, gn, gk = m // bm, n // bn, k // bk

    if mode == "tn":
        a_spec = pl.BlockSpec((bk, bm), lambda i, j, q: (q, i))
    else:
        a_spec = pl.BlockSpec((bm, bk), lambda i, j, q: (i, q))
    if mode == "nt":
        if stacked:
            per = ks // bk
            b_spec = pl.BlockSpec((None, bn, bk), lambda i, j, q: (q // per, j, q % per))
        else:
            b_spec = pl.BlockSpec((bn, bk), lambda i, j, q: (j, q))
    else:
        if stacked:
            per = ns // bn
            b_spec = pl.BlockSpec((None, bk, bn), lambda i, j, q: (j // per, q, j % per))
        else:
            b_spec = pl.BlockSpec((bk, bn), lambda i, j, q: (q, j))
    ex_spec = pl.BlockSpec((bm, bn), lambda i, j, q: (i, j))
    if out_stack:
        per_o = ns_out // bn
        o_spec = pl.BlockSpec((None, bm, bn), lambda i, j, q: (j // per_o, i, j % per_o))
        o_shape = (out_stack, m, ns_out)
    else:
        o_spec = ex_spec
        o_shape = (m, n)
    n_ex, n_out = len(extras), len(out_dtypes)

    def body(a_ref, b_ref, *rest):
        ex, outs = rest[:n_ex], rest[n_ex:n_ex + n_out]
        part = lax.dot_general(a_ref[...], b_ref[...], dn, preferred_element_type=F32)

        def finish(acc):
            res = epilogue(acc, *[e[...] for e in ex]) if epilogue else (acc,)
            for o, r in zip(outs, res):
                o[...] = r.astype(o.dtype)

        if gk == 1:
            finish(part)
        else:
            acc_ref = rest[-1]
            q = pl.program_id(2)

            @pl.when(q == 0)
            def _():
                acc_ref[...] = part

            @pl.when(q > 0)
            def _():
                acc_ref[...] += part

            @pl.when(q == gk - 1)
            def _():
                finish(acc_ref[...])

    res, c_res = _call(
        body, name=name, grid=(gm, gn, gk),
        in_specs=[a_spec, b_spec] + [ex_spec] * n_ex,
        out_specs=[o_spec] * n_out,
        out_shape=[jax.ShapeDtypeStruct(o_shape, dt) for dt in out_dtypes],
        scratch_shapes=[pltpu.VMEM((bm, bn), F32)] if gk > 1 else [],
        sem=("parallel", "parallel", "arbitrary"), args=(a, b, *extras), comm=comm, after=after)
    res = res[0] if n_out == 1 else res
    return res if comm is None else (res, c_res)


def _rms_fwd(x, g, *, name, tm=256, comm=None):
    s, d = x.shape
    tm = min(tm, s)

    def body(x_ref, g_ref, o_ref):
        xf = x_ref[...]
        r = lax.rsqrt(jnp.mean(xf * xf, axis=-1, keepdims=True) + EPS)
        o_ref[...] = (xf * r * g_ref[...]).astype(o_ref.dtype)

    res, c_res = _call(
        body, name=name, grid=(s // tm,),
        in_specs=[pl.BlockSpec((tm, d), lambda i: (i, 0)), pl.BlockSpec((1, d), lambda i: (0, 0))],
        out_specs=[pl.BlockSpec((tm, d), lambda i: (i, 0))],
        out_shape=[jax.ShapeDtypeStruct((s, d), BF16)],
        sem=("parallel",), args=(x, g), comm=comm)
    return res[0] if comm is None else (res[0], c_res)


def _rms_bwd(x, dy, g, add, *, name, want_bf16, tm=256):
    s, d = x.shape
    tm = min(tm, s)

    def body(x_ref, dy_ref, g_ref, add_ref, dx_ref, *rest):
        dg_ref = rest[-1]
        i = pl.program_id(0)
        xf = x_ref[...]
        dyf = dy_ref[...].astype(F32)
        r = lax.rsqrt(jnp.mean(xf * xf, axis=-1, keepdims=True) + EPS)
        xh = xf * r
        dyg = dyf * g_ref[...]
        dx = r * (dyg - xh * jnp.mean(dyg * xh, axis=-1, keepdims=True))
        tot = add_ref[...] + dx
        dx_ref[...] = tot
        if want_bf16:
            rest[0][...] = tot.astype(BF16)
        part = jnp.sum(dyf * xh, axis=0, keepdims=True)

        @pl.when(i == 0)
        def _():
            dg_ref[...] = part

        @pl.when(i > 0)
        def _():
            dg_ref[...] += part

    row = pl.BlockSpec((tm, d), lambda i: (i, 0))
    vec = pl.BlockSpec((1, d), lambda i: (0, 0))
    out_specs = [row] + ([row] if want_bf16 else []) + [vec]
    out_shape = [jax.ShapeDtypeStruct((s, d), F32)]
    if want_bf16:
        out_shape.append(jax.ShapeDtypeStruct((s, d), BF16))
    out_shape.append(jax.ShapeDtypeStruct((1, d), F32))
    return pl.pallas_call(
        body,
        name=name,
        grid=(s // tm,),
        in_specs=[row, row, vec, row],
        out_specs=out_specs,
        out_shape=out_shape,
        compiler_params=_params(("arbitrary",)),
    )(x, dy, g, add)


def _tail(h2, gate, pp, target, g_ple, g_final, *, tm=128):
    s, d = h2.shape
    tm = min(tm, s)

    def body(h2_ref, gate_ref, pp_ref, t_ref, gp_ref, gf_ref, dh3_ref, dz_ref, dpp_ref, dgf_ref, dgp_ref, loss_ref):
        i = pl.program_id(0)
        ppf = pp_ref[...]
        gate_v = gate_ref[...]
        r_p = lax.rsqrt(jnp.mean(ppf * ppf, axis=-1, keepdims=True) + EPS)
        eh = ppf * r_p
        e = eh * gp_ref[...]
        h3 = h2_ref[...] + gate_v * e
        r_f = lax.rsqrt(jnp.mean(h3 * h3, axis=-1, keepdims=True) + EPS)
        yh = h3 * r_f
        diff = yh * gf_ref[...] - t_ref[...]
        loss_part = 0.5 * jnp.sum(jnp.mean(diff * diff, axis=-1, keepdims=True), axis=0, keepdims=True)
        dy = diff / d
        dgf = jnp.sum(dy * yh, axis=0, keepdims=True)
        dyg = dy * gf_ref[...]
        dh3 = r_f * (dyg - yh * jnp.mean(dyg * yh, axis=-1, keepdims=True))
        dh3_ref[...] = dh3
        de = dh3 * gate_v
        dz_ref[...] = (dh3 * e * gate_v * (1.0 - gate_v)).astype(BF16)
        dgp = jnp.sum(de * eh, axis=0, keepdims=True)
        deg = de * gp_ref[...]
        dpp_ref[...] = (r_p * (deg - eh * jnp.mean(deg * eh, axis=-1, keepdims=True))).astype(BF16)
        loss_row = jnp.broadcast_to(loss_part, (1, 128))

        @pl.when(i == 0)
        def _():
            dgf_ref[...] = dgf
            dgp_ref[...] = dgp
            loss_ref[...] = loss_row

        @pl.when(i > 0)
        def _():
            dgf_ref[...] += dgf
            dgp_ref[...] += dgp
            loss_ref[...] += loss_row

    row = pl.BlockSpec((tm, d), lambda i: (i, 0))
    vec = pl.BlockSpec((1, d), lambda i: (0, 0))
    return pl.pallas_call(
        body,
        name="tail_fwd_bwd",
        grid=(s // tm,),
        in_specs=[row, row, row, row, vec, vec],
        out_specs=[row, row, row, vec, vec, pl.BlockSpec((1, 128), lambda i: (0, 0))],
        out_shape=[
            jax.ShapeDtypeStruct((s, d), F32),
            jax.ShapeDtypeStruct((s, d), BF16),
            jax.ShapeDtypeStruct((s, d), BF16),
            jax.ShapeDtypeStruct((1, d), F32),
            jax.ShapeDtypeStruct((1, d), F32),
            jax.ShapeDtypeStruct((1, 128), F32),
        ],
        compiler_params=_params(("arbitrary",)),
    )(h2, gate, pp, target, g_ple, g_final)


def _rope_tables(s):
    rows = s // GRID_W
    half = HEAD_DIM // 2
    inv_freq = ROPE_THETA ** (-jnp.arange(0, half, 2, dtype=F32) / half)
    ang_r = jnp.arange(rows, dtype=jnp.int32).astype(F32)[:, None] * inv_freq
    ang_c = jnp.arange(GRID_W, dtype=jnp.int32).astype(F32)[:, None] * inv_freq
    cr, sr = (jnp.repeat(t, GRID_W, axis=0) for t in (jnp.cos(ang_r), jnp.sin(ang_r)))
    cc, sc = (jnp.tile(t, (rows, 1)) for t in (jnp.cos(ang_c), jnp.sin(ang_c)))
    cos_t = jnp.concatenate([cr, cr, cc, cc], axis=-1)
    sin_t = jnp.concatenate([-sr, sr, -sc, sc], axis=-1)
    return cos_t, sin_t


def _swap_quarters(x):
    lane = lax.broadcasted_iota(jnp.int32, x.shape, x.ndim - 1)
    up = pltpu.roll(x, HEAD_DIM - 32, x.ndim - 1)
    down = pltpu.roll(x, 32, x.ndim - 1)
    return jnp.where((lane % 64) < 32, up, down)


def _cols(first, count=1):
    return slice(first * HEAD_DIM, (first + count) * HEAD_DIM)


def _qk_prep(proj, g_q, g_k, cos_t, sin_t, *, tm=256, comm=None):
    s, n = proj.shape
    tm = min(tm, s)

    def body(x_ref, gq_ref, gk_ref, c_ref, s_ref, o_ref):
        cos_v, sin_v = c_ref[...], s_ref[...]
        for h in range(COL_VA):
            x = x_ref[:, _cols(h)]
            g = gq_ref[...] if h < COL_KA else gk_ref[...]
            xn = x * lax.rsqrt(jnp.mean(x * x, axis=-1, keepdims=True) + EPS) * g
            xr = xn * cos_v + _swap_quarters(xn) * sin_v
            if h < COL_KA:
                xr = xr * Q_SCALE
            o_ref[:, _cols(h)] = xr.astype(BF16)
        o_ref[:, _cols(COL_VA, 2)] = x_ref[:, _cols(COL_VA, 2)].astype(BF16)
        o_ref[:, _cols(COL_QB, N_HEADS_B)] = (x_ref[:, _cols(COL_QB, N_HEADS_B)] * Q_SCALE).astype(BF16)
        o_ref[:, _cols(COL_KB, 4)] = x_ref[:, _cols(COL_KB, 4)].astype(BF16)

    row = pl.BlockSpec((tm, n), lambda i: (i, 0))
    tab = pl.BlockSpec((tm, HEAD_DIM), lambda i: (i, 0))
    vec = pl.BlockSpec((1, HEAD_DIM), lambda i: (0, 0))
    res, c_res = _call(
        body, name="qk_prep", grid=(s // tm,),
        in_specs=[row, vec, vec, tab, tab],
        out_specs=[row],
        out_shape=[jax.ShapeDtypeStruct((s, n), BF16)],
        sem=("parallel",), args=(proj, g_q, g_k, cos_t, sin_t), comm=comm)
    return res[0] if comm is None else (res[0], c_res)


def _qk_bwd(dqa, dka, dva, dqb, dkpad, dvpad, proj, g_q, g_k, cos_t, sin_t, *, comm=None, after=None):
    s, n = proj.shape
    tm = min(PAD_LO, s)
    assert PAD_LO % tm == 0
    lo = PAD_LO // tm

    def body(dqa_ref, dka_ref, dva_ref, dqb_ref, dkb_ref, dvb_ref, x_ref, gq_ref, gk_ref, c_ref, s_ref,
             o_ref, dgq_ref, dgk_ref):
        i = pl.program_id(0)
        cos_v, sin_v = c_ref[...], s_ref[...]

        def head(d, x, g):
            dn = d * cos_v + _swap_quarters(d * sin_v)
            r = lax.rsqrt(jnp.mean(x * x, axis=-1, keepdims=True) + EPS)
            xh = x * r
            dng = dn * g
            dx = r * (dng - xh * jnp.mean(dng * xh, axis=-1, keepdims=True))
            return dx.astype(BF16), jnp.sum(dn * xh, axis=0, keepdims=True)

        acc_q = jnp.zeros((1, HEAD_DIM), F32)
        acc_k = jnp.zeros((1, HEAD_DIM), F32)
        for h in range(N_HEADS_A):
            o_ref[:, _cols(h)], part = head(dqa_ref[:, _cols(h)] * ATT_SCALE, x_ref[:, _cols(h)], gq_ref[...])
            acc_q = acc_q + part
        for h in range(N_KV_A):
            o_ref[:, _cols(COL_KA + h)], part = head(dka_ref[:, _cols(h)] * LN2, x_ref[:, _cols(COL_KA + h)],
                                                     gk_ref[...])
            acc_k = acc_k + part
        o_ref[:, _cols(COL_VA, 2)] = dva_ref[...].astype(BF16)
        o_ref[:, _cols(COL_QB, N_HEADS_B)] = (dqb_ref[...] * ATT_SCALE).astype(BF16)
        o_ref[:, _cols(COL_KB, 2)] = (dkb_ref[...] * LN2).astype(BF16)
        o_ref[:, _cols(COL_VB, 2)] = dvb_ref[...].astype(BF16)

        @pl.when(i == 0)
        def _():
            dgq_ref[...] = acc_q
            dgk_ref[...] = acc_k

        @pl.when(i > 0)
        def _():
            dgq_ref[...] += acc_q
            dgk_ref[...] += acc_k

    def rows(width, shift=0):
        return pl.BlockSpec((tm, width), lambda i: (i + shift, 0))

    kv_w = N_KV_A * HEAD_DIM
    q_w = N_HEADS_A * HEAD_DIM
    vec = pl.BlockSpec((1, HEAD_DIM), lambda i: (0, 0))
    res, c_res = _call(
        body, name="qk_bwd", grid=(s // tm,),
        in_specs=[rows(q_w), rows(kv_w), rows(kv_w), rows(q_w), rows(kv_w, lo), rows(kv_w, lo), rows(n),
                  vec, vec, rows(HEAD_DIM), rows(HEAD_DIM)],
        out_specs=[rows(n), vec, vec],
        out_shape=[
            jax.ShapeDtypeStruct((s, n), BF16),
            jax.ShapeDtypeStruct((1, HEAD_DIM), F32),
            jax.ShapeDtypeStruct((1, HEAD_DIM), F32),
        ],
        sem=("arbitrary",), args=(dqa, dka, dva, dqb, dkpad, dvpad, proj, g_q, g_k, cos_t, sin_t), comm=comm,
        after=after)
    return res if comm is None else (res, c_res)


_NT = (((1,), (1,)), ((), ()))
_TN = (((0,), (0,)), ((), ()))


def _attn_a_fwd(pb, *, tq=2048, sub=256, comm=None, after=None):
    s = pb.shape[0]
    tq = min(tq, s)

    sub = min(sub, tq)

    def body(q_ref, k_ref, v_ref, o_ref, lse_ref):
        k = k_ref[...]
        v = v_ref[...]
        for r in range(tq // sub):
            rows = pl.ds(r * sub, sub)
            sc = lax.dot_general(q_ref[rows, :], k, _NT, preferred_element_type=F32)
            m = jnp.max(sc, axis=-1, keepdims=True)
            p = jnp.exp2(sc - m)
            l = jnp.sum(p, axis=-1, keepdims=True)
            o = jnp.dot(p.astype(BF16), v, preferred_element_type=F32)
            o_ref[rows, :] = (o / l).astype(BF16)
            lse_ref[rows, :] = jnp.broadcast_to(m + jnp.log2(l), (sub, HEAD_DIM))

    res, c_res = _call(
        body, name="attn_a_fwd", grid=(N_HEADS_A, s // tq),
        in_specs=[
            pl.BlockSpec((tq, HEAD_DIM), lambda h, i: (i, COL_QA + h)),
            pl.BlockSpec((s, HEAD_DIM), lambda h, i: (0, COL_KA + h // GROUP)),
            pl.BlockSpec((s, HEAD_DIM), lambda h, i: (0, COL_VA + h // GROUP)),
        ],
        out_specs=[
            pl.BlockSpec((tq, HEAD_DIM), lambda h, i: (i, h)),
            pl.BlockSpec((None, tq, HEAD_DIM), lambda h, i: (h, i, 0)),
        ],
        out_shape=[
            jax.ShapeDtypeStruct((s, (N_HEADS_A + N_HEADS_B) * HEAD_DIM), BF16),
            jax.ShapeDtypeStruct((N_HEADS_A, s, HEAD_DIM), F32),
        ],
        sem=("parallel", "parallel"), args=(pb, pb, pb), comm=comm, after=after)
    return res if comm is None else (res, c_res)


def _attn_a_bwd(pb, att, datt, lse, *, tq=2048, sub=256, comm=None):
    s = pb.shape[0]
    tq = min(tq, s)
    sub = min(sub, tq)

    def body(q_ref, k_ref, v_ref, o_ref, do_ref, lse_ref, dq_ref, dk_ref, dv_ref):
        first = jnp.logical_and(pl.program_id(1) == 0, pl.program_id(2) == 0)
        k = k_ref[...]
        v = v_ref[...]
        dk = dv = None
        for r in range(tq // sub):
            rows = pl.ds(r * sub, sub)
            q = q_ref[rows, :]
            do = do_ref[rows, :]
            sc = lax.dot_general(q, k, _NT, preferred_element_type=F32)
            p = jnp.exp2(sc - lse_ref[rows, :][:, :1])
            dp = lax.dot_general(do, v, _NT, preferred_element_type=F32)
            delta = jnp.sum(do.astype(F32) * o_ref[rows, :].astype(F32), axis=-1, keepdims=True)
            ds = (p * (dp - delta)).astype(BF16)
            dq_ref[rows, :] = jnp.dot(ds, k, preferred_element_type=F32)
            dk_r = lax.dot_general(ds, q, _TN, preferred_element_type=F32)
            dv_r = lax.dot_general(p.astype(BF16), do, _TN, preferred_element_type=F32)
            dk = dk_r if dk is None else dk + dk_r
            dv = dv_r if dv is None else dv + dv_r

        @pl.when(first)
        def _():
            dk_ref[...] = dk
            dv_ref[...] = dv

        @pl.when(jnp.logical_not(first))
        def _():
            dk_ref[...] += dk
            dv_ref[...] += dv

    qmap = lambda kv, g, i: (i, kv * GROUP + g)
    res, c_res = _call(
        body, name="attn_a_bwd", grid=(N_KV_A, GROUP, s // tq),
        in_specs=[
            pl.BlockSpec((tq, HEAD_DIM), lambda kv, g, i: (i, COL_QA + kv * GROUP + g)),
            pl.BlockSpec((s, HEAD_DIM), lambda kv, g, i: (0, COL_KA + kv)),
            pl.BlockSpec((s, HEAD_DIM), lambda kv, g, i: (0, COL_VA + kv)),
            pl.BlockSpec((tq, HEAD_DIM), qmap),
            pl.BlockSpec((tq, HEAD_DIM), qmap),
            pl.BlockSpec((None, tq, HEAD_DIM), lambda kv, g, i: (kv * GROUP + g, i, 0)),
        ],
        out_specs=[
            pl.BlockSpec((tq, HEAD_DIM), qmap),
            pl.BlockSpec((s, HEAD_DIM), lambda kv, g, i: (0, kv)),
            pl.BlockSpec((s, HEAD_DIM), lambda kv, g, i: (0, kv)),
        ],
        out_shape=[
            jax.ShapeDtypeStruct((s, N_HEADS_A * HEAD_DIM), F32),
            jax.ShapeDtypeStruct((s, N_KV_A * HEAD_DIM), F32),
            jax.ShapeDtypeStruct((s, N_KV_A * HEAD_DIM), F32),
        ],
        sem=("arbitrary", "arbitrary", "arbitrary"), args=(pb, pb, pb, att, datt, lse), comm=comm)
    return res if comm is None else (res, c_res)


def _t5_bucket(rel):
    nb = N_BUCKETS // 2
    ret = jnp.where(rel > 0, nb, 0)
    n = jnp.abs(rel)
    max_exact = nb // 2
    nf = jnp.maximum(n, 1).astype(F32)
    large = max_exact + (jnp.log(nf / max_exact) / math.log(MAX_DISTANCE / max_exact)
                         * (nb - max_exact)).astype(jnp.int32)
    large = jnp.minimum(large, nb - 1)
    return ret + jnp.where(n < max_exact, n, large)


def _band_buckets():
    r = jnp.arange(BLOCK_Q, dtype=jnp.int32)
    j = jnp.arange(3 * BLOCK_Q, dtype=jnp.int32)
    return _t5_bucket((j[None, :] - BLOCK_Q) - r[:, None])


def _band_bias(bucket, table_ref, h):
    acc = jnp.zeros(bucket.shape, F32)
    for b in range(N_BUCKETS):
        acc = jnp.where(bucket == b, table_ref[b, h], acc)
    return acc


GQ = GROUP * BLOCK_Q


def _stack_heads(x):
    return jnp.concatenate([x[:, _cols(g)] for g in range(GROUP)], axis=0)


def _unstack_heads(x):
    return jnp.concatenate([x[g * BLOCK_Q:(g + 1) * BLOCK_Q] for g in range(GROUP)], axis=1)


def _group_bias(bucket, table_ref, kv):
    return jnp.concatenate([_band_bias(bucket, table_ref, kv * GROUP + g) * LOG2E for g in range(GROUP)], axis=0)


def _group_sink(sink_ref, kv):
    head = lax.broadcasted_iota(jnp.int32, (GQ, 1), 0) // BLOCK_Q
    snk = jnp.zeros((GQ, 1), F32)
    for g in range(GROUP):
        snk = jnp.where(head == g, sink_ref[0, kv * GROUP + g] * LOG2E, snk)
    return snk


def _band_mask(n, s):
    r = lax.broadcasted_iota(jnp.int32, (GQ, 3 * BLOCK_Q), 0) % BLOCK_Q
    j = lax.broadcasted_iota(jnp.int32, (GQ, 3 * BLOCK_Q), 1)
    rel = j - BLOCK_Q - r
    kabs = n * BLOCK_Q + j - BLOCK_Q
    return (jnp.abs(rel) <= WINDOW) & (kabs >= 0) & (kabs < s)


def _band_start(n):
    return pl.multiple_of(n * BLOCK_Q + (PAD_LO - BLOCK_Q), BLOCK_Q)


def _attn_b_fwd(pb, kpad, vpad, bucket, table, sink, att, *, comm=None, after=None):
    s = pb.shape[0]
    nblk = s // BLOCK_Q
    sp = kpad.shape[0]

    def body(table_ref, sink_ref, q0_ref, q1_ref, k_ref, v_ref, bucket_ref, _, o_ref, lse_ref, bias_ref):
        n = pl.program_id(0)

        @pl.when(n == 0)
        def _():
            for kv in range(N_KV_B):
                bias_ref[kv * GQ:(kv + 1) * GQ, :] = _group_bias(bucket_ref[...], table_ref, kv)

        band = pl.ds(_band_start(n), 3 * BLOCK_Q)
        mask = _band_mask(n, s)
        for kv, q_ref in enumerate((q0_ref, q1_ref)):
            kb = k_ref[band, _cols(kv)]
            vb = v_ref[band, _cols(kv)]
            sc = lax.dot_general(_stack_heads(q_ref[...]), kb, _NT, preferred_element_type=F32)
            sc = jnp.where(mask, sc + bias_ref[kv * GQ:(kv + 1) * GQ, :], NEG_INF)
            snk = _group_sink(sink_ref, kv)
            m = jnp.maximum(jnp.max(sc, axis=-1, keepdims=True), snk)
            p = jnp.exp2(sc - m)
            l = jnp.sum(p, axis=-1, keepdims=True) + jnp.exp2(snk - m)
            o = jnp.dot(p.astype(BF16), vb, preferred_element_type=F32)
            o_ref[:, _cols(kv * GROUP, GROUP)] = _unstack_heads((o / l).astype(BF16))
            lse = m + jnp.log2(l)
            for g in range(GROUP):
                lse_ref[kv * GROUP + g] = jnp.broadcast_to(lse[g * BLOCK_Q:(g + 1) * BLOCK_Q], (BLOCK_Q, HEAD_DIM))

    smem = pl.BlockSpec(memory_space=pltpu.SMEM)
    wide = GROUP * HEAD_DIM
    whole = pl.BlockSpec((sp, N_KV_B * HEAD_DIM), lambda n: (0, 0))
    res, c_res = _call(
        body, name="attn_b_fwd", grid=(nblk,),
        in_specs=[
            smem,
            smem,
            pl.BlockSpec((BLOCK_Q, wide), lambda n: (n, COL_QB // GROUP)),
            pl.BlockSpec((BLOCK_Q, wide), lambda n: (n, COL_QB // GROUP + 1)),
            whole,
            whole,
            pl.BlockSpec((BLOCK_Q, 3 * BLOCK_Q), lambda n: (0, 0)),
            _ANY,
        ],
        out_specs=[
            pl.BlockSpec((BLOCK_Q, N_HEADS_B * HEAD_DIM), lambda n: (n, 1)),
            pl.BlockSpec((N_HEADS_B, BLOCK_Q, HEAD_DIM), lambda n: (0, n, 0)),
        ],
        out_shape=[
            jax.ShapeDtypeStruct(att.shape, BF16),
            jax.ShapeDtypeStruct((N_HEADS_B, s, HEAD_DIM), F32),
        ],
        scratch_shapes=[pltpu.VMEM((N_KV_B * GQ, 3 * BLOCK_Q), F32)],
        sem=("arbitrary",), args=(table, sink, pb, pb, kpad, vpad, bucket, att), comm=comm, after=after,
        aliases={7: 0})
    return res if comm is None else (res, c_res)


def _attn_b_bwd(pb, kpad, vpad, att, datt, lse, bucket, table, sink, *, comm=None, after=None):
    s = pb.shape[0]
    nblk = s // BLOCK_Q
    sp = kpad.shape[0]

    def body(table_ref, sink_ref, q0_ref, q1_ref, k_ref, v_ref, o_ref, do_ref, lse_ref, bucket_ref,
             dq_ref, dk_ref, dv_ref, dtab_ref, dsink_ref, bias_ref, dbias_ref):
        n = pl.program_id(0)

        @pl.when(n == 0)
        def _():
            dk_ref[...] = jnp.zeros_like(dk_ref)
            dv_ref[...] = jnp.zeros_like(dv_ref)
            dbias_ref[...] = jnp.zeros_like(dbias_ref)
            dsink_ref[...] = jnp.zeros_like(dsink_ref)
            for kv in range(N_KV_B):
                bias_ref[kv * GQ:(kv + 1) * GQ, :] = _group_bias(bucket_ref[...], table_ref, kv)

        band = pl.ds(_band_start(n), 3 * BLOCK_Q)
        mask = _band_mask(n, s)
        for kv, q_ref in enumerate((q0_ref, q1_ref)):
            wide_cols = _cols(kv * GROUP, GROUP)
            q = _stack_heads(q_ref[...])
            do = _stack_heads(do_ref[:, wide_cols])
            o = _stack_heads(o_ref[:, wide_cols])
            kb = k_ref[band, _cols(kv)]
            vb = v_ref[band, _cols(kv)]
            lse = jnp.concatenate([lse_ref[kv * GROUP + g][:, :1] for g in range(GROUP)], axis=0)
            sc = lax.dot_general(q, kb, _NT, preferred_element_type=F32)
            sc = jnp.where(mask, sc + bias_ref[kv * GQ:(kv + 1) * GQ, :], NEG_INF)
            p = jnp.exp2(sc - lse)
            dp = lax.dot_general(do, vb, _NT, preferred_element_type=F32)
            delta = jnp.sum(do.astype(F32) * o.astype(F32), axis=-1, keepdims=True)
            ds = p * (dp - delta)
            dsb = ds.astype(BF16)
            dq_ref[:, wide_cols] = _unstack_heads(jnp.dot(dsb, kb, preferred_element_type=F32))
            dk_ref[band, _cols(kv)] += lax.dot_general(dsb, q, _TN, preferred_element_type=F32)
            dv_ref[band, _cols(kv)] += lax.dot_general(p.astype(BF16), do, _TN, preferred_element_type=F32)
            dbias_ref[kv * GQ:(kv + 1) * GQ, :] += ds
            sink_part = -jnp.exp2(_group_sink(sink_ref, kv) - lse) * delta
            for g in range(GROUP):
                rows = slice(g * BLOCK_Q, (g + 1) * BLOCK_Q)
                dsink_ref[kv * GROUP + g] += jnp.broadcast_to(
                    jnp.sum(sink_part[rows], axis=0, keepdims=True), (1, HEAD_DIM))

        @pl.when(n == nblk - 1)
        def _():
            bucket_v = bucket_ref[...]
            row = lax.broadcasted_iota(jnp.int32, (N_BUCKETS, HEAD_DIM), 0)
            for h in range(N_HEADS_B):
                acc = dbias_ref[h * BLOCK_Q:(h + 1) * BLOCK_Q, :]
                tot = jnp.zeros((N_BUCKETS, HEAD_DIM), F32)
                for b in range(N_BUCKETS):
                    tot = jnp.where(row == b, jnp.sum(jnp.where(bucket_v == b, acc, 0.0), keepdims=True), tot)
                dtab_ref[h] = tot

    smem = pl.BlockSpec(memory_space=pltpu.SMEM)
    wide = GROUP * HEAD_DIM
    whole = pl.BlockSpec((sp, N_KV_B * HEAD_DIM), lambda n: (0, 0))
    group_b = pl.BlockSpec((BLOCK_Q, N_HEADS_B * HEAD_DIM), lambda n: (n, 1))
    res, c_res = _call(
        body, name="attn_b_bwd", grid=(nblk,),
        in_specs=[
            smem,
            smem,
            pl.BlockSpec((BLOCK_Q, wide), lambda n: (n, COL_QB // GROUP)),
            pl.BlockSpec((BLOCK_Q, wide), lambda n: (n, COL_QB // GROUP + 1)),
            whole,
            whole,
            group_b,
            group_b,
            pl.BlockSpec((N_HEADS_B, BLOCK_Q, HEAD_DIM), lambda n: (0, n, 0)),
            pl.BlockSpec((BLOCK_Q, 3 * BLOCK_Q), lambda n: (0, 0)),
        ],
        out_specs=[
            pl.BlockSpec((BLOCK_Q, N_HEADS_B * HEAD_DIM), lambda n: (n, 0)),
            whole,
            whole,
            pl.BlockSpec((N_HEADS_B, N_BUCKETS, HEAD_DIM), lambda n: (0, 0, 0)),
            pl.BlockSpec((N_HEADS_B, 1, HEAD_DIM), lambda n: (0, 0, 0)),
        ],
        out_shape=[
            jax.ShapeDtypeStruct((s, N_HEADS_B * HEAD_DIM), F32),
            jax.ShapeDtypeStruct((sp, N_KV_B * HEAD_DIM), F32),
            jax.ShapeDtypeStruct((sp, N_KV_B * HEAD_DIM), F32),
            jax.ShapeDtypeStruct((N_HEADS_B, N_BUCKETS, HEAD_DIM), F32),
            jax.ShapeDtypeStruct((N_HEADS_B, 1, HEAD_DIM), F32),
        ],
        scratch_shapes=[pltpu.VMEM((N_KV_B * GQ, 3 * BLOCK_Q), F32), pltpu.VMEM((N_KV_B * GQ, 3 * BLOCK_Q), F32)],
        sem=("arbitrary",),
        args=(table, sink, pb, pb, kpad, vpad, att, datt, lse, bucket), comm=comm, after=after)
    return res if comm is None else (res, c_res)


_MESH = pl.DeviceIdType.MESH


def _other_chips(x, y):
    return [(x, 1 - y), (1 - x, y), (1 - x, 1 - y)]


_HBM = pl.BlockSpec(memory_space=pltpu.HBM)
_SEM = pl.BlockSpec(memory_space=pltpu.SEMAPHORE)
_SPLIT = pltpu.CompilerParams(has_side_effects=pltpu.SideEffectType.DATAFLOW_SIDE_EFFECTING)


def _in_hbm(a):
    return pltpu.with_memory_space_constraint(a, pltpu.HBM)


def _my_half(rows):
    c = lax.axis_index("c")
    half = rows // 2
    return pl.ds(pl.multiple_of(c * half, half), half), pl.ds(pl.multiple_of((1 - c) * half, half), half)


def _gather_route(shapes):
    def route(src, land):
        x, y, c = lax.axis_index("x"), lax.axis_index("y"), lax.axis_index("c")
        out = []
        for t, shape in enumerate(shapes):
            mine, _ = _my_half(shape[0])
            for px, py in _other_chips(x, y):
                out.append((src[t].at[mine], land[t].at[2 * x + y, mine], land[t].at[2 * px + py, mine], (px, py, c)))
        return out

    return route


def _exchange_route(n_t):
    def route(src, land):
        x, y, c = lax.axis_index("x"), lax.axis_index("y"), lax.axis_index("c")
        out = []
        for t in range(n_t):
            for px, py in _other_chips(x, y):
                k = 2 * px + py
                out.append((src[t].at[k], land[t].at[2 * (2 * x + y) + c], land[t].at[2 * k + c], (px, py, c)))
        return out

    return route


def _own_slot(shape, dtype, slot, block):
    return lax.dynamic_update_slice(lax.empty(shape, dtype), block[None], (slot,) + (0,) * (len(shape) - 1))


def _split_start(name, srcs, lands, route, after):
    n = len(srcs)

    def body(*refs):
        src, land, send_sems, recv_sems, token = refs[:n], refs[n:2 * n], refs[2 * n + 1], refs[2 * n + 2], refs[-1]
        for i, (src_ref, dst_ref, _, to) in enumerate(route(src, land)):
            pltpu.make_async_remote_copy(src_ref=src_ref, dst_ref=dst_ref, send_sem=send_sems.at[i],
                                         recv_sem=recv_sems.at[i], device_id=to, device_id_type=_MESH).start()
        token[...] = jnp.zeros_like(token)

    sem = pltpu.SemaphoreType.DMA((3 * n,))
    lands = list(lands)
    res = pl.pallas_call(
        body, name=name,
        in_specs=[_HBM] * (2 * n) + [_ANY],
        out_specs=[_SEM, _SEM] + [_HBM] * (2 * n) + [pl.BlockSpec(memory_space=pltpu.VMEM)],
        out_shape=[sem, sem] + [pltpu.HBM(a.shape, a.dtype) for a in list(srcs) + lands]
        + [jax.ShapeDtypeStruct((8, 128), F32)],
        input_output_aliases={i: 2 + i for i in range(2 * n)},
        compiler_params=_SPLIT,
    )(*[_in_hbm(a) for a in srcs], *[_in_hbm(a) for a in lands], after)
    return (res[0], res[1]), res[2:2 + n], res[2 + n:2 + 2 * n], res[-1]


def _split_wait(name, srcs, lands, sems, route, after):
    n = len(srcs)

    def body(*refs):
        src, land, send_sems, recv_sems = refs[:n], refs[n:2 * n], refs[2 * n], refs[2 * n + 1]
        for i, (src_ref, _, dst_ref, to) in enumerate(route(src, land)):
            cp = pltpu.make_async_remote_copy(src_ref=src_ref, dst_ref=dst_ref, send_sem=send_sems.at[i],
                                              recv_sem=recv_sems.at[i], device_id=to, device_id_type=_MESH)
            cp.wait_send()
            cp.wait_recv()

    res = pl.pallas_call(
        body, name=name,
        in_specs=[_HBM] * (2 * n) + [_SEM, _SEM, _ANY],
        out_specs=[_HBM] * (2 * n),
        out_shape=[pltpu.HBM(a.shape, a.dtype) for a in list(srcs) + list(lands)],
        input_output_aliases={i: i for i in range(2 * n)},
        compiler_params=_SPLIT,
    )(*srcs, *lands, sems[0], sems[1], after)
    return res[:n], res[n:]


def _comm_only(name, comm):
    return _call(lambda: None, name=name, grid=(1,), in_specs=[], out_specs=[], out_shape=[], args=(), comm=comm)[1]


def _swap_comm(shards, lands):
    n_t = len(lands)

    def copies(land, sems, later):
        send_sems, recv_sems = sems
        x, y = lax.axis_index("x"), lax.axis_index("y")
        sends, recvs = [], []
        for t in range(n_t):
            mine, other = _my_half(shards[t].shape[0])
            for j, (px, py) in enumerate(_other_chips(x, y)):
                k = 2 * px + py
                for part, out in ((mine, sends), (other, recvs)) if later else ((mine, sends),):
                    out.append(pltpu.make_async_remote_copy(
                        src_ref=land[t].at[k, part], dst_ref=land[t].at[k, part], send_sem=send_sems.at[3 * t + j],
                        recv_sem=recv_sems.at[3 * t + j], device_id=_sibling(), device_id_type=_MESH))
        return sends, recvs

    def start(ins, land, sems):
        for cp in copies(land, sems, False)[0]:
            cp.start()

    def finish(ins, land, sems):
        sends, recvs = copies(land, sems, True)
        for cp in recvs:
            cp.wait_recv()
        for cp in sends:
            cp.wait_send()

    return _Comm(
        lands, [jax.ShapeDtypeStruct(a.shape, a.dtype) for a in lands],
        [pltpu.SemaphoreType.DMA((3 * n_t,)), pltpu.SemaphoreType.DMA((3 * n_t,))],
        start, finish, aliases={t: t for t in range(n_t)})


def _forward_comm(partials, lands):
    n_t = len(lands)

    def copies(land, sems, later):
        send_sems, recv_sems = sems
        x, y, c = lax.axis_index("x"), lax.axis_index("y"), lax.axis_index("c")
        sends, recvs = [], []
        for t in range(n_t):
            for j, k in enumerate([2 * x + y] + [2 * px + py for px, py in _other_chips(x, y)]):
                for slot, out in ((2 * k + c, sends), (2 * k + 1 - c, recvs)) if later else ((2 * k + c, sends),):
                    out.append(pltpu.make_async_remote_copy(
                        src_ref=land[t].at[slot], dst_ref=land[t].at[slot], send_sem=send_sems.at[4 * t + j],
                        recv_sem=recv_sems.at[4 * t + j], device_id=_sibling(), device_id_type=_MESH))
        return sends, recvs

    def start(ins, land, sems):
        for cp in copies(land, sems, False)[0]:
            cp.start()

    def finish(ins, land, sems):
        sends, recvs = copies(land, sems, True)
        for cp in recvs:
            cp.wait_recv()
        for cp in sends:
            cp.wait_send()

    return _Comm(
        lands, [jax.ShapeDtypeStruct(a.shape, a.dtype) for a in lands],
        [pltpu.SemaphoreType.DMA((4 * n_t,)), pltpu.SemaphoreType.DMA((4 * n_t,))],
        start, finish, aliases={t: t for t in range(n_t)})


def _allreduce_small(pack):
    rows, d = pack.shape

    def body(p_ref, sum_ref, all_ref, send_sems, recv_sems):
        x, y, c = lax.axis_index("x"), lax.axis_index("y"), lax.axis_index("c")
        me = 4 * x + 2 * y + c
        all_ref[me] = p_ref[...]
        peers = []
        for dx in range(2):
            for dy in range(2):
                for dc in range(2):
                    if dx or dy or dc:
                        px = 1 - x if dx else x
                        py = 1 - y if dy else y
                        pc = 1 - c if dc else c
                        peers.append((4 * dx + 2 * dy + dc - 1, (px, py, pc)))
        sends = []
        for k, to in peers:
            cp = pltpu.make_async_remote_copy(
                src_ref=p_ref, dst_ref=all_ref.at[me], send_sem=send_sems.at[k], recv_sem=recv_sems.at[k],
                device_id=to, device_id_type=_MESH)
            cp.start()
            sends.append(cp)
        for k, (px, py, pc) in peers:
            pltpu.make_async_remote_copy(
                src_ref=p_ref, dst_ref=all_ref.at[4 * px + 2 * py + pc], send_sem=send_sems.at[k],
                recv_sem=recv_sems.at[k], device_id=(px, py, pc), device_id_type=_MESH).wait_recv()
        for cp in sends:
            cp.wait_send()
        tot = all_ref[0]
        for i in range(1, N_DEV):
            tot = tot + all_ref[i]
        sum_ref[...] = tot

    vm = pl.BlockSpec(memory_space=pltpu.VMEM)
    return pl.pallas_call(
        body,
        name="allreduce_small",
        in_specs=[vm],
        out_specs=vm,
        out_shape=jax.ShapeDtypeStruct((rows, d), F32),
        scratch_shapes=[
            pltpu.VMEM((N_DEV, rows, d), F32),
            pltpu.SemaphoreType.DMA((N_DEV - 1,)),
            pltpu.SemaphoreType.DMA((N_DEV - 1,)),
        ],
    )(pack)


def _adamw_math(w, g, m, v):
    m = ADAM_B1 * m + (1.0 - ADAM_B1) * g
    v = ADAM_B2 * v + (1.0 - ADAM_B2) * (g * g)
    m_hat = m / (1.0 - ADAM_B1 ** ADAM_STEP)
    v_hat = v / (1.0 - ADAM_B2 ** ADAM_STEP)
    delta = -ADAM_LR * (m_hat / (jnp.sqrt(v_hat) + ADAM_EPS) + ADAM_WD * w)
    return delta, m, v


def _sum_adamw(parts, w, m, v, *, name, tr=256):
    r, c = w.shape
    tr = min(tr, r)
    tc = min(c, 1024)

    def body(p_ref, w_ref, m_ref, v_ref, g_ref, d_ref, m2_ref, v2_ref):
        g = p_ref[0].astype(F32)
        for i in range(1, N_DEV):
            g = g + p_ref[i].astype(F32)
        delta, m2, v2 = _adamw_math(w_ref[...], g, m_ref[...], v_ref[...])
        g_ref[...] = g
        d_ref[...] = delta
        m2_ref[...] = m2
        v2_ref[...] = v2

    blk = pl.BlockSpec((tr, tc), lambda i, j: (i, j))
    return pl.pallas_call(
        body,
        name=name,
        grid=(r // tr, c // tc),
        in_specs=[pl.BlockSpec((N_DEV, tr, tc), lambda i, j: (0, i, j)), blk, blk, blk],
        out_specs=[blk] * 4,
        out_shape=[jax.ShapeDtypeStruct((r, c), F32)] * 4,
        compiler_params=_params(("parallel", "parallel")),
    )(parts, w, m, v)


def _adamw_small(g, w, m, v):
    def body(g_ref, w_ref, m_ref, v_ref, d_ref, m2_ref, v2_ref):
        delta, m2, v2 = _adamw_math(w_ref[...], g_ref[...], m_ref[...], v_ref[...])
        d_ref[...] = delta
        m2_ref[...] = m2
        v2_ref[...] = v2

    vm = pl.BlockSpec(memory_space=pltpu.VMEM)
    return pl.pallas_call(
        body,
        name="adamw_small",
        in_specs=[vm] * 4,
        out_specs=[vm] * 3,
        out_shape=[jax.ShapeDtypeStruct(g.shape, F32)] * 3,
    )(g, w, m, v)


def _relu2_epilogue(acc):
    ra = jnp.maximum(acc, 0.0)
    return ra * ra, ra


def _rows(stacked):
    return stacked.reshape(stacked.shape[0] * stacked.shape[1], stacked.shape[2])


def _by_chip(mat):
    return mat.reshape(N_CHIPS, mat.shape[0] // N_CHIPS, mat.shape[1])


def _local_step(x, p, target, shards, small, update):
    s, d = x.shape
    cos_t, sin_t = _rope_tables(s)
    bucket = _band_buckets()
    p_bf = p.astype(BF16)
    wts = {}

    chip = 2 * lax.axis_index("x") + lax.axis_index("y")
    core = lax.axis_index("c")

    def gather(tag, names, after):
        srcs = [cast[n] for n in names]
        route = _gather_route([a.shape for a in srcs])
        sems, srcs, lands, token = _split_start(f"gather_start_{tag}", srcs, [zones[n] for n in names], route, after)

        def landed(done):
            got_srcs, got_lands = _split_wait(f"gather_wait_{tag}", srcs, lands, sems, route, done)
            comm = _swap_comm(got_srcs, got_lands)
            comm.waited = got_srcs[0]
            return comm

        return landed, token

    def prepare(n, zero):
        cast[n] = (shards[n] + zero).astype(BF16)
        zones[n] = _own_slot((N_CHIPS,) + cast[n].shape, BF16, chip, cast[n])

    cast, zones = {}, {}
    prepare("w_in", 0.0)
    in_landed, token = gather("in", ["w_in"], small["attn_norm_g"])
    for n in shards:
        if n != "w_in":
            prepare(n, token[:1, :1])
    g_attn = small["attn_norm_g"] + token[:1, :1]
    u = _rms_fwd(x, g_attn, name="norm_attn")
    prepared = u[:1, :1].astype(F32) + sum(lax.dynamic_slice(z, (chip, 0, 0), (1, 1, 1))[0].astype(F32)
                                           for n, z in zones.items() if n != "w_in")
    (wts["w_in"],) = _comm_only("swap_w_in", in_landed(prepared))
    mid_landed, token = gather("mid", ["w_out"], wts["w_in"])
    proj = _matmul(u, wts["w_in"], mode="nn", out_dtypes=[F32], name="mm_in", bn=768, after=token)
    pb, (w_out_s,) = _qk_prep(proj, small["q_norm_g"], small["k_norm_g"], cos_t, sin_t, comm=mid_landed(proj))
    wts["w_out"] = _rows(w_out_s)
    up_landed, token = gather("up", ["w_up"], pb)
    att_a, lse_a = _attn_a_fwd(pb, after=token)
    pad = ((PAD_LO, PAD_HI), (0, 0))
    kpad = jnp.pad(pb[:, COL_KB * HEAD_DIM:COL_VB * HEAD_DIM], pad)
    vpad = jnp.pad(pb[:, COL_VB * HEAD_DIM:], pad)
    up_swap = up_landed(att_a)
    down_landed, token = gather("down", ["w_down"], up_swap.waited)
    (att, lse_b), (wts["w_up"],) = _attn_b_fwd(pb, kpad, vpad, bucket, small["rel_bias_table"],
                                               small["sink_logits"], att_a, comm=up_swap, after=token)
    h1 = _matmul(att, wts["w_out"], mode="nn", out_dtypes=[F32], name="mm_out",
                 epilogue=lambda acc, res: (acc + res,), extras=(x,))
    mn = _rms_fwd(h1, small["mlp_norm_g"], name="norm_mlp")
    r, ra = _matmul(mn, wts["w_up"], mode="nn", out_dtypes=[BF16, BF16], name="mm_up", epilogue=_relu2_epilogue,
                    bm=2048)
    (w_down_s,) = _comm_only("swap_w_down", down_landed(r))
    wts["w_down"] = _rows(w_down_s)
    late_landed, token = gather("late", ["w_gate", "ple_w"], w_down_s)
    h2 = _matmul(r, wts["w_down"], mode="nn", out_dtypes=[F32], name="mm_down",
                 epilogue=lambda acc, res: (acc + res,), extras=(h1,), after=token)
    ng, (w_gate_s, wts["ple_w"]) = _rms_fwd(h2, small["gate_norm_g"], name="norm_gate", comm=late_landed(h2))
    wts["w_gate"] = _rows(w_gate_s)
    gate = _matmul(ng, wts["w_gate"], mode="nn", out_dtypes=[F32], name="mm_gate",
                   epilogue=lambda acc: (1.0 / (1.0 + jnp.exp(-acc)),))
    pp = _matmul(p_bf, wts["ple_w"], mode="nn", out_dtypes=[F32], name="mm_ple", bn=512)
    dh3, dz, dpp, dg_final, dg_ple, loss = _tail(h2, gate, pp, target, small["ple_norm_g"], small["final_norm_g"])

    dng = _matmul(dz, wts["w_gate"], mode="nt", out_dtypes=[F32], name="mm_gate_dx")
    gw_gate = _matmul(ng, dz, mode="tn", out_dtypes=[BF16], name="mm_gate_dw")
    gw_ple = _matmul(p_bf, dpp, mode="tn", out_dtypes=[BF16], name="mm_ple_dw", bn=512, out_stack=N_CHIPS)
    dh2, dh2_bf, dg_gate = _rms_bwd(h2, dng, small["gate_norm_g"], dh3, name="norm_gate_bwd", want_bf16=True)

    def exchange(tag, partials, after):
        route = _exchange_route(len(partials))
        lands = [_own_slot((N_DEV,) + g.shape[1:], g.dtype, 2 * chip + core,
                           lax.dynamic_index_in_dim(g, chip, 0, keepdims=False)) for g in partials]
        sems, srcs, lands, token = _split_start(f"exchange_start_{tag}", partials, lands, route, after)

        def landed(done):
            got_srcs, got_lands = _split_wait(f"exchange_wait_{tag}", srcs, lands, sems, route, done)
            comm = _forward_comm(got_srcs, got_lands)
            comm.waited = got_srcs[0]
            return comm

        return landed, token

    big = {}
    gate_landed, token = exchange("gate", [_by_chip(gw_gate), gw_ple], dh2_bf)
    gw_down = _matmul(r, dh2_bf, mode="tn", out_dtypes=[BF16], name="mm_down_dw", after=token)
    da, (parts_gate, parts_ple) = _matmul(
        dh2_bf, wts["w_down"], mode="nt", out_dtypes=[BF16], name="mm_down_dx", bm=2048,
        epilogue=lambda acc, ra_v: (acc * (2.0 * ra_v.astype(F32)),), extras=(ra,), comm=gate_landed(gw_down))
    down_landed, token = exchange("down", [_by_chip(gw_down)], da)
    big["w_gate"], big["ple_w"] = update("w_gate", parts_gate), update("ple_w", parts_ple)
    gw_up = _matmul(mn, da, mode="tn", out_dtypes=[BF16], name="mm_up_dw", out_stack=N_CHIPS, after=token)
    dmn = _matmul(da, wts["w_up"], mode="nt", out_dtypes=[F32], name="mm_up_dx")
    dh1, dh1_bf, dg_mlp = _rms_bwd(h1, dmn, small["mlp_norm_g"], dh2, name="norm_mlp_bwd", want_bf16=True)
    datt, (parts_down,) = _matmul(dh1_bf, wts["w_out"], mode="nt", out_dtypes=[BF16], name="mm_out_dx",
                                  comm=down_landed(dh1_bf))
    gw_out = _matmul(att, dh1_bf, mode="tn", out_dtypes=[BF16], name="mm_out_dw")
    up_landed, token = exchange("up", [gw_up], datt)
    dqb, dkpad, dvpad, dtab, dsink = _attn_b_bwd(pb, kpad, vpad, att, datt, lse_b, bucket,
                                                 small["rel_bias_table"], small["sink_logits"], after=token)
    dqa, dka, dva = _attn_a_bwd(pb, att, datt, lse_a)
    up_forward = up_landed(dqa)
    out_landed, token = exchange("out", [_by_chip(gw_out)], up_forward.waited)
    (dproj, dg_q, dg_k), (parts_up,) = _qk_bwd(dqa, dka, dva, dqb, dkpad, dvpad, proj,
                                               small["q_norm_g"], small["k_norm_g"], cos_t, sin_t,
                                               comm=up_forward, after=token)
    gw_in = _matmul(u, dproj, mode="tn", out_dtypes=[BF16], name="mm_in_dw", bn=768, out_stack=N_CHIPS)
    out_forward = out_landed(gw_in)
    in_landed, token = exchange("in", [gw_in], out_forward.waited)
    du, (parts_out,) = _matmul(dproj, wts["w_in"], mode="nt", out_dtypes=[F32], name="mm_in_dx", bk=768,
                               comm=out_forward, after=token)
    grad_x, dg_attn = _rms_bwd(x, du, small["attn_norm_g"], dh1, name="norm_attn_bwd", want_bf16=False)
    for n, parts in (("w_down", parts_down), ("w_up", parts_up), ("w_out", parts_out)):
        big[n] = update(n, parts)
    done = dg_attn + sum(big[n][0][0, :1, :] for n in ("w_down", "w_up", "w_out"))
    (parts_in,) = _comm_only("forward_w_in", in_landed(done))
    big["w_in"] = update("w_in", parts_in)

    small_g = {
        "attn_norm_g": dg_attn, "mlp_norm_g": dg_mlp, "ple_norm_g": dg_ple, "gate_norm_g": dg_gate,
        "final_norm_g": dg_final, "q_norm_g": dg_q, "k_norm_g": dg_k,
        "sink_logits": dsink[:, 0, 0][None, :], "rel_bias_table": dtab[:, :, 0].T,
    }
    return loss, grad_x, big, small_g


_SMALL_ROWS = ["attn_norm_g", "mlp_norm_g", "ple_norm_g", "gate_norm_g", "final_norm_g"]
_PACK_ROWS = 8


def _pack_small(vals, d):
    rows = [vals[n].reshape(1, d) for n in _SMALL_ROWS]
    misc = jnp.concatenate([
        vals["q_norm_g"].reshape(1, HEAD_DIM), vals["k_norm_g"].reshape(1, HEAD_DIM),
        jnp.pad(vals["sink_logits"].reshape(1, N_HEADS_B), ((0, 0), (0, HEAD_DIM - N_HEADS_B))),
        vals["rel_bias_table"].reshape(1, N_BUCKETS * N_HEADS_B)], axis=1)
    rows.append(jnp.pad(misc, ((0, 0), (0, d - misc.shape[1]))))
    rows.append(jnp.zeros((_PACK_ROWS - len(rows), d), F32))
    return jnp.concatenate(rows, axis=0).astype(F32)


def _unpack_small(pack, shapes):
    out = {n: pack[i].reshape(shapes[n]) for i, n in enumerate(_SMALL_ROWS)}
    misc = pack[len(_SMALL_ROWS)]
    out["q_norm_g"] = misc[:HEAD_DIM].reshape(shapes["q_norm_g"])
    out["k_norm_g"] = misc[HEAD_DIM:2 * HEAD_DIM].reshape(shapes["k_norm_g"])
    out["sink_logits"] = misc[2 * HEAD_DIM:2 * HEAD_DIM + N_HEADS_B].reshape(shapes["sink_logits"])
    out["rel_bias_table"] = misc[3 * HEAD_DIM:3 * HEAD_DIM + N_BUCKETS * N_HEADS_B].reshape(shapes["rel_bias_table"])
    return out


_WEIGHTS = ["attn_norm_g", "w_in", "q_norm_g", "k_norm_g", "sink_logits", "w_out", "mlp_norm_g", "w_up", "w_down",
            "ple_w", "ple_norm_g", "gate_norm_g", "w_gate", "rel_bias_table", "final_norm_g"]
_BIG = ["w_in", "w_out", "w_up", "w_down", "ple_w", "w_gate"]


def kernel(x, p, attn_norm_g, w_in, q_norm_g, k_norm_g, sink_logits, w_out, mlp_norm_g, w_up, w_down, ple_w, ple_norm_g, gate_norm_g, w_gate, rel_bias_table, final_norm_g, loss_target, m_attn_norm_g, m_w_in, m_q_norm_g, m_k_norm_g, m_sink_logits, m_w_out, m_mlp_norm_g, m_w_up, m_w_down, m_ple_w, m_ple_norm_g, m_gate_norm_g, m_w_gate, m_rel_bias_table, m_final_norm_g, v_attn_norm_g, v_w_in, v_q_norm_g, v_k_norm_g, v_sink_logits, v_w_out, v_mlp_norm_g, v_w_up, v_w_down, v_ple_w, v_ple_norm_g, v_gate_norm_g, v_w_gate, v_rel_bias_table, v_final_norm_g):
    given = dict(locals())
    w = {n: given[n] for n in _WEIGHTS}
    m = {n: given["m_" + n] for n in _WEIGHTS}
    v = {n: given["v_" + n] for n in _WEIGHTS}
    d = x.shape[-1]

    shards = {n: w[n][0] for n in _BIG}
    small = {
        "attn_norm_g": w["attn_norm_g"], "mlp_norm_g": w["mlp_norm_g"], "ple_norm_g": w["ple_norm_g"],
        "gate_norm_g": w["gate_norm_g"], "final_norm_g": w["final_norm_g"].reshape(1, d),
        "q_norm_g": w["q_norm_g"], "k_norm_g": w["k_norm_g"], "sink_logits": w["sink_logits"],
        "rel_bias_table": w["rel_bias_table"],
    }

    def update(n, parts):
        res = _sum_adamw(parts, w[n][0], m[n][0], v[n][0], name="adamw_" + n)
        return [t.reshape(w[n].shape) for t in res]

    loss_part, grad_x, big, small_g = _local_step(x[0], p[0, 0], loss_target[0], shards, small, update)
    grads, deltas, new_m, new_v = [{n: big[n][i] for n in _BIG} for i in range(4)]

    shapes = {n: w[n].shape for n in _WEIGHTS if n not in _BIG}
    pack = _pack_small(small_g, d)
    pack = pack.at[_PACK_ROWS - 1, :1].add(0.0 * grads["w_in"][0, 0, :1])
    pack = pack.at[_PACK_ROWS - 1, 1].set(loss_part[0, 0])
    g_small = _allreduce_small(pack)
    loss = g_small[_PACK_ROWS - 1, 1]
    d_small, m_small, v_small = _adamw_small(g_small, _pack_small(w, d), _pack_small(m, d), _pack_small(v, d))
    grads.update(_unpack_small(g_small, shapes))
    deltas.update(_unpack_small(d_small, shapes))
    new_m.update(_unpack_small(m_small, shapes))
    new_v.update(_unpack_small(v_small, shapes))

    return (loss, grad_x[None], *[grads[n] for n in _WEIGHTS], *[deltas[n] for n in _WEIGHTS],
            *[new_m[n] for n in _WEIGHTS], *[new_v[n] for n in _WEIGHTS])
```

```python
import functools
import math

import jax
import jax.numpy as jnp
import numpy as np
from jax import lax
from jax.experimental import pallas as pl
from jax.experimental.pallas import tpu as pltpu

F32 = jnp.float32
BF16 = jnp.bfloat16

HEAD_DIM = 128
N_HEADS_A = 8
N_KV_A = 2
N_HEADS_B = 8
N_KV_B = 2
GROUP = 4
GRID_W = 64
BLOCK_Q = 128
WINDOW = 128
N_BUCKETS = 32
MAX_DISTANCE = 128
ROPE_THETA = 10000.0
EPS = 1e-6
NEG_INF = -1e30
ATT_SCALE = HEAD_DIM ** -0.5
LOG2E = math.log2(math.e)
LN2 = math.log(2.0)
Q_SCALE = ATT_SCALE * LOG2E
PAD_LO, PAD_HI = 256, 128

ADAM_LR = 0.001
ADAM_B1 = 0.9
ADAM_B2 = 0.999
ADAM_EPS = 1e-08
ADAM_WD = 0.01
ADAM_STEP = 10

N_CHIPS = 4
N_DEV = 8
COL_QA, COL_KA, COL_VA, COL_QB, COL_KB, COL_VB = 0, 8, 10, 12, 20, 22
N_COLS = 24

VMEM_LIMIT = 52 * 1024 * 1024


def _params(sem=None, collective_id=None):
    return pltpu.CompilerParams(dimension_semantics=sem, vmem_limit_bytes=VMEM_LIMIT, collective_id=collective_id)


_ANY = pl.BlockSpec(memory_space=pl.ANY)
_MESH = pl.DeviceIdType.MESH
SIBLING_BARRIER_ID = 1


def _sibling():
    return (lax.axis_index("x"), lax.axis_index("y"), 1 - lax.axis_index("c"))


class _Comm:
    def __init__(self, inputs, out_shapes, sems, start, finish, aliases=None):
        self.inputs, self.out_shapes, self.sems = list(inputs), list(out_shapes), list(sems)
        self.start, self.finish, self.aliases = start, finish, dict(aliases or {})


def _call(body, *, name, grid, in_specs, out_specs, out_shape, args, scratch_shapes=(), sem=None, comm=None,
          after=None, aliases=None):
    in_specs, out_specs, out_shape = list(in_specs), list(out_specs), list(out_shape)
    scratch_shapes = list(scratch_shapes)
    n_in, n_out, n_sc = len(in_specs), len(out_specs), len(scratch_shapes)
    behind = [] if after is None else [after]
    aliases = dict(aliases or {})
    if comm is None:
        res = pl.pallas_call(
            (lambda *refs: body(*refs[:n_in], *refs[n_in + len(behind):])) if behind else body,
            name=name, grid=grid, in_specs=in_specs + [_ANY] * len(behind), out_specs=out_specs,
            out_shape=out_shape, scratch_shapes=scratch_shapes, input_output_aliases=aliases,
            compiler_params=_params(sem))(*args, *behind)
        return list(res), []
    c_in, c_out = len(comm.inputs), len(comm.out_shapes)

    def hosted(*refs):
        pos = [0]

        def take(n):
            pos[0] += n
            return refs[pos[0] - n:pos[0]]

        ins, c_ins, _, outs, c_outs, scr = (take(n_in), take(c_in), take(len(behind)), take(n_out), take(c_out),
                                            take(n_sc))
        c_sems = refs[pos[0]:]
        ids = [pl.program_id(a) for a in range(len(grid))]
        first = functools.reduce(jnp.logical_and, [i == 0 for i in ids])
        last = functools.reduce(jnp.logical_and, [i == g - 1 for i, g in zip(ids, grid)])

        @pl.when(first)
        def _():
            barrier = pltpu.get_barrier_semaphore()
            pl.semaphore_signal(barrier, inc=1, device_id=_sibling(), device_id_type=_MESH)
            pl.semaphore_wait(barrier, 1)
            comm.start(c_ins, c_outs, c_sems)

        body(*ins, *outs, *scr)

        @pl.when(last)
        def _():
            comm.finish(c_ins, c_outs, c_sems)

    res = pl.pallas_call(
        hosted, name=name, grid=grid, in_specs=in_specs + [_ANY] * (c_in + len(behind)),
        out_specs=out_specs + [_ANY] * c_out,
        out_shape=out_shape + comm.out_shapes, scratch_shapes=scratch_shapes + comm.sems,
        input_output_aliases={**aliases, **{n_in + i: n_out + o for i, o in comm.aliases.items()}},
        compiler_params=_params(("arbitrary",) * len(grid), SIBLING_BARRIER_ID))(*args, *comm.inputs, *behind)
    return list(res[:n_out]), list(res[n_out:])


def _matmul(a, b, *, mode, out_dtypes, name, epilogue=None, extras=(), bm=1024, bn=1024, bk=2048,
            out_stack=0, comm=None, after=None):
    stacked = b.ndim == 3
    if mode == "nn":
        m, k = a.shape
        if stacked:
            nj, kb, ns = b.shape
            n, ks = nj * ns, k
        else:
            kb, n = b.shape
            ns, ks = n, k
        dn = (((1,), (0,)), ((), ()))
    elif mode == "nt":
        m, k = a.shape
        if stacked:
            nj, n, ks = b.shape
            kb = nj * ks
        else:
            n, kb = b.shape
            ks = kb
        ns = n
        dn = (((1,), (1,)), ((), ()))
    else:
        k, m = a.shape
        kb, n = b.shape
        ns, ks = n, k
        dn = (((0,), (0,)), ((), ()))
    assert k == kb and not (stacked and mode == "tn")
    ns_out = n // out_stack if out_stack else n
    whole_k = stacked and mode == "nt" and bk >= k
    bm, bn, bk = min(bm, m), min(bn, ns, ns_out), k if whole_k else min(bk, ks)
    assert m % bm == 0 and ns % bn == 0 and ns_out % bn == 0 and ks % bk == 0 or whole_k
    gm, gn, gk = m // bm, n // bn, k // bk

    if mode == "tn":
        a_spec = pl.BlockSpec((bk, bm), lambda i, j, q: (q, i))
    else:
        a_spec = pl.BlockSpec((bm, bk), lambda i, j, q: (i, q))
    if mode == "nt":
        if whole_k:
            b_spec = pl.BlockSpec((nj, bn, ks), lambda i, j, q: (0, j, 0))
        elif stacked:
            per = ks // bk
            b_spec = pl.BlockSpec((None, bn, bk), lambda i, j, q: (q // per, j, q % per))
        else:
            b_spec = pl.BlockSpec((bn, bk), lambda i, j, q: (j, q))
    else:
        if stacked:
            per = ns // bn
            b_spec = pl.BlockSpec((None, bk, bn), lambda i, j, q: (j // per, q, j % per))
        else:
            b_spec = pl.BlockSpec((bk, bn), lambda i, j, q: (q, j))
    ex_spec = pl.BlockSpec((bm, bn), lambda i, j, q: (i, j))
    if out_stack:
        per_o = ns_out // bn
        o_spec = pl.BlockSpec((None, bm, bn), lambda i, j, q: (j // per_o, i, j % per_o))
        o_shape = (out_stack, m, ns_out)
    else:
        o_spec = ex_spec
        o_shape = (m, n)
    n_ex, n_out = len(extras), len(out_dtypes)

    def body(a_ref, b_ref, *rest):
        ex, outs = rest[:n_ex], rest[n_ex:n_ex + n_out]
        if whole_k:
            part = sum(lax.dot_general(a_ref[:, t * ks:(t + 1) * ks], b_ref[t], dn, preferred_element_type=F32)
                       for t in range(nj))
        else:
            part = lax.dot_general(a_ref[...], b_ref[...], dn, preferred_element_type=F32)

        def finish(acc):
            res = epilogue(acc, *[e[...] for e in ex]) if epilogue else (acc,)
            for o, r in zip(outs, res):
                o[...] = r.astype(o.dtype)

        if gk == 1:
            finish(part)
        else:
            acc_ref = rest[-1]
            q = pl.program_id(2)

            @pl.when(q == 0)
            def _():
                acc_ref[...] = part

            @pl.when(q > 0)
            def _():
                acc_ref[...] += part

            @pl.when(q == gk - 1)
            def _():
                finish(acc_ref[...])

    res, c_res = _call(
        body, name=name, grid=(gm, gn, gk),
        in_specs=[a_spec, b_spec] + [ex_spec] * n_ex,
        out_specs=[o_spec] * n_out,
        out_shape=[jax.ShapeDtypeStruct(o_shape, dt) for dt in out_dtypes],
        scratch_shapes=[pltpu.VMEM((bm, bn), F32)] if gk > 1 else [],
        sem=("parallel", "parallel", "arbitrary"), args=(a, b, *extras), comm=comm, after=after)
    res = res[0] if n_out == 1 else res
    return res if comm is None else (res, c_res)


def _rms_fwd(x, g, *, name, tm=256, comm=None):
    s, d = x.shape
    tm = min(tm, s)

    def body(x_ref, g_ref, o_ref):
        xf = x_ref[...]
        r = lax.rsqrt(jnp.mean(xf * xf, axis=-1, keepdims=True) + EPS)
        o_ref[...] = (xf * r * g_ref[...]).astype(o_ref.dtype)

    res, c_res = _call(
        body, name=name, grid=(s // tm,),
        in_specs=[pl.BlockSpec((tm, d), lambda i: (i, 0)), pl.BlockSpec((1, d), lambda i: (0, 0))],
        out_specs=[pl.BlockSpec((tm, d), lambda i: (i, 0))],
        out_shape=[jax.ShapeDtypeStruct((s, d), BF16)],
        sem=("parallel",), args=(x, g), comm=comm)
    return res[0] if comm is None else (res[0], c_res)


def _rms_bwd(x, dy, g, add, *, name, want_bf16, tm=256):
    s, d = x.shape
    tm = min(tm, s)

    def body(x_ref, dy_ref, g_ref, add_ref, dx_ref, *rest):
        dg_ref = rest[-1]
        i = pl.program_id(0)
        xf = x_ref[...]
        dyf = dy_ref[...].astype(F32)
        r = lax.rsqrt(jnp.mean(xf * xf, axis=-1, keepdims=True) + EPS)
        xh = xf * r
        dyg = dyf * g_ref[...]
        dx = r * (dyg - xh * jnp.mean(dyg * xh, axis=-1, keepdims=True))
        tot = add_ref[...] + dx
        dx_ref[...] = tot
        if want_bf16:
            rest[0][...] = tot.astype(BF16)
        part = jnp.sum(dyf * xh, axis=0, keepdims=True)

        @pl.when(i == 0)
        def _():
            dg_ref[...] = part

        @pl.when(i > 0)
        def _():
            dg_ref[...] += part

    row = pl.BlockSpec((tm, d), lambda i: (i, 0))
    vec = pl.BlockSpec((1, d), lambda i: (0, 0))
    out_specs = [row] + ([row] if want_bf16 else []) + [vec]
    out_shape = [jax.ShapeDtypeStruct((s, d), F32)]
    if want_bf16:
        out_shape.append(jax.ShapeDtypeStruct((s, d), BF16))
    out_shape.append(jax.ShapeDtypeStruct((1, d), F32))
    return pl.pallas_call(
        body,
        name=name,
        grid=(s // tm,),
        in_specs=[row, row, vec, row],
        out_specs=out_specs,
        out_shape=out_shape,
        compiler_params=_params(("arbitrary",)),
    )(x, dy, g, add)


def _tail(h2, gate, pp, target, g_ple, g_final, *, tm=128):
    s, d = h2.shape
    tm = min(tm, s)

    def body(h2_ref, gate_ref, pp_ref, t_ref, gp_ref, gf_ref, dh3_ref, dz_ref, dpp_ref, dgf_ref, dgp_ref, loss_ref):
        i = pl.program_id(0)
        ppf = pp_ref[...]
        gate_v = gate_ref[...]
        r_p = lax.rsqrt(jnp.mean(ppf * ppf, axis=-1, keepdims=True) + EPS)
        eh = ppf * r_p
        e = eh * gp_ref[...]
        h3 = h2_ref[...] + gate_v * e
        r_f = lax.rsqrt(jnp.mean(h3 * h3, axis=-1, keepdims=True) + EPS)
        yh = h3 * r_f
        diff = yh * gf_ref[...] - t_ref[...]
        loss_part = 0.5 * jnp.sum(jnp.mean(diff * diff, axis=-1, keepdims=True), axis=0, keepdims=True)
        dy = diff / d
        dgf = jnp.sum(dy * yh, axis=0, keepdims=True)
        dyg = dy * gf_ref[...]
        dh3 = r_f * (dyg - yh * jnp.mean(dyg * yh, axis=-1, keepdims=True))
        dh3_ref[...] = dh3
        de = dh3 * gate_v
        dz_ref[...] = (dh3 * e * gate_v * (1.0 - gate_v)).astype(BF16)
        dgp = jnp.sum(de * eh, axis=0, keepdims=True)
        deg = de * gp_ref[...]
        dpp_ref[...] = (r_p * (deg - eh * jnp.mean(deg * eh, axis=-1, keepdims=True))).astype(BF16)
        loss_row = jnp.broadcast_to(loss_part, (1, 128))

        @pl.when(i == 0)
        def _():
            dgf_ref[...] = dgf
            dgp_ref[...] = dgp
            loss_ref[...] = loss_row

        @pl.when(i > 0)
        def _():
            dgf_ref[...] += dgf
            dgp_ref[...] += dgp
            loss_ref[...] += loss_row

    row = pl.BlockSpec((tm, d), lambda i: (i, 0))
    vec = pl.BlockSpec((1, d), lambda i: (0, 0))
    return pl.pallas_call(
        body,
        name="tail_fwd_bwd",
        grid=(s // tm,),
        in_specs=[row, row, row, row, vec, vec],
        out_specs=[row, row, row, vec, vec, pl.BlockSpec((1, 128), lambda i: (0, 0))],
        out_shape=[
            jax.ShapeDtypeStruct((s, d), F32),
            jax.ShapeDtypeStruct((s, d), BF16),
            jax.ShapeDtypeStruct((s, d), BF16),
            jax.ShapeDtypeStruct((1, d), F32),
            jax.ShapeDtypeStruct((1, d), F32),
            jax.ShapeDtypeStruct((1, 128), F32),
        ],
        compiler_params=_params(("arbitrary",)),
    )(h2, gate, pp, target, g_ple, g_final)


def _rope_tables(s):
    rows = s // GRID_W
    half = HEAD_DIM // 2
    inv_freq = ROPE_THETA ** (-jnp.arange(0, half, 2, dtype=F32) / half)
    ang_r = jnp.arange(rows, dtype=jnp.int32).astype(F32)[:, None] * inv_freq
    ang_c = jnp.arange(GRID_W, dtype=jnp.int32).astype(F32)[:, None] * inv_freq
    cr, sr = (jnp.repeat(t, GRID_W, axis=0) for t in (jnp.cos(ang_r), jnp.sin(ang_r)))
    cc, sc = (jnp.tile(t, (rows, 1)) for t in (jnp.cos(ang_c), jnp.sin(ang_c)))
    cos_t = jnp.concatenate([cr, cr, cc, cc], axis=-1)
    sin_t = jnp.concatenate([-sr, sr, -sc, sc], axis=-1)
    return cos_t, sin_t


def _swap_quarters(x):
    lane = lax.broadcasted_iota(jnp.int32, x.shape, x.ndim - 1)
    up = pltpu.roll(x, HEAD_DIM - 32, x.ndim - 1)
    down = pltpu.roll(x, 32, x.ndim - 1)
    return jnp.where((lane % 64) < 32, up, down)


def _cols(first, count=1):
    return slice(first * HEAD_DIM, (first + count) * HEAD_DIM)


def _qk_prep(proj, g_q, g_k, cos_t, sin_t, *, tm=256, comm=None):
    s, n = proj.shape
    tm = min(tm, s)

    def body(x_ref, gq_ref, gk_ref, c_ref, s_ref, o_ref):
        cos_v, sin_v = c_ref[...], s_ref[...]
        for h in range(COL_VA):
            x = x_ref[:, _cols(h)]
            g = gq_ref[...] if h < COL_KA else gk_ref[...]
            xn = x * lax.rsqrt(jnp.mean(x * x, axis=-1, keepdims=True) + EPS) * g
            xr = xn * cos_v + _swap_quarters(xn) * sin_v
            if h < COL_KA:
                xr = xr * Q_SCALE
            o_ref[:, _cols(h)] = xr.astype(BF16)
        o_ref[:, _cols(COL_VA, 2)] = x_ref[:, _cols(COL_VA, 2)].astype(BF16)
        o_ref[:, _cols(COL_QB, N_HEADS_B)] = (x_ref[:, _cols(COL_QB, N_HEADS_B)] * Q_SCALE).astype(BF16)
        o_ref[:, _cols(COL_KB, 4)] = x_ref[:, _cols(COL_KB, 4)].astype(BF16)

    row = pl.BlockSpec((tm, n), lambda i: (i, 0))
    tab = pl.BlockSpec((tm, HEAD_DIM), lambda i: (i, 0))
    vec = pl.BlockSpec((1, HEAD_DIM), lambda i: (0, 0))
    res, c_res = _call(
        body, name="qk_prep", grid=(s // tm,),
        in_specs=[row, vec, vec, tab, tab],
        out_specs=[row],
        out_shape=[jax.ShapeDtypeStruct((s, n), BF16)],
        sem=("parallel",), args=(proj, g_q, g_k, cos_t, sin_t), comm=comm)
    return res[0] if comm is None else (res[0], c_res)


def _qk_bwd(dqa, dka, dva, dqb, dkpad, dvpad, proj, g_q, g_k, cos_t, sin_t, *, comm=None, after=None):
    s, n = proj.shape
    tm = min(PAD_LO, s)
    assert PAD_LO % tm == 0
    lo = PAD_LO // tm

    def body(dqa_ref, dka_ref, dva_ref, dqb_ref, dkb_ref, dvb_ref, x_ref, gq_ref, gk_ref, c_ref, s_ref,
             o_ref, dgq_ref, dgk_ref):
        i = pl.program_id(0)
        cos_v, sin_v = c_ref[...], s_ref[...]

        def head(d, x, g):
            dn = d * cos_v + _swap_quarters(d * sin_v)
            r = lax.rsqrt(jnp.mean(x * x, axis=-1, keepdims=True) + EPS)
            xh = x * r
            dng = dn * g
            dx = r * (dng - xh * jnp.mean(dng * xh, axis=-1, keepdims=True))
            return dx.astype(BF16), jnp.sum(dn * xh, axis=0, keepdims=True)

        acc_q = jnp.zeros((1, HEAD_DIM), F32)
        acc_k = jnp.zeros((1, HEAD_DIM), F32)
        for h in range(N_HEADS_A):
            o_ref[:, _cols(h)], part = head(dqa_ref[:, _cols(h)] * ATT_SCALE, x_ref[:, _cols(h)], gq_ref[...])
            acc_q = acc_q + part
        for h in range(N_KV_A):
            o_ref[:, _cols(COL_KA + h)], part = head(dka_ref[:, _cols(h)] * LN2, x_ref[:, _cols(COL_KA + h)],
                                                     gk_ref[...])
            acc_k = acc_k + part
        o_ref[:, _cols(COL_VA, 2)] = dva_ref[...].astype(BF16)
        o_ref[:, _cols(COL_QB, N_HEADS_B)] = (dqb_ref[...] * ATT_SCALE).astype(BF16)
        o_ref[:, _cols(COL_KB, 2)] = (dkb_ref[...] * LN2).astype(BF16)
        o_ref[:, _cols(COL_VB, 2)] = dvb_ref[...].astype(BF16)

        @pl.when(i == 0)
        def _():
            dgq_ref[...] = acc_q
            dgk_ref[...] = acc_k

        @pl.when(i > 0)
        def _():
            dgq_ref[...] += acc_q
            dgk_ref[...] += acc_k

    def rows(width, shift=0):
        return pl.BlockSpec((tm, width), lambda i: (i + shift, 0))

    kv_w = N_KV_A * HEAD_DIM
    q_w = N_HEADS_A * HEAD_DIM
    vec = pl.BlockSpec((1, HEAD_DIM), lambda i: (0, 0))
    res, c_res = _call(
        body, name="qk_bwd", grid=(s // tm,),
        in_specs=[rows(q_w), rows(kv_w), rows(kv_w), rows(q_w), rows(kv_w, lo), rows(kv_w, lo), rows(n),
                  vec, vec, rows(HEAD_DIM), rows(HEAD_DIM)],
        out_specs=[rows(n), vec, vec],
        out_shape=[
            jax.ShapeDtypeStruct((s, n), BF16),
            jax.ShapeDtypeStruct((1, HEAD_DIM), F32),
            jax.ShapeDtypeStruct((1, HEAD_DIM), F32),
        ],
        sem=("arbitrary",), args=(dqa, dka, dva, dqb, dkpad, dvpad, proj, g_q, g_k, cos_t, sin_t), comm=comm,
        after=after)
    return res if comm is None else (res, c_res)


_NT = (((1,), (1,)), ((), ()))
_TN = (((0,), (0,)), ((), ()))


def _attn_a_fwd(pb, *, tq=2048, sub=256, comm=None, after=None):
    s = pb.shape[0]
    tq = min(tq, s)

    sub = min(sub, tq)

    def body(q_ref, k_ref, v_ref, o_ref, lse_ref):
        k = k_ref[...]
        v = v_ref[...]
        for r in range(tq // sub):
            rows = pl.ds(r * sub, sub)
            sc = lax.dot_general(q_ref[rows, :], k, _NT, preferred_element_type=F32)
            m = jnp.max(sc, axis=-1, keepdims=True)
            p = jnp.exp2(sc - m)
            l = jnp.sum(p, axis=-1, keepdims=True)
            o = jnp.dot(p.astype(BF16), v, preferred_element_type=F32)
            o_ref[rows, :] = (o / l).astype(BF16)
            lse_ref[rows, :] = jnp.broadcast_to(m + jnp.log2(l), (sub, HEAD_DIM))

    res, c_res = _call(
        body, name="attn_a_fwd", grid=(N_HEADS_A, s // tq),
        in_specs=[
            pl.BlockSpec((tq, HEAD_DIM), lambda h, i: (i, COL_QA + h)),
            pl.BlockSpec((s, HEAD_DIM), lambda h, i: (0, COL_KA + h // GROUP)),
            pl.BlockSpec((s, HEAD_DIM), lambda h, i: (0, COL_VA + h // GROUP)),
        ],
        out_specs=[
            pl.BlockSpec((tq, HEAD_DIM), lambda h, i: (i, h)),
            pl.BlockSpec((None, tq, HEAD_DIM), lambda h, i: (h, i, 0)),
        ],
        out_shape=[
            jax.ShapeDtypeStruct((s, (N_HEADS_A + N_HEADS_B) * HEAD_DIM), BF16),
            jax.ShapeDtypeStruct((N_HEADS_A, s, HEAD_DIM), F32),
        ],
        sem=("parallel", "parallel"), args=(pb, pb, pb), comm=comm, after=after)
    return res if comm is None else (res, c_res)


def _attn_a_bwd(pb, att, datt, lse, *, tq=2048, sub=256, comm=None):
    s = pb.shape[0]
    tq = min(tq, s)
    sub = min(sub, tq)

    def body(q_ref, k_ref, v_ref, o_ref, do_ref, lse_ref, dq_ref, dk_ref, dv_ref):
        first = jnp.logical_and(pl.program_id(1) == 0, pl.program_id(2) == 0)
        k = k_ref[...]
        v = v_ref[...]
        dk = dv = None
        for r in range(tq // sub):
            rows = pl.ds(r * sub, sub)
            q = q_ref[rows, :]
            do = do_ref[rows, :]
            sc = lax.dot_general(q, k, _NT, preferred_element_type=F32)
            p = jnp.exp2(sc - lse_ref[rows, :][:, :1])
            dp = lax.dot_general(do, v, _NT, preferred_element_type=F32)
            delta = jnp.sum(do.astype(F32) * o_ref[rows, :].astype(F32), axis=-1, keepdims=True)
            ds = (p * (dp - delta)).astype(BF16)
            dq_ref[rows, :] = jnp.dot(ds, k, preferred_element_type=F32)
            dk_r = lax.dot_general(ds, q, _TN, preferred_element_type=F32)
            dv_r = lax.dot_general(p.astype(BF16), do, _TN, preferred_element_type=F32)
            dk = dk_r if dk is None else dk + dk_r
            dv = dv_r if dv is None else dv + dv_r

        @pl.when(first)
        def _():
            dk_ref[...] = dk
            dv_ref[...] = dv

        @pl.when(jnp.logical_not(first))
        def _():
            dk_ref[...] += dk
            dv_ref[...] += dv

    qmap = lambda kv, g, i: (i, kv * GROUP + g)
    res, c_res = _call(
        body, name="attn_a_bwd", grid=(N_KV_A, GROUP, s // tq),
        in_specs=[
            pl.BlockSpec((tq, HEAD_DIM), lambda kv, g, i: (i, COL_QA + kv * GROUP + g)),
            pl.BlockSpec((s, HEAD_DIM), lambda kv, g, i: (0, COL_KA + kv)),
            pl.BlockSpec((s, HEAD_DIM), lambda kv, g, i: (0, COL_VA + kv)),
            pl.BlockSpec((tq, HEAD_DIM), qmap),
            pl.BlockSpec((tq, HEAD_DIM), qmap),
            pl.BlockSpec((None, tq, HEAD_DIM), lambda kv, g, i: (kv * GROUP + g, i, 0)),
        ],
        out_specs=[
            pl.BlockSpec((tq, HEAD_DIM), qmap),
            pl.BlockSpec((s, HEAD_DIM), lambda kv, g, i: (0, kv)),
            pl.BlockSpec((s, HEAD_DIM), lambda kv, g, i: (0, kv)),
        ],
        out_shape=[
            jax.ShapeDtypeStruct((s, N_HEADS_A * HEAD_DIM), F32),
            jax.ShapeDtypeStruct((s, N_KV_A * HEAD_DIM), F32),
            jax.ShapeDtypeStruct((s, N_KV_A * HEAD_DIM), F32),
        ],
        sem=("arbitrary", "arbitrary", "arbitrary"), args=(pb, pb, pb, att, datt, lse), comm=comm)
    return res if comm is None else (res, c_res)


def _t5_bucket(rel):
    nb = N_BUCKETS // 2
    ret = jnp.where(rel > 0, nb, 0)
    n = jnp.abs(rel)
    max_exact = nb // 2
    nf = jnp.maximum(n, 1).astype(F32)
    large = max_exact + (jnp.log(nf / max_exact) / math.log(MAX_DISTANCE / max_exact)
                         * (nb - max_exact)).astype(jnp.int32)
    large = jnp.minimum(large, nb - 1)
    return ret + jnp.where(n < max_exact, n, large)


def _band_buckets():
    r = jnp.arange(BLOCK_Q, dtype=jnp.int32)
    j = jnp.arange(3 * BLOCK_Q, dtype=jnp.int32)
    return _t5_bucket((j[None, :] - BLOCK_Q) - r[:, None])


def _band_bias(bucket, table_ref, h):
    acc = jnp.zeros(bucket.shape, F32)
    for b in range(N_BUCKETS):
        acc = jnp.where(bucket == b, table_ref[b, h], acc)
    return acc


GQ = GROUP * BLOCK_Q


def _stack_heads(x):
    return jnp.concatenate([x[:, _cols(g)] for g in range(GROUP)], axis=0)


def _unstack_heads(x):
    return jnp.concatenate([x[g * BLOCK_Q:(g + 1) * BLOCK_Q] for g in range(GROUP)], axis=1)


def _group_bias(bucket, table_ref, kv):
    return jnp.concatenate([_band_bias(bucket, table_ref, kv * GROUP + g) * LOG2E for g in range(GROUP)], axis=0)


def _group_sink(sink_ref, kv):
    head = lax.broadcasted_iota(jnp.int32, (GQ, 1), 0) // BLOCK_Q
    snk = jnp.zeros((GQ, 1), F32)
    for g in range(GROUP):
        snk = jnp.where(head == g, sink_ref[0, kv * GROUP + g] * LOG2E, snk)
    return snk


def _band_mask(n, s):
    r = lax.broadcasted_iota(jnp.int32, (GQ, 3 * BLOCK_Q), 0) % BLOCK_Q
    j = lax.broadcasted_iota(jnp.int32, (GQ, 3 * BLOCK_Q), 1)
    rel = j - BLOCK_Q - r
    kabs = n * BLOCK_Q + j - BLOCK_Q
    return (jnp.abs(rel) <= WINDOW) & (kabs >= 0) & (kabs < s)


def _band_start(n):
    return pl.multiple_of(n * BLOCK_Q + (PAD_LO - BLOCK_Q), BLOCK_Q)


def _attn_b_fwd(pb, kpad, vpad, bucket, table, sink, att, *, comm=None, after=None):
    s = pb.shape[0]
    nblk = s // BLOCK_Q
    sp = kpad.shape[0]

    def body(table_ref, sink_ref, q0_ref, q1_ref, k_ref, v_ref, bucket_ref, _, o_ref, lse_ref, bias_ref):
        n = pl.program_id(0)

        @pl.when(n == 0)
        def _():
            for kv in range(N_KV_B):
                bias_ref[kv * GQ:(kv + 1) * GQ, :] = _group_bias(bucket_ref[...], table_ref, kv)

        band = pl.ds(_band_start(n), 3 * BLOCK_Q)
        mask = _band_mask(n, s)
        for kv, q_ref in enumerate((q0_ref, q1_ref)):
            kb = k_ref[band, _cols(kv)]
            vb = v_ref[band, _cols(kv)]
            sc = lax.dot_general(_stack_heads(q_ref[...]), kb, _NT, preferred_element_type=F32)
            sc = jnp.where(mask, sc + bias_ref[kv * GQ:(kv + 1) * GQ, :], NEG_INF)
            snk = _group_sink(sink_ref, kv)
            m = jnp.maximum(jnp.max(sc, axis=-1, keepdims=True), snk)
            p = jnp.exp2(sc - m)
            l = jnp.sum(p, axis=-1, keepdims=True) + jnp.exp2(snk - m)
            o = jnp.dot(p.astype(BF16), vb, preferred_element_type=F32)
            o_ref[:, _cols(kv * GROUP, GROUP)] = _unstack_heads((o / l).astype(BF16))
            lse = m + jnp.log2(l)
            for g in range(GROUP):
                lse_ref[kv * GROUP + g] = jnp.broadcast_to(lse[g * BLOCK_Q:(g + 1) * BLOCK_Q], (BLOCK_Q, HEAD_DIM))

    smem = pl.BlockSpec(memory_space=pltpu.SMEM)
    wide = GROUP * HEAD_DIM
    whole = pl.BlockSpec((sp, N_KV_B * HEAD_DIM), lambda n: (0, 0))
    res, c_res = _call(
        body, name="attn_b_fwd", grid=(nblk,),
        in_specs=[
            smem,
            smem,
            pl.BlockSpec((BLOCK_Q, wide), lambda n: (n, COL_QB // GROUP)),
            pl.BlockSpec((BLOCK_Q, wide), lambda n: (n, COL_QB // GROUP + 1)),
            whole,
            whole,
            pl.BlockSpec((BLOCK_Q, 3 * BLOCK_Q), lambda n: (0, 0)),
            _ANY,
        ],
        out_specs=[
            pl.BlockSpec((BLOCK_Q, N_HEADS_B * HEAD_DIM), lambda n: (n, 1)),
            pl.BlockSpec((N_HEADS_B, BLOCK_Q, HEAD_DIM), lambda n: (0, n, 0)),
        ],
        out_shape=[
            jax.ShapeDtypeStruct(att.shape, BF16),
            jax.ShapeDtypeStruct((N_HEADS_B, s, HEAD_DIM), F32),
        ],
        scratch_shapes=[pltpu.VMEM((N_KV_B * GQ, 3 * BLOCK_Q), F32)],
        sem=("arbitrary",), args=(table, sink, pb, pb, kpad, vpad, bucket, att), comm=comm, after=after,
        aliases={7: 0})
    return res if comm is None else (res, c_res)


def _attn_b_bwd(pb, kpad, vpad, att, datt, lse, bucket, table, sink, *, comm=None, after=None):
    s = pb.shape[0]
    nblk = s // BLOCK_Q
    sp = kpad.shape[0]

    def body(table_ref, sink_ref, q0_ref, q1_ref, k_ref, v_ref, o_ref, do_ref, lse_ref, bucket_ref,
             dq_ref, dk_ref, dv_ref, dtab_ref, dsink_ref, bias_ref, dbias_ref):
        n = pl.program_id(0)

        @pl.when(n == 0)
        def _():
            dk_ref[...] = jnp.zeros_like(dk_ref)
            dv_ref[...] = jnp.zeros_like(dv_ref)
            dbias_ref[...] = jnp.zeros_like(dbias_ref)
            dsink_ref[...] = jnp.zeros_like(dsink_ref)
            for kv in range(N_KV_B):
                bias_ref[kv * GQ:(kv + 1) * GQ, :] = _group_bias(bucket_ref[...], table_ref, kv)

        band = pl.ds(_band_start(n), 3 * BLOCK_Q)
        mask = _band_mask(n, s)
        for kv, q_ref in enumerate((q0_ref, q1_ref)):
            wide_cols = _cols(kv * GROUP, GROUP)
            q = _stack_heads(q_ref[...])
            do = _stack_heads(do_ref[:, wide_cols])
            o = _stack_heads(o_ref[:, wide_cols])
            kb = k_ref[band, _cols(kv)]
            vb = v_ref[band, _cols(kv)]
            lse = jnp.concatenate([lse_ref[kv * GROUP + g][:, :1] for g in range(GROUP)], axis=0)
            sc = lax.dot_general(q, kb, _NT, preferred_element_type=F32)
            sc = jnp.where(mask, sc + bias_ref[kv * GQ:(kv + 1) * GQ, :], NEG_INF)
            p = jnp.exp2(sc - lse)
            dp = lax.dot_general(do, vb, _NT, preferred_element_type=F32)
            delta = jnp.sum(do.astype(F32) * o.astype(F32), axis=-1, keepdims=True)
            ds = p * (dp - delta)
            dsb = ds.astype(BF16)
            dq_ref[:, wide_cols] = _unstack_heads(jnp.dot(dsb, kb, preferred_element_type=F32))
            dk_ref[band, _cols(kv)] += lax.dot_general(dsb, q, _TN, preferred_element_type=F32)
            dv_ref[band, _cols(kv)] += lax.dot_general(p.astype(BF16), do, _TN, preferred_element_type=F32)
            dbias_ref[kv * GQ:(kv + 1) * GQ, :] += ds
            sink_part = -jnp.exp2(_group_sink(sink_ref, kv) - lse) * delta
            for g in range(GROUP):
                rows = slice(g * BLOCK_Q, (g + 1) * BLOCK_Q)
                dsink_ref[kv * GROUP + g] += jnp.broadcast_to(
                    jnp.sum(sink_part[rows], axis=0, keepdims=True), (1, HEAD_DIM))

        @pl.when(n == nblk - 1)
        def _():
            bucket_v = bucket_ref[...]
            row = lax.broadcasted_iota(jnp.int32, (N_BUCKETS, HEAD_DIM), 0)
            for h in range(N_HEADS_B):
                acc = dbias_ref[h * BLOCK_Q:(h + 1) * BLOCK_Q, :]
                tot = jnp.zeros((N_BUCKETS, HEAD_DIM), F32)
                for b in range(N_BUCKETS):
                    tot = jnp.where(row == b, jnp.sum(jnp.where(bucket_v == b, acc, 0.0), keepdims=True), tot)
                dtab_ref[h] = tot

    smem = pl.BlockSpec(memory_space=pltpu.SMEM)
    wide = GROUP * HEAD_DIM
    whole = pl.BlockSpec((sp, N_KV_B * HEAD_DIM), lambda n: (0, 0))
    group_b = pl.BlockSpec((BLOCK_Q, N_HEADS_B * HEAD_DIM), lambda n: (n, 1))
    res, c_res = _call(
        body, name="attn_b_bwd", grid=(nblk,),
        in_specs=[
            smem,
            smem,
            pl.BlockSpec((BLOCK_Q, wide), lambda n: (n, COL_QB // GROUP)),
            pl.BlockSpec((BLOCK_Q, wide), lambda n: (n, COL_QB // GROUP + 1)),
            whole,
            whole,
            group_b,
            group_b,
            pl.BlockSpec((N_HEADS_B, BLOCK_Q, HEAD_DIM), lambda n: (0, n, 0)),
            pl.BlockSpec((BLOCK_Q, 3 * BLOCK_Q), lambda n: (0, 0)),
        ],
        out_specs=[
            pl.BlockSpec((BLOCK_Q, N_HEADS_B * HEAD_DIM), lambda n: (n, 0)),
            whole,
            whole,
            pl.BlockSpec((N_HEADS_B, N_BUCKETS, HEAD_DIM), lambda n: (0, 0, 0)),
            pl.BlockSpec((N_HEADS_B, 1, HEAD_DIM), lambda n: (0, 0, 0)),
        ],
        out_shape=[
            jax.ShapeDtypeStruct((s, N_HEADS_B * HEAD_DIM), F32),
            jax.ShapeDtypeStruct((sp, N_KV_B * HEAD_DIM), F32),
            jax.ShapeDtypeStruct((sp, N_KV_B * HEAD_DIM), F32),
            jax.ShapeDtypeStruct((N_HEADS_B, N_BUCKETS, HEAD_DIM), F32),
            jax.ShapeDtypeStruct((N_HEADS_B, 1, HEAD_DIM), F32),
        ],
        scratch_shapes=[pltpu.VMEM((N_KV_B * GQ, 3 * BLOCK_Q), F32), pltpu.VMEM((N_KV_B * GQ, 3 * BLOCK_Q), F32)],
        sem=("arbitrary",),
        args=(table, sink, pb, pb, kpad, vpad, att, datt, lse, bucket), comm=comm, after=after)
    return res if comm is None else (res, c_res)


_MESH = pl.DeviceIdType.MESH


def _other_chips(x, y):
    return [(x, 1 - y), (1 - x, y), (1 - x, 1 - y)]


_HBM = pl.BlockSpec(memory_space=pltpu.HBM)
_SEM = pl.BlockSpec(memory_space=pltpu.SEMAPHORE)
_SPLIT = pltpu.CompilerParams(has_side_effects=pltpu.SideEffectType.DATAFLOW_SIDE_EFFECTING)


def _in_hbm(a):
    return pltpu.with_memory_space_constraint(a, pltpu.HBM)


def _my_half(rows):
    c = lax.axis_index("c")
    half = rows // 2
    return pl.ds(pl.multiple_of(c * half, half), half), pl.ds(pl.multiple_of((1 - c) * half, half), half)


def _gather_route(shapes):
    def route(src, land):
        x, y, c = lax.axis_index("x"), lax.axis_index("y"), lax.axis_index("c")
        out = []
        for t, shape in enumerate(shapes):
            mine, _ = _my_half(shape[0])
            for px, py in _other_chips(x, y):
                out.append((src[t].at[mine], land[t].at[2 * x + y, mine], land[t].at[2 * px + py, mine], (px, py, c)))
        return out

    return route


def _exchange_route(n_t):
    def route(src, land):
        x, y, c = lax.axis_index("x"), lax.axis_index("y"), lax.axis_index("c")
        out = []
        for t in range(n_t):
            for px, py in _other_chips(x, y):
                k = 2 * px + py
                out.append((src[t].at[k], land[t].at[2 * (2 * x + y) + c], land[t].at[2 * k + c], (px, py, c)))
        return out

    return route


def _own_slot(shape, dtype, slot, block):
    return lax.dynamic_update_slice(lax.empty(shape, dtype), block[None], (slot,) + (0,) * (len(shape) - 1))


def _split_start(name, srcs, lands, route, after):
    n = len(srcs)

    def body(*refs):
        src, land, send_sems, recv_sems, token = refs[:n], refs[n:2 * n], refs[2 * n + 1], refs[2 * n + 2], refs[-1]
        for i, (src_ref, dst_ref, _, to) in enumerate(route(src, land)):
            pltpu.make_async_remote_copy(src_ref=src_ref, dst_ref=dst_ref, send_sem=send_sems.at[i],
                                         recv_sem=recv_sems.at[i], device_id=to, device_id_type=_MESH).start()
        token[...] = jnp.zeros_like(token)

    sem = pltpu.SemaphoreType.DMA((3 * n,))
    lands = list(lands)
    res = pl.pallas_call(
        body, name=name,
        in_specs=[_HBM] * (2 * n) + [_ANY],
        out_specs=[_SEM, _SEM] + [_HBM] * (2 * n) + [pl.BlockSpec(memory_space=pltpu.VMEM)],
        out_shape=[sem, sem] + [pltpu.HBM(a.shape, a.dtype) for a in list(srcs) + lands]
        + [jax.ShapeDtypeStruct((8, 128), F32)],
        input_output_aliases={i: 2 + i for i in range(2 * n)},
        compiler_params=_SPLIT,
    )(*[_in_hbm(a) for a in srcs], *[_in_hbm(a) for a in lands], after)
    return (res[0], res[1]), res[2:2 + n], res[2 + n:2 + 2 * n], res[-1]


def _split_wait(name, srcs, lands, sems, route, after):
    n = len(srcs)

    def body(*refs):
        src, land, send_sems, recv_sems = refs[:n], refs[n:2 * n], refs[2 * n], refs[2 * n + 1]
        for i, (src_ref, _, dst_ref, to) in enumerate(route(src, land)):
            cp = pltpu.make_async_remote_copy(src_ref=src_ref, dst_ref=dst_ref, send_sem=send_sems.at[i],
                                              recv_sem=recv_sems.at[i], device_id=to, device_id_type=_MESH)
            cp.wait_send()
            cp.wait_recv()

    res = pl.pallas_call(
        body, name=name,
        in_specs=[_HBM] * (2 * n) + [_SEM, _SEM, _ANY],
        out_specs=[_HBM] * (2 * n),
        out_shape=[pltpu.HBM(a.shape, a.dtype) for a in list(srcs) + list(lands)],
        input_output_aliases={i: i for i in range(2 * n)},
        compiler_params=_SPLIT,
    )(*srcs, *lands, sems[0], sems[1], after)
    return res[:n], res[n:]


def _comm_only(name, comm):
    return _call(lambda: None, name=name, grid=(1,), in_specs=[], out_specs=[], out_shape=[], args=(), comm=comm)[1]


def _swap_comm(shards, lands):
    n_t = len(lands)

    def copies(land, sems, later):
        send_sems, recv_sems = sems
        x, y = lax.axis_index("x"), lax.axis_index("y")
        sends, recvs = [], []
        for t in range(n_t):
            mine, other = _my_half(shards[t].shape[0])
            for j, (px, py) in enumerate(_other_chips(x, y)):
                k = 2 * px + py
                for part, out in ((mine, sends), (other, recvs)) if later else ((mine, sends),):
                    out.append(pltpu.make_async_remote_copy(
                        src_ref=land[t].at[k, part], dst_ref=land[t].at[k, part], send_sem=send_sems.at[3 * t + j],
                        recv_sem=recv_sems.at[3 * t + j], device_id=_sibling(), device_id_type=_MESH))
        return sends, recvs

    def start(ins, land, sems):
        for cp in copies(land, sems, False)[0]:
            cp.start()

    def finish(ins, land, sems):
        sends, recvs = copies(land, sems, True)
        for cp in recvs:
            cp.wait_recv()
        for cp in sends:
            cp.wait_send()

    return _Comm(
        lands, [jax.ShapeDtypeStruct(a.shape, a.dtype) for a in lands],
        [pltpu.SemaphoreType.DMA((3 * n_t,)), pltpu.SemaphoreType.DMA((3 * n_t,))],
        start, finish, aliases={t: t for t in range(n_t)})


def _forward_comm(partials, lands):
    n_t = len(lands)

    def copies(land, sems, later):
        send_sems, recv_sems = sems
        x, y, c = lax.axis_index("x"), lax.axis_index("y"), lax.axis_index("c")
        sends, recvs = [], []
        for t in range(n_t):
            for j, k in enumerate([2 * x + y] + [2 * px + py for px, py in _other_chips(x, y)]):
                for slot, out in ((2 * k + c, sends), (2 * k + 1 - c, recvs)) if later else ((2 * k + c, sends),):
                    out.append(pltpu.make_async_remote_copy(
                        src_ref=land[t].at[slot], dst_ref=land[t].at[slot], send_sem=send_sems.at[4 * t + j],
                        recv_sem=recv_sems.at[4 * t + j], device_id=_sibling(), device_id_type=_MESH))
        return sends, recvs

    def start(ins, land, sems):
        for cp in copies(land, sems, False)[0]:
            cp.start()

    def finish(ins, land, sems):
        sends, recvs = copies(land, sems, True)
        for cp in recvs:
            cp.wait_recv()
        for cp in sends:
            cp.wait_send()

    return _Comm(
        lands, [jax.ShapeDtypeStruct(a.shape, a.dtype) for a in lands],
        [pltpu.SemaphoreType.DMA((4 * n_t,)), pltpu.SemaphoreType.DMA((4 * n_t,))],
        start, finish, aliases={t: t for t in range(n_t)})


def _allreduce_small(pack):
    rows, d = pack.shape

    def body(p_ref, sum_ref, all_ref, send_sems, recv_sems):
        x, y, c = lax.axis_index("x"), lax.axis_index("y"), lax.axis_index("c")
        me = 4 * x + 2 * y + c
        all_ref[me] = p_ref[...]
        peers = []
        for dx in range(2):
            for dy in range(2):
                for dc in range(2):
                    if dx or dy or dc:
                        px = 1 - x if dx else x
                        py = 1 - y if dy else y
                        pc = 1 - c if dc else c
                        peers.append((4 * dx + 2 * dy + dc - 1, (px, py, pc)))
        sends = []
        for k, to in peers:
            cp = pltpu.make_async_remote_copy(
                src_ref=p_ref, dst_ref=all_ref.at[me], send_sem=send_sems.at[k], recv_sem=recv_sems.at[k],
                device_id=to, device_id_type=_MESH)
            cp.start()
            sends.append(cp)
        for k, (px, py, pc) in peers:
            pltpu.make_async_remote_copy(
                src_ref=p_ref, dst_ref=all_ref.at[4 * px + 2 * py + pc], send_sem=send_sems.at[k],
                recv_sem=recv_sems.at[k], device_id=(px, py, pc), device_id_type=_MESH).wait_recv()
        for cp in sends:
            cp.wait_send()
        tot = all_ref[0]
        for i in range(1, N_DEV):
            tot = tot + all_ref[i]
        sum_ref[...] = tot

    vm = pl.BlockSpec(memory_space=pltpu.VMEM)
    return pl.pallas_call(
        body,
        name="allreduce_small",
        in_specs=[vm],
        out_specs=vm,
        out_shape=jax.ShapeDtypeStruct((rows, d), F32),
        scratch_shapes=[
            pltpu.VMEM((N_DEV, rows, d), F32),
            pltpu.SemaphoreType.DMA((N_DEV - 1,)),
            pltpu.SemaphoreType.DMA((N_DEV - 1,)),
        ],
    )(pack)


def _adamw_math(w, g, m, v):
    m = ADAM_B1 * m + (1.0 - ADAM_B1) * g
    v = ADAM_B2 * v + (1.0 - ADAM_B2) * (g * g)
    m_hat = m / (1.0 - ADAM_B1 ** ADAM_STEP)
    v_hat = v / (1.0 - ADAM_B2 ** ADAM_STEP)
    delta = -ADAM_LR * (m_hat / (jnp.sqrt(v_hat) + ADAM_EPS) + ADAM_WD * w)
    return delta, m, v


def _sum_adamw(parts, w, m, v, *, name, tr=256):
    r, c = w.shape
    tr = min(tr, r)
    tc = min(c, 1024)

    def body(p_ref, w_ref, m_ref, v_ref, g_ref, d_ref, m2_ref, v2_ref):
        g = p_ref[0].astype(F32)
        for i in range(1, N_DEV):
            g = g + p_ref[i].astype(F32)
        delta, m2, v2 = _adamw_math(w_ref[...], g, m_ref[...], v_ref[...])
        g_ref[...] = g
        d_ref[...] = delta
        m2_ref[...] = m2
        v2_ref[...] = v2

    blk = pl.BlockSpec((tr, tc), lambda i, j: (i, j))
    return pl.pallas_call(
        body,
        name=name,
        grid=(r // tr, c // tc),
        in_specs=[pl.BlockSpec((N_DEV, tr, tc), lambda i, j: (0, i, j)), blk, blk, blk],
        out_specs=[blk] * 4,
        out_shape=[jax.ShapeDtypeStruct((r, c), F32)] * 4,
        compiler_params=_params(("parallel", "parallel")),
    )(parts, w, m, v)


def _adamw_small(g, w, m, v):
    def body(g_ref, w_ref, m_ref, v_ref, d_ref, m2_ref, v2_ref):
        delta, m2, v2 = _adamw_math(w_ref[...], g_ref[...], m_ref[...], v_ref[...])
        d_ref[...] = delta
        m2_ref[...] = m2
        v2_ref[...] = v2

    vm = pl.BlockSpec(memory_space=pltpu.VMEM)
    return pl.pallas_call(
        body,
        name="adamw_small",
        in_specs=[vm] * 4,
        out_specs=[vm] * 3,
        out_shape=[jax.ShapeDtypeStruct(g.shape, F32)] * 3,
    )(g, w, m, v)


def _relu2_epilogue(acc):
    ra = jnp.maximum(acc, 0.0)
    return ra * ra, ra


def _rows(stacked):
    return stacked.reshape(stacked.shape[0] * stacked.shape[1], stacked.shape[2])


def _by_chip(mat):
    return mat.reshape(N_CHIPS, mat.shape[0] // N_CHIPS, mat.shape[1])


def _local_step(x, p, target, shards, small, update):
    s, d = x.shape
    cos_t, sin_t = _rope_tables(s)
    bucket = _band_buckets()
    p_bf = p.astype(BF16)
    wts = {}

    chip = 2 * lax.axis_index("x") + lax.axis_index("y")
    core = lax.axis_index("c")

    def gather(tag, names, after):
        srcs = [cast[n] for n in names]
        route = _gather_route([a.shape for a in srcs])
        sems, srcs, lands, token = _split_start(f"gather_start_{tag}", srcs, [zones[n] for n in names], route, after)

        def landed(done):
            got_srcs, got_lands = _split_wait(f"gather_wait_{tag}", srcs, lands, sems, route, done)
            comm = _swap_comm(got_srcs, got_lands)
            comm.waited = got_srcs[0]
            return comm

        return landed, token

    def prepare(n, zero):
        cast[n] = (shards[n] + zero).astype(BF16)
        zones[n] = _own_slot((N_CHIPS,) + cast[n].shape, BF16, chip, cast[n])

    cast, zones = {}, {}
    prepare("w_in", 0.0)
    in_landed, token = gather("in", ["w_in"], small["attn_norm_g"])
    for n in shards:
        if n != "w_in":
            prepare(n, token[:1, :1])
    g_attn = small["attn_norm_g"] + token[:1, :1]
    u = _rms_fwd(x, g_attn, name="norm_attn")
    prepared = u[:1, :1].astype(F32) + sum(
        (lax.dynamic_slice(zones[n], (chip, 0, 0), (1, 1, 1))[0] + cast[n][:1, :1]).astype(F32)
        for n in zones if n != "w_in")
    (wts["w_in"],) = _comm_only("swap_w_in", in_landed(prepared))
    mid_landed, token = gather("mid", ["w_out"], wts["w_in"])
    proj = _matmul(u, wts["w_in"], mode="nn", out_dtypes=[F32], name="mm_in", bn=768, after=token)
    pb, (w_out_s,) = _qk_prep(proj, small["q_norm_g"], small["k_norm_g"], cos_t, sin_t, comm=mid_landed(proj))
    wts["w_out"] = _rows(w_out_s)
    up_landed, token = gather("up", ["w_up"], pb)
    att_a, lse_a = _attn_a_fwd(pb, after=token)
    pad = ((PAD_LO, PAD_HI), (0, 0))
    kpad = jnp.pad(pb[:, COL_KB * HEAD_DIM:COL_VB * HEAD_DIM], pad)
    vpad = jnp.pad(pb[:, COL_VB * HEAD_DIM:], pad)
    up_swap = up_landed(att_a)
    down_landed, token = gather("down", ["w_down"], up_swap.waited)
    (att, lse_b), (wts["w_up"],) = _attn_b_fwd(pb, kpad, vpad, bucket, small["rel_bias_table"],
                                               small["sink_logits"], att_a, comm=up_swap, after=token)
    h1 = _matmul(att, wts["w_out"], mode="nn", out_dtypes=[F32], name="mm_out",
                 epilogue=lambda acc, res: (acc + res,), extras=(x,))
    mn = _rms_fwd(h1, small["mlp_norm_g"], name="norm_mlp")
    r, ra = _matmul(mn, wts["w_up"], mode="nn", out_dtypes=[BF16, BF16], name="mm_up", epilogue=_relu2_epilogue,
                    bm=2048)
    (w_down_s,) = _comm_only("swap_w_down", down_landed(r))
    wts["w_down"] = _rows(w_down_s)
    late_landed, token = gather("late", ["w_gate", "ple_w"], w_down_s)
    h2 = _matmul(r, wts["w_down"], mode="nn", out_dtypes=[F32], name="mm_down",
                 epilogue=lambda acc, res: (acc + res,), extras=(h1,), after=token)
    ng, (w_gate_s, wts["ple_w"]) = _rms_fwd(h2, small["gate_norm_g"], name="norm_gate", comm=late_landed(h2))
    wts["w_gate"] = _rows(w_gate_s)
    gate = _matmul(ng, wts["w_gate"], mode="nn", out_dtypes=[F32], name="mm_gate", bm=2048,
                   epilogue=lambda acc: (1.0 / (1.0 + jnp.exp(-acc)),))
    pp = _matmul(p_bf, wts["ple_w"], mode="nn", out_dtypes=[F32], name="mm_ple", bn=512)
    dh3, dz, dpp, dg_final, dg_ple, loss = _tail(h2, gate, pp, target, small["ple_norm_g"], small["final_norm_g"])

    dng = _matmul(dz, wts["w_gate"], mode="nt", out_dtypes=[F32], name="mm_gate_dx", bm=2048)
    gw_gate = _matmul(ng, dz, mode="tn", out_dtypes=[BF16], name="mm_gate_dw")
    gw_ple = _matmul(p_bf, dpp, mode="tn", out_dtypes=[BF16], name="mm_ple_dw", bn=512, out_stack=N_CHIPS)
    dh2, dh2_bf, dg_gate = _rms_bwd(h2, dng, small["gate_norm_g"], dh3, name="norm_gate_bwd", want_bf16=True)

    def exchange(tag, partials, after):
        route = _exchange_route(len(partials))
        lands = [_own_slot((N_DEV,) + g.shape[1:], g.dtype, 2 * chip + core,
                           lax.dynamic_index_in_dim(g, chip, 0, keepdims=False)) for g in partials]
        sems, srcs, lands, token = _split_start(f"exchange_start_{tag}", partials, lands, route, after)

        def landed(done):
            got_srcs, got_lands = _split_wait(f"exchange_wait_{tag}", srcs, lands, sems, route, done)
            comm = _forward_comm(got_srcs, got_lands)
            comm.waited = got_srcs[0]
            return comm

        return landed, token

    big = {}
    gate_landed, token = exchange("gate", [_by_chip(gw_gate), gw_ple], dh2_bf)
    gw_down = _matmul(r, dh2_bf, mode="tn", out_dtypes=[BF16], name="mm_down_dw", after=token)
    da, (parts_gate, parts_ple) = _matmul(
        dh2_bf, wts["w_down"], mode="nt", out_dtypes=[BF16], name="mm_down_dx", bm=2048,
        epilogue=lambda acc, ra_v: (acc * (2.0 * ra_v.astype(F32)),), extras=(ra,), comm=gate_landed(gw_down))
    down_landed, token = exchange("down", [_by_chip(gw_down)], da)
    big["w_gate"], big["ple_w"] = update("w_gate", parts_gate), update("ple_w", parts_ple)
    gw_up = _matmul(mn, da, mode="tn", out_dtypes=[BF16], name="mm_up_dw", out_stack=N_CHIPS, after=token)
    dmn = _matmul(da, wts["w_up"], mode="nt", out_dtypes=[F32], name="mm_up_dx")
    dh1, dh1_bf, dg_mlp = _rms_bwd(h1, dmn, small["mlp_norm_g"], dh2, name="norm_mlp_bwd", want_bf16=True)
    datt, (parts_down,) = _matmul(dh1_bf, wts["w_out"], mode="nt", out_dtypes=[BF16], name="mm_out_dx",
                                  bm=2048, comm=down_landed(dh1_bf))
    gw_out = _matmul(att, dh1_bf, mode="tn", out_dtypes=[BF16], name="mm_out_dw")
    up_landed, token = exchange("up", [gw_up], datt)
    dqb, dkpad, dvpad, dtab, dsink = _attn_b_bwd(pb, kpad, vpad, att, datt, lse_b, bucket,
                                                 small["rel_bias_table"], small["sink_logits"], after=token)
    dqa, dka, dva = _attn_a_bwd(pb, att, datt, lse_a)
    up_forward = up_landed(dqa)
    out_landed, token = exchange("out", [_by_chip(gw_out)], up_forward.waited)
    (dproj, dg_q, dg_k), (parts_up,) = _qk_bwd(dqa, dka, dva, dqb, dkpad, dvpad, proj,
                                               small["q_norm_g"], small["k_norm_g"], cos_t, sin_t,
                                               comm=up_forward, after=token)
    gw_in = _matmul(u, dproj, mode="tn", out_dtypes=[BF16], name="mm_in_dw", bn=768, out_stack=N_CHIPS)
    out_forward = out_landed(gw_in)
    in_landed, token = exchange("in", [gw_in], out_forward.waited)
    du, (parts_out,) = _matmul(dproj, wts["w_in"], mode="nt", out_dtypes=[F32], name="mm_in_dx", bk=3072,
                               comm=out_forward, after=token)
    grad_x, dg_attn = _rms_bwd(x, du, small["attn_norm_g"], dh1, name="norm_attn_bwd", want_bf16=False)
    for n, parts in (("w_down", parts_down), ("w_up", parts_up), ("w_out", parts_out)):
        big[n] = update(n, parts)
    done = dg_attn + sum(big[n][0][0, :1, :] for n in ("w_down", "w_up", "w_out"))
    (parts_in,) = _comm_only("forward_w_in", in_landed(done))
    big["w_in"] = update("w_in", parts_in)

    small_g = {
        "attn_norm_g": dg_attn, "mlp_norm_g": dg_mlp, "ple_norm_g": dg_ple, "gate_norm_g": dg_gate,
        "final_norm_g": dg_final, "q_norm_g": dg_q, "k_norm_g": dg_k,
        "sink_logits": dsink[:, 0, 0][None, :], "rel_bias_table": dtab[:, :, 0].T,
    }
    return loss, grad_x, big, small_g


_SMALL_ROWS = ["attn_norm_g", "mlp_norm_g", "ple_norm_g", "gate_norm_g", "final_norm_g"]
_PACK_ROWS = 8


def _pack_small(vals, d):
    rows = [vals[n].reshape(1, d) for n in _SMALL_ROWS]
    misc = jnp.concatenate([
        vals["q_norm_g"].reshape(1, HEAD_DIM), vals["k_norm_g"].reshape(1, HEAD_DIM),
        jnp.pad(vals["sink_logits"].reshape(1, N_HEADS_B), ((0, 0), (0, HEAD_DIM - N_HEADS_B))),
        vals["rel_bias_table"].reshape(1, N_BUCKETS * N_HEADS_B)], axis=1)
    rows.append(jnp.pad(misc, ((0, 0), (0, d - misc.shape[1]))))
    rows.append(jnp.zeros((_PACK_ROWS - len(rows), d), F32))
    return jnp.concatenate(rows, axis=0).astype(F32)


def _unpack_small(pack, shapes):
    out = {n: pack[i].reshape(shapes[n]) for i, n in enumerate(_SMALL_ROWS)}
    misc = pack[len(_SMALL_ROWS)]
    out["q_norm_g"] = misc[:HEAD_DIM].reshape(shapes["q_norm_g"])
    out["k_norm_g"] = misc[HEAD_DIM:2 * HEAD_DIM].reshape(shapes["k_norm_g"])
    out["sink_logits"] = misc[2 * HEAD_DIM:2 * HEAD_DIM + N_HEADS_B].reshape(shapes["sink_logits"])
    out["rel_bias_table"] = misc[3 * HEAD_DIM:3 * HEAD_DIM + N_BUCKETS * N_HEADS_B].reshape(shapes["rel_bias_table"])
    return out


_WEIGHTS = ["attn_norm_g", "w_in", "q_norm_g", "k_norm_g", "sink_logits", "w_out", "mlp_norm_g", "w_up", "w_down",
            "ple_w", "ple_norm_g", "gate_norm_g", "w_gate", "rel_bias_table", "final_norm_g"]
_BIG = ["w_in", "w_out", "w_up", "w_down", "ple_w", "w_gate"]


def kernel(x, p, attn_norm_g, w_in, q_norm_g, k_norm_g, sink_logits, w_out, mlp_norm_g, w_up, w_down, ple_w, ple_norm_g, gate_norm_g, w_gate, rel_bias_table, final_norm_g, loss_target, m_attn_norm_g, m_w_in, m_q_norm_g, m_k_norm_g, m_sink_logits, m_w_out, m_mlp_norm_g, m_w_up, m_w_down, m_ple_w, m_ple_norm_g, m_gate_norm_g, m_w_gate, m_rel_bias_table, m_final_norm_g, v_attn_norm_g, v_w_in, v_q_norm_g, v_k_norm_g, v_sink_logits, v_w_out, v_mlp_norm_g, v_w_up, v_w_down, v_ple_w, v_ple_norm_g, v_gate_norm_g, v_w_gate, v_rel_bias_table, v_final_norm_g):
    given = dict(locals())
    w = {n: given[n] for n in _WEIGHTS}
    m = {n: given["m_" + n] for n in _WEIGHTS}
    v = {n: given["v_" + n] for n in _WEIGHTS}
    d = x.shape[-1]

    shards = {n: w[n][0] for n in _BIG}
    small = {
        "attn_norm_g": w["attn_norm_g"], "mlp_norm_g": w["mlp_norm_g"], "ple_norm_g": w["ple_norm_g"],
        "gate_norm_g": w["gate_norm_g"], "final_norm_g": w["final_norm_g"].reshape(1, d),
        "q_norm_g": w["q_norm_g"], "k_norm_g": w["k_norm_g"], "sink_logits": w["sink_logits"],
        "rel_bias_table": w["rel_bias_table"],
    }

    def update(n, parts):
        res = _sum_adamw(parts, w[n][0], m[n][0], v[n][0], name="adamw_" + n)
        return [t.reshape(w[n].shape) for t in res]

    loss_part, grad_x, big, small_g = _local_step(x[0], p[0, 0], loss_target[0], shards, small, update)
    grads, deltas, new_m, new_v = [{n: big[n][i] for n in _BIG} for i in range(4)]

    shapes = {n: w[n].shape for n in _WEIGHTS if n not in _BIG}
    pack = _pack_small(small_g, d)
    pack = pack.at[_PACK_ROWS - 1, :1].add(0.0 * grads["w_in"][0, 0, :1])
    pack = pack.at[_PACK_ROWS - 1, 1].set(loss_part[0, 0])
    g_small = _allreduce_small(pack)
    loss = g_small[_PACK_ROWS - 1, 1]
    d_small, m_small, v_small = _adamw_small(g_small, _pack_small(w, d), _pack_small(m, d), _pack_small(v, d))
    grads.update(_unpack_small(g_small, shapes))
    deltas.update(_unpack_small(d_small, shapes))
    new_m.update(_unpack_small(m_small, shapes))
    new_v.update(_unpack_small(v_small, shapes))

    return (loss, grad_x[None], *[grads[n] for n in _WEIGHTS], *[deltas[n] for n in _WEIGHTS],
            *[new_m[n] for n in _WEIGHTS], *[new_v[n] for n in _WEIGHTS])
```

```python
import functools
import math

import jax
import jax.numpy as jnp
import numpy as np
from jax import lax
from jax.experimental import pallas as pl
from jax.experimental.pallas import tpu as pltpu

F32 = jnp.float32
BF16 = jnp.bfloat16

HEAD_DIM = 128
N_HEADS_A = 8
N_KV_A = 2
N_HEADS_B = 8
N_KV_B = 2
GROUP = 4
GRID_W = 64
BLOCK_Q = 128
WINDOW = 128
N_BUCKETS = 32
MAX_DISTANCE = 128
ROPE_THETA = 10000.0
EPS = 1e-6
NEG_INF = -1e30
ATT_SCALE = HEAD_DIM ** -0.5
LOG2E = math.log2(math.e)
LN2 = math.log(2.0)
Q_SCALE = ATT_SCALE * LOG2E
PAD_LO, PAD_HI = 256, 128

ADAM_LR = 0.001
ADAM_B1 = 0.9
ADAM_B2 = 0.999
ADAM_EPS = 1e-08
ADAM_WD = 0.01
ADAM_STEP = 10

N_CHIPS = 4
N_DEV = 8
COL_QA, COL_KA, COL_VA, COL_QB, COL_KB, COL_VB = 0, 8, 10, 12, 20, 22
N_COLS = 24

VMEM_LIMIT = 52 * 1024 * 1024


def _params(sem=None, collective_id=None):
    return pltpu.CompilerParams(dimension_semantics=sem, vmem_limit_bytes=VMEM_LIMIT, collective_id=collective_id)


_ANY = pl.BlockSpec(memory_space=pl.ANY)
_MESH = pl.DeviceIdType.MESH
SIBLING_BARRIER_ID = 1


def _sibling():
    return (lax.axis_index("x"), lax.axis_index("y"), 1 - lax.axis_index("c"))


class _Comm:
    def __init__(self, inputs, out_shapes, sems, start, finish, aliases=None):
        self.inputs, self.out_shapes, self.sems = list(inputs), list(out_shapes), list(sems)
        self.start, self.finish, self.aliases = start, finish, dict(aliases or {})


def _call(body, *, name, grid, in_specs, out_specs, out_shape, args, scratch_shapes=(), sem=None, comm=None,
          after=None, aliases=None):
    in_specs, out_specs, out_shape = list(in_specs), list(out_specs), list(out_shape)
    scratch_shapes = list(scratch_shapes)
    n_in, n_out, n_sc = len(in_specs), len(out_specs), len(scratch_shapes)
    behind = [] if after is None else [after]
    aliases = dict(aliases or {})
    if comm is None:
        res = pl.pallas_call(
            (lambda *refs: body(*refs[:n_in], *refs[n_in + len(behind):])) if behind else body,
            name=name, grid=grid, in_specs=in_specs + [_ANY] * len(behind), out_specs=out_specs,
            out_shape=out_shape, scratch_shapes=scratch_shapes, input_output_aliases=aliases,
            compiler_params=_params(sem))(*args, *behind)
        return list(res), []
    c_in, c_out = len(comm.inputs), len(comm.out_shapes)

    def hosted(*refs):
        pos = [0]

        def take(n):
            pos[0] += n
            return refs[pos[0] - n:pos[0]]

        ins, c_ins, _, outs, c_outs, scr = (take(n_in), take(c_in), take(len(behind)), take(n_out), take(c_out),
                                            take(n_sc))
        c_sems = refs[pos[0]:]
        ids = [pl.program_id(a) for a in range(len(grid))]
        first = functools.reduce(jnp.logical_and, [i == 0 for i in ids])
        last = functools.reduce(jnp.logical_and, [i == g - 1 for i, g in zip(ids, grid)])

        @pl.when(first)
        def _():
            barrier = pltpu.get_barrier_semaphore()
            pl.semaphore_signal(barrier, inc=1, device_id=_sibling(), device_id_type=_MESH)
            pl.semaphore_wait(barrier, 1)
            comm.start(c_ins, c_outs, c_sems)

        body(*ins, *outs, *scr)

        @pl.when(last)
        def _():
            comm.finish(c_ins, c_outs, c_sems)

    res = pl.pallas_call(
        hosted, name=name, grid=grid, in_specs=in_specs + [_ANY] * (c_in + len(behind)),
        out_specs=out_specs + [_ANY] * c_out,
        out_shape=out_shape + comm.out_shapes, scratch_shapes=scratch_shapes + comm.sems,
        input_output_aliases={**aliases, **{n_in + i: n_out + o for i, o in comm.aliases.items()}},
        compiler_params=_params(("arbitrary",) * len(grid), SIBLING_BARRIER_ID))(*args, *comm.inputs, *behind)
    return list(res[:n_out]), list(res[n_out:])


def _matmul(a, b, *, mode, out_dtypes, name, epilogue=None, extras=(), bm=1024, bn=1024, bk=2048,
            out_stack=0, comm=None, after=None):
    stacked = b.ndim == 3
    if mode == "nn":
        m, k = a.shape
        if stacked:
            nj, kb, ns = b.shape
            n, ks = nj * ns, k
        else:
            kb, n = b.shape
            ns, ks = n, k
        dn = (((1,), (0,)), ((), ()))
    elif mode == "nt":
        m, k = a.shape
        if stacked:
            nj, n, ks = b.shape
            kb = nj * ks
        else:
            n, kb = b.shape
            ks = kb
        ns = n
        dn = (((1,), (1,)), ((), ()))
    else:
        k, m = a.shape
        kb, n = b.shape
        ns, ks = n, k
        dn = (((0,), (0,)), ((), ()))
    assert k == kb and not (stacked and mode == "tn")
    ns_out = n // out_stack if out_stack else n
    whole_k = stacked and mode == "nt" and bk >= k
    bm, bn, bk = min(bm, m), min(bn, ns, ns_out), k if whole_k else min(bk, ks)
    assert m % bm == 0 and ns % bn == 0 and ns_out % bn == 0 and ks % bk == 0 or whole_k
    gm, gn, gk = m // bm, n // bn, k // bk

    if mode == "tn":
        a_spec = pl.BlockSpec((bk, bm), lambda i, j, q: (q, i))
    else:
        a_spec = pl.BlockSpec((bm, bk), lambda i, j, q: (i, q))
    if mode == "nt":
        if whole_k:
            b_spec = pl.BlockSpec((nj, bn, ks), lambda i, j, q: (0, j, 0))
        elif stacked:
            per = ks // bk
            b_spec = pl.BlockSpec((None, bn, bk), lambda i, j, q: (q // per, j, q % per))
        else:
            b_spec = pl.BlockSpec((bn, bk), lambda i, j, q: (j, q))
    else:
        if stacked:
            per = ns // bn
            b_spec = pl.BlockSpec((None, bk, bn), lambda i, j, q: (j // per, q, j % per))
        else:
            b_spec = pl.BlockSpec((bk, bn), lambda i, j, q: (q, j))
    ex_spec = pl.BlockSpec((bm, bn), lambda i, j, q: (i, j))
    if out_stack:
        per_o = ns_out // bn
        o_spec = pl.BlockSpec((None, bm, bn), lambda i, j, q: (j // per_o, i, j % per_o))
        o_shape = (out_stack, m, ns_out)
    else:
        o_spec = ex_spec
        o_shape = (m, n)
    n_ex, n_out = len(extras), len(out_dtypes)

    def body(a_ref, b_ref, *rest):
        ex, outs = rest[:n_ex], rest[n_ex:n_ex + n_out]
        if whole_k:
            part = sum(lax.dot_general(a_ref[:, t * ks:(t + 1) * ks], b_ref[t], dn, preferred_element_type=F32)
                       for t in range(nj))
        else:
            part = lax.dot_general(a_ref[...], b_ref[...], dn, preferred_element_type=F32)

        def finish(acc):
            res = epilogue(acc, *[e[...] for e in ex]) if epilogue else (acc,)
            for o, r in zip(outs, res):
                o[...] = r.astype(o.dtype)

        if gk == 1:
            finish(part)
        else:
            acc_ref = rest[-1]
            q = pl.program_id(2)

            @pl.when(q == 0)
            def _():
                acc_ref[...] = part

            @pl.when(q > 0)
            def _():
                acc_ref[...] += part

            @pl.when(q == gk - 1)
            def _():
                finish(acc_ref[...])

    res, c_res = _call(
        body, name=name, grid=(gm, gn, gk),
        in_specs=[a_spec, b_spec] + [ex_spec] * n_ex,
        out_specs=[o_spec] * n_out,
        out_shape=[jax.ShapeDtypeStruct(o_shape, dt) for dt in out_dtypes],
        scratch_shapes=[pltpu.VMEM((bm, bn), F32)] if gk > 1 else [],
        sem=("parallel", "parallel", "arbitrary"), args=(a, b, *extras), comm=comm, after=after)
    res = res[0] if n_out == 1 else res
    return res if comm is None else (res, c_res)


def _rms_fwd(x, g, *, name, tm=256, comm=None):
    s, d = x.shape
    tm = min(tm, s)

    def body(x_ref, g_ref, o_ref):
        xf = x_ref[...]
        r = lax.rsqrt(jnp.mean(xf * xf, axis=-1, keepdims=True) + EPS)
        o_ref[...] = (xf * r * g_ref[...]).astype(o_ref.dtype)

    res, c_res = _call(
        body, name=name, grid=(s // tm,),
        in_specs=[pl.BlockSpec((tm, d), lambda i: (i, 0)), pl.BlockSpec((1, d), lambda i: (0, 0))],
        out_specs=[pl.BlockSpec((tm, d), lambda i: (i, 0))],
        out_shape=[jax.ShapeDtypeStruct((s, d), BF16)],
        sem=("parallel",), args=(x, g), comm=comm)
    return res[0] if comm is None else (res[0], c_res)


def _rms_bwd(x, dy, g, add, *, name, want_bf16, tm=256):
    s, d = x.shape
    tm = min(tm, s)

    def body(x_ref, dy_ref, g_ref, add_ref, dx_ref, *rest):
        dg_ref = rest[-1]
        i = pl.program_id(0)
        xf = x_ref[...]
        dyf = dy_ref[...].astype(F32)
        r = lax.rsqrt(jnp.mean(xf * xf, axis=-1, keepdims=True) + EPS)
        xh = xf * r
        dyg = dyf * g_ref[...]
        dx = r * (dyg - xh * jnp.mean(dyg * xh, axis=-1, keepdims=True))
        tot = add_ref[...] + dx
        dx_ref[...] = tot
        if want_bf16:
            rest[0][...] = tot.astype(BF16)
        part = jnp.sum(dyf * xh, axis=0, keepdims=True)

        @pl.when(i == 0)
        def _():
            dg_ref[...] = part

        @pl.when(i > 0)
        def _():
            dg_ref[...] += part

    row = pl.BlockSpec((tm, d), lambda i: (i, 0))
    vec = pl.BlockSpec((1, d), lambda i: (0, 0))
    out_specs = [row] + ([row] if want_bf16 else []) + [vec]
    out_shape = [jax.ShapeDtypeStruct((s, d), F32)]
    if want_bf16:
        out_shape.append(jax.ShapeDtypeStruct((s, d), BF16))
    out_shape.append(jax.ShapeDtypeStruct((1, d), F32))
    return pl.pallas_call(
        body,
        name=name,
        grid=(s // tm,),
        in_specs=[row, row, vec, row],
        out_specs=out_specs,
        out_shape=out_shape,
        compiler_params=_params(("arbitrary",)),
    )(x, dy, g, add)


def _tail(h2, gate, pp, target, g_ple, g_final, *, tm=128):
    s, d = h2.shape
    tm = min(tm, s)

    def body(h2_ref, gate_ref, pp_ref, t_ref, gp_ref, gf_ref, dh3_ref, dz_ref, dpp_ref, dgf_ref, dgp_ref, loss_ref):
        i = pl.program_id(0)
        ppf = pp_ref[...]
        gate_v = gate_ref[...]
        r_p = lax.rsqrt(jnp.mean(ppf * ppf, axis=-1, keepdims=True) + EPS)
        eh = ppf * r_p
        e = eh * gp_ref[...]
        h3 = h2_ref[...] + gate_v * e
        r_f = lax.rsqrt(jnp.mean(h3 * h3, axis=-1, keepdims=True) + EPS)
        yh = h3 * r_f
        diff = yh * gf_ref[...] - t_ref[...]
        loss_part = 0.5 * jnp.sum(jnp.mean(diff * diff, axis=-1, keepdims=True), axis=0, keepdims=True)
        dy = diff / d
        dgf = jnp.sum(dy * yh, axis=0, keepdims=True)
        dyg = dy * gf_ref[...]
        dh3 = r_f * (dyg - yh * jnp.mean(dyg * yh, axis=-1, keepdims=True))
        dh3_ref[...] = dh3
        de = dh3 * gate_v
        dz_ref[...] = (dh3 * e * gate_v * (1.0 - gate_v)).astype(BF16)
        dgp = jnp.sum(de * eh, axis=0, keepdims=True)
        deg = de * gp_ref[...]
        dpp_ref[...] = (r_p * (deg - eh * jnp.mean(deg * eh, axis=-1, keepdims=True))).astype(BF16)
        loss_row = jnp.broadcast_to(loss_part, (1, 128))

        @pl.when(i == 0)
        def _():
            dgf_ref[...] = dgf
            dgp_ref[...] = dgp
            loss_ref[...] = loss_row

        @pl.when(i > 0)
        def _():
            dgf_ref[...] += dgf
            dgp_ref[...] += dgp
            loss_ref[...] += loss_row

    row = pl.BlockSpec((tm, d), lambda i: (i, 0))
    vec = pl.BlockSpec((1, d), lambda i: (0, 0))
    return pl.pallas_call(
        body,
        name="tail_fwd_bwd",
        grid=(s // tm,),
        in_specs=[row, row, row, row, vec, vec],
        out_specs=[row, row, row, vec, vec, pl.BlockSpec((1, 128), lambda i: (0, 0))],
        out_shape=[
            jax.ShapeDtypeStruct((s, d), F32),
            jax.ShapeDtypeStruct((s, d), BF16),
            jax.ShapeDtypeStruct((s, d), BF16),
            jax.ShapeDtypeStruct((1, d), F32),
            jax.ShapeDtypeStruct((1, d), F32),
            jax.ShapeDtypeStruct((1, 128), F32),
        ],
        compiler_params=_params(("arbitrary",)),
    )(h2, gate, pp, target, g_ple, g_final)


def _rope_tables(s):
    rows = s // GRID_W
    half = HEAD_DIM // 2
    inv_freq = ROPE_THETA ** (-jnp.arange(0, half, 2, dtype=F32) / half)
    ang_r = jnp.arange(rows, dtype=jnp.int32).astype(F32)[:, None] * inv_freq
    ang_c = jnp.arange(GRID_W, dtype=jnp.int32).astype(F32)[:, None] * inv_freq
    cr, sr = (jnp.repeat(t, GRID_W, axis=0) for t in (jnp.cos(ang_r), jnp.sin(ang_r)))
    cc, sc = (jnp.tile(t, (rows, 1)) for t in (jnp.cos(ang_c), jnp.sin(ang_c)))
    cos_t = jnp.concatenate([cr, cr, cc, cc], axis=-1)
    sin_t = jnp.concatenate([-sr, sr, -sc, sc], axis=-1)
    return cos_t, sin_t


def _swap_quarters(x):
    lane = lax.broadcasted_iota(jnp.int32, x.shape, x.ndim - 1)
    up = pltpu.roll(x, HEAD_DIM - 32, x.ndim - 1)
    down = pltpu.roll(x, 32, x.ndim - 1)
    return jnp.where((lane % 64) < 32, up, down)


def _cols(first, count=1):
    return slice(first * HEAD_DIM, (first + count) * HEAD_DIM)


def _qk_prep(proj, g_q, g_k, cos_t, sin_t, *, tm=256, comm=None):
    s, n = proj.shape
    tm = min(tm, s)

    def body(x_ref, gq_ref, gk_ref, c_ref, s_ref, o_ref):
        cos_v, sin_v = c_ref[...], s_ref[...]
        for h in range(COL_VA):
            x = x_ref[:, _cols(h)]
            g = gq_ref[...] if h < COL_KA else gk_ref[...]
            xn = x * lax.rsqrt(jnp.mean(x * x, axis=-1, keepdims=True) + EPS) * g
            xr = xn * cos_v + _swap_quarters(xn) * sin_v
            if h < COL_KA:
                xr = xr * Q_SCALE
            o_ref[:, _cols(h)] = xr.astype(BF16)
        o_ref[:, _cols(COL_VA, 2)] = x_ref[:, _cols(COL_VA, 2)].astype(BF16)
        o_ref[:, _cols(COL_QB, N_HEADS_B)] = (x_ref[:, _cols(COL_QB, N_HEADS_B)] * Q_SCALE).astype(BF16)
        o_ref[:, _cols(COL_KB, 4)] = x_ref[:, _cols(COL_KB, 4)].astype(BF16)

    row = pl.BlockSpec((tm, n), lambda i: (i, 0))
    tab = pl.BlockSpec((tm, HEAD_DIM), lambda i: (i, 0))
    vec = pl.BlockSpec((1, HEAD_DIM), lambda i: (0, 0))
    res, c_res = _call(
        body, name="qk_prep", grid=(s // tm,),
        in_specs=[row, vec, vec, tab, tab],
        out_specs=[row],
        out_shape=[jax.ShapeDtypeStruct((s, n), BF16)],
        sem=("parallel",), args=(proj, g_q, g_k, cos_t, sin_t), comm=comm)
    return res[0] if comm is None else (res[0], c_res)


def _qk_bwd(dqa, dka, dva, dqb, dkpad, dvpad, proj, g_q, g_k, cos_t, sin_t, *, comm=None, after=None):
    s, n = proj.shape
    tm = min(PAD_LO, s)
    assert PAD_LO % tm == 0
    lo = PAD_LO // tm

    def body(dqa_ref, dka_ref, dva_ref, dqb_ref, dkb_ref, dvb_ref, x_ref, gq_ref, gk_ref, c_ref, s_ref,
             o_ref, dgq_ref, dgk_ref):
        i = pl.program_id(0)
        cos_v, sin_v = c_ref[...], s_ref[...]

        def head(d, x, g):
            dn = d * cos_v + _swap_quarters(d * sin_v)
            r = lax.rsqrt(jnp.mean(x * x, axis=-1, keepdims=True) + EPS)
            xh = x * r
            dng = dn * g
            dx = r * (dng - xh * jnp.mean(dng * xh, axis=-1, keepdims=True))
            return dx.astype(BF16), jnp.sum(dn * xh, axis=0, keepdims=True)

        acc_q = jnp.zeros((1, HEAD_DIM), F32)
        acc_k = jnp.zeros((1, HEAD_DIM), F32)
        for h in range(N_HEADS_A):
            o_ref[:, _cols(h)], part = head(dqa_ref[:, _cols(h)] * ATT_SCALE, x_ref[:, _cols(h)], gq_ref[...])
            acc_q = acc_q + part
        for h in range(N_KV_A):
            o_ref[:, _cols(COL_KA + h)], part = head(dka_ref[:, _cols(h)] * LN2, x_ref[:, _cols(COL_KA + h)],
                                                     gk_ref[...])
            acc_k = acc_k + part
        o_ref[:, _cols(COL_VA, 2)] = dva_ref[...].astype(BF16)
        o_ref[:, _cols(COL_QB, N_HEADS_B)] = (dqb_ref[...] * ATT_SCALE).astype(BF16)
        o_ref[:, _cols(COL_KB, 2)] = (dkb_ref[...] * LN2).astype(BF16)
        o_ref[:, _cols(COL_VB, 2)] = dvb_ref[...].astype(BF16)

        @pl.when(i == 0)
        def _():
            dgq_ref[...] = acc_q
            dgk_ref[...] = acc_k

        @pl.when(i > 0)
        def _():
            dgq_ref[...] += acc_q
            dgk_ref[...] += acc_k

    def rows(width, shift=0):
        return pl.BlockSpec((tm, width), lambda i: (i + shift, 0))

    kv_w = N_KV_A * HEAD_DIM
    q_w = N_HEADS_A * HEAD_DIM
    vec = pl.BlockSpec((1, HEAD_DIM), lambda i: (0, 0))
    res, c_res = _call(
        body, name="qk_bwd", grid=(s // tm,),
        in_specs=[rows(q_w), rows(kv_w), rows(kv_w), rows(q_w), rows(kv_w, lo), rows(kv_w, lo), rows(n),
                  vec, vec, rows(HEAD_DIM), rows(HEAD_DIM)],
        out_specs=[rows(n), vec, vec],
        out_shape=[
            jax.ShapeDtypeStruct((s, n), BF16),
            jax.ShapeDtypeStruct((1, HEAD_DIM), F32),
            jax.ShapeDtypeStruct((1, HEAD_DIM), F32),
        ],
        sem=("arbitrary",), args=(dqa, dka, dva, dqb, dkpad, dvpad, proj, g_q, g_k, cos_t, sin_t), comm=comm,
        after=after)
    return res if comm is None else (res, c_res)


_NT = (((1,), (1,)), ((), ()))
_TN = (((0,), (0,)), ((), ()))


def _attn_a_fwd(pb, *, tq=2048, sub=256, comm=None, after=None):
    s = pb.shape[0]
    tq = min(tq, s)

    sub = min(sub, tq)

    def body(q_ref, k_ref, v_ref, o_ref, lse_ref):
        k = k_ref[...]
        v = v_ref[...]
        for r in range(tq // sub):
            rows = pl.ds(r * sub, sub)
            sc = lax.dot_general(q_ref[rows, :], k, _NT, preferred_element_type=F32)
            m = jnp.max(sc, axis=-1, keepdims=True)
            p = jnp.exp2(sc - m)
            l = jnp.sum(p, axis=-1, keepdims=True)
            o = jnp.dot(p.astype(BF16), v, preferred_element_type=F32)
            o_ref[rows, :] = (o / l).astype(BF16)
            lse_ref[rows, :] = jnp.broadcast_to(m + jnp.log2(l), (sub, HEAD_DIM))

    res, c_res = _call(
        body, name="attn_a_fwd", grid=(N_HEADS_A, s // tq),
        in_specs=[
            pl.BlockSpec((tq, HEAD_DIM), lambda h, i: (i, COL_QA + h)),
            pl.BlockSpec((s, HEAD_DIM), lambda h, i: (0, COL_KA + h // GROUP)),
            pl.BlockSpec((s, HEAD_DIM), lambda h, i: (0, COL_VA + h // GROUP)),
        ],
        out_specs=[
            pl.BlockSpec((tq, HEAD_DIM), lambda h, i: (i, h)),
            pl.BlockSpec((None, tq, HEAD_DIM), lambda h, i: (h, i, 0)),
        ],
        out_shape=[
            jax.ShapeDtypeStruct((s, (N_HEADS_A + N_HEADS_B) * HEAD_DIM), BF16),
            jax.ShapeDtypeStruct((N_HEADS_A, s, HEAD_DIM), F32),
        ],
        sem=("parallel", "parallel"), args=(pb, pb, pb), comm=comm, after=after)
    return res if comm is None else (res, c_res)


def _attn_a_bwd(pb, att, datt, lse, *, tq=2048, sub=256, comm=None):
    s = pb.shape[0]
    tq = min(tq, s)
    sub = min(sub, tq)

    def body(q_ref, k_ref, v_ref, o_ref, do_ref, lse_ref, dq_ref, dk_ref, dv_ref):
        first = jnp.logical_and(pl.program_id(1) == 0, pl.program_id(2) == 0)
        k = k_ref[...]
        v = v_ref[...]
        dk = dv = None
        for r in range(tq // sub):
            rows = pl.ds(r * sub, sub)
            q = q_ref[rows, :]
            do = do_ref[rows, :]
            sc = lax.dot_general(q, k, _NT, preferred_element_type=F32)
            p = jnp.exp2(sc - lse_ref[rows, :][:, :1])
            dp = lax.dot_general(do, v, _NT, preferred_element_type=F32)
            delta = jnp.sum(do.astype(F32) * o_ref[rows, :].astype(F32), axis=-1, keepdims=True)
            ds = (p * (dp - delta)).astype(BF16)
            dq_ref[rows, :] = jnp.dot(ds, k, preferred_element_type=F32)
            dk_r = lax.dot_general(ds, q, _TN, preferred_element_type=F32)
            dv_r = lax.dot_general(p.astype(BF16), do, _TN, preferred_element_type=F32)
            dk = dk_r if dk is None else dk + dk_r
            dv = dv_r if dv is None else dv + dv_r

        @pl.when(first)
        def _():
            dk_ref[...] = dk
            dv_ref[...] = dv

        @pl.when(jnp.logical_not(first))
        def _():
            dk_ref[...] += dk
            dv_ref[...] += dv

    qmap = lambda kv, g, i: (i, kv * GROUP + g)
    res, c_res = _call(
        body, name="attn_a_bwd", grid=(N_KV_A, GROUP, s // tq),
        in_specs=[
            pl.BlockSpec((tq, HEAD_DIM), lambda kv, g, i: (i, COL_QA + kv * GROUP + g)),
            pl.BlockSpec((s, HEAD_DIM), lambda kv, g, i: (0, COL_KA + kv)),
            pl.BlockSpec((s, HEAD_DIM), lambda kv, g, i: (0, COL_VA + kv)),
            pl.BlockSpec((tq, HEAD_DIM), qmap),
            pl.BlockSpec((tq, HEAD_DIM), qmap),
            pl.BlockSpec((None, tq, HEAD_DIM), lambda kv, g, i: (kv * GROUP + g, i, 0)),
        ],
        out_specs=[
            pl.BlockSpec((tq, HEAD_DIM), qmap),
            pl.BlockSpec((s, HEAD_DIM), lambda kv, g, i: (0, kv)),
            pl.BlockSpec((s, HEAD_DIM), lambda kv, g, i: (0, kv)),
        ],
        out_shape=[
            jax.ShapeDtypeStruct((s, N_HEADS_A * HEAD_DIM), F32),
            jax.ShapeDtypeStruct((s, N_KV_A * HEAD_DIM), F32),
            jax.ShapeDtypeStruct((s, N_KV_A * HEAD_DIM), F32),
        ],
        sem=("arbitrary", "arbitrary", "arbitrary"), args=(pb, pb, pb, att, datt, lse), comm=comm)
    return res if comm is None else (res, c_res)


def _t5_bucket(rel):
    nb = N_BUCKETS // 2
    ret = jnp.where(rel > 0, nb, 0)
    n = jnp.abs(rel)
    max_exact = nb // 2
    nf = jnp.maximum(n, 1).astype(F32)
    large = max_exact + (jnp.log(nf / max_exact) / math.log(MAX_DISTANCE / max_exact)
                         * (nb - max_exact)).astype(jnp.int32)
    large = jnp.minimum(large, nb - 1)
    return ret + jnp.where(n < max_exact, n, large)


def _band_buckets():
    r = jnp.arange(BLOCK_Q, dtype=jnp.int32)
    j = jnp.arange(3 * BLOCK_Q, dtype=jnp.int32)
    return _t5_bucket((j[None, :] - BLOCK_Q) - r[:, None])


def _band_bias(bucket, table_ref, h):
    acc = jnp.zeros(bucket.shape, F32)
    for b in range(N_BUCKETS):
        acc = jnp.where(bucket == b, table_ref[b, h], acc)
    return acc


GQ = GROUP * BLOCK_Q


def _stack_heads(x):
    return jnp.concatenate([x[:, _cols(g)] for g in range(GROUP)], axis=0)


def _unstack_heads(x):
    return jnp.concatenate([x[g * BLOCK_Q:(g + 1) * BLOCK_Q] for g in range(GROUP)], axis=1)


def _group_bias(bucket, table_ref, kv):
    return jnp.concatenate([_band_bias(bucket, table_ref, kv * GROUP + g) * LOG2E for g in range(GROUP)], axis=0)


def _group_sink(sink_ref, kv):
    head = lax.broadcasted_iota(jnp.int32, (GQ, 1), 0) // BLOCK_Q
    snk = jnp.zeros((GQ, 1), F32)
    for g in range(GROUP):
        snk = jnp.where(head == g, sink_ref[0, kv * GROUP + g] * LOG2E, snk)
    return snk


def _band_mask(n, s):
    r = lax.broadcasted_iota(jnp.int32, (GQ, 3 * BLOCK_Q), 0) % BLOCK_Q
    j = lax.broadcasted_iota(jnp.int32, (GQ, 3 * BLOCK_Q), 1)
    rel = j - BLOCK_Q - r
    kabs = n * BLOCK_Q + j - BLOCK_Q
    return (jnp.abs(rel) <= WINDOW) & (kabs >= 0) & (kabs < s)


def _band_start(n):
    return pl.multiple_of(n * BLOCK_Q + (PAD_LO - BLOCK_Q), BLOCK_Q)


def _attn_b_fwd(pb, kpad, vpad, bucket, table, sink, att, *, comm=None, after=None):
    s = pb.shape[0]
    nblk = s // BLOCK_Q
    sp = kpad.shape[0]

    def body(table_ref, sink_ref, q0_ref, q1_ref, k_ref, v_ref, bucket_ref, _, o_ref, lse_ref, bias_ref):
        n = pl.program_id(0)

        @pl.when(n == 0)
        def _():
            for kv in range(N_KV_B):
                bias_ref[kv * GQ:(kv + 1) * GQ, :] = _group_bias(bucket_ref[...], table_ref, kv)

        band = pl.ds(_band_start(n), 3 * BLOCK_Q)
        mask = _band_mask(n, s)
        for kv, q_ref in enumerate((q0_ref, q1_ref)):
            kb = k_ref[band, _cols(kv)]
            vb = v_ref[band, _cols(kv)]
            sc = lax.dot_general(_stack_heads(q_ref[...]), kb, _NT, preferred_element_type=F32)
            sc = jnp.where(mask, sc + bias_ref[kv * GQ:(kv + 1) * GQ, :], NEG_INF)
            snk = _group_sink(sink_ref, kv)
            m = jnp.maximum(jnp.max(sc, axis=-1, keepdims=True), snk)
            p = jnp.exp2(sc - m)
            l = jnp.sum(p, axis=-1, keepdims=True) + jnp.exp2(snk - m)
            o = jnp.dot(p.astype(BF16), vb, preferred_element_type=F32)
            o_ref[:, _cols(kv * GROUP, GROUP)] = _unstack_heads((o / l).astype(BF16))
            lse = m + jnp.log2(l)
            for g in range(GROUP):
                lse_ref[kv * GROUP + g] = jnp.broadcast_to(lse[g * BLOCK_Q:(g + 1) * BLOCK_Q], (BLOCK_Q, HEAD_DIM))

    smem = pl.BlockSpec(memory_space=pltpu.SMEM)
    wide = GROUP * HEAD_DIM
    whole = pl.BlockSpec((sp, N_KV_B * HEAD_DIM), lambda n: (0, 0))
    res, c_res = _call(
        body, name="attn_b_fwd", grid=(nblk,),
        in_specs=[
            smem,
            smem,
            pl.BlockSpec((BLOCK_Q, wide), lambda n: (n, COL_QB // GROUP)),
            pl.BlockSpec((BLOCK_Q, wide), lambda n: (n, COL_QB // GROUP + 1)),
            whole,
            whole,
            pl.BlockSpec((BLOCK_Q, 3 * BLOCK_Q), lambda n: (0, 0)),
            _ANY,
        ],
        out_specs=[
            pl.BlockSpec((BLOCK_Q, N_HEADS_B * HEAD_DIM), lambda n: (n, 1)),
            pl.BlockSpec((N_HEADS_B, BLOCK_Q, HEAD_DIM), lambda n: (0, n, 0)),
        ],
        out_shape=[
            jax.ShapeDtypeStruct(att.shape, BF16),
            jax.ShapeDtypeStruct((N_HEADS_B, s, HEAD_DIM), F32),
        ],
        scratch_shapes=[pltpu.VMEM((N_KV_B * GQ, 3 * BLOCK_Q), F32)],
        sem=("arbitrary",), args=(table, sink, pb, pb, kpad, vpad, bucket, att), comm=comm, after=after,
        aliases={7: 0})
    return res if comm is None else (res, c_res)


def _attn_b_bwd(pb, kpad, vpad, att, datt, lse, bucket, table, sink, *, comm=None, after=None):
    s = pb.shape[0]
    nblk = s // BLOCK_Q
    sp = kpad.shape[0]

    def body(table_ref, sink_ref, q0_ref, q1_ref, k_ref, v_ref, o_ref, do_ref, lse_ref, bucket_ref,
             dq_ref, dk_ref, dv_ref, dtab_ref, dsink_ref, bias_ref, dbias_ref):
        n = pl.program_id(0)

        @pl.when(n == 0)
        def _():
            dk_ref[...] = jnp.zeros_like(dk_ref)
            dv_ref[...] = jnp.zeros_like(dv_ref)
            dbias_ref[...] = jnp.zeros_like(dbias_ref)
            dsink_ref[...] = jnp.zeros_like(dsink_ref)
            for kv in range(N_KV_B):
                bias_ref[kv * GQ:(kv + 1) * GQ, :] = _group_bias(bucket_ref[...], table_ref, kv)

        band = pl.ds(_band_start(n), 3 * BLOCK_Q)
        mask = _band_mask(n, s)
        for kv, q_ref in enumerate((q0_ref, q1_ref)):
            wide_cols = _cols(kv * GROUP, GROUP)
            q = _stack_heads(q_ref[...])
            do = _stack_heads(do_ref[:, wide_cols])
            o = _stack_heads(o_ref[:, wide_cols])
            kb = k_ref[band, _cols(kv)]
            vb = v_ref[band, _cols(kv)]
            lse = jnp.concatenate([lse_ref[kv * GROUP + g][:, :1] for g in range(GROUP)], axis=0)
            sc = lax.dot_general(q, kb, _NT, preferred_element_type=F32)
            sc = jnp.where(mask, sc + bias_ref[kv * GQ:(kv + 1) * GQ, :], NEG_INF)
            p = jnp.exp2(sc - lse)
            dp = lax.dot_general(do, vb, _NT, preferred_element_type=F32)
            delta = jnp.sum(do.astype(F32) * o.astype(F32), axis=-1, keepdims=True)
            ds = p * (dp - delta)
            dsb = ds.astype(BF16)
            dq_ref[:, wide_cols] = _unstack_heads(jnp.dot(dsb, kb, preferred_element_type=F32))
            dk_ref[band, _cols(kv)] += lax.dot_general(dsb, q, _TN, preferred_element_type=F32)
            dv_ref[band, _cols(kv)] += lax.dot_general(p.astype(BF16), do, _TN, preferred_element_type=F32)
            dbias_ref[kv * GQ:(kv + 1) * GQ, :] += ds
            sink_part = -jnp.exp2(_group_sink(sink_ref, kv) - lse) * delta
            for g in range(GROUP):
                rows = slice(g * BLOCK_Q, (g + 1) * BLOCK_Q)
                dsink_ref[kv * GROUP + g] += jnp.broadcast_to(
                    jnp.sum(sink_part[rows], axis=0, keepdims=True), (1, HEAD_DIM))

        @pl.when(n == nblk - 1)
        def _():
            bucket_v = bucket_ref[...]
            row = lax.broadcasted_iota(jnp.int32, (N_BUCKETS, HEAD_DIM), 0)
            for h in range(N_HEADS_B):
                acc = dbias_ref[h * BLOCK_Q:(h + 1) * BLOCK_Q, :]
                tot = jnp.zeros((N_BUCKETS, HEAD_DIM), F32)
                for b in range(N_BUCKETS):
                    tot = jnp.where(row == b, jnp.sum(jnp.where(bucket_v == b, acc, 0.0), keepdims=True), tot)
                dtab_ref[h] = tot

    smem = pl.BlockSpec(memory_space=pltpu.SMEM)
    wide = GROUP * HEAD_DIM
    whole = pl.BlockSpec((sp, N_KV_B * HEAD_DIM), lambda n: (0, 0))
    group_b = pl.BlockSpec((BLOCK_Q, N_HEADS_B * HEAD_DIM), lambda n: (n, 1))
    res, c_res = _call(
        body, name="attn_b_bwd", grid=(nblk,),
        in_specs=[
            smem,
            smem,
            pl.BlockSpec((BLOCK_Q, wide), lambda n: (n, COL_QB // GROUP)),
            pl.BlockSpec((BLOCK_Q, wide), lambda n: (n, COL_QB // GROUP + 1)),
            whole,
            whole,
            group_b,
            group_b,
            pl.BlockSpec((N_HEADS_B, BLOCK_Q, HEAD_DIM), lambda n: (0, n, 0)),
            pl.BlockSpec((BLOCK_Q, 3 * BLOCK_Q), lambda n: (0, 0)),
        ],
        out_specs=[
            pl.BlockSpec((BLOCK_Q, N_HEADS_B * HEAD_DIM), lambda n: (n, 0)),
            whole,
            whole,
            pl.BlockSpec((N_HEADS_B, N_BUCKETS, HEAD_DIM), lambda n: (0, 0, 0)),
            pl.BlockSpec((N_HEADS_B, 1, HEAD_DIM), lambda n: (0, 0, 0)),
        ],
        out_shape=[
            jax.ShapeDtypeStruct((s, N_HEADS_B * HEAD_DIM), F32),
            jax.ShapeDtypeStruct((sp, N_KV_B * HEAD_DIM), F32),
            jax.ShapeDtypeStruct((sp, N_KV_B * HEAD_DIM), F32),
            jax.ShapeDtypeStruct((N_HEADS_B, N_BUCKETS, HEAD_DIM), F32),
            jax.ShapeDtypeStruct((N_HEADS_B, 1, HEAD_DIM), F32),
        ],
        scratch_shapes=[pltpu.VMEM((N_KV_B * GQ, 3 * BLOCK_Q), F32), pltpu.VMEM((N_KV_B * GQ, 3 * BLOCK_Q), F32)],
        sem=("arbitrary",),
        args=(table, sink, pb, pb, kpad, vpad, att, datt, lse, bucket), comm=comm, after=after)
    return res if comm is None else (res, c_res)


_MESH = pl.DeviceIdType.MESH


def _other_chips(x, y):
    return [(x, 1 - y), (1 - x, y), (1 - x, 1 - y)]


_HBM = pl.BlockSpec(memory_space=pltpu.HBM)
_SEM = pl.BlockSpec(memory_space=pltpu.SEMAPHORE)
_SPLIT = pltpu.CompilerParams(has_side_effects=pltpu.SideEffectType.DATAFLOW_SIDE_EFFECTING)


def _in_hbm(a):
    return pltpu.with_memory_space_constraint(a, pltpu.HBM)


def _my_half(rows):
    c = lax.axis_index("c")
    half = rows // 2
    return pl.ds(pl.multiple_of(c * half, half), half), pl.ds(pl.multiple_of((1 - c) * half, half), half)


def _gather_route(shapes):
    def route(src, land):
        x, y, c = lax.axis_index("x"), lax.axis_index("y"), lax.axis_index("c")
        out = []
        for t, shape in enumerate(shapes):
            mine, _ = _my_half(shape[0])
            for px, py in _other_chips(x, y):
                out.append((src[t].at[mine], land[t].at[2 * x + y, mine], land[t].at[2 * px + py, mine], (px, py, c)))
        return out

    return route


def _exchange_route(n_t):
    def route(src, land):
        x, y, c = lax.axis_index("x"), lax.axis_index("y"), lax.axis_index("c")
        out = []
        for t in range(n_t):
            for px, py in _other_chips(x, y):
                k = 2 * px + py
                out.append((src[t].at[k], land[t].at[2 * (2 * x + y) + c], land[t].at[2 * k + c], (px, py, c)))
        return out

    return route


def _own_slot(shape, dtype, slot, block):
    return lax.dynamic_update_slice(lax.empty(shape, dtype), block[None], (slot,) + (0,) * (len(shape) - 1))


def _split_start(name, srcs, lands, route, after):
    n = len(srcs)

    def body(*refs):
        src, land, send_sems, recv_sems, token = refs[:n], refs[n:2 * n], refs[2 * n + 1], refs[2 * n + 2], refs[-1]
        for i, (src_ref, dst_ref, _, to) in enumerate(route(src, land)):
            pltpu.make_async_remote_copy(src_ref=src_ref, dst_ref=dst_ref, send_sem=send_sems.at[i],
                                         recv_sem=recv_sems.at[i], device_id=to, device_id_type=_MESH).start()
        token[...] = jnp.zeros_like(token)

    sem = pltpu.SemaphoreType.DMA((3 * n,))
    lands = list(lands)
    res = pl.pallas_call(
        body, name=name,
        in_specs=[_HBM] * (2 * n) + [_ANY],
        out_specs=[_SEM, _SEM] + [_HBM] * (2 * n) + [pl.BlockSpec(memory_space=pltpu.VMEM)],
        out_shape=[sem, sem] + [pltpu.HBM(a.shape, a.dtype) for a in list(srcs) + lands]
        + [jax.ShapeDtypeStruct((8, 128), F32)],
        input_output_aliases={i: 2 + i for i in range(2 * n)},
        compiler_params=_SPLIT,
    )(*[_in_hbm(a) for a in srcs], *[_in_hbm(a) for a in lands], after)
    return (res[0], res[1]), res[2:2 + n], res[2 + n:2 + 2 * n], res[-1]


def _split_wait(name, srcs, lands, sems, route, after):
    n = len(srcs)

    def body(*refs):
        src, land, send_sems, recv_sems = refs[:n], refs[n:2 * n], refs[2 * n], refs[2 * n + 1]
        for i, (src_ref, _, dst_ref, to) in enumerate(route(src, land)):
            cp = pltpu.make_async_remote_copy(src_ref=src_ref, dst_ref=dst_ref, send_sem=send_sems.at[i],
                                              recv_sem=recv_sems.at[i], device_id=to, device_id_type=_MESH)
            cp.wait_send()
            cp.wait_recv()

    res = pl.pallas_call(
        body, name=name,
        in_specs=[_HBM] * (2 * n) + [_SEM, _SEM, _ANY],
        out_specs=[_HBM] * (2 * n),
        out_shape=[pltpu.HBM(a.shape, a.dtype) for a in list(srcs) + list(lands)],
        input_output_aliases={i: i for i in range(2 * n)},
        compiler_params=_SPLIT,
    )(*srcs, *lands, sems[0], sems[1], after)
    return res[:n], res[n:]


def _comm_only(name, comm):
    return _call(lambda: None, name=name, grid=(1,), in_specs=[], out_specs=[], out_shape=[], args=(), comm=comm)[1]


def _swap_comm(shards, lands):
    n_t = len(lands)

    def copies(land, sems, later):
        send_sems, recv_sems = sems
        x, y = lax.axis_index("x"), lax.axis_index("y")
        sends, recvs = [], []
        for t in range(n_t):
            mine, other = _my_half(shards[t].shape[0])
            for j, (px, py) in enumerate(_other_chips(x, y)):
                k = 2 * px + py
                for part, out in ((mine, sends), (other, recvs)) if later else ((mine, sends),):
                    out.append(pltpu.make_async_remote_copy(
                        src_ref=land[t].at[k, part], dst_ref=land[t].at[k, part], send_sem=send_sems.at[3 * t + j],
                        recv_sem=recv_sems.at[3 * t + j], device_id=_sibling(), device_id_type=_MESH))
        return sends, recvs

    def start(ins, land, sems):
        for cp in copies(land, sems, False)[0]:
            cp.start()

    def finish(ins, land, sems):
        sends, recvs = copies(land, sems, True)
        for cp in recvs:
            cp.wait_recv()
        for cp in sends:
            cp.wait_send()

    return _Comm(
        lands, [jax.ShapeDtypeStruct(a.shape, a.dtype) for a in lands],
        [pltpu.SemaphoreType.DMA((3 * n_t,)), pltpu.SemaphoreType.DMA((3 * n_t,))],
        start, finish, aliases={t: t for t in range(n_t)})


def _forward_comm(partials, lands):
    n_t = len(lands)

    def copies(land, sems, later):
        send_sems, recv_sems = sems
        x, y, c = lax.axis_index("x"), lax.axis_index("y"), lax.axis_index("c")
        sends, recvs = [], []
        for t in range(n_t):
            for j, k in enumerate([2 * x + y] + [2 * px + py for px, py in _other_chips(x, y)]):
                for slot, out in ((2 * k + c, sends), (2 * k + 1 - c, recvs)) if later else ((2 * k + c, sends),):
                    out.append(pltpu.make_async_remote_copy(
                        src_ref=land[t].at[slot], dst_ref=land[t].at[slot], send_sem=send_sems.at[4 * t + j],
                        recv_sem=recv_sems.at[4 * t + j], device_id=_sibling(), device_id_type=_MESH))
        return sends, recvs

    def start(ins, land, sems):
        for cp in copies(land, sems, False)[0]:
            cp.start()

    def finish(ins, land, sems):
        sends, recvs = copies(land, sems, True)
        for cp in recvs:
            cp.wait_recv()
        for cp in sends:
            cp.wait_send()

    return _Comm(
        lands, [jax.ShapeDtypeStruct(a.shape, a.dtype) for a in lands],
        [pltpu.SemaphoreType.DMA((4 * n_t,)), pltpu.SemaphoreType.DMA((4 * n_t,))],
        start, finish, aliases={t: t for t in range(n_t)})


def _allreduce_small(pack):
    rows, d = pack.shape

    def body(p_ref, sum_ref, all_ref, send_sems, recv_sems):
        x, y, c = lax.axis_index("x"), lax.axis_index("y"), lax.axis_index("c")
        me = 4 * x + 2 * y + c
        all_ref[me] = p_ref[...]
        peers = []
        for dx in range(2):
            for dy in range(2):
                for dc in range(2):
                    if dx or dy or dc:
                        px = 1 - x if dx else x
                        py = 1 - y if dy else y
                        pc = 1 - c if dc else c
                        peers.append((4 * dx + 2 * dy + dc - 1, (px, py, pc)))
        sends = []
        for k, to in peers:
            cp = pltpu.make_async_remote_copy(
                src_ref=p_ref, dst_ref=all_ref.at[me], send_sem=send_sems.at[k], recv_sem=recv_sems.at[k],
                device_id=to, device_id_type=_MESH)
            cp.start()
            sends.append(cp)
        for k, (px, py, pc) in peers:
            pltpu.make_async_remote_copy(
                src_ref=p_ref, dst_ref=all_ref.at[4 * px + 2 * py + pc], send_sem=send_sems.at[k],
                recv_sem=recv_sems.at[k], device_id=(px, py, pc), device_id_type=_MESH).wait_recv()
        for cp in sends:
            cp.wait_send()
        tot = all_ref[0]
        for i in range(1, N_DEV):
            tot = tot + all_ref[i]
        sum_ref[...] = tot

    vm = pl.BlockSpec(memory_space=pltpu.VMEM)
    return pl.pallas_call(
        body,
        name="allreduce_small",
        in_specs=[vm],
        out_specs=vm,
        out_shape=jax.ShapeDtypeStruct((rows, d), F32),
        scratch_shapes=[
            pltpu.VMEM((N_DEV, rows, d), F32),
            pltpu.SemaphoreType.DMA((N_DEV - 1,)),
            pltpu.SemaphoreType.DMA((N_DEV - 1,)),
        ],
    )(pack)


def _adamw_math(w, g, m, v):
    m = ADAM_B1 * m + (1.0 - ADAM_B1) * g
    v = ADAM_B2 * v + (1.0 - ADAM_B2) * (g * g)
    m_hat = m / (1.0 - ADAM_B1 ** ADAM_STEP)
    v_hat = v / (1.0 - ADAM_B2 ** ADAM_STEP)
    delta = -ADAM_LR * (m_hat / (jnp.sqrt(v_hat) + ADAM_EPS) + ADAM_WD * w)
    return delta, m, v


def _sum_adamw(parts, w, m, v, *, name, tr=256):
    r, c = w.shape
    tr = min(tr, r)
    tc = min(c, 1024)

    def body(p_ref, w_ref, m_ref, v_ref, g_ref, d_ref, m2_ref, v2_ref):
        g = p_ref[0].astype(F32)
        for i in range(1, N_DEV):
            g = g + p_ref[i].astype(F32)
        delta, m2, v2 = _adamw_math(w_ref[...], g, m_ref[...], v_ref[...])
        g_ref[...] = g
        d_ref[...] = delta
        m2_ref[...] = m2
        v2_ref[...] = v2

    blk = pl.BlockSpec((tr, tc), lambda i, j: (i, j))
    return pl.pallas_call(
        body,
        name=name,
        grid=(r // tr, c // tc),
        in_specs=[pl.BlockSpec((N_DEV, tr, tc), lambda i, j: (0, i, j)), blk, blk, blk],
        out_specs=[blk] * 4,
        out_shape=[jax.ShapeDtypeStruct((r, c), F32)] * 4,
        compiler_params=_params(("parallel", "parallel")),
    )(parts, w, m, v)


def _adamw_small(g, w, m, v):
    def body(g_ref, w_ref, m_ref, v_ref, d_ref, m2_ref, v2_ref):
        delta, m2, v2 = _adamw_math(w_ref[...], g_ref[...], m_ref[...], v_ref[...])
        d_ref[...] = delta
        m2_ref[...] = m2
        v2_ref[...] = v2

    vm = pl.BlockSpec(memory_space=pltpu.VMEM)
    return pl.pallas_call(
        body,
        name="adamw_small",
        in_specs=[vm] * 4,
        out_specs=[vm] * 3,
        out_shape=[jax.ShapeDtypeStruct(g.shape, F32)] * 3,
    )(g, w, m, v)


def _relu2_epilogue(acc):
    ra = jnp.maximum(acc, 0.0)
    return ra * ra, ra


def _rows(stacked):
    return stacked.reshape(stacked.shape[0] * stacked.shape[1], stacked.shape[2])


def _by_chip(mat):
    return mat.reshape(N_CHIPS, mat.shape[0] // N_CHIPS, mat.shape[1])


def _local_step(x, p, target, shards, small, update):
    s, d = x.shape
    cos_t, sin_t = _rope_tables(s)
    bucket = _band_buckets()
    p_bf = p.astype(BF16)
    wts = {}

    chip = 2 * lax.axis_index("x") + lax.axis_index("y")
    core = lax.axis_index("c")

    def gather(tag, names, after):
        srcs = [cast[n] for n in names]
        route = _gather_route([a.shape for a in srcs])
        sems, srcs, lands, token = _split_start(f"gather_start_{tag}", srcs, [zones[n] for n in names], route, after)

        def landed(done):
            got_srcs, got_lands = _split_wait(f"gather_wait_{tag}", srcs, lands, sems, route, done)
            comm = _swap_comm(got_srcs, got_lands)
            comm.waited = got_srcs[0]
            return comm

        return landed, token

    def prepare(n, zero):
        cast[n] = (shards[n] + zero).astype(BF16)
        zones[n] = _own_slot((N_CHIPS,) + cast[n].shape, BF16, chip, cast[n])

    cast, zones = {}, {}
    prepare("w_in", 0.0)
    in_landed, token = gather("in", ["w_in"], small["attn_norm_g"])
    for n in shards:
        if n != "w_in":
            prepare(n, token[:1, :1])
    g_attn = small["attn_norm_g"] + token[:1, :1]
    u = _rms_fwd(x, g_attn, name="norm_attn")
    prepared = u[:1, :1].astype(F32) + sum(
        (lax.dynamic_slice(zones[n], (chip, 0, 0), (1, 1, 1))[0] + cast[n][:1, :1]).astype(F32)
        for n in zones if n != "w_in")
    (wts["w_in"],) = _comm_only("swap_w_in", in_landed(prepared))
    mid_landed, token = gather("mid", ["w_out"], wts["w_in"])
    proj = _matmul(u, wts["w_in"], mode="nn", out_dtypes=[F32], name="mm_in", bn=768, after=token)
    pb, (w_out_s,) = _qk_prep(proj, small["q_norm_g"], small["k_norm_g"], cos_t, sin_t, comm=mid_landed(proj))
    wts["w_out"] = _rows(w_out_s)
    up_landed, token = gather("up", ["w_up"], pb)
    att_a, lse_a = _attn_a_fwd(pb, after=token)
    pad = ((PAD_LO, PAD_HI), (0, 0))
    kpad = jnp.pad(pb[:, COL_KB * HEAD_DIM:COL_VB * HEAD_DIM], pad)
    vpad = jnp.pad(pb[:, COL_VB * HEAD_DIM:], pad)
    up_swap = up_landed(att_a)
    down_landed, token = gather("down", ["w_down"], up_swap.waited)
    (att, lse_b), (wts["w_up"],) = _attn_b_fwd(pb, kpad, vpad, bucket, small["rel_bias_table"],
                                               small["sink_logits"], att_a, comm=up_swap, after=token)
    h1 = _matmul(att, wts["w_out"], mode="nn", out_dtypes=[F32], name="mm_out",
                 epilogue=lambda acc, res: (acc + res,), extras=(x,))
    mn = _rms_fwd(h1, small["mlp_norm_g"], name="norm_mlp")
    r, ra = _matmul(mn, wts["w_up"], mode="nn", out_dtypes=[BF16, BF16], name="mm_up", epilogue=_relu2_epilogue,
                    bm=2048)
    (w_down_s,) = _comm_only("swap_w_down", down_landed(r))
    wts["w_down"] = _rows(w_down_s)
    late_landed, token = gather("late", ["w_gate", "ple_w"], w_down_s)
    h2 = _matmul(r, wts["w_down"], mode="nn", out_dtypes=[F32], name="mm_down",
                 epilogue=lambda acc, res: (acc + res,), extras=(h1,), after=token, bm=512, bn=512, bk=8192)
    ng, (w_gate_s, wts["ple_w"]) = _rms_fwd(h2, small["gate_norm_g"], name="norm_gate", comm=late_landed(h2))
    wts["w_gate"] = _rows(w_gate_s)
    gate = _matmul(ng, wts["w_gate"], mode="nn", out_dtypes=[F32], name="mm_gate",
                   epilogue=lambda acc: (1.0 / (1.0 + jnp.exp(-acc)),))
    pp = _matmul(p_bf, wts["ple_w"], mode="nn", out_dtypes=[F32], name="mm_ple", bn=512)
    dh3, dz, dpp, dg_final, dg_ple, loss = _tail(h2, gate, pp, target, small["ple_norm_g"], small["final_norm_g"])

    dng = _matmul(dz, wts["w_gate"], mode="nt", out_dtypes=[F32], name="mm_gate_dx")
    gw_gate = _matmul(ng, dz, mode="tn", out_dtypes=[BF16], name="mm_gate_dw")
    gw_ple = _matmul(p_bf, dpp, mode="tn", out_dtypes=[BF16], name="mm_ple_dw", bn=512, out_stack=N_CHIPS)
    dh2, dh2_bf, dg_gate = _rms_bwd(h2, dng, small["gate_norm_g"], dh3, name="norm_gate_bwd", want_bf16=True)

    def exchange(tag, partials, after):
        route = _exchange_route(len(partials))
        lands = [_own_slot((N_DEV,) + g.shape[1:], g.dtype, 2 * chip + core,
                           lax.dynamic_index_in_dim(g, chip, 0, keepdims=False)) for g in partials]
        sems, srcs, lands, token = _split_start(f"exchange_start_{tag}", partials, lands, route, after)

        def landed(done):
            got_srcs, got_lands = _split_wait(f"exchange_wait_{tag}", srcs, lands, sems, route, done)
            comm = _forward_comm(got_srcs, got_lands)
            comm.waited = got_srcs[0]
            return comm

        return landed, token

    big = {}
    gate_landed, token = exchange("gate", [_by_chip(gw_gate), gw_ple], dh2_bf)
    gw_down = _matmul(r, dh2_bf, mode="tn", out_dtypes=[BF16], name="mm_down_dw", after=token, bm=512, bk=4096)
    da, (parts_gate, parts_ple) = _matmul(
        dh2_bf, wts["w_down"], mode="nt", out_dtypes=[BF16], name="mm_down_dx", bm=2048,
        epilogue=lambda acc, ra_v: (acc * (2.0 * ra_v.astype(F32)),), extras=(ra,), comm=gate_landed(gw_down))
    down_landed, token = exchange("down", [_by_chip(gw_down)], da)
    big["w_gate"], big["ple_w"] = update("w_gate", parts_gate), update("ple_w", parts_ple)
    gw_up = _matmul(mn, da, mode="tn", out_dtypes=[BF16], name="mm_up_dw", out_stack=N_CHIPS, after=token,
                    bm=512, bk=4096)
    dmn = _matmul(da, wts["w_up"], mode="nt", out_dtypes=[F32], name="mm_up_dx", bm=512, bn=512, bk=8192)
    dh1, dh1_bf, dg_mlp = _rms_bwd(h1, dmn, small["mlp_norm_g"], dh2, name="norm_mlp_bwd", want_bf16=True)
    datt, (parts_down,) = _matmul(dh1_bf, wts["w_out"], mode="nt", out_dtypes=[BF16], name="mm_out_dx",
                                  comm=down_landed(dh1_bf))
    gw_out = _matmul(att, dh1_bf, mode="tn", out_dtypes=[BF16], name="mm_out_dw")
    up_landed, token = exchange("up", [gw_up], datt)
    dqb, dkpad, dvpad, dtab, dsink = _attn_b_bwd(pb, kpad, vpad, att, datt, lse_b, bucket,
                                                 small["rel_bias_table"], small["sink_logits"], after=token)
    dqa, dka, dva = _attn_a_bwd(pb, att, datt, lse_a)
    up_forward = up_landed(dqa)
    out_landed, token = exchange("out", [_by_chip(gw_out)], up_forward.waited)
    (dproj, dg_q, dg_k), (parts_up,) = _qk_bwd(dqa, dka, dva, dqb, dkpad, dvpad, proj,
                                               small["q_norm_g"], small["k_norm_g"], cos_t, sin_t,
                                               comm=up_forward, after=token)
    gw_in = _matmul(u, dproj, mode="tn", out_dtypes=[BF16], name="mm_in_dw", bn=768, out_stack=N_CHIPS)
    out_forward = out_landed(gw_in)
    in_landed, token = exchange("in", [gw_in], out_forward.waited)
    du, (parts_out,) = _matmul(dproj, wts["w_in"], mode="nt", out_dtypes=[F32], name="mm_in_dx", bk=3072,
                               comm=out_forward, after=token)
    grad_x, dg_attn = _rms_bwd(x, du, small["attn_norm_g"], dh1, name="norm_attn_bwd", want_bf16=False)
    for n, parts in (("w_down", parts_down), ("w_up", parts_up), ("w_out", parts_out)):
        big[n] = update(n, parts)
    done = dg_attn + sum(big[n][0][0, :1, :] for n in ("w_down", "w_up", "w_out"))
    (parts_in,) = _comm_only("forward_w_in", in_landed(done))
    big["w_in"] = update("w_in", parts_in)

    small_g = {
        "attn_norm_g": dg_attn, "mlp_norm_g": dg_mlp, "ple_norm_g": dg_ple, "gate_norm_g": dg_gate,
        "final_norm_g": dg_final, "q_norm_g": dg_q, "k_norm_g": dg_k,
        "sink_logits": dsink[:, 0, 0][None, :], "rel_bias_table": dtab[:, :, 0].T,
    }
    return loss, grad_x, big, small_g


_SMALL_ROWS = ["attn_norm_g", "mlp_norm_g", "ple_norm_g", "gate_norm_g", "final_norm_g"]
_PACK_ROWS = 8


def _pack_small(vals, d):
    rows = [vals[n].reshape(1, d) for n in _SMALL_ROWS]
    misc = jnp.concatenate([
        vals["q_norm_g"].reshape(1, HEAD_DIM), vals["k_norm_g"].reshape(1, HEAD_DIM),
        jnp.pad(vals["sink_logits"].reshape(1, N_HEADS_B), ((0, 0), (0, HEAD_DIM - N_HEADS_B))),
        vals["rel_bias_table"].reshape(1, N_BUCKETS * N_HEADS_B)], axis=1)
    rows.append(jnp.pad(misc, ((0, 0), (0, d - misc.shape[1]))))
    rows.append(jnp.zeros((_PACK_ROWS - len(rows), d), F32))
    return jnp.concatenate(rows, axis=0).astype(F32)


def _unpack_small(pack, shapes):
    out = {n: pack[i].reshape(shapes[n]) for i, n in enumerate(_SMALL_ROWS)}
    misc = pack[len(_SMALL_ROWS)]
    out["q_norm_g"] = misc[:HEAD_DIM].reshape(shapes["q_norm_g"])
    out["k_norm_g"] = misc[HEAD_DIM:2 * HEAD_DIM].reshape(shapes["k_norm_g"])
    out["sink_logits"] = misc[2 * HEAD_DIM:2 * HEAD_DIM + N_HEADS_B].reshape(shapes["sink_logits"])
    out["rel_bias_table"] = misc[3 * HEAD_DIM:3 * HEAD_DIM + N_BUCKETS * N_HEADS_B].reshape(shapes["rel_bias_table"])
    return out


_WEIGHTS = ["attn_norm_g", "w_in", "q_norm_g", "k_norm_g", "sink_logits", "w_out", "mlp_norm_g", "w_up", "w_down",
            "ple_w", "ple_norm_g", "gate_norm_g", "w_gate", "rel_bias_table", "final_norm_g"]
_BIG = ["w_in", "w_out", "w_up", "w_down", "ple_w", "w_gate"]


def kernel(x, p, attn_norm_g, w_in, q_norm_g, k_norm_g, sink_logits, w_out, mlp_norm_g, w_up, w_down, ple_w, ple_norm_g, gate_norm_g, w_gate, rel_bias_table, final_norm_g, loss_target, m_attn_norm_g, m_w_in, m_q_norm_g, m_k_norm_g, m_sink_logits, m_w_out, m_mlp_norm_g, m_w_up, m_w_down, m_ple_w, m_ple_norm_g, m_gate_norm_g, m_w_gate, m_rel_bias_table, m_final_norm_g, v_attn_norm_g, v_w_in, v_q_norm_g, v_k_norm_g, v_sink_logits, v_w_out, v_mlp_norm_g, v_w_up, v_w_down, v_ple_w, v_ple_norm_g, v_gate_norm_g, v_w_gate, v_rel_bias_table, v_final_norm_g):
    given = dict(locals())
    w = {n: given[n] for n in _WEIGHTS}
    m = {n: given["m_" + n] for n in _WEIGHTS}
    v = {n: given["v_" + n] for n in _WEIGHTS}
    d = x.shape[-1]

    shards = {n: w[n][0] for n in _BIG}
    small = {
        "attn_norm_g": w["attn_norm_g"], "mlp_norm_g": w["mlp_norm_g"], "ple_norm_g": w["ple_norm_g"],
        "gate_norm_g": w["gate_norm_g"], "final_norm_g": w["final_norm_g"].reshape(1, d),
        "q_norm_g": w["q_norm_g"], "k_norm_g": w["k_norm_g"], "sink_logits": w["sink_logits"],
        "rel_bias_table": w["rel_bias_table"],
    }

    def update(n, parts):
        res = _sum_adamw(parts, w[n][0], m[n][0], v[n][0], name="adamw_" + n)
        return [t.reshape(w[n].shape) for t in res]

    loss_part, grad_x, big, small_g = _local_step(x[0], p[0, 0], loss_target[0], shards, small, update)
    grads, deltas, new_m, new_v = [{n: big[n][i] for n in _BIG} for i in range(4)]

    shapes = {n: w[n].shape for n in _WEIGHTS if n not in _BIG}
    pack = _pack_small(small_g, d)
    pack = pack.at[_PACK_ROWS - 1, :1].add(0.0 * grads["w_in"][0, 0, :1])
    pack = pack.at[_PACK_ROWS - 1, 1].set(loss_part[0, 0])
    g_small = _allreduce_small(pack)
    loss = g_small[_PACK_ROWS - 1, 1]
    d_small, m_small, v_small = _adamw_small(g_small, _pack_small(w, d), _pack_small(m, d), _pack_small(v, d))
    grads.update(_unpack_small(g_small, shapes))
    deltas.update(_unpack_small(d_small, shapes))
    new_m.update(_unpack_small(m_small, shapes))
    new_v.update(_unpack_small(v_small, shapes))

    return (loss, grad_x[None], *[grads[n] for n in _WEIGHTS], *[deltas[n] for n in _WEIGHTS],
            *[new_m[n] for n in _WEIGHTS], *[new_v[n] for n in _WEIGHTS])
```

```python
import functools
import math

import jax
import jax.numpy as jnp
import numpy as np
from jax import lax
from jax.experimental import pallas as pl
from jax.experimental.pallas import tpu as pltpu

F32 = jnp.float32
BF16 = jnp.bfloat16

HEAD_DIM = 128
N_HEADS_A = 8
N_KV_A = 2
N_HEADS_B = 8
N_KV_B = 2
GROUP = 4
GRID_W = 64
BLOCK_Q = 128
WINDOW = 128
N_BUCKETS = 32
MAX_DISTANCE = 128
ROPE_THETA = 10000.0
EPS = 1e-6
NEG_INF = -1e30
ATT_SCALE = HEAD_DIM ** -0.5
LOG2E = math.log2(math.e)
LN2 = math.log(2.0)
Q_SCALE = ATT_SCALE * LOG2E
PAD_LO, PAD_HI = 256, 128

ADAM_LR = 0.001
ADAM_B1 = 0.9
ADAM_B2 = 0.999
ADAM_EPS = 1e-08
ADAM_WD = 0.01
ADAM_STEP = 10

N_CHIPS = 4
N_DEV = 8
COL_QA, COL_KA, COL_VA, COL_QB, COL_KB, COL_VB = 0, 8, 10, 12, 20, 22
N_COLS = 24

VMEM_LIMIT = 52 * 1024 * 1024


def _params(sem=None, collective_id=None):
    return pltpu.CompilerParams(dimension_semantics=sem, vmem_limit_bytes=VMEM_LIMIT, collective_id=collective_id)


_ANY = pl.BlockSpec(memory_space=pl.ANY)
_MESH = pl.DeviceIdType.MESH
SIBLING_BARRIER_ID = 1


def _sibling():
    return (lax.axis_index("x"), lax.axis_index("y"), 1 - lax.axis_index("c"))


class _Comm:
    def __init__(self, inputs, out_shapes, sems, start, finish, aliases=None):
        self.inputs, self.out_shapes, self.sems = list(inputs), list(out_shapes), list(sems)
        self.start, self.finish, self.aliases = start, finish, dict(aliases or {})


def _call(body, *, name, grid, in_specs, out_specs, out_shape, args, scratch_shapes=(), sem=None, comm=None,
          after=None, aliases=None):
    in_specs, out_specs, out_shape = list(in_specs), list(out_specs), list(out_shape)
    scratch_shapes = list(scratch_shapes)
    n_in, n_out, n_sc = len(in_specs), len(out_specs), len(scratch_shapes)
    behind = [] if after is None else [after]
    aliases = dict(aliases or {})
    if comm is None:
        res = pl.pallas_call(
            (lambda *refs: body(*refs[:n_in], *refs[n_in + len(behind):])) if behind else body,
            name=name, grid=grid, in_specs=in_specs + [_ANY] * len(behind), out_specs=out_specs,
            out_shape=out_shape, scratch_shapes=scratch_shapes, input_output_aliases=aliases,
            compiler_params=_params(sem))(*args, *behind)
        return list(res), []
    c_in, c_out = len(comm.inputs), len(comm.out_shapes)

    def hosted(*refs):
        pos = [0]

        def take(n):
            pos[0] += n
            return refs[pos[0] - n:pos[0]]

        ins, c_ins, _, outs, c_outs, scr = (take(n_in), take(c_in), take(len(behind)), take(n_out), take(c_out),
                                            take(n_sc))
        c_sems = refs[pos[0]:]
        ids = [pl.program_id(a) for a in range(len(grid))]
        first = functools.reduce(jnp.logical_and, [i == 0 for i in ids])
        last = functools.reduce(jnp.logical_and, [i == g - 1 for i, g in zip(ids, grid)])

        @pl.when(first)
        def _():
            barrier = pltpu.get_barrier_semaphore()
            pl.semaphore_signal(barrier, inc=1, device_id=_sibling(), device_id_type=_MESH)
            pl.semaphore_wait(barrier, 1)
            comm.start(c_ins, c_outs, c_sems)

        body(*ins, *outs, *scr)

        @pl.when(last)
        def _():
            comm.finish(c_ins, c_outs, c_sems)

    res = pl.pallas_call(
        hosted, name=name, grid=grid, in_specs=in_specs + [_ANY] * (c_in + len(behind)),
        out_specs=out_specs + [_ANY] * c_out,
        out_shape=out_shape + comm.out_shapes, scratch_shapes=scratch_shapes + comm.sems,
        input_output_aliases={**aliases, **{n_in + i: n_out + o for i, o in comm.aliases.items()}},
        compiler_params=_params(("arbitrary",) * len(grid), SIBLING_BARRIER_ID))(*args, *comm.inputs, *behind)
    return list(res[:n_out]), list(res[n_out:])


def _matmul(a, b, *, mode, out_dtypes, name, epilogue=None, extras=(), bm=1024, bn=1024, bk=2048,
            out_stack=0, comm=None, after=None, vecs=()):
    stacked = b.ndim == 3
    if mode == "nn":
        m, k = a.shape
        if stacked:
            nj, kb, ns = b.shape
            n, ks = nj * ns, k
        else:
            kb, n = b.shape
            ns, ks = n, k
        dn = (((1,), (0,)), ((), ()))
    elif mode == "nt":
        m, k = a.shape
        if stacked:
            nj, n, ks = b.shape
            kb = nj * ks
        else:
            n, kb = b.shape
            ks = kb
        ns = n
        dn = (((1,), (1,)), ((), ()))
    else:
        k, m = a.shape
        kb, n = b.shape
        ns, ks = n, k
        dn = (((0,), (0,)), ((), ()))
    assert k == kb and not (stacked and mode == "tn")
    ns_out = n // out_stack if out_stack else n
    whole_k = stacked and mode == "nt" and bk >= k
    bm, bn, bk = min(bm, m), min(bn, ns, ns_out), k if whole_k else min(bk, ks)
    assert m % bm == 0 and ns % bn == 0 and ns_out % bn == 0 and ks % bk == 0 or whole_k
    gm, gn, gk = m // bm, n // bn, k // bk

    if mode == "tn":
        a_spec = pl.BlockSpec((bk, bm), lambda i, j, q: (q, i))
    else:
        a_spec = pl.BlockSpec((bm, bk), lambda i, j, q: (i, q))
    if mode == "nt":
        if whole_k:
            b_spec = pl.BlockSpec((nj, bn, ks), lambda i, j, q: (0, j, 0))
        elif stacked:
            per = ks // bk
            b_spec = pl.BlockSpec((None, bn, bk), lambda i, j, q: (q // per, j, q % per))
        else:
            b_spec = pl.BlockSpec((bn, bk), lambda i, j, q: (j, q))
    else:
        if stacked:
            per = ns // bn
            b_spec = pl.BlockSpec((None, bk, bn), lambda i, j, q: (j // per, q, j % per))
        else:
            b_spec = pl.BlockSpec((bk, bn), lambda i, j, q: (q, j))
    ex_spec = pl.BlockSpec((bm, bn), lambda i, j, q: (i, j))
    if out_stack:
        per_o = ns_out // bn
        o_spec = pl.BlockSpec((None, bm, bn), lambda i, j, q: (j // per_o, i, j % per_o))
        o_shape = (out_stack, m, ns_out)
    else:
        o_spec = ex_spec
        o_shape = (m, n)
    n_ex, n_out = len(extras) + len(vecs), len(out_dtypes)

    def body(a_ref, b_ref, *rest):
        ex, outs = rest[:n_ex], rest[n_ex:n_ex + n_out]
        if whole_k:
            part = sum(lax.dot_general(a_ref[:, t * ks:(t + 1) * ks], b_ref[t], dn, preferred_element_type=F32)
                       for t in range(nj))
        else:
            part = lax.dot_general(a_ref[...], b_ref[...], dn, preferred_element_type=F32)

        def finish(acc):
            res = epilogue(acc, *[e[...] for e in ex]) if epilogue else (acc,)
            for o, r in zip(outs, res):
                o[...] = r.astype(o.dtype)

        if gk == 1:
            finish(part)
        else:
            acc_ref = rest[-1]
            q = pl.program_id(2)

            @pl.when(q == 0)
            def _():
                acc_ref[...] = part

            @pl.when(q > 0)
            def _():
                acc_ref[...] += part

            @pl.when(q == gk - 1)
            def _():
                finish(acc_ref[...])

    res, c_res = _call(
        body, name=name, grid=(gm, gn, gk),
        in_specs=[a_spec, b_spec] + [ex_spec] * len(extras)
        + [pl.BlockSpec((1, bn), lambda i, j, q: (0, j))] * len(vecs),
        out_specs=[o_spec] * n_out,
        out_shape=[jax.ShapeDtypeStruct(o_shape, dt) for dt in out_dtypes],
        scratch_shapes=[pltpu.VMEM((bm, bn), F32)] if gk > 1 else [],
        sem=("parallel", "parallel", "arbitrary"), args=(a, b, *extras, *vecs), comm=comm, after=after)
    res = res[0] if n_out == 1 else res
    return res if comm is None else (res, c_res)


def _rms_fwd(x, g, *, name, tm=256, comm=None):
    s, d = x.shape
    tm = min(tm, s)

    def body(x_ref, g_ref, o_ref):
        xf = x_ref[...]
        r = lax.rsqrt(jnp.mean(xf * xf, axis=-1, keepdims=True) + EPS)
        o_ref[...] = (xf * r * g_ref[...]).astype(o_ref.dtype)

    res, c_res = _call(
        body, name=name, grid=(s // tm,),
        in_specs=[pl.BlockSpec((tm, d), lambda i: (i, 0)), pl.BlockSpec((1, d), lambda i: (0, 0))],
        out_specs=[pl.BlockSpec((tm, d), lambda i: (i, 0))],
        out_shape=[jax.ShapeDtypeStruct((s, d), BF16)],
        sem=("parallel",), args=(x, g), comm=comm)
    return res[0] if comm is None else (res[0], c_res)


def _rms_bwd(x, dy, g, add, *, name, want_bf16, tm=256):
    s, d = x.shape
    tm = min(tm, s)

    def body(x_ref, dy_ref, g_ref, add_ref, dx_ref, *rest):
        dg_ref = rest[-1]
        i = pl.program_id(0)
        xf = x_ref[...]
        dyf = dy_ref[...].astype(F32)
        r = lax.rsqrt(jnp.mean(xf * xf, axis=-1, keepdims=True) + EPS)
        xh = xf * r
        dyg = dyf * g_ref[...]
        dx = r * (dyg - xh * jnp.mean(dyg * xh, axis=-1, keepdims=True))
        tot = add_ref[...] + dx
        dx_ref[...] = tot
        if want_bf16:
            rest[0][...] = tot.astype(BF16)
        part = jnp.sum(dyf * xh, axis=0, keepdims=True)

        @pl.when(i == 0)
        def _():
            dg_ref[...] = part

        @pl.when(i > 0)
        def _():
            dg_ref[...] += part

    row = pl.BlockSpec((tm, d), lambda i: (i, 0))
    vec = pl.BlockSpec((1, d), lambda i: (0, 0))
    out_specs = [row] + ([row] if want_bf16 else []) + [vec]
    out_shape = [jax.ShapeDtypeStruct((s, d), F32)]
    if want_bf16:
        out_shape.append(jax.ShapeDtypeStruct((s, d), BF16))
    out_shape.append(jax.ShapeDtypeStruct((1, d), F32))
    return pl.pallas_call(
        body,
        name=name,
        grid=(s // tm,),
        in_specs=[row, row, vec, row],
        out_specs=out_specs,
        out_shape=out_shape,
        compiler_params=_params(("arbitrary",)),
    )(x, dy, g, add)


def _tail(h2, gate, pp, target, g_ple, g_final, *, tm=128):
    s, d = h2.shape
    tm = min(tm, s)

    def body(h2_ref, gate_ref, pp_ref, t_ref, gp_ref, gf_ref, dh3_ref, dz_ref, dpp_ref, dgf_ref, dgp_ref, loss_ref):
        i = pl.program_id(0)
        ppf = pp_ref[...]
        gate_v = gate_ref[...]
        r_p = lax.rsqrt(jnp.mean(ppf * ppf, axis=-1, keepdims=True) + EPS)
        eh = ppf * r_p
        e = eh * gp_ref[...]
        h3 = h2_ref[...] + gate_v * e
        r_f = lax.rsqrt(jnp.mean(h3 * h3, axis=-1, keepdims=True) + EPS)
        yh = h3 * r_f
        diff = yh * gf_ref[...] - t_ref[...]
        loss_part = 0.5 * jnp.sum(jnp.mean(diff * diff, axis=-1, keepdims=True), axis=0, keepdims=True)
        dy = diff / d
        dgf = jnp.sum(dy * yh, axis=0, keepdims=True)
        dyg = dy * gf_ref[...]
        dh3 = r_f * (dyg - yh * jnp.mean(dyg * yh, axis=-1, keepdims=True))
        dh3_ref[...] = dh3
        de = dh3 * gate_v
        dz_ref[...] = (dh3 * e * gate_v * (1.0 - gate_v)).astype(BF16)
        dgp = jnp.sum(de * eh, axis=0, keepdims=True)
        deg = de * gp_ref[...]
        dpp_ref[...] = (r_p * (deg - eh * jnp.mean(deg * eh, axis=-1, keepdims=True))).astype(BF16)
        loss_row = jnp.broadcast_to(loss_part, (1, 128))

        @pl.when(i == 0)
        def _():
            dgf_ref[...] = dgf
            dgp_ref[...] = dgp
            loss_ref[...] = loss_row

        @pl.when(i > 0)
        def _():
            dgf_ref[...] += dgf
            dgp_ref[...] += dgp
            loss_ref[...] += loss_row

    row = pl.BlockSpec((tm, d), lambda i: (i, 0))
    vec = pl.BlockSpec((1, d), lambda i: (0, 0))
    return pl.pallas_call(
        body,
        name="tail_fwd_bwd",
        grid=(s // tm,),
        in_specs=[row, row, row, row, vec, vec],
        out_specs=[row, row, row, vec, vec, pl.BlockSpec((1, 128), lambda i: (0, 0))],
        out_shape=[
            jax.ShapeDtypeStruct((s, d), F32),
            jax.ShapeDtypeStruct((s, d), BF16),
            jax.ShapeDtypeStruct((s, d), BF16),
            jax.ShapeDtypeStruct((1, d), F32),
            jax.ShapeDtypeStruct((1, d), F32),
            jax.ShapeDtypeStruct((1, 128), F32),
        ],
        compiler_params=_params(("arbitrary",)),
    )(h2, gate, pp, target, g_ple, g_final)


def _rope_tables(s):
    rows = s // GRID_W
    half = HEAD_DIM // 2
    inv_freq = ROPE_THETA ** (-jnp.arange(0, half, 2, dtype=F32) / half)
    ang_r = jnp.arange(rows, dtype=jnp.int32).astype(F32)[:, None] * inv_freq
    ang_c = jnp.arange(GRID_W, dtype=jnp.int32).astype(F32)[:, None] * inv_freq
    cr, sr = (jnp.repeat(t, GRID_W, axis=0) for t in (jnp.cos(ang_r), jnp.sin(ang_r)))
    cc, sc = (jnp.tile(t, (rows, 1)) for t in (jnp.cos(ang_c), jnp.sin(ang_c)))
    cos_t = jnp.concatenate([cr, cr, cc, cc], axis=-1)
    sin_t = jnp.concatenate([-sr, sr, -sc, sc], axis=-1)
    return cos_t, sin_t


def _swap_quarters(x):
    lane = lax.broadcasted_iota(jnp.int32, x.shape, x.ndim - 1)
    up = pltpu.roll(x, HEAD_DIM - 32, x.ndim - 1)
    down = pltpu.roll(x, 32, x.ndim - 1)
    return jnp.where((lane % 64) < 32, up, down)


def _cols(first, count=1):
    return slice(first * HEAD_DIM, (first + count) * HEAD_DIM)


def _qk_prep(proj, g_q, g_k, cos_t, sin_t, *, tm=256, comm=None):
    s, n = proj.shape
    tm = min(tm, s)

    def body(x_ref, gq_ref, gk_ref, c_ref, s_ref, o_ref):
        cos_v, sin_v = c_ref[...], s_ref[...]
        for h in range(COL_VA):
            x = x_ref[:, _cols(h)]
            g = gq_ref[...] if h < COL_KA else gk_ref[...]
            xn = x * lax.rsqrt(jnp.mean(x * x, axis=-1, keepdims=True) + EPS) * g
            xr = xn * cos_v + _swap_quarters(xn) * sin_v
            if h < COL_KA:
                xr = xr * Q_SCALE
            o_ref[:, _cols(h)] = xr.astype(BF16)
        o_ref[:, _cols(COL_VA, 2)] = x_ref[:, _cols(COL_VA, 2)].astype(BF16)
        o_ref[:, _cols(COL_QB, N_HEADS_B)] = (x_ref[:, _cols(COL_QB, N_HEADS_B)] * Q_SCALE).astype(BF16)
        o_ref[:, _cols(COL_KB, 4)] = x_ref[:, _cols(COL_KB, 4)].astype(BF16)

    row = pl.BlockSpec((tm, n), lambda i: (i, 0))
    tab = pl.BlockSpec((tm, HEAD_DIM), lambda i: (i, 0))
    vec = pl.BlockSpec((1, HEAD_DIM), lambda i: (0, 0))
    res, c_res = _call(
        body, name="qk_prep", grid=(s // tm,),
        in_specs=[row, vec, vec, tab, tab],
        out_specs=[row],
        out_shape=[jax.ShapeDtypeStruct((s, n), BF16)],
        sem=("parallel",), args=(proj, g_q, g_k, cos_t, sin_t), comm=comm)
    return res[0] if comm is None else (res[0], c_res)


def _qk_bwd(dqa, dka, dva, dqb, dkpad, dvpad, proj, g_q, g_k, cos_t, sin_t, *, comm=None, after=None):
    s, n = proj.shape
    tm = min(PAD_LO, s)
    assert PAD_LO % tm == 0
    lo = PAD_LO // tm

    def body(dqa_ref, dka_ref, dva_ref, dqb_ref, dkb_ref, dvb_ref, x_ref, gq_ref, gk_ref, c_ref, s_ref,
             o_ref, dgq_ref, dgk_ref):
        i = pl.program_id(0)
        cos_v, sin_v = c_ref[...], s_ref[...]

        def head(d, x, g):
            dn = d * cos_v + _swap_quarters(d * sin_v)
            r = lax.rsqrt(jnp.mean(x * x, axis=-1, keepdims=True) + EPS)
            xh = x * r
            dng = dn * g
            dx = r * (dng - xh * jnp.mean(dng * xh, axis=-1, keepdims=True))
            return dx.astype(BF16), jnp.sum(dn * xh, axis=0, keepdims=True)

        acc_q = jnp.zeros((1, HEAD_DIM), F32)
        acc_k = jnp.zeros((1, HEAD_DIM), F32)
        for h in range(N_HEADS_A):
            o_ref[:, _cols(h)], part = head(dqa_ref[:, _cols(h)] * ATT_SCALE, x_ref[:, _cols(h)], gq_ref[...])
            acc_q = acc_q + part
        for h in range(N_KV_A):
            o_ref[:, _cols(COL_KA + h)], part = head(dka_ref[:, _cols(h)] * LN2, x_ref[:, _cols(COL_KA + h)],
                                                     gk_ref[...])
            acc_k = acc_k + part
        o_ref[:, _cols(COL_VA, 2)] = dva_ref[...].astype(BF16)
        o_ref[:, _cols(COL_QB, N_HEADS_B)] = (dqb_ref[...] * ATT_SCALE).astype(BF16)
        o_ref[:, _cols(COL_KB, 2)] = (dkb_ref[...] * LN2).astype(BF16)
        o_ref[:, _cols(COL_VB, 2)] = dvb_ref[...].astype(BF16)

        @pl.when(i == 0)
        def _():
            dgq_ref[...] = acc_q
            dgk_ref[...] = acc_k

        @pl.when(i > 0)
        def _():
            dgq_ref[...] += acc_q
            dgk_ref[...] += acc_k

    def rows(width, shift=0):
        return pl.BlockSpec((tm, width), lambda i: (i + shift, 0))

    kv_w = N_KV_A * HEAD_DIM
    q_w = N_HEADS_A * HEAD_DIM
    vec = pl.BlockSpec((1, HEAD_DIM), lambda i: (0, 0))
    res, c_res = _call(
        body, name="qk_bwd", grid=(s // tm,),
        in_specs=[rows(q_w), rows(kv_w), rows(kv_w), rows(q_w), rows(kv_w, lo), rows(kv_w, lo), rows(n),
                  vec, vec, rows(HEAD_DIM), rows(HEAD_DIM)],
        out_specs=[rows(n), vec, vec],
        out_shape=[
            jax.ShapeDtypeStruct((s, n), BF16),
            jax.ShapeDtypeStruct((1, HEAD_DIM), F32),
            jax.ShapeDtypeStruct((1, HEAD_DIM), F32),
        ],
        sem=("arbitrary",), args=(dqa, dka, dva, dqb, dkpad, dvpad, proj, g_q, g_k, cos_t, sin_t), comm=comm,
        after=after)
    return res if comm is None else (res, c_res)


_NT = (((1,), (1,)), ((), ()))
_TN = (((0,), (0,)), ((), ()))


def _attn_a_fwd(pb, *, tq=2048, sub=256, comm=None, after=None):
    s = pb.shape[0]
    tq = min(tq, s)

    sub = min(sub, tq)

    def body(q_ref, k_ref, v_ref, o_ref, lse_ref):
        k = k_ref[...]
        v = v_ref[...]
        for r in range(tq // sub):
            rows = pl.ds(r * sub, sub)
            sc = lax.dot_general(q_ref[rows, :], k, _NT, preferred_element_type=F32)
            m = jnp.max(sc, axis=-1, keepdims=True)
            p = jnp.exp2(sc - m)
            l = jnp.sum(p, axis=-1, keepdims=True)
            o = jnp.dot(p.astype(BF16), v, preferred_element_type=F32)
            o_ref[rows, :] = (o / l).astype(BF16)
            lse_ref[rows, :] = jnp.broadcast_to(m + jnp.log2(l), (sub, HEAD_DIM))

    res, c_res = _call(
        body, name="attn_a_fwd", grid=(N_HEADS_A, s // tq),
        in_specs=[
            pl.BlockSpec((tq, HEAD_DIM), lambda h, i: (i, COL_QA + h)),
            pl.BlockSpec((s, HEAD_DIM), lambda h, i: (0, COL_KA + h // GROUP)),
            pl.BlockSpec((s, HEAD_DIM), lambda h, i: (0, COL_VA + h // GROUP)),
        ],
        out_specs=[
            pl.BlockSpec((tq, HEAD_DIM), lambda h, i: (i, h)),
            pl.BlockSpec((None, tq, HEAD_DIM), lambda h, i: (h, i, 0)),
        ],
        out_shape=[
            jax.ShapeDtypeStruct((s, (N_HEADS_A + N_HEADS_B) * HEAD_DIM), BF16),
            jax.ShapeDtypeStruct((N_HEADS_A, s, HEAD_DIM), F32),
        ],
        sem=("parallel", "parallel"), args=(pb, pb, pb), comm=comm, after=after)
    return res if comm is None else (res, c_res)


def _attn_a_bwd(pb, att, datt, lse, *, tq=2048, sub=256, comm=None):
    s = pb.shape[0]
    tq = min(tq, s)
    sub = min(sub, tq)

    def body(q_ref, k_ref, v_ref, o_ref, do_ref, lse_ref, dq_ref, dk_ref, dv_ref):
        first = jnp.logical_and(pl.program_id(1) == 0, pl.program_id(2) == 0)
        k = k_ref[...]
        v = v_ref[...]
        dk = dv = None
        for r in range(tq // sub):
            rows = pl.ds(r * sub, sub)
            q = q_ref[rows, :]
            do = do_ref[rows, :]
            sc = lax.dot_general(q, k, _NT, preferred_element_type=F32)
            p = jnp.exp2(sc - lse_ref[rows, :][:, :1])
            dp = lax.dot_general(do, v, _NT, preferred_element_type=F32)
            delta = jnp.sum(do.astype(F32) * o_ref[rows, :].astype(F32), axis=-1, keepdims=True)
            ds = (p * (dp - delta)).astype(BF16)
            dq_ref[rows, :] = jnp.dot(ds, k, preferred_element_type=F32)
            dk_r = lax.dot_general(ds, q, _TN, preferred_element_type=F32)
            dv_r = lax.dot_general(p.astype(BF16), do, _TN, preferred_element_type=F32)
            dk = dk_r if dk is None else dk + dk_r
            dv = dv_r if dv is None else dv + dv_r

        @pl.when(first)
        def _():
            dk_ref[...] = dk
            dv_ref[...] = dv

        @pl.when(jnp.logical_not(first))
        def _():
            dk_ref[...] += dk
            dv_ref[...] += dv

    qmap = lambda kv, g, i: (i, kv * GROUP + g)
    res, c_res = _call(
        body, name="attn_a_bwd", grid=(N_KV_A, GROUP, s // tq),
        in_specs=[
            pl.BlockSpec((tq, HEAD_DIM), lambda kv, g, i: (i, COL_QA + kv * GROUP + g)),
            pl.BlockSpec((s, HEAD_DIM), lambda kv, g, i: (0, COL_KA + kv)),
            pl.BlockSpec((s, HEAD_DIM), lambda kv, g, i: (0, COL_VA + kv)),
            pl.BlockSpec((tq, HEAD_DIM), qmap),
            pl.BlockSpec((tq, HEAD_DIM), qmap),
            pl.BlockSpec((None, tq, HEAD_DIM), lambda kv, g, i: (kv * GROUP + g, i, 0)),
        ],
        out_specs=[
            pl.BlockSpec((tq, HEAD_DIM), qmap),
            pl.BlockSpec((s, HEAD_DIM), lambda kv, g, i: (0, kv)),
            pl.BlockSpec((s, HEAD_DIM), lambda kv, g, i: (0, kv)),
        ],
        out_shape=[
            jax.ShapeDtypeStruct((s, N_HEADS_A * HEAD_DIM), F32),
            jax.ShapeDtypeStruct((s, N_KV_A * HEAD_DIM), F32),
            jax.ShapeDtypeStruct((s, N_KV_A * HEAD_DIM), F32),
        ],
        sem=("arbitrary", "arbitrary", "arbitrary"), args=(pb, pb, pb, att, datt, lse), comm=comm)
    return res if comm is None else (res, c_res)


def _t5_bucket(rel):
    nb = N_BUCKETS // 2
    ret = jnp.where(rel > 0, nb, 0)
    n = jnp.abs(rel)
    max_exact = nb // 2
    nf = jnp.maximum(n, 1).astype(F32)
    large = max_exact + (jnp.log(nf / max_exact) / math.log(MAX_DISTANCE / max_exact)
                         * (nb - max_exact)).astype(jnp.int32)
    large = jnp.minimum(large, nb - 1)
    return ret + jnp.where(n < max_exact, n, large)


def _band_buckets():
    r = jnp.arange(BLOCK_Q, dtype=jnp.int32)
    j = jnp.arange(3 * BLOCK_Q, dtype=jnp.int32)
    return _t5_bucket((j[None, :] - BLOCK_Q) - r[:, None])


def _band_bias(bucket, table_ref, h):
    acc = jnp.zeros(bucket.shape, F32)
    for b in range(N_BUCKETS):
        acc = jnp.where(bucket == b, table_ref[b, h], acc)
    return acc


GQ = GROUP * BLOCK_Q


def _stack_heads(x):
    return jnp.concatenate([x[:, _cols(g)] for g in range(GROUP)], axis=0)


def _unstack_heads(x):
    return jnp.concatenate([x[g * BLOCK_Q:(g + 1) * BLOCK_Q] for g in range(GROUP)], axis=1)


def _group_bias(bucket, table_ref, kv):
    return jnp.concatenate([_band_bias(bucket, table_ref, kv * GROUP + g) * LOG2E for g in range(GROUP)], axis=0)


def _group_sink(sink_ref, kv):
    head = lax.broadcasted_iota(jnp.int32, (GQ, 1), 0) // BLOCK_Q
    snk = jnp.zeros((GQ, 1), F32)
    for g in range(GROUP):
        snk = jnp.where(head == g, sink_ref[0, kv * GROUP + g] * LOG2E, snk)
    return snk


def _band_mask(n, s):
    r = lax.broadcasted_iota(jnp.int32, (GQ, 3 * BLOCK_Q), 0) % BLOCK_Q
    j = lax.broadcasted_iota(jnp.int32, (GQ, 3 * BLOCK_Q), 1)
    rel = j - BLOCK_Q - r
    kabs = n * BLOCK_Q + j - BLOCK_Q
    return (jnp.abs(rel) <= WINDOW) & (kabs >= 0) & (kabs < s)


def _band_start(n):
    return pl.multiple_of(n * BLOCK_Q + (PAD_LO - BLOCK_Q), BLOCK_Q)


def _attn_b_fwd(pb, kpad, vpad, bucket, table, sink, att, *, comm=None, after=None):
    s = pb.shape[0]
    nblk = s // BLOCK_Q
    sp = kpad.shape[0]

    def body(table_ref, sink_ref, q0_ref, q1_ref, k_ref, v_ref, bucket_ref, _, o_ref, lse_ref, bias_ref):
        n = pl.program_id(0)

        @pl.when(n == 0)
        def _():
            for kv in range(N_KV_B):
                bias_ref[kv * GQ:(kv + 1) * GQ, :] = _group_bias(bucket_ref[...], table_ref, kv)

        band = pl.ds(_band_start(n), 3 * BLOCK_Q)
        mask = _band_mask(n, s)
        for kv, q_ref in enumerate((q0_ref, q1_ref)):
            kb = k_ref[band, _cols(kv)]
            vb = v_ref[band, _cols(kv)]
            sc = lax.dot_general(_stack_heads(q_ref[...]), kb, _NT, preferred_element_type=F32)
            sc = jnp.where(mask, sc + bias_ref[kv * GQ:(kv + 1) * GQ, :], NEG_INF)
            snk = _group_sink(sink_ref, kv)
            m = jnp.maximum(jnp.max(sc, axis=-1, keepdims=True), snk)
            p = jnp.exp2(sc - m)
            l = jnp.sum(p, axis=-1, keepdims=True) + jnp.exp2(snk - m)
            o = jnp.dot(p.astype(BF16), vb, preferred_element_type=F32)
            o_ref[:, _cols(kv * GROUP, GROUP)] = _unstack_heads((o / l).astype(BF16))
            lse = m + jnp.log2(l)
            for g in range(GROUP):
                lse_ref[kv * GROUP + g] = jnp.broadcast_to(lse[g * BLOCK_Q:(g + 1) * BLOCK_Q], (BLOCK_Q, HEAD_DIM))

    smem = pl.BlockSpec(memory_space=pltpu.SMEM)
    wide = GROUP * HEAD_DIM
    whole = pl.BlockSpec((sp, N_KV_B * HEAD_DIM), lambda n: (0, 0))
    res, c_res = _call(
        body, name="attn_b_fwd", grid=(nblk,),
        in_specs=[
            smem,
            smem,
            pl.BlockSpec((BLOCK_Q, wide), lambda n: (n, COL_QB // GROUP)),
            pl.BlockSpec((BLOCK_Q, wide), lambda n: (n, COL_QB // GROUP + 1)),
            whole,
            whole,
            pl.BlockSpec((BLOCK_Q, 3 * BLOCK_Q), lambda n: (0, 0)),
            _ANY,
        ],
        out_specs=[
            pl.BlockSpec((BLOCK_Q, N_HEADS_B * HEAD_DIM), lambda n: (n, 1)),
            pl.BlockSpec((N_HEADS_B, BLOCK_Q, HEAD_DIM), lambda n: (0, n, 0)),
        ],
        out_shape=[
            jax.ShapeDtypeStruct(att.shape, BF16),
            jax.ShapeDtypeStruct((N_HEADS_B, s, HEAD_DIM), F32),
        ],
        scratch_shapes=[pltpu.VMEM((N_KV_B * GQ, 3 * BLOCK_Q), F32)],
        sem=("arbitrary",), args=(table, sink, pb, pb, kpad, vpad, bucket, att), comm=comm, after=after,
        aliases={7: 0})
    return res if comm is None else (res, c_res)


def _attn_b_bwd(pb, kpad, vpad, att, datt, lse, bucket, table, sink, *, comm=None, after=None):
    s = pb.shape[0]
    nblk = s // BLOCK_Q
    sp = kpad.shape[0]

    def body(table_ref, sink_ref, q0_ref, q1_ref, k_ref, v_ref, o_ref, do_ref, lse_ref, bucket_ref,
             dq_ref, dk_ref, dv_ref, dtab_ref, dsink_ref, bias_ref, dbias_ref):
        n = pl.program_id(0)

        @pl.when(n == 0)
        def _():
            dk_ref[...] = jnp.zeros_like(dk_ref)
            dv_ref[...] = jnp.zeros_like(dv_ref)
            dbias_ref[...] = jnp.zeros_like(dbias_ref)
            dsink_ref[...] = jnp.zeros_like(dsink_ref)
            for kv in range(N_KV_B):
                bias_ref[kv * GQ:(kv + 1) * GQ, :] = _group_bias(bucket_ref[...], table_ref, kv)

        band = pl.ds(_band_start(n), 3 * BLOCK_Q)
        mask = _band_mask(n, s)
        for kv, q_ref in enumerate((q0_ref, q1_ref)):
            wide_cols = _cols(kv * GROUP, GROUP)
            q = _stack_heads(q_ref[...])
            do = _stack_heads(do_ref[:, wide_cols])
            o = _stack_heads(o_ref[:, wide_cols])
            kb = k_ref[band, _cols(kv)]
            vb = v_ref[band, _cols(kv)]
            lse = jnp.concatenate([lse_ref[kv * GROUP + g][:, :1] for g in range(GROUP)], axis=0)
            sc = lax.dot_general(q, kb, _NT, preferred_element_type=F32)
            sc = jnp.where(mask, sc + bias_ref[kv * GQ:(kv + 1) * GQ, :], NEG_INF)
            p = jnp.exp2(sc - lse)
            dp = lax.dot_general(do, vb, _NT, preferred_element_type=F32)
            delta = jnp.sum(do.astype(F32) * o.astype(F32), axis=-1, keepdims=True)
            ds = p * (dp - delta)
            dsb = ds.astype(BF16)
            dq_ref[:, wide_cols] = _unstack_heads(jnp.dot(dsb, kb, preferred_element_type=F32))
            dk_ref[band, _cols(kv)] += lax.dot_general(dsb, q, _TN, preferred_element_type=F32)
            dv_ref[band, _cols(kv)] += lax.dot_general(p.astype(BF16), do, _TN, preferred_element_type=F32)
            dbias_ref[kv * GQ:(kv + 1) * GQ, :] += ds
            sink_part = -jnp.exp2(_group_sink(sink_ref, kv) - lse) * delta
            for g in range(GROUP):
                rows = slice(g * BLOCK_Q, (g + 1) * BLOCK_Q)
                dsink_ref[kv * GROUP + g] += jnp.broadcast_to(
                    jnp.sum(sink_part[rows], axis=0, keepdims=True), (1, HEAD_DIM))

        @pl.when(n == nblk - 1)
        def _():
            bucket_v = bucket_ref[...]
            row = lax.broadcasted_iota(jnp.int32, (N_BUCKETS, HEAD_DIM), 0)
            for h in range(N_HEADS_B):
                acc = dbias_ref[h * BLOCK_Q:(h + 1) * BLOCK_Q, :]
                tot = jnp.zeros((N_BUCKETS, HEAD_DIM), F32)
                for b in range(N_BUCKETS):
                    tot = jnp.where(row == b, jnp.sum(jnp.where(bucket_v == b, acc, 0.0), keepdims=True), tot)
                dtab_ref[h] = tot

    smem = pl.BlockSpec(memory_space=pltpu.SMEM)
    wide = GROUP * HEAD_DIM
    whole = pl.BlockSpec((sp, N_KV_B * HEAD_DIM), lambda n: (0, 0))
    group_b = pl.BlockSpec((BLOCK_Q, N_HEADS_B * HEAD_DIM), lambda n: (n, 1))
    res, c_res = _call(
        body, name="attn_b_bwd", grid=(nblk,),
        in_specs=[
            smem,
            smem,
            pl.BlockSpec((BLOCK_Q, wide), lambda n: (n, COL_QB // GROUP)),
            pl.BlockSpec((BLOCK_Q, wide), lambda n: (n, COL_QB // GROUP + 1)),
            whole,
            whole,
            group_b,
            group_b,
            pl.BlockSpec((N_HEADS_B, BLOCK_Q, HEAD_DIM), lambda n: (0, n, 0)),
            pl.BlockSpec((BLOCK_Q, 3 * BLOCK_Q), lambda n: (0, 0)),
        ],
        out_specs=[
            pl.BlockSpec((BLOCK_Q, N_HEADS_B * HEAD_DIM), lambda n: (n, 0)),
            whole,
            whole,
            pl.BlockSpec((N_HEADS_B, N_BUCKETS, HEAD_DIM), lambda n: (0, 0, 0)),
            pl.BlockSpec((N_HEADS_B, 1, HEAD_DIM), lambda n: (0, 0, 0)),
        ],
        out_shape=[
            jax.ShapeDtypeStruct((s, N_HEADS_B * HEAD_DIM), F32),
            jax.ShapeDtypeStruct((sp, N_KV_B * HEAD_DIM), F32),
            jax.ShapeDtypeStruct((sp, N_KV_B * HEAD_DIM), F32),
            jax.ShapeDtypeStruct((N_HEADS_B, N_BUCKETS, HEAD_DIM), F32),
            jax.ShapeDtypeStruct((N_HEADS_B, 1, HEAD_DIM), F32),
        ],
        scratch_shapes=[pltpu.VMEM((N_KV_B * GQ, 3 * BLOCK_Q), F32), pltpu.VMEM((N_KV_B * GQ, 3 * BLOCK_Q), F32)],
        sem=("arbitrary",),
        args=(table, sink, pb, pb, kpad, vpad, att, datt, lse, bucket), comm=comm, after=after)
    return res if comm is None else (res, c_res)


_MESH = pl.DeviceIdType.MESH


def _other_chips(x, y):
    return [(x, 1 - y), (1 - x, y), (1 - x, 1 - y)]


_HBM = pl.BlockSpec(memory_space=pltpu.HBM)
_SEM = pl.BlockSpec(memory_space=pltpu.SEMAPHORE)
_SPLIT = pltpu.CompilerParams(has_side_effects=pltpu.SideEffectType.DATAFLOW_SIDE_EFFECTING)


def _in_hbm(a):
    return pltpu.with_memory_space_constraint(a, pltpu.HBM)


def _my_half(rows):
    c = lax.axis_index("c")
    half = rows // 2
    return pl.ds(pl.multiple_of(c * half, half), half), pl.ds(pl.multiple_of((1 - c) * half, half), half)


def _gather_route(shapes):
    def route(src, land):
        x, y, c = lax.axis_index("x"), lax.axis_index("y"), lax.axis_index("c")
        out = []
        for t, shape in enumerate(shapes):
            mine, _ = _my_half(shape[0])
            for px, py in _other_chips(x, y):
                out.append((src[t].at[mine], land[t].at[2 * x + y, mine], land[t].at[2 * px + py, mine], (px, py, c)))
        return out

    return route


def _exchange_route(n_t):
    def route(src, land):
        x, y, c = lax.axis_index("x"), lax.axis_index("y"), lax.axis_index("c")
        out = []
        for t in range(n_t):
            for px, py in _other_chips(x, y):
                k = 2 * px + py
                out.append((src[t].at[k], land[t].at[2 * (2 * x + y) + c], land[t].at[2 * k + c], (px, py, c)))
        return out

    return route


def _own_slot(shape, dtype, slot, block):
    return lax.dynamic_update_slice(lax.empty(shape, dtype), block[None], (slot,) + (0,) * (len(shape) - 1))


def _split_start(name, srcs, lands, route, after):
    n = len(srcs)

    def body(*refs):
        src, land, send_sems, recv_sems, token = refs[:n], refs[n:2 * n], refs[2 * n + 1], refs[2 * n + 2], refs[-1]
        for i, (src_ref, dst_ref, _, to) in enumerate(route(src, land)):
            pltpu.make_async_remote_copy(src_ref=src_ref, dst_ref=dst_ref, send_sem=send_sems.at[i],
                                         recv_sem=recv_sems.at[i], device_id=to, device_id_type=_MESH).start()
        token[...] = jnp.zeros_like(token)

    sem = pltpu.SemaphoreType.DMA((3 * n,))
    lands = list(lands)
    res = pl.pallas_call(
        body, name=name,
        in_specs=[_HBM] * (2 * n) + [_ANY],
        out_specs=[_SEM, _SEM] + [_HBM] * (2 * n) + [pl.BlockSpec(memory_space=pltpu.VMEM)],
        out_shape=[sem, sem] + [pltpu.HBM(a.shape, a.dtype) for a in list(srcs) + lands]
        + [jax.ShapeDtypeStruct((8, 128), F32)],
        input_output_aliases={i: 2 + i for i in range(2 * n)},
        compiler_params=_SPLIT,
    )(*[_in_hbm(a) for a in srcs], *[_in_hbm(a) for a in lands], after)
    return (res[0], res[1]), res[2:2 + n], res[2 + n:2 + 2 * n], res[-1]


def _split_wait(name, srcs, lands, sems, route, after):
    n = len(srcs)

    def body(*refs):
        src, land, send_sems, recv_sems = refs[:n], refs[n:2 * n], refs[2 * n], refs[2 * n + 1]
        for i, (src_ref, _, dst_ref, to) in enumerate(route(src, land)):
            cp = pltpu.make_async_remote_copy(src_ref=src_ref, dst_ref=dst_ref, send_sem=send_sems.at[i],
                                              recv_sem=recv_sems.at[i], device_id=to, device_id_type=_MESH)
            cp.wait_send()
            cp.wait_recv()

    res = pl.pallas_call(
        body, name=name,
        in_specs=[_HBM] * (2 * n) + [_SEM, _SEM, _ANY],
        out_specs=[_HBM] * (2 * n),
        out_shape=[pltpu.HBM(a.shape, a.dtype) for a in list(srcs) + list(lands)],
        input_output_aliases={i: i for i in range(2 * n)},
        compiler_params=_SPLIT,
    )(*srcs, *lands, sems[0], sems[1], after)
    return res[:n], res[n:]


def _comm_only(name, comm):
    return _call(lambda: None, name=name, grid=(1,), in_specs=[], out_specs=[], out_shape=[], args=(), comm=comm)[1]


def _swap_comm(shards, lands):
    n_t = len(lands)

    def copies(land, sems, later):
        send_sems, recv_sems = sems
        x, y = lax.axis_index("x"), lax.axis_index("y")
        sends, recvs = [], []
        for t in range(n_t):
            mine, other = _my_half(shards[t].shape[0])
            for j, (px, py) in enumerate(_other_chips(x, y)):
                k = 2 * px + py
                for part, out in ((mine, sends), (other, recvs)) if later else ((mine, sends),):
                    out.append(pltpu.make_async_remote_copy(
                        src_ref=land[t].at[k, part], dst_ref=land[t].at[k, part], send_sem=send_sems.at[3 * t + j],
                        recv_sem=recv_sems.at[3 * t + j], device_id=_sibling(), device_id_type=_MESH))
        return sends, recvs

    def start(ins, land, sems):
        for cp in copies(land, sems, False)[0]:
            cp.start()

    def finish(ins, land, sems):
        sends, recvs = copies(land, sems, True)
        for cp in recvs:
            cp.wait_recv()
        for cp in sends:
            cp.wait_send()

    return _Comm(
        lands, [jax.ShapeDtypeStruct(a.shape, a.dtype) for a in lands],
        [pltpu.SemaphoreType.DMA((3 * n_t,)), pltpu.SemaphoreType.DMA((3 * n_t,))],
        start, finish, aliases={t: t for t in range(n_t)})


def _forward_comm(partials, lands):
    n_t = len(lands)

    def copies(land, sems, later):
        send_sems, recv_sems = sems
        x, y, c = lax.axis_index("x"), lax.axis_index("y"), lax.axis_index("c")
        sends, recvs = [], []
        for t in range(n_t):
            for j, k in enumerate([2 * x + y] + [2 * px + py for px, py in _other_chips(x, y)]):
                for slot, out in ((2 * k + c, sends), (2 * k + 1 - c, recvs)) if later else ((2 * k + c, sends),):
                    out.append(pltpu.make_async_remote_copy(
                        src_ref=land[t].at[slot], dst_ref=land[t].at[slot], send_sem=send_sems.at[4 * t + j],
                        recv_sem=recv_sems.at[4 * t + j], device_id=_sibling(), device_id_type=_MESH))
        return sends, recvs

    def start(ins, land, sems):
        for cp in copies(land, sems, False)[0]:
            cp.start()

    def finish(ins, land, sems):
        sends, recvs = copies(land, sems, True)
        for cp in recvs:
            cp.wait_recv()
        for cp in sends:
            cp.wait_send()

    return _Comm(
        lands, [jax.ShapeDtypeStruct(a.shape, a.dtype) for a in lands],
        [pltpu.SemaphoreType.DMA((4 * n_t,)), pltpu.SemaphoreType.DMA((4 * n_t,))],
        start, finish, aliases={t: t for t in range(n_t)})


def _allreduce_small(pack):
    rows, d = pack.shape

    def body(p_ref, sum_ref, all_ref, send_sems, recv_sems):
        x, y, c = lax.axis_index("x"), lax.axis_index("y"), lax.axis_index("c")
        me = 4 * x + 2 * y + c
        all_ref[me] = p_ref[...]
        peers = []
        for dx in range(2):
            for dy in range(2):
                for dc in range(2):
                    if dx or dy or dc:
                        px = 1 - x if dx else x
                        py = 1 - y if dy else y
                        pc = 1 - c if dc else c
                        peers.append((4 * dx + 2 * dy + dc - 1, (px, py, pc)))
        sends = []
        for k, to in peers:
            cp = pltpu.make_async_remote_copy(
                src_ref=p_ref, dst_ref=all_ref.at[me], send_sem=send_sems.at[k], recv_sem=recv_sems.at[k],
                device_id=to, device_id_type=_MESH)
            cp.start()
            sends.append(cp)
        for k, (px, py, pc) in peers:
            pltpu.make_async_remote_copy(
                src_ref=p_ref, dst_ref=all_ref.at[4 * px + 2 * py + pc], send_sem=send_sems.at[k],
                recv_sem=recv_sems.at[k], device_id=(px, py, pc), device_id_type=_MESH).wait_recv()
        for cp in sends:
            cp.wait_send()
        tot = all_ref[0]
        for i in range(1, N_DEV):
            tot = tot + all_ref[i]
        sum_ref[...] = tot

    vm = pl.BlockSpec(memory_space=pltpu.VMEM)
    return pl.pallas_call(
        body,
        name="allreduce_small",
        in_specs=[vm],
        out_specs=vm,
        out_shape=jax.ShapeDtypeStruct((rows, d), F32),
        scratch_shapes=[
            pltpu.VMEM((N_DEV, rows, d), F32),
            pltpu.SemaphoreType.DMA((N_DEV - 1,)),
            pltpu.SemaphoreType.DMA((N_DEV - 1,)),
        ],
    )(pack)


def _adamw_math(w, g, m, v):
    m = ADAM_B1 * m + (1.0 - ADAM_B1) * g
    v = ADAM_B2 * v + (1.0 - ADAM_B2) * (g * g)
    m_hat = m / (1.0 - ADAM_B1 ** ADAM_STEP)
    v_hat = v / (1.0 - ADAM_B2 ** ADAM_STEP)
    delta = -ADAM_LR * (m_hat / (jnp.sqrt(v_hat) + ADAM_EPS) + ADAM_WD * w)
    return delta, m, v


def _sum_adamw(parts, w, m, v, *, name, tr=256):
    r, c = w.shape
    tr = min(tr, r)
    tc = min(c, 1024)

    def body(p_ref, w_ref, m_ref, v_ref, g_ref, d_ref, m2_ref, v2_ref):
        g = p_ref[0].astype(F32)
        for i in range(1, N_DEV):
            g = g + p_ref[i].astype(F32)
        delta, m2, v2 = _adamw_math(w_ref[...], g, m_ref[...], v_ref[...])
        g_ref[...] = g
        d_ref[...] = delta
        m2_ref[...] = m2
        v2_ref[...] = v2

    blk = pl.BlockSpec((tr, tc), lambda i, j: (i, j))
    return pl.pallas_call(
        body,
        name=name,
        grid=(r // tr, c // tc),
        in_specs=[pl.BlockSpec((N_DEV, tr, tc), lambda i, j: (0, i, j)), blk, blk, blk],
        out_specs=[blk] * 4,
        out_shape=[jax.ShapeDtypeStruct((r, c), F32)] * 4,
        compiler_params=_params(("parallel", "parallel")),
    )(parts, w, m, v)


def _adamw_small(g, w, m, v):
    def body(g_ref, w_ref, m_ref, v_ref, d_ref, m2_ref, v2_ref):
        delta, m2, v2 = _adamw_math(w_ref[...], g_ref[...], m_ref[...], v_ref[...])
        d_ref[...] = delta
        m2_ref[...] = m2
        v2_ref[...] = v2

    vm = pl.BlockSpec(memory_space=pltpu.VMEM)
    return pl.pallas_call(
        body,
        name="adamw_small",
        in_specs=[vm] * 4,
        out_specs=[vm] * 3,
        out_shape=[jax.ShapeDtypeStruct(g.shape, F32)] * 3,
    )(g, w, m, v)


def _relu2_epilogue(acc):
    ra = jnp.maximum(acc, 0.0)
    return ra * ra, ra


def _residual_norm_epilogue(acc, res, g):
    h = acc + res
    return h, h * lax.rsqrt(jnp.mean(h * h, axis=-1, keepdims=True) + EPS) * g


def _rows(stacked):
    return stacked.reshape(stacked.shape[0] * stacked.shape[1], stacked.shape[2])


def _by_chip(mat):
    return mat.reshape(N_CHIPS, mat.shape[0] // N_CHIPS, mat.shape[1])


def _local_step(x, p, target, shards, small, update):
    s, d = x.shape
    cos_t, sin_t = _rope_tables(s)
    bucket = _band_buckets()
    p_bf = p.astype(BF16)
    wts = {}

    chip = 2 * lax.axis_index("x") + lax.axis_index("y")
    core = lax.axis_index("c")

    def gather(tag, names, after):
        srcs = [cast[n] for n in names]
        route = _gather_route([a.shape for a in srcs])
        sems, srcs, lands, token = _split_start(f"gather_start_{tag}", srcs, [zones[n] for n in names], route, after)

        def landed(done):
            got_srcs, got_lands = _split_wait(f"gather_wait_{tag}", srcs, lands, sems, route, done)
            comm = _swap_comm(got_srcs, got_lands)
            comm.waited = got_srcs[0]
            return comm

        return landed, token

    def prepare(n, zero):
        cast[n] = (shards[n] + zero).astype(BF16)
        zones[n] = _own_slot((N_CHIPS,) + cast[n].shape, BF16, chip, cast[n])

    cast, zones = {}, {}
    prepare("w_in", 0.0)
    in_landed, token = gather("in", ["w_in"], small["attn_norm_g"])
    for n in shards:
        if n != "w_in":
            prepare(n, token[:1, :1])
    g_attn = small["attn_norm_g"] + token[:1, :1]
    u = _rms_fwd(x, g_attn, name="norm_attn")
    prepared = u[:1, :1].astype(F32) + sum(
        (lax.dynamic_slice(zones[n], (chip, 0, 0), (1, 1, 1))[0] + cast[n][:1, :1]).astype(F32)
        for n in zones if n != "w_in")
    (wts["w_in"],) = _comm_only("swap_w_in", in_landed(prepared))
    mid_landed, token = gather("mid", ["w_out"], wts["w_in"])
    proj = _matmul(u, wts["w_in"], mode="nn", out_dtypes=[F32], name="mm_in", bn=768, after=token)
    pb, (w_out_s,) = _qk_prep(proj, small["q_norm_g"], small["k_norm_g"], cos_t, sin_t, comm=mid_landed(proj))
    wts["w_out"] = _rows(w_out_s)
    up_landed, token = gather("up", ["w_up"], pb)
    att_a, lse_a = _attn_a_fwd(pb, after=token)
    pad = ((PAD_LO, PAD_HI), (0, 0))
    kpad = jnp.pad(pb[:, COL_KB * HEAD_DIM:COL_VB * HEAD_DIM], pad)
    vpad = jnp.pad(pb[:, COL_VB * HEAD_DIM:], pad)
    up_swap = up_landed(att_a)
    down_landed, token = gather("down", ["w_down"], up_swap.waited)
    (att, lse_b), (wts["w_up"],) = _attn_b_fwd(pb, kpad, vpad, bucket, small["rel_bias_table"],
                                               small["sink_logits"], att_a, comm=up_swap, after=token)
    h1, mn = _matmul(att, wts["w_out"], mode="nn", out_dtypes=[F32, BF16], name="mm_out", bm=512, bn=d,
                     epilogue=_residual_norm_epilogue, extras=(x,), vecs=(small["mlp_norm_g"],))
    r, ra = _matmul(mn, wts["w_up"], mode="nn", out_dtypes=[BF16, BF16], name="mm_up", epilogue=_relu2_epilogue,
                    bm=2048)
    (w_down_s,) = _comm_only("swap_w_down", down_landed(r))
    wts["w_down"] = _rows(w_down_s)
    late_landed, token = gather("late", ["w_gate", "ple_w"], w_down_s)
    h2, ng = _matmul(r, wts["w_down"], mode="nn", out_dtypes=[F32, BF16], name="mm_down", bm=512, bn=d, bk=1024,
                     epilogue=_residual_norm_epilogue, extras=(h1,), vecs=(small["gate_norm_g"],), after=token)
    w_gate_s, wts["ple_w"] = _comm_only("swap_late", late_landed(h2))
    wts["w_gate"] = _rows(w_gate_s)
    gate = _matmul(ng, wts["w_gate"], mode="nn", out_dtypes=[F32], name="mm_gate",
                   epilogue=lambda acc: (1.0 / (1.0 + jnp.exp(-acc)),))
    pp = _matmul(p_bf, wts["ple_w"], mode="nn", out_dtypes=[F32], name="mm_ple", bn=512)
    dh3, dz, dpp, dg_final, dg_ple, loss = _tail(h2, gate, pp, target, small["ple_norm_g"], small["final_norm_g"])

    dng = _matmul(dz, wts["w_gate"], mode="nt", out_dtypes=[F32], name="mm_gate_dx")
    gw_gate = _matmul(ng, dz, mode="tn", out_dtypes=[BF16], name="mm_gate_dw")
    gw_ple = _matmul(p_bf, dpp, mode="tn", out_dtypes=[BF16], name="mm_ple_dw", bn=512, out_stack=N_CHIPS)
    dh2, dh2_bf, dg_gate = _rms_bwd(h2, dng, small["gate_norm_g"], dh3, name="norm_gate_bwd", want_bf16=True)

    def exchange(tag, partials, after):
        route = _exchange_route(len(partials))
        lands = [_own_slot((N_DEV,) + g.shape[1:], g.dtype, 2 * chip + core,
                           lax.dynamic_index_in_dim(g, chip, 0, keepdims=False)) for g in partials]
        sems, srcs, lands, token = _split_start(f"exchange_start_{tag}", partials, lands, route, after)

        def landed(done):
            got_srcs, got_lands = _split_wait(f"exchange_wait_{tag}", srcs, lands, sems, route, done)
            comm = _forward_comm(got_srcs, got_lands)
            comm.waited = got_srcs[0]
            return comm

        return landed, token

    big = {}
    gate_landed, token = exchange("gate", [_by_chip(gw_gate), gw_ple], dh2_bf)
    gw_down = _matmul(r, dh2_bf, mode="tn", out_dtypes=[BF16], name="mm_down_dw", after=token, bm=512, bk=4096)
    da, (parts_gate, parts_ple) = _matmul(
        dh2_bf, wts["w_down"], mode="nt", out_dtypes=[BF16], name="mm_down_dx", bm=2048,
        epilogue=lambda acc, ra_v: (acc * (2.0 * ra_v.astype(F32)),), extras=(ra,), comm=gate_landed(gw_down))
    down_landed, token = exchange("down", [_by_chip(gw_down)], da)
    big["w_gate"], big["ple_w"] = update("w_gate", parts_gate), update("ple_w", parts_ple)
    gw_up = _matmul(mn, da, mode="tn", out_dtypes=[BF16], name="mm_up_dw", out_stack=N_CHIPS, after=token,
                    bm=512, bk=4096)
    dmn = _matmul(da, wts["w_up"], mode="nt", out_dtypes=[F32], name="mm_up_dx")
    dh1, dh1_bf, dg_mlp = _rms_bwd(h1, dmn, small["mlp_norm_g"], dh2, name="norm_mlp_bwd", want_bf16=True)
    datt, (parts_down,) = _matmul(dh1_bf, wts["w_out"], mode="nt", out_dtypes=[BF16], name="mm_out_dx",
                                  comm=down_landed(dh1_bf))
    gw_out = _matmul(att, dh1_bf, mode="tn", out_dtypes=[BF16], name="mm_out_dw")
    up_landed, token = exchange("up", [gw_up], datt)
    dqb, dkpad, dvpad, dtab, dsink = _attn_b_bwd(pb, kpad, vpad, att, datt, lse_b, bucket,
                                                 small["rel_bias_table"], small["sink_logits"], after=token)
    dqa, dka, dva = _attn_a_bwd(pb, att, datt, lse_a)
    up_forward = up_landed(dqa)
    out_landed, token = exchange("out", [_by_chip(gw_out)], up_forward.waited)
    (dproj, dg_q, dg_k), (parts_up,) = _qk_bwd(dqa, dka, dva, dqb, dkpad, dvpad, proj,
                                               small["q_norm_g"], small["k_norm_g"], cos_t, sin_t,
                                               comm=up_forward, after=token)
    gw_in = _matmul(u, dproj, mode="tn", out_dtypes=[BF16], name="mm_in_dw", bn=768, out_stack=N_CHIPS)
    out_forward = out_landed(gw_in)
    in_landed, token = exchange("in", [gw_in], out_forward.waited)
    du, (parts_out,) = _matmul(dproj, wts["w_in"], mode="nt", out_dtypes=[F32], name="mm_in_dx", bk=3072,
                               comm=out_forward, after=token)
    grad_x, dg_attn = _rms_bwd(x, du, small["attn_norm_g"], dh1, name="norm_attn_bwd", want_bf16=False)
    for n, parts in (("w_down", parts_down), ("w_up", parts_up), ("w_out", parts_out)):
        big[n] = update(n, parts)
    done = dg_attn + sum(big[n][0][0, :1, :] for n in ("w_down", "w_up", "w_out"))
    (parts_in,) = _comm_only("forward_w_in", in_landed(done))
    big["w_in"] = update("w_in", parts_in)

    small_g = {
        "attn_norm_g": dg_attn, "mlp_norm_g": dg_mlp, "ple_norm_g": dg_ple, "gate_norm_g": dg_gate,
        "final_norm_g": dg_final, "q_norm_g": dg_q, "k_norm_g": dg_k,
        "sink_logits": dsink[:, 0, 0][None, :], "rel_bias_table": dtab[:, :, 0].T,
    }
    return loss, grad_x, big, small_g


_SMALL_ROWS = ["attn_norm_g", "mlp_norm_g", "ple_norm_g", "gate_norm_g", "final_norm_g"]
_PACK_ROWS = 8


def _pack_small(vals, d):
    rows = [vals[n].reshape(1, d) for n in _SMALL_ROWS]
    misc = jnp.concatenate([
        vals["q_norm_g"].reshape(1, HEAD_DIM), vals["k_norm_g"].reshape(1, HEAD_DIM),
        jnp.pad(vals["sink_logits"].reshape(1, N_HEADS_B), ((0, 0), (0, HEAD_DIM - N_HEADS_B))),
        vals["rel_bias_table"].reshape(1, N_BUCKETS * N_HEADS_B)], axis=1)
    rows.append(jnp.pad(misc, ((0, 0), (0, d - misc.shape[1]))))
    rows.append(jnp.zeros((_PACK_ROWS - len(rows), d), F32))
    return jnp.concatenate(rows, axis=0).astype(F32)


def _unpack_small(pack, shapes):
    out = {n: pack[i].reshape(shapes[n]) for i, n in enumerate(_SMALL_ROWS)}
    misc = pack[len(_SMALL_ROWS)]
    out["q_norm_g"] = misc[:HEAD_DIM].reshape(shapes["q_norm_g"])
    out["k_norm_g"] = misc[HEAD_DIM:2 * HEAD_DIM].reshape(shapes["k_norm_g"])
    out["sink_logits"] = misc[2 * HEAD_DIM:2 * HEAD_DIM + N_HEADS_B].reshape(shapes["sink_logits"])
    out["rel_bias_table"] = misc[3 * HEAD_DIM:3 * HEAD_DIM + N_BUCKETS * N_HEADS_B].reshape(shapes["rel_bias_table"])
    return out


_WEIGHTS = ["attn_norm_g", "w_in", "q_norm_g", "k_norm_g", "sink_logits", "w_out", "mlp_norm_g", "w_up", "w_down",
            "ple_w", "ple_norm_g", "gate_norm_g", "w_gate", "rel_bias_table", "final_norm_g"]
_BIG = ["w_in", "w_out", "w_up", "w_down", "ple_w", "w_gate"]


def kernel(x, p, attn_norm_g, w_in, q_norm_g, k_norm_g, sink_logits, w_out, mlp_norm_g, w_up, w_down, ple_w, ple_norm_g, gate_norm_g, w_gate, rel_bias_table, final_norm_g, loss_target, m_attn_norm_g, m_w_in, m_q_norm_g, m_k_norm_g, m_sink_logits, m_w_out, m_mlp_norm_g, m_w_up, m_w_down, m_ple_w, m_ple_norm_g, m_gate_norm_g, m_w_gate, m_rel_bias_table, m_final_norm_g, v_attn_norm_g, v_w_in, v_q_norm_g, v_k_norm_g, v_sink_logits, v_w_out, v_mlp_norm_g, v_w_up, v_w_down, v_ple_w, v_ple_norm_g, v_gate_norm_g, v_w_gate, v_rel_bias_table, v_final_norm_g):
    given = dict(locals())
    w = {n: given[n] for n in _WEIGHTS}
    m = {n: given["m_" + n] for n in _WEIGHTS}
    v = {n: given["v_" + n] for n in _WEIGHTS}
    d = x.shape[-1]

    shards = {n: w[n][0] for n in _BIG}
    small = {
        "attn_norm_g": w["attn_norm_g"], "mlp_norm_g": w["mlp_norm_g"], "ple_norm_g": w["ple_norm_g"],
        "gate_norm_g": w["gate_norm_g"], "final_norm_g": w["final_norm_g"].reshape(1, d),
        "q_norm_g": w["q_norm_g"], "k_norm_g": w["k_norm_g"], "sink_logits": w["sink_logits"],
        "rel_bias_table": w["rel_bias_table"],
    }

    def update(n, parts):
        res = _sum_adamw(parts, w[n][0], m[n][0], v[n][0], name="adamw_" + n)
        return [t.reshape(w[n].shape) for t in res]

    loss_part, grad_x, big, small_g = _local_step(x[0], p[0, 0], loss_target[0], shards, small, update)
    grads, deltas, new_m, new_v = [{n: big[n][i] for n in _BIG} for i in range(4)]

    shapes = {n: w[n].shape for n in _WEIGHTS if n not in _BIG}
    pack = _pack_small(small_g, d)
    pack = pack.at[_PACK_ROWS - 1, :1].add(0.0 * grads["w_in"][0, 0, :1])
    pack = pack.at[_PACK_ROWS - 1, 1].set(loss_part[0, 0])
    g_small = _allreduce_small(pack)
    loss = g_small[_PACK_ROWS - 1, 1]
    d_small, m_small, v_small = _adamw_small(g_small, _pack_small(w, d), _pack_small(m, d), _pack_small(v, d))
    grads.update(_unpack_small(g_small, shapes))
    deltas.update(_unpack_small(d_small, shapes))
    new_m.update(_unpack_small(m_small, shapes))
    new_v.update(_unpack_small(v_small, shapes))

    return (loss, grad_x[None], *[grads[n] for n in _WEIGHTS], *[deltas[n] for n in _WEIGHTS],
            *[new_m[n] for n in _WEIGHTS], *[new_v[n] for n in _WEIGHTS])
```

```python
import functools
import math

import jax
import jax.numpy as jnp
import numpy as np
from jax import lax
from jax.experimental import pallas as pl
from jax.experimental.pallas import tpu as pltpu

F32 = jnp.float32
BF16 = jnp.bfloat16

HEAD_DIM = 128
N_HEADS_A = 8
N_KV_A = 2
N_HEADS_B = 8
N_KV_B = 2
GROUP = 4
GRID_W = 64
BLOCK_Q = 128
WINDOW = 128
N_BUCKETS = 32
MAX_DISTANCE = 128
ROPE_THETA = 10000.0
EPS = 1e-6
NEG_INF = -1e30
ATT_SCALE = HEAD_DIM ** -0.5
LOG2E = math.log2(math.e)
LN2 = math.log(2.0)
Q_SCALE = ATT_SCALE * LOG2E
PAD_LO, PAD_HI = 256, 128

ADAM_LR = 0.001
ADAM_B1 = 0.9
ADAM_B2 = 0.999
ADAM_EPS = 1e-08
ADAM_WD = 0.01
ADAM_STEP = 10

N_CHIPS = 4
N_DEV = 8
COL_QA, COL_KA, COL_VA, COL_QB, COL_KB, COL_VB = 0, 8, 10, 12, 20, 22
N_COLS = 24

VMEM_LIMIT = 52 * 1024 * 1024


def _params(sem=None, collective_id=None):
    return pltpu.CompilerParams(dimension_semantics=sem, vmem_limit_bytes=VMEM_LIMIT, collective_id=collective_id)


_ANY = pl.BlockSpec(memory_space=pl.ANY)
_MESH = pl.DeviceIdType.MESH
SIBLING_BARRIER_ID = 1


def _sibling():
    return (lax.axis_index("x"), lax.axis_index("y"), 1 - lax.axis_index("c"))


class _Comm:
    def __init__(self, inputs, out_shapes, sems, start, finish, aliases=None):
        self.inputs, self.out_shapes, self.sems = list(inputs), list(out_shapes), list(sems)
        self.start, self.finish, self.aliases = start, finish, dict(aliases or {})


def _call(body, *, name, grid, in_specs, out_specs, out_shape, args, scratch_shapes=(), sem=None, comm=None,
          after=None, aliases=None):
    in_specs, out_specs, out_shape = list(in_specs), list(out_specs), list(out_shape)
    scratch_shapes = list(scratch_shapes)
    n_in, n_out, n_sc = len(in_specs), len(out_specs), len(scratch_shapes)
    behind = [] if after is None else [after]
    aliases = dict(aliases or {})
    if comm is None:
        res = pl.pallas_call(
            (lambda *refs: body(*refs[:n_in], *refs[n_in + len(behind):])) if behind else body,
            name=name, grid=grid, in_specs=in_specs + [_ANY] * len(behind), out_specs=out_specs,
            out_shape=out_shape, scratch_shapes=scratch_shapes, input_output_aliases=aliases,
            compiler_params=_params(sem))(*args, *behind)
        return list(res), []
    c_in, c_out = len(comm.inputs), len(comm.out_shapes)

    def hosted(*refs):
        pos = [0]

        def take(n):
            pos[0] += n
            return refs[pos[0] - n:pos[0]]

        ins, c_ins, _, outs, c_outs, scr = (take(n_in), take(c_in), take(len(behind)), take(n_out), take(c_out),
                                            take(n_sc))
        c_sems = refs[pos[0]:]
        ids = [pl.program_id(a) for a in range(len(grid))]
        first = functools.reduce(jnp.logical_and, [i == 0 for i in ids])
        last = functools.reduce(jnp.logical_and, [i == g - 1 for i, g in zip(ids, grid)])

        @pl.when(first)
        def _():
            barrier = pltpu.get_barrier_semaphore()
            pl.semaphore_signal(barrier, inc=1, device_id=_sibling(), device_id_type=_MESH)
            pl.semaphore_wait(barrier, 1)
            comm.start(c_ins, c_outs, c_sems)

        body(*ins, *outs, *scr)

        @pl.when(last)
        def _():
            comm.finish(c_ins, c_outs, c_sems)

    res = pl.pallas_call(
        hosted, name=name, grid=grid, in_specs=in_specs + [_ANY] * (c_in + len(behind)),
        out_specs=out_specs + [_ANY] * c_out,
        out_shape=out_shape + comm.out_shapes, scratch_shapes=scratch_shapes + comm.sems,
        input_output_aliases={**aliases, **{n_in + i: n_out + o for i, o in comm.aliases.items()}},
        compiler_params=_params(("arbitrary",) * len(grid), SIBLING_BARRIER_ID))(*args, *comm.inputs, *behind)
    return list(res[:n_out]), list(res[n_out:])


def _matmul(a, b, *, mode, out_dtypes, name, epilogue=None, extras=(), bm=1024, bn=1024, bk=2048,
            out_stack=0, comm=None, after=None, vecs=()):
    stacked = b.ndim == 3
    if mode == "nn":
        m, k = a.shape
        if stacked:
            nj, kb, ns = b.shape
            n, ks = nj * ns, k
        else:
            kb, n = b.shape
            ns, ks = n, k
        dn = (((1,), (0,)), ((), ()))
    elif mode == "nt":
        m, k = a.shape
        if stacked:
            nj, n, ks = b.shape
            kb = nj * ks
        else:
            n, kb = b.shape
            ks = kb
        ns = n
        dn = (((1,), (1,)), ((), ()))
    else:
        k, m = a.shape
        kb, n = b.shape
        ns, ks = n, k
        dn = (((0,), (0,)), ((), ()))
    assert k == kb and not (stacked and mode == "tn")
    ns_out = n // out_stack if out_stack else n
    whole_k = stacked and mode == "nt" and bk >= k
    bm, bn, bk = min(bm, m), min(bn, ns, ns_out), k if whole_k else min(bk, ks)
    assert m % bm == 0 and ns % bn == 0 and ns_out % bn == 0 and ks % bk == 0 or whole_k
    gm, gn, gk = m // bm, n // bn, k // bk

    if mode == "tn":
        a_spec = pl.BlockSpec((bk, bm), lambda i, j, q: (q, i))
    else:
        a_spec = pl.BlockSpec((bm, bk), lambda i, j, q: (i, q))
    if mode == "nt":
        if whole_k:
            b_spec = pl.BlockSpec((nj, bn, ks), lambda i, j, q: (0, j, 0))
        elif stacked:
            per = ks // bk
            b_spec = pl.BlockSpec((None, bn, bk), lambda i, j, q: (q // per, j, q % per))
        else:
            b_spec = pl.BlockSpec((bn, bk), lambda i, j, q: (j, q))
    else:
        if stacked:
            per = ns // bn
            b_spec = pl.BlockSpec((None, bk, bn), lambda i, j, q: (j // per, q, j % per))
        else:
            b_spec = pl.BlockSpec((bk, bn), lambda i, j, q: (q, j))
    ex_spec = pl.BlockSpec((bm, bn), lambda i, j, q: (i, j))
    if out_stack:
        per_o = ns_out // bn
        o_spec = pl.BlockSpec((None, bm, bn), lambda i, j, q: (j // per_o, i, j % per_o))
        o_shape = (out_stack, m, ns_out)
    else:
        o_spec = ex_spec
        o_shape = (m, n)
    n_ex, n_out = len(extras) + len(vecs), len(out_dtypes)

    def body(a_ref, b_ref, *rest):
        ex, outs = rest[:n_ex], rest[n_ex:n_ex + n_out]
        if whole_k:
            part = sum(lax.dot_general(a_ref[:, t * ks:(t + 1) * ks], b_ref[t], dn, preferred_element_type=F32)
                       for t in range(nj))
        else:
            part = lax.dot_general(a_ref[...], b_ref[...], dn, preferred_element_type=F32)

        def finish(acc):
            res = epilogue(acc, *[e[...] for e in ex]) if epilogue else (acc,)
            for o, r in zip(outs, res):
                o[...] = r.astype(o.dtype)

        if gk == 1:
            finish(part)
        else:
            acc_ref = rest[-1]
            q = pl.program_id(2)

            @pl.when(q == 0)
            def _():
                acc_ref[...] = part

            @pl.when(q > 0)
            def _():
                acc_ref[...] += part

            @pl.when(q == gk - 1)
            def _():
                finish(acc_ref[...])

    res, c_res = _call(
        body, name=name, grid=(gm, gn, gk),
        in_specs=[a_spec, b_spec] + [ex_spec] * len(extras)
        + [pl.BlockSpec((1, bn), lambda i, j, q: (0, j))] * len(vecs),
        out_specs=[o_spec] * n_out,
        out_shape=[jax.ShapeDtypeStruct(o_shape, dt) for dt in out_dtypes],
        scratch_shapes=[pltpu.VMEM((bm, bn), F32)] if gk > 1 else [],
        sem=("parallel", "parallel", "arbitrary"), args=(a, b, *extras, *vecs), comm=comm, after=after)
    res = res[0] if n_out == 1 else res
    return res if comm is None else (res, c_res)


def _rms_fwd(x, g, *, name, tm=256, comm=None):
    s, d = x.shape
    tm = min(tm, s)

    def body(x_ref, g_ref, o_ref):
        xf = x_ref[...]
        r = lax.rsqrt(jnp.mean(xf * xf, axis=-1, keepdims=True) + EPS)
        o_ref[...] = (xf * r * g_ref[...]).astype(o_ref.dtype)

    res, c_res = _call(
        body, name=name, grid=(s // tm,),
        in_specs=[pl.BlockSpec((tm, d), lambda i: (i, 0)), pl.BlockSpec((1, d), lambda i: (0, 0))],
        out_specs=[pl.BlockSpec((tm, d), lambda i: (i, 0))],
        out_shape=[jax.ShapeDtypeStruct((s, d), BF16)],
        sem=("parallel",), args=(x, g), comm=comm)
    return res[0] if comm is None else (res[0], c_res)


def _rms_bwd(x, dy, g, add, *, name, want_bf16, tm=256):
    s, d = x.shape
    tm = min(tm, s)

    def body(x_ref, dy_ref, g_ref, add_ref, dx_ref, *rest):
        dg_ref = rest[-1]
        i = pl.program_id(0)
        xf = x_ref[...]
        dyf = dy_ref[...].astype(F32)
        r = lax.rsqrt(jnp.mean(xf * xf, axis=-1, keepdims=True) + EPS)
        xh = xf * r
        dyg = dyf * g_ref[...]
        dx = r * (dyg - xh * jnp.mean(dyg * xh, axis=-1, keepdims=True))
        tot = add_ref[...] + dx
        dx_ref[...] = tot
        if want_bf16:
            rest[0][...] = tot.astype(BF16)
        part = jnp.sum(dyf * xh, axis=0, keepdims=True)

        @pl.when(i == 0)
        def _():
            dg_ref[...] = part

        @pl.when(i > 0)
        def _():
            dg_ref[...] += part

    row = pl.BlockSpec((tm, d), lambda i: (i, 0))
    vec = pl.BlockSpec((1, d), lambda i: (0, 0))
    out_specs = [row] + ([row] if want_bf16 else []) + [vec]
    out_shape = [jax.ShapeDtypeStruct((s, d), F32)]
    if want_bf16:
        out_shape.append(jax.ShapeDtypeStruct((s, d), BF16))
    out_shape.append(jax.ShapeDtypeStruct((1, d), F32))
    return pl.pallas_call(
        body,
        name=name,
        grid=(s // tm,),
        in_specs=[row, row, vec, row],
        out_specs=out_specs,
        out_shape=out_shape,
        compiler_params=_params(("arbitrary",)),
    )(x, dy, g, add)


def _tail(h2, gate, pp, target, g_ple, g_final, *, tm=128):
    s, d = h2.shape
    tm = min(tm, s)

    def body(h2_ref, gate_ref, pp_ref, t_ref, gp_ref, gf_ref, dh3_ref, dz_ref, dpp_ref, dgf_ref, dgp_ref, loss_ref):
        i = pl.program_id(0)
        ppf = pp_ref[...]
        gate_v = gate_ref[...]
        r_p = lax.rsqrt(jnp.mean(ppf * ppf, axis=-1, keepdims=True) + EPS)
        eh = ppf * r_p
        e = eh * gp_ref[...]
        h3 = h2_ref[...] + gate_v * e
        r_f = lax.rsqrt(jnp.mean(h3 * h3, axis=-1, keepdims=True) + EPS)
        yh = h3 * r_f
        diff = yh * gf_ref[...] - t_ref[...]
        loss_part = 0.5 * jnp.sum(jnp.mean(diff * diff, axis=-1, keepdims=True), axis=0, keepdims=True)
        dy = diff / d
        dgf = jnp.sum(dy * yh, axis=0, keepdims=True)
        dyg = dy * gf_ref[...]
        dh3 = r_f * (dyg - yh * jnp.mean(dyg * yh, axis=-1, keepdims=True))
        dh3_ref[...] = dh3
        de = dh3 * gate_v
        dz_ref[...] = (dh3 * e * gate_v * (1.0 - gate_v)).astype(BF16)
        dgp = jnp.sum(de * eh, axis=0, keepdims=True)
        deg = de * gp_ref[...]
        dpp_ref[...] = (r_p * (deg - eh * jnp.mean(deg * eh, axis=-1, keepdims=True))).astype(BF16)
        loss_row = jnp.broadcast_to(loss_part, (1, 128))

        @pl.when(i == 0)
        def _():
            dgf_ref[...] = dgf
            dgp_ref[...] = dgp
            loss_ref[...] = loss_row

        @pl.when(i > 0)
        def _():
            dgf_ref[...] += dgf
            dgp_ref[...] += dgp
            loss_ref[...] += loss_row

    row = pl.BlockSpec((tm, d), lambda i: (i, 0))
    vec = pl.BlockSpec((1, d), lambda i: (0, 0))
    return pl.pallas_call(
        body,
        name="tail_fwd_bwd",
        grid=(s // tm,),
        in_specs=[row, row, row, row, vec, vec],
        out_specs=[row, row, row, vec, vec, pl.BlockSpec((1, 128), lambda i: (0, 0))],
        out_shape=[
            jax.ShapeDtypeStruct((s, d), F32),
            jax.ShapeDtypeStruct((s, d), BF16),
            jax.ShapeDtypeStruct((s, d), BF16),
            jax.ShapeDtypeStruct((1, d), F32),
            jax.ShapeDtypeStruct((1, d), F32),
            jax.ShapeDtypeStruct((1, 128), F32),
        ],
        compiler_params=_params(("arbitrary",)),
    )(h2, gate, pp, target, g_ple, g_final)


def _rope_tables(s):
    rows = s // GRID_W
    half = HEAD_DIM // 2
    inv_freq = ROPE_THETA ** (-jnp.arange(0, half, 2, dtype=F32) / half)
    ang_r = jnp.arange(rows, dtype=jnp.int32).astype(F32)[:, None] * inv_freq
    ang_c = jnp.arange(GRID_W, dtype=jnp.int32).astype(F32)[:, None] * inv_freq
    cr, sr = (jnp.repeat(t, GRID_W, axis=0) for t in (jnp.cos(ang_r), jnp.sin(ang_r)))
    cc, sc = (jnp.tile(t, (rows, 1)) for t in (jnp.cos(ang_c), jnp.sin(ang_c)))
    cos_t = jnp.concatenate([cr, cr, cc, cc], axis=-1)
    sin_t = jnp.concatenate([-sr, sr, -sc, sc], axis=-1)
    return cos_t, sin_t


def _swap_quarters(x):
    lane = lax.broadcasted_iota(jnp.int32, x.shape, x.ndim - 1)
    up = pltpu.roll(x, HEAD_DIM - 32, x.ndim - 1)
    down = pltpu.roll(x, 32, x.ndim - 1)
    return jnp.where((lane % 64) < 32, up, down)


def _cols(first, count=1):
    return slice(first * HEAD_DIM, (first + count) * HEAD_DIM)


def _qk_prep(proj, g_q, g_k, cos_t, sin_t, *, tm=256, comm=None):
    s, n = proj.shape
    tm = min(tm, s)

    def body(x_ref, gq_ref, gk_ref, c_ref, s_ref, o_ref):
        cos_v, sin_v = c_ref[...], s_ref[...]
        for h in range(COL_VA):
            x = x_ref[:, _cols(h)]
            g = gq_ref[...] if h < COL_KA else gk_ref[...]
            xn = x * lax.rsqrt(jnp.mean(x * x, axis=-1, keepdims=True) + EPS) * g
            xr = xn * cos_v + _swap_quarters(xn) * sin_v
            if h < COL_KA:
                xr = xr * Q_SCALE
            o_ref[:, _cols(h)] = xr.astype(BF16)
        o_ref[:, _cols(COL_VA, 2)] = x_ref[:, _cols(COL_VA, 2)].astype(BF16)
        o_ref[:, _cols(COL_QB, N_HEADS_B)] = (x_ref[:, _cols(COL_QB, N_HEADS_B)] * Q_SCALE).astype(BF16)
        o_ref[:, _cols(COL_KB, 4)] = x_ref[:, _cols(COL_KB, 4)].astype(BF16)

    row = pl.BlockSpec((tm, n), lambda i: (i, 0))
    tab = pl.BlockSpec((tm, HEAD_DIM), lambda i: (i, 0))
    vec = pl.BlockSpec((1, HEAD_DIM), lambda i: (0, 0))
    res, c_res = _call(
        body, name="qk_prep", grid=(s // tm,),
        in_specs=[row, vec, vec, tab, tab],
        out_specs=[row],
        out_shape=[jax.ShapeDtypeStruct((s, n), BF16)],
        sem=("parallel",), args=(proj, g_q, g_k, cos_t, sin_t), comm=comm)
    return res[0] if comm is None else (res[0], c_res)


def _qk_bwd(dqa, dka, dva, dqb, dkpad, dvpad, proj, g_q, g_k, cos_t, sin_t, *, comm=None, after=None):
    s, n = proj.shape
    tm = min(PAD_LO, s)
    assert PAD_LO % tm == 0
    lo = PAD_LO // tm

    def body(dqa_ref, dka_ref, dva_ref, dqb_ref, dkb_ref, dvb_ref, x_ref, gq_ref, gk_ref, c_ref, s_ref,
             o_ref, dgq_ref, dgk_ref):
        i = pl.program_id(0)
        cos_v, sin_v = c_ref[...], s_ref[...]

        def head(d, x, g):
            dn = d * cos_v + _swap_quarters(d * sin_v)
            r = lax.rsqrt(jnp.mean(x * x, axis=-1, keepdims=True) + EPS)
            xh = x * r
            dng = dn * g
            dx = r * (dng - xh * jnp.mean(dng * xh, axis=-1, keepdims=True))
            return dx.astype(BF16), jnp.sum(dn * xh, axis=0, keepdims=True)

        acc_q = jnp.zeros((1, HEAD_DIM), F32)
        acc_k = jnp.zeros((1, HEAD_DIM), F32)
        for h in range(N_HEADS_A):
            o_ref[:, _cols(h)], part = head(dqa_ref[:, _cols(h)] * ATT_SCALE, x_ref[:, _cols(h)], gq_ref[...])
            acc_q = acc_q + part
        for h in range(N_KV_A):
            o_ref[:, _cols(COL_KA + h)], part = head(dka_ref[:, _cols(h)] * LN2, x_ref[:, _cols(COL_KA + h)],
                                                     gk_ref[...])
            acc_k = acc_k + part
        o_ref[:, _cols(COL_VA, 2)] = dva_ref[...].astype(BF16)
        o_ref[:, _cols(COL_QB, N_HEADS_B)] = (dqb_ref[...] * ATT_SCALE).astype(BF16)
        o_ref[:, _cols(COL_KB, 2)] = (dkb_ref[...] * LN2).astype(BF16)
        o_ref[:, _cols(COL_VB, 2)] = dvb_ref[...].astype(BF16)

        @pl.when(i == 0)
        def _():
            dgq_ref[...] = acc_q
            dgk_ref[...] = acc_k

        @pl.when(i > 0)
        def _():
            dgq_ref[...] += acc_q
            dgk_ref[...] += acc_k

    def rows(width, shift=0):
        return pl.BlockSpec((tm, width), lambda i: (i + shift, 0))

    kv_w = N_KV_A * HEAD_DIM
    q_w = N_HEADS_A * HEAD_DIM
    vec = pl.BlockSpec((1, HEAD_DIM), lambda i: (0, 0))
    res, c_res = _call(
        body, name="qk_bwd", grid=(s // tm,),
        in_specs=[rows(q_w), rows(kv_w), rows(kv_w), rows(q_w), rows(kv_w, lo), rows(kv_w, lo), rows(n),
                  vec, vec, rows(HEAD_DIM), rows(HEAD_DIM)],
        out_specs=[rows(n), vec, vec],
        out_shape=[
            jax.ShapeDtypeStruct((s, n), BF16),
            jax.ShapeDtypeStruct((1, HEAD_DIM), F32),
            jax.ShapeDtypeStruct((1, HEAD_DIM), F32),
        ],
        sem=("arbitrary",), args=(dqa, dka, dva, dqb, dkpad, dvpad, proj, g_q, g_k, cos_t, sin_t), comm=comm,
        after=after)
    return res if comm is None else (res, c_res)


_NT = (((1,), (1,)), ((), ()))
_TN = (((0,), (0,)), ((), ()))


def _attn_a_fwd(pb, *, tq=4096, sub=256, comm=None, after=None):
    s = pb.shape[0]
    tq = min(tq, s)

    sub = min(sub, tq)

    def body(q_ref, k_ref, v_ref, o_ref, lse_ref):
        k = k_ref[...]
        v = v_ref[...]
        for r in range(tq // sub):
            rows = pl.ds(r * sub, sub)
            sc = lax.dot_general(q_ref[rows, :], k, _NT, preferred_element_type=F32)
            m = jnp.max(sc, axis=-1, keepdims=True)
            p = jnp.exp2(sc - m)
            l = jnp.sum(p, axis=-1, keepdims=True)
            o = jnp.dot(p.astype(BF16), v, preferred_element_type=F32)
            o_ref[rows, :] = (o / l).astype(BF16)
            lse_ref[rows, :] = jnp.broadcast_to(m + jnp.log2(l), (sub, HEAD_DIM))

    res, c_res = _call(
        body, name="attn_a_fwd", grid=(N_HEADS_A, s // tq),
        in_specs=[
            pl.BlockSpec((tq, HEAD_DIM), lambda h, i: (i, COL_QA + h)),
            pl.BlockSpec((s, HEAD_DIM), lambda h, i: (0, COL_KA + h // GROUP)),
            pl.BlockSpec((s, HEAD_DIM), lambda h, i: (0, COL_VA + h // GROUP)),
        ],
        out_specs=[
            pl.BlockSpec((tq, HEAD_DIM), lambda h, i: (i, h)),
            pl.BlockSpec((None, tq, HEAD_DIM), lambda h, i: (h, i, 0)),
        ],
        out_shape=[
            jax.ShapeDtypeStruct((s, (N_HEADS_A + N_HEADS_B) * HEAD_DIM), BF16),
            jax.ShapeDtypeStruct((N_HEADS_A, s, HEAD_DIM), F32),
        ],
        sem=("parallel", "parallel"), args=(pb, pb, pb), comm=comm, after=after)
    return res if comm is None else (res, c_res)


def _attn_a_bwd(pb, att, datt, lse, *, tq=4096, sub=256, comm=None):
    s = pb.shape[0]
    tq = min(tq, s)
    sub = min(sub, tq)

    def body(q_ref, k_ref, v_ref, o_ref, do_ref, lse_ref, dq_ref, dk_ref, dv_ref):
        first = jnp.logical_and(pl.program_id(1) == 0, pl.program_id(2) == 0)
        k = k_ref[...]
        v = v_ref[...]
        dk = dv = None
        for r in range(tq // sub):
            rows = pl.ds(r * sub, sub)
            q = q_ref[rows, :]
            do = do_ref[rows, :]
            sc = lax.dot_general(q, k, _NT, preferred_element_type=F32)
            p = jnp.exp2(sc - lse_ref[rows, :][:, :1])
            dp = lax.dot_general(do, v, _NT, preferred_element_type=F32)
            delta = jnp.sum(do.astype(F32) * o_ref[rows, :].astype(F32), axis=-1, keepdims=True)
            ds = (p * (dp - delta)).astype(BF16)
            dq_ref[rows, :] = jnp.dot(ds, k, preferred_element_type=F32)
            dk_r = lax.dot_general(ds, q, _TN, preferred_element_type=F32)
            dv_r = lax.dot_general(p.astype(BF16), do, _TN, preferred_element_type=F32)
            dk = dk_r if dk is None else dk + dk_r
            dv = dv_r if dv is None else dv + dv_r

        @pl.when(first)
        def _():
            dk_ref[...] = dk
            dv_ref[...] = dv

        @pl.when(jnp.logical_not(first))
        def _():
            dk_ref[...] += dk
            dv_ref[...] += dv

    qmap = lambda kv, g, i: (i, kv * GROUP + g)
    res, c_res = _call(
        body, name="attn_a_bwd", grid=(N_KV_A, GROUP, s // tq),
        in_specs=[
            pl.BlockSpec((tq, HEAD_DIM), lambda kv, g, i: (i, COL_QA + kv * GROUP + g)),
            pl.BlockSpec((s, HEAD_DIM), lambda kv, g, i: (0, COL_KA + kv)),
            pl.BlockSpec((s, HEAD_DIM), lambda kv, g, i: (0, COL_VA + kv)),
            pl.BlockSpec((tq, HEAD_DIM), qmap),
            pl.BlockSpec((tq, HEAD_DIM), qmap),
            pl.BlockSpec((None, tq, HEAD_DIM), lambda kv, g, i: (kv * GROUP + g, i, 0)),
        ],
        out_specs=[
            pl.BlockSpec((tq, HEAD_DIM), qmap),
            pl.BlockSpec((s, HEAD_DIM), lambda kv, g, i: (0, kv)),
            pl.BlockSpec((s, HEAD_DIM), lambda kv, g, i: (0, kv)),
        ],
        out_shape=[
            jax.ShapeDtypeStruct((s, N_HEADS_A * HEAD_DIM), F32),
            jax.ShapeDtypeStruct((s, N_KV_A * HEAD_DIM), F32),
            jax.ShapeDtypeStruct((s, N_KV_A * HEAD_DIM), F32),
        ],
        sem=("arbitrary", "arbitrary", "arbitrary"), args=(pb, pb, pb, att, datt, lse), comm=comm)
    return res if comm is None else (res, c_res)


def _t5_bucket(rel):
    nb = N_BUCKETS // 2
    ret = jnp.where(rel > 0, nb, 0)
    n = jnp.abs(rel)
    max_exact = nb // 2
    nf = jnp.maximum(n, 1).astype(F32)
    large = max_exact + (jnp.log(nf / max_exact) / math.log(MAX_DISTANCE / max_exact)
                         * (nb - max_exact)).astype(jnp.int32)
    large = jnp.minimum(large, nb - 1)
    return ret + jnp.where(n < max_exact, n, large)


def _band_buckets():
    r = jnp.arange(BLOCK_Q, dtype=jnp.int32)
    j = jnp.arange(3 * BLOCK_Q, dtype=jnp.int32)
    return _t5_bucket((j[None, :] - BLOCK_Q) - r[:, None])


def _band_bias(bucket, table_ref, h):
    acc = jnp.zeros(bucket.shape, F32)
    for b in range(N_BUCKETS):
        acc = jnp.where(bucket == b, table_ref[b, h], acc)
    return acc


GQ = GROUP * BLOCK_Q


def _stack_heads(x):
    return jnp.concatenate([x[:, _cols(g)] for g in range(GROUP)], axis=0)


def _unstack_heads(x):
    return jnp.concatenate([x[g * BLOCK_Q:(g + 1) * BLOCK_Q] for g in range(GROUP)], axis=1)


def _group_bias(bucket, table_ref, kv):
    return jnp.concatenate([_band_bias(bucket, table_ref, kv * GROUP + g) * LOG2E for g in range(GROUP)], axis=0)


def _group_sink(sink_ref, kv):
    head = lax.broadcasted_iota(jnp.int32, (GQ, 1), 0) // BLOCK_Q
    snk = jnp.zeros((GQ, 1), F32)
    for g in range(GROUP):
        snk = jnp.where(head == g, sink_ref[0, kv * GROUP + g] * LOG2E, snk)
    return snk


def _band_mask(n, s):
    r = lax.broadcasted_iota(jnp.int32, (GQ, 3 * BLOCK_Q), 0) % BLOCK_Q
    j = lax.broadcasted_iota(jnp.int32, (GQ, 3 * BLOCK_Q), 1)
    rel = j - BLOCK_Q - r
    kabs = n * BLOCK_Q + j - BLOCK_Q
    return (jnp.abs(rel) <= WINDOW) & (kabs >= 0) & (kabs < s)


def _band_start(n):
    return pl.multiple_of(n * BLOCK_Q + (PAD_LO - BLOCK_Q), BLOCK_Q)


def _attn_b_fwd(pb, kpad, vpad, bucket, table, sink, att, *, comm=None, after=None):
    s = pb.shape[0]
    nblk = s // BLOCK_Q
    sp = kpad.shape[0]

    def body(table_ref, sink_ref, q0_ref, q1_ref, k_ref, v_ref, bucket_ref, _, o_ref, lse_ref, bias_ref):
        n = pl.program_id(0)

        @pl.when(n == 0)
        def _():
            for kv in range(N_KV_B):
                bias_ref[kv * GQ:(kv + 1) * GQ, :] = _group_bias(bucket_ref[...], table_ref, kv)

        band = pl.ds(_band_start(n), 3 * BLOCK_Q)
        mask = _band_mask(n, s)
        for kv, q_ref in enumerate((q0_ref, q1_ref)):
            kb = k_ref[band, _cols(kv)]
            vb = v_ref[band, _cols(kv)]
            sc = lax.dot_general(_stack_heads(q_ref[...]), kb, _NT, preferred_element_type=F32)
            sc = jnp.where(mask, sc + bias_ref[kv * GQ:(kv + 1) * GQ, :], NEG_INF)
            snk = _group_sink(sink_ref, kv)
            m = jnp.maximum(jnp.max(sc, axis=-1, keepdims=True), snk)
            p = jnp.exp2(sc - m)
            l = jnp.sum(p, axis=-1, keepdims=True) + jnp.exp2(snk - m)
            o = jnp.dot(p.astype(BF16), vb, preferred_element_type=F32)
            o_ref[:, _cols(kv * GROUP, GROUP)] = _unstack_heads((o / l).astype(BF16))
            lse = m + jnp.log2(l)
            for g in range(GROUP):
                lse_ref[kv * GROUP + g] = jnp.broadcast_to(lse[g * BLOCK_Q:(g + 1) * BLOCK_Q], (BLOCK_Q, HEAD_DIM))

    smem = pl.BlockSpec(memory_space=pltpu.SMEM)
    wide = GROUP * HEAD_DIM
    whole = pl.BlockSpec((sp, N_KV_B * HEAD_DIM), lambda n: (0, 0))
    res, c_res = _call(
        body, name="attn_b_fwd", grid=(nblk,),
        in_specs=[
            smem,
            smem,
            pl.BlockSpec((BLOCK_Q, wide), lambda n: (n, COL_QB // GROUP)),
            pl.BlockSpec((BLOCK_Q, wide), lambda n: (n, COL_QB // GROUP + 1)),
            whole,
            whole,
            pl.BlockSpec((BLOCK_Q, 3 * BLOCK_Q), lambda n: (0, 0)),
            _ANY,
        ],
        out_specs=[
            pl.BlockSpec((BLOCK_Q, N_HEADS_B * HEAD_DIM), lambda n: (n, 1)),
            pl.BlockSpec((N_HEADS_B, BLOCK_Q, HEAD_DIM), lambda n: (0, n, 0)),
        ],
        out_shape=[
            jax.ShapeDtypeStruct(att.shape, BF16),
            jax.ShapeDtypeStruct((N_HEADS_B, s, HEAD_DIM), F32),
        ],
        scratch_shapes=[pltpu.VMEM((N_KV_B * GQ, 3 * BLOCK_Q), F32)],
        sem=("arbitrary",), args=(table, sink, pb, pb, kpad, vpad, bucket, att), comm=comm, after=after,
        aliases={7: 0})
    return res if comm is None else (res, c_res)


def _attn_b_bwd(pb, kpad, vpad, att, datt, lse, bucket, table, sink, *, comm=None, after=None):
    s = pb.shape[0]
    nblk = s // BLOCK_Q
    sp = kpad.shape[0]

    def body(table_ref, sink_ref, q0_ref, q1_ref, k_ref, v_ref, o_ref, do_ref, lse_ref, bucket_ref,
             dq_ref, dk_ref, dv_ref, dtab_ref, dsink_ref, bias_ref, dbias_ref):
        n = pl.program_id(0)

        @pl.when(n == 0)
        def _():
            dk_ref[...] = jnp.zeros_like(dk_ref)
            dv_ref[...] = jnp.zeros_like(dv_ref)
            dbias_ref[...] = jnp.zeros_like(dbias_ref)
            dsink_ref[...] = jnp.zeros_like(dsink_ref)
            for kv in range(N_KV_B):
                bias_ref[kv * GQ:(kv + 1) * GQ, :] = _group_bias(bucket_ref[...], table_ref, kv)

        band = pl.ds(_band_start(n), 3 * BLOCK_Q)
        mask = _band_mask(n, s)
        for kv, q_ref in enumerate((q0_ref, q1_ref)):
            wide_cols = _cols(kv * GROUP, GROUP)
            q = _stack_heads(q_ref[...])
            do = _stack_heads(do_ref[:, wide_cols])
            o = _stack_heads(o_ref[:, wide_cols])
            kb = k_ref[band, _cols(kv)]
            vb = v_ref[band, _cols(kv)]
            lse = jnp.concatenate([lse_ref[kv * GROUP + g][:, :1] for g in range(GROUP)], axis=0)
            sc = lax.dot_general(q, kb, _NT, preferred_element_type=F32)
            sc = jnp.where(mask, sc + bias_ref[kv * GQ:(kv + 1) * GQ, :], NEG_INF)
            p = jnp.exp2(sc - lse)
            dp = lax.dot_general(do, vb, _NT, preferred_element_type=F32)
            delta = jnp.sum(do.astype(F32) * o.astype(F32), axis=-1, keepdims=True)
            ds = p * (dp - delta)
            dsb = ds.astype(BF16)
            dq_ref[:, wide_cols] = _unstack_heads(jnp.dot(dsb, kb, preferred_element_type=F32))
            dk_ref[band, _cols(kv)] += lax.dot_general(dsb, q, _TN, preferred_element_type=F32)
            dv_ref[band, _cols(kv)] += lax.dot_general(p.astype(BF16), do, _TN, preferred_element_type=F32)
            dbias_ref[kv * GQ:(kv + 1) * GQ, :] += ds
            sink_part = -jnp.exp2(_group_sink(sink_ref, kv) - lse) * delta
            for g in range(GROUP):
                rows = slice(g * BLOCK_Q, (g + 1) * BLOCK_Q)
                dsink_ref[kv * GROUP + g] += jnp.broadcast_to(
                    jnp.sum(sink_part[rows], axis=0, keepdims=True), (1, HEAD_DIM))

        @pl.when(n == nblk - 1)
        def _():
            bucket_v = bucket_ref[...]
            row = lax.broadcasted_iota(jnp.int32, (N_BUCKETS, HEAD_DIM), 0)
            for h in range(N_HEADS_B):
                acc = dbias_ref[h * BLOCK_Q:(h + 1) * BLOCK_Q, :]
                tot = jnp.zeros((N_BUCKETS, HEAD_DIM), F32)
                for b in range(N_BUCKETS):
                    tot = jnp.where(row == b, jnp.sum(jnp.where(bucket_v == b, acc, 0.0), keepdims=True), tot)
                dtab_ref[h] = tot

    smem = pl.BlockSpec(memory_space=pltpu.SMEM)
    wide = GROUP * HEAD_DIM
    whole = pl.BlockSpec((sp, N_KV_B * HEAD_DIM), lambda n: (0, 0))
    group_b = pl.BlockSpec((BLOCK_Q, N_HEADS_B * HEAD_DIM), lambda n: (n, 1))
    res, c_res = _call(
        body, name="attn_b_bwd", grid=(nblk,),
        in_specs=[
            smem,
            smem,
            pl.BlockSpec((BLOCK_Q, wide), lambda n: (n, COL_QB // GROUP)),
            pl.BlockSpec((BLOCK_Q, wide), lambda n: (n, COL_QB // GROUP + 1)),
            whole,
            whole,
            group_b,
            group_b,
            pl.BlockSpec((N_HEADS_B, BLOCK_Q, HEAD_DIM), lambda n: (0, n, 0)),
            pl.BlockSpec((BLOCK_Q, 3 * BLOCK_Q), lambda n: (0, 0)),
        ],
        out_specs=[
            pl.BlockSpec((BLOCK_Q, N_HEADS_B * HEAD_DIM), lambda n: (n, 0)),
            whole,
            whole,
            pl.BlockSpec((N_HEADS_B, N_BUCKETS, HEAD_DIM), lambda n: (0, 0, 0)),
            pl.BlockSpec((N_HEADS_B, 1, HEAD_DIM), lambda n: (0, 0, 0)),
        ],
        out_shape=[
            jax.ShapeDtypeStruct((s, N_HEADS_B * HEAD_DIM), F32),
            jax.ShapeDtypeStruct((sp, N_KV_B * HEAD_DIM), F32),
            jax.ShapeDtypeStruct((sp, N_KV_B * HEAD_DIM), F32),
            jax.ShapeDtypeStruct((N_HEADS_B, N_BUCKETS, HEAD_DIM), F32),
            jax.ShapeDtypeStruct((N_HEADS_B, 1, HEAD_DIM), F32),
        ],
        scratch_shapes=[pltpu.VMEM((N_KV_B * GQ, 3 * BLOCK_Q), F32), pltpu.VMEM((N_KV_B * GQ, 3 * BLOCK_Q), F32)],
        sem=("arbitrary",),
        args=(table, sink, pb, pb, kpad, vpad, att, datt, lse, bucket), comm=comm, after=after)
    return res if comm is None else (res, c_res)


_MESH = pl.DeviceIdType.MESH


def _other_chips(x, y):
    return [(x, 1 - y), (1 - x, y), (1 - x, 1 - y)]


_HBM = pl.BlockSpec(memory_space=pltpu.HBM)
_SEM = pl.BlockSpec(memory_space=pltpu.SEMAPHORE)
_SPLIT = pltpu.CompilerParams(has_side_effects=pltpu.SideEffectType.DATAFLOW_SIDE_EFFECTING)


def _in_hbm(a):
    return pltpu.with_memory_space_constraint(a, pltpu.HBM)


def _my_half(rows):
    c = lax.axis_index("c")
    half = rows // 2
    return pl.ds(pl.multiple_of(c * half, half), half), pl.ds(pl.multiple_of((1 - c) * half, half), half)


def _gather_route(shapes):
    def route(src, land):
        x, y, c = lax.axis_index("x"), lax.axis_index("y"), lax.axis_index("c")
        out = []
        for t, shape in enumerate(shapes):
            mine, _ = _my_half(shape[0])
            for px, py in _other_chips(x, y):
                out.append((src[t].at[mine], land[t].at[2 * x + y, mine], land[t].at[2 * px + py, mine], (px, py, c)))
        return out

    return route


def _exchange_route(n_t):
    def route(src, land):
        x, y, c = lax.axis_index("x"), lax.axis_index("y"), lax.axis_index("c")
        out = []
        for t in range(n_t):
            for px, py in _other_chips(x, y):
                k = 2 * px + py
                out.append((src[t].at[k], land[t].at[2 * (2 * x + y) + c], land[t].at[2 * k + c], (px, py, c)))
        return out

    return route


def _own_slot(shape, dtype, slot, block):
    return lax.dynamic_update_slice(lax.empty(shape, dtype), block[None], (slot,) + (0,) * (len(shape) - 1))


def _split_start(name, srcs, lands, route, after):
    n = len(srcs)

    def body(*refs):
        src, land, send_sems, recv_sems, token = refs[:n], refs[n:2 * n], refs[2 * n + 1], refs[2 * n + 2], refs[-1]
        for i, (src_ref, dst_ref, _, to) in enumerate(route(src, land)):
            pltpu.make_async_remote_copy(src_ref=src_ref, dst_ref=dst_ref, send_sem=send_sems.at[i],
                                         recv_sem=recv_sems.at[i], device_id=to, device_id_type=_MESH).start()
        token[...] = jnp.zeros_like(token)

    sem = pltpu.SemaphoreType.DMA((3 * n,))
    lands = list(lands)
    res = pl.pallas_call(
        body, name=name,
        in_specs=[_HBM] * (2 * n) + [_ANY],
        out_specs=[_SEM, _SEM] + [_HBM] * (2 * n) + [pl.BlockSpec(memory_space=pltpu.VMEM)],
        out_shape=[sem, sem] + [pltpu.HBM(a.shape, a.dtype) for a in list(srcs) + lands]
        + [jax.ShapeDtypeStruct((8, 128), F32)],
        input_output_aliases={i: 2 + i for i in range(2 * n)},
        compiler_params=_SPLIT,
    )(*[_in_hbm(a) for a in srcs], *[_in_hbm(a) for a in lands], after)
    return (res[0], res[1]), res[2:2 + n], res[2 + n:2 + 2 * n], res[-1]


def _split_wait(name, srcs, lands, sems, route, after):
    n = len(srcs)

    def body(*refs):
        src, land, send_sems, recv_sems = refs[:n], refs[n:2 * n], refs[2 * n], refs[2 * n + 1]
        for i, (src_ref, _, dst_ref, to) in enumerate(route(src, land)):
            cp = pltpu.make_async_remote_copy(src_ref=src_ref, dst_ref=dst_ref, send_sem=send_sems.at[i],
                                              recv_sem=recv_sems.at[i], device_id=to, device_id_type=_MESH)
            cp.wait_send()
            cp.wait_recv()

    res = pl.pallas_call(
        body, name=name,
        in_specs=[_HBM] * (2 * n) + [_SEM, _SEM, _ANY],
        out_specs=[_HBM] * (2 * n),
        out_shape=[pltpu.HBM(a.shape, a.dtype) for a in list(srcs) + list(lands)],
        input_output_aliases={i: i for i in range(2 * n)},
        compiler_params=_SPLIT,
    )(*srcs, *lands, sems[0], sems[1], after)
    return res[:n], res[n:]


def _comm_only(name, comm):
    return _call(lambda: None, name=name, grid=(1,), in_specs=[], out_specs=[], out_shape=[], args=(), comm=comm)[1]


def _swap_comm(shards, lands):
    n_t = len(lands)

    def copies(land, sems, later):
        send_sems, recv_sems = sems
        x, y = lax.axis_index("x"), lax.axis_index("y")
        sends, recvs = [], []
        for t in range(n_t):
            mine, other = _my_half(shards[t].shape[0])
            for j, (px, py) in enumerate(_other_chips(x, y)):
                k = 2 * px + py
                for part, out in ((mine, sends), (other, recvs)) if later else ((mine, sends),):
                    out.append(pltpu.make_async_remote_copy(
                        src_ref=land[t].at[k, part], dst_ref=land[t].at[k, part], send_sem=send_sems.at[3 * t + j],
                        recv_sem=recv_sems.at[3 * t + j], device_id=_sibling(), device_id_type=_MESH))
        return sends, recvs

    def start(ins, land, sems):
        for cp in copies(land, sems, False)[0]:
            cp.start()

    def finish(ins, land, sems):
        sends, recvs = copies(land, sems, True)
        for cp in recvs:
            cp.wait_recv()
        for cp in sends:
            cp.wait_send()

    return _Comm(
        lands, [jax.ShapeDtypeStruct(a.shape, a.dtype) for a in lands],
        [pltpu.SemaphoreType.DMA((3 * n_t,)), pltpu.SemaphoreType.DMA((3 * n_t,))],
        start, finish, aliases={t: t for t in range(n_t)})


def _forward_comm(partials, lands):
    n_t = len(lands)

    def copies(land, sems, later):
        send_sems, recv_sems = sems
        x, y, c = lax.axis_index("x"), lax.axis_index("y"), lax.axis_index("c")
        sends, recvs = [], []
        for t in range(n_t):
            for j, k in enumerate([2 * x + y] + [2 * px + py for px, py in _other_chips(x, y)]):
                for slot, out in ((2 * k + c, sends), (2 * k + 1 - c, recvs)) if later else ((2 * k + c, sends),):
                    out.append(pltpu.make_async_remote_copy(
                        src_ref=land[t].at[slot], dst_ref=land[t].at[slot], send_sem=send_sems.at[4 * t + j],
                        recv_sem=recv_sems.at[4 * t + j], device_id=_sibling(), device_id_type=_MESH))
        return sends, recvs

    def start(ins, land, sems):
        for cp in copies(land, sems, False)[0]:
            cp.start()

    def finish(ins, land, sems):
        sends, recvs = copies(land, sems, True)
        for cp in recvs:
            cp.wait_recv()
        for cp in sends:
            cp.wait_send()

    return _Comm(
        lands, [jax.ShapeDtypeStruct(a.shape, a.dtype) for a in lands],
        [pltpu.SemaphoreType.DMA((4 * n_t,)), pltpu.SemaphoreType.DMA((4 * n_t,))],
        start, finish, aliases={t: t for t in range(n_t)})


def _allreduce_small(pack):
    rows, d = pack.shape

    def body(p_ref, sum_ref, all_ref, send_sems, recv_sems):
        x, y, c = lax.axis_index("x"), lax.axis_index("y"), lax.axis_index("c")
        me = 4 * x + 2 * y + c
        all_ref[me] = p_ref[...]
        peers = []
        for dx in range(2):
            for dy in range(2):
                for dc in range(2):
                    if dx or dy or dc:
                        px = 1 - x if dx else x
                        py = 1 - y if dy else y
                        pc = 1 - c if dc else c
                        peers.append((4 * dx + 2 * dy + dc - 1, (px, py, pc)))
        sends = []
        for k, to in peers:
            cp = pltpu.make_async_remote_copy(
                src_ref=p_ref, dst_ref=all_ref.at[me], send_sem=send_sems.at[k], recv_sem=recv_sems.at[k],
                device_id=to, device_id_type=_MESH)
            cp.start()
            sends.append(cp)
        for k, (px, py, pc) in peers:
            pltpu.make_async_remote_copy(
                src_ref=p_ref, dst_ref=all_ref.at[4 * px + 2 * py + pc], send_sem=send_sems.at[k],
                recv_sem=recv_sems.at[k], device_id=(px, py, pc), device_id_type=_MESH).wait_recv()
        for cp in sends:
            cp.wait_send()
        tot = all_ref[0]
        for i in range(1, N_DEV):
            tot = tot + all_ref[i]
        sum_ref[...] = tot

    vm = pl.BlockSpec(memory_space=pltpu.VMEM)
    return pl.pallas_call(
        body,
        name="allreduce_small",
        in_specs=[vm],
        out_specs=vm,
        out_shape=jax.ShapeDtypeStruct((rows, d), F32),
        scratch_shapes=[
            pltpu.VMEM((N_DEV, rows, d), F32),
            pltpu.SemaphoreType.DMA((N_DEV - 1,)),
            pltpu.SemaphoreType.DMA((N_DEV - 1,)),
        ],
    )(pack)


def _adamw_math(w, g, m, v):
    m = ADAM_B1 * m + (1.0 - ADAM_B1) * g
    v = ADAM_B2 * v + (1.0 - ADAM_B2) * (g * g)
    m_hat = m / (1.0 - ADAM_B1 ** ADAM_STEP)
    v_hat = v / (1.0 - ADAM_B2 ** ADAM_STEP)
    delta = -ADAM_LR * (m_hat / (jnp.sqrt(v_hat) + ADAM_EPS) + ADAM_WD * w)
    return delta, m, v


def _sum_adamw(parts, w, m, v, *, name, tr=256):
    r, c = w.shape
    tr = min(tr, r)
    tc = min(c, 1024)

    def body(p_ref, w_ref, m_ref, v_ref, g_ref, d_ref, m2_ref, v2_ref):
        g = p_ref[0].astype(F32)
        for i in range(1, N_DEV):
            g = g + p_ref[i].astype(F32)
        delta, m2, v2 = _adamw_math(w_ref[...], g, m_ref[...], v_ref[...])
        g_ref[...] = g
        d_ref[...] = delta
        m2_ref[...] = m2
        v2_ref[...] = v2

    blk = pl.BlockSpec((tr, tc), lambda i, j: (i, j))
    return pl.pallas_call(
        body,
        name=name,
        grid=(r // tr, c // tc),
        in_specs=[pl.BlockSpec((N_DEV, tr, tc), lambda i, j: (0, i, j)), blk, blk, blk],
        out_specs=[blk] * 4,
        out_shape=[jax.ShapeDtypeStruct((r, c), F32)] * 4,
        compiler_params=_params(("parallel", "parallel")),
    )(parts, w, m, v)


def _adamw_small(g, w, m, v):
    def body(g_ref, w_ref, m_ref, v_ref, d_ref, m2_ref, v2_ref):
        delta, m2, v2 = _adamw_math(w_ref[...], g_ref[...], m_ref[...], v_ref[...])
        d_ref[...] = delta
        m2_ref[...] = m2
        v2_ref[...] = v2

    vm = pl.BlockSpec(memory_space=pltpu.VMEM)
    return pl.pallas_call(
        body,
        name="adamw_small",
        in_specs=[vm] * 4,
        out_specs=[vm] * 3,
        out_shape=[jax.ShapeDtypeStruct(g.shape, F32)] * 3,
    )(g, w, m, v)


def _relu2_epilogue(acc):
    ra = jnp.maximum(acc, 0.0)
    return ra * ra, ra


def _residual_norm_epilogue(acc, res, g):
    h = acc + res
    return h, h * lax.rsqrt(jnp.mean(h * h, axis=-1, keepdims=True) + EPS) * g


def _rows(stacked):
    return stacked.reshape(stacked.shape[0] * stacked.shape[1], stacked.shape[2])


def _by_chip(mat):
    return mat.reshape(N_CHIPS, mat.shape[0] // N_CHIPS, mat.shape[1])


def _local_step(x, p, target, shards, small, update):
    s, d = x.shape
    cos_t, sin_t = _rope_tables(s)
    bucket = _band_buckets()
    p_bf = p.astype(BF16)
    wts = {}

    chip = 2 * lax.axis_index("x") + lax.axis_index("y")
    core = lax.axis_index("c")

    def gather(tag, names, after):
        srcs = [cast[n] for n in names]
        route = _gather_route([a.shape for a in srcs])
        sems, srcs, lands, token = _split_start(f"gather_start_{tag}", srcs, [zones[n] for n in names], route, after)

        def landed(done):
            got_srcs, got_lands = _split_wait(f"gather_wait_{tag}", srcs, lands, sems, route, done)
            comm = _swap_comm(got_srcs, got_lands)
            comm.waited = got_srcs[0]
            return comm

        return landed, token

    def prepare(n, zero):
        cast[n] = (shards[n] + zero).astype(BF16)
        zones[n] = _own_slot((N_CHIPS,) + cast[n].shape, BF16, chip, cast[n])

    cast, zones = {}, {}
    prepare("w_in", 0.0)
    in_landed, token = gather("in", ["w_in"], small["attn_norm_g"])
    for n in shards:
        if n != "w_in":
            prepare(n, token[:1, :1])
    g_attn = small["attn_norm_g"] + token[:1, :1]
    u = _rms_fwd(x, g_attn, name="norm_attn")
    prepared = u[:1, :1].astype(F32) + sum(
        (lax.dynamic_slice(zones[n], (chip, 0, 0), (1, 1, 1))[0] + cast[n][:1, :1]).astype(F32)
        for n in zones if n != "w_in")
    (wts["w_in"],) = _comm_only("swap_w_in", in_landed(prepared))
    mid_landed, token = gather("mid", ["w_out"], wts["w_in"])
    proj = _matmul(u, wts["w_in"], mode="nn", out_dtypes=[F32], name="mm_in", bn=768, after=token)
    pb, (w_out_s,) = _qk_prep(proj, small["q_norm_g"], small["k_norm_g"], cos_t, sin_t, comm=mid_landed(proj))
    wts["w_out"] = _rows(w_out_s)
    up_landed, token = gather("up", ["w_up"], pb)
    att_a, lse_a = _attn_a_fwd(pb, after=token)
    pad = ((PAD_LO, PAD_HI), (0, 0))
    kpad = jnp.pad(pb[:, COL_KB * HEAD_DIM:COL_VB * HEAD_DIM], pad)
    vpad = jnp.pad(pb[:, COL_VB * HEAD_DIM:], pad)
    up_swap = up_landed(att_a)
    down_landed, token = gather("down", ["w_down"], up_swap.waited)
    (att, lse_b), (wts["w_up"],) = _attn_b_fwd(pb, kpad, vpad, bucket, small["rel_bias_table"],
                                               small["sink_logits"], att_a, comm=up_swap, after=token)
    h1, mn = _matmul(att, wts["w_out"], mode="nn", out_dtypes=[F32, BF16], name="mm_out", bm=512, bn=d,
                     epilogue=_residual_norm_epilogue, extras=(x,), vecs=(small["mlp_norm_g"],))
    r, ra = _matmul(mn, wts["w_up"], mode="nn", out_dtypes=[BF16, BF16], name="mm_up", epilogue=_relu2_epilogue,
                    bm=2048)
    (w_down_s,) = _comm_only("swap_w_down", down_landed(r))
    wts["w_down"] = _rows(w_down_s)
    late_landed, token = gather("late", ["w_gate", "ple_w"], w_down_s)
    h2 = _matmul(r, wts["w_down"], mode="nn", out_dtypes=[F32], name="mm_down",
                 epilogue=lambda acc, res: (acc + res,), extras=(h1,), after=token)
    ng, (w_gate_s, wts["ple_w"]) = _rms_fwd(h2, small["gate_norm_g"], name="norm_gate", comm=late_landed(h2))
    wts["w_gate"] = _rows(w_gate_s)
    gate = _matmul(ng, wts["w_gate"], mode="nn", out_dtypes=[F32], name="mm_gate",
                   epilogue=lambda acc: (1.0 / (1.0 + jnp.exp(-acc)),))
    pp = _matmul(p_bf, wts["ple_w"], mode="nn", out_dtypes=[F32], name="mm_ple", bn=512)
    dh3, dz, dpp, dg_final, dg_ple, loss = _tail(h2, gate, pp, target, small["ple_norm_g"], small["final_norm_g"])

    dng = _matmul(dz, wts["w_gate"], mode="nt", out_dtypes=[F32], name="mm_gate_dx")
    gw_gate = _matmul(ng, dz, mode="tn", out_dtypes=[BF16], name="mm_gate_dw")
    gw_ple = _matmul(p_bf, dpp, mode="tn", out_dtypes=[BF16], name="mm_ple_dw", bn=512, out_stack=N_CHIPS)
    dh2, dh2_bf, dg_gate = _rms_bwd(h2, dng, small["gate_norm_g"], dh3, name="norm_gate_bwd", want_bf16=True)

    def exchange(tag, partials, after):
        route = _exchange_route(len(partials))
        lands = [_own_slot((N_DEV,) + g.shape[1:], g.dtype, 2 * chip + core,
                           lax.dynamic_index_in_dim(g, chip, 0, keepdims=False)) for g in partials]
        sems, srcs, lands, token = _split_start(f"exchange_start_{tag}", partials, lands, route, after)

        def landed(done):
            got_srcs, got_lands = _split_wait(f"exchange_wait_{tag}", srcs, lands, sems, route, done)
            comm = _forward_comm(got_srcs, got_lands)
            comm.waited = got_srcs[0]
            return comm

        return landed, token

    big = {}
    gate_landed, token = exchange("gate", [_by_chip(gw_gate), gw_ple], dh2_bf)
    gw_down = _matmul(r, dh2_bf, mode="tn", out_dtypes=[BF16], name="mm_down_dw", after=token, bm=512, bk=4096)
    da, (parts_gate, parts_ple) = _matmul(
        dh2_bf, wts["w_down"], mode="nt", out_dtypes=[BF16], name="mm_down_dx", bm=2048,
        epilogue=lambda acc, ra_v: (acc * (2.0 * ra_v.astype(F32)),), extras=(ra,), comm=gate_landed(gw_down))
    down_landed, token = exchange("down", [_by_chip(gw_down)], da)
    big["w_gate"], big["ple_w"] = update("w_gate", parts_gate), update("ple_w", parts_ple)
    gw_up = _matmul(mn, da, mode="tn", out_dtypes=[BF16], name="mm_up_dw", out_stack=N_CHIPS, after=token,
                    bm=512, bk=4096)
    dmn = _matmul(da, wts["w_up"], mode="nt", out_dtypes=[F32], name="mm_up_dx")
    dh1, dh1_bf, dg_mlp = _rms_bwd(h1, dmn, small["mlp_norm_g"], dh2, name="norm_mlp_bwd", want_bf16=True)
    datt, (parts_down,) = _matmul(dh1_bf, wts["w_out"], mode="nt", out_dtypes=[BF16], name="mm_out_dx",
                                  comm=down_landed(dh1_bf))
    gw_out = _matmul(att, dh1_bf, mode="tn", out_dtypes=[BF16], name="mm_out_dw")
    up_landed, token = exchange("up", [gw_up], datt)
    dqb, dkpad, dvpad, dtab, dsink = _attn_b_bwd(pb, kpad, vpad, att, datt, lse_b, bucket,
                                                 small["rel_bias_table"], small["sink_logits"], after=token)
    dqa, dka, dva = _attn_a_bwd(pb, att, datt, lse_a)
    up_forward = up_landed(dqa)
    out_landed, token = exchange("out", [_by_chip(gw_out)], up_forward.waited)
    (dproj, dg_q, dg_k), (parts_up,) = _qk_bwd(dqa, dka, dva, dqb, dkpad, dvpad, proj,
                                               small["q_norm_g"], small["k_norm_g"], cos_t, sin_t,
                                               comm=up_forward, after=token)
    gw_in = _matmul(u, dproj, mode="tn", out_dtypes=[BF16], name="mm_in_dw", bn=768, out_stack=N_CHIPS)
    out_forward = out_landed(gw_in)
    in_landed, token = exchange("in", [gw_in], out_forward.waited)
    du, (parts_out,) = _matmul(dproj, wts["w_in"], mode="nt", out_dtypes=[F32], name="mm_in_dx", bk=3072,
                               comm=out_forward, after=token)
    grad_x, dg_attn = _rms_bwd(x, du, small["attn_norm_g"], dh1, name="norm_attn_bwd", want_bf16=False)
    for n, parts in (("w_down", parts_down), ("w_up", parts_up), ("w_out", parts_out)):
        big[n] = update(n, parts)
    done = dg_attn + sum(big[n][0][0, :1, :] for n in ("w_down", "w_up", "w_out"))
    (parts_in,) = _comm_only("forward_w_in", in_landed(done))
    big["w_in"] = update("w_in", parts_in)

    small_g = {
        "attn_norm_g": dg_attn, "mlp_norm_g": dg_mlp, "ple_norm_g": dg_ple, "gate_norm_g": dg_gate,
        "final_norm_g": dg_final, "q_norm_g": dg_q, "k_norm_g": dg_k,
        "sink_logits": dsink[:, 0, 0][None, :], "rel_bias_table": dtab[:, :, 0].T,
    }
    return loss, grad_x, big, small_g


_SMALL_ROWS = ["attn_norm_g", "mlp_norm_g", "ple_norm_g", "gate_norm_g", "final_norm_g"]
_PACK_ROWS = 8


def _pack_small(vals, d):
    rows = [vals[n].reshape(1, d) for n in _SMALL_ROWS]
    misc = jnp.concatenate([
        vals["q_norm_g"].reshape(1, HEAD_DIM), vals["k_norm_g"].reshape(1, HEAD_DIM),
        jnp.pad(vals["sink_logits"].reshape(1, N_HEADS_B), ((0, 0), (0, HEAD_DIM - N_HEADS_B))),
        vals["rel_bias_table"].reshape(1, N_BUCKETS * N_HEADS_B)], axis=1)
    rows.append(jnp.pad(misc, ((0, 0), (0, d - misc.shape[1]))))
    rows.append(jnp.zeros((_PACK_ROWS - len(rows), d), F32))
    return jnp.concatenate(rows, axis=0).astype(F32)


def _unpack_small(pack, shapes):
    out = {n: pack[i].reshape(shapes[n]) for i, n in enumerate(_SMALL_ROWS)}
    misc = pack[len(_SMALL_ROWS)]
    out["q_norm_g"] = misc[:HEAD_DIM].reshape(shapes["q_norm_g"])
    out["k_norm_g"] = misc[HEAD_DIM:2 * HEAD_DIM].reshape(shapes["k_norm_g"])
    out["sink_logits"] = misc[2 * HEAD_DIM:2 * HEAD_DIM + N_HEADS_B].reshape(shapes["sink_logits"])
    out["rel_bias_table"] = misc[3 * HEAD_DIM:3 * HEAD_DIM + N_BUCKETS * N_HEADS_B].reshape(shapes["rel_bias_table"])
    return out


_WEIGHTS = ["attn_norm_g", "w_in", "q_norm_g", "k_norm_g", "sink_logits", "w_out", "mlp_norm_g", "w_up", "w_down",
            "ple_w", "ple_norm_g", "gate_norm_g", "w_gate", "rel_bias_table", "final_norm_g"]
_BIG = ["w_in", "w_out", "w_up", "w_down", "ple_w", "w_gate"]


def kernel(x, p, attn_norm_g, w_in, q_norm_g, k_norm_g, sink_logits, w_out, mlp_norm_g, w_up, w_down, ple_w, ple_norm_g, gate_norm_g, w_gate, rel_bias_table, final_norm_g, loss_target, m_attn_norm_g, m_w_in, m_q_norm_g, m_k_norm_g, m_sink_logits, m_w_out, m_mlp_norm_g, m_w_up, m_w_down, m_ple_w, m_ple_norm_g, m_gate_norm_g, m_w_gate, m_rel_bias_table, m_final_norm_g, v_attn_norm_g, v_w_in, v_q_norm_g, v_k_norm_g, v_sink_logits, v_w_out, v_mlp_norm_g, v_w_up, v_w_down, v_ple_w, v_ple_norm_g, v_gate_norm_g, v_w_gate, v_rel_bias_table, v_final_norm_g):
    given = dict(locals())
    w = {n: given[n] for n in _WEIGHTS}
    m = {n: given["m_" + n] for n in _WEIGHTS}
    v = {n: given["v_" + n] for n in _WEIGHTS}
    d = x.shape[-1]

    shards = {n: w[n][0] for n in _BIG}
    small = {
        "attn_norm_g": w["attn_norm_g"], "mlp_norm_g": w["mlp_norm_g"], "ple_norm_g": w["ple_norm_g"],
        "gate_norm_g": w["gate_norm_g"], "final_norm_g": w["final_norm_g"].reshape(1, d),
        "q_norm_g": w["q_norm_g"], "k_norm_g": w["k_norm_g"], "sink_logits": w["sink_logits"],
        "rel_bias_table": w["rel_bias_table"],
    }

    def update(n, parts):
        res = _sum_adamw(parts, w[n][0], m[n][0], v[n][0], name="adamw_" + n)
        return [t.reshape(w[n].shape) for t in res]

    loss_part, grad_x, big, small_g = _local_step(x[0], p[0, 0], loss_target[0], shards, small, update)
    grads, deltas, new_m, new_v = [{n: big[n][i] for n in _BIG} for i in range(4)]

    shapes = {n: w[n].shape for n in _WEIGHTS if n not in _BIG}
    pack = _pack_small(small_g, d)
    pack = pack.at[_PACK_ROWS - 1, :1].add(0.0 * grads["w_in"][0, 0, :1])
    pack = pack.at[_PACK_ROWS - 1, 1].set(loss_part[0, 0])
    g_small = _allreduce_small(pack)
    loss = g_small[_PACK_ROWS - 1, 1]
    d_small, m_small, v_small = _adamw_small(g_small, _pack_small(w, d), _pack_small(m, d), _pack_small(v, d))
    grads.update(_unpack_small(g_small, shapes))
    deltas.update(_unpack_small(d_small, shapes))
    new_m.update(_unpack_small(m_small, shapes))
    new_v.update(_unpack_small(v_small, shapes))

    return (loss, grad_x[None], *[grads[n] for n in _WEIGHTS], *[deltas[n] for n in _WEIGHTS],
            *[new_m[n] for n in _WEIGHTS], *[new_v[n] for n in _WEIGHTS])
```

```python
import functools
import math

import jax
import jax.numpy as jnp
import numpy as np
from jax import lax
from jax.experimental import pallas as pl
from jax.experimental.pallas import tpu as pltpu

F32 = jnp.float32
BF16 = jnp.bfloat16

HEAD_DIM = 128
N_HEADS_A = 8
N_KV_A = 2
N_HEADS_B = 8
N_KV_B = 2
GROUP = 4
GRID_W = 64
BLOCK_Q = 128
WINDOW = 128
N_BUCKETS = 32
MAX_DISTANCE = 128
ROPE_THETA = 10000.0
EPS = 1e-6
NEG_INF = -1e30
ATT_SCALE = HEAD_DIM ** -0.5
LOG2E = math.log2(math.e)
LN2 = math.log(2.0)
Q_SCALE = ATT_SCALE * LOG2E
PAD_LO, PAD_HI = 256, 128

ADAM_LR = 0.001
ADAM_B1 = 0.9
ADAM_B2 = 0.999
ADAM_EPS = 1e-08
ADAM_WD = 0.01
ADAM_STEP = 10

N_CHIPS = 4
N_DEV = 8
COL_QA, COL_KA, COL_VA, COL_QB, COL_KB, COL_VB = 0, 8, 10, 12, 20, 22
N_COLS = 24

VMEM_LIMIT = 52 * 1024 * 1024


def _params(sem=None, collective_id=None):
    return pltpu.CompilerParams(dimension_semantics=sem, vmem_limit_bytes=VMEM_LIMIT, collective_id=collective_id)


_ANY = pl.BlockSpec(memory_space=pl.ANY)
_MESH = pl.DeviceIdType.MESH
SIBLING_BARRIER_ID = 1


def _sibling():
    return (lax.axis_index("x"), lax.axis_index("y"), 1 - lax.axis_index("c"))


class _Comm:
    def __init__(self, inputs, out_shapes, sems, start, finish, aliases=None):
        self.inputs, self.out_shapes, self.sems = list(inputs), list(out_shapes), list(sems)
        self.start, self.finish, self.aliases = start, finish, dict(aliases or {})


def _call(body, *, name, grid, in_specs, out_specs, out_shape, args, scratch_shapes=(), sem=None, comm=None,
          after=None, aliases=None):
    in_specs, out_specs, out_shape = list(in_specs), list(out_specs), list(out_shape)
    scratch_shapes = list(scratch_shapes)
    n_in, n_out, n_sc = len(in_specs), len(out_specs), len(scratch_shapes)
    behind = [] if after is None else [after]
    aliases = dict(aliases or {})
    if comm is None:
        res = pl.pallas_call(
            (lambda *refs: body(*refs[:n_in], *refs[n_in + len(behind):])) if behind else body,
            name=name, grid=grid, in_specs=in_specs + [_ANY] * len(behind), out_specs=out_specs,
            out_shape=out_shape, scratch_shapes=scratch_shapes, input_output_aliases=aliases,
            compiler_params=_params(sem))(*args, *behind)
        return list(res), []
    c_in, c_out = len(comm.inputs), len(comm.out_shapes)

    def hosted(*refs):
        pos = [0]

        def take(n):
            pos[0] += n
            return refs[pos[0] - n:pos[0]]

        ins, c_ins, _, outs, c_outs, scr = (take(n_in), take(c_in), take(len(behind)), take(n_out), take(c_out),
                                            take(n_sc))
        c_sems = refs[pos[0]:]
        ids = [pl.program_id(a) for a in range(len(grid))]
        first = functools.reduce(jnp.logical_and, [i == 0 for i in ids])
        last = functools.reduce(jnp.logical_and, [i == g - 1 for i, g in zip(ids, grid)])

        @pl.when(first)
        def _():
            barrier = pltpu.get_barrier_semaphore()
            pl.semaphore_signal(barrier, inc=1, device_id=_sibling(), device_id_type=_MESH)
            pl.semaphore_wait(barrier, 1)
            comm.start(c_ins, c_outs, c_sems)

        body(*ins, *outs, *scr)

        @pl.when(last)
        def _():
            comm.finish(c_ins, c_outs, c_sems)

    res = pl.pallas_call(
        hosted, name=name, grid=grid, in_specs=in_specs + [_ANY] * (c_in + len(behind)),
        out_specs=out_specs + [_ANY] * c_out,
        out_shape=out_shape + comm.out_shapes, scratch_shapes=scratch_shapes + comm.sems,
        input_output_aliases={**aliases, **{n_in + i: n_out + o for i, o in comm.aliases.items()}},
        compiler_params=_params(("arbitrary",) * len(grid), SIBLING_BARRIER_ID))(*args, *comm.inputs, *behind)
    return list(res[:n_out]), list(res[n_out:])


def _matmul(a, b, *, mode, out_dtypes, name, epilogue=None, extras=(), bm=1024, bn=1024, bk=2048,
            out_stack=0, comm=None, after=None, vecs=()):
    stacked = b.ndim == 3
    if mode == "nn":
        m, k = a.shape
        if stacked:
            nj, kb, ns = b.shape
            n, ks = nj * ns, k
        else:
            kb, n = b.shape
            ns, ks = n, k
        dn = (((1,), (0,)), ((), ()))
    elif mode == "nt":
        m, k = a.shape
        if stacked:
            nj, n, ks = b.shape
            kb = nj * ks
        else:
            n, kb = b.shape
            ks = kb
        ns = n
        dn = (((1,), (1,)), ((), ()))
    else:
        k, m = a.shape
        kb, n = b.shape
        ns, ks = n, k
        dn = (((0,), (0,)), ((), ()))
    assert k == kb and not (stacked and mode == "tn")
    ns_out = n // out_stack if out_stack else n
    whole_k = stacked and mode == "nt" and bk >= k
    bm, bn, bk = min(bm, m), min(bn, ns, ns_out), k if whole_k else min(bk, ks)
    assert m % bm == 0 and ns % bn == 0 and ns_out % bn == 0 and ks % bk == 0 or whole_k
    gm, gn, gk = m // bm, n // bn, k // bk

    if mode == "tn":
        a_spec = pl.BlockSpec((bk, bm), lambda i, j, q: (q, i))
    else:
        a_spec = pl.BlockSpec((bm, bk), lambda i, j, q: (i, q))
    if mode == "nt":
        if whole_k:
            b_spec = pl.BlockSpec((nj, bn, ks), lambda i, j, q: (0, j, 0))
        elif stacked:
            per = ks // bk
            b_spec = pl.BlockSpec((None, bn, bk), lambda i, j, q: (q // per, j, q % per))
        else:
            b_spec = pl.BlockSpec((bn, bk), lambda i, j, q: (j, q))
    else:
        if stacked:
            per = ns // bn
            b_spec = pl.BlockSpec((None, bk, bn), lambda i, j, q: (j // per, q, j % per))
        else:
            b_spec = pl.BlockSpec((bk, bn), lambda i, j, q: (q, j))
    ex_spec = pl.BlockSpec((bm, bn), lambda i, j, q: (i, j))
    if out_stack:
        per_o = ns_out // bn
        o_spec = pl.BlockSpec((None, bm, bn), lambda i, j, q: (j // per_o, i, j % per_o))
        o_shape = (out_stack, m, ns_out)
    else:
        o_spec = ex_spec
        o_shape = (m, n)
    n_ex, n_out = len(extras) + len(vecs), len(out_dtypes)

    def body(a_ref, b_ref, *rest):
        ex, outs = rest[:n_ex], rest[n_ex:n_ex + n_out]
        if whole_k:
            part = sum(lax.dot_general(a_ref[:, t * ks:(t + 1) * ks], b_ref[t], dn, preferred_element_type=F32)
                       for t in range(nj))
        else:
            part = lax.dot_general(a_ref[...], b_ref[...], dn, preferred_element_type=F32)

        def finish(acc):
            res = epilogue(acc, *[e[...] for e in ex]) if epilogue else (acc,)
            for o, r in zip(outs, res):
                o[...] = r.astype(o.dtype)

        if gk == 1:
            finish(part)
        else:
            acc_ref = rest[-1]
            q = pl.program_id(2)

            @pl.when(q == 0)
            def _():
                acc_ref[...] = part

            @pl.when(q > 0)
            def _():
                acc_ref[...] += part

            @pl.when(q == gk - 1)
            def _():
                finish(acc_ref[...])

    res, c_res = _call(
        body, name=name, grid=(gm, gn, gk),
        in_specs=[a_spec, b_spec] + [ex_spec] * len(extras)
        + [pl.BlockSpec((1, bn), lambda i, j, q: (0, j))] * len(vecs),
        out_specs=[o_spec] * n_out,
        out_shape=[jax.ShapeDtypeStruct(o_shape, dt) for dt in out_dtypes],
        scratch_shapes=[pltpu.VMEM((bm, bn), F32)] if gk > 1 else [],
        sem=("parallel", "parallel", "arbitrary"), args=(a, b, *extras, *vecs), comm=comm, after=after)
    res = res[0] if n_out == 1 else res
    return res if comm is None else (res, c_res)


def _rms_fwd(x, g, *, name, tm=256, comm=None):
    s, d = x.shape
    tm = min(tm, s)

    def body(x_ref, g_ref, o_ref):
        xf = x_ref[...]
        r = lax.rsqrt(jnp.mean(xf * xf, axis=-1, keepdims=True) + EPS)
        o_ref[...] = (xf * r * g_ref[...]).astype(o_ref.dtype)

    res, c_res = _call(
        body, name=name, grid=(s // tm,),
        in_specs=[pl.BlockSpec((tm, d), lambda i: (i, 0)), pl.BlockSpec((1, d), lambda i: (0, 0))],
        out_specs=[pl.BlockSpec((tm, d), lambda i: (i, 0))],
        out_shape=[jax.ShapeDtypeStruct((s, d), BF16)],
        sem=("parallel",), args=(x, g), comm=comm)
    return res[0] if comm is None else (res[0], c_res)


def _rms_bwd(x, dy, g, add, *, name, want_bf16, tm=256):
    s, d = x.shape
    tm = min(tm, s)

    def body(x_ref, dy_ref, g_ref, add_ref, dx_ref, *rest):
        dg_ref = rest[-1]
        i = pl.program_id(0)
        xf = x_ref[...]
        dyf = dy_ref[...].astype(F32)
        r = lax.rsqrt(jnp.mean(xf * xf, axis=-1, keepdims=True) + EPS)
        xh = xf * r
        dyg = dyf * g_ref[...]
        dx = r * (dyg - xh * jnp.mean(dyg * xh, axis=-1, keepdims=True))
        tot = add_ref[...] + dx
        dx_ref[...] = tot
        if want_bf16:
            rest[0][...] = tot.astype(BF16)
        part = jnp.sum(dyf * xh, axis=0, keepdims=True)

        @pl.when(i == 0)
        def _():
            dg_ref[...] = part

        @pl.when(i > 0)
        def _():
            dg_ref[...] += part

    row = pl.BlockSpec((tm, d), lambda i: (i, 0))
    vec = pl.BlockSpec((1, d), lambda i: (0, 0))
    out_specs = [row] + ([row] if want_bf16 else []) + [vec]
    out_shape = [jax.ShapeDtypeStruct((s, d), F32)]
    if want_bf16:
        out_shape.append(jax.ShapeDtypeStruct((s, d), BF16))
    out_shape.append(jax.ShapeDtypeStruct((1, d), F32))
    return pl.pallas_call(
        body,
        name=name,
        grid=(s // tm,),
        in_specs=[row, row, vec, row],
        out_specs=out_specs,
        out_shape=out_shape,
        compiler_params=_params(("arbitrary",)),
    )(x, dy, g, add)


def _tail(h2, gate, pp, target, g_ple, g_final, *, tm=128):
    s, d = h2.shape
    tm = min(tm, s)

    def body(h2_ref, gate_ref, pp_ref, t_ref, gp_ref, gf_ref, dh3_ref, dz_ref, dpp_ref, dgf_ref, dgp_ref, loss_ref):
        i = pl.program_id(0)
        ppf = pp_ref[...]
        gate_v = gate_ref[...]
        r_p = lax.rsqrt(jnp.mean(ppf * ppf, axis=-1, keepdims=True) + EPS)
        eh = ppf * r_p
        e = eh * gp_ref[...]
        h3 = h2_ref[...] + gate_v * e
        r_f = lax.rsqrt(jnp.mean(h3 * h3, axis=-1, keepdims=True) + EPS)
        yh = h3 * r_f
        diff = yh * gf_ref[...] - t_ref[...]
        loss_part = 0.5 * jnp.sum(jnp.mean(diff * diff, axis=-1, keepdims=True), axis=0, keepdims=True)
        dy = diff / d
        dgf = jnp.sum(dy * yh, axis=0, keepdims=True)
        dyg = dy * gf_ref[...]
        dh3 = r_f * (dyg - yh * jnp.mean(dyg * yh, axis=-1, keepdims=True))
        dh3_ref[...] = dh3
        de = dh3 * gate_v
        dz_ref[...] = (dh3 * e * gate_v * (1.0 - gate_v)).astype(BF16)
        dgp = jnp.sum(de * eh, axis=0, keepdims=True)
        deg = de * gp_ref[...]
        dpp_ref[...] = (r_p * (deg - eh * jnp.mean(deg * eh, axis=-1, keepdims=True))).astype(BF16)
        loss_row = jnp.broadcast_to(loss_part, (1, 128))

        @pl.when(i == 0)
        def _():
            dgf_ref[...] = dgf
            dgp_ref[...] = dgp
            loss_ref[...] = loss_row

        @pl.when(i > 0)
        def _():
            dgf_ref[...] += dgf
            dgp_ref[...] += dgp
            loss_ref[...] += loss_row

    row = pl.BlockSpec((tm, d), lambda i: (i, 0))
    vec = pl.BlockSpec((1, d), lambda i: (0, 0))
    return pl.pallas_call(
        body,
        name="tail_fwd_bwd",
        grid=(s // tm,),
        in_specs=[row, row, row, row, vec, vec],
        out_specs=[row, row, row, vec, vec, pl.BlockSpec((1, 128), lambda i: (0, 0))],
        out_shape=[
            jax.ShapeDtypeStruct((s, d), F32),
            jax.ShapeDtypeStruct((s, d), BF16),
            jax.ShapeDtypeStruct((s, d), BF16),
            jax.ShapeDtypeStruct((1, d), F32),
            jax.ShapeDtypeStruct((1, d), F32),
            jax.ShapeDtypeStruct((1, 128), F32),
        ],
        compiler_params=_params(("arbitrary",)),
    )(h2, gate, pp, target, g_ple, g_final)


def _rope_tables(s):
    rows = s // GRID_W
    half = HEAD_DIM // 2
    inv_freq = ROPE_THETA ** (-jnp.arange(0, half, 2, dtype=F32) / half)
    ang_r = jnp.arange(rows, dtype=jnp.int32).astype(F32)[:, None] * inv_freq
    ang_c = jnp.arange(GRID_W, dtype=jnp.int32).astype(F32)[:, None] * inv_freq
    cr, sr = (jnp.repeat(t, GRID_W, axis=0) for t in (jnp.cos(ang_r), jnp.sin(ang_r)))
    cc, sc = (jnp.tile(t, (rows, 1)) for t in (jnp.cos(ang_c), jnp.sin(ang_c)))
    cos_t = jnp.concatenate([cr, cr, cc, cc], axis=-1)
    sin_t = jnp.concatenate([-sr, sr, -sc, sc], axis=-1)
    return cos_t, sin_t


def _swap_quarters(x):
    lane = lax.broadcasted_iota(jnp.int32, x.shape, x.ndim - 1)
    up = pltpu.roll(x, HEAD_DIM - 32, x.ndim - 1)
    down = pltpu.roll(x, 32, x.ndim - 1)
    return jnp.where((lane % 64) < 32, up, down)


def _cols(first, count=1):
    return slice(first * HEAD_DIM, (first + count) * HEAD_DIM)


def _qk_prep(proj, g_q, g_k, cos_t, sin_t, *, tm=256, comm=None):
    s, n = proj.shape
    tm = min(tm, s)

    def body(x_ref, gq_ref, gk_ref, c_ref, s_ref, o_ref):
        cos_v, sin_v = c_ref[...], s_ref[...]
        for h in range(COL_VA):
            x = x_ref[:, _cols(h)]
            g = gq_ref[...] if h < COL_KA else gk_ref[...]
            xn = x * lax.rsqrt(jnp.mean(x * x, axis=-1, keepdims=True) + EPS) * g
            xr = xn * cos_v + _swap_quarters(xn) * sin_v
            if h < COL_KA:
                xr = xr * Q_SCALE
            o_ref[:, _cols(h)] = xr.astype(BF16)
        o_ref[:, _cols(COL_VA, 2)] = x_ref[:, _cols(COL_VA, 2)].astype(BF16)
        o_ref[:, _cols(COL_QB, N_HEADS_B)] = (x_ref[:, _cols(COL_QB, N_HEADS_B)] * Q_SCALE).astype(BF16)
        o_ref[:, _cols(COL_KB, 4)] = x_ref[:, _cols(COL_KB, 4)].astype(BF16)

    row = pl.BlockSpec((tm, n), lambda i: (i, 0))
    tab = pl.BlockSpec((tm, HEAD_DIM), lambda i: (i, 0))
    vec = pl.BlockSpec((1, HEAD_DIM), lambda i: (0, 0))
    res, c_res = _call(
        body, name="qk_prep", grid=(s // tm,),
        in_specs=[row, vec, vec, tab, tab],
        out_specs=[row],
        out_shape=[jax.ShapeDtypeStruct((s, n), BF16)],
        sem=("parallel",), args=(proj, g_q, g_k, cos_t, sin_t), comm=comm)
    return res[0] if comm is None else (res[0], c_res)


def _qk_bwd(dqa, dka, dva, dqb, dkpad, dvpad, proj, g_q, g_k, cos_t, sin_t, *, comm=None, after=None):
    s, n = proj.shape
    tm = min(PAD_LO, s)
    assert PAD_LO % tm == 0
    lo = PAD_LO // tm

    def body(dqa_ref, dka_ref, dva_ref, dqb_ref, dkb_ref, dvb_ref, x_ref, gq_ref, gk_ref, c_ref, s_ref,
             o_ref, dgq_ref, dgk_ref):
        i = pl.program_id(0)
        cos_v, sin_v = c_ref[...], s_ref[...]

        def head(d, x, g):
            dn = d * cos_v + _swap_quarters(d * sin_v)
            r = lax.rsqrt(jnp.mean(x * x, axis=-1, keepdims=True) + EPS)
            xh = x * r
            dng = dn * g
            dx = r * (dng - xh * jnp.mean(dng * xh, axis=-1, keepdims=True))
            return dx.astype(BF16), jnp.sum(dn * xh, axis=0, keepdims=True)

        acc_q = jnp.zeros((1, HEAD_DIM), F32)
        acc_k = jnp.zeros((1, HEAD_DIM), F32)
        for h in range(N_HEADS_A):
            o_ref[:, _cols(h)], part = head(dqa_ref[:, _cols(h)] * ATT_SCALE, x_ref[:, _cols(h)], gq_ref[...])
            acc_q = acc_q + part
        for h in range(N_KV_A):
            o_ref[:, _cols(COL_KA + h)], part = head(dka_ref[:, _cols(h)] * LN2, x_ref[:, _cols(COL_KA + h)],
                                                     gk_ref[...])
            acc_k = acc_k + part
        o_ref[:, _cols(COL_VA, 2)] = dva_ref[...].astype(BF16)
        o_ref[:, _cols(COL_QB, N_HEADS_B)] = (dqb_ref[...] * ATT_SCALE).astype(BF16)
        o_ref[:, _cols(COL_KB, 2)] = (dkb_ref[...] * LN2).astype(BF16)
        o_ref[:, _cols(COL_VB, 2)] = dvb_ref[...].astype(BF16)

        @pl.when(i == 0)
        def _():
            dgq_ref[...] = acc_q
            dgk_ref[...] = acc_k

        @pl.when(i > 0)
        def _():
            dgq_ref[...] += acc_q
            dgk_ref[...] += acc_k

    def rows(width, shift=0):
        return pl.BlockSpec((tm, width), lambda i: (i + shift, 0))

    kv_w = N_KV_A * HEAD_DIM
    q_w = N_HEADS_A * HEAD_DIM
    vec = pl.BlockSpec((1, HEAD_DIM), lambda i: (0, 0))
    res, c_res = _call(
        body, name="qk_bwd", grid=(s // tm,),
        in_specs=[rows(q_w), rows(kv_w), rows(kv_w), rows(q_w), rows(kv_w, lo), rows(kv_w, lo), rows(n),
                  vec, vec, rows(HEAD_DIM), rows(HEAD_DIM)],
        out_specs=[rows(n), vec, vec],
        out_shape=[
            jax.ShapeDtypeStruct((s, n), BF16),
            jax.ShapeDtypeStruct((1, HEAD_DIM), F32),
            jax.ShapeDtypeStruct((1, HEAD_DIM), F32),
        ],
        sem=("arbitrary",), args=(dqa, dka, dva, dqb, dkpad, dvpad, proj, g_q, g_k, cos_t, sin_t), comm=comm,
        after=after)
    return res if comm is None else (res, c_res)


_NT = (((1,), (1,)), ((), ()))
_TN = (((0,), (0,)), ((), ()))


def _attn_a_fwd(pb, *, tq=4096, sub=256, comm=None, after=None):
    s = pb.shape[0]
    tq = min(tq, s)

    sub = min(sub, tq)

    def body(q_ref, k_ref, v_ref, o_ref, lse_ref):
        k = k_ref[...]
        v = v_ref[...]
        for r in range(tq // sub):
            rows = pl.ds(r * sub, sub)
            sc = lax.dot_general(q_ref[rows, :], k, _NT, preferred_element_type=F32)
            m = jnp.max(sc, axis=-1, keepdims=True)
            p = jnp.exp2(sc - m)
            l = jnp.sum(p, axis=-1, keepdims=True)
            o = jnp.dot(p.astype(BF16), v, preferred_element_type=F32)
            o_ref[rows, :] = (o / l).astype(BF16)
            lse_ref[rows, :] = jnp.broadcast_to(m + jnp.log2(l), (sub, HEAD_DIM))

    res, c_res = _call(
        body, name="attn_a_fwd", grid=(N_HEADS_A, s // tq),
        in_specs=[
            pl.BlockSpec((tq, HEAD_DIM), lambda h, i: (i, COL_QA + h)),
            pl.BlockSpec((s, HEAD_DIM), lambda h, i: (0, COL_KA + h // GROUP)),
            pl.BlockSpec((s, HEAD_DIM), lambda h, i: (0, COL_VA + h // GROUP)),
        ],
        out_specs=[
            pl.BlockSpec((tq, HEAD_DIM), lambda h, i: (i, h)),
            pl.BlockSpec((None, tq, HEAD_DIM), lambda h, i: (h, i, 0)),
        ],
        out_shape=[
            jax.ShapeDtypeStruct((s, (N_HEADS_A + N_HEADS_B) * HEAD_DIM), BF16),
            jax.ShapeDtypeStruct((N_HEADS_A, s, HEAD_DIM), F32),
        ],
        sem=("parallel", "parallel"), args=(pb, pb, pb), comm=comm, after=after)
    return res if comm is None else (res, c_res)


def _attn_a_bwd(pb, att, datt, lse, *, tq=2048, sub=256, comm=None):
    s = pb.shape[0]
    tq = min(tq, s)
    sub = min(sub, tq)

    def body(q_ref, k_ref, v_ref, o_ref, do_ref, lse_ref, dq_ref, dk_ref, dv_ref):
        first = jnp.logical_and(pl.program_id(1) == 0, pl.program_id(2) == 0)
        k = k_ref[...]
        v = v_ref[...]
        dk = dv = None
        for r in range(tq // sub):
            rows = pl.ds(r * sub, sub)
            q = q_ref[rows, :]
            do = do_ref[rows, :]
            sc = lax.dot_general(q, k, _NT, preferred_element_type=F32)
            p = jnp.exp2(sc - lse_ref[rows, :][:, :1])
            dp = lax.dot_general(do, v, _NT, preferred_element_type=F32)
            delta = jnp.sum(do.astype(F32) * o_ref[rows, :].astype(F32), axis=-1, keepdims=True)
            ds = (p * (dp - delta)).astype(BF16)
            dq_ref[rows, :] = jnp.dot(ds, k, preferred_element_type=F32)
            dk_r = lax.dot_general(ds, q, _TN, preferred_element_type=F32)
            dv_r = lax.dot_general(p.astype(BF16), do, _TN, preferred_element_type=F32)
            dk = dk_r if dk is None else dk + dk_r
            dv = dv_r if dv is None else dv + dv_r

        @pl.when(first)
        def _():
            dk_ref[...] = dk
            dv_ref[...] = dv

        @pl.when(jnp.logical_not(first))
        def _():
            dk_ref[...] += dk
            dv_ref[...] += dv

    qmap = lambda kv, g, i: (i, kv * GROUP + g)
    res, c_res = _call(
        body, name="attn_a_bwd", grid=(N_KV_A, GROUP, s // tq),
        in_specs=[
            pl.BlockSpec((tq, HEAD_DIM), lambda kv, g, i: (i, COL_QA + kv * GROUP + g)),
            pl.BlockSpec((s, HEAD_DIM), lambda kv, g, i: (0, COL_KA + kv)),
            pl.BlockSpec((s, HEAD_DIM), lambda kv, g, i: (0, COL_VA + kv)),
            pl.BlockSpec((tq, HEAD_DIM), qmap),
            pl.BlockSpec((tq, HEAD_DIM), qmap),
            pl.BlockSpec((None, tq, HEAD_DIM), lambda kv, g, i: (kv * GROUP + g, i, 0)),
        ],
        out_specs=[
            pl.BlockSpec((tq, HEAD_DIM), qmap),
            pl.BlockSpec((s, HEAD_DIM), lambda kv, g, i: (0, kv)),
            pl.BlockSpec((s, HEAD_DIM), lambda kv, g, i: (0, kv)),
        ],
        out_shape=[
            jax.ShapeDtypeStruct((s, N_HEADS_A * HEAD_DIM), F32),
            jax.ShapeDtypeStruct((s, N_KV_A * HEAD_DIM), F32),
            jax.ShapeDtypeStruct((s, N_KV_A * HEAD_DIM), F32),
        ],
        sem=("arbitrary", "arbitrary", "arbitrary"), args=(pb, pb, pb, att, datt, lse), comm=comm)
    return res if comm is None else (res, c_res)


def _t5_bucket(rel):
    nb = N_BUCKETS // 2
    ret = jnp.where(rel > 0, nb, 0)
    n = jnp.abs(rel)
    max_exact = nb // 2
    nf = jnp.maximum(n, 1).astype(F32)
    large = max_exact + (jnp.log(nf / max_exact) / math.log(MAX_DISTANCE / max_exact)
                         * (nb - max_exact)).astype(jnp.int32)
    large = jnp.minimum(large, nb - 1)
    return ret + jnp.where(n < max_exact, n, large)


def _band_buckets():
    r = jnp.arange(BLOCK_Q, dtype=jnp.int32)
    j = jnp.arange(3 * BLOCK_Q, dtype=jnp.int32)
    return _t5_bucket((j[None, :] - BLOCK_Q) - r[:, None])


def _band_bias(bucket, table_ref, h):
    acc = jnp.zeros(bucket.shape, F32)
    for b in range(N_BUCKETS):
        acc = jnp.where(bucket == b, table_ref[b, h], acc)
    return acc


GQ = GROUP * BLOCK_Q


def _stack_heads(x):
    return jnp.concatenate([x[:, _cols(g)] for g in range(GROUP)], axis=0)


def _unstack_heads(x):
    return jnp.concatenate([x[g * BLOCK_Q:(g + 1) * BLOCK_Q] for g in range(GROUP)], axis=1)


def _group_bias(bucket, table_ref, kv):
    return jnp.concatenate([_band_bias(bucket, table_ref, kv * GROUP + g) * LOG2E for g in range(GROUP)], axis=0)


def _group_sink(sink_ref, kv):
    head = lax.broadcasted_iota(jnp.int32, (GQ, 1), 0) // BLOCK_Q
    snk = jnp.zeros((GQ, 1), F32)
    for g in range(GROUP):
        snk = jnp.where(head == g, sink_ref[0, kv * GROUP + g] * LOG2E, snk)
    return snk


def _band_mask(n, s):
    r = lax.broadcasted_iota(jnp.int32, (GQ, 3 * BLOCK_Q), 0) % BLOCK_Q
    j = lax.broadcasted_iota(jnp.int32, (GQ, 3 * BLOCK_Q), 1)
    rel = j - BLOCK_Q - r
    kabs = n * BLOCK_Q + j - BLOCK_Q
    return (jnp.abs(rel) <= WINDOW) & (kabs >= 0) & (kabs < s)


def _band_start(n):
    return pl.multiple_of(n * BLOCK_Q + (PAD_LO - BLOCK_Q), BLOCK_Q)


def _attn_b_fwd(pb, kpad, vpad, bucket, table, sink, att, *, comm=None, after=None):
    s = pb.shape[0]
    nblk = s // BLOCK_Q
    sp = kpad.shape[0]

    def body(table_ref, sink_ref, q0_ref, q1_ref, k_ref, v_ref, bucket_ref, _, o_ref, lse_ref, bias_ref):
        n = pl.program_id(0)

        @pl.when(n == 0)
        def _():
            for kv in range(N_KV_B):
                bias_ref[kv * GQ:(kv + 1) * GQ, :] = _group_bias(bucket_ref[...], table_ref, kv)

        band = pl.ds(_band_start(n), 3 * BLOCK_Q)
        mask = _band_mask(n, s)
        for kv, q_ref in enumerate((q0_ref, q1_ref)):
            kb = k_ref[band, _cols(kv)]
            vb = v_ref[band, _cols(kv)]
            sc = lax.dot_general(_stack_heads(q_ref[...]), kb, _NT, preferred_element_type=F32)
            sc = jnp.where(mask, sc + bias_ref[kv * GQ:(kv + 1) * GQ, :], NEG_INF)
            snk = _group_sink(sink_ref, kv)
            m = jnp.maximum(jnp.max(sc, axis=-1, keepdims=True), snk)
            p = jnp.exp2(sc - m)
            l = jnp.sum(p, axis=-1, keepdims=True) + jnp.exp2(snk - m)
            o = jnp.dot(p.astype(BF16), vb, preferred_element_type=F32)
            o_ref[:, _cols(kv * GROUP, GROUP)] = _unstack_heads((o / l).astype(BF16))
            lse = m + jnp.log2(l)
            for g in range(GROUP):
                lse_ref[kv * GROUP + g] = jnp.broadcast_to(lse[g * BLOCK_Q:(g + 1) * BLOCK_Q], (BLOCK_Q, HEAD_DIM))

    smem = pl.BlockSpec(memory_space=pltpu.SMEM)
    wide = GROUP * HEAD_DIM
    whole = pl.BlockSpec((sp, N_KV_B * HEAD_DIM), lambda n: (0, 0))
    res, c_res = _call(
        body, name="attn_b_fwd", grid=(nblk,),
        in_specs=[
            smem,
            smem,
            pl.BlockSpec((BLOCK_Q, wide), lambda n: (n, COL_QB // GROUP)),
            pl.BlockSpec((BLOCK_Q, wide), lambda n: (n, COL_QB // GROUP + 1)),
            whole,
            whole,
            pl.BlockSpec((BLOCK_Q, 3 * BLOCK_Q), lambda n: (0, 0)),
            _ANY,
        ],
        out_specs=[
            pl.BlockSpec((BLOCK_Q, N_HEADS_B * HEAD_DIM), lambda n: (n, 1)),
            pl.BlockSpec((N_HEADS_B, BLOCK_Q, HEAD_DIM), lambda n: (0, n, 0)),
        ],
        out_shape=[
            jax.ShapeDtypeStruct(att.shape, BF16),
            jax.ShapeDtypeStruct((N_HEADS_B, s, HEAD_DIM), F32),
        ],
        scratch_shapes=[pltpu.VMEM((N_KV_B * GQ, 3 * BLOCK_Q), F32)],
        sem=("arbitrary",), args=(table, sink, pb, pb, kpad, vpad, bucket, att), comm=comm, after=after,
        aliases={7: 0})
    return res if comm is None else (res, c_res)


def _attn_b_bwd(pb, kpad, vpad, att, datt, lse, bucket, table, sink, *, comm=None, after=None):
    s = pb.shape[0]
    nblk = s // BLOCK_Q
    sp = kpad.shape[0]

    def body(table_ref, sink_ref, q0_ref, q1_ref, k_ref, v_ref, o_ref, do_ref, lse_ref, bucket_ref,
             dq_ref, dk_ref, dv_ref, dtab_ref, dsink_ref, bias_ref, dbias_ref):
        n = pl.program_id(0)

        @pl.when(n == 0)
        def _():
            dk_ref[...] = jnp.zeros_like(dk_ref)
            dv_ref[...] = jnp.zeros_like(dv_ref)
            dbias_ref[...] = jnp.zeros_like(dbias_ref)
            dsink_ref[...] = jnp.zeros_like(dsink_ref)
            for kv in range(N_KV_B):
                bias_ref[kv * GQ:(kv + 1) * GQ, :] = _group_bias(bucket_ref[...], table_ref, kv)

        band = pl.ds(_band_start(n), 3 * BLOCK_Q)
        mask = _band_mask(n, s)
        for kv, q_ref in enumerate((q0_ref, q1_ref)):
            wide_cols = _cols(kv * GROUP, GROUP)
            q = _stack_heads(q_ref[...])
            do = _stack_heads(do_ref[:, wide_cols])
            o = _stack_heads(o_ref[:, wide_cols])
            kb = k_ref[band, _cols(kv)]
            vb = v_ref[band, _cols(kv)]
            lse = jnp.concatenate([lse_ref[kv * GROUP + g][:, :1] for g in range(GROUP)], axis=0)
            sc = lax.dot_general(q, kb, _NT, preferred_element_type=F32)
            sc = jnp.where(mask, sc + bias_ref[kv * GQ:(kv + 1) * GQ, :], NEG_INF)
            p = jnp.exp2(sc - lse)
            dp = lax.dot_general(do, vb, _NT, preferred_element_type=F32)
            delta = jnp.sum(do.astype(F32) * o.astype(F32), axis=-1, keepdims=True)
            ds = p * (dp - delta)
            dsb = ds.astype(BF16)
            dq_ref[:, wide_cols] = _unstack_heads(jnp.dot(dsb, kb, preferred_element_type=F32))
            dk_ref[band, _cols(kv)] += lax.dot_general(dsb, q, _TN, preferred_element_type=F32)
            dv_ref[band, _cols(kv)] += lax.dot_general(p.astype(BF16), do, _TN, preferred_element_type=F32)
            dbias_ref[kv * GQ:(kv + 1) * GQ, :] += ds
            sink_part = -jnp.exp2(_group_sink(sink_ref, kv) - lse) * delta
            for g in range(GROUP):
                rows = slice(g * BLOCK_Q, (g + 1) * BLOCK_Q)
                dsink_ref[kv * GROUP + g] += jnp.broadcast_to(
                    jnp.sum(sink_part[rows], axis=0, keepdims=True), (1, HEAD_DIM))

        @pl.when(n == nblk - 1)
        def _():
            bucket_v = bucket_ref[...]
            row = lax.broadcasted_iota(jnp.int32, (N_BUCKETS, HEAD_DIM), 0)
            for h in range(N_HEADS_B):
                acc = dbias_ref[h * BLOCK_Q:(h + 1) * BLOCK_Q, :]
                tot = jnp.zeros((N_BUCKETS, HEAD_DIM), F32)
                for b in range(N_BUCKETS):
                    tot = jnp.where(row == b, jnp.sum(jnp.where(bucket_v == b, acc, 0.0), keepdims=True), tot)
                dtab_ref[h] = tot

    smem = pl.BlockSpec(memory_space=pltpu.SMEM)
    wide = GROUP * HEAD_DIM
    whole = pl.BlockSpec((sp, N_KV_B * HEAD_DIM), lambda n: (0, 0))
    group_b = pl.BlockSpec((BLOCK_Q, N_HEADS_B * HEAD_DIM), lambda n: (n, 1))
    res, c_res = _call(
        body, name="attn_b_bwd", grid=(nblk,),
        in_specs=[
            smem,
            smem,
            pl.BlockSpec((BLOCK_Q, wide), lambda n: (n, COL_QB // GROUP)),
            pl.BlockSpec((BLOCK_Q, wide), lambda n: (n, COL_QB // GROUP + 1)),
            whole,
            whole,
            group_b,
            group_b,
            pl.BlockSpec((N_HEADS_B, BLOCK_Q, HEAD_DIM), lambda n: (0, n, 0)),
            pl.BlockSpec((BLOCK_Q, 3 * BLOCK_Q), lambda n: (0, 0)),
        ],
        out_specs=[
            pl.BlockSpec((BLOCK_Q, N_HEADS_B * HEAD_DIM), lambda n: (n, 0)),
            whole,
            whole,
            pl.BlockSpec((N_HEADS_B, N_BUCKETS, HEAD_DIM), lambda n: (0, 0, 0)),
            pl.BlockSpec((N_HEADS_B, 1, HEAD_DIM), lambda n: (0, 0, 0)),
        ],
        out_shape=[
            jax.ShapeDtypeStruct((s, N_HEADS_B * HEAD_DIM), F32),
            jax.ShapeDtypeStruct((sp, N_KV_B * HEAD_DIM), F32),
            jax.ShapeDtypeStruct((sp, N_KV_B * HEAD_DIM), F32),
            jax.ShapeDtypeStruct((N_HEADS_B, N_BUCKETS, HEAD_DIM), F32),
            jax.ShapeDtypeStruct((N_HEADS_B, 1, HEAD_DIM), F32),
        ],
        scratch_shapes=[pltpu.VMEM((N_KV_B * GQ, 3 * BLOCK_Q), F32), pltpu.VMEM((N_KV_B * GQ, 3 * BLOCK_Q), F32)],
        sem=("arbitrary",),
        args=(table, sink, pb, pb, kpad, vpad, att, datt, lse, bucket), comm=comm, after=after)
    return res if comm is None else (res, c_res)


_MESH = pl.DeviceIdType.MESH


def _other_chips(x, y):
    return [(x, 1 - y), (1 - x, y), (1 - x, 1 - y)]


_HBM = pl.BlockSpec(memory_space=pltpu.HBM)
_SEM = pl.BlockSpec(memory_space=pltpu.SEMAPHORE)
_SPLIT = pltpu.CompilerParams(has_side_effects=pltpu.SideEffectType.DATAFLOW_SIDE_EFFECTING)


def _in_hbm(a):
    return pltpu.with_memory_space_constraint(a, pltpu.HBM)


def _my_half(rows):
    c = lax.axis_index("c")
    half = rows // 2
    return pl.ds(pl.multiple_of(c * half, half), half), pl.ds(pl.multiple_of((1 - c) * half, half), half)


def _gather_route(shapes):
    def route(src, land):
        x, y, c = lax.axis_index("x"), lax.axis_index("y"), lax.axis_index("c")
        out = []
        for t, shape in enumerate(shapes):
            mine, _ = _my_half(shape[0])
            for px, py in _other_chips(x, y):
                out.append((src[t].at[mine], land[t].at[2 * x + y, mine], land[t].at[2 * px + py, mine], (px, py, c)))
        return out

    return route


def _exchange_route(n_t):
    def route(src, land):
        x, y, c = lax.axis_index("x"), lax.axis_index("y"), lax.axis_index("c")
        out = []
        for t in range(n_t):
            for px, py in _other_chips(x, y):
                k = 2 * px + py
                out.append((src[t].at[k], land[t].at[2 * (2 * x + y) + c], land[t].at[2 * k + c], (px, py, c)))
        return out

    return route


def _own_slot(shape, dtype, slot, block):
    return lax.dynamic_update_slice(lax.empty(shape, dtype), block[None], (slot,) + (0,) * (len(shape) - 1))


def _split_start(name, srcs, lands, route, after):
    n = len(srcs)

    def body(*refs):
        src, land, send_sems, recv_sems, token = refs[:n], refs[n:2 * n], refs[2 * n + 1], refs[2 * n + 2], refs[-1]
        for i, (src_ref, dst_ref, _, to) in enumerate(route(src, land)):
            pltpu.make_async_remote_copy(src_ref=src_ref, dst_ref=dst_ref, send_sem=send_sems.at[i],
                                         recv_sem=recv_sems.at[i], device_id=to, device_id_type=_MESH).start()
        token[...] = jnp.zeros_like(token)

    sem = pltpu.SemaphoreType.DMA((3 * n,))
    lands = list(lands)
    res = pl.pallas_call(
        body, name=name,
        in_specs=[_HBM] * (2 * n) + [_ANY],
        out_specs=[_SEM, _SEM] + [_HBM] * (2 * n) + [pl.BlockSpec(memory_space=pltpu.VMEM)],
        out_shape=[sem, sem] + [pltpu.HBM(a.shape, a.dtype) for a in list(srcs) + lands]
        + [jax.ShapeDtypeStruct((8, 128), F32)],
        input_output_aliases={i: 2 + i for i in range(2 * n)},
        compiler_params=_SPLIT,
    )(*[_in_hbm(a) for a in srcs], *[_in_hbm(a) for a in lands], after)
    return (res[0], res[1]), res[2:2 + n], res[2 + n:2 + 2 * n], res[-1]


def _split_wait(name, srcs, lands, sems, route, after):
    n = len(srcs)

    def body(*refs):
        src, land, send_sems, recv_sems = refs[:n], refs[n:2 * n], refs[2 * n], refs[2 * n + 1]
        for i, (src_ref, _, dst_ref, to) in enumerate(route(src, land)):
            cp = pltpu.make_async_remote_copy(src_ref=src_ref, dst_ref=dst_ref, send_sem=send_sems.at[i],
                                              recv_sem=recv_sems.at[i], device_id=to, device_id_type=_MESH)
            cp.wait_send()
            cp.wait_recv()

    res = pl.pallas_call(
        body, name=name,
        in_specs=[_HBM] * (2 * n) + [_SEM, _SEM, _ANY],
        out_specs=[_HBM] * (2 * n),
        out_shape=[pltpu.HBM(a.shape, a.dtype) for a in list(srcs) + list(lands)],
        input_output_aliases={i: i for i in range(2 * n)},
        compiler_params=_SPLIT,
    )(*srcs, *lands, sems[0], sems[1], after)
    return res[:n], res[n:]


def _comm_only(name, comm):
    return _call(lambda: None, name=name, grid=(1,), in_specs=[], out_specs=[], out_shape=[], args=(), comm=comm)[1]


def _swap_comm(shards, lands):
    n_t = len(lands)

    def copies(land, sems, later):
        send_sems, recv_sems = sems
        x, y = lax.axis_index("x"), lax.axis_index("y")
        sends, recvs = [], []
        for t in range(n_t):
            mine, other = _my_half(shards[t].shape[0])
            for j, (px, py) in enumerate(_other_chips(x, y)):
                k = 2 * px + py
                for part, out in ((mine, sends), (other, recvs)) if later else ((mine, sends),):
                    out.append(pltpu.make_async_remote_copy(
                        src_ref=land[t].at[k, part], dst_ref=land[t].at[k, part], send_sem=send_sems.at[3 * t + j],
                        recv_sem=recv_sems.at[3 * t + j], device_id=_sibling(), device_id_type=_MESH))
        return sends, recvs

    def start(ins, land, sems):
        for cp in copies(land, sems, False)[0]:
            cp.start()

    def finish(ins, land, sems):
        sends, recvs = copies(land, sems, True)
        for cp in recvs:
            cp.wait_recv()
        for cp in sends:
            cp.wait_send()

    return _Comm(
        lands, [jax.ShapeDtypeStruct(a.shape, a.dtype) for a in lands],
        [pltpu.SemaphoreType.DMA((3 * n_t,)), pltpu.SemaphoreType.DMA((3 * n_t,))],
        start, finish, aliases={t: t for t in range(n_t)})


def _forward_comm(partials, lands):
    n_t = len(lands)

    def copies(land, sems, later):
        send_sems, recv_sems = sems
        x, y, c = lax.axis_index("x"), lax.axis_index("y"), lax.axis_index("c")
        sends, recvs = [], []
        for t in range(n_t):
            for j, k in enumerate([2 * x + y] + [2 * px + py for px, py in _other_chips(x, y)]):
                for slot, out in ((2 * k + c, sends), (2 * k + 1 - c, recvs)) if later else ((2 * k + c, sends),):
                    out.append(pltpu.make_async_remote_copy(
                        src_ref=land[t].at[slot], dst_ref=land[t].at[slot], send_sem=send_sems.at[4 * t + j],
                        recv_sem=recv_sems.at[4 * t + j], device_id=_sibling(), device_id_type=_MESH))
        return sends, recvs

    def start(ins, land, sems):
        for cp in copies(land, sems, False)[0]:
            cp.start()

    def finish(ins, land, sems):
        sends, recvs = copies(land, sems, True)
        for cp in recvs:
            cp.wait_recv()
        for cp in sends:
            cp.wait_send()

    return _Comm(
        lands, [jax.ShapeDtypeStruct(a.shape, a.dtype) for a in lands],
        [pltpu.SemaphoreType.DMA((4 * n_t,)), pltpu.SemaphoreType.DMA((4 * n_t,))],
        start, finish, aliases={t: t for t in range(n_t)})


def _allreduce_small(pack):
    rows, d = pack.shape

    def body(p_ref, sum_ref, all_ref, send_sems, recv_sems):
        x, y, c = lax.axis_index("x"), lax.axis_index("y"), lax.axis_index("c")
        me = 4 * x + 2 * y + c
        all_ref[me] = p_ref[...]
        peers = []
        for dx in range(2):
            for dy in range(2):
                for dc in range(2):
                    if dx or dy or dc:
                        px = 1 - x if dx else x
                        py = 1 - y if dy else y
                        pc = 1 - c if dc else c
                        peers.append((4 * dx + 2 * dy + dc - 1, (px, py, pc)))
        sends = []
        for k, to in peers:
            cp = pltpu.make_async_remote_copy(
                src_ref=p_ref, dst_ref=all_ref.at[me], send_sem=send_sems.at[k], recv_sem=recv_sems.at[k],
                device_id=to, device_id_type=_MESH)
            cp.start()
            sends.append(cp)
        for k, (px, py, pc) in peers:
            pltpu.make_async_remote_copy(
                src_ref=p_ref, dst_ref=all_ref.at[4 * px + 2 * py + pc], send_sem=send_sems.at[k],
                recv_sem=recv_sems.at[k], device_id=(px, py, pc), device_id_type=_MESH).wait_recv()
        for cp in sends:
            cp.wait_send()
        tot = all_ref[0]
        for i in range(1, N_DEV):
            tot = tot + all_ref[i]
        sum_ref[...] = tot

    vm = pl.BlockSpec(memory_space=pltpu.VMEM)
    return pl.pallas_call(
        body,
        name="allreduce_small",
        in_specs=[vm],
        out_specs=vm,
        out_shape=jax.ShapeDtypeStruct((rows, d), F32),
        scratch_shapes=[
            pltpu.VMEM((N_DEV, rows, d), F32),
            pltpu.SemaphoreType.DMA((N_DEV - 1,)),
            pltpu.SemaphoreType.DMA((N_DEV - 1,)),
        ],
    )(pack)


def _adamw_math(w, g, m, v):
    m = ADAM_B1 * m + (1.0 - ADAM_B1) * g
    v = ADAM_B2 * v + (1.0 - ADAM_B2) * (g * g)
    m_hat = m / (1.0 - ADAM_B1 ** ADAM_STEP)
    v_hat = v / (1.0 - ADAM_B2 ** ADAM_STEP)
    delta = -ADAM_LR * (m_hat / (jnp.sqrt(v_hat) + ADAM_EPS) + ADAM_WD * w)
    return delta, m, v


def _sum_adamw(parts, w, m, v, *, name, tr=256):
    r, c = w.shape
    tr = min(tr, r)
    tc = min(c, 1024)

    def body(p_ref, w_ref, m_ref, v_ref, g_ref, d_ref, m2_ref, v2_ref):
        g = p_ref[0].astype(F32)
        for i in range(1, N_DEV):
            g = g + p_ref[i].astype(F32)
        delta, m2, v2 = _adamw_math(w_ref[...], g, m_ref[...], v_ref[...])
        g_ref[...] = g
        d_ref[...] = delta
        m2_ref[...] = m2
        v2_ref[...] = v2

    blk = pl.BlockSpec((tr, tc), lambda i, j: (i, j))
    return pl.pallas_call(
        body,
        name=name,
        grid=(r // tr, c // tc),
        in_specs=[pl.BlockSpec((N_DEV, tr, tc), lambda i, j: (0, i, j)), blk, blk, blk],
        out_specs=[blk] * 4,
        out_shape=[jax.ShapeDtypeStruct((r, c), F32)] * 4,
        compiler_params=_params(("parallel", "parallel")),
    )(parts, w, m, v)


def _adamw_small(g, w, m, v):
    def body(g_ref, w_ref, m_ref, v_ref, d_ref, m2_ref, v2_ref):
        delta, m2, v2 = _adamw_math(w_ref[...], g_ref[...], m_ref[...], v_ref[...])
        d_ref[...] = delta
        m2_ref[...] = m2
        v2_ref[...] = v2

    vm = pl.BlockSpec(memory_space=pltpu.VMEM)
    return pl.pallas_call(
        body,
        name="adamw_small",
        in_specs=[vm] * 4,
        out_specs=[vm] * 3,
        out_shape=[jax.ShapeDtypeStruct(g.shape, F32)] * 3,
    )(g, w, m, v)


def _relu2_epilogue(acc):
    ra = jnp.maximum(acc, 0.0)
    return ra * ra, ra


def _residual_norm_epilogue(acc, res, g):
    h = acc + res
    return h, h * lax.rsqrt(jnp.mean(h * h, axis=-1, keepdims=True) + EPS) * g


def _rows(stacked):
    return stacked.reshape(stacked.shape[0] * stacked.shape[1], stacked.shape[2])


def _by_chip(mat):
    return mat.reshape(N_CHIPS, mat.shape[0] // N_CHIPS, mat.shape[1])


def _local_step(x, p, target, shards, small, update):
    s, d = x.shape
    cos_t, sin_t = _rope_tables(s)
    bucket = _band_buckets()
    p_bf = p.astype(BF16)
    wts = {}

    chip = 2 * lax.axis_index("x") + lax.axis_index("y")
    core = lax.axis_index("c")

    def gather(tag, names, after):
        srcs = [cast[n] for n in names]
        route = _gather_route([a.shape for a in srcs])
        sems, srcs, lands, token = _split_start(f"gather_start_{tag}", srcs, [zones[n] for n in names], route, after)

        def landed(done):
            got_srcs, got_lands = _split_wait(f"gather_wait_{tag}", srcs, lands, sems, route, done)
            comm = _swap_comm(got_srcs, got_lands)
            comm.waited = got_srcs[0]
            return comm

        return landed, token

    def prepare(n, zero):
        cast[n] = (shards[n] + zero).astype(BF16)
        zones[n] = _own_slot((N_CHIPS,) + cast[n].shape, BF16, chip, cast[n])

    cast, zones = {}, {}
    prepare("w_in", 0.0)
    in_landed, token = gather("in", ["w_in"], small["attn_norm_g"])
    for n in shards:
        if n != "w_in":
            prepare(n, token[:1, :1])
    g_attn = small["attn_norm_g"] + token[:1, :1]
    u = _rms_fwd(x, g_attn, name="norm_attn")
    prepared = u[:1, :1].astype(F32) + sum(
        (lax.dynamic_slice(zones[n], (chip, 0, 0), (1, 1, 1))[0] + cast[n][:1, :1]).astype(F32)
        for n in zones if n != "w_in")
    (wts["w_in"],) = _comm_only("swap_w_in", in_landed(prepared))
    mid_landed, token = gather("mid", ["w_out"], wts["w_in"])
    proj = _matmul(u, wts["w_in"], mode="nn", out_dtypes=[F32], name="mm_in", bn=768, after=token)
    pb, (w_out_s,) = _qk_prep(proj, small["q_norm_g"], small["k_norm_g"], cos_t, sin_t, comm=mid_landed(proj))
    wts["w_out"] = _rows(w_out_s)
    up_landed, token = gather("up", ["w_up"], pb)
    att_a, lse_a = _attn_a_fwd(pb, after=token)
    pad = ((PAD_LO, PAD_HI), (0, 0))
    kpad = jnp.pad(pb[:, COL_KB * HEAD_DIM:COL_VB * HEAD_DIM], pad)
    vpad = jnp.pad(pb[:, COL_VB * HEAD_DIM:], pad)
    up_swap = up_landed(att_a)
    down_landed, token = gather("down", ["w_down"], up_swap.waited)
    (att, lse_b), (wts["w_up"],) = _attn_b_fwd(pb, kpad, vpad, bucket, small["rel_bias_table"],
                                               small["sink_logits"], att_a, comm=up_swap, after=token)
    h1, mn = _matmul(att, wts["w_out"], mode="nn", out_dtypes=[F32, BF16], name="mm_out", bm=512, bn=d,
                     epilogue=_residual_norm_epilogue, extras=(x,), vecs=(small["mlp_norm_g"],))
    r, ra = _matmul(mn, wts["w_up"], mode="nn", out_dtypes=[BF16, BF16], name="mm_up", epilogue=_relu2_epilogue,
                    bm=2048)
    (w_down_s,) = _comm_only("swap_w_down", down_landed(r))
    wts["w_down"] = _rows(w_down_s)
    late_landed, token = gather("late", ["w_gate", "ple_w"], w_down_s)
    h2 = _matmul(r, wts["w_down"], mode="nn", out_dtypes=[F32], name="mm_down",
                 epilogue=lambda acc, res: (acc + res,), extras=(h1,), after=token)
    ng, (w_gate_s, wts["ple_w"]) = _rms_fwd(h2, small["gate_norm_g"], name="norm_gate", comm=late_landed(h2))
    wts["w_gate"] = _rows(w_gate_s)
    gate = _matmul(ng, wts["w_gate"], mode="nn", out_dtypes=[F32], name="mm_gate",
                   epilogue=lambda acc: (1.0 / (1.0 + jnp.exp(-acc)),))
    pp = _matmul(p_bf, wts["ple_w"], mode="nn", out_dtypes=[F32], name="mm_ple", bn=512)
    dh3, dz, dpp, dg_final, dg_ple, loss = _tail(h2, gate, pp, target, small["ple_norm_g"], small["final_norm_g"])

    dng = _matmul(dz, wts["w_gate"], mode="nt", out_dtypes=[F32], name="mm_gate_dx")
    gw_gate = _matmul(ng, dz, mode="tn", out_dtypes=[BF16], name="mm_gate_dw")
    gw_ple = _matmul(p_bf, dpp, mode="tn", out_dtypes=[BF16], name="mm_ple_dw", bn=512, out_stack=N_CHIPS)
    dh2, dh2_bf, dg_gate = _rms_bwd(h2, dng, small["gate_norm_g"], dh3, name="norm_gate_bwd", want_bf16=True)

    def exchange(tag, partials, after):
        route = _exchange_route(len(partials))
        lands = [_own_slot((N_DEV,) + g.shape[1:], g.dtype, 2 * chip + core,
                           lax.dynamic_index_in_dim(g, chip, 0, keepdims=False)) for g in partials]
        sems, srcs, lands, token = _split_start(f"exchange_start_{tag}", partials, lands, route, after)

        def landed(done):
            got_srcs, got_lands = _split_wait(f"exchange_wait_{tag}", srcs, lands, sems, route, done)
            comm = _forward_comm(got_srcs, got_lands)
            comm.waited = got_srcs[0]
            return comm

        return landed, token

    big = {}
    gate_landed, token = exchange("gate", [_by_chip(gw_gate), gw_ple], dh2_bf)
    gw_down = _matmul(r, dh2_bf, mode="tn", out_dtypes=[BF16], name="mm_down_dw", after=token, bm=512, bk=4096)
    da, (parts_gate, parts_ple) = _matmul(
        dh2_bf, wts["w_down"], mode="nt", out_dtypes=[BF16], name="mm_down_dx", bm=2048,
        epilogue=lambda acc, ra_v: (acc * (2.0 * ra_v.astype(F32)),), extras=(ra,), comm=gate_landed(gw_down))
    down_landed, token = exchange("down", [_by_chip(gw_down)], da)
    big["w_gate"], big["ple_w"] = update("w_gate", parts_gate), update("ple_w", parts_ple)
    gw_up = _matmul(mn, da, mode="tn", out_dtypes=[BF16], name="mm_up_dw", out_stack=N_CHIPS, after=token,
                    bm=512, bk=4096)
    dmn = _matmul(da, wts["w_up"], mode="nt", out_dtypes=[F32], name="mm_up_dx")
    dh1, dh1_bf, dg_mlp = _rms_bwd(h1, dmn, small["mlp_norm_g"], dh2, name="norm_mlp_bwd", want_bf16=True)
    datt, (parts_down,) = _matmul(dh1_bf, wts["w_out"], mode="nt", out_dtypes=[BF16], name="mm_out_dx",
                                  comm=down_landed(dh1_bf))
    gw_out = _matmul(att, dh1_bf, mode="tn", out_dtypes=[BF16], name="mm_out_dw")
    up_landed, token = exchange("up", [gw_up], datt)
    dqb, dkpad, dvpad, dtab, dsink = _attn_b_bwd(pb, kpad, vpad, att, datt, lse_b, bucket,
                                                 small["rel_bias_table"], small["sink_logits"], after=token)
    dqa, dka, dva = _attn_a_bwd(pb, att, datt, lse_a)
    up_forward = up_landed(dqa)
    out_landed, token = exchange("out", [_by_chip(gw_out)], up_forward.waited)
    (dproj, dg_q, dg_k), (parts_up,) = _qk_bwd(dqa, dka, dva, dqb, dkpad, dvpad, proj,
                                               small["q_norm_g"], small["k_norm_g"], cos_t, sin_t,
                                               comm=up_forward, after=token)
    gw_in = _matmul(u, dproj, mode="tn", out_dtypes=[BF16], name="mm_in_dw", bn=768, out_stack=N_CHIPS)
    out_forward = out_landed(gw_in)
    in_landed, token = exchange("in", [gw_in], out_forward.waited)
    du, (parts_out,) = _matmul(dproj, wts["w_in"], mode="nt", out_dtypes=[F32], name="mm_in_dx", bk=3072,
                               comm=out_forward, after=token)
    grad_x, dg_attn = _rms_bwd(x, du, small["attn_norm_g"], dh1, name="norm_attn_bwd", want_bf16=False)
    for n, parts in (("w_down", parts_down), ("w_up", parts_up), ("w_out", parts_out)):
        big[n] = update(n, parts)
    done = dg_attn + sum(big[n][0][0, :1, :] for n in ("w_down", "w_up", "w_out"))
    (parts_in,) = _comm_only("forward_w_in", in_landed(done))
    big["w_in"] = update("w_in", parts_in)

    small_g = {
        "attn_norm_g": dg_attn, "mlp_norm_g": dg_mlp, "ple_norm_g": dg_ple, "gate_norm_g": dg_gate,
        "final_norm_g": dg_final, "q_norm_g": dg_q, "k_norm_g": dg_k,
        "sink_logits": dsink[:, 0, 0][None, :], "rel_bias_table": dtab[:, :, 0].T,
    }
    return loss, grad_x, big, small_g


_SMALL_ROWS = ["attn_norm_g", "mlp_norm_g", "ple_norm_g", "gate_norm_g", "final_norm_g"]
_PACK_ROWS = 8


def _pack_small(vals, d):
    rows = [vals[n].reshape(1, d) for n in _SMALL_ROWS]
    misc = jnp.concatenate([
        vals["q_norm_g"].reshape(1, HEAD_DIM), vals["k_norm_g"].reshape(1, HEAD_DIM),
        jnp.pad(vals["sink_logits"].reshape(1, N_HEADS_B), ((0, 0), (0, HEAD_DIM - N_HEADS_B))),
        vals["rel_bias_table"].reshape(1, N_BUCKETS * N_HEADS_B)], axis=1)
    rows.append(jnp.pad(misc, ((0, 0), (0, d - misc.shape[1]))))
    rows.append(jnp.zeros((_PACK_ROWS - len(rows), d), F32))
    return jnp.concatenate(rows, axis=0).astype(F32)


def _unpack_small(pack, shapes):
    out = {n: pack[i].reshape(shapes[n]) for i, n in enumerate(_SMALL_ROWS)}
    misc = pack[len(_SMALL_ROWS)]
    out["q_norm_g"] = misc[:HEAD_DIM].reshape(shapes["q_norm_g"])
    out["k_norm_g"] = misc[HEAD_DIM:2 * HEAD_DIM].reshape(shapes["k_norm_g"])
    out["sink_logits"] = misc[2 * HEAD_DIM:2 * HEAD_DIM + N_HEADS_B].reshape(shapes["sink_logits"])
    out["rel_bias_table"] = misc[3 * HEAD_DIM:3 * HEAD_DIM + N_BUCKETS * N_HEADS_B].reshape(shapes["rel_bias_table"])
    return out


_WEIGHTS = ["attn_norm_g", "w_in", "q_norm_g", "k_norm_g", "sink_logits", "w_out", "mlp_norm_g", "w_up", "w_down",
            "ple_w", "ple_norm_g", "gate_norm_g", "w_gate", "rel_bias_table", "final_norm_g"]
_BIG = ["w_in", "w_out", "w_up", "w_down", "ple_w", "w_gate"]


def kernel(x, p, attn_norm_g, w_in, q_norm_g, k_norm_g, sink_logits, w_out, mlp_norm_g, w_up, w_down, ple_w, ple_norm_g, gate_norm_g, w_gate, rel_bias_table, final_norm_g, loss_target, m_attn_norm_g, m_w_in, m_q_norm_g, m_k_norm_g, m_sink_logits, m_w_out, m_mlp_norm_g, m_w_up, m_w_down, m_ple_w, m_ple_norm_g, m_gate_norm_g, m_w_gate, m_rel_bias_table, m_final_norm_g, v_attn_norm_g, v_w_in, v_q_norm_g, v_k_norm_g, v_sink_logits, v_w_out, v_mlp_norm_g, v_w_up, v_w_down, v_ple_w, v_ple_norm_g, v_gate_norm_g, v_w_gate, v_rel_bias_table, v_final_norm_g):
    given = dict(locals())
    w = {n: given[n] for n in _WEIGHTS}
    m = {n: given["m_" + n] for n in _WEIGHTS}
    v = {n: given["v_" + n] for n in _WEIGHTS}
    d = x.shape[-1]

    shards = {n: w[n][0] for n in _BIG}
    small = {
        "attn_norm_g": w["attn_norm_g"], "mlp_norm_g": w["mlp_norm_g"], "ple_norm_g": w["ple_norm_g"],
        "gate_norm_g": w["gate_norm_g"], "final_norm_g": w["final_norm_g"].reshape(1, d),
        "q_norm_g": w["q_norm_g"], "k_norm_g": w["k_norm_g"], "sink_logits": w["sink_logits"],
        "rel_bias_table": w["rel_bias_table"],
    }

    def update(n, parts):
        res = _sum_adamw(parts, w[n][0], m[n][0], v[n][0], name="adamw_" + n)
        return [t.reshape(w[n].shape) for t in res]

    loss_part, grad_x, big, small_g = _local_step(x[0], p[0, 0], loss_target[0], shards, small, update)
    grads, deltas, new_m, new_v = [{n: big[n][i] for n in _BIG} for i in range(4)]

    shapes = {n: w[n].shape for n in _WEIGHTS if n not in _BIG}
    pack = _pack_small(small_g, d)
    pack = pack.at[_PACK_ROWS - 1, :1].add(0.0 * grads["w_in"][0, 0, :1])
    pack = pack.at[_PACK_ROWS - 1, 1].set(loss_part[0, 0])
    g_small = _allreduce_small(pack)
    loss = g_small[_PACK_ROWS - 1, 1]
    d_small, m_small, v_small = _adamw_small(g_small, _pack_small(w, d), _pack_small(m, d), _pack_small(v, d))
    grads.update(_unpack_small(g_small, shapes))
    deltas.update(_unpack_small(d_small, shapes))
    new_m.update(_unpack_small(m_small, shapes))
    new_v.update(_unpack_small(v_small, shapes))

    return (loss, grad_x[None], *[grads[n] for n in _WEIGHTS], *[deltas[n] for n in _WEIGHTS],
            *[new_m[n] for n in _WEIGHTS], *[new_v[n] for n in _WEIGHTS])
```

```python
import functools
import math

import jax
import jax.numpy as jnp
import numpy as np
from jax import lax
from jax.experimental import pallas as pl
from jax.experimental.pallas import tpu as pltpu

F32 = jnp.float32
BF16 = jnp.bfloat16

HEAD_DIM = 128
N_HEADS_A = 8
N_KV_A = 2
N_HEADS_B = 8
N_KV_B = 2
GROUP = 4
GRID_W = 64
BLOCK_Q = 128
WINDOW = 128
N_BUCKETS = 32
MAX_DISTANCE = 128
ROPE_THETA = 10000.0
EPS = 1e-6
NEG_INF = -1e30
ATT_SCALE = HEAD_DIM ** -0.5
LOG2E = math.log2(math.e)
LN2 = math.log(2.0)
Q_SCALE = ATT_SCALE * LOG2E
PAD_LO, PAD_HI = 256, 128
P_SLOTS = 4

ADAM_LR = 0.001
ADAM_B1 = 0.9
ADAM_B2 = 0.999
ADAM_EPS = 1e-08
ADAM_WD = 0.01
ADAM_STEP = 10

N_CHIPS = 4
N_DEV = 8
COL_QA, COL_KA, COL_VA, COL_QB, COL_KB, COL_VB = 0, 8, 10, 12, 20, 22
N_COLS = 24

VMEM_LIMIT = 52 * 1024 * 1024


def _params(sem=None, collective_id=None):
    return pltpu.CompilerParams(dimension_semantics=sem, vmem_limit_bytes=VMEM_LIMIT, collective_id=collective_id)


_ANY = pl.BlockSpec(memory_space=pl.ANY)
_MESH = pl.DeviceIdType.MESH
SIBLING_BARRIER_ID = 1


def _sibling():
    return (lax.axis_index("x"), lax.axis_index("y"), 1 - lax.axis_index("c"))


class _Comm:
    def __init__(self, inputs, out_shapes, sems, start, finish, aliases=None):
        self.inputs, self.out_shapes, self.sems = list(inputs), list(out_shapes), list(sems)
        self.start, self.finish, self.aliases = start, finish, dict(aliases or {})


def _call(body, *, name, grid, in_specs, out_specs, out_shape, args, scratch_shapes=(), sem=None, comm=None,
          after=None, aliases=None):
    in_specs, out_specs, out_shape = list(in_specs), list(out_specs), list(out_shape)
    scratch_shapes = list(scratch_shapes)
    n_in, n_out, n_sc = len(in_specs), len(out_specs), len(scratch_shapes)
    behind = [] if after is None else [after]
    aliases = dict(aliases or {})
    if comm is None:
        res = pl.pallas_call(
            (lambda *refs: body(*refs[:n_in], *refs[n_in + len(behind):])) if behind else body,
            name=name, grid=grid, in_specs=in_specs + [_ANY] * len(behind), out_specs=out_specs,
            out_shape=out_shape, scratch_shapes=scratch_shapes, input_output_aliases=aliases,
            compiler_params=_params(sem))(*args, *behind)
        return list(res), []
    c_in, c_out = len(comm.inputs), len(comm.out_shapes)

    def hosted(*refs):
        pos = [0]

        def take(n):
            pos[0] += n
            return refs[pos[0] - n:pos[0]]

        ins, c_ins, _, outs, c_outs, scr = (take(n_in), take(c_in), take(len(behind)), take(n_out), take(c_out),
                                            take(n_sc))
        c_sems = refs[pos[0]:]
        ids = [pl.program_id(a) for a in range(len(grid))]
        first = functools.reduce(jnp.logical_and, [i == 0 for i in ids])
        last = functools.reduce(jnp.logical_and, [i == g - 1 for i, g in zip(ids, grid)])

        @pl.when(first)
        def _():
            barrier = pltpu.get_barrier_semaphore()
            pl.semaphore_signal(barrier, inc=1, device_id=_sibling(), device_id_type=_MESH)
            pl.semaphore_wait(barrier, 1)
            comm.start(c_ins, c_outs, c_sems)

        body(*ins, *outs, *scr)

        @pl.when(last)
        def _():
            comm.finish(c_ins, c_outs, c_sems)

    res = pl.pallas_call(
        hosted, name=name, grid=grid, in_specs=in_specs + [_ANY] * (c_in + len(behind)),
        out_specs=out_specs + [_ANY] * c_out,
        out_shape=out_shape + comm.out_shapes, scratch_shapes=scratch_shapes + comm.sems,
        input_output_aliases={**aliases, **{n_in + i: n_out + o for i, o in comm.aliases.items()}},
        compiler_params=_params(("arbitrary",) * len(grid), SIBLING_BARRIER_ID))(*args, *comm.inputs, *behind)
    return list(res[:n_out]), list(res[n_out:])


def _matmul(a, b, *, mode, out_dtypes, name, epilogue=None, extras=(), bm=1024, bn=1024, bk=2048,
            out_stack=0, comm=None, after=None, vecs=()):
    stacked = b.ndim == 3
    if mode == "nn":
        m, k = a.shape
        if stacked:
            nj, kb, ns = b.shape
            n, ks = nj * ns, k
        else:
            kb, n = b.shape
            ns, ks = n, k
        dn = (((1,), (0,)), ((), ()))
    elif mode == "nt":
        m, k = a.shape
        if stacked:
            nj, n, ks = b.shape
            kb = nj * ks
        else:
            n, kb = b.shape
            ks = kb
        ns = n
        dn = (((1,), (1,)), ((), ()))
    else:
        k, m = a.shape
        kb, n = b.shape
        ns, ks = n, k
        dn = (((0,), (0,)), ((), ()))
    assert k == kb and not (stacked and mode == "tn")
    ns_out = n // out_stack if out_stack else n
    whole_k = stacked and mode == "nt" and bk >= k
    bm, bn, bk = min(bm, m), min(bn, ns, ns_out), k if whole_k else min(bk, ks)
    assert m % bm == 0 and ns % bn == 0 and ns_out % bn == 0 and ks % bk == 0 or whole_k
    gm, gn, gk = m // bm, n // bn, k // bk

    if mode == "tn":
        a_spec = pl.BlockSpec((bk, bm), lambda i, j, q: (q, i))
    else:
        a_spec = pl.BlockSpec((bm, bk), lambda i, j, q: (i, q))
    if mode == "nt":
        if whole_k:
            b_spec = pl.BlockSpec((nj, bn, ks), lambda i, j, q: (0, j, 0))
        elif stacked:
            per = ks // bk
            b_spec = pl.BlockSpec((None, bn, bk), lambda i, j, q: (q // per, j, q % per))
        else:
            b_spec = pl.BlockSpec((bn, bk), lambda i, j, q: (j, q))
    else:
        if stacked:
            per = ns // bn
            b_spec = pl.BlockSpec((None, bk, bn), lambda i, j, q: (j // per, q, j % per))
        else:
            b_spec = pl.BlockSpec((bk, bn), lambda i, j, q: (q, j))
    ex_spec = pl.BlockSpec((bm, bn), lambda i, j, q: (i, j))
    if out_stack:
        per_o = ns_out // bn
        o_spec = pl.BlockSpec((None, bm, bn), lambda i, j, q: (j // per_o, i, j % per_o))
        o_shape = (out_stack, m, ns_out)
    else:
        o_spec = ex_spec
        o_shape = (m, n)
    n_ex, n_out = len(extras) + len(vecs), len(out_dtypes)

    def body(a_ref, b_ref, *rest):
        ex, outs = rest[:n_ex], rest[n_ex:n_ex + n_out]
        if whole_k:
            part = sum(lax.dot_general(a_ref[:, t * ks:(t + 1) * ks], b_ref[t], dn, preferred_element_type=F32)
                       for t in range(nj))
        else:
            part = lax.dot_general(a_ref[...], b_ref[...], dn, preferred_element_type=F32)

        def finish(acc):
            res = epilogue(acc, *[e[...] for e in ex]) if epilogue else (acc,)
            for o, r in zip(outs, res):
                o[...] = r.astype(o.dtype)

        if gk == 1:
            finish(part)
        else:
            acc_ref = rest[-1]
            q = pl.program_id(2)

            @pl.when(q == 0)
            def _():
                acc_ref[...] = part

            @pl.when(q > 0)
            def _():
                acc_ref[...] += part

            @pl.when(q == gk - 1)
            def _():
                finish(acc_ref[...])

    res, c_res = _call(
        body, name=name, grid=(gm, gn, gk),
        in_specs=[a_spec, b_spec] + [ex_spec] * len(extras)
        + [pl.BlockSpec((1, bn), lambda i, j, q: (0, j))] * len(vecs),
        out_specs=[o_spec] * n_out,
        out_shape=[jax.ShapeDtypeStruct(o_shape, dt) for dt in out_dtypes],
        scratch_shapes=[pltpu.VMEM((bm, bn), F32)] if gk > 1 else [],
        sem=("parallel", "parallel", "arbitrary"), args=(a, b, *extras, *vecs), comm=comm, after=after)
    res = res[0] if n_out == 1 else res
    return res if comm is None else (res, c_res)


def _rms_fwd(x, g, *, name, tm=256, comm=None):
    s, d = x.shape
    tm = min(tm, s)

    def body(x_ref, g_ref, o_ref):
        xf = x_ref[...]
        r = lax.rsqrt(jnp.mean(xf * xf, axis=-1, keepdims=True) + EPS)
        o_ref[...] = (xf * r * g_ref[...]).astype(o_ref.dtype)

    res, c_res = _call(
        body, name=name, grid=(s // tm,),
        in_specs=[pl.BlockSpec((tm, d), lambda i: (i, 0)), pl.BlockSpec((1, d), lambda i: (0, 0))],
        out_specs=[pl.BlockSpec((tm, d), lambda i: (i, 0))],
        out_shape=[jax.ShapeDtypeStruct((s, d), BF16)],
        sem=("parallel",), args=(x, g), comm=comm)
    return res[0] if comm is None else (res[0], c_res)


def _rms_bwd(x, dy, g, add, *, name, want_bf16, tm=256):
    s, d = x.shape
    tm = min(tm, s)

    def body(x_ref, dy_ref, g_ref, add_ref, dx_ref, *rest):
        dg_ref = rest[-1]
        i = pl.program_id(0)
        xf = x_ref[...]
        dyf = dy_ref[...].astype(F32)
        r = lax.rsqrt(jnp.mean(xf * xf, axis=-1, keepdims=True) + EPS)
        xh = xf * r
        dyg = dyf * g_ref[...]
        dx = r * (dyg - xh * jnp.mean(dyg * xh, axis=-1, keepdims=True))
        tot = add_ref[...] + dx
        dx_ref[...] = tot
        if want_bf16:
            rest[0][...] = tot.astype(BF16)
        part = jnp.sum(dyf * xh, axis=0, keepdims=True)

        @pl.when(i == 0)
        def _():
            dg_ref[...] = part

        @pl.when(i > 0)
        def _():
            dg_ref[...] += part

    row = pl.BlockSpec((tm, d), lambda i: (i, 0))
    vec = pl.BlockSpec((1, d), lambda i: (0, 0))
    out_specs = [row] + ([row] if want_bf16 else []) + [vec]
    out_shape = [jax.ShapeDtypeStruct((s, d), F32)]
    if want_bf16:
        out_shape.append(jax.ShapeDtypeStruct((s, d), BF16))
    out_shape.append(jax.ShapeDtypeStruct((1, d), F32))
    return pl.pallas_call(
        body,
        name=name,
        grid=(s // tm,),
        in_specs=[row, row, vec, row],
        out_specs=out_specs,
        out_shape=out_shape,
        compiler_params=_params(("arbitrary",)),
    )(x, dy, g, add)


def _tail(h2, gate, pp, target, g_ple, g_final, *, tm=128):
    s, d = h2.shape
    tm = min(tm, s)

    def body(h2_ref, gate_ref, pp_ref, t_ref, gp_ref, gf_ref, dh3_ref, dz_ref, dpp_ref, dgf_ref, dgp_ref, loss_ref):
        i = pl.program_id(0)
        ppf = pp_ref[...]
        gate_v = gate_ref[...]
        r_p = lax.rsqrt(jnp.mean(ppf * ppf, axis=-1, keepdims=True) + EPS)
        eh = ppf * r_p
        e = eh * gp_ref[...]
        h3 = h2_ref[...] + gate_v * e
        r_f = lax.rsqrt(jnp.mean(h3 * h3, axis=-1, keepdims=True) + EPS)
        yh = h3 * r_f
        diff = yh * gf_ref[...] - t_ref[...]
        loss_part = 0.5 * jnp.sum(jnp.mean(diff * diff, axis=-1, keepdims=True), axis=0, keepdims=True)
        dy = diff / d
        dgf = jnp.sum(dy * yh, axis=0, keepdims=True)
        dyg = dy * gf_ref[...]
        dh3 = r_f * (dyg - yh * jnp.mean(dyg * yh, axis=-1, keepdims=True))
        dh3_ref[...] = dh3
        de = dh3 * gate_v
        dz_ref[...] = (dh3 * e * gate_v * (1.0 - gate_v)).astype(BF16)
        dgp = jnp.sum(de * eh, axis=0, keepdims=True)
        deg = de * gp_ref[...]
        dpp_ref[...] = (r_p * (deg - eh * jnp.mean(deg * eh, axis=-1, keepdims=True))).astype(BF16)
        loss_row = jnp.broadcast_to(loss_part, (1, 128))

        @pl.when(i == 0)
        def _():
            dgf_ref[...] = dgf
            dgp_ref[...] = dgp
            loss_ref[...] = loss_row

        @pl.when(i > 0)
        def _():
            dgf_ref[...] += dgf
            dgp_ref[...] += dgp
            loss_ref[...] += loss_row

    row = pl.BlockSpec((tm, d), lambda i: (i, 0))
    vec = pl.BlockSpec((1, d), lambda i: (0, 0))
    return pl.pallas_call(
        body,
        name="tail_fwd_bwd",
        grid=(s // tm,),
        in_specs=[row, row, row, row, vec, vec],
        out_specs=[row, row, row, vec, vec, pl.BlockSpec((1, 128), lambda i: (0, 0))],
        out_shape=[
            jax.ShapeDtypeStruct((s, d), F32),
            jax.ShapeDtypeStruct((s, d), BF16),
            jax.ShapeDtypeStruct((s, d), BF16),
            jax.ShapeDtypeStruct((1, d), F32),
            jax.ShapeDtypeStruct((1, d), F32),
            jax.ShapeDtypeStruct((1, 128), F32),
        ],
        compiler_params=_params(("arbitrary",)),
    )(h2, gate, pp, target, g_ple, g_final)


def _rope_tables(s):
    rows = s // GRID_W
    half = HEAD_DIM // 2
    inv_freq = ROPE_THETA ** (-jnp.arange(0, half, 2, dtype=F32) / half)
    ang_r = jnp.arange(rows, dtype=jnp.int32).astype(F32)[:, None] * inv_freq
    ang_c = jnp.arange(GRID_W, dtype=jnp.int32).astype(F32)[:, None] * inv_freq
    cr, sr = (jnp.repeat(t, GRID_W, axis=0) for t in (jnp.cos(ang_r), jnp.sin(ang_r)))
    cc, sc = (jnp.tile(t, (rows, 1)) for t in (jnp.cos(ang_c), jnp.sin(ang_c)))
    cos_t = jnp.concatenate([cr, cr, cc, cc], axis=-1)
    sin_t = jnp.concatenate([-sr, sr, -sc, sc], axis=-1)
    return cos_t, sin_t


def _swap_quarters(x):
    lane = lax.broadcasted_iota(jnp.int32, x.shape, x.ndim - 1)
    up = pltpu.roll(x, HEAD_DIM - 32, x.ndim - 1)
    down = pltpu.roll(x, 32, x.ndim - 1)
    return jnp.where((lane % 64) < 32, up, down)


def _cols(first, count=1):
    return slice(first * HEAD_DIM, (first + count) * HEAD_DIM)


def _qk_prep(proj, g_q, g_k, cos_t, sin_t, *, tm=256, comm=None):
    s, n = proj.shape
    tm = min(tm, s)

    def body(x_ref, gq_ref, gk_ref, c_ref, s_ref, o_ref):
        cos_v, sin_v = c_ref[...], s_ref[...]
        for h in range(COL_VA):
            x = x_ref[:, _cols(h)]
            g = gq_ref[...] if h < COL_KA else gk_ref[...]
            xn = x * lax.rsqrt(jnp.mean(x * x, axis=-1, keepdims=True) + EPS) * g
            xr = xn * cos_v + _swap_quarters(xn) * sin_v
            if h < COL_KA:
                xr = xr * Q_SCALE
            o_ref[:, _cols(h)] = xr.astype(BF16)
        o_ref[:, _cols(COL_VA, 2)] = x_ref[:, _cols(COL_VA, 2)].astype(BF16)
        o_ref[:, _cols(COL_QB, N_HEADS_B)] = (x_ref[:, _cols(COL_QB, N_HEADS_B)] * Q_SCALE).astype(BF16)
        o_ref[:, _cols(COL_KB, 4)] = x_ref[:, _cols(COL_KB, 4)].astype(BF16)

    row = pl.BlockSpec((tm, n), lambda i: (i, 0))
    tab = pl.BlockSpec((tm, HEAD_DIM), lambda i: (i, 0))
    vec = pl.BlockSpec((1, HEAD_DIM), lambda i: (0, 0))
    res, c_res = _call(
        body, name="qk_prep", grid=(s // tm,),
        in_specs=[row, vec, vec, tab, tab],
        out_specs=[row],
        out_shape=[jax.ShapeDtypeStruct((s, n), BF16)],
        sem=("parallel",), args=(proj, g_q, g_k, cos_t, sin_t), comm=comm)
    return res[0] if comm is None else (res[0], c_res)


def _qk_bwd(dqa, dka, dva, dqb, dkpad, dvpad, proj, g_q, g_k, cos_t, sin_t, *, comm=None, after=None):
    s, n = proj.shape
    tm = min(PAD_LO, s)
    assert PAD_LO % tm == 0
    lo = PAD_LO // tm

    def body(dqa_ref, dka_ref, dva_ref, dqb_ref, dkb_ref, dvb_ref, x_ref, gq_ref, gk_ref, c_ref, s_ref,
             o_ref, dgq_ref, dgk_ref):
        i = pl.program_id(0)
        cos_v, sin_v = c_ref[...], s_ref[...]

        def head(d, x, g):
            dn = d * cos_v + _swap_quarters(d * sin_v)
            r = lax.rsqrt(jnp.mean(x * x, axis=-1, keepdims=True) + EPS)
            xh = x * r
            dng = dn * g
            dx = r * (dng - xh * jnp.mean(dng * xh, axis=-1, keepdims=True))
            return dx.astype(BF16), jnp.sum(dn * xh, axis=0, keepdims=True)

        acc_q = jnp.zeros((1, HEAD_DIM), F32)
        acc_k = jnp.zeros((1, HEAD_DIM), F32)
        for h in range(N_HEADS_A):
            o_ref[:, _cols(h)], part = head(dqa_ref[:, _cols(h)] * ATT_SCALE, x_ref[:, _cols(h)], gq_ref[...])
            acc_q = acc_q + part
        for h in range(N_KV_A):
            o_ref[:, _cols(COL_KA + h)], part = head(dka_ref[:, _cols(h)] * LN2, x_ref[:, _cols(COL_KA + h)],
                                                     gk_ref[...])
            acc_k = acc_k + part
        o_ref[:, _cols(COL_VA, 2)] = dva_ref[...].astype(BF16)
        o_ref[:, _cols(COL_QB, N_HEADS_B)] = (dqb_ref[...] * ATT_SCALE).astype(BF16)
        o_ref[:, _cols(COL_KB, 2)] = (dkb_ref[...] * LN2).astype(BF16)
        o_ref[:, _cols(COL_VB, 2)] = dvb_ref[...].astype(BF16)

        @pl.when(i == 0)
        def _():
            dgq_ref[...] = acc_q
            dgk_ref[...] = acc_k

        @pl.when(i > 0)
        def _():
            dgq_ref[...] += acc_q
            dgk_ref[...] += acc_k

    def rows(width, shift=0):
        return pl.BlockSpec((tm, width), lambda i: (i + shift, 0))

    kv_w = N_KV_A * HEAD_DIM
    q_w = N_HEADS_A * HEAD_DIM
    vec = pl.BlockSpec((1, HEAD_DIM), lambda i: (0, 0))
    res, c_res = _call(
        body, name="qk_bwd", grid=(s // tm,),
        in_specs=[rows(q_w), rows(kv_w), rows(kv_w), rows(q_w), rows(kv_w, lo), rows(kv_w, lo), rows(n),
                  vec, vec, rows(HEAD_DIM), rows(HEAD_DIM)],
        out_specs=[rows(n), vec, vec],
        out_shape=[
            jax.ShapeDtypeStruct((s, n), BF16),
            jax.ShapeDtypeStruct((1, HEAD_DIM), F32),
            jax.ShapeDtypeStruct((1, HEAD_DIM), F32),
        ],
        sem=("arbitrary",), args=(dqa, dka, dva, dqb, dkpad, dvpad, proj, g_q, g_k, cos_t, sin_t), comm=comm,
        after=after)
    return res if comm is None else (res, c_res)


_NT = (((1,), (1,)), ((), ()))
_TN = (((0,), (0,)), ((), ()))


def _attn_a_fwd(pb, *, tq=4096, sub=256, comm=None, after=None):
    s = pb.shape[0]
    tq = min(tq, s)
    sub = min(sub, tq)
    n_chain = tq // sub
    n_slot = min(P_SLOTS, n_chain)

    def body(q_ref, k_ref, v_ref, o_ref, linv_ref, p_hbm, p_buf, p_sems):
        h, i = pl.program_id(0), pl.program_id(1)
        k = k_ref[...]
        v = v_ref[...]

        def store(r):
            return pltpu.make_async_copy(p_buf.at[r % n_slot], p_hbm.at[h, pl.ds(i * tq + r * sub, sub), :],
                                         p_sems.at[r % n_slot])

        for r in range(n_chain):
            rows = pl.ds(r * sub, sub)
            sc = lax.dot_general(q_ref[rows, :], k, _NT, preferred_element_type=F32)
            m = jnp.max(sc, axis=-1, keepdims=True)
            p = jnp.exp2(sc - m)
            linv = 1.0 / jnp.sum(p, axis=-1, keepdims=True)
            p16 = p.astype(BF16)
            o = jnp.dot(p16, v, preferred_element_type=F32)
            o_ref[rows, :] = (o * linv).astype(BF16)
            linv_ref[rows, :] = jnp.broadcast_to(linv, (sub, HEAD_DIM))
            if r >= n_slot:
                store(r - n_slot).wait()
            p_buf[r % n_slot] = p16
            store(r).start()
        for r in range(n_chain - n_slot, n_chain):
            store(r).wait()

    res, c_res = _call(
        body, name="attn_a_fwd", grid=(N_HEADS_A, s // tq),
        in_specs=[
            pl.BlockSpec((tq, HEAD_DIM), lambda h, i: (i, COL_QA + h)),
            pl.BlockSpec((s, HEAD_DIM), lambda h, i: (0, COL_KA + h // GROUP)),
            pl.BlockSpec((s, HEAD_DIM), lambda h, i: (0, COL_VA + h // GROUP)),
        ],
        out_specs=[
            pl.BlockSpec((tq, HEAD_DIM), lambda h, i: (i, h)),
            pl.BlockSpec((None, tq, HEAD_DIM), lambda h, i: (h, i, 0)),
            _ANY,
        ],
        out_shape=[
            jax.ShapeDtypeStruct((s, (N_HEADS_A + N_HEADS_B) * HEAD_DIM), BF16),
            jax.ShapeDtypeStruct((N_HEADS_A, s, HEAD_DIM), F32),
            jax.ShapeDtypeStruct((N_HEADS_A, s, s), BF16),
        ],
        scratch_shapes=[pltpu.VMEM((n_slot, sub, s), BF16), pltpu.SemaphoreType.DMA((n_slot,))],
        sem=("arbitrary", "arbitrary"), args=(pb, pb, pb), comm=comm, after=after)
    return res if comm is None else (res, c_res)


def _attn_a_bwd(pb, att, datt, linv_all, p_all, *, tq=2048, sub=256, comm=None):
    s = pb.shape[0]
    tq = min(tq, s)
    sub = min(sub, tq)
    n_chain = tq // sub
    n_slot = min(P_SLOTS, n_chain)

    def body(q_ref, k_ref, v_ref, o_ref, do_ref, linv_ref, p_hbm, dq_ref, dk_ref, dv_ref, p_buf, p_sems):
        first = jnp.logical_and(pl.program_id(1) == 0, pl.program_id(2) == 0)
        h = pl.program_id(0) * GROUP + pl.program_id(1)
        base = pl.program_id(2) * tq
        k = k_ref[...]
        v = v_ref[...]

        def fetch(r):
            return pltpu.make_async_copy(p_hbm.at[h, pl.ds(base + r * sub, sub), :], p_buf.at[r % n_slot],
                                         p_sems.at[r % n_slot])

        for r in range(n_slot):
            fetch(r).start()
        dk = dv = None
        for r in range(n_chain):
            rows = pl.ds(r * sub, sub)
            q = q_ref[rows, :]
            do = do_ref[rows, :]
            linv = linv_ref[rows, :][:, :1]
            dp = lax.dot_general(do, v, _NT, preferred_element_type=F32)
            delta = jnp.sum(do.astype(F32) * o_ref[rows, :].astype(F32), axis=-1, keepdims=True)
            fetch(r).wait()
            p = p_buf[r % n_slot]
            ds = (p.astype(F32) * ((dp - delta) * linv)).astype(BF16)
            dq_ref[rows, :] = jnp.dot(ds, k, preferred_element_type=F32)
            dk_r = lax.dot_general(ds, q, _TN, preferred_element_type=F32)
            dv_r = lax.dot_general(p, (do.astype(F32) * linv).astype(BF16), _TN, preferred_element_type=F32)
            if r + n_slot < n_chain:
                fetch(r + n_slot).start()
            dk = dk_r if dk is None else dk + dk_r
            dv = dv_r if dv is None else dv + dv_r

        @pl.when(first)
        def _():
            dk_ref[...] = dk
            dv_ref[...] = dv

        @pl.when(jnp.logical_not(first))
        def _():
            dk_ref[...] += dk
            dv_ref[...] += dv

    qmap = lambda kv, g, i: (i, kv * GROUP + g)
    res, c_res = _call(
        body, name="attn_a_bwd", grid=(N_KV_A, GROUP, s // tq),
        in_specs=[
            pl.BlockSpec((tq, HEAD_DIM), lambda kv, g, i: (i, COL_QA + kv * GROUP + g)),
            pl.BlockSpec((s, HEAD_DIM), lambda kv, g, i: (0, COL_KA + kv)),
            pl.BlockSpec((s, HEAD_DIM), lambda kv, g, i: (0, COL_VA + kv)),
            pl.BlockSpec((tq, HEAD_DIM), qmap),
            pl.BlockSpec((tq, HEAD_DIM), qmap),
            pl.BlockSpec((None, tq, HEAD_DIM), lambda kv, g, i: (kv * GROUP + g, i, 0)),
            _ANY,
        ],
        out_specs=[
            pl.BlockSpec((tq, HEAD_DIM), qmap),
            pl.BlockSpec((s, HEAD_DIM), lambda kv, g, i: (0, kv)),
            pl.BlockSpec((s, HEAD_DIM), lambda kv, g, i: (0, kv)),
        ],
        out_shape=[
            jax.ShapeDtypeStruct((s, N_HEADS_A * HEAD_DIM), F32),
            jax.ShapeDtypeStruct((s, N_KV_A * HEAD_DIM), F32),
            jax.ShapeDtypeStruct((s, N_KV_A * HEAD_DIM), F32),
        ],
        scratch_shapes=[pltpu.VMEM((n_slot, sub, s), BF16), pltpu.SemaphoreType.DMA((n_slot,))],
        sem=("arbitrary", "arbitrary", "arbitrary"), args=(pb, pb, pb, att, datt, linv_all, p_all), comm=comm)
    return res if comm is None else (res, c_res)


def _t5_bucket(rel):
    nb = N_BUCKETS // 2
    ret = jnp.where(rel > 0, nb, 0)
    n = jnp.abs(rel)
    max_exact = nb // 2
    nf = jnp.maximum(n, 1).astype(F32)
    large = max_exact + (jnp.log(nf / max_exact) / math.log(MAX_DISTANCE / max_exact)
                         * (nb - max_exact)).astype(jnp.int32)
    large = jnp.minimum(large, nb - 1)
    return ret + jnp.where(n < max_exact, n, large)


def _band_buckets():
    r = jnp.arange(BLOCK_Q, dtype=jnp.int32)
    j = jnp.arange(3 * BLOCK_Q, dtype=jnp.int32)
    return _t5_bucket((j[None, :] - BLOCK_Q) - r[:, None])


def _band_bias(bucket, table_ref, h):
    acc = jnp.zeros(bucket.shape, F32)
    for b in range(N_BUCKETS):
        acc = jnp.where(bucket == b, table_ref[b, h], acc)
    return acc


GQ = GROUP * BLOCK_Q


def _stack_heads(x):
    return jnp.concatenate([x[:, _cols(g)] for g in range(GROUP)], axis=0)


def _unstack_heads(x):
    return jnp.concatenate([x[g * BLOCK_Q:(g + 1) * BLOCK_Q] for g in range(GROUP)], axis=1)


def _group_bias(bucket, table_ref, kv):
    return jnp.concatenate([_band_bias(bucket, table_ref, kv * GROUP + g) * LOG2E for g in range(GROUP)], axis=0)


def _group_sink(sink_ref, kv):
    head = lax.broadcasted_iota(jnp.int32, (GQ, 1), 0) // BLOCK_Q
    snk = jnp.zeros((GQ, 1), F32)
    for g in range(GROUP):
        snk = jnp.where(head == g, sink_ref[0, kv * GROUP + g] * LOG2E, snk)
    return snk


def _band_mask(n, s):
    r = lax.broadcasted_iota(jnp.int32, (GQ, 3 * BLOCK_Q), 0) % BLOCK_Q
    j = lax.broadcasted_iota(jnp.int32, (GQ, 3 * BLOCK_Q), 1)
    rel = j - BLOCK_Q - r
    kabs = n * BLOCK_Q + j - BLOCK_Q
    return (jnp.abs(rel) <= WINDOW) & (kabs >= 0) & (kabs < s)


def _band_start(n):
    return pl.multiple_of(n * BLOCK_Q + (PAD_LO - BLOCK_Q), BLOCK_Q)


def _attn_b_fwd(pb, kpad, vpad, bucket, table, sink, att, *, comm=None, after=None):
    s = pb.shape[0]
    nblk = s // BLOCK_Q
    sp = kpad.shape[0]

    def body(table_ref, sink_ref, q0_ref, q1_ref, k_ref, v_ref, bucket_ref, _, o_ref, lse_ref, bias_ref):
        n = pl.program_id(0)

        @pl.when(n == 0)
        def _():
            for kv in range(N_KV_B):
                bias_ref[kv * GQ:(kv + 1) * GQ, :] = _group_bias(bucket_ref[...], table_ref, kv)

        band = pl.ds(_band_start(n), 3 * BLOCK_Q)
        mask = _band_mask(n, s)
        for kv, q_ref in enumerate((q0_ref, q1_ref)):
            kb = k_ref[band, _cols(kv)]
            vb = v_ref[band, _cols(kv)]
            sc = lax.dot_general(_stack_heads(q_ref[...]), kb, _NT, preferred_element_type=F32)
            sc = jnp.where(mask, sc + bias_ref[kv * GQ:(kv + 1) * GQ, :], NEG_INF)
            snk = _group_sink(sink_ref, kv)
            m = jnp.maximum(jnp.max(sc, axis=-1, keepdims=True), snk)
            p = jnp.exp2(sc - m)
            l = jnp.sum(p, axis=-1, keepdims=True) + jnp.exp2(snk - m)
            o = jnp.dot(p.astype(BF16), vb, preferred_element_type=F32)
            o_ref[:, _cols(kv * GROUP, GROUP)] = _unstack_heads((o / l).astype(BF16))
            lse = m + jnp.log2(l)
            for g in range(GROUP):
                lse_ref[kv * GROUP + g] = jnp.broadcast_to(lse[g * BLOCK_Q:(g + 1) * BLOCK_Q], (BLOCK_Q, HEAD_DIM))

    smem = pl.BlockSpec(memory_space=pltpu.SMEM)
    wide = GROUP * HEAD_DIM
    whole = pl.BlockSpec((sp, N_KV_B * HEAD_DIM), lambda n: (0, 0))
    res, c_res = _call(
        body, name="attn_b_fwd", grid=(nblk,),
        in_specs=[
            smem,
            smem,
            pl.BlockSpec((BLOCK_Q, wide), lambda n: (n, COL_QB // GROUP)),
            pl.BlockSpec((BLOCK_Q, wide), lambda n: (n, COL_QB // GROUP + 1)),
            whole,
            whole,
            pl.BlockSpec((BLOCK_Q, 3 * BLOCK_Q), lambda n: (0, 0)),
            _ANY,
        ],
        out_specs=[
            pl.BlockSpec((BLOCK_Q, N_HEADS_B * HEAD_DIM), lambda n: (n, 1)),
            pl.BlockSpec((N_HEADS_B, BLOCK_Q, HEAD_DIM), lambda n: (0, n, 0)),
        ],
        out_shape=[
            jax.ShapeDtypeStruct(att.shape, BF16),
            jax.ShapeDtypeStruct((N_HEADS_B, s, HEAD_DIM), F32),
        ],
        scratch_shapes=[pltpu.VMEM((N_KV_B * GQ, 3 * BLOCK_Q), F32)],
        sem=("arbitrary",), args=(table, sink, pb, pb, kpad, vpad, bucket, att), comm=comm, after=after,
        aliases={7: 0})
    return res if comm is None else (res, c_res)


def _attn_b_bwd(pb, kpad, vpad, att, datt, lse, bucket, table, sink, *, comm=None, after=None):
    s = pb.shape[0]
    nblk = s // BLOCK_Q
    sp = kpad.shape[0]

    def body(table_ref, sink_ref, q0_ref, q1_ref, k_ref, v_ref, o_ref, do_ref, lse_ref, bucket_ref,
             dq_ref, dk_ref, dv_ref, dtab_ref, dsink_ref, bias_ref, dbias_ref):
        n = pl.program_id(0)

        @pl.when(n == 0)
        def _():
            dk_ref[...] = jnp.zeros_like(dk_ref)
            dv_ref[...] = jnp.zeros_like(dv_ref)
            dbias_ref[...] = jnp.zeros_like(dbias_ref)
            dsink_ref[...] = jnp.zeros_like(dsink_ref)
            for kv in range(N_KV_B):
                bias_ref[kv * GQ:(kv + 1) * GQ, :] = _group_bias(bucket_ref[...], table_ref, kv)

        band = pl.ds(_band_start(n), 3 * BLOCK_Q)
        mask = _band_mask(n, s)
        for kv, q_ref in enumerate((q0_ref, q1_ref)):
            wide_cols = _cols(kv * GROUP, GROUP)
            q = _stack_heads(q_ref[...])
            do = _stack_heads(do_ref[:, wide_cols])
            o = _stack_heads(o_ref[:, wide_cols])
            kb = k_ref[band, _cols(kv)]
            vb = v_ref[band, _cols(kv)]
            lse = jnp.concatenate([lse_ref[kv * GROUP + g][:, :1] for g in range(GROUP)], axis=0)
            sc = lax.dot_general(q, kb, _NT, preferred_element_type=F32)
            sc = jnp.where(mask, sc + bias_ref[kv * GQ:(kv + 1) * GQ, :], NEG_INF)
            p = jnp.exp2(sc - lse)
            dp = lax.dot_general(do, vb, _NT, preferred_element_type=F32)
            delta = jnp.sum(do.astype(F32) * o.astype(F32), axis=-1, keepdims=True)
            ds = p * (dp - delta)
            dsb = ds.astype(BF16)
            dq_ref[:, wide_cols] = _unstack_heads(jnp.dot(dsb, kb, preferred_element_type=F32))
            dk_ref[band, _cols(kv)] += lax.dot_general(dsb, q, _TN, preferred_element_type=F32)
            dv_ref[band, _cols(kv)] += lax.dot_general(p.astype(BF16), do, _TN, preferred_element_type=F32)
            dbias_ref[kv * GQ:(kv + 1) * GQ, :] += ds
            sink_part = -jnp.exp2(_group_sink(sink_ref, kv) - lse) * delta
            for g in range(GROUP):
                rows = slice(g * BLOCK_Q, (g + 1) * BLOCK_Q)
                dsink_ref[kv * GROUP + g] += jnp.broadcast_to(
                    jnp.sum(sink_part[rows], axis=0, keepdims=True), (1, HEAD_DIM))

        @pl.when(n == nblk - 1)
        def _():
            bucket_v = bucket_ref[...]
            row = lax.broadcasted_iota(jnp.int32, (N_BUCKETS, HEAD_DIM), 0)
            for h in range(N_HEADS_B):
                acc = dbias_ref[h * BLOCK_Q:(h + 1) * BLOCK_Q, :]
                tot = jnp.zeros((N_BUCKETS, HEAD_DIM), F32)
                for b in range(N_BUCKETS):
                    tot = jnp.where(row == b, jnp.sum(jnp.where(bucket_v == b, acc, 0.0), keepdims=True), tot)
                dtab_ref[h] = tot

    smem = pl.BlockSpec(memory_space=pltpu.SMEM)
    wide = GROUP * HEAD_DIM
    whole = pl.BlockSpec((sp, N_KV_B * HEAD_DIM), lambda n: (0, 0))
    group_b = pl.BlockSpec((BLOCK_Q, N_HEADS_B * HEAD_DIM), lambda n: (n, 1))
    res, c_res = _call(
        body, name="attn_b_bwd", grid=(nblk,),
        in_specs=[
            smem,
            smem,
            pl.BlockSpec((BLOCK_Q, wide), lambda n: (n, COL_QB // GROUP)),
            pl.BlockSpec((BLOCK_Q, wide), lambda n: (n, COL_QB // GROUP + 1)),
            whole,
            whole,
            group_b,
            group_b,
            pl.BlockSpec((N_HEADS_B, BLOCK_Q, HEAD_DIM), lambda n: (0, n, 0)),
            pl.BlockSpec((BLOCK_Q, 3 * BLOCK_Q), lambda n: (0, 0)),
        ],
        out_specs=[
            pl.BlockSpec((BLOCK_Q, N_HEADS_B * HEAD_DIM), lambda n: (n, 0)),
            whole,
            whole,
            pl.BlockSpec((N_HEADS_B, N_BUCKETS, HEAD_DIM), lambda n: (0, 0, 0)),
            pl.BlockSpec((N_HEADS_B, 1, HEAD_DIM), lambda n: (0, 0, 0)),
        ],
        out_shape=[
            jax.ShapeDtypeStruct((s, N_HEADS_B * HEAD_DIM), F32),
            jax.ShapeDtypeStruct((sp, N_KV_B * HEAD_DIM), F32),
            jax.ShapeDtypeStruct((sp, N_KV_B * HEAD_DIM), F32),
            jax.ShapeDtypeStruct((N_HEADS_B, N_BUCKETS, HEAD_DIM), F32),
            jax.ShapeDtypeStruct((N_HEADS_B, 1, HEAD_DIM), F32),
        ],
        scratch_shapes=[pltpu.VMEM((N_KV_B * GQ, 3 * BLOCK_Q), F32), pltpu.VMEM((N_KV_B * GQ, 3 * BLOCK_Q), F32)],
        sem=("arbitrary",),
        args=(table, sink, pb, pb, kpad, vpad, att, datt, lse, bucket), comm=comm, after=after)
    return res if comm is None else (res, c_res)


_MESH = pl.DeviceIdType.MESH


def _other_chips(x, y):
    return [(x, 1 - y), (1 - x, y), (1 - x, 1 - y)]


_HBM = pl.BlockSpec(memory_space=pltpu.HBM)
_SEM = pl.BlockSpec(memory_space=pltpu.SEMAPHORE)
_SPLIT = pltpu.CompilerParams(has_side_effects=pltpu.SideEffectType.DATAFLOW_SIDE_EFFECTING)


def _in_hbm(a):
    return pltpu.with_memory_space_constraint(a, pltpu.HBM)


def _my_half(rows):
    c = lax.axis_index("c")
    half = rows // 2
    return pl.ds(pl.multiple_of(c * half, half), half), pl.ds(pl.multiple_of((1 - c) * half, half), half)


def _gather_route(shapes):
    def route(src, land):
        x, y, c = lax.axis_index("x"), lax.axis_index("y"), lax.axis_index("c")
        out = []
        for t, shape in enumerate(shapes):
            mine, _ = _my_half(shape[0])
            for px, py in _other_chips(x, y):
                out.append((src[t].at[mine], land[t].at[2 * x + y, mine], land[t].at[2 * px + py, mine], (px, py, c)))
        return out

    return route


def _exchange_route(n_t):
    def route(src, land):
        x, y, c = lax.axis_index("x"), lax.axis_index("y"), lax.axis_index("c")
        out = []
        for t in range(n_t):
            for px, py in _other_chips(x, y):
                k = 2 * px + py
                out.append((src[t].at[k], land[t].at[2 * (2 * x + y) + c], land[t].at[2 * k + c], (px, py, c)))
        return out

    return route


def _own_slot(shape, dtype, slot, block):
    return lax.dynamic_update_slice(lax.empty(shape, dtype), block[None], (slot,) + (0,) * (len(shape) - 1))


def _split_start(name, srcs, lands, route, after):
    n = len(srcs)

    def body(*refs):
        src, land, send_sems, recv_sems, token = refs[:n], refs[n:2 * n], refs[2 * n + 1], refs[2 * n + 2], refs[-1]
        for i, (src_ref, dst_ref, _, to) in enumerate(route(src, land)):
            pltpu.make_async_remote_copy(src_ref=src_ref, dst_ref=dst_ref, send_sem=send_sems.at[i],
                                         recv_sem=recv_sems.at[i], device_id=to, device_id_type=_MESH).start()
        token[...] = jnp.zeros_like(token)

    sem = pltpu.SemaphoreType.DMA((3 * n,))
    lands = list(lands)
    res = pl.pallas_call(
        body, name=name,
        in_specs=[_HBM] * (2 * n) + [_ANY],
        out_specs=[_SEM, _SEM] + [_HBM] * (2 * n) + [pl.BlockSpec(memory_space=pltpu.VMEM)],
        out_shape=[sem, sem] + [pltpu.HBM(a.shape, a.dtype) for a in list(srcs) + lands]
        + [jax.ShapeDtypeStruct((8, 128), F32)],
        input_output_aliases={i: 2 + i for i in range(2 * n)},
        compiler_params=_SPLIT,
    )(*[_in_hbm(a) for a in srcs], *[_in_hbm(a) for a in lands], after)
    return (res[0], res[1]), res[2:2 + n], res[2 + n:2 + 2 * n], res[-1]


def _split_wait(name, srcs, lands, sems, route, after):
    n = len(srcs)

    def body(*refs):
        src, land, send_sems, recv_sems = refs[:n], refs[n:2 * n], refs[2 * n], refs[2 * n + 1]
        for i, (src_ref, _, dst_ref, to) in enumerate(route(src, land)):
            cp = pltpu.make_async_remote_copy(src_ref=src_ref, dst_ref=dst_ref, send_sem=send_sems.at[i],
                                              recv_sem=recv_sems.at[i], device_id=to, device_id_type=_MESH)
            cp.wait_send()
            cp.wait_recv()

    res = pl.pallas_call(
        body, name=name,
        in_specs=[_HBM] * (2 * n) + [_SEM, _SEM, _ANY],
        out_specs=[_HBM] * (2 * n),
        out_shape=[pltpu.HBM(a.shape, a.dtype) for a in list(srcs) + list(lands)],
        input_output_aliases={i: i for i in range(2 * n)},
        compiler_params=_SPLIT,
    )(*srcs, *lands, sems[0], sems[1], after)
    return res[:n], res[n:]


def _comm_only(name, comm):
    return _call(lambda: None, name=name, grid=(1,), in_specs=[], out_specs=[], out_shape=[], args=(), comm=comm)[1]


def _swap_comm(shards, lands):
    n_t = len(lands)

    def copies(land, sems, later):
        send_sems, recv_sems = sems
        x, y = lax.axis_index("x"), lax.axis_index("y")
        sends, recvs = [], []
        for t in range(n_t):
            mine, other = _my_half(shards[t].shape[0])
            for j, (px, py) in enumerate(_other_chips(x, y)):
                k = 2 * px + py
                for part, out in ((mine, sends), (other, recvs)) if later else ((mine, sends),):
                    out.append(pltpu.make_async_remote_copy(
                        src_ref=land[t].at[k, part], dst_ref=land[t].at[k, part], send_sem=send_sems.at[3 * t + j],
                        recv_sem=recv_sems.at[3 * t + j], device_id=_sibling(), device_id_type=_MESH))
        return sends, recvs

    def start(ins, land, sems):
        for cp in copies(land, sems, False)[0]:
            cp.start()

    def finish(ins, land, sems):
        sends, recvs = copies(land, sems, True)
        for cp in recvs:
            cp.wait_recv()
        for cp in sends:
            cp.wait_send()

    return _Comm(
        lands, [jax.ShapeDtypeStruct(a.shape, a.dtype) for a in lands],
        [pltpu.SemaphoreType.DMA((3 * n_t,)), pltpu.SemaphoreType.DMA((3 * n_t,))],
        start, finish, aliases={t: t for t in range(n_t)})


def _forward_comm(partials, lands):
    n_t = len(lands)

    def copies(land, sems, later):
        send_sems, recv_sems = sems
        x, y, c = lax.axis_index("x"), lax.axis_index("y"), lax.axis_index("c")
        sends, recvs = [], []
        for t in range(n_t):
            for j, k in enumerate([2 * x + y] + [2 * px + py for px, py in _other_chips(x, y)]):
                for slot, out in ((2 * k + c, sends), (2 * k + 1 - c, recvs)) if later else ((2 * k + c, sends),):
                    out.append(pltpu.make_async_remote_copy(
                        src_ref=land[t].at[slot], dst_ref=land[t].at[slot], send_sem=send_sems.at[4 * t + j],
                        recv_sem=recv_sems.at[4 * t + j], device_id=_sibling(), device_id_type=_MESH))
        return sends, recvs

    def start(ins, land, sems):
        for cp in copies(land, sems, False)[0]:
            cp.start()

    def finish(ins, land, sems):
        sends, recvs = copies(land, sems, True)
        for cp in recvs:
            cp.wait_recv()
        for cp in sends:
            cp.wait_send()

    return _Comm(
        lands, [jax.ShapeDtypeStruct(a.shape, a.dtype) for a in lands],
        [pltpu.SemaphoreType.DMA((4 * n_t,)), pltpu.SemaphoreType.DMA((4 * n_t,))],
        start, finish, aliases={t: t for t in range(n_t)})


def _allreduce_small(pack):
    rows, d = pack.shape

    def body(p_ref, sum_ref, all_ref, send_sems, recv_sems):
        x, y, c = lax.axis_index("x"), lax.axis_index("y"), lax.axis_index("c")
        me = 4 * x + 2 * y + c
        all_ref[me] = p_ref[...]
        peers = []
        for dx in range(2):
            for dy in range(2):
                for dc in range(2):
                    if dx or dy or dc:
                        px = 1 - x if dx else x
                        py = 1 - y if dy else y
                        pc = 1 - c if dc else c
                        peers.append((4 * dx + 2 * dy + dc - 1, (px, py, pc)))
        sends = []
        for k, to in peers:
            cp = pltpu.make_async_remote_copy(
                src_ref=p_ref, dst_ref=all_ref.at[me], send_sem=send_sems.at[k], recv_sem=recv_sems.at[k],
                device_id=to, device_id_type=_MESH)
            cp.start()
            sends.append(cp)
        for k, (px, py, pc) in peers:
            pltpu.make_async_remote_copy(
                src_ref=p_ref, dst_ref=all_ref.at[4 * px + 2 * py + pc], send_sem=send_sems.at[k],
                recv_sem=recv_sems.at[k], device_id=(px, py, pc), device_id_type=_MESH).wait_recv()
        for cp in sends:
            cp.wait_send()
        tot = all_ref[0]
        for i in range(1, N_DEV):
            tot = tot + all_ref[i]
        sum_ref[...] = tot

    vm = pl.BlockSpec(memory_space=pltpu.VMEM)
    return pl.pallas_call(
        body,
        name="allreduce_small",
        in_specs=[vm],
        out_specs=vm,
        out_shape=jax.ShapeDtypeStruct((rows, d), F32),
        scratch_shapes=[
            pltpu.VMEM((N_DEV, rows, d), F32),
            pltpu.SemaphoreType.DMA((N_DEV - 1,)),
            pltpu.SemaphoreType.DMA((N_DEV - 1,)),
        ],
    )(pack)


def _adamw_math(w, g, m, v):
    m = ADAM_B1 * m + (1.0 - ADAM_B1) * g
    v = ADAM_B2 * v + (1.0 - ADAM_B2) * (g * g)
    m_hat = m / (1.0 - ADAM_B1 ** ADAM_STEP)
    v_hat = v / (1.0 - ADAM_B2 ** ADAM_STEP)
    delta = -ADAM_LR * (m_hat / (jnp.sqrt(v_hat) + ADAM_EPS) + ADAM_WD * w)
    return delta, m, v


def _sum_adamw(parts, w, m, v, *, name, tr=256):
    r, c = w.shape
    tr = min(tr, r)
    tc = min(c, 1024)

    def body(p_ref, w_ref, m_ref, v_ref, g_ref, d_ref, m2_ref, v2_ref):
        g = p_ref[0].astype(F32)
        for i in range(1, N_DEV):
            g = g + p_ref[i].astype(F32)
        delta, m2, v2 = _adamw_math(w_ref[...], g, m_ref[...], v_ref[...])
        g_ref[...] = g
        d_ref[...] = delta
        m2_ref[...] = m2
        v2_ref[...] = v2

    blk = pl.BlockSpec((tr, tc), lambda i, j: (i, j))
    return pl.pallas_call(
        body,
        name=name,
        grid=(r // tr, c // tc),
        in_specs=[pl.BlockSpec((N_DEV, tr, tc), lambda i, j: (0, i, j)), blk, blk, blk],
        out_specs=[blk] * 4,
        out_shape=[jax.ShapeDtypeStruct((r, c), F32)] * 4,
        compiler_params=_params(("parallel", "parallel")),
    )(parts, w, m, v)


def _adamw_small(g, w, m, v):
    def body(g_ref, w_ref, m_ref, v_ref, d_ref, m2_ref, v2_ref):
        delta, m2, v2 = _adamw_math(w_ref[...], g_ref[...], m_ref[...], v_ref[...])
        d_ref[...] = delta
        m2_ref[...] = m2
        v2_ref[...] = v2

    vm = pl.BlockSpec(memory_space=pltpu.VMEM)
    return pl.pallas_call(
        body,
        name="adamw_small",
        in_specs=[vm] * 4,
        out_specs=[vm] * 3,
        out_shape=[jax.ShapeDtypeStruct(g.shape, F32)] * 3,
    )(g, w, m, v)


def _relu2_epilogue(acc):
    ra = jnp.maximum(acc, 0.0)
    return ra * ra, ra


def _residual_norm_epilogue(acc, res, g):
    h = acc + res
    return h, h * lax.rsqrt(jnp.mean(h * h, axis=-1, keepdims=True) + EPS) * g


def _rows(stacked):
    return stacked.reshape(stacked.shape[0] * stacked.shape[1], stacked.shape[2])


def _by_chip(mat):
    return mat.reshape(N_CHIPS, mat.shape[0] // N_CHIPS, mat.shape[1])


def _local_step(x, p, target, shards, small, update):
    s, d = x.shape
    cos_t, sin_t = _rope_tables(s)
    bucket = _band_buckets()
    p_bf = p.astype(BF16)
    wts = {}

    chip = 2 * lax.axis_index("x") + lax.axis_index("y")
    core = lax.axis_index("c")

    def gather(tag, names, after):
        srcs = [cast[n] for n in names]
        route = _gather_route([a.shape for a in srcs])
        sems, srcs, lands, token = _split_start(f"gather_start_{tag}", srcs, [zones[n] for n in names], route, after)

        def landed(done):
            got_srcs, got_lands = _split_wait(f"gather_wait_{tag}", srcs, lands, sems, route, done)
            comm = _swap_comm(got_srcs, got_lands)
            comm.waited = got_srcs[0]
            return comm

        return landed, token

    def prepare(n, zero):
        cast[n] = (shards[n] + zero).astype(BF16)
        zones[n] = _own_slot((N_CHIPS,) + cast[n].shape, BF16, chip, cast[n])

    cast, zones = {}, {}
    prepare("w_in", 0.0)
    in_landed, token = gather("in", ["w_in"], small["attn_norm_g"])
    for n in shards:
        if n != "w_in":
            prepare(n, token[:1, :1])
    g_attn = small["attn_norm_g"] + token[:1, :1]
    u = _rms_fwd(x, g_attn, name="norm_attn")
    prepared = u[:1, :1].astype(F32) + sum(
        (lax.dynamic_slice(zones[n], (chip, 0, 0), (1, 1, 1))[0] + cast[n][:1, :1]).astype(F32)
        for n in zones if n != "w_in")
    (wts["w_in"],) = _comm_only("swap_w_in", in_landed(prepared))
    mid_landed, token = gather("mid", ["w_out"], wts["w_in"])
    proj = _matmul(u, wts["w_in"], mode="nn", out_dtypes=[F32], name="mm_in", bn=768, after=token)
    pb, (w_out_s,) = _qk_prep(proj, small["q_norm_g"], small["k_norm_g"], cos_t, sin_t, comm=mid_landed(proj))
    wts["w_out"] = _rows(w_out_s)
    up_landed, token = gather("up", ["w_up"], pb)
    att_a, linv_a, p_a = _attn_a_fwd(pb, after=token)
    pad = ((PAD_LO, PAD_HI), (0, 0))
    kpad = jnp.pad(pb[:, COL_KB * HEAD_DIM:COL_VB * HEAD_DIM], pad)
    vpad = jnp.pad(pb[:, COL_VB * HEAD_DIM:], pad)
    up_swap = up_landed(att_a)
    down_landed, token = gather("down", ["w_down"], up_swap.waited)
    (att, lse_b), (wts["w_up"],) = _attn_b_fwd(pb, kpad, vpad, bucket, small["rel_bias_table"],
                                               small["sink_logits"], att_a, comm=up_swap, after=token)
    h1, mn = _matmul(att, wts["w_out"], mode="nn", out_dtypes=[F32, BF16], name="mm_out", bm=512, bn=d,
                     epilogue=_residual_norm_epilogue, extras=(x,), vecs=(small["mlp_norm_g"],))
    r, ra = _matmul(mn, wts["w_up"], mode="nn", out_dtypes=[BF16, BF16], name="mm_up", epilogue=_relu2_epilogue,
                    bm=2048)
    (w_down_s,) = _comm_only("swap_w_down", down_landed(r))
    wts["w_down"] = _rows(w_down_s)
    late_landed, token = gather("late", ["w_gate", "ple_w"], w_down_s)
    h2 = _matmul(r, wts["w_down"], mode="nn", out_dtypes=[F32], name="mm_down",
                 epilogue=lambda acc, res: (acc + res,), extras=(h1,), after=token)
    ng, (w_gate_s, wts["ple_w"]) = _rms_fwd(h2, small["gate_norm_g"], name="norm_gate", comm=late_landed(h2))
    wts["w_gate"] = _rows(w_gate_s)
    gate = _matmul(ng, wts["w_gate"], mode="nn", out_dtypes=[F32], name="mm_gate",
                   epilogue=lambda acc: (1.0 / (1.0 + jnp.exp(-acc)),))
    pp = _matmul(p_bf, wts["ple_w"], mode="nn", out_dtypes=[F32], name="mm_ple", bn=512)
    dh3, dz, dpp, dg_final, dg_ple, loss = _tail(h2, gate, pp, target, small["ple_norm_g"], small["final_norm_g"])

    dng = _matmul(dz, wts["w_gate"], mode="nt", out_dtypes=[F32], name="mm_gate_dx")
    gw_gate = _matmul(ng, dz, mode="tn", out_dtypes=[BF16], name="mm_gate_dw")
    gw_ple = _matmul(p_bf, dpp, mode="tn", out_dtypes=[BF16], name="mm_ple_dw", bn=512, out_stack=N_CHIPS)
    dh2, dh2_bf, dg_gate = _rms_bwd(h2, dng, small["gate_norm_g"], dh3, name="norm_gate_bwd", want_bf16=True)

    def exchange(tag, partials, after):
        route = _exchange_route(len(partials))
        lands = [_own_slot((N_DEV,) + g.shape[1:], g.dtype, 2 * chip + core,
                           lax.dynamic_index_in_dim(g, chip, 0, keepdims=False)) for g in partials]
        sems, srcs, lands, token = _split_start(f"exchange_start_{tag}", partials, lands, route, after)

        def landed(done):
            got_srcs, got_lands = _split_wait(f"exchange_wait_{tag}", srcs, lands, sems, route, done)
            comm = _forward_comm(got_srcs, got_lands)
            comm.waited = got_srcs[0]
            return comm

        return landed, token

    big = {}
    gate_landed, token = exchange("gate", [_by_chip(gw_gate), gw_ple], dh2_bf)
    gw_down = _matmul(r, dh2_bf, mode="tn", out_dtypes=[BF16], name="mm_down_dw", after=token, bm=512, bk=4096)
    da, (parts_gate, parts_ple) = _matmul(
        dh2_bf, wts["w_down"], mode="nt", out_dtypes=[BF16], name="mm_down_dx", bm=2048,
        epilogue=lambda acc, ra_v: (acc * (2.0 * ra_v.astype(F32)),), extras=(ra,), comm=gate_landed(gw_down))
    down_landed, token = exchange("down", [_by_chip(gw_down)], da)
    big["w_gate"], big["ple_w"] = update("w_gate", parts_gate), update("ple_w", parts_ple)
    gw_up = _matmul(mn, da, mode="tn", out_dtypes=[BF16], name="mm_up_dw", out_stack=N_CHIPS, after=token,
                    bm=512, bk=4096)
    dmn = _matmul(da, wts["w_up"], mode="nt", out_dtypes=[F32], name="mm_up_dx")
    dh1, dh1_bf, dg_mlp = _rms_bwd(h1, dmn, small["mlp_norm_g"], dh2, name="norm_mlp_bwd", want_bf16=True)
    datt, (parts_down,) = _matmul(dh1_bf, wts["w_out"], mode="nt", out_dtypes=[BF16], name="mm_out_dx",
                                  comm=down_landed(dh1_bf))
    gw_out = _matmul(att, dh1_bf, mode="tn", out_dtypes=[BF16], name="mm_out_dw")
    up_landed, token = exchange("up", [gw_up], datt)
    dqb, dkpad, dvpad, dtab, dsink = _attn_b_bwd(pb, kpad, vpad, att, datt, lse_b, bucket,
                                                 small["rel_bias_table"], small["sink_logits"], after=token)
    dqa, dka, dva = _attn_a_bwd(pb, att, datt, linv_a, p_a)
    up_forward = up_landed(dqa)
    out_landed, token = exchange("out", [_by_chip(gw_out)], up_forward.waited)
    (dproj, dg_q, dg_k), (parts_up,) = _qk_bwd(dqa, dka, dva, dqb, dkpad, dvpad, proj,
                                               small["q_norm_g"], small["k_norm_g"], cos_t, sin_t,
                                               comm=up_forward, after=token)
    gw_in = _matmul(u, dproj, mode="tn", out_dtypes=[BF16], name="mm_in_dw", bn=768, out_stack=N_CHIPS)
    out_forward = out_landed(gw_in)
    in_landed, token = exchange("in", [gw_in], out_forward.waited)
    du, (parts_out,) = _matmul(dproj, wts["w_in"], mode="nt", out_dtypes=[F32], name="mm_in_dx", bk=3072,
                               comm=out_forward, after=token)
    grad_x, dg_attn = _rms_bwd(x, du, small["attn_norm_g"], dh1, name="norm_attn_bwd", want_bf16=False)
    for n, parts in (("w_down", parts_down), ("w_up", parts_up), ("w_out", parts_out)):
        big[n] = update(n, parts)
    done = dg_attn + sum(big[n][0][0, :1, :] for n in ("w_down", "w_up", "w_out"))
    (parts_in,) = _comm_only("forward_w_in", in_landed(done))
    big["w_in"] = update("w_in", parts_in)

    small_g = {
        "attn_norm_g": dg_attn, "mlp_norm_g": dg_mlp, "ple_norm_g": dg_ple, "gate_norm_g": dg_gate,
        "final_norm_g": dg_final, "q_norm_g": dg_q, "k_norm_g": dg_k,
        "sink_logits": dsink[:, 0, 0][None, :], "rel_bias_table": dtab[:, :, 0].T,
    }
    return loss, grad_x, big, small_g


_SMALL_ROWS = ["attn_norm_g", "mlp_norm_g", "ple_norm_g", "gate_norm_g", "final_norm_g"]
_PACK_ROWS = 8


def _pack_small(vals, d):
    rows = [vals[n].reshape(1, d) for n in _SMALL_ROWS]
    misc = jnp.concatenate([
        vals["q_norm_g"].reshape(1, HEAD_DIM), vals["k_norm_g"].reshape(1, HEAD_DIM),
        jnp.pad(vals["sink_logits"].reshape(1, N_HEADS_B), ((0, 0), (0, HEAD_DIM - N_HEADS_B))),
        vals["rel_bias_table"].reshape(1, N_BUCKETS * N_HEADS_B)], axis=1)
    rows.append(jnp.pad(misc, ((0, 0), (0, d - misc.shape[1]))))
    rows.append(jnp.zeros((_PACK_ROWS - len(rows), d), F32))
    return jnp.concatenate(rows, axis=0).astype(F32)


def _unpack_small(pack, shapes):
    out = {n: pack[i].reshape(shapes[n]) for i, n in enumerate(_SMALL_ROWS)}
    misc = pack[len(_SMALL_ROWS)]
    out["q_norm_g"] = misc[:HEAD_DIM].reshape(shapes["q_norm_g"])
    out["k_norm_g"] = misc[HEAD_DIM:2 * HEAD_DIM].reshape(shapes["k_norm_g"])
    out["sink_logits"] = misc[2 * HEAD_DIM:2 * HEAD_DIM + N_HEADS_B].reshape(shapes["sink_logits"])
    out["rel_bias_table"] = misc[3 * HEAD_DIM:3 * HEAD_DIM + N_BUCKETS * N_HEADS_B].reshape(shapes["rel_bias_table"])
    return out


_WEIGHTS = ["attn_norm_g", "w_in", "q_norm_g", "k_norm_g", "sink_logits", "w_out", "mlp_norm_g", "w_up", "w_down",
            "ple_w", "ple_norm_g", "gate_norm_g", "w_gate", "rel_bias_table", "final_norm_g"]
_BIG = ["w_in", "w_out", "w_up", "w_down", "ple_w", "w_gate"]


def kernel(x, p, attn_norm_g, w_in, q_norm_g, k_norm_g, sink_logits, w_out, mlp_norm_g, w_up, w_down, ple_w, ple_norm_g, gate_norm_g, w_gate, rel_bias_table, final_norm_g, loss_target, m_attn_norm_g, m_w_in, m_q_norm_g, m_k_norm_g, m_sink_logits, m_w_out, m_mlp_norm_g, m_w_up, m_w_down, m_ple_w, m_ple_norm_g, m_gate_norm_g, m_w_gate, m_rel_bias_table, m_final_norm_g, v_attn_norm_g, v_w_in, v_q_norm_g, v_k_norm_g, v_sink_logits, v_w_out, v_mlp_norm_g, v_w_up, v_w_down, v_ple_w, v_ple_norm_g, v_gate_norm_g, v_w_gate, v_rel_bias_table, v_final_norm_g):
    given = dict(locals())
    w = {n: given[n] for n in _WEIGHTS}
    m = {n: given["m_" + n] for n in _WEIGHTS}
    v = {n: given["v_" + n] for n in _WEIGHTS}
    d = x.shape[-1]

    shards = {n: w[n][0] for n in _BIG}
    small = {
        "attn_norm_g": w["attn_norm_g"], "mlp_norm_g": w["mlp_norm_g"], "ple_norm_g": w["ple_norm_g"],
        "gate_norm_g": w["gate_norm_g"], "final_norm_g": w["final_norm_g"].reshape(1, d),
        "q_norm_g": w["q_norm_g"], "k_norm_g": w["k_norm_g"], "sink_logits": w["sink_logits"],
        "rel_bias_table": w["rel_bias_table"],
    }

    def update(n, parts):
        res = _sum_adamw(parts, w[n][0], m[n][0], v[n][0], name="adamw_" + n)
        return [t.reshape(w[n].shape) for t in res]

    loss_part, grad_x, big, small_g = _local_step(x[0], p[0, 0], loss_target[0], shards, small, update)
    grads, deltas, new_m, new_v = [{n: big[n][i] for n in _BIG} for i in range(4)]

    shapes = {n: w[n].shape for n in _WEIGHTS if n not in _BIG}
    pack = _pack_small(small_g, d)
    pack = pack.at[_PACK_ROWS - 1, :1].add(0.0 * grads["w_in"][0, 0, :1])
    pack = pack.at[_PACK_ROWS - 1, 1].set(loss_part[0, 0])
    g_small = _allreduce_small(pack)
    loss = g_small[_PACK_ROWS - 1, 1]
    d_small, m_small, v_small = _adamw_small(g_small, _pack_small(w, d), _pack_small(m, d), _pack_small(v, d))
    grads.update(_unpack_small(g_small, shapes))
    deltas.update(_unpack_small(d_small, shapes))
    new_m.update(_unpack_small(m_small, shapes))
    new_v.update(_unpack_small(v_small, shapes))

    return (loss, grad_x[None], *[grads[n] for n in _WEIGHTS], *[deltas[n] for n in _WEIGHTS],
            *[new_m[n] for n in _WEIGHTS], *[new_v[n] for n in _WEIGHTS])
```

```python
import functools
import math

import jax
import jax.numpy as jnp
import numpy as np
from jax import lax
from jax.experimental import pallas as pl
from jax.experimental.pallas import tpu as pltpu

F32 = jnp.float32
BF16 = jnp.bfloat16

HEAD_DIM = 128
N_HEADS_A = 8
N_KV_A = 2
N_HEADS_B = 8
N_KV_B = 2
GROUP = 4
GRID_W = 64
BLOCK_Q = 128
WINDOW = 128
N_BUCKETS = 32
MAX_DISTANCE = 128
ROPE_THETA = 10000.0
EPS = 1e-6
NEG_INF = -1e30
ATT_SCALE = HEAD_DIM ** -0.5
LOG2E = math.log2(math.e)
LN2 = math.log(2.0)
Q_SCALE = ATT_SCALE * LOG2E
PAD_LO, PAD_HI = 256, 128
ADAM_LR = 0.001
ADAM_B1 = 0.9
ADAM_B2 = 0.999
ADAM_EPS = 1e-08
ADAM_WD = 0.01
ADAM_STEP = 10

N_CHIPS = 4
N_DEV = 8
COL_QA, COL_KA, COL_VA, COL_QB, COL_KB, COL_VB = 0, 8, 10, 12, 20, 22
N_COLS = 24

VMEM_LIMIT = 52 * 1024 * 1024


def _params(sem=None, collective_id=None):
    return pltpu.CompilerParams(dimension_semantics=sem, vmem_limit_bytes=VMEM_LIMIT, collective_id=collective_id)


_ANY = pl.BlockSpec(memory_space=pl.ANY)
_MESH = pl.DeviceIdType.MESH
SIBLING_BARRIER_ID = 1


def _sibling():
    return (lax.axis_index("x"), lax.axis_index("y"), 1 - lax.axis_index("c"))


class _Comm:
    def __init__(self, inputs, out_shapes, sems, start, finish, aliases=None):
        self.inputs, self.out_shapes, self.sems = list(inputs), list(out_shapes), list(sems)
        self.start, self.finish, self.aliases = start, finish, dict(aliases or {})


def _call(body, *, name, grid, in_specs, out_specs, out_shape, args, scratch_shapes=(), sem=None, comm=None,
          after=None, aliases=None):
    in_specs, out_specs, out_shape = list(in_specs), list(out_specs), list(out_shape)
    scratch_shapes = list(scratch_shapes)
    n_in, n_out, n_sc = len(in_specs), len(out_specs), len(scratch_shapes)
    behind = [] if after is None else [after]
    aliases = dict(aliases or {})
    if comm is None:
        res = pl.pallas_call(
            (lambda *refs: body(*refs[:n_in], *refs[n_in + len(behind):])) if behind else body,
            name=name, grid=grid, in_specs=in_specs + [_ANY] * len(behind), out_specs=out_specs,
            out_shape=out_shape, scratch_shapes=scratch_shapes, input_output_aliases=aliases,
            compiler_params=_params(sem))(*args, *behind)
        return list(res), []
    c_in, c_out = len(comm.inputs), len(comm.out_shapes)

    def hosted(*refs):
        pos = [0]

        def take(n):
            pos[0] += n
            return refs[pos[0] - n:pos[0]]

        ins, c_ins, _, outs, c_outs, scr = (take(n_in), take(c_in), take(len(behind)), take(n_out), take(c_out),
                                            take(n_sc))
        c_sems = refs[pos[0]:]
        ids = [pl.program_id(a) for a in range(len(grid))]
        first = functools.reduce(jnp.logical_and, [i == 0 for i in ids])
        last = functools.reduce(jnp.logical_and, [i == g - 1 for i, g in zip(ids, grid)])

        @pl.when(first)
        def _():
            barrier = pltpu.get_barrier_semaphore()
            pl.semaphore_signal(barrier, inc=1, device_id=_sibling(), device_id_type=_MESH)
            pl.semaphore_wait(barrier, 1)
            comm.start(c_ins, c_outs, c_sems)

        body(*ins, *outs, *scr)

        @pl.when(last)
        def _():
            comm.finish(c_ins, c_outs, c_sems)

    res = pl.pallas_call(
        hosted, name=name, grid=grid, in_specs=in_specs + [_ANY] * (c_in + len(behind)),
        out_specs=out_specs + [_ANY] * c_out,
        out_shape=out_shape + comm.out_shapes, scratch_shapes=scratch_shapes + comm.sems,
        input_output_aliases={**aliases, **{n_in + i: n_out + o for i, o in comm.aliases.items()}},
        compiler_params=_params(("arbitrary",) * len(grid), SIBLING_BARRIER_ID))(*args, *comm.inputs, *behind)
    return list(res[:n_out]), list(res[n_out:])


def _matmul(a, b, *, mode, out_dtypes, name, epilogue=None, extras=(), bm=1024, bn=1024, bk=2048,
            out_stack=0, comm=None, after=None, vecs=()):
    stacked = b.ndim == 3
    if mode == "nn":
        m, k = a.shape
        if stacked:
            nj, kb, ns = b.shape
            n, ks = nj * ns, k
        else:
            kb, n = b.shape
            ns, ks = n, k
        dn = (((1,), (0,)), ((), ()))
    elif mode == "nt":
        m, k = a.shape
        if stacked:
            nj, n, ks = b.shape
            kb = nj * ks
        else:
            n, kb = b.shape
            ks = kb
        ns = n
        dn = (((1,), (1,)), ((), ()))
    else:
        k, m = a.shape
        kb, n = b.shape
        ns, ks = n, k
        dn = (((0,), (0,)), ((), ()))
    assert k == kb and not (stacked and mode == "tn")
    ns_out = n // out_stack if out_stack else n
    whole_k = stacked and mode == "nt" and bk >= k
    bm, bn, bk = min(bm, m), min(bn, ns, ns_out), k if whole_k else min(bk, ks)
    assert m % bm == 0 and ns % bn == 0 and ns_out % bn == 0 and ks % bk == 0 or whole_k
    gm, gn, gk = m // bm, n // bn, k // bk

    if mode == "tn":
        a_spec = pl.BlockSpec((bk, bm), lambda i, j, q: (q, i))
    else:
        a_spec = pl.BlockSpec((bm, bk), lambda i, j, q: (i, q))
    if mode == "nt":
        if whole_k:
            b_spec = pl.BlockSpec((nj, bn, ks), lambda i, j, q: (0, j, 0))
        elif stacked:
            per = ks // bk
            b_spec = pl.BlockSpec((None, bn, bk), lambda i, j, q: (q // per, j, q % per))
        else:
            b_spec = pl.BlockSpec((bn, bk), lambda i, j, q: (j, q))
    else:
        if stacked:
            per = ns // bn
            b_spec = pl.BlockSpec((None, bk, bn), lambda i, j, q: (j // per, q, j % per))
        else:
            b_spec = pl.BlockSpec((bk, bn), lambda i, j, q: (q, j))
    ex_spec = pl.BlockSpec((bm, bn), lambda i, j, q: (i, j))
    if out_stack:
        per_o = ns_out // bn
        o_spec = pl.BlockSpec((None, bm, bn), lambda i, j, q: (j // per_o, i, j % per_o))
        o_shape = (out_stack, m, ns_out)
    else:
        o_spec = ex_spec
        o_shape = (m, n)
    n_ex, n_out = len(extras) + len(vecs), len(out_dtypes)

    def body(a_ref, b_ref, *rest):
        ex, outs = rest[:n_ex], rest[n_ex:n_ex + n_out]
        if whole_k:
            part = sum(lax.dot_general(a_ref[:, t * ks:(t + 1) * ks], b_ref[t], dn, preferred_element_type=F32)
                       for t in range(nj))
        else:
            part = lax.dot_general(a_ref[...], b_ref[...], dn, preferred_element_type=F32)

        def finish(acc):
            res = epilogue(acc, *[e[...] for e in ex]) if epilogue else (acc,)
            for o, r in zip(outs, res):
                o[...] = r.astype(o.dtype)

        if gk == 1:
            finish(part)
        else:
            acc_ref = rest[-1]
            q = pl.program_id(2)

            @pl.when(q == 0)
            def _():
                acc_ref[...] = part

            @pl.when(q > 0)
            def _():
                acc_ref[...] += part

            @pl.when(q == gk - 1)
            def _():
                finish(acc_ref[...])

    res, c_res = _call(
        body, name=name, grid=(gm, gn, gk),
        in_specs=[a_spec, b_spec] + [ex_spec] * len(extras)
        + [pl.BlockSpec((1, bn), lambda i, j, q: (0, j))] * len(vecs),
        out_specs=[o_spec] * n_out,
        out_shape=[jax.ShapeDtypeStruct(o_shape, dt) for dt in out_dtypes],
        scratch_shapes=[pltpu.VMEM((bm, bn), F32)] if gk > 1 else [],
        sem=("parallel", "parallel", "arbitrary"), args=(a, b, *extras, *vecs), comm=comm, after=after)
    res = res[0] if n_out == 1 else res
    return res if comm is None else (res, c_res)


def _rms_fwd(x, g, *, name, tm=256, comm=None):
    s, d = x.shape
    tm = min(tm, s)

    def body(x_ref, g_ref, o_ref):
        xf = x_ref[...]
        r = lax.rsqrt(jnp.mean(xf * xf, axis=-1, keepdims=True) + EPS)
        o_ref[...] = (xf * r * g_ref[...]).astype(o_ref.dtype)

    res, c_res = _call(
        body, name=name, grid=(s // tm,),
        in_specs=[pl.BlockSpec((tm, d), lambda i: (i, 0)), pl.BlockSpec((1, d), lambda i: (0, 0))],
        out_specs=[pl.BlockSpec((tm, d), lambda i: (i, 0))],
        out_shape=[jax.ShapeDtypeStruct((s, d), BF16)],
        sem=("parallel",), args=(x, g), comm=comm)
    return res[0] if comm is None else (res[0], c_res)


def _rms_bwd(x, dy, g, add, *, name, want_bf16, tm=256):
    s, d = x.shape
    tm = min(tm, s)

    def body(x_ref, dy_ref, g_ref, add_ref, dx_ref, *rest):
        dg_ref = rest[-1]
        i = pl.program_id(0)
        xf = x_ref[...]
        dyf = dy_ref[...].astype(F32)
        r = lax.rsqrt(jnp.mean(xf * xf, axis=-1, keepdims=True) + EPS)
        xh = xf * r
        dyg = dyf * g_ref[...]
        dx = r * (dyg - xh * jnp.mean(dyg * xh, axis=-1, keepdims=True))
        tot = add_ref[...] + dx
        dx_ref[...] = tot
        if want_bf16:
            rest[0][...] = tot.astype(BF16)
        part = jnp.sum(dyf * xh, axis=0, keepdims=True)

        @pl.when(i == 0)
        def _():
            dg_ref[...] = part

        @pl.when(i > 0)
        def _():
            dg_ref[...] += part

    row = pl.BlockSpec((tm, d), lambda i: (i, 0))
    vec = pl.BlockSpec((1, d), lambda i: (0, 0))
    out_specs = [row] + ([row] if want_bf16 else []) + [vec]
    out_shape = [jax.ShapeDtypeStruct((s, d), F32)]
    if want_bf16:
        out_shape.append(jax.ShapeDtypeStruct((s, d), BF16))
    out_shape.append(jax.ShapeDtypeStruct((1, d), F32))
    return pl.pallas_call(
        body,
        name=name,
        grid=(s // tm,),
        in_specs=[row, row, vec, row],
        out_specs=out_specs,
        out_shape=out_shape,
        compiler_params=_params(("arbitrary",)),
    )(x, dy, g, add)


def _tail(h2, gate, pp, target, g_ple, g_final, *, tm=128):
    s, d = h2.shape
    tm = min(tm, s)

    def body(h2_ref, gate_ref, pp_ref, t_ref, gp_ref, gf_ref, dh3_ref, dz_ref, dpp_ref, dgf_ref, dgp_ref, loss_ref):
        i = pl.program_id(0)
        ppf = pp_ref[...]
        gate_v = gate_ref[...]
        r_p = lax.rsqrt(jnp.mean(ppf * ppf, axis=-1, keepdims=True) + EPS)
        eh = ppf * r_p
        e = eh * gp_ref[...]
        h3 = h2_ref[...] + gate_v * e
        r_f = lax.rsqrt(jnp.mean(h3 * h3, axis=-1, keepdims=True) + EPS)
        yh = h3 * r_f
        diff = yh * gf_ref[...] - t_ref[...]
        loss_part = 0.5 * jnp.sum(jnp.mean(diff * diff, axis=-1, keepdims=True), axis=0, keepdims=True)
        dy = diff / d
        dgf = jnp.sum(dy * yh, axis=0, keepdims=True)
        dyg = dy * gf_ref[...]
        dh3 = r_f * (dyg - yh * jnp.mean(dyg * yh, axis=-1, keepdims=True))
        dh3_ref[...] = dh3
        de = dh3 * gate_v
        dz_ref[...] = (dh3 * e * gate_v * (1.0 - gate_v)).astype(BF16)
        dgp = jnp.sum(de * eh, axis=0, keepdims=True)
        deg = de * gp_ref[...]
        dpp_ref[...] = (r_p * (deg - eh * jnp.mean(deg * eh, axis=-1, keepdims=True))).astype(BF16)
        loss_row = jnp.broadcast_to(loss_part, (1, 128))

        @pl.when(i == 0)
        def _():
            dgf_ref[...] = dgf
            dgp_ref[...] = dgp
            loss_ref[...] = loss_row

        @pl.when(i > 0)
        def _():
            dgf_ref[...] += dgf
            dgp_ref[...] += dgp
            loss_ref[...] += loss_row

    row = pl.BlockSpec((tm, d), lambda i: (i, 0))
    vec = pl.BlockSpec((1, d), lambda i: (0, 0))
    return pl.pallas_call(
        body,
        name="tail_fwd_bwd",
        grid=(s // tm,),
        in_specs=[row, row, row, row, vec, vec],
        out_specs=[row, row, row, vec, vec, pl.BlockSpec((1, 128), lambda i: (0, 0))],
        out_shape=[
            jax.ShapeDtypeStruct((s, d), F32),
            jax.ShapeDtypeStruct((s, d), BF16),
            jax.ShapeDtypeStruct((s, d), BF16),
            jax.ShapeDtypeStruct((1, d), F32),
            jax.ShapeDtypeStruct((1, d), F32),
            jax.ShapeDtypeStruct((1, 128), F32),
        ],
        compiler_params=_params(("arbitrary",)),
    )(h2, gate, pp, target, g_ple, g_final)


def _rope_tables(s):
    rows = s // GRID_W
    half = HEAD_DIM // 2
    inv_freq = ROPE_THETA ** (-jnp.arange(0, half, 2, dtype=F32) / half)
    ang_r = jnp.arange(rows, dtype=jnp.int32).astype(F32)[:, None] * inv_freq
    ang_c = jnp.arange(GRID_W, dtype=jnp.int32).astype(F32)[:, None] * inv_freq
    cr, sr = (jnp.repeat(t, GRID_W, axis=0) for t in (jnp.cos(ang_r), jnp.sin(ang_r)))
    cc, sc = (jnp.tile(t, (rows, 1)) for t in (jnp.cos(ang_c), jnp.sin(ang_c)))
    cos_t = jnp.concatenate([cr, cr, cc, cc], axis=-1)
    sin_t = jnp.concatenate([-sr, sr, -sc, sc], axis=-1)
    return cos_t, sin_t


def _swap_quarters(x):
    lane = lax.broadcasted_iota(jnp.int32, x.shape, x.ndim - 1)
    up = pltpu.roll(x, HEAD_DIM - 32, x.ndim - 1)
    down = pltpu.roll(x, 32, x.ndim - 1)
    return jnp.where((lane % 64) < 32, up, down)


def _cols(first, count=1):
    return slice(first * HEAD_DIM, (first + count) * HEAD_DIM)


def _qk_prep(proj, g_q, g_k, cos_t, sin_t, *, tm=256, comm=None):
    s, n = proj.shape
    tm = min(tm, s)

    def body(x_ref, gq_ref, gk_ref, c_ref, s_ref, o_ref):
        cos_v, sin_v = c_ref[...], s_ref[...]
        for h in range(COL_VA):
            x = x_ref[:, _cols(h)]
            g = gq_ref[...] if h < COL_KA else gk_ref[...]
            xn = x * lax.rsqrt(jnp.mean(x * x, axis=-1, keepdims=True) + EPS) * g
            xr = xn * cos_v + _swap_quarters(xn) * sin_v
            if h < COL_KA:
                xr = xr * Q_SCALE
            o_ref[:, _cols(h)] = xr.astype(BF16)
        o_ref[:, _cols(COL_VA, 2)] = x_ref[:, _cols(COL_VA, 2)].astype(BF16)
        o_ref[:, _cols(COL_QB, N_HEADS_B)] = (x_ref[:, _cols(COL_QB, N_HEADS_B)] * Q_SCALE).astype(BF16)
        o_ref[:, _cols(COL_KB, 4)] = x_ref[:, _cols(COL_KB, 4)].astype(BF16)

    row = pl.BlockSpec((tm, n), lambda i: (i, 0))
    tab = pl.BlockSpec((tm, HEAD_DIM), lambda i: (i, 0))
    vec = pl.BlockSpec((1, HEAD_DIM), lambda i: (0, 0))
    res, c_res = _call(
        body, name="qk_prep", grid=(s // tm,),
        in_specs=[row, vec, vec, tab, tab],
        out_specs=[row],
        out_shape=[jax.ShapeDtypeStruct((s, n), BF16)],
        sem=("parallel",), args=(proj, g_q, g_k, cos_t, sin_t), comm=comm)
    return res[0] if comm is None else (res[0], c_res)


def _qk_bwd(dqa, dka, dva, dqb, dkpad, dvpad, proj, g_q, g_k, cos_t, sin_t, *, comm=None, after=None):
    s, n = proj.shape
    tm = min(PAD_LO, s)
    assert PAD_LO % tm == 0
    lo = PAD_LO // tm

    def body(dqa_ref, dka_ref, dva_ref, dqb_ref, dkb_ref, dvb_ref, x_ref, gq_ref, gk_ref, c_ref, s_ref,
             o_ref, dgq_ref, dgk_ref):
        i = pl.program_id(0)
        cos_v, sin_v = c_ref[...], s_ref[...]

        def head(d, x, g):
            dn = d * cos_v + _swap_quarters(d * sin_v)
            r = lax.rsqrt(jnp.mean(x * x, axis=-1, keepdims=True) + EPS)
            xh = x * r
            dng = dn * g
            dx = r * (dng - xh * jnp.mean(dng * xh, axis=-1, keepdims=True))
            return dx.astype(BF16), jnp.sum(dn * xh, axis=0, keepdims=True)

        acc_q = jnp.zeros((1, HEAD_DIM), F32)
        acc_k = jnp.zeros((1, HEAD_DIM), F32)
        for h in range(N_HEADS_A):
            o_ref[:, _cols(h)], part = head(dqa_ref[:, _cols(h)] * ATT_SCALE, x_ref[:, _cols(h)], gq_ref[...])
            acc_q = acc_q + part
        for h in range(N_KV_A):
            o_ref[:, _cols(COL_KA + h)], part = head(dka_ref[:, _cols(h)] * LN2, x_ref[:, _cols(COL_KA + h)],
                                                     gk_ref[...])
            acc_k = acc_k + part
        o_ref[:, _cols(COL_VA, 2)] = dva_ref[...].astype(BF16)
        o_ref[:, _cols(COL_QB, N_HEADS_B)] = (dqb_ref[...] * ATT_SCALE).astype(BF16)
        o_ref[:, _cols(COL_KB, 2)] = (dkb_ref[...] * LN2).astype(BF16)
        o_ref[:, _cols(COL_VB, 2)] = dvb_ref[...].astype(BF16)

        @pl.when(i == 0)
        def _():
            dgq_ref[...] = acc_q
            dgk_ref[...] = acc_k

        @pl.when(i > 0)
        def _():
            dgq_ref[...] += acc_q
            dgk_ref[...] += acc_k

    def rows(width, shift=0):
        return pl.BlockSpec((tm, width), lambda i: (i + shift, 0))

    kv_w = N_KV_A * HEAD_DIM
    q_w = N_HEADS_A * HEAD_DIM
    vec = pl.BlockSpec((1, HEAD_DIM), lambda i: (0, 0))
    res, c_res = _call(
        body, name="qk_bwd", grid=(s // tm,),
        in_specs=[rows(q_w), rows(kv_w), rows(kv_w), rows(q_w), rows(kv_w, lo), rows(kv_w, lo), rows(n),
                  vec, vec, rows(HEAD_DIM), rows(HEAD_DIM)],
        out_specs=[rows(n), vec, vec],
        out_shape=[
            jax.ShapeDtypeStruct((s, n), BF16),
            jax.ShapeDtypeStruct((1, HEAD_DIM), F32),
            jax.ShapeDtypeStruct((1, HEAD_DIM), F32),
        ],
        sem=("arbitrary",), args=(dqa, dka, dva, dqb, dkpad, dvpad, proj, g_q, g_k, cos_t, sin_t), comm=comm,
        after=after)
    return res if comm is None else (res, c_res)


_NT = (((1,), (1,)), ((), ()))
_TN = (((0,), (0,)), ((), ()))


def _attn_a_fwd(pb, *, tq=4096, sub=256, comm=None, after=None):
    s = pb.shape[0]
    tq = min(tq, s)
    sub = min(sub, tq)

    def body(q_ref, k_ref, v_ref, o_ref, lse_ref):
        k = k_ref[...]
        v = v_ref[...]
        for r in range(tq // sub):
            rows = pl.ds(r * sub, sub)
            sc = lax.dot_general(q_ref[rows, :], k, _NT, preferred_element_type=F32)
            m = jnp.max(sc, axis=-1, keepdims=True)
            p = jnp.exp2(sc - m)
            l = jnp.sum(p, axis=-1, keepdims=True)
            o = jnp.dot(p.astype(BF16), v, preferred_element_type=F32)
            o_ref[rows, :] = (o / l).astype(BF16)
            lse_ref[rows, :] = jnp.broadcast_to(m + jnp.log2(l), (sub, HEAD_DIM))

    res, c_res = _call(
        body, name="attn_a_fwd", grid=(N_HEADS_A, s // tq),
        in_specs=[
            pl.BlockSpec((tq, HEAD_DIM), lambda h, i: (i, COL_QA + h)),
            pl.BlockSpec((s, HEAD_DIM), lambda h, i: (0, COL_KA + h // GROUP)),
            pl.BlockSpec((s, HEAD_DIM), lambda h, i: (0, COL_VA + h // GROUP)),
        ],
        out_specs=[
            pl.BlockSpec((tq, HEAD_DIM), lambda h, i: (i, h)),
            pl.BlockSpec((None, tq, HEAD_DIM), lambda h, i: (h, i, 0)),
        ],
        out_shape=[
            jax.ShapeDtypeStruct((s, (N_HEADS_A + N_HEADS_B) * HEAD_DIM), BF16),
            jax.ShapeDtypeStruct((N_HEADS_A, s, HEAD_DIM), F32),
        ],
        sem=("parallel", "parallel"), args=(pb, pb, pb), comm=comm, after=after)
    return res if comm is None else (res, c_res)


def _attn_a_bwd(pb, att, datt, lse, *, tq=1024, sub=256, comm=None):
    s = pb.shape[0]
    tq = min(tq, s)
    sub = min(sub, tq)

    def body(q_ref, k_ref, v_ref, o_ref, do_ref, lse_ref, dq_ref, dk_ref, dv_ref):
        first = jnp.logical_and(pl.program_id(1) == 0, pl.program_id(2) == 0)
        k = k_ref[...]
        v = v_ref[...]
        dk = dv = None
        for r in range(tq // sub):
            rows = pl.ds(r * sub, sub)
            q = q_ref[rows, :]
            do = do_ref[rows, :]
            sc = lax.dot_general(q, k, _NT, preferred_element_type=F32)
            p = jnp.exp2(sc - lse_ref[rows, :][:, :1])
            dp = lax.dot_general(do, v, _NT, preferred_element_type=F32)
            delta = jnp.sum(do.astype(F32) * o_ref[rows, :].astype(F32), axis=-1, keepdims=True)
            ds = (p * (dp - delta)).astype(BF16)
            dq_ref[rows, :] = jnp.dot(ds, k, preferred_element_type=F32)
            dk_r = lax.dot_general(ds, q, _TN, preferred_element_type=F32)
            dv_r = lax.dot_general(p.astype(BF16), do, _TN, preferred_element_type=F32)
            dk = dk_r if dk is None else dk + dk_r
            dv = dv_r if dv is None else dv + dv_r

        @pl.when(first)
        def _():
            dk_ref[...] = dk
            dv_ref[...] = dv

        @pl.when(jnp.logical_not(first))
        def _():
            dk_ref[...] += dk
            dv_ref[...] += dv

    qmap = lambda kv, g, i: (i, kv * GROUP + g)
    res, c_res = _call(
        body, name="attn_a_bwd", grid=(N_KV_A, GROUP, s // tq),
        in_specs=[
            pl.BlockSpec((tq, HEAD_DIM), lambda kv, g, i: (i, COL_QA + kv * GROUP + g)),
            pl.BlockSpec((s, HEAD_DIM), lambda kv, g, i: (0, COL_KA + kv)),
            pl.BlockSpec((s, HEAD_DIM), lambda kv, g, i: (0, COL_VA + kv)),
            pl.BlockSpec((tq, HEAD_DIM), qmap),
            pl.BlockSpec((tq, HEAD_DIM), qmap),
            pl.BlockSpec((None, tq, HEAD_DIM), lambda kv, g, i: (kv * GROUP + g, i, 0)),
        ],
        out_specs=[
            pl.BlockSpec((tq, HEAD_DIM), qmap),
            pl.BlockSpec((s, HEAD_DIM), lambda kv, g, i: (0, kv)),
            pl.BlockSpec((s, HEAD_DIM), lambda kv, g, i: (0, kv)),
        ],
        out_shape=[
            jax.ShapeDtypeStruct((s, N_HEADS_A * HEAD_DIM), F32),
            jax.ShapeDtypeStruct((s, N_KV_A * HEAD_DIM), F32),
            jax.ShapeDtypeStruct((s, N_KV_A * HEAD_DIM), F32),
        ],
        sem=("arbitrary", "arbitrary", "arbitrary"), args=(pb, pb, pb, att, datt, lse), comm=comm)
    return res if comm is None else (res, c_res)


def _t5_bucket(rel):
    nb = N_BUCKETS // 2
    ret = jnp.where(rel > 0, nb, 0)
    n = jnp.abs(rel)
    max_exact = nb // 2
    nf = jnp.maximum(n, 1).astype(F32)
    large = max_exact + (jnp.log(nf / max_exact) / math.log(MAX_DISTANCE / max_exact)
                         * (nb - max_exact)).astype(jnp.int32)
    large = jnp.minimum(large, nb - 1)
    return ret + jnp.where(n < max_exact, n, large)


def _band_buckets():
    r = jnp.arange(BLOCK_Q, dtype=jnp.int32)
    j = jnp.arange(3 * BLOCK_Q, dtype=jnp.int32)
    return _t5_bucket((j[None, :] - BLOCK_Q) - r[:, None])


def _band_bias(bucket, table_ref, h):
    acc = jnp.zeros(bucket.shape, F32)
    for b in range(N_BUCKETS):
        acc = jnp.where(bucket == b, table_ref[b, h], acc)
    return acc


GQ = GROUP * BLOCK_Q


def _stack_heads(x):
    return jnp.concatenate([x[:, _cols(g)] for g in range(GROUP)], axis=0)


def _unstack_heads(x):
    return jnp.concatenate([x[g * BLOCK_Q:(g + 1) * BLOCK_Q] for g in range(GROUP)], axis=1)


def _group_bias(bucket, table_ref, kv):
    return jnp.concatenate([_band_bias(bucket, table_ref, kv * GROUP + g) * LOG2E for g in range(GROUP)], axis=0)


def _group_sink(sink_ref, kv):
    head = lax.broadcasted_iota(jnp.int32, (GQ, 1), 0) // BLOCK_Q
    snk = jnp.zeros((GQ, 1), F32)
    for g in range(GROUP):
        snk = jnp.where(head == g, sink_ref[0, kv * GROUP + g] * LOG2E, snk)
    return snk


def _band_mask(n, s):
    r = lax.broadcasted_iota(jnp.int32, (GQ, 3 * BLOCK_Q), 0) % BLOCK_Q
    j = lax.broadcasted_iota(jnp.int32, (GQ, 3 * BLOCK_Q), 1)
    rel = j - BLOCK_Q - r
    kabs = n * BLOCK_Q + j - BLOCK_Q
    return (jnp.abs(rel) <= WINDOW) & (kabs >= 0) & (kabs < s)


def _band_start(n):
    return pl.multiple_of(n * BLOCK_Q + (PAD_LO - BLOCK_Q), BLOCK_Q)


def _attn_b_fwd(pb, kpad, vpad, bucket, table, sink, att, *, comm=None, after=None):
    s = pb.shape[0]
    nblk = s // BLOCK_Q
    sp = kpad.shape[0]

    def body(table_ref, sink_ref, q0_ref, q1_ref, k_ref, v_ref, bucket_ref, _, o_ref, lse_ref, bias_ref):
        n = pl.program_id(0)

        @pl.when(n == 0)
        def _():
            for kv in range(N_KV_B):
                bias_ref[kv * GQ:(kv + 1) * GQ, :] = _group_bias(bucket_ref[...], table_ref, kv)

        band = pl.ds(_band_start(n), 3 * BLOCK_Q)
        mask = _band_mask(n, s)
        for kv, q_ref in enumerate((q0_ref, q1_ref)):
            kb = k_ref[band, _cols(kv)]
            vb = v_ref[band, _cols(kv)]
            sc = lax.dot_general(_stack_heads(q_ref[...]), kb, _NT, preferred_element_type=F32)
            sc = jnp.where(mask, sc + bias_ref[kv * GQ:(kv + 1) * GQ, :], NEG_INF)
            snk = _group_sink(sink_ref, kv)
            m = jnp.maximum(jnp.max(sc, axis=-1, keepdims=True), snk)
            p = jnp.exp2(sc - m)
            l = jnp.sum(p, axis=-1, keepdims=True) + jnp.exp2(snk - m)
            o = jnp.dot(p.astype(BF16), vb, preferred_element_type=F32)
            o_ref[:, _cols(kv * GROUP, GROUP)] = _unstack_heads((o / l).astype(BF16))
            lse = m + jnp.log2(l)
            for g in range(GROUP):
                lse_ref[kv * GROUP + g] = jnp.broadcast_to(lse[g * BLOCK_Q:(g + 1) * BLOCK_Q], (BLOCK_Q, HEAD_DIM))

    smem = pl.BlockSpec(memory_space=pltpu.SMEM)
    wide = GROUP * HEAD_DIM
    whole = pl.BlockSpec((sp, N_KV_B * HEAD_DIM), lambda n: (0, 0))
    res, c_res = _call(
        body, name="attn_b_fwd", grid=(nblk,),
        in_specs=[
            smem,
            smem,
            pl.BlockSpec((BLOCK_Q, wide), lambda n: (n, COL_QB // GROUP)),
            pl.BlockSpec((BLOCK_Q, wide), lambda n: (n, COL_QB // GROUP + 1)),
            whole,
            whole,
            pl.BlockSpec((BLOCK_Q, 3 * BLOCK_Q), lambda n: (0, 0)),
            _ANY,
        ],
        out_specs=[
            pl.BlockSpec((BLOCK_Q, N_HEADS_B * HEAD_DIM), lambda n: (n, 1)),
            pl.BlockSpec((N_HEADS_B, BLOCK_Q, HEAD_DIM), lambda n: (0, n, 0)),
        ],
        out_shape=[
            jax.ShapeDtypeStruct(att.shape, BF16),
            jax.ShapeDtypeStruct((N_HEADS_B, s, HEAD_DIM), F32),
        ],
        scratch_shapes=[pltpu.VMEM((N_KV_B * GQ, 3 * BLOCK_Q), F32)],
        sem=("arbitrary",), args=(table, sink, pb, pb, kpad, vpad, bucket, att), comm=comm, after=after,
        aliases={7: 0})
    return res if comm is None else (res, c_res)


def _attn_b_bwd(pb, kpad, vpad, att, datt, lse, bucket, table, sink, *, comm=None, after=None):
    s = pb.shape[0]
    nblk = s // BLOCK_Q
    sp = kpad.shape[0]

    def body(table_ref, sink_ref, q0_ref, q1_ref, k_ref, v_ref, o_ref, do_ref, lse_ref, bucket_ref,
             dq_ref, dk_ref, dv_ref, dtab_ref, dsink_ref, bias_ref, dbias_ref):
        n = pl.program_id(0)

        @pl.when(n == 0)
        def _():
            dk_ref[...] = jnp.zeros_like(dk_ref)
            dv_ref[...] = jnp.zeros_like(dv_ref)
            dbias_ref[...] = jnp.zeros_like(dbias_ref)
            dsink_ref[...] = jnp.zeros_like(dsink_ref)
            for kv in range(N_KV_B):
                bias_ref[kv * GQ:(kv + 1) * GQ, :] = _group_bias(bucket_ref[...], table_ref, kv)

        band = pl.ds(_band_start(n), 3 * BLOCK_Q)
        mask = _band_mask(n, s)
        for kv, q_ref in enumerate((q0_ref, q1_ref)):
            wide_cols = _cols(kv * GROUP, GROUP)
            q = _stack_heads(q_ref[...])
            do = _stack_heads(do_ref[:, wide_cols])
            o = _stack_heads(o_ref[:, wide_cols])
            kb = k_ref[band, _cols(kv)]
            vb = v_ref[band, _cols(kv)]
            lse = jnp.concatenate([lse_ref[kv * GROUP + g][:, :1] for g in range(GROUP)], axis=0)
            sc = lax.dot_general(q, kb, _NT, preferred_element_type=F32)
            sc = jnp.where(mask, sc + bias_ref[kv * GQ:(kv + 1) * GQ, :], NEG_INF)
            p = jnp.exp2(sc - lse)
            dp = lax.dot_general(do, vb, _NT, preferred_element_type=F32)
            delta = jnp.sum(do.astype(F32) * o.astype(F32), axis=-1, keepdims=True)
            ds = p * (dp - delta)
            dsb = ds.astype(BF16)
            dq_ref[:, wide_cols] = _unstack_heads(jnp.dot(dsb, kb, preferred_element_type=F32))
            dk_ref[band, _cols(kv)] += lax.dot_general(dsb, q, _TN, preferred_element_type=F32)
            dv_ref[band, _cols(kv)] += lax.dot_general(p.astype(BF16), do, _TN, preferred_element_type=F32)
            dbias_ref[kv * GQ:(kv + 1) * GQ, :] += ds
            sink_part = -jnp.exp2(_group_sink(sink_ref, kv) - lse) * delta
            for g in range(GROUP):
                rows = slice(g * BLOCK_Q, (g + 1) * BLOCK_Q)
                dsink_ref[kv * GROUP + g] += jnp.broadcast_to(
                    jnp.sum(sink_part[rows], axis=0, keepdims=True), (1, HEAD_DIM))

        @pl.when(n == nblk - 1)
        def _():
            bucket_v = bucket_ref[...]
            row = lax.broadcasted_iota(jnp.int32, (N_BUCKETS, HEAD_DIM), 0)
            for h in range(N_HEADS_B):
                acc = dbias_ref[h * BLOCK_Q:(h + 1) * BLOCK_Q, :]
                tot = jnp.zeros((N_BUCKETS, HEAD_DIM), F32)
                for b in range(N_BUCKETS):
                    tot = jnp.where(row == b, jnp.sum(jnp.where(bucket_v == b, acc, 0.0), keepdims=True), tot)
                dtab_ref[h] = tot

    smem = pl.BlockSpec(memory_space=pltpu.SMEM)
    wide = GROUP * HEAD_DIM
    whole = pl.BlockSpec((sp, N_KV_B * HEAD_DIM), lambda n: (0, 0))
    group_b = pl.BlockSpec((BLOCK_Q, N_HEADS_B * HEAD_DIM), lambda n: (n, 1))
    res, c_res = _call(
        body, name="attn_b_bwd", grid=(nblk,),
        in_specs=[
            smem,
            smem,
            pl.BlockSpec((BLOCK_Q, wide), lambda n: (n, COL_QB // GROUP)),
            pl.BlockSpec((BLOCK_Q, wide), lambda n: (n, COL_QB // GROUP + 1)),
            whole,
            whole,
            group_b,
            group_b,
            pl.BlockSpec((N_HEADS_B, BLOCK_Q, HEAD_DIM), lambda n: (0, n, 0)),
            pl.BlockSpec((BLOCK_Q, 3 * BLOCK_Q), lambda n: (0, 0)),
        ],
        out_specs=[
            pl.BlockSpec((BLOCK_Q, N_HEADS_B * HEAD_DIM), lambda n: (n, 0)),
            whole,
            whole,
            pl.BlockSpec((N_HEADS_B, N_BUCKETS, HEAD_DIM), lambda n: (0, 0, 0)),
            pl.BlockSpec((N_HEADS_B, 1, HEAD_DIM), lambda n: (0, 0, 0)),
        ],
        out_shape=[
            jax.ShapeDtypeStruct((s, N_HEADS_B * HEAD_DIM), F32),
            jax.ShapeDtypeStruct((sp, N_KV_B * HEAD_DIM), F32),
            jax.ShapeDtypeStruct((sp, N_KV_B * HEAD_DIM), F32),
            jax.ShapeDtypeStruct((N_HEADS_B, N_BUCKETS, HEAD_DIM), F32),
            jax.ShapeDtypeStruct((N_HEADS_B, 1, HEAD_DIM), F32),
        ],
        scratch_shapes=[pltpu.VMEM((N_KV_B * GQ, 3 * BLOCK_Q), F32), pltpu.VMEM((N_KV_B * GQ, 3 * BLOCK_Q), F32)],
        sem=("arbitrary",),
        args=(table, sink, pb, pb, kpad, vpad, att, datt, lse, bucket), comm=comm, after=after)
    return res if comm is None else (res, c_res)


_MESH = pl.DeviceIdType.MESH


def _other_chips(x, y):
    return [(x, 1 - y), (1 - x, y), (1 - x, 1 - y)]


_HBM = pl.BlockSpec(memory_space=pltpu.HBM)
_SEM = pl.BlockSpec(memory_space=pltpu.SEMAPHORE)
_SPLIT = pltpu.CompilerParams(has_side_effects=pltpu.SideEffectType.DATAFLOW_SIDE_EFFECTING)


def _in_hbm(a):
    return pltpu.with_memory_space_constraint(a, pltpu.HBM)


def _my_half(rows):
    c = lax.axis_index("c")
    half = rows // 2
    return pl.ds(pl.multiple_of(c * half, half), half), pl.ds(pl.multiple_of((1 - c) * half, half), half)


def _gather_route(shapes):
    def route(src, land):
        x, y, c = lax.axis_index("x"), lax.axis_index("y"), lax.axis_index("c")
        out = []
        for t, shape in enumerate(shapes):
            mine, _ = _my_half(shape[0])
            for px, py in _other_chips(x, y):
                out.append((src[t].at[mine], land[t].at[2 * x + y, mine], land[t].at[2 * px + py, mine], (px, py, c)))
        return out

    return route


def _exchange_route(n_t):
    def route(src, land):
        x, y, c = lax.axis_index("x"), lax.axis_index("y"), lax.axis_index("c")
        out = []
        for t in range(n_t):
            for px, py in _other_chips(x, y):
                k = 2 * px + py
                out.append((src[t].at[k], land[t].at[2 * (2 * x + y) + c], land[t].at[2 * k + c], (px, py, c)))
        return out

    return route


def _own_slot(shape, dtype, slot, block):
    return lax.dynamic_update_slice(lax.empty(shape, dtype), block[None], (slot,) + (0,) * (len(shape) - 1))


def _split_start(name, srcs, lands, route, after):
    n = len(srcs)

    def body(*refs):
        src, land, send_sems, recv_sems, token = refs[:n], refs[n:2 * n], refs[2 * n + 1], refs[2 * n + 2], refs[-1]
        for i, (src_ref, dst_ref, _, to) in enumerate(route(src, land)):
            pltpu.make_async_remote_copy(src_ref=src_ref, dst_ref=dst_ref, send_sem=send_sems.at[i],
                                         recv_sem=recv_sems.at[i], device_id=to, device_id_type=_MESH).start()
        token[...] = jnp.zeros_like(token)

    sem = pltpu.SemaphoreType.DMA((3 * n,))
    lands = list(lands)
    res = pl.pallas_call(
        body, name=name,
        in_specs=[_HBM] * (2 * n) + [_ANY],
        out_specs=[_SEM, _SEM] + [_HBM] * (2 * n) + [pl.BlockSpec(memory_space=pltpu.VMEM)],
        out_shape=[sem, sem] + [pltpu.HBM(a.shape, a.dtype) for a in list(srcs) + lands]
        + [jax.ShapeDtypeStruct((8, 128), F32)],
        input_output_aliases={i: 2 + i for i in range(2 * n)},
        compiler_params=_SPLIT,
    )(*[_in_hbm(a) for a in srcs], *[_in_hbm(a) for a in lands], after)
    return (res[0], res[1]), res[2:2 + n], res[2 + n:2 + 2 * n], res[-1]


def _split_wait(name, srcs, lands, sems, route, after):
    n = len(srcs)

    def body(*refs):
        src, land, send_sems, recv_sems = refs[:n], refs[n:2 * n], refs[2 * n], refs[2 * n + 1]
        for i, (src_ref, _, dst_ref, to) in enumerate(route(src, land)):
            cp = pltpu.make_async_remote_copy(src_ref=src_ref, dst_ref=dst_ref, send_sem=send_sems.at[i],
                                              recv_sem=recv_sems.at[i], device_id=to, device_id_type=_MESH)
            cp.wait_send()
            cp.wait_recv()

    res = pl.pallas_call(
        body, name=name,
        in_specs=[_HBM] * (2 * n) + [_SEM, _SEM, _ANY],
        out_specs=[_HBM] * (2 * n),
        out_shape=[pltpu.HBM(a.shape, a.dtype) for a in list(srcs) + list(lands)],
        input_output_aliases={i: i for i in range(2 * n)},
        compiler_params=_SPLIT,
    )(*srcs, *lands, sems[0], sems[1], after)
    return res[:n], res[n:]


def _comm_only(name, comm):
    return _call(lambda: None, name=name, grid=(1,), in_specs=[], out_specs=[], out_shape=[], args=(), comm=comm)[1]


def _swap_comm(shards, lands):
    n_t = len(lands)

    def copies(land, sems, later):
        send_sems, recv_sems = sems
        x, y = lax.axis_index("x"), lax.axis_index("y")
        sends, recvs = [], []
        for t in range(n_t):
            mine, other = _my_half(shards[t].shape[0])
            for j, (px, py) in enumerate(_other_chips(x, y)):
                k = 2 * px + py
                for part, out in ((mine, sends), (other, recvs)) if later else ((mine, sends),):
                    out.append(pltpu.make_async_remote_copy(
                        src_ref=land[t].at[k, part], dst_ref=land[t].at[k, part], send_sem=send_sems.at[3 * t + j],
                        recv_sem=recv_sems.at[3 * t + j], device_id=_sibling(), device_id_type=_MESH))
        return sends, recvs

    def start(ins, land, sems):
        for cp in copies(land, sems, False)[0]:
            cp.start()

    def finish(ins, land, sems):
        sends, recvs = copies(land, sems, True)
        for cp in recvs:
            cp.wait_recv()
        for cp in sends:
            cp.wait_send()

    return _Comm(
        lands, [jax.ShapeDtypeStruct(a.shape, a.dtype) for a in lands],
        [pltpu.SemaphoreType.DMA((3 * n_t,)), pltpu.SemaphoreType.DMA((3 * n_t,))],
        start, finish, aliases={t: t for t in range(n_t)})


def _forward_comm(partials, lands):
    n_t = len(lands)

    def copies(land, sems, later):
        send_sems, recv_sems = sems
        x, y, c = lax.axis_index("x"), lax.axis_index("y"), lax.axis_index("c")
        sends, recvs = [], []
        for t in range(n_t):
            for j, k in enumerate([2 * x + y] + [2 * px + py for px, py in _other_chips(x, y)]):
                for slot, out in ((2 * k + c, sends), (2 * k + 1 - c, recvs)) if later else ((2 * k + c, sends),):
                    out.append(pltpu.make_async_remote_copy(
                        src_ref=land[t].at[slot], dst_ref=land[t].at[slot], send_sem=send_sems.at[4 * t + j],
                        recv_sem=recv_sems.at[4 * t + j], device_id=_sibling(), device_id_type=_MESH))
        return sends, recvs

    def start(ins, land, sems):
        for cp in copies(land, sems, False)[0]:
            cp.start()

    def finish(ins, land, sems):
        sends, recvs = copies(land, sems, True)
        for cp in recvs:
            cp.wait_recv()
        for cp in sends:
            cp.wait_send()

    return _Comm(
        lands, [jax.ShapeDtypeStruct(a.shape, a.dtype) for a in lands],
        [pltpu.SemaphoreType.DMA((4 * n_t,)), pltpu.SemaphoreType.DMA((4 * n_t,))],
        start, finish, aliases={t: t for t in range(n_t)})


def _allreduce_small(pack):
    rows, d = pack.shape

    def body(p_ref, sum_ref, all_ref, send_sems, recv_sems):
        x, y, c = lax.axis_index("x"), lax.axis_index("y"), lax.axis_index("c")
        me = 4 * x + 2 * y + c
        all_ref[me] = p_ref[...]
        peers = []
        for dx in range(2):
            for dy in range(2):
                for dc in range(2):
                    if dx or dy or dc:
                        px = 1 - x if dx else x
                        py = 1 - y if dy else y
                        pc = 1 - c if dc else c
                        peers.append((4 * dx + 2 * dy + dc - 1, (px, py, pc)))
        sends = []
        for k, to in peers:
            cp = pltpu.make_async_remote_copy(
                src_ref=p_ref, dst_ref=all_ref.at[me], send_sem=send_sems.at[k], recv_sem=recv_sems.at[k],
                device_id=to, device_id_type=_MESH)
            cp.start()
            sends.append(cp)
        for k, (px, py, pc) in peers:
            pltpu.make_async_remote_copy(
                src_ref=p_ref, dst_ref=all_ref.at[4 * px + 2 * py + pc], send_sem=send_sems.at[k],
                recv_sem=recv_sems.at[k], device_id=(px, py, pc), device_id_type=_MESH).wait_recv()
        for cp in sends:
            cp.wait_send()
        tot = all_ref[0]
        for i in range(1, N_DEV):
            tot = tot + all_ref[i]
        sum_ref[...] = tot

    vm = pl.BlockSpec(memory_space=pltpu.VMEM)
    return pl.pallas_call(
        body,
        name="allreduce_small",
        in_specs=[vm],
        out_specs=vm,
        out_shape=jax.ShapeDtypeStruct((rows, d), F32),
        scratch_shapes=[
            pltpu.VMEM((N_DEV, rows, d), F32),
            pltpu.SemaphoreType.DMA((N_DEV - 1,)),
            pltpu.SemaphoreType.DMA((N_DEV - 1,)),
        ],
    )(pack)


def _adamw_math(w, g, m, v):
    m = ADAM_B1 * m + (1.0 - ADAM_B1) * g
    v = ADAM_B2 * v + (1.0 - ADAM_B2) * (g * g)
    m_hat = m / (1.0 - ADAM_B1 ** ADAM_STEP)
    v_hat = v / (1.0 - ADAM_B2 ** ADAM_STEP)
    delta = -ADAM_LR * (m_hat / (jnp.sqrt(v_hat) + ADAM_EPS) + ADAM_WD * w)
    return delta, m, v


def _sum_adamw(parts, w, m, v, *, name, tr=256):
    r, c = w.shape
    tr = min(tr, r)
    tc = min(c, 1024)

    def body(p_ref, w_ref, m_ref, v_ref, g_ref, d_ref, m2_ref, v2_ref):
        g = p_ref[0].astype(F32)
        for i in range(1, N_DEV):
            g = g + p_ref[i].astype(F32)
        delta, m2, v2 = _adamw_math(w_ref[...], g, m_ref[...], v_ref[...])
        g_ref[...] = g
        d_ref[...] = delta
        m2_ref[...] = m2
        v2_ref[...] = v2

    blk = pl.BlockSpec((tr, tc), lambda i, j: (i, j))
    return pl.pallas_call(
        body,
        name=name,
        grid=(r // tr, c // tc),
        in_specs=[pl.BlockSpec((N_DEV, tr, tc), lambda i, j: (0, i, j)), blk, blk, blk],
        out_specs=[blk] * 4,
        out_shape=[jax.ShapeDtypeStruct((r, c), F32)] * 4,
        compiler_params=_params(("parallel", "parallel")),
    )(parts, w, m, v)


def _adamw_small(g, w, m, v):
    def body(g_ref, w_ref, m_ref, v_ref, d_ref, m2_ref, v2_ref):
        delta, m2, v2 = _adamw_math(w_ref[...], g_ref[...], m_ref[...], v_ref[...])
        d_ref[...] = delta
        m2_ref[...] = m2
        v2_ref[...] = v2

    vm = pl.BlockSpec(memory_space=pltpu.VMEM)
    return pl.pallas_call(
        body,
        name="adamw_small",
        in_specs=[vm] * 4,
        out_specs=[vm] * 3,
        out_shape=[jax.ShapeDtypeStruct(g.shape, F32)] * 3,
    )(g, w, m, v)


def _relu2_epilogue(acc):
    ra = jnp.maximum(acc, 0.0)
    return ra * ra, ra


def _residual_norm_epilogue(acc, res, g):
    h = acc + res
    return h, h * lax.rsqrt(jnp.mean(h * h, axis=-1, keepdims=True) + EPS) * g


def _rows(stacked):
    return stacked.reshape(stacked.shape[0] * stacked.shape[1], stacked.shape[2])


def _by_chip(mat):
    return mat.reshape(N_CHIPS, mat.shape[0] // N_CHIPS, mat.shape[1])


def _local_step(x, p, target, shards, small, update):
    s, d = x.shape
    cos_t, sin_t = _rope_tables(s)
    bucket = _band_buckets()
    p_bf = p.astype(BF16)
    wts = {}

    chip = 2 * lax.axis_index("x") + lax.axis_index("y")
    core = lax.axis_index("c")

    def gather(tag, names, after):
        srcs = [cast[n] for n in names]
        route = _gather_route([a.shape for a in srcs])
        sems, srcs, lands, token = _split_start(f"gather_start_{tag}", srcs, [zones[n] for n in names], route, after)

        def landed(done):
            got_srcs, got_lands = _split_wait(f"gather_wait_{tag}", srcs, lands, sems, route, done)
            comm = _swap_comm(got_srcs, got_lands)
            comm.waited = got_srcs[0]
            return comm

        return landed, token

    def prepare(n, zero):
        cast[n] = (shards[n] + zero).astype(BF16)
        zones[n] = _own_slot((N_CHIPS,) + cast[n].shape, BF16, chip, cast[n])

    cast, zones = {}, {}
    prepare("w_in", 0.0)
    in_landed, token = gather("in", ["w_in"], small["attn_norm_g"])
    for n in shards:
        if n != "w_in":
            prepare(n, token[:1, :1])
    g_attn = small["attn_norm_g"] + token[:1, :1]
    u = _rms_fwd(x, g_attn, name="norm_attn")
    prepared = u[:1, :1].astype(F32) + sum(
        (lax.dynamic_slice(zones[n], (chip, 0, 0), (1, 1, 1))[0] + cast[n][:1, :1]).astype(F32)
        for n in zones if n != "w_in")
    (wts["w_in"],) = _comm_only("swap_w_in", in_landed(prepared))
    mid_landed, token = gather("mid", ["w_out"], wts["w_in"])
    proj = _matmul(u, wts["w_in"], mode="nn", out_dtypes=[F32], name="mm_in", bn=768, after=token)
    pb, (w_out_s,) = _qk_prep(proj, small["q_norm_g"], small["k_norm_g"], cos_t, sin_t, comm=mid_landed(proj))
    wts["w_out"] = _rows(w_out_s)
    up_landed, token = gather("up", ["w_up"], pb)
    att_a, lse_a = _attn_a_fwd(pb, after=token)
    pad = ((PAD_LO, PAD_HI), (0, 0))
    kpad = jnp.pad(pb[:, COL_KB * HEAD_DIM:COL_VB * HEAD_DIM], pad)
    vpad = jnp.pad(pb[:, COL_VB * HEAD_DIM:], pad)
    up_swap = up_landed(att_a)
    down_landed, token = gather("down", ["w_down"], up_swap.waited)
    (att, lse_b), (wts["w_up"],) = _attn_b_fwd(pb, kpad, vpad, bucket, small["rel_bias_table"],
                                               small["sink_logits"], att_a, comm=up_swap, after=token)
    h1, mn = _matmul(att, wts["w_out"], mode="nn", out_dtypes=[F32, BF16], name="mm_out", bm=512, bn=d,
                     epilogue=_residual_norm_epilogue, extras=(x,), vecs=(small["mlp_norm_g"],))
    r, ra = _matmul(mn, wts["w_up"], mode="nn", out_dtypes=[BF16, BF16], name="mm_up", epilogue=_relu2_epilogue,
                    bm=2048)
    (w_down_s,) = _comm_only("swap_w_down", down_landed(r))
    wts["w_down"] = _rows(w_down_s)
    late_landed, token = gather("late", ["w_gate", "ple_w"], w_down_s)
    h2 = _matmul(r, wts["w_down"], mode="nn", out_dtypes=[F32], name="mm_down",
                 epilogue=lambda acc, res: (acc + res,), extras=(h1,), after=token)
    ng, (w_gate_s, wts["ple_w"]) = _rms_fwd(h2, small["gate_norm_g"], name="norm_gate", comm=late_landed(h2))
    wts["w_gate"] = _rows(w_gate_s)
    gate = _matmul(ng, wts["w_gate"], mode="nn", out_dtypes=[F32], name="mm_gate",
                   epilogue=lambda acc: (1.0 / (1.0 + jnp.exp(-acc)),))
    pp = _matmul(p_bf, wts["ple_w"], mode="nn", out_dtypes=[F32], name="mm_ple", bn=512)
    dh3, dz, dpp, dg_final, dg_ple, loss = _tail(h2, gate, pp, target, small["ple_norm_g"], small["final_norm_g"])

    dng = _matmul(dz, wts["w_gate"], mode="nt", out_dtypes=[F32], name="mm_gate_dx")
    gw_gate = _matmul(ng, dz, mode="tn", out_dtypes=[BF16], name="mm_gate_dw")
    gw_ple = _matmul(p_bf, dpp, mode="tn", out_dtypes=[BF16], name="mm_ple_dw", bn=512, out_stack=N_CHIPS)
    dh2, dh2_bf, dg_gate = _rms_bwd(h2, dng, small["gate_norm_g"], dh3, name="norm_gate_bwd", want_bf16=True)

    def exchange(tag, partials, after):
        route = _exchange_route(len(partials))
        lands = [_own_slot((N_DEV,) + g.shape[1:], g.dtype, 2 * chip + core,
                           lax.dynamic_index_in_dim(g, chip, 0, keepdims=False)) for g in partials]
        sems, srcs, lands, token = _split_start(f"exchange_start_{tag}", partials, lands, route, after)

        def landed(done):
            got_srcs, got_lands = _split_wait(f"exchange_wait_{tag}", srcs, lands, sems, route, done)
            comm = _forward_comm(got_srcs, got_lands)
            comm.waited = got_srcs[0]
            return comm

        return landed, token

    big = {}
    gate_landed, token = exchange("gate", [_by_chip(gw_gate), gw_ple], dh2_bf)
    gw_down = _matmul(r, dh2_bf, mode="tn", out_dtypes=[BF16], name="mm_down_dw", after=token, bm=512, bk=4096)
    da, (parts_gate, parts_ple) = _matmul(
        dh2_bf, wts["w_down"], mode="nt", out_dtypes=[BF16], name="mm_down_dx", bm=2048,
        epilogue=lambda acc, ra_v: (acc * (2.0 * ra_v.astype(F32)),), extras=(ra,), comm=gate_landed(gw_down))
    down_landed, token = exchange("down", [_by_chip(gw_down)], da)
    big["w_gate"], big["ple_w"] = update("w_gate", parts_gate), update("ple_w", parts_ple)
    gw_up = _matmul(mn, da, mode="tn", out_dtypes=[BF16], name="mm_up_dw", out_stack=N_CHIPS, after=token,
                    bm=512, bk=4096)
    dmn = _matmul(da, wts["w_up"], mode="nt", out_dtypes=[F32], name="mm_up_dx")
    dh1, dh1_bf, dg_mlp = _rms_bwd(h1, dmn, small["mlp_norm_g"], dh2, name="norm_mlp_bwd", want_bf16=True)
    datt, (parts_down,) = _matmul(dh1_bf, wts["w_out"], mode="nt", out_dtypes=[BF16], name="mm_out_dx",
                                  comm=down_landed(dh1_bf))
    gw_out = _matmul(att, dh1_bf, mode="tn", out_dtypes=[BF16], name="mm_out_dw")
    up_landed, token = exchange("up", [gw_up], datt)
    dqb, dkpad, dvpad, dtab, dsink = _attn_b_bwd(pb, kpad, vpad, att, datt, lse_b, bucket,
                                                 small["rel_bias_table"], small["sink_logits"], after=token)
    dqa, dka, dva = _attn_a_bwd(pb, att, datt, lse_a)
    up_forward = up_landed(dqa)
    out_landed, token = exchange("out", [_by_chip(gw_out)], up_forward.waited)
    (dproj, dg_q, dg_k), (parts_up,) = _qk_bwd(dqa, dka, dva, dqb, dkpad, dvpad, proj,
                                               small["q_norm_g"], small["k_norm_g"], cos_t, sin_t,
                                               comm=up_forward, after=token)
    gw_in = _matmul(u, dproj, mode="tn", out_dtypes=[BF16], name="mm_in_dw", bn=768, out_stack=N_CHIPS)
    out_forward = out_landed(gw_in)
    in_landed, token = exchange("in", [gw_in], out_forward.waited)
    du, (parts_out,) = _matmul(dproj, wts["w_in"], mode="nt", out_dtypes=[F32], name="mm_in_dx", bk=3072,
                               comm=out_forward, after=token)
    grad_x, dg_attn = _rms_bwd(x, du, small["attn_norm_g"], dh1, name="norm_attn_bwd", want_bf16=False)
    for n, parts in (("w_down", parts_down), ("w_up", parts_up), ("w_out", parts_out)):
        big[n] = update(n, parts)
    done = dg_attn + sum(big[n][0][0, :1, :] for n in ("w_down", "w_up", "w_out"))
    (parts_in,) = _comm_only("forward_w_in", in_landed(done))
    big["w_in"] = update("w_in", parts_in)

    small_g = {
        "attn_norm_g": dg_attn, "mlp_norm_g": dg_mlp, "ple_norm_g": dg_ple, "gate_norm_g": dg_gate,
        "final_norm_g": dg_final, "q_norm_g": dg_q, "k_norm_g": dg_k,
        "sink_logits": dsink[:, 0, 0][None, :], "rel_bias_table": dtab[:, :, 0].T,
    }
    return loss, grad_x, big, small_g


_SMALL_ROWS = ["attn_norm_g", "mlp_norm_g", "ple_norm_g", "gate_norm_g", "final_norm_g"]
_PACK_ROWS = 8


def _pack_small(vals, d):
    rows = [vals[n].reshape(1, d) for n in _SMALL_ROWS]
    misc = jnp.concatenate([
        vals["q_norm_g"].reshape(1, HEAD_DIM), vals["k_norm_g"].reshape(1, HEAD_DIM),
        jnp.pad(vals["sink_logits"].reshape(1, N_HEADS_B), ((0, 0), (0, HEAD_DIM - N_HEADS_B))),
        vals["rel_bias_table"].reshape(1, N_BUCKETS * N_HEADS_B)], axis=1)
    rows.append(jnp.pad(misc, ((0, 0), (0, d - misc.shape[1]))))
    rows.append(jnp.zeros((_PACK_ROWS - len(rows), d), F32))
    return jnp.concatenate(rows, axis=0).astype(F32)


def _unpack_small(pack, shapes):
    out = {n: pack[i].reshape(shapes[n]) for i, n in enumerate(_SMALL_ROWS)}
    misc = pack[len(_SMALL_ROWS)]
    out["q_norm_g"] = misc[:HEAD_DIM].reshape(shapes["q_norm_g"])
    out["k_norm_g"] = misc[HEAD_DIM:2 * HEAD_DIM].reshape(shapes["k_norm_g"])
    out["sink_logits"] = misc[2 * HEAD_DIM:2 * HEAD_DIM + N_HEADS_B].reshape(shapes["sink_logits"])
    out["rel_bias_table"] = misc[3 * HEAD_DIM:3 * HEAD_DIM + N_BUCKETS * N_HEADS_B].reshape(shapes["rel_bias_table"])
    return out


_WEIGHTS = ["attn_norm_g", "w_in", "q_norm_g", "k_norm_g", "sink_logits", "w_out", "mlp_norm_g", "w_up", "w_down",
            "ple_w", "ple_norm_g", "gate_norm_g", "w_gate", "rel_bias_table", "final_norm_g"]
_BIG = ["w_in", "w_out", "w_up", "w_down", "ple_w", "w_gate"]


def kernel(x, p, attn_norm_g, w_in, q_norm_g, k_norm_g, sink_logits, w_out, mlp_norm_g, w_up, w_down, ple_w, ple_norm_g, gate_norm_g, w_gate, rel_bias_table, final_norm_g, loss_target, m_attn_norm_g, m_w_in, m_q_norm_g, m_k_norm_g, m_sink_logits, m_w_out, m_mlp_norm_g, m_w_up, m_w_down, m_ple_w, m_ple_norm_g, m_gate_norm_g, m_w_gate, m_rel_bias_table, m_final_norm_g, v_attn_norm_g, v_w_in, v_q_norm_g, v_k_norm_g, v_sink_logits, v_w_out, v_mlp_norm_g, v_w_up, v_w_down, v_ple_w, v_ple_norm_g, v_gate_norm_g, v_w_gate, v_rel_bias_table, v_final_norm_g):
    given = dict(locals())
    w = {n: given[n] for n in _WEIGHTS}
    m = {n: given["m_" + n] for n in _WEIGHTS}
    v = {n: given["v_" + n] for n in _WEIGHTS}
    d = x.shape[-1]

    shards = {n: w[n][0] for n in _BIG}
    small = {
        "attn_norm_g": w["attn_norm_g"], "mlp_norm_g": w["mlp_norm_g"], "ple_norm_g": w["ple_norm_g"],
        "gate_norm_g": w["gate_norm_g"], "final_norm_g": w["final_norm_g"].reshape(1, d),
        "q_norm_g": w["q_norm_g"], "k_norm_g": w["k_norm_g"], "sink_logits": w["sink_logits"],
        "rel_bias_table": w["rel_bias_table"],
    }

    def update(n, parts):
        res = _sum_adamw(parts, w[n][0], m[n][0], v[n][0], name="adamw_" + n)
        return [t.reshape(w[n].shape) for t in res]

    loss_part, grad_x, big, small_g = _local_step(x[0], p[0, 0], loss_target[0], shards, small, update)
    grads, deltas, new_m, new_v = [{n: big[n][i] for n in _BIG} for i in range(4)]

    shapes = {n: w[n].shape for n in _WEIGHTS if n not in _BIG}
    pack = _pack_small(small_g, d)
    pack = pack.at[_PACK_ROWS - 1, :1].add(0.0 * grads["w_in"][0, 0, :1])
    pack = pack.at[_PACK_ROWS - 1, 1].set(loss_part[0, 0])
    g_small = _allreduce_small(pack)
    loss = g_small[_PACK_ROWS - 1, 1]
    d_small, m_small, v_small = _adamw_small(g_small, _pack_small(w, d), _pack_small(m, d), _pack_small(v, d))
    grads.update(_unpack_small(g_small, shapes))
    deltas.update(_unpack_small(d_small, shapes))
    new_m.update(_unpack_small(m_small, shapes))
    new_v.update(_unpack_small(v_small, shapes))

    return (loss, grad_x[None], *[grads[n] for n in _WEIGHTS], *[deltas[n] for n in _WEIGHTS],
            *[new_m[n] for n in _WEIGHTS], *[new_v[n] for n in _WEIGHTS])
```

```python
import functools
import math

import jax
import jax.numpy as jnp
from jax import lax
from jax.experimental import pallas as pl
from jax.experimental.pallas import tpu as pltpu

F32 = jnp.float32
BF16 = jnp.bfloat16

HEAD_DIM = 128
N_HEADS_A = 8
N_KV_A = 2
N_HEADS_B = 8
N_KV_B = 2
GROUP = 4
GRID_W = 64
BLOCK_Q = 128
WINDOW = 128
N_BUCKETS = 32
MAX_DISTANCE = 128
ROPE_THETA = 10000.0
EPS = 1e-6
NEG_INF = -1e30
ATT_SCALE = HEAD_DIM ** -0.5
LOG2E = math.log2(math.e)
LN2 = math.log(2.0)
Q_SCALE = ATT_SCALE * LOG2E
PAD_LO, PAD_HI = 256, 128
ADAM_LR = 0.001
ADAM_B1 = 0.9
ADAM_B2 = 0.999
ADAM_EPS = 1e-08
ADAM_WD = 0.01
ADAM_STEP = 10

N_CHIPS = 4
N_DEV = 8
COL_QA, COL_KA, COL_VA, COL_QB, COL_KB, COL_VB = 0, 8, 10, 12, 20, 22

VMEM_LIMIT = 52 * 1024 * 1024


def _params(sem=None, collective_id=None):
    return pltpu.CompilerParams(dimension_semantics=sem, vmem_limit_bytes=VMEM_LIMIT, collective_id=collective_id)


_ANY = pl.BlockSpec(memory_space=pl.ANY)
_MESH = pl.DeviceIdType.MESH
SIBLING_BARRIER_ID = 1


def _sibling():
    return (lax.axis_index("x"), lax.axis_index("y"), 1 - lax.axis_index("c"))


class _Comm:
    def __init__(self, inputs, out_shapes, sems, start, finish, aliases=None):
        self.inputs, self.out_shapes, self.sems = list(inputs), list(out_shapes), list(sems)
        self.start, self.finish, self.aliases = start, finish, dict(aliases or {})


def _call(body, *, name, grid, in_specs, out_specs, out_shape, args, scratch_shapes=(), sem=None, comm=None,
          after=None, aliases=None):
    in_specs, out_specs, out_shape = list(in_specs), list(out_specs), list(out_shape)
    scratch_shapes = list(scratch_shapes)
    n_in, n_out, n_sc = len(in_specs), len(out_specs), len(scratch_shapes)
    behind = [] if after is None else [after]
    aliases = dict(aliases or {})
    if comm is None:
        res = pl.pallas_call(
            (lambda *refs: body(*refs[:n_in], *refs[n_in + len(behind):])) if behind else body,
            name=name, grid=grid, in_specs=in_specs + [_ANY] * len(behind), out_specs=out_specs,
            out_shape=out_shape, scratch_shapes=scratch_shapes, input_output_aliases=aliases,
            compiler_params=_params(sem))(*args, *behind)
        return list(res), []
    c_in, c_out = len(comm.inputs), len(comm.out_shapes)

    def hosted(*refs):
        pos = [0]

        def take(n):
            pos[0] += n
            return refs[pos[0] - n:pos[0]]

        ins, c_ins, _, outs, c_outs, scr = (take(n_in), take(c_in), take(len(behind)), take(n_out), take(c_out),
                                            take(n_sc))
        c_sems = refs[pos[0]:]
        ids = [pl.program_id(a) for a in range(len(grid))]
        first = functools.reduce(jnp.logical_and, [i == 0 for i in ids])
        last = functools.reduce(jnp.logical_and, [i == g - 1 for i, g in zip(ids, grid)])

        @pl.when(first)
        def _():
            barrier = pltpu.get_barrier_semaphore()
            pl.semaphore_signal(barrier, inc=1, device_id=_sibling(), device_id_type=_MESH)
            pl.semaphore_wait(barrier, 1)
            comm.start(c_ins, c_outs, c_sems)

        body(*ins, *outs, *scr)

        @pl.when(last)
        def _():
            comm.finish(c_ins, c_outs, c_sems)

    res = pl.pallas_call(
        hosted, name=name, grid=grid, in_specs=in_specs + [_ANY] * (c_in + len(behind)),
        out_specs=out_specs + [_ANY] * c_out,
        out_shape=out_shape + comm.out_shapes, scratch_shapes=scratch_shapes + comm.sems,
        input_output_aliases={**aliases, **{n_in + i: n_out + o for i, o in comm.aliases.items()}},
        compiler_params=_params(("arbitrary",) * len(grid), SIBLING_BARRIER_ID))(*args, *comm.inputs, *behind)
    return list(res[:n_out]), list(res[n_out:])


def _matmul(a, b, *, mode, out_dtypes, name, epilogue=None, extras=(), bm=1024, bn=1024, bk=2048,
            out_stack=0, comm=None, after=None, vecs=()):
    stacked = b.ndim == 3
    if mode == "nn":
        m, k = a.shape
        if stacked:
            nj, kb, ns = b.shape
            n, ks = nj * ns, k
        else:
            kb, n = b.shape
            ns, ks = n, k
        dn = (((1,), (0,)), ((), ()))
    elif mode == "nt":
        m, k = a.shape
        if stacked:
            nj, n, ks = b.shape
            kb = nj * ks
        else:
            n, kb = b.shape
            ks = kb
        ns = n
        dn = (((1,), (1,)), ((), ()))
    else:
        k, m = a.shape
        kb, n = b.shape
        ns, ks = n, k
        dn = (((0,), (0,)), ((), ()))
    assert k == kb and not (stacked and mode == "tn")
    ns_out = n // out_stack if out_stack else n
    whole_k = stacked and mode == "nt" and bk >= k
    bm, bn, bk = min(bm, m), min(bn, ns, ns_out), k if whole_k else min(bk, ks)
    assert m % bm == 0 and ns % bn == 0 and ns_out % bn == 0 and ks % bk == 0 or whole_k
    gm, gn, gk = m // bm, n // bn, k // bk

    if mode == "tn":
        a_spec = pl.BlockSpec((bk, bm), lambda i, j, q: (q, i))
    else:
        a_spec = pl.BlockSpec((bm, bk), lambda i, j, q: (i, q))
    if mode == "nt":
        if whole_k:
            b_spec = pl.BlockSpec((nj, bn, ks), lambda i, j, q: (0, j, 0))
        elif stacked:
            per = ks // bk
            b_spec = pl.BlockSpec((None, bn, bk), lambda i, j, q: (q // per, j, q % per))
        else:
            b_spec = pl.BlockSpec((bn, bk), lambda i, j, q: (j, q))
    else:
        if stacked:
            per = ns // bn
            b_spec = pl.BlockSpec((None, bk, bn), lambda i, j, q: (j // per, q, j % per))
        else:
            b_spec = pl.BlockSpec((bk, bn), lambda i, j, q: (q, j))
    ex_spec = pl.BlockSpec((bm, bn), lambda i, j, q: (i, j))
    if out_stack:
        per_o = ns_out // bn
        o_spec = pl.BlockSpec((None, bm, bn), lambda i, j, q: (j // per_o, i, j % per_o))
        o_shape = (out_stack, m, ns_out)
    else:
        o_spec = ex_spec
        o_shape = (m, n)
    n_ex, n_out = len(extras) + len(vecs), len(out_dtypes)

    def body(a_ref, b_ref, *rest):
        ex, outs = rest[:n_ex], rest[n_ex:n_ex + n_out]
        if whole_k:
            part = sum(lax.dot_general(a_ref[:, t * ks:(t + 1) * ks], b_ref[t], dn, preferred_element_type=F32)
                       for t in range(nj))
        else:
            part = lax.dot_general(a_ref[...], b_ref[...], dn, preferred_element_type=F32)

        def finish(acc):
            res = epilogue(acc, *[e[...] for e in ex]) if epilogue else (acc,)
            for o, r in zip(outs, res):
                o[...] = r.astype(o.dtype)

        if gk == 1:
            finish(part)
        else:
            acc_ref = rest[-1]
            q = pl.program_id(2)

            @pl.when(q == 0)
            def _():
                acc_ref[...] = part

            @pl.when(q > 0)
            def _():
                acc_ref[...] += part

            @pl.when(q == gk - 1)
            def _():
                finish(acc_ref[...])

    res, c_res = _call(
        body, name=name, grid=(gm, gn, gk),
        in_specs=[a_spec, b_spec] + [ex_spec] * len(extras)
        + [pl.BlockSpec((1, bn), lambda i, j, q: (0, j))] * len(vecs),
        out_specs=[o_spec] * n_out,
        out_shape=[jax.ShapeDtypeStruct(o_shape, dt) for dt in out_dtypes],
        scratch_shapes=[pltpu.VMEM((bm, bn), F32)] if gk > 1 else [],
        sem=("parallel", "parallel", "arbitrary"), args=(a, b, *extras, *vecs), comm=comm, after=after)
    res = res[0] if n_out == 1 else res
    return res if comm is None else (res, c_res)


def _rms_fwd(x, g, *, name, tm=256, comm=None):
    s, d = x.shape
    tm = min(tm, s)

    def body(x_ref, g_ref, o_ref):
        xf = x_ref[...]
        r = lax.rsqrt(jnp.mean(xf * xf, axis=-1, keepdims=True) + EPS)
        o_ref[...] = (xf * r * g_ref[...]).astype(o_ref.dtype)

    res, c_res = _call(
        body, name=name, grid=(s // tm,),
        in_specs=[pl.BlockSpec((tm, d), lambda i: (i, 0)), pl.BlockSpec((1, d), lambda i: (0, 0))],
        out_specs=[pl.BlockSpec((tm, d), lambda i: (i, 0))],
        out_shape=[jax.ShapeDtypeStruct((s, d), BF16)],
        sem=("parallel",), args=(x, g), comm=comm)
    return res[0] if comm is None else (res[0], c_res)


def _rms_bwd(x, dy, g, add, *, name, want_bf16, tm=256):
    s, d = x.shape
    tm = min(tm, s)

    def body(x_ref, dy_ref, g_ref, add_ref, dx_ref, *rest):
        dg_ref = rest[-1]
        i = pl.program_id(0)
        xf = x_ref[...]
        dyf = dy_ref[...].astype(F32)
        r = lax.rsqrt(jnp.mean(xf * xf, axis=-1, keepdims=True) + EPS)
        xh = xf * r
        dyg = dyf * g_ref[...]
        dx = r * (dyg - xh * jnp.mean(dyg * xh, axis=-1, keepdims=True))
        tot = add_ref[...] + dx
        dx_ref[...] = tot
        if want_bf16:
            rest[0][...] = tot.astype(BF16)
        part = jnp.sum(dyf * xh, axis=0, keepdims=True)

        @pl.when(i == 0)
        def _():
            dg_ref[...] = part

        @pl.when(i > 0)
        def _():
            dg_ref[...] += part

    row = pl.BlockSpec((tm, d), lambda i: (i, 0))
    vec = pl.BlockSpec((1, d), lambda i: (0, 0))
    out_specs = [row] + ([row] if want_bf16 else []) + [vec]
    out_shape = [jax.ShapeDtypeStruct((s, d), F32)]
    if want_bf16:
        out_shape.append(jax.ShapeDtypeStruct((s, d), BF16))
    out_shape.append(jax.ShapeDtypeStruct((1, d), F32))
    return pl.pallas_call(
        body,
        name=name,
        grid=(s // tm,),
        in_specs=[row, row, vec, row],
        out_specs=out_specs,
        out_shape=out_shape,
        compiler_params=_params(("arbitrary",)),
    )(x, dy, g, add)


def _tail(h2, gate, pp, target, g_ple, g_final, *, tm=256):
    s, d = h2.shape
    tm = min(tm, s)

    def body(h2_ref, gate_ref, pp_ref, t_ref, gp_ref, gf_ref, dh3_ref, dz_ref, dpp_ref, dgf_ref, dgp_ref, loss_ref):
        i = pl.program_id(0)
        ppf = pp_ref[...]
        gate_v = gate_ref[...]
        r_p = lax.rsqrt(jnp.mean(ppf * ppf, axis=-1, keepdims=True) + EPS)
        eh = ppf * r_p
        e = eh * gp_ref[...]
        h3 = h2_ref[...] + gate_v * e
        r_f = lax.rsqrt(jnp.mean(h3 * h3, axis=-1, keepdims=True) + EPS)
        yh = h3 * r_f
        diff = yh * gf_ref[...] - t_ref[...]
        loss_part = 0.5 * jnp.sum(jnp.mean(diff * diff, axis=-1, keepdims=True), axis=0, keepdims=True)
        dy = diff / d
        dgf = jnp.sum(dy * yh, axis=0, keepdims=True)
        dyg = dy * gf_ref[...]
        dh3 = r_f * (dyg - yh * jnp.mean(dyg * yh, axis=-1, keepdims=True))
        dh3_ref[...] = dh3
        de = dh3 * gate_v
        dz_ref[...] = (dh3 * e * gate_v * (1.0 - gate_v)).astype(BF16)
        dgp = jnp.sum(de * eh, axis=0, keepdims=True)
        deg = de * gp_ref[...]
        dpp_ref[...] = (r_p * (deg - eh * jnp.mean(deg * eh, axis=-1, keepdims=True))).astype(BF16)
        loss_row = jnp.broadcast_to(loss_part, (1, 128))

        @pl.when(i == 0)
        def _():
            dgf_ref[...] = dgf
            dgp_ref[...] = dgp
            loss_ref[...] = loss_row

        @pl.when(i > 0)
        def _():
            dgf_ref[...] += dgf
            dgp_ref[...] += dgp
            loss_ref[...] += loss_row

    row = pl.BlockSpec((tm, d), lambda i: (i, 0))
    vec = pl.BlockSpec((1, d), lambda i: (0, 0))
    return pl.pallas_call(
        body,
        name="tail_fwd_bwd",
        grid=(s // tm,),
        in_specs=[row, row, row, row, vec, vec],
        out_specs=[row, row, row, vec, vec, pl.BlockSpec((1, 128), lambda i: (0, 0))],
        out_shape=[
            jax.ShapeDtypeStruct((s, d), F32),
            jax.ShapeDtypeStruct((s, d), BF16),
            jax.ShapeDtypeStruct((s, d), BF16),
            jax.ShapeDtypeStruct((1, d), F32),
            jax.ShapeDtypeStruct((1, d), F32),
            jax.ShapeDtypeStruct((1, 128), F32),
        ],
        compiler_params=_params(("arbitrary",)),
    )(h2, gate, pp, target, g_ple, g_final)


def _rope_tables(s):
    rows = s // GRID_W
    half = HEAD_DIM // 2
    inv_freq = ROPE_THETA ** (-jnp.arange(0, half, 2, dtype=F32) / half)
    ang_r = jnp.arange(rows, dtype=jnp.int32).astype(F32)[:, None] * inv_freq
    ang_c = jnp.arange(GRID_W, dtype=jnp.int32).astype(F32)[:, None] * inv_freq
    cr, sr = (jnp.repeat(t, GRID_W, axis=0) for t in (jnp.cos(ang_r), jnp.sin(ang_r)))
    cc, sc = (jnp.tile(t, (rows, 1)) for t in (jnp.cos(ang_c), jnp.sin(ang_c)))
    cos_t = jnp.concatenate([cr, cr, cc, cc], axis=-1)
    sin_t = jnp.concatenate([-sr, sr, -sc, sc], axis=-1)
    return cos_t, sin_t


def _swap_quarters(x):
    lane = lax.broadcasted_iota(jnp.int32, x.shape, x.ndim - 1)
    up = pltpu.roll(x, HEAD_DIM - 32, x.ndim - 1)
    down = pltpu.roll(x, 32, x.ndim - 1)
    return jnp.where((lane % 64) < 32, up, down)


def _cols(first, count=1):
    return slice(first * HEAD_DIM, (first + count) * HEAD_DIM)


def _qk_prep(proj, g_q, g_k, cos_t, sin_t, *, tm=256, comm=None):
    s, n = proj.shape
    tm = min(tm, s)

    def body(x_ref, gq_ref, gk_ref, c_ref, s_ref, o_ref):
        cos_v, sin_v = c_ref[...], s_ref[...]
        for h in range(COL_VA):
            x = x_ref[:, _cols(h)]
            g = gq_ref[...] if h < COL_KA else gk_ref[...]
            xn = x * lax.rsqrt(jnp.mean(x * x, axis=-1, keepdims=True) + EPS) * g
            xr = xn * cos_v + _swap_quarters(xn) * sin_v
            if h < COL_KA:
                xr = xr * Q_SCALE
            o_ref[:, _cols(h)] = xr.astype(BF16)
        o_ref[:, _cols(COL_VA, 2)] = x_ref[:, _cols(COL_VA, 2)].astype(BF16)
        o_ref[:, _cols(COL_QB, N_HEADS_B)] = (x_ref[:, _cols(COL_QB, N_HEADS_B)] * Q_SCALE).astype(BF16)
        o_ref[:, _cols(COL_KB, 4)] = x_ref[:, _cols(COL_KB, 4)].astype(BF16)

    row = pl.BlockSpec((tm, n), lambda i: (i, 0))
    tab = pl.BlockSpec((tm, HEAD_DIM), lambda i: (i, 0))
    vec = pl.BlockSpec((1, HEAD_DIM), lambda i: (0, 0))
    res, c_res = _call(
        body, name="qk_prep", grid=(s // tm,),
        in_specs=[row, vec, vec, tab, tab],
        out_specs=[row],
        out_shape=[jax.ShapeDtypeStruct((s, n), BF16)],
        sem=("parallel",), args=(proj, g_q, g_k, cos_t, sin_t), comm=comm)
    return res[0] if comm is None else (res[0], c_res)


def _qk_bwd(dqa, dka, dva, dqb, dkpad, dvpad, proj, g_q, g_k, cos_t, sin_t, *, comm=None, after=None):
    s, n = proj.shape
    tm = min(PAD_LO, s)
    assert PAD_LO % tm == 0
    lo = PAD_LO // tm

    def body(dqa_ref, dka_ref, dva_ref, dqb_ref, dkb_ref, dvb_ref, x_ref, gq_ref, gk_ref, c_ref, s_ref,
             o_ref, dgq_ref, dgk_ref):
        i = pl.program_id(0)
        cos_v, sin_v = c_ref[...], s_ref[...]

        def head(d, x, g):
            dn = d * cos_v + _swap_quarters(d * sin_v)
            r = lax.rsqrt(jnp.mean(x * x, axis=-1, keepdims=True) + EPS)
            xh = x * r
            dng = dn * g
            dx = r * (dng - xh * jnp.mean(dng * xh, axis=-1, keepdims=True))
            return dx.astype(BF16), jnp.sum(dn * xh, axis=0, keepdims=True)

        acc_q = jnp.zeros((1, HEAD_DIM), F32)
        acc_k = jnp.zeros((1, HEAD_DIM), F32)
        for h in range(N_HEADS_A):
            o_ref[:, _cols(h)], part = head(dqa_ref[:, _cols(h)] * ATT_SCALE, x_ref[:, _cols(h)], gq_ref[...])
            acc_q = acc_q + part
        for h in range(N_KV_A):
            o_ref[:, _cols(COL_KA + h)], part = head(dka_ref[:, _cols(h)] * LN2, x_ref[:, _cols(COL_KA + h)],
                                                     gk_ref[...])
            acc_k = acc_k + part
        o_ref[:, _cols(COL_VA, 2)] = dva_ref[...].astype(BF16)
        o_ref[:, _cols(COL_QB, N_HEADS_B)] = (dqb_ref[...] * ATT_SCALE).astype(BF16)
        o_ref[:, _cols(COL_KB, 2)] = (dkb_ref[...] * LN2).astype(BF16)
        o_ref[:, _cols(COL_VB, 2)] = dvb_ref[...].astype(BF16)

        @pl.when(i == 0)
        def _():
            dgq_ref[...] = acc_q
            dgk_ref[...] = acc_k

        @pl.when(i > 0)
        def _():
            dgq_ref[...] += acc_q
            dgk_ref[...] += acc_k

    def rows(width, shift=0):
        return pl.BlockSpec((tm, width), lambda i: (i + shift, 0))

    kv_w = N_KV_A * HEAD_DIM
    q_w = N_HEADS_A * HEAD_DIM
    vec = pl.BlockSpec((1, HEAD_DIM), lambda i: (0, 0))
    res, c_res = _call(
        body, name="qk_bwd", grid=(s // tm,),
        in_specs=[rows(q_w), rows(kv_w), rows(kv_w), rows(q_w), rows(kv_w, lo), rows(kv_w, lo), rows(n),
                  vec, vec, rows(HEAD_DIM), rows(HEAD_DIM)],
        out_specs=[rows(n), vec, vec],
        out_shape=[
            jax.ShapeDtypeStruct((s, n), BF16),
            jax.ShapeDtypeStruct((1, HEAD_DIM), F32),
            jax.ShapeDtypeStruct((1, HEAD_DIM), F32),
        ],
        sem=("arbitrary",), args=(dqa, dka, dva, dqb, dkpad, dvpad, proj, g_q, g_k, cos_t, sin_t), comm=comm,
        after=after)
    return res if comm is None else (res, c_res)


_NT = (((1,), (1,)), ((), ()))
_TN = (((0,), (0,)), ((), ()))


def _attn_a_fwd(pb, *, tq=4096, sub=256, comm=None, after=None):
    s = pb.shape[0]
    tq = min(tq, s)
    sub = min(sub, tq)

    def body(q_ref, k_ref, v_ref, o_ref, lse_ref):
        k = k_ref[...]
        v = v_ref[...]
        for r in range(tq // sub):
            rows = pl.ds(r * sub, sub)
            sc = lax.dot_general(q_ref[rows, :], k, _NT, preferred_element_type=F32)
            m = jnp.max(sc, axis=-1, keepdims=True)
            p = jnp.exp2(sc - m)
            l = jnp.sum(p, axis=-1, keepdims=True)
            o = jnp.dot(p.astype(BF16), v, preferred_element_type=F32)
            o_ref[rows, :] = (o / l).astype(BF16)
            lse_ref[rows, :] = jnp.broadcast_to(m + jnp.log2(l), (sub, HEAD_DIM))

    res, c_res = _call(
        body, name="attn_a_fwd", grid=(N_HEADS_A, s // tq),
        in_specs=[
            pl.BlockSpec((tq, HEAD_DIM), lambda h, i: (i, COL_QA + h)),
            pl.BlockSpec((s, HEAD_DIM), lambda h, i: (0, COL_KA + h // GROUP)),
            pl.BlockSpec((s, HEAD_DIM), lambda h, i: (0, COL_VA + h // GROUP)),
        ],
        out_specs=[
            pl.BlockSpec((tq, HEAD_DIM), lambda h, i: (i, h)),
            pl.BlockSpec((None, tq, HEAD_DIM), lambda h, i: (h, i, 0)),
        ],
        out_shape=[
            jax.ShapeDtypeStruct((s, (N_HEADS_A + N_HEADS_B) * HEAD_DIM), BF16),
            jax.ShapeDtypeStruct((N_HEADS_A, s, HEAD_DIM), F32),
        ],
        sem=("parallel", "parallel"), args=(pb, pb, pb), comm=comm, after=after)
    return res if comm is None else (res, c_res)


def _attn_a_bwd(pb, att, datt, lse, *, tq=1024, sub=256, comm=None):
    s = pb.shape[0]
    tq = min(tq, s)
    sub = min(sub, tq)

    def body(q_ref, k_ref, v_ref, o_ref, do_ref, lse_ref, dq_ref, dk_ref, dv_ref):
        first = jnp.logical_and(pl.program_id(1) == 0, pl.program_id(2) == 0)
        k = k_ref[...]
        v = v_ref[...]
        dk = dv = None
        for r in range(tq // sub):
            rows = pl.ds(r * sub, sub)
            q = q_ref[rows, :]
            do = do_ref[rows, :]
            sc = lax.dot_general(q, k, _NT, preferred_element_type=F32)
            p = jnp.exp2(sc - lse_ref[rows, :][:, :1])
            dp = lax.dot_general(do, v, _NT, preferred_element_type=F32)
            delta = jnp.sum(do.astype(F32) * o_ref[rows, :].astype(F32), axis=-1, keepdims=True)
            ds = (p * (dp - delta)).astype(BF16)
            dq_ref[rows, :] = jnp.dot(ds, k, preferred_element_type=F32)
            dk_r = lax.dot_general(ds, q, _TN, preferred_element_type=F32)
            dv_r = lax.dot_general(p.astype(BF16), do, _TN, preferred_element_type=F32)
            dk = dk_r if dk is None else dk + dk_r
            dv = dv_r if dv is None else dv + dv_r

        @pl.when(first)
        def _():
            dk_ref[...] = dk
            dv_ref[...] = dv

        @pl.when(jnp.logical_not(first))
        def _():
            dk_ref[...] += dk
            dv_ref[...] += dv

    qmap = lambda kv, g, i: (i, kv * GROUP + g)
    res, c_res = _call(
        body, name="attn_a_bwd", grid=(N_KV_A, GROUP, s // tq),
        in_specs=[
            pl.BlockSpec((tq, HEAD_DIM), lambda kv, g, i: (i, COL_QA + kv * GROUP + g)),
            pl.BlockSpec((s, HEAD_DIM), lambda kv, g, i: (0, COL_KA + kv)),
            pl.BlockSpec((s, HEAD_DIM), lambda kv, g, i: (0, COL_VA + kv)),
            pl.BlockSpec((tq, HEAD_DIM), qmap),
            pl.BlockSpec((tq, HEAD_DIM), qmap),
            pl.BlockSpec((None, tq, HEAD_DIM), lambda kv, g, i: (kv * GROUP + g, i, 0)),
        ],
        out_specs=[
            pl.BlockSpec((tq, HEAD_DIM), qmap),
            pl.BlockSpec((s, HEAD_DIM), lambda kv, g, i: (0, kv)),
            pl.BlockSpec((s, HEAD_DIM), lambda kv, g, i: (0, kv)),
        ],
        out_shape=[
            jax.ShapeDtypeStruct((s, N_HEADS_A * HEAD_DIM), F32),
            jax.ShapeDtypeStruct((s, N_KV_A * HEAD_DIM), F32),
            jax.ShapeDtypeStruct((s, N_KV_A * HEAD_DIM), F32),
        ],
        sem=("arbitrary", "arbitrary", "arbitrary"), args=(pb, pb, pb, att, datt, lse), comm=comm)
    return res if comm is None else (res, c_res)


def _t5_bucket(rel):
    nb = N_BUCKETS // 2
    ret = jnp.where(rel > 0, nb, 0)
    n = jnp.abs(rel)
    max_exact = nb // 2
    nf = jnp.maximum(n, 1).astype(F32)
    large = max_exact + (jnp.log(nf / max_exact) / math.log(MAX_DISTANCE / max_exact)
                         * (nb - max_exact)).astype(jnp.int32)
    large = jnp.minimum(large, nb - 1)
    return ret + jnp.where(n < max_exact, n, large)


def _band_buckets():
    r = jnp.arange(BLOCK_Q, dtype=jnp.int32)
    j = jnp.arange(3 * BLOCK_Q, dtype=jnp.int32)
    return _t5_bucket((j[None, :] - BLOCK_Q) - r[:, None])


def _band_bias(bucket, table_ref, h):
    acc = jnp.zeros(bucket.shape, F32)
    for b in range(N_BUCKETS):
        acc = jnp.where(bucket == b, table_ref[b, h], acc)
    return acc


GQ = GROUP * BLOCK_Q


def _stack_heads(x):
    return jnp.concatenate([x[:, _cols(g)] for g in range(GROUP)], axis=0)


def _unstack_heads(x):
    return jnp.concatenate([x[g * BLOCK_Q:(g + 1) * BLOCK_Q] for g in range(GROUP)], axis=1)


def _group_bias(bucket, table_ref, kv):
    return jnp.concatenate([_band_bias(bucket, table_ref, kv * GROUP + g) * LOG2E for g in range(GROUP)], axis=0)


def _group_sink(sink_ref, kv):
    head = lax.broadcasted_iota(jnp.int32, (GQ, 1), 0) // BLOCK_Q
    snk = jnp.zeros((GQ, 1), F32)
    for g in range(GROUP):
        snk = jnp.where(head == g, sink_ref[0, kv * GROUP + g] * LOG2E, snk)
    return snk


def _band_mask(n, s):
    r = lax.broadcasted_iota(jnp.int32, (GQ, 3 * BLOCK_Q), 0) % BLOCK_Q
    j = lax.broadcasted_iota(jnp.int32, (GQ, 3 * BLOCK_Q), 1)
    rel = j - BLOCK_Q - r
    kabs = n * BLOCK_Q + j - BLOCK_Q
    return (jnp.abs(rel) <= WINDOW) & (kabs >= 0) & (kabs < s)


def _band_start(n):
    return pl.multiple_of(n * BLOCK_Q + (PAD_LO - BLOCK_Q), BLOCK_Q)


def _attn_b_fwd(pb, kpad, vpad, bucket, table, sink, att, *, comm=None, after=None):
    s = pb.shape[0]
    nblk = s // BLOCK_Q
    sp = kpad.shape[0]

    def body(table_ref, sink_ref, q0_ref, q1_ref, k_ref, v_ref, bucket_ref, _, o_ref, lse_ref, bias_ref):
        n = pl.program_id(0)

        @pl.when(n == 0)
        def _():
            for kv in range(N_KV_B):
                bias_ref[kv * GQ:(kv + 1) * GQ, :] = _group_bias(bucket_ref[...], table_ref, kv)

        band = pl.ds(_band_start(n), 3 * BLOCK_Q)
        mask = _band_mask(n, s)
        for kv, q_ref in enumerate((q0_ref, q1_ref)):
            kb = k_ref[band, _cols(kv)]
            vb = v_ref[band, _cols(kv)]
            sc = lax.dot_general(_stack_heads(q_ref[...]), kb, _NT, preferred_element_type=F32)
            sc = jnp.where(mask, sc + bias_ref[kv * GQ:(kv + 1) * GQ, :], NEG_INF)
            snk = _group_sink(sink_ref, kv)
            m = jnp.maximum(jnp.max(sc, axis=-1, keepdims=True), snk)
            p = jnp.exp2(sc - m)
            l = jnp.sum(p, axis=-1, keepdims=True) + jnp.exp2(snk - m)
            o = jnp.dot(p.astype(BF16), vb, preferred_element_type=F32)
            o_ref[:, _cols(kv * GROUP, GROUP)] = _unstack_heads((o / l).astype(BF16))
            lse = m + jnp.log2(l)
            for g in range(GROUP):
                lse_ref[kv * GROUP + g] = jnp.broadcast_to(lse[g * BLOCK_Q:(g + 1) * BLOCK_Q], (BLOCK_Q, HEAD_DIM))

    smem = pl.BlockSpec(memory_space=pltpu.SMEM)
    wide = GROUP * HEAD_DIM
    whole = pl.BlockSpec((sp, N_KV_B * HEAD_DIM), lambda n: (0, 0))
    res, c_res = _call(
        body, name="attn_b_fwd", grid=(nblk,),
        in_specs=[
            smem,
            smem,
            pl.BlockSpec((BLOCK_Q, wide), lambda n: (n, COL_QB // GROUP)),
            pl.BlockSpec((BLOCK_Q, wide), lambda n: (n, COL_QB // GROUP + 1)),
            whole,
            whole,
            pl.BlockSpec((BLOCK_Q, 3 * BLOCK_Q), lambda n: (0, 0)),
            _ANY,
        ],
        out_specs=[
            pl.BlockSpec((BLOCK_Q, N_HEADS_B * HEAD_DIM), lambda n: (n, 1)),
            pl.BlockSpec((N_HEADS_B, BLOCK_Q, HEAD_DIM), lambda n: (0, n, 0)),
        ],
        out_shape=[
            jax.ShapeDtypeStruct(att.shape, BF16),
            jax.ShapeDtypeStruct((N_HEADS_B, s, HEAD_DIM), F32),
        ],
        scratch_shapes=[pltpu.VMEM((N_KV_B * GQ, 3 * BLOCK_Q), F32)],
        sem=("arbitrary",), args=(table, sink, pb, pb, kpad, vpad, bucket, att), comm=comm, after=after,
        aliases={7: 0})
    return res if comm is None else (res, c_res)


def _attn_b_bwd(pb, kpad, vpad, att, datt, lse, bucket, table, sink, *, comm=None, after=None):
    s = pb.shape[0]
    nblk = s // BLOCK_Q
    sp = kpad.shape[0]

    def body(table_ref, sink_ref, q0_ref, q1_ref, k_ref, v_ref, o_ref, do_ref, lse_ref, bucket_ref,
             dq_ref, dk_ref, dv_ref, dtab_ref, dsink_ref, bias_ref, dbias_ref):
        n = pl.program_id(0)

        @pl.when(n == 0)
        def _():
            dk_ref[...] = jnp.zeros_like(dk_ref)
            dv_ref[...] = jnp.zeros_like(dv_ref)
            dbias_ref[...] = jnp.zeros_like(dbias_ref)
            dsink_ref[...] = jnp.zeros_like(dsink_ref)
            for kv in range(N_KV_B):
                bias_ref[kv * GQ:(kv + 1) * GQ, :] = _group_bias(bucket_ref[...], table_ref, kv)

        band = pl.ds(_band_start(n), 3 * BLOCK_Q)
        mask = _band_mask(n, s)
        for kv, q_ref in enumerate((q0_ref, q1_ref)):
            wide_cols = _cols(kv * GROUP, GROUP)
            q = _stack_heads(q_ref[...])
            do = _stack_heads(do_ref[:, wide_cols])
            o = _stack_heads(o_ref[:, wide_cols])
            kb = k_ref[band, _cols(kv)]
            vb = v_ref[band, _cols(kv)]
            lse = jnp.concatenate([lse_ref[kv * GROUP + g][:, :1] for g in range(GROUP)], axis=0)
            sc = lax.dot_general(q, kb, _NT, preferred_element_type=F32)
            sc = jnp.where(mask, sc + bias_ref[kv * GQ:(kv + 1) * GQ, :], NEG_INF)
            p = jnp.exp2(sc - lse)
            dp = lax.dot_general(do, vb, _NT, preferred_element_type=F32)
            delta = jnp.sum(do.astype(F32) * o.astype(F32), axis=-1, keepdims=True)
            ds = p * (dp - delta)
            dsb = ds.astype(BF16)
            dq_ref[:, wide_cols] = _unstack_heads(jnp.dot(dsb, kb, preferred_element_type=F32))
            dk_ref[band, _cols(kv)] += lax.dot_general(dsb, q, _TN, preferred_element_type=F32)
            dv_ref[band, _cols(kv)] += lax.dot_general(p.astype(BF16), do, _TN, preferred_element_type=F32)
            dbias_ref[kv * GQ:(kv + 1) * GQ, :] += ds
            sink_part = -jnp.exp2(_group_sink(sink_ref, kv) - lse) * delta
            for g in range(GROUP):
                rows = slice(g * BLOCK_Q, (g + 1) * BLOCK_Q)
                dsink_ref[kv * GROUP + g] += jnp.broadcast_to(
                    jnp.sum(sink_part[rows], axis=0, keepdims=True), (1, HEAD_DIM))

        @pl.when(n == nblk - 1)
        def _():
            bucket_v = bucket_ref[...]
            row = lax.broadcasted_iota(jnp.int32, (N_BUCKETS, HEAD_DIM), 0)
            for h in range(N_HEADS_B):
                acc = dbias_ref[h * BLOCK_Q:(h + 1) * BLOCK_Q, :]
                tot = jnp.zeros((N_BUCKETS, HEAD_DIM), F32)
                for b in range(N_BUCKETS):
                    tot = jnp.where(row == b, jnp.sum(jnp.where(bucket_v == b, acc, 0.0), keepdims=True), tot)
                dtab_ref[h] = tot

    smem = pl.BlockSpec(memory_space=pltpu.SMEM)
    wide = GROUP * HEAD_DIM
    whole = pl.BlockSpec((sp, N_KV_B * HEAD_DIM), lambda n: (0, 0))
    group_b = pl.BlockSpec((BLOCK_Q, N_HEADS_B * HEAD_DIM), lambda n: (n, 1))
    res, c_res = _call(
        body, name="attn_b_bwd", grid=(nblk,),
        in_specs=[
            smem,
            smem,
            pl.BlockSpec((BLOCK_Q, wide), lambda n: (n, COL_QB // GROUP)),
            pl.BlockSpec((BLOCK_Q, wide), lambda n: (n, COL_QB // GROUP + 1)),
            whole,
            whole,
            group_b,
            group_b,
            pl.BlockSpec((N_HEADS_B, BLOCK_Q, HEAD_DIM), lambda n: (0, n, 0)),
            pl.BlockSpec((BLOCK_Q, 3 * BLOCK_Q), lambda n: (0, 0)),
        ],
        out_specs=[
            pl.BlockSpec((BLOCK_Q, N_HEADS_B * HEAD_DIM), lambda n: (n, 0)),
            whole,
            whole,
            pl.BlockSpec((N_HEADS_B, N_BUCKETS, HEAD_DIM), lambda n: (0, 0, 0)),
            pl.BlockSpec((N_HEADS_B, 1, HEAD_DIM), lambda n: (0, 0, 0)),
        ],
        out_shape=[
            jax.ShapeDtypeStruct((s, N_HEADS_B * HEAD_DIM), F32),
            jax.ShapeDtypeStruct((sp, N_KV_B * HEAD_DIM), F32),
            jax.ShapeDtypeStruct((sp, N_KV_B * HEAD_DIM), F32),
            jax.ShapeDtypeStruct((N_HEADS_B, N_BUCKETS, HEAD_DIM), F32),
            jax.ShapeDtypeStruct((N_HEADS_B, 1, HEAD_DIM), F32),
        ],
        scratch_shapes=[pltpu.VMEM((N_KV_B * GQ, 3 * BLOCK_Q), F32), pltpu.VMEM((N_KV_B * GQ, 3 * BLOCK_Q), F32)],
        sem=("arbitrary",),
        args=(table, sink, pb, pb, kpad, vpad, att, datt, lse, bucket), comm=comm, after=after)
    return res if comm is None else (res, c_res)


def _other_chips(x, y):
    return [(x, 1 - y), (1 - x, y), (1 - x, 1 - y)]


_HBM = pl.BlockSpec(memory_space=pltpu.HBM)
_SEM = pl.BlockSpec(memory_space=pltpu.SEMAPHORE)
_SPLIT = pltpu.CompilerParams(has_side_effects=pltpu.SideEffectType.DATAFLOW_SIDE_EFFECTING)


def _in_hbm(a):
    return pltpu.with_memory_space_constraint(a, pltpu.HBM)


def _my_half(rows):
    c = lax.axis_index("c")
    half = rows // 2
    return pl.ds(pl.multiple_of(c * half, half), half), pl.ds(pl.multiple_of((1 - c) * half, half), half)


def _gather_route(shapes):
    def route(src, land):
        x, y, c = lax.axis_index("x"), lax.axis_index("y"), lax.axis_index("c")
        out = []
        for t, shape in enumerate(shapes):
            mine, _ = _my_half(shape[0])
            for px, py in _other_chips(x, y):
                out.append((src[t].at[mine], land[t].at[2 * x + y, mine], land[t].at[2 * px + py, mine], (px, py, c)))
        return out

    return route


def _exchange_route(n_t):
    def route(src, land):
        x, y, c = lax.axis_index("x"), lax.axis_index("y"), lax.axis_index("c")
        out = []
        for t in range(n_t):
            for px, py in _other_chips(x, y):
                k = 2 * px + py
                out.append((src[t].at[k], land[t].at[2 * (2 * x + y) + c], land[t].at[2 * k + c], (px, py, c)))
        return out

    return route


def _own_slot(shape, dtype, slot, block):
    return lax.dynamic_update_slice(lax.empty(shape, dtype), block[None], (slot,) + (0,) * (len(shape) - 1))


def _split_start(name, srcs, lands, route, after):
    n = len(srcs)

    def body(*refs):
        src, land, send_sems, recv_sems, token = refs[:n], refs[n:2 * n], refs[2 * n + 1], refs[2 * n + 2], refs[-1]
        for i, (src_ref, dst_ref, _, to) in enumerate(route(src, land)):
            pltpu.make_async_remote_copy(src_ref=src_ref, dst_ref=dst_ref, send_sem=send_sems.at[i],
                                         recv_sem=recv_sems.at[i], device_id=to, device_id_type=_MESH).start()
        token[...] = jnp.zeros_like(token)

    sem = pltpu.SemaphoreType.DMA((3 * n,))
    lands = list(lands)
    res = pl.pallas_call(
        body, name=name,
        in_specs=[_HBM] * (2 * n) + [_ANY],
        out_specs=[_SEM, _SEM] + [_HBM] * (2 * n) + [pl.BlockSpec(memory_space=pltpu.VMEM)],
        out_shape=[sem, sem] + [pltpu.HBM(a.shape, a.dtype) for a in list(srcs) + lands]
        + [jax.ShapeDtypeStruct((8, 128), F32)],
        input_output_aliases={i: 2 + i for i in range(2 * n)},
        compiler_params=_SPLIT,
    )(*[_in_hbm(a) for a in srcs], *[_in_hbm(a) for a in lands], after)
    return (res[0], res[1]), res[2:2 + n], res[2 + n:2 + 2 * n], res[-1]


def _split_wait(name, srcs, lands, sems, route, after):
    n = len(srcs)

    def body(*refs):
        src, land, send_sems, recv_sems = refs[:n], refs[n:2 * n], refs[2 * n], refs[2 * n + 1]
        for i, (src_ref, _, dst_ref, to) in enumerate(route(src, land)):
            cp = pltpu.make_async_remote_copy(src_ref=src_ref, dst_ref=dst_ref, send_sem=send_sems.at[i],
                                              recv_sem=recv_sems.at[i], device_id=to, device_id_type=_MESH)
            cp.wait_send()
            cp.wait_recv()

    res = pl.pallas_call(
        body, name=name,
        in_specs=[_HBM] * (2 * n) + [_SEM, _SEM, _ANY],
        out_specs=[_HBM] * (2 * n),
        out_shape=[pltpu.HBM(a.shape, a.dtype) for a in list(srcs) + list(lands)],
        input_output_aliases={i: i for i in range(2 * n)},
        compiler_params=_SPLIT,
    )(*srcs, *lands, sems[0], sems[1], after)
    return res[:n], res[n:]


def _comm_only(name, comm):
    return _call(lambda: None, name=name, grid=(1,), in_specs=[], out_specs=[], out_shape=[], args=(), comm=comm)[1]


def _swap_comm(shards, lands):
    n_t = len(lands)

    def copies(land, sems, later):
        send_sems, recv_sems = sems
        x, y = lax.axis_index("x"), lax.axis_index("y")
        sends, recvs = [], []
        for t in range(n_t):
            mine, other = _my_half(shards[t].shape[0])
            for j, (px, py) in enumerate(_other_chips(x, y)):
                k = 2 * px + py
                for part, out in ((mine, sends), (other, recvs)) if later else ((mine, sends),):
                    out.append(pltpu.make_async_remote_copy(
                        src_ref=land[t].at[k, part], dst_ref=land[t].at[k, part], send_sem=send_sems.at[3 * t + j],
                        recv_sem=recv_sems.at[3 * t + j], device_id=_sibling(), device_id_type=_MESH))
        return sends, recvs

    def start(ins, land, sems):
        for cp in copies(land, sems, False)[0]:
            cp.start()

    def finish(ins, land, sems):
        sends, recvs = copies(land, sems, True)
        for cp in recvs:
            cp.wait_recv()
        for cp in sends:
            cp.wait_send()

    return _Comm(
        lands, [jax.ShapeDtypeStruct(a.shape, a.dtype) for a in lands],
        [pltpu.SemaphoreType.DMA((3 * n_t,)), pltpu.SemaphoreType.DMA((3 * n_t,))],
        start, finish, aliases={t: t for t in range(n_t)})


def _forward_comm(partials, lands):
    n_t = len(lands)

    def copies(land, sems, later):
        send_sems, recv_sems = sems
        x, y, c = lax.axis_index("x"), lax.axis_index("y"), lax.axis_index("c")
        sends, recvs = [], []
        for t in range(n_t):
            for j, k in enumerate([2 * x + y] + [2 * px + py for px, py in _other_chips(x, y)]):
                for slot, out in ((2 * k + c, sends), (2 * k + 1 - c, recvs)) if later else ((2 * k + c, sends),):
                    out.append(pltpu.make_async_remote_copy(
                        src_ref=land[t].at[slot], dst_ref=land[t].at[slot], send_sem=send_sems.at[4 * t + j],
                        recv_sem=recv_sems.at[4 * t + j], device_id=_sibling(), device_id_type=_MESH))
        return sends, recvs

    def start(ins, land, sems):
        for cp in copies(land, sems, False)[0]:
            cp.start()

    def finish(ins, land, sems):
        sends, recvs = copies(land, sems, True)
        for cp in recvs:
            cp.wait_recv()
        for cp in sends:
            cp.wait_send()

    return _Comm(
        lands, [jax.ShapeDtypeStruct(a.shape, a.dtype) for a in lands],
        [pltpu.SemaphoreType.DMA((4 * n_t,)), pltpu.SemaphoreType.DMA((4 * n_t,))],
        start, finish, aliases={t: t for t in range(n_t)})


def _allreduce_small(pack):
    rows, d = pack.shape

    def body(p_ref, sum_ref, all_ref, send_sems, recv_sems):
        x, y, c = lax.axis_index("x"), lax.axis_index("y"), lax.axis_index("c")
        me = 4 * x + 2 * y + c
        all_ref[me] = p_ref[...]
        peers = []
        for dx in range(2):
            for dy in range(2):
                for dc in range(2):
                    if dx or dy or dc:
                        px = 1 - x if dx else x
                        py = 1 - y if dy else y
                        pc = 1 - c if dc else c
                        peers.append((4 * dx + 2 * dy + dc - 1, (px, py, pc)))
        sends = []
        for k, to in peers:
            cp = pltpu.make_async_remote_copy(
                src_ref=p_ref, dst_ref=all_ref.at[me], send_sem=send_sems.at[k], recv_sem=recv_sems.at[k],
                device_id=to, device_id_type=_MESH)
            cp.start()
            sends.append(cp)
        for k, (px, py, pc) in peers:
            pltpu.make_async_remote_copy(
                src_ref=p_ref, dst_ref=all_ref.at[4 * px + 2 * py + pc], send_sem=send_sems.at[k],
                recv_sem=recv_sems.at[k], device_id=(px, py, pc), device_id_type=_MESH).wait_recv()
        for cp in sends:
            cp.wait_send()
        tot = all_ref[0]
        for i in range(1, N_DEV):
            tot = tot + all_ref[i]
        sum_ref[...] = tot

    vm = pl.BlockSpec(memory_space=pltpu.VMEM)
    return pl.pallas_call(
        body,
        name="allreduce_small",
        in_specs=[vm],
        out_specs=vm,
        out_shape=jax.ShapeDtypeStruct((rows, d), F32),
        scratch_shapes=[
            pltpu.VMEM((N_DEV, rows, d), F32),
            pltpu.SemaphoreType.DMA((N_DEV - 1,)),
            pltpu.SemaphoreType.DMA((N_DEV - 1,)),
        ],
    )(pack)


def _adamw_math(w, g, m, v):
    m = ADAM_B1 * m + (1.0 - ADAM_B1) * g
    v = ADAM_B2 * v + (1.0 - ADAM_B2) * (g * g)
    m_hat = m / (1.0 - ADAM_B1 ** ADAM_STEP)
    v_hat = v / (1.0 - ADAM_B2 ** ADAM_STEP)
    delta = -ADAM_LR * (m_hat / (jnp.sqrt(v_hat) + ADAM_EPS) + ADAM_WD * w)
    return delta, m, v


def _sum_adamw(parts, w, m, v, *, name, tr=256):
    r, c = w.shape
    tr = min(tr, r)
    tc = min(c, 1024)

    def body(p_ref, w_ref, m_ref, v_ref, g_ref, d_ref, m2_ref, v2_ref):
        g = p_ref[0].astype(F32)
        for i in range(1, N_DEV):
            g = g + p_ref[i].astype(F32)
        delta, m2, v2 = _adamw_math(w_ref[...], g, m_ref[...], v_ref[...])
        g_ref[...] = g
        d_ref[...] = delta
        m2_ref[...] = m2
        v2_ref[...] = v2

    blk = pl.BlockSpec((tr, tc), lambda i, j: (i, j))
    return pl.pallas_call(
        body,
        name=name,
        grid=(r // tr, c // tc),
        in_specs=[pl.BlockSpec((N_DEV, tr, tc), lambda i, j: (0, i, j)), blk, blk, blk],
        out_specs=[blk] * 4,
        out_shape=[jax.ShapeDtypeStruct((r, c), F32)] * 4,
        compiler_params=_params(("parallel", "parallel")),
    )(parts, w, m, v)


def _adamw_small(g, w, m, v):
    def body(g_ref, w_ref, m_ref, v_ref, d_ref, m2_ref, v2_ref):
        delta, m2, v2 = _adamw_math(w_ref[...], g_ref[...], m_ref[...], v_ref[...])
        d_ref[...] = delta
        m2_ref[...] = m2
        v2_ref[...] = v2

    vm = pl.BlockSpec(memory_space=pltpu.VMEM)
    return pl.pallas_call(
        body,
        name="adamw_small",
        in_specs=[vm] * 4,
        out_specs=[vm] * 3,
        out_shape=[jax.ShapeDtypeStruct(g.shape, F32)] * 3,
    )(g, w, m, v)


def _relu2_epilogue(acc):
    ra = jnp.maximum(acc, 0.0)
    return ra * ra, ra


def _residual_norm_epilogue(acc, res, g):
    h = acc + res
    return h, h * lax.rsqrt(jnp.mean(h * h, axis=-1, keepdims=True) + EPS) * g


def _rows(stacked):
    return stacked.reshape(stacked.shape[0] * stacked.shape[1], stacked.shape[2])


def _by_chip(mat):
    return mat.reshape(N_CHIPS, mat.shape[0] // N_CHIPS, mat.shape[1])


def _local_step(x, p, target, shards, small, update):
    s, d = x.shape
    cos_t, sin_t = _rope_tables(s)
    bucket = _band_buckets()
    p_bf = p.astype(BF16)
    wts = {}

    chip = 2 * lax.axis_index("x") + lax.axis_index("y")
    core = lax.axis_index("c")

    def gather(tag, names, after):
        srcs = [cast[n] for n in names]
        route = _gather_route([a.shape for a in srcs])
        sems, srcs, lands, token = _split_start(f"gather_start_{tag}", srcs, [zones[n] for n in names], route, after)

        def landed(done):
            got_srcs, got_lands = _split_wait(f"gather_wait_{tag}", srcs, lands, sems, route, done)
            comm = _swap_comm(got_srcs, got_lands)
            comm.waited = got_srcs[0]
            return comm

        return landed, token

    def prepare(n, zero):
        cast[n] = (shards[n] + zero).astype(BF16)
        zones[n] = _own_slot((N_CHIPS,) + cast[n].shape, BF16, chip, cast[n])

    cast, zones = {}, {}
    prepare("w_in", 0.0)
    in_landed, token = gather("in", ["w_in"], small["attn_norm_g"])
    for n in shards:
        if n != "w_in":
            prepare(n, token[:1, :1])
    g_attn = small["attn_norm_g"] + token[:1, :1]
    u = _rms_fwd(x, g_attn, name="norm_attn")
    prepared = u[:1, :1].astype(F32) + sum(
        (lax.dynamic_slice(zones[n], (chip, 0, 0), (1, 1, 1))[0] + cast[n][:1, :1]).astype(F32)
        for n in zones if n != "w_in")
    (wts["w_in"],) = _comm_only("swap_w_in", in_landed(prepared))
    mid_landed, token = gather("mid", ["w_out"], wts["w_in"])
    proj = _matmul(u, wts["w_in"], mode="nn", out_dtypes=[F32], name="mm_in", bn=768, after=token)
    pb, (w_out_s,) = _qk_prep(proj, small["q_norm_g"], small["k_norm_g"], cos_t, sin_t, comm=mid_landed(proj))
    wts["w_out"] = _rows(w_out_s)
    up_landed, token = gather("up", ["w_up"], pb)
    att_a, lse_a = _attn_a_fwd(pb, after=token)
    pad = ((PAD_LO, PAD_HI), (0, 0))
    kpad = jnp.pad(pb[:, COL_KB * HEAD_DIM:COL_VB * HEAD_DIM], pad)
    vpad = jnp.pad(pb[:, COL_VB * HEAD_DIM:], pad)
    up_swap = up_landed(att_a)
    down_landed, token = gather("down", ["w_down"], up_swap.waited)
    (att, lse_b), (wts["w_up"],) = _attn_b_fwd(pb, kpad, vpad, bucket, small["rel_bias_table"],
                                               small["sink_logits"], att_a, comm=up_swap, after=token)
    h1, mn = _matmul(att, wts["w_out"], mode="nn", out_dtypes=[F32, BF16], name="mm_out", bm=512, bn=d,
                     epilogue=_residual_norm_epilogue, extras=(x,), vecs=(small["mlp_norm_g"],))
    r, ra = _matmul(mn, wts["w_up"], mode="nn", out_dtypes=[BF16, BF16], name="mm_up", epilogue=_relu2_epilogue,
                    bm=2048)
    (w_down_s,) = _comm_only("swap_w_down", down_landed(r))
    wts["w_down"] = _rows(w_down_s)
    late_landed, token = gather("late", ["w_gate", "ple_w"], w_down_s)
    h2 = _matmul(r, wts["w_down"], mode="nn", out_dtypes=[F32], name="mm_down",
                 epilogue=lambda acc, res: (acc + res,), extras=(h1,), after=token)
    ng, (w_gate_s, wts["ple_w"]) = _rms_fwd(h2, small["gate_norm_g"], name="norm_gate", comm=late_landed(h2))
    wts["w_gate"] = _rows(w_gate_s)
    gate = _matmul(ng, wts["w_gate"], mode="nn", out_dtypes=[F32], name="mm_gate",
                   epilogue=lambda acc: (1.0 / (1.0 + jnp.exp(-acc)),))
    pp = _matmul(p_bf, wts["ple_w"], mode="nn", out_dtypes=[F32], name="mm_ple", bn=512)
    dh3, dz, dpp, dg_final, dg_ple, loss = _tail(h2, gate, pp, target, small["ple_norm_g"], small["final_norm_g"])

    dng = _matmul(dz, wts["w_gate"], mode="nt", out_dtypes=[F32], name="mm_gate_dx")
    gw_gate = _matmul(ng, dz, mode="tn", out_dtypes=[BF16], name="mm_gate_dw", bm=512, bk=4096)
    gw_ple = _matmul(p_bf, dpp, mode="tn", out_dtypes=[BF16], name="mm_ple_dw", bn=512, out_stack=N_CHIPS)
    dh2, dh2_bf, dg_gate = _rms_bwd(h2, dng, small["gate_norm_g"], dh3, name="norm_gate_bwd", want_bf16=True)

    def exchange(tag, partials, after):
        route = _exchange_route(len(partials))
        lands = [_own_slot((N_DEV,) + g.shape[1:], g.dtype, 2 * chip + core,
                           lax.dynamic_index_in_dim(g, chip, 0, keepdims=False)) for g in partials]
        sems, srcs, lands, token = _split_start(f"exchange_start_{tag}", partials, lands, route, after)

        def landed(done):
            got_srcs, got_lands = _split_wait(f"exchange_wait_{tag}", srcs, lands, sems, route, done)
            comm = _forward_comm(got_srcs, got_lands)
            comm.waited = got_srcs[0]
            return comm

        return landed, token

    big = {}
    gate_landed, token = exchange("gate", [_by_chip(gw_gate), gw_ple], dh2_bf)
    gw_down = _matmul(r, dh2_bf, mode="tn", out_dtypes=[BF16], name="mm_down_dw", after=token, bm=512, bk=4096)
    da, (parts_gate, parts_ple) = _matmul(
        dh2_bf, wts["w_down"], mode="nt", out_dtypes=[BF16], name="mm_down_dx", bm=2048,
        epilogue=lambda acc, ra_v: (acc * (2.0 * ra_v.astype(F32)),), extras=(ra,), comm=gate_landed(gw_down))
    down_landed, token = exchange("down", [_by_chip(gw_down)], da)
    big["w_gate"], big["ple_w"] = update("w_gate", parts_gate), update("ple_w", parts_ple)
    gw_up = _matmul(mn, da, mode="tn", out_dtypes=[BF16], name="mm_up_dw", out_stack=N_CHIPS, after=token,
                    bm=512, bk=4096)
    dmn = _matmul(da, wts["w_up"], mode="nt", out_dtypes=[F32], name="mm_up_dx")
    dh1, dh1_bf, dg_mlp = _rms_bwd(h1, dmn, small["mlp_norm_g"], dh2, name="norm_mlp_bwd", want_bf16=True)
    datt, (parts_down,) = _matmul(dh1_bf, wts["w_out"], mode="nt", out_dtypes=[BF16], name="mm_out_dx",
                                  comm=down_landed(dh1_bf))
    gw_out = _matmul(att, dh1_bf, mode="tn", out_dtypes=[BF16], name="mm_out_dw", bm=512, bk=4096)
    up_landed, token = exchange("up", [gw_up], datt)
    dqb, dkpad, dvpad, dtab, dsink = _attn_b_bwd(pb, kpad, vpad, att, datt, lse_b, bucket,
                                                 small["rel_bias_table"], small["sink_logits"], after=token)
    dqa, dka, dva = _attn_a_bwd(pb, att, datt, lse_a)
    up_forward = up_landed(dqa)
    out_landed, token = exchange("out", [_by_chip(gw_out)], up_forward.waited)
    (dproj, dg_q, dg_k), (parts_up,) = _qk_bwd(dqa, dka, dva, dqb, dkpad, dvpad, proj,
                                               small["q_norm_g"], small["k_norm_g"], cos_t, sin_t,
                                               comm=up_forward, after=token)
    gw_in = _matmul(u, dproj, mode="tn", out_dtypes=[BF16], name="mm_in_dw", bn=768, out_stack=N_CHIPS,
                    bm=512, bk=4096)
    out_forward = out_landed(gw_in)
    in_landed, token = exchange("in", [gw_in], out_forward.waited)
    du, (parts_out,) = _matmul(dproj, wts["w_in"], mode="nt", out_dtypes=[F32], name="mm_in_dx", bk=3072,
                               comm=out_forward, after=token)
    grad_x, dg_attn = _rms_bwd(x, du, small["attn_norm_g"], dh1, name="norm_attn_bwd", want_bf16=False)
    for n, parts in (("w_down", parts_down), ("w_up", parts_up), ("w_out", parts_out)):
        big[n] = update(n, parts)
    done = dg_attn + sum(big[n][0][0, :1, :] for n in ("w_down", "w_up", "w_out"))
    (parts_in,) = _comm_only("forward_w_in", in_landed(done))
    big["w_in"] = update("w_in", parts_in)

    small_g = {
        "attn_norm_g": dg_attn, "mlp_norm_g": dg_mlp, "ple_norm_g": dg_ple, "gate_norm_g": dg_gate,
        "final_norm_g": dg_final, "q_norm_g": dg_q, "k_norm_g": dg_k,
        "sink_logits": dsink[:, 0, 0][None, :], "rel_bias_table": dtab[:, :, 0].T,
    }
    return loss, grad_x, big, small_g


_SMALL_ROWS = ["attn_norm_g", "mlp_norm_g", "ple_norm_g", "gate_norm_g", "final_norm_g"]
_PACK_ROWS = 8


def _pack_small(vals, d):
    rows = [vals[n].reshape(1, d) for n in _SMALL_ROWS]
    misc = jnp.concatenate([
        vals["q_norm_g"].reshape(1, HEAD_DIM), vals["k_norm_g"].reshape(1, HEAD_DIM),
        jnp.pad(vals["sink_logits"].reshape(1, N_HEADS_B), ((0, 0), (0, HEAD_DIM - N_HEADS_B))),
        vals["rel_bias_table"].reshape(1, N_BUCKETS * N_HEADS_B)], axis=1)
    rows.append(jnp.pad(misc, ((0, 0), (0, d - misc.shape[1]))))
    rows.append(jnp.zeros((_PACK_ROWS - len(rows), d), F32))
    return jnp.concatenate(rows, axis=0).astype(F32)


def _unpack_small(pack, shapes):
    out = {n: pack[i].reshape(shapes[n]) for i, n in enumerate(_SMALL_ROWS)}
    misc = pack[len(_SMALL_ROWS)]
    out["q_norm_g"] = misc[:HEAD_DIM].reshape(shapes["q_norm_g"])
    out["k_norm_g"] = misc[HEAD_DIM:2 * HEAD_DIM].reshape(shapes["k_norm_g"])
    out["sink_logits"] = misc[2 * HEAD_DIM:2 * HEAD_DIM + N_HEADS_B].reshape(shapes["sink_logits"])
    out["rel_bias_table"] = misc[3 * HEAD_DIM:3 * HEAD_DIM + N_BUCKETS * N_HEADS_B].reshape(shapes["rel_bias_table"])
    return out


_WEIGHTS = ["attn_norm_g", "w_in", "q_norm_g", "k_norm_g", "sink_logits", "w_out", "mlp_norm_g", "w_up", "w_down",
            "ple_w", "ple_norm_g", "gate_norm_g", "w_gate", "rel_bias_table", "final_norm_g"]
_BIG = ["w_in", "w_out", "w_up", "w_down", "ple_w", "w_gate"]


def kernel(x, p, attn_norm_g, w_in, q_norm_g, k_norm_g, sink_logits, w_out, mlp_norm_g, w_up, w_down, ple_w, ple_norm_g, gate_norm_g, w_gate, rel_bias_table, final_norm_g, loss_target, m_attn_norm_g, m_w_in, m_q_norm_g, m_k_norm_g, m_sink_logits, m_w_out, m_mlp_norm_g, m_w_up, m_w_down, m_ple_w, m_ple_norm_g, m_gate_norm_g, m_w_gate, m_rel_bias_table, m_final_norm_g, v_attn_norm_g, v_w_in, v_q_norm_g, v_k_norm_g, v_sink_logits, v_w_out, v_mlp_norm_g, v_w_up, v_w_down, v_ple_w, v_ple_norm_g, v_gate_norm_g, v_w_gate, v_rel_bias_table, v_final_norm_g):
    given = dict(locals())
    w = {n: given[n] for n in _WEIGHTS}
    m = {n: given["m_" + n] for n in _WEIGHTS}
    v = {n: given["v_" + n] for n in _WEIGHTS}
    d = x.shape[-1]

    shards = {n: w[n][0] for n in _BIG}
    small = {
        "attn_norm_g": w["attn_norm_g"], "mlp_norm_g": w["mlp_norm_g"], "ple_norm_g": w["ple_norm_g"],
        "gate_norm_g": w["gate_norm_g"], "final_norm_g": w["final_norm_g"].reshape(1, d),
        "q_norm_g": w["q_norm_g"], "k_norm_g": w["k_norm_g"], "sink_logits": w["sink_logits"],
        "rel_bias_table": w["rel_bias_table"],
    }

    def update(n, parts):
        res = _sum_adamw(parts, w[n][0], m[n][0], v[n][0], name="adamw_" + n)
        return [t.reshape(w[n].shape) for t in res]

    loss_part, grad_x, big, small_g = _local_step(x[0], p[0, 0], loss_target[0], shards, small, update)
    grads, deltas, new_m, new_v = [{n: big[n][i] for n in _BIG} for i in range(4)]

    shapes = {n: w[n].shape for n in _WEIGHTS if n not in _BIG}
    pack = _pack_small(small_g, d)
    pack = pack.at[_PACK_ROWS - 1, :1].add(0.0 * grads["w_in"][0, 0, :1])
    pack = pack.at[_PACK_ROWS - 1, 1].set(loss_part[0, 0])
    g_small = _allreduce_small(pack)
    loss = g_small[_PACK_ROWS - 1, 1]
    d_small, m_small, v_small = _adamw_small(g_small, _pack_small(w, d), _pack_small(m, d), _pack_small(v, d))
    grads.update(_unpack_small(g_small, shapes))
    deltas.update(_unpack_small(d_small, shapes))
    new_m.update(_unpack_small(m_small, shapes))
    new_v.update(_unpack_small(v_small, shapes))

    return (loss, grad_x[None], *[grads[n] for n in _WEIGHTS], *[deltas[n] for n in _WEIGHTS],
            *[new_m[n] for n in _WEIGHTS], *[new_v[n] for n in _WEIGHTS])
```

```python
import functools
import math

import jax
import jax.numpy as jnp
from jax import lax
from jax.experimental import pallas as pl
from jax.experimental.pallas import tpu as pltpu

F32 = jnp.float32
BF16 = jnp.bfloat16

HEAD_DIM = 128
N_HEADS_A = 8
N_KV_A = 2
N_HEADS_B = 8
N_KV_B = 2
GROUP = 4
GRID_W = 64
BLOCK_Q = 128
WINDOW = 128
N_BUCKETS = 32
MAX_DISTANCE = 128
ROPE_THETA = 10000.0
EPS = 1e-6
NEG_INF = -1e30
ATT_SCALE = HEAD_DIM ** -0.5
LOG2E = math.log2(math.e)
LN2 = math.log(2.0)
Q_SCALE = ATT_SCALE * LOG2E
PAD_LO, PAD_HI = 256, 128
ADAM_LR = 0.001
ADAM_B1 = 0.9
ADAM_B2 = 0.999
ADAM_EPS = 1e-08
ADAM_WD = 0.01
ADAM_STEP = 10

N_CHIPS = 4
N_DEV = 8
COL_QA, COL_KA, COL_VA, COL_QB, COL_KB, COL_VB = 0, 8, 10, 12, 20, 22

VMEM_LIMIT = 52 * 1024 * 1024


def _params(sem=None, collective_id=None):
    return pltpu.CompilerParams(dimension_semantics=sem, vmem_limit_bytes=VMEM_LIMIT, collective_id=collective_id)


_ANY = pl.BlockSpec(memory_space=pl.ANY)
_MESH = pl.DeviceIdType.MESH
SIBLING_BARRIER_ID = 1


def _sibling():
    return (lax.axis_index("x"), lax.axis_index("y"), 1 - lax.axis_index("c"))


class _Comm:
    def __init__(self, inputs, out_shapes, sems, start, finish, aliases=None):
        self.inputs, self.out_shapes, self.sems = list(inputs), list(out_shapes), list(sems)
        self.start, self.finish, self.aliases = start, finish, dict(aliases or {})


def _call(body, *, name, grid, in_specs, out_specs, out_shape, args, scratch_shapes=(), sem=None, comm=None,
          after=None, aliases=None):
    in_specs, out_specs, out_shape = list(in_specs), list(out_specs), list(out_shape)
    scratch_shapes = list(scratch_shapes)
    n_in, n_out, n_sc = len(in_specs), len(out_specs), len(scratch_shapes)
    behind = [] if after is None else [after]
    aliases = dict(aliases or {})
    if comm is None:
        res = pl.pallas_call(
            (lambda *refs: body(*refs[:n_in], *refs[n_in + len(behind):])) if behind else body,
            name=name, grid=grid, in_specs=in_specs + [_ANY] * len(behind), out_specs=out_specs,
            out_shape=out_shape, scratch_shapes=scratch_shapes, input_output_aliases=aliases,
            compiler_params=_params(sem))(*args, *behind)
        return list(res), []
    c_in, c_out = len(comm.inputs), len(comm.out_shapes)

    def hosted(*refs):
        pos = [0]

        def take(n):
            pos[0] += n
            return refs[pos[0] - n:pos[0]]

        ins, c_ins, _, outs, c_outs, scr = (take(n_in), take(c_in), take(len(behind)), take(n_out), take(c_out),
                                            take(n_sc))
        c_sems = refs[pos[0]:]
        ids = [pl.program_id(a) for a in range(len(grid))]
        first = functools.reduce(jnp.logical_and, [i == 0 for i in ids])
        last = functools.reduce(jnp.logical_and, [i == g - 1 for i, g in zip(ids, grid)])

        @pl.when(first)
        def _():
            barrier = pltpu.get_barrier_semaphore()
            pl.semaphore_signal(barrier, inc=1, device_id=_sibling(), device_id_type=_MESH)
            pl.semaphore_wait(barrier, 1)
            comm.start(c_ins, c_outs, c_sems)

        body(*ins, *outs, *scr)

        @pl.when(last)
        def _():
            comm.finish(c_ins, c_outs, c_sems)

    res = pl.pallas_call(
        hosted, name=name, grid=grid, in_specs=in_specs + [_ANY] * (c_in + len(behind)),
        out_specs=out_specs + [_ANY] * c_out,
        out_shape=out_shape + comm.out_shapes, scratch_shapes=scratch_shapes + comm.sems,
        input_output_aliases={**aliases, **{n_in + i: n_out + o for i, o in comm.aliases.items()}},
        compiler_params=_params(("arbitrary",) * len(grid), SIBLING_BARRIER_ID))(*args, *comm.inputs, *behind)
    return list(res[:n_out]), list(res[n_out:])


def _matmul(a, b, *, mode, out_dtypes, name, epilogue=None, extras=(), bm=1024, bn=1024, bk=2048,
            out_stack=0, comm=None, after=None, vecs=()):
    stacked = b.ndim == 3
    if mode == "nn":
        m, k = a.shape
        if stacked:
            nj, kb, ns = b.shape
            n, ks = nj * ns, k
        else:
            kb, n = b.shape
            ns, ks = n, k
        dn = (((1,), (0,)), ((), ()))
    elif mode == "nt":
        m, k = a.shape
        if stacked:
            nj, n, ks = b.shape
            kb = nj * ks
        else:
            n, kb = b.shape
            ks = kb
        ns = n
        dn = (((1,), (1,)), ((), ()))
    else:
        k, m = a.shape
        kb, n = b.shape
        ns, ks = n, k
        dn = (((0,), (0,)), ((), ()))
    assert k == kb and not (stacked and mode == "tn")
    ns_out = n // out_stack if out_stack else n
    per_blk = min(bk, k) // ks if stacked and mode == "nt" and bk > ks else 0
    bm, bn, bk = min(bm, m), min(bn, ns, ns_out), per_blk * ks if per_blk else min(bk, ks)
    assert m % bm == 0 and ns % bn == 0 and ns_out % bn == 0 and (k % bk == 0 if per_blk else ks % bk == 0)
    gm, gn, gk = m // bm, n // bn, k // bk

    if mode == "tn":
        a_spec = pl.BlockSpec((bk, bm), lambda i, j, q: (q, i))
    else:
        a_spec = pl.BlockSpec((bm, bk), lambda i, j, q: (i, q))
    if mode == "nt":
        if per_blk:
            b_spec = pl.BlockSpec((per_blk, bn, ks), lambda i, j, q: (q, j, 0))
        elif stacked:
            per = ks // bk
            b_spec = pl.BlockSpec((None, bn, bk), lambda i, j, q: (q // per, j, q % per))
        else:
            b_spec = pl.BlockSpec((bn, bk), lambda i, j, q: (j, q))
    else:
        if stacked:
            per = ns // bn
            b_spec = pl.BlockSpec((None, bk, bn), lambda i, j, q: (j // per, q, j % per))
        else:
            b_spec = pl.BlockSpec((bk, bn), lambda i, j, q: (q, j))
    ex_spec = pl.BlockSpec((bm, bn), lambda i, j, q: (i, j))
    if out_stack:
        per_o = ns_out // bn
        o_spec = pl.BlockSpec((None, bm, bn), lambda i, j, q: (j // per_o, i, j % per_o))
        o_shape = (out_stack, m, ns_out)
    else:
        o_spec = ex_spec
        o_shape = (m, n)
    n_ex, n_out = len(extras) + len(vecs), len(out_dtypes)

    def body(a_ref, b_ref, *rest):
        ex, outs = rest[:n_ex], rest[n_ex:n_ex + n_out]
        if per_blk:
            part = sum(lax.dot_general(a_ref[:, t * ks:(t + 1) * ks], b_ref[t], dn, preferred_element_type=F32)
                       for t in range(per_blk))
        else:
            part = lax.dot_general(a_ref[...], b_ref[...], dn, preferred_element_type=F32)

        def finish(acc):
            res = epilogue(acc, *[e[...] for e in ex]) if epilogue else (acc,)
            for o, r in zip(outs, res):
                o[...] = r.astype(o.dtype)

        if gk == 1:
            finish(part)
        else:
            acc_ref = rest[-1]
            q = pl.program_id(2)

            @pl.when(q == 0)
            def _():
                acc_ref[...] = part

            @pl.when(q > 0)
            def _():
                acc_ref[...] += part

            @pl.when(q == gk - 1)
            def _():
                finish(acc_ref[...])

    res, c_res = _call(
        body, name=name, grid=(gm, gn, gk),
        in_specs=[a_spec, b_spec] + [ex_spec] * len(extras)
        + [pl.BlockSpec((1, bn), lambda i, j, q: (0, j))] * len(vecs),
        out_specs=[o_spec] * n_out,
        out_shape=[jax.ShapeDtypeStruct(o_shape, dt) for dt in out_dtypes],
        scratch_shapes=[pltpu.VMEM((bm, bn), F32)] if gk > 1 else [],
        sem=("parallel", "parallel", "arbitrary"), args=(a, b, *extras, *vecs), comm=comm, after=after)
    res = res[0] if n_out == 1 else res
    return res if comm is None else (res, c_res)


def _rms_fwd(x, g, *, name, tm=256, comm=None):
    s, d = x.shape
    tm = min(tm, s)

    def body(x_ref, g_ref, o_ref):
        xf = x_ref[...]
        r = lax.rsqrt(jnp.mean(xf * xf, axis=-1, keepdims=True) + EPS)
        o_ref[...] = (xf * r * g_ref[...]).astype(o_ref.dtype)

    res, c_res = _call(
        body, name=name, grid=(s // tm,),
        in_specs=[pl.BlockSpec((tm, d), lambda i: (i, 0)), pl.BlockSpec((1, d), lambda i: (0, 0))],
        out_specs=[pl.BlockSpec((tm, d), lambda i: (i, 0))],
        out_shape=[jax.ShapeDtypeStruct((s, d), BF16)],
        sem=("parallel",), args=(x, g), comm=comm)
    return res[0] if comm is None else (res[0], c_res)


def _rms_bwd(x, dy, g, add, *, name, want_bf16, tm=256):
    s, d = x.shape
    tm = min(tm, s)

    def body(x_ref, dy_ref, g_ref, add_ref, dx_ref, *rest):
        dg_ref = rest[-1]
        i = pl.program_id(0)
        xf = x_ref[...]
        dyf = dy_ref[...].astype(F32)
        r = lax.rsqrt(jnp.mean(xf * xf, axis=-1, keepdims=True) + EPS)
        xh = xf * r
        dyg = dyf * g_ref[...]
        dx = r * (dyg - xh * jnp.mean(dyg * xh, axis=-1, keepdims=True))
        tot = add_ref[...] + dx
        dx_ref[...] = tot
        if want_bf16:
            rest[0][...] = tot.astype(BF16)
        part = jnp.sum(dyf * xh, axis=0, keepdims=True)

        @pl.when(i == 0)
        def _():
            dg_ref[...] = part

        @pl.when(i > 0)
        def _():
            dg_ref[...] += part

    row = pl.BlockSpec((tm, d), lambda i: (i, 0))
    vec = pl.BlockSpec((1, d), lambda i: (0, 0))
    out_specs = [row] + ([row] if want_bf16 else []) + [vec]
    out_shape = [jax.ShapeDtypeStruct((s, d), F32)]
    if want_bf16:
        out_shape.append(jax.ShapeDtypeStruct((s, d), BF16))
    out_shape.append(jax.ShapeDtypeStruct((1, d), F32))
    return pl.pallas_call(
        body,
        name=name,
        grid=(s // tm,),
        in_specs=[row, row, vec, row],
        out_specs=out_specs,
        out_shape=out_shape,
        compiler_params=_params(("arbitrary",)),
    )(x, dy, g, add)


def _tail(h2, gate, pp, target, g_ple, g_final, *, tm=256):
    s, d = h2.shape
    tm = min(tm, s)

    def body(h2_ref, gate_ref, pp_ref, t_ref, gp_ref, gf_ref, dh3_ref, dz_ref, dpp_ref, dgf_ref, dgp_ref, loss_ref):
        i = pl.program_id(0)
        ppf = pp_ref[...]
        gate_v = gate_ref[...]
        r_p = lax.rsqrt(jnp.mean(ppf * ppf, axis=-1, keepdims=True) + EPS)
        eh = ppf * r_p
        e = eh * gp_ref[...]
        h3 = h2_ref[...] + gate_v * e
        r_f = lax.rsqrt(jnp.mean(h3 * h3, axis=-1, keepdims=True) + EPS)
        yh = h3 * r_f
        diff = yh * gf_ref[...] - t_ref[...]
        loss_part = 0.5 * jnp.sum(jnp.mean(diff * diff, axis=-1, keepdims=True), axis=0, keepdims=True)
        dy = diff / d
        dgf = jnp.sum(dy * yh, axis=0, keepdims=True)
        dyg = dy * gf_ref[...]
        dh3 = r_f * (dyg - yh * jnp.mean(dyg * yh, axis=-1, keepdims=True))
        dh3_ref[...] = dh3
        de = dh3 * gate_v
        dz_ref[...] = (dh3 * e * gate_v * (1.0 - gate_v)).astype(BF16)
        dgp = jnp.sum(de * eh, axis=0, keepdims=True)
        deg = de * gp_ref[...]
        dpp_ref[...] = (r_p * (deg - eh * jnp.mean(deg * eh, axis=-1, keepdims=True))).astype(BF16)
        loss_row = jnp.broadcast_to(loss_part, (1, 128))

        @pl.when(i == 0)
        def _():
            dgf_ref[...] = dgf
            dgp_ref[...] = dgp
            loss_ref[...] = loss_row

        @pl.when(i > 0)
        def _():
            dgf_ref[...] += dgf
            dgp_ref[...] += dgp
            loss_ref[...] += loss_row

    row = pl.BlockSpec((tm, d), lambda i: (i, 0))
    vec = pl.BlockSpec((1, d), lambda i: (0, 0))
    return pl.pallas_call(
        body,
        name="tail_fwd_bwd",
        grid=(s // tm,),
        in_specs=[row, row, row, row, vec, vec],
        out_specs=[row, row, row, vec, vec, pl.BlockSpec((1, 128), lambda i: (0, 0))],
        out_shape=[
            jax.ShapeDtypeStruct((s, d), F32),
            jax.ShapeDtypeStruct((s, d), BF16),
            jax.ShapeDtypeStruct((s, d), BF16),
            jax.ShapeDtypeStruct((1, d), F32),
            jax.ShapeDtypeStruct((1, d), F32),
            jax.ShapeDtypeStruct((1, 128), F32),
        ],
        compiler_params=_params(("arbitrary",)),
    )(h2, gate, pp, target, g_ple, g_final)


def _rope_tables(s):
    rows = s // GRID_W
    half = HEAD_DIM // 2
    inv_freq = ROPE_THETA ** (-jnp.arange(0, half, 2, dtype=F32) / half)
    ang_r = jnp.arange(rows, dtype=jnp.int32).astype(F32)[:, None] * inv_freq
    ang_c = jnp.arange(GRID_W, dtype=jnp.int32).astype(F32)[:, None] * inv_freq
    cr, sr = (jnp.repeat(t, GRID_W, axis=0) for t in (jnp.cos(ang_r), jnp.sin(ang_r)))
    cc, sc = (jnp.tile(t, (rows, 1)) for t in (jnp.cos(ang_c), jnp.sin(ang_c)))
    cos_t = jnp.concatenate([cr, cr, cc, cc], axis=-1)
    sin_t = jnp.concatenate([-sr, sr, -sc, sc], axis=-1)
    return cos_t, sin_t


def _swap_quarters(x):
    lane = lax.broadcasted_iota(jnp.int32, x.shape, x.ndim - 1)
    up = pltpu.roll(x, HEAD_DIM - 32, x.ndim - 1)
    down = pltpu.roll(x, 32, x.ndim - 1)
    return jnp.where((lane % 64) < 32, up, down)


def _cols(first, count=1):
    return slice(first * HEAD_DIM, (first + count) * HEAD_DIM)


def _qk_prep(proj, g_q, g_k, cos_t, sin_t, *, tm=256, comm=None):
    s, n = proj.shape
    tm = min(tm, s)

    def body(x_ref, gq_ref, gk_ref, c_ref, s_ref, o_ref):
        cos_v, sin_v = c_ref[...], s_ref[...]
        for h in range(COL_VA):
            x = x_ref[:, _cols(h)]
            g = gq_ref[...] if h < COL_KA else gk_ref[...]
            xn = x * lax.rsqrt(jnp.mean(x * x, axis=-1, keepdims=True) + EPS) * g
            xr = xn * cos_v + _swap_quarters(xn) * sin_v
            if h < COL_KA:
                xr = xr * Q_SCALE
            o_ref[:, _cols(h)] = xr.astype(BF16)
        o_ref[:, _cols(COL_VA, 2)] = x_ref[:, _cols(COL_VA, 2)].astype(BF16)
        o_ref[:, _cols(COL_QB, N_HEADS_B)] = (x_ref[:, _cols(COL_QB, N_HEADS_B)] * Q_SCALE).astype(BF16)
        o_ref[:, _cols(COL_KB, 4)] = x_ref[:, _cols(COL_KB, 4)].astype(BF16)

    row = pl.BlockSpec((tm, n), lambda i: (i, 0))
    tab = pl.BlockSpec((tm, HEAD_DIM), lambda i: (i, 0))
    vec = pl.BlockSpec((1, HEAD_DIM), lambda i: (0, 0))
    res, c_res = _call(
        body, name="qk_prep", grid=(s // tm,),
        in_specs=[row, vec, vec, tab, tab],
        out_specs=[row],
        out_shape=[jax.ShapeDtypeStruct((s, n), BF16)],
        sem=("parallel",), args=(proj, g_q, g_k, cos_t, sin_t), comm=comm)
    return res[0] if comm is None else (res[0], c_res)


def _qk_bwd(dqa, dka, dva, dqb, dkpad, dvpad, proj, g_q, g_k, cos_t, sin_t, *, comm=None, after=None):
    s, n = proj.shape
    tm = min(PAD_LO, s)
    assert PAD_LO % tm == 0
    lo = PAD_LO // tm

    def body(dqa_ref, dka_ref, dva_ref, dqb_ref, dkb_ref, dvb_ref, x_ref, gq_ref, gk_ref, c_ref, s_ref,
             o_ref, dgq_ref, dgk_ref):
        i = pl.program_id(0)
        cos_v, sin_v = c_ref[...], s_ref[...]

        def head(d, x, g):
            dn = d * cos_v + _swap_quarters(d * sin_v)
            r = lax.rsqrt(jnp.mean(x * x, axis=-1, keepdims=True) + EPS)
            xh = x * r
            dng = dn * g
            dx = r * (dng - xh * jnp.mean(dng * xh, axis=-1, keepdims=True))
            return dx.astype(BF16), jnp.sum(dn * xh, axis=0, keepdims=True)

        acc_q = jnp.zeros((1, HEAD_DIM), F32)
        acc_k = jnp.zeros((1, HEAD_DIM), F32)
        for h in range(N_HEADS_A):
            o_ref[:, _cols(h)], part = head(dqa_ref[:, _cols(h)] * ATT_SCALE, x_ref[:, _cols(h)], gq_ref[...])
            acc_q = acc_q + part
        for h in range(N_KV_A):
            o_ref[:, _cols(COL_KA + h)], part = head(dka_ref[:, _cols(h)] * LN2, x_ref[:, _cols(COL_KA + h)],
                                                     gk_ref[...])
            acc_k = acc_k + part
        o_ref[:, _cols(COL_VA, 2)] = dva_ref[...].astype(BF16)
        o_ref[:, _cols(COL_QB, N_HEADS_B)] = (dqb_ref[...] * ATT_SCALE).astype(BF16)
        o_ref[:, _cols(COL_KB, 2)] = (dkb_ref[...] * LN2).astype(BF16)
        o_ref[:, _cols(COL_VB, 2)] = dvb_ref[...].astype(BF16)

        @pl.when(i == 0)
        def _():
            dgq_ref[...] = acc_q
            dgk_ref[...] = acc_k

        @pl.when(i > 0)
        def _():
            dgq_ref[...] += acc_q
            dgk_ref[...] += acc_k

    def rows(width, shift=0):
        return pl.BlockSpec((tm, width), lambda i: (i + shift, 0))

    kv_w = N_KV_A * HEAD_DIM
    q_w = N_HEADS_A * HEAD_DIM
    vec = pl.BlockSpec((1, HEAD_DIM), lambda i: (0, 0))
    res, c_res = _call(
        body, name="qk_bwd", grid=(s // tm,),
        in_specs=[rows(q_w), rows(kv_w), rows(kv_w), rows(q_w), rows(kv_w, lo), rows(kv_w, lo), rows(n),
                  vec, vec, rows(HEAD_DIM), rows(HEAD_DIM)],
        out_specs=[rows(n), vec, vec],
        out_shape=[
            jax.ShapeDtypeStruct((s, n), BF16),
            jax.ShapeDtypeStruct((1, HEAD_DIM), F32),
            jax.ShapeDtypeStruct((1, HEAD_DIM), F32),
        ],
        sem=("arbitrary",), args=(dqa, dka, dva, dqb, dkpad, dvpad, proj, g_q, g_k, cos_t, sin_t), comm=comm,
        after=after)
    return res if comm is None else (res, c_res)


_NT = (((1,), (1,)), ((), ()))
_TN = (((0,), (0,)), ((), ()))


def _attn_a_fwd(pb, *, tq=4096, sub=256, comm=None, after=None):
    s = pb.shape[0]
    tq = min(tq, s)
    sub = min(sub, tq)

    def body(q_ref, k_ref, v_ref, o_ref, lse_ref):
        k = k_ref[...]
        v = v_ref[...]
        for r in range(tq // sub):
            rows = pl.ds(r * sub, sub)
            sc = lax.dot_general(q_ref[rows, :], k, _NT, preferred_element_type=F32)
            m = jnp.max(sc, axis=-1, keepdims=True)
            p = jnp.exp2(sc - m)
            l = jnp.sum(p, axis=-1, keepdims=True)
            o = jnp.dot(p.astype(BF16), v, preferred_element_type=F32)
            o_ref[rows, :] = (o / l).astype(BF16)
            lse_ref[rows, :] = jnp.broadcast_to(m + jnp.log2(l), (sub, HEAD_DIM))

    res, c_res = _call(
        body, name="attn_a_fwd", grid=(N_HEADS_A, s // tq),
        in_specs=[
            pl.BlockSpec((tq, HEAD_DIM), lambda h, i: (i, COL_QA + h)),
            pl.BlockSpec((s, HEAD_DIM), lambda h, i: (0, COL_KA + h // GROUP)),
            pl.BlockSpec((s, HEAD_DIM), lambda h, i: (0, COL_VA + h // GROUP)),
        ],
        out_specs=[
            pl.BlockSpec((tq, HEAD_DIM), lambda h, i: (i, h)),
            pl.BlockSpec((None, tq, HEAD_DIM), lambda h, i: (h, i, 0)),
        ],
        out_shape=[
            jax.ShapeDtypeStruct((s, (N_HEADS_A + N_HEADS_B) * HEAD_DIM), BF16),
            jax.ShapeDtypeStruct((N_HEADS_A, s, HEAD_DIM), F32),
        ],
        sem=("parallel", "parallel"), args=(pb, pb, pb), comm=comm, after=after)
    return res if comm is None else (res, c_res)


def _attn_a_bwd(pb, att, datt, lse, *, tq=1024, sub=256, comm=None):
    s = pb.shape[0]
    tq = min(tq, s)
    sub = min(sub, tq)

    def body(q_ref, k_ref, v_ref, o_ref, do_ref, lse_ref, dq_ref, dk_ref, dv_ref):
        first = jnp.logical_and(pl.program_id(1) == 0, pl.program_id(2) == 0)
        k = k_ref[...]
        v = v_ref[...]
        dk = dv = None
        for r in range(tq // sub):
            rows = pl.ds(r * sub, sub)
            q = q_ref[rows, :]
            do = do_ref[rows, :]
            sc = lax.dot_general(q, k, _NT, preferred_element_type=F32)
            p = jnp.exp2(sc - lse_ref[rows, :][:, :1])
            dp = lax.dot_general(do, v, _NT, preferred_element_type=F32)
            delta = jnp.sum(do.astype(F32) * o_ref[rows, :].astype(F32), axis=-1, keepdims=True)
            ds = (p * (dp - delta)).astype(BF16)
            dq_ref[rows, :] = jnp.dot(ds, k, preferred_element_type=F32)
            dk_r = lax.dot_general(ds, q, _TN, preferred_element_type=F32)
            dv_r = lax.dot_general(p.astype(BF16), do, _TN, preferred_element_type=F32)
            dk = dk_r if dk is None else dk + dk_r
            dv = dv_r if dv is None else dv + dv_r

        @pl.when(first)
        def _():
            dk_ref[...] = dk
            dv_ref[...] = dv

        @pl.when(jnp.logical_not(first))
        def _():
            dk_ref[...] += dk
            dv_ref[...] += dv

    qmap = lambda kv, g, i: (i, kv * GROUP + g)
    res, c_res = _call(
        body, name="attn_a_bwd", grid=(N_KV_A, GROUP, s // tq),
        in_specs=[
            pl.BlockSpec((tq, HEAD_DIM), lambda kv, g, i: (i, COL_QA + kv * GROUP + g)),
            pl.BlockSpec((s, HEAD_DIM), lambda kv, g, i: (0, COL_KA + kv)),
            pl.BlockSpec((s, HEAD_DIM), lambda kv, g, i: (0, COL_VA + kv)),
            pl.BlockSpec((tq, HEAD_DIM), qmap),
            pl.BlockSpec((tq, HEAD_DIM), qmap),
            pl.BlockSpec((None, tq, HEAD_DIM), lambda kv, g, i: (kv * GROUP + g, i, 0)),
        ],
        out_specs=[
            pl.BlockSpec((tq, HEAD_DIM), qmap),
            pl.BlockSpec((s, HEAD_DIM), lambda kv, g, i: (0, kv)),
            pl.BlockSpec((s, HEAD_DIM), lambda kv, g, i: (0, kv)),
        ],
        out_shape=[
            jax.ShapeDtypeStruct((s, N_HEADS_A * HEAD_DIM), F32),
            jax.ShapeDtypeStruct((s, N_KV_A * HEAD_DIM), F32),
            jax.ShapeDtypeStruct((s, N_KV_A * HEAD_DIM), F32),
        ],
        sem=("arbitrary", "arbitrary", "arbitrary"), args=(pb, pb, pb, att, datt, lse), comm=comm)
    return res if comm is None else (res, c_res)


def _t5_bucket(rel):
    nb = N_BUCKETS // 2
    ret = jnp.where(rel > 0, nb, 0)
    n = jnp.abs(rel)
    max_exact = nb // 2
    nf = jnp.maximum(n, 1).astype(F32)
    large = max_exact + (jnp.log(nf / max_exact) / math.log(MAX_DISTANCE / max_exact)
                         * (nb - max_exact)).astype(jnp.int32)
    large = jnp.minimum(large, nb - 1)
    return ret + jnp.where(n < max_exact, n, large)


def _band_buckets():
    r = jnp.arange(BLOCK_Q, dtype=jnp.int32)
    j = jnp.arange(3 * BLOCK_Q, dtype=jnp.int32)
    return _t5_bucket((j[None, :] - BLOCK_Q) - r[:, None])


def _band_bias(bucket, table_ref, h):
    acc = jnp.zeros(bucket.shape, F32)
    for b in range(N_BUCKETS):
        acc = jnp.where(bucket == b, table_ref[b, h], acc)
    return acc


GQ = GROUP * BLOCK_Q


def _stack_heads(x):
    return jnp.concatenate([x[:, _cols(g)] for g in range(GROUP)], axis=0)


def _unstack_heads(x):
    return jnp.concatenate([x[g * BLOCK_Q:(g + 1) * BLOCK_Q] for g in range(GROUP)], axis=1)


def _group_bias(bucket, table_ref, kv):
    return jnp.concatenate([_band_bias(bucket, table_ref, kv * GROUP + g) * LOG2E for g in range(GROUP)], axis=0)


def _group_sink(sink_ref, kv):
    head = lax.broadcasted_iota(jnp.int32, (GQ, 1), 0) // BLOCK_Q
    snk = jnp.zeros((GQ, 1), F32)
    for g in range(GROUP):
        snk = jnp.where(head == g, sink_ref[0, kv * GROUP + g] * LOG2E, snk)
    return snk


def _band_mask(n, s):
    r = lax.broadcasted_iota(jnp.int32, (GQ, 3 * BLOCK_Q), 0) % BLOCK_Q
    j = lax.broadcasted_iota(jnp.int32, (GQ, 3 * BLOCK_Q), 1)
    rel = j - BLOCK_Q - r
    kabs = n * BLOCK_Q + j - BLOCK_Q
    return (jnp.abs(rel) <= WINDOW) & (kabs >= 0) & (kabs < s)


def _band_start(n):
    return pl.multiple_of(n * BLOCK_Q + (PAD_LO - BLOCK_Q), BLOCK_Q)


def _attn_b_fwd(pb, kpad, vpad, bucket, table, sink, att, *, comm=None, after=None):
    s = pb.shape[0]
    nblk = s // BLOCK_Q
    sp = kpad.shape[0]

    def body(table_ref, sink_ref, q0_ref, q1_ref, k_ref, v_ref, bucket_ref, _, o_ref, lse_ref, bias_ref):
        n = pl.program_id(0)

        @pl.when(n == 0)
        def _():
            for kv in range(N_KV_B):
                bias_ref[kv * GQ:(kv + 1) * GQ, :] = _group_bias(bucket_ref[...], table_ref, kv)

        band = pl.ds(_band_start(n), 3 * BLOCK_Q)
        mask = _band_mask(n, s)
        for kv, q_ref in enumerate((q0_ref, q1_ref)):
            kb = k_ref[band, _cols(kv)]
            vb = v_ref[band, _cols(kv)]
            sc = lax.dot_general(_stack_heads(q_ref[...]), kb, _NT, preferred_element_type=F32)
            sc = jnp.where(mask, sc + bias_ref[kv * GQ:(kv + 1) * GQ, :], NEG_INF)
            snk = _group_sink(sink_ref, kv)
            m = jnp.maximum(jnp.max(sc, axis=-1, keepdims=True), snk)
            p = jnp.exp2(sc - m)
            l = jnp.sum(p, axis=-1, keepdims=True) + jnp.exp2(snk - m)
            o = jnp.dot(p.astype(BF16), vb, preferred_element_type=F32)
            o_ref[:, _cols(kv * GROUP, GROUP)] = _unstack_heads((o / l).astype(BF16))
            lse = m + jnp.log2(l)
            for g in range(GROUP):
                lse_ref[kv * GROUP + g] = jnp.broadcast_to(lse[g * BLOCK_Q:(g + 1) * BLOCK_Q], (BLOCK_Q, HEAD_DIM))

    smem = pl.BlockSpec(memory_space=pltpu.SMEM)
    wide = GROUP * HEAD_DIM
    whole = pl.BlockSpec((sp, N_KV_B * HEAD_DIM), lambda n: (0, 0))
    res, c_res = _call(
        body, name="attn_b_fwd", grid=(nblk,),
        in_specs=[
            smem,
            smem,
            pl.BlockSpec((BLOCK_Q, wide), lambda n: (n, COL_QB // GROUP)),
            pl.BlockSpec((BLOCK_Q, wide), lambda n: (n, COL_QB // GROUP + 1)),
            whole,
            whole,
            pl.BlockSpec((BLOCK_Q, 3 * BLOCK_Q), lambda n: (0, 0)),
            _ANY,
        ],
        out_specs=[
            pl.BlockSpec((BLOCK_Q, N_HEADS_B * HEAD_DIM), lambda n: (n, 1)),
            pl.BlockSpec((N_HEADS_B, BLOCK_Q, HEAD_DIM), lambda n: (0, n, 0)),
        ],
        out_shape=[
            jax.ShapeDtypeStruct(att.shape, BF16),
            jax.ShapeDtypeStruct((N_HEADS_B, s, HEAD_DIM), F32),
        ],
        scratch_shapes=[pltpu.VMEM((N_KV_B * GQ, 3 * BLOCK_Q), F32)],
        sem=("arbitrary",), args=(table, sink, pb, pb, kpad, vpad, bucket, att), comm=comm, after=after,
        aliases={7: 0})
    return res if comm is None else (res, c_res)


def _attn_b_bwd(pb, kpad, vpad, att, datt, lse, bucket, table, sink, *, comm=None, after=None):
    s = pb.shape[0]
    nblk = s // BLOCK_Q
    sp = kpad.shape[0]

    def body(table_ref, sink_ref, q0_ref, q1_ref, k_ref, v_ref, o_ref, do_ref, lse_ref, bucket_ref,
             dq_ref, dk_ref, dv_ref, dtab_ref, dsink_ref, bias_ref, dbias_ref):
        n = pl.program_id(0)

        @pl.when(n == 0)
        def _():
            dk_ref[...] = jnp.zeros_like(dk_ref)
            dv_ref[...] = jnp.zeros_like(dv_ref)
            dbias_ref[...] = jnp.zeros_like(dbias_ref)
            dsink_ref[...] = jnp.zeros_like(dsink_ref)
            for kv in range(N_KV_B):
                bias_ref[kv * GQ:(kv + 1) * GQ, :] = _group_bias(bucket_ref[...], table_ref, kv)

        band = pl.ds(_band_start(n), 3 * BLOCK_Q)
        mask = _band_mask(n, s)
        for kv, q_ref in enumerate((q0_ref, q1_ref)):
            wide_cols = _cols(kv * GROUP, GROUP)
            q = _stack_heads(q_ref[...])
            do = _stack_heads(do_ref[:, wide_cols])
            o = _stack_heads(o_ref[:, wide_cols])
            kb = k_ref[band, _cols(kv)]
            vb = v_ref[band, _cols(kv)]
            lse = jnp.concatenate([lse_ref[kv * GROUP + g][:, :1] for g in range(GROUP)], axis=0)
            sc = lax.dot_general(q, kb, _NT, preferred_element_type=F32)
            sc = jnp.where(mask, sc + bias_ref[kv * GQ:(kv + 1) * GQ, :], NEG_INF)
            p = jnp.exp2(sc - lse)
            dp = lax.dot_general(do, vb, _NT, preferred_element_type=F32)
            delta = jnp.sum(do.astype(F32) * o.astype(F32), axis=-1, keepdims=True)
            ds = p * (dp - delta)
            dsb = ds.astype(BF16)
            dq_ref[:, wide_cols] = _unstack_heads(jnp.dot(dsb, kb, preferred_element_type=F32))
            dk_ref[band, _cols(kv)] += lax.dot_general(dsb, q, _TN, preferred_element_type=F32)
            dv_ref[band, _cols(kv)] += lax.dot_general(p.astype(BF16), do, _TN, preferred_element_type=F32)
            dbias_ref[kv * GQ:(kv + 1) * GQ, :] += ds
            sink_part = -jnp.exp2(_group_sink(sink_ref, kv) - lse) * delta
            for g in range(GROUP):
                rows = slice(g * BLOCK_Q, (g + 1) * BLOCK_Q)
                dsink_ref[kv * GROUP + g] += jnp.broadcast_to(
                    jnp.sum(sink_part[rows], axis=0, keepdims=True), (1, HEAD_DIM))

        @pl.when(n == nblk - 1)
        def _():
            bucket_v = bucket_ref[...]
            row = lax.broadcasted_iota(jnp.int32, (N_BUCKETS, HEAD_DIM), 0)
            for h in range(N_HEADS_B):
                acc = dbias_ref[h * BLOCK_Q:(h + 1) * BLOCK_Q, :]
                tot = jnp.zeros((N_BUCKETS, HEAD_DIM), F32)
                for b in range(N_BUCKETS):
                    tot = jnp.where(row == b, jnp.sum(jnp.where(bucket_v == b, acc, 0.0), keepdims=True), tot)
                dtab_ref[h] = tot

    smem = pl.BlockSpec(memory_space=pltpu.SMEM)
    wide = GROUP * HEAD_DIM
    whole = pl.BlockSpec((sp, N_KV_B * HEAD_DIM), lambda n: (0, 0))
    group_b = pl.BlockSpec((BLOCK_Q, N_HEADS_B * HEAD_DIM), lambda n: (n, 1))
    res, c_res = _call(
        body, name="attn_b_bwd", grid=(nblk,),
        in_specs=[
            smem,
            smem,
            pl.BlockSpec((BLOCK_Q, wide), lambda n: (n, COL_QB // GROUP)),
            pl.BlockSpec((BLOCK_Q, wide), lambda n: (n, COL_QB // GROUP + 1)),
            whole,
            whole,
            group_b,
            group_b,
            pl.BlockSpec((N_HEADS_B, BLOCK_Q, HEAD_DIM), lambda n: (0, n, 0)),
            pl.BlockSpec((BLOCK_Q, 3 * BLOCK_Q), lambda n: (0, 0)),
        ],
        out_specs=[
            pl.BlockSpec((BLOCK_Q, N_HEADS_B * HEAD_DIM), lambda n: (n, 0)),
            whole,
            whole,
            pl.BlockSpec((N_HEADS_B, N_BUCKETS, HEAD_DIM), lambda n: (0, 0, 0)),
            pl.BlockSpec((N_HEADS_B, 1, HEAD_DIM), lambda n: (0, 0, 0)),
        ],
        out_shape=[
            jax.ShapeDtypeStruct((s, N_HEADS_B * HEAD_DIM), F32),
            jax.ShapeDtypeStruct((sp, N_KV_B * HEAD_DIM), F32),
            jax.ShapeDtypeStruct((sp, N_KV_B * HEAD_DIM), F32),
            jax.ShapeDtypeStruct((N_HEADS_B, N_BUCKETS, HEAD_DIM), F32),
            jax.ShapeDtypeStruct((N_HEADS_B, 1, HEAD_DIM), F32),
        ],
        scratch_shapes=[pltpu.VMEM((N_KV_B * GQ, 3 * BLOCK_Q), F32), pltpu.VMEM((N_KV_B * GQ, 3 * BLOCK_Q), F32)],
        sem=("arbitrary",),
        args=(table, sink, pb, pb, kpad, vpad, att, datt, lse, bucket), comm=comm, after=after)
    return res if comm is None else (res, c_res)


def _other_chips(x, y):
    return [(x, 1 - y), (1 - x, y), (1 - x, 1 - y)]


_HBM = pl.BlockSpec(memory_space=pltpu.HBM)
_SEM = pl.BlockSpec(memory_space=pltpu.SEMAPHORE)
_SPLIT = pltpu.CompilerParams(has_side_effects=pltpu.SideEffectType.DATAFLOW_SIDE_EFFECTING)


def _in_hbm(a):
    return pltpu.with_memory_space_constraint(a, pltpu.HBM)


def _my_half(rows):
    c = lax.axis_index("c")
    half = rows // 2
    return pl.ds(pl.multiple_of(c * half, half), half), pl.ds(pl.multiple_of((1 - c) * half, half), half)


def _gather_route(shapes):
    def route(src, land):
        x, y, c = lax.axis_index("x"), lax.axis_index("y"), lax.axis_index("c")
        out = []
        for t, shape in enumerate(shapes):
            mine, _ = _my_half(shape[0])
            for px, py in _other_chips(x, y):
                out.append((src[t].at[mine], land[t].at[2 * x + y, mine], land[t].at[2 * px + py, mine], (px, py, c)))
        return out

    return route


def _exchange_route(n_t):
    def route(src, land):
        x, y, c = lax.axis_index("x"), lax.axis_index("y"), lax.axis_index("c")
        out = []
        for t in range(n_t):
            for px, py in _other_chips(x, y):
                k = 2 * px + py
                out.append((src[t].at[k], land[t].at[2 * (2 * x + y) + c], land[t].at[2 * k + c], (px, py, c)))
        return out

    return route


def _own_slot(shape, dtype, slot, block):
    return lax.dynamic_update_slice(lax.empty(shape, dtype), block[None], (slot,) + (0,) * (len(shape) - 1))


def _split_start(name, srcs, lands, route, after):
    n = len(srcs)

    def body(*refs):
        src, land, send_sems, recv_sems, token = refs[:n], refs[n:2 * n], refs[2 * n + 1], refs[2 * n + 2], refs[-1]
        for i, (src_ref, dst_ref, _, to) in enumerate(route(src, land)):
            pltpu.make_async_remote_copy(src_ref=src_ref, dst_ref=dst_ref, send_sem=send_sems.at[i],
                                         recv_sem=recv_sems.at[i], device_id=to, device_id_type=_MESH).start()
        token[...] = jnp.zeros_like(token)

    sem = pltpu.SemaphoreType.DMA((3 * n,))
    lands = list(lands)
    res = pl.pallas_call(
        body, name=name,
        in_specs=[_HBM] * (2 * n) + [_ANY],
        out_specs=[_SEM, _SEM] + [_HBM] * (2 * n) + [pl.BlockSpec(memory_space=pltpu.VMEM)],
        out_shape=[sem, sem] + [pltpu.HBM(a.shape, a.dtype) for a in list(srcs) + lands]
        + [jax.ShapeDtypeStruct((8, 128), F32)],
        input_output_aliases={i: 2 + i for i in range(2 * n)},
        compiler_params=_SPLIT,
    )(*[_in_hbm(a) for a in srcs], *[_in_hbm(a) for a in lands], after)
    return (res[0], res[1]), res[2:2 + n], res[2 + n:2 + 2 * n], res[-1]


def _split_wait(name, srcs, lands, sems, route, after):
    n = len(srcs)

    def body(*refs):
        src, land, send_sems, recv_sems = refs[:n], refs[n:2 * n], refs[2 * n], refs[2 * n + 1]
        for i, (src_ref, _, dst_ref, to) in enumerate(route(src, land)):
            cp = pltpu.make_async_remote_copy(src_ref=src_ref, dst_ref=dst_ref, send_sem=send_sems.at[i],
                                              recv_sem=recv_sems.at[i], device_id=to, device_id_type=_MESH)
            cp.wait_send()
            cp.wait_recv()

    res = pl.pallas_call(
        body, name=name,
        in_specs=[_HBM] * (2 * n) + [_SEM, _SEM, _ANY],
        out_specs=[_HBM] * (2 * n),
        out_shape=[pltpu.HBM(a.shape, a.dtype) for a in list(srcs) + list(lands)],
        input_output_aliases={i: i for i in range(2 * n)},
        compiler_params=_SPLIT,
    )(*srcs, *lands, sems[0], sems[1], after)
    return res[:n], res[n:]


def _comm_only(name, comm):
    return _call(lambda: None, name=name, grid=(1,), in_specs=[], out_specs=[], out_shape=[], args=(), comm=comm)[1]


def _swap_comm(shards, lands):
    n_t = len(lands)

    def copies(land, sems, later):
        send_sems, recv_sems = sems
        x, y = lax.axis_index("x"), lax.axis_index("y")
        sends, recvs = [], []
        for t in range(n_t):
            mine, other = _my_half(shards[t].shape[0])
            for j, (px, py) in enumerate(_other_chips(x, y)):
                k = 2 * px + py
                for part, out in ((mine, sends), (other, recvs)) if later else ((mine, sends),):
                    out.append(pltpu.make_async_remote_copy(
                        src_ref=land[t].at[k, part], dst_ref=land[t].at[k, part], send_sem=send_sems.at[3 * t + j],
                        recv_sem=recv_sems.at[3 * t + j], device_id=_sibling(), device_id_type=_MESH))
        return sends, recvs

    def start(ins, land, sems):
        for cp in copies(land, sems, False)[0]:
            cp.start()

    def finish(ins, land, sems):
        sends, recvs = copies(land, sems, True)
        for cp in recvs:
            cp.wait_recv()
        for cp in sends:
            cp.wait_send()

    return _Comm(
        lands, [jax.ShapeDtypeStruct(a.shape, a.dtype) for a in lands],
        [pltpu.SemaphoreType.DMA((3 * n_t,)), pltpu.SemaphoreType.DMA((3 * n_t,))],
        start, finish, aliases={t: t for t in range(n_t)})


def _forward_comm(partials, lands):
    n_t = len(lands)

    def copies(land, sems, later):
        send_sems, recv_sems = sems
        x, y, c = lax.axis_index("x"), lax.axis_index("y"), lax.axis_index("c")
        sends, recvs = [], []
        for t in range(n_t):
            for j, k in enumerate([2 * x + y] + [2 * px + py for px, py in _other_chips(x, y)]):
                for slot, out in ((2 * k + c, sends), (2 * k + 1 - c, recvs)) if later else ((2 * k + c, sends),):
                    out.append(pltpu.make_async_remote_copy(
                        src_ref=land[t].at[slot], dst_ref=land[t].at[slot], send_sem=send_sems.at[4 * t + j],
                        recv_sem=recv_sems.at[4 * t + j], device_id=_sibling(), device_id_type=_MESH))
        return sends, recvs

    def start(ins, land, sems):
        for cp in copies(land, sems, False)[0]:
            cp.start()

    def finish(ins, land, sems):
        sends, recvs = copies(land, sems, True)
        for cp in recvs:
            cp.wait_recv()
        for cp in sends:
            cp.wait_send()

    return _Comm(
        lands, [jax.ShapeDtypeStruct(a.shape, a.dtype) for a in lands],
        [pltpu.SemaphoreType.DMA((4 * n_t,)), pltpu.SemaphoreType.DMA((4 * n_t,))],
        start, finish, aliases={t: t for t in range(n_t)})


def _allreduce_small(pack):
    rows, d = pack.shape

    def body(p_ref, sum_ref, all_ref, send_sems, recv_sems):
        x, y, c = lax.axis_index("x"), lax.axis_index("y"), lax.axis_index("c")
        me = 4 * x + 2 * y + c
        all_ref[me] = p_ref[...]
        peers = []
        for dx in range(2):
            for dy in range(2):
                for dc in range(2):
                    if dx or dy or dc:
                        px = 1 - x if dx else x
                        py = 1 - y if dy else y
                        pc = 1 - c if dc else c
                        peers.append((4 * dx + 2 * dy + dc - 1, (px, py, pc)))
        sends = []
        for k, to in peers:
            cp = pltpu.make_async_remote_copy(
                src_ref=p_ref, dst_ref=all_ref.at[me], send_sem=send_sems.at[k], recv_sem=recv_sems.at[k],
                device_id=to, device_id_type=_MESH)
            cp.start()
            sends.append(cp)
        for k, (px, py, pc) in peers:
            pltpu.make_async_remote_copy(
                src_ref=p_ref, dst_ref=all_ref.at[4 * px + 2 * py + pc], send_sem=send_sems.at[k],
                recv_sem=recv_sems.at[k], device_id=(px, py, pc), device_id_type=_MESH).wait_recv()
        for cp in sends:
            cp.wait_send()
        tot = all_ref[0]
        for i in range(1, N_DEV):
            tot = tot + all_ref[i]
        sum_ref[...] = tot

    vm = pl.BlockSpec(memory_space=pltpu.VMEM)
    return pl.pallas_call(
        body,
        name="allreduce_small",
        in_specs=[vm],
        out_specs=vm,
        out_shape=jax.ShapeDtypeStruct((rows, d), F32),
        scratch_shapes=[
            pltpu.VMEM((N_DEV, rows, d), F32),
            pltpu.SemaphoreType.DMA((N_DEV - 1,)),
            pltpu.SemaphoreType.DMA((N_DEV - 1,)),
        ],
    )(pack)


def _adamw_math(w, g, m, v):
    m = ADAM_B1 * m + (1.0 - ADAM_B1) * g
    v = ADAM_B2 * v + (1.0 - ADAM_B2) * (g * g)
    m_hat = m / (1.0 - ADAM_B1 ** ADAM_STEP)
    v_hat = v / (1.0 - ADAM_B2 ** ADAM_STEP)
    delta = -ADAM_LR * (m_hat / (jnp.sqrt(v_hat) + ADAM_EPS) + ADAM_WD * w)
    return delta, m, v


def _sum_adamw(parts, w, m, v, *, name, tr=256):
    r, c = w.shape
    tr = min(tr, r)
    tc = min(c, 1024)

    def body(p_ref, w_ref, m_ref, v_ref, g_ref, d_ref, m2_ref, v2_ref):
        g = p_ref[0].astype(F32)
        for i in range(1, N_DEV):
            g = g + p_ref[i].astype(F32)
        delta, m2, v2 = _adamw_math(w_ref[...], g, m_ref[...], v_ref[...])
        g_ref[...] = g
        d_ref[...] = delta
        m2_ref[...] = m2
        v2_ref[...] = v2

    blk = pl.BlockSpec((tr, tc), lambda i, j: (i, j))
    return pl.pallas_call(
        body,
        name=name,
        grid=(r // tr, c // tc),
        in_specs=[pl.BlockSpec((N_DEV, tr, tc), lambda i, j: (0, i, j)), blk, blk, blk],
        out_specs=[blk] * 4,
        out_shape=[jax.ShapeDtypeStruct((r, c), F32)] * 4,
        compiler_params=_params(("parallel", "parallel")),
    )(parts, w, m, v)


def _adamw_small(g, w, m, v):
    def body(g_ref, w_ref, m_ref, v_ref, d_ref, m2_ref, v2_ref):
        delta, m2, v2 = _adamw_math(w_ref[...], g_ref[...], m_ref[...], v_ref[...])
        d_ref[...] = delta
        m2_ref[...] = m2
        v2_ref[...] = v2

    vm = pl.BlockSpec(memory_space=pltpu.VMEM)
    return pl.pallas_call(
        body,
        name="adamw_small",
        in_specs=[vm] * 4,
        out_specs=[vm] * 3,
        out_shape=[jax.ShapeDtypeStruct(g.shape, F32)] * 3,
    )(g, w, m, v)


def _relu2_epilogue(acc):
    ra = jnp.maximum(acc, 0.0)
    return ra * ra, ra


def _residual_norm_epilogue(acc, res, g):
    h = acc + res
    return h, h * lax.rsqrt(jnp.mean(h * h, axis=-1, keepdims=True) + EPS) * g


def _rows(stacked):
    return stacked.reshape(stacked.shape[0] * stacked.shape[1], stacked.shape[2])


def _by_chip(mat):
    return mat.reshape(N_CHIPS, mat.shape[0] // N_CHIPS, mat.shape[1])


def _local_step(x, p, target, shards, small, update):
    s, d = x.shape
    cos_t, sin_t = _rope_tables(s)
    bucket = _band_buckets()
    p_bf = p.astype(BF16)
    wts = {}

    chip = 2 * lax.axis_index("x") + lax.axis_index("y")
    core = lax.axis_index("c")

    def gather(tag, names, after):
        srcs = [cast[n] for n in names]
        route = _gather_route([a.shape for a in srcs])
        sems, srcs, lands, token = _split_start(f"gather_start_{tag}", srcs, [zones[n] for n in names], route, after)

        def landed(done):
            got_srcs, got_lands = _split_wait(f"gather_wait_{tag}", srcs, lands, sems, route, done)
            comm = _swap_comm(got_srcs, got_lands)
            comm.waited = got_srcs[0]
            return comm

        return landed, token

    def prepare(n, zero):
        cast[n] = (shards[n] + zero).astype(BF16)
        zones[n] = _own_slot((N_CHIPS,) + cast[n].shape, BF16, chip, cast[n])

    cast, zones = {}, {}
    prepare("w_in", 0.0)
    in_landed, token = gather("in", ["w_in"], small["attn_norm_g"])
    for n in shards:
        if n != "w_in":
            prepare(n, token[:1, :1])
    g_attn = small["attn_norm_g"] + token[:1, :1]
    u = _rms_fwd(x, g_attn, name="norm_attn")
    prepared = u[:1, :1].astype(F32) + sum(
        (lax.dynamic_slice(zones[n], (chip, 0, 0), (1, 1, 1))[0] + cast[n][:1, :1]).astype(F32)
        for n in zones if n != "w_in")
    (wts["w_in"],) = _comm_only("swap_w_in", in_landed(prepared))
    mid_landed, token = gather("mid", ["w_out"], wts["w_in"])
    proj = _matmul(u, wts["w_in"], mode="nn", out_dtypes=[F32], name="mm_in", bn=768, after=token)
    pb, (w_out_s,) = _qk_prep(proj, small["q_norm_g"], small["k_norm_g"], cos_t, sin_t, comm=mid_landed(proj))
    wts["w_out"] = _rows(w_out_s)
    up_landed, token = gather("up", ["w_up"], pb)
    att_a, lse_a = _attn_a_fwd(pb, after=token)
    pad = ((PAD_LO, PAD_HI), (0, 0))
    kpad = jnp.pad(pb[:, COL_KB * HEAD_DIM:COL_VB * HEAD_DIM], pad)
    vpad = jnp.pad(pb[:, COL_VB * HEAD_DIM:], pad)
    up_swap = up_landed(att_a)
    down_landed, token = gather("down", ["w_down"], up_swap.waited)
    (att, lse_b), (wts["w_up"],) = _attn_b_fwd(pb, kpad, vpad, bucket, small["rel_bias_table"],
                                               small["sink_logits"], att_a, comm=up_swap, after=token)
    h1, mn = _matmul(att, wts["w_out"], mode="nn", out_dtypes=[F32, BF16], name="mm_out", bm=512, bn=d,
                     epilogue=_residual_norm_epilogue, extras=(x,), vecs=(small["mlp_norm_g"],))
    r, ra = _matmul(mn, wts["w_up"], mode="nn", out_dtypes=[BF16, BF16], name="mm_up", epilogue=_relu2_epilogue,
                    bm=2048)
    (w_down_s,) = _comm_only("swap_w_down", down_landed(r))
    wts["w_down"] = _rows(w_down_s)
    late_landed, token = gather("late", ["w_gate", "ple_w"], w_down_s)
    h2 = _matmul(r, wts["w_down"], mode="nn", out_dtypes=[F32], name="mm_down",
                 epilogue=lambda acc, res: (acc + res,), extras=(h1,), after=token, bn=512, bk=4096)
    ng, (w_gate_s, wts["ple_w"]) = _rms_fwd(h2, small["gate_norm_g"], name="norm_gate", comm=late_landed(h2))
    wts["w_gate"] = _rows(w_gate_s)
    gate = _matmul(ng, wts["w_gate"], mode="nn", out_dtypes=[F32], name="mm_gate",
                   epilogue=lambda acc: (1.0 / (1.0 + jnp.exp(-acc)),))
    pp = _matmul(p_bf, wts["ple_w"], mode="nn", out_dtypes=[F32], name="mm_ple", bn=512)
    dh3, dz, dpp, dg_final, dg_ple, loss = _tail(h2, gate, pp, target, small["ple_norm_g"], small["final_norm_g"])

    dng = _matmul(dz, wts["w_gate"], mode="nt", out_dtypes=[F32], name="mm_gate_dx")
    gw_gate = _matmul(ng, dz, mode="tn", out_dtypes=[BF16], name="mm_gate_dw", bm=512, bk=4096)
    gw_ple = _matmul(p_bf, dpp, mode="tn", out_dtypes=[BF16], name="mm_ple_dw", bn=512, out_stack=N_CHIPS)
    dh2, dh2_bf, dg_gate = _rms_bwd(h2, dng, small["gate_norm_g"], dh3, name="norm_gate_bwd", want_bf16=True)

    def exchange(tag, partials, after):
        route = _exchange_route(len(partials))
        lands = [_own_slot((N_DEV,) + g.shape[1:], g.dtype, 2 * chip + core,
                           lax.dynamic_index_in_dim(g, chip, 0, keepdims=False)) for g in partials]
        sems, srcs, lands, token = _split_start(f"exchange_start_{tag}", partials, lands, route, after)

        def landed(done):
            got_srcs, got_lands = _split_wait(f"exchange_wait_{tag}", srcs, lands, sems, route, done)
            comm = _forward_comm(got_srcs, got_lands)
            comm.waited = got_srcs[0]
            return comm

        return landed, token

    big = {}
    gate_landed, token = exchange("gate", [_by_chip(gw_gate), gw_ple], dh2_bf)
    gw_down = _matmul(r, dh2_bf, mode="tn", out_dtypes=[BF16], name="mm_down_dw", after=token, bm=512, bk=4096)
    da, (parts_gate, parts_ple) = _matmul(
        dh2_bf, wts["w_down"], mode="nt", out_dtypes=[BF16], name="mm_down_dx", bm=2048,
        epilogue=lambda acc, ra_v: (acc * (2.0 * ra_v.astype(F32)),), extras=(ra,), comm=gate_landed(gw_down))
    down_landed, token = exchange("down", [_by_chip(gw_down)], da)
    big["w_gate"], big["ple_w"] = update("w_gate", parts_gate), update("ple_w", parts_ple)
    gw_up = _matmul(mn, da, mode="tn", out_dtypes=[BF16], name="mm_up_dw", out_stack=N_CHIPS, after=token,
                    bm=512, bk=4096)
    dmn = _matmul(da, wts["w_up"], mode="nt", out_dtypes=[F32], name="mm_up_dx", bk=4096)
    dh1, dh1_bf, dg_mlp = _rms_bwd(h1, dmn, small["mlp_norm_g"], dh2, name="norm_mlp_bwd", want_bf16=True)
    datt, (parts_down,) = _matmul(dh1_bf, wts["w_out"], mode="nt", out_dtypes=[BF16], name="mm_out_dx",
                                  comm=down_landed(dh1_bf))
    gw_out = _matmul(att, dh1_bf, mode="tn", out_dtypes=[BF16], name="mm_out_dw", bm=512, bk=4096)
    up_landed, token = exchange("up", [gw_up], datt)
    dqb, dkpad, dvpad, dtab, dsink = _attn_b_bwd(pb, kpad, vpad, att, datt, lse_b, bucket,
                                                 small["rel_bias_table"], small["sink_logits"], after=token)
    dqa, dka, dva = _attn_a_bwd(pb, att, datt, lse_a)
    up_forward = up_landed(dqa)
    out_landed, token = exchange("out", [_by_chip(gw_out)], up_forward.waited)
    (dproj, dg_q, dg_k), (parts_up,) = _qk_bwd(dqa, dka, dva, dqb, dkpad, dvpad, proj,
                                               small["q_norm_g"], small["k_norm_g"], cos_t, sin_t,
                                               comm=up_forward, after=token)
    gw_in = _matmul(u, dproj, mode="tn", out_dtypes=[BF16], name="mm_in_dw", bn=768, out_stack=N_CHIPS,
                    bm=512, bk=4096)
    out_forward = out_landed(gw_in)
    in_landed, token = exchange("in", [gw_in], out_forward.waited)
    du, (parts_out,) = _matmul(dproj, wts["w_in"], mode="nt", out_dtypes=[F32], name="mm_in_dx", bk=3072,
                               comm=out_forward, after=token)
    grad_x, dg_attn = _rms_bwd(x, du, small["attn_norm_g"], dh1, name="norm_attn_bwd", want_bf16=False)
    for n, parts in (("w_down", parts_down), ("w_up", parts_up), ("w_out", parts_out)):
        big[n] = update(n, parts)
    done = dg_attn + sum(big[n][0][0, :1, :] for n in ("w_down", "w_up", "w_out"))
    (parts_in,) = _comm_only("forward_w_in", in_landed(done))
    big["w_in"] = update("w_in", parts_in)

    small_g = {
        "attn_norm_g": dg_attn, "mlp_norm_g": dg_mlp, "ple_norm_g": dg_ple, "gate_norm_g": dg_gate,
        "final_norm_g": dg_final, "q_norm_g": dg_q, "k_norm_g": dg_k,
        "sink_logits": dsink[:, 0, 0][None, :], "rel_bias_table": dtab[:, :, 0].T,
    }
    return loss, grad_x, big, small_g


_SMALL_ROWS = ["attn_norm_g", "mlp_norm_g", "ple_norm_g", "gate_norm_g", "final_norm_g"]
_PACK_ROWS = 8


def _pack_small(vals, d):
    rows = [vals[n].reshape(1, d) for n in _SMALL_ROWS]
    misc = jnp.concatenate([
        vals["q_norm_g"].reshape(1, HEAD_DIM), vals["k_norm_g"].reshape(1, HEAD_DIM),
        jnp.pad(vals["sink_logits"].reshape(1, N_HEADS_B), ((0, 0), (0, HEAD_DIM - N_HEADS_B))),
        vals["rel_bias_table"].reshape(1, N_BUCKETS * N_HEADS_B)], axis=1)
    rows.append(jnp.pad(misc, ((0, 0), (0, d - misc.shape[1]))))
    rows.append(jnp.zeros((_PACK_ROWS - len(rows), d), F32))
    return jnp.concatenate(rows, axis=0).astype(F32)


def _unpack_small(pack, shapes):
    out = {n: pack[i].reshape(shapes[n]) for i, n in enumerate(_SMALL_ROWS)}
    misc = pack[len(_SMALL_ROWS)]
    out["q_norm_g"] = misc[:HEAD_DIM].reshape(shapes["q_norm_g"])
    out["k_norm_g"] = misc[HEAD_DIM:2 * HEAD_DIM].reshape(shapes["k_norm_g"])
    out["sink_logits"] = misc[2 * HEAD_DIM:2 * HEAD_DIM + N_HEADS_B].reshape(shapes["sink_logits"])
    out["rel_bias_table"] = misc[3 * HEAD_DIM:3 * HEAD_DIM + N_BUCKETS * N_HEADS_B].reshape(shapes["rel_bias_table"])
    return out


_WEIGHTS = ["attn_norm_g", "w_in", "q_norm_g", "k_norm_g", "sink_logits", "w_out", "mlp_norm_g", "w_up", "w_down",
            "ple_w", "ple_norm_g", "gate_norm_g", "w_gate", "rel_bias_table", "final_norm_g"]
_BIG = ["w_in", "w_out", "w_up", "w_down", "ple_w", "w_gate"]


def kernel(x, p, attn_norm_g, w_in, q_norm_g, k_norm_g, sink_logits, w_out, mlp_norm_g, w_up, w_down, ple_w, ple_norm_g, gate_norm_g, w_gate, rel_bias_table, final_norm_g, loss_target, m_attn_norm_g, m_w_in, m_q_norm_g, m_k_norm_g, m_sink_logits, m_w_out, m_mlp_norm_g, m_w_up, m_w_down, m_ple_w, m_ple_norm_g, m_gate_norm_g, m_w_gate, m_rel_bias_table, m_final_norm_g, v_attn_norm_g, v_w_in, v_q_norm_g, v_k_norm_g, v_sink_logits, v_w_out, v_mlp_norm_g, v_w_up, v_w_down, v_ple_w, v_ple_norm_g, v_gate_norm_g, v_w_gate, v_rel_bias_table, v_final_norm_g):
    given = dict(locals())
    w = {n: given[n] for n in _WEIGHTS}
    m = {n: given["m_" + n] for n in _WEIGHTS}
    v = {n: given["v_" + n] for n in _WEIGHTS}
    d = x.shape[-1]

    shards = {n: w[n][0] for n in _BIG}
    small = {
        "attn_norm_g": w["attn_norm_g"], "mlp_norm_g": w["mlp_norm_g"], "ple_norm_g": w["ple_norm_g"],
        "gate_norm_g": w["gate_norm_g"], "final_norm_g": w["final_norm_g"].reshape(1, d),
        "q_norm_g": w["q_norm_g"], "k_norm_g": w["k_norm_g"], "sink_logits": w["sink_logits"],
        "rel_bias_table": w["rel_bias_table"],
    }

    def update(n, parts):
        res = _sum_adamw(parts, w[n][0], m[n][0], v[n][0], name="adamw_" + n)
        return [t.reshape(w[n].shape) for t in res]

    loss_part, grad_x, big, small_g = _local_step(x[0], p[0, 0], loss_target[0], shards, small, update)
    grads, deltas, new_m, new_v = [{n: big[n][i] for n in _BIG} for i in range(4)]

    shapes = {n: w[n].shape for n in _WEIGHTS if n not in _BIG}
    pack = _pack_small(small_g, d)
    pack = pack.at[_PACK_ROWS - 1, :1].add(0.0 * grads["w_in"][0, 0, :1])
    pack = pack.at[_PACK_ROWS - 1, 1].set(loss_part[0, 0])
    g_small = _allreduce_small(pack)
    loss = g_small[_PACK_ROWS - 1, 1]
    d_small, m_small, v_small = _adamw_small(g_small, _pack_small(w, d), _pack_small(m, d), _pack_small(v, d))
    grads.update(_unpack_small(g_small, shapes))
    deltas.update(_unpack_small(d_small, shapes))
    new_m.update(_unpack_small(m_small, shapes))
    new_v.update(_unpack_small(v_small, shapes))

    return (loss, grad_x[None], *[grads[n] for n in _WEIGHTS], *[deltas[n] for n in _WEIGHTS],
            *[new_m[n] for n in _WEIGHTS], *[new_v[n] for n in _WEIGHTS])
```

```python
import functools
import math

import jax
import jax.numpy as jnp
from jax import lax
from jax.experimental import pallas as pl
from jax.experimental.pallas import tpu as pltpu

F32 = jnp.float32
BF16 = jnp.bfloat16

HEAD_DIM = 128
N_HEADS_A = 8
N_KV_A = 2
N_HEADS_B = 8
N_KV_B = 2
GROUP = 4
GRID_W = 64
BLOCK_Q = 128
WINDOW = 128
N_BUCKETS = 32
MAX_DISTANCE = 128
ROPE_THETA = 10000.0
EPS = 1e-6
NEG_INF = -1e30
ATT_SCALE = HEAD_DIM ** -0.5
LOG2E = math.log2(math.e)
LN2 = math.log(2.0)
Q_SCALE = ATT_SCALE * LOG2E
PAD_LO, PAD_HI = 256, 128
ADAM_LR = 0.001
ADAM_B1 = 0.9
ADAM_B2 = 0.999
ADAM_EPS = 1e-08
ADAM_WD = 0.01
ADAM_STEP = 10

N_CHIPS = 4
N_DEV = 8
COL_QA, COL_KA, COL_VA, COL_QB, COL_KB, COL_VB = 0, 8, 10, 12, 20, 22

VMEM_LIMIT = 52 * 1024 * 1024


def _params(sem=None, collective_id=None):
    return pltpu.CompilerParams(dimension_semantics=sem, vmem_limit_bytes=VMEM_LIMIT, collective_id=collective_id)


_ANY = pl.BlockSpec(memory_space=pl.ANY)
_MESH = pl.DeviceIdType.MESH
SIBLING_BARRIER_ID = 1


def _sibling():
    return (lax.axis_index("x"), lax.axis_index("y"), 1 - lax.axis_index("c"))


class _Comm:
    def __init__(self, inputs, out_shapes, sems, start, finish, aliases=None):
        self.inputs, self.out_shapes, self.sems = list(inputs), list(out_shapes), list(sems)
        self.start, self.finish, self.aliases = start, finish, dict(aliases or {})


def _call(body, *, name, grid, in_specs, out_specs, out_shape, args, scratch_shapes=(), sem=None, comm=None,
          after=None, aliases=None):
    in_specs, out_specs, out_shape = list(in_specs), list(out_specs), list(out_shape)
    scratch_shapes = list(scratch_shapes)
    n_in, n_out, n_sc = len(in_specs), len(out_specs), len(scratch_shapes)
    behind = [] if after is None else [after]
    aliases = dict(aliases or {})
    if comm is None:
        res = pl.pallas_call(
            (lambda *refs: body(*refs[:n_in], *refs[n_in + len(behind):])) if behind else body,
            name=name, grid=grid, in_specs=in_specs + [_ANY] * len(behind), out_specs=out_specs,
            out_shape=out_shape, scratch_shapes=scratch_shapes, input_output_aliases=aliases,
            compiler_params=_params(sem))(*args, *behind)
        return list(res), []
    c_in, c_out = len(comm.inputs), len(comm.out_shapes)

    def hosted(*refs):
        pos = [0]

        def take(n):
            pos[0] += n
            return refs[pos[0] - n:pos[0]]

        ins, c_ins, _, outs, c_outs, scr = (take(n_in), take(c_in), take(len(behind)), take(n_out), take(c_out),
                                            take(n_sc))
        c_sems = refs[pos[0]:]
        ids = [pl.program_id(a) for a in range(len(grid))]
        first = functools.reduce(jnp.logical_and, [i == 0 for i in ids])
        last = functools.reduce(jnp.logical_and, [i == g - 1 for i, g in zip(ids, grid)])

        @pl.when(first)
        def _():
            barrier = pltpu.get_barrier_semaphore()
            pl.semaphore_signal(barrier, inc=1, device_id=_sibling(), device_id_type=_MESH)
            pl.semaphore_wait(barrier, 1)
            comm.start(c_ins, c_outs, c_sems)

        body(*ins, *outs, *scr)

        @pl.when(last)
        def _():
            comm.finish(c_ins, c_outs, c_sems)

    res = pl.pallas_call(
        hosted, name=name, grid=grid, in_specs=in_specs + [_ANY] * (c_in + len(behind)),
        out_specs=out_specs + [_ANY] * c_out,
        out_shape=out_shape + comm.out_shapes, scratch_shapes=scratch_shapes + comm.sems,
        input_output_aliases={**aliases, **{n_in + i: n_out + o for i, o in comm.aliases.items()}},
        compiler_params=_params(("arbitrary",) * len(grid), SIBLING_BARRIER_ID))(*args, *comm.inputs, *behind)
    return list(res[:n_out]), list(res[n_out:])


def _matmul(a, b, *, mode, out_dtypes, name, epilogue=None, extras=(), bm=1024, bn=1024, bk=2048,
            out_stack=0, comm=None, after=None, vecs=()):
    stacked = b.ndim == 3
    if mode == "nn":
        m, k = a.shape
        if stacked:
            nj, kb, ns = b.shape
            n, ks = nj * ns, k
        else:
            kb, n = b.shape
            ns, ks = n, k
        dn = (((1,), (0,)), ((), ()))
    elif mode == "nt":
        m, k = a.shape
        if stacked:
            nj, n, ks = b.shape
            kb = nj * ks
        else:
            n, kb = b.shape
            ks = kb
        ns = n
        dn = (((1,), (1,)), ((), ()))
    else:
        k, m = a.shape
        kb, n = b.shape
        ns, ks = n, k
        dn = (((0,), (0,)), ((), ()))
    assert k == kb and not (stacked and mode == "tn")
    ns_out = n // out_stack if out_stack else n
    per_blk = min(bk, k) // ks if stacked and mode == "nt" and bk > ks else 0
    bm, bn, bk = min(bm, m), min(bn, ns, ns_out), per_blk * ks if per_blk else min(bk, ks)
    assert m % bm == 0 and ns % bn == 0 and ns_out % bn == 0 and (k % bk == 0 if per_blk else ks % bk == 0)
    gm, gn, gk = m // bm, n // bn, k // bk

    if mode == "tn":
        a_spec = pl.BlockSpec((bk, bm), lambda i, j, q: (q, i))
    else:
        a_spec = pl.BlockSpec((bm, bk), lambda i, j, q: (i, q))
    if mode == "nt":
        if per_blk:
            b_spec = pl.BlockSpec((per_blk, bn, ks), lambda i, j, q: (q, j, 0))
        elif stacked:
            per = ks // bk
            b_spec = pl.BlockSpec((None, bn, bk), lambda i, j, q: (q // per, j, q % per))
        else:
            b_spec = pl.BlockSpec((bn, bk), lambda i, j, q: (j, q))
    else:
        if stacked:
            per = ns // bn
            b_spec = pl.BlockSpec((None, bk, bn), lambda i, j, q: (j // per, q, j % per))
        else:
            b_spec = pl.BlockSpec((bk, bn), lambda i, j, q: (q, j))
    ex_spec = pl.BlockSpec((bm, bn), lambda i, j, q: (i, j))
    if out_stack:
        per_o = ns_out // bn
        o_spec = pl.BlockSpec((None, bm, bn), lambda i, j, q: (j // per_o, i, j % per_o))
        o_shape = (out_stack, m, ns_out)
    else:
        o_spec = ex_spec
        o_shape = (m, n)
    n_ex, n_out = len(extras) + len(vecs), len(out_dtypes)

    def body(a_ref, b_ref, *rest):
        ex, outs = rest[:n_ex], rest[n_ex:n_ex + n_out]
        if per_blk:
            part = sum(lax.dot_general(a_ref[:, t * ks:(t + 1) * ks], b_ref[t], dn, preferred_element_type=F32)
                       for t in range(per_blk))
        else:
            part = lax.dot_general(a_ref[...], b_ref[...], dn, preferred_element_type=F32)

        def finish(acc):
            res = epilogue(acc, *[e[...] for e in ex]) if epilogue else (acc,)
            for o, r in zip(outs, res):
                o[...] = r.astype(o.dtype)

        if gk == 1:
            finish(part)
        else:
            acc_ref = rest[-1]
            q = pl.program_id(2)

            @pl.when(q == 0)
            def _():
                acc_ref[...] = part

            @pl.when(q > 0)
            def _():
                acc_ref[...] += part

            @pl.when(q == gk - 1)
            def _():
                finish(acc_ref[...])

    res, c_res = _call(
        body, name=name, grid=(gm, gn, gk),
        in_specs=[a_spec, b_spec] + [ex_spec] * len(extras)
        + [pl.BlockSpec((1, bn), lambda i, j, q: (0, j))] * len(vecs),
        out_specs=[o_spec] * n_out,
        out_shape=[jax.ShapeDtypeStruct(o_shape, dt) for dt in out_dtypes],
        scratch_shapes=[pltpu.VMEM((bm, bn), F32)] if gk > 1 else [],
        sem=("parallel", "parallel", "arbitrary"), args=(a, b, *extras, *vecs), comm=comm, after=after)
    res = res[0] if n_out == 1 else res
    return res if comm is None else (res, c_res)


def _rms_fwd(x, g, *, name, tm=256, comm=None):
    s, d = x.shape
    tm = min(tm, s)

    def body(x_ref, g_ref, o_ref):
        xf = x_ref[...]
        r = lax.rsqrt(jnp.mean(xf * xf, axis=-1, keepdims=True) + EPS)
        o_ref[...] = (xf * r * g_ref[...]).astype(o_ref.dtype)

    res, c_res = _call(
        body, name=name, grid=(s // tm,),
        in_specs=[pl.BlockSpec((tm, d), lambda i: (i, 0)), pl.BlockSpec((1, d), lambda i: (0, 0))],
        out_specs=[pl.BlockSpec((tm, d), lambda i: (i, 0))],
        out_shape=[jax.ShapeDtypeStruct((s, d), BF16)],
        sem=("parallel",), args=(x, g), comm=comm)
    return res[0] if comm is None else (res[0], c_res)


def _rms_bwd(x, dy, g, add, *, name, want_bf16, tm=256):
    s, d = x.shape
    tm = min(tm, s)

    def body(x_ref, dy_ref, g_ref, add_ref, dx_ref, *rest):
        dg_ref = rest[-1]
        i = pl.program_id(0)
        xf = x_ref[...]
        dyf = dy_ref[...].astype(F32)
        r = lax.rsqrt(jnp.mean(xf * xf, axis=-1, keepdims=True) + EPS)
        xh = xf * r
        dyg = dyf * g_ref[...]
        dx = r * (dyg - xh * jnp.mean(dyg * xh, axis=-1, keepdims=True))
        tot = add_ref[...] + dx
        dx_ref[...] = tot
        if want_bf16:
            rest[0][...] = tot.astype(BF16)
        part = jnp.sum(dyf * xh, axis=0, keepdims=True)

        @pl.when(i == 0)
        def _():
            dg_ref[...] = part

        @pl.when(i > 0)
        def _():
            dg_ref[...] += part

    row = pl.BlockSpec((tm, d), lambda i: (i, 0))
    vec = pl.BlockSpec((1, d), lambda i: (0, 0))
    out_specs = [row] + ([row] if want_bf16 else []) + [vec]
    out_shape = [jax.ShapeDtypeStruct((s, d), F32)]
    if want_bf16:
        out_shape.append(jax.ShapeDtypeStruct((s, d), BF16))
    out_shape.append(jax.ShapeDtypeStruct((1, d), F32))
    return pl.pallas_call(
        body,
        name=name,
        grid=(s // tm,),
        in_specs=[row, row, vec, row],
        out_specs=out_specs,
        out_shape=out_shape,
        compiler_params=_params(("arbitrary",)),
    )(x, dy, g, add)


def _tail(h2, gate, pp, target, g_ple, g_final, *, tm=256):
    s, d = h2.shape
    tm = min(tm, s)

    def body(h2_ref, gate_ref, pp_ref, t_ref, gp_ref, gf_ref, dh3_ref, dz_ref, dpp_ref, dgf_ref, dgp_ref, loss_ref):
        i = pl.program_id(0)
        ppf = pp_ref[...]
        gate_v = gate_ref[...]
        r_p = lax.rsqrt(jnp.mean(ppf * ppf, axis=-1, keepdims=True) + EPS)
        eh = ppf * r_p
        e = eh * gp_ref[...]
        h3 = h2_ref[...] + gate_v * e
        r_f = lax.rsqrt(jnp.mean(h3 * h3, axis=-1, keepdims=True) + EPS)
        yh = h3 * r_f
        diff = yh * gf_ref[...] - t_ref[...]
        loss_part = 0.5 * jnp.sum(jnp.mean(diff * diff, axis=-1, keepdims=True), axis=0, keepdims=True)
        dy = diff / d
        dgf = jnp.sum(dy * yh, axis=0, keepdims=True)
        dyg = dy * gf_ref[...]
        dh3 = r_f * (dyg - yh * jnp.mean(dyg * yh, axis=-1, keepdims=True))
        dh3_ref[...] = dh3
        de = dh3 * gate_v
        dz_ref[...] = (dh3 * e * gate_v * (1.0 - gate_v)).astype(BF16)
        dgp = jnp.sum(de * eh, axis=0, keepdims=True)
        deg = de * gp_ref[...]
        dpp_ref[...] = (r_p * (deg - eh * jnp.mean(deg * eh, axis=-1, keepdims=True))).astype(BF16)
        loss_row = jnp.broadcast_to(loss_part, (1, 128))

        @pl.when(i == 0)
        def _():
            dgf_ref[...] = dgf
            dgp_ref[...] = dgp
            loss_ref[...] = loss_row

        @pl.when(i > 0)
        def _():
            dgf_ref[...] += dgf
            dgp_ref[...] += dgp
            loss_ref[...] += loss_row

    row = pl.BlockSpec((tm, d), lambda i: (i, 0))
    vec = pl.BlockSpec((1, d), lambda i: (0, 0))
    return pl.pallas_call(
        body,
        name="tail_fwd_bwd",
        grid=(s // tm,),
        in_specs=[row, row, row, row, vec, vec],
        out_specs=[row, row, row, vec, vec, pl.BlockSpec((1, 128), lambda i: (0, 0))],
        out_shape=[
            jax.ShapeDtypeStruct((s, d), F32),
            jax.ShapeDtypeStruct((s, d), BF16),
            jax.ShapeDtypeStruct((s, d), BF16),
            jax.ShapeDtypeStruct((1, d), F32),
            jax.ShapeDtypeStruct((1, d), F32),
            jax.ShapeDtypeStruct((1, 128), F32),
        ],
        compiler_params=_params(("arbitrary",)),
    )(h2, gate, pp, target, g_ple, g_final)


def _rope_tables(s):
    rows = s // GRID_W
    half = HEAD_DIM // 2
    inv_freq = ROPE_THETA ** (-jnp.arange(0, half, 2, dtype=F32) / half)
    ang_r = jnp.arange(rows, dtype=jnp.int32).astype(F32)[:, None] * inv_freq
    ang_c = jnp.arange(GRID_W, dtype=jnp.int32).astype(F32)[:, None] * inv_freq
    cr, sr = (jnp.repeat(t, GRID_W, axis=0) for t in (jnp.cos(ang_r), jnp.sin(ang_r)))
    cc, sc = (jnp.tile(t, (rows, 1)) for t in (jnp.cos(ang_c), jnp.sin(ang_c)))
    cos_t = jnp.concatenate([cr, cr, cc, cc], axis=-1)
    sin_t = jnp.concatenate([-sr, sr, -sc, sc], axis=-1)
    return cos_t, sin_t


def _swap_quarters(x):
    lane = lax.broadcasted_iota(jnp.int32, x.shape, x.ndim - 1)
    up = pltpu.roll(x, HEAD_DIM - 32, x.ndim - 1)
    down = pltpu.roll(x, 32, x.ndim - 1)
    return jnp.where((lane % 64) < 32, up, down)


def _cols(first, count=1):
    return slice(first * HEAD_DIM, (first + count) * HEAD_DIM)


def _qk_prep(proj, g_q, g_k, cos_t, sin_t, *, tm=256, comm=None):
    s, n = proj.shape
    tm = min(tm, s)

    def body(x_ref, gq_ref, gk_ref, c_ref, s_ref, o_ref):
        cos_v, sin_v = c_ref[...], s_ref[...]
        for h in range(COL_VA):
            x = x_ref[:, _cols(h)]
            g = gq_ref[...] if h < COL_KA else gk_ref[...]
            xn = x * lax.rsqrt(jnp.mean(x * x, axis=-1, keepdims=True) + EPS) * g
            xr = xn * cos_v + _swap_quarters(xn) * sin_v
            if h < COL_KA:
                xr = xr * Q_SCALE
            o_ref[:, _cols(h)] = xr.astype(BF16)
        o_ref[:, _cols(COL_VA, 2)] = x_ref[:, _cols(COL_VA, 2)].astype(BF16)
        o_ref[:, _cols(COL_QB, N_HEADS_B)] = (x_ref[:, _cols(COL_QB, N_HEADS_B)] * Q_SCALE).astype(BF16)
        o_ref[:, _cols(COL_KB, 4)] = x_ref[:, _cols(COL_KB, 4)].astype(BF16)

    row = pl.BlockSpec((tm, n), lambda i: (i, 0))
    tab = pl.BlockSpec((tm, HEAD_DIM), lambda i: (i, 0))
    vec = pl.BlockSpec((1, HEAD_DIM), lambda i: (0, 0))
    res, c_res = _call(
        body, name="qk_prep", grid=(s // tm,),
        in_specs=[row, vec, vec, tab, tab],
        out_specs=[row],
        out_shape=[jax.ShapeDtypeStruct((s, n), BF16)],
        sem=("parallel",), args=(proj, g_q, g_k, cos_t, sin_t), comm=comm)
    return res[0] if comm is None else (res[0], c_res)


def _qk_bwd(dqa, dka, dva, dqb, dkpad, dvpad, proj, g_q, g_k, cos_t, sin_t, *, comm=None, after=None):
    s, n = proj.shape
    tm = min(PAD_LO, s)
    assert PAD_LO % tm == 0
    lo = PAD_LO // tm

    def body(dqa_ref, dka_ref, dva_ref, dqb_ref, dkb_ref, dvb_ref, x_ref, gq_ref, gk_ref, c_ref, s_ref,
             o_ref, dgq_ref, dgk_ref):
        i = pl.program_id(0)
        cos_v, sin_v = c_ref[...], s_ref[...]

        def head(d, x, g):
            dn = d * cos_v + _swap_quarters(d * sin_v)
            r = lax.rsqrt(jnp.mean(x * x, axis=-1, keepdims=True) + EPS)
            xh = x * r
            dng = dn * g
            dx = r * (dng - xh * jnp.mean(dng * xh, axis=-1, keepdims=True))
            return dx.astype(BF16), jnp.sum(dn * xh, axis=0, keepdims=True)

        acc_q = jnp.zeros((1, HEAD_DIM), F32)
        acc_k = jnp.zeros((1, HEAD_DIM), F32)
        for h in range(N_HEADS_A):
            o_ref[:, _cols(h)], part = head(dqa_ref[:, _cols(h)] * ATT_SCALE, x_ref[:, _cols(h)], gq_ref[...])
            acc_q = acc_q + part
        for h in range(N_KV_A):
            o_ref[:, _cols(COL_KA + h)], part = head(dka_ref[:, _cols(h)] * LN2, x_ref[:, _cols(COL_KA + h)],
                                                     gk_ref[...])
            acc_k = acc_k + part
        o_ref[:, _cols(COL_VA, 2)] = dva_ref[...].astype(BF16)
        o_ref[:, _cols(COL_QB, N_HEADS_B)] = (dqb_ref[...] * ATT_SCALE).astype(BF16)
        o_ref[:, _cols(COL_KB, 2)] = (dkb_ref[...] * LN2).astype(BF16)
        o_ref[:, _cols(COL_VB, 2)] = dvb_ref[...].astype(BF16)

        @pl.when(i == 0)
        def _():
            dgq_ref[...] = acc_q
            dgk_ref[...] = acc_k

        @pl.when(i > 0)
        def _():
            dgq_ref[...] += acc_q
            dgk_ref[...] += acc_k

    def rows(width, shift=0):
        return pl.BlockSpec((tm, width), lambda i: (i + shift, 0))

    kv_w = N_KV_A * HEAD_DIM
    q_w = N_HEADS_A * HEAD_DIM
    vec = pl.BlockSpec((1, HEAD_DIM), lambda i: (0, 0))
    res, c_res = _call(
        body, name="qk_bwd", grid=(s // tm,),
        in_specs=[rows(q_w), rows(kv_w), rows(kv_w), rows(q_w), rows(kv_w, lo), rows(kv_w, lo), rows(n),
                  vec, vec, rows(HEAD_DIM), rows(HEAD_DIM)],
        out_specs=[rows(n), vec, vec],
        out_shape=[
            jax.ShapeDtypeStruct((s, n), BF16),
            jax.ShapeDtypeStruct((1, HEAD_DIM), F32),
            jax.ShapeDtypeStruct((1, HEAD_DIM), F32),
        ],
        sem=("arbitrary",), args=(dqa, dka, dva, dqb, dkpad, dvpad, proj, g_q, g_k, cos_t, sin_t), comm=comm,
        after=after)
    return res if comm is None else (res, c_res)


_NT = (((1,), (1,)), ((), ()))
_TN = (((0,), (0,)), ((), ()))


def _attn_a_fwd(pb, *, tq=4096, sub=256, comm=None, after=None):
    s = pb.shape[0]
    tq = min(tq, s)
    sub = min(sub, tq)

    def body(q_ref, k_ref, v_ref, o_ref, lse_ref):
        k = k_ref[...]
        v = v_ref[...]
        for r in range(tq // sub):
            rows = pl.ds(r * sub, sub)
            sc = lax.dot_general(q_ref[rows, :], k, _NT, preferred_element_type=F32)
            m = jnp.max(sc, axis=-1, keepdims=True)
            p = jnp.exp2(sc - m)
            l = jnp.sum(p, axis=-1, keepdims=True)
            o = jnp.dot(p.astype(BF16), v, preferred_element_type=F32)
            o_ref[rows, :] = (o / l).astype(BF16)
            lse_ref[rows, :] = jnp.broadcast_to(m + jnp.log2(l), (sub, HEAD_DIM))

    res, c_res = _call(
        body, name="attn_a_fwd", grid=(N_HEADS_A, s // tq),
        in_specs=[
            pl.BlockSpec((tq, HEAD_DIM), lambda h, i: (i, COL_QA + h)),
            pl.BlockSpec((s, HEAD_DIM), lambda h, i: (0, COL_KA + h // GROUP)),
            pl.BlockSpec((s, HEAD_DIM), lambda h, i: (0, COL_VA + h // GROUP)),
        ],
        out_specs=[
            pl.BlockSpec((tq, HEAD_DIM), lambda h, i: (i, h)),
            pl.BlockSpec((None, tq, HEAD_DIM), lambda h, i: (h, i, 0)),
        ],
        out_shape=[
            jax.ShapeDtypeStruct((s, (N_HEADS_A + N_HEADS_B) * HEAD_DIM), BF16),
            jax.ShapeDtypeStruct((N_HEADS_A, s, HEAD_DIM), F32),
        ],
        sem=("parallel", "parallel"), args=(pb, pb, pb), comm=comm, after=after)
    return res if comm is None else (res, c_res)


def _attn_a_bwd(pb, att, datt, lse, *, tq=1024, sub=256, comm=None):
    s = pb.shape[0]
    tq = min(tq, s)
    sub = min(sub, tq)

    def body(q_ref, k_ref, v_ref, o_ref, do_ref, lse_ref, dq_ref, dk_ref, dv_ref):
        first = jnp.logical_and(pl.program_id(1) == 0, pl.program_id(2) == 0)
        k = k_ref[...]
        v = v_ref[...]
        dk = dv = None
        for r in range(tq // sub):
            rows = pl.ds(r * sub, sub)
            q = q_ref[rows, :]
            do = do_ref[rows, :]
            sc = lax.dot_general(q, k, _NT, preferred_element_type=F32)
            p = jnp.exp2(sc - lse_ref[rows, :][:, :1])
            dp = lax.dot_general(do, v, _NT, preferred_element_type=F32)
            delta = jnp.sum(do.astype(F32) * o_ref[rows, :].astype(F32), axis=-1, keepdims=True)
            ds = (p * (dp - delta)).astype(BF16)
            dq_ref[rows, :] = jnp.dot(ds, k, preferred_element_type=F32)
            dk_r = lax.dot_general(ds, q, _TN, preferred_element_type=F32)
            dv_r = lax.dot_general(p.astype(BF16), do, _TN, preferred_element_type=F32)
            dk = dk_r if dk is None else dk + dk_r
            dv = dv_r if dv is None else dv + dv_r

        @pl.when(first)
        def _():
            dk_ref[...] = dk
            dv_ref[...] = dv

        @pl.when(jnp.logical_not(first))
        def _():
            dk_ref[...] += dk
            dv_ref[...] += dv

    qmap = lambda kv, g, i: (i, kv * GROUP + g)
    res, c_res = _call(
        body, name="attn_a_bwd", grid=(N_KV_A, GROUP, s // tq),
        in_specs=[
            pl.BlockSpec((tq, HEAD_DIM), lambda kv, g, i: (i, COL_QA + kv * GROUP + g)),
            pl.BlockSpec((s, HEAD_DIM), lambda kv, g, i: (0, COL_KA + kv)),
            pl.BlockSpec((s, HEAD_DIM), lambda kv, g, i: (0, COL_VA + kv)),
            pl.BlockSpec((tq, HEAD_DIM), qmap),
            pl.BlockSpec((tq, HEAD_DIM), qmap),
            pl.BlockSpec((None, tq, HEAD_DIM), lambda kv, g, i: (kv * GROUP + g, i, 0)),
        ],
        out_specs=[
            pl.BlockSpec((tq, HEAD_DIM), qmap),
            pl.BlockSpec((s, HEAD_DIM), lambda kv, g, i: (0, kv)),
            pl.BlockSpec((s, HEAD_DIM), lambda kv, g, i: (0, kv)),
        ],
        out_shape=[
            jax.ShapeDtypeStruct((s, N_HEADS_A * HEAD_DIM), F32),
            jax.ShapeDtypeStruct((s, N_KV_A * HEAD_DIM), F32),
            jax.ShapeDtypeStruct((s, N_KV_A * HEAD_DIM), F32),
        ],
        sem=("arbitrary", "arbitrary", "arbitrary"), args=(pb, pb, pb, att, datt, lse), comm=comm)
    return res if comm is None else (res, c_res)


def _t5_bucket(rel):
    nb = N_BUCKETS // 2
    ret = jnp.where(rel > 0, nb, 0)
    n = jnp.abs(rel)
    max_exact = nb // 2
    nf = jnp.maximum(n, 1).astype(F32)
    large = max_exact + (jnp.log(nf / max_exact) / math.log(MAX_DISTANCE / max_exact)
                         * (nb - max_exact)).astype(jnp.int32)
    large = jnp.minimum(large, nb - 1)
    return ret + jnp.where(n < max_exact, n, large)


def _band_buckets():
    r = jnp.arange(BLOCK_Q, dtype=jnp.int32)
    j = jnp.arange(3 * BLOCK_Q, dtype=jnp.int32)
    return _t5_bucket((j[None, :] - BLOCK_Q) - r[:, None])


def _band_bias(bucket, table_ref, h):
    acc = jnp.zeros(bucket.shape, F32)
    for b in range(N_BUCKETS):
        acc = jnp.where(bucket == b, table_ref[b, h], acc)
    return acc


GQ = GROUP * BLOCK_Q


def _stack_heads(x):
    return jnp.concatenate([x[:, _cols(g)] for g in range(GROUP)], axis=0)


def _unstack_heads(x):
    return jnp.concatenate([x[g * BLOCK_Q:(g + 1) * BLOCK_Q] for g in range(GROUP)], axis=1)


def _group_bias(bucket, table_ref, kv):
    return jnp.concatenate([_band_bias(bucket, table_ref, kv * GROUP + g) * LOG2E for g in range(GROUP)], axis=0)


def _group_sink(sink_ref, kv):
    head = lax.broadcasted_iota(jnp.int32, (GQ, 1), 0) // BLOCK_Q
    snk = jnp.zeros((GQ, 1), F32)
    for g in range(GROUP):
        snk = jnp.where(head == g, sink_ref[0, kv * GROUP + g] * LOG2E, snk)
    return snk


def _band_mask(n, s):
    r = lax.broadcasted_iota(jnp.int32, (GQ, 3 * BLOCK_Q), 0) % BLOCK_Q
    j = lax.broadcasted_iota(jnp.int32, (GQ, 3 * BLOCK_Q), 1)
    rel = j - BLOCK_Q - r
    kabs = n * BLOCK_Q + j - BLOCK_Q
    return (jnp.abs(rel) <= WINDOW) & (kabs >= 0) & (kabs < s)


def _band_start(n):
    return pl.multiple_of(n * BLOCK_Q + (PAD_LO - BLOCK_Q), BLOCK_Q)


def _attn_b_fwd(pb, kpad, vpad, bucket, table, sink, att, *, comm=None, after=None):
    s = pb.shape[0]
    nblk = s // BLOCK_Q
    sp = kpad.shape[0]

    def body(table_ref, sink_ref, q0_ref, q1_ref, k_ref, v_ref, bucket_ref, _, o_ref, lse_ref, bias_ref):
        n = pl.program_id(0)

        @pl.when(n == 0)
        def _():
            for kv in range(N_KV_B):
                bias_ref[kv * GQ:(kv + 1) * GQ, :] = _group_bias(bucket_ref[...], table_ref, kv)

        band = pl.ds(_band_start(n), 3 * BLOCK_Q)
        mask = _band_mask(n, s)
        for kv, q_ref in enumerate((q0_ref, q1_ref)):
            kb = k_ref[band, _cols(kv)]
            vb = v_ref[band, _cols(kv)]
            sc = lax.dot_general(_stack_heads(q_ref[...]), kb, _NT, preferred_element_type=F32)
            sc = jnp.where(mask, sc + bias_ref[kv * GQ:(kv + 1) * GQ, :], NEG_INF)
            snk = _group_sink(sink_ref, kv)
            m = jnp.maximum(jnp.max(sc, axis=-1, keepdims=True), snk)
            p = jnp.exp2(sc - m)
            l = jnp.sum(p, axis=-1, keepdims=True) + jnp.exp2(snk - m)
            o = jnp.dot(p.astype(BF16), vb, preferred_element_type=F32)
            o_ref[:, _cols(kv * GROUP, GROUP)] = _unstack_heads((o / l).astype(BF16))
            lse = m + jnp.log2(l)
            for g in range(GROUP):
                lse_ref[kv * GROUP + g] = jnp.broadcast_to(lse[g * BLOCK_Q:(g + 1) * BLOCK_Q], (BLOCK_Q, HEAD_DIM))

    smem = pl.BlockSpec(memory_space=pltpu.SMEM)
    wide = GROUP * HEAD_DIM
    whole = pl.BlockSpec((sp, N_KV_B * HEAD_DIM), lambda n: (0, 0))
    res, c_res = _call(
        body, name="attn_b_fwd", grid=(nblk,),
        in_specs=[
            smem,
            smem,
            pl.BlockSpec((BLOCK_Q, wide), lambda n: (n, COL_QB // GROUP)),
            pl.BlockSpec((BLOCK_Q, wide), lambda n: (n, COL_QB // GROUP + 1)),
            whole,
            whole,
            pl.BlockSpec((BLOCK_Q, 3 * BLOCK_Q), lambda n: (0, 0)),
            _ANY,
        ],
        out_specs=[
            pl.BlockSpec((BLOCK_Q, N_HEADS_B * HEAD_DIM), lambda n: (n, 1)),
            pl.BlockSpec((N_HEADS_B, BLOCK_Q, HEAD_DIM), lambda n: (0, n, 0)),
        ],
        out_shape=[
            jax.ShapeDtypeStruct(att.shape, BF16),
            jax.ShapeDtypeStruct((N_HEADS_B, s, HEAD_DIM), F32),
        ],
        scratch_shapes=[pltpu.VMEM((N_KV_B * GQ, 3 * BLOCK_Q), F32)],
        sem=("arbitrary",), args=(table, sink, pb, pb, kpad, vpad, bucket, att), comm=comm, after=after,
        aliases={7: 0})
    return res if comm is None else (res, c_res)


def _attn_b_bwd(pb, kpad, vpad, att, datt, lse, bucket, table, sink, *, comm=None, after=None):
    s = pb.shape[0]
    nblk = s // BLOCK_Q
    sp = kpad.shape[0]

    def body(table_ref, sink_ref, q0_ref, q1_ref, k_ref, v_ref, o_ref, do_ref, lse_ref, bucket_ref,
             dq_ref, dk_ref, dv_ref, dtab_ref, dsink_ref, bias_ref, dbias_ref):
        n = pl.program_id(0)

        @pl.when(n == 0)
        def _():
            dk_ref[...] = jnp.zeros_like(dk_ref)
            dv_ref[...] = jnp.zeros_like(dv_ref)
            dbias_ref[...] = jnp.zeros_like(dbias_ref)
            dsink_ref[...] = jnp.zeros_like(dsink_ref)
            for kv in range(N_KV_B):
                bias_ref[kv * GQ:(kv + 1) * GQ, :] = _group_bias(bucket_ref[...], table_ref, kv)

        band = pl.ds(_band_start(n), 3 * BLOCK_Q)
        mask = _band_mask(n, s)
        for kv, q_ref in enumerate((q0_ref, q1_ref)):
            wide_cols = _cols(kv * GROUP, GROUP)
            q = _stack_heads(q_ref[...])
            do = _stack_heads(do_ref[:, wide_cols])
            o = _stack_heads(o_ref[:, wide_cols])
            kb = k_ref[band, _cols(kv)]
            vb = v_ref[band, _cols(kv)]
            lse = jnp.concatenate([lse_ref[kv * GROUP + g][:, :1] for g in range(GROUP)], axis=0)
            sc = lax.dot_general(q, kb, _NT, preferred_element_type=F32)
            sc = jnp.where(mask, sc + bias_ref[kv * GQ:(kv + 1) * GQ, :], NEG_INF)
            p = jnp.exp2(sc - lse)
            dp = lax.dot_general(do, vb, _NT, preferred_element_type=F32)
            delta = jnp.sum(do.astype(F32) * o.astype(F32), axis=-1, keepdims=True)
            ds = p * (dp - delta)
            dsb = ds.astype(BF16)
            dq_ref[:, wide_cols] = _unstack_heads(jnp.dot(dsb, kb, preferred_element_type=F32))
            dk_ref[band, _cols(kv)] += lax.dot_general(dsb, q, _TN, preferred_element_type=F32)
            dv_ref[band, _cols(kv)] += lax.dot_general(p.astype(BF16), do, _TN, preferred_element_type=F32)
            dbias_ref[kv * GQ:(kv + 1) * GQ, :] += ds
            sink_part = -jnp.exp2(_group_sink(sink_ref, kv) - lse) * delta
            for g in range(GROUP):
                rows = slice(g * BLOCK_Q, (g + 1) * BLOCK_Q)
                dsink_ref[kv * GROUP + g] += jnp.broadcast_to(
                    jnp.sum(sink_part[rows], axis=0, keepdims=True), (1, HEAD_DIM))

        @pl.when(n == nblk - 1)
        def _():
            bucket_v = bucket_ref[...]
            row = lax.broadcasted_iota(jnp.int32, (N_BUCKETS, HEAD_DIM), 0)
            for h in range(N_HEADS_B):
                acc = dbias_ref[h * BLOCK_Q:(h + 1) * BLOCK_Q, :]
                tot = jnp.zeros((N_BUCKETS, HEAD_DIM), F32)
                for b in range(N_BUCKETS):
                    tot = jnp.where(row == b, jnp.sum(jnp.where(bucket_v == b, acc, 0.0), keepdims=True), tot)
                dtab_ref[h] = tot

    smem = pl.BlockSpec(memory_space=pltpu.SMEM)
    wide = GROUP * HEAD_DIM
    whole = pl.BlockSpec((sp, N_KV_B * HEAD_DIM), lambda n: (0, 0))
    group_b = pl.BlockSpec((BLOCK_Q, N_HEADS_B * HEAD_DIM), lambda n: (n, 1))
    res, c_res = _call(
        body, name="attn_b_bwd", grid=(nblk,),
        in_specs=[
            smem,
            smem,
            pl.BlockSpec((BLOCK_Q, wide), lambda n: (n, COL_QB // GROUP)),
            pl.BlockSpec((BLOCK_Q, wide), lambda n: (n, COL_QB // GROUP + 1)),
            whole,
            whole,
            group_b,
            group_b,
            pl.BlockSpec((N_HEADS_B, BLOCK_Q, HEAD_DIM), lambda n: (0, n, 0)),
            pl.BlockSpec((BLOCK_Q, 3 * BLOCK_Q), lambda n: (0, 0)),
        ],
        out_specs=[
            pl.BlockSpec((BLOCK_Q, N_HEADS_B * HEAD_DIM), lambda n: (n, 0)),
            whole,
            whole,
            pl.BlockSpec((N_HEADS_B, N_BUCKETS, HEAD_DIM), lambda n: (0, 0, 0)),
            pl.BlockSpec((N_HEADS_B, 1, HEAD_DIM), lambda n: (0, 0, 0)),
        ],
        out_shape=[
            jax.ShapeDtypeStruct((s, N_HEADS_B * HEAD_DIM), F32),
            jax.ShapeDtypeStruct((sp, N_KV_B * HEAD_DIM), F32),
            jax.ShapeDtypeStruct((sp, N_KV_B * HEAD_DIM), F32),
            jax.ShapeDtypeStruct((N_HEADS_B, N_BUCKETS, HEAD_DIM), F32),
            jax.ShapeDtypeStruct((N_HEADS_B, 1, HEAD_DIM), F32),
        ],
        scratch_shapes=[pltpu.VMEM((N_KV_B * GQ, 3 * BLOCK_Q), F32), pltpu.VMEM((N_KV_B * GQ, 3 * BLOCK_Q), F32)],
        sem=("arbitrary",),
        args=(table, sink, pb, pb, kpad, vpad, att, datt, lse, bucket), comm=comm, after=after)
    return res if comm is None else (res, c_res)


def _other_chips(x, y):
    return [(x, 1 - y), (1 - x, y), (1 - x, 1 - y)]


_HBM = pl.BlockSpec(memory_space=pltpu.HBM)
_SEM = pl.BlockSpec(memory_space=pltpu.SEMAPHORE)
_SPLIT = pltpu.CompilerParams(has_side_effects=pltpu.SideEffectType.DATAFLOW_SIDE_EFFECTING)


def _in_hbm(a):
    return pltpu.with_memory_space_constraint(a, pltpu.HBM)


def _my_half(rows):
    c = lax.axis_index("c")
    half = rows // 2
    return pl.ds(pl.multiple_of(c * half, half), half), pl.ds(pl.multiple_of((1 - c) * half, half), half)


def _gather_route(shapes):
    def route(src, land):
        x, y, c = lax.axis_index("x"), lax.axis_index("y"), lax.axis_index("c")
        out = []
        for t, shape in enumerate(shapes):
            mine, _ = _my_half(shape[0])
            for px, py in _other_chips(x, y):
                out.append((src[t].at[mine], land[t].at[2 * x + y, mine], land[t].at[2 * px + py, mine], (px, py, c)))
        return out

    return route


def _exchange_route(n_t):
    def route(src, land):
        x, y, c = lax.axis_index("x"), lax.axis_index("y"), lax.axis_index("c")
        out = []
        for t in range(n_t):
            for px, py in _other_chips(x, y):
                k = 2 * px + py
                out.append((src[t].at[k], land[t].at[2 * (2 * x + y) + c], land[t].at[2 * k + c], (px, py, c)))
        return out

    return route


def _own_slot(shape, dtype, slot, block):
    return lax.dynamic_update_slice(lax.empty(shape, dtype), block[None], (slot,) + (0,) * (len(shape) - 1))


def _split_start(name, srcs, lands, route, after):
    n = len(srcs)

    def body(*refs):
        src, land, send_sems, recv_sems, token = refs[:n], refs[n:2 * n], refs[2 * n + 1], refs[2 * n + 2], refs[-1]
        for i, (src_ref, dst_ref, _, to) in enumerate(route(src, land)):
            pltpu.make_async_remote_copy(src_ref=src_ref, dst_ref=dst_ref, send_sem=send_sems.at[i],
                                         recv_sem=recv_sems.at[i], device_id=to, device_id_type=_MESH).start()
        token[...] = jnp.zeros_like(token)

    sem = pltpu.SemaphoreType.DMA((3 * n,))
    lands = list(lands)
    res = pl.pallas_call(
        body, name=name,
        in_specs=[_HBM] * (2 * n) + [_ANY],
        out_specs=[_SEM, _SEM] + [_HBM] * (2 * n) + [pl.BlockSpec(memory_space=pltpu.VMEM)],
        out_shape=[sem, sem] + [pltpu.HBM(a.shape, a.dtype) for a in list(srcs) + lands]
        + [jax.ShapeDtypeStruct((8, 128), F32)],
        input_output_aliases={i: 2 + i for i in range(2 * n)},
        compiler_params=_SPLIT,
    )(*[_in_hbm(a) for a in srcs], *[_in_hbm(a) for a in lands], after)
    return (res[0], res[1]), res[2:2 + n], res[2 + n:2 + 2 * n], res[-1]


def _split_wait(name, srcs, lands, sems, route, after):
    n = len(srcs)

    def body(*refs):
        src, land, send_sems, recv_sems = refs[:n], refs[n:2 * n], refs[2 * n], refs[2 * n + 1]
        for i, (src_ref, _, dst_ref, to) in enumerate(route(src, land)):
            cp = pltpu.make_async_remote_copy(src_ref=src_ref, dst_ref=dst_ref, send_sem=send_sems.at[i],
                                              recv_sem=recv_sems.at[i], device_id=to, device_id_type=_MESH)
            cp.wait_send()
            cp.wait_recv()

    res = pl.pallas_call(
        body, name=name,
        in_specs=[_HBM] * (2 * n) + [_SEM, _SEM, _ANY],
        out_specs=[_HBM] * (2 * n),
        out_shape=[pltpu.HBM(a.shape, a.dtype) for a in list(srcs) + list(lands)],
        input_output_aliases={i: i for i in range(2 * n)},
        compiler_params=_SPLIT,
    )(*srcs, *lands, sems[0], sems[1], after)
    return res[:n], res[n:]


def _comm_only(name, comm):
    return _call(lambda: None, name=name, grid=(1,), in_specs=[], out_specs=[], out_shape=[], args=(), comm=comm)[1]


def _swap_comm(shards, lands):
    n_t = len(lands)

    def copies(land, sems, later):
        send_sems, recv_sems = sems
        x, y = lax.axis_index("x"), lax.axis_index("y")
        sends, recvs = [], []
        for t in range(n_t):
            mine, other = _my_half(shards[t].shape[0])
            for j, (px, py) in enumerate(_other_chips(x, y)):
                k = 2 * px + py
                for part, out in ((mine, sends), (other, recvs)) if later else ((mine, sends),):
                    out.append(pltpu.make_async_remote_copy(
                        src_ref=land[t].at[k, part], dst_ref=land[t].at[k, part], send_sem=send_sems.at[3 * t + j],
                        recv_sem=recv_sems.at[3 * t + j], device_id=_sibling(), device_id_type=_MESH))
        return sends, recvs

    def start(ins, land, sems):
        for cp in copies(land, sems, False)[0]:
            cp.start()

    def finish(ins, land, sems):
        sends, recvs = copies(land, sems, True)
        for cp in recvs:
            cp.wait_recv()
        for cp in sends:
            cp.wait_send()

    return _Comm(
        lands, [jax.ShapeDtypeStruct(a.shape, a.dtype) for a in lands],
        [pltpu.SemaphoreType.DMA((3 * n_t,)), pltpu.SemaphoreType.DMA((3 * n_t,))],
        start, finish, aliases={t: t for t in range(n_t)})


def _forward_comm(partials, lands):
    n_t = len(lands)

    def copies(land, sems, later):
        send_sems, recv_sems = sems
        x, y, c = lax.axis_index("x"), lax.axis_index("y"), lax.axis_index("c")
        sends, recvs = [], []
        for t in range(n_t):
            for j, k in enumerate([2 * x + y] + [2 * px + py for px, py in _other_chips(x, y)]):
                for slot, out in ((2 * k + c, sends), (2 * k + 1 - c, recvs)) if later else ((2 * k + c, sends),):
                    out.append(pltpu.make_async_remote_copy(
                        src_ref=land[t].at[slot], dst_ref=land[t].at[slot], send_sem=send_sems.at[4 * t + j],
                        recv_sem=recv_sems.at[4 * t + j], device_id=_sibling(), device_id_type=_MESH))
        return sends, recvs

    def start(ins, land, sems):
        for cp in copies(land, sems, False)[0]:
            cp.start()

    def finish(ins, land, sems):
        sends, recvs = copies(land, sems, True)
        for cp in recvs:
            cp.wait_recv()
        for cp in sends:
            cp.wait_send()

    return _Comm(
        lands, [jax.ShapeDtypeStruct(a.shape, a.dtype) for a in lands],
        [pltpu.SemaphoreType.DMA((4 * n_t,)), pltpu.SemaphoreType.DMA((4 * n_t,))],
        start, finish, aliases={t: t for t in range(n_t)})


def _allreduce_small(pack):
    rows, d = pack.shape

    def body(p_ref, sum_ref, all_ref, send_sems, recv_sems):
        x, y, c = lax.axis_index("x"), lax.axis_index("y"), lax.axis_index("c")
        me = 4 * x + 2 * y + c
        all_ref[me] = p_ref[...]
        peers = []
        for dx in range(2):
            for dy in range(2):
                for dc in range(2):
                    if dx or dy or dc:
                        px = 1 - x if dx else x
                        py = 1 - y if dy else y
                        pc = 1 - c if dc else c
                        peers.append((4 * dx + 2 * dy + dc - 1, (px, py, pc)))
        sends = []
        for k, to in peers:
            cp = pltpu.make_async_remote_copy(
                src_ref=p_ref, dst_ref=all_ref.at[me], send_sem=send_sems.at[k], recv_sem=recv_sems.at[k],
                device_id=to, device_id_type=_MESH)
            cp.start()
            sends.append(cp)
        for k, (px, py, pc) in peers:
            pltpu.make_async_remote_copy(
                src_ref=p_ref, dst_ref=all_ref.at[4 * px + 2 * py + pc], send_sem=send_sems.at[k],
                recv_sem=recv_sems.at[k], device_id=(px, py, pc), device_id_type=_MESH).wait_recv()
        for cp in sends:
            cp.wait_send()
        tot = all_ref[0]
        for i in range(1, N_DEV):
            tot = tot + all_ref[i]
        sum_ref[...] = tot

    vm = pl.BlockSpec(memory_space=pltpu.VMEM)
    return pl.pallas_call(
        body,
        name="allreduce_small",
        in_specs=[vm],
        out_specs=vm,
        out_shape=jax.ShapeDtypeStruct((rows, d), F32),
        scratch_shapes=[
            pltpu.VMEM((N_DEV, rows, d), F32),
            pltpu.SemaphoreType.DMA((N_DEV - 1,)),
            pltpu.SemaphoreType.DMA((N_DEV - 1,)),
        ],
    )(pack)


def _adamw_math(w, g, m, v):
    m = ADAM_B1 * m + (1.0 - ADAM_B1) * g
    v = ADAM_B2 * v + (1.0 - ADAM_B2) * (g * g)
    m_hat = m / (1.0 - ADAM_B1 ** ADAM_STEP)
    v_hat = v / (1.0 - ADAM_B2 ** ADAM_STEP)
    delta = -ADAM_LR * (m_hat / (jnp.sqrt(v_hat) + ADAM_EPS) + ADAM_WD * w)
    return delta, m, v


def _sum_adamw(parts, w, m, v, *, name, tr=256):
    r, c = w.shape
    tr = min(tr, r)
    tc = min(c, 1024)

    def body(p_ref, w_ref, m_ref, v_ref, g_ref, d_ref, m2_ref, v2_ref):
        g = p_ref[0].astype(F32)
        for i in range(1, N_DEV):
            g = g + p_ref[i].astype(F32)
        delta, m2, v2 = _adamw_math(w_ref[...], g, m_ref[...], v_ref[...])
        g_ref[...] = g
        d_ref[...] = delta
        m2_ref[...] = m2
        v2_ref[...] = v2

    blk = pl.BlockSpec((tr, tc), lambda i, j: (i, j))
    return pl.pallas_call(
        body,
        name=name,
        grid=(r // tr, c // tc),
        in_specs=[pl.BlockSpec((N_DEV, tr, tc), lambda i, j: (0, i, j)), blk, blk, blk],
        out_specs=[blk] * 4,
        out_shape=[jax.ShapeDtypeStruct((r, c), F32)] * 4,
        compiler_params=_params(("parallel", "parallel")),
    )(parts, w, m, v)


def _adamw_small(g, w, m, v):
    def body(g_ref, w_ref, m_ref, v_ref, d_ref, m2_ref, v2_ref):
        delta, m2, v2 = _adamw_math(w_ref[...], g_ref[...], m_ref[...], v_ref[...])
        d_ref[...] = delta
        m2_ref[...] = m2
        v2_ref[...] = v2

    vm = pl.BlockSpec(memory_space=pltpu.VMEM)
    return pl.pallas_call(
        body,
        name="adamw_small",
        in_specs=[vm] * 4,
        out_specs=[vm] * 3,
        out_shape=[jax.ShapeDtypeStruct(g.shape, F32)] * 3,
    )(g, w, m, v)


def _relu2_epilogue(acc):
    ra = jnp.maximum(acc, 0.0)
    return ra * ra, ra


def _residual_norm_epilogue(acc, res, g):
    h = acc + res
    return h, h * lax.rsqrt(jnp.mean(h * h, axis=-1, keepdims=True) + EPS) * g


def _rows(stacked):
    return stacked.reshape(stacked.shape[0] * stacked.shape[1], stacked.shape[2])


def _by_chip(mat):
    return mat.reshape(N_CHIPS, mat.shape[0] // N_CHIPS, mat.shape[1])


def _local_step(x, p, target, shards, small, update):
    s, d = x.shape
    cos_t, sin_t = _rope_tables(s)
    bucket = _band_buckets()
    p_bf = p.astype(BF16)
    wts = {}

    chip = 2 * lax.axis_index("x") + lax.axis_index("y")
    core = lax.axis_index("c")

    def gather(tag, names, after):
        srcs = [cast[n] for n in names]
        route = _gather_route([a.shape for a in srcs])
        sems, srcs, lands, token = _split_start(f"gather_start_{tag}", srcs, [zones[n] for n in names], route, after)

        def landed(done):
            got_srcs, got_lands = _split_wait(f"gather_wait_{tag}", srcs, lands, sems, route, done)
            comm = _swap_comm(got_srcs, got_lands)
            comm.waited = got_srcs[0]
            return comm

        return landed, token

    def prepare(n, zero):
        cast[n] = (shards[n] + zero).astype(BF16)
        zones[n] = _own_slot((N_CHIPS,) + cast[n].shape, BF16, chip, cast[n])

    cast, zones = {}, {}
    prepare("w_in", 0.0)
    in_landed, token = gather("in", ["w_in"], small["attn_norm_g"])
    for n in shards:
        if n != "w_in":
            prepare(n, token[:1, :1])
    g_attn = small["attn_norm_g"] + token[:1, :1]
    u = _rms_fwd(x, g_attn, name="norm_attn")
    prepared = u[:1, :1].astype(F32) + sum(
        (lax.dynamic_slice(zones[n], (chip, 0, 0), (1, 1, 1))[0] + cast[n][:1, :1]).astype(F32)
        for n in zones if n != "w_in")
    (wts["w_in"],) = _comm_only("swap_w_in", in_landed(prepared))
    mid_landed, token = gather("mid", ["w_out"], wts["w_in"])
    proj = _matmul(u, wts["w_in"], mode="nn", out_dtypes=[F32], name="mm_in", bn=768, after=token)
    pb, (w_out_s,) = _qk_prep(proj, small["q_norm_g"], small["k_norm_g"], cos_t, sin_t, comm=mid_landed(proj))
    wts["w_out"] = _rows(w_out_s)
    up_landed, token = gather("up", ["w_up"], pb)
    att_a, lse_a = _attn_a_fwd(pb, after=token)
    pad = ((PAD_LO, PAD_HI), (0, 0))
    kpad = jnp.pad(pb[:, COL_KB * HEAD_DIM:COL_VB * HEAD_DIM], pad)
    vpad = jnp.pad(pb[:, COL_VB * HEAD_DIM:], pad)
    up_swap = up_landed(att_a)
    down_landed, token = gather("down", ["w_down"], up_swap.waited)
    (att, lse_b), (wts["w_up"],) = _attn_b_fwd(pb, kpad, vpad, bucket, small["rel_bias_table"],
                                               small["sink_logits"], att_a, comm=up_swap, after=token)
    h1, mn = _matmul(att, wts["w_out"], mode="nn", out_dtypes=[F32, BF16], name="mm_out", bm=512, bn=d,
                     epilogue=_residual_norm_epilogue, extras=(x,), vecs=(small["mlp_norm_g"],))
    r, ra = _matmul(mn, wts["w_up"], mode="nn", out_dtypes=[BF16, BF16], name="mm_up", epilogue=_relu2_epilogue,
                    bm=2048)
    (w_down_s,) = _comm_only("swap_w_down", down_landed(r))
    wts["w_down"] = _rows(w_down_s)
    late_landed, token = gather("late", ["w_gate", "ple_w"], w_down_s)
    h2 = _matmul(r, wts["w_down"], mode="nn", out_dtypes=[F32], name="mm_down",
                 epilogue=lambda acc, res: (acc + res,), extras=(h1,), after=token)
    ng, (w_gate_s, wts["ple_w"]) = _rms_fwd(h2, small["gate_norm_g"], name="norm_gate", comm=late_landed(h2))
    wts["w_gate"] = _rows(w_gate_s)
    gate = _matmul(ng, wts["w_gate"], mode="nn", out_dtypes=[F32], name="mm_gate",
                   epilogue=lambda acc: (1.0 / (1.0 + jnp.exp(-acc)),))
    pp = _matmul(p_bf, wts["ple_w"], mode="nn", out_dtypes=[F32], name="mm_ple", bn=512)
    dh3, dz, dpp, dg_final, dg_ple, loss = _tail(h2, gate, pp, target, small["ple_norm_g"], small["final_norm_g"])

    dng = _matmul(dz, wts["w_gate"], mode="nt", out_dtypes=[F32], name="mm_gate_dx")
    gw_gate = _matmul(ng, dz, mode="tn", out_dtypes=[BF16], name="mm_gate_dw", bm=512, bk=4096)
    gw_ple = _matmul(p_bf, dpp, mode="tn", out_dtypes=[BF16], name="mm_ple_dw", bn=512, out_stack=N_CHIPS)
    dh2, dh2_bf, dg_gate = _rms_bwd(h2, dng, small["gate_norm_g"], dh3, name="norm_gate_bwd", want_bf16=True)

    def exchange(tag, partials, after):
        route = _exchange_route(len(partials))
        lands = [_own_slot((N_DEV,) + g.shape[1:], g.dtype, 2 * chip + core,
                           lax.dynamic_index_in_dim(g, chip, 0, keepdims=False)) for g in partials]
        sems, srcs, lands, token = _split_start(f"exchange_start_{tag}", partials, lands, route, after)

        def landed(done):
            got_srcs, got_lands = _split_wait(f"exchange_wait_{tag}", srcs, lands, sems, route, done)
            comm = _forward_comm(got_srcs, got_lands)
            comm.waited = got_srcs[0]
            return comm

        return landed, token

    big = {}
    gate_landed, token = exchange("gate", [_by_chip(gw_gate), gw_ple], dh2_bf)
    gw_down = _matmul(r, dh2_bf, mode="tn", out_dtypes=[BF16], name="mm_down_dw", after=token, bm=512, bk=4096)
    da, (parts_gate, parts_ple) = _matmul(
        dh2_bf, wts["w_down"], mode="nt", out_dtypes=[BF16], name="mm_down_dx", bm=2048,
        epilogue=lambda acc, ra_v: (acc * (2.0 * ra_v.astype(F32)),), extras=(ra,), comm=gate_landed(gw_down))
    down_landed, token = exchange("down", [_by_chip(gw_down)], da)
    big["w_gate"], big["ple_w"] = update("w_gate", parts_gate), update("ple_w", parts_ple)
    gw_up = _matmul(mn, da, mode="tn", out_dtypes=[BF16], name="mm_up_dw", out_stack=N_CHIPS, after=token,
                    bm=512, bk=4096)
    dmn = _matmul(da, wts["w_up"], mode="nt", out_dtypes=[F32], name="mm_up_dx", bk=4096)
    dh1, dh1_bf, dg_mlp = _rms_bwd(h1, dmn, small["mlp_norm_g"], dh2, name="norm_mlp_bwd", want_bf16=True)
    datt, (parts_down,) = _matmul(dh1_bf, wts["w_out"], mode="nt", out_dtypes=[BF16], name="mm_out_dx",
                                  comm=down_landed(dh1_bf))
    gw_out = _matmul(att, dh1_bf, mode="tn", out_dtypes=[BF16], name="mm_out_dw", bm=512, bk=4096)
    up_landed, token = exchange("up", [gw_up], datt)
    dqb, dkpad, dvpad, dtab, dsink = _attn_b_bwd(pb, kpad, vpad, att, datt, lse_b, bucket,
                                                 small["rel_bias_table"], small["sink_logits"], after=token)
    dqa, dka, dva = _attn_a_bwd(pb, att, datt, lse_a)
    up_forward = up_landed(dqa)
    out_landed, token = exchange("out", [_by_chip(gw_out)], up_forward.waited)
    (dproj, dg_q, dg_k), (parts_up,) = _qk_bwd(dqa, dka, dva, dqb, dkpad, dvpad, proj,
                                               small["q_norm_g"], small["k_norm_g"], cos_t, sin_t,
                                               comm=up_forward, after=token)
    gw_in = _matmul(u, dproj, mode="tn", out_dtypes=[BF16], name="mm_in_dw", bn=768, out_stack=N_CHIPS,
                    bm=512, bk=4096)
    out_forward = out_landed(gw_in)
    in_landed, token = exchange("in", [gw_in], out_forward.waited)
    du, (parts_out,) = _matmul(dproj, wts["w_in"], mode="nt", out_dtypes=[F32], name="mm_in_dx", bk=3072,
                               comm=out_forward, after=token)
    grad_x, dg_attn = _rms_bwd(x, du, small["attn_norm_g"], dh1, name="norm_attn_bwd", want_bf16=False)
    for n, parts in (("w_down", parts_down), ("w_up", parts_up), ("w_out", parts_out)):
        big[n] = update(n, parts)
    done = dg_attn + sum(big[n][0][0, :1, :] for n in ("w_down", "w_up", "w_out"))
    (parts_in,) = _comm_only("forward_w_in", in_landed(done))
    big["w_in"] = update("w_in", parts_in)

    small_g = {
        "attn_norm_g": dg_attn, "mlp_norm_g": dg_mlp, "ple_norm_g": dg_ple, "gate_norm_g": dg_gate,
        "final_norm_g": dg_final, "q_norm_g": dg_q, "k_norm_g": dg_k,
        "sink_logits": dsink[:, 0, 0][None, :], "rel_bias_table": dtab[:, :, 0].T,
    }
    return loss, grad_x, big, small_g


_SMALL_ROWS = ["attn_norm_g", "mlp_norm_g", "ple_norm_g", "gate_norm_g", "final_norm_g"]
_PACK_ROWS = 8


def _pack_small(vals, d):
    rows = [vals[n].reshape(1, d) for n in _SMALL_ROWS]
    misc = jnp.concatenate([
        vals["q_norm_g"].reshape(1, HEAD_DIM), vals["k_norm_g"].reshape(1, HEAD_DIM),
        jnp.pad(vals["sink_logits"].reshape(1, N_HEADS_B), ((0, 0), (0, HEAD_DIM - N_HEADS_B))),
        vals["rel_bias_table"].reshape(1, N_BUCKETS * N_HEADS_B)], axis=1)
    rows.append(jnp.pad(misc, ((0, 0), (0, d - misc.shape[1]))))
    rows.append(jnp.zeros((_PACK_ROWS - len(rows), d), F32))
    return jnp.concatenate(rows, axis=0).astype(F32)


def _unpack_small(pack, shapes):
    out = {n: pack[i].reshape(shapes[n]) for i, n in enumerate(_SMALL_ROWS)}
    misc = pack[len(_SMALL_ROWS)]
    out["q_norm_g"] = misc[:HEAD_DIM].reshape(shapes["q_norm_g"])
    out["k_norm_g"] = misc[HEAD_DIM:2 * HEAD_DIM].reshape(shapes["k_norm_g"])
    out["sink_logits"] = misc[2 * HEAD_DIM:2 * HEAD_DIM + N_HEADS_B].reshape(shapes["sink_logits"])
    out["rel_bias_table"] = misc[3 * HEAD_DIM:3 * HEAD_DIM + N_BUCKETS * N_HEADS_B].reshape(shapes["rel_bias_table"])
    return out


_WEIGHTS = ["attn_norm_g", "w_in", "q_norm_g", "k_norm_g", "sink_logits", "w_out", "mlp_norm_g", "w_up", "w_down",
            "ple_w", "ple_norm_g", "gate_norm_g", "w_gate", "rel_bias_table", "final_norm_g"]
_BIG = ["w_in", "w_out", "w_up", "w_down", "ple_w", "w_gate"]


def kernel(x, p, attn_norm_g, w_in, q_norm_g, k_norm_g, sink_logits, w_out, mlp_norm_g, w_up, w_down, ple_w, ple_norm_g, gate_norm_g, w_gate, rel_bias_table, final_norm_g, loss_target, m_attn_norm_g, m_w_in, m_q_norm_g, m_k_norm_g, m_sink_logits, m_w_out, m_mlp_norm_g, m_w_up, m_w_down, m_ple_w, m_ple_norm_g, m_gate_norm_g, m_w_gate, m_rel_bias_table, m_final_norm_g, v_attn_norm_g, v_w_in, v_q_norm_g, v_k_norm_g, v_sink_logits, v_w_out, v_mlp_norm_g, v_w_up, v_w_down, v_ple_w, v_ple_norm_g, v_gate_norm_g, v_w_gate, v_rel_bias_table, v_final_norm_g):
    given = dict(locals())
    w = {n: given[n] for n in _WEIGHTS}
    m = {n: given["m_" + n] for n in _WEIGHTS}
    v = {n: given["v_" + n] for n in _WEIGHTS}
    d = x.shape[-1]

    shards = {n: w[n][0] for n in _BIG}
    small = {
        "attn_norm_g": w["attn_norm_g"], "mlp_norm_g": w["mlp_norm_g"], "ple_norm_g": w["ple_norm_g"],
        "gate_norm_g": w["gate_norm_g"], "final_norm_g": w["final_norm_g"].reshape(1, d),
        "q_norm_g": w["q_norm_g"], "k_norm_g": w["k_norm_g"], "sink_logits": w["sink_logits"],
        "rel_bias_table": w["rel_bias_table"],
    }

    def update(n, parts):
        res = _sum_adamw(parts, w[n][0], m[n][0], v[n][0], name="adamw_" + n)
        return [t.reshape(w[n].shape) for t in res]

    loss_part, grad_x, big, small_g = _local_step(x[0], p[0, 0], loss_target[0], shards, small, update)
    grads, deltas, new_m, new_v = [{n: big[n][i] for n in _BIG} for i in range(4)]

    shapes = {n: w[n].shape for n in _WEIGHTS if n not in _BIG}
    pack = _pack_small(small_g, d)
    pack = pack.at[_PACK_ROWS - 1, :1].add(0.0 * grads["w_in"][0, 0, :1])
    pack = pack.at[_PACK_ROWS - 1, 1].set(loss_part[0, 0])
    g_small = _allreduce_small(pack)
    loss = g_small[_PACK_ROWS - 1, 1]
    d_small, m_small, v_small = _adamw_small(g_small, _pack_small(w, d), _pack_small(m, d), _pack_small(v, d))
    grads.update(_unpack_small(g_small, shapes))
    deltas.update(_unpack_small(d_small, shapes))
    new_m.update(_unpack_small(m_small, shapes))
    new_v.update(_unpack_small(v_small, shapes))

    return (loss, grad_x[None], *[grads[n] for n in _WEIGHTS], *[deltas[n] for n in _WEIGHTS],
            *[new_m[n] for n in _WEIGHTS], *[new_v[n] for n in _WEIGHTS])
```

```python
import functools
import math

import jax
import jax.numpy as jnp
from jax import lax
from jax.experimental import pallas as pl
from jax.experimental.pallas import tpu as pltpu

F32 = jnp.float32
BF16 = jnp.bfloat16

HEAD_DIM = 128
N_HEADS_A = 8
N_KV_A = 2
N_HEADS_B = 8
N_KV_B = 2
GROUP = 4
GRID_W = 64
BLOCK_Q = 128
WINDOW = 128
N_BUCKETS = 32
MAX_DISTANCE = 128
ROPE_THETA = 10000.0
EPS = 1e-6
NEG_INF = -1e30
ATT_SCALE = HEAD_DIM ** -0.5
LOG2E = math.log2(math.e)
LN2 = math.log(2.0)
Q_SCALE = ATT_SCALE * LOG2E
PAD_LO, PAD_HI = 256, 128
ADAM_LR = 0.001
ADAM_B1 = 0.9
ADAM_B2 = 0.999
ADAM_EPS = 1e-08
ADAM_WD = 0.01
ADAM_STEP = 10

N_CHIPS = 4
N_DEV = 8
COL_QA, COL_KA, COL_VA, COL_QB, COL_KB, COL_VB = 0, 8, 10, 12, 20, 22

VMEM_LIMIT = 52 * 1024 * 1024


def _params(sem=None, collective_id=None):
    return pltpu.CompilerParams(dimension_semantics=sem, vmem_limit_bytes=VMEM_LIMIT, collective_id=collective_id)


_ANY = pl.BlockSpec(memory_space=pl.ANY)
_MESH = pl.DeviceIdType.MESH
SIBLING_BARRIER_ID = 1


def _sibling():
    return (lax.axis_index("x"), lax.axis_index("y"), 1 - lax.axis_index("c"))


class _Comm:
    def __init__(self, inputs, out_shapes, sems, start, finish, aliases=None):
        self.inputs, self.out_shapes, self.sems = list(inputs), list(out_shapes), list(sems)
        self.start, self.finish, self.aliases = start, finish, dict(aliases or {})


def _call(body, *, name, grid, in_specs, out_specs, out_shape, args, scratch_shapes=(), sem=None, comm=None,
          after=None, aliases=None):
    in_specs, out_specs, out_shape = list(in_specs), list(out_specs), list(out_shape)
    scratch_shapes = list(scratch_shapes)
    n_in, n_out, n_sc = len(in_specs), len(out_specs), len(scratch_shapes)
    behind = [] if after is None else [after]
    aliases = dict(aliases or {})
    if comm is None:
        res = pl.pallas_call(
            (lambda *refs: body(*refs[:n_in], *refs[n_in + len(behind):])) if behind else body,
            name=name, grid=grid, in_specs=in_specs + [_ANY] * len(behind), out_specs=out_specs,
            out_shape=out_shape, scratch_shapes=scratch_shapes, input_output_aliases=aliases,
            compiler_params=_params(sem))(*args, *behind)
        return list(res), []
    c_in, c_out = len(comm.inputs), len(comm.out_shapes)

    def hosted(*refs):
        pos = [0]

        def take(n):
            pos[0] += n
            return refs[pos[0] - n:pos[0]]

        ins, c_ins, _, outs, c_outs, scr = (take(n_in), take(c_in), take(len(behind)), take(n_out), take(c_out),
                                            take(n_sc))
        c_sems = refs[pos[0]:]
        ids = [pl.program_id(a) for a in range(len(grid))]
        first = functools.reduce(jnp.logical_and, [i == 0 for i in ids])
        last = functools.reduce(jnp.logical_and, [i == g - 1 for i, g in zip(ids, grid)])

        @pl.when(first)
        def _():
            barrier = pltpu.get_barrier_semaphore()
            pl.semaphore_signal(barrier, inc=1, device_id=_sibling(), device_id_type=_MESH)
            pl.semaphore_wait(barrier, 1)
            comm.start(c_ins, c_outs, c_sems)

        body(*ins, *outs, *scr)

        @pl.when(last)
        def _():
            comm.finish(c_ins, c_outs, c_sems)

    res = pl.pallas_call(
        hosted, name=name, grid=grid, in_specs=in_specs + [_ANY] * (c_in + len(behind)),
        out_specs=out_specs + [_ANY] * c_out,
        out_shape=out_shape + comm.out_shapes, scratch_shapes=scratch_shapes + comm.sems,
        input_output_aliases={**aliases, **{n_in + i: n_out + o for i, o in comm.aliases.items()}},
        compiler_params=_params(("arbitrary",) * len(grid), SIBLING_BARRIER_ID))(*args, *comm.inputs, *behind)
    return list(res[:n_out]), list(res[n_out:])


def _matmul(a, b, *, mode, out_dtypes, name, epilogue=None, extras=(), bm=1024, bn=1024, bk=2048,
            out_stack=0, comm=None, after=None, vecs=()):
    stacked = b.ndim == 3
    if mode == "nn":
        m, k = a.shape
        if stacked:
            nj, kb, ns = b.shape
            n, ks = nj * ns, k
        else:
            kb, n = b.shape
            ns, ks = n, k
        dn = (((1,), (0,)), ((), ()))
    elif mode == "nt":
        m, k = a.shape
        if stacked:
            nj, n, ks = b.shape
            kb = nj * ks
        else:
            n, kb = b.shape
            ks = kb
        ns = n
        dn = (((1,), (1,)), ((), ()))
    else:
        k, m = a.shape
        kb, n = b.shape
        ns, ks = n, k
        dn = (((0,), (0,)), ((), ()))
    assert k == kb and not (stacked and mode == "tn")
    ns_out = n // out_stack if out_stack else n
    per_blk = min(bk, k) // ks if stacked and mode == "nt" and bk > ks else 0
    bm, bn, bk = min(bm, m), min(bn, ns, ns_out), per_blk * ks if per_blk else min(bk, ks)
    assert m % bm == 0 and ns % bn == 0 and ns_out % bn == 0 and (k % bk == 0 if per_blk else ks % bk == 0)
    gm, gn, gk = m // bm, n // bn, k // bk

    if mode == "tn":
        a_spec = pl.BlockSpec((bk, bm), lambda i, j, q: (q, i))
    else:
        a_spec = pl.BlockSpec((bm, bk), lambda i, j, q: (i, q))
    if mode == "nt":
        if per_blk:
            b_spec = pl.BlockSpec((per_blk, bn, ks), lambda i, j, q: (q, j, 0))
        elif stacked:
            per = ks // bk
            b_spec = pl.BlockSpec((None, bn, bk), lambda i, j, q: (q // per, j, q % per))
        else:
            b_spec = pl.BlockSpec((bn, bk), lambda i, j, q: (j, q))
    else:
        if stacked:
            per = ns // bn
            b_spec = pl.BlockSpec((None, bk, bn), lambda i, j, q: (j // per, q, j % per))
        else:
            b_spec = pl.BlockSpec((bk, bn), lambda i, j, q: (q, j))
    ex_spec = pl.BlockSpec((bm, bn), lambda i, j, q: (i, j))
    if out_stack:
        per_o = ns_out // bn
        o_spec = pl.BlockSpec((None, bm, bn), lambda i, j, q: (j // per_o, i, j % per_o))
        o_shape = (out_stack, m, ns_out)
    else:
        o_spec = ex_spec
        o_shape = (m, n)
    n_ex, n_out = len(extras) + len(vecs), len(out_dtypes)

    def body(a_ref, b_ref, *rest):
        ex, outs = rest[:n_ex], rest[n_ex:n_ex + n_out]
        if per_blk:
            part = sum(lax.dot_general(a_ref[:, t * ks:(t + 1) * ks], b_ref[t], dn, preferred_element_type=F32)
                       for t in range(per_blk))
        else:
            part = lax.dot_general(a_ref[...], b_ref[...], dn, preferred_element_type=F32)

        def finish(acc):
            res = epilogue(acc, *[e[...] for e in ex]) if epilogue else (acc,)
            for o, r in zip(outs, res):
                o[...] = r.astype(o.dtype)

        if gk == 1:
            finish(part)
        else:
            acc_ref = rest[-1]
            q = pl.program_id(2)

            @pl.when(q == 0)
            def _():
                acc_ref[...] = part

            @pl.when(q > 0)
            def _():
                acc_ref[...] += part

            @pl.when(q == gk - 1)
            def _():
                finish(acc_ref[...])

    res, c_res = _call(
        body, name=name, grid=(gm, gn, gk),
        in_specs=[a_spec, b_spec] + [ex_spec] * len(extras)
        + [pl.BlockSpec((1, bn), lambda i, j, q: (0, j))] * len(vecs),
        out_specs=[o_spec] * n_out,
        out_shape=[jax.ShapeDtypeStruct(o_shape, dt) for dt in out_dtypes],
        scratch_shapes=[pltpu.VMEM((bm, bn), F32)] if gk > 1 else [],
        sem=("parallel", "parallel", "arbitrary"), args=(a, b, *extras, *vecs), comm=comm, after=after)
    res = res[0] if n_out == 1 else res
    return res if comm is None else (res, c_res)


def _rms_fwd(x, g, *, name, tm=256, comm=None):
    s, d = x.shape
    tm = min(tm, s)

    def body(x_ref, g_ref, o_ref):
        xf = x_ref[...]
        r = lax.rsqrt(jnp.mean(xf * xf, axis=-1, keepdims=True) + EPS)
        o_ref[...] = (xf * r * g_ref[...]).astype(o_ref.dtype)

    res, c_res = _call(
        body, name=name, grid=(s // tm,),
        in_specs=[pl.BlockSpec((tm, d), lambda i: (i, 0)), pl.BlockSpec((1, d), lambda i: (0, 0))],
        out_specs=[pl.BlockSpec((tm, d), lambda i: (i, 0))],
        out_shape=[jax.ShapeDtypeStruct((s, d), BF16)],
        sem=("parallel",), args=(x, g), comm=comm)
    return res[0] if comm is None else (res[0], c_res)


def _rms_bwd(x, dy, g, add, *, name, want_bf16, tm=256):
    s, d = x.shape
    tm = min(tm, s)

    def body(x_ref, dy_ref, g_ref, add_ref, dx_ref, *rest):
        dg_ref = rest[-1]
        i = pl.program_id(0)
        xf = x_ref[...]
        dyf = dy_ref[...].astype(F32)
        r = lax.rsqrt(jnp.mean(xf * xf, axis=-1, keepdims=True) + EPS)
        xh = xf * r
        dyg = dyf * g_ref[...]
        dx = r * (dyg - xh * jnp.mean(dyg * xh, axis=-1, keepdims=True))
        tot = add_ref[...] + dx
        dx_ref[...] = tot
        if want_bf16:
            rest[0][...] = tot.astype(BF16)
        part = jnp.sum(dyf * xh, axis=0, keepdims=True)

        @pl.when(i == 0)
        def _():
            dg_ref[...] = part

        @pl.when(i > 0)
        def _():
            dg_ref[...] += part

    row = pl.BlockSpec((tm, d), lambda i: (i, 0))
    vec = pl.BlockSpec((1, d), lambda i: (0, 0))
    out_specs = [row] + ([row] if want_bf16 else []) + [vec]
    out_shape = [jax.ShapeDtypeStruct((s, d), F32)]
    if want_bf16:
        out_shape.append(jax.ShapeDtypeStruct((s, d), BF16))
    out_shape.append(jax.ShapeDtypeStruct((1, d), F32))
    return pl.pallas_call(
        body,
        name=name,
        grid=(s // tm,),
        in_specs=[row, row, vec, row],
        out_specs=out_specs,
        out_shape=out_shape,
        compiler_params=_params(("arbitrary",)),
    )(x, dy, g, add)


def _tail(h2, gate, pp, target, g_ple, g_final, *, tm=256):
    s, d = h2.shape
    tm = min(tm, s)

    def body(h2_ref, gate_ref, pp_ref, t_ref, gp_ref, gf_ref, dh3_ref, dz_ref, dpp_ref, dgf_ref, dgp_ref, loss_ref):
        i = pl.program_id(0)
        ppf = pp_ref[...]
        gate_v = gate_ref[...]
        r_p = lax.rsqrt(jnp.mean(ppf * ppf, axis=-1, keepdims=True) + EPS)
        eh = ppf * r_p
        e = eh * gp_ref[...]
        h3 = h2_ref[...] + gate_v * e
        r_f = lax.rsqrt(jnp.mean(h3 * h3, axis=-1, keepdims=True) + EPS)
        yh = h3 * r_f
        diff = yh * gf_ref[...] - t_ref[...]
        loss_part = 0.5 * jnp.sum(jnp.mean(diff * diff, axis=-1, keepdims=True), axis=0, keepdims=True)
        dy = diff / d
        dgf = jnp.sum(dy * yh, axis=0, keepdims=True)
        dyg = dy * gf_ref[...]
        dh3 = r_f * (dyg - yh * jnp.mean(dyg * yh, axis=-1, keepdims=True))
        dh3_ref[...] = dh3
        de = dh3 * gate_v
        dz_ref[...] = (dh3 * e * gate_v * (1.0 - gate_v)).astype(BF16)
        dgp = jnp.sum(de * eh, axis=0, keepdims=True)
        deg = de * gp_ref[...]
        dpp_ref[...] = (r_p * (deg - eh * jnp.mean(deg * eh, axis=-1, keepdims=True))).astype(BF16)
        loss_row = jnp.broadcast_to(loss_part, (1, 128))

        @pl.when(i == 0)
        def _():
            dgf_ref[...] = dgf
            dgp_ref[...] = dgp
            loss_ref[...] = loss_row

        @pl.when(i > 0)
        def _():
            dgf_ref[...] += dgf
            dgp_ref[...] += dgp
            loss_ref[...] += loss_row

    row = pl.BlockSpec((tm, d), lambda i: (i, 0))
    vec = pl.BlockSpec((1, d), lambda i: (0, 0))
    return pl.pallas_call(
        body,
        name="tail_fwd_bwd",
        grid=(s // tm,),
        in_specs=[row, row, row, row, vec, vec],
        out_specs=[row, row, row, vec, vec, pl.BlockSpec((1, 128), lambda i: (0, 0))],
        out_shape=[
            jax.ShapeDtypeStruct((s, d), F32),
            jax.ShapeDtypeStruct((s, d), BF16),
            jax.ShapeDtypeStruct((s, d), BF16),
            jax.ShapeDtypeStruct((1, d), F32),
            jax.ShapeDtypeStruct((1, d), F32),
            jax.ShapeDtypeStruct((1, 128), F32),
        ],
        compiler_params=_params(("arbitrary",)),
    )(h2, gate, pp, target, g_ple, g_final)


def _rope_tables(s):
    rows = s // GRID_W
    half = HEAD_DIM // 2
    inv_freq = ROPE_THETA ** (-jnp.arange(0, half, 2, dtype=F32) / half)
    ang_r = jnp.arange(rows, dtype=jnp.int32).astype(F32)[:, None] * inv_freq
    ang_c = jnp.arange(GRID_W, dtype=jnp.int32).astype(F32)[:, None] * inv_freq
    cr, sr = (jnp.repeat(t, GRID_W, axis=0) for t in (jnp.cos(ang_r), jnp.sin(ang_r)))
    cc, sc = (jnp.tile(t, (rows, 1)) for t in (jnp.cos(ang_c), jnp.sin(ang_c)))
    cos_t = jnp.concatenate([cr, cr, cc, cc], axis=-1)
    sin_t = jnp.concatenate([-sr, sr, -sc, sc], axis=-1)
    return cos_t, sin_t


def _low_quarters(shape):
    return (lax.broadcasted_iota(jnp.int32, shape, len(shape) - 1) % 64) < 32


def _swap_quarters(x, low):
    up = pltpu.roll(x, HEAD_DIM - 32, x.ndim - 1)
    down = pltpu.roll(x, 32, x.ndim - 1)
    return jnp.where(low, up, down)


def _cols(first, count=1):
    return slice(first * HEAD_DIM, (first + count) * HEAD_DIM)


def _qk_prep(proj, g_q, g_k, cos_t, sin_t, *, tm=256, comm=None):
    s, n = proj.shape
    tm = min(tm, s)

    def body(x_ref, gq_ref, gk_ref, c_ref, s_ref, o_ref):
        cos_v, sin_v = c_ref[...], s_ref[...]
        low = _low_quarters(cos_v.shape)
        for h in range(COL_VA):
            x = x_ref[:, _cols(h)]
            g = gq_ref[...] if h < COL_KA else gk_ref[...]
            xn = x * lax.rsqrt(jnp.mean(x * x, axis=-1, keepdims=True) + EPS) * g
            xr = xn * cos_v + _swap_quarters(xn, low) * sin_v
            if h < COL_KA:
                xr = xr * Q_SCALE
            o_ref[:, _cols(h)] = xr.astype(BF16)
        o_ref[:, _cols(COL_VA, 2)] = x_ref[:, _cols(COL_VA, 2)].astype(BF16)
        o_ref[:, _cols(COL_QB, N_HEADS_B)] = (x_ref[:, _cols(COL_QB, N_HEADS_B)] * Q_SCALE).astype(BF16)
        o_ref[:, _cols(COL_KB, 4)] = x_ref[:, _cols(COL_KB, 4)].astype(BF16)

    row = pl.BlockSpec((tm, n), lambda i: (i, 0))
    tab = pl.BlockSpec((tm, HEAD_DIM), lambda i: (i, 0))
    vec = pl.BlockSpec((1, HEAD_DIM), lambda i: (0, 0))
    res, c_res = _call(
        body, name="qk_prep", grid=(s // tm,),
        in_specs=[row, vec, vec, tab, tab],
        out_specs=[row],
        out_shape=[jax.ShapeDtypeStruct((s, n), BF16)],
        sem=("parallel",), args=(proj, g_q, g_k, cos_t, sin_t), comm=comm)
    return res[0] if comm is None else (res[0], c_res)


def _qk_bwd(dqa, dka, dva, dqb, dkpad, dvpad, proj, g_q, g_k, cos_t, sin_t, *, comm=None, after=None):
    s, n = proj.shape
    tm = min(PAD_LO, s)
    assert PAD_LO % tm == 0
    lo = PAD_LO // tm

    def body(dqa_ref, dka_ref, dva_ref, dqb_ref, dkb_ref, dvb_ref, x_ref, gq_ref, gk_ref, c_ref, s_ref,
             o_ref, dgq_ref, dgk_ref):
        i = pl.program_id(0)
        cos_v, sin_v = c_ref[...], s_ref[...]
        low = _low_quarters(cos_v.shape)

        def head(d, x, g):
            dn = d * cos_v + _swap_quarters(d * sin_v, low)
            r = lax.rsqrt(jnp.mean(x * x, axis=-1, keepdims=True) + EPS)
            xh = x * r
            dng = dn * g
            dx = r * (dng - xh * jnp.mean(dng * xh, axis=-1, keepdims=True))
            return dx.astype(BF16), jnp.sum(dn * xh, axis=0, keepdims=True)

        acc_q = jnp.zeros((1, HEAD_DIM), F32)
        acc_k = jnp.zeros((1, HEAD_DIM), F32)
        for h in range(N_HEADS_A):
            o_ref[:, _cols(h)], part = head(dqa_ref[:, _cols(h)] * ATT_SCALE, x_ref[:, _cols(h)], gq_ref[...])
            acc_q = acc_q + part
        for h in range(N_KV_A):
            o_ref[:, _cols(COL_KA + h)], part = head(dka_ref[:, _cols(h)] * LN2, x_ref[:, _cols(COL_KA + h)],
                                                     gk_ref[...])
            acc_k = acc_k + part
        o_ref[:, _cols(COL_VA, 2)] = dva_ref[...].astype(BF16)
        o_ref[:, _cols(COL_QB, N_HEADS_B)] = (dqb_ref[...] * ATT_SCALE).astype(BF16)
        o_ref[:, _cols(COL_KB, 2)] = (dkb_ref[...] * LN2).astype(BF16)
        o_ref[:, _cols(COL_VB, 2)] = dvb_ref[...].astype(BF16)

        @pl.when(i == 0)
        def _():
            dgq_ref[...] = acc_q
            dgk_ref[...] = acc_k

        @pl.when(i > 0)
        def _():
            dgq_ref[...] += acc_q
            dgk_ref[...] += acc_k

    def rows(width, shift=0):
        return pl.BlockSpec((tm, width), lambda i: (i + shift, 0))

    kv_w = N_KV_A * HEAD_DIM
    q_w = N_HEADS_A * HEAD_DIM
    vec = pl.BlockSpec((1, HEAD_DIM), lambda i: (0, 0))
    res, c_res = _call(
        body, name="qk_bwd", grid=(s // tm,),
        in_specs=[rows(q_w), rows(kv_w), rows(kv_w), rows(q_w), rows(kv_w, lo), rows(kv_w, lo), rows(n),
                  vec, vec, rows(HEAD_DIM), rows(HEAD_DIM)],
        out_specs=[rows(n), vec, vec],
        out_shape=[
            jax.ShapeDtypeStruct((s, n), BF16),
            jax.ShapeDtypeStruct((1, HEAD_DIM), F32),
            jax.ShapeDtypeStruct((1, HEAD_DIM), F32),
        ],
        sem=("arbitrary",), args=(dqa, dka, dva, dqb, dkpad, dvpad, proj, g_q, g_k, cos_t, sin_t), comm=comm,
        after=after)
    return res if comm is None else (res, c_res)


_NT = (((1,), (1,)), ((), ()))
_TN = (((0,), (0,)), ((), ()))


def _attn_a_fwd(pb, *, tq=4096, sub=256, comm=None, after=None):
    s = pb.shape[0]
    tq = min(tq, s)
    sub = min(sub, tq)

    def body(q_ref, k_ref, v_ref, o_ref, lse_ref):
        k = k_ref[...]
        v = v_ref[...]
        for r in range(tq // sub):
            rows = pl.ds(r * sub, sub)
            sc = lax.dot_general(q_ref[rows, :], k, _NT, preferred_element_type=F32)
            m = jnp.max(sc, axis=-1, keepdims=True)
            p = jnp.exp2(sc - m)
            l = jnp.sum(p, axis=-1, keepdims=True)
            o = jnp.dot(p.astype(BF16), v, preferred_element_type=F32)
            o_ref[rows, :] = (o / l).astype(BF16)
            lse_ref[rows, :] = jnp.broadcast_to(m + jnp.log2(l), (sub, HEAD_DIM))

    res, c_res = _call(
        body, name="attn_a_fwd", grid=(N_HEADS_A, s // tq),
        in_specs=[
            pl.BlockSpec((tq, HEAD_DIM), lambda h, i: (i, COL_QA + h)),
            pl.BlockSpec((s, HEAD_DIM), lambda h, i: (0, COL_KA + h // GROUP)),
            pl.BlockSpec((s, HEAD_DIM), lambda h, i: (0, COL_VA + h // GROUP)),
        ],
        out_specs=[
            pl.BlockSpec((tq, HEAD_DIM), lambda h, i: (i, h)),
            pl.BlockSpec((None, tq, HEAD_DIM), lambda h, i: (h, i, 0)),
        ],
        out_shape=[
            jax.ShapeDtypeStruct((s, (N_HEADS_A + N_HEADS_B) * HEAD_DIM), BF16),
            jax.ShapeDtypeStruct((N_HEADS_A, s, HEAD_DIM), F32),
        ],
        sem=("parallel", "parallel"), args=(pb, pb, pb), comm=comm, after=after)
    return res if comm is None else (res, c_res)


def _attn_a_bwd(pb, att, datt, lse, *, tq=1024, sub=256, comm=None):
    s = pb.shape[0]
    tq = min(tq, s)
    sub = min(sub, tq)

    def body(q_ref, k_ref, v_ref, o_ref, do_ref, lse_ref, dq_ref, dk_ref, dv_ref):
        first = jnp.logical_and(pl.program_id(1) == 0, pl.program_id(2) == 0)
        k = k_ref[...]
        v = v_ref[...]
        dk = dv = None
        for r in range(tq // sub):
            rows = pl.ds(r * sub, sub)
            q = q_ref[rows, :]
            do = do_ref[rows, :]
            sc = lax.dot_general(q, k, _NT, preferred_element_type=F32)
            p = jnp.exp2(sc - lse_ref[rows, :][:, :1])
            dp = lax.dot_general(do, v, _NT, preferred_element_type=F32)
            delta = jnp.sum(do.astype(F32) * o_ref[rows, :].astype(F32), axis=-1, keepdims=True)
            ds = (p * (dp - delta)).astype(BF16)
            dq_ref[rows, :] = jnp.dot(ds, k, preferred_element_type=F32)
            dk_r = lax.dot_general(ds, q, _TN, preferred_element_type=F32)
            dv_r = lax.dot_general(p.astype(BF16), do, _TN, preferred_element_type=F32)
            dk = dk_r if dk is None else dk + dk_r
            dv = dv_r if dv is None else dv + dv_r

        @pl.when(first)
        def _():
            dk_ref[...] = dk
            dv_ref[...] = dv

        @pl.when(jnp.logical_not(first))
        def _():
            dk_ref[...] += dk
            dv_ref[...] += dv

    qmap = lambda kv, g, i: (i, kv * GROUP + g)
    res, c_res = _call(
        body, name="attn_a_bwd", grid=(N_KV_A, GROUP, s // tq),
        in_specs=[
            pl.BlockSpec((tq, HEAD_DIM), lambda kv, g, i: (i, COL_QA + kv * GROUP + g)),
            pl.BlockSpec((s, HEAD_DIM), lambda kv, g, i: (0, COL_KA + kv)),
            pl.BlockSpec((s, HEAD_DIM), lambda kv, g, i: (0, COL_VA + kv)),
            pl.BlockSpec((tq, HEAD_DIM), qmap),
            pl.BlockSpec((tq, HEAD_DIM), qmap),
            pl.BlockSpec((None, tq, HEAD_DIM), lambda kv, g, i: (kv * GROUP + g, i, 0)),
        ],
        out_specs=[
            pl.BlockSpec((tq, HEAD_DIM), qmap),
            pl.BlockSpec((s, HEAD_DIM), lambda kv, g, i: (0, kv)),
            pl.BlockSpec((s, HEAD_DIM), lambda kv, g, i: (0, kv)),
        ],
        out_shape=[
            jax.ShapeDtypeStruct((s, N_HEADS_A * HEAD_DIM), F32),
            jax.ShapeDtypeStruct((s, N_KV_A * HEAD_DIM), F32),
            jax.ShapeDtypeStruct((s, N_KV_A * HEAD_DIM), F32),
        ],
        sem=("arbitrary", "arbitrary", "arbitrary"), args=(pb, pb, pb, att, datt, lse), comm=comm)
    return res if comm is None else (res, c_res)


def _t5_bucket(rel):
    nb = N_BUCKETS // 2
    ret = jnp.where(rel > 0, nb, 0)
    n = jnp.abs(rel)
    max_exact = nb // 2
    nf = jnp.maximum(n, 1).astype(F32)
    large = max_exact + (jnp.log(nf / max_exact) / math.log(MAX_DISTANCE / max_exact)
                         * (nb - max_exact)).astype(jnp.int32)
    large = jnp.minimum(large, nb - 1)
    return ret + jnp.where(n < max_exact, n, large)


def _band_buckets():
    r = jnp.arange(BLOCK_Q, dtype=jnp.int32)
    j = jnp.arange(3 * BLOCK_Q, dtype=jnp.int32)
    return _t5_bucket((j[None, :] - BLOCK_Q) - r[:, None])


def _band_bias(bucket, table_ref, h):
    acc = jnp.zeros(bucket.shape, F32)
    for b in range(N_BUCKETS):
        acc = jnp.where(bucket == b, table_ref[b, h], acc)
    return acc


GQ = GROUP * BLOCK_Q


def _stack_heads(x):
    return jnp.concatenate([x[:, _cols(g)] for g in range(GROUP)], axis=0)


def _unstack_heads(x):
    return jnp.concatenate([x[g * BLOCK_Q:(g + 1) * BLOCK_Q] for g in range(GROUP)], axis=1)


def _group_bias(bucket, table_ref, kv):
    r = lax.broadcasted_iota(jnp.int32, (BLOCK_Q, 3 * BLOCK_Q), 0)
    j = lax.broadcasted_iota(jnp.int32, (BLOCK_Q, 3 * BLOCK_Q), 1)
    inside = jnp.abs(j - BLOCK_Q - r) <= WINDOW
    return jnp.concatenate([jnp.where(inside, _band_bias(bucket, table_ref, kv * GROUP + g) * LOG2E, NEG_INF)
                            for g in range(GROUP)], axis=0)


def _group_sink(sink_ref, kv):
    head = lax.broadcasted_iota(jnp.int32, (GQ, 1), 0) // BLOCK_Q
    snk = jnp.zeros((GQ, 1), F32)
    for g in range(GROUP):
        snk = jnp.where(head == g, sink_ref[0, kv * GROUP + g] * LOG2E, snk)
    return snk


def _band_mask(n, s):
    kabs = n * BLOCK_Q + lax.broadcasted_iota(jnp.int32, (1, 3 * BLOCK_Q), 1) - BLOCK_Q
    return (kabs >= 0) & (kabs < s)


def _band_start(n):
    return pl.multiple_of(n * BLOCK_Q + (PAD_LO - BLOCK_Q), BLOCK_Q)


def _attn_b_fwd(pb, kpad, vpad, bucket, table, sink, att, *, comm=None, after=None):
    s = pb.shape[0]
    nblk = s // BLOCK_Q
    sp = kpad.shape[0]

    def body(table_ref, sink_ref, q0_ref, q1_ref, k_ref, v_ref, bucket_ref, _, o_ref, lse_ref, bias_ref):
        n = pl.program_id(0)

        @pl.when(n == 0)
        def _():
            for kv in range(N_KV_B):
                bias_ref[kv * GQ:(kv + 1) * GQ, :] = _group_bias(bucket_ref[...], table_ref, kv)

        band = pl.ds(_band_start(n), 3 * BLOCK_Q)
        mask = _band_mask(n, s)
        for kv, q_ref in enumerate((q0_ref, q1_ref)):
            kb = k_ref[band, _cols(kv)]
            vb = v_ref[band, _cols(kv)]
            sc = lax.dot_general(_stack_heads(q_ref[...]), kb, _NT, preferred_element_type=F32)
            sc = jnp.where(mask, sc + bias_ref[kv * GQ:(kv + 1) * GQ, :], NEG_INF)
            snk = _group_sink(sink_ref, kv)
            m = jnp.maximum(jnp.max(sc, axis=-1, keepdims=True), snk)
            p = jnp.exp2(sc - m)
            l = jnp.sum(p, axis=-1, keepdims=True) + jnp.exp2(snk - m)
            o = jnp.dot(p.astype(BF16), vb, preferred_element_type=F32)
            o_ref[:, _cols(kv * GROUP, GROUP)] = _unstack_heads((o / l).astype(BF16))
            lse = m + jnp.log2(l)
            for g in range(GROUP):
                lse_ref[kv * GROUP + g] = jnp.broadcast_to(lse[g * BLOCK_Q:(g + 1) * BLOCK_Q], (BLOCK_Q, HEAD_DIM))

    smem = pl.BlockSpec(memory_space=pltpu.SMEM)
    wide = GROUP * HEAD_DIM
    whole = pl.BlockSpec((sp, N_KV_B * HEAD_DIM), lambda n: (0, 0))
    res, c_res = _call(
        body, name="attn_b_fwd", grid=(nblk,),
        in_specs=[
            smem,
            smem,
            pl.BlockSpec((BLOCK_Q, wide), lambda n: (n, COL_QB // GROUP)),
            pl.BlockSpec((BLOCK_Q, wide), lambda n: (n, COL_QB // GROUP + 1)),
            whole,
            whole,
            pl.BlockSpec((BLOCK_Q, 3 * BLOCK_Q), lambda n: (0, 0)),
            _ANY,
        ],
        out_specs=[
            pl.BlockSpec((BLOCK_Q, N_HEADS_B * HEAD_DIM), lambda n: (n, 1)),
            pl.BlockSpec((N_HEADS_B, BLOCK_Q, HEAD_DIM), lambda n: (0, n, 0)),
        ],
        out_shape=[
            jax.ShapeDtypeStruct(att.shape, BF16),
            jax.ShapeDtypeStruct((N_HEADS_B, s, HEAD_DIM), F32),
        ],
        scratch_shapes=[pltpu.VMEM((N_KV_B * GQ, 3 * BLOCK_Q), F32)],
        sem=("arbitrary",), args=(table, sink, pb, pb, kpad, vpad, bucket, att), comm=comm, after=after,
        aliases={7: 0})
    return res if comm is None else (res, c_res)


def _attn_b_bwd(pb, kpad, vpad, att, datt, lse, bucket, table, sink, *, comm=None, after=None):
    s = pb.shape[0]
    nblk = s // BLOCK_Q
    sp = kpad.shape[0]

    def body(table_ref, sink_ref, q0_ref, q1_ref, k_ref, v_ref, o_ref, do_ref, lse_ref, bucket_ref,
             dq_ref, dk_ref, dv_ref, dtab_ref, dsink_ref, bias_ref, dbias_ref):
        n = pl.program_id(0)

        @pl.when(n == 0)
        def _():
            dk_ref[...] = jnp.zeros_like(dk_ref)
            dv_ref[...] = jnp.zeros_like(dv_ref)
            dbias_ref[...] = jnp.zeros_like(dbias_ref)
            dsink_ref[...] = jnp.zeros_like(dsink_ref)
            for kv in range(N_KV_B):
                bias_ref[kv * GQ:(kv + 1) * GQ, :] = _group_bias(bucket_ref[...], table_ref, kv)

        band = pl.ds(_band_start(n), 3 * BLOCK_Q)
        mask = _band_mask(n, s)
        for kv, q_ref in enumerate((q0_ref, q1_ref)):
            wide_cols = _cols(kv * GROUP, GROUP)
            q = _stack_heads(q_ref[...])
            do = _stack_heads(do_ref[:, wide_cols])
            o = _stack_heads(o_ref[:, wide_cols])
            kb = k_ref[band, _cols(kv)]
            vb = v_ref[band, _cols(kv)]
            lse = jnp.concatenate([lse_ref[kv * GROUP + g][:, :1] for g in range(GROUP)], axis=0)
            sc = lax.dot_general(q, kb, _NT, preferred_element_type=F32)
            sc = jnp.where(mask, sc + bias_ref[kv * GQ:(kv + 1) * GQ, :], NEG_INF)
            p = jnp.exp2(sc - lse)
            dp = lax.dot_general(do, vb, _NT, preferred_element_type=F32)
            delta = jnp.sum(do.astype(F32) * o.astype(F32), axis=-1, keepdims=True)
            ds = p * (dp - delta)
            dsb = ds.astype(BF16)
            dq_ref[:, wide_cols] = _unstack_heads(jnp.dot(dsb, kb, preferred_element_type=F32))
            dk_ref[band, _cols(kv)] += lax.dot_general(dsb, q, _TN, preferred_element_type=F32)
            dv_ref[band, _cols(kv)] += lax.dot_general(p.astype(BF16), do, _TN, preferred_element_type=F32)
            dbias_ref[kv * GQ:(kv + 1) * GQ, :] += ds
            sink_part = -jnp.exp2(_group_sink(sink_ref, kv) - lse) * delta
            for g in range(GROUP):
                rows = slice(g * BLOCK_Q, (g + 1) * BLOCK_Q)
                dsink_ref[kv * GROUP + g] += jnp.broadcast_to(
                    jnp.sum(sink_part[rows], axis=0, keepdims=True), (1, HEAD_DIM))

        @pl.when(n == nblk - 1)
        def _():
            bucket_v = bucket_ref[...]
            row = lax.broadcasted_iota(jnp.int32, (N_BUCKETS, HEAD_DIM), 0)
            for h in range(N_HEADS_B):
                acc = dbias_ref[h * BLOCK_Q:(h + 1) * BLOCK_Q, :]
                tot = jnp.zeros((N_BUCKETS, HEAD_DIM), F32)
                for b in range(N_BUCKETS):
                    tot = jnp.where(row == b, jnp.sum(jnp.where(bucket_v == b, acc, 0.0), keepdims=True), tot)
                dtab_ref[h] = tot

    smem = pl.BlockSpec(memory_space=pltpu.SMEM)
    wide = GROUP * HEAD_DIM
    whole = pl.BlockSpec((sp, N_KV_B * HEAD_DIM), lambda n: (0, 0))
    group_b = pl.BlockSpec((BLOCK_Q, N_HEADS_B * HEAD_DIM), lambda n: (n, 1))
    res, c_res = _call(
        body, name="attn_b_bwd", grid=(nblk,),
        in_specs=[
            smem,
            smem,
            pl.BlockSpec((BLOCK_Q, wide), lambda n: (n, COL_QB // GROUP)),
            pl.BlockSpec((BLOCK_Q, wide), lambda n: (n, COL_QB // GROUP + 1)),
            whole,
            whole,
            group_b,
            group_b,
            pl.BlockSpec((N_HEADS_B, BLOCK_Q, HEAD_DIM), lambda n: (0, n, 0)),
            pl.BlockSpec((BLOCK_Q, 3 * BLOCK_Q), lambda n: (0, 0)),
        ],
        out_specs=[
            pl.BlockSpec((BLOCK_Q, N_HEADS_B * HEAD_DIM), lambda n: (n, 0)),
            whole,
            whole,
            pl.BlockSpec((N_HEADS_B, N_BUCKETS, HEAD_DIM), lambda n: (0, 0, 0)),
            pl.BlockSpec((N_HEADS_B, 1, HEAD_DIM), lambda n: (0, 0, 0)),
        ],
        out_shape=[
            jax.ShapeDtypeStruct((s, N_HEADS_B * HEAD_DIM), F32),
            jax.ShapeDtypeStruct((sp, N_KV_B * HEAD_DIM), F32),
            jax.ShapeDtypeStruct((sp, N_KV_B * HEAD_DIM), F32),
            jax.ShapeDtypeStruct((N_HEADS_B, N_BUCKETS, HEAD_DIM), F32),
            jax.ShapeDtypeStruct((N_HEADS_B, 1, HEAD_DIM), F32),
        ],
        scratch_shapes=[pltpu.VMEM((N_KV_B * GQ, 3 * BLOCK_Q), F32), pltpu.VMEM((N_KV_B * GQ, 3 * BLOCK_Q), F32)],
        sem=("arbitrary",),
        args=(table, sink, pb, pb, kpad, vpad, att, datt, lse, bucket), comm=comm, after=after)
    return res if comm is None else (res, c_res)


def _other_chips(x, y):
    return [(x, 1 - y), (1 - x, y), (1 - x, 1 - y)]


_HBM = pl.BlockSpec(memory_space=pltpu.HBM)
_SEM = pl.BlockSpec(memory_space=pltpu.SEMAPHORE)
_SPLIT = pltpu.CompilerParams(has_side_effects=pltpu.SideEffectType.DATAFLOW_SIDE_EFFECTING)


def _in_hbm(a):
    return pltpu.with_memory_space_constraint(a, pltpu.HBM)


def _my_half(rows):
    c = lax.axis_index("c")
    half = rows // 2
    return pl.ds(pl.multiple_of(c * half, half), half), pl.ds(pl.multiple_of((1 - c) * half, half), half)


def _gather_route(shapes):
    def route(src, land):
        x, y, c = lax.axis_index("x"), lax.axis_index("y"), lax.axis_index("c")
        out = []
        for t, shape in enumerate(shapes):
            mine, _ = _my_half(shape[0])
            for px, py in _other_chips(x, y):
                out.append((src[t].at[mine], land[t].at[2 * x + y, mine], land[t].at[2 * px + py, mine], (px, py, c)))
        return out

    return route


def _exchange_route(n_t):
    def route(src, land):
        x, y, c = lax.axis_index("x"), lax.axis_index("y"), lax.axis_index("c")
        out = []
        for t in range(n_t):
            for px, py in _other_chips(x, y):
                k = 2 * px + py
                out.append((src[t].at[k], land[t].at[2 * (2 * x + y) + c], land[t].at[2 * k + c], (px, py, c)))
        return out

    return route


def _own_slot(shape, dtype, slot, block):
    return lax.dynamic_update_slice(lax.empty(shape, dtype), block[None], (slot,) + (0,) * (len(shape) - 1))


def _split_start(name, srcs, lands, route, after):
    n = len(srcs)

    def body(*refs):
        src, land, send_sems, recv_sems, token = refs[:n], refs[n:2 * n], refs[2 * n + 1], refs[2 * n + 2], refs[-1]
        for i, (src_ref, dst_ref, _, to) in enumerate(route(src, land)):
            pltpu.make_async_remote_copy(src_ref=src_ref, dst_ref=dst_ref, send_sem=send_sems.at[i],
                                         recv_sem=recv_sems.at[i], device_id=to, device_id_type=_MESH).start()
        token[...] = jnp.zeros_like(token)

    sem = pltpu.SemaphoreType.DMA((3 * n,))
    lands = list(lands)
    res = pl.pallas_call(
        body, name=name,
        in_specs=[_HBM] * (2 * n) + [_ANY],
        out_specs=[_SEM, _SEM] + [_HBM] * (2 * n) + [pl.BlockSpec(memory_space=pltpu.VMEM)],
        out_shape=[sem, sem] + [pltpu.HBM(a.shape, a.dtype) for a in list(srcs) + lands]
        + [jax.ShapeDtypeStruct((8, 128), F32)],
        input_output_aliases={i: 2 + i for i in range(2 * n)},
        compiler_params=_SPLIT,
    )(*[_in_hbm(a) for a in srcs], *[_in_hbm(a) for a in lands], after)
    return (res[0], res[1]), res[2:2 + n], res[2 + n:2 + 2 * n], res[-1]


def _split_wait(name, srcs, lands, sems, route, after):
    n = len(srcs)

    def body(*refs):
        src, land, send_sems, recv_sems = refs[:n], refs[n:2 * n], refs[2 * n], refs[2 * n + 1]
        for i, (src_ref, _, dst_ref, to) in enumerate(route(src, land)):
            cp = pltpu.make_async_remote_copy(src_ref=src_ref, dst_ref=dst_ref, send_sem=send_sems.at[i],
                                              recv_sem=recv_sems.at[i], device_id=to, device_id_type=_MESH)
            cp.wait_send()
            cp.wait_recv()

    res = pl.pallas_call(
        body, name=name,
        in_specs=[_HBM] * (2 * n) + [_SEM, _SEM, _ANY],
        out_specs=[_HBM] * (2 * n),
        out_shape=[pltpu.HBM(a.shape, a.dtype) for a in list(srcs) + list(lands)],
        input_output_aliases={i: i for i in range(2 * n)},
        compiler_params=_SPLIT,
    )(*srcs, *lands, sems[0], sems[1], after)
    return res[:n], res[n:]


def _comm_only(name, comm):
    return _call(lambda: None, name=name, grid=(1,), in_specs=[], out_specs=[], out_shape=[], args=(), comm=comm)[1]


def _swap_comm(shards, lands):
    n_t = len(lands)

    def copies(land, sems, later):
        send_sems, recv_sems = sems
        x, y = lax.axis_index("x"), lax.axis_index("y")
        sends, recvs = [], []
        for t in range(n_t):
            mine, other = _my_half(shards[t].shape[0])
            for j, (px, py) in enumerate(_other_chips(x, y)):
                k = 2 * px + py
                for part, out in ((mine, sends), (other, recvs)) if later else ((mine, sends),):
                    out.append(pltpu.make_async_remote_copy(
                        src_ref=land[t].at[k, part], dst_ref=land[t].at[k, part], send_sem=send_sems.at[3 * t + j],
                        recv_sem=recv_sems.at[3 * t + j], device_id=_sibling(), device_id_type=_MESH))
        return sends, recvs

    def start(ins, land, sems):
        for cp in copies(land, sems, False)[0]:
            cp.start()

    def finish(ins, land, sems):
        sends, recvs = copies(land, sems, True)
        for cp in recvs:
            cp.wait_recv()
        for cp in sends:
            cp.wait_send()

    return _Comm(
        lands, [jax.ShapeDtypeStruct(a.shape, a.dtype) for a in lands],
        [pltpu.SemaphoreType.DMA((3 * n_t,)), pltpu.SemaphoreType.DMA((3 * n_t,))],
        start, finish, aliases={t: t for t in range(n_t)})


def _forward_comm(partials, lands):
    n_t = len(lands)

    def copies(land, sems, later):
        send_sems, recv_sems = sems
        x, y, c = lax.axis_index("x"), lax.axis_index("y"), lax.axis_index("c")
        sends, recvs = [], []
        for t in range(n_t):
            for j, k in enumerate([2 * x + y] + [2 * px + py for px, py in _other_chips(x, y)]):
                for slot, out in ((2 * k + c, sends), (2 * k + 1 - c, recvs)) if later else ((2 * k + c, sends),):
                    out.append(pltpu.make_async_remote_copy(
                        src_ref=land[t].at[slot], dst_ref=land[t].at[slot], send_sem=send_sems.at[4 * t + j],
                        recv_sem=recv_sems.at[4 * t + j], device_id=_sibling(), device_id_type=_MESH))
        return sends, recvs

    def start(ins, land, sems):
        for cp in copies(land, sems, False)[0]:
            cp.start()

    def finish(ins, land, sems):
        sends, recvs = copies(land, sems, True)
        for cp in recvs:
            cp.wait_recv()
        for cp in sends:
            cp.wait_send()

    return _Comm(
        lands, [jax.ShapeDtypeStruct(a.shape, a.dtype) for a in lands],
        [pltpu.SemaphoreType.DMA((4 * n_t,)), pltpu.SemaphoreType.DMA((4 * n_t,))],
        start, finish, aliases={t: t for t in range(n_t)})


def _allreduce_small(pack):
    rows, d = pack.shape

    def body(p_ref, sum_ref, all_ref, send_sems, recv_sems):
        x, y, c = lax.axis_index("x"), lax.axis_index("y"), lax.axis_index("c")
        me = 4 * x + 2 * y + c
        all_ref[me] = p_ref[...]
        peers = []
        for dx in range(2):
            for dy in range(2):
                for dc in range(2):
                    if dx or dy or dc:
                        px = 1 - x if dx else x
                        py = 1 - y if dy else y
                        pc = 1 - c if dc else c
                        peers.append((4 * dx + 2 * dy + dc - 1, (px, py, pc)))
        sends = []
        for k, to in peers:
            cp = pltpu.make_async_remote_copy(
                src_ref=p_ref, dst_ref=all_ref.at[me], send_sem=send_sems.at[k], recv_sem=recv_sems.at[k],
                device_id=to, device_id_type=_MESH)
            cp.start()
            sends.append(cp)
        for k, (px, py, pc) in peers:
            pltpu.make_async_remote_copy(
                src_ref=p_ref, dst_ref=all_ref.at[4 * px + 2 * py + pc], send_sem=send_sems.at[k],
                recv_sem=recv_sems.at[k], device_id=(px, py, pc), device_id_type=_MESH).wait_recv()
        for cp in sends:
            cp.wait_send()
        tot = all_ref[0]
        for i in range(1, N_DEV):
            tot = tot + all_ref[i]
        sum_ref[...] = tot

    vm = pl.BlockSpec(memory_space=pltpu.VMEM)
    return pl.pallas_call(
        body,
        name="allreduce_small",
        in_specs=[vm],
        out_specs=vm,
        out_shape=jax.ShapeDtypeStruct((rows, d), F32),
        scratch_shapes=[
            pltpu.VMEM((N_DEV, rows, d), F32),
            pltpu.SemaphoreType.DMA((N_DEV - 1,)),
            pltpu.SemaphoreType.DMA((N_DEV - 1,)),
        ],
    )(pack)


def _adamw_math(w, g, m, v):
    m = ADAM_B1 * m + (1.0 - ADAM_B1) * g
    v = ADAM_B2 * v + (1.0 - ADAM_B2) * (g * g)
    m_hat = m / (1.0 - ADAM_B1 ** ADAM_STEP)
    v_hat = v / (1.0 - ADAM_B2 ** ADAM_STEP)
    delta = -ADAM_LR * (m_hat / (jnp.sqrt(v_hat) + ADAM_EPS) + ADAM_WD * w)
    return delta, m, v


def _sum_adamw(parts, w, m, v, *, name, tr=256):
    r, c = w.shape
    tr = min(tr, r)
    tc = min(c, 1024)

    def body(p_ref, w_ref, m_ref, v_ref, g_ref, d_ref, m2_ref, v2_ref):
        g = p_ref[0].astype(F32)
        for i in range(1, N_DEV):
            g = g + p_ref[i].astype(F32)
        delta, m2, v2 = _adamw_math(w_ref[...], g, m_ref[...], v_ref[...])
        g_ref[...] = g
        d_ref[...] = delta
        m2_ref[...] = m2
        v2_ref[...] = v2

    blk = pl.BlockSpec((tr, tc), lambda i, j: (i, j))
    return pl.pallas_call(
        body,
        name=name,
        grid=(r // tr, c // tc),
        in_specs=[pl.BlockSpec((N_DEV, tr, tc), lambda i, j: (0, i, j)), blk, blk, blk],
        out_specs=[blk] * 4,
        out_shape=[jax.ShapeDtypeStruct((r, c), F32)] * 4,
        compiler_params=_params(("parallel", "parallel")),
    )(parts, w, m, v)


def _adamw_small(g, w, m, v):
    def body(g_ref, w_ref, m_ref, v_ref, d_ref, m2_ref, v2_ref):
        delta, m2, v2 = _adamw_math(w_ref[...], g_ref[...], m_ref[...], v_ref[...])
        d_ref[...] = delta
        m2_ref[...] = m2
        v2_ref[...] = v2

    vm = pl.BlockSpec(memory_space=pltpu.VMEM)
    return pl.pallas_call(
        body,
        name="adamw_small",
        in_specs=[vm] * 4,
        out_specs=[vm] * 3,
        out_shape=[jax.ShapeDtypeStruct(g.shape, F32)] * 3,
    )(g, w, m, v)


def _relu2_epilogue(acc):
    ra = jnp.maximum(acc, 0.0)
    return ra * ra, ra


def _residual_norm_epilogue(acc, res, g):
    h = acc + res
    return h, h * lax.rsqrt(jnp.mean(h * h, axis=-1, keepdims=True) + EPS) * g


def _rows(stacked):
    return stacked.reshape(stacked.shape[0] * stacked.shape[1], stacked.shape[2])


def _by_chip(mat):
    return mat.reshape(N_CHIPS, mat.shape[0] // N_CHIPS, mat.shape[1])


def _local_step(x, p, target, shards, small, update):
    s, d = x.shape
    cos_t, sin_t = _rope_tables(s)
    bucket = _band_buckets()
    p_bf = p.astype(BF16)
    wts = {}

    chip = 2 * lax.axis_index("x") + lax.axis_index("y")
    core = lax.axis_index("c")

    def gather(tag, names, after):
        srcs = [cast[n] for n in names]
        route = _gather_route([a.shape for a in srcs])
        sems, srcs, lands, token = _split_start(f"gather_start_{tag}", srcs, [zones[n] for n in names], route, after)

        def landed(done):
            got_srcs, got_lands = _split_wait(f"gather_wait_{tag}", srcs, lands, sems, route, done)
            comm = _swap_comm(got_srcs, got_lands)
            comm.waited = got_srcs[0]
            return comm

        return landed, token

    def prepare(n, zero):
        cast[n] = (shards[n] + zero).astype(BF16)
        zones[n] = _own_slot((N_CHIPS,) + cast[n].shape, BF16, chip, cast[n])

    cast, zones = {}, {}
    prepare("w_in", 0.0)
    in_landed, token = gather("in", ["w_in"], small["attn_norm_g"])
    for n in shards:
        if n != "w_in":
            prepare(n, token[:1, :1])
    g_attn = small["attn_norm_g"] + token[:1, :1]
    u = _rms_fwd(x, g_attn, name="norm_attn")
    prepared = u[:1, :1].astype(F32) + sum(
        (lax.dynamic_slice(zones[n], (chip, 0, 0), (1, 1, 1))[0] + cast[n][:1, :1]).astype(F32)
        for n in zones if n != "w_in")
    (wts["w_in"],) = _comm_only("swap_w_in", in_landed(prepared))
    mid_landed, token = gather("mid", ["w_out"], wts["w_in"])
    proj = _matmul(u, wts["w_in"], mode="nn", out_dtypes=[F32], name="mm_in", bn=768, after=token)
    pb, (w_out_s,) = _qk_prep(proj, small["q_norm_g"], small["k_norm_g"], cos_t, sin_t, comm=mid_landed(proj))
    wts["w_out"] = _rows(w_out_s)
    up_landed, token = gather("up", ["w_up"], pb)
    att_a, lse_a = _attn_a_fwd(pb, after=token)
    pad = ((PAD_LO, PAD_HI), (0, 0))
    kpad = jnp.pad(pb[:, COL_KB * HEAD_DIM:COL_VB * HEAD_DIM], pad)
    vpad = jnp.pad(pb[:, COL_VB * HEAD_DIM:], pad)
    up_swap = up_landed(att_a)
    down_landed, token = gather("down", ["w_down"], up_swap.waited)
    (att, lse_b), (wts["w_up"],) = _attn_b_fwd(pb, kpad, vpad, bucket, small["rel_bias_table"],
                                               small["sink_logits"], att_a, comm=up_swap, after=token)
    h1, mn = _matmul(att, wts["w_out"], mode="nn", out_dtypes=[F32, BF16], name="mm_out", bm=512, bn=d,
                     epilogue=_residual_norm_epilogue, extras=(x,), vecs=(small["mlp_norm_g"],))
    r, ra = _matmul(mn, wts["w_up"], mode="nn", out_dtypes=[BF16, BF16], name="mm_up", epilogue=_relu2_epilogue,
                    bm=2048)
    (w_down_s,) = _comm_only("swap_w_down", down_landed(r))
    wts["w_down"] = _rows(w_down_s)
    late_landed, token = gather("late", ["w_gate", "ple_w"], w_down_s)
    h2 = _matmul(r, wts["w_down"], mode="nn", out_dtypes=[F32], name="mm_down",
                 epilogue=lambda acc, res: (acc + res,), extras=(h1,), after=token)
    ng, (w_gate_s, wts["ple_w"]) = _rms_fwd(h2, small["gate_norm_g"], name="norm_gate", comm=late_landed(h2))
    wts["w_gate"] = _rows(w_gate_s)
    gate = _matmul(ng, wts["w_gate"], mode="nn", out_dtypes=[F32], name="mm_gate",
                   epilogue=lambda acc: (1.0 / (1.0 + jnp.exp(-acc)),))
    pp = _matmul(p_bf, wts["ple_w"], mode="nn", out_dtypes=[F32], name="mm_ple", bn=512)
    dh3, dz, dpp, dg_final, dg_ple, loss = _tail(h2, gate, pp, target, small["ple_norm_g"], small["final_norm_g"])

    dng = _matmul(dz, wts["w_gate"], mode="nt", out_dtypes=[F32], name="mm_gate_dx")
    gw_gate = _matmul(ng, dz, mode="tn", out_dtypes=[BF16], name="mm_gate_dw", bm=512, bk=4096)
    gw_ple = _matmul(p_bf, dpp, mode="tn", out_dtypes=[BF16], name="mm_ple_dw", bn=512, out_stack=N_CHIPS)
    dh2, dh2_bf, dg_gate = _rms_bwd(h2, dng, small["gate_norm_g"], dh3, name="norm_gate_bwd", want_bf16=True)

    def exchange(tag, partials, after):
        route = _exchange_route(len(partials))
        lands = [_own_slot((N_DEV,) + g.shape[1:], g.dtype, 2 * chip + core,
                           lax.dynamic_index_in_dim(g, chip, 0, keepdims=False)) for g in partials]
        sems, srcs, lands, token = _split_start(f"exchange_start_{tag}", partials, lands, route, after)

        def landed(done):
            got_srcs, got_lands = _split_wait(f"exchange_wait_{tag}", srcs, lands, sems, route, done)
            comm = _forward_comm(got_srcs, got_lands)
            comm.waited = got_srcs[0]
            return comm

        return landed, token

    big = {}
    gate_landed, token = exchange("gate", [_by_chip(gw_gate), gw_ple], dh2_bf)
    gw_down = _matmul(r, dh2_bf, mode="tn", out_dtypes=[BF16], name="mm_down_dw", after=token, bm=512, bk=4096)
    da, (parts_gate, parts_ple) = _matmul(
        dh2_bf, wts["w_down"], mode="nt", out_dtypes=[BF16], name="mm_down_dx", bm=2048,
        epilogue=lambda acc, ra_v: (acc * (2.0 * ra_v.astype(F32)),), extras=(ra,), comm=gate_landed(gw_down))
    down_landed, token = exchange("down", [_by_chip(gw_down)], da)
    big["w_gate"], big["ple_w"] = update("w_gate", parts_gate), update("ple_w", parts_ple)
    gw_up = _matmul(mn, da, mode="tn", out_dtypes=[BF16], name="mm_up_dw", out_stack=N_CHIPS, after=token,
                    bm=512, bk=4096)
    dmn = _matmul(da, wts["w_up"], mode="nt", out_dtypes=[F32], name="mm_up_dx", bk=4096)
    dh1, dh1_bf, dg_mlp = _rms_bwd(h1, dmn, small["mlp_norm_g"], dh2, name="norm_mlp_bwd", want_bf16=True)
    datt, (parts_down,) = _matmul(dh1_bf, wts["w_out"], mode="nt", out_dtypes=[BF16], name="mm_out_dx",
                                  comm=down_landed(dh1_bf))
    gw_out = _matmul(att, dh1_bf, mode="tn", out_dtypes=[BF16], name="mm_out_dw", bm=512, bk=4096)
    up_landed, token = exchange("up", [gw_up], datt)
    dqb, dkpad, dvpad, dtab, dsink = _attn_b_bwd(pb, kpad, vpad, att, datt, lse_b, bucket,
                                                 small["rel_bias_table"], small["sink_logits"], after=token)
    dqa, dka, dva = _attn_a_bwd(pb, att, datt, lse_a)
    up_forward = up_landed(dqa)
    out_landed, token = exchange("out", [_by_chip(gw_out)], up_forward.waited)
    (dproj, dg_q, dg_k), (parts_up,) = _qk_bwd(dqa, dka, dva, dqb, dkpad, dvpad, proj,
                                               small["q_norm_g"], small["k_norm_g"], cos_t, sin_t,
                                               comm=up_forward, after=token)
    gw_in = _matmul(u, dproj, mode="tn", out_dtypes=[BF16], name="mm_in_dw", bn=768, out_stack=N_CHIPS,
                    bm=512, bk=4096)
    out_forward = out_landed(gw_in)
    in_landed, token = exchange("in", [gw_in], out_forward.waited)
    du, (parts_out,) = _matmul(dproj, wts["w_in"], mode="nt", out_dtypes=[F32], name="mm_in_dx", bk=3072,
                               comm=out_forward, after=token)
    grad_x, dg_attn = _rms_bwd(x, du, small["attn_norm_g"], dh1, name="norm_attn_bwd", want_bf16=False)
    for n, parts in (("w_down", parts_down), ("w_up", parts_up), ("w_out", parts_out)):
        big[n] = update(n, parts)
    done = dg_attn + sum(big[n][0][0, :1, :] for n in ("w_down", "w_up", "w_out"))
    (parts_in,) = _comm_only("forward_w_in", in_landed(done))
    big["w_in"] = update("w_in", parts_in)

    small_g = {
        "attn_norm_g": dg_attn, "mlp_norm_g": dg_mlp, "ple_norm_g": dg_ple, "gate_norm_g": dg_gate,
        "final_norm_g": dg_final, "q_norm_g": dg_q, "k_norm_g": dg_k,
        "sink_logits": dsink[:, 0, 0][None, :], "rel_bias_table": dtab[:, :, 0].T,
    }
    return loss, grad_x, big, small_g


_SMALL_ROWS = ["attn_norm_g", "mlp_norm_g", "ple_norm_g", "gate_norm_g", "final_norm_g"]
_PACK_ROWS = 8


def _pack_small(vals, d):
    rows = [vals[n].reshape(1, d) for n in _SMALL_ROWS]
    misc = jnp.concatenate([
        vals["q_norm_g"].reshape(1, HEAD_DIM), vals["k_norm_g"].reshape(1, HEAD_DIM),
        jnp.pad(vals["sink_logits"].reshape(1, N_HEADS_B), ((0, 0), (0, HEAD_DIM - N_HEADS_B))),
        vals["rel_bias_table"].reshape(1, N_BUCKETS * N_HEADS_B)], axis=1)
    rows.append(jnp.pad(misc, ((0, 0), (0, d - misc.shape[1]))))
    rows.append(jnp.zeros((_PACK_ROWS - len(rows), d), F32))
    return jnp.concatenate(rows, axis=0).astype(F32)


def _unpack_small(pack, shapes):
    out = {n: pack[i].reshape(shapes[n]) for i, n in enumerate(_SMALL_ROWS)}
    misc = pack[len(_SMALL_ROWS)]
    out["q_norm_g"] = misc[:HEAD_DIM].reshape(shapes["q_norm_g"])
    out["k_norm_g"] = misc[HEAD_DIM:2 * HEAD_DIM].reshape(shapes["k_norm_g"])
    out["sink_logits"] = misc[2 * HEAD_DIM:2 * HEAD_DIM + N_HEADS_B].reshape(shapes["sink_logits"])
    out["rel_bias_table"] = misc[3 * HEAD_DIM:3 * HEAD_DIM + N_BUCKETS * N_HEADS_B].reshape(shapes["rel_bias_table"])
    return out


_WEIGHTS = ["attn_norm_g", "w_in", "q_norm_g", "k_norm_g", "sink_logits", "w_out", "mlp_norm_g", "w_up", "w_down",
            "ple_w", "ple_norm_g", "gate_norm_g", "w_gate", "rel_bias_table", "final_norm_g"]
_BIG = ["w_in", "w_out", "w_up", "w_down", "ple_w", "w_gate"]


def kernel(x, p, attn_norm_g, w_in, q_norm_g, k_norm_g, sink_logits, w_out, mlp_norm_g, w_up, w_down, ple_w, ple_norm_g, gate_norm_g, w_gate, rel_bias_table, final_norm_g, loss_target, m_attn_norm_g, m_w_in, m_q_norm_g, m_k_norm_g, m_sink_logits, m_w_out, m_mlp_norm_g, m_w_up, m_w_down, m_ple_w, m_ple_norm_g, m_gate_norm_g, m_w_gate, m_rel_bias_table, m_final_norm_g, v_attn_norm_g, v_w_in, v_q_norm_g, v_k_norm_g, v_sink_logits, v_w_out, v_mlp_norm_g, v_w_up, v_w_down, v_ple_w, v_ple_norm_g, v_gate_norm_g, v_w_gate, v_rel_bias_table, v_final_norm_g):
    given = dict(locals())
    w = {n: given[n] for n in _WEIGHTS}
    m = {n: given["m_" + n] for n in _WEIGHTS}
    v = {n: given["v_" + n] for n in _WEIGHTS}
    d = x.shape[-1]

    shards = {n: w[n][0] for n in _BIG}
    small = {
        "attn_norm_g": w["attn_norm_g"], "mlp_norm_g": w["mlp_norm_g"], "ple_norm_g": w["ple_norm_g"],
        "gate_norm_g": w["gate_norm_g"], "final_norm_g": w["final_norm_g"].reshape(1, d),
        "q_norm_g": w["q_norm_g"], "k_norm_g": w["k_norm_g"], "sink_logits": w["sink_logits"],
        "rel_bias_table": w["rel_bias_table"],
    }

    def update(n, parts):
        res = _sum_adamw(parts, w[n][0], m[n][0], v[n][0], name="adamw_" + n)
        return [t.reshape(w[n].shape) for t in res]

    loss_part, grad_x, big, small_g = _local_step(x[0], p[0, 0], loss_target[0], shards, small, update)
    grads, deltas, new_m, new_v = [{n: big[n][i] for n in _BIG} for i in range(4)]

    shapes = {n: w[n].shape for n in _WEIGHTS if n not in _BIG}
    pack = _pack_small(small_g, d)
    pack = pack.at[_PACK_ROWS - 1, :1].add(0.0 * grads["w_in"][0, 0, :1])
    pack = pack.at[_PACK_ROWS - 1, 1].set(loss_part[0, 0])
    g_small = _allreduce_small(pack)
    loss = g_small[_PACK_ROWS - 1, 1]
    d_small, m_small, v_small = _adamw_small(g_small, _pack_small(w, d), _pack_small(m, d), _pack_small(v, d))
    grads.update(_unpack_small(g_small, shapes))
    deltas.update(_unpack_small(d_small, shapes))
    new_m.update(_unpack_small(m_small, shapes))
    new_v.update(_unpack_small(v_small, shapes))

    return (loss, grad_x[None], *[grads[n] for n in _WEIGHTS], *[deltas[n] for n in _WEIGHTS],
            *[new_m[n] for n in _WEIGHTS], *[new_v[n] for n in _WEIGHTS])
```

```python
import functools
import math

import jax
import jax.numpy as jnp
from jax import lax
from jax.experimental import pallas as pl
from jax.experimental.pallas import tpu as pltpu

F32 = jnp.float32
BF16 = jnp.bfloat16

HEAD_DIM = 128
N_HEADS_A = 8
N_KV_A = 2
N_HEADS_B = 8
N_KV_B = 2
GROUP = 4
GRID_W = 64
BLOCK_Q = 128
WINDOW = 128
N_BUCKETS = 32
MAX_DISTANCE = 128
ROPE_THETA = 10000.0
EPS = 1e-6
NEG_INF = -1e30
ATT_SCALE = HEAD_DIM ** -0.5
LOG2E = math.log2(math.e)
LN2 = math.log(2.0)
Q_SCALE = ATT_SCALE * LOG2E
PAD_LO, PAD_HI = 256, 128
ADAM_LR = 0.001
ADAM_B1 = 0.9
ADAM_B2 = 0.999
ADAM_EPS = 1e-08
ADAM_WD = 0.01
ADAM_STEP = 10

N_CHIPS = 4
N_DEV = 8
COL_QA, COL_KA, COL_VA, COL_QB, COL_KB, COL_VB = 0, 8, 10, 12, 20, 22

VMEM_LIMIT = 52 * 1024 * 1024


def _params(sem=None, collective_id=None):
    return pltpu.CompilerParams(dimension_semantics=sem, vmem_limit_bytes=VMEM_LIMIT, collective_id=collective_id)


_ANY = pl.BlockSpec(memory_space=pl.ANY)
_MESH = pl.DeviceIdType.MESH
SIBLING_BARRIER_ID = 1


def _sibling():
    return (lax.axis_index("x"), lax.axis_index("y"), 1 - lax.axis_index("c"))


class _Comm:
    def __init__(self, inputs, out_shapes, sems, start, finish, aliases=None):
        self.inputs, self.out_shapes, self.sems = list(inputs), list(out_shapes), list(sems)
        self.start, self.finish, self.aliases = start, finish, dict(aliases or {})


def _call(body, *, name, grid, in_specs, out_specs, out_shape, args, scratch_shapes=(), sem=None, comm=None,
          after=None, aliases=None):
    in_specs, out_specs, out_shape = list(in_specs), list(out_specs), list(out_shape)
    scratch_shapes = list(scratch_shapes)
    n_in, n_out, n_sc = len(in_specs), len(out_specs), len(scratch_shapes)
    behind = [] if after is None else [after]
    aliases = dict(aliases or {})
    if comm is None:
        res = pl.pallas_call(
            (lambda *refs: body(*refs[:n_in], *refs[n_in + len(behind):])) if behind else body,
            name=name, grid=grid, in_specs=in_specs + [_ANY] * len(behind), out_specs=out_specs,
            out_shape=out_shape, scratch_shapes=scratch_shapes, input_output_aliases=aliases,
            compiler_params=_params(sem))(*args, *behind)
        return list(res), []
    c_in, c_out = len(comm.inputs), len(comm.out_shapes)

    def hosted(*refs):
        pos = [0]

        def take(n):
            pos[0] += n
            return refs[pos[0] - n:pos[0]]

        ins, c_ins, _, outs, c_outs, scr = (take(n_in), take(c_in), take(len(behind)), take(n_out), take(c_out),
                                            take(n_sc))
        c_sems = refs[pos[0]:]
        ids = [pl.program_id(a) for a in range(len(grid))]
        first = functools.reduce(jnp.logical_and, [i == 0 for i in ids])
        last = functools.reduce(jnp.logical_and, [i == g - 1 for i, g in zip(ids, grid)])

        @pl.when(first)
        def _():
            barrier = pltpu.get_barrier_semaphore()
            pl.semaphore_signal(barrier, inc=1, device_id=_sibling(), device_id_type=_MESH)
            pl.semaphore_wait(barrier, 1)
            comm.start(c_ins, c_outs, c_sems)

        body(*ins, *outs, *scr)

        @pl.when(last)
        def _():
            comm.finish(c_ins, c_outs, c_sems)

    res = pl.pallas_call(
        hosted, name=name, grid=grid, in_specs=in_specs + [_ANY] * (c_in + len(behind)),
        out_specs=out_specs + [_ANY] * c_out,
        out_shape=out_shape + comm.out_shapes, scratch_shapes=scratch_shapes + comm.sems,
        input_output_aliases={**aliases, **{n_in + i: n_out + o for i, o in comm.aliases.items()}},
        compiler_params=_params(("arbitrary",) * len(grid), SIBLING_BARRIER_ID))(*args, *comm.inputs, *behind)
    return list(res[:n_out]), list(res[n_out:])


def _matmul(a, b, *, mode, out_dtypes, name, epilogue=None, extras=(), bm=1024, bn=1024, bk=2048,
            out_stack=0, comm=None, after=None, vecs=()):
    stacked = b.ndim == 3
    if mode == "nn":
        m, k = a.shape
        if stacked:
            nj, kb, ns = b.shape
            n, ks = nj * ns, k
        else:
            kb, n = b.shape
            ns, ks = n, k
        dn = (((1,), (0,)), ((), ()))
    elif mode == "nt":
        m, k = a.shape
        if stacked:
            nj, n, ks = b.shape
            kb = nj * ks
        else:
            n, kb = b.shape
            ks = kb
        ns = n
        dn = (((1,), (1,)), ((), ()))
    else:
        k, m = a.shape
        kb, n = b.shape
        ns, ks = n, k
        dn = (((0,), (0,)), ((), ()))
    assert k == kb and not (stacked and mode == "tn")
    ns_out = n // out_stack if out_stack else n
    per_blk = min(bk, k) // ks if stacked and mode == "nt" and bk > ks else 0
    bm, bn, bk = min(bm, m), min(bn, ns, ns_out), per_blk * ks if per_blk else min(bk, ks)
    assert m % bm == 0 and ns % bn == 0 and ns_out % bn == 0 and (k % bk == 0 if per_blk else ks % bk == 0)
    gm, gn, gk = m // bm, n // bn, k // bk

    if mode == "tn":
        a_spec = pl.BlockSpec((bk, bm), lambda i, j, q: (q, i))
    else:
        a_spec = pl.BlockSpec((bm, bk), lambda i, j, q: (i, q))
    if mode == "nt":
        if per_blk:
            b_spec = pl.BlockSpec((per_blk, bn, ks), lambda i, j, q: (q, j, 0))
        elif stacked:
            per = ks // bk
            b_spec = pl.BlockSpec((None, bn, bk), lambda i, j, q: (q // per, j, q % per))
        else:
            b_spec = pl.BlockSpec((bn, bk), lambda i, j, q: (j, q))
    else:
        if stacked:
            per = ns // bn
            b_spec = pl.BlockSpec((None, bk, bn), lambda i, j, q: (j // per, q, j % per))
        else:
            b_spec = pl.BlockSpec((bk, bn), lambda i, j, q: (q, j))
    ex_spec = pl.BlockSpec((bm, bn), lambda i, j, q: (i, j))
    if out_stack:
        per_o = ns_out // bn
        o_spec = pl.BlockSpec((None, bm, bn), lambda i, j, q: (j // per_o, i, j % per_o))
        o_shape = (out_stack, m, ns_out)
    else:
        o_spec = ex_spec
        o_shape = (m, n)
    n_ex, n_out = len(extras) + len(vecs), len(out_dtypes)

    def body(a_ref, b_ref, *rest):
        ex, outs = rest[:n_ex], rest[n_ex:n_ex + n_out]
        if per_blk:
            part = sum(lax.dot_general(a_ref[:, t * ks:(t + 1) * ks], b_ref[t], dn, preferred_element_type=F32)
                       for t in range(per_blk))
        else:
            part = lax.dot_general(a_ref[...], b_ref[...], dn, preferred_element_type=F32)

        def finish(acc):
            res = epilogue(acc, *[e[...] for e in ex]) if epilogue else (acc,)
            for o, r in zip(outs, res):
                o[...] = r.astype(o.dtype)

        if gk == 1:
            finish(part)
        else:
            acc_ref = rest[-1]
            q = pl.program_id(2)

            @pl.when(q == 0)
            def _():
                acc_ref[...] = part

            @pl.when(q > 0)
            def _():
                acc_ref[...] += part

            @pl.when(q == gk - 1)
            def _():
                finish(acc_ref[...])

    res, c_res = _call(
        body, name=name, grid=(gm, gn, gk),
        in_specs=[a_spec, b_spec] + [ex_spec] * len(extras)
        + [pl.BlockSpec((1, bn), lambda i, j, q: (0, j))] * len(vecs),
        out_specs=[o_spec] * n_out,
        out_shape=[jax.ShapeDtypeStruct(o_shape, dt) for dt in out_dtypes],
        scratch_shapes=[pltpu.VMEM((bm, bn), F32)] if gk > 1 else [],
        sem=("parallel", "parallel", "arbitrary"), args=(a, b, *extras, *vecs), comm=comm, after=after)
    res = res[0] if n_out == 1 else res
    return res if comm is None else (res, c_res)


def _rms_fwd(x, g, *, name, tm=256, comm=None):
    s, d = x.shape
    tm = min(tm, s)

    def body(x_ref, g_ref, o_ref):
        xf = x_ref[...]
        r = lax.rsqrt(jnp.mean(xf * xf, axis=-1, keepdims=True) + EPS)
        o_ref[...] = (xf * r * g_ref[...]).astype(o_ref.dtype)

    res, c_res = _call(
        body, name=name, grid=(s // tm,),
        in_specs=[pl.BlockSpec((tm, d), lambda i: (i, 0)), pl.BlockSpec((1, d), lambda i: (0, 0))],
        out_specs=[pl.BlockSpec((tm, d), lambda i: (i, 0))],
        out_shape=[jax.ShapeDtypeStruct((s, d), BF16)],
        sem=("parallel",), args=(x, g), comm=comm)
    return res[0] if comm is None else (res[0], c_res)


def _rms_bwd(x, dy, g, add, *, name, want_bf16, tm=256):
    s, d = x.shape
    tm = min(tm, s)

    def body(x_ref, dy_ref, g_ref, add_ref, dx_ref, *rest):
        dg_ref = rest[-1]
        i = pl.program_id(0)
        xf = x_ref[...]
        dyf = dy_ref[...].astype(F32)
        r = lax.rsqrt(jnp.mean(xf * xf, axis=-1, keepdims=True) + EPS)
        xh = xf * r
        dyg = dyf * g_ref[...]
        dx = r * (dyg - xh * jnp.mean(dyg * xh, axis=-1, keepdims=True))
        tot = add_ref[...] + dx
        dx_ref[...] = tot
        if want_bf16:
            rest[0][...] = tot.astype(BF16)
        part = jnp.sum(dyf * xh, axis=0, keepdims=True)

        @pl.when(i == 0)
        def _():
            dg_ref[...] = part

        @pl.when(i > 0)
        def _():
            dg_ref[...] += part

    row = pl.BlockSpec((tm, d), lambda i: (i, 0))
    vec = pl.BlockSpec((1, d), lambda i: (0, 0))
    out_specs = [row] + ([row] if want_bf16 else []) + [vec]
    out_shape = [jax.ShapeDtypeStruct((s, d), F32)]
    if want_bf16:
        out_shape.append(jax.ShapeDtypeStruct((s, d), BF16))
    out_shape.append(jax.ShapeDtypeStruct((1, d), F32))
    return pl.pallas_call(
        body,
        name=name,
        grid=(s // tm,),
        in_specs=[row, row, vec, row],
        out_specs=out_specs,
        out_shape=out_shape,
        compiler_params=_params(("arbitrary",)),
    )(x, dy, g, add)


def _tail(h2, gate, pp, target, g_ple, g_final, *, tm=256):
    s, d = h2.shape
    tm = min(tm, s)

    def body(h2_ref, gate_ref, pp_ref, t_ref, gp_ref, gf_ref, dh3_ref, dz_ref, dpp_ref, dgf_ref, dgp_ref, loss_ref):
        i = pl.program_id(0)
        ppf = pp_ref[...]
        gate_v = gate_ref[...]
        r_p = lax.rsqrt(jnp.mean(ppf * ppf, axis=-1, keepdims=True) + EPS)
        eh = ppf * r_p
        e = eh * gp_ref[...]
        h3 = h2_ref[...] + gate_v * e
        r_f = lax.rsqrt(jnp.mean(h3 * h3, axis=-1, keepdims=True) + EPS)
        yh = h3 * r_f
        diff = yh * gf_ref[...] - t_ref[...]
        loss_part = 0.5 * jnp.sum(jnp.mean(diff * diff, axis=-1, keepdims=True), axis=0, keepdims=True)
        dy = diff / d
        dgf = jnp.sum(dy * yh, axis=0, keepdims=True)
        dyg = dy * gf_ref[...]
        dh3 = r_f * (dyg - yh * jnp.mean(dyg * yh, axis=-1, keepdims=True))
        dh3_ref[...] = dh3
        de = dh3 * gate_v
        dz_ref[...] = (dh3 * e * gate_v * (1.0 - gate_v)).astype(BF16)
        dgp = jnp.sum(de * eh, axis=0, keepdims=True)
        deg = de * gp_ref[...]
        dpp_ref[...] = (r_p * (deg - eh * jnp.mean(deg * eh, axis=-1, keepdims=True))).astype(BF16)
        loss_row = jnp.broadcast_to(loss_part, (1, 128))

        @pl.when(i == 0)
        def _():
            dgf_ref[...] = dgf
            dgp_ref[...] = dgp
            loss_ref[...] = loss_row

        @pl.when(i > 0)
        def _():
            dgf_ref[...] += dgf
            dgp_ref[...] += dgp
            loss_ref[...] += loss_row

    row = pl.BlockSpec((tm, d), lambda i: (i, 0))
    vec = pl.BlockSpec((1, d), lambda i: (0, 0))
    return pl.pallas_call(
        body,
        name="tail_fwd_bwd",
        grid=(s // tm,),
        in_specs=[row, row, row, row, vec, vec],
        out_specs=[row, row, row, vec, vec, pl.BlockSpec((1, 128), lambda i: (0, 0))],
        out_shape=[
            jax.ShapeDtypeStruct((s, d), F32),
            jax.ShapeDtypeStruct((s, d), BF16),
            jax.ShapeDtypeStruct((s, d), BF16),
            jax.ShapeDtypeStruct((1, d), F32),
            jax.ShapeDtypeStruct((1, d), F32),
            jax.ShapeDtypeStruct((1, 128), F32),
        ],
        compiler_params=_params(("arbitrary",)),
    )(h2, gate, pp, target, g_ple, g_final)


def _rope_tables(s):
    rows = s // GRID_W
    half = HEAD_DIM // 2
    inv_freq = ROPE_THETA ** (-jnp.arange(0, half, 2, dtype=F32) / half)
    ang_r = jnp.arange(rows, dtype=jnp.int32).astype(F32)[:, None] * inv_freq
    ang_c = jnp.arange(GRID_W, dtype=jnp.int32).astype(F32)[:, None] * inv_freq
    cr, sr = (jnp.repeat(t, GRID_W, axis=0) for t in (jnp.cos(ang_r), jnp.sin(ang_r)))
    cc, sc = (jnp.tile(t, (rows, 1)) for t in (jnp.cos(ang_c), jnp.sin(ang_c)))
    cos_t = jnp.concatenate([cr, cr, cc, cc], axis=-1)
    sin_t = jnp.concatenate([-sr, sr, -sc, sc], axis=-1)
    return cos_t, sin_t


def _low_quarters(shape):
    return (lax.broadcasted_iota(jnp.int32, shape, len(shape) - 1) % 64) < 32


def _swap_quarters(x, low):
    up = pltpu.roll(x, HEAD_DIM - 32, x.ndim - 1)
    down = pltpu.roll(x, 32, x.ndim - 1)
    return jnp.where(low, up, down)


def _cols(first, count=1):
    return slice(first * HEAD_DIM, (first + count) * HEAD_DIM)


def _qk_prep(proj, g_q, g_k, cos_t, sin_t, *, tm=256, comm=None):
    s, n = proj.shape
    tm = min(tm, s)

    def body(x_ref, gq_ref, gk_ref, c_ref, s_ref, o_ref):
        cos_v, sin_v = c_ref[...], s_ref[...]
        low = _low_quarters(cos_v.shape)
        for h in range(COL_VA):
            x = x_ref[:, _cols(h)]
            g = gq_ref[...] if h < COL_KA else gk_ref[...]
            xn = x * lax.rsqrt(jnp.mean(x * x, axis=-1, keepdims=True) + EPS) * g
            xr = xn * cos_v + _swap_quarters(xn, low) * sin_v
            if h < COL_KA:
                xr = xr * Q_SCALE
            o_ref[:, _cols(h)] = xr.astype(BF16)
        o_ref[:, _cols(COL_VA, 2)] = x_ref[:, _cols(COL_VA, 2)].astype(BF16)
        o_ref[:, _cols(COL_QB, N_HEADS_B)] = (x_ref[:, _cols(COL_QB, N_HEADS_B)] * Q_SCALE).astype(BF16)
        o_ref[:, _cols(COL_KB, 4)] = x_ref[:, _cols(COL_KB, 4)].astype(BF16)

    row = pl.BlockSpec((tm, n), lambda i: (i, 0))
    tab = pl.BlockSpec((tm, HEAD_DIM), lambda i: (i, 0))
    vec = pl.BlockSpec((1, HEAD_DIM), lambda i: (0, 0))
    res, c_res = _call(
        body, name="qk_prep", grid=(s // tm,),
        in_specs=[row, vec, vec, tab, tab],
        out_specs=[row],
        out_shape=[jax.ShapeDtypeStruct((s, n), BF16)],
        sem=("parallel",), args=(proj, g_q, g_k, cos_t, sin_t), comm=comm)
    return res[0] if comm is None else (res[0], c_res)


def _qk_bwd(dqa, dka, dva, dqb, dkpad, dvpad, proj, g_q, g_k, cos_t, sin_t, *, comm=None, after=None):
    s, n = proj.shape
    tm = min(PAD_LO, s)
    assert PAD_LO % tm == 0
    lo = PAD_LO // tm

    def body(dqa_ref, dka_ref, dva_ref, dqb_ref, dkb_ref, dvb_ref, x_ref, gq_ref, gk_ref, c_ref, s_ref,
             o_ref, dgq_ref, dgk_ref):
        i = pl.program_id(0)
        cos_v, sin_v = c_ref[...], s_ref[...]
        low = _low_quarters(cos_v.shape)

        def head(d, x, g):
            dn = d * cos_v + _swap_quarters(d * sin_v, low)
            r = lax.rsqrt(jnp.mean(x * x, axis=-1, keepdims=True) + EPS)
            xh = x * r
            dng = dn * g
            dx = r * (dng - xh * jnp.mean(dng * xh, axis=-1, keepdims=True))
            return dx.astype(BF16), jnp.sum(dn * xh, axis=0, keepdims=True)

        acc_q = jnp.zeros((1, HEAD_DIM), F32)
        acc_k = jnp.zeros((1, HEAD_DIM), F32)
        for h in range(N_HEADS_A):
            o_ref[:, _cols(h)], part = head(dqa_ref[:, _cols(h)] * ATT_SCALE, x_ref[:, _cols(h)], gq_ref[...])
            acc_q = acc_q + part
        for h in range(N_KV_A):
            o_ref[:, _cols(COL_KA + h)], part = head(dka_ref[:, _cols(h)] * LN2, x_ref[:, _cols(COL_KA + h)],
                                                     gk_ref[...])
            acc_k = acc_k + part
        o_ref[:, _cols(COL_VA, 2)] = dva_ref[...].astype(BF16)
        o_ref[:, _cols(COL_QB, N_HEADS_B)] = (dqb_ref[...] * ATT_SCALE).astype(BF16)
        o_ref[:, _cols(COL_KB, 2)] = (dkb_ref[...] * LN2).astype(BF16)
        o_ref[:, _cols(COL_VB, 2)] = dvb_ref[...].astype(BF16)

        @pl.when(i == 0)
        def _():
            dgq_ref[...] = acc_q
            dgk_ref[...] = acc_k

        @pl.when(i > 0)
        def _():
            dgq_ref[...] += acc_q
            dgk_ref[...] += acc_k

    def rows(width, shift=0):
        return pl.BlockSpec((tm, width), lambda i: (i + shift, 0))

    kv_w = N_KV_A * HEAD_DIM
    q_w = N_HEADS_A * HEAD_DIM
    vec = pl.BlockSpec((1, HEAD_DIM), lambda i: (0, 0))
    res, c_res = _call(
        body, name="qk_bwd", grid=(s // tm,),
        in_specs=[rows(q_w), rows(kv_w), rows(kv_w), rows(q_w), rows(kv_w, lo), rows(kv_w, lo), rows(n),
                  vec, vec, rows(HEAD_DIM), rows(HEAD_DIM)],
        out_specs=[rows(n), vec, vec],
        out_shape=[
            jax.ShapeDtypeStruct((s, n), BF16),
            jax.ShapeDtypeStruct((1, HEAD_DIM), F32),
            jax.ShapeDtypeStruct((1, HEAD_DIM), F32),
        ],
        sem=("arbitrary",), args=(dqa, dka, dva, dqb, dkpad, dvpad, proj, g_q, g_k, cos_t, sin_t), comm=comm,
        after=after)
    return res if comm is None else (res, c_res)


_NT = (((1,), (1,)), ((), ()))
_TN = (((0,), (0,)), ((), ()))


def _attn_a_fwd(pb, *, tq=4096, sub=256, comm=None, after=None):
    s = pb.shape[0]
    tq = min(tq, s)
    sub = min(sub, tq)

    def body(q_ref, k_ref, v_ref, o_ref, lse_ref):
        k = k_ref[...]
        v = v_ref[...]
        for r in range(tq // sub):
            rows = pl.ds(r * sub, sub)
            sc = lax.dot_general(q_ref[rows, :], k, _NT, preferred_element_type=F32)
            m = jnp.max(sc, axis=-1, keepdims=True)
            p = jnp.exp2(sc - m)
            l = jnp.sum(p, axis=-1, keepdims=True)
            o = jnp.dot(p.astype(BF16), v, preferred_element_type=F32)
            o_ref[rows, :] = (o / l).astype(BF16)
            lse_ref[rows, :] = jnp.broadcast_to(m + jnp.log2(l), (sub, HEAD_DIM))

    res, c_res = _call(
        body, name="attn_a_fwd", grid=(N_HEADS_A, s // tq),
        in_specs=[
            pl.BlockSpec((tq, HEAD_DIM), lambda h, i: (i, COL_QA + h)),
            pl.BlockSpec((s, HEAD_DIM), lambda h, i: (0, COL_KA + h // GROUP)),
            pl.BlockSpec((s, HEAD_DIM), lambda h, i: (0, COL_VA + h // GROUP)),
        ],
        out_specs=[
            pl.BlockSpec((tq, HEAD_DIM), lambda h, i: (i, h)),
            pl.BlockSpec((None, tq, HEAD_DIM), lambda h, i: (h, i, 0)),
        ],
        out_shape=[
            jax.ShapeDtypeStruct((s, (N_HEADS_A + N_HEADS_B) * HEAD_DIM), BF16),
            jax.ShapeDtypeStruct((N_HEADS_A, s, HEAD_DIM), F32),
        ],
        sem=("parallel", "parallel"), args=(pb, pb, pb), comm=comm, after=after)
    return res if comm is None else (res, c_res)


def _attn_a_bwd(pb, att, datt, lse, *, tq=1024, sub=256, comm=None):
    s = pb.shape[0]
    tq = min(tq, s)
    sub = min(sub, tq)

    def body(q_ref, k_ref, v_ref, o_ref, do_ref, lse_ref, dq_ref, dk_ref, dv_ref):
        first = jnp.logical_and(pl.program_id(1) == 0, pl.program_id(2) == 0)
        k = k_ref[...]
        v = v_ref[...]
        dk = dv = None
        for r in range(tq // sub):
            rows = pl.ds(r * sub, sub)
            q = q_ref[rows, :]
            do = do_ref[rows, :]
            sc = lax.dot_general(q, k, _NT, preferred_element_type=F32)
            p = jnp.exp2(sc - lse_ref[rows, :][:, :1])
            dp = lax.dot_general(do, v, _NT, preferred_element_type=F32)
            delta = jnp.sum(do.astype(F32) * o_ref[rows, :].astype(F32), axis=-1, keepdims=True)
            ds = (p * (dp - delta)).astype(BF16)
            dq_ref[rows, :] = jnp.dot(ds, k, preferred_element_type=F32)
            dk_r = lax.dot_general(ds, q, _TN, preferred_element_type=F32)
            dv_r = lax.dot_general(p.astype(BF16), do, _TN, preferred_element_type=F32)
            dk = dk_r if dk is None else dk + dk_r
            dv = dv_r if dv is None else dv + dv_r

        @pl.when(first)
        def _():
            dk_ref[...] = dk
            dv_ref[...] = dv

        @pl.when(jnp.logical_not(first))
        def _():
            dk_ref[...] += dk
            dv_ref[...] += dv

    qmap = lambda kv, g, i: (i, kv * GROUP + g)
    res, c_res = _call(
        body, name="attn_a_bwd", grid=(N_KV_A, GROUP, s // tq),
        in_specs=[
            pl.BlockSpec((tq, HEAD_DIM), lambda kv, g, i: (i, COL_QA + kv * GROUP + g)),
            pl.BlockSpec((s, HEAD_DIM), lambda kv, g, i: (0, COL_KA + kv)),
            pl.BlockSpec((s, HEAD_DIM), lambda kv, g, i: (0, COL_VA + kv)),
            pl.BlockSpec((tq, HEAD_DIM), qmap),
            pl.BlockSpec((tq, HEAD_DIM), qmap),
            pl.BlockSpec((None, tq, HEAD_DIM), lambda kv, g, i: (kv * GROUP + g, i, 0)),
        ],
        out_specs=[
            pl.BlockSpec((tq, HEAD_DIM), qmap),
            pl.BlockSpec((s, HEAD_DIM), lambda kv, g, i: (0, kv)),
            pl.BlockSpec((s, HEAD_DIM), lambda kv, g, i: (0, kv)),
        ],
        out_shape=[
            jax.ShapeDtypeStruct((s, N_HEADS_A * HEAD_DIM), F32),
            jax.ShapeDtypeStruct((s, N_KV_A * HEAD_DIM), F32),
            jax.ShapeDtypeStruct((s, N_KV_A * HEAD_DIM), F32),
        ],
        sem=("arbitrary", "arbitrary", "arbitrary"), args=(pb, pb, pb, att, datt, lse), comm=comm)
    return res if comm is None else (res, c_res)


def _t5_bucket(rel):
    nb = N_BUCKETS // 2
    ret = jnp.where(rel > 0, nb, 0)
    n = jnp.abs(rel)
    max_exact = nb // 2
    nf = jnp.maximum(n, 1).astype(F32)
    large = max_exact + (jnp.log(nf / max_exact) / math.log(MAX_DISTANCE / max_exact)
                         * (nb - max_exact)).astype(jnp.int32)
    large = jnp.minimum(large, nb - 1)
    return ret + jnp.where(n < max_exact, n, large)


def _band_buckets():
    r = jnp.arange(BLOCK_Q, dtype=jnp.int32)
    j = jnp.arange(3 * BLOCK_Q, dtype=jnp.int32)
    return _t5_bucket((j[None, :] - BLOCK_Q) - r[:, None])


def _band_bias(bucket, table_ref, h):
    acc = jnp.zeros(bucket.shape, F32)
    for b in range(N_BUCKETS):
        acc = jnp.where(bucket == b, table_ref[b, h], acc)
    return acc


GQ = GROUP * BLOCK_Q


def _stack_heads(x):
    return jnp.concatenate([x[:, _cols(g)] for g in range(GROUP)], axis=0)


def _unstack_heads(x):
    return jnp.concatenate([x[g * BLOCK_Q:(g + 1) * BLOCK_Q] for g in range(GROUP)], axis=1)


def _group_bias(bucket, table_ref, kv):
    r = lax.broadcasted_iota(jnp.int32, (BLOCK_Q, 3 * BLOCK_Q), 0)
    j = lax.broadcasted_iota(jnp.int32, (BLOCK_Q, 3 * BLOCK_Q), 1)
    inside = jnp.abs(j - BLOCK_Q - r) <= WINDOW
    return jnp.concatenate([jnp.where(inside, _band_bias(bucket, table_ref, kv * GROUP + g) * LOG2E, NEG_INF)
                            for g in range(GROUP)], axis=0)


def _group_sink(sink_ref, kv):
    head = lax.broadcasted_iota(jnp.int32, (GQ, 1), 0) // BLOCK_Q
    snk = jnp.zeros((GQ, 1), F32)
    for g in range(GROUP):
        snk = jnp.where(head == g, sink_ref[0, kv * GROUP + g] * LOG2E, snk)
    return snk


def _band_mask(n, s):
    kabs = n * BLOCK_Q + lax.broadcasted_iota(jnp.int32, (1, 3 * BLOCK_Q), 1) - BLOCK_Q
    return (kabs >= 0) & (kabs < s)


def _band_start(n):
    return pl.multiple_of(n * BLOCK_Q + (PAD_LO - BLOCK_Q), BLOCK_Q)


def _attn_b_fwd(pb, kpad, vpad, bucket, table, sink, att, *, comm=None, after=None):
    s = pb.shape[0]
    nblk = s // BLOCK_Q
    sp = kpad.shape[0]

    def body(table_ref, sink_ref, q0_ref, q1_ref, k_ref, v_ref, bucket_ref, _, o_ref, lse_ref, bias_ref):
        n = pl.program_id(0)

        @pl.when(n == 0)
        def _():
            for kv in range(N_KV_B):
                bias_ref[kv * GQ:(kv + 1) * GQ, :] = _group_bias(bucket_ref[...], table_ref, kv)

        band = pl.ds(_band_start(n), 3 * BLOCK_Q)
        mask = _band_mask(n, s)
        for kv, q_ref in enumerate((q0_ref, q1_ref)):
            kb = k_ref[band, _cols(kv)]
            vb = v_ref[band, _cols(kv)]
            sc = lax.dot_general(_stack_heads(q_ref[...]), kb, _NT, preferred_element_type=F32)
            sc = jnp.where(mask, sc + bias_ref[kv * GQ:(kv + 1) * GQ, :], NEG_INF)
            snk = _group_sink(sink_ref, kv)
            m = jnp.maximum(jnp.max(sc, axis=-1, keepdims=True), snk)
            p = jnp.exp2(sc - m)
            l = jnp.sum(p, axis=-1, keepdims=True) + jnp.exp2(snk - m)
            o = jnp.dot(p.astype(BF16), vb, preferred_element_type=F32)
            o_ref[:, _cols(kv * GROUP, GROUP)] = _unstack_heads((o / l).astype(BF16))
            lse = m + jnp.log2(l)
            for g in range(GROUP):
                lse_ref[kv * GROUP + g] = jnp.broadcast_to(lse[g * BLOCK_Q:(g + 1) * BLOCK_Q], (BLOCK_Q, HEAD_DIM))

    smem = pl.BlockSpec(memory_space=pltpu.SMEM)
    wide = GROUP * HEAD_DIM
    whole = pl.BlockSpec((sp, N_KV_B * HEAD_DIM), lambda n: (0, 0))
    res, c_res = _call(
        body, name="attn_b_fwd", grid=(nblk,),
        in_specs=[
            smem,
            smem,
            pl.BlockSpec((BLOCK_Q, wide), lambda n: (n, COL_QB // GROUP)),
            pl.BlockSpec((BLOCK_Q, wide), lambda n: (n, COL_QB // GROUP + 1)),
            whole,
            whole,
            pl.BlockSpec((BLOCK_Q, 3 * BLOCK_Q), lambda n: (0, 0)),
            _ANY,
        ],
        out_specs=[
            pl.BlockSpec((BLOCK_Q, N_HEADS_B * HEAD_DIM), lambda n: (n, 1)),
            pl.BlockSpec((N_HEADS_B, BLOCK_Q, HEAD_DIM), lambda n: (0, n, 0)),
        ],
        out_shape=[
            jax.ShapeDtypeStruct(att.shape, BF16),
            jax.ShapeDtypeStruct((N_HEADS_B, s, HEAD_DIM), F32),
        ],
        scratch_shapes=[pltpu.VMEM((N_KV_B * GQ, 3 * BLOCK_Q), F32)],
        sem=("arbitrary",), args=(table, sink, pb, pb, kpad, vpad, bucket, att), comm=comm, after=after,
        aliases={7: 0})
    return res if comm is None else (res, c_res)


def _attn_b_bwd(pb, kpad, vpad, att, datt, lse, bucket, table, sink, *, comm=None, after=None):
    s = pb.shape[0]
    nblk = s // BLOCK_Q
    sp = kpad.shape[0]

    def body(table_ref, sink_ref, q0_ref, q1_ref, k_ref, v_ref, o_ref, do_ref, lse_ref, bucket_ref,
             dq_ref, dk_ref, dv_ref, dtab_ref, dsink_ref, bias_ref, dbias_ref):
        n = pl.program_id(0)

        @pl.when(n == 0)
        def _():
            dk_ref[...] = jnp.zeros_like(dk_ref)
            dv_ref[...] = jnp.zeros_like(dv_ref)
            dbias_ref[...] = jnp.zeros_like(dbias_ref)
            dsink_ref[...] = jnp.zeros_like(dsink_ref)
            for kv in range(N_KV_B):
                bias_ref[kv * GQ:(kv + 1) * GQ, :] = _group_bias(bucket_ref[...], table_ref, kv)

        band = pl.ds(_band_start(n), 3 * BLOCK_Q)
        mask = _band_mask(n, s)
        for kv, q_ref in enumerate((q0_ref, q1_ref)):
            wide_cols = _cols(kv * GROUP, GROUP)
            q = _stack_heads(q_ref[...])
            do = _stack_heads(do_ref[:, wide_cols])
            o = _stack_heads(o_ref[:, wide_cols])
            kb = k_ref[band, _cols(kv)]
            vb = v_ref[band, _cols(kv)]
            lse = jnp.concatenate([lse_ref[kv * GROUP + g][:, :1] for g in range(GROUP)], axis=0)
            sc = lax.dot_general(q, kb, _NT, preferred_element_type=F32)
            sc = jnp.where(mask, sc + bias_ref[kv * GQ:(kv + 1) * GQ, :], NEG_INF)
            p = jnp.exp2(sc - lse)
            dp = lax.dot_general(do, vb, _NT, preferred_element_type=F32)
            delta = jnp.sum(do.astype(F32) * o.astype(F32), axis=-1, keepdims=True)
            ds = p * (dp - delta)
            dsb = ds.astype(BF16)
            dq_ref[:, wide_cols] = _unstack_heads(jnp.dot(dsb, kb, preferred_element_type=F32))
            dk_ref[band, _cols(kv)] += lax.dot_general(dsb, q, _TN, preferred_element_type=F32)
            dv_ref[band, _cols(kv)] += lax.dot_general(p.astype(BF16), do, _TN, preferred_element_type=F32)
            dbias_ref[kv * GQ:(kv + 1) * GQ, :] += ds
            sink_part = -jnp.exp2(_group_sink(sink_ref, kv) - lse) * delta
            for g in range(GROUP):
                rows = slice(g * BLOCK_Q, (g + 1) * BLOCK_Q)
                dsink_ref[kv * GROUP + g] += jnp.broadcast_to(
                    jnp.sum(sink_part[rows], axis=0, keepdims=True), (1, HEAD_DIM))

        @pl.when(n == nblk - 1)
        def _():
            bucket_v = bucket_ref[...]
            row = lax.broadcasted_iota(jnp.int32, (N_BUCKETS, HEAD_DIM), 0)
            for h in range(N_HEADS_B):
                acc = dbias_ref[h * BLOCK_Q:(h + 1) * BLOCK_Q, :]
                tot = jnp.zeros((N_BUCKETS, HEAD_DIM), F32)
                for b in range(N_BUCKETS):
                    tot = jnp.where(row == b, jnp.sum(jnp.where(bucket_v == b, acc, 0.0), keepdims=True), tot)
                dtab_ref[h] = tot

    smem = pl.BlockSpec(memory_space=pltpu.SMEM)
    wide = GROUP * HEAD_DIM
    whole = pl.BlockSpec((sp, N_KV_B * HEAD_DIM), lambda n: (0, 0))
    group_b = pl.BlockSpec((BLOCK_Q, N_HEADS_B * HEAD_DIM), lambda n: (n, 1))
    res, c_res = _call(
        body, name="attn_b_bwd", grid=(nblk,),
        in_specs=[
            smem,
            smem,
            pl.BlockSpec((BLOCK_Q, wide), lambda n: (n, COL_QB // GROUP)),
            pl.BlockSpec((BLOCK_Q, wide), lambda n: (n, COL_QB // GROUP + 1)),
            whole,
            whole,
            group_b,
            group_b,
            pl.BlockSpec((N_HEADS_B, BLOCK_Q, HEAD_DIM), lambda n: (0, n, 0)),
            pl.BlockSpec((BLOCK_Q, 3 * BLOCK_Q), lambda n: (0, 0)),
        ],
        out_specs=[
            pl.BlockSpec((BLOCK_Q, N_HEADS_B * HEAD_DIM), lambda n: (n, 0)),
            whole,
            whole,
            pl.BlockSpec((N_HEADS_B, N_BUCKETS, HEAD_DIM), lambda n: (0, 0, 0)),
            pl.BlockSpec((N_HEADS_B, 1, HEAD_DIM), lambda n: (0, 0, 0)),
        ],
        out_shape=[
            jax.ShapeDtypeStruct((s, N_HEADS_B * HEAD_DIM), F32),
            jax.ShapeDtypeStruct((sp, N_KV_B * HEAD_DIM), F32),
            jax.ShapeDtypeStruct((sp, N_KV_B * HEAD_DIM), F32),
            jax.ShapeDtypeStruct((N_HEADS_B, N_BUCKETS, HEAD_DIM), F32),
            jax.ShapeDtypeStruct((N_HEADS_B, 1, HEAD_DIM), F32),
        ],
        scratch_shapes=[pltpu.VMEM((N_KV_B * GQ, 3 * BLOCK_Q), F32), pltpu.VMEM((N_KV_B * GQ, 3 * BLOCK_Q), F32)],
        sem=("arbitrary",),
        args=(table, sink, pb, pb, kpad, vpad, att, datt, lse, bucket), comm=comm, after=after)
    return res if comm is None else (res, c_res)


def _other_chips(x, y):
    return [(x, 1 - y), (1 - x, y), (1 - x, 1 - y)]


_HBM = pl.BlockSpec(memory_space=pltpu.HBM)
_SEM = pl.BlockSpec(memory_space=pltpu.SEMAPHORE)
_SPLIT = pltpu.CompilerParams(has_side_effects=pltpu.SideEffectType.DATAFLOW_SIDE_EFFECTING)


def _in_hbm(a):
    return pltpu.with_memory_space_constraint(a, pltpu.HBM)


def _my_half(rows):
    c = lax.axis_index("c")
    half = rows // 2
    return pl.ds(pl.multiple_of(c * half, half), half), pl.ds(pl.multiple_of((1 - c) * half, half), half)


def _gather_route(shapes):
    def route(src, land):
        x, y, c = lax.axis_index("x"), lax.axis_index("y"), lax.axis_index("c")
        out = []
        for t, shape in enumerate(shapes):
            mine, _ = _my_half(shape[0])
            for px, py in _other_chips(x, y):
                out.append((src[t].at[mine], land[t].at[2 * x + y, mine], land[t].at[2 * px + py, mine], (px, py, c)))
        return out

    return route


def _exchange_route(n_t):
    def route(src, land):
        x, y, c = lax.axis_index("x"), lax.axis_index("y"), lax.axis_index("c")
        out = []
        for t in range(n_t):
            for px, py in _other_chips(x, y):
                k = 2 * px + py
                out.append((src[t].at[k], land[t].at[2 * (2 * x + y) + c], land[t].at[2 * k + c], (px, py, c)))
        return out

    return route


def _own_slot(shape, dtype, slot, block):
    return lax.dynamic_update_slice(lax.empty(shape, dtype), block[None], (slot,) + (0,) * (len(shape) - 1))


def _split_start(name, srcs, lands, route, after):
    n = len(srcs)

    def body(*refs):
        src, land, send_sems, recv_sems, token = refs[:n], refs[n:2 * n], refs[2 * n + 1], refs[2 * n + 2], refs[-1]
        for i, (src_ref, dst_ref, _, to) in enumerate(route(src, land)):
            pltpu.make_async_remote_copy(src_ref=src_ref, dst_ref=dst_ref, send_sem=send_sems.at[i],
                                         recv_sem=recv_sems.at[i], device_id=to, device_id_type=_MESH).start()
        token[...] = jnp.zeros_like(token)

    sem = pltpu.SemaphoreType.DMA((3 * n,))
    lands = list(lands)
    res = pl.pallas_call(
        body, name=name,
        in_specs=[_HBM] * (2 * n) + [_ANY],
        out_specs=[_SEM, _SEM] + [_HBM] * (2 * n) + [pl.BlockSpec(memory_space=pltpu.VMEM)],
        out_shape=[sem, sem] + [pltpu.HBM(a.shape, a.dtype) for a in list(srcs) + lands]
        + [jax.ShapeDtypeStruct((8, 128), F32)],
        input_output_aliases={i: 2 + i for i in range(2 * n)},
        compiler_params=_SPLIT,
    )(*[_in_hbm(a) for a in srcs], *[_in_hbm(a) for a in lands], after)
    return (res[0], res[1]), res[2:2 + n], res[2 + n:2 + 2 * n], res[-1]


def _split_wait(name, srcs, lands, sems, route, after):
    n = len(srcs)

    def body(*refs):
        src, land, send_sems, recv_sems = refs[:n], refs[n:2 * n], refs[2 * n], refs[2 * n + 1]
        for i, (src_ref, _, dst_ref, to) in enumerate(route(src, land)):
            cp = pltpu.make_async_remote_copy(src_ref=src_ref, dst_ref=dst_ref, send_sem=send_sems.at[i],
                                              recv_sem=recv_sems.at[i], device_id=to, device_id_type=_MESH)
            cp.wait_send()
            cp.wait_recv()

    res = pl.pallas_call(
        body, name=name,
        in_specs=[_HBM] * (2 * n) + [_SEM, _SEM, _ANY],
        out_specs=[_HBM] * (2 * n),
        out_shape=[pltpu.HBM(a.shape, a.dtype) for a in list(srcs) + list(lands)],
        input_output_aliases={i: i for i in range(2 * n)},
        compiler_params=_SPLIT,
    )(*srcs, *lands, sems[0], sems[1], after)
    return res[:n], res[n:]


def _comm_only(name, comm):
    return _call(lambda: None, name=name, grid=(1,), in_specs=[], out_specs=[], out_shape=[], args=(), comm=comm)[1]


def _swap_comm(shards, lands):
    n_t = len(lands)

    def copies(land, sems, later):
        send_sems, recv_sems = sems
        x, y = lax.axis_index("x"), lax.axis_index("y")
        sends, recvs = [], []
        for t in range(n_t):
            mine, other = _my_half(shards[t].shape[0])
            for j, (px, py) in enumerate(_other_chips(x, y)):
                k = 2 * px + py
                for part, out in ((mine, sends), (other, recvs)) if later else ((mine, sends),):
                    out.append(pltpu.make_async_remote_copy(
                        src_ref=land[t].at[k, part], dst_ref=land[t].at[k, part], send_sem=send_sems.at[3 * t + j],
                        recv_sem=recv_sems.at[3 * t + j], device_id=_sibling(), device_id_type=_MESH))
        return sends, recvs

    def start(ins, land, sems):
        for cp in copies(land, sems, False)[0]:
            cp.start()

    def finish(ins, land, sems):
        sends, recvs = copies(land, sems, True)
        for cp in recvs:
            cp.wait_recv()
        for cp in sends:
            cp.wait_send()

    return _Comm(
        lands, [jax.ShapeDtypeStruct(a.shape, a.dtype) for a in lands],
        [pltpu.SemaphoreType.DMA((3 * n_t,)), pltpu.SemaphoreType.DMA((3 * n_t,))],
        start, finish, aliases={t: t for t in range(n_t)})


def _forward_comm(partials, lands):
    n_t = len(lands)

    def copies(land, sems, later):
        send_sems, recv_sems = sems
        x, y, c = lax.axis_index("x"), lax.axis_index("y"), lax.axis_index("c")
        sends, recvs = [], []
        for t in range(n_t):
            for j, k in enumerate([2 * x + y] + [2 * px + py for px, py in _other_chips(x, y)]):
                for slot, out in ((2 * k + c, sends), (2 * k + 1 - c, recvs)) if later else ((2 * k + c, sends),):
                    out.append(pltpu.make_async_remote_copy(
                        src_ref=land[t].at[slot], dst_ref=land[t].at[slot], send_sem=send_sems.at[4 * t + j],
                        recv_sem=recv_sems.at[4 * t + j], device_id=_sibling(), device_id_type=_MESH))
        return sends, recvs

    def start(ins, land, sems):
        for cp in copies(land, sems, False)[0]:
            cp.start()

    def finish(ins, land, sems):
        sends, recvs = copies(land, sems, True)
        for cp in recvs:
            cp.wait_recv()
        for cp in sends:
            cp.wait_send()

    return _Comm(
        lands, [jax.ShapeDtypeStruct(a.shape, a.dtype) for a in lands],
        [pltpu.SemaphoreType.DMA((4 * n_t,)), pltpu.SemaphoreType.DMA((4 * n_t,))],
        start, finish, aliases={t: t for t in range(n_t)})


def _allreduce_small(pack):
    rows, d = pack.shape

    def body(p_ref, sum_ref, all_ref, send_sems, recv_sems):
        x, y, c = lax.axis_index("x"), lax.axis_index("y"), lax.axis_index("c")
        me = 4 * x + 2 * y + c
        all_ref[me] = p_ref[...]
        peers = []
        for dx in range(2):
            for dy in range(2):
                for dc in range(2):
                    if dx or dy or dc:
                        px = 1 - x if dx else x
                        py = 1 - y if dy else y
                        pc = 1 - c if dc else c
                        peers.append((4 * dx + 2 * dy + dc - 1, (px, py, pc)))
        sends = []
        for k, to in peers:
            cp = pltpu.make_async_remote_copy(
                src_ref=p_ref, dst_ref=all_ref.at[me], send_sem=send_sems.at[k], recv_sem=recv_sems.at[k],
                device_id=to, device_id_type=_MESH)
            cp.start()
            sends.append(cp)
        for k, (px, py, pc) in peers:
            pltpu.make_async_remote_copy(
                src_ref=p_ref, dst_ref=all_ref.at[4 * px + 2 * py + pc], send_sem=send_sems.at[k],
                recv_sem=recv_sems.at[k], device_id=(px, py, pc), device_id_type=_MESH).wait_recv()
        for cp in sends:
            cp.wait_send()
        tot = all_ref[0]
        for i in range(1, N_DEV):
            tot = tot + all_ref[i]
        sum_ref[...] = tot

    vm = pl.BlockSpec(memory_space=pltpu.VMEM)
    return pl.pallas_call(
        body,
        name="allreduce_small",
        in_specs=[vm],
        out_specs=vm,
        out_shape=jax.ShapeDtypeStruct((rows, d), F32),
        scratch_shapes=[
            pltpu.VMEM((N_DEV, rows, d), F32),
            pltpu.SemaphoreType.DMA((N_DEV - 1,)),
            pltpu.SemaphoreType.DMA((N_DEV - 1,)),
        ],
    )(pack)


def _adamw_math(w, g, m, v):
    m = ADAM_B1 * m + (1.0 - ADAM_B1) * g
    v = ADAM_B2 * v + (1.0 - ADAM_B2) * (g * g)
    m_hat = m / (1.0 - ADAM_B1 ** ADAM_STEP)
    v_hat = v / (1.0 - ADAM_B2 ** ADAM_STEP)
    delta = -ADAM_LR * (m_hat / (jnp.sqrt(v_hat) + ADAM_EPS) + ADAM_WD * w)
    return delta, m, v


def _sum_adamw(parts, w, m, v, *, name, tr=256):
    r, c = w.shape
    tr = min(tr, r)
    tc = min(c, 1024)

    def body(p_ref, w_ref, m_ref, v_ref, g_ref, d_ref, m2_ref, v2_ref):
        g = p_ref[0].astype(F32)
        for i in range(1, N_DEV):
            g = g + p_ref[i].astype(F32)
        delta, m2, v2 = _adamw_math(w_ref[...], g, m_ref[...], v_ref[...])
        g_ref[...] = g
        d_ref[...] = delta
        m2_ref[...] = m2
        v2_ref[...] = v2

    blk = pl.BlockSpec((tr, tc), lambda i, j: (i, j))
    return pl.pallas_call(
        body,
        name=name,
        grid=(r // tr, c // tc),
        in_specs=[pl.BlockSpec((N_DEV, tr, tc), lambda i, j: (0, i, j)), blk, blk, blk],
        out_specs=[blk] * 4,
        out_shape=[jax.ShapeDtypeStruct((r, c), F32)] * 4,
        compiler_params=_params(("parallel", "parallel")),
    )(parts, w, m, v)


def _adamw_small(g, w, m, v):
    def body(g_ref, w_ref, m_ref, v_ref, d_ref, m2_ref, v2_ref):
        delta, m2, v2 = _adamw_math(w_ref[...], g_ref[...], m_ref[...], v_ref[...])
        d_ref[...] = delta
        m2_ref[...] = m2
        v2_ref[...] = v2

    vm = pl.BlockSpec(memory_space=pltpu.VMEM)
    return pl.pallas_call(
        body,
        name="adamw_small",
        in_specs=[vm] * 4,
        out_specs=[vm] * 3,
        out_shape=[jax.ShapeDtypeStruct(g.shape, F32)] * 3,
    )(g, w, m, v)


def _relu2_epilogue(acc):
    ra = jnp.maximum(acc, 0.0)
    return ra * ra, ra


def _residual_norm_epilogue(acc, res, g):
    h = acc + res
    return h, h * lax.rsqrt(jnp.mean(h * h, axis=-1, keepdims=True) + EPS) * g


def _rows(stacked):
    return stacked.reshape(stacked.shape[0] * stacked.shape[1], stacked.shape[2])


def _by_chip(mat):
    return mat.reshape(N_CHIPS, mat.shape[0] // N_CHIPS, mat.shape[1])


def _local_step(x, p, target, shards, small, update):
    s, d = x.shape
    cos_t, sin_t = _rope_tables(s)
    bucket = _band_buckets()
    p_bf = p.astype(BF16)
    wts = {}

    chip = 2 * lax.axis_index("x") + lax.axis_index("y")
    core = lax.axis_index("c")

    def gather(tag, names, after):
        srcs = [cast[n] for n in names]
        route = _gather_route([a.shape for a in srcs])
        sems, srcs, lands, token = _split_start(f"gather_start_{tag}", srcs, [zones[n] for n in names], route, after)

        def landed(done):
            got_srcs, got_lands = _split_wait(f"gather_wait_{tag}", srcs, lands, sems, route, done)
            comm = _swap_comm(got_srcs, got_lands)
            comm.waited = got_srcs[0]
            return comm

        return landed, token

    def prepare(n, zero):
        cast[n] = (shards[n] + zero).astype(BF16)
        zones[n] = _own_slot((N_CHIPS,) + cast[n].shape, BF16, chip, cast[n])

    cast, zones = {}, {}
    prepare("w_in", 0.0)
    in_landed, token = gather("in", ["w_in"], small["attn_norm_g"])
    for n in shards:
        if n != "w_in":
            prepare(n, token[:1, :1])
    g_attn = small["attn_norm_g"] + token[:1, :1]
    u = _rms_fwd(x, g_attn, name="norm_attn")
    prepared = u[:1, :1].astype(F32) + sum(
        (lax.dynamic_slice(zones[n], (chip, 0, 0), (1, 1, 1))[0] + cast[n][:1, :1]).astype(F32)
        for n in zones if n != "w_in")
    (wts["w_in"],) = _comm_only("swap_w_in", in_landed(prepared))
    mid_landed, token = gather("mid", ["w_out"], wts["w_in"])
    proj = _matmul(u, wts["w_in"], mode="nn", out_dtypes=[F32], name="mm_in", bn=768, after=token)
    pb, (w_out_s,) = _qk_prep(proj, small["q_norm_g"], small["k_norm_g"], cos_t, sin_t, comm=mid_landed(proj))
    wts["w_out"] = _rows(w_out_s)
    up_landed, token = gather("up", ["w_up"], pb)
    att_a, lse_a = _attn_a_fwd(pb, after=token)
    pad = ((PAD_LO, PAD_HI), (0, 0))
    kpad = jnp.pad(pb[:, COL_KB * HEAD_DIM:COL_VB * HEAD_DIM], pad)
    vpad = jnp.pad(pb[:, COL_VB * HEAD_DIM:], pad)
    up_swap = up_landed(att_a)
    down_landed, token = gather("down", ["w_down"], up_swap.waited)
    (att, lse_b), (wts["w_up"],) = _attn_b_fwd(pb, kpad, vpad, bucket, small["rel_bias_table"],
                                               small["sink_logits"], att_a, comm=up_swap, after=token)
    h1, mn = _matmul(att, wts["w_out"], mode="nn", out_dtypes=[F32, BF16], name="mm_out", bm=512, bn=d,
                     epilogue=_residual_norm_epilogue, extras=(x,), vecs=(small["mlp_norm_g"],))
    r, ra = _matmul(mn, wts["w_up"], mode="nn", out_dtypes=[BF16, BF16], name="mm_up", epilogue=_relu2_epilogue,
                    bm=2048)
    (w_down_s,) = _comm_only("swap_w_down", down_landed(r))
    wts["w_down"] = _rows(w_down_s)
    late_landed, token = gather("late", ["w_gate", "ple_w"], w_down_s)
    h2 = _matmul(r, wts["w_down"], mode="nn", out_dtypes=[F32], name="mm_down",
                 epilogue=lambda acc, res: (acc + res,), extras=(h1,), after=token)
    ng, (w_gate_s, wts["ple_w"]) = _rms_fwd(h2, small["gate_norm_g"], name="norm_gate", comm=late_landed(h2))
    wts["w_gate"] = _rows(w_gate_s)
    gate = _matmul(ng, wts["w_gate"], mode="nn", out_dtypes=[F32], name="mm_gate",
                   epilogue=lambda acc: (1.0 / (1.0 + jnp.exp(-acc)),))
    pp = _matmul(p_bf, wts["ple_w"], mode="nn", out_dtypes=[F32], name="mm_ple", bn=512)
    dh3, dz, dpp, dg_final, dg_ple, loss = _tail(h2, gate, pp, target, small["ple_norm_g"], small["final_norm_g"])

    dng = _matmul(dz, wts["w_gate"], mode="nt", out_dtypes=[F32], name="mm_gate_dx")
    gw_gate = _matmul(ng, dz, mode="tn", out_dtypes=[BF16], name="mm_gate_dw", bm=512, bk=4096)
    gw_ple = _matmul(p_bf, dpp, mode="tn", out_dtypes=[BF16], name="mm_ple_dw", bn=512, out_stack=N_CHIPS)
    dh2, dh2_bf, dg_gate = _rms_bwd(h2, dng, small["gate_norm_g"], dh3, name="norm_gate_bwd", want_bf16=True)

    def exchange(tag, partials, after):
        route = _exchange_route(len(partials))
        lands = [_own_slot((N_DEV,) + g.shape[1:], g.dtype, 2 * chip + core,
                           lax.dynamic_index_in_dim(g, chip, 0, keepdims=False)) for g in partials]
        sems, srcs, lands, token = _split_start(f"exchange_start_{tag}", partials, lands, route, after)

        def landed(done):
            got_srcs, got_lands = _split_wait(f"exchange_wait_{tag}", srcs, lands, sems, route, done)
            comm = _forward_comm(got_srcs, got_lands)
            comm.waited = got_srcs[0]
            return comm

        return landed, token

    big = {}
    gate_landed, token = exchange("gate", [_by_chip(gw_gate), gw_ple], dh2_bf)
    gw_down = _matmul(r, dh2_bf, mode="tn", out_dtypes=[BF16], name="mm_down_dw", after=token, bm=512, bk=4096)
    da, (parts_gate, parts_ple) = _matmul(
        dh2_bf, wts["w_down"], mode="nt", out_dtypes=[BF16], name="mm_down_dx", bm=2048,
        epilogue=lambda acc, ra_v: (acc * (2.0 * ra_v.astype(F32)),), extras=(ra,), comm=gate_landed(gw_down))
    down_landed, token = exchange("down", [_by_chip(gw_down)], da)
    big["w_gate"], big["ple_w"] = update("w_gate", parts_gate), update("ple_w", parts_ple)
    gw_up = _matmul(mn, da, mode="tn", out_dtypes=[BF16], name="mm_up_dw", out_stack=N_CHIPS, after=token,
                    bm=512, bk=4096)
    dmn = _matmul(da, wts["w_up"], mode="nt", out_dtypes=[F32], name="mm_up_dx", bk=4096)
    dh1, dh1_bf, dg_mlp = _rms_bwd(h1, dmn, small["mlp_norm_g"], dh2, name="norm_mlp_bwd", want_bf16=True)
    down_forward = down_landed(dh1_bf)
    up_landed, token = exchange("up", [gw_up], down_forward.waited)
    datt = _matmul(dh1_bf, wts["w_out"], mode="nt", out_dtypes=[BF16], name="mm_out_dx", after=token)
    gw_out = _matmul(att, dh1_bf, mode="tn", out_dtypes=[BF16], name="mm_out_dw", bm=512, bk=4096)
    dqb, dkpad, dvpad, dtab, dsink = _attn_b_bwd(pb, kpad, vpad, att, datt, lse_b, bucket,
                                                 small["rel_bias_table"], small["sink_logits"])
    (dqa, dka, dva), (parts_down,) = _attn_a_bwd(pb, att, datt, lse_a, comm=down_forward)
    up_forward = up_landed(dqa)
    out_landed, token = exchange("out", [_by_chip(gw_out)], up_forward.waited)
    (dproj, dg_q, dg_k), (parts_up,) = _qk_bwd(dqa, dka, dva, dqb, dkpad, dvpad, proj,
                                               small["q_norm_g"], small["k_norm_g"], cos_t, sin_t,
                                               comm=up_forward, after=token)
    gw_in = _matmul(u, dproj, mode="tn", out_dtypes=[BF16], name="mm_in_dw", bn=768, out_stack=N_CHIPS,
                    bm=512, bk=4096)
    out_forward = out_landed(gw_in)
    in_landed, token = exchange("in", [gw_in], out_forward.waited)
    du, (parts_out,) = _matmul(dproj, wts["w_in"], mode="nt", out_dtypes=[F32], name="mm_in_dx", bk=3072,
                               comm=out_forward, after=token)
    grad_x, dg_attn = _rms_bwd(x, du, small["attn_norm_g"], dh1, name="norm_attn_bwd", want_bf16=False)
    for n, parts in (("w_down", parts_down), ("w_up", parts_up), ("w_out", parts_out)):
        big[n] = update(n, parts)
    done = dg_attn + sum(big[n][0][0, :1, :] for n in ("w_down", "w_up", "w_out"))
    (parts_in,) = _comm_only("forward_w_in", in_landed(done))
    big["w_in"] = update("w_in", parts_in)

    small_g = {
        "attn_norm_g": dg_attn, "mlp_norm_g": dg_mlp, "ple_norm_g": dg_ple, "gate_norm_g": dg_gate,
        "final_norm_g": dg_final, "q_norm_g": dg_q, "k_norm_g": dg_k,
        "sink_logits": dsink[:, 0, 0][None, :], "rel_bias_table": dtab[:, :, 0].T,
    }
    return loss, grad_x, big, small_g


_SMALL_ROWS = ["attn_norm_g", "mlp_norm_g", "ple_norm_g", "gate_norm_g", "final_norm_g"]
_PACK_ROWS = 8


def _pack_small(vals, d):
    rows = [vals[n].reshape(1, d) for n in _SMALL_ROWS]
    misc = jnp.concatenate([
        vals["q_norm_g"].reshape(1, HEAD_DIM), vals["k_norm_g"].reshape(1, HEAD_DIM),
        jnp.pad(vals["sink_logits"].reshape(1, N_HEADS_B), ((0, 0), (0, HEAD_DIM - N_HEADS_B))),
        vals["rel_bias_table"].reshape(1, N_BUCKETS * N_HEADS_B)], axis=1)
    rows.append(jnp.pad(misc, ((0, 0), (0, d - misc.shape[1]))))
    rows.append(jnp.zeros((_PACK_ROWS - len(rows), d), F32))
    return jnp.concatenate(rows, axis=0).astype(F32)


def _unpack_small(pack, shapes):
    out = {n: pack[i].reshape(shapes[n]) for i, n in enumerate(_SMALL_ROWS)}
    misc = pack[len(_SMALL_ROWS)]
    out["q_norm_g"] = misc[:HEAD_DIM].reshape(shapes["q_norm_g"])
    out["k_norm_g"] = misc[HEAD_DIM:2 * HEAD_DIM].reshape(shapes["k_norm_g"])
    out["sink_logits"] = misc[2 * HEAD_DIM:2 * HEAD_DIM + N_HEADS_B].reshape(shapes["sink_logits"])
    out["rel_bias_table"] = misc[3 * HEAD_DIM:3 * HEAD_DIM + N_BUCKETS * N_HEADS_B].reshape(shapes["rel_bias_table"])
    return out


_WEIGHTS = ["attn_norm_g", "w_in", "q_norm_g", "k_norm_g", "sink_logits", "w_out", "mlp_norm_g", "w_up", "w_down",
            "ple_w", "ple_norm_g", "gate_norm_g", "w_gate", "rel_bias_table", "final_norm_g"]
_BIG = ["w_in", "w_out", "w_up", "w_down", "ple_w", "w_gate"]


def kernel(x, p, attn_norm_g, w_in, q_norm_g, k_norm_g, sink_logits, w_out, mlp_norm_g, w_up, w_down, ple_w, ple_norm_g, gate_norm_g, w_gate, rel_bias_table, final_norm_g, loss_target, m_attn_norm_g, m_w_in, m_q_norm_g, m_k_norm_g, m_sink_logits, m_w_out, m_mlp_norm_g, m_w_up, m_w_down, m_ple_w, m_ple_norm_g, m_gate_norm_g, m_w_gate, m_rel_bias_table, m_final_norm_g, v_attn_norm_g, v_w_in, v_q_norm_g, v_k_norm_g, v_sink_logits, v_w_out, v_mlp_norm_g, v_w_up, v_w_down, v_ple_w, v_ple_norm_g, v_gate_norm_g, v_w_gate, v_rel_bias_table, v_final_norm_g):
    given = dict(locals())
    w = {n: given[n] for n in _WEIGHTS}
    m = {n: given["m_" + n] for n in _WEIGHTS}
    v = {n: given["v_" + n] for n in _WEIGHTS}
    d = x.shape[-1]

    shards = {n: w[n][0] for n in _BIG}
    small = {
        "attn_norm_g": w["attn_norm_g"], "mlp_norm_g": w["mlp_norm_g"], "ple_norm_g": w["ple_norm_g"],
        "gate_norm_g": w["gate_norm_g"], "final_norm_g": w["final_norm_g"].reshape(1, d),
        "q_norm_g": w["q_norm_g"], "k_norm_g": w["k_norm_g"], "sink_logits": w["sink_logits"],
        "rel_bias_table": w["rel_bias_table"],
    }

    def update(n, parts):
        res = _sum_adamw(parts, w[n][0], m[n][0], v[n][0], name="adamw_" + n)
        return [t.reshape(w[n].shape) for t in res]

    loss_part, grad_x, big, small_g = _local_step(x[0], p[0, 0], loss_target[0], shards, small, update)
    grads, deltas, new_m, new_v = [{n: big[n][i] for n in _BIG} for i in range(4)]

    shapes = {n: w[n].shape for n in _WEIGHTS if n not in _BIG}
    pack = _pack_small(small_g, d)
    pack = pack.at[_PACK_ROWS - 1, :1].add(0.0 * grads["w_in"][0, 0, :1])
    pack = pack.at[_PACK_ROWS - 1, 1].set(loss_part[0, 0])
    g_small = _allreduce_small(pack)
    loss = g_small[_PACK_ROWS - 1, 1]
    d_small, m_small, v_small = _adamw_small(g_small, _pack_small(w, d), _pack_small(m, d), _pack_small(v, d))
    grads.update(_unpack_small(g_small, shapes))
    deltas.update(_unpack_small(d_small, shapes))
    new_m.update(_unpack_small(m_small, shapes))
    new_v.update(_unpack_small(v_small, shapes))

    return (loss, grad_x[None], *[grads[n] for n in _WEIGHTS], *[deltas[n] for n in _WEIGHTS],
            *[new_m[n] for n in _WEIGHTS], *[new_v[n] for n in _WEIGHTS])
```

```python
import functools
import math

import jax
import jax.numpy as jnp
from jax import lax
from jax.experimental import pallas as pl
from jax.experimental.pallas import tpu as pltpu

F32 = jnp.float32
BF16 = jnp.bfloat16

HEAD_DIM = 128
N_HEADS_A = 8
N_KV_A = 2
N_HEADS_B = 8
N_KV_B = 2
GROUP = 4
GRID_W = 64
BLOCK_Q = 128
WINDOW = 128
N_BUCKETS = 32
MAX_DISTANCE = 128
ROPE_THETA = 10000.0
EPS = 1e-6
NEG_INF = -1e30
ATT_SCALE = HEAD_DIM ** -0.5
LOG2E = math.log2(math.e)
LN2 = math.log(2.0)
Q_SCALE = ATT_SCALE * LOG2E
PAD_LO, PAD_HI = 256, 128
ADAM_LR = 0.001
ADAM_B1 = 0.9
ADAM_B2 = 0.999
ADAM_EPS = 1e-08
ADAM_WD = 0.01
ADAM_STEP = 10

N_CHIPS = 4
N_DEV = 8
COL_QA, COL_KA, COL_VA, COL_QB, COL_KB, COL_VB = 0, 8, 10, 12, 20, 22

VMEM_LIMIT = 52 * 1024 * 1024


def _params(sem=None, collective_id=None):
    return pltpu.CompilerParams(dimension_semantics=sem, vmem_limit_bytes=VMEM_LIMIT, collective_id=collective_id)


_ANY = pl.BlockSpec(memory_space=pl.ANY)
_MESH = pl.DeviceIdType.MESH
SIBLING_BARRIER_ID = 1


def _sibling():
    return (lax.axis_index("x"), lax.axis_index("y"), 1 - lax.axis_index("c"))


class _Comm:
    def __init__(self, inputs, out_shapes, sems, start, finish, aliases=None):
        self.inputs, self.out_shapes, self.sems = list(inputs), list(out_shapes), list(sems)
        self.start, self.finish, self.aliases = start, finish, dict(aliases or {})


def _call(body, *, name, grid, in_specs, out_specs, out_shape, args, scratch_shapes=(), sem=None, comm=None,
          after=None, aliases=None):
    in_specs, out_specs, out_shape = list(in_specs), list(out_specs), list(out_shape)
    scratch_shapes = list(scratch_shapes)
    n_in, n_out, n_sc = len(in_specs), len(out_specs), len(scratch_shapes)
    behind = [] if after is None else [after]
    aliases = dict(aliases or {})
    if comm is None:
        res = pl.pallas_call(
            (lambda *refs: body(*refs[:n_in], *refs[n_in + len(behind):])) if behind else body,
            name=name, grid=grid, in_specs=in_specs + [_ANY] * len(behind), out_specs=out_specs,
            out_shape=out_shape, scratch_shapes=scratch_shapes, input_output_aliases=aliases,
            compiler_params=_params(sem))(*args, *behind)
        return list(res), []
    c_in, c_out = len(comm.inputs), len(comm.out_shapes)

    def hosted(*refs):
        pos = [0]

        def take(n):
            pos[0] += n
            return refs[pos[0] - n:pos[0]]

        ins, c_ins, _, outs, c_outs, scr = (take(n_in), take(c_in), take(len(behind)), take(n_out), take(c_out),
                                            take(n_sc))
        c_sems = refs[pos[0]:]
        ids = [pl.program_id(a) for a in range(len(grid))]
        first = functools.reduce(jnp.logical_and, [i == 0 for i in ids])
        last = functools.reduce(jnp.logical_and, [i == g - 1 for i, g in zip(ids, grid)])

        @pl.when(first)
        def _():
            barrier = pltpu.get_barrier_semaphore()
            pl.semaphore_signal(barrier, inc=1, device_id=_sibling(), device_id_type=_MESH)
            pl.semaphore_wait(barrier, 1)
            comm.start(c_ins, c_outs, c_sems)

        body(*ins, *outs, *scr)

        @pl.when(last)
        def _():
            comm.finish(c_ins, c_outs, c_sems)

    res = pl.pallas_call(
        hosted, name=name, grid=grid, in_specs=in_specs + [_ANY] * (c_in + len(behind)),
        out_specs=out_specs + [_ANY] * c_out,
        out_shape=out_shape + comm.out_shapes, scratch_shapes=scratch_shapes + comm.sems,
        input_output_aliases={**aliases, **{n_in + i: n_out + o for i, o in comm.aliases.items()}},
        compiler_params=_params(("arbitrary",) * len(grid), SIBLING_BARRIER_ID))(*args, *comm.inputs, *behind)
    return list(res[:n_out]), list(res[n_out:])


def _matmul(a, b, *, mode, out_dtypes, name, epilogue=None, extras=(), bm=1024, bn=1024, bk=2048,
            out_stack=0, comm=None, after=None, vecs=()):
    stacked = b.ndim == 3
    if mode == "nn":
        m, k = a.shape
        if stacked:
            nj, kb, ns = b.shape
            n, ks = nj * ns, k
        else:
            kb, n = b.shape
            ns, ks = n, k
        dn = (((1,), (0,)), ((), ()))
    elif mode == "nt":
        m, k = a.shape
        if stacked:
            nj, n, ks = b.shape
            kb = nj * ks
        else:
            n, kb = b.shape
            ks = kb
        ns = n
        dn = (((1,), (1,)), ((), ()))
    else:
        k, m = a.shape
        kb, n = b.shape
        ns, ks = n, k
        dn = (((0,), (0,)), ((), ()))
    assert k == kb and not (stacked and mode == "tn")
    ns_out = n // out_stack if out_stack else n
    per_blk = min(bk, k) // ks if stacked and mode == "nt" and bk > ks else 0
    bm, bn, bk = min(bm, m), min(bn, ns, ns_out), per_blk * ks if per_blk else min(bk, ks)
    assert m % bm == 0 and ns % bn == 0 and ns_out % bn == 0 and (k % bk == 0 if per_blk else ks % bk == 0)
    gm, gn, gk = m // bm, n // bn, k // bk

    if mode == "tn":
        a_spec = pl.BlockSpec((bk, bm), lambda i, j, q: (q, i))
    else:
        a_spec = pl.BlockSpec((bm, bk), lambda i, j, q: (i, q))
    if mode == "nt":
        if per_blk:
            b_spec = pl.BlockSpec((per_blk, bn, ks), lambda i, j, q: (q, j, 0))
        elif stacked:
            per = ks // bk
            b_spec = pl.BlockSpec((None, bn, bk), lambda i, j, q: (q // per, j, q % per))
        else:
            b_spec = pl.BlockSpec((bn, bk), lambda i, j, q: (j, q))
    else:
        if stacked:
            per = ns // bn
            b_spec = pl.BlockSpec((None, bk, bn), lambda i, j, q: (j // per, q, j % per))
        else:
            b_spec = pl.BlockSpec((bk, bn), lambda i, j, q: (q, j))
    ex_spec = pl.BlockSpec((bm, bn), lambda i, j, q: (i, j))
    if out_stack:
        per_o = ns_out // bn
        o_spec = pl.BlockSpec((None, bm, bn), lambda i, j, q: (j // per_o, i, j % per_o))
        o_shape = (out_stack, m, ns_out)
    else:
        o_spec = ex_spec
        o_shape = (m, n)
    n_ex, n_out = len(extras) + len(vecs), len(out_dtypes)

    def body(a_ref, b_ref, *rest):
        ex, outs = rest[:n_ex], rest[n_ex:n_ex + n_out]
        if per_blk:
            part = sum(lax.dot_general(a_ref[:, t * ks:(t + 1) * ks], b_ref[t], dn, preferred_element_type=F32)
                       for t in range(per_blk))
        else:
            part = lax.dot_general(a_ref[...], b_ref[...], dn, preferred_element_type=F32)

        def finish(acc):
            res = epilogue(acc, *[e[...] for e in ex]) if epilogue else (acc,)
            for o, r in zip(outs, res):
                o[...] = r.astype(o.dtype)

        if gk == 1:
            finish(part)
        else:
            acc_ref = rest[-1]
            q = pl.program_id(2)

            @pl.when(q == 0)
            def _():
                acc_ref[...] = part

            @pl.when(q > 0)
            def _():
                acc_ref[...] += part

            @pl.when(q == gk - 1)
            def _():
                finish(acc_ref[...])

    res, c_res = _call(
        body, name=name, grid=(gm, gn, gk),
        in_specs=[a_spec, b_spec] + [ex_spec] * len(extras)
        + [pl.BlockSpec((1, bn), lambda i, j, q: (0, j))] * len(vecs),
        out_specs=[o_spec] * n_out,
        out_shape=[jax.ShapeDtypeStruct(o_shape, dt) for dt in out_dtypes],
        scratch_shapes=[pltpu.VMEM((bm, bn), F32)] if gk > 1 else [],
        sem=("parallel", "parallel", "arbitrary"), args=(a, b, *extras, *vecs), comm=comm, after=after)
    res = res[0] if n_out == 1 else res
    return res if comm is None else (res, c_res)


def _rms_fwd(x, g, *, name, tm=256, comm=None):
    s, d = x.shape
    tm = min(tm, s)

    def body(x_ref, g_ref, o_ref):
        xf = x_ref[...]
        r = lax.rsqrt(jnp.mean(xf * xf, axis=-1, keepdims=True) + EPS)
        o_ref[...] = (xf * r * g_ref[...]).astype(o_ref.dtype)

    res, c_res = _call(
        body, name=name, grid=(s // tm,),
        in_specs=[pl.BlockSpec((tm, d), lambda i: (i, 0)), pl.BlockSpec((1, d), lambda i: (0, 0))],
        out_specs=[pl.BlockSpec((tm, d), lambda i: (i, 0))],
        out_shape=[jax.ShapeDtypeStruct((s, d), BF16)],
        sem=("parallel",), args=(x, g), comm=comm)
    return res[0] if comm is None else (res[0], c_res)


def _rms_bwd(x, dy, g, add, *, name, want_bf16, tm=256):
    s, d = x.shape
    tm = min(tm, s)

    def body(x_ref, dy_ref, g_ref, add_ref, dx_ref, *rest):
        dg_ref = rest[-1]
        i = pl.program_id(0)
        xf = x_ref[...]
        dyf = dy_ref[...].astype(F32)
        r = lax.rsqrt(jnp.mean(xf * xf, axis=-1, keepdims=True) + EPS)
        xh = xf * r
        dyg = dyf * g_ref[...]
        dx = r * (dyg - xh * jnp.mean(dyg * xh, axis=-1, keepdims=True))
        tot = add_ref[...] + dx
        dx_ref[...] = tot
        if want_bf16:
            rest[0][...] = tot.astype(BF16)
        part = jnp.sum(dyf * xh, axis=0, keepdims=True)

        @pl.when(i == 0)
        def _():
            dg_ref[...] = part

        @pl.when(i > 0)
        def _():
            dg_ref[...] += part

    row = pl.BlockSpec((tm, d), lambda i: (i, 0))
    vec = pl.BlockSpec((1, d), lambda i: (0, 0))
    out_specs = [row] + ([row] if want_bf16 else []) + [vec]
    out_shape = [jax.ShapeDtypeStruct((s, d), F32)]
    if want_bf16:
        out_shape.append(jax.ShapeDtypeStruct((s, d), BF16))
    out_shape.append(jax.ShapeDtypeStruct((1, d), F32))
    return pl.pallas_call(
        body,
        name=name,
        grid=(s // tm,),
        in_specs=[row, row, vec, row],
        out_specs=out_specs,
        out_shape=out_shape,
        compiler_params=_params(("arbitrary",)),
    )(x, dy, g, add)


def _tail(h2, gate, pp, target, g_ple, g_final, *, tm=256):
    s, d = h2.shape
    tm = min(tm, s)

    def body(h2_ref, gate_ref, pp_ref, t_ref, gp_ref, gf_ref, dh3_ref, dz_ref, dpp_ref, dgf_ref, dgp_ref, loss_ref):
        i = pl.program_id(0)
        ppf = pp_ref[...]
        gate_v = gate_ref[...]
        r_p = lax.rsqrt(jnp.mean(ppf * ppf, axis=-1, keepdims=True) + EPS)
        eh = ppf * r_p
        e = eh * gp_ref[...]
        h3 = h2_ref[...] + gate_v * e
        r_f = lax.rsqrt(jnp.mean(h3 * h3, axis=-1, keepdims=True) + EPS)
        yh = h3 * r_f
        diff = yh * gf_ref[...] - t_ref[...]
        loss_part = 0.5 * jnp.sum(jnp.mean(diff * diff, axis=-1, keepdims=True), axis=0, keepdims=True)
        dy = diff / d
        dgf = jnp.sum(dy * yh, axis=0, keepdims=True)
        dyg = dy * gf_ref[...]
        dh3 = r_f * (dyg - yh * jnp.mean(dyg * yh, axis=-1, keepdims=True))
        dh3_ref[...] = dh3
        de = dh3 * gate_v
        dz_ref[...] = (dh3 * e * gate_v * (1.0 - gate_v)).astype(BF16)
        dgp = jnp.sum(de * eh, axis=0, keepdims=True)
        deg = de * gp_ref[...]
        dpp_ref[...] = (r_p * (deg - eh * jnp.mean(deg * eh, axis=-1, keepdims=True))).astype(BF16)
        loss_row = jnp.broadcast_to(loss_part, (1, 128))

        @pl.when(i == 0)
        def _():
            dgf_ref[...] = dgf
            dgp_ref[...] = dgp
            loss_ref[...] = loss_row

        @pl.when(i > 0)
        def _():
            dgf_ref[...] += dgf
            dgp_ref[...] += dgp
            loss_ref[...] += loss_row

    row = pl.BlockSpec((tm, d), lambda i: (i, 0))
    vec = pl.BlockSpec((1, d), lambda i: (0, 0))
    return pl.pallas_call(
        body,
        name="tail_fwd_bwd",
        grid=(s // tm,),
        in_specs=[row, row, row, row, vec, vec],
        out_specs=[row, row, row, vec, vec, pl.BlockSpec((1, 128), lambda i: (0, 0))],
        out_shape=[
            jax.ShapeDtypeStruct((s, d), F32),
            jax.ShapeDtypeStruct((s, d), BF16),
            jax.ShapeDtypeStruct((s, d), BF16),
            jax.ShapeDtypeStruct((1, d), F32),
            jax.ShapeDtypeStruct((1, d), F32),
            jax.ShapeDtypeStruct((1, 128), F32),
        ],
        compiler_params=_params(("arbitrary",)),
    )(h2, gate, pp, target, g_ple, g_final)


def _rope_tables(s):
    rows = s // GRID_W
    half = HEAD_DIM // 2
    inv_freq = ROPE_THETA ** (-jnp.arange(0, half, 2, dtype=F32) / half)
    ang_r = jnp.arange(rows, dtype=jnp.int32).astype(F32)[:, None] * inv_freq
    ang_c = jnp.arange(GRID_W, dtype=jnp.int32).astype(F32)[:, None] * inv_freq
    cr, sr = (jnp.repeat(t, GRID_W, axis=0) for t in (jnp.cos(ang_r), jnp.sin(ang_r)))
    cc, sc = (jnp.tile(t, (rows, 1)) for t in (jnp.cos(ang_c), jnp.sin(ang_c)))
    cos_t = jnp.concatenate([cr, cr, cc, cc], axis=-1)
    sin_t = jnp.concatenate([-sr, sr, -sc, sc], axis=-1)
    return cos_t, sin_t


def _low_quarters(shape):
    return (lax.broadcasted_iota(jnp.int32, shape, len(shape) - 1) % 64) < 32


def _swap_quarters(x, low):
    up = pltpu.roll(x, HEAD_DIM - 32, x.ndim - 1)
    down = pltpu.roll(x, 32, x.ndim - 1)
    return jnp.where(low, up, down)


def _cols(first, count=1):
    return slice(first * HEAD_DIM, (first + count) * HEAD_DIM)


def _qk_prep(proj, g_q, g_k, cos_t, sin_t, *, tm=256, comm=None):
    s, n = proj.shape
    tm = min(tm, s)

    def body(x_ref, gq_ref, gk_ref, c_ref, s_ref, o_ref):
        cos_v, sin_v = c_ref[...], s_ref[...]
        low = _low_quarters(cos_v.shape)
        for h in range(COL_VA):
            x = x_ref[:, _cols(h)]
            g = gq_ref[...] if h < COL_KA else gk_ref[...]
            xn = x * lax.rsqrt(jnp.mean(x * x, axis=-1, keepdims=True) + EPS) * g
            xr = xn * cos_v + _swap_quarters(xn, low) * sin_v
            if h < COL_KA:
                xr = xr * Q_SCALE
            o_ref[:, _cols(h)] = xr.astype(BF16)
        o_ref[:, _cols(COL_VA, 2)] = x_ref[:, _cols(COL_VA, 2)].astype(BF16)
        o_ref[:, _cols(COL_QB, N_HEADS_B)] = (x_ref[:, _cols(COL_QB, N_HEADS_B)] * Q_SCALE).astype(BF16)
        o_ref[:, _cols(COL_KB, 4)] = x_ref[:, _cols(COL_KB, 4)].astype(BF16)

    row = pl.BlockSpec((tm, n), lambda i: (i, 0))
    tab = pl.BlockSpec((tm, HEAD_DIM), lambda i: (i, 0))
    vec = pl.BlockSpec((1, HEAD_DIM), lambda i: (0, 0))
    res, c_res = _call(
        body, name="qk_prep", grid=(s // tm,),
        in_specs=[row, vec, vec, tab, tab],
        out_specs=[row],
        out_shape=[jax.ShapeDtypeStruct((s, n), BF16)],
        sem=("parallel",), args=(proj, g_q, g_k, cos_t, sin_t), comm=comm)
    return res[0] if comm is None else (res[0], c_res)


def _qk_bwd(dqa, dka, dva, dqb, dkpad, dvpad, proj, g_q, g_k, cos_t, sin_t, *, comm=None, after=None):
    s, n = proj.shape
    tm = min(PAD_LO, s)
    assert PAD_LO % tm == 0
    lo = PAD_LO // tm

    def body(dqa_ref, dka_ref, dva_ref, dqb_ref, dkb_ref, dvb_ref, x_ref, gq_ref, gk_ref, c_ref, s_ref,
             o_ref, dgq_ref, dgk_ref):
        i = pl.program_id(0)
        cos_v, sin_v = c_ref[...], s_ref[...]
        low = _low_quarters(cos_v.shape)

        def head(d, x, g):
            dn = d * cos_v + _swap_quarters(d * sin_v, low)
            r = lax.rsqrt(jnp.mean(x * x, axis=-1, keepdims=True) + EPS)
            xh = x * r
            dng = dn * g
            dx = r * (dng - xh * jnp.mean(dng * xh, axis=-1, keepdims=True))
            return dx.astype(BF16), jnp.sum(dn * xh, axis=0, keepdims=True)

        acc_q = jnp.zeros((1, HEAD_DIM), F32)
        acc_k = jnp.zeros((1, HEAD_DIM), F32)
        for h in range(N_HEADS_A):
            o_ref[:, _cols(h)], part = head(dqa_ref[:, _cols(h)] * ATT_SCALE, x_ref[:, _cols(h)], gq_ref[...])
            acc_q = acc_q + part
        for h in range(N_KV_A):
            o_ref[:, _cols(COL_KA + h)], part = head(dka_ref[:, _cols(h)] * LN2, x_ref[:, _cols(COL_KA + h)],
                                                     gk_ref[...])
            acc_k = acc_k + part
        o_ref[:, _cols(COL_VA, 2)] = dva_ref[...].astype(BF16)
        o_ref[:, _cols(COL_QB, N_HEADS_B)] = (dqb_ref[...] * ATT_SCALE).astype(BF16)
        o_ref[:, _cols(COL_KB, 2)] = (dkb_ref[...] * LN2).astype(BF16)
        o_ref[:, _cols(COL_VB, 2)] = dvb_ref[...].astype(BF16)

        @pl.when(i == 0)
        def _():
            dgq_ref[...] = acc_q
            dgk_ref[...] = acc_k

        @pl.when(i > 0)
        def _():
            dgq_ref[...] += acc_q
            dgk_ref[...] += acc_k

    def rows(width, shift=0):
        return pl.BlockSpec((tm, width), lambda i: (i + shift, 0))

    kv_w = N_KV_A * HEAD_DIM
    q_w = N_HEADS_A * HEAD_DIM
    vec = pl.BlockSpec((1, HEAD_DIM), lambda i: (0, 0))
    res, c_res = _call(
        body, name="qk_bwd", grid=(s // tm,),
        in_specs=[rows(q_w), rows(kv_w), rows(kv_w), rows(q_w), rows(kv_w, lo), rows(kv_w, lo), rows(n),
                  vec, vec, rows(HEAD_DIM), rows(HEAD_DIM)],
        out_specs=[rows(n), vec, vec],
        out_shape=[
            jax.ShapeDtypeStruct((s, n), BF16),
            jax.ShapeDtypeStruct((1, HEAD_DIM), F32),
            jax.ShapeDtypeStruct((1, HEAD_DIM), F32),
        ],
        sem=("arbitrary",), args=(dqa, dka, dva, dqb, dkpad, dvpad, proj, g_q, g_k, cos_t, sin_t), comm=comm,
        after=after)
    return res if comm is None else (res, c_res)


_NT = (((1,), (1,)), ((), ()))
_TN = (((0,), (0,)), ((), ()))


def _attn_a_fwd(pb, *, tq=4096, sub=256, comm=None, after=None):
    s = pb.shape[0]
    tq = min(tq, s)
    sub = min(sub, tq)

    def body(q_ref, k_ref, v_ref, o_ref, lse_ref):
        k = k_ref[...]
        v = v_ref[...]
        for r in range(tq // sub):
            rows = pl.ds(r * sub, sub)
            sc = lax.dot_general(q_ref[rows, :], k, _NT, preferred_element_type=F32)
            m = jnp.max(sc, axis=-1, keepdims=True)
            p = jnp.exp2(sc - m)
            l = jnp.sum(p, axis=-1, keepdims=True)
            o = jnp.dot(p.astype(BF16), v, preferred_element_type=F32)
            o_ref[rows, :] = (o / l).astype(BF16)
            lse_ref[rows, :] = jnp.broadcast_to(m + jnp.log2(l), (sub, HEAD_DIM))

    res, c_res = _call(
        body, name="attn_a_fwd", grid=(N_HEADS_A, s // tq),
        in_specs=[
            pl.BlockSpec((tq, HEAD_DIM), lambda h, i: (i, COL_QA + h)),
            pl.BlockSpec((s, HEAD_DIM), lambda h, i: (0, COL_KA + h // GROUP)),
            pl.BlockSpec((s, HEAD_DIM), lambda h, i: (0, COL_VA + h // GROUP)),
        ],
        out_specs=[
            pl.BlockSpec((tq, HEAD_DIM), lambda h, i: (i, h)),
            pl.BlockSpec((None, tq, HEAD_DIM), lambda h, i: (h, i, 0)),
        ],
        out_shape=[
            jax.ShapeDtypeStruct((s, (N_HEADS_A + N_HEADS_B) * HEAD_DIM), BF16),
            jax.ShapeDtypeStruct((N_HEADS_A, s, HEAD_DIM), F32),
        ],
        sem=("parallel", "parallel"), args=(pb, pb, pb), comm=comm, after=after)
    return res if comm is None else (res, c_res)


def _attn_a_bwd(pb, att, datt, lse, *, tq=1024, sub=256, comm=None):
    s = pb.shape[0]
    tq = min(tq, s)
    sub = min(sub, tq)

    def body(q_ref, k_ref, v_ref, o_ref, do_ref, lse_ref, dq_ref, dk_ref, dv_ref):
        first = jnp.logical_and(pl.program_id(1) == 0, pl.program_id(2) == 0)
        k = k_ref[...]
        v = v_ref[...]
        dk = dv = None
        for r in range(tq // sub):
            rows = pl.ds(r * sub, sub)
            q = q_ref[rows, :]
            do = do_ref[rows, :]
            sc = lax.dot_general(q, k, _NT, preferred_element_type=F32)
            p = jnp.exp2(sc - lse_ref[rows, :][:, :1])
            dp = lax.dot_general(do, v, _NT, preferred_element_type=F32)
            delta = jnp.sum(do.astype(F32) * o_ref[rows, :].astype(F32), axis=-1, keepdims=True)
            ds = (p * (dp - delta)).astype(BF16)
            dq_ref[rows, :] = jnp.dot(ds, k, preferred_element_type=F32)
            dk_r = lax.dot_general(ds, q, _TN, preferred_element_type=F32)
            dv_r = lax.dot_general(p.astype(BF16), do, _TN, preferred_element_type=F32)
            dk = dk_r if dk is None else dk + dk_r
            dv = dv_r if dv is None else dv + dv_r

        @pl.when(first)
        def _():
            dk_ref[...] = dk
            dv_ref[...] = dv

        @pl.when(jnp.logical_not(first))
        def _():
            dk_ref[...] += dk
            dv_ref[...] += dv

    qmap = lambda kv, g, i: (i, kv * GROUP + g)
    res, c_res = _call(
        body, name="attn_a_bwd", grid=(N_KV_A, GROUP, s // tq),
        in_specs=[
            pl.BlockSpec((tq, HEAD_DIM), lambda kv, g, i: (i, COL_QA + kv * GROUP + g)),
            pl.BlockSpec((s, HEAD_DIM), lambda kv, g, i: (0, COL_KA + kv)),
            pl.BlockSpec((s, HEAD_DIM), lambda kv, g, i: (0, COL_VA + kv)),
            pl.BlockSpec((tq, HEAD_DIM), qmap),
            pl.BlockSpec((tq, HEAD_DIM), qmap),
            pl.BlockSpec((None, tq, HEAD_DIM), lambda kv, g, i: (kv * GROUP + g, i, 0)),
        ],
        out_specs=[
            pl.BlockSpec((tq, HEAD_DIM), qmap),
            pl.BlockSpec((s, HEAD_DIM), lambda kv, g, i: (0, kv)),
            pl.BlockSpec((s, HEAD_DIM), lambda kv, g, i: (0, kv)),
        ],
        out_shape=[
            jax.ShapeDtypeStruct((s, N_HEADS_A * HEAD_DIM), F32),
            jax.ShapeDtypeStruct((s, N_KV_A * HEAD_DIM), F32),
            jax.ShapeDtypeStruct((s, N_KV_A * HEAD_DIM), F32),
        ],
        sem=("arbitrary", "arbitrary", "arbitrary"), args=(pb, pb, pb, att, datt, lse), comm=comm)
    return res if comm is None else (res, c_res)


def _t5_bucket(rel):
    nb = N_BUCKETS // 2
    ret = jnp.where(rel > 0, nb, 0)
    n = jnp.abs(rel)
    max_exact = nb // 2
    nf = jnp.maximum(n, 1).astype(F32)
    large = max_exact + (jnp.log(nf / max_exact) / math.log(MAX_DISTANCE / max_exact)
                         * (nb - max_exact)).astype(jnp.int32)
    large = jnp.minimum(large, nb - 1)
    return ret + jnp.where(n < max_exact, n, large)


def _band_buckets():
    r = jnp.arange(BLOCK_Q, dtype=jnp.int32)
    j = jnp.arange(3 * BLOCK_Q, dtype=jnp.int32)
    return _t5_bucket((j[None, :] - BLOCK_Q) - r[:, None])


def _band_bias(bucket, table_ref, h):
    acc = jnp.zeros(bucket.shape, F32)
    for b in range(N_BUCKETS):
        acc = jnp.where(bucket == b, table_ref[b, h], acc)
    return acc


GQ = GROUP * BLOCK_Q


def _stack_heads(x):
    return jnp.concatenate([x[:, _cols(g)] for g in range(GROUP)], axis=0)


def _unstack_heads(x):
    return jnp.concatenate([x[g * BLOCK_Q:(g + 1) * BLOCK_Q] for g in range(GROUP)], axis=1)


def _group_bias(bucket, table_ref, kv):
    r = lax.broadcasted_iota(jnp.int32, (BLOCK_Q, 3 * BLOCK_Q), 0)
    j = lax.broadcasted_iota(jnp.int32, (BLOCK_Q, 3 * BLOCK_Q), 1)
    inside = jnp.abs(j - BLOCK_Q - r) <= WINDOW
    return jnp.concatenate([jnp.where(inside, _band_bias(bucket, table_ref, kv * GROUP + g) * LOG2E, NEG_INF)
                            for g in range(GROUP)], axis=0)


def _group_sink(sink_ref, kv):
    head = lax.broadcasted_iota(jnp.int32, (GQ, 1), 0) // BLOCK_Q
    snk = jnp.zeros((GQ, 1), F32)
    for g in range(GROUP):
        snk = jnp.where(head == g, sink_ref[0, kv * GROUP + g] * LOG2E, snk)
    return snk


def _band_mask(n, s):
    kabs = n * BLOCK_Q + lax.broadcasted_iota(jnp.int32, (1, 3 * BLOCK_Q), 1) - BLOCK_Q
    return (kabs >= 0) & (kabs < s)


def _band_start(n):
    return pl.multiple_of(n * BLOCK_Q + (PAD_LO - BLOCK_Q), BLOCK_Q)


def _attn_b_fwd(pb, kpad, vpad, bucket, table, sink, att, *, comm=None, after=None):
    s = pb.shape[0]
    nblk = s // BLOCK_Q
    sp = kpad.shape[0]

    def body(table_ref, sink_ref, q0_ref, q1_ref, k_ref, v_ref, bucket_ref, _, o_ref, lse_ref, bias_ref):
        n = pl.program_id(0)

        @pl.when(n == 0)
        def _():
            for kv in range(N_KV_B):
                bias_ref[kv * GQ:(kv + 1) * GQ, :] = _group_bias(bucket_ref[...], table_ref, kv)

        band = pl.ds(_band_start(n), 3 * BLOCK_Q)
        mask = _band_mask(n, s)
        for kv, q_ref in enumerate((q0_ref, q1_ref)):
            kb = k_ref[band, _cols(kv)]
            vb = v_ref[band, _cols(kv)]
            sc = lax.dot_general(_stack_heads(q_ref[...]), kb, _NT, preferred_element_type=F32)
            sc = jnp.where(mask, sc + bias_ref[kv * GQ:(kv + 1) * GQ, :], NEG_INF)
            snk = _group_sink(sink_ref, kv)
            m = jnp.maximum(jnp.max(sc, axis=-1, keepdims=True), snk)
            p = jnp.exp2(sc - m)
            l = jnp.sum(p, axis=-1, keepdims=True) + jnp.exp2(snk - m)
            o = jnp.dot(p.astype(BF16), vb, preferred_element_type=F32)
            o_ref[:, _cols(kv * GROUP, GROUP)] = _unstack_heads((o / l).astype(BF16))
            lse = m + jnp.log2(l)
            for g in range(GROUP):
                lse_ref[kv * GROUP + g] = jnp.broadcast_to(lse[g * BLOCK_Q:(g + 1) * BLOCK_Q], (BLOCK_Q, HEAD_DIM))

    smem = pl.BlockSpec(memory_space=pltpu.SMEM)
    wide = GROUP * HEAD_DIM
    whole = pl.BlockSpec((sp, N_KV_B * HEAD_DIM), lambda n: (0, 0))
    res, c_res = _call(
        body, name="attn_b_fwd", grid=(nblk,),
        in_specs=[
            smem,
            smem,
            pl.BlockSpec((BLOCK_Q, wide), lambda n: (n, COL_QB // GROUP)),
            pl.BlockSpec((BLOCK_Q, wide), lambda n: (n, COL_QB // GROUP + 1)),
            whole,
            whole,
            pl.BlockSpec((BLOCK_Q, 3 * BLOCK_Q), lambda n: (0, 0)),
            _ANY,
        ],
        out_specs=[
            pl.BlockSpec((BLOCK_Q, N_HEADS_B * HEAD_DIM), lambda n: (n, 1)),
            pl.BlockSpec((N_HEADS_B, BLOCK_Q, HEAD_DIM), lambda n: (0, n, 0)),
        ],
        out_shape=[
            jax.ShapeDtypeStruct(att.shape, BF16),
            jax.ShapeDtypeStruct((N_HEADS_B, s, HEAD_DIM), F32),
        ],
        scratch_shapes=[pltpu.VMEM((N_KV_B * GQ, 3 * BLOCK_Q), F32)],
        sem=("arbitrary",), args=(table, sink, pb, pb, kpad, vpad, bucket, att), comm=comm, after=after,
        aliases={7: 0})
    return res if comm is None else (res, c_res)


def _attn_b_bwd(pb, kpad, vpad, att, datt, lse, bucket, table, sink, *, comm=None, after=None):
    s = pb.shape[0]
    nblk = s // BLOCK_Q
    sp = kpad.shape[0]

    def body(table_ref, sink_ref, q0_ref, q1_ref, k_ref, v_ref, o_ref, do_ref, lse_ref, bucket_ref,
             dq_ref, dk_ref, dv_ref, dtab_ref, dsink_ref, bias_ref, dbias_ref):
        n = pl.program_id(0)

        @pl.when(n == 0)
        def _():
            dk_ref[...] = jnp.zeros_like(dk_ref)
            dv_ref[...] = jnp.zeros_like(dv_ref)
            dbias_ref[...] = jnp.zeros_like(dbias_ref)
            dsink_ref[...] = jnp.zeros_like(dsink_ref)
            for kv in range(N_KV_B):
                bias_ref[kv * GQ:(kv + 1) * GQ, :] = _group_bias(bucket_ref[...], table_ref, kv)

        band = pl.ds(_band_start(n), 3 * BLOCK_Q)
        mask = _band_mask(n, s)
        for kv, q_ref in enumerate((q0_ref, q1_ref)):
            wide_cols = _cols(kv * GROUP, GROUP)
            q = _stack_heads(q_ref[...])
            do = _stack_heads(do_ref[:, wide_cols])
            o = _stack_heads(o_ref[:, wide_cols])
            kb = k_ref[band, _cols(kv)]
            vb = v_ref[band, _cols(kv)]
            lse = jnp.concatenate([lse_ref[kv * GROUP + g][:, :1] for g in range(GROUP)], axis=0)
            sc = lax.dot_general(q, kb, _NT, preferred_element_type=F32)
            sc = jnp.where(mask, sc + bias_ref[kv * GQ:(kv + 1) * GQ, :], NEG_INF)
            p = jnp.exp2(sc - lse)
            dp = lax.dot_general(do, vb, _NT, preferred_element_type=F32)
            delta = jnp.sum(do.astype(F32) * o.astype(F32), axis=-1, keepdims=True)
            ds = p * (dp - delta)
            dsb = ds.astype(BF16)
            dq_ref[:, wide_cols] = _unstack_heads(jnp.dot(dsb, kb, preferred_element_type=F32))
            dk_ref[band, _cols(kv)] += lax.dot_general(dsb, q, _TN, preferred_element_type=F32)
            dv_ref[band, _cols(kv)] += lax.dot_general(p.astype(BF16), do, _TN, preferred_element_type=F32)
            dbias_ref[kv * GQ:(kv + 1) * GQ, :] += ds
            sink_part = -jnp.exp2(_group_sink(sink_ref, kv) - lse) * delta
            for g in range(GROUP):
                rows = slice(g * BLOCK_Q, (g + 1) * BLOCK_Q)
                dsink_ref[kv * GROUP + g] += jnp.broadcast_to(
                    jnp.sum(sink_part[rows], axis=0, keepdims=True), (1, HEAD_DIM))

        @pl.when(n == nblk - 1)
        def _():
            bucket_v = bucket_ref[...]
            row = lax.broadcasted_iota(jnp.int32, (N_BUCKETS, HEAD_DIM), 0)
            for h in range(N_HEADS_B):
                acc = dbias_ref[h * BLOCK_Q:(h + 1) * BLOCK_Q, :]
                tot = jnp.zeros((N_BUCKETS, HEAD_DIM), F32)
                for b in range(N_BUCKETS):
                    tot = jnp.where(row == b, jnp.sum(jnp.where(bucket_v == b, acc, 0.0), keepdims=True), tot)
                dtab_ref[h] = tot

    smem = pl.BlockSpec(memory_space=pltpu.SMEM)
    wide = GROUP * HEAD_DIM
    whole = pl.BlockSpec((sp, N_KV_B * HEAD_DIM), lambda n: (0, 0))
    group_b = pl.BlockSpec((BLOCK_Q, N_HEADS_B * HEAD_DIM), lambda n: (n, 1))
    res, c_res = _call(
        body, name="attn_b_bwd", grid=(nblk,),
        in_specs=[
            smem,
            smem,
            pl.BlockSpec((BLOCK_Q, wide), lambda n: (n, COL_QB // GROUP)),
            pl.BlockSpec((BLOCK_Q, wide), lambda n: (n, COL_QB // GROUP + 1)),
            whole,
            whole,
            group_b,
            group_b,
            pl.BlockSpec((N_HEADS_B, BLOCK_Q, HEAD_DIM), lambda n: (0, n, 0)),
            pl.BlockSpec((BLOCK_Q, 3 * BLOCK_Q), lambda n: (0, 0)),
        ],
        out_specs=[
            pl.BlockSpec((BLOCK_Q, N_HEADS_B * HEAD_DIM), lambda n: (n, 0)),
            whole,
            whole,
            pl.BlockSpec((N_HEADS_B, N_BUCKETS, HEAD_DIM), lambda n: (0, 0, 0)),
            pl.BlockSpec((N_HEADS_B, 1, HEAD_DIM), lambda n: (0, 0, 0)),
        ],
        out_shape=[
            jax.ShapeDtypeStruct((s, N_HEADS_B * HEAD_DIM), F32),
            jax.ShapeDtypeStruct((sp, N_KV_B * HEAD_DIM), F32),
            jax.ShapeDtypeStruct((sp, N_KV_B * HEAD_DIM), F32),
            jax.ShapeDtypeStruct((N_HEADS_B, N_BUCKETS, HEAD_DIM), F32),
            jax.ShapeDtypeStruct((N_HEADS_B, 1, HEAD_DIM), F32),
        ],
        scratch_shapes=[pltpu.VMEM((N_KV_B * GQ, 3 * BLOCK_Q), F32), pltpu.VMEM((N_KV_B * GQ, 3 * BLOCK_Q), F32)],
        sem=("arbitrary",),
        args=(table, sink, pb, pb, kpad, vpad, att, datt, lse, bucket), comm=comm, after=after)
    return res if comm is None else (res, c_res)


def _other_chips(x, y):
    return [(x, 1 - y), (1 - x, y), (1 - x, 1 - y)]


_HBM = pl.BlockSpec(memory_space=pltpu.HBM)
_SEM = pl.BlockSpec(memory_space=pltpu.SEMAPHORE)
_SPLIT = pltpu.CompilerParams(has_side_effects=pltpu.SideEffectType.DATAFLOW_SIDE_EFFECTING)


def _in_hbm(a):
    return pltpu.with_memory_space_constraint(a, pltpu.HBM)


def _my_half(rows):
    c = lax.axis_index("c")
    half = rows // 2
    return pl.ds(pl.multiple_of(c * half, half), half), pl.ds(pl.multiple_of((1 - c) * half, half), half)


def _gather_route(shapes):
    def route(src, land):
        x, y, c = lax.axis_index("x"), lax.axis_index("y"), lax.axis_index("c")
        out = []
        for t, shape in enumerate(shapes):
            mine, _ = _my_half(shape[0])
            for px, py in _other_chips(x, y):
                out.append((src[t].at[mine], land[t].at[2 * x + y, mine], land[t].at[2 * px + py, mine], (px, py, c)))
        return out

    return route


def _exchange_route(n_t):
    def route(src, land):
        x, y, c = lax.axis_index("x"), lax.axis_index("y"), lax.axis_index("c")
        out = []
        for t in range(n_t):
            for px, py in _other_chips(x, y):
                k = 2 * px + py
                out.append((src[t].at[k], land[t].at[2 * (2 * x + y) + c], land[t].at[2 * k + c], (px, py, c)))
        return out

    return route


def _own_slot(shape, dtype, slot, block):
    return lax.dynamic_update_slice(lax.empty(shape, dtype), block[None], (slot,) + (0,) * (len(shape) - 1))


def _split_start(name, srcs, lands, route, after):
    n = len(srcs)

    def body(*refs):
        src, land, send_sems, recv_sems, token = refs[:n], refs[n:2 * n], refs[2 * n + 1], refs[2 * n + 2], refs[-1]
        for i, (src_ref, dst_ref, _, to) in enumerate(route(src, land)):
            pltpu.make_async_remote_copy(src_ref=src_ref, dst_ref=dst_ref, send_sem=send_sems.at[i],
                                         recv_sem=recv_sems.at[i], device_id=to, device_id_type=_MESH).start()
        token[...] = jnp.zeros_like(token)

    sem = pltpu.SemaphoreType.DMA((3 * n,))
    lands = list(lands)
    res = pl.pallas_call(
        body, name=name,
        in_specs=[_HBM] * (2 * n) + [_ANY],
        out_specs=[_SEM, _SEM] + [_HBM] * (2 * n) + [pl.BlockSpec(memory_space=pltpu.VMEM)],
        out_shape=[sem, sem] + [pltpu.HBM(a.shape, a.dtype) for a in list(srcs) + lands]
        + [jax.ShapeDtypeStruct((8, 128), F32)],
        input_output_aliases={i: 2 + i for i in range(2 * n)},
        compiler_params=_SPLIT,
    )(*[_in_hbm(a) for a in srcs], *[_in_hbm(a) for a in lands], after)
    return (res[0], res[1]), res[2:2 + n], res[2 + n:2 + 2 * n], res[-1]


def _split_wait(name, srcs, lands, sems, route, after):
    n = len(srcs)

    def body(*refs):
        src, land, send_sems, recv_sems = refs[:n], refs[n:2 * n], refs[2 * n], refs[2 * n + 1]
        for i, (src_ref, _, dst_ref, to) in enumerate(route(src, land)):
            cp = pltpu.make_async_remote_copy(src_ref=src_ref, dst_ref=dst_ref, send_sem=send_sems.at[i],
                                              recv_sem=recv_sems.at[i], device_id=to, device_id_type=_MESH)
            cp.wait_send()
            cp.wait_recv()

    res = pl.pallas_call(
        body, name=name,
        in_specs=[_HBM] * (2 * n) + [_SEM, _SEM, _ANY],
        out_specs=[_HBM] * (2 * n),
        out_shape=[pltpu.HBM(a.shape, a.dtype) for a in list(srcs) + list(lands)],
        input_output_aliases={i: i for i in range(2 * n)},
        compiler_params=_SPLIT,
    )(*srcs, *lands, sems[0], sems[1], after)
    return res[:n], res[n:]


def _comm_only(name, comm):
    return _call(lambda: None, name=name, grid=(1,), in_specs=[], out_specs=[], out_shape=[], args=(), comm=comm)[1]


def _swap_comm(shards, lands):
    n_t = len(lands)

    def copies(land, sems, later):
        send_sems, recv_sems = sems
        x, y = lax.axis_index("x"), lax.axis_index("y")
        sends, recvs = [], []
        for t in range(n_t):
            mine, other = _my_half(shards[t].shape[0])
            for j, (px, py) in enumerate(_other_chips(x, y)):
                k = 2 * px + py
                for part, out in ((mine, sends), (other, recvs)) if later else ((mine, sends),):
                    out.append(pltpu.make_async_remote_copy(
                        src_ref=land[t].at[k, part], dst_ref=land[t].at[k, part], send_sem=send_sems.at[3 * t + j],
                        recv_sem=recv_sems.at[3 * t + j], device_id=_sibling(), device_id_type=_MESH))
        return sends, recvs

    def start(ins, land, sems):
        for cp in copies(land, sems, False)[0]:
            cp.start()

    def finish(ins, land, sems):
        sends, recvs = copies(land, sems, True)
        for cp in recvs:
            cp.wait_recv()
        for cp in sends:
            cp.wait_send()

    return _Comm(
        lands, [jax.ShapeDtypeStruct(a.shape, a.dtype) for a in lands],
        [pltpu.SemaphoreType.DMA((3 * n_t,)), pltpu.SemaphoreType.DMA((3 * n_t,))],
        start, finish, aliases={t: t for t in range(n_t)})


def _forward_comm(partials, lands):
    n_t = len(lands)

    def copies(land, sems, later):
        send_sems, recv_sems = sems
        x, y, c = lax.axis_index("x"), lax.axis_index("y"), lax.axis_index("c")
        sends, recvs = [], []
        for t in range(n_t):
            for j, k in enumerate([2 * x + y] + [2 * px + py for px, py in _other_chips(x, y)]):
                for slot, out in ((2 * k + c, sends), (2 * k + 1 - c, recvs)) if later else ((2 * k + c, sends),):
                    out.append(pltpu.make_async_remote_copy(
                        src_ref=land[t].at[slot], dst_ref=land[t].at[slot], send_sem=send_sems.at[4 * t + j],
                        recv_sem=recv_sems.at[4 * t + j], device_id=_sibling(), device_id_type=_MESH))
        return sends, recvs

    def start(ins, land, sems):
        for cp in copies(land, sems, False)[0]:
            cp.start()

    def finish(ins, land, sems):
        sends, recvs = copies(land, sems, True)
        for cp in recvs:
            cp.wait_recv()
        for cp in sends:
            cp.wait_send()

    return _Comm(
        lands, [jax.ShapeDtypeStruct(a.shape, a.dtype) for a in lands],
        [pltpu.SemaphoreType.DMA((4 * n_t,)), pltpu.SemaphoreType.DMA((4 * n_t,))],
        start, finish, aliases={t: t for t in range(n_t)})


def _allreduce_small(pack):
    rows, d = pack.shape

    def body(p_ref, sum_ref, all_ref, send_sems, recv_sems):
        x, y, c = lax.axis_index("x"), lax.axis_index("y"), lax.axis_index("c")
        me = 4 * x + 2 * y + c
        all_ref[me] = p_ref[...]
        peers = []
        for dx in range(2):
            for dy in range(2):
                for dc in range(2):
                    if dx or dy or dc:
                        px = 1 - x if dx else x
                        py = 1 - y if dy else y
                        pc = 1 - c if dc else c
                        peers.append((4 * dx + 2 * dy + dc - 1, (px, py, pc)))
        sends = []
        for k, to in peers:
            cp = pltpu.make_async_remote_copy(
                src_ref=p_ref, dst_ref=all_ref.at[me], send_sem=send_sems.at[k], recv_sem=recv_sems.at[k],
                device_id=to, device_id_type=_MESH)
            cp.start()
            sends.append(cp)
        for k, (px, py, pc) in peers:
            pltpu.make_async_remote_copy(
                src_ref=p_ref, dst_ref=all_ref.at[4 * px + 2 * py + pc], send_sem=send_sems.at[k],
                recv_sem=recv_sems.at[k], device_id=(px, py, pc), device_id_type=_MESH).wait_recv()
        for cp in sends:
            cp.wait_send()
        tot = all_ref[0]
        for i in range(1, N_DEV):
            tot = tot + all_ref[i]
        sum_ref[...] = tot

    vm = pl.BlockSpec(memory_space=pltpu.VMEM)
    return pl.pallas_call(
        body,
        name="allreduce_small",
        in_specs=[vm],
        out_specs=vm,
        out_shape=jax.ShapeDtypeStruct((rows, d), F32),
        scratch_shapes=[
            pltpu.VMEM((N_DEV, rows, d), F32),
            pltpu.SemaphoreType.DMA((N_DEV - 1,)),
            pltpu.SemaphoreType.DMA((N_DEV - 1,)),
        ],
    )(pack)


def _adamw_math(w, g, m, v):
    m = ADAM_B1 * m + (1.0 - ADAM_B1) * g
    v = ADAM_B2 * v + (1.0 - ADAM_B2) * (g * g)
    m_hat = m / (1.0 - ADAM_B1 ** ADAM_STEP)
    v_hat = v / (1.0 - ADAM_B2 ** ADAM_STEP)
    delta = -ADAM_LR * (m_hat / (jnp.sqrt(v_hat) + ADAM_EPS) + ADAM_WD * w)
    return delta, m, v


def _sum_adamw(parts, w, m, v, *, name, tr=256):
    r, c = w.shape
    tr = min(tr, r)
    tc = min(c, 1024)

    def body(p_ref, w_ref, m_ref, v_ref, g_ref, d_ref, m2_ref, v2_ref):
        g = p_ref[0].astype(F32)
        for i in range(1, N_DEV):
            g = g + p_ref[i].astype(F32)
        delta, m2, v2 = _adamw_math(w_ref[...], g, m_ref[...], v_ref[...])
        g_ref[...] = g
        d_ref[...] = delta
        m2_ref[...] = m2
        v2_ref[...] = v2

    blk = pl.BlockSpec((tr, tc), lambda i, j: (i, j))
    return pl.pallas_call(
        body,
        name=name,
        grid=(r // tr, c // tc),
        in_specs=[pl.BlockSpec((N_DEV, tr, tc), lambda i, j: (0, i, j)), blk, blk, blk],
        out_specs=[blk] * 4,
        out_shape=[jax.ShapeDtypeStruct((r, c), F32)] * 4,
        compiler_params=_params(("parallel", "parallel")),
    )(parts, w, m, v)


def _adamw_small(g, w, m, v):
    def body(g_ref, w_ref, m_ref, v_ref, d_ref, m2_ref, v2_ref):
        delta, m2, v2 = _adamw_math(w_ref[...], g_ref[...], m_ref[...], v_ref[...])
        d_ref[...] = delta
        m2_ref[...] = m2
        v2_ref[...] = v2

    vm = pl.BlockSpec(memory_space=pltpu.VMEM)
    return pl.pallas_call(
        body,
        name="adamw_small",
        in_specs=[vm] * 4,
        out_specs=[vm] * 3,
        out_shape=[jax.ShapeDtypeStruct(g.shape, F32)] * 3,
    )(g, w, m, v)


def _relu2_epilogue(acc):
    ra = jnp.maximum(acc, 0.0)
    return ra * ra, ra


def _residual_norm_epilogue(acc, res, g):
    h = acc + res
    return h, h * lax.rsqrt(jnp.mean(h * h, axis=-1, keepdims=True) + EPS) * g


def _rows(stacked):
    return stacked.reshape(stacked.shape[0] * stacked.shape[1], stacked.shape[2])


def _by_chip(mat):
    return mat.reshape(N_CHIPS, mat.shape[0] // N_CHIPS, mat.shape[1])


def _local_step(x, p, target, shards, small, update):
    s, d = x.shape
    cos_t, sin_t = _rope_tables(s)
    bucket = _band_buckets()
    p_bf = p.astype(BF16)
    wts = {}

    chip = 2 * lax.axis_index("x") + lax.axis_index("y")
    core = lax.axis_index("c")

    def gather(tag, names, after):
        srcs = [cast[n] for n in names]
        route = _gather_route([a.shape for a in srcs])
        sems, srcs, lands, token = _split_start(f"gather_start_{tag}", srcs, [zones[n] for n in names], route, after)

        def landed(done):
            got_srcs, got_lands = _split_wait(f"gather_wait_{tag}", srcs, lands, sems, route, done)
            comm = _swap_comm(got_srcs, got_lands)
            comm.waited = got_srcs[0]
            return comm

        return landed, token

    def prepare(n, zero):
        cast[n] = (shards[n] + zero).astype(BF16)
        zones[n] = _own_slot((N_CHIPS,) + cast[n].shape, BF16, chip, cast[n])

    cast, zones = {}, {}
    prepare("w_in", 0.0)
    in_landed, token = gather("in", ["w_in"], small["attn_norm_g"])
    for n in shards:
        if n != "w_in":
            prepare(n, token[:1, :1])
    g_attn = small["attn_norm_g"] + token[:1, :1]
    u = _rms_fwd(x, g_attn, name="norm_attn")
    prepared = u[:1, :1].astype(F32) + sum(
        (lax.dynamic_slice(zones[n], (chip, 0, 0), (1, 1, 1))[0] + cast[n][:1, :1]).astype(F32)
        for n in zones if n != "w_in")
    (wts["w_in"],) = _comm_only("swap_w_in", in_landed(prepared))
    mid_landed, token = gather("mid", ["w_out"], wts["w_in"])
    proj = _matmul(u, wts["w_in"], mode="nn", out_dtypes=[F32], name="mm_in", bn=768, after=token)
    pb, (w_out_s,) = _qk_prep(proj, small["q_norm_g"], small["k_norm_g"], cos_t, sin_t, comm=mid_landed(proj))
    wts["w_out"] = _rows(w_out_s)
    up_landed, token = gather("up", ["w_up"], pb)
    att_a, lse_a = _attn_a_fwd(pb, after=token)
    pad = ((PAD_LO, PAD_HI), (0, 0))
    kpad = jnp.pad(pb[:, COL_KB * HEAD_DIM:COL_VB * HEAD_DIM], pad)
    vpad = jnp.pad(pb[:, COL_VB * HEAD_DIM:], pad)
    up_swap = up_landed(att_a)
    down_landed, token = gather("down", ["w_down"], up_swap.waited)
    (att, lse_b), (wts["w_up"],) = _attn_b_fwd(pb, kpad, vpad, bucket, small["rel_bias_table"],
                                               small["sink_logits"], att_a, comm=up_swap, after=token)
    h1, mn = _matmul(att, wts["w_out"], mode="nn", out_dtypes=[F32, BF16], name="mm_out", bm=512, bn=d,
                     epilogue=_residual_norm_epilogue, extras=(x,), vecs=(small["mlp_norm_g"],))
    r, ra = _matmul(mn, wts["w_up"], mode="nn", out_dtypes=[BF16, BF16], name="mm_up", epilogue=_relu2_epilogue,
                    bm=2048)
    (w_down_s,) = _comm_only("swap_w_down", down_landed(r))
    wts["w_down"] = _rows(w_down_s)
    late_landed, token = gather("late", ["w_gate", "ple_w"], w_down_s)
    h2 = _matmul(r, wts["w_down"], mode="nn", out_dtypes=[F32], name="mm_down",
                 epilogue=lambda acc, res: (acc + res,), extras=(h1,), after=token)
    ng, (w_gate_s, wts["ple_w"]) = _rms_fwd(h2, small["gate_norm_g"], name="norm_gate", comm=late_landed(h2))
    wts["w_gate"] = _rows(w_gate_s)
    gate = _matmul(ng, wts["w_gate"], mode="nn", out_dtypes=[F32], name="mm_gate",
                   epilogue=lambda acc: (1.0 / (1.0 + jnp.exp(-acc)),))
    pp = _matmul(p_bf, wts["ple_w"], mode="nn", out_dtypes=[F32], name="mm_ple", bn=512)
    dh3, dz, dpp, dg_final, dg_ple, loss = _tail(h2, gate, pp, target, small["ple_norm_g"], small["final_norm_g"])

    dng = _matmul(dz, wts["w_gate"], mode="nt", out_dtypes=[F32], name="mm_gate_dx")
    gw_gate = _matmul(ng, dz, mode="tn", out_dtypes=[BF16], name="mm_gate_dw", bm=512, bk=4096)
    gw_ple = _matmul(p_bf, dpp, mode="tn", out_dtypes=[BF16], name="mm_ple_dw", bn=512, out_stack=N_CHIPS)
    dh2, dh2_bf, dg_gate = _rms_bwd(h2, dng, small["gate_norm_g"], dh3, name="norm_gate_bwd", want_bf16=True)

    def exchange(tag, partials, after):
        route = _exchange_route(len(partials))
        lands = [_own_slot((N_DEV,) + g.shape[1:], g.dtype, 2 * chip + core,
                           lax.dynamic_index_in_dim(g, chip, 0, keepdims=False)) for g in partials]
        sems, srcs, lands, token = _split_start(f"exchange_start_{tag}", partials, lands, route, after)

        def landed(done):
            got_srcs, got_lands = _split_wait(f"exchange_wait_{tag}", srcs, lands, sems, route, done)
            comm = _forward_comm(got_srcs, got_lands)
            comm.waited = got_srcs[0]
            return comm

        return landed, token

    big = {}
    gate_landed, token = exchange("gate", [_by_chip(gw_gate), gw_ple], dh2_bf)
    gw_down = _matmul(r, dh2_bf, mode="tn", out_dtypes=[BF16], name="mm_down_dw", after=token, bm=512, bk=4096)
    da, (parts_gate, parts_ple) = _matmul(
        dh2_bf, wts["w_down"], mode="nt", out_dtypes=[BF16], name="mm_down_dx", bm=2048,
        epilogue=lambda acc, ra_v: (acc * (2.0 * ra_v.astype(F32)),), extras=(ra,), comm=gate_landed(gw_down))
    down_landed, token = exchange("down", [_by_chip(gw_down)], da)
    big["w_gate"], big["ple_w"] = update("w_gate", parts_gate), update("ple_w", parts_ple)
    gw_up = _matmul(mn, da, mode="tn", out_dtypes=[BF16], name="mm_up_dw", out_stack=N_CHIPS, after=token,
                    bm=512, bk=4096)
    dmn = _matmul(da, wts["w_up"], mode="nt", out_dtypes=[F32], name="mm_up_dx", bk=4096,
                  after=gw_up)
    dh1, dh1_bf, dg_mlp = _rms_bwd(h1, dmn, small["mlp_norm_g"], dh2, name="norm_mlp_bwd", want_bf16=True)
    down_forward = down_landed(dh1_bf)
    up_landed, token = exchange("up", [gw_up], down_forward.waited)
    datt = _matmul(dh1_bf, wts["w_out"], mode="nt", out_dtypes=[BF16], name="mm_out_dx", after=token)
    gw_out = _matmul(att, dh1_bf, mode="tn", out_dtypes=[BF16], name="mm_out_dw", bm=512, bk=4096)
    dqb, dkpad, dvpad, dtab, dsink = _attn_b_bwd(pb, kpad, vpad, att, datt, lse_b, bucket,
                                                 small["rel_bias_table"], small["sink_logits"])
    (dqa, dka, dva), (parts_down,) = _attn_a_bwd(pb, att, datt, lse_a, comm=down_forward)
    up_forward = up_landed(dqa)
    out_landed, token = exchange("out", [_by_chip(gw_out)], up_forward.waited)
    (dproj, dg_q, dg_k), (parts_up,) = _qk_bwd(dqa, dka, dva, dqb, dkpad, dvpad, proj,
                                               small["q_norm_g"], small["k_norm_g"], cos_t, sin_t,
                                               comm=up_forward, after=token)
    gw_in = _matmul(u, dproj, mode="tn", out_dtypes=[BF16], name="mm_in_dw", bn=768, out_stack=N_CHIPS,
                    bm=512, bk=4096)
    out_forward = out_landed(gw_in)
    in_landed, token = exchange("in", [gw_in], out_forward.waited)
    du, (parts_out,) = _matmul(dproj, wts["w_in"], mode="nt", out_dtypes=[F32], name="mm_in_dx", bk=3072,
                               comm=out_forward, after=token)
    grad_x, dg_attn = _rms_bwd(x, du, small["attn_norm_g"], dh1, name="norm_attn_bwd", want_bf16=False)
    for n, parts in (("w_down", parts_down), ("w_up", parts_up), ("w_out", parts_out)):
        big[n] = update(n, parts)
    done = dg_attn + sum(big[n][0][0, :1, :] for n in ("w_down", "w_up", "w_out"))
    (parts_in,) = _comm_only("forward_w_in", in_landed(done))
    big["w_in"] = update("w_in", parts_in)

    small_g = {
        "attn_norm_g": dg_attn, "mlp_norm_g": dg_mlp, "ple_norm_g": dg_ple, "gate_norm_g": dg_gate,
        "final_norm_g": dg_final, "q_norm_g": dg_q, "k_norm_g": dg_k,
        "sink_logits": dsink[:, 0, 0][None, :], "rel_bias_table": dtab[:, :, 0].T,
    }
    return loss, grad_x, big, small_g


_SMALL_ROWS = ["attn_norm_g", "mlp_norm_g", "ple_norm_g", "gate_norm_g", "final_norm_g"]
_PACK_ROWS = 8


def _pack_small(vals, d):
    rows = [vals[n].reshape(1, d) for n in _SMALL_ROWS]
    misc = jnp.concatenate([
        vals["q_norm_g"].reshape(1, HEAD_DIM), vals["k_norm_g"].reshape(1, HEAD_DIM),
        jnp.pad(vals["sink_logits"].reshape(1, N_HEADS_B), ((0, 0), (0, HEAD_DIM - N_HEADS_B))),
        vals["rel_bias_table"].reshape(1, N_BUCKETS * N_HEADS_B)], axis=1)
    rows.append(jnp.pad(misc, ((0, 0), (0, d - misc.shape[1]))))
    rows.append(jnp.zeros((_PACK_ROWS - len(rows), d), F32))
    return jnp.concatenate(rows, axis=0).astype(F32)


def _unpack_small(pack, shapes):
    out = {n: pack[i].reshape(shapes[n]) for i, n in enumerate(_SMALL_ROWS)}
    misc = pack[len(_SMALL_ROWS)]
    out["q_norm_g"] = misc[:HEAD_DIM].reshape(shapes["q_norm_g"])
    out["k_norm_g"] = misc[HEAD_DIM:2 * HEAD_DIM].reshape(shapes["k_norm_g"])
    out["sink_logits"] = misc[2 * HEAD_DIM:2 * HEAD_DIM + N_HEADS_B].reshape(shapes["sink_logits"])
    out["rel_bias_table"] = misc[3 * HEAD_DIM:3 * HEAD_DIM + N_BUCKETS * N_HEADS_B].reshape(shapes["rel_bias_table"])
    return out


_WEIGHTS = ["attn_norm_g", "w_in", "q_norm_g", "k_norm_g", "sink_logits", "w_out", "mlp_norm_g", "w_up", "w_down",
            "ple_w", "ple_norm_g", "gate_norm_g", "w_gate", "rel_bias_table", "final_norm_g"]
_BIG = ["w_in", "w_out", "w_up", "w_down", "ple_w", "w_gate"]


def kernel(x, p, attn_norm_g, w_in, q_norm_g, k_norm_g, sink_logits, w_out, mlp_norm_g, w_up, w_down, ple_w, ple_norm_g, gate_norm_g, w_gate, rel_bias_table, final_norm_g, loss_target, m_attn_norm_g, m_w_in, m_q_norm_g, m_k_norm_g, m_sink_logits, m_w_out, m_mlp_norm_g, m_w_up, m_w_down, m_ple_w, m_ple_norm_g, m_gate_norm_g, m_w_gate, m_rel_bias_table, m_final_norm_g, v_attn_norm_g, v_w_in, v_q_norm_g, v_k_norm_g, v_sink_logits, v_w_out, v_mlp_norm_g, v_w_up, v_w_down, v_ple_w, v_ple_norm_g, v_gate_norm_g, v_w_gate, v_rel_bias_table, v_final_norm_g):
    given = dict(locals())
    w = {n: given[n] for n in _WEIGHTS}
    m = {n: given["m_" + n] for n in _WEIGHTS}
    v = {n: given["v_" + n] for n in _WEIGHTS}
    d = x.shape[-1]

    shards = {n: w[n][0] for n in _BIG}
    small = {
        "attn_norm_g": w["attn_norm_g"], "mlp_norm_g": w["mlp_norm_g"], "ple_norm_g": w["ple_norm_g"],
        "gate_norm_g": w["gate_norm_g"], "final_norm_g": w["final_norm_g"].reshape(1, d),
        "q_norm_g": w["q_norm_g"], "k_norm_g": w["k_norm_g"], "sink_logits": w["sink_logits"],
        "rel_bias_table": w["rel_bias_table"],
    }

    def update(n, parts):
        res = _sum_adamw(parts, w[n][0], m[n][0], v[n][0], name="adamw_" + n)
        return [t.reshape(w[n].shape) for t in res]

    loss_part, grad_x, big, small_g = _local_step(x[0], p[0, 0], loss_target[0], shards, small, update)
    grads, deltas, new_m, new_v = [{n: big[n][i] for n in _BIG} for i in range(4)]

    shapes = {n: w[n].shape for n in _WEIGHTS if n not in _BIG}
    pack = _pack_small(small_g, d)
    pack = pack.at[_PACK_ROWS - 1, :1].add(0.0 * grads["w_in"][0, 0, :1])
    pack = pack.at[_PACK_ROWS - 1, 1].set(loss_part[0, 0])
    g_small = _allreduce_small(pack)
    loss = g_small[_PACK_ROWS - 1, 1]
    d_small, m_small, v_small = _adamw_small(g_small, _pack_small(w, d), _pack_small(m, d), _pack_small(v, d))
    grads.update(_unpack_small(g_small, shapes))
    deltas.update(_unpack_small(d_small, shapes))
    new_m.update(_unpack_small(m_small, shapes))
    new_v.update(_unpack_small(v_small, shapes))

    return (loss, grad_x[None], *[grads[n] for n in _WEIGHTS], *[deltas[n] for n in _WEIGHTS],
            *[new_m[n] for n in _WEIGHTS], *[new_v[n] for n in _WEIGHTS])
```

```python
import functools
import math

import jax
import jax.numpy as jnp
from jax import lax
from jax.experimental import pallas as pl
from jax.experimental.pallas import tpu as pltpu

F32 = jnp.float32
BF16 = jnp.bfloat16

HEAD_DIM = 128
N_HEADS_A = 8
N_KV_A = 2
N_HEADS_B = 8
N_KV_B = 2
GROUP = 4
GRID_W = 64
BLOCK_Q = 128
WINDOW = 128
N_BUCKETS = 32
MAX_DISTANCE = 128
ROPE_THETA = 10000.0
EPS = 1e-6
NEG_INF = -1e30
ATT_SCALE = HEAD_DIM ** -0.5
LOG2E = math.log2(math.e)
LN2 = math.log(2.0)
Q_SCALE = ATT_SCALE * LOG2E
PAD_LO, PAD_HI = 256, 128
ADAM_LR = 0.001
ADAM_B1 = 0.9
ADAM_B2 = 0.999
ADAM_EPS = 1e-08
ADAM_WD = 0.01
ADAM_STEP = 10

N_CHIPS = 4
N_DEV = 8
COL_QA, COL_KA, COL_VA, COL_QB, COL_KB, COL_VB = 0, 8, 10, 12, 20, 22

VMEM_LIMIT = 52 * 1024 * 1024


def _params(sem=None, collective_id=None):
    return pltpu.CompilerParams(dimension_semantics=sem, vmem_limit_bytes=VMEM_LIMIT, collective_id=collective_id)


_ANY = pl.BlockSpec(memory_space=pl.ANY)
_MESH = pl.DeviceIdType.MESH
SIBLING_BARRIER_ID = 1


def _sibling():
    return (lax.axis_index("x"), lax.axis_index("y"), 1 - lax.axis_index("c"))


class _Comm:
    def __init__(self, inputs, out_shapes, sems, start, finish, aliases=None):
        self.inputs, self.out_shapes, self.sems = list(inputs), list(out_shapes), list(sems)
        self.start, self.finish, self.aliases = start, finish, dict(aliases or {})


def _call(body, *, name, grid, in_specs, out_specs, out_shape, args, scratch_shapes=(), sem=None, comm=None,
          after=None, aliases=None):
    in_specs, out_specs, out_shape = list(in_specs), list(out_specs), list(out_shape)
    scratch_shapes = list(scratch_shapes)
    n_in, n_out, n_sc = len(in_specs), len(out_specs), len(scratch_shapes)
    behind = [] if after is None else [after]
    aliases = dict(aliases or {})
    if comm is None:
        res = pl.pallas_call(
            (lambda *refs: body(*refs[:n_in], *refs[n_in + len(behind):])) if behind else body,
            name=name, grid=grid, in_specs=in_specs + [_ANY] * len(behind), out_specs=out_specs,
            out_shape=out_shape, scratch_shapes=scratch_shapes, input_output_aliases=aliases,
            compiler_params=_params(sem))(*args, *behind)
        return list(res), []
    c_in, c_out = len(comm.inputs), len(comm.out_shapes)

    def hosted(*refs):
        pos = [0]

        def take(n):
            pos[0] += n
            return refs[pos[0] - n:pos[0]]

        ins, c_ins, _, outs, c_outs, scr = (take(n_in), take(c_in), take(len(behind)), take(n_out), take(c_out),
                                            take(n_sc))
        c_sems = refs[pos[0]:]
        ids = [pl.program_id(a) for a in range(len(grid))]
        first = functools.reduce(jnp.logical_and, [i == 0 for i in ids])
        last = functools.reduce(jnp.logical_and, [i == g - 1 for i, g in zip(ids, grid)])

        @pl.when(first)
        def _():
            barrier = pltpu.get_barrier_semaphore()
            pl.semaphore_signal(barrier, inc=1, device_id=_sibling(), device_id_type=_MESH)
            pl.semaphore_wait(barrier, 1)
            comm.start(c_ins, c_outs, c_sems)

        body(*ins, *outs, *scr)

        @pl.when(last)
        def _():
            comm.finish(c_ins, c_outs, c_sems)

    res = pl.pallas_call(
        hosted, name=name, grid=grid, in_specs=in_specs + [_ANY] * (c_in + len(behind)),
        out_specs=out_specs + [_ANY] * c_out,
        out_shape=out_shape + comm.out_shapes, scratch_shapes=scratch_shapes + comm.sems,
        input_output_aliases={**aliases, **{n_in + i: n_out + o for i, o in comm.aliases.items()}},
        compiler_params=_params(("arbitrary",) * len(grid), SIBLING_BARRIER_ID))(*args, *comm.inputs, *behind)
    return list(res[:n_out]), list(res[n_out:])


def _matmul(a, b, *, mode, out_dtypes, name, epilogue=None, extras=(), bm=1024, bn=1024, bk=2048,
            out_stack=0, comm=None, after=None, vecs=()):
    stacked = b.ndim == 3
    if mode == "nn":
        m, k = a.shape
        if stacked:
            nj, kb, ns = b.shape
            n, ks = nj * ns, k
        else:
            kb, n = b.shape
            ns, ks = n, k
        dn = (((1,), (0,)), ((), ()))
    elif mode == "nt":
        m, k = a.shape
        if stacked:
            nj, n, ks = b.shape
            kb = nj * ks
        else:
            n, kb = b.shape
            ks = kb
        ns = n
        dn = (((1,), (1,)), ((), ()))
    else:
        k, m = a.shape
        kb, n = b.shape
        ns, ks = n, k
        dn = (((0,), (0,)), ((), ()))
    assert k == kb and not (stacked and mode == "tn")
    ns_out = n // out_stack if out_stack else n
    per_blk = min(bk, k) // ks if stacked and mode == "nt" and bk > ks else 0
    bm, bn, bk = min(bm, m), min(bn, ns, ns_out), per_blk * ks if per_blk else min(bk, ks)
    assert m % bm == 0 and ns % bn == 0 and ns_out % bn == 0 and (k % bk == 0 if per_blk else ks % bk == 0)
    gm, gn, gk = m // bm, n // bn, k // bk

    if mode == "tn":
        a_spec = pl.BlockSpec((bk, bm), lambda i, j, q: (q, i))
    else:
        a_spec = pl.BlockSpec((bm, bk), lambda i, j, q: (i, q))
    if mode == "nt":
        if per_blk:
            b_spec = pl.BlockSpec((per_blk, bn, ks), lambda i, j, q: (q, j, 0))
        elif stacked:
            per = ks // bk
            b_spec = pl.BlockSpec((None, bn, bk), lambda i, j, q: (q // per, j, q % per))
        else:
            b_spec = pl.BlockSpec((bn, bk), lambda i, j, q: (j, q))
    else:
        if stacked:
            per = ns // bn
            b_spec = pl.BlockSpec((None, bk, bn), lambda i, j, q: (j // per, q, j % per))
        else:
            b_spec = pl.BlockSpec((bk, bn), lambda i, j, q: (q, j))
    ex_spec = pl.BlockSpec((bm, bn), lambda i, j, q: (i, j))
    if out_stack:
        per_o = ns_out // bn
        o_spec = pl.BlockSpec((None, bm, bn), lambda i, j, q: (j // per_o, i, j % per_o))
        o_shape = (out_stack, m, ns_out)
    else:
        o_spec = ex_spec
        o_shape = (m, n)
    n_ex, n_out = len(extras) + len(vecs), len(out_dtypes)

    def body(a_ref, b_ref, *rest):
        ex, outs = rest[:n_ex], rest[n_ex:n_ex + n_out]
        if per_blk:
            part = sum(lax.dot_general(a_ref[:, t * ks:(t + 1) * ks], b_ref[t], dn, preferred_element_type=F32)
                       for t in range(per_blk))
        else:
            part = lax.dot_general(a_ref[...], b_ref[...], dn, preferred_element_type=F32)

        def finish(acc):
            res = epilogue(acc, *[e[...] for e in ex]) if epilogue else (acc,)
            for o, r in zip(outs, res):
                o[...] = r.astype(o.dtype)

        if gk == 1:
            finish(part)
        else:
            acc_ref = rest[-1]
            q = pl.program_id(2)

            @pl.when(q == 0)
            def _():
                acc_ref[...] = part

            @pl.when(q > 0)
            def _():
                acc_ref[...] += part

            @pl.when(q == gk - 1)
            def _():
                finish(acc_ref[...])

    res, c_res = _call(
        body, name=name, grid=(gm, gn, gk),
        in_specs=[a_spec, b_spec] + [ex_spec] * len(extras)
        + [pl.BlockSpec((1, bn), lambda i, j, q: (0, j))] * len(vecs),
        out_specs=[o_spec] * n_out,
        out_shape=[jax.ShapeDtypeStruct(o_shape, dt) for dt in out_dtypes],
        scratch_shapes=[pltpu.VMEM((bm, bn), F32)] if gk > 1 else [],
        sem=("parallel", "parallel", "arbitrary"), args=(a, b, *extras, *vecs), comm=comm, after=after)
    res = res[0] if n_out == 1 else res
    return res if comm is None else (res, c_res)


def _rms_fwd(x, g, *, name, tm=256, comm=None):
    s, d = x.shape
    tm = min(tm, s)

    def body(x_ref, g_ref, o_ref):
        xf = x_ref[...]
        r = lax.rsqrt(jnp.mean(xf * xf, axis=-1, keepdims=True) + EPS)
        o_ref[...] = (xf * r * g_ref[...]).astype(o_ref.dtype)

    res, c_res = _call(
        body, name=name, grid=(s // tm,),
        in_specs=[pl.BlockSpec((tm, d), lambda i: (i, 0)), pl.BlockSpec((1, d), lambda i: (0, 0))],
        out_specs=[pl.BlockSpec((tm, d), lambda i: (i, 0))],
        out_shape=[jax.ShapeDtypeStruct((s, d), BF16)],
        sem=("parallel",), args=(x, g), comm=comm)
    return res[0] if comm is None else (res[0], c_res)


def _rms_bwd(x, dy, g, add, *, name, want_bf16, tm=256):
    s, d = x.shape
    tm = min(tm, s)

    def body(x_ref, dy_ref, g_ref, add_ref, dx_ref, *rest):
        dg_ref = rest[-1]
        i = pl.program_id(0)
        xf = x_ref[...]
        dyf = dy_ref[...].astype(F32)
        r = lax.rsqrt(jnp.mean(xf * xf, axis=-1, keepdims=True) + EPS)
        xh = xf * r
        dyg = dyf * g_ref[...]
        dx = r * (dyg - xh * jnp.mean(dyg * xh, axis=-1, keepdims=True))
        tot = add_ref[...] + dx
        dx_ref[...] = tot
        if want_bf16:
            rest[0][...] = tot.astype(BF16)
        part = jnp.sum(dyf * xh, axis=0, keepdims=True)

        @pl.when(i == 0)
        def _():
            dg_ref[...] = part

        @pl.when(i > 0)
        def _():
            dg_ref[...] += part

    row = pl.BlockSpec((tm, d), lambda i: (i, 0))
    vec = pl.BlockSpec((1, d), lambda i: (0, 0))
    out_specs = [row] + ([row] if want_bf16 else []) + [vec]
    out_shape = [jax.ShapeDtypeStruct((s, d), F32)]
    if want_bf16:
        out_shape.append(jax.ShapeDtypeStruct((s, d), BF16))
    out_shape.append(jax.ShapeDtypeStruct((1, d), F32))
    return pl.pallas_call(
        body,
        name=name,
        grid=(s // tm,),
        in_specs=[row, row, vec, row],
        out_specs=out_specs,
        out_shape=out_shape,
        compiler_params=_params(("arbitrary",)),
    )(x, dy, g, add)


def _tail(h2, gate, pp, target, g_ple, g_final, *, tm=256):
    s, d = h2.shape
    tm = min(tm, s)

    def body(h2_ref, gate_ref, pp_ref, t_ref, gp_ref, gf_ref, dh3_ref, dz_ref, dpp_ref, dgf_ref, dgp_ref, loss_ref):
        i = pl.program_id(0)
        ppf = pp_ref[...]
        gate_v = gate_ref[...]
        r_p = lax.rsqrt(jnp.mean(ppf * ppf, axis=-1, keepdims=True) + EPS)
        eh = ppf * r_p
        e = eh * gp_ref[...]
        h3 = h2_ref[...] + gate_v * e
        r_f = lax.rsqrt(jnp.mean(h3 * h3, axis=-1, keepdims=True) + EPS)
        yh = h3 * r_f
        diff = yh * gf_ref[...] - t_ref[...]
        loss_part = 0.5 * jnp.sum(jnp.mean(diff * diff, axis=-1, keepdims=True), axis=0, keepdims=True)
        dy = diff / d
        dgf = jnp.sum(dy * yh, axis=0, keepdims=True)
        dyg = dy * gf_ref[...]
        dh3 = r_f * (dyg - yh * jnp.mean(dyg * yh, axis=-1, keepdims=True))
        dh3_ref[...] = dh3
        de = dh3 * gate_v
        dz_ref[...] = (dh3 * e * gate_v * (1.0 - gate_v)).astype(BF16)
        dgp = jnp.sum(de * eh, axis=0, keepdims=True)
        deg = de * gp_ref[...]
        dpp_ref[...] = (r_p * (deg - eh * jnp.mean(deg * eh, axis=-1, keepdims=True))).astype(BF16)
        loss_row = jnp.broadcast_to(loss_part, (1, 128))

        @pl.when(i == 0)
        def _():
            dgf_ref[...] = dgf
            dgp_ref[...] = dgp
            loss_ref[...] = loss_row

        @pl.when(i > 0)
        def _():
            dgf_ref[...] += dgf
            dgp_ref[...] += dgp
            loss_ref[...] += loss_row

    row = pl.BlockSpec((tm, d), lambda i: (i, 0))
    vec = pl.BlockSpec((1, d), lambda i: (0, 0))
    return pl.pallas_call(
        body,
        name="tail_fwd_bwd",
        grid=(s // tm,),
        in_specs=[row, row, row, row, vec, vec],
        out_specs=[row, row, row, vec, vec, pl.BlockSpec((1, 128), lambda i: (0, 0))],
        out_shape=[
            jax.ShapeDtypeStruct((s, d), F32),
            jax.ShapeDtypeStruct((s, d), BF16),
            jax.ShapeDtypeStruct((s, d), BF16),
            jax.ShapeDtypeStruct((1, d), F32),
            jax.ShapeDtypeStruct((1, d), F32),
            jax.ShapeDtypeStruct((1, 128), F32),
        ],
        compiler_params=_params(("arbitrary",)),
    )(h2, gate, pp, target, g_ple, g_final)


def _rope_tables(s):
    rows = s // GRID_W
    half = HEAD_DIM // 2
    inv_freq = ROPE_THETA ** (-jnp.arange(0, half, 2, dtype=F32) / half)
    ang_r = jnp.arange(rows, dtype=jnp.int32).astype(F32)[:, None] * inv_freq
    ang_c = jnp.arange(GRID_W, dtype=jnp.int32).astype(F32)[:, None] * inv_freq
    cr, sr = (jnp.repeat(t, GRID_W, axis=0) for t in (jnp.cos(ang_r), jnp.sin(ang_r)))
    cc, sc = (jnp.tile(t, (rows, 1)) for t in (jnp.cos(ang_c), jnp.sin(ang_c)))
    cos_t = jnp.concatenate([cr, cr, cc, cc], axis=-1)
    sin_t = jnp.concatenate([-sr, sr, -sc, sc], axis=-1)
    return cos_t, sin_t


def _low_quarters(shape):
    return (lax.broadcasted_iota(jnp.int32, shape, len(shape) - 1) % 64) < 32


def _swap_quarters(x, low):
    up = pltpu.roll(x, HEAD_DIM - 32, x.ndim - 1)
    down = pltpu.roll(x, 32, x.ndim - 1)
    return jnp.where(low, up, down)


def _cols(first, count=1):
    return slice(first * HEAD_DIM, (first + count) * HEAD_DIM)


def _qk_prep(proj, g_q, g_k, cos_t, sin_t, *, tm=256, comm=None):
    s, n = proj.shape
    tm = min(tm, s)

    def body(x_ref, gq_ref, gk_ref, c_ref, s_ref, o_ref):
        cos_v, sin_v = c_ref[...], s_ref[...]
        low = _low_quarters(cos_v.shape)
        for h in range(COL_VA):
            x = x_ref[:, _cols(h)]
            g = gq_ref[...] if h < COL_KA else gk_ref[...]
            xn = x * lax.rsqrt(jnp.mean(x * x, axis=-1, keepdims=True) + EPS) * g
            xr = xn * cos_v + _swap_quarters(xn, low) * sin_v
            if h < COL_KA:
                xr = xr * Q_SCALE
            o_ref[:, _cols(h)] = xr.astype(BF16)
        o_ref[:, _cols(COL_VA, 2)] = x_ref[:, _cols(COL_VA, 2)].astype(BF16)
        o_ref[:, _cols(COL_QB, N_HEADS_B)] = (x_ref[:, _cols(COL_QB, N_HEADS_B)] * Q_SCALE).astype(BF16)
        o_ref[:, _cols(COL_KB, 4)] = x_ref[:, _cols(COL_KB, 4)].astype(BF16)

    row = pl.BlockSpec((tm, n), lambda i: (i, 0))
    tab = pl.BlockSpec((tm, HEAD_DIM), lambda i: (i, 0))
    vec = pl.BlockSpec((1, HEAD_DIM), lambda i: (0, 0))
    res, c_res = _call(
        body, name="qk_prep", grid=(s // tm,),
        in_specs=[row, vec, vec, tab, tab],
        out_specs=[row],
        out_shape=[jax.ShapeDtypeStruct((s, n), BF16)],
        sem=("parallel",), args=(proj, g_q, g_k, cos_t, sin_t), comm=comm)
    return res[0] if comm is None else (res[0], c_res)


def _qk_bwd(dqa, dka, dva, dqb, dkpad, dvpad, proj, g_q, g_k, cos_t, sin_t, *, comm=None, after=None):
    s, n = proj.shape
    tm = min(PAD_LO, s)
    assert PAD_LO % tm == 0
    lo = PAD_LO // tm

    def body(dqa_ref, dka_ref, dva_ref, dqb_ref, dkb_ref, dvb_ref, x_ref, gq_ref, gk_ref, c_ref, s_ref,
             o_ref, dgq_ref, dgk_ref):
        i = pl.program_id(0)
        cos_v, sin_v = c_ref[...], s_ref[...]
        low = _low_quarters(cos_v.shape)

        def head(d, x, g):
            dn = d * cos_v + _swap_quarters(d * sin_v, low)
            r = lax.rsqrt(jnp.mean(x * x, axis=-1, keepdims=True) + EPS)
            xh = x * r
            dng = dn * g
            dx = r * (dng - xh * jnp.mean(dng * xh, axis=-1, keepdims=True))
            return dx.astype(BF16), jnp.sum(dn * xh, axis=0, keepdims=True)

        acc_q = jnp.zeros((1, HEAD_DIM), F32)
        acc_k = jnp.zeros((1, HEAD_DIM), F32)
        for h in range(N_HEADS_A):
            o_ref[:, _cols(h)], part = head(dqa_ref[:, _cols(h)] * ATT_SCALE, x_ref[:, _cols(h)], gq_ref[...])
            acc_q = acc_q + part
        for h in range(N_KV_A):
            o_ref[:, _cols(COL_KA + h)], part = head(dka_ref[:, _cols(h)] * LN2, x_ref[:, _cols(COL_KA + h)],
                                                     gk_ref[...])
            acc_k = acc_k + part
        o_ref[:, _cols(COL_VA, 2)] = dva_ref[...].astype(BF16)
        o_ref[:, _cols(COL_QB, N_HEADS_B)] = (dqb_ref[...] * ATT_SCALE).astype(BF16)
        o_ref[:, _cols(COL_KB, 2)] = (dkb_ref[...] * LN2).astype(BF16)
        o_ref[:, _cols(COL_VB, 2)] = dvb_ref[...].astype(BF16)

        @pl.when(i == 0)
        def _():
            dgq_ref[...] = acc_q
            dgk_ref[...] = acc_k

        @pl.when(i > 0)
        def _():
            dgq_ref[...] += acc_q
            dgk_ref[...] += acc_k

    def rows(width, shift=0):
        return pl.BlockSpec((tm, width), lambda i: (i + shift, 0))

    kv_w = N_KV_A * HEAD_DIM
    q_w = N_HEADS_A * HEAD_DIM
    vec = pl.BlockSpec((1, HEAD_DIM), lambda i: (0, 0))
    res, c_res = _call(
        body, name="qk_bwd", grid=(s // tm,),
        in_specs=[rows(q_w), rows(kv_w), rows(kv_w), rows(q_w), rows(kv_w, lo), rows(kv_w, lo), rows(n),
                  vec, vec, rows(HEAD_DIM), rows(HEAD_DIM)],
        out_specs=[rows(n), vec, vec],
        out_shape=[
            jax.ShapeDtypeStruct((s, n), BF16),
            jax.ShapeDtypeStruct((1, HEAD_DIM), F32),
            jax.ShapeDtypeStruct((1, HEAD_DIM), F32),
        ],
        sem=("arbitrary",), args=(dqa, dka, dva, dqb, dkpad, dvpad, proj, g_q, g_k, cos_t, sin_t), comm=comm,
        after=after)
    return res if comm is None else (res, c_res)


_NT = (((1,), (1,)), ((), ()))
_TN = (((0,), (0,)), ((), ()))


def _attn_a_fwd(pb, *, tq=4096, sub=256, comm=None, after=None):
    s = pb.shape[0]
    tq = min(tq, s)
    sub = min(sub, tq)

    def body(q_ref, k_ref, v_ref, o_ref, lse_ref):
        k = k_ref[...]
        v = v_ref[...]
        for r in range(tq // sub):
            rows = pl.ds(r * sub, sub)
            sc = lax.dot_general(q_ref[rows, :], k, _NT, preferred_element_type=F32)
            m = jnp.max(sc, axis=-1, keepdims=True)
            p = jnp.exp2(sc - m)
            l = jnp.sum(p, axis=-1, keepdims=True)
            o = jnp.dot(p.astype(BF16), v, preferred_element_type=F32)
            o_ref[rows, :] = (o / l).astype(BF16)
            lse_ref[rows, :] = jnp.broadcast_to(m + jnp.log2(l), (sub, HEAD_DIM))

    res, c_res = _call(
        body, name="attn_a_fwd", grid=(N_HEADS_A, s // tq),
        in_specs=[
            pl.BlockSpec((tq, HEAD_DIM), lambda h, i: (i, COL_QA + h)),
            pl.BlockSpec((s, HEAD_DIM), lambda h, i: (0, COL_KA + h // GROUP)),
            pl.BlockSpec((s, HEAD_DIM), lambda h, i: (0, COL_VA + h // GROUP)),
        ],
        out_specs=[
            pl.BlockSpec((tq, HEAD_DIM), lambda h, i: (i, h)),
            pl.BlockSpec((None, tq, HEAD_DIM), lambda h, i: (h, i, 0)),
        ],
        out_shape=[
            jax.ShapeDtypeStruct((s, (N_HEADS_A + N_HEADS_B) * HEAD_DIM), BF16),
            jax.ShapeDtypeStruct((N_HEADS_A, s, HEAD_DIM), F32),
        ],
        sem=("parallel", "parallel"), args=(pb, pb, pb), comm=comm, after=after)
    return res if comm is None else (res, c_res)


def _attn_a_bwd(pb, att, datt, lse, *, tq=2048, sub=256, comm=None):
    s = pb.shape[0]
    tq = min(tq, s)
    sub = min(sub, tq)

    def body(q_ref, k_ref, v_ref, o_ref, do_ref, lse_ref, dq_ref, dk_ref, dv_ref):
        first = jnp.logical_and(pl.program_id(1) == 0, pl.program_id(2) == 0)
        k = k_ref[...]
        v = v_ref[...]
        dk = dv = None
        for r in range(tq // sub):
            rows = pl.ds(r * sub, sub)
            q = q_ref[rows, :]
            do = do_ref[rows, :]
            sc = lax.dot_general(q, k, _NT, preferred_element_type=F32)
            p = jnp.exp2(sc - lse_ref[rows, :][:, :1])
            dp = lax.dot_general(do, v, _NT, preferred_element_type=F32)
            delta = jnp.sum(do.astype(F32) * o_ref[rows, :].astype(F32), axis=-1, keepdims=True)
            ds = (p * (dp - delta)).astype(BF16)
            dq_ref[rows, :] = jnp.dot(ds, k, preferred_element_type=F32)
            dk_r = lax.dot_general(ds, q, _TN, preferred_element_type=F32)
            dv_r = lax.dot_general(p.astype(BF16), do, _TN, preferred_element_type=F32)
            dk = dk_r if dk is None else dk + dk_r
            dv = dv_r if dv is None else dv + dv_r

        @pl.when(first)
        def _():
            dk_ref[...] = dk
            dv_ref[...] = dv

        @pl.when(jnp.logical_not(first))
        def _():
            dk_ref[...] += dk
            dv_ref[...] += dv

    qmap = lambda kv, g, i: (i, kv * GROUP + g)
    res, c_res = _call(
        body, name="attn_a_bwd", grid=(N_KV_A, GROUP, s // tq),
        in_specs=[
            pl.BlockSpec((tq, HEAD_DIM), lambda kv, g, i: (i, COL_QA + kv * GROUP + g)),
            pl.BlockSpec((s, HEAD_DIM), lambda kv, g, i: (0, COL_KA + kv)),
            pl.BlockSpec((s, HEAD_DIM), lambda kv, g, i: (0, COL_VA + kv)),
            pl.BlockSpec((tq, HEAD_DIM), qmap),
            pl.BlockSpec((tq, HEAD_DIM), qmap),
            pl.BlockSpec((None, tq, HEAD_DIM), lambda kv, g, i: (kv * GROUP + g, i, 0)),
        ],
        out_specs=[
            pl.BlockSpec((tq, HEAD_DIM), qmap),
            pl.BlockSpec((s, HEAD_DIM), lambda kv, g, i: (0, kv)),
            pl.BlockSpec((s, HEAD_DIM), lambda kv, g, i: (0, kv)),
        ],
        out_shape=[
            jax.ShapeDtypeStruct((s, N_HEADS_A * HEAD_DIM), F32),
            jax.ShapeDtypeStruct((s, N_KV_A * HEAD_DIM), F32),
            jax.ShapeDtypeStruct((s, N_KV_A * HEAD_DIM), F32),
        ],
        sem=("arbitrary", "arbitrary", "arbitrary"), args=(pb, pb, pb, att, datt, lse), comm=comm)
    return res if comm is None else (res, c_res)


def _t5_bucket(rel):
    nb = N_BUCKETS // 2
    ret = jnp.where(rel > 0, nb, 0)
    n = jnp.abs(rel)
    max_exact = nb // 2
    nf = jnp.maximum(n, 1).astype(F32)
    large = max_exact + (jnp.log(nf / max_exact) / math.log(MAX_DISTANCE / max_exact)
                         * (nb - max_exact)).astype(jnp.int32)
    large = jnp.minimum(large, nb - 1)
    return ret + jnp.where(n < max_exact, n, large)


def _band_buckets():
    r = jnp.arange(BLOCK_Q, dtype=jnp.int32)
    j = jnp.arange(3 * BLOCK_Q, dtype=jnp.int32)
    return _t5_bucket((j[None, :] - BLOCK_Q) - r[:, None])


def _band_bias(bucket, table_ref, h):
    acc = jnp.zeros(bucket.shape, F32)
    for b in range(N_BUCKETS):
        acc = jnp.where(bucket == b, table_ref[b, h], acc)
    return acc


GQ = GROUP * BLOCK_Q


def _stack_heads(x):
    return jnp.concatenate([x[:, _cols(g)] for g in range(GROUP)], axis=0)


def _unstack_heads(x):
    return jnp.concatenate([x[g * BLOCK_Q:(g + 1) * BLOCK_Q] for g in range(GROUP)], axis=1)


def _group_bias(bucket, table_ref, kv):
    r = lax.broadcasted_iota(jnp.int32, (BLOCK_Q, 3 * BLOCK_Q), 0)
    j = lax.broadcasted_iota(jnp.int32, (BLOCK_Q, 3 * BLOCK_Q), 1)
    inside = jnp.abs(j - BLOCK_Q - r) <= WINDOW
    return jnp.concatenate([jnp.where(inside, _band_bias(bucket, table_ref, kv * GROUP + g) * LOG2E, NEG_INF)
                            for g in range(GROUP)], axis=0)


def _group_sink(sink_ref, kv):
    head = lax.broadcasted_iota(jnp.int32, (GQ, 1), 0) // BLOCK_Q
    snk = jnp.zeros((GQ, 1), F32)
    for g in range(GROUP):
        snk = jnp.where(head == g, sink_ref[0, kv * GROUP + g] * LOG2E, snk)
    return snk


def _band_mask(n, s):
    kabs = n * BLOCK_Q + lax.broadcasted_iota(jnp.int32, (1, 3 * BLOCK_Q), 1) - BLOCK_Q
    return (kabs >= 0) & (kabs < s)


def _band_start(n):
    return pl.multiple_of(n * BLOCK_Q + (PAD_LO - BLOCK_Q), BLOCK_Q)


def _attn_b_fwd(pb, kpad, vpad, bucket, table, sink, att, *, comm=None, after=None):
    s = pb.shape[0]
    nblk = s // BLOCK_Q
    sp = kpad.shape[0]

    def body(table_ref, sink_ref, q0_ref, q1_ref, k_ref, v_ref, bucket_ref, _, o_ref, lse_ref, bias_ref):
        n = pl.program_id(0)

        @pl.when(n == 0)
        def _():
            for kv in range(N_KV_B):
                bias_ref[kv * GQ:(kv + 1) * GQ, :] = _group_bias(bucket_ref[...], table_ref, kv)

        band = pl.ds(_band_start(n), 3 * BLOCK_Q)
        mask = _band_mask(n, s)
        for kv, q_ref in enumerate((q0_ref, q1_ref)):
            kb = k_ref[band, _cols(kv)]
            vb = v_ref[band, _cols(kv)]
            sc = lax.dot_general(_stack_heads(q_ref[...]), kb, _NT, preferred_element_type=F32)
            sc = jnp.where(mask, sc + bias_ref[kv * GQ:(kv + 1) * GQ, :], NEG_INF)
            snk = _group_sink(sink_ref, kv)
            m = jnp.maximum(jnp.max(sc, axis=-1, keepdims=True), snk)
            p = jnp.exp2(sc - m)
            l = jnp.sum(p, axis=-1, keepdims=True) + jnp.exp2(snk - m)
            o = jnp.dot(p.astype(BF16), vb, preferred_element_type=F32)
            o_ref[:, _cols(kv * GROUP, GROUP)] = _unstack_heads((o / l).astype(BF16))
            lse = m + jnp.log2(l)
            for g in range(GROUP):
                lse_ref[kv * GROUP + g] = jnp.broadcast_to(lse[g * BLOCK_Q:(g + 1) * BLOCK_Q], (BLOCK_Q, HEAD_DIM))

    smem = pl.BlockSpec(memory_space=pltpu.SMEM)
    wide = GROUP * HEAD_DIM
    whole = pl.BlockSpec((sp, N_KV_B * HEAD_DIM), lambda n: (0, 0))
    res, c_res = _call(
        body, name="attn_b_fwd", grid=(nblk,),
        in_specs=[
            smem,
            smem,
            pl.BlockSpec((BLOCK_Q, wide), lambda n: (n, COL_QB // GROUP)),
            pl.BlockSpec((BLOCK_Q, wide), lambda n: (n, COL_QB // GROUP + 1)),
            whole,
            whole,
            pl.BlockSpec((BLOCK_Q, 3 * BLOCK_Q), lambda n: (0, 0)),
            _ANY,
        ],
        out_specs=[
            pl.BlockSpec((BLOCK_Q, N_HEADS_B * HEAD_DIM), lambda n: (n, 1)),
            pl.BlockSpec((N_HEADS_B, BLOCK_Q, HEAD_DIM), lambda n: (0, n, 0)),
        ],
        out_shape=[
            jax.ShapeDtypeStruct(att.shape, BF16),
            jax.ShapeDtypeStruct((N_HEADS_B, s, HEAD_DIM), F32),
        ],
        scratch_shapes=[pltpu.VMEM((N_KV_B * GQ, 3 * BLOCK_Q), F32)],
        sem=("arbitrary",), args=(table, sink, pb, pb, kpad, vpad, bucket, att), comm=comm, after=after,
        aliases={7: 0})
    return res if comm is None else (res, c_res)


def _attn_b_bwd(pb, kpad, vpad, att, datt, lse, bucket, table, sink, *, comm=None, after=None):
    s = pb.shape[0]
    nblk = s // BLOCK_Q
    sp = kpad.shape[0]

    def body(table_ref, sink_ref, q0_ref, q1_ref, k_ref, v_ref, o_ref, do_ref, lse_ref, bucket_ref,
             dq_ref, dk_ref, dv_ref, dtab_ref, dsink_ref, bias_ref, dbias_ref):
        n = pl.program_id(0)

        @pl.when(n == 0)
        def _():
            dk_ref[...] = jnp.zeros_like(dk_ref)
            dv_ref[...] = jnp.zeros_like(dv_ref)
            dbias_ref[...] = jnp.zeros_like(dbias_ref)
            dsink_ref[...] = jnp.zeros_like(dsink_ref)
            for kv in range(N_KV_B):
                bias_ref[kv * GQ:(kv + 1) * GQ, :] = _group_bias(bucket_ref[...], table_ref, kv)

        band = pl.ds(_band_start(n), 3 * BLOCK_Q)
        mask = _band_mask(n, s)
        for kv, q_ref in enumerate((q0_ref, q1_ref)):
            wide_cols = _cols(kv * GROUP, GROUP)
            q = _stack_heads(q_ref[...])
            do = _stack_heads(do_ref[:, wide_cols])
            o = _stack_heads(o_ref[:, wide_cols])
            kb = k_ref[band, _cols(kv)]
            vb = v_ref[band, _cols(kv)]
            lse = jnp.concatenate([lse_ref[kv * GROUP + g][:, :1] for g in range(GROUP)], axis=0)
            sc = lax.dot_general(q, kb, _NT, preferred_element_type=F32)
            sc = jnp.where(mask, sc + bias_ref[kv * GQ:(kv + 1) * GQ, :], NEG_INF)
            p = jnp.exp2(sc - lse)
            dp = lax.dot_general(do, vb, _NT, preferred_element_type=F32)
            delta = jnp.sum(do.astype(F32) * o.astype(F32), axis=-1, keepdims=True)
            ds = p * (dp - delta)
            dsb = ds.astype(BF16)
            dq_ref[:, wide_cols] = _unstack_heads(jnp.dot(dsb, kb, preferred_element_type=F32))
            dk_ref[band, _cols(kv)] += lax.dot_general(dsb, q, _TN, preferred_element_type=F32)
            dv_ref[band, _cols(kv)] += lax.dot_general(p.astype(BF16), do, _TN, preferred_element_type=F32)
            dbias_ref[kv * GQ:(kv + 1) * GQ, :] += ds
            sink_part = -jnp.exp2(_group_sink(sink_ref, kv) - lse) * delta
            for g in range(GROUP):
                rows = slice(g * BLOCK_Q, (g + 1) * BLOCK_Q)
                dsink_ref[kv * GROUP + g] += jnp.broadcast_to(
                    jnp.sum(sink_part[rows], axis=0, keepdims=True), (1, HEAD_DIM))

        @pl.when(n == nblk - 1)
        def _():
            bucket_v = bucket_ref[...]
            row = lax.broadcasted_iota(jnp.int32, (N_BUCKETS, HEAD_DIM), 0)
            for h in range(N_HEADS_B):
                acc = dbias_ref[h * BLOCK_Q:(h + 1) * BLOCK_Q, :]
                tot = jnp.zeros((N_BUCKETS, HEAD_DIM), F32)
                for b in range(N_BUCKETS):
                    tot = jnp.where(row == b, jnp.sum(jnp.where(bucket_v == b, acc, 0.0), keepdims=True), tot)
                dtab_ref[h] = tot

    smem = pl.BlockSpec(memory_space=pltpu.SMEM)
    wide = GROUP * HEAD_DIM
    whole = pl.BlockSpec((sp, N_KV_B * HEAD_DIM), lambda n: (0, 0))
    group_b = pl.BlockSpec((BLOCK_Q, N_HEADS_B * HEAD_DIM), lambda n: (n, 1))
    res, c_res = _call(
        body, name="attn_b_bwd", grid=(nblk,),
        in_specs=[
            smem,
            smem,
            pl.BlockSpec((BLOCK_Q, wide), lambda n: (n, COL_QB // GROUP)),
            pl.BlockSpec((BLOCK_Q, wide), lambda n: (n, COL_QB // GROUP + 1)),
            whole,
            whole,
            group_b,
            group_b,
            pl.BlockSpec((N_HEADS_B, BLOCK_Q, HEAD_DIM), lambda n: (0, n, 0)),
            pl.BlockSpec((BLOCK_Q, 3 * BLOCK_Q), lambda n: (0, 0)),
        ],
        out_specs=[
            pl.BlockSpec((BLOCK_Q, N_HEADS_B * HEAD_DIM), lambda n: (n, 0)),
            whole,
            whole,
            pl.BlockSpec((N_HEADS_B, N_BUCKETS, HEAD_DIM), lambda n: (0, 0, 0)),
            pl.BlockSpec((N_HEADS_B, 1, HEAD_DIM), lambda n: (0, 0, 0)),
        ],
        out_shape=[
            jax.ShapeDtypeStruct((s, N_HEADS_B * HEAD_DIM), F32),
            jax.ShapeDtypeStruct((sp, N_KV_B * HEAD_DIM), F32),
            jax.ShapeDtypeStruct((sp, N_KV_B * HEAD_DIM), F32),
            jax.ShapeDtypeStruct((N_HEADS_B, N_BUCKETS, HEAD_DIM), F32),
            jax.ShapeDtypeStruct((N_HEADS_B, 1, HEAD_DIM), F32),
        ],
        scratch_shapes=[pltpu.VMEM((N_KV_B * GQ, 3 * BLOCK_Q), F32), pltpu.VMEM((N_KV_B * GQ, 3 * BLOCK_Q), F32)],
        sem=("arbitrary",),
        args=(table, sink, pb, pb, kpad, vpad, att, datt, lse, bucket), comm=comm, after=after)
    return res if comm is None else (res, c_res)


def _other_chips(x, y):
    return [(x, 1 - y), (1 - x, y), (1 - x, 1 - y)]


_HBM = pl.BlockSpec(memory_space=pltpu.HBM)
_SEM = pl.BlockSpec(memory_space=pltpu.SEMAPHORE)
_SPLIT = pltpu.CompilerParams(has_side_effects=pltpu.SideEffectType.DATAFLOW_SIDE_EFFECTING)


def _in_hbm(a):
    return pltpu.with_memory_space_constraint(a, pltpu.HBM)


def _my_half(rows):
    c = lax.axis_index("c")
    half = rows // 2
    return pl.ds(pl.multiple_of(c * half, half), half), pl.ds(pl.multiple_of((1 - c) * half, half), half)


def _gather_route(shapes):
    def route(src, land):
        x, y, c = lax.axis_index("x"), lax.axis_index("y"), lax.axis_index("c")
        out = []
        for t, shape in enumerate(shapes):
            mine, _ = _my_half(shape[0])
            for px, py in _other_chips(x, y):
                out.append((src[t].at[mine], land[t].at[2 * x + y, mine], land[t].at[2 * px + py, mine], (px, py, c)))
        return out

    return route


def _exchange_route(n_t):
    def route(src, land):
        x, y, c = lax.axis_index("x"), lax.axis_index("y"), lax.axis_index("c")
        out = []
        for t in range(n_t):
            for px, py in _other_chips(x, y):
                k = 2 * px + py
                out.append((src[t].at[k], land[t].at[2 * (2 * x + y) + c], land[t].at[2 * k + c], (px, py, c)))
        return out

    return route


def _own_slot(shape, dtype, slot, block):
    return lax.dynamic_update_slice(lax.empty(shape, dtype), block[None], (slot,) + (0,) * (len(shape) - 1))


def _split_start(name, srcs, lands, route, after):
    n = len(srcs)

    def body(*refs):
        src, land, send_sems, recv_sems, token = refs[:n], refs[n:2 * n], refs[2 * n + 1], refs[2 * n + 2], refs[-1]
        for i, (src_ref, dst_ref, _, to) in enumerate(route(src, land)):
            pltpu.make_async_remote_copy(src_ref=src_ref, dst_ref=dst_ref, send_sem=send_sems.at[i],
                                         recv_sem=recv_sems.at[i], device_id=to, device_id_type=_MESH).start()
        token[...] = jnp.zeros_like(token)

    sem = pltpu.SemaphoreType.DMA((3 * n,))
    lands = list(lands)
    res = pl.pallas_call(
        body, name=name,
        in_specs=[_HBM] * (2 * n) + [_ANY],
        out_specs=[_SEM, _SEM] + [_HBM] * (2 * n) + [pl.BlockSpec(memory_space=pltpu.VMEM)],
        out_shape=[sem, sem] + [pltpu.HBM(a.shape, a.dtype) for a in list(srcs) + lands]
        + [jax.ShapeDtypeStruct((8, 128), F32)],
        input_output_aliases={i: 2 + i for i in range(2 * n)},
        compiler_params=_SPLIT,
    )(*[_in_hbm(a) for a in srcs], *[_in_hbm(a) for a in lands], after)
    return (res[0], res[1]), res[2:2 + n], res[2 + n:2 + 2 * n], res[-1]


def _split_wait(name, srcs, lands, sems, route, after):
    n = len(srcs)

    def body(*refs):
        src, land, send_sems, recv_sems = refs[:n], refs[n:2 * n], refs[2 * n], refs[2 * n + 1]
        for i, (src_ref, _, dst_ref, to) in enumerate(route(src, land)):
            cp = pltpu.make_async_remote_copy(src_ref=src_ref, dst_ref=dst_ref, send_sem=send_sems.at[i],
                                              recv_sem=recv_sems.at[i], device_id=to, device_id_type=_MESH)
            cp.wait_send()
            cp.wait_recv()

    res = pl.pallas_call(
        body, name=name,
        in_specs=[_HBM] * (2 * n) + [_SEM, _SEM, _ANY],
        out_specs=[_HBM] * (2 * n),
        out_shape=[pltpu.HBM(a.shape, a.dtype) for a in list(srcs) + list(lands)],
        input_output_aliases={i: i for i in range(2 * n)},
        compiler_params=_SPLIT,
    )(*srcs, *lands, sems[0], sems[1], after)
    return res[:n], res[n:]


def _comm_only(name, comm):
    return _call(lambda: None, name=name, grid=(1,), in_specs=[], out_specs=[], out_shape=[], args=(), comm=comm)[1]


def _swap_comm(shards, lands):
    n_t = len(lands)

    def copies(land, sems, later):
        send_sems, recv_sems = sems
        x, y = lax.axis_index("x"), lax.axis_index("y")
        sends, recvs = [], []
        for t in range(n_t):
            mine, other = _my_half(shards[t].shape[0])
            for j, (px, py) in enumerate(_other_chips(x, y)):
                k = 2 * px + py
                for part, out in ((mine, sends), (other, recvs)) if later else ((mine, sends),):
                    out.append(pltpu.make_async_remote_copy(
                        src_ref=land[t].at[k, part], dst_ref=land[t].at[k, part], send_sem=send_sems.at[3 * t + j],
                        recv_sem=recv_sems.at[3 * t + j], device_id=_sibling(), device_id_type=_MESH))
        return sends, recvs

    def start(ins, land, sems):
        for cp in copies(land, sems, False)[0]:
            cp.start()

    def finish(ins, land, sems):
        sends, recvs = copies(land, sems, True)
        for cp in recvs:
            cp.wait_recv()
        for cp in sends:
            cp.wait_send()

    return _Comm(
        lands, [jax.ShapeDtypeStruct(a.shape, a.dtype) for a in lands],
        [pltpu.SemaphoreType.DMA((3 * n_t,)), pltpu.SemaphoreType.DMA((3 * n_t,))],
        start, finish, aliases={t: t for t in range(n_t)})


def _forward_comm(partials, lands):
    n_t = len(lands)

    def copies(land, sems, later):
        send_sems, recv_sems = sems
        x, y, c = lax.axis_index("x"), lax.axis_index("y"), lax.axis_index("c")
        sends, recvs = [], []
        for t in range(n_t):
            for j, k in enumerate([2 * x + y] + [2 * px + py for px, py in _other_chips(x, y)]):
                for slot, out in ((2 * k + c, sends), (2 * k + 1 - c, recvs)) if later else ((2 * k + c, sends),):
                    out.append(pltpu.make_async_remote_copy(
                        src_ref=land[t].at[slot], dst_ref=land[t].at[slot], send_sem=send_sems.at[4 * t + j],
                        recv_sem=recv_sems.at[4 * t + j], device_id=_sibling(), device_id_type=_MESH))
        return sends, recvs

    def start(ins, land, sems):
        for cp in copies(land, sems, False)[0]:
            cp.start()

    def finish(ins, land, sems):
        sends, recvs = copies(land, sems, True)
        for cp in recvs:
            cp.wait_recv()
        for cp in sends:
            cp.wait_send()

    return _Comm(
        lands, [jax.ShapeDtypeStruct(a.shape, a.dtype) for a in lands],
        [pltpu.SemaphoreType.DMA((4 * n_t,)), pltpu.SemaphoreType.DMA((4 * n_t,))],
        start, finish, aliases={t: t for t in range(n_t)})


def _allreduce_small(pack):
    rows, d = pack.shape

    def body(p_ref, sum_ref, all_ref, send_sems, recv_sems):
        x, y, c = lax.axis_index("x"), lax.axis_index("y"), lax.axis_index("c")
        me = 4 * x + 2 * y + c
        all_ref[me] = p_ref[...]
        peers = []
        for dx in range(2):
            for dy in range(2):
                for dc in range(2):
                    if dx or dy or dc:
                        px = 1 - x if dx else x
                        py = 1 - y if dy else y
                        pc = 1 - c if dc else c
                        peers.append((4 * dx + 2 * dy + dc - 1, (px, py, pc)))
        sends = []
        for k, to in peers:
            cp = pltpu.make_async_remote_copy(
                src_ref=p_ref, dst_ref=all_ref.at[me], send_sem=send_sems.at[k], recv_sem=recv_sems.at[k],
                device_id=to, device_id_type=_MESH)
            cp.start()
            sends.append(cp)
        for k, (px, py, pc) in peers:
            pltpu.make_async_remote_copy(
                src_ref=p_ref, dst_ref=all_ref.at[4 * px + 2 * py + pc], send_sem=send_sems.at[k],
                recv_sem=recv_sems.at[k], device_id=(px, py, pc), device_id_type=_MESH).wait_recv()
        for cp in sends:
            cp.wait_send()
        tot = all_ref[0]
        for i in range(1, N_DEV):
            tot = tot + all_ref[i]
        sum_ref[...] = tot

    vm = pl.BlockSpec(memory_space=pltpu.VMEM)
    return pl.pallas_call(
        body,
        name="allreduce_small",
        in_specs=[vm],
        out_specs=vm,
        out_shape=jax.ShapeDtypeStruct((rows, d), F32),
        scratch_shapes=[
            pltpu.VMEM((N_DEV, rows, d), F32),
            pltpu.SemaphoreType.DMA((N_DEV - 1,)),
            pltpu.SemaphoreType.DMA((N_DEV - 1,)),
        ],
    )(pack)


def _adamw_math(w, g, m, v):
    m = ADAM_B1 * m + (1.0 - ADAM_B1) * g
    v = ADAM_B2 * v + (1.0 - ADAM_B2) * (g * g)
    m_hat = m / (1.0 - ADAM_B1 ** ADAM_STEP)
    v_hat = v / (1.0 - ADAM_B2 ** ADAM_STEP)
    delta = -ADAM_LR * (m_hat / (jnp.sqrt(v_hat) + ADAM_EPS) + ADAM_WD * w)
    return delta, m, v


def _sum_adamw(parts, w, m, v, *, name, tr=256):
    r, c = w.shape
    tr = min(tr, r)
    tc = min(c, 1024)

    def body(p_ref, w_ref, m_ref, v_ref, g_ref, d_ref, m2_ref, v2_ref):
        g = p_ref[0].astype(F32)
        for i in range(1, N_DEV):
            g = g + p_ref[i].astype(F32)
        delta, m2, v2 = _adamw_math(w_ref[...], g, m_ref[...], v_ref[...])
        g_ref[...] = g
        d_ref[...] = delta
        m2_ref[...] = m2
        v2_ref[...] = v2

    blk = pl.BlockSpec((tr, tc), lambda i, j: (i, j))
    return pl.pallas_call(
        body,
        name=name,
        grid=(r // tr, c // tc),
        in_specs=[pl.BlockSpec((N_DEV, tr, tc), lambda i, j: (0, i, j)), blk, blk, blk],
        out_specs=[blk] * 4,
        out_shape=[jax.ShapeDtypeStruct((r, c), F32)] * 4,
        compiler_params=_params(("parallel", "parallel")),
    )(parts, w, m, v)


def _adamw_small(g, w, m, v):
    def body(g_ref, w_ref, m_ref, v_ref, d_ref, m2_ref, v2_ref):
        delta, m2, v2 = _adamw_math(w_ref[...], g_ref[...], m_ref[...], v_ref[...])
        d_ref[...] = delta
        m2_ref[...] = m2
        v2_ref[...] = v2

    vm = pl.BlockSpec(memory_space=pltpu.VMEM)
    return pl.pallas_call(
        body,
        name="adamw_small",
        in_specs=[vm] * 4,
        out_specs=[vm] * 3,
        out_shape=[jax.ShapeDtypeStruct(g.shape, F32)] * 3,
    )(g, w, m, v)


def _relu2_epilogue(acc):
    ra = jnp.maximum(acc, 0.0)
    return ra * ra, ra


def _residual_norm_epilogue(acc, res, g):
    h = acc + res
    return h, h * lax.rsqrt(jnp.mean(h * h, axis=-1, keepdims=True) + EPS) * g


def _rows(stacked):
    return stacked.reshape(stacked.shape[0] * stacked.shape[1], stacked.shape[2])


def _by_chip(mat):
    return mat.reshape(N_CHIPS, mat.shape[0] // N_CHIPS, mat.shape[1])


def _local_step(x, p, target, shards, small, update):
    s, d = x.shape
    cos_t, sin_t = _rope_tables(s)
    bucket = _band_buckets()
    p_bf = p.astype(BF16)
    wts = {}

    chip = 2 * lax.axis_index("x") + lax.axis_index("y")
    core = lax.axis_index("c")

    def gather(tag, names, after):
        srcs = [cast[n] for n in names]
        route = _gather_route([a.shape for a in srcs])
        sems, srcs, lands, token = _split_start(f"gather_start_{tag}", srcs, [zones[n] for n in names], route, after)

        def landed(done):
            got_srcs, got_lands = _split_wait(f"gather_wait_{tag}", srcs, lands, sems, route, done)
            comm = _swap_comm(got_srcs, got_lands)
            comm.waited = got_srcs[0]
            return comm

        return landed, token

    def prepare(n, zero):
        cast[n] = (shards[n] + zero).astype(BF16)
        zones[n] = _own_slot((N_CHIPS,) + cast[n].shape, BF16, chip, cast[n])

    cast, zones = {}, {}
    prepare("w_in", 0.0)
    in_landed, token = gather("in", ["w_in"], small["attn_norm_g"])
    for n in shards:
        if n != "w_in":
            prepare(n, token[:1, :1])
    g_attn = small["attn_norm_g"] + token[:1, :1]
    u = _rms_fwd(x, g_attn, name="norm_attn")
    prepared = u[:1, :1].astype(F32) + sum(
        (lax.dynamic_slice(zones[n], (chip, 0, 0), (1, 1, 1))[0] + cast[n][:1, :1]).astype(F32)
        for n in zones if n != "w_in")
    (wts["w_in"],) = _comm_only("swap_w_in", in_landed(prepared))
    mid_landed, token = gather("mid", ["w_out"], wts["w_in"])
    proj = _matmul(u, wts["w_in"], mode="nn", out_dtypes=[F32], name="mm_in", bn=768, after=token)
    pb, (w_out_s,) = _qk_prep(proj, small["q_norm_g"], small["k_norm_g"], cos_t, sin_t, comm=mid_landed(proj))
    wts["w_out"] = _rows(w_out_s)
    up_landed, token = gather("up", ["w_up"], pb)
    att_a, lse_a = _attn_a_fwd(pb, after=token)
    pad = ((PAD_LO, PAD_HI), (0, 0))
    kpad = jnp.pad(pb[:, COL_KB * HEAD_DIM:COL_VB * HEAD_DIM], pad)
    vpad = jnp.pad(pb[:, COL_VB * HEAD_DIM:], pad)
    up_swap = up_landed(att_a)
    down_landed, token = gather("down", ["w_down"], up_swap.waited)
    (att, lse_b), (wts["w_up"],) = _attn_b_fwd(pb, kpad, vpad, bucket, small["rel_bias_table"],
                                               small["sink_logits"], att_a, comm=up_swap, after=token)
    h1, mn = _matmul(att, wts["w_out"], mode="nn", out_dtypes=[F32, BF16], name="mm_out", bm=512, bn=d,
                     epilogue=_residual_norm_epilogue, extras=(x,), vecs=(small["mlp_norm_g"],))
    r, ra = _matmul(mn, wts["w_up"], mode="nn", out_dtypes=[BF16, BF16], name="mm_up", epilogue=_relu2_epilogue,
                    bm=2048)
    (w_down_s,) = _comm_only("swap_w_down", down_landed(r))
    wts["w_down"] = _rows(w_down_s)
    late_landed, token = gather("late", ["w_gate", "ple_w"], w_down_s)
    h2 = _matmul(r, wts["w_down"], mode="nn", out_dtypes=[F32], name="mm_down",
                 epilogue=lambda acc, res: (acc + res,), extras=(h1,), after=token)
    ng, (w_gate_s, wts["ple_w"]) = _rms_fwd(h2, small["gate_norm_g"], name="norm_gate", comm=late_landed(h2))
    wts["w_gate"] = _rows(w_gate_s)
    gate = _matmul(ng, wts["w_gate"], mode="nn", out_dtypes=[F32], name="mm_gate",
                   epilogue=lambda acc: (1.0 / (1.0 + jnp.exp(-acc)),))
    pp = _matmul(p_bf, wts["ple_w"], mode="nn", out_dtypes=[F32], name="mm_ple", bn=512)
    dh3, dz, dpp, dg_final, dg_ple, loss = _tail(h2, gate, pp, target, small["ple_norm_g"], small["final_norm_g"])

    dng = _matmul(dz, wts["w_gate"], mode="nt", out_dtypes=[F32], name="mm_gate_dx")
    gw_gate = _matmul(ng, dz, mode="tn", out_dtypes=[BF16], name="mm_gate_dw", bm=512, bk=4096)
    gw_ple = _matmul(p_bf, dpp, mode="tn", out_dtypes=[BF16], name="mm_ple_dw", bn=512, out_stack=N_CHIPS)
    dh2, dh2_bf, dg_gate = _rms_bwd(h2, dng, small["gate_norm_g"], dh3, name="norm_gate_bwd", want_bf16=True)

    def exchange(tag, partials, after):
        route = _exchange_route(len(partials))
        lands = [_own_slot((N_DEV,) + g.shape[1:], g.dtype, 2 * chip + core,
                           lax.dynamic_index_in_dim(g, chip, 0, keepdims=False)) for g in partials]
        sems, srcs, lands, token = _split_start(f"exchange_start_{tag}", partials, lands, route, after)

        def landed(done):
            got_srcs, got_lands = _split_wait(f"exchange_wait_{tag}", srcs, lands, sems, route, done)
            comm = _forward_comm(got_srcs, got_lands)
            comm.waited = got_srcs[0]
            return comm

        return landed, token

    big = {}
    gate_landed, token = exchange("gate", [_by_chip(gw_gate), gw_ple], dh2_bf)
    gw_down = _matmul(r, dh2_bf, mode="tn", out_dtypes=[BF16], name="mm_down_dw", after=token, bm=512, bk=4096)
    da, (parts_gate, parts_ple) = _matmul(
        dh2_bf, wts["w_down"], mode="nt", out_dtypes=[BF16], name="mm_down_dx", bm=2048,
        epilogue=lambda acc, ra_v: (acc * (2.0 * ra_v.astype(F32)),), extras=(ra,), comm=gate_landed(gw_down))
    down_landed, token = exchange("down", [_by_chip(gw_down)], da)
    big["w_gate"], big["ple_w"] = update("w_gate", parts_gate), update("ple_w", parts_ple)
    gw_up = _matmul(mn, da, mode="tn", out_dtypes=[BF16], name="mm_up_dw", out_stack=N_CHIPS, after=token,
                    bm=512, bk=4096)
    dmn = _matmul(da, wts["w_up"], mode="nt", out_dtypes=[F32], name="mm_up_dx", bk=4096,
                  after=gw_up)
    dh1, dh1_bf, dg_mlp = _rms_bwd(h1, dmn, small["mlp_norm_g"], dh2, name="norm_mlp_bwd", want_bf16=True)
    down_forward = down_landed(dh1_bf)
    up_landed, token = exchange("up", [gw_up], down_forward.waited)
    datt = _matmul(dh1_bf, wts["w_out"], mode="nt", out_dtypes=[BF16], name="mm_out_dx", after=token)
    gw_out = _matmul(att, dh1_bf, mode="tn", out_dtypes=[BF16], name="mm_out_dw", bm=512, bk=4096)
    dqb, dkpad, dvpad, dtab, dsink = _attn_b_bwd(pb, kpad, vpad, att, datt, lse_b, bucket,
                                                 small["rel_bias_table"], small["sink_logits"])
    (dqa, dka, dva), (parts_down,) = _attn_a_bwd(pb, att, datt, lse_a, comm=down_forward)
    up_forward = up_landed(dqa)
    out_landed, token = exchange("out", [_by_chip(gw_out)], up_forward.waited)
    (dproj, dg_q, dg_k), (parts_up,) = _qk_bwd(dqa, dka, dva, dqb, dkpad, dvpad, proj,
                                               small["q_norm_g"], small["k_norm_g"], cos_t, sin_t,
                                               comm=up_forward, after=token)
    gw_in = _matmul(u, dproj, mode="tn", out_dtypes=[BF16], name="mm_in_dw", bn=768, out_stack=N_CHIPS,
                    bm=512, bk=4096)
    out_forward = out_landed(gw_in)
    in_landed, token = exchange("in", [gw_in], out_forward.waited)
    du, (parts_out,) = _matmul(dproj, wts["w_in"], mode="nt", out_dtypes=[F32], name="mm_in_dx", bk=3072,
                               comm=out_forward, after=token)
    grad_x, dg_attn = _rms_bwd(x, du, small["attn_norm_g"], dh1, name="norm_attn_bwd", want_bf16=False)
    for n, parts in (("w_down", parts_down), ("w_up", parts_up), ("w_out", parts_out)):
        big[n] = update(n, parts)
    done = dg_attn + sum(big[n][0][0, :1, :] for n in ("w_down", "w_up", "w_out"))
    (parts_in,) = _comm_only("forward_w_in", in_landed(done))
    big["w_in"] = update("w_in", parts_in)

    small_g = {
        "attn_norm_g": dg_attn, "mlp_norm_g": dg_mlp, "ple_norm_g": dg_ple, "gate_norm_g": dg_gate,
        "final_norm_g": dg_final, "q_norm_g": dg_q, "k_norm_g": dg_k,
        "sink_logits": dsink[:, 0, 0][None, :], "rel_bias_table": dtab[:, :, 0].T,
    }
    return loss, grad_x, big, small_g


_SMALL_ROWS = ["attn_norm_g", "mlp_norm_g", "ple_norm_g", "gate_norm_g", "final_norm_g"]
_PACK_ROWS = 8


def _pack_small(vals, d):
    rows = [vals[n].reshape(1, d) for n in _SMALL_ROWS]
    misc = jnp.concatenate([
        vals["q_norm_g"].reshape(1, HEAD_DIM), vals["k_norm_g"].reshape(1, HEAD_DIM),
        jnp.pad(vals["sink_logits"].reshape(1, N_HEADS_B), ((0, 0), (0, HEAD_DIM - N_HEADS_B))),
        vals["rel_bias_table"].reshape(1, N_BUCKETS * N_HEADS_B)], axis=1)
    rows.append(jnp.pad(misc, ((0, 0), (0, d - misc.shape[1]))))
    rows.append(jnp.zeros((_PACK_ROWS - len(rows), d), F32))
    return jnp.concatenate(rows, axis=0).astype(F32)


def _unpack_small(pack, shapes):
    out = {n: pack[i].reshape(shapes[n]) for i, n in enumerate(_SMALL_ROWS)}
    misc = pack[len(_SMALL_ROWS)]
    out["q_norm_g"] = misc[:HEAD_DIM].reshape(shapes["q_norm_g"])
    out["k_norm_g"] = misc[HEAD_DIM:2 * HEAD_DIM].reshape(shapes["k_norm_g"])
    out["sink_logits"] = misc[2 * HEAD_DIM:2 * HEAD_DIM + N_HEADS_B].reshape(shapes["sink_logits"])
    out["rel_bias_table"] = misc[3 * HEAD_DIM:3 * HEAD_DIM + N_BUCKETS * N_HEADS_B].reshape(shapes["rel_bias_table"])
    return out


_WEIGHTS = ["attn_norm_g", "w_in", "q_norm_g", "k_norm_g", "sink_logits", "w_out", "mlp_norm_g", "w_up", "w_down",
            "ple_w", "ple_norm_g", "gate_norm_g", "w_gate", "rel_bias_table", "final_norm_g"]
_BIG = ["w_in", "w_out", "w_up", "w_down", "ple_w", "w_gate"]


def kernel(x, p, attn_norm_g, w_in, q_norm_g, k_norm_g, sink_logits, w_out, mlp_norm_g, w_up, w_down, ple_w, ple_norm_g, gate_norm_g, w_gate, rel_bias_table, final_norm_g, loss_target, m_attn_norm_g, m_w_in, m_q_norm_g, m_k_norm_g, m_sink_logits, m_w_out, m_mlp_norm_g, m_w_up, m_w_down, m_ple_w, m_ple_norm_g, m_gate_norm_g, m_w_gate, m_rel_bias_table, m_final_norm_g, v_attn_norm_g, v_w_in, v_q_norm_g, v_k_norm_g, v_sink_logits, v_w_out, v_mlp_norm_g, v_w_up, v_w_down, v_ple_w, v_ple_norm_g, v_gate_norm_g, v_w_gate, v_rel_bias_table, v_final_norm_g):
    given = dict(locals())
    w = {n: given[n] for n in _WEIGHTS}
    m = {n: given["m_" + n] for n in _WEIGHTS}
    v = {n: given["v_" + n] for n in _WEIGHTS}
    d = x.shape[-1]

    shards = {n: w[n][0] for n in _BIG}
    small = {
        "attn_norm_g": w["attn_norm_g"], "mlp_norm_g": w["mlp_norm_g"], "ple_norm_g": w["ple_norm_g"],
        "gate_norm_g": w["gate_norm_g"], "final_norm_g": w["final_norm_g"].reshape(1, d),
        "q_norm_g": w["q_norm_g"], "k_norm_g": w["k_norm_g"], "sink_logits": w["sink_logits"],
        "rel_bias_table": w["rel_bias_table"],
    }

    def update(n, parts):
        res = _sum_adamw(parts, w[n][0], m[n][0], v[n][0], name="adamw_" + n)
        return [t.reshape(w[n].shape) for t in res]

    loss_part, grad_x, big, small_g = _local_step(x[0], p[0, 0], loss_target[0], shards, small, update)
    grads, deltas, new_m, new_v = [{n: big[n][i] for n in _BIG} for i in range(4)]

    shapes = {n: w[n].shape for n in _WEIGHTS if n not in _BIG}
    pack = _pack_small(small_g, d)
    pack = pack.at[_PACK_ROWS - 1, :1].add(0.0 * grads["w_in"][0, 0, :1])
    pack = pack.at[_PACK_ROWS - 1, 1].set(loss_part[0, 0])
    g_small = _allreduce_small(pack)
    loss = g_small[_PACK_ROWS - 1, 1]
    d_small, m_small, v_small = _adamw_small(g_small, _pack_small(w, d), _pack_small(m, d), _pack_small(v, d))
    grads.update(_unpack_small(g_small, shapes))
    deltas.update(_unpack_small(d_small, shapes))
    new_m.update(_unpack_small(m_small, shapes))
    new_v.update(_unpack_small(v_small, shapes))

    return (loss, grad_x[None], *[grads[n] for n in _WEIGHTS], *[deltas[n] for n in _WEIGHTS],
            *[new_m[n] for n in _WEIGHTS], *[new_v[n] for n in _WEIGHTS])
```

```python
import functools
import math

import jax
import jax.numpy as jnp
from jax import lax
from jax.experimental import pallas as pl
from jax.experimental.pallas import tpu as pltpu

F32 = jnp.float32
BF16 = jnp.bfloat16

HEAD_DIM = 128
N_HEADS_A = 8
N_KV_A = 2
N_HEADS_B = 8
N_KV_B = 2
GROUP = 4
GRID_W = 64
BLOCK_Q = 128
WINDOW = 128
N_BUCKETS = 32
MAX_DISTANCE = 128
ROPE_THETA = 10000.0
EPS = 1e-6
NEG_INF = -1e30
ATT_SCALE = HEAD_DIM ** -0.5
LOG2E = math.log2(math.e)
LN2 = math.log(2.0)
Q_SCALE = ATT_SCALE * LOG2E
PAD_LO, PAD_HI = 256, 128
ADAM_LR = 0.001
ADAM_B1 = 0.9
ADAM_B2 = 0.999
ADAM_EPS = 1e-08
ADAM_WD = 0.01
ADAM_STEP = 10

N_CHIPS = 4
N_DEV = 8
COL_QA, COL_KA, COL_VA, COL_QB, COL_KB, COL_VB = 0, 8, 10, 12, 20, 22

VMEM_LIMIT = 52 * 1024 * 1024


def _params(sem=None, collective_id=None):
    return pltpu.CompilerParams(dimension_semantics=sem, vmem_limit_bytes=VMEM_LIMIT, collective_id=collective_id)


_ANY = pl.BlockSpec(memory_space=pl.ANY)
_MESH = pl.DeviceIdType.MESH
SIBLING_BARRIER_ID = 1


def _sibling():
    return (lax.axis_index("x"), lax.axis_index("y"), 1 - lax.axis_index("c"))


class _Comm:
    def __init__(self, inputs, out_shapes, sems, start, finish, aliases=None):
        self.inputs, self.out_shapes, self.sems = list(inputs), list(out_shapes), list(sems)
        self.start, self.finish, self.aliases = start, finish, dict(aliases or {})


def _call(body, *, name, grid, in_specs, out_specs, out_shape, args, scratch_shapes=(), sem=None, comm=None,
          after=None, aliases=None):
    in_specs, out_specs, out_shape = list(in_specs), list(out_specs), list(out_shape)
    scratch_shapes = list(scratch_shapes)
    n_in, n_out, n_sc = len(in_specs), len(out_specs), len(scratch_shapes)
    behind = [] if after is None else [after]
    aliases = dict(aliases or {})
    if comm is None:
        res = pl.pallas_call(
            (lambda *refs: body(*refs[:n_in], *refs[n_in + len(behind):])) if behind else body,
            name=name, grid=grid, in_specs=in_specs + [_ANY] * len(behind), out_specs=out_specs,
            out_shape=out_shape, scratch_shapes=scratch_shapes, input_output_aliases=aliases,
            compiler_params=_params(sem))(*args, *behind)
        return list(res), []
    c_in, c_out = len(comm.inputs), len(comm.out_shapes)

    def hosted(*refs):
        pos = [0]

        def take(n):
            pos[0] += n
            return refs[pos[0] - n:pos[0]]

        ins, c_ins, _, outs, c_outs, scr = (take(n_in), take(c_in), take(len(behind)), take(n_out), take(c_out),
                                            take(n_sc))
        c_sems = refs[pos[0]:]
        ids = [pl.program_id(a) for a in range(len(grid))]
        first = functools.reduce(jnp.logical_and, [i == 0 for i in ids])
        last = functools.reduce(jnp.logical_and, [i == g - 1 for i, g in zip(ids, grid)])

        @pl.when(first)
        def _():
            barrier = pltpu.get_barrier_semaphore()
            pl.semaphore_signal(barrier, inc=1, device_id=_sibling(), device_id_type=_MESH)
            pl.semaphore_wait(barrier, 1)
            comm.start(c_ins, c_outs, c_sems)

        body(*ins, *outs, *scr)

        @pl.when(last)
        def _():
            comm.finish(c_ins, c_outs, c_sems)

    res = pl.pallas_call(
        hosted, name=name, grid=grid, in_specs=in_specs + [_ANY] * (c_in + len(behind)),
        out_specs=out_specs + [_ANY] * c_out,
        out_shape=out_shape + comm.out_shapes, scratch_shapes=scratch_shapes + comm.sems,
        input_output_aliases={**aliases, **{n_in + i: n_out + o for i, o in comm.aliases.items()}},
        compiler_params=_params(("arbitrary",) * len(grid), SIBLING_BARRIER_ID))(*args, *comm.inputs, *behind)
    return list(res[:n_out]), list(res[n_out:])


def _matmul(a, b, *, mode, out_dtypes, name, epilogue=None, extras=(), bm=1024, bn=1024, bk=2048,
            out_stack=0, comm=None, after=None, vecs=()):
    stacked = b.ndim == 3
    if mode == "nn":
        m, k = a.shape
        if stacked:
            nj, kb, ns = b.shape
            n, ks = nj * ns, k
        else:
            kb, n = b.shape
            ns, ks = n, k
        dn = (((1,), (0,)), ((), ()))
    elif mode == "nt":
        m, k = a.shape
        if stacked:
            nj, n, ks = b.shape
            kb = nj * ks
        else:
            n, kb = b.shape
            ks = kb
        ns = n
        dn = (((1,), (1,)), ((), ()))
    else:
        k, m = a.shape
        kb, n = b.shape
        ns, ks = n, k
        dn = (((0,), (0,)), ((), ()))
    assert k == kb and not (stacked and mode == "tn")
    ns_out = n // out_stack if out_stack else n
    per_blk = min(bk, k) // ks if stacked and mode == "nt" and bk > ks else 0
    bm, bn, bk = min(bm, m), min(bn, ns, ns_out), per_blk * ks if per_blk else min(bk, ks)
    assert m % bm == 0 and ns % bn == 0 and ns_out % bn == 0 and (k % bk == 0 if per_blk else ks % bk == 0)
    gm, gn, gk = m // bm, n // bn, k // bk

    if mode == "tn":
        a_spec = pl.BlockSpec((bk, bm), lambda i, j, q: (q, i))
    else:
        a_spec = pl.BlockSpec((bm, bk), lambda i, j, q: (i, q))
    if mode == "nt":
        if per_blk:
            b_spec = pl.BlockSpec((per_blk, bn, ks), lambda i, j, q: (q, j, 0))
        elif stacked:
            per = ks // bk
            b_spec = pl.BlockSpec((None, bn, bk), lambda i, j, q: (q // per, j, q % per))
        else:
            b_spec = pl.BlockSpec((bn, bk), lambda i, j, q: (j, q))
    else:
        if stacked:
            per = ns // bn
            b_spec = pl.BlockSpec((None, bk, bn), lambda i, j, q: (j // per, q, j % per))
        else:
            b_spec = pl.BlockSpec((bk, bn), lambda i, j, q: (q, j))
    ex_spec = pl.BlockSpec((bm, bn), lambda i, j, q: (i, j))
    if out_stack:
        per_o = ns_out // bn
        o_spec = pl.BlockSpec((None, bm, bn), lambda i, j, q: (j // per_o, i, j % per_o))
        o_shape = (out_stack, m, ns_out)
    else:
        o_spec = ex_spec
        o_shape = (m, n)
    n_ex, n_out = len(extras) + len(vecs), len(out_dtypes)

    def body(a_ref, b_ref, *rest):
        ex, outs = rest[:n_ex], rest[n_ex:n_ex + n_out]
        if per_blk:
            part = sum(lax.dot_general(a_ref[:, t * ks:(t + 1) * ks], b_ref[t], dn, preferred_element_type=F32)
                       for t in range(per_blk))
        else:
            part = lax.dot_general(a_ref[...], b_ref[...], dn, preferred_element_type=F32)

        def finish(acc):
            res = epilogue(acc, *[e[...] for e in ex]) if epilogue else (acc,)
            for o, r in zip(outs, res):
                o[...] = r.astype(o.dtype)

        if gk == 1:
            finish(part)
        else:
            acc_ref = rest[-1]
            q = pl.program_id(2)

            @pl.when(q == 0)
            def _():
                acc_ref[...] = part

            @pl.when(q > 0)
            def _():
                acc_ref[...] += part

            @pl.when(q == gk - 1)
            def _():
                finish(acc_ref[...])

    res, c_res = _call(
        body, name=name, grid=(gm, gn, gk),
        in_specs=[a_spec, b_spec] + [ex_spec] * len(extras)
        + [pl.BlockSpec((1, bn), lambda i, j, q: (0, j))] * len(vecs),
        out_specs=[o_spec] * n_out,
        out_shape=[jax.ShapeDtypeStruct(o_shape, dt) for dt in out_dtypes],
        scratch_shapes=[pltpu.VMEM((bm, bn), F32)] if gk > 1 else [],
        sem=("parallel", "parallel", "arbitrary"), args=(a, b, *extras, *vecs), comm=comm, after=after)
    res = res[0] if n_out == 1 else res
    return res if comm is None else (res, c_res)


def _rms_fwd(x, g, *, name, tm=256, comm=None):
    s, d = x.shape
    tm = min(tm, s)

    def body(x_ref, g_ref, o_ref):
        xf = x_ref[...]
        r = lax.rsqrt(jnp.mean(xf * xf, axis=-1, keepdims=True) + EPS)
        o_ref[...] = (xf * r * g_ref[...]).astype(o_ref.dtype)

    res, c_res = _call(
        body, name=name, grid=(s // tm,),
        in_specs=[pl.BlockSpec((tm, d), lambda i: (i, 0)), pl.BlockSpec((1, d), lambda i: (0, 0))],
        out_specs=[pl.BlockSpec((tm, d), lambda i: (i, 0))],
        out_shape=[jax.ShapeDtypeStruct((s, d), BF16)],
        sem=("parallel",), args=(x, g), comm=comm)
    return res[0] if comm is None else (res[0], c_res)


def _rms_bwd(x, dy, g, add, *, name, want_bf16, tm=512):
    s, d = x.shape
    tm = min(tm, s)

    def body(x_ref, dy_ref, g_ref, add_ref, dx_ref, *rest):
        dg_ref = rest[-1]
        i = pl.program_id(0)
        xf = x_ref[...]
        dyf = dy_ref[...].astype(F32)
        r = lax.rsqrt(jnp.mean(xf * xf, axis=-1, keepdims=True) + EPS)
        xh = xf * r
        dyg = dyf * g_ref[...]
        dx = r * (dyg - xh * jnp.mean(dyg * xh, axis=-1, keepdims=True))
        tot = add_ref[...] + dx
        dx_ref[...] = tot
        if want_bf16:
            rest[0][...] = tot.astype(BF16)
        part = jnp.sum(dyf * xh, axis=0, keepdims=True)

        @pl.when(i == 0)
        def _():
            dg_ref[...] = part

        @pl.when(i > 0)
        def _():
            dg_ref[...] += part

    row = pl.BlockSpec((tm, d), lambda i: (i, 0))
    vec = pl.BlockSpec((1, d), lambda i: (0, 0))
    out_specs = [row] + ([row] if want_bf16 else []) + [vec]
    out_shape = [jax.ShapeDtypeStruct((s, d), F32)]
    if want_bf16:
        out_shape.append(jax.ShapeDtypeStruct((s, d), BF16))
    out_shape.append(jax.ShapeDtypeStruct((1, d), F32))
    return pl.pallas_call(
        body,
        name=name,
        grid=(s // tm,),
        in_specs=[row, row, vec, row],
        out_specs=out_specs,
        out_shape=out_shape,
        compiler_params=_params(("arbitrary",)),
    )(x, dy, g, add)


def _tail(h2, gate, pp, target, g_ple, g_final, *, tm=256):
    s, d = h2.shape
    tm = min(tm, s)

    def body(h2_ref, gate_ref, pp_ref, t_ref, gp_ref, gf_ref, dh3_ref, dz_ref, dpp_ref, dgf_ref, dgp_ref, loss_ref):
        i = pl.program_id(0)
        ppf = pp_ref[...]
        gate_v = gate_ref[...]
        r_p = lax.rsqrt(jnp.mean(ppf * ppf, axis=-1, keepdims=True) + EPS)
        eh = ppf * r_p
        e = eh * gp_ref[...]
        h3 = h2_ref[...] + gate_v * e
        r_f = lax.rsqrt(jnp.mean(h3 * h3, axis=-1, keepdims=True) + EPS)
        yh = h3 * r_f
        diff = yh * gf_ref[...] - t_ref[...]
        loss_part = 0.5 * jnp.sum(jnp.mean(diff * diff, axis=-1, keepdims=True), axis=0, keepdims=True)
        dy = diff / d
        dgf = jnp.sum(dy * yh, axis=0, keepdims=True)
        dyg = dy * gf_ref[...]
        dh3 = r_f * (dyg - yh * jnp.mean(dyg * yh, axis=-1, keepdims=True))
        dh3_ref[...] = dh3
        de = dh3 * gate_v
        dz_ref[...] = (dh3 * e * gate_v * (1.0 - gate_v)).astype(BF16)
        dgp = jnp.sum(de * eh, axis=0, keepdims=True)
        deg = de * gp_ref[...]
        dpp_ref[...] = (r_p * (deg - eh * jnp.mean(deg * eh, axis=-1, keepdims=True))).astype(BF16)
        loss_row = jnp.broadcast_to(loss_part, (1, 128))

        @pl.when(i == 0)
        def _():
            dgf_ref[...] = dgf
            dgp_ref[...] = dgp
            loss_ref[...] = loss_row

        @pl.when(i > 0)
        def _():
            dgf_ref[...] += dgf
            dgp_ref[...] += dgp
            loss_ref[...] += loss_row

    row = pl.BlockSpec((tm, d), lambda i: (i, 0))
    vec = pl.BlockSpec((1, d), lambda i: (0, 0))
    return pl.pallas_call(
        body,
        name="tail_fwd_bwd",
        grid=(s // tm,),
        in_specs=[row, row, row, row, vec, vec],
        out_specs=[row, row, row, vec, vec, pl.BlockSpec((1, 128), lambda i: (0, 0))],
        out_shape=[
            jax.ShapeDtypeStruct((s, d), F32),
            jax.ShapeDtypeStruct((s, d), BF16),
            jax.ShapeDtypeStruct((s, d), BF16),
            jax.ShapeDtypeStruct((1, d), F32),
            jax.ShapeDtypeStruct((1, d), F32),
            jax.ShapeDtypeStruct((1, 128), F32),
        ],
        compiler_params=_params(("arbitrary",)),
    )(h2, gate, pp, target, g_ple, g_final)


def _rope_tables(s):
    rows = s // GRID_W
    half = HEAD_DIM // 2
    inv_freq = ROPE_THETA ** (-jnp.arange(0, half, 2, dtype=F32) / half)
    ang_r = jnp.arange(rows, dtype=jnp.int32).astype(F32)[:, None] * inv_freq
    ang_c = jnp.arange(GRID_W, dtype=jnp.int32).astype(F32)[:, None] * inv_freq
    cr, sr = (jnp.repeat(t, GRID_W, axis=0) for t in (jnp.cos(ang_r), jnp.sin(ang_r)))
    cc, sc = (jnp.tile(t, (rows, 1)) for t in (jnp.cos(ang_c), jnp.sin(ang_c)))
    cos_t = jnp.concatenate([cr, cr, cc, cc], axis=-1)
    sin_t = jnp.concatenate([-sr, sr, -sc, sc], axis=-1)
    return cos_t, sin_t


def _low_quarters(shape):
    return (lax.broadcasted_iota(jnp.int32, shape, len(shape) - 1) % 64) < 32


def _swap_quarters(x, low):
    up = pltpu.roll(x, HEAD_DIM - 32, x.ndim - 1)
    down = pltpu.roll(x, 32, x.ndim - 1)
    return jnp.where(low, up, down)


def _cols(first, count=1):
    return slice(first * HEAD_DIM, (first + count) * HEAD_DIM)


def _qk_prep(proj, g_q, g_k, cos_t, sin_t, *, tm=256, comm=None):
    s, n = proj.shape
    tm = min(tm, s)

    def body(x_ref, gq_ref, gk_ref, c_ref, s_ref, o_ref):
        cos_v, sin_v = c_ref[...], s_ref[...]
        low = _low_quarters(cos_v.shape)
        for h in range(COL_VA):
            x = x_ref[:, _cols(h)]
            g = gq_ref[...] if h < COL_KA else gk_ref[...]
            xn = x * lax.rsqrt(jnp.mean(x * x, axis=-1, keepdims=True) + EPS) * g
            xr = xn * cos_v + _swap_quarters(xn, low) * sin_v
            if h < COL_KA:
                xr = xr * Q_SCALE
            o_ref[:, _cols(h)] = xr.astype(BF16)
        o_ref[:, _cols(COL_VA, 2)] = x_ref[:, _cols(COL_VA, 2)].astype(BF16)
        o_ref[:, _cols(COL_QB, N_HEADS_B)] = (x_ref[:, _cols(COL_QB, N_HEADS_B)] * Q_SCALE).astype(BF16)
        o_ref[:, _cols(COL_KB, 4)] = x_ref[:, _cols(COL_KB, 4)].astype(BF16)

    row = pl.BlockSpec((tm, n), lambda i: (i, 0))
    tab = pl.BlockSpec((tm, HEAD_DIM), lambda i: (i, 0))
    vec = pl.BlockSpec((1, HEAD_DIM), lambda i: (0, 0))
    res, c_res = _call(
        body, name="qk_prep", grid=(s // tm,),
        in_specs=[row, vec, vec, tab, tab],
        out_specs=[row],
        out_shape=[jax.ShapeDtypeStruct((s, n), BF16)],
        sem=("parallel",), args=(proj, g_q, g_k, cos_t, sin_t), comm=comm)
    return res[0] if comm is None else (res[0], c_res)


def _qk_bwd(dqa, dka, dva, dqb, dkpad, dvpad, proj, g_q, g_k, cos_t, sin_t, *, comm=None, after=None):
    s, n = proj.shape
    tm = min(PAD_LO, s)
    assert PAD_LO % tm == 0
    lo = PAD_LO // tm

    def body(dqa_ref, dka_ref, dva_ref, dqb_ref, dkb_ref, dvb_ref, x_ref, gq_ref, gk_ref, c_ref, s_ref,
             o_ref, dgq_ref, dgk_ref):
        i = pl.program_id(0)
        cos_v, sin_v = c_ref[...], s_ref[...]
        low = _low_quarters(cos_v.shape)

        def head(d, x, g):
            dn = d * cos_v + _swap_quarters(d * sin_v, low)
            r = lax.rsqrt(jnp.mean(x * x, axis=-1, keepdims=True) + EPS)
            xh = x * r
            dng = dn * g
            dx = r * (dng - xh * jnp.mean(dng * xh, axis=-1, keepdims=True))
            return dx.astype(BF16), jnp.sum(dn * xh, axis=0, keepdims=True)

        acc_q = jnp.zeros((1, HEAD_DIM), F32)
        acc_k = jnp.zeros((1, HEAD_DIM), F32)
        for h in range(N_HEADS_A):
            o_ref[:, _cols(h)], part = head(dqa_ref[:, _cols(h)] * ATT_SCALE, x_ref[:, _cols(h)], gq_ref[...])
            acc_q = acc_q + part
        for h in range(N_KV_A):
            o_ref[:, _cols(COL_KA + h)], part = head(dka_ref[:, _cols(h)] * LN2, x_ref[:, _cols(COL_KA + h)],
                                                     gk_ref[...])
            acc_k = acc_k + part
        o_ref[:, _cols(COL_VA, 2)] = dva_ref[...].astype(BF16)
        o_ref[:, _cols(COL_QB, N_HEADS_B)] = (dqb_ref[...] * ATT_SCALE).astype(BF16)
        o_ref[:, _cols(COL_KB, 2)] = (dkb_ref[...] * LN2).astype(BF16)
        o_ref[:, _cols(COL_VB, 2)] = dvb_ref[...].astype(BF16)

        @pl.when(i == 0)
        def _():
            dgq_ref[...] = acc_q
            dgk_ref[...] = acc_k

        @pl.when(i > 0)
        def _():
            dgq_ref[...] += acc_q
            dgk_ref[...] += acc_k

    def rows(width, shift=0):
        return pl.BlockSpec((tm, width), lambda i: (i + shift, 0))

    kv_w = N_KV_A * HEAD_DIM
    q_w = N_HEADS_A * HEAD_DIM
    vec = pl.BlockSpec((1, HEAD_DIM), lambda i: (0, 0))
    res, c_res = _call(
        body, name="qk_bwd", grid=(s // tm,),
        in_specs=[rows(q_w), rows(kv_w), rows(kv_w), rows(q_w), rows(kv_w, lo), rows(kv_w, lo), rows(n),
                  vec, vec, rows(HEAD_DIM), rows(HEAD_DIM)],
        out_specs=[rows(n), vec, vec],
        out_shape=[
            jax.ShapeDtypeStruct((s, n), BF16),
            jax.ShapeDtypeStruct((1, HEAD_DIM), F32),
            jax.ShapeDtypeStruct((1, HEAD_DIM), F32),
        ],
        sem=("arbitrary",), args=(dqa, dka, dva, dqb, dkpad, dvpad, proj, g_q, g_k, cos_t, sin_t), comm=comm,
        after=after)
    return res if comm is None else (res, c_res)


_NT = (((1,), (1,)), ((), ()))
_TN = (((0,), (0,)), ((), ()))


def _attn_a_fwd(pb, *, tq=4096, sub=256, comm=None, after=None):
    s = pb.shape[0]
    tq = min(tq, s)
    sub = min(sub, tq)

    def body(q_ref, k_ref, v_ref, o_ref, lse_ref):
        k = k_ref[...]
        v = v_ref[...]
        for r in range(tq // sub):
            rows = pl.ds(r * sub, sub)
            sc = lax.dot_general(q_ref[rows, :], k, _NT, preferred_element_type=F32)
            m = jnp.max(sc, axis=-1, keepdims=True)
            p = jnp.exp2(sc - m)
            l = jnp.sum(p, axis=-1, keepdims=True)
            o = jnp.dot(p.astype(BF16), v, preferred_element_type=F32)
            o_ref[rows, :] = (o / l).astype(BF16)
            lse_ref[rows, :] = jnp.broadcast_to(m + jnp.log2(l), (sub, HEAD_DIM))

    res, c_res = _call(
        body, name="attn_a_fwd", grid=(N_HEADS_A, s // tq),
        in_specs=[
            pl.BlockSpec((tq, HEAD_DIM), lambda h, i: (i, COL_QA + h)),
            pl.BlockSpec((s, HEAD_DIM), lambda h, i: (0, COL_KA + h // GROUP)),
            pl.BlockSpec((s, HEAD_DIM), lambda h, i: (0, COL_VA + h // GROUP)),
        ],
        out_specs=[
            pl.BlockSpec((tq, HEAD_DIM), lambda h, i: (i, h)),
            pl.BlockSpec((None, tq, HEAD_DIM), lambda h, i: (h, i, 0)),
        ],
        out_shape=[
            jax.ShapeDtypeStruct((s, (N_HEADS_A + N_HEADS_B) * HEAD_DIM), BF16),
            jax.ShapeDtypeStruct((N_HEADS_A, s, HEAD_DIM), F32),
        ],
        sem=("parallel", "parallel"), args=(pb, pb, pb), comm=comm, after=after)
    return res if comm is None else (res, c_res)


def _attn_a_bwd(pb, att, datt, lse, *, tq=2048, sub=256, comm=None):
    s = pb.shape[0]
    tq = min(tq, s)
    sub = min(sub, tq)

    def body(q_ref, k_ref, v_ref, o_ref, do_ref, lse_ref, dq_ref, dk_ref, dv_ref):
        first = jnp.logical_and(pl.program_id(1) == 0, pl.program_id(2) == 0)
        k = k_ref[...]
        v = v_ref[...]
        dk = dv = None
        for r in range(tq // sub):
            rows = pl.ds(r * sub, sub)
            q = q_ref[rows, :]
            do = do_ref[rows, :]
            sc = lax.dot_general(q, k, _NT, preferred_element_type=F32)
            p = jnp.exp2(sc - lse_ref[rows, :][:, :1])
            dp = lax.dot_general(do, v, _NT, preferred_element_type=F32)
            delta = jnp.sum(do.astype(F32) * o_ref[rows, :].astype(F32), axis=-1, keepdims=True)
            ds = (p * (dp - delta)).astype(BF16)
            dq_ref[rows, :] = jnp.dot(ds, k, preferred_element_type=F32)
            dk_r = lax.dot_general(ds, q, _TN, preferred_element_type=F32)
            dv_r = lax.dot_general(p.astype(BF16), do, _TN, preferred_element_type=F32)
            dk = dk_r if dk is None else dk + dk_r
            dv = dv_r if dv is None else dv + dv_r

        @pl.when(first)
        def _():
            dk_ref[...] = dk
            dv_ref[...] = dv

        @pl.when(jnp.logical_not(first))
        def _():
            dk_ref[...] += dk
            dv_ref[...] += dv

    qmap = lambda kv, g, i: (i, kv * GROUP + g)
    res, c_res = _call(
        body, name="attn_a_bwd", grid=(N_KV_A, GROUP, s // tq),
        in_specs=[
            pl.BlockSpec((tq, HEAD_DIM), lambda kv, g, i: (i, COL_QA + kv * GROUP + g)),
            pl.BlockSpec((s, HEAD_DIM), lambda kv, g, i: (0, COL_KA + kv)),
            pl.BlockSpec((s, HEAD_DIM), lambda kv, g, i: (0, COL_VA + kv)),
            pl.BlockSpec((tq, HEAD_DIM), qmap),
            pl.BlockSpec((tq, HEAD_DIM), qmap),
            pl.BlockSpec((None, tq, HEAD_DIM), lambda kv, g, i: (kv * GROUP + g, i, 0)),
        ],
        out_specs=[
            pl.BlockSpec((tq, HEAD_DIM), qmap),
            pl.BlockSpec((s, HEAD_DIM), lambda kv, g, i: (0, kv)),
            pl.BlockSpec((s, HEAD_DIM), lambda kv, g, i: (0, kv)),
        ],
        out_shape=[
            jax.ShapeDtypeStruct((s, N_HEADS_A * HEAD_DIM), F32),
            jax.ShapeDtypeStruct((s, N_KV_A * HEAD_DIM), F32),
            jax.ShapeDtypeStruct((s, N_KV_A * HEAD_DIM), F32),
        ],
        sem=("arbitrary", "arbitrary", "arbitrary"), args=(pb, pb, pb, att, datt, lse), comm=comm)
    return res if comm is None else (res, c_res)


def _t5_bucket(rel):
    nb = N_BUCKETS // 2
    ret = jnp.where(rel > 0, nb, 0)
    n = jnp.abs(rel)
    max_exact = nb // 2
    nf = jnp.maximum(n, 1).astype(F32)
    large = max_exact + (jnp.log(nf / max_exact) / math.log(MAX_DISTANCE / max_exact)
                         * (nb - max_exact)).astype(jnp.int32)
    large = jnp.minimum(large, nb - 1)
    return ret + jnp.where(n < max_exact, n, large)


def _band_buckets():
    r = jnp.arange(BLOCK_Q, dtype=jnp.int32)
    j = jnp.arange(3 * BLOCK_Q, dtype=jnp.int32)
    return _t5_bucket((j[None, :] - BLOCK_Q) - r[:, None])


def _band_bias(bucket, table_ref, h):
    acc = jnp.zeros(bucket.shape, F32)
    for b in range(N_BUCKETS):
        acc = jnp.where(bucket == b, table_ref[b, h], acc)
    return acc


GQ = GROUP * BLOCK_Q


def _stack_heads(x):
    return jnp.concatenate([x[:, _cols(g)] for g in range(GROUP)], axis=0)


def _unstack_heads(x):
    return jnp.concatenate([x[g * BLOCK_Q:(g + 1) * BLOCK_Q] for g in range(GROUP)], axis=1)


def _group_bias(bucket, table_ref, kv):
    r = lax.broadcasted_iota(jnp.int32, (BLOCK_Q, 3 * BLOCK_Q), 0)
    j = lax.broadcasted_iota(jnp.int32, (BLOCK_Q, 3 * BLOCK_Q), 1)
    inside = jnp.abs(j - BLOCK_Q - r) <= WINDOW
    return jnp.concatenate([jnp.where(inside, _band_bias(bucket, table_ref, kv * GROUP + g) * LOG2E, NEG_INF)
                            for g in range(GROUP)], axis=0)


def _group_sink(sink_ref, kv):
    head = lax.broadcasted_iota(jnp.int32, (GQ, 1), 0) // BLOCK_Q
    snk = jnp.zeros((GQ, 1), F32)
    for g in range(GROUP):
        snk = jnp.where(head == g, sink_ref[0, kv * GROUP + g] * LOG2E, snk)
    return snk


def _band_mask(n, s):
    kabs = n * BLOCK_Q + lax.broadcasted_iota(jnp.int32, (1, 3 * BLOCK_Q), 1) - BLOCK_Q
    return (kabs >= 0) & (kabs < s)


def _band_start(n):
    return pl.multiple_of(n * BLOCK_Q + (PAD_LO - BLOCK_Q), BLOCK_Q)


def _attn_b_fwd(pb, kpad, vpad, bucket, table, sink, att, *, comm=None, after=None):
    s = pb.shape[0]
    nblk = s // BLOCK_Q
    sp = kpad.shape[0]

    def body(table_ref, sink_ref, q0_ref, q1_ref, k_ref, v_ref, bucket_ref, _, o_ref, lse_ref, bias_ref):
        n = pl.program_id(0)

        @pl.when(n == 0)
        def _():
            for kv in range(N_KV_B):
                bias_ref[kv * GQ:(kv + 1) * GQ, :] = _group_bias(bucket_ref[...], table_ref, kv)

        band = pl.ds(_band_start(n), 3 * BLOCK_Q)
        mask = _band_mask(n, s)
        for kv, q_ref in enumerate((q0_ref, q1_ref)):
            kb = k_ref[band, _cols(kv)]
            vb = v_ref[band, _cols(kv)]
            sc = lax.dot_general(_stack_heads(q_ref[...]), kb, _NT, preferred_element_type=F32)
            sc = jnp.where(mask, sc + bias_ref[kv * GQ:(kv + 1) * GQ, :], NEG_INF)
            snk = _group_sink(sink_ref, kv)
            m = jnp.maximum(jnp.max(sc, axis=-1, keepdims=True), snk)
            p = jnp.exp2(sc - m)
            l = jnp.sum(p, axis=-1, keepdims=True) + jnp.exp2(snk - m)
            o = jnp.dot(p.astype(BF16), vb, preferred_element_type=F32)
            o_ref[:, _cols(kv * GROUP, GROUP)] = _unstack_heads((o / l).astype(BF16))
            lse = m + jnp.log2(l)
            for g in range(GROUP):
                lse_ref[kv * GROUP + g] = jnp.broadcast_to(lse[g * BLOCK_Q:(g + 1) * BLOCK_Q], (BLOCK_Q, HEAD_DIM))

    smem = pl.BlockSpec(memory_space=pltpu.SMEM)
    wide = GROUP * HEAD_DIM
    whole = pl.BlockSpec((sp, N_KV_B * HEAD_DIM), lambda n: (0, 0))
    res, c_res = _call(
        body, name="attn_b_fwd", grid=(nblk,),
        in_specs=[
            smem,
            smem,
            pl.BlockSpec((BLOCK_Q, wide), lambda n: (n, COL_QB // GROUP)),
            pl.BlockSpec((BLOCK_Q, wide), lambda n: (n, COL_QB // GROUP + 1)),
            whole,
            whole,
            pl.BlockSpec((BLOCK_Q, 3 * BLOCK_Q), lambda n: (0, 0)),
            _ANY,
        ],
        out_specs=[
            pl.BlockSpec((BLOCK_Q, N_HEADS_B * HEAD_DIM), lambda n: (n, 1)),
            pl.BlockSpec((N_HEADS_B, BLOCK_Q, HEAD_DIM), lambda n: (0, n, 0)),
        ],
        out_shape=[
            jax.ShapeDtypeStruct(att.shape, BF16),
            jax.ShapeDtypeStruct((N_HEADS_B, s, HEAD_DIM), F32),
        ],
        scratch_shapes=[pltpu.VMEM((N_KV_B * GQ, 3 * BLOCK_Q), F32)],
        sem=("arbitrary",), args=(table, sink, pb, pb, kpad, vpad, bucket, att), comm=comm, after=after,
        aliases={7: 0})
    return res if comm is None else (res, c_res)


def _attn_b_bwd(pb, kpad, vpad, att, datt, lse, bucket, table, sink, *, comm=None, after=None):
    s = pb.shape[0]
    nblk = s // BLOCK_Q
    sp = kpad.shape[0]

    def body(table_ref, sink_ref, q0_ref, q1_ref, k_ref, v_ref, o_ref, do_ref, lse_ref, bucket_ref,
             dq_ref, dk_ref, dv_ref, dtab_ref, dsink_ref, bias_ref, dbias_ref):
        n = pl.program_id(0)

        @pl.when(n == 0)
        def _():
            dk_ref[...] = jnp.zeros_like(dk_ref)
            dv_ref[...] = jnp.zeros_like(dv_ref)
            dbias_ref[...] = jnp.zeros_like(dbias_ref)
            dsink_ref[...] = jnp.zeros_like(dsink_ref)
            for kv in range(N_KV_B):
                bias_ref[kv * GQ:(kv + 1) * GQ, :] = _group_bias(bucket_ref[...], table_ref, kv)

        band = pl.ds(_band_start(n), 3 * BLOCK_Q)
        mask = _band_mask(n, s)
        for kv, q_ref in enumerate((q0_ref, q1_ref)):
            wide_cols = _cols(kv * GROUP, GROUP)
            q = _stack_heads(q_ref[...])
            do = _stack_heads(do_ref[:, wide_cols])
            o = _stack_heads(o_ref[:, wide_cols])
            kb = k_ref[band, _cols(kv)]
            vb = v_ref[band, _cols(kv)]
            lse = jnp.concatenate([lse_ref[kv * GROUP + g][:, :1] for g in range(GROUP)], axis=0)
            sc = lax.dot_general(q, kb, _NT, preferred_element_type=F32)
            sc = jnp.where(mask, sc + bias_ref[kv * GQ:(kv + 1) * GQ, :], NEG_INF)
            p = jnp.exp2(sc - lse)
            dp = lax.dot_general(do, vb, _NT, preferred_element_type=F32)
            delta = jnp.sum(do.astype(F32) * o.astype(F32), axis=-1, keepdims=True)
            ds = p * (dp - delta)
            dsb = ds.astype(BF16)
            dq_ref[:, wide_cols] = _unstack_heads(jnp.dot(dsb, kb, preferred_element_type=F32))
            dk_ref[band, _cols(kv)] += lax.dot_general(dsb, q, _TN, preferred_element_type=F32)
            dv_ref[band, _cols(kv)] += lax.dot_general(p.astype(BF16), do, _TN, preferred_element_type=F32)
            dbias_ref[kv * GQ:(kv + 1) * GQ, :] += ds
            sink_part = -jnp.exp2(_group_sink(sink_ref, kv) - lse) * delta
            for g in range(GROUP):
                rows = slice(g * BLOCK_Q, (g + 1) * BLOCK_Q)
                dsink_ref[kv * GROUP + g] += jnp.broadcast_to(
                    jnp.sum(sink_part[rows], axis=0, keepdims=True), (1, HEAD_DIM))

        @pl.when(n == nblk - 1)
        def _():
            bucket_v = bucket_ref[...]
            row = lax.broadcasted_iota(jnp.int32, (N_BUCKETS, HEAD_DIM), 0)
            for h in range(N_HEADS_B):
                acc = dbias_ref[h * BLOCK_Q:(h + 1) * BLOCK_Q, :]
                tot = jnp.zeros((N_BUCKETS, HEAD_DIM), F32)
                for b in range(N_BUCKETS):
                    tot = jnp.where(row == b, jnp.sum(jnp.where(bucket_v == b, acc, 0.0), keepdims=True), tot)
                dtab_ref[h] = tot

    smem = pl.BlockSpec(memory_space=pltpu.SMEM)
    wide = GROUP * HEAD_DIM
    whole = pl.BlockSpec((sp, N_KV_B * HEAD_DIM), lambda n: (0, 0))
    group_b = pl.BlockSpec((BLOCK_Q, N_HEADS_B * HEAD_DIM), lambda n: (n, 1))
    res, c_res = _call(
        body, name="attn_b_bwd", grid=(nblk,),
        in_specs=[
            smem,
            smem,
            pl.BlockSpec((BLOCK_Q, wide), lambda n: (n, COL_QB // GROUP)),
            pl.BlockSpec((BLOCK_Q, wide), lambda n: (n, COL_QB // GROUP + 1)),
            whole,
            whole,
            group_b,
            group_b,
            pl.BlockSpec((N_HEADS_B, BLOCK_Q, HEAD_DIM), lambda n: (0, n, 0)),
            pl.BlockSpec((BLOCK_Q, 3 * BLOCK_Q), lambda n: (0, 0)),
        ],
        out_specs=[
            pl.BlockSpec((BLOCK_Q, N_HEADS_B * HEAD_DIM), lambda n: (n, 0)),
            whole,
            whole,
            pl.BlockSpec((N_HEADS_B, N_BUCKETS, HEAD_DIM), lambda n: (0, 0, 0)),
            pl.BlockSpec((N_HEADS_B, 1, HEAD_DIM), lambda n: (0, 0, 0)),
        ],
        out_shape=[
            jax.ShapeDtypeStruct((s, N_HEADS_B * HEAD_DIM), F32),
            jax.ShapeDtypeStruct((sp, N_KV_B * HEAD_DIM), F32),
            jax.ShapeDtypeStruct((sp, N_KV_B * HEAD_DIM), F32),
            jax.ShapeDtypeStruct((N_HEADS_B, N_BUCKETS, HEAD_DIM), F32),
            jax.ShapeDtypeStruct((N_HEADS_B, 1, HEAD_DIM), F32),
        ],
        scratch_shapes=[pltpu.VMEM((N_KV_B * GQ, 3 * BLOCK_Q), F32), pltpu.VMEM((N_KV_B * GQ, 3 * BLOCK_Q), F32)],
        sem=("arbitrary",),
        args=(table, sink, pb, pb, kpad, vpad, att, datt, lse, bucket), comm=comm, after=after)
    return res if comm is None else (res, c_res)


def _other_chips(x, y):
    return [(x, 1 - y), (1 - x, y), (1 - x, 1 - y)]


_HBM = pl.BlockSpec(memory_space=pltpu.HBM)
_SEM = pl.BlockSpec(memory_space=pltpu.SEMAPHORE)
_SPLIT = pltpu.CompilerParams(has_side_effects=pltpu.SideEffectType.DATAFLOW_SIDE_EFFECTING)


def _in_hbm(a):
    return pltpu.with_memory_space_constraint(a, pltpu.HBM)


def _my_half(rows):
    c = lax.axis_index("c")
    half = rows // 2
    return pl.ds(pl.multiple_of(c * half, half), half), pl.ds(pl.multiple_of((1 - c) * half, half), half)


def _gather_route(shapes):
    def route(src, land):
        x, y, c = lax.axis_index("x"), lax.axis_index("y"), lax.axis_index("c")
        out = []
        for t, shape in enumerate(shapes):
            mine, _ = _my_half(shape[0])
            for px, py in _other_chips(x, y):
                out.append((src[t].at[mine], land[t].at[2 * x + y, mine], land[t].at[2 * px + py, mine], (px, py, c)))
        return out

    return route


def _exchange_route(n_t):
    def route(src, land):
        x, y, c = lax.axis_index("x"), lax.axis_index("y"), lax.axis_index("c")
        out = []
        for t in range(n_t):
            for px, py in _other_chips(x, y):
                k = 2 * px + py
                out.append((src[t].at[k], land[t].at[2 * (2 * x + y) + c], land[t].at[2 * k + c], (px, py, c)))
        return out

    return route


def _own_slot(shape, dtype, slot, block):
    return lax.dynamic_update_slice(lax.empty(shape, dtype), block[None], (slot,) + (0,) * (len(shape) - 1))


def _split_start(name, srcs, lands, route, after):
    n = len(srcs)

    def body(*refs):
        src, land, send_sems, recv_sems, token = refs[:n], refs[n:2 * n], refs[2 * n + 1], refs[2 * n + 2], refs[-1]
        for i, (src_ref, dst_ref, _, to) in enumerate(route(src, land)):
            pltpu.make_async_remote_copy(src_ref=src_ref, dst_ref=dst_ref, send_sem=send_sems.at[i],
                                         recv_sem=recv_sems.at[i], device_id=to, device_id_type=_MESH).start()
        token[...] = jnp.zeros_like(token)

    sem = pltpu.SemaphoreType.DMA((3 * n,))
    lands = list(lands)
    res = pl.pallas_call(
        body, name=name,
        in_specs=[_HBM] * (2 * n) + [_ANY],
        out_specs=[_SEM, _SEM] + [_HBM] * (2 * n) + [pl.BlockSpec(memory_space=pltpu.VMEM)],
        out_shape=[sem, sem] + [pltpu.HBM(a.shape, a.dtype) for a in list(srcs) + lands]
        + [jax.ShapeDtypeStruct((8, 128), F32)],
        input_output_aliases={i: 2 + i for i in range(2 * n)},
        compiler_params=_SPLIT,
    )(*[_in_hbm(a) for a in srcs], *[_in_hbm(a) for a in lands], after)
    return (res[0], res[1]), res[2:2 + n], res[2 + n:2 + 2 * n], res[-1]


def _split_wait(name, srcs, lands, sems, route, after):
    n = len(srcs)

    def body(*refs):
        src, land, send_sems, recv_sems = refs[:n], refs[n:2 * n], refs[2 * n], refs[2 * n + 1]
        for i, (src_ref, _, dst_ref, to) in enumerate(route(src, land)):
            cp = pltpu.make_async_remote_copy(src_ref=src_ref, dst_ref=dst_ref, send_sem=send_sems.at[i],
                                              recv_sem=recv_sems.at[i], device_id=to, device_id_type=_MESH)
            cp.wait_send()
            cp.wait_recv()

    res = pl.pallas_call(
        body, name=name,
        in_specs=[_HBM] * (2 * n) + [_SEM, _SEM, _ANY],
        out_specs=[_HBM] * (2 * n),
        out_shape=[pltpu.HBM(a.shape, a.dtype) for a in list(srcs) + list(lands)],
        input_output_aliases={i: i for i in range(2 * n)},
        compiler_params=_SPLIT,
    )(*srcs, *lands, sems[0], sems[1], after)
    return res[:n], res[n:]


def _comm_only(name, comm):
    return _call(lambda: None, name=name, grid=(1,), in_specs=[], out_specs=[], out_shape=[], args=(), comm=comm)[1]


def _swap_comm(shards, lands):
    n_t = len(lands)

    def copies(land, sems, later):
        send_sems, recv_sems = sems
        x, y = lax.axis_index("x"), lax.axis_index("y")
        sends, recvs = [], []
        for t in range(n_t):
            mine, other = _my_half(shards[t].shape[0])
            for j, (px, py) in enumerate(_other_chips(x, y)):
                k = 2 * px + py
                for part, out in ((mine, sends), (other, recvs)) if later else ((mine, sends),):
                    out.append(pltpu.make_async_remote_copy(
                        src_ref=land[t].at[k, part], dst_ref=land[t].at[k, part], send_sem=send_sems.at[3 * t + j],
                        recv_sem=recv_sems.at[3 * t + j], device_id=_sibling(), device_id_type=_MESH))
        return sends, recvs

    def start(ins, land, sems):
        for cp in copies(land, sems, False)[0]:
            cp.start()

    def finish(ins, land, sems):
        sends, recvs = copies(land, sems, True)
        for cp in recvs:
            cp.wait_recv()
        for cp in sends:
            cp.wait_send()

    return _Comm(
        lands, [jax.ShapeDtypeStruct(a.shape, a.dtype) for a in lands],
        [pltpu.SemaphoreType.DMA((3 * n_t,)), pltpu.SemaphoreType.DMA((3 * n_t,))],
        start, finish, aliases={t: t for t in range(n_t)})


def _forward_comm(partials, lands):
    n_t = len(lands)

    def copies(land, sems, later):
        send_sems, recv_sems = sems
        x, y, c = lax.axis_index("x"), lax.axis_index("y"), lax.axis_index("c")
        sends, recvs = [], []
        for t in range(n_t):
            for j, k in enumerate([2 * x + y] + [2 * px + py for px, py in _other_chips(x, y)]):
                for slot, out in ((2 * k + c, sends), (2 * k + 1 - c, recvs)) if later else ((2 * k + c, sends),):
                    out.append(pltpu.make_async_remote_copy(
                        src_ref=land[t].at[slot], dst_ref=land[t].at[slot], send_sem=send_sems.at[4 * t + j],
                        recv_sem=recv_sems.at[4 * t + j], device_id=_sibling(), device_id_type=_MESH))
        return sends, recvs

    def start(ins, land, sems):
        for cp in copies(land, sems, False)[0]:
            cp.start()

    def finish(ins, land, sems):
        sends, recvs = copies(land, sems, True)
        for cp in recvs:
            cp.wait_recv()
        for cp in sends:
            cp.wait_send()

    return _Comm(
        lands, [jax.ShapeDtypeStruct(a.shape, a.dtype) for a in lands],
        [pltpu.SemaphoreType.DMA((4 * n_t,)), pltpu.SemaphoreType.DMA((4 * n_t,))],
        start, finish, aliases={t: t for t in range(n_t)})


def _allreduce_small(pack):
    rows, d = pack.shape

    def body(p_ref, sum_ref, all_ref, send_sems, recv_sems):
        x, y, c = lax.axis_index("x"), lax.axis_index("y"), lax.axis_index("c")
        me = 4 * x + 2 * y + c
        all_ref[me] = p_ref[...]
        peers = []
        for dx in range(2):
            for dy in range(2):
                for dc in range(2):
                    if dx or dy or dc:
                        px = 1 - x if dx else x
                        py = 1 - y if dy else y
                        pc = 1 - c if dc else c
                        peers.append((4 * dx + 2 * dy + dc - 1, (px, py, pc)))
        sends = []
        for k, to in peers:
            cp = pltpu.make_async_remote_copy(
                src_ref=p_ref, dst_ref=all_ref.at[me], send_sem=send_sems.at[k], recv_sem=recv_sems.at[k],
                device_id=to, device_id_type=_MESH)
            cp.start()
            sends.append(cp)
        for k, (px, py, pc) in peers:
            pltpu.make_async_remote_copy(
                src_ref=p_ref, dst_ref=all_ref.at[4 * px + 2 * py + pc], send_sem=send_sems.at[k],
                recv_sem=recv_sems.at[k], device_id=(px, py, pc), device_id_type=_MESH).wait_recv()
        for cp in sends:
            cp.wait_send()
        tot = all_ref[0]
        for i in range(1, N_DEV):
            tot = tot + all_ref[i]
        sum_ref[...] = tot

    vm = pl.BlockSpec(memory_space=pltpu.VMEM)
    return pl.pallas_call(
        body,
        name="allreduce_small",
        in_specs=[vm],
        out_specs=vm,
        out_shape=jax.ShapeDtypeStruct((rows, d), F32),
        scratch_shapes=[
            pltpu.VMEM((N_DEV, rows, d), F32),
            pltpu.SemaphoreType.DMA((N_DEV - 1,)),
            pltpu.SemaphoreType.DMA((N_DEV - 1,)),
        ],
    )(pack)


def _adamw_math(w, g, m, v):
    m = ADAM_B1 * m + (1.0 - ADAM_B1) * g
    v = ADAM_B2 * v + (1.0 - ADAM_B2) * (g * g)
    m_hat = m / (1.0 - ADAM_B1 ** ADAM_STEP)
    v_hat = v / (1.0 - ADAM_B2 ** ADAM_STEP)
    delta = -ADAM_LR * (m_hat / (jnp.sqrt(v_hat) + ADAM_EPS) + ADAM_WD * w)
    return delta, m, v


def _sum_adamw(parts, w, m, v, *, name, tr=512):
    r, c = w.shape
    tr = min(tr, r)
    tc = min(c, 1024)

    def body(p_ref, w_ref, m_ref, v_ref, g_ref, d_ref, m2_ref, v2_ref):
        g = p_ref[0].astype(F32)
        for i in range(1, N_DEV):
            g = g + p_ref[i].astype(F32)
        delta, m2, v2 = _adamw_math(w_ref[...], g, m_ref[...], v_ref[...])
        g_ref[...] = g
        d_ref[...] = delta
        m2_ref[...] = m2
        v2_ref[...] = v2

    blk = pl.BlockSpec((tr, tc), lambda i, j: (i, j))
    return pl.pallas_call(
        body,
        name=name,
        grid=(r // tr, c // tc),
        in_specs=[pl.BlockSpec((N_DEV, tr, tc), lambda i, j: (0, i, j)), blk, blk, blk],
        out_specs=[blk] * 4,
        out_shape=[jax.ShapeDtypeStruct((r, c), F32)] * 4,
        compiler_params=_params(("parallel", "parallel")),
    )(parts, w, m, v)


def _adamw_small(g, w, m, v):
    def body(g_ref, w_ref, m_ref, v_ref, d_ref, m2_ref, v2_ref):
        delta, m2, v2 = _adamw_math(w_ref[...], g_ref[...], m_ref[...], v_ref[...])
        d_ref[...] = delta
        m2_ref[...] = m2
        v2_ref[...] = v2

    vm = pl.BlockSpec(memory_space=pltpu.VMEM)
    return pl.pallas_call(
        body,
        name="adamw_small",
        in_specs=[vm] * 4,
        out_specs=[vm] * 3,
        out_shape=[jax.ShapeDtypeStruct(g.shape, F32)] * 3,
    )(g, w, m, v)


def _relu2_epilogue(acc):
    ra = jnp.maximum(acc, 0.0)
    return ra * ra, ra


def _residual_norm_epilogue(acc, res, g):
    h = acc + res
    return h, h * lax.rsqrt(jnp.mean(h * h, axis=-1, keepdims=True) + EPS) * g


def _rows(stacked):
    return stacked.reshape(stacked.shape[0] * stacked.shape[1], stacked.shape[2])


def _by_chip(mat):
    return mat.reshape(N_CHIPS, mat.shape[0] // N_CHIPS, mat.shape[1])


def _local_step(x, p, target, shards, small, update):
    s, d = x.shape
    cos_t, sin_t = _rope_tables(s)
    bucket = _band_buckets()
    p_bf = p.astype(BF16)
    wts = {}

    chip = 2 * lax.axis_index("x") + lax.axis_index("y")
    core = lax.axis_index("c")

    def gather(tag, names, after):
        srcs = [cast[n] for n in names]
        route = _gather_route([a.shape for a in srcs])
        sems, srcs, lands, token = _split_start(f"gather_start_{tag}", srcs, [zones[n] for n in names], route, after)

        def landed(done):
            got_srcs, got_lands = _split_wait(f"gather_wait_{tag}", srcs, lands, sems, route, done)
            comm = _swap_comm(got_srcs, got_lands)
            comm.waited = got_srcs[0]
            return comm

        return landed, token

    def prepare(n, zero):
        cast[n] = (shards[n] + zero).astype(BF16)
        zones[n] = _own_slot((N_CHIPS,) + cast[n].shape, BF16, chip, cast[n])

    cast, zones = {}, {}
    prepare("w_in", 0.0)
    in_landed, token = gather("in", ["w_in"], small["attn_norm_g"])
    for n in shards:
        if n != "w_in":
            prepare(n, token[:1, :1])
    g_attn = small["attn_norm_g"] + token[:1, :1]
    u = _rms_fwd(x, g_attn, name="norm_attn")
    prepared = u[:1, :1].astype(F32) + sum(
        (lax.dynamic_slice(zones[n], (chip, 0, 0), (1, 1, 1))[0] + cast[n][:1, :1]).astype(F32)
        for n in zones if n != "w_in")
    (wts["w_in"],) = _comm_only("swap_w_in", in_landed(prepared))
    mid_landed, token = gather("mid", ["w_out"], wts["w_in"])
    proj = _matmul(u, wts["w_in"], mode="nn", out_dtypes=[F32], name="mm_in", bn=768, after=token)
    pb, (w_out_s,) = _qk_prep(proj, small["q_norm_g"], small["k_norm_g"], cos_t, sin_t, comm=mid_landed(proj))
    wts["w_out"] = _rows(w_out_s)
    up_landed, token = gather("up", ["w_up"], pb)
    att_a, lse_a = _attn_a_fwd(pb, after=token)
    pad = ((PAD_LO, PAD_HI), (0, 0))
    kpad = jnp.pad(pb[:, COL_KB * HEAD_DIM:COL_VB * HEAD_DIM], pad)
    vpad = jnp.pad(pb[:, COL_VB * HEAD_DIM:], pad)
    up_swap = up_landed(att_a)
    down_landed, token = gather("down", ["w_down"], up_swap.waited)
    (att, lse_b), (wts["w_up"],) = _attn_b_fwd(pb, kpad, vpad, bucket, small["rel_bias_table"],
                                               small["sink_logits"], att_a, comm=up_swap, after=token)
    h1, mn = _matmul(att, wts["w_out"], mode="nn", out_dtypes=[F32, BF16], name="mm_out", bm=512, bn=d,
                     epilogue=_residual_norm_epilogue, extras=(x,), vecs=(small["mlp_norm_g"],))
    r, ra = _matmul(mn, wts["w_up"], mode="nn", out_dtypes=[BF16, BF16], name="mm_up", epilogue=_relu2_epilogue,
                    bm=2048)
    (w_down_s,) = _comm_only("swap_w_down", down_landed(r))
    wts["w_down"] = _rows(w_down_s)
    late_landed, token = gather("late", ["w_gate", "ple_w"], w_down_s)
    h2 = _matmul(r, wts["w_down"], mode="nn", out_dtypes=[F32], name="mm_down",
                 epilogue=lambda acc, res: (acc + res,), extras=(h1,), after=token)
    ng, (w_gate_s, wts["ple_w"]) = _rms_fwd(h2, small["gate_norm_g"], name="norm_gate", comm=late_landed(h2))
    wts["w_gate"] = _rows(w_gate_s)
    gate = _matmul(ng, wts["w_gate"], mode="nn", out_dtypes=[F32], name="mm_gate",
                   epilogue=lambda acc: (1.0 / (1.0 + jnp.exp(-acc)),))
    pp = _matmul(p_bf, wts["ple_w"], mode="nn", out_dtypes=[F32], name="mm_ple", bn=512)
    dh3, dz, dpp, dg_final, dg_ple, loss = _tail(h2, gate, pp, target, small["ple_norm_g"], small["final_norm_g"])

    dng = _matmul(dz, wts["w_gate"], mode="nt", out_dtypes=[F32], name="mm_gate_dx")
    gw_gate = _matmul(ng, dz, mode="tn", out_dtypes=[BF16], name="mm_gate_dw", bm=512, bk=4096)
    gw_ple = _matmul(p_bf, dpp, mode="tn", out_dtypes=[BF16], name="mm_ple_dw", bn=512, out_stack=N_CHIPS)
    dh2, dh2_bf, dg_gate = _rms_bwd(h2, dng, small["gate_norm_g"], dh3, name="norm_gate_bwd", want_bf16=True)

    def exchange(tag, partials, after):
        route = _exchange_route(len(partials))
        lands = [_own_slot((N_DEV,) + g.shape[1:], g.dtype, 2 * chip + core,
                           lax.dynamic_index_in_dim(g, chip, 0, keepdims=False)) for g in partials]
        sems, srcs, lands, token = _split_start(f"exchange_start_{tag}", partials, lands, route, after)

        def landed(done):
            got_srcs, got_lands = _split_wait(f"exchange_wait_{tag}", srcs, lands, sems, route, done)
            comm = _forward_comm(got_srcs, got_lands)
            comm.waited = got_srcs[0]
            return comm

        return landed, token

    big = {}
    gate_landed, token = exchange("gate", [_by_chip(gw_gate), gw_ple], dh2_bf)
    gw_down = _matmul(r, dh2_bf, mode="tn", out_dtypes=[BF16], name="mm_down_dw", after=token, bm=512, bk=4096)
    da, (parts_gate, parts_ple) = _matmul(
        dh2_bf, wts["w_down"], mode="nt", out_dtypes=[BF16], name="mm_down_dx", bm=2048,
        epilogue=lambda acc, ra_v: (acc * (2.0 * ra_v.astype(F32)),), extras=(ra,), comm=gate_landed(gw_down))
    down_landed, token = exchange("down", [_by_chip(gw_down)], da)
    big["w_gate"], big["ple_w"] = update("w_gate", parts_gate), update("ple_w", parts_ple)
    gw_up = _matmul(mn, da, mode="tn", out_dtypes=[BF16], name="mm_up_dw", out_stack=N_CHIPS, after=token,
                    bm=512, bk=4096)
    dmn = _matmul(da, wts["w_up"], mode="nt", out_dtypes=[F32], name="mm_up_dx", bk=4096,
                  after=gw_up)
    dh1, dh1_bf, dg_mlp = _rms_bwd(h1, dmn, small["mlp_norm_g"], dh2, name="norm_mlp_bwd", want_bf16=True)
    down_forward = down_landed(dh1_bf)
    up_landed, token = exchange("up", [gw_up], down_forward.waited)
    datt = _matmul(dh1_bf, wts["w_out"], mode="nt", out_dtypes=[BF16], name="mm_out_dx", after=token)
    gw_out = _matmul(att, dh1_bf, mode="tn", out_dtypes=[BF16], name="mm_out_dw", bm=512, bk=4096)
    dqb, dkpad, dvpad, dtab, dsink = _attn_b_bwd(pb, kpad, vpad, att, datt, lse_b, bucket,
                                                 small["rel_bias_table"], small["sink_logits"])
    (dqa, dka, dva), (parts_down,) = _attn_a_bwd(pb, att, datt, lse_a, comm=down_forward)
    up_forward = up_landed(dqa)
    out_landed, token = exchange("out", [_by_chip(gw_out)], up_forward.waited)
    (dproj, dg_q, dg_k), (parts_up,) = _qk_bwd(dqa, dka, dva, dqb, dkpad, dvpad, proj,
                                               small["q_norm_g"], small["k_norm_g"], cos_t, sin_t,
                                               comm=up_forward, after=token)
    gw_in = _matmul(u, dproj, mode="tn", out_dtypes=[BF16], name="mm_in_dw", bn=768, out_stack=N_CHIPS,
                    bm=512, bk=4096)
    out_forward = out_landed(gw_in)
    in_landed, token = exchange("in", [gw_in], out_forward.waited)
    du, (parts_out,) = _matmul(dproj, wts["w_in"], mode="nt", out_dtypes=[F32], name="mm_in_dx", bk=3072,
                               comm=out_forward, after=token)
    grad_x, dg_attn = _rms_bwd(x, du, small["attn_norm_g"], dh1, name="norm_attn_bwd", want_bf16=False)
    for n, parts in (("w_down", parts_down), ("w_up", parts_up), ("w_out", parts_out)):
        big[n] = update(n, parts)
    done = dg_attn + sum(big[n][0][0, :1, :] for n in ("w_down", "w_up", "w_out"))
    (parts_in,) = _comm_only("forward_w_in", in_landed(done))
    big["w_in"] = update("w_in", parts_in)

    small_g = {
        "attn_norm_g": dg_attn, "mlp_norm_g": dg_mlp, "ple_norm_g": dg_ple, "gate_norm_g": dg_gate,
        "final_norm_g": dg_final, "q_norm_g": dg_q, "k_norm_g": dg_k,
        "sink_logits": dsink[:, 0, 0][None, :], "rel_bias_table": dtab[:, :, 0].T,
    }
    return loss, grad_x, big, small_g


_SMALL_ROWS = ["attn_norm_g", "mlp_norm_g", "ple_norm_g", "gate_norm_g", "final_norm_g"]
_PACK_ROWS = 8


def _pack_small(vals, d):
    rows = [vals[n].reshape(1, d) for n in _SMALL_ROWS]
    misc = jnp.concatenate([
        vals["q_norm_g"].reshape(1, HEAD_DIM), vals["k_norm_g"].reshape(1, HEAD_DIM),
        jnp.pad(vals["sink_logits"].reshape(1, N_HEADS_B), ((0, 0), (0, HEAD_DIM - N_HEADS_B))),
        vals["rel_bias_table"].reshape(1, N_BUCKETS * N_HEADS_B)], axis=1)
    rows.append(jnp.pad(misc, ((0, 0), (0, d - misc.shape[1]))))
    rows.append(jnp.zeros((_PACK_ROWS - len(rows), d), F32))
    return jnp.concatenate(rows, axis=0).astype(F32)


def _unpack_small(pack, shapes):
    out = {n: pack[i].reshape(shapes[n]) for i, n in enumerate(_SMALL_ROWS)}
    misc = pack[len(_SMALL_ROWS)]
    out["q_norm_g"] = misc[:HEAD_DIM].reshape(shapes["q_norm_g"])
    out["k_norm_g"] = misc[HEAD_DIM:2 * HEAD_DIM].reshape(shapes["k_norm_g"])
    out["sink_logits"] = misc[2 * HEAD_DIM:2 * HEAD_DIM + N_HEADS_B].reshape(shapes["sink_logits"])
    out["rel_bias_table"] = misc[3 * HEAD_DIM:3 * HEAD_DIM + N_BUCKETS * N_HEADS_B].reshape(shapes["rel_bias_table"])
    return out


_WEIGHTS = ["attn_norm_g", "w_in", "q_norm_g", "k_norm_g", "sink_logits", "w_out", "mlp_norm_g", "w_up", "w_down",
            "ple_w", "ple_norm_g", "gate_norm_g", "w_gate", "rel_bias_table", "final_norm_g"]
_BIG = ["w_in", "w_out", "w_up", "w_down", "ple_w", "w_gate"]


def kernel(x, p, attn_norm_g, w_in, q_norm_g, k_norm_g, sink_logits, w_out, mlp_norm_g, w_up, w_down, ple_w, ple_norm_g, gate_norm_g, w_gate, rel_bias_table, final_norm_g, loss_target, m_attn_norm_g, m_w_in, m_q_norm_g, m_k_norm_g, m_sink_logits, m_w_out, m_mlp_norm_g, m_w_up, m_w_down, m_ple_w, m_ple_norm_g, m_gate_norm_g, m_w_gate, m_rel_bias_table, m_final_norm_g, v_attn_norm_g, v_w_in, v_q_norm_g, v_k_norm_g, v_sink_logits, v_w_out, v_mlp_norm_g, v_w_up, v_w_down, v_ple_w, v_ple_norm_g, v_gate_norm_g, v_w_gate, v_rel_bias_table, v_final_norm_g):
    given = dict(locals())
    w = {n: given[n] for n in _WEIGHTS}
    m = {n: given["m_" + n] for n in _WEIGHTS}
    v = {n: given["v_" + n] for n in _WEIGHTS}
    d = x.shape[-1]

    shards = {n: w[n][0] for n in _BIG}
    small = {
        "attn_norm_g": w["attn_norm_g"], "mlp_norm_g": w["mlp_norm_g"], "ple_norm_g": w["ple_norm_g"],
        "gate_norm_g": w["gate_norm_g"], "final_norm_g": w["final_norm_g"].reshape(1, d),
        "q_norm_g": w["q_norm_g"], "k_norm_g": w["k_norm_g"], "sink_logits": w["sink_logits"],
        "rel_bias_table": w["rel_bias_table"],
    }

    def update(n, parts):
        res = _sum_adamw(parts, w[n][0], m[n][0], v[n][0], name="adamw_" + n)
        return [t.reshape(w[n].shape) for t in res]

    loss_part, grad_x, big, small_g = _local_step(x[0], p[0, 0], loss_target[0], shards, small, update)
    grads, deltas, new_m, new_v = [{n: big[n][i] for n in _BIG} for i in range(4)]

    shapes = {n: w[n].shape for n in _WEIGHTS if n not in _BIG}
    pack = _pack_small(small_g, d)
    pack = pack.at[_PACK_ROWS - 1, :1].add(0.0 * grads["w_in"][0, 0, :1])
    pack = pack.at[_PACK_ROWS - 1, 1].set(loss_part[0, 0])
    g_small = _allreduce_small(pack)
    loss = g_small[_PACK_ROWS - 1, 1]
    d_small, m_small, v_small = _adamw_small(g_small, _pack_small(w, d), _pack_small(m, d), _pack_small(v, d))
    grads.update(_unpack_small(g_small, shapes))
    deltas.update(_unpack_small(d_small, shapes))
    new_m.update(_unpack_small(m_small, shapes))
    new_v.update(_unpack_small(v_small, shapes))

    return (loss, grad_x[None], *[grads[n] for n in _WEIGHTS], *[deltas[n] for n in _WEIGHTS],
            *[new_m[n] for n in _WEIGHTS], *[new_v[n] for n in _WEIGHTS])
```

```python
import functools
import math

import jax
import jax.numpy as jnp
from jax import lax
from jax.experimental import pallas as pl
from jax.experimental.pallas import tpu as pltpu

F32 = jnp.float32
BF16 = jnp.bfloat16

HEAD_DIM = 128
N_HEADS_A = 8
N_KV_A = 2
N_HEADS_B = 8
N_KV_B = 2
GROUP = 4
GRID_W = 64
BLOCK_Q = 128
WINDOW = 128
N_BUCKETS = 32
MAX_DISTANCE = 128
ROPE_THETA = 10000.0
EPS = 1e-6
NEG_INF = -1e30
ATT_SCALE = HEAD_DIM ** -0.5
LOG2E = math.log2(math.e)
LN2 = math.log(2.0)
Q_SCALE = ATT_SCALE * LOG2E
PAD_LO, PAD_HI = 256, 128
ADAM_LR = 0.001
ADAM_B1 = 0.9
ADAM_B2 = 0.999
ADAM_EPS = 1e-08
ADAM_WD = 0.01
ADAM_STEP = 10

N_CHIPS = 4
N_DEV = 8
COL_QA, COL_KA, COL_VA, COL_QB, COL_KB, COL_VB = 0, 8, 10, 12, 20, 22

VMEM_LIMIT = 52 * 1024 * 1024


def _params(sem=None, collective_id=None):
    return pltpu.CompilerParams(dimension_semantics=sem, vmem_limit_bytes=VMEM_LIMIT, collective_id=collective_id)


_ANY = pl.BlockSpec(memory_space=pl.ANY)
_MESH = pl.DeviceIdType.MESH
SIBLING_BARRIER_ID = 1


def _sibling():
    return (lax.axis_index("x"), lax.axis_index("y"), 1 - lax.axis_index("c"))


class _Comm:
    def __init__(self, inputs, out_shapes, sems, start, finish, aliases=None):
        self.inputs, self.out_shapes, self.sems = list(inputs), list(out_shapes), list(sems)
        self.start, self.finish, self.aliases = start, finish, dict(aliases or {})


def _call(body, *, name, grid, in_specs, out_specs, out_shape, args, scratch_shapes=(), sem=None, comm=None,
          after=None, aliases=None):
    in_specs, out_specs, out_shape = list(in_specs), list(out_specs), list(out_shape)
    scratch_shapes = list(scratch_shapes)
    n_in, n_out, n_sc = len(in_specs), len(out_specs), len(scratch_shapes)
    behind = [] if after is None else [after]
    aliases = dict(aliases or {})
    if comm is None:
        res = pl.pallas_call(
            (lambda *refs: body(*refs[:n_in], *refs[n_in + len(behind):])) if behind else body,
            name=name, grid=grid, in_specs=in_specs + [_ANY] * len(behind), out_specs=out_specs,
            out_shape=out_shape, scratch_shapes=scratch_shapes, input_output_aliases=aliases,
            compiler_params=_params(sem))(*args, *behind)
        return list(res), []
    c_in, c_out = len(comm.inputs), len(comm.out_shapes)

    def hosted(*refs):
        pos = [0]

        def take(n):
            pos[0] += n
            return refs[pos[0] - n:pos[0]]

        ins, c_ins, _, outs, c_outs, scr = (take(n_in), take(c_in), take(len(behind)), take(n_out), take(c_out),
                                            take(n_sc))
        c_sems = refs[pos[0]:]
        ids = [pl.program_id(a) for a in range(len(grid))]
        first = functools.reduce(jnp.logical_and, [i == 0 for i in ids])
        last = functools.reduce(jnp.logical_and, [i == g - 1 for i, g in zip(ids, grid)])

        @pl.when(first)
        def _():
            barrier = pltpu.get_barrier_semaphore()
            pl.semaphore_signal(barrier, inc=1, device_id=_sibling(), device_id_type=_MESH)
            pl.semaphore_wait(barrier, 1)
            comm.start(c_ins, c_outs, c_sems)

        body(*ins, *outs, *scr)

        @pl.when(last)
        def _():
            comm.finish(c_ins, c_outs, c_sems)

    res = pl.pallas_call(
        hosted, name=name, grid=grid, in_specs=in_specs + [_ANY] * (c_in + len(behind)),
        out_specs=out_specs + [_ANY] * c_out,
        out_shape=out_shape + comm.out_shapes, scratch_shapes=scratch_shapes + comm.sems,
        input_output_aliases={**aliases, **{n_in + i: n_out + o for i, o in comm.aliases.items()}},
        compiler_params=_params(("arbitrary",) * len(grid), SIBLING_BARRIER_ID))(*args, *comm.inputs, *behind)
    return list(res[:n_out]), list(res[n_out:])


def _matmul(a, b, *, mode, out_dtypes, name, epilogue=None, extras=(), bm=1024, bn=1024, bk=2048,
            out_stack=0, comm=None, after=None, vecs=()):
    stacked = b.ndim == 3
    if mode == "nn":
        m, k = a.shape
        if stacked:
            nj, kb, ns = b.shape
            n, ks = nj * ns, k
        else:
            kb, n = b.shape
            ns, ks = n, k
        dn = (((1,), (0,)), ((), ()))
    elif mode == "nt":
        m, k = a.shape
        if stacked:
            nj, n, ks = b.shape
            kb = nj * ks
        else:
            n, kb = b.shape
            ks = kb
        ns = n
        dn = (((1,), (1,)), ((), ()))
    else:
        k, m = a.shape
        kb, n = b.shape
        ns, ks = n, k
        dn = (((0,), (0,)), ((), ()))
    assert k == kb and not (stacked and mode == "tn")
    ns_out = n // out_stack if out_stack else n
    per_blk = min(bk, k) // ks if stacked and mode == "nt" and bk > ks else 0
    bm, bn, bk = min(bm, m), min(bn, ns, ns_out), per_blk * ks if per_blk else min(bk, ks)
    assert m % bm == 0 and ns % bn == 0 and ns_out % bn == 0 and (k % bk == 0 if per_blk else ks % bk == 0)
    gm, gn, gk = m // bm, n // bn, k // bk

    if mode == "tn":
        a_spec = pl.BlockSpec((bk, bm), lambda i, j, q: (q, i))
    else:
        a_spec = pl.BlockSpec((bm, bk), lambda i, j, q: (i, q))
    if mode == "nt":
        if per_blk:
            b_spec = pl.BlockSpec((per_blk, bn, ks), lambda i, j, q: (q, j, 0))
        elif stacked:
            per = ks // bk
            b_spec = pl.BlockSpec((None, bn, bk), lambda i, j, q: (q // per, j, q % per))
        else:
            b_spec = pl.BlockSpec((bn, bk), lambda i, j, q: (j, q))
    else:
        if stacked:
            per = ns // bn
            b_spec = pl.BlockSpec((None, bk, bn), lambda i, j, q: (j // per, q, j % per))
        else:
            b_spec = pl.BlockSpec((bk, bn), lambda i, j, q: (q, j))
    ex_spec = pl.BlockSpec((bm, bn), lambda i, j, q: (i, j))
    if out_stack:
        per_o = ns_out // bn
        o_spec = pl.BlockSpec((None, bm, bn), lambda i, j, q: (j // per_o, i, j % per_o))
        o_shape = (out_stack, m, ns_out)
    else:
        o_spec = ex_spec
        o_shape = (m, n)
    n_ex, n_out = len(extras) + len(vecs), len(out_dtypes)

    def body(a_ref, b_ref, *rest):
        ex, outs = rest[:n_ex], rest[n_ex:n_ex + n_out]
        if per_blk:
            part = sum(lax.dot_general(a_ref[:, t * ks:(t + 1) * ks], b_ref[t], dn, preferred_element_type=F32)
                       for t in range(per_blk))
        else:
            part = lax.dot_general(a_ref[...], b_ref[...], dn, preferred_element_type=F32)

        def finish(acc):
            res = epilogue(acc, *[e[...] for e in ex]) if epilogue else (acc,)
            for o, r in zip(outs, res):
                o[...] = r.astype(o.dtype)

        if gk == 1:
            finish(part)
        else:
            acc_ref = rest[-1]
            q = pl.program_id(2)

            @pl.when(q == 0)
            def _():
                acc_ref[...] = part

            @pl.when(q > 0)
            def _():
                acc_ref[...] += part

            @pl.when(q == gk - 1)
            def _():
                finish(acc_ref[...])

    res, c_res = _call(
        body, name=name, grid=(gm, gn, gk),
        in_specs=[a_spec, b_spec] + [ex_spec] * len(extras)
        + [pl.BlockSpec((1, bn), lambda i, j, q: (0, j))] * len(vecs),
        out_specs=[o_spec] * n_out,
        out_shape=[jax.ShapeDtypeStruct(o_shape, dt) for dt in out_dtypes],
        scratch_shapes=[pltpu.VMEM((bm, bn), F32)] if gk > 1 else [],
        sem=("parallel", "parallel", "arbitrary"), args=(a, b, *extras, *vecs), comm=comm, after=after)
    res = res[0] if n_out == 1 else res
    return res if comm is None else (res, c_res)


def _rms_fwd(x, g, *, name, tm=512, comm=None):
    s, d = x.shape
    tm = min(tm, s)

    def body(x_ref, g_ref, o_ref):
        xf = x_ref[...]
        r = lax.rsqrt(jnp.mean(xf * xf, axis=-1, keepdims=True) + EPS)
        o_ref[...] = (xf * r * g_ref[...]).astype(o_ref.dtype)

    res, c_res = _call(
        body, name=name, grid=(s // tm,),
        in_specs=[pl.BlockSpec((tm, d), lambda i: (i, 0)), pl.BlockSpec((1, d), lambda i: (0, 0))],
        out_specs=[pl.BlockSpec((tm, d), lambda i: (i, 0))],
        out_shape=[jax.ShapeDtypeStruct((s, d), BF16)],
        sem=("parallel",), args=(x, g), comm=comm)
    return res[0] if comm is None else (res[0], c_res)


def _rms_bwd(x, dy, g, add, *, name, want_bf16, tm=512):
    s, d = x.shape
    tm = min(tm, s)

    def body(x_ref, dy_ref, g_ref, add_ref, dx_ref, *rest):
        dg_ref = rest[-1]
        i = pl.program_id(0)
        xf = x_ref[...]
        dyf = dy_ref[...].astype(F32)
        r = lax.rsqrt(jnp.mean(xf * xf, axis=-1, keepdims=True) + EPS)
        xh = xf * r
        dyg = dyf * g_ref[...]
        dx = r * (dyg - xh * jnp.mean(dyg * xh, axis=-1, keepdims=True))
        tot = add_ref[...] + dx
        dx_ref[...] = tot
        if want_bf16:
            rest[0][...] = tot.astype(BF16)
        part = jnp.sum(dyf * xh, axis=0, keepdims=True)

        @pl.when(i == 0)
        def _():
            dg_ref[...] = part

        @pl.when(i > 0)
        def _():
            dg_ref[...] += part

    row = pl.BlockSpec((tm, d), lambda i: (i, 0))
    vec = pl.BlockSpec((1, d), lambda i: (0, 0))
    out_specs = [row] + ([row] if want_bf16 else []) + [vec]
    out_shape = [jax.ShapeDtypeStruct((s, d), F32)]
    if want_bf16:
        out_shape.append(jax.ShapeDtypeStruct((s, d), BF16))
    out_shape.append(jax.ShapeDtypeStruct((1, d), F32))
    return pl.pallas_call(
        body,
        name=name,
        grid=(s // tm,),
        in_specs=[row, row, vec, row],
        out_specs=out_specs,
        out_shape=out_shape,
        compiler_params=_params(("arbitrary",)),
    )(x, dy, g, add)


def _tail(h2, gate, pp, target, g_ple, g_final, *, tm=256):
    s, d = h2.shape
    tm = min(tm, s)

    def body(h2_ref, gate_ref, pp_ref, t_ref, gp_ref, gf_ref, dh3_ref, dz_ref, dpp_ref, dgf_ref, dgp_ref, loss_ref):
        i = pl.program_id(0)
        ppf = pp_ref[...]
        gate_v = gate_ref[...]
        r_p = lax.rsqrt(jnp.mean(ppf * ppf, axis=-1, keepdims=True) + EPS)
        eh = ppf * r_p
        e = eh * gp_ref[...]
        h3 = h2_ref[...] + gate_v * e
        r_f = lax.rsqrt(jnp.mean(h3 * h3, axis=-1, keepdims=True) + EPS)
        yh = h3 * r_f
        diff = yh * gf_ref[...] - t_ref[...]
        loss_part = 0.5 * jnp.sum(jnp.mean(diff * diff, axis=-1, keepdims=True), axis=0, keepdims=True)
        dy = diff / d
        dgf = jnp.sum(dy * yh, axis=0, keepdims=True)
        dyg = dy * gf_ref[...]
        dh3 = r_f * (dyg - yh * jnp.mean(dyg * yh, axis=-1, keepdims=True))
        dh3_ref[...] = dh3
        de = dh3 * gate_v
        dz_ref[...] = (dh3 * e * gate_v * (1.0 - gate_v)).astype(BF16)
        dgp = jnp.sum(de * eh, axis=0, keepdims=True)
        deg = de * gp_ref[...]
        dpp_ref[...] = (r_p * (deg - eh * jnp.mean(deg * eh, axis=-1, keepdims=True))).astype(BF16)
        loss_row = jnp.broadcast_to(loss_part, (1, 128))

        @pl.when(i == 0)
        def _():
            dgf_ref[...] = dgf
            dgp_ref[...] = dgp
            loss_ref[...] = loss_row

        @pl.when(i > 0)
        def _():
            dgf_ref[...] += dgf
            dgp_ref[...] += dgp
            loss_ref[...] += loss_row

    row = pl.BlockSpec((tm, d), lambda i: (i, 0))
    vec = pl.BlockSpec((1, d), lambda i: (0, 0))
    return pl.pallas_call(
        body,
        name="tail_fwd_bwd",
        grid=(s // tm,),
        in_specs=[row, row, row, row, vec, vec],
        out_specs=[row, row, row, vec, vec, pl.BlockSpec((1, 128), lambda i: (0, 0))],
        out_shape=[
            jax.ShapeDtypeStruct((s, d), F32),
            jax.ShapeDtypeStruct((s, d), BF16),
            jax.ShapeDtypeStruct((s, d), BF16),
            jax.ShapeDtypeStruct((1, d), F32),
            jax.ShapeDtypeStruct((1, d), F32),
            jax.ShapeDtypeStruct((1, 128), F32),
        ],
        compiler_params=_params(("arbitrary",)),
    )(h2, gate, pp, target, g_ple, g_final)


def _rope_tables(s):
    rows = s // GRID_W
    half = HEAD_DIM // 2
    inv_freq = ROPE_THETA ** (-jnp.arange(0, half, 2, dtype=F32) / half)
    ang_r = jnp.arange(rows, dtype=jnp.int32).astype(F32)[:, None] * inv_freq
    ang_c = jnp.arange(GRID_W, dtype=jnp.int32).astype(F32)[:, None] * inv_freq
    cr, sr = (jnp.repeat(t, GRID_W, axis=0) for t in (jnp.cos(ang_r), jnp.sin(ang_r)))
    cc, sc = (jnp.tile(t, (rows, 1)) for t in (jnp.cos(ang_c), jnp.sin(ang_c)))
    cos_t = jnp.concatenate([cr, cr, cc, cc], axis=-1)
    sin_t = jnp.concatenate([-sr, sr, -sc, sc], axis=-1)
    return cos_t, sin_t


def _low_quarters(shape):
    return (lax.broadcasted_iota(jnp.int32, shape, len(shape) - 1) % 64) < 32


def _swap_quarters(x, low):
    up = pltpu.roll(x, HEAD_DIM - 32, x.ndim - 1)
    down = pltpu.roll(x, 32, x.ndim - 1)
    return jnp.where(low, up, down)


def _cols(first, count=1):
    return slice(first * HEAD_DIM, (first + count) * HEAD_DIM)


def _qk_prep(proj, g_q, g_k, cos_t, sin_t, *, tm=512, comm=None):
    s, n = proj.shape
    tm = min(tm, s)

    def body(x_ref, gq_ref, gk_ref, c_ref, s_ref, o_ref):
        cos_v, sin_v = c_ref[...], s_ref[...]
        low = _low_quarters(cos_v.shape)
        for h in range(COL_VA):
            x = x_ref[:, _cols(h)]
            g = gq_ref[...] if h < COL_KA else gk_ref[...]
            xn = x * lax.rsqrt(jnp.mean(x * x, axis=-1, keepdims=True) + EPS) * g
            xr = xn * cos_v + _swap_quarters(xn, low) * sin_v
            if h < COL_KA:
                xr = xr * Q_SCALE
            o_ref[:, _cols(h)] = xr.astype(BF16)
        o_ref[:, _cols(COL_VA, 2)] = x_ref[:, _cols(COL_VA, 2)].astype(BF16)
        o_ref[:, _cols(COL_QB, N_HEADS_B)] = (x_ref[:, _cols(COL_QB, N_HEADS_B)] * Q_SCALE).astype(BF16)
        o_ref[:, _cols(COL_KB, 4)] = x_ref[:, _cols(COL_KB, 4)].astype(BF16)

    row = pl.BlockSpec((tm, n), lambda i: (i, 0))
    tab = pl.BlockSpec((tm, HEAD_DIM), lambda i: (i, 0))
    vec = pl.BlockSpec((1, HEAD_DIM), lambda i: (0, 0))
    res, c_res = _call(
        body, name="qk_prep", grid=(s // tm,),
        in_specs=[row, vec, vec, tab, tab],
        out_specs=[row],
        out_shape=[jax.ShapeDtypeStruct((s, n), BF16)],
        sem=("parallel",), args=(proj, g_q, g_k, cos_t, sin_t), comm=comm)
    return res[0] if comm is None else (res[0], c_res)


def _qk_bwd(dqa, dka, dva, dqb, dkpad, dvpad, proj, g_q, g_k, cos_t, sin_t, *, comm=None, after=None):
    s, n = proj.shape
    tm = min(PAD_LO, s)
    assert PAD_LO % tm == 0
    lo = PAD_LO // tm

    def body(dqa_ref, dka_ref, dva_ref, dqb_ref, dkb_ref, dvb_ref, x_ref, gq_ref, gk_ref, c_ref, s_ref,
             o_ref, dgq_ref, dgk_ref):
        i = pl.program_id(0)
        cos_v, sin_v = c_ref[...], s_ref[...]
        low = _low_quarters(cos_v.shape)

        def head(d, x, g):
            dn = d * cos_v + _swap_quarters(d * sin_v, low)
            r = lax.rsqrt(jnp.mean(x * x, axis=-1, keepdims=True) + EPS)
            xh = x * r
            dng = dn * g
            dx = r * (dng - xh * jnp.mean(dng * xh, axis=-1, keepdims=True))
            return dx.astype(BF16), jnp.sum(dn * xh, axis=0, keepdims=True)

        acc_q = jnp.zeros((1, HEAD_DIM), F32)
        acc_k = jnp.zeros((1, HEAD_DIM), F32)
        for h in range(N_HEADS_A):
            o_ref[:, _cols(h)], part = head(dqa_ref[:, _cols(h)] * ATT_SCALE, x_ref[:, _cols(h)], gq_ref[...])
            acc_q = acc_q + part
        for h in range(N_KV_A):
            o_ref[:, _cols(COL_KA + h)], part = head(dka_ref[:, _cols(h)] * LN2, x_ref[:, _cols(COL_KA + h)],
                                                     gk_ref[...])
            acc_k = acc_k + part
        o_ref[:, _cols(COL_VA, 2)] = dva_ref[...].astype(BF16)
        o_ref[:, _cols(COL_QB, N_HEADS_B)] = (dqb_ref[...] * ATT_SCALE).astype(BF16)
        o_ref[:, _cols(COL_KB, 2)] = (dkb_ref[...] * LN2).astype(BF16)
        o_ref[:, _cols(COL_VB, 2)] = dvb_ref[...].astype(BF16)

        @pl.when(i == 0)
        def _():
            dgq_ref[...] = acc_q
            dgk_ref[...] = acc_k

        @pl.when(i > 0)
        def _():
            dgq_ref[...] += acc_q
            dgk_ref[...] += acc_k

    def rows(width, shift=0):
        return pl.BlockSpec((tm, width), lambda i: (i + shift, 0))

    kv_w = N_KV_A * HEAD_DIM
    q_w = N_HEADS_A * HEAD_DIM
    vec = pl.BlockSpec((1, HEAD_DIM), lambda i: (0, 0))
    res, c_res = _call(
        body, name="qk_bwd", grid=(s // tm,),
        in_specs=[rows(q_w), rows(kv_w), rows(kv_w), rows(q_w), rows(kv_w, lo), rows(kv_w, lo), rows(n),
                  vec, vec, rows(HEAD_DIM), rows(HEAD_DIM)],
        out_specs=[rows(n), vec, vec],
        out_shape=[
            jax.ShapeDtypeStruct((s, n), BF16),
            jax.ShapeDtypeStruct((1, HEAD_DIM), F32),
            jax.ShapeDtypeStruct((1, HEAD_DIM), F32),
        ],
        sem=("arbitrary",), args=(dqa, dka, dva, dqb, dkpad, dvpad, proj, g_q, g_k, cos_t, sin_t), comm=comm,
        after=after)
    return res if comm is None else (res, c_res)


_NT = (((1,), (1,)), ((), ()))
_TN = (((0,), (0,)), ((), ()))


def _attn_a_fwd(pb, *, tq=4096, sub=256, comm=None, after=None):
    s = pb.shape[0]
    tq = min(tq, s)
    sub = min(sub, tq)

    def body(q_ref, k_ref, v_ref, o_ref, lse_ref):
        k = k_ref[...]
        v = v_ref[...]
        for r in range(tq // sub):
            rows = pl.ds(r * sub, sub)
            sc = lax.dot_general(q_ref[rows, :], k, _NT, preferred_element_type=F32)
            m = jnp.max(sc, axis=-1, keepdims=True)
            p = jnp.exp2(sc - m)
            l = jnp.sum(p, axis=-1, keepdims=True)
            o = jnp.dot(p.astype(BF16), v, preferred_element_type=F32)
            o_ref[rows, :] = (o / l).astype(BF16)
            lse_ref[rows, :] = jnp.broadcast_to(m + jnp.log2(l), (sub, HEAD_DIM))

    res, c_res = _call(
        body, name="attn_a_fwd", grid=(N_HEADS_A, s // tq),
        in_specs=[
            pl.BlockSpec((tq, HEAD_DIM), lambda h, i: (i, COL_QA + h)),
            pl.BlockSpec((s, HEAD_DIM), lambda h, i: (0, COL_KA + h // GROUP)),
            pl.BlockSpec((s, HEAD_DIM), lambda h, i: (0, COL_VA + h // GROUP)),
        ],
        out_specs=[
            pl.BlockSpec((tq, HEAD_DIM), lambda h, i: (i, h)),
            pl.BlockSpec((None, tq, HEAD_DIM), lambda h, i: (h, i, 0)),
        ],
        out_shape=[
            jax.ShapeDtypeStruct((s, (N_HEADS_A + N_HEADS_B) * HEAD_DIM), BF16),
            jax.ShapeDtypeStruct((N_HEADS_A, s, HEAD_DIM), F32),
        ],
        sem=("parallel", "parallel"), args=(pb, pb, pb), comm=comm, after=after)
    return res if comm is None else (res, c_res)


def _attn_a_bwd(pb, att, datt, lse, *, tq=2048, sub=256, comm=None):
    s = pb.shape[0]
    tq = min(tq, s)
    sub = min(sub, tq)

    def body(q_ref, k_ref, v_ref, o_ref, do_ref, lse_ref, dq_ref, dk_ref, dv_ref):
        first = jnp.logical_and(pl.program_id(1) == 0, pl.program_id(2) == 0)
        k = k_ref[...]
        v = v_ref[...]
        dk = dv = None
        for r in range(tq // sub):
            rows = pl.ds(r * sub, sub)
            q = q_ref[rows, :]
            do = do_ref[rows, :]
            sc = lax.dot_general(q, k, _NT, preferred_element_type=F32)
            p = jnp.exp2(sc - lse_ref[rows, :][:, :1])
            dp = lax.dot_general(do, v, _NT, preferred_element_type=F32)
            delta = jnp.sum(do.astype(F32) * o_ref[rows, :].astype(F32), axis=-1, keepdims=True)
            ds = (p * (dp - delta)).astype(BF16)
            dq_ref[rows, :] = jnp.dot(ds, k, preferred_element_type=F32)
            dk_r = lax.dot_general(ds, q, _TN, preferred_element_type=F32)
            dv_r = lax.dot_general(p.astype(BF16), do, _TN, preferred_element_type=F32)
            dk = dk_r if dk is None else dk + dk_r
            dv = dv_r if dv is None else dv + dv_r

        @pl.when(first)
        def _():
            dk_ref[...] = dk
            dv_ref[...] = dv

        @pl.when(jnp.logical_not(first))
        def _():
            dk_ref[...] += dk
            dv_ref[...] += dv

    qmap = lambda kv, g, i: (i, kv * GROUP + g)
    res, c_res = _call(
        body, name="attn_a_bwd", grid=(N_KV_A, GROUP, s // tq),
        in_specs=[
            pl.BlockSpec((tq, HEAD_DIM), lambda kv, g, i: (i, COL_QA + kv * GROUP + g)),
            pl.BlockSpec((s, HEAD_DIM), lambda kv, g, i: (0, COL_KA + kv)),
            pl.BlockSpec((s, HEAD_DIM), lambda kv, g, i: (0, COL_VA + kv)),
            pl.BlockSpec((tq, HEAD_DIM), qmap),
            pl.BlockSpec((tq, HEAD_DIM), qmap),
            pl.BlockSpec((None, tq, HEAD_DIM), lambda kv, g, i: (kv * GROUP + g, i, 0)),
        ],
        out_specs=[
            pl.BlockSpec((tq, HEAD_DIM), qmap),
            pl.BlockSpec((s, HEAD_DIM), lambda kv, g, i: (0, kv)),
            pl.BlockSpec((s, HEAD_DIM), lambda kv, g, i: (0, kv)),
        ],
        out_shape=[
            jax.ShapeDtypeStruct((s, N_HEADS_A * HEAD_DIM), F32),
            jax.ShapeDtypeStruct((s, N_KV_A * HEAD_DIM), F32),
            jax.ShapeDtypeStruct((s, N_KV_A * HEAD_DIM), F32),
        ],
        sem=("arbitrary", "arbitrary", "arbitrary"), args=(pb, pb, pb, att, datt, lse), comm=comm)
    return res if comm is None else (res, c_res)


def _t5_bucket(rel):
    nb = N_BUCKETS // 2
    ret = jnp.where(rel > 0, nb, 0)
    n = jnp.abs(rel)
    max_exact = nb // 2
    nf = jnp.maximum(n, 1).astype(F32)
    large = max_exact + (jnp.log(nf / max_exact) / math.log(MAX_DISTANCE / max_exact)
                         * (nb - max_exact)).astype(jnp.int32)
    large = jnp.minimum(large, nb - 1)
    return ret + jnp.where(n < max_exact, n, large)


def _band_buckets():
    r = jnp.arange(BLOCK_Q, dtype=jnp.int32)
    j = jnp.arange(3 * BLOCK_Q, dtype=jnp.int32)
    return _t5_bucket((j[None, :] - BLOCK_Q) - r[:, None])


def _band_bias(bucket, table_ref, h):
    acc = jnp.zeros(bucket.shape, F32)
    for b in range(N_BUCKETS):
        acc = jnp.where(bucket == b, table_ref[b, h], acc)
    return acc


GQ = GROUP * BLOCK_Q


def _stack_heads(x):
    return jnp.concatenate([x[:, _cols(g)] for g in range(GROUP)], axis=0)


def _unstack_heads(x):
    return jnp.concatenate([x[g * BLOCK_Q:(g + 1) * BLOCK_Q] for g in range(GROUP)], axis=1)


def _group_bias(bucket, table_ref, kv):
    r = lax.broadcasted_iota(jnp.int32, (BLOCK_Q, 3 * BLOCK_Q), 0)
    j = lax.broadcasted_iota(jnp.int32, (BLOCK_Q, 3 * BLOCK_Q), 1)
    inside = jnp.abs(j - BLOCK_Q - r) <= WINDOW
    return jnp.concatenate([jnp.where(inside, _band_bias(bucket, table_ref, kv * GROUP + g) * LOG2E, NEG_INF)
                            for g in range(GROUP)], axis=0)


def _group_sink(sink_ref, kv):
    head = lax.broadcasted_iota(jnp.int32, (GQ, 1), 0) // BLOCK_Q
    snk = jnp.zeros((GQ, 1), F32)
    for g in range(GROUP):
        snk = jnp.where(head == g, sink_ref[0, kv * GROUP + g] * LOG2E, snk)
    return snk


def _band_mask(n, s):
    kabs = n * BLOCK_Q + lax.broadcasted_iota(jnp.int32, (1, 3 * BLOCK_Q), 1) - BLOCK_Q
    return (kabs >= 0) & (kabs < s)


def _band_start(n):
    return pl.multiple_of(n * BLOCK_Q + (PAD_LO - BLOCK_Q), BLOCK_Q)


def _attn_b_fwd(pb, kpad, vpad, bucket, table, sink, att, *, comm=None, after=None):
    s = pb.shape[0]
    nblk = s // BLOCK_Q
    sp = kpad.shape[0]

    def body(table_ref, sink_ref, q0_ref, q1_ref, k_ref, v_ref, bucket_ref, _, o_ref, lse_ref, bias_ref):
        n = pl.program_id(0)

        @pl.when(n == 0)
        def _():
            for kv in range(N_KV_B):
                bias_ref[kv * GQ:(kv + 1) * GQ, :] = _group_bias(bucket_ref[...], table_ref, kv)

        band = pl.ds(_band_start(n), 3 * BLOCK_Q)
        mask = _band_mask(n, s)
        for kv, q_ref in enumerate((q0_ref, q1_ref)):
            kb = k_ref[band, _cols(kv)]
            vb = v_ref[band, _cols(kv)]
            sc = lax.dot_general(_stack_heads(q_ref[...]), kb, _NT, preferred_element_type=F32)
            sc = jnp.where(mask, sc + bias_ref[kv * GQ:(kv + 1) * GQ, :], NEG_INF)
            snk = _group_sink(sink_ref, kv)
            m = jnp.maximum(jnp.max(sc, axis=-1, keepdims=True), snk)
            p = jnp.exp2(sc - m)
            l = jnp.sum(p, axis=-1, keepdims=True) + jnp.exp2(snk - m)
            o = jnp.dot(p.astype(BF16), vb, preferred_element_type=F32)
            o_ref[:, _cols(kv * GROUP, GROUP)] = _unstack_heads((o / l).astype(BF16))
            lse = m + jnp.log2(l)
            for g in range(GROUP):
                lse_ref[kv * GROUP + g] = jnp.broadcast_to(lse[g * BLOCK_Q:(g + 1) * BLOCK_Q], (BLOCK_Q, HEAD_DIM))

    smem = pl.BlockSpec(memory_space=pltpu.SMEM)
    wide = GROUP * HEAD_DIM
    whole = pl.BlockSpec((sp, N_KV_B * HEAD_DIM), lambda n: (0, 0))
    res, c_res = _call(
        body, name="attn_b_fwd", grid=(nblk,),
        in_specs=[
            smem,
            smem,
            pl.BlockSpec((BLOCK_Q, wide), lambda n: (n, COL_QB // GROUP)),
            pl.BlockSpec((BLOCK_Q, wide), lambda n: (n, COL_QB // GROUP + 1)),
            whole,
            whole,
            pl.BlockSpec((BLOCK_Q, 3 * BLOCK_Q), lambda n: (0, 0)),
            _ANY,
        ],
        out_specs=[
            pl.BlockSpec((BLOCK_Q, N_HEADS_B * HEAD_DIM), lambda n: (n, 1)),
            pl.BlockSpec((N_HEADS_B, BLOCK_Q, HEAD_DIM), lambda n: (0, n, 0)),
        ],
        out_shape=[
            jax.ShapeDtypeStruct(att.shape, BF16),
            jax.ShapeDtypeStruct((N_HEADS_B, s, HEAD_DIM), F32),
        ],
        scratch_shapes=[pltpu.VMEM((N_KV_B * GQ, 3 * BLOCK_Q), F32)],
        sem=("arbitrary",), args=(table, sink, pb, pb, kpad, vpad, bucket, att), comm=comm, after=after,
        aliases={7: 0})
    return res if comm is None else (res, c_res)


def _attn_b_bwd(pb, kpad, vpad, att, datt, lse, bucket, table, sink, *, comm=None, after=None):
    s = pb.shape[0]
    nblk = s // BLOCK_Q
    sp = kpad.shape[0]

    def body(table_ref, sink_ref, q0_ref, q1_ref, k_ref, v_ref, o_ref, do_ref, lse_ref, bucket_ref,
             dq_ref, dk_ref, dv_ref, dtab_ref, dsink_ref, bias_ref, dbias_ref):
        n = pl.program_id(0)

        @pl.when(n == 0)
        def _():
            dk_ref[...] = jnp.zeros_like(dk_ref)
            dv_ref[...] = jnp.zeros_like(dv_ref)
            dbias_ref[...] = jnp.zeros_like(dbias_ref)
            dsink_ref[...] = jnp.zeros_like(dsink_ref)
            for kv in range(N_KV_B):
                bias_ref[kv * GQ:(kv + 1) * GQ, :] = _group_bias(bucket_ref[...], table_ref, kv)

        band = pl.ds(_band_start(n), 3 * BLOCK_Q)
        mask = _band_mask(n, s)
        for kv, q_ref in enumerate((q0_ref, q1_ref)):
            wide_cols = _cols(kv * GROUP, GROUP)
            q = _stack_heads(q_ref[...])
            do = _stack_heads(do_ref[:, wide_cols])
            o = _stack_heads(o_ref[:, wide_cols])
            kb = k_ref[band, _cols(kv)]
            vb = v_ref[band, _cols(kv)]
            lse = jnp.concatenate([lse_ref[kv * GROUP + g][:, :1] for g in range(GROUP)], axis=0)
            sc = lax.dot_general(q, kb, _NT, preferred_element_type=F32)
            sc = jnp.where(mask, sc + bias_ref[kv * GQ:(kv + 1) * GQ, :], NEG_INF)
            p = jnp.exp2(sc - lse)
            dp = lax.dot_general(do, vb, _NT, preferred_element_type=F32)
            delta = jnp.sum(do.astype(F32) * o.astype(F32), axis=-1, keepdims=True)
            ds = p * (dp - delta)
            dsb = ds.astype(BF16)
            dq_ref[:, wide_cols] = _unstack_heads(jnp.dot(dsb, kb, preferred_element_type=F32))
            dk_ref[band, _cols(kv)] += lax.dot_general(dsb, q, _TN, preferred_element_type=F32)
            dv_ref[band, _cols(kv)] += lax.dot_general(p.astype(BF16), do, _TN, preferred_element_type=F32)
            dbias_ref[kv * GQ:(kv + 1) * GQ, :] += ds
            sink_part = -jnp.exp2(_group_sink(sink_ref, kv) - lse) * delta
            for g in range(GROUP):
                rows = slice(g * BLOCK_Q, (g + 1) * BLOCK_Q)
                dsink_ref[kv * GROUP + g] += jnp.broadcast_to(
                    jnp.sum(sink_part[rows], axis=0, keepdims=True), (1, HEAD_DIM))

        @pl.when(n == nblk - 1)
        def _():
            bucket_v = bucket_ref[...]
            row = lax.broadcasted_iota(jnp.int32, (N_BUCKETS, HEAD_DIM), 0)
            for h in range(N_HEADS_B):
                acc = dbias_ref[h * BLOCK_Q:(h + 1) * BLOCK_Q, :]
                tot = jnp.zeros((N_BUCKETS, HEAD_DIM), F32)
                for b in range(N_BUCKETS):
                    tot = jnp.where(row == b, jnp.sum(jnp.where(bucket_v == b, acc, 0.0), keepdims=True), tot)
                dtab_ref[h] = tot

    smem = pl.BlockSpec(memory_space=pltpu.SMEM)
    wide = GROUP * HEAD_DIM
    whole = pl.BlockSpec((sp, N_KV_B * HEAD_DIM), lambda n: (0, 0))
    group_b = pl.BlockSpec((BLOCK_Q, N_HEADS_B * HEAD_DIM), lambda n: (n, 1))
    res, c_res = _call(
        body, name="attn_b_bwd", grid=(nblk,),
        in_specs=[
            smem,
            smem,
            pl.BlockSpec((BLOCK_Q, wide), lambda n: (n, COL_QB // GROUP)),
            pl.BlockSpec((BLOCK_Q, wide), lambda n: (n, COL_QB // GROUP + 1)),
            whole,
            whole,
            group_b,
            group_b,
            pl.BlockSpec((N_HEADS_B, BLOCK_Q, HEAD_DIM), lambda n: (0, n, 0)),
            pl.BlockSpec((BLOCK_Q, 3 * BLOCK_Q), lambda n: (0, 0)),
        ],
        out_specs=[
            pl.BlockSpec((BLOCK_Q, N_HEADS_B * HEAD_DIM), lambda n: (n, 0)),
            whole,
            whole,
            pl.BlockSpec((N_HEADS_B, N_BUCKETS, HEAD_DIM), lambda n: (0, 0, 0)),
            pl.BlockSpec((N_HEADS_B, 1, HEAD_DIM), lambda n: (0, 0, 0)),
        ],
        out_shape=[
            jax.ShapeDtypeStruct((s, N_HEADS_B * HEAD_DIM), F32),
            jax.ShapeDtypeStruct((sp, N_KV_B * HEAD_DIM), F32),
            jax.ShapeDtypeStruct((sp, N_KV_B * HEAD_DIM), F32),
            jax.ShapeDtypeStruct((N_HEADS_B, N_BUCKETS, HEAD_DIM), F32),
            jax.ShapeDtypeStruct((N_HEADS_B, 1, HEAD_DIM), F32),
        ],
        scratch_shapes=[pltpu.VMEM((N_KV_B * GQ, 3 * BLOCK_Q), F32), pltpu.VMEM((N_KV_B * GQ, 3 * BLOCK_Q), F32)],
        sem=("arbitrary",),
        args=(table, sink, pb, pb, kpad, vpad, att, datt, lse, bucket), comm=comm, after=after)
    return res if comm is None else (res, c_res)


def _other_chips(x, y):
    return [(x, 1 - y), (1 - x, y), (1 - x, 1 - y)]


_HBM = pl.BlockSpec(memory_space=pltpu.HBM)
_SEM = pl.BlockSpec(memory_space=pltpu.SEMAPHORE)
_SPLIT = pltpu.CompilerParams(has_side_effects=pltpu.SideEffectType.DATAFLOW_SIDE_EFFECTING)


def _in_hbm(a):
    return pltpu.with_memory_space_constraint(a, pltpu.HBM)


def _my_half(rows):
    c = lax.axis_index("c")
    half = rows // 2
    return pl.ds(pl.multiple_of(c * half, half), half), pl.ds(pl.multiple_of((1 - c) * half, half), half)


def _gather_route(shapes):
    def route(src, land):
        x, y, c = lax.axis_index("x"), lax.axis_index("y"), lax.axis_index("c")
        out = []
        for t, shape in enumerate(shapes):
            mine, _ = _my_half(shape[0])
            for px, py in _other_chips(x, y):
                out.append((src[t].at[mine], land[t].at[2 * x + y, mine], land[t].at[2 * px + py, mine], (px, py, c)))
        return out

    return route


def _exchange_route(n_t):
    def route(src, land):
        x, y, c = lax.axis_index("x"), lax.axis_index("y"), lax.axis_index("c")
        out = []
        for t in range(n_t):
            for px, py in _other_chips(x, y):
                k = 2 * px + py
                out.append((src[t].at[k], land[t].at[2 * (2 * x + y) + c], land[t].at[2 * k + c], (px, py, c)))
        return out

    return route


def _own_slot(shape, dtype, slot, block):
    return lax.dynamic_update_slice(lax.empty(shape, dtype), block[None], (slot,) + (0,) * (len(shape) - 1))


def _split_start(name, srcs, lands, route, after):
    n = len(srcs)

    def body(*refs):
        src, land, send_sems, recv_sems, token = refs[:n], refs[n:2 * n], refs[2 * n + 1], refs[2 * n + 2], refs[-1]
        for i, (src_ref, dst_ref, _, to) in enumerate(route(src, land)):
            pltpu.make_async_remote_copy(src_ref=src_ref, dst_ref=dst_ref, send_sem=send_sems.at[i],
                                         recv_sem=recv_sems.at[i], device_id=to, device_id_type=_MESH).start()
        token[...] = jnp.zeros_like(token)

    sem = pltpu.SemaphoreType.DMA((3 * n,))
    lands = list(lands)
    res = pl.pallas_call(
        body, name=name,
        in_specs=[_HBM] * (2 * n) + [_ANY],
        out_specs=[_SEM, _SEM] + [_HBM] * (2 * n) + [pl.BlockSpec(memory_space=pltpu.VMEM)],
        out_shape=[sem, sem] + [pltpu.HBM(a.shape, a.dtype) for a in list(srcs) + lands]
        + [jax.ShapeDtypeStruct((8, 128), F32)],
        input_output_aliases={i: 2 + i for i in range(2 * n)},
        compiler_params=_SPLIT,
    )(*[_in_hbm(a) for a in srcs], *[_in_hbm(a) for a in lands], after)
    return (res[0], res[1]), res[2:2 + n], res[2 + n:2 + 2 * n], res[-1]


def _split_wait(name, srcs, lands, sems, route, after):
    n = len(srcs)

    def body(*refs):
        src, land, send_sems, recv_sems = refs[:n], refs[n:2 * n], refs[2 * n], refs[2 * n + 1]
        for i, (src_ref, _, dst_ref, to) in enumerate(route(src, land)):
            cp = pltpu.make_async_remote_copy(src_ref=src_ref, dst_ref=dst_ref, send_sem=send_sems.at[i],
                                              recv_sem=recv_sems.at[i], device_id=to, device_id_type=_MESH)
            cp.wait_send()
            cp.wait_recv()

    res = pl.pallas_call(
        body, name=name,
        in_specs=[_HBM] * (2 * n) + [_SEM, _SEM, _ANY],
        out_specs=[_HBM] * (2 * n),
        out_shape=[pltpu.HBM(a.shape, a.dtype) for a in list(srcs) + list(lands)],
        input_output_aliases={i: i for i in range(2 * n)},
        compiler_params=_SPLIT,
    )(*srcs, *lands, sems[0], sems[1], after)
    return res[:n], res[n:]


def _comm_only(name, comm):
    return _call(lambda: None, name=name, grid=(1,), in_specs=[], out_specs=[], out_shape=[], args=(), comm=comm)[1]


def _swap_comm(shards, lands):
    n_t = len(lands)

    def copies(land, sems, later):
        send_sems, recv_sems = sems
        x, y = lax.axis_index("x"), lax.axis_index("y")
        sends, recvs = [], []
        for t in range(n_t):
            mine, other = _my_half(shards[t].shape[0])
            for j, (px, py) in enumerate(_other_chips(x, y)):
                k = 2 * px + py
                for part, out in ((mine, sends), (other, recvs)) if later else ((mine, sends),):
                    out.append(pltpu.make_async_remote_copy(
                        src_ref=land[t].at[k, part], dst_ref=land[t].at[k, part], send_sem=send_sems.at[3 * t + j],
                        recv_sem=recv_sems.at[3 * t + j], device_id=_sibling(), device_id_type=_MESH))
        return sends, recvs

    def start(ins, land, sems):
        for cp in copies(land, sems, False)[0]:
            cp.start()

    def finish(ins, land, sems):
        sends, recvs = copies(land, sems, True)
        for cp in recvs:
            cp.wait_recv()
        for cp in sends:
            cp.wait_send()

    return _Comm(
        lands, [jax.ShapeDtypeStruct(a.shape, a.dtype) for a in lands],
        [pltpu.SemaphoreType.DMA((3 * n_t,)), pltpu.SemaphoreType.DMA((3 * n_t,))],
        start, finish, aliases={t: t for t in range(n_t)})


def _forward_comm(partials, lands):
    n_t = len(lands)

    def copies(land, sems, later):
        send_sems, recv_sems = sems
        x, y, c = lax.axis_index("x"), lax.axis_index("y"), lax.axis_index("c")
        sends, recvs = [], []
        for t in range(n_t):
            for j, k in enumerate([2 * x + y] + [2 * px + py for px, py in _other_chips(x, y)]):
                for slot, out in ((2 * k + c, sends), (2 * k + 1 - c, recvs)) if later else ((2 * k + c, sends),):
                    out.append(pltpu.make_async_remote_copy(
                        src_ref=land[t].at[slot], dst_ref=land[t].at[slot], send_sem=send_sems.at[4 * t + j],
                        recv_sem=recv_sems.at[4 * t + j], device_id=_sibling(), device_id_type=_MESH))
        return sends, recvs

    def start(ins, land, sems):
        for cp in copies(land, sems, False)[0]:
            cp.start()

    def finish(ins, land, sems):
        sends, recvs = copies(land, sems, True)
        for cp in recvs:
            cp.wait_recv()
        for cp in sends:
            cp.wait_send()

    return _Comm(
        lands, [jax.ShapeDtypeStruct(a.shape, a.dtype) for a in lands],
        [pltpu.SemaphoreType.DMA((4 * n_t,)), pltpu.SemaphoreType.DMA((4 * n_t,))],
        start, finish, aliases={t: t for t in range(n_t)})


def _allreduce_small(pack):
    rows, d = pack.shape

    def body(p_ref, sum_ref, all_ref, send_sems, recv_sems):
        x, y, c = lax.axis_index("x"), lax.axis_index("y"), lax.axis_index("c")
        me = 4 * x + 2 * y + c
        all_ref[me] = p_ref[...]
        peers = []
        for dx in range(2):
            for dy in range(2):
                for dc in range(2):
                    if dx or dy or dc:
                        px = 1 - x if dx else x
                        py = 1 - y if dy else y
                        pc = 1 - c if dc else c
                        peers.append((4 * dx + 2 * dy + dc - 1, (px, py, pc)))
        sends = []
        for k, to in peers:
            cp = pltpu.make_async_remote_copy(
                src_ref=p_ref, dst_ref=all_ref.at[me], send_sem=send_sems.at[k], recv_sem=recv_sems.at[k],
                device_id=to, device_id_type=_MESH)
            cp.start()
            sends.append(cp)
        for k, (px, py, pc) in peers:
            pltpu.make_async_remote_copy(
                src_ref=p_ref, dst_ref=all_ref.at[4 * px + 2 * py + pc], send_sem=send_sems.at[k],
                recv_sem=recv_sems.at[k], device_id=(px, py, pc), device_id_type=_MESH).wait_recv()
        for cp in sends:
            cp.wait_send()
        tot = all_ref[0]
        for i in range(1, N_DEV):
            tot = tot + all_ref[i]
        sum_ref[...] = tot

    vm = pl.BlockSpec(memory_space=pltpu.VMEM)
    return pl.pallas_call(
        body,
        name="allreduce_small",
        in_specs=[vm],
        out_specs=vm,
        out_shape=jax.ShapeDtypeStruct((rows, d), F32),
        scratch_shapes=[
            pltpu.VMEM((N_DEV, rows, d), F32),
            pltpu.SemaphoreType.DMA((N_DEV - 1,)),
            pltpu.SemaphoreType.DMA((N_DEV - 1,)),
        ],
    )(pack)


def _adamw_math(w, g, m, v):
    m = ADAM_B1 * m + (1.0 - ADAM_B1) * g
    v = ADAM_B2 * v + (1.0 - ADAM_B2) * (g * g)
    m_hat = m / (1.0 - ADAM_B1 ** ADAM_STEP)
    v_hat = v / (1.0 - ADAM_B2 ** ADAM_STEP)
    delta = -ADAM_LR * (m_hat / (jnp.sqrt(v_hat) + ADAM_EPS) + ADAM_WD * w)
    return delta, m, v


def _sum_adamw(parts, w, m, v, *, name, tr=512):
    r, c = w.shape
    tr = min(tr, r)
    tc = min(c, 1024)

    def body(p_ref, w_ref, m_ref, v_ref, g_ref, d_ref, m2_ref, v2_ref):
        g = p_ref[0].astype(F32)
        for i in range(1, N_DEV):
            g = g + p_ref[i].astype(F32)
        delta, m2, v2 = _adamw_math(w_ref[...], g, m_ref[...], v_ref[...])
        g_ref[...] = g
        d_ref[...] = delta
        m2_ref[...] = m2
        v2_ref[...] = v2

    blk = pl.BlockSpec((tr, tc), lambda i, j: (i, j))
    return pl.pallas_call(
        body,
        name=name,
        grid=(r // tr, c // tc),
        in_specs=[pl.BlockSpec((N_DEV, tr, tc), lambda i, j: (0, i, j)), blk, blk, blk],
        out_specs=[blk] * 4,
        out_shape=[jax.ShapeDtypeStruct((r, c), F32)] * 4,
        compiler_params=_params(("parallel", "parallel")),
    )(parts, w, m, v)


def _adamw_small(g, w, m, v):
    def body(g_ref, w_ref, m_ref, v_ref, d_ref, m2_ref, v2_ref):
        delta, m2, v2 = _adamw_math(w_ref[...], g_ref[...], m_ref[...], v_ref[...])
        d_ref[...] = delta
        m2_ref[...] = m2
        v2_ref[...] = v2

    vm = pl.BlockSpec(memory_space=pltpu.VMEM)
    return pl.pallas_call(
        body,
        name="adamw_small",
        in_specs=[vm] * 4,
        out_specs=[vm] * 3,
        out_shape=[jax.ShapeDtypeStruct(g.shape, F32)] * 3,
    )(g, w, m, v)


def _relu2_epilogue(acc):
    ra = jnp.maximum(acc, 0.0)
    return ra * ra, ra


def _residual_norm_epilogue(acc, res, g):
    h = acc + res
    return h, h * lax.rsqrt(jnp.mean(h * h, axis=-1, keepdims=True) + EPS) * g


def _rows(stacked):
    return stacked.reshape(stacked.shape[0] * stacked.shape[1], stacked.shape[2])


def _by_chip(mat):
    return mat.reshape(N_CHIPS, mat.shape[0] // N_CHIPS, mat.shape[1])


def _local_step(x, p, target, shards, small, update):
    s, d = x.shape
    cos_t, sin_t = _rope_tables(s)
    bucket = _band_buckets()
    p_bf = p.astype(BF16)
    wts = {}

    chip = 2 * lax.axis_index("x") + lax.axis_index("y")
    core = lax.axis_index("c")

    def gather(tag, names, after):
        srcs = [cast[n] for n in names]
        route = _gather_route([a.shape for a in srcs])
        sems, srcs, lands, token = _split_start(f"gather_start_{tag}", srcs, [zones[n] for n in names], route, after)

        def landed(done):
            got_srcs, got_lands = _split_wait(f"gather_wait_{tag}", srcs, lands, sems, route, done)
            comm = _swap_comm(got_srcs, got_lands)
            comm.waited = got_srcs[0]
            return comm

        return landed, token

    def prepare(n, zero):
        cast[n] = (shards[n] + zero).astype(BF16)
        zones[n] = _own_slot((N_CHIPS,) + cast[n].shape, BF16, chip, cast[n])

    cast, zones = {}, {}
    prepare("w_in", 0.0)
    in_landed, token = gather("in", ["w_in"], small["attn_norm_g"])
    for n in shards:
        if n != "w_in":
            prepare(n, token[:1, :1])
    g_attn = small["attn_norm_g"] + token[:1, :1]
    u = _rms_fwd(x, g_attn, name="norm_attn")
    prepared = u[:1, :1].astype(F32) + sum(
        (lax.dynamic_slice(zones[n], (chip, 0, 0), (1, 1, 1))[0] + cast[n][:1, :1]).astype(F32)
        for n in zones if n != "w_in")
    (wts["w_in"],) = _comm_only("swap_w_in", in_landed(prepared))
    mid_landed, token = gather("mid", ["w_out"], wts["w_in"])
    proj = _matmul(u, wts["w_in"], mode="nn", out_dtypes=[F32], name="mm_in", bn=768, bm=2048, after=token)
    pb, (w_out_s,) = _qk_prep(proj, small["q_norm_g"], small["k_norm_g"], cos_t, sin_t, comm=mid_landed(proj))
    wts["w_out"] = _rows(w_out_s)
    up_landed, token = gather("up", ["w_up"], pb)
    att_a, lse_a = _attn_a_fwd(pb, after=token)
    pad = ((PAD_LO, PAD_HI), (0, 0))
    kpad = jnp.pad(pb[:, COL_KB * HEAD_DIM:COL_VB * HEAD_DIM], pad)
    vpad = jnp.pad(pb[:, COL_VB * HEAD_DIM:], pad)
    up_swap = up_landed(att_a)
    down_landed, token = gather("down", ["w_down"], up_swap.waited)
    (att, lse_b), (wts["w_up"],) = _attn_b_fwd(pb, kpad, vpad, bucket, small["rel_bias_table"],
                                               small["sink_logits"], att_a, comm=up_swap, after=token)
    h1, mn = _matmul(att, wts["w_out"], mode="nn", out_dtypes=[F32, BF16], name="mm_out", bm=512, bn=d,
                     epilogue=_residual_norm_epilogue, extras=(x,), vecs=(small["mlp_norm_g"],))
    r, ra = _matmul(mn, wts["w_up"], mode="nn", out_dtypes=[BF16, BF16], name="mm_up", epilogue=_relu2_epilogue,
                    bm=2048)
    (w_down_s,) = _comm_only("swap_w_down", down_landed(r))
    wts["w_down"] = _rows(w_down_s)
    late_landed, token = gather("late", ["w_gate", "ple_w"], w_down_s)
    h2 = _matmul(r, wts["w_down"], mode="nn", out_dtypes=[F32], name="mm_down",
                 epilogue=lambda acc, res: (acc + res,), extras=(h1,), after=token)
    ng, (w_gate_s, wts["ple_w"]) = _rms_fwd(h2, small["gate_norm_g"], name="norm_gate", comm=late_landed(h2))
    wts["w_gate"] = _rows(w_gate_s)
    gate = _matmul(ng, wts["w_gate"], mode="nn", out_dtypes=[F32], name="mm_gate",
                   epilogue=lambda acc: (1.0 / (1.0 + jnp.exp(-acc)),))
    pp = _matmul(p_bf, wts["ple_w"], mode="nn", out_dtypes=[F32], name="mm_ple", bn=512)
    dh3, dz, dpp, dg_final, dg_ple, loss = _tail(h2, gate, pp, target, small["ple_norm_g"], small["final_norm_g"])

    dng = _matmul(dz, wts["w_gate"], mode="nt", out_dtypes=[F32], name="mm_gate_dx")
    gw_gate = _matmul(ng, dz, mode="tn", out_dtypes=[BF16], name="mm_gate_dw", bm=512, bk=4096)
    gw_ple = _matmul(p_bf, dpp, mode="tn", out_dtypes=[BF16], name="mm_ple_dw", bn=512, out_stack=N_CHIPS)
    dh2, dh2_bf, dg_gate = _rms_bwd(h2, dng, small["gate_norm_g"], dh3, name="norm_gate_bwd", want_bf16=True)

    def exchange(tag, partials, after):
        route = _exchange_route(len(partials))
        lands = [_own_slot((N_DEV,) + g.shape[1:], g.dtype, 2 * chip + core,
                           lax.dynamic_index_in_dim(g, chip, 0, keepdims=False)) for g in partials]
        sems, srcs, lands, token = _split_start(f"exchange_start_{tag}", partials, lands, route, after)

        def landed(done):
            got_srcs, got_lands = _split_wait(f"exchange_wait_{tag}", srcs, lands, sems, route, done)
            comm = _forward_comm(got_srcs, got_lands)
            comm.waited = got_srcs[0]
            return comm

        return landed, token

    big = {}
    gate_landed, token = exchange("gate", [_by_chip(gw_gate), gw_ple], dh2_bf)
    gw_down = _matmul(r, dh2_bf, mode="tn", out_dtypes=[BF16], name="mm_down_dw", after=token, bm=512, bk=4096)
    da, (parts_gate, parts_ple) = _matmul(
        dh2_bf, wts["w_down"], mode="nt", out_dtypes=[BF16], name="mm_down_dx", bm=2048,
        epilogue=lambda acc, ra_v: (acc * (2.0 * ra_v.astype(F32)),), extras=(ra,), comm=gate_landed(gw_down))
    down_landed, token = exchange("down", [_by_chip(gw_down)], da)
    big["w_gate"], big["ple_w"] = update("w_gate", parts_gate), update("ple_w", parts_ple)
    gw_up = _matmul(mn, da, mode="tn", out_dtypes=[BF16], name="mm_up_dw", out_stack=N_CHIPS, after=token,
                    bm=512, bk=4096)
    dmn = _matmul(da, wts["w_up"], mode="nt", out_dtypes=[F32], name="mm_up_dx", bk=4096,
                  after=gw_up)
    dh1, dh1_bf, dg_mlp = _rms_bwd(h1, dmn, small["mlp_norm_g"], dh2, name="norm_mlp_bwd", want_bf16=True)
    down_forward = down_landed(dh1_bf)
    up_landed, token = exchange("up", [gw_up], down_forward.waited)
    datt = _matmul(dh1_bf, wts["w_out"], mode="nt", out_dtypes=[BF16], name="mm_out_dx", after=token)
    gw_out = _matmul(att, dh1_bf, mode="tn", out_dtypes=[BF16], name="mm_out_dw", bm=512, bk=4096)
    dqb, dkpad, dvpad, dtab, dsink = _attn_b_bwd(pb, kpad, vpad, att, datt, lse_b, bucket,
                                                 small["rel_bias_table"], small["sink_logits"])
    (dqa, dka, dva), (parts_down,) = _attn_a_bwd(pb, att, datt, lse_a, comm=down_forward)
    up_forward = up_landed(dqa)
    out_landed, token = exchange("out", [_by_chip(gw_out)], up_forward.waited)
    (dproj, dg_q, dg_k), (parts_up,) = _qk_bwd(dqa, dka, dva, dqb, dkpad, dvpad, proj,
                                               small["q_norm_g"], small["k_norm_g"], cos_t, sin_t,
                                               comm=up_forward, after=token)
    gw_in = _matmul(u, dproj, mode="tn", out_dtypes=[BF16], name="mm_in_dw", bn=768, out_stack=N_CHIPS,
                    bm=512, bk=4096)
    out_forward = out_landed(gw_in)
    in_landed, token = exchange("in", [gw_in], out_forward.waited)
    du, (parts_out,) = _matmul(dproj, wts["w_in"], mode="nt", out_dtypes=[F32], name="mm_in_dx", bk=3072,
                               comm=out_forward, after=token)
    grad_x, dg_attn = _rms_bwd(x, du, small["attn_norm_g"], dh1, name="norm_attn_bwd", want_bf16=False)
    for n, parts in (("w_down", parts_down), ("w_up", parts_up), ("w_out", parts_out)):
        big[n] = update(n, parts)
    done = dg_attn + sum(big[n][0][0, :1, :] for n in ("w_down", "w_up", "w_out"))
    (parts_in,) = _comm_only("forward_w_in", in_landed(done))
    big["w_in"] = update("w_in", parts_in)

    small_g = {
        "attn_norm_g": dg_attn, "mlp_norm_g": dg_mlp, "ple_norm_g": dg_ple, "gate_norm_g": dg_gate,
        "final_norm_g": dg_final, "q_norm_g": dg_q, "k_norm_g": dg_k,
        "sink_logits": dsink[:, 0, 0][None, :], "rel_bias_table": dtab[:, :, 0].T,
    }
    return loss, grad_x, big, small_g


_SMALL_ROWS = ["attn_norm_g", "mlp_norm_g", "ple_norm_g", "gate_norm_g", "final_norm_g"]
_PACK_ROWS = 8


def _pack_small(vals, d):
    rows = [vals[n].reshape(1, d) for n in _SMALL_ROWS]
    misc = jnp.concatenate([
        vals["q_norm_g"].reshape(1, HEAD_DIM), vals["k_norm_g"].reshape(1, HEAD_DIM),
        jnp.pad(vals["sink_logits"].reshape(1, N_HEADS_B), ((0, 0), (0, HEAD_DIM - N_HEADS_B))),
        vals["rel_bias_table"].reshape(1, N_BUCKETS * N_HEADS_B)], axis=1)
    rows.append(jnp.pad(misc, ((0, 0), (0, d - misc.shape[1]))))
    rows.append(jnp.zeros((_PACK_ROWS - len(rows), d), F32))
    return jnp.concatenate(rows, axis=0).astype(F32)


def _unpack_small(pack, shapes):
    out = {n: pack[i].reshape(shapes[n]) for i, n in enumerate(_SMALL_ROWS)}
    misc = pack[len(_SMALL_ROWS)]
    out["q_norm_g"] = misc[:HEAD_DIM].reshape(shapes["q_norm_g"])
    out["k_norm_g"] = misc[HEAD_DIM:2 * HEAD_DIM].reshape(shapes["k_norm_g"])
    out["sink_logits"] = misc[2 * HEAD_DIM:2 * HEAD_DIM + N_HEADS_B].reshape(shapes["sink_logits"])
    out["rel_bias_table"] = misc[3 * HEAD_DIM:3 * HEAD_DIM + N_BUCKETS * N_HEADS_B].reshape(shapes["rel_bias_table"])
    return out


_WEIGHTS = ["attn_norm_g", "w_in", "q_norm_g", "k_norm_g", "sink_logits", "w_out", "mlp_norm_g", "w_up", "w_down",
            "ple_w", "ple_norm_g", "gate_norm_g", "w_gate", "rel_bias_table", "final_norm_g"]
_BIG = ["w_in", "w_out", "w_up", "w_down", "ple_w", "w_gate"]


def kernel(x, p, attn_norm_g, w_in, q_norm_g, k_norm_g, sink_logits, w_out, mlp_norm_g, w_up, w_down, ple_w, ple_norm_g, gate_norm_g, w_gate, rel_bias_table, final_norm_g, loss_target, m_attn_norm_g, m_w_in, m_q_norm_g, m_k_norm_g, m_sink_logits, m_w_out, m_mlp_norm_g, m_w_up, m_w_down, m_ple_w, m_ple_norm_g, m_gate_norm_g, m_w_gate, m_rel_bias_table, m_final_norm_g, v_attn_norm_g, v_w_in, v_q_norm_g, v_k_norm_g, v_sink_logits, v_w_out, v_mlp_norm_g, v_w_up, v_w_down, v_ple_w, v_ple_norm_g, v_gate_norm_g, v_w_gate, v_rel_bias_table, v_final_norm_g):
    given = dict(locals())
    w = {n: given[n] for n in _WEIGHTS}
    m = {n: given["m_" + n] for n in _WEIGHTS}
    v = {n: given["v_" + n] for n in _WEIGHTS}
    d = x.shape[-1]

    shards = {n: w[n][0] for n in _BIG}
    small = {
        "attn_norm_g": w["attn_norm_g"], "mlp_norm_g": w["mlp_norm_g"], "ple_norm_g": w["ple_norm_g"],
        "gate_norm_g": w["gate_norm_g"], "final_norm_g": w["final_norm_g"].reshape(1, d),
        "q_norm_g": w["q_norm_g"], "k_norm_g": w["k_norm_g"], "sink_logits": w["sink_logits"],
        "rel_bias_table": w["rel_bias_table"],
    }

    def update(n, parts):
        res = _sum_adamw(parts, w[n][0], m[n][0], v[n][0], name="adamw_" + n)
        return [t.reshape(w[n].shape) for t in res]

    loss_part, grad_x, big, small_g = _local_step(x[0], p[0, 0], loss_target[0], shards, small, update)
    grads, deltas, new_m, new_v = [{n: big[n][i] for n in _BIG} for i in range(4)]

    shapes = {n: w[n].shape for n in _WEIGHTS if n not in _BIG}
    pack = _pack_small(small_g, d)
    pack = pack.at[_PACK_ROWS - 1, :1].add(0.0 * grads["w_in"][0, 0, :1])
    pack = pack.at[_PACK_ROWS - 1, 1].set(loss_part[0, 0])
    g_small = _allreduce_small(pack)
    loss = g_small[_PACK_ROWS - 1, 1]
    d_small, m_small, v_small = _adamw_small(g_small, _pack_small(w, d), _pack_small(m, d), _pack_small(v, d))
    grads.update(_unpack_small(g_small, shapes))
    deltas.update(_unpack_small(d_small, shapes))
    new_m.update(_unpack_small(m_small, shapes))
    new_v.update(_unpack_small(v_small, shapes))

    return (loss, grad_x[None], *[grads[n] for n in _WEIGHTS], *[deltas[n] for n in _WEIGHTS],
            *[new_m[n] for n in _WEIGHTS], *[new_v[n] for n in _WEIGHTS])
```

```python
import functools
import math

import jax
import jax.numpy as jnp
from jax import lax
from jax.experimental import pallas as pl
from jax.experimental.pallas import tpu as pltpu

F32 = jnp.float32
BF16 = jnp.bfloat16

HEAD_DIM = 128
N_HEADS_A = 8
N_KV_A = 2
N_HEADS_B = 8
N_KV_B = 2
GROUP = 4
GRID_W = 64
BLOCK_Q = 128
WINDOW = 128
N_BUCKETS = 32
MAX_DISTANCE = 128
ROPE_THETA = 10000.0
EPS = 1e-6
NEG_INF = -1e30
ATT_SCALE = HEAD_DIM ** -0.5
LOG2E = math.log2(math.e)
LN2 = math.log(2.0)
Q_SCALE = ATT_SCALE * LOG2E
PAD_LO, PAD_HI = 512, 128
ADAM_LR = 0.001
ADAM_B1 = 0.9
ADAM_B2 = 0.999
ADAM_EPS = 1e-08
ADAM_WD = 0.01
ADAM_STEP = 10

N_CHIPS = 4
N_DEV = 8
COL_QA, COL_KA, COL_VA, COL_QB, COL_KB, COL_VB = 0, 8, 10, 12, 20, 22

VMEM_LIMIT = 52 * 1024 * 1024


def _params(sem=None, collective_id=None):
    return pltpu.CompilerParams(dimension_semantics=sem, vmem_limit_bytes=VMEM_LIMIT, collective_id=collective_id)


_ANY = pl.BlockSpec(memory_space=pl.ANY)
_MESH = pl.DeviceIdType.MESH
SIBLING_BARRIER_ID = 1


def _sibling():
    return (lax.axis_index("x"), lax.axis_index("y"), 1 - lax.axis_index("c"))


class _Comm:
    def __init__(self, inputs, out_shapes, sems, start, finish, aliases=None):
        self.inputs, self.out_shapes, self.sems = list(inputs), list(out_shapes), list(sems)
        self.start, self.finish, self.aliases = start, finish, dict(aliases or {})


def _call(body, *, name, grid, in_specs, out_specs, out_shape, args, scratch_shapes=(), sem=None, comm=None,
          after=None, aliases=None):
    in_specs, out_specs, out_shape = list(in_specs), list(out_specs), list(out_shape)
    scratch_shapes = list(scratch_shapes)
    n_in, n_out, n_sc = len(in_specs), len(out_specs), len(scratch_shapes)
    behind = [] if after is None else [after]
    aliases = dict(aliases or {})
    if comm is None:
        res = pl.pallas_call(
            (lambda *refs: body(*refs[:n_in], *refs[n_in + len(behind):])) if behind else body,
            name=name, grid=grid, in_specs=in_specs + [_ANY] * len(behind), out_specs=out_specs,
            out_shape=out_shape, scratch_shapes=scratch_shapes, input_output_aliases=aliases,
            compiler_params=_params(sem))(*args, *behind)
        return list(res), []
    c_in, c_out = len(comm.inputs), len(comm.out_shapes)

    def hosted(*refs):
        pos = [0]

        def take(n):
            pos[0] += n
            return refs[pos[0] - n:pos[0]]

        ins, c_ins, _, outs, c_outs, scr = (take(n_in), take(c_in), take(len(behind)), take(n_out), take(c_out),
                                            take(n_sc))
        c_sems = refs[pos[0]:]
        ids = [pl.program_id(a) for a in range(len(grid))]
        first = functools.reduce(jnp.logical_and, [i == 0 for i in ids])
        last = functools.reduce(jnp.logical_and, [i == g - 1 for i, g in zip(ids, grid)])

        @pl.when(first)
        def _():
            barrier = pltpu.get_barrier_semaphore()
            pl.semaphore_signal(barrier, inc=1, device_id=_sibling(), device_id_type=_MESH)
            pl.semaphore_wait(barrier, 1)
            comm.start(c_ins, c_outs, c_sems)

        body(*ins, *outs, *scr)

        @pl.when(last)
        def _():
            comm.finish(c_ins, c_outs, c_sems)

    res = pl.pallas_call(
        hosted, name=name, grid=grid, in_specs=in_specs + [_ANY] * (c_in + len(behind)),
        out_specs=out_specs + [_ANY] * c_out,
        out_shape=out_shape + comm.out_shapes, scratch_shapes=scratch_shapes + comm.sems,
        input_output_aliases={**aliases, **{n_in + i: n_out + o for i, o in comm.aliases.items()}},
        compiler_params=_params(("arbitrary",) * len(grid), SIBLING_BARRIER_ID))(*args, *comm.inputs, *behind)
    return list(res[:n_out]), list(res[n_out:])


def _matmul(a, b, *, mode, out_dtypes, name, epilogue=None, extras=(), bm=1024, bn=1024, bk=2048,
            out_stack=0, comm=None, after=None, vecs=()):
    stacked = b.ndim == 3
    if mode == "nn":
        m, k = a.shape
        if stacked:
            nj, kb, ns = b.shape
            n, ks = nj * ns, k
        else:
            kb, n = b.shape
            ns, ks = n, k
        dn = (((1,), (0,)), ((), ()))
    elif mode == "nt":
        m, k = a.shape
        if stacked:
            nj, n, ks = b.shape
            kb = nj * ks
        else:
            n, kb = b.shape
            ks = kb
        ns = n
        dn = (((1,), (1,)), ((), ()))
    else:
        k, m = a.shape
        kb, n = b.shape
        ns, ks = n, k
        dn = (((0,), (0,)), ((), ()))
    assert k == kb and not (stacked and mode == "tn")
    ns_out = n // out_stack if out_stack else n
    per_blk = min(bk, k) // ks if stacked and mode == "nt" and bk > ks else 0
    bm, bn, bk = min(bm, m), min(bn, ns, ns_out), per_blk * ks if per_blk else min(bk, ks)
    assert m % bm == 0 and ns % bn == 0 and ns_out % bn == 0 and (k % bk == 0 if per_blk else ks % bk == 0)
    gm, gn, gk = m // bm, n // bn, k // bk

    if mode == "tn":
        a_spec = pl.BlockSpec((bk, bm), lambda i, j, q: (q, i))
    else:
        a_spec = pl.BlockSpec((bm, bk), lambda i, j, q: (i, q))
    if mode == "nt":
        if per_blk:
            b_spec = pl.BlockSpec((per_blk, bn, ks), lambda i, j, q: (q, j, 0))
        elif stacked:
            per = ks // bk
            b_spec = pl.BlockSpec((None, bn, bk), lambda i, j, q: (q // per, j, q % per))
        else:
            b_spec = pl.BlockSpec((bn, bk), lambda i, j, q: (j, q))
    else:
        if stacked:
            per = ns // bn
            b_spec = pl.BlockSpec((None, bk, bn), lambda i, j, q: (j // per, q, j % per))
        else:
            b_spec = pl.BlockSpec((bk, bn), lambda i, j, q: (q, j))
    ex_spec = pl.BlockSpec((bm, bn), lambda i, j, q: (i, j))
    if out_stack:
        per_o = ns_out // bn
        o_spec = pl.BlockSpec((None, bm, bn), lambda i, j, q: (j // per_o, i, j % per_o))
        o_shape = (out_stack, m, ns_out)
    else:
        o_spec = ex_spec
        o_shape = (m, n)
    n_ex, n_out = len(extras) + len(vecs), len(out_dtypes)

    def body(a_ref, b_ref, *rest):
        ex, outs = rest[:n_ex], rest[n_ex:n_ex + n_out]
        if per_blk:
            part = sum(lax.dot_general(a_ref[:, t * ks:(t + 1) * ks], b_ref[t], dn, preferred_element_type=F32)
                       for t in range(per_blk))
        else:
            part = lax.dot_general(a_ref[...], b_ref[...], dn, preferred_element_type=F32)

        def finish(acc):
            res = epilogue(acc, *[e[...] for e in ex]) if epilogue else (acc,)
            for o, r in zip(outs, res):
                o[...] = r.astype(o.dtype)

        if gk == 1:
            finish(part)
        else:
            acc_ref = rest[-1]
            q = pl.program_id(2)

            @pl.when(q == 0)
            def _():
                acc_ref[...] = part

            @pl.when(q > 0)
            def _():
                acc_ref[...] += part

            @pl.when(q == gk - 1)
            def _():
                finish(acc_ref[...])

    res, c_res = _call(
        body, name=name, grid=(gm, gn, gk),
        in_specs=[a_spec, b_spec] + [ex_spec] * len(extras)
        + [pl.BlockSpec((1, bn), lambda i, j, q: (0, j))] * len(vecs),
        out_specs=[o_spec] * n_out,
        out_shape=[jax.ShapeDtypeStruct(o_shape, dt) for dt in out_dtypes],
        scratch_shapes=[pltpu.VMEM((bm, bn), F32)] if gk > 1 else [],
        sem=("parallel", "parallel", "arbitrary"), args=(a, b, *extras, *vecs), comm=comm, after=after)
    res = res[0] if n_out == 1 else res
    return res if comm is None else (res, c_res)


def _rms_fwd(x, g, *, name, tm=1024, comm=None):
    s, d = x.shape
    tm = min(tm, s)

    def body(x_ref, g_ref, o_ref):
        xf = x_ref[...]
        r = lax.rsqrt(jnp.mean(xf * xf, axis=-1, keepdims=True) + EPS)
        o_ref[...] = (xf * r * g_ref[...]).astype(o_ref.dtype)

    res, c_res = _call(
        body, name=name, grid=(s // tm,),
        in_specs=[pl.BlockSpec((tm, d), lambda i: (i, 0)), pl.BlockSpec((1, d), lambda i: (0, 0))],
        out_specs=[pl.BlockSpec((tm, d), lambda i: (i, 0))],
        out_shape=[jax.ShapeDtypeStruct((s, d), BF16)],
        sem=("parallel",), args=(x, g), comm=comm)
    return res[0] if comm is None else (res[0], c_res)


def _rms_bwd(x, dy, g, add, *, name, want_bf16, tm=512):
    s, d = x.shape
    tm = min(tm, s)

    def body(x_ref, dy_ref, g_ref, add_ref, dx_ref, *rest):
        dg_ref = rest[-1]
        i = pl.program_id(0)
        xf = x_ref[...]
        dyf = dy_ref[...].astype(F32)
        r = lax.rsqrt(jnp.mean(xf * xf, axis=-1, keepdims=True) + EPS)
        xh = xf * r
        dyg = dyf * g_ref[...]
        dx = r * (dyg - xh * jnp.mean(dyg * xh, axis=-1, keepdims=True))
        tot = add_ref[...] + dx
        dx_ref[...] = tot
        if want_bf16:
            rest[0][...] = tot.astype(BF16)
        part = jnp.sum(dyf * xh, axis=0, keepdims=True)

        @pl.when(i == 0)
        def _():
            dg_ref[...] = part

        @pl.when(i > 0)
        def _():
            dg_ref[...] += part

    row = pl.BlockSpec((tm, d), lambda i: (i, 0))
    vec = pl.BlockSpec((1, d), lambda i: (0, 0))
    out_specs = [row] + ([row] if want_bf16 else []) + [vec]
    out_shape = [jax.ShapeDtypeStruct((s, d), F32)]
    if want_bf16:
        out_shape.append(jax.ShapeDtypeStruct((s, d), BF16))
    out_shape.append(jax.ShapeDtypeStruct((1, d), F32))
    return pl.pallas_call(
        body,
        name=name,
        grid=(s // tm,),
        in_specs=[row, row, vec, row],
        out_specs=out_specs,
        out_shape=out_shape,
        compiler_params=_params(("arbitrary",)),
    )(x, dy, g, add)


def _tail(h2, gate, pp, target, g_ple, g_final, *, tm=256):
    s, d = h2.shape
    tm = min(tm, s)

    def body(h2_ref, gate_ref, pp_ref, t_ref, gp_ref, gf_ref, dh3_ref, dz_ref, dpp_ref, dgf_ref, dgp_ref, loss_ref):
        i = pl.program_id(0)
        ppf = pp_ref[...]
        gate_v = gate_ref[...]
        r_p = lax.rsqrt(jnp.mean(ppf * ppf, axis=-1, keepdims=True) + EPS)
        eh = ppf * r_p
        e = eh * gp_ref[...]
        h3 = h2_ref[...] + gate_v * e
        r_f = lax.rsqrt(jnp.mean(h3 * h3, axis=-1, keepdims=True) + EPS)
        yh = h3 * r_f
        diff = yh * gf_ref[...] - t_ref[...]
        loss_part = 0.5 * jnp.sum(jnp.mean(diff * diff, axis=-1, keepdims=True), axis=0, keepdims=True)
        dy = diff / d
        dgf = jnp.sum(dy * yh, axis=0, keepdims=True)
        dyg = dy * gf_ref[...]
        dh3 = r_f * (dyg - yh * jnp.mean(dyg * yh, axis=-1, keepdims=True))
        dh3_ref[...] = dh3
        de = dh3 * gate_v
        dz_ref[...] = (dh3 * e * gate_v * (1.0 - gate_v)).astype(BF16)
        dgp = jnp.sum(de * eh, axis=0, keepdims=True)
        deg = de * gp_ref[...]
        dpp_ref[...] = (r_p * (deg - eh * jnp.mean(deg * eh, axis=-1, keepdims=True))).astype(BF16)
        loss_row = jnp.broadcast_to(loss_part, (1, 128))

        @pl.when(i == 0)
        def _():
            dgf_ref[...] = dgf
            dgp_ref[...] = dgp
            loss_ref[...] = loss_row

        @pl.when(i > 0)
        def _():
            dgf_ref[...] += dgf
            dgp_ref[...] += dgp
            loss_ref[...] += loss_row

    row = pl.BlockSpec((tm, d), lambda i: (i, 0))
    vec = pl.BlockSpec((1, d), lambda i: (0, 0))
    return pl.pallas_call(
        body,
        name="tail_fwd_bwd",
        grid=(s // tm,),
        in_specs=[row, row, row, row, vec, vec],
        out_specs=[row, row, row, vec, vec, pl.BlockSpec((1, 128), lambda i: (0, 0))],
        out_shape=[
            jax.ShapeDtypeStruct((s, d), F32),
            jax.ShapeDtypeStruct((s, d), BF16),
            jax.ShapeDtypeStruct((s, d), BF16),
            jax.ShapeDtypeStruct((1, d), F32),
            jax.ShapeDtypeStruct((1, d), F32),
            jax.ShapeDtypeStruct((1, 128), F32),
        ],
        compiler_params=_params(("arbitrary",)),
    )(h2, gate, pp, target, g_ple, g_final)


def _rope_tables(s):
    rows = s // GRID_W
    half = HEAD_DIM // 2
    inv_freq = ROPE_THETA ** (-jnp.arange(0, half, 2, dtype=F32) / half)
    ang_r = jnp.arange(rows, dtype=jnp.int32).astype(F32)[:, None] * inv_freq
    ang_c = jnp.arange(GRID_W, dtype=jnp.int32).astype(F32)[:, None] * inv_freq
    cr, sr = (jnp.repeat(t, GRID_W, axis=0) for t in (jnp.cos(ang_r), jnp.sin(ang_r)))
    cc, sc = (jnp.tile(t, (rows, 1)) for t in (jnp.cos(ang_c), jnp.sin(ang_c)))
    cos_t = jnp.concatenate([cr, cr, cc, cc], axis=-1)
    sin_t = jnp.concatenate([-sr, sr, -sc, sc], axis=-1)
    return cos_t, sin_t


def _low_quarters(shape):
    return (lax.broadcasted_iota(jnp.int32, shape, len(shape) - 1) % 64) < 32


def _swap_quarters(x, low):
    up = pltpu.roll(x, HEAD_DIM - 32, x.ndim - 1)
    down = pltpu.roll(x, 32, x.ndim - 1)
    return jnp.where(low, up, down)


def _cols(first, count=1):
    return slice(first * HEAD_DIM, (first + count) * HEAD_DIM)


def _qk_prep(proj, g_q, g_k, cos_t, sin_t, *, tm=1024, comm=None):
    s, n = proj.shape
    tm = min(tm, s)

    def body(x_ref, gq_ref, gk_ref, c_ref, s_ref, o_ref):
        cos_v, sin_v = c_ref[...], s_ref[...]
        low = _low_quarters(cos_v.shape)
        for h in range(COL_VA):
            x = x_ref[:, _cols(h)]
            g = gq_ref[...] if h < COL_KA else gk_ref[...]
            xn = x * lax.rsqrt(jnp.mean(x * x, axis=-1, keepdims=True) + EPS) * g
            xr = xn * cos_v + _swap_quarters(xn, low) * sin_v
            if h < COL_KA:
                xr = xr * Q_SCALE
            o_ref[:, _cols(h)] = xr.astype(BF16)
        o_ref[:, _cols(COL_VA, 2)] = x_ref[:, _cols(COL_VA, 2)].astype(BF16)
        o_ref[:, _cols(COL_QB, N_HEADS_B)] = (x_ref[:, _cols(COL_QB, N_HEADS_B)] * Q_SCALE).astype(BF16)
        o_ref[:, _cols(COL_KB, 4)] = x_ref[:, _cols(COL_KB, 4)].astype(BF16)

    row = pl.BlockSpec((tm, n), lambda i: (i, 0))
    tab = pl.BlockSpec((tm, HEAD_DIM), lambda i: (i, 0))
    vec = pl.BlockSpec((1, HEAD_DIM), lambda i: (0, 0))
    res, c_res = _call(
        body, name="qk_prep", grid=(s // tm,),
        in_specs=[row, vec, vec, tab, tab],
        out_specs=[row],
        out_shape=[jax.ShapeDtypeStruct((s, n), BF16)],
        sem=("parallel",), args=(proj, g_q, g_k, cos_t, sin_t), comm=comm)
    return res[0] if comm is None else (res[0], c_res)


def _qk_bwd(dqa, dka, dva, dqb, dkpad, dvpad, proj, g_q, g_k, cos_t, sin_t, *, comm=None, after=None):
    s, n = proj.shape
    tm = min(PAD_LO, s)
    assert PAD_LO % tm == 0
    lo = PAD_LO // tm

    def body(dqa_ref, dka_ref, dva_ref, dqb_ref, dkb_ref, dvb_ref, x_ref, gq_ref, gk_ref, c_ref, s_ref,
             o_ref, dgq_ref, dgk_ref):
        i = pl.program_id(0)
        cos_v, sin_v = c_ref[...], s_ref[...]
        low = _low_quarters(cos_v.shape)

        def head(d, x, g):
            dn = d * cos_v + _swap_quarters(d * sin_v, low)
            r = lax.rsqrt(jnp.mean(x * x, axis=-1, keepdims=True) + EPS)
            xh = x * r
            dng = dn * g
            dx = r * (dng - xh * jnp.mean(dng * xh, axis=-1, keepdims=True))
            return dx.astype(BF16), jnp.sum(dn * xh, axis=0, keepdims=True)

        acc_q = jnp.zeros((1, HEAD_DIM), F32)
        acc_k = jnp.zeros((1, HEAD_DIM), F32)
        for h in range(N_HEADS_A):
            o_ref[:, _cols(h)], part = head(dqa_ref[:, _cols(h)] * ATT_SCALE, x_ref[:, _cols(h)], gq_ref[...])
            acc_q = acc_q + part
        for h in range(N_KV_A):
            o_ref[:, _cols(COL_KA + h)], part = head(dka_ref[:, _cols(h)] * LN2, x_ref[:, _cols(COL_KA + h)],
                                                     gk_ref[...])
            acc_k = acc_k + part
        o_ref[:, _cols(COL_VA, 2)] = dva_ref[...].astype(BF16)
        o_ref[:, _cols(COL_QB, N_HEADS_B)] = (dqb_ref[...] * ATT_SCALE).astype(BF16)
        o_ref[:, _cols(COL_KB, 2)] = (dkb_ref[...] * LN2).astype(BF16)
        o_ref[:, _cols(COL_VB, 2)] = dvb_ref[...].astype(BF16)

        @pl.when(i == 0)
        def _():
            dgq_ref[...] = acc_q
            dgk_ref[...] = acc_k

        @pl.when(i > 0)
        def _():
            dgq_ref[...] += acc_q
            dgk_ref[...] += acc_k

    def rows(width, shift=0):
        return pl.BlockSpec((tm, width), lambda i: (i + shift, 0))

    kv_w = N_KV_A * HEAD_DIM
    q_w = N_HEADS_A * HEAD_DIM
    vec = pl.BlockSpec((1, HEAD_DIM), lambda i: (0, 0))
    res, c_res = _call(
        body, name="qk_bwd", grid=(s // tm,),
        in_specs=[rows(q_w), rows(kv_w), rows(kv_w), rows(q_w), rows(kv_w, lo), rows(kv_w, lo), rows(n),
                  vec, vec, rows(HEAD_DIM), rows(HEAD_DIM)],
        out_specs=[rows(n), vec, vec],
        out_shape=[
            jax.ShapeDtypeStruct((s, n), BF16),
            jax.ShapeDtypeStruct((1, HEAD_DIM), F32),
            jax.ShapeDtypeStruct((1, HEAD_DIM), F32),
        ],
        sem=("arbitrary",), args=(dqa, dka, dva, dqb, dkpad, dvpad, proj, g_q, g_k, cos_t, sin_t), comm=comm,
        after=after)
    return res if comm is None else (res, c_res)


_NT = (((1,), (1,)), ((), ()))
_TN = (((0,), (0,)), ((), ()))


def _attn_a_fwd(pb, *, tq=4096, sub=256, comm=None, after=None):
    s = pb.shape[0]
    tq = min(tq, s)
    sub = min(sub, tq)

    def body(q_ref, k_ref, v_ref, o_ref, lse_ref):
        k = k_ref[...]
        v = v_ref[...]
        for r in range(tq // sub):
            rows = pl.ds(r * sub, sub)
            sc = lax.dot_general(q_ref[rows, :], k, _NT, preferred_element_type=F32)
            m = jnp.max(sc, axis=-1, keepdims=True)
            p = jnp.exp2(sc - m)
            l = jnp.sum(p, axis=-1, keepdims=True)
            o = jnp.dot(p.astype(BF16), v, preferred_element_type=F32)
            o_ref[rows, :] = (o / l).astype(BF16)
            lse_ref[rows, :] = jnp.broadcast_to(m + jnp.log2(l), (sub, HEAD_DIM))

    res, c_res = _call(
        body, name="attn_a_fwd", grid=(N_HEADS_A, s // tq),
        in_specs=[
            pl.BlockSpec((tq, HEAD_DIM), lambda h, i: (i, COL_QA + h)),
            pl.BlockSpec((s, HEAD_DIM), lambda h, i: (0, COL_KA + h // GROUP)),
            pl.BlockSpec((s, HEAD_DIM), lambda h, i: (0, COL_VA + h // GROUP)),
        ],
        out_specs=[
            pl.BlockSpec((tq, HEAD_DIM), lambda h, i: (i, h)),
            pl.BlockSpec((None, tq, HEAD_DIM), lambda h, i: (h, i, 0)),
        ],
        out_shape=[
            jax.ShapeDtypeStruct((s, (N_HEADS_A + N_HEADS_B) * HEAD_DIM), BF16),
            jax.ShapeDtypeStruct((N_HEADS_A, s, HEAD_DIM), F32),
        ],
        sem=("parallel", "parallel"), args=(pb, pb, pb), comm=comm, after=after)
    return res if comm is None else (res, c_res)


def _attn_a_bwd(pb, att, datt, lse, *, tq=2048, sub=256, comm=None):
    s = pb.shape[0]
    tq = min(tq, s)
    sub = min(sub, tq)

    def body(q_ref, k_ref, v_ref, o_ref, do_ref, lse_ref, dq_ref, dk_ref, dv_ref):
        first = jnp.logical_and(pl.program_id(1) == 0, pl.program_id(2) == 0)
        k = k_ref[...]
        v = v_ref[...]
        dk = dv = None
        for r in range(tq // sub):
            rows = pl.ds(r * sub, sub)
            q = q_ref[rows, :]
            do = do_ref[rows, :]
            sc = lax.dot_general(q, k, _NT, preferred_element_type=F32)
            p = jnp.exp2(sc - lse_ref[rows, :][:, :1])
            dp = lax.dot_general(do, v, _NT, preferred_element_type=F32)
            delta = jnp.sum(do.astype(F32) * o_ref[rows, :].astype(F32), axis=-1, keepdims=True)
            ds = (p * (dp - delta)).astype(BF16)
            dq_ref[rows, :] = jnp.dot(ds, k, preferred_element_type=F32)
            dk_r = lax.dot_general(ds, q, _TN, preferred_element_type=F32)
            dv_r = lax.dot_general(p.astype(BF16), do, _TN, preferred_element_type=F32)
            dk = dk_r if dk is None else dk + dk_r
            dv = dv_r if dv is None else dv + dv_r

        @pl.when(first)
        def _():
            dk_ref[...] = dk
            dv_ref[...] = dv

        @pl.when(jnp.logical_not(first))
        def _():
            dk_ref[...] += dk
            dv_ref[...] += dv

    qmap = lambda kv, g, i: (i, kv * GROUP + g)
    res, c_res = _call(
        body, name="attn_a_bwd", grid=(N_KV_A, GROUP, s // tq),
        in_specs=[
            pl.BlockSpec((tq, HEAD_DIM), lambda kv, g, i: (i, COL_QA + kv * GROUP + g)),
            pl.BlockSpec((s, HEAD_DIM), lambda kv, g, i: (0, COL_KA + kv)),
            pl.BlockSpec((s, HEAD_DIM), lambda kv, g, i: (0, COL_VA + kv)),
            pl.BlockSpec((tq, HEAD_DIM), qmap),
            pl.BlockSpec((tq, HEAD_DIM), qmap),
            pl.BlockSpec((None, tq, HEAD_DIM), lambda kv, g, i: (kv * GROUP + g, i, 0)),
        ],
        out_specs=[
            pl.BlockSpec((tq, HEAD_DIM), qmap),
            pl.BlockSpec((s, HEAD_DIM), lambda kv, g, i: (0, kv)),
            pl.BlockSpec((s, HEAD_DIM), lambda kv, g, i: (0, kv)),
        ],
        out_shape=[
            jax.ShapeDtypeStruct((s, N_HEADS_A * HEAD_DIM), F32),
            jax.ShapeDtypeStruct((s, N_KV_A * HEAD_DIM), F32),
            jax.ShapeDtypeStruct((s, N_KV_A * HEAD_DIM), F32),
        ],
        sem=("arbitrary", "arbitrary", "arbitrary"), args=(pb, pb, pb, att, datt, lse), comm=comm)
    return res if comm is None else (res, c_res)


def _t5_bucket(rel):
    nb = N_BUCKETS // 2
    ret = jnp.where(rel > 0, nb, 0)
    n = jnp.abs(rel)
    max_exact = nb // 2
    nf = jnp.maximum(n, 1).astype(F32)
    large = max_exact + (jnp.log(nf / max_exact) / math.log(MAX_DISTANCE / max_exact)
                         * (nb - max_exact)).astype(jnp.int32)
    large = jnp.minimum(large, nb - 1)
    return ret + jnp.where(n < max_exact, n, large)


def _band_buckets():
    r = jnp.arange(BLOCK_Q, dtype=jnp.int32)
    j = jnp.arange(3 * BLOCK_Q, dtype=jnp.int32)
    return _t5_bucket((j[None, :] - BLOCK_Q) - r[:, None])


def _band_bias(bucket, table_ref, h):
    acc = jnp.zeros(bucket.shape, F32)
    for b in range(N_BUCKETS):
        acc = jnp.where(bucket == b, table_ref[b, h], acc)
    return acc


GQ = GROUP * BLOCK_Q


def _stack_heads(x):
    return jnp.concatenate([x[:, _cols(g)] for g in range(GROUP)], axis=0)


def _unstack_heads(x):
    return jnp.concatenate([x[g * BLOCK_Q:(g + 1) * BLOCK_Q] for g in range(GROUP)], axis=1)


def _group_bias(bucket, table_ref, kv):
    r = lax.broadcasted_iota(jnp.int32, (BLOCK_Q, 3 * BLOCK_Q), 0)
    j = lax.broadcasted_iota(jnp.int32, (BLOCK_Q, 3 * BLOCK_Q), 1)
    inside = jnp.abs(j - BLOCK_Q - r) <= WINDOW
    return jnp.concatenate([jnp.where(inside, _band_bias(bucket, table_ref, kv * GROUP + g) * LOG2E, NEG_INF)
                            for g in range(GROUP)], axis=0)


def _group_sink(sink_ref, kv):
    head = lax.broadcasted_iota(jnp.int32, (GQ, 1), 0) // BLOCK_Q
    snk = jnp.zeros((GQ, 1), F32)
    for g in range(GROUP):
        snk = jnp.where(head == g, sink_ref[0, kv * GROUP + g] * LOG2E, snk)
    return snk


def _band_mask(n, s):
    kabs = n * BLOCK_Q + lax.broadcasted_iota(jnp.int32, (1, 3 * BLOCK_Q), 1) - BLOCK_Q
    return (kabs >= 0) & (kabs < s)


def _band_start(n):
    return pl.multiple_of(n * BLOCK_Q + (PAD_LO - BLOCK_Q), BLOCK_Q)


def _attn_b_fwd(pb, kpad, vpad, bucket, table, sink, att, *, comm=None, after=None):
    s = pb.shape[0]
    nblk = s // BLOCK_Q
    sp = kpad.shape[0]

    def body(table_ref, sink_ref, q0_ref, q1_ref, k_ref, v_ref, bucket_ref, _, o_ref, lse_ref, bias_ref):
        n = pl.program_id(0)

        @pl.when(n == 0)
        def _():
            for kv in range(N_KV_B):
                bias_ref[kv * GQ:(kv + 1) * GQ, :] = _group_bias(bucket_ref[...], table_ref, kv)

        band = pl.ds(_band_start(n), 3 * BLOCK_Q)
        mask = _band_mask(n, s)
        for kv, q_ref in enumerate((q0_ref, q1_ref)):
            kb = k_ref[band, _cols(kv)]
            vb = v_ref[band, _cols(kv)]
            sc = lax.dot_general(_stack_heads(q_ref[...]), kb, _NT, preferred_element_type=F32)
            sc = jnp.where(mask, sc + bias_ref[kv * GQ:(kv + 1) * GQ, :], NEG_INF)
            snk = _group_sink(sink_ref, kv)
            m = jnp.maximum(jnp.max(sc, axis=-1, keepdims=True), snk)
            p = jnp.exp2(sc - m)
            l = jnp.sum(p, axis=-1, keepdims=True) + jnp.exp2(snk - m)
            o = jnp.dot(p.astype(BF16), vb, preferred_element_type=F32)
            o_ref[:, _cols(kv * GROUP, GROUP)] = _unstack_heads((o / l).astype(BF16))
            lse = m + jnp.log2(l)
            for g in range(GROUP):
                lse_ref[kv * GROUP + g] = jnp.broadcast_to(lse[g * BLOCK_Q:(g + 1) * BLOCK_Q], (BLOCK_Q, HEAD_DIM))

    smem = pl.BlockSpec(memory_space=pltpu.SMEM)
    wide = GROUP * HEAD_DIM
    whole = pl.BlockSpec((sp, N_KV_B * HEAD_DIM), lambda n: (0, 0))
    res, c_res = _call(
        body, name="attn_b_fwd", grid=(nblk,),
        in_specs=[
            smem,
            smem,
            pl.BlockSpec((BLOCK_Q, wide), lambda n: (n, COL_QB // GROUP)),
            pl.BlockSpec((BLOCK_Q, wide), lambda n: (n, COL_QB // GROUP + 1)),
            whole,
            whole,
            pl.BlockSpec((BLOCK_Q, 3 * BLOCK_Q), lambda n: (0, 0)),
            _ANY,
        ],
        out_specs=[
            pl.BlockSpec((BLOCK_Q, N_HEADS_B * HEAD_DIM), lambda n: (n, 1)),
            pl.BlockSpec((N_HEADS_B, BLOCK_Q, HEAD_DIM), lambda n: (0, n, 0)),
        ],
        out_shape=[
            jax.ShapeDtypeStruct(att.shape, BF16),
            jax.ShapeDtypeStruct((N_HEADS_B, s, HEAD_DIM), F32),
        ],
        scratch_shapes=[pltpu.VMEM((N_KV_B * GQ, 3 * BLOCK_Q), F32)],
        sem=("arbitrary",), args=(table, sink, pb, pb, kpad, vpad, bucket, att), comm=comm, after=after,
        aliases={7: 0})
    return res if comm is None else (res, c_res)


def _attn_b_bwd(pb, kpad, vpad, att, datt, lse, bucket, table, sink, *, comm=None, after=None):
    s = pb.shape[0]
    nblk = s // BLOCK_Q
    sp = kpad.shape[0]

    def body(table_ref, sink_ref, q0_ref, q1_ref, k_ref, v_ref, o_ref, do_ref, lse_ref, bucket_ref,
             dq_ref, dk_ref, dv_ref, dtab_ref, dsink_ref, bias_ref, dbias_ref):
        n = pl.program_id(0)

        @pl.when(n == 0)
        def _():
            dk_ref[...] = jnp.zeros_like(dk_ref)
            dv_ref[...] = jnp.zeros_like(dv_ref)
            dbias_ref[...] = jnp.zeros_like(dbias_ref)
            dsink_ref[...] = jnp.zeros_like(dsink_ref)
            for kv in range(N_KV_B):
                bias_ref[kv * GQ:(kv + 1) * GQ, :] = _group_bias(bucket_ref[...], table_ref, kv)

        band = pl.ds(_band_start(n), 3 * BLOCK_Q)
        mask = _band_mask(n, s)
        for kv, q_ref in enumerate((q0_ref, q1_ref)):
            wide_cols = _cols(kv * GROUP, GROUP)
            q = _stack_heads(q_ref[...])
            do = _stack_heads(do_ref[:, wide_cols])
            o = _stack_heads(o_ref[:, wide_cols])
            kb = k_ref[band, _cols(kv)]
            vb = v_ref[band, _cols(kv)]
            lse = jnp.concatenate([lse_ref[kv * GROUP + g][:, :1] for g in range(GROUP)], axis=0)
            sc = lax.dot_general(q, kb, _NT, preferred_element_type=F32)
            sc = jnp.where(mask, sc + bias_ref[kv * GQ:(kv + 1) * GQ, :], NEG_INF)
            p = jnp.exp2(sc - lse)
            dp = lax.dot_general(do, vb, _NT, preferred_element_type=F32)
            delta = jnp.sum(do.astype(F32) * o.astype(F32), axis=-1, keepdims=True)
            ds = p * (dp - delta)
            dsb = ds.astype(BF16)
            dq_ref[:, wide_cols] = _unstack_heads(jnp.dot(dsb, kb, preferred_element_type=F32))
            dk_ref[band, _cols(kv)] += lax.dot_general(dsb, q, _TN, preferred_element_type=F32)
            dv_ref[band, _cols(kv)] += lax.dot_general(p.astype(BF16), do, _TN, preferred_element_type=F32)
            dbias_ref[kv * GQ:(kv + 1) * GQ, :] += ds
            sink_part = -jnp.exp2(_group_sink(sink_ref, kv) - lse) * delta
            for g in range(GROUP):
                rows = slice(g * BLOCK_Q, (g + 1) * BLOCK_Q)
                dsink_ref[kv * GROUP + g] += jnp.broadcast_to(
                    jnp.sum(sink_part[rows], axis=0, keepdims=True), (1, HEAD_DIM))

        @pl.when(n == nblk - 1)
        def _():
            bucket_v = bucket_ref[...]
            row = lax.broadcasted_iota(jnp.int32, (N_BUCKETS, HEAD_DIM), 0)
            for h in range(N_HEADS_B):
                acc = dbias_ref[h * BLOCK_Q:(h + 1) * BLOCK_Q, :]
                tot = jnp.zeros((N_BUCKETS, HEAD_DIM), F32)
                for b in range(N_BUCKETS):
                    tot = jnp.where(row == b, jnp.sum(jnp.where(bucket_v == b, acc, 0.0), keepdims=True), tot)
                dtab_ref[h] = tot

    smem = pl.BlockSpec(memory_space=pltpu.SMEM)
    wide = GROUP * HEAD_DIM
    whole = pl.BlockSpec((sp, N_KV_B * HEAD_DIM), lambda n: (0, 0))
    group_b = pl.BlockSpec((BLOCK_Q, N_HEADS_B * HEAD_DIM), lambda n: (n, 1))
    res, c_res = _call(
        body, name="attn_b_bwd", grid=(nblk,),
        in_specs=[
            smem,
            smem,
            pl.BlockSpec((BLOCK_Q, wide), lambda n: (n, COL_QB // GROUP)),
            pl.BlockSpec((BLOCK_Q, wide), lambda n: (n, COL_QB // GROUP + 1)),
            whole,
            whole,
            group_b,
            group_b,
            pl.BlockSpec((N_HEADS_B, BLOCK_Q, HEAD_DIM), lambda n: (0, n, 0)),
            pl.BlockSpec((BLOCK_Q, 3 * BLOCK_Q), lambda n: (0, 0)),
        ],
        out_specs=[
            pl.BlockSpec((BLOCK_Q, N_HEADS_B * HEAD_DIM), lambda n: (n, 0)),
            whole,
            whole,
            pl.BlockSpec((N_HEADS_B, N_BUCKETS, HEAD_DIM), lambda n: (0, 0, 0)),
            pl.BlockSpec((N_HEADS_B, 1, HEAD_DIM), lambda n: (0, 0, 0)),
        ],
        out_shape=[
            jax.ShapeDtypeStruct((s, N_HEADS_B * HEAD_DIM), F32),
            jax.ShapeDtypeStruct((sp, N_KV_B * HEAD_DIM), F32),
            jax.ShapeDtypeStruct((sp, N_KV_B * HEAD_DIM), F32),
            jax.ShapeDtypeStruct((N_HEADS_B, N_BUCKETS, HEAD_DIM), F32),
            jax.ShapeDtypeStruct((N_HEADS_B, 1, HEAD_DIM), F32),
        ],
        scratch_shapes=[pltpu.VMEM((N_KV_B * GQ, 3 * BLOCK_Q), F32), pltpu.VMEM((N_KV_B * GQ, 3 * BLOCK_Q), F32)],
        sem=("arbitrary",),
        args=(table, sink, pb, pb, kpad, vpad, att, datt, lse, bucket), comm=comm, after=after)
    return res if comm is None else (res, c_res)


def _other_chips(x, y):
    return [(x, 1 - y), (1 - x, y), (1 - x, 1 - y)]


_HBM = pl.BlockSpec(memory_space=pltpu.HBM)
_SEM = pl.BlockSpec(memory_space=pltpu.SEMAPHORE)
_SPLIT = pltpu.CompilerParams(has_side_effects=pltpu.SideEffectType.DATAFLOW_SIDE_EFFECTING)


def _in_hbm(a):
    return pltpu.with_memory_space_constraint(a, pltpu.HBM)


def _my_half(rows):
    c = lax.axis_index("c")
    half = rows // 2
    return pl.ds(pl.multiple_of(c * half, half), half), pl.ds(pl.multiple_of((1 - c) * half, half), half)


def _gather_route(shapes):
    def route(src, land):
        x, y, c = lax.axis_index("x"), lax.axis_index("y"), lax.axis_index("c")
        out = []
        for t, shape in enumerate(shapes):
            mine, _ = _my_half(shape[0])
            for px, py in _other_chips(x, y):
                out.append((src[t].at[mine], land[t].at[2 * x + y, mine], land[t].at[2 * px + py, mine], (px, py, c)))
        return out

    return route


def _exchange_route(n_t):
    def route(src, land):
        x, y, c = lax.axis_index("x"), lax.axis_index("y"), lax.axis_index("c")
        out = []
        for t in range(n_t):
            for px, py in _other_chips(x, y):
                k = 2 * px + py
                out.append((src[t].at[k], land[t].at[2 * (2 * x + y) + c], land[t].at[2 * k + c], (px, py, c)))
        return out

    return route


def _own_slot(shape, dtype, slot, block):
    return lax.dynamic_update_slice(lax.empty(shape, dtype), block[None], (slot,) + (0,) * (len(shape) - 1))


def _split_start(name, srcs, lands, route, after):
    n = len(srcs)

    def body(*refs):
        src, land, send_sems, recv_sems, token = refs[:n], refs[n:2 * n], refs[2 * n + 1], refs[2 * n + 2], refs[-1]
        for i, (src_ref, dst_ref, _, to) in enumerate(route(src, land)):
            pltpu.make_async_remote_copy(src_ref=src_ref, dst_ref=dst_ref, send_sem=send_sems.at[i],
                                         recv_sem=recv_sems.at[i], device_id=to, device_id_type=_MESH).start()
        token[...] = jnp.zeros_like(token)

    sem = pltpu.SemaphoreType.DMA((3 * n,))
    lands = list(lands)
    res = pl.pallas_call(
        body, name=name,
        in_specs=[_HBM] * (2 * n) + [_ANY],
        out_specs=[_SEM, _SEM] + [_HBM] * (2 * n) + [pl.BlockSpec(memory_space=pltpu.VMEM)],
        out_shape=[sem, sem] + [pltpu.HBM(a.shape, a.dtype) for a in list(srcs) + lands]
        + [jax.ShapeDtypeStruct((8, 128), F32)],
        input_output_aliases={i: 2 + i for i in range(2 * n)},
        compiler_params=_SPLIT,
    )(*[_in_hbm(a) for a in srcs], *[_in_hbm(a) for a in lands], after)
    return (res[0], res[1]), res[2:2 + n], res[2 + n:2 + 2 * n], res[-1]


def _split_wait(name, srcs, lands, sems, route, after):
    n = len(srcs)

    def body(*refs):
        src, land, send_sems, recv_sems = refs[:n], refs[n:2 * n], refs[2 * n], refs[2 * n + 1]
        for i, (src_ref, _, dst_ref, to) in enumerate(route(src, land)):
            cp = pltpu.make_async_remote_copy(src_ref=src_ref, dst_ref=dst_ref, send_sem=send_sems.at[i],
                                              recv_sem=recv_sems.at[i], device_id=to, device_id_type=_MESH)
            cp.wait_send()
            cp.wait_recv()

    res = pl.pallas_call(
        body, name=name,
        in_specs=[_HBM] * (2 * n) + [_SEM, _SEM, _ANY],
        out_specs=[_HBM] * (2 * n),
        out_shape=[pltpu.HBM(a.shape, a.dtype) for a in list(srcs) + list(lands)],
        input_output_aliases={i: i for i in range(2 * n)},
        compiler_params=_SPLIT,
    )(*srcs, *lands, sems[0], sems[1], after)
    return res[:n], res[n:]


def _comm_only(name, comm):
    return _call(lambda: None, name=name, grid=(1,), in_specs=[], out_specs=[], out_shape=[], args=(), comm=comm)[1]


def _swap_comm(shards, lands):
    n_t = len(lands)

    def copies(land, sems, later):
        send_sems, recv_sems = sems
        x, y = lax.axis_index("x"), lax.axis_index("y")
        sends, recvs = [], []
        for t in range(n_t):
            mine, other = _my_half(shards[t].shape[0])
            for j, (px, py) in enumerate(_other_chips(x, y)):
                k = 2 * px + py
                for part, out in ((mine, sends), (other, recvs)) if later else ((mine, sends),):
                    out.append(pltpu.make_async_remote_copy(
                        src_ref=land[t].at[k, part], dst_ref=land[t].at[k, part], send_sem=send_sems.at[3 * t + j],
                        recv_sem=recv_sems.at[3 * t + j], device_id=_sibling(), device_id_type=_MESH))
        return sends, recvs

    def start(ins, land, sems):
        for cp in copies(land, sems, False)[0]:
            cp.start()

    def finish(ins, land, sems):
        sends, recvs = copies(land, sems, True)
        for cp in recvs:
            cp.wait_recv()
        for cp in sends:
            cp.wait_send()

    return _Comm(
        lands, [jax.ShapeDtypeStruct(a.shape, a.dtype) for a in lands],
        [pltpu.SemaphoreType.DMA((3 * n_t,)), pltpu.SemaphoreType.DMA((3 * n_t,))],
        start, finish, aliases={t: t for t in range(n_t)})


def _forward_comm(partials, lands):
    n_t = len(lands)

    def copies(land, sems, later):
        send_sems, recv_sems = sems
        x, y, c = lax.axis_index("x"), lax.axis_index("y"), lax.axis_index("c")
        sends, recvs = [], []
        for t in range(n_t):
            for j, k in enumerate([2 * x + y] + [2 * px + py for px, py in _other_chips(x, y)]):
                for slot, out in ((2 * k + c, sends), (2 * k + 1 - c, recvs)) if later else ((2 * k + c, sends),):
                    out.append(pltpu.make_async_remote_copy(
                        src_ref=land[t].at[slot], dst_ref=land[t].at[slot], send_sem=send_sems.at[4 * t + j],
                        recv_sem=recv_sems.at[4 * t + j], device_id=_sibling(), device_id_type=_MESH))
        return sends, recvs

    def start(ins, land, sems):
        for cp in copies(land, sems, False)[0]:
            cp.start()

    def finish(ins, land, sems):
        sends, recvs = copies(land, sems, True)
        for cp in recvs:
            cp.wait_recv()
        for cp in sends:
            cp.wait_send()

    return _Comm(
        lands, [jax.ShapeDtypeStruct(a.shape, a.dtype) for a in lands],
        [pltpu.SemaphoreType.DMA((4 * n_t,)), pltpu.SemaphoreType.DMA((4 * n_t,))],
        start, finish, aliases={t: t for t in range(n_t)})


def _allreduce_small(pack):
    rows, d = pack.shape

    def body(p_ref, sum_ref, all_ref, send_sems, recv_sems):
        x, y, c = lax.axis_index("x"), lax.axis_index("y"), lax.axis_index("c")
        me = 4 * x + 2 * y + c
        all_ref[me] = p_ref[...]
        peers = []
        for dx in range(2):
            for dy in range(2):
                for dc in range(2):
                    if dx or dy or dc:
                        px = 1 - x if dx else x
                        py = 1 - y if dy else y
                        pc = 1 - c if dc else c
                        peers.append((4 * dx + 2 * dy + dc - 1, (px, py, pc)))
        sends = []
        for k, to in peers:
            cp = pltpu.make_async_remote_copy(
                src_ref=p_ref, dst_ref=all_ref.at[me], send_sem=send_sems.at[k], recv_sem=recv_sems.at[k],
                device_id=to, device_id_type=_MESH)
            cp.start()
            sends.append(cp)
        for k, (px, py, pc) in peers:
            pltpu.make_async_remote_copy(
                src_ref=p_ref, dst_ref=all_ref.at[4 * px + 2 * py + pc], send_sem=send_sems.at[k],
                recv_sem=recv_sems.at[k], device_id=(px, py, pc), device_id_type=_MESH).wait_recv()
        for cp in sends:
            cp.wait_send()
        tot = all_ref[0]
        for i in range(1, N_DEV):
            tot = tot + all_ref[i]
        sum_ref[...] = tot

    vm = pl.BlockSpec(memory_space=pltpu.VMEM)
    return pl.pallas_call(
        body,
        name="allreduce_small",
        in_specs=[vm],
        out_specs=vm,
        out_shape=jax.ShapeDtypeStruct((rows, d), F32),
        scratch_shapes=[
            pltpu.VMEM((N_DEV, rows, d), F32),
            pltpu.SemaphoreType.DMA((N_DEV - 1,)),
            pltpu.SemaphoreType.DMA((N_DEV - 1,)),
        ],
    )(pack)


def _adamw_math(w, g, m, v):
    m = ADAM_B1 * m + (1.0 - ADAM_B1) * g
    v = ADAM_B2 * v + (1.0 - ADAM_B2) * (g * g)
    m_hat = m / (1.0 - ADAM_B1 ** ADAM_STEP)
    v_hat = v / (1.0 - ADAM_B2 ** ADAM_STEP)
    delta = -ADAM_LR * (m_hat / (jnp.sqrt(v_hat) + ADAM_EPS) + ADAM_WD * w)
    return delta, m, v


def _sum_adamw(parts, w, m, v, *, name, tr=512):
    r, c = w.shape
    tr = min(tr, r)
    tc = min(c, 1024)

    def body(p_ref, w_ref, m_ref, v_ref, g_ref, d_ref, m2_ref, v2_ref):
        g = p_ref[0].astype(F32)
        for i in range(1, N_DEV):
            g = g + p_ref[i].astype(F32)
        delta, m2, v2 = _adamw_math(w_ref[...], g, m_ref[...], v_ref[...])
        g_ref[...] = g
        d_ref[...] = delta
        m2_ref[...] = m2
        v2_ref[...] = v2

    blk = pl.BlockSpec((tr, tc), lambda i, j: (i, j))
    return pl.pallas_call(
        body,
        name=name,
        grid=(r // tr, c // tc),
        in_specs=[pl.BlockSpec((N_DEV, tr, tc), lambda i, j: (0, i, j)), blk, blk, blk],
        out_specs=[blk] * 4,
        out_shape=[jax.ShapeDtypeStruct((r, c), F32)] * 4,
        compiler_params=_params(("parallel", "parallel")),
    )(parts, w, m, v)


def _adamw_small(g, w, m, v):
    def body(g_ref, w_ref, m_ref, v_ref, d_ref, m2_ref, v2_ref):
        delta, m2, v2 = _adamw_math(w_ref[...], g_ref[...], m_ref[...], v_ref[...])
        d_ref[...] = delta
        m2_ref[...] = m2
        v2_ref[...] = v2

    vm = pl.BlockSpec(memory_space=pltpu.VMEM)
    return pl.pallas_call(
        body,
        name="adamw_small",
        in_specs=[vm] * 4,
        out_specs=[vm] * 3,
        out_shape=[jax.ShapeDtypeStruct(g.shape, F32)] * 3,
    )(g, w, m, v)


def _relu2_epilogue(acc):
    ra = jnp.maximum(acc, 0.0)
    return ra * ra, ra


def _residual_norm_epilogue(acc, res, g):
    h = acc + res
    return h, h * lax.rsqrt(jnp.mean(h * h, axis=-1, keepdims=True) + EPS) * g


def _rows(stacked):
    return stacked.reshape(stacked.shape[0] * stacked.shape[1], stacked.shape[2])


def _by_chip(mat):
    return mat.reshape(N_CHIPS, mat.shape[0] // N_CHIPS, mat.shape[1])


def _local_step(x, p, target, shards, small, update):
    s, d = x.shape
    cos_t, sin_t = _rope_tables(s)
    bucket = _band_buckets()
    p_bf = p.astype(BF16)
    wts = {}

    chip = 2 * lax.axis_index("x") + lax.axis_index("y")
    core = lax.axis_index("c")

    def gather(tag, names, after):
        srcs = [cast[n] for n in names]
        route = _gather_route([a.shape for a in srcs])
        sems, srcs, lands, token = _split_start(f"gather_start_{tag}", srcs, [zones[n] for n in names], route, after)

        def landed(done):
            got_srcs, got_lands = _split_wait(f"gather_wait_{tag}", srcs, lands, sems, route, done)
            comm = _swap_comm(got_srcs, got_lands)
            comm.waited = got_srcs[0]
            return comm

        return landed, token

    def prepare(n, zero):
        cast[n] = (shards[n] + zero).astype(BF16)
        zones[n] = _own_slot((N_CHIPS,) + cast[n].shape, BF16, chip, cast[n])

    cast, zones = {}, {}
    prepare("w_in", 0.0)
    in_landed, token = gather("in", ["w_in"], small["attn_norm_g"])
    for n in shards:
        if n != "w_in":
            prepare(n, token[:1, :1])
    g_attn = small["attn_norm_g"] + token[:1, :1]
    u = _rms_fwd(x, g_attn, name="norm_attn")
    prepared = u[:1, :1].astype(F32) + sum(
        (lax.dynamic_slice(zones[n], (chip, 0, 0), (1, 1, 1))[0] + cast[n][:1, :1]).astype(F32)
        for n in zones if n != "w_in")
    (wts["w_in"],) = _comm_only("swap_w_in", in_landed(prepared))
    mid_landed, token = gather("mid", ["w_out"], wts["w_in"])
    proj = _matmul(u, wts["w_in"], mode="nn", out_dtypes=[F32], name="mm_in", bn=768, bm=2048, after=token)
    pb, (w_out_s,) = _qk_prep(proj, small["q_norm_g"], small["k_norm_g"], cos_t, sin_t, comm=mid_landed(proj))
    wts["w_out"] = _rows(w_out_s)
    up_landed, token = gather("up", ["w_up"], pb)
    att_a, lse_a = _attn_a_fwd(pb, after=token)
    pad = ((PAD_LO, PAD_HI), (0, 0))
    kpad = jnp.pad(pb[:, COL_KB * HEAD_DIM:COL_VB * HEAD_DIM], pad)
    vpad = jnp.pad(pb[:, COL_VB * HEAD_DIM:], pad)
    up_swap = up_landed(att_a)
    down_landed, token = gather("down", ["w_down"], up_swap.waited)
    (att, lse_b), (wts["w_up"],) = _attn_b_fwd(pb, kpad, vpad, bucket, small["rel_bias_table"],
                                               small["sink_logits"], att_a, comm=up_swap, after=token)
    h1, mn = _matmul(att, wts["w_out"], mode="nn", out_dtypes=[F32, BF16], name="mm_out", bm=512, bn=d,
                     epilogue=_residual_norm_epilogue, extras=(x,), vecs=(small["mlp_norm_g"],))
    r, ra = _matmul(mn, wts["w_up"], mode="nn", out_dtypes=[BF16, BF16], name="mm_up", epilogue=_relu2_epilogue,
                    bm=2048)
    (w_down_s,) = _comm_only("swap_w_down", down_landed(r))
    wts["w_down"] = _rows(w_down_s)
    late_landed, token = gather("late", ["w_gate", "ple_w"], w_down_s)
    h2 = _matmul(r, wts["w_down"], mode="nn", out_dtypes=[F32], name="mm_down",
                 epilogue=lambda acc, res: (acc + res,), extras=(h1,), after=token)
    ng, (w_gate_s, wts["ple_w"]) = _rms_fwd(h2, small["gate_norm_g"], name="norm_gate", comm=late_landed(h2))
    wts["w_gate"] = _rows(w_gate_s)
    gate = _matmul(ng, wts["w_gate"], mode="nn", out_dtypes=[F32], name="mm_gate",
                   epilogue=lambda acc: (1.0 / (1.0 + jnp.exp(-acc)),))
    pp = _matmul(p_bf, wts["ple_w"], mode="nn", out_dtypes=[F32], name="mm_ple", bn=512)
    dh3, dz, dpp, dg_final, dg_ple, loss = _tail(h2, gate, pp, target, small["ple_norm_g"], small["final_norm_g"])

    dng = _matmul(dz, wts["w_gate"], mode="nt", out_dtypes=[F32], name="mm_gate_dx")
    gw_gate = _matmul(ng, dz, mode="tn", out_dtypes=[BF16], name="mm_gate_dw", bm=512, bk=4096)
    gw_ple = _matmul(p_bf, dpp, mode="tn", out_dtypes=[BF16], name="mm_ple_dw", bn=512, out_stack=N_CHIPS)
    dh2, dh2_bf, dg_gate = _rms_bwd(h2, dng, small["gate_norm_g"], dh3, name="norm_gate_bwd", want_bf16=True)

    def exchange(tag, partials, after):
        route = _exchange_route(len(partials))
        lands = [_own_slot((N_DEV,) + g.shape[1:], g.dtype, 2 * chip + core,
                           lax.dynamic_index_in_dim(g, chip, 0, keepdims=False)) for g in partials]
        sems, srcs, lands, token = _split_start(f"exchange_start_{tag}", partials, lands, route, after)

        def landed(done):
            got_srcs, got_lands = _split_wait(f"exchange_wait_{tag}", srcs, lands, sems, route, done)
            comm = _forward_comm(got_srcs, got_lands)
            comm.waited = got_srcs[0]
            return comm

        return landed, token

    big = {}
    gate_landed, token = exchange("gate", [_by_chip(gw_gate), gw_ple], dh2_bf)
    gw_down = _matmul(r, dh2_bf, mode="tn", out_dtypes=[BF16], name="mm_down_dw", after=token, bm=512, bk=4096)
    da, (parts_gate, parts_ple) = _matmul(
        dh2_bf, wts["w_down"], mode="nt", out_dtypes=[BF16], name="mm_down_dx", bm=2048,
        epilogue=lambda acc, ra_v: (acc * (2.0 * ra_v.astype(F32)),), extras=(ra,), comm=gate_landed(gw_down))
    down_landed, token = exchange("down", [_by_chip(gw_down)], da)
    big["w_gate"], big["ple_w"] = update("w_gate", parts_gate), update("ple_w", parts_ple)
    gw_up = _matmul(mn, da, mode="tn", out_dtypes=[BF16], name="mm_up_dw", out_stack=N_CHIPS, after=token,
                    bm=512, bk=4096)
    dmn = _matmul(da, wts["w_up"], mode="nt", out_dtypes=[F32], name="mm_up_dx", bk=4096,
                  after=gw_up)
    dh1, dh1_bf, dg_mlp = _rms_bwd(h1, dmn, small["mlp_norm_g"], dh2, name="norm_mlp_bwd", want_bf16=True)
    down_forward = down_landed(dh1_bf)
    up_landed, token = exchange("up", [gw_up], down_forward.waited)
    datt = _matmul(dh1_bf, wts["w_out"], mode="nt", out_dtypes=[BF16], name="mm_out_dx", after=token)
    gw_out = _matmul(att, dh1_bf, mode="tn", out_dtypes=[BF16], name="mm_out_dw", bm=512, bk=4096)
    dqb, dkpad, dvpad, dtab, dsink = _attn_b_bwd(pb, kpad, vpad, att, datt, lse_b, bucket,
                                                 small["rel_bias_table"], small["sink_logits"])
    (dqa, dka, dva), (parts_down,) = _attn_a_bwd(pb, att, datt, lse_a, comm=down_forward)
    up_forward = up_landed(dqa)
    out_landed, token = exchange("out", [_by_chip(gw_out)], up_forward.waited)
    (dproj, dg_q, dg_k), (parts_up,) = _qk_bwd(dqa, dka, dva, dqb, dkpad, dvpad, proj,
                                               small["q_norm_g"], small["k_norm_g"], cos_t, sin_t,
                                               comm=up_forward, after=token)
    gw_in = _matmul(u, dproj, mode="tn", out_dtypes=[BF16], name="mm_in_dw", bn=768, out_stack=N_CHIPS,
                    bm=512, bk=4096)
    out_forward = out_landed(gw_in)
    in_landed, token = exchange("in", [gw_in], out_forward.waited)
    du, (parts_out,) = _matmul(dproj, wts["w_in"], mode="nt", out_dtypes=[F32], name="mm_in_dx", bk=3072,
                               comm=out_forward, after=token)
    grad_x, dg_attn = _rms_bwd(x, du, small["attn_norm_g"], dh1, name="norm_attn_bwd", want_bf16=False)
    for n, parts in (("w_down", parts_down), ("w_up", parts_up), ("w_out", parts_out)):
        big[n] = update(n, parts)
    done = dg_attn + sum(big[n][0][0, :1, :] for n in ("w_down", "w_up", "w_out"))
    (parts_in,) = _comm_only("forward_w_in", in_landed(done))
    big["w_in"] = update("w_in", parts_in)

    small_g = {
        "attn_norm_g": dg_attn, "mlp_norm_g": dg_mlp, "ple_norm_g": dg_ple, "gate_norm_g": dg_gate,
        "final_norm_g": dg_final, "q_norm_g": dg_q, "k_norm_g": dg_k,
        "sink_logits": dsink[:, 0, 0][None, :], "rel_bias_table": dtab[:, :, 0].T,
    }
    return loss, grad_x, big, small_g


_SMALL_ROWS = ["attn_norm_g", "mlp_norm_g", "ple_norm_g", "gate_norm_g", "final_norm_g"]
_PACK_ROWS = 8


def _pack_small(vals, d):
    rows = [vals[n].reshape(1, d) for n in _SMALL_ROWS]
    misc = jnp.concatenate([
        vals["q_norm_g"].reshape(1, HEAD_DIM), vals["k_norm_g"].reshape(1, HEAD_DIM),
        jnp.pad(vals["sink_logits"].reshape(1, N_HEADS_B), ((0, 0), (0, HEAD_DIM - N_HEADS_B))),
        vals["rel_bias_table"].reshape(1, N_BUCKETS * N_HEADS_B)], axis=1)
    rows.append(jnp.pad(misc, ((0, 0), (0, d - misc.shape[1]))))
    rows.append(jnp.zeros((_PACK_ROWS - len(rows), d), F32))
    return jnp.concatenate(rows, axis=0).astype(F32)


def _unpack_small(pack, shapes):
    out = {n: pack[i].reshape(shapes[n]) for i, n in enumerate(_SMALL_ROWS)}
    misc = pack[len(_SMALL_ROWS)]
    out["q_norm_g"] = misc[:HEAD_DIM].reshape(shapes["q_norm_g"])
    out["k_norm_g"] = misc[HEAD_DIM:2 * HEAD_DIM].reshape(shapes["k_norm_g"])
    out["sink_logits"] = misc[2 * HEAD_DIM:2 * HEAD_DIM + N_HEADS_B].reshape(shapes["sink_logits"])
    out["rel_bias_table"] = misc[3 * HEAD_DIM:3 * HEAD_DIM + N_BUCKETS * N_HEADS_B].reshape(shapes["rel_bias_table"])
    return out


_WEIGHTS = ["attn_norm_g", "w_in", "q_norm_g", "k_norm_g", "sink_logits", "w_out", "mlp_norm_g", "w_up", "w_down",
            "ple_w", "ple_norm_g", "gate_norm_g", "w_gate", "rel_bias_table", "final_norm_g"]
_BIG = ["w_in", "w_out", "w_up", "w_down", "ple_w", "w_gate"]


def kernel(x, p, attn_norm_g, w_in, q_norm_g, k_norm_g, sink_logits, w_out, mlp_norm_g, w_up, w_down, ple_w, ple_norm_g, gate_norm_g, w_gate, rel_bias_table, final_norm_g, loss_target, m_attn_norm_g, m_w_in, m_q_norm_g, m_k_norm_g, m_sink_logits, m_w_out, m_mlp_norm_g, m_w_up, m_w_down, m_ple_w, m_ple_norm_g, m_gate_norm_g, m_w_gate, m_rel_bias_table, m_final_norm_g, v_attn_norm_g, v_w_in, v_q_norm_g, v_k_norm_g, v_sink_logits, v_w_out, v_mlp_norm_g, v_w_up, v_w_down, v_ple_w, v_ple_norm_g, v_gate_norm_g, v_w_gate, v_rel_bias_table, v_final_norm_g):
    given = dict(locals())
    w = {n: given[n] for n in _WEIGHTS}
    m = {n: given["m_" + n] for n in _WEIGHTS}
    v = {n: given["v_" + n] for n in _WEIGHTS}
    d = x.shape[-1]

    shards = {n: w[n][0] for n in _BIG}
    small = {
        "attn_norm_g": w["attn_norm_g"], "mlp_norm_g": w["mlp_norm_g"], "ple_norm_g": w["ple_norm_g"],
        "gate_norm_g": w["gate_norm_g"], "final_norm_g": w["final_norm_g"].reshape(1, d),
        "q_norm_g": w["q_norm_g"], "k_norm_g": w["k_norm_g"], "sink_logits": w["sink_logits"],
        "rel_bias_table": w["rel_bias_table"],
    }

    def update(n, parts):
        res = _sum_adamw(parts, w[n][0], m[n][0], v[n][0], name="adamw_" + n)
        return [t.reshape(w[n].shape) for t in res]

    loss_part, grad_x, big, small_g = _local_step(x[0], p[0, 0], loss_target[0], shards, small, update)
    grads, deltas, new_m, new_v = [{n: big[n][i] for n in _BIG} for i in range(4)]

    shapes = {n: w[n].shape for n in _WEIGHTS if n not in _BIG}
    pack = _pack_small(small_g, d)
    pack = pack.at[_PACK_ROWS - 1, :1].add(0.0 * grads["w_in"][0, 0, :1])
    pack = pack.at[_PACK_ROWS - 1, 1].set(loss_part[0, 0])
    g_small = _allreduce_small(pack)
    loss = g_small[_PACK_ROWS - 1, 1]
    d_small, m_small, v_small = _adamw_small(g_small, _pack_small(w, d), _pack_small(m, d), _pack_small(v, d))
    grads.update(_unpack_small(g_small, shapes))
    deltas.update(_unpack_small(d_small, shapes))
    new_m.update(_unpack_small(m_small, shapes))
    new_v.update(_unpack_small(v_small, shapes))

    return (loss, grad_x[None], *[grads[n] for n in _WEIGHTS], *[deltas[n] for n in _WEIGHTS],
            *[new_m[n] for n in _WEIGHTS], *[new_v[n] for n in _WEIGHTS])
```
